```python
import functools
import jax, jax.numpy as jnp
from jax import lax
import numpy as np

D_MODEL = 1024
BATCH = 8
SEQ = 8192
DEPTH = 1

GRID_W = 64
CTX_LEN = 256
EPS = 1e-6

GLA_HEADS = 4
GLA_DK = D_MODEL // 2 // GLA_HEADS
GLA_DV = D_MODEL // GLA_HEADS
GLA_QK = GLA_HEADS * GLA_DK
GLA_V = GLA_HEADS * GLA_DV
GLA_RANK = 16
GLA_TAU = 16.0
GLA_CHUNK = 64

SSM_INNER = 2 * D_MODEL
SSM_HEADDIM = 64
SSM_HEADS = SSM_INNER // SSM_HEADDIM
SSM_GROUPS = 4
SSM_HPG = SSM_HEADS // SSM_GROUPS
SSM_STATE = 128
SSM_BC = SSM_GROUPS * SSM_STATE
SSM_CONV = 4
CONV_LEFT = 2
SSM_CONV_DIM = SSM_INNER + 2 * SSM_BC
SSM_CHUNK = 128

D_FF = ((8 * D_MODEL // 3 + 255) // 256) * 256

IN_WIDTHS = (GLA_QK, GLA_QK, GLA_V, GLA_V, GLA_RANK, GLA_RANK,
             SSM_INNER, SSM_INNER, SSM_BC, SSM_BC, SSM_HEADS, SSM_HEADS,
             D_MODEL, D_MODEL)
D_IN = sum(IN_WIDTHS)

kernel_name = "hybrid_gla_ssd_prefix_dit"


def rms_norm(x, w):
    xf = x.astype(jnp.float32)
    y = xf * lax.rsqrt(jnp.mean(xf * xf, axis=-1, keepdims=True) + EPS)
    return (y * w.astype(jnp.float32)).astype(x.dtype)


def modulate(h, shift, scale):
    return h * (1 + scale) + shift


def _split_in(proj):
    idx = np.cumsum(IN_WIDTHS)[:-1].tolist()
    return jnp.split(proj, idx, axis=-1)


def _chunks(t, size):
    Bn, L = t.shape[:2]
    return jnp.moveaxis(t.reshape(Bn, L // size, size, *t.shape[2:]), 1, 0)


def _unchunks(t):
    n, Bn, C = t.shape[:3]
    return jnp.moveaxis(t, 0, 1).reshape(Bn, n * C, *t.shape[3:])


def _dwconv_centred(u, w, b):
    W = u.shape[1]
    up = jnp.pad(u, ((0, 0), (CONV_LEFT, SSM_CONV - 1 - CONV_LEFT), (0, 0)))
    out = b + up[:, 0:W] * w[0]
    for j in range(1, SSM_CONV):
        out = out + up[:, j:j + W] * w[j]
    return out


def _seq_conv(u, w, b):
    return _dwconv_centred(u, w, b)


def _row_conv(u, w, b, rows):
    Bn, L, Cc = u.shape
    return _dwconv_centred(u.reshape(Bn * rows, GRID_W, Cc), w, b).reshape(Bn, L, Cc)


def _gla_scan(q, k, v, log_g, s0):
    out_dtype = v.dtype
    f32 = jnp.float32
    qc, kc, vc, gc = (_chunks(t.astype(f32), GLA_CHUNK) for t in (q, k, v, log_g))
    mask = jnp.tril(jnp.ones((GLA_CHUNK, GLA_CHUNK), bool))[None, :, :, None, None]

    def step(S, inp):
        qi, ki, vi, gi = inp
        b = jnp.cumsum(gi, axis=1)
        b_last = b[:, -1]
        diff = b[:, :, None] - b[:, None, :]
        decay = jnp.exp(jnp.where(mask, diff, -jnp.inf))
        att = jnp.einsum('bthk,bshk,btshk->bhts', qi, ki, decay)
        o = jnp.einsum('bhts,bshv->bthv', att, vi) + jnp.einsum('bthk,bhkv->bthv', qi * jnp.exp(b), S)
        S_new = jnp.exp(b_last)[..., None] * S + jnp.einsum(
            'bshk,bshv->bhkv', ki * jnp.exp(b_last[:, None] - b), vi)
        return S_new, o

    S_fin, o = lax.scan(step, s0, (qc, kc, vc, gc))
    return _unchunks(o).astype(out_dtype), S_fin


def _gla_state(k, v, log_g):
    b = jnp.cumsum(log_g.astype(jnp.float32), axis=1)
    return jnp.einsum('blhk,blhv->bhkv', k.astype(jnp.float32) * jnp.exp(b[:, -1:] - b), v.astype(jnp.float32))


def _ssd_scan(x, bm, cm, a, s0):
    out_dtype = bm.dtype
    f32 = jnp.float32
    xc, bc, cc, ac = (_chunks(t.astype(f32), SSM_CHUNK) for t in (x, bm, cm, a))
    mask = jnp.tril(jnp.ones((SSM_CHUNK, SSM_CHUNK), bool))[None, :, :, None, None]

    def step(S, inp):
        xi, bi, ci, ai = inp
        cum = jnp.cumsum(ai, axis=1)
        seg = cum[:, :, None] - cum[:, None, :]
        Lm = jnp.exp(jnp.where(mask, seg, -jnp.inf))
        cb = jnp.einsum('btgn,bsgn->btsg', ci, bi)
        y = jnp.einsum('btsg,btsge,bsgep->btgep', cb, Lm, xi)
        y = y + jnp.einsum('btgn,bgenp->btgep', ci, S) * jnp.exp(cum)[..., None]
        S_new = jnp.exp(cum[:, -1])[..., None, None] * S + jnp.einsum(
            'bsgn,bsge,bsgep->bgenp', bi, jnp.exp(cum[:, -1:] - cum), xi)
        return S_new, y

    S_fin, y = lax.scan(step, s0, (xc, bc, cc, ac))
    return _unchunks(y).astype(out_dtype), S_fin


def _ssd_state(x, bm, a):
    cum = jnp.cumsum(a.astype(jnp.float32), axis=1)
    return jnp.einsum('blgn,blge,blgep->bgenp', bm.astype(jnp.float32),
                      jnp.exp(cum[:, -1:] - cum), x.astype(jnp.float32))


def _flip(t):
    return jnp.flip(t, axis=1)


def _bidir(scan_fn, fwd_in, bwd_in, s_f, s_b):
    o_f, st_f = scan_fn(*fwd_in, s_f)
    o_b, st_b = scan_fn(*[_flip(t) for t in bwd_in], s_b)
    return o_f + _flip(o_b), st_f, st_b


def _gla_features(q, k, v, lr_f, lr_b, lp):
    Bn, L, _ = q.shape
    heads = lambda t, d: t.reshape(Bn, L, GLA_HEADS, d)
    q = heads(q, GLA_DK) * (GLA_DK ** -0.5)
    k = heads(k, GLA_DK)
    v = heads(v, GLA_DV)

    def log_gate(lr, up, bias):
        pre = (lr @ up + bias).astype(jnp.float32)
        return heads(jax.nn.log_sigmoid(pre) / GLA_TAU, GLA_DK)

    return (q, k, v, log_gate(lr_f, lp['gla_up_f'], lp['gla_bias_f']),
            log_gate(lr_b, lp['gla_up_b'], lp['gla_bias_b']))


def _ssm_features(xs, bm, cm, dt_f, dt_b, lp, conv_fn):
    xbc = jax.nn.silu(conv_fn(jnp.concatenate([xs, bm, cm], axis=-1), lp['conv_w'], lp['conv_b']))
    xs, bm, cm = jnp.split(xbc, [SSM_INNER, SSM_INNER + SSM_BC], axis=-1)
    Bn, L, _ = xs.shape
    xh = xs.reshape(Bn, L, SSM_GROUPS, SSM_HPG, SSM_HEADDIM)
    bm = bm.reshape(Bn, L, SSM_GROUPS, SSM_STATE)
    cm = cm.reshape(Bn, L, SSM_GROUPS, SSM_STATE)

    def direction(dt_raw, dt_bias, a_log):
        dt = jax.nn.softplus((dt_raw + dt_bias).astype(jnp.float32)).reshape(Bn, L, SSM_GROUPS, SSM_HPG)
        a = -jnp.exp(a_log.astype(jnp.float32)).reshape(SSM_GROUPS, SSM_HPG) * dt
        return xh * dt[..., None], a

    return (xh, bm, cm, direction(dt_f, lp['dt_bias_f'], lp['a_log_f']),
            direction(dt_b, lp['dt_bias_b'], lp['a_log_b']))


def _mixer(parts, lp, init, conv_fn):
    q, k, v, r, lr_f, lr_b, z, xs, bm, cm, dt_f, dt_b, gate_a, gate_b = parts
    Bn, L, _ = q.shape
    gq, gk, gv, lg_f, lg_b = _gla_features(q, k, v, lr_f, lr_b, lp)
    o_a, st_af, st_ab = _bidir(_gla_scan, (gq, gk, gv, lg_f), (gq, gk, gv, lg_b), init[0], init[1])
    o_a = rms_norm(o_a, lp['gla_norm_w']) * jax.nn.silu(r).reshape(Bn, L, GLA_HEADS, GLA_DV)
    y_a = o_a.reshape(Bn, L, GLA_V) @ lp['w_pa']
    xh, sb, sc, (x_f, a_f), (x_b, a_b) = _ssm_features(xs, bm, cm, dt_f, dt_b, lp, conv_fn)
    o_b, st_bf, st_bb = _bidir(_ssd_scan, (x_f, sb, sc, a_f), (x_b, sb, sc, a_b), init[2], init[3])
    o_b = o_b + lp['d_skip'].reshape(SSM_GROUPS, SSM_HPG, 1) * xh
    o_b = o_b.reshape(Bn, L, SSM_INNER) * jax.nn.silu(z)
    o_b = rms_norm(o_b.reshape(Bn, L, SSM_GROUPS, SSM_INNER // SSM_GROUPS),
                   lp['ssm_norm_w'].reshape(SSM_GROUPS, SSM_INNER // SSM_GROUPS)).reshape(Bn, L, SSM_INNER)
    y_b = o_b @ lp['w_pb']
    merged = jax.nn.sigmoid(gate_a) * y_a + jax.nn.sigmoid(gate_b) * y_b
    return merged @ lp['w_out'], (st_af, st_ab, st_bf, st_bb)


def _context_states(parts, lp):
    k, v, lr_f, lr_b = parts[1], parts[2], parts[4], parts[5]
    xs, bm, cm, dt_f, dt_b = parts[7], parts[8], parts[9], parts[10], parts[11]
    _, gk, gv, lg_f, lg_b = _gla_features(parts[0], k, v, lr_f, lr_b, lp)
    _, sb, _, (x_f, a_f), (x_b, a_b) = _ssm_features(xs, bm, cm, dt_f, dt_b, lp, _seq_conv)
    return (_gla_state(gk, gv, lg_f), _gla_state(_flip(gk), _flip(gv), _flip(lg_b)),
            _ssd_state(x_f, sb, a_f), _ssd_state(_flip(x_b), _flip(sb), _flip(a_b)))


def _zero_states(Bn):
    f32 = jnp.float32
    g = jnp.zeros((Bn, GLA_HEADS, GLA_DK, GLA_DV), f32)
    s = jnp.zeros((Bn, SSM_GROUPS, SSM_HPG, SSM_STATE, SSM_HEADDIM), f32)
    return (g, g, s, s)


def _swiglu(h, w_gate, w_up, w_down):
    return (jax.nn.silu(h @ w_gate) * (h @ w_up)) @ w_down


def _fwd_setup_inputs(seed: int = 0) -> dict:
    key = jax.random.key(seed)
    ks = iter(jax.random.split(key, 40))
    nrm = lambda shape, s: jax.random.normal(next(ks), shape, jnp.float32) * s
    L_ = DEPTH
    dt = jnp.exp(jax.random.uniform(next(ks), (2, L_, SSM_HEADS), jnp.float32,
                                    np.log(1e-3), np.log(1e-1)))
    dt_bias = dt + jnp.log(-jnp.expm1(-dt))
    a_log = jnp.log(jax.random.uniform(next(ks), (2, L_, SSM_HEADS), jnp.float32, 1.0, 16.0))
    return {
        'x': nrm((BATCH, SEQ, D_MODEL), 1.0),
        'c': nrm((BATCH, D_MODEL), 1.0),
        'ctx': nrm((BATCH, CTX_LEN, D_MODEL), 1.0),
        'c_ctx': nrm((D_MODEL,), 1.0),
        'w_ada': nrm((L_, D_MODEL, 6 * D_MODEL), 0.02),
        'b_ada': nrm((L_, 6 * D_MODEL), 0.01),
        'norm1_w': 1.0 + nrm((L_, D_MODEL), 0.02),
        'w_in': nrm((L_, D_MODEL, D_IN), D_MODEL ** -0.5),
        'gla_up_f': nrm((L_, GLA_RANK, GLA_QK), GLA_RANK ** -0.5),
        'gla_bias_f': nrm((L_, GLA_QK), 0.1),
        'gla_up_b': nrm((L_, GLA_RANK, GLA_QK), GLA_RANK ** -0.5),
        'gla_bias_b': nrm((L_, GLA_QK), 0.1),
        'gla_norm_w': 1.0 + nrm((L_, GLA_DV), 0.02),
        'conv_w': nrm((L_, SSM_CONV, SSM_CONV_DIM), SSM_CONV ** -0.5),
        'conv_b': nrm((L_, SSM_CONV_DIM), 0.02),
        'dt_bias_f': dt_bias[0],
        'dt_bias_b': dt_bias[1],
        'a_log_f': a_log[0],
        'a_log_b': a_log[1],
        'd_skip': 1.0 + nrm((L_, SSM_HEADS), 0.02),
        'ssm_norm_w': 1.0 + nrm((L_, SSM_INNER), 0.02),
        'w_pa': nrm((L_, GLA_V, D_MODEL), GLA_V ** -0.5),
        'w_pb': nrm((L_, SSM_INNER, D_MODEL), SSM_INNER ** -0.5),
        'w_out': nrm((L_, D_MODEL, D_MODEL), D_MODEL ** -0.5),
        'norm2_w': 1.0 + nrm((L_, D_MODEL), 0.02),
        'w_gate': nrm((L_, D_MODEL, D_FF), D_MODEL ** -0.5),
        'w_up': nrm((L_, D_MODEL, D_FF), D_MODEL ** -0.5),
        'w_down': nrm((L_, D_FF, D_MODEL), D_FF ** -0.5),
        'final_norm_w': 1.0 + nrm((D_MODEL,), 0.02),
    }


def _fwd_reference(x, c, ctx, c_ctx, w_ada, b_ada, norm1_w, w_in, gla_up_f, gla_bias_f, gla_up_b, gla_bias_b,
              gla_norm_w, conv_w, conv_b, dt_bias_f, dt_bias_b, a_log_f, a_log_b, d_skip, ssm_norm_w,
              w_pa, w_pb, w_out, norm2_w, w_gate, w_up, w_down, final_norm_w):
    rows = x.shape[1] // GRID_W
    latent_conv = functools.partial(_row_conv, rows=rows)
    h_lat, h_ctx = x, ctx
    for layer in range(DEPTH):
        lp = {
            'gla_up_f': gla_up_f[layer], 'gla_bias_f': gla_bias_f[layer],
            'gla_up_b': gla_up_b[layer], 'gla_bias_b': gla_bias_b[layer],
            'gla_norm_w': gla_norm_w[layer], 'conv_w': conv_w[layer], 'conv_b': conv_b[layer],
            'dt_bias_f': dt_bias_f[layer], 'dt_bias_b': dt_bias_b[layer],
            'a_log_f': a_log_f[layer], 'a_log_b': a_log_b[layer], 'd_skip': d_skip[layer],
            'ssm_norm_w': ssm_norm_w[layer], 'w_pa': w_pa[layer], 'w_pb': w_pb[layer], 'w_out': w_out[layer],
        }
        ada = jax.nn.silu(c)[:, None, :] @ w_ada[layer] + b_ada[layer]
        sh1, sc1, g1, sh2, sc2, g2 = jnp.split(ada, 6, axis=-1)
        ada_c = jax.nn.silu(c_ctx) @ w_ada[layer] + b_ada[layer]
        csh1, csc1, cg1, csh2, csc2, cg2 = jnp.split(ada_c, 6, axis=-1)

        parts_c = _split_in(modulate(rms_norm(h_ctx, norm1_w[layer]), csh1, csc1) @ w_in[layer])
        if layer == DEPTH - 1:
            states = _context_states(parts_c, lp)
        else:
            mix_c, states = _mixer(parts_c, lp, _zero_states(h_ctx.shape[0]), _seq_conv)
            h_ctx = h_ctx + cg1 * mix_c
            h_ctx = h_ctx + cg2 * _swiglu(modulate(rms_norm(h_ctx, norm2_w[layer]), csh2, csc2),
                                          w_gate[layer], w_up[layer], w_down[layer])

        parts = _split_in(modulate(rms_norm(h_lat, norm1_w[layer]), sh1, sc1) @ w_in[layer])
        mix, _ = _mixer(parts, lp, states, latent_conv)
        h_lat = h_lat + g1 * mix
        h_lat = h_lat + g2 * _swiglu(modulate(rms_norm(h_lat, norm2_w[layer]), sh2, sc2),
                                     w_gate[layer], w_up[layer], w_down[layer])
    return rms_norm(h_lat, final_norm_w)


import jax as _jax
import jax.numpy as _jnp

TWIN_FORMAT = 'train_step'
FWD_PARAMS = ['x', 'c', 'ctx', 'c_ctx', 'w_ada', 'b_ada', 'norm1_w', 'w_in', 'gla_up_f', 'gla_bias_f', 'gla_up_b', 'gla_bias_b', 'gla_norm_w', 'conv_w', 'conv_b', 'dt_bias_f', 'dt_bias_b', 'a_log_f', 'a_log_b', 'd_skip', 'ssm_norm_w', 'w_pa', 'w_pb', 'w_out', 'norm2_w', 'w_gate', 'w_up', 'w_down', 'final_norm_w']
TWIN_WEIGHTS = ['c_ctx', 'w_ada', 'b_ada', 'norm1_w', 'w_in', 'gla_up_f', 'gla_bias_f', 'gla_up_b', 'gla_bias_b', 'gla_norm_w', 'conv_w', 'conv_b', 'dt_bias_f', 'dt_bias_b', 'a_log_f', 'a_log_b', 'd_skip', 'ssm_norm_w', 'w_pa', 'w_pb', 'w_out', 'norm2_w', 'w_gate', 'w_up', 'w_down', 'final_norm_w']
TWIN_DIFF_INPUT = 'x'
TWIN_INPUTS = ['x', 'c', 'ctx', 'c_ctx', 'w_ada', 'b_ada', 'norm1_w', 'w_in', 'gla_up_f', 'gla_bias_f', 'gla_up_b', 'gla_bias_b', 'gla_norm_w', 'conv_w', 'conv_b', 'dt_bias_f', 'dt_bias_b', 'a_log_f', 'a_log_b', 'd_skip', 'ssm_norm_w', 'w_pa', 'w_pb', 'w_out', 'norm2_w', 'w_gate', 'w_up', 'w_down', 'final_norm_w', 'loss_target', 'm_c_ctx', 'm_w_ada', 'm_b_ada', 'm_norm1_w', 'm_w_in', 'm_gla_up_f', 'm_gla_bias_f', 'm_gla_up_b', 'm_gla_bias_b', 'm_gla_norm_w', 'm_conv_w', 'm_conv_b', 'm_dt_bias_f', 'm_dt_bias_b', 'm_a_log_f', 'm_a_log_b', 'm_d_skip', 'm_ssm_norm_w', 'm_w_pa', 'm_w_pb', 'm_w_out', 'm_norm2_w', 'm_w_gate', 'm_w_up', 'm_w_down', 'm_final_norm_w', 'v_c_ctx', 'v_w_ada', 'v_b_ada', 'v_norm1_w', 'v_w_in', 'v_gla_up_f', 'v_gla_bias_f', 'v_gla_up_b', 'v_gla_bias_b', 'v_gla_norm_w', 'v_conv_w', 'v_conv_b', 'v_dt_bias_f', 'v_dt_bias_b', 'v_a_log_f', 'v_a_log_b', 'v_d_skip', 'v_ssm_norm_w', 'v_w_pa', 'v_w_pb', 'v_w_out', 'v_norm2_w', 'v_w_gate', 'v_w_up', 'v_w_down', 'v_final_norm_w']
TWIN_OUTPUTS = ['loss', 'grad_x', 'grad_c_ctx', 'grad_w_ada', 'grad_b_ada', 'grad_norm1_w', 'grad_w_in', 'grad_gla_up_f', 'grad_gla_bias_f', 'grad_gla_up_b', 'grad_gla_bias_b', 'grad_gla_norm_w', 'grad_conv_w', 'grad_conv_b', 'grad_dt_bias_f', 'grad_dt_bias_b', 'grad_a_log_f', 'grad_a_log_b', 'grad_d_skip', 'grad_ssm_norm_w', 'grad_w_pa', 'grad_w_pb', 'grad_w_out', 'grad_norm2_w', 'grad_w_gate', 'grad_w_up', 'grad_w_down', 'grad_final_norm_w', 'delta_c_ctx', 'delta_w_ada', 'delta_b_ada', 'delta_norm1_w', 'delta_w_in', 'delta_gla_up_f', 'delta_gla_bias_f', 'delta_gla_up_b', 'delta_gla_bias_b', 'delta_gla_norm_w', 'delta_conv_w', 'delta_conv_b', 'delta_dt_bias_f', 'delta_dt_bias_b', 'delta_a_log_f', 'delta_a_log_b', 'delta_d_skip', 'delta_ssm_norm_w', 'delta_w_pa', 'delta_w_pb', 'delta_w_out', 'delta_norm2_w', 'delta_w_gate', 'delta_w_up', 'delta_w_down', 'delta_final_norm_w', 'new_m_c_ctx', 'new_m_w_ada', 'new_m_b_ada', 'new_m_norm1_w', 'new_m_w_in', 'new_m_gla_up_f', 'new_m_gla_bias_f', 'new_m_gla_up_b', 'new_m_gla_bias_b', 'new_m_gla_norm_w', 'new_m_conv_w', 'new_m_conv_b', 'new_m_dt_bias_f', 'new_m_dt_bias_b', 'new_m_a_log_f', 'new_m_a_log_b', 'new_m_d_skip', 'new_m_ssm_norm_w', 'new_m_w_pa', 'new_m_w_pb', 'new_m_w_out', 'new_m_norm2_w', 'new_m_w_gate', 'new_m_w_up', 'new_m_w_down', 'new_m_final_norm_w', 'new_v_c_ctx', 'new_v_w_ada', 'new_v_b_ada', 'new_v_norm1_w', 'new_v_w_in', 'new_v_gla_up_f', 'new_v_gla_bias_f', 'new_v_gla_up_b', 'new_v_gla_bias_b', 'new_v_gla_norm_w', 'new_v_conv_w', 'new_v_conv_b', 'new_v_dt_bias_f', 'new_v_dt_bias_b', 'new_v_a_log_f', 'new_v_a_log_b', 'new_v_d_skip', 'new_v_ssm_norm_w', 'new_v_w_pa', 'new_v_w_pb', 'new_v_w_out', 'new_v_norm2_w', 'new_v_w_gate', 'new_v_w_up', 'new_v_w_down', 'new_v_final_norm_w']
TWIN_LEAF_KINDS = {'loss': 'loss', 'grad_x': 'grad_x', 'grad_c_ctx': 'grad_w', 'grad_w_ada': 'grad_w', 'grad_b_ada': 'grad_w', 'grad_norm1_w': 'grad_w', 'grad_w_in': 'grad_w', 'grad_gla_up_f': 'grad_w', 'grad_gla_bias_f': 'grad_w', 'grad_gla_up_b': 'grad_w', 'grad_gla_bias_b': 'grad_w', 'grad_gla_norm_w': 'grad_w', 'grad_conv_w': 'grad_w', 'grad_conv_b': 'grad_w', 'grad_dt_bias_f': 'grad_w', 'grad_dt_bias_b': 'grad_w', 'grad_a_log_f': 'grad_w', 'grad_a_log_b': 'grad_w', 'grad_d_skip': 'grad_w', 'grad_ssm_norm_w': 'grad_w', 'grad_w_pa': 'grad_w', 'grad_w_pb': 'grad_w', 'grad_w_out': 'grad_w', 'grad_norm2_w': 'grad_w', 'grad_w_gate': 'grad_w', 'grad_w_up': 'grad_w', 'grad_w_down': 'grad_w', 'grad_final_norm_w': 'grad_w', 'delta_c_ctx': 'delta_w', 'delta_w_ada': 'delta_w', 'delta_b_ada': 'delta_w', 'delta_norm1_w': 'delta_w', 'delta_w_in': 'delta_w', 'delta_gla_up_f': 'delta_w', 'delta_gla_bias_f': 'delta_w', 'delta_gla_up_b': 'delta_w', 'delta_gla_bias_b': 'delta_w', 'delta_gla_norm_w': 'delta_w', 'delta_conv_w': 'delta_w', 'delta_conv_b': 'delta_w', 'delta_dt_bias_f': 'delta_w', 'delta_dt_bias_b': 'delta_w', 'delta_a_log_f': 'delta_w', 'delta_a_log_b': 'delta_w', 'delta_d_skip': 'delta_w', 'delta_ssm_norm_w': 'delta_w', 'delta_w_pa': 'delta_w', 'delta_w_pb': 'delta_w', 'delta_w_out': 'delta_w', 'delta_norm2_w': 'delta_w', 'delta_w_gate': 'delta_w', 'delta_w_up': 'delta_w', 'delta_w_down': 'delta_w', 'delta_final_norm_w': 'delta_w', 'new_m_c_ctx': 'new_m', 'new_m_w_ada': 'new_m', 'new_m_b_ada': 'new_m', 'new_m_norm1_w': 'new_m', 'new_m_w_in': 'new_m', 'new_m_gla_up_f': 'new_m', 'new_m_gla_bias_f': 'new_m', 'new_m_gla_up_b': 'new_m', 'new_m_gla_bias_b': 'new_m', 'new_m_gla_norm_w': 'new_m', 'new_m_conv_w': 'new_m', 'new_m_conv_b': 'new_m', 'new_m_dt_bias_f': 'new_m', 'new_m_dt_bias_b': 'new_m', 'new_m_a_log_f': 'new_m', 'new_m_a_log_b': 'new_m', 'new_m_d_skip': 'new_m', 'new_m_ssm_norm_w': 'new_m', 'new_m_w_pa': 'new_m', 'new_m_w_pb': 'new_m', 'new_m_w_out': 'new_m', 'new_m_norm2_w': 'new_m', 'new_m_w_gate': 'new_m', 'new_m_w_up': 'new_m', 'new_m_w_down': 'new_m', 'new_m_final_norm_w': 'new_m', 'new_v_c_ctx': 'new_v', 'new_v_w_ada': 'new_v', 'new_v_b_ada': 'new_v', 'new_v_norm1_w': 'new_v', 'new_v_w_in': 'new_v', 'new_v_gla_up_f': 'new_v', 'new_v_gla_bias_f': 'new_v', 'new_v_gla_up_b': 'new_v', 'new_v_gla_bias_b': 'new_v', 'new_v_gla_norm_w': 'new_v', 'new_v_conv_w': 'new_v', 'new_v_conv_b': 'new_v', 'new_v_dt_bias_f': 'new_v', 'new_v_dt_bias_b': 'new_v', 'new_v_a_log_f': 'new_v', 'new_v_a_log_b': 'new_v', 'new_v_d_skip': 'new_v', 'new_v_ssm_norm_w': 'new_v', 'new_v_w_pa': 'new_v', 'new_v_w_pb': 'new_v', 'new_v_w_out': 'new_v', 'new_v_norm2_w': 'new_v', 'new_v_w_gate': 'new_v', 'new_v_w_up': 'new_v', 'new_v_w_down': 'new_v', 'new_v_final_norm_w': 'new_v'}


def _forward(args):
    return _fwd_reference(*[args[k] for k in FWD_PARAMS])


def _output_shape():
    def fwd():
        inp = _fwd_setup_inputs(0)
        return _fwd_reference(*[inp[k] for k in FWD_PARAMS])
    out = _jax.eval_shape(fwd)
    return out.shape, out.dtype

N_MICROBATCH = 1
ADAM_LR = 0.001
ADAM_B1 = 0.9
ADAM_B2 = 0.999
ADAM_EPS = 1e-08
ADAM_WD = 0.01
ADAM_STEP = 10
PER_EXAMPLE_BATCH_AXIS = {'x': 0, 'c': 0, 'ctx': 0, 'loss_target': 0}
SHARED_INPUTS = []
_WEIGHT_DTYPES = {'c_ctx': _jnp.float32, 'w_ada': _jnp.float32, 'b_ada': _jnp.float32, 'norm1_w': _jnp.float32, 'w_in': _jnp.float32, 'gla_up_f': _jnp.float32, 'gla_bias_f': _jnp.float32, 'gla_up_b': _jnp.float32, 'gla_bias_b': _jnp.float32, 'gla_norm_w': _jnp.float32, 'conv_w': _jnp.float32, 'conv_b': _jnp.float32, 'dt_bias_f': _jnp.float32, 'dt_bias_b': _jnp.float32, 'a_log_f': _jnp.float32, 'a_log_b': _jnp.float32, 'd_skip': _jnp.float32, 'ssm_norm_w': _jnp.float32, 'w_pa': _jnp.float32, 'w_pb': _jnp.float32, 'w_out': _jnp.float32, 'norm2_w': _jnp.float32, 'w_gate': _jnp.float32, 'w_up': _jnp.float32, 'w_down': _jnp.float32, 'final_norm_w': _jnp.float32}
MOMENT_SCALE = {'c_ctx': 9.816526e-03, 'w_ada': 8.680668e-02, 'b_ada': 1.446365e-01, 'norm1_w': 1.013195e-01, 'w_in': 3.661008e-02, 'gla_up_f': 8.495184e-03, 'gla_bias_f': 1.925592e-02, 'gla_up_b': 7.824105e-03, 'gla_bias_b': 1.795626e-02, 'gla_norm_w': 8.554514e-02, 'conv_w': 3.233172e-02, 'conv_b': 4.178182e-02, 'dt_bias_f': 6.692011e-02, 'dt_bias_b': 8.408729e-02, 'a_log_f': 9.687870e-02, 'a_log_b': 1.044248e-01, 'd_skip': 1.065366e-01, 'ssm_norm_w': 3.769245e-02, 'w_pa': 3.696477e-02, 'w_pb': 5.354105e-02, 'w_out': 6.493367e-02, 'norm2_w': 9.813659e-02, 'w_gate': 4.490242e-02, 'w_up': 4.344391e-02, 'w_down': 7.194714e-02, 'final_norm_w': 6.401328e+01}


def _to_microbatches(a, axis):
    t = _jnp.moveaxis(a, axis, 0)
    t = t.reshape((N_MICROBATCH, t.shape[0] // N_MICROBATCH) + t.shape[1:])
    return _jnp.moveaxis(t, 1, axis + 1)


def setup_inputs(seed: int = 0) -> dict:
    inp = _fwd_setup_inputs(seed)
    key = _jax.random.fold_in(_jax.random.key(seed), 7919)
    shape, _ = _output_shape()
    out = dict(inp)
    out["loss_target"] = _jax.random.normal(_jax.random.fold_in(key, 0), shape, _jnp.float32)
    for i, name in enumerate(TWIN_WEIGHTS):
        w = inp[name].astype(_jnp.float32)
        if MOMENT_SCALE is None:
            s = _jnp.sqrt(_jnp.mean(_jnp.square(w)) + 1e-30)
        else:
            s = MOMENT_SCALE[name]
        km, kv = _jax.random.split(_jax.random.fold_in(key, i + 1))
        out[name] = w
        out["m_" + name] = s * _jax.random.normal(km, w.shape, _jnp.float32)
        out["v_" + name] = (s * s) * _jax.random.uniform(kv, w.shape, _jnp.float32, 0.5, 1.5)
    if N_MICROBATCH > 1:
        for name, axis in PER_EXAMPLE_BATCH_AXIS.items():
            out[name] = _to_microbatches(out[name], axis)
    return {'x': out['x'], 'c': out['c'], 'ctx': out['ctx'], 'c_ctx': out['c_ctx'], 'w_ada': out['w_ada'], 'b_ada': out['b_ada'], 'norm1_w': out['norm1_w'], 'w_in': out['w_in'], 'gla_up_f': out['gla_up_f'], 'gla_bias_f': out['gla_bias_f'], 'gla_up_b': out['gla_up_b'], 'gla_bias_b': out['gla_bias_b'], 'gla_norm_w': out['gla_norm_w'], 'conv_w': out['conv_w'], 'conv_b': out['conv_b'], 'dt_bias_f': out['dt_bias_f'], 'dt_bias_b': out['dt_bias_b'], 'a_log_f': out['a_log_f'], 'a_log_b': out['a_log_b'], 'd_skip': out['d_skip'], 'ssm_norm_w': out['ssm_norm_w'], 'w_pa': out['w_pa'], 'w_pb': out['w_pb'], 'w_out': out['w_out'], 'norm2_w': out['norm2_w'], 'w_gate': out['w_gate'], 'w_up': out['w_up'], 'w_down': out['w_down'], 'final_norm_w': out['final_norm_w'], 'loss_target': out['loss_target'], 'm_c_ctx': out['m_c_ctx'], 'm_w_ada': out['m_w_ada'], 'm_b_ada': out['m_b_ada'], 'm_norm1_w': out['m_norm1_w'], 'm_w_in': out['m_w_in'], 'm_gla_up_f': out['m_gla_up_f'], 'm_gla_bias_f': out['m_gla_bias_f'], 'm_gla_up_b': out['m_gla_up_b'], 'm_gla_bias_b': out['m_gla_bias_b'], 'm_gla_norm_w': out['m_gla_norm_w'], 'm_conv_w': out['m_conv_w'], 'm_conv_b': out['m_conv_b'], 'm_dt_bias_f': out['m_dt_bias_f'], 'm_dt_bias_b': out['m_dt_bias_b'], 'm_a_log_f': out['m_a_log_f'], 'm_a_log_b': out['m_a_log_b'], 'm_d_skip': out['m_d_skip'], 'm_ssm_norm_w': out['m_ssm_norm_w'], 'm_w_pa': out['m_w_pa'], 'm_w_pb': out['m_w_pb'], 'm_w_out': out['m_w_out'], 'm_norm2_w': out['m_norm2_w'], 'm_w_gate': out['m_w_gate'], 'm_w_up': out['m_w_up'], 'm_w_down': out['m_w_down'], 'm_final_norm_w': out['m_final_norm_w'], 'v_c_ctx': out['v_c_ctx'], 'v_w_ada': out['v_w_ada'], 'v_b_ada': out['v_b_ada'], 'v_norm1_w': out['v_norm1_w'], 'v_w_in': out['v_w_in'], 'v_gla_up_f': out['v_gla_up_f'], 'v_gla_bias_f': out['v_gla_bias_f'], 'v_gla_up_b': out['v_gla_up_b'], 'v_gla_bias_b': out['v_gla_bias_b'], 'v_gla_norm_w': out['v_gla_norm_w'], 'v_conv_w': out['v_conv_w'], 'v_conv_b': out['v_conv_b'], 'v_dt_bias_f': out['v_dt_bias_f'], 'v_dt_bias_b': out['v_dt_bias_b'], 'v_a_log_f': out['v_a_log_f'], 'v_a_log_b': out['v_a_log_b'], 'v_d_skip': out['v_d_skip'], 'v_ssm_norm_w': out['v_ssm_norm_w'], 'v_w_pa': out['v_w_pa'], 'v_w_pb': out['v_w_pb'], 'v_w_out': out['v_w_out'], 'v_norm2_w': out['v_norm2_w'], 'v_w_gate': out['v_w_gate'], 'v_w_up': out['v_w_up'], 'v_w_down': out['v_w_down'], 'v_final_norm_w': out['v_final_norm_w']}


def _loss(weights, diff, rest, loss_target):
    with _jax.named_scope("forward"):
        args = {**rest, TWIN_DIFF_INPUT: diff, **{k: w.astype(_WEIGHT_DTYPES[k]) for k, w in weights.items()}}
        y = _forward(args)
    with _jax.named_scope("loss_head"):
        err = _jnp.square(y.astype(_jnp.float32) - loss_target)
        return 0.5 * _jnp.sum(_jnp.mean(err, axis=-1)) if err.ndim else 0.5 * err


def _adamw(w, g, m, v):
    m = ADAM_B1 * m + (1.0 - ADAM_B1) * g
    v = ADAM_B2 * v + (1.0 - ADAM_B2) * _jnp.square(g)
    m_hat = m / (1.0 - ADAM_B1 ** ADAM_STEP)
    v_hat = v / (1.0 - ADAM_B2 ** ADAM_STEP)
    delta = -ADAM_LR * (m_hat / (_jnp.sqrt(v_hat) + ADAM_EPS) + ADAM_WD * w)
    return delta, m, v


def reference(x, c, ctx, c_ctx, w_ada, b_ada, norm1_w, w_in, gla_up_f, gla_bias_f, gla_up_b, gla_bias_b, gla_norm_w, conv_w, conv_b, dt_bias_f, dt_bias_b, a_log_f, a_log_b, d_skip, ssm_norm_w, w_pa, w_pb, w_out, norm2_w, w_gate, w_up, w_down, final_norm_w, loss_target, m_c_ctx, m_w_ada, m_b_ada, m_norm1_w, m_w_in, m_gla_up_f, m_gla_bias_f, m_gla_up_b, m_gla_bias_b, m_gla_norm_w, m_conv_w, m_conv_b, m_dt_bias_f, m_dt_bias_b, m_a_log_f, m_a_log_b, m_d_skip, m_ssm_norm_w, m_w_pa, m_w_pb, m_w_out, m_norm2_w, m_w_gate, m_w_up, m_w_down, m_final_norm_w, v_c_ctx, v_w_ada, v_b_ada, v_norm1_w, v_w_in, v_gla_up_f, v_gla_bias_f, v_gla_up_b, v_gla_bias_b, v_gla_norm_w, v_conv_w, v_conv_b, v_dt_bias_f, v_dt_bias_b, v_a_log_f, v_a_log_b, v_d_skip, v_ssm_norm_w, v_w_pa, v_w_pb, v_w_out, v_norm2_w, v_w_gate, v_w_up, v_w_down, v_final_norm_w):
    given = dict(x=x, c=c, ctx=ctx, c_ctx=c_ctx, w_ada=w_ada, b_ada=b_ada, norm1_w=norm1_w, w_in=w_in, gla_up_f=gla_up_f, gla_bias_f=gla_bias_f, gla_up_b=gla_up_b, gla_bias_b=gla_bias_b, gla_norm_w=gla_norm_w, conv_w=conv_w, conv_b=conv_b, dt_bias_f=dt_bias_f, dt_bias_b=dt_bias_b, a_log_f=a_log_f, a_log_b=a_log_b, d_skip=d_skip, ssm_norm_w=ssm_norm_w, w_pa=w_pa, w_pb=w_pb, w_out=w_out, norm2_w=norm2_w, w_gate=w_gate, w_up=w_up, w_down=w_down, final_norm_w=final_norm_w, loss_target=loss_target, m_c_ctx=m_c_ctx, m_w_ada=m_w_ada, m_b_ada=m_b_ada, m_norm1_w=m_norm1_w, m_w_in=m_w_in, m_gla_up_f=m_gla_up_f, m_gla_bias_f=m_gla_bias_f, m_gla_up_b=m_gla_up_b, m_gla_bias_b=m_gla_bias_b, m_gla_norm_w=m_gla_norm_w, m_conv_w=m_conv_w, m_conv_b=m_conv_b, m_dt_bias_f=m_dt_bias_f, m_dt_bias_b=m_dt_bias_b, m_a_log_f=m_a_log_f, m_a_log_b=m_a_log_b, m_d_skip=m_d_skip, m_ssm_norm_w=m_ssm_norm_w, m_w_pa=m_w_pa, m_w_pb=m_w_pb, m_w_out=m_w_out, m_norm2_w=m_norm2_w, m_w_gate=m_w_gate, m_w_up=m_w_up, m_w_down=m_w_down, m_final_norm_w=m_final_norm_w, v_c_ctx=v_c_ctx, v_w_ada=v_w_ada, v_b_ada=v_b_ada, v_norm1_w=v_norm1_w, v_w_in=v_w_in, v_gla_up_f=v_gla_up_f, v_gla_bias_f=v_gla_bias_f, v_gla_up_b=v_gla_up_b, v_gla_bias_b=v_gla_bias_b, v_gla_norm_w=v_gla_norm_w, v_conv_w=v_conv_w, v_conv_b=v_conv_b, v_dt_bias_f=v_dt_bias_f, v_dt_bias_b=v_dt_bias_b, v_a_log_f=v_a_log_f, v_a_log_b=v_a_log_b, v_d_skip=v_d_skip, v_ssm_norm_w=v_ssm_norm_w, v_w_pa=v_w_pa, v_w_pb=v_w_pb, v_w_out=v_w_out, v_norm2_w=v_norm2_w, v_w_gate=v_w_gate, v_w_up=v_w_up, v_w_down=v_w_down, v_final_norm_w=v_final_norm_w)
    weights = {n: given[n] for n in TWIN_WEIGHTS}
    shared = {n: given[n] for n in SHARED_INPUTS}
    per_example = {n: given[n] for n in ['x', 'c', 'ctx']}
    grad_fn = _jax.value_and_grad(_loss, argnums=(0, 1))

    def one_microbatch(ex, loss_target):
        ex = dict(ex)
        diff = ex.pop(TWIN_DIFF_INPUT)
        return grad_fn(weights, diff, {**shared, **ex}, loss_target)

    if N_MICROBATCH == 1:
        loss, (grad_w, grad_x) = one_microbatch(per_example, given["loss_target"])
    else:
        def body(carry, xs):
            loss_sum, grad_sum = carry
            l_k, (gw_k, gx_k) = one_microbatch(xs[0], xs[1])
            with _jax.named_scope("update"):
                return (loss_sum + l_k, _jax.tree.map(_jnp.add, grad_sum, gw_k)), gx_k

        init = (_jnp.zeros((), _jnp.float32), _jax.tree.map(_jnp.zeros_like, weights))
        (loss, grad_w), grad_x = _jax.lax.scan(body, init, (per_example, given["loss_target"]))
    with _jax.named_scope("update"):
        delta_w, new_m, new_v = {}, {}, {}
        for n in TWIN_WEIGHTS:
            delta_w[n], new_m[n], new_v[n] = _adamw(weights[n], grad_w[n], given["m_" + n], given["v_" + n])
    return (loss, grad_x, *[grad_w[n] for n in TWIN_WEIGHTS], *[delta_w[n] for n in TWIN_WEIGHTS],
            *[new_m[n] for n in TWIN_WEIGHTS], *[new_v[n] for n in TWIN_WEIGHTS])
```

```python
import functools

import numpy as np
import jax
import jax.numpy as jnp
from jax import lax
from jax.experimental import pallas as pl
from jax.experimental.pallas import tpu as pltpu

F32 = jnp.float32
BF16 = jnp.bfloat16
MESH = pl.DeviceIdType.MESH

D = 1024
EPS = 1e-6
GRID_W = 64
GLA_H, GLA_DK, GLA_DV, GLA_RANK, GLA_TAU = 4, 128, 256, 16, 16.0
GLA_C = 64
SSM_INNER, SSM_P, SSM_HEADS, SSM_G, SSM_HPG, SSM_N = 2048, 64, 32, 4, 8, 128
SSM_C = 128
SSM_CONV, CONV_LEFT = 4, 2
D_FF = 2816
IN_WIDTHS = (512, 512, 1024, 1024, 16, 16, 2048, 2048, 512, 512, 32, 32, 1024, 1024)
D_IN = sum(IN_WIDTHS)
PERM = (6, 7, 8, 9, 0, 1, 2, 3, 12, 13, 4, 5, 10, 11)
PW = 10368
SMALL_PAD = PW - D_IN
COL_Z, COL_XBC, COL_Q, COL_K, COL_V, COL_R, COL_GA, COL_GB, COL_SM = 0, 2048, 5120, 5632, 6144, 7168, 8192, 9216, 10240
SM_LRF, SM_LRB, SM_DTF, SM_DTB = 0, 16, 32, 64
EXP_CLAMP = 80.0
ADAM_LR, ADAM_B1, ADAM_B2, ADAM_EPS, ADAM_WD, ADAM_STEP = 0.001, 0.9, 0.999, 1e-08, 0.01, 10
N_CHIPS, N_DEV = 4, 8
VMEM_LIMIT = 56 * 1024 * 1024


def _cparams(sem=None):
    return pltpu.CompilerParams(dimension_semantics=sem, vmem_limit_bytes=VMEM_LIMIT)


def _dg(a, b, ca, cb):
    return lax.dot_general(a, b, (((ca,), (cb,)), ((), ())), preferred_element_type=F32)


def _nn(a, b):
    return _dg(a, b, 1, 0)


def _nt(a, b):
    return _dg(a, b, 1, 1)


def _tn(a, b):
    return _dg(a, b, 0, 0)


def _bf(x):
    return x.astype(BF16)


def _split(x):
    hi = x.astype(BF16)
    return hi, (x - hi.astype(F32)).astype(BF16)


def _nn_x(a, b_exact):
    hi, lo = _split(a)
    return _nn(hi, b_exact) + _nn(lo, b_exact)


def _x_nn(a_exact, b):
    hi, lo = _split(b)
    return _nn(a_exact, hi) + _nn(a_exact, lo)


def _nn3(a, b_hi, b_lo):
    hi, lo = _split(a)
    return _nn(hi, b_hi) + _nn(lo, b_hi) + _nn(hi, b_lo)


def _sigmoid(x):
    return 1.0 / (1.0 + jnp.exp(-x))


def _silu(x):
    return x * _sigmoid(x)


def _dsilu(x):
    s = _sigmoid(x)
    return s * (1.0 + x * (1.0 - s))


def _softplus(x):
    return jnp.maximum(x, 0.0) + jnp.log(1.0 + jnp.exp(-jnp.abs(x)))


def _log_sigmoid(x):
    return jnp.minimum(x, 0.0) - jnp.log(1.0 + jnp.exp(-jnp.abs(x)))


def _tile(n, target, mult=8):
    best = None
    for t in range(mult, min(n, target) + 1, mult):
        if n % t == 0:
            best = t
    assert best is not None, (n, target, mult)
    return best


def _mm(a, b, mode, out_dtype, name, tm=512, tn=1024, tk=2048):
    if mode == "nn":
        (M, K), N = a.shape, b.shape[1]
    elif mode == "nt":
        (M, K), N = a.shape, b.shape[0]
    else:
        (K, M), N = a.shape, b.shape[1]
    tm, tn, tk = _tile(M, tm, 128), _tile(N, tn, 128), _tile(K, tk, 128)
    nk = K // tk
    ca, cb = {"nn": (1, 0), "nt": (1, 1), "tn": (0, 0)}[mode]

    def body(a_ref, b_ref, o_ref, *acc):
        part = _dg(a_ref[...], b_ref[...], ca, cb)
        if nk == 1:
            o_ref[...] = part.astype(out_dtype)
        else:
            k = pl.program_id(2)

            @pl.when(k == 0)
            def _():
                acc[0][...] = part

            @pl.when(k > 0)
            def _():
                acc[0][...] += part

            @pl.when(k == nk - 1)
            def _():
                o_ref[...] = acc[0][...].astype(out_dtype)

    a_spec = pl.BlockSpec((tk, tm), lambda i, j, k: (k, i)) if mode == "tn" else pl.BlockSpec((tm, tk), lambda i, j, k: (i, k))
    b_spec = pl.BlockSpec((tn, tk), lambda i, j, k: (j, k)) if mode == "nt" else pl.BlockSpec((tk, tn), lambda i, j, k: (k, j))
    return pl.pallas_call(
        body, name=name, grid=(M // tm, N // tn, nk), in_specs=[a_spec, b_spec],
        out_specs=pl.BlockSpec((tm, tn), lambda i, j, k: (i, j)),
        out_shape=jax.ShapeDtypeStruct((M, N), out_dtype),
        scratch_shapes=[pltpu.VMEM((tm, tn), F32)] if nk > 1 else [],
        compiler_params=_cparams(("arbitrary", "arbitrary", "arbitrary")),
    )(a, b)


def _place():
    return lax.axis_index("x"), lax.axis_index("y"), lax.axis_index("c")


def _flip(v, bit):
    return 1 - v if bit else v


def _allgather_small(v, name):
    R, C = v.shape

    def body(v_ref, out_ref, send_sems, recv_sems, local_sem):
        x, y, c = _place()
        me = 4 * x + 2 * y + c
        mine = pltpu.make_async_copy(v_ref, out_ref.at[me], local_sem)
        mine.start()

        def peer(r):
            return _flip(x, (r >> 2) & 1), _flip(y, (r >> 1) & 1), _flip(c, r & 1)

        sends = [pltpu.make_async_remote_copy(
            src_ref=v_ref, dst_ref=out_ref.at[me], send_sem=send_sems.at[r - 1], recv_sem=recv_sems.at[r - 1],
            device_id=peer(r), device_id_type=MESH) for r in range(1, N_DEV)]
        for cp in sends:
            cp.start()
        for r in range(1, N_DEV):
            px, py, pc = peer(r)
            pltpu.make_async_remote_copy(
                src_ref=v_ref, dst_ref=out_ref.at[4 * px + 2 * py + pc], send_sem=send_sems.at[r - 1],
                recv_sem=recv_sems.at[r - 1], device_id=(x, y, c), device_id_type=MESH).wait_recv()
        for cp in sends:
            cp.wait_send()
        mine.wait()

    return pl.pallas_call(
        body, name=name, out_shape=jax.ShapeDtypeStruct((N_DEV, R, C), v.dtype),
        in_specs=[pl.BlockSpec(memory_space=pltpu.VMEM)], out_specs=pl.BlockSpec(memory_space=pltpu.VMEM),
        scratch_shapes=[pltpu.SemaphoreType.DMA((N_DEV - 1,)), pltpu.SemaphoreType.DMA((N_DEV - 1,)), pltpu.SemaphoreType.DMA],
        compiler_params=pltpu.CompilerParams(vmem_limit_bytes=VMEM_LIMIT),
    )(v)


_CHIP_RELATIONS = ((1, 0), (0, 1), (1, 1))


def _gather_weights(shards, name):
    n = len(shards)

    def body(*refs):
        ins, outs = refs[:n], refs[n:2 * n]
        send_sems, recv_sems, local_sems = refs[2 * n:]
        x, y, c = _place()
        chip = 2 * x + y
        local = [pltpu.make_async_copy(ins[i], outs[i].at[chip], local_sems.at[i]) for i in range(n)]
        for cp in local:
            cp.start()
        sends = []
        for i in range(n):
            for j, (fx, fy) in enumerate(_CHIP_RELATIONS):
                sends.append(pltpu.make_async_remote_copy(
                    src_ref=ins[i], dst_ref=outs[i].at[chip], send_sem=send_sems.at[i, j], recv_sem=recv_sems.at[i, j],
                    device_id=(_flip(x, fx), _flip(y, fy), c), device_id_type=MESH))
        for cp in sends:
            cp.start()
        for i in range(n):
            for j, (fx, fy) in enumerate(_CHIP_RELATIONS):
                src = 2 * _flip(x, fx) + _flip(y, fy)
                pltpu.make_async_remote_copy(
                    src_ref=ins[i], dst_ref=outs[i].at[src], send_sem=send_sems.at[i, j], recv_sem=recv_sems.at[i, j],
                    device_id=(x, y, c), device_id_type=MESH).wait_recv()
        for cp in sends:
            cp.wait_send()
        for cp in local:
            cp.wait()

    any_spec = pl.BlockSpec(memory_space=pl.ANY)
    return pl.pallas_call(
        body, name=name, out_shape=[jax.ShapeDtypeStruct((N_CHIPS,) + s.shape, s.dtype) for s in shards],
        in_specs=[any_spec] * n, out_specs=[any_spec] * n,
        scratch_shapes=[pltpu.SemaphoreType.DMA((n, 3)), pltpu.SemaphoreType.DMA((n, 3)), pltpu.SemaphoreType.DMA((n,))],
    )(*shards)


def _exchange_grads(blocks, name):
    n = len(blocks)

    def body(*refs):
        ins, outs = refs[:n], refs[n:2 * n]
        send_sems, recv_sems, local_sems = refs[2 * n:]
        x, y, c = _place()
        chip = 2 * x + y
        sibling = (x, y, 1 - c)
        local = [pltpu.make_async_copy(ins[i].at[chip], outs[i].at[chip], local_sems.at[i]) for i in range(n)]
        for cp in local:
            cp.start()
        first = []
        for i in range(n):
            for j, (fx, fy) in enumerate(_CHIP_RELATIONS):
                px, py = _flip(x, fx), _flip(y, fy)
                first.append(pltpu.make_async_remote_copy(
                    src_ref=ins[i].at[2 * px + py], dst_ref=outs[i].at[chip], send_sem=send_sems.at[i, j],
                    recv_sem=recv_sems.at[i, j], device_id=(px, py, c), device_id_type=MESH))
            first.append(pltpu.make_async_remote_copy(
                src_ref=ins[i].at[chip], dst_ref=outs[i].at[N_CHIPS + chip], send_sem=send_sems.at[i, 3],
                recv_sem=recv_sems.at[i, 3], device_id=sibling, device_id_type=MESH))
        for cp in first:
            cp.start()
        passed = []
        for i in range(n):
            for j, (fx, fy) in enumerate(_CHIP_RELATIONS):
                src = 2 * _flip(x, fx) + _flip(y, fy)
                pltpu.make_async_remote_copy(
                    src_ref=ins[i].at[src], dst_ref=outs[i].at[src], send_sem=send_sems.at[i, j], recv_sem=recv_sems.at[i, j],
                    device_id=(x, y, c), device_id_type=MESH).wait_recv()
                fwd = pltpu.make_async_remote_copy(
                    src_ref=outs[i].at[src], dst_ref=outs[i].at[N_CHIPS + src], send_sem=send_sems.at[i, 4 + j],
                    recv_sem=recv_sems.at[i, 4 + j], device_id=sibling, device_id_type=MESH)
                fwd.start()
                passed.append(fwd)
        for i in range(n):
            for j in (3, 4, 5, 6):
                pltpu.make_async_remote_copy(
                    src_ref=ins[i].at[0], dst_ref=outs[i].at[0], send_sem=send_sems.at[i, j], recv_sem=recv_sems.at[i, j],
                    device_id=(x, y, c), device_id_type=MESH).wait_recv()
        for cp in first + passed:
            cp.wait_send()
        for cp in local:
            cp.wait()

    any_spec = pl.BlockSpec(memory_space=pl.ANY)
    return pl.pallas_call(
        body, name=name, out_shape=[jax.ShapeDtypeStruct((N_DEV,) + b.shape[1:], b.dtype) for b in blocks],
        in_specs=[any_spec] * n, out_specs=[any_spec] * n,
        scratch_shapes=[pltpu.SemaphoreType.DMA((n, 7)), pltpu.SemaphoreType.DMA((n, 7)), pltpu.SemaphoreType.DMA((n,))],
    )(*blocks)


def _row_spec(tr, w, col=0):
    return pl.BlockSpec((tr, w), lambda i: (i, col))


def _vec_spec(w):
    return pl.BlockSpec((1, w), lambda i: (0, 0))


def _acc_spec(w):
    return pl.BlockSpec((8, w), lambda i: (0, 0))


def _rms(x):
    return lax.rsqrt(jnp.mean(x * x, axis=-1, keepdims=True) + EPS)


def _rms_bwd(dn, n, rstd):
    return rstd * (dn - n * jnp.mean(dn * n, axis=-1, keepdims=True))


def _colsum(x):
    return jnp.sum(x, axis=0, keepdims=True)


def _zero_first(ref):
    @pl.when(pl.program_id(0) == 0)
    def _():
        ref[...] = jnp.zeros_like(ref)


def _norm_mod(xall, w, mod, n_lat_tiles, tr):
    R = xall.shape[0]

    def body(x_ref, w_ref, mod_ref, o_ref):
        x = x_ref[...]
        n = x * _rms(x) * w_ref[...]
        o_ref[...] = (n * (1.0 + mod_ref[0, 1]) + mod_ref[0, 0]).astype(BF16)

    return pl.pallas_call(
        body, name="norm1_mod", grid=(R // tr,),
        in_specs=[_row_spec(tr, D), _vec_spec(D),
                  pl.BlockSpec((1, 2, 1, D), lambda i: (jnp.where(i >= n_lat_tiles, 1, 0), 0, 0, 0))],
        out_specs=_row_spec(tr, D), out_shape=jax.ShapeDtypeStruct((R, D), BF16),
        compiler_params=_cparams(("arbitrary",)),
    )(xall, w, mod)


def _resid_norm_mod(xall, mix, g1, w2, sh2, sc2, tr):
    R = xall.shape[0]

    def body(x_ref, mix_ref, g1_ref, w_ref, sh_ref, sc_ref, h2_ref, u_ref):
        h2 = x_ref[...] + g1_ref[...] * mix_ref[...]
        h2_ref[...] = h2
        n = h2 * _rms(h2) * w_ref[...]
        u_ref[...] = (n * (1.0 + sc_ref[...]) + sh_ref[...]).astype(BF16)

    return pl.pallas_call(
        body, name="resid_norm2_mod", grid=(R // tr,),
        in_specs=[_row_spec(tr, D), _row_spec(tr, D)] + [_vec_spec(D)] * 4,
        out_specs=[_row_spec(tr, D), _row_spec(tr, D)],
        out_shape=[jax.ShapeDtypeStruct((R, D), F32), jax.ShapeDtypeStruct((R, D), BF16)],
        compiler_params=_cparams(("arbitrary",)),
    )(xall, mix, g1, w2, sh2, sc2)


def _loss_head(h2, f, target, g2, fw, n_lat_tiles, tr):
    R = h2.shape[0]

    def body(h2_ref, f_ref, t_ref, g2_ref, fw_ref, dh3_ref, df_ref, acc_ref):
        _zero_first(acc_ref)
        lat = pl.program_id(0) < n_lat_tiles
        fv = f_ref[...]
        h3 = h2_ref[...] + g2_ref[...] * fv
        rstd = _rms(h3)
        n = h3 * rstd
        err = n * fw_ref[...] - t_ref[...]
        dy = err * (1.0 / D)
        dh3 = jnp.where(lat, _rms_bwd(dy * fw_ref[...], n, rstd), 0.0)
        dh3_ref[...] = dh3
        df_ref[...] = (g2_ref[...] * dh3).astype(BF16)
        acc_ref[0:1, :] += jnp.where(lat, _colsum(dy * n), 0.0)
        acc_ref[1:2, :] += _colsum(dh3 * fv)
        acc_ref[2:3, :] += jnp.where(lat, _colsum(err * err) * (0.5 / D), 0.0)

    return pl.pallas_call(
        body, name="loss_head", grid=(R // tr,),
        in_specs=[_row_spec(tr, D), _row_spec(tr, D),
                  pl.BlockSpec((tr, D), lambda i: (jnp.minimum(i, n_lat_tiles - 1), 0)), _vec_spec(D), _vec_spec(D)],
        out_specs=[_row_spec(tr, D), _row_spec(tr, D), _acc_spec(D)],
        out_shape=[jax.ShapeDtypeStruct((R, D), F32), jax.ShapeDtypeStruct((R, D), BF16), jax.ShapeDtypeStruct((8, D), F32)],
        compiler_params=_cparams(("arbitrary",)),
    )(h2, f, target, g2, fw)


def _ffn_in_bwd(du_a, du_b, h2, dh3, mix, sc2, g1, w2, tr):
    R = h2.shape[0]

    def body(dua_ref, dub_ref, h2_ref, dh3_ref, mix_ref, sc_ref, g1_ref, w_ref, dh2_ref, dmix_ref, acc_ref):
        _zero_first(acc_ref)
        du = dua_ref[...] + dub_ref[...]
        h2 = h2_ref[...]
        rstd = _rms(h2)
        n = h2 * rstd
        dnw = du * (1.0 + sc_ref[...])
        dh2 = dh3_ref[...] + _rms_bwd(dnw * w_ref[...], n, rstd)
        dh2_ref[...] = dh2
        dmix_ref[...] = (g1_ref[...] * dh2).astype(BF16)
        acc_ref[0:1, :] += _colsum(du * n * w_ref[...])
        acc_ref[1:2, :] += _colsum(du)
        acc_ref[2:3, :] += _colsum(dnw * n)
        acc_ref[3:4, :] += _colsum(dh2 * mix_ref[...])

    return pl.pallas_call(
        body, name="ffn_in_bwd", grid=(R // tr,),
        in_specs=[_row_spec(tr, D)] * 5 + [_vec_spec(D)] * 3,
        out_specs=[_row_spec(tr, D), _row_spec(tr, D), _acc_spec(D)],
        out_shape=[jax.ShapeDtypeStruct((R, D), F32), jax.ShapeDtypeStruct((R, D), BF16), jax.ShapeDtypeStruct((8, D), F32)],
        compiler_params=_cparams(("arbitrary",)),
    )(du_a, du_b, h2, dh3, mix, sc2, g1, w2)


def _norm1_bwd(dh1, xall, dh2, w1, mod, n_lat_tiles, tr):
    R = xall.shape[0]

    def body(dh1_ref, x_ref, dh2_ref, w_ref, mod_ref, dx_ref, acc_ref):
        i = pl.program_id(0)

        @pl.when((i == 0) | (i == n_lat_tiles))
        def _():
            acc_ref[...] = jnp.zeros_like(acc_ref)

        dh1 = dh1_ref[...]
        x = x_ref[...]
        rstd = _rms(x)
        n = x * rstd
        dnw = dh1 * (1.0 + mod_ref[0, 1])
        dx_ref[...] = dh2_ref[...] + _rms_bwd(dnw * w_ref[...], n, rstd)
        acc_ref[0, 0:1, :] += _colsum(dh1 * n * w_ref[...])
        acc_ref[0, 1:2, :] += _colsum(dh1)
        acc_ref[0, 2:3, :] += _colsum(dnw * n)

    sel = lambda i: jnp.where(i >= n_lat_tiles, 1, 0)
    return pl.pallas_call(
        body, name="norm1_bwd", grid=(R // tr,),
        in_specs=[_row_spec(tr, D)] * 3 + [_vec_spec(D), pl.BlockSpec((1, 2, 1, D), lambda i: (sel(i), 0, 0, 0))],
        out_specs=[_row_spec(tr, D), pl.BlockSpec((1, 8, D), lambda i: (sel(i), 0, 0))],
        out_shape=[jax.ShapeDtypeStruct((R, D), F32), jax.ShapeDtypeStruct((2, 8, D), F32)],
        compiler_params=_cparams(("arbitrary",)),
    )(dh1, xall, dh2, w1, mod)


def _swiglu_act(gp, up, tr):
    R = gp.shape[0]

    def body(g_ref, u_ref, o_ref):
        o_ref[...] = (_silu(g_ref[...]) * u_ref[...]).astype(BF16)

    return pl.pallas_call(
        body, name="swiglu_act", grid=(R // tr,), in_specs=[_row_spec(tr, D_FF)] * 2, out_specs=_row_spec(tr, D_FF),
        out_shape=jax.ShapeDtypeStruct((R, D_FF), BF16), compiler_params=_cparams(("arbitrary",)),
    )(gp, up)


def _swiglu_act_bwd(da, gp, up, tr):
    R = gp.shape[0]

    def body(da_ref, g_ref, u_ref, dg_ref, du_ref):
        da, g = da_ref[...], g_ref[...]
        dg_ref[...] = (da * u_ref[...] * _dsilu(g)).astype(BF16)
        du_ref[...] = (da * _silu(g)).astype(BF16)

    return pl.pallas_call(
        body, name="swiglu_act_bwd", grid=(R // tr,), in_specs=[_row_spec(tr, D_FF)] * 3, out_specs=[_row_spec(tr, D_FF)] * 2,
        out_shape=[jax.ShapeDtypeStruct((R, D_FF), BF16)] * 2, compiler_params=_cparams(("arbitrary",)),
    )(da, gp, up)


def _merge(ya, yb, parts, tr):
    R = ya.shape[0]

    def body(ya_ref, yb_ref, ga_ref, gb_ref, o_ref):
        o_ref[...] = (_sigmoid(ga_ref[...]) * ya_ref[...] + _sigmoid(gb_ref[...]) * yb_ref[...]).astype(BF16)

    return pl.pallas_call(
        body, name="merge", grid=(R // tr,),
        in_specs=[_row_spec(tr, D), _row_spec(tr, D), _row_spec(tr, D, COL_GA // D), _row_spec(tr, D, COL_GB // D)],
        out_specs=_row_spec(tr, D), out_shape=jax.ShapeDtypeStruct((R, D), BF16), compiler_params=_cparams(("arbitrary",)),
    )(ya, yb, parts, parts)


def _merge_bwd(dm, ya, yb, parts, tr):
    R = ya.shape[0]

    def body(dm_ref, ya_ref, yb_ref, ga_ref, gb_ref, dya_ref, dyb_ref, dg_ref):
        dm = dm_ref[...]
        sa, sb = _sigmoid(ga_ref[...]), _sigmoid(gb_ref[...])
        dya_ref[...] = (dm * sa).astype(BF16)
        dyb_ref[...] = (dm * sb).astype(BF16)
        dg_ref[:, 0:D] = (dm * ya_ref[...] * sa * (1.0 - sa)).astype(BF16)
        dg_ref[:, D:2 * D] = (dm * yb_ref[...] * sb * (1.0 - sb)).astype(BF16)

    return pl.pallas_call(
        body, name="merge_bwd", grid=(R // tr,),
        in_specs=[_row_spec(tr, D)] * 3 + [_row_spec(tr, D, COL_GA // D), _row_spec(tr, D, COL_GB // D)],
        out_specs=[_row_spec(tr, D), _row_spec(tr, D), _row_spec(tr, 2 * D)],
        out_shape=[jax.ShapeDtypeStruct((R, D), BF16)] * 2 + [jax.ShapeDtypeStruct((R, 2 * D), BF16)],
        compiler_params=_cparams(("arbitrary",)),
    )(dm, ya, yb, parts, parts)


def _gla_out(o2, parts, gw4, tr):
    R = parts.shape[0]

    def body(o_ref, r_ref, w_ref, out_ref):
        oa = o_ref[0] + o_ref[1]
        sr = _silu(r_ref[...])
        for h in range(GLA_H):
            s = slice(h * GLA_DV, (h + 1) * GLA_DV)
            o = oa[:, s]
            out_ref[:, s] = (o * _rms(o) * w_ref[:, s] * sr[:, s]).astype(BF16)

    return pl.pallas_call(
        body, name="gla_out", grid=(R // tr,),
        in_specs=[pl.BlockSpec((2, tr, D), lambda i: (0, i, 0)), _row_spec(tr, D, COL_R // D), _vec_spec(D)],
        out_specs=_row_spec(tr, D), out_shape=jax.ShapeDtypeStruct((R, D), BF16), compiler_params=_cparams(("arbitrary",)),
    )(o2, parts, gw4)


def _gla_out_bwd(dout, o2, parts, gw4, tr):
    R = parts.shape[0]

    def body(d_ref, o_ref, r_ref, w_ref, do_ref, dr_ref, acc_ref):
        _zero_first(acc_ref)
        oa = o_ref[0] + o_ref[1]
        r = r_ref[...]
        sr = _silu(r)
        dout = d_ref[...]
        for h in range(GLA_H):
            s = slice(h * GLA_DV, (h + 1) * GLA_DV)
            o = oa[:, s]
            rstd = _rms(o)
            n = o * rstd
            w = w_ref[:, s]
            dr_ref[:, s] = (dout[:, s] * n * w * _dsilu(r[:, s])).astype(BF16)
            dnw = dout[:, s] * sr[:, s]
            do_ref[:, s] = _rms_bwd(dnw * w, n, rstd)
            acc_ref[0:1, s] += _colsum(dnw * n)

    return pl.pallas_call(
        body, name="gla_out_bwd", grid=(R // tr,),
        in_specs=[_row_spec(tr, D), pl.BlockSpec((2, tr, D), lambda i: (0, i, 0)), _row_spec(tr, D, COL_R // D), _vec_spec(D)],
        out_specs=[_row_spec(tr, D), _row_spec(tr, D), _acc_spec(D)],
        out_shape=[jax.ShapeDtypeStruct((R, D), F32), jax.ShapeDtypeStruct((R, D), BF16), jax.ShapeDtypeStruct((8, D), F32)],
        compiler_params=_cparams(("arbitrary",)),
    )(dout, o2, parts, gw4)


SSM_GW = SSM_INNER // SSM_G


def _ssd_out(y2, xbc, parts, dskip, nw, tr):
    R = parts.shape[0]

    def body(y_ref, x_ref, z_ref, ds_ref, w_ref, out_ref):
        ob = (y_ref[0] + y_ref[1] + ds_ref[...] * x_ref[...]) * _silu(z_ref[...])
        for g in range(SSM_G):
            s = slice(g * SSM_GW, (g + 1) * SSM_GW)
            o = ob[:, s]
            out_ref[:, s] = (o * _rms(o) * w_ref[:, s]).astype(BF16)

    return pl.pallas_call(
        body, name="ssd_out", grid=(R // tr,),
        in_specs=[pl.BlockSpec((2, tr, SSM_INNER), lambda i: (0, i, 0)), _row_spec(tr, SSM_INNER), _row_spec(tr, SSM_INNER, COL_Z // SSM_INNER),
                  _vec_spec(SSM_INNER), _vec_spec(SSM_INNER)],
        out_specs=_row_spec(tr, SSM_INNER), out_shape=jax.ShapeDtypeStruct((R, SSM_INNER), BF16),
        compiler_params=_cparams(("arbitrary",)),
    )(y2, xbc, parts, dskip, nw)


def _ssd_out_bwd(dout, y2, xbc, parts, dskip, nw, tr):
    R = parts.shape[0]

    def body(d_ref, y_ref, x_ref, z_ref, ds_ref, w_ref, dy_ref, dz_ref, dxs_ref, acc_ref):
        _zero_first(acc_ref)
        x, z = x_ref[...], z_ref[...]
        pre = y_ref[0] + y_ref[1] + ds_ref[...] * x
        sz = _silu(z)
        ob = pre * sz
        dout = d_ref[...]
        for g in range(SSM_G):
            s = slice(g * SSM_GW, (g + 1) * SSM_GW)
            o = ob[:, s]
            rstd = _rms(o)
            n = o * rstd
            dob = _rms_bwd(dout[:, s] * w_ref[:, s], n, rstd)
            dz_ref[:, s] = (dob * pre[:, s] * _dsilu(z[:, s])).astype(BF16)
            dy = dob * sz[:, s]
            dy_ref[:, s] = dy
            dxs_ref[:, s] = dy * ds_ref[:, s]
            acc_ref[0:1, s] += _colsum(dout[:, s] * n)
            acc_ref[1:2, s] += _colsum(dy * x[:, s])

    return pl.pallas_call(
        body, name="ssd_out_bwd", grid=(R // tr,),
        in_specs=[_row_spec(tr, SSM_INNER), pl.BlockSpec((2, tr, SSM_INNER), lambda i: (0, i, 0)), _row_spec(tr, SSM_INNER),
                  _row_spec(tr, SSM_INNER, COL_Z // SSM_INNER), _vec_spec(SSM_INNER), _vec_spec(SSM_INNER)],
        out_specs=[_row_spec(tr, SSM_INNER), _row_spec(tr, SSM_INNER), _row_spec(tr, SSM_INNER), _acc_spec(SSM_INNER)],
        out_shape=[jax.ShapeDtypeStruct((R, SSM_INNER), F32), jax.ShapeDtypeStruct((R, SSM_INNER), BF16),
                   jax.ShapeDtypeStruct((R, SSM_INNER), F32), jax.ShapeDtypeStruct((8, SSM_INNER), F32)],
        compiler_params=_cparams(("arbitrary",)),
    )(dout, y2, xbc, parts, dskip, nw)


CONV_W = SSM_INNER + 2 * SSM_G * SSM_N
CONV_BLK = 1024


def _conv_masks(tr, is_ctx):
    t = lax.broadcasted_iota(jnp.int32, (tr, 1), 0)
    pos = jnp.where(is_ctx, t, t & (GRID_W - 1))
    seg = jnp.where(is_ctx, tr, GRID_W)
    return pos, seg


def _shift_rows(u, s, tr):
    return u if s == 0 else pltpu.roll(u, (-s) % tr, 0)


def _conv_fwd(parts, cw, cb, n_lat_tiles, tr):
    R = parts.shape[0]

    def body(u_ref, w_ref, b_ref, o_ref):
        pos, seg = _conv_masks(tr, pl.program_id(0) >= n_lat_tiles)
        u = u_ref[...]
        acc = jnp.zeros_like(u) + b_ref[...]
        for j in range(SSM_CONV):
            s = j - CONV_LEFT
            ok = (pos + s >= 0) & (pos + s < seg)
            acc = acc + jnp.where(ok, _shift_rows(u, s, tr), 0.0) * w_ref[j:j + 1, :]
        o_ref[...] = _silu(acc)

    return pl.pallas_call(
        body, name="conv_fwd", grid=(R // tr, CONV_W // CONV_BLK),
        in_specs=[pl.BlockSpec((tr, CONV_BLK), lambda i, j: (i, COL_XBC // CONV_BLK + j)),
                  pl.BlockSpec((SSM_CONV, CONV_BLK), lambda i, j: (0, j)), pl.BlockSpec((1, CONV_BLK), lambda i, j: (0, j))],
        out_specs=pl.BlockSpec((tr, CONV_BLK), lambda i, j: (i, j)), out_shape=jax.ShapeDtypeStruct((R, CONV_W), F32),
        compiler_params=_cparams(("arbitrary", "arbitrary")),
    )(parts, cw, cb)


def _conv_bwd(dxbc, parts, cw, cb, n_lat_tiles, tr):
    R = parts.shape[0]

    def body(d_ref, u_ref, w_ref, b_ref, du_ref, acc_ref):
        @pl.when(pl.program_id(1) == 0)
        def _():
            acc_ref[...] = jnp.zeros_like(acc_ref)

        pos, seg = _conv_masks(tr, pl.program_id(1) >= n_lat_tiles)
        u = u_ref[...]
        pre = jnp.zeros_like(u) + b_ref[...]
        taps = []
        for j in range(SSM_CONV):
            s = j - CONV_LEFT
            ok = (pos + s >= 0) & (pos + s < seg)
            tap = jnp.where(ok, _shift_rows(u, s, tr), 0.0)
            taps.append(tap)
            pre = pre + tap * w_ref[j:j + 1, :]
        dpre = d_ref[...] * _dsilu(pre)
        du = jnp.zeros_like(u)
        for j in range(SSM_CONV):
            s = j - CONV_LEFT
            acc_ref[j:j + 1, :] += _colsum(dpre * taps[j])
            ok = (pos - s >= 0) & (pos - s < seg)
            du = du + jnp.where(ok, _shift_rows(dpre, -s, tr), 0.0) * w_ref[j:j + 1, :]
        acc_ref[SSM_CONV:SSM_CONV + 1, :] += _colsum(dpre)
        du_ref[...] = du.astype(BF16)

    return pl.pallas_call(
        body, name="conv_bwd", grid=(CONV_W // CONV_BLK, R // tr),
        in_specs=[pl.BlockSpec((tr, CONV_BLK), lambda j, i: (i, j)),
                  pl.BlockSpec((tr, CONV_BLK), lambda j, i: (i, COL_XBC // CONV_BLK + j)),
                  pl.BlockSpec((SSM_CONV, CONV_BLK), lambda j, i: (0, j)), pl.BlockSpec((1, CONV_BLK), lambda j, i: (0, j))],
        out_specs=[pl.BlockSpec((tr, CONV_BLK), lambda j, i: (i, j)), pl.BlockSpec((8, CONV_BLK), lambda j, i: (0, j))],
        out_shape=[jax.ShapeDtypeStruct((R, CONV_W), BF16), jax.ShapeDtypeStruct((8, CONV_W), F32)],
        compiler_params=_cparams(("arbitrary", "arbitrary")),
    )(dxbc, parts, cw, cb)


def _chunk_row_block(d, i, n_lat, n_ctx):
    fwd = jnp.where(i < n_ctx, n_lat + i, i - n_ctx)
    rev = jnp.where(i < n_ctx, n_lat + n_ctx - 1 - i, n_lat + n_ctx - 1 - i)
    return jnp.where(d == 0, fwd, rev)


def _tri(n, d, transpose=False):
    row = lax.broadcasted_iota(jnp.int32, (n, n), 0)
    col = lax.broadcasted_iota(jnp.int32, (n, n), 1)
    diff = (col - row) if transpose else (row - col)
    return diff * (1 - 2 * d) >= 0


def _gla_gates(sm, uhi, ulo, bias, d):
    pre = _nn3(sm, uhi, ulo) + bias
    g = _log_sigmoid(pre) * (1.0 / GLA_TAU)
    mask = _tri(GLA_C, d)
    b = _x_nn(mask.astype(BF16), g)
    b_tot = _colsum(g)
    b_ref = b[GLA_C // 2:GLA_C // 2 + 1, :]
    e_q = jnp.exp(jnp.minimum(b - b_ref, EXP_CLAMP))
    e_k = jnp.exp(jnp.minimum(b_ref - b, EXP_CLAMP))
    return pre, mask, b_tot, e_q, e_k, jnp.exp(b), jnp.exp(b_tot - b)


def _gla_specs(n_lat, n_ctx, step_of):
    rb = lambda d, i: _chunk_row_block(d, step_of(i), n_lat, n_ctx)
    return [
        pl.BlockSpec((GLA_C, GLA_DK), lambda d, h, i: (rb(d, i), COL_Q // GLA_DK + h)),
        pl.BlockSpec((GLA_C, GLA_DK), lambda d, h, i: (rb(d, i), COL_K // GLA_DK + h)),
        pl.BlockSpec((GLA_C, GLA_DV), lambda d, h, i: (rb(d, i), COL_V // GLA_DV + h)),
        pl.BlockSpec((GLA_C, 128), lambda d, h, i: (rb(d, i), COL_SM // 128)),
        pl.BlockSpec((1, 1, 128, GLA_DK), lambda d, h, i: (d, h, 0, 0)),
        pl.BlockSpec((1, 1, 128, GLA_DK), lambda d, h, i: (d, h, 0, 0)),
        pl.BlockSpec((1, 1, 1, GLA_DK), lambda d, h, i: (d, h, 0, 0)),
    ], rb


def _gla_fwd(parts, uhi, ulo, bias, n_lat, n_ctx):
    R = parts.shape[0]
    n_steps = n_lat + n_ctx
    scale = GLA_DK ** -0.5

    def body(q_ref, k_ref, v_ref, sm_ref, uhi_ref, ulo_ref, bias_ref, o_ref, hist_ref, st):
        d = pl.program_id(0)

        @pl.when(pl.program_id(2) == 0)
        def _():
            st[...] = jnp.zeros_like(st)

        _, mask, b_tot, e_q, e_k, e_in, e_out = _gla_gates(sm_ref[...], uhi_ref[0, 0], ulo_ref[0, 0], bias_ref[0, 0], d)
        q, k, v = q_ref[...] * scale, k_ref[...], _bf(v_ref[...])
        att = jnp.where(mask, _nt(_bf(q * e_q), _bf(k * e_k)), 0.0)
        s_in = st[...]
        hist_ref[0, 0, 0] = s_in
        o_ref[0] = _nn(_bf(att), v) + _nt(_bf(q * e_in), _bf(s_in))
        st[...] = jnp.exp(b_tot) * s_in + _tn(v, _bf(k * e_out))

    in_specs, rb = _gla_specs(n_lat, n_ctx, lambda i: i)
    return pl.pallas_call(
        body, name="gla_fwd", grid=(2, GLA_H, n_steps), in_specs=in_specs,
        out_specs=[pl.BlockSpec((1, GLA_C, GLA_DV), lambda d, h, i: (d, rb(d, i), h)),
                   pl.BlockSpec((1, 1, 1, GLA_DV, GLA_DK), lambda d, h, i: (d, i, h, 0, 0))],
        out_shape=[jax.ShapeDtypeStruct((2, R, GLA_H * GLA_DV), F32),
                   jax.ShapeDtypeStruct((2, n_steps, GLA_H, GLA_DV, GLA_DK), F32)],
        scratch_shapes=[pltpu.VMEM((GLA_DV, GLA_DK), F32)],
        compiler_params=_cparams(("arbitrary", "arbitrary", "arbitrary")),
    )(parts, parts, parts, parts, uhi, ulo, bias)


def _gla_bwd(do, parts, uhi, ulo, bias, hist, n_lat, n_ctx):
    R = parts.shape[0]
    n_steps = n_lat + n_ctx
    scale = GLA_DK ** -0.5
    step_of = lambda j: n_steps - 1 - j

    def body(q_ref, k_ref, v_ref, sm_ref, uhi_ref, ulo_ref, bias_ref, do_ref, hist_ref, dq_ref, dk_ref, dv_ref, dp_ref, dst):
        d = pl.program_id(0)

        @pl.when(pl.program_id(2) == 0)
        def _():
            dst[...] = jnp.zeros_like(dst)

        pre, mask, b_tot, e_q, e_k, e_in, e_out = _gla_gates(sm_ref[...], uhi_ref[0, 0], ulo_ref[0, 0], bias_ref[0, 0], d)
        q, k, v = q_ref[...] * scale, k_ref[...], _bf(v_ref[...])
        dout = _bf(do_ref[...])
        s_in, ds = hist_ref[0, 0, 0], dst[...]
        qb, kb, q_in, k_out = _bf(q * e_q), _bf(k * e_k), _bf(q * e_in), k * e_out
        att = jnp.where(mask, _nt(qb, kb), 0.0)
        datt = _bf(jnp.where(mask, _nt(dout, v), 0.0))
        dv_ref[0] = _tn(_bf(att), dout) + _nt(_bf(k_out), _bf(ds))
        dq = _nn(datt, kb) * e_q + _nn(dout, _bf(s_in)) * e_in
        dk_out = _nn(v, _bf(ds))
        dk = _tn(datt, qb) * e_k + dk_out * e_out
        dq_ref[0] = dq * scale
        dk_ref[0] = dk
        db_tot = _colsum(dk_out * k_out) + jnp.exp(b_tot) * _colsum(ds * s_in)
        dg = _x_nn(_tri(GLA_C, d, transpose=True).astype(BF16), dq * q - dk * k) + db_tot
        dp_ref[0] = dg * (1.0 / GLA_TAU) * _sigmoid(-pre)
        dst[...] = jnp.exp(b_tot) * ds + _tn(dout, q_in)

    in_specs, rb = _gla_specs(n_lat, n_ctx, step_of)
    in_specs += [pl.BlockSpec((GLA_C, GLA_DV), lambda d, h, j: (rb(d, j), h)),
                 pl.BlockSpec((1, 1, 1, GLA_DV, GLA_DK), lambda d, h, j: (d, step_of(j), h, 0, 0))]
    qk_spec = pl.BlockSpec((1, GLA_C, GLA_DK), lambda d, h, j: (d, rb(d, j), h))
    return pl.pallas_call(
        body, name="gla_bwd", grid=(2, GLA_H, n_steps), in_specs=in_specs,
        out_specs=[qk_spec, qk_spec, pl.BlockSpec((1, GLA_C, GLA_DV), lambda d, h, j: (d, rb(d, j), h)), qk_spec],
        out_shape=[jax.ShapeDtypeStruct((2, R, GLA_H * GLA_DK), F32)] * 2
        + [jax.ShapeDtypeStruct((2, R, GLA_H * GLA_DV), F32), jax.ShapeDtypeStruct((2, R, GLA_H * GLA_DK), F32)],
        scratch_shapes=[pltpu.VMEM((GLA_DV, GLA_DK), F32)],
        compiler_params=_cparams(("arbitrary", "arbitrary", "arbitrary")),
    )(parts, parts, parts, parts, uhi, ulo, bias, do, hist)


def _ssd_consts(dt_bias, a_log):
    sel = np.zeros((2, SSM_G, 128, 128), np.float32)
    for d, base in enumerate((SM_DTF, SM_DTB)):
        for g in range(SSM_G):
            for e in range(SSM_HPG):
                sel[d, g, base + SSM_HPG * g + e, e] = 1.0
    e512 = np.zeros((128, SSM_GW), np.float32)
    e1024 = np.zeros((128, SSM_HPG * 128), np.float32)
    for e in range(SSM_HPG):
        e512[e, SSM_P * e:SSM_P * (e + 1)] = 1.0
        e1024[e, 128 * e:128 * (e + 1)] = 1.0
    pad = lambda v: jnp.pad(v.reshape(2, SSM_G, 1, SSM_HPG), ((0, 0), (0, 0), (0, 0), (0, 128 - SSM_HPG)))
    return dict(
        sel=jnp.asarray(sel, BF16), sel_t=jnp.asarray(sel.transpose(0, 1, 3, 2), BF16),
        e512=jnp.asarray(e512, BF16), e512_t=jnp.asarray(e512.T, BF16), e1024=jnp.asarray(e1024, BF16),
        dtb=pad(dt_bias), a=pad(-jnp.exp(a_log)))


def _ssd_common(sm, sel, dtb, a_neg, e512, e1024, d):
    dtr8 = _nn_x(sm, sel) + dtb
    dt8 = _softplus(dtr8)
    a8 = a_neg * dt8
    mask = _tri(SSM_C, d)
    mask_t = _tri(SSM_C, d, transpose=True).astype(BF16)
    cum8 = _x_nn(mask.astype(BF16), a8)
    a_hi, a_lo = _split(a8)
    cum_t = _tn(a_hi, mask_t) + _tn(a_lo, mask_t)
    a_exp = _nn_x(a8, e512)
    return dict(dtr8=dtr8, a8=a8, mask=mask, mask_t=mask_t, cum_t=cum_t, dt_exp=_nn_x(dt8, e512), a_exp=a_exp,
                cum_exp=_nn_x(cum8, e512), cum_rep=_nn_x(cum8, e1024), tot_exp=_colsum(a_exp))


def _ssd_decay(cm, e):
    diff = cm["cum_rep"][:, 128 * e:128 * (e + 1)] - cm["cum_t"][e:e + 1, :]
    return jnp.where(cm["mask"], jnp.exp(jnp.minimum(diff, 0.0)), 0.0)


def _ssd_specs(n_lat, n_ctx, step_of):
    rb = lambda d, i: _chunk_row_block(d, step_of(i), n_lat, n_ctx)
    g4 = lambda w: pl.BlockSpec((1, 1, 128, w), lambda d, g, i: (d, g, 0, 0))
    return [
        pl.BlockSpec((SSM_C, SSM_GW), lambda d, g, i: (rb(d, i), g)),
        pl.BlockSpec((SSM_C, SSM_N), lambda d, g, i: (rb(d, i), SSM_INNER // SSM_N + g)),
        pl.BlockSpec((SSM_C, SSM_N), lambda d, g, i: (rb(d, i), SSM_INNER // SSM_N + SSM_G + g)),
        pl.BlockSpec((SSM_C, 128), lambda d, g, i: (rb(d, i), COL_SM // 128)),
        g4(128),
        pl.BlockSpec((1, 1, 1, 128), lambda d, g, i: (d, g, 0, 0)),
        pl.BlockSpec((1, 1, 1, 128), lambda d, g, i: (d, g, 0, 0)),
        pl.BlockSpec((128, SSM_GW), lambda d, g, i: (0, 0)),
        pl.BlockSpec((128, SSM_HPG * 128), lambda d, g, i: (0, 0)),
    ], rb


def _ssd_fwd(xbc, parts, k, n_lat, n_ctx):
    R = parts.shape[0]
    n_steps = n_lat + n_ctx

    def body(x_ref, b_ref, c_ref, sm_ref, sel_ref, dtb_ref, a_ref, e512_ref, e1024_ref, y_ref, hist_ref, st):
        d = pl.program_id(0)

        @pl.when(pl.program_id(2) == 0)
        def _():
            st[...] = jnp.zeros_like(st)

        cm = _ssd_common(sm_ref[...], sel_ref[0, 0], dtb_ref[0, 0], a_ref[0, 0], e512_ref[...], e1024_ref[...], d)
        bm, cmat = _bf(b_ref[...]), _bf(c_ref[...])
        xdt = x_ref[...] * cm["dt_exp"]
        cb = _nt(cmat, bm)
        ys = [_nn(_bf(cb * _ssd_decay(cm, e)), _bf(xdt[:, SSM_P * e:SSM_P * (e + 1)])) for e in range(SSM_HPG)]
        s_in = st[...]
        hist_ref[0, 0, 0] = s_in
        y_ref[0] = jnp.concatenate(ys, axis=1) + jnp.exp(cm["cum_exp"]) * _nn(cmat, _bf(s_in))
        st[...] = jnp.exp(cm["tot_exp"]) * s_in + _tn(bm, _bf(xdt * jnp.exp(cm["tot_exp"] - cm["cum_exp"])))

    in_specs, rb = _ssd_specs(n_lat, n_ctx, lambda i: i)
    return pl.pallas_call(
        body, name="ssd_fwd", grid=(2, SSM_G, n_steps), in_specs=in_specs,
        out_specs=[pl.BlockSpec((1, SSM_C, SSM_GW), lambda d, g, i: (d, rb(d, i), g)),
                   pl.BlockSpec((1, 1, 1, SSM_N, SSM_GW), lambda d, g, i: (d, i, g, 0, 0))],
        out_shape=[jax.ShapeDtypeStruct((2, R, SSM_INNER), F32),
                   jax.ShapeDtypeStruct((2, n_steps, SSM_G, SSM_N, SSM_GW), F32)],
        scratch_shapes=[pltpu.VMEM((SSM_N, SSM_GW), F32)],
        compiler_params=_cparams(("arbitrary", "arbitrary", "arbitrary")),
    )(xbc, xbc, xbc, parts, k["sel"], k["dtb"], k["a"], k["e512"], k["e1024"])


def _ssd_bwd(dy, xbc, parts, k, hist, n_lat, n_ctx):
    R = parts.shape[0]
    n_steps = n_lat + n_ctx
    step_of = lambda j: n_steps - 1 - j

    def body(x_ref, b_ref, c_ref, sm_ref, sel_ref, dtb_ref, a_ref, e512_ref, e1024_ref, selt_ref, e512t_ref, dy_ref,
             hist_ref, dx_ref, db_ref, dc_ref, dsm_ref, acc_ref, dst):
        d = pl.program_id(0)

        @pl.when(pl.program_id(2) == 0)
        def _():
            dst[...] = jnp.zeros_like(dst)
            acc_ref[...] = jnp.zeros_like(acc_ref)

        a_neg, e512_t = a_ref[0, 0], e512t_ref[...]
        cm = _ssd_common(sm_ref[...], sel_ref[0, 0], dtb_ref[0, 0], a_neg, e512_ref[...], e1024_ref[...], d)
        x, dyv = x_ref[...], dy_ref[...]
        bm, cmat = _bf(b_ref[...]), _bf(c_ref[...])
        xdt = x * cm["dt_exp"]
        cb = _nt(cmat, bm)
        s_in, ds = hist_ref[0, 0, 0], dst[...]
        w = jnp.exp(cm["tot_exp"] - cm["cum_exp"])
        z = _nn(bm, _bf(ds))
        decay_in = jnp.exp(cm["cum_exp"])
        gy = _bf(dyv * decay_in)
        dcb = jnp.zeros((SSM_C, SSM_C), F32)
        dxs, crossing = [], []
        row = lax.broadcasted_iota(jnp.int32, (SSM_C, SSM_C), 0)
        col = lax.broadcasted_iota(jnp.int32, (SSM_C, SSM_C), 1)
        eye = (row == col).astype(BF16)
        before = (cm["mask_t"] - eye)
        for e in range(SSM_HPG):
            s = slice(SSM_P * e, SSM_P * (e + 1))
            lm = _ssd_decay(cm, e)
            dy_e = _bf(dyv[:, s])
            m_e = cb * lm
            dm_e = _nt(dy_e, _bf(xdt[:, s]))
            dcb = dcb + dm_e * lm
            dxs.append(_tn(_bf(m_e), dy_e))
            through = jnp.where(cm["mask"], _nn_x(dm_e * m_e, before), 0.0)
            crossing.append(_colsum(through))
        da_rows = jnp.concatenate(crossing + [jnp.zeros((128 - SSM_HPG, SSM_C), F32)], axis=0)
        r_hi, r_lo = _split(da_rows)
        da8_intra = _tn(r_hi, eye) + _tn(r_lo, eye)
        dx_state = w * z
        dxdt = jnp.concatenate(dxs, axis=1) + dx_state
        dcb = _bf(dcb)
        c_s = _nn(cmat, _bf(s_in))
        dc_ref[0] = _nn(dcb, bm) + _nt(gy, _bf(s_in))
        db_ref[0] = _tn(dcb, cmat) + _nt(_bf(w * xdt), _bf(ds))
        dst[...] = jnp.exp(cm["tot_exp"]) * ds + _tn(cmat, gy)
        dcum8 = _nn_x(dyv * decay_in * c_s - xdt * dx_state, e512_t)
        tot8 = _colsum(cm["a8"])
        dtot8 = _colsum(_nn_x(xdt * dx_state, e512_t)) + jnp.exp(tot8) * _colsum(_nn_x(ds * s_in, e512_t))
        da8 = da8_intra + _x_nn(cm["mask_t"], dcum8) + dtot8
        ddt8 = da8 * a_neg + _nn_x(dxdt * x, e512_t)
        dsm_ref[0, 0] = _nn_x(ddt8 * _sigmoid(cm["dtr8"]), selt_ref[0, 0])
        dx_ref[0] = dxdt * cm["dt_exp"]
        acc_ref[0, 0, 0:1, :] += _colsum(da8 * cm["a8"])

    in_specs, rb = _ssd_specs(n_lat, n_ctx, step_of)
    in_specs += [pl.BlockSpec((1, 1, 128, 128), lambda d, g, j: (d, g, 0, 0)),
                 pl.BlockSpec((SSM_GW, 128), lambda d, g, j: (0, 0)),
                 pl.BlockSpec((SSM_C, SSM_GW), lambda d, g, j: (rb(d, j), g)),
                 pl.BlockSpec((1, 1, 1, SSM_N, SSM_GW), lambda d, g, j: (d, step_of(j), g, 0, 0))]
    bc_spec = pl.BlockSpec((1, SSM_C, SSM_N), lambda d, g, j: (d, rb(d, j), g))
    return pl.pallas_call(
        body, name="ssd_bwd", grid=(2, SSM_G, n_steps), in_specs=in_specs,
        out_specs=[pl.BlockSpec((1, SSM_C, SSM_GW), lambda d, g, j: (d, rb(d, j), g)), bc_spec, bc_spec,
                   pl.BlockSpec((1, 1, SSM_C, 128), lambda d, g, j: (d, g, rb(d, j), 0)),
                   pl.BlockSpec((1, 1, 8, 128), lambda d, g, j: (d, g, 0, 0))],
        out_shape=[jax.ShapeDtypeStruct((2, R, SSM_INNER), F32), jax.ShapeDtypeStruct((2, R, SSM_G * SSM_N), F32),
                   jax.ShapeDtypeStruct((2, R, SSM_G * SSM_N), F32), jax.ShapeDtypeStruct((2, SSM_G, R, 128), F32),
                   jax.ShapeDtypeStruct((2, SSM_G, 8, 128), F32)],
        scratch_shapes=[pltpu.VMEM((SSM_N, SSM_GW), F32)],
        compiler_params=_cparams(("arbitrary", "arbitrary", "arbitrary")),
    )(xbc, xbc, xbc, parts, k["sel"], k["dtb"], k["a"], k["e512"], k["e1024"], k["sel_t"], k["e512_t"], dy, hist)


def _gla_assemble(dq, dk, dv, tr):
    R = dq.shape[1]
    qk = GLA_H * GLA_DK

    def body(dq_ref, dk_ref, dv_ref, o_ref):
        o_ref[:, 0:qk] = (dq_ref[0] + dq_ref[1]).astype(BF16)
        o_ref[:, qk:2 * qk] = (dk_ref[0] + dk_ref[1]).astype(BF16)
        o_ref[:, 2 * qk:] = (dv_ref[0] + dv_ref[1]).astype(BF16)

    s3 = lambda w: pl.BlockSpec((2, tr, w), lambda i: (0, i, 0))
    return pl.pallas_call(
        body, name="gla_assemble", grid=(R // tr,), in_specs=[s3(qk), s3(qk), s3(D)], out_specs=_row_spec(tr, 2 * D),
        out_shape=jax.ShapeDtypeStruct((R, 2 * D), BF16), compiler_params=_cparams(("arbitrary",)),
    )(dq, dk, dv)


def _xbc_assemble(dx, db, dc, dxs_skip, tr):
    R = dx.shape[1]
    bc = SSM_G * SSM_N

    def body(dx_ref, db_ref, dc_ref, sk_ref, o_ref):
        o_ref[:, 0:SSM_INNER] = dx_ref[0] + dx_ref[1] + sk_ref[...]
        o_ref[:, SSM_INNER:SSM_INNER + bc] = db_ref[0] + db_ref[1]
        o_ref[:, SSM_INNER + bc:] = dc_ref[0] + dc_ref[1]

    s3 = lambda w: pl.BlockSpec((2, tr, w), lambda i: (0, i, 0))
    return pl.pallas_call(
        body, name="xbc_assemble", grid=(R // tr,), in_specs=[s3(SSM_INNER), s3(bc), s3(bc), _row_spec(tr, SSM_INNER)],
        out_specs=_row_spec(tr, CONV_W), out_shape=jax.ShapeDtypeStruct((R, CONV_W), F32),
        compiler_params=_cparams(("arbitrary",)),
    )(dx, db, dc, dxs_skip)


def _small_assemble(dp, dsm, parts, ut_hi, ut_lo, tr):
    R = parts.shape[0]
    qk = GLA_H * GLA_DK

    def body(dp_ref, dsm_ref, sm_ref, uth_ref, utl_ref, o_ref, dup_ref, acc_ref, acc2_ref):
        @pl.when(pl.program_id(0) == 0)
        def _():
            dup_ref[...] = jnp.zeros_like(dup_ref)
            acc_ref[...] = jnp.zeros_like(acc_ref)
            acc2_ref[...] = jnp.zeros_like(acc2_ref)

        ssd = dsm_ref[0, 0]
        for d in range(2):
            for g in range(SSM_G):
                if d or g:
                    ssd = ssd + dsm_ref[d, g]
        acc2_ref[0:1, :] += _colsum(ssd)
        sm_hi, sm_lo = _split(sm_ref[...])
        out = ssd
        for d in range(2):
            dpd = dp_ref[d]
            out = out + _nn3(dpd, uth_ref[d], utl_ref[d])
            p_hi, p_lo = _split(dpd)
            dup_ref[d] += _tn(sm_hi, p_hi) + _tn(sm_lo, p_hi) + _tn(sm_hi, p_lo)
            acc_ref[d:d + 1, :] += _colsum(dpd)
        o_ref[...] = out.astype(BF16)

    return pl.pallas_call(
        body, name="small_assemble", grid=(R // tr,),
        in_specs=[pl.BlockSpec((2, tr, qk), lambda i: (0, i, 0)), pl.BlockSpec((2, SSM_G, tr, 128), lambda i: (0, 0, i, 0)),
                  _row_spec(tr, 128, COL_SM // 128), pl.BlockSpec((2, qk, 128), lambda i: (0, 0, 0)),
                  pl.BlockSpec((2, qk, 128), lambda i: (0, 0, 0))],
        out_specs=[_row_spec(tr, 128), pl.BlockSpec((2, 128, qk), lambda i: (0, 0, 0)), _acc_spec(qk), _acc_spec(128)],
        out_shape=[jax.ShapeDtypeStruct((R, 128), BF16), jax.ShapeDtypeStruct((2, 128, qk), F32),
                   jax.ShapeDtypeStruct((8, qk), F32), jax.ShapeDtypeStruct((8, 128), F32)],
        compiler_params=_cparams(("arbitrary",)),
    )(dp, dsm, parts, ut_hi, ut_lo)


ADA_ROWS = 16
ADA_TILE = 512


def _dot3_f32(a, b, ca, cb):
    a_hi, a_lo = _split(a)
    b_hi, b_lo = _split(b)
    return _dg(a_hi, b_hi, ca, cb) + _dg(a_lo, b_hi, ca, cb) + _dg(a_hi, b_lo, ca, cb)


def _ada_fwd(cvec, w, b):
    cols = w.shape[1]

    def body(c_ref, w_ref, b_ref, o_ref):
        o_ref[...] = _dot3_f32(_silu(c_ref[...]), w_ref[...], 1, 0) + b_ref[...]

    return pl.pallas_call(
        body, name="ada_fwd", grid=(cols // ADA_TILE,),
        in_specs=[pl.BlockSpec((ADA_ROWS, D), lambda j: (0, 0)), pl.BlockSpec((D, ADA_TILE), lambda j: (0, j)),
                  pl.BlockSpec((1, ADA_TILE), lambda j: (0, j))],
        out_specs=pl.BlockSpec((ADA_ROWS, ADA_TILE), lambda j: (0, j)), out_shape=jax.ShapeDtypeStruct((ADA_ROWS, cols), F32),
        compiler_params=_cparams(("arbitrary",)),
    )(cvec, w, b)


def _adam(w, g, m, v):
    m2 = ADAM_B1 * m + (1.0 - ADAM_B1) * g
    v2 = ADAM_B2 * v + (1.0 - ADAM_B2) * (g * g)
    m_hat = m2 / (1.0 - ADAM_B1 ** ADAM_STEP)
    v_hat = v2 / (1.0 - ADAM_B2 ** ADAM_STEP)
    return -ADAM_LR * (m_hat / (jnp.sqrt(v_hat) + ADAM_EPS) + ADAM_WD * w), m2, v2


def _wada_bwd_adam(cvec, dada, w, m, v):
    rows, cols = w.shape
    tr = _tile(rows, 256, 128)

    def body(c_ref, d_ref, w_ref, m_ref, v_ref, g_ref, dl_ref, m2_ref, v2_ref, p_ref):
        wv = w_ref[...]
        g = _dot3_f32(_silu(c_ref[...]), d_ref[...], 0, 0)
        g_ref[...] = g
        dl_ref[...], m2_ref[...], v2_ref[...] = _adam(wv, g, m_ref[...], v_ref[...])
        p_ref[...] = _dot3_f32(d_ref[...], wv, 1, 1)

    blk = pl.BlockSpec((tr, cols), lambda i: (i, 0))
    return pl.pallas_call(
        body, name="wada_bwd_adam", grid=(rows // tr,),
        in_specs=[pl.BlockSpec((ADA_ROWS, tr), lambda i: (0, i)), pl.BlockSpec((ADA_ROWS, cols), lambda i: (0, 0)), blk, blk, blk],
        out_specs=[blk, blk, blk, blk, pl.BlockSpec((ADA_ROWS, tr), lambda i: (0, i))],
        out_shape=[jax.ShapeDtypeStruct((rows, cols), F32)] * 4 + [jax.ShapeDtypeStruct((ADA_ROWS, rows), F32)],
        compiler_params=_cparams(("arbitrary",)),
    )(cvec, dada, w, m, v)


def _reduce_adam(parts8, w, m, v, name):
    rows, cols = w.shape
    tr = _tile(rows, 64, 16)

    def body(p_ref, w_ref, m_ref, v_ref, g_ref, dl_ref, m2_ref, v2_ref):
        g = p_ref[0].astype(F32) + p_ref[N_CHIPS].astype(F32)
        for j in range(1, N_CHIPS):
            g = g + (p_ref[j].astype(F32) + p_ref[N_CHIPS + j].astype(F32))
        g_ref[...] = g
        dl_ref[...], m2_ref[...], v2_ref[...] = _adam(w_ref[...], g, m_ref[...], v_ref[...])

    blk = pl.BlockSpec((tr, cols), lambda i: (i, 0))
    return pl.pallas_call(
        body, name=name, grid=(rows // tr,), in_specs=[pl.BlockSpec((N_DEV, tr, cols), lambda i: (0, i, 0)), blk, blk, blk],
        out_specs=[blk] * 4, out_shape=[jax.ShapeDtypeStruct((rows, cols), F32)] * 4, compiler_params=_cparams(("arbitrary",)),
    )(parts8, w, m, v)


SMALL_W = 1024


def _sum8(g8):
    rows = g8.shape[1]

    def body(g_ref, o_ref):
        s = g_ref[0]
        for j in range(1, N_DEV):
            s = s + g_ref[j]
        o_ref[...] = s

    return pl.pallas_call(
        body, name="sum8", out_shape=jax.ShapeDtypeStruct((rows, SMALL_W), F32),
        in_specs=[pl.BlockSpec(memory_space=pltpu.VMEM)], out_specs=pl.BlockSpec(memory_space=pltpu.VMEM),
        compiler_params=pltpu.CompilerParams(vmem_limit_bytes=VMEM_LIMIT),
    )(g8)


def _cctx_grad(p8, c_ctx):
    def body(p_ref, c_ref, o_ref):
        s = p_ref[0]
        for chip in range(1, N_CHIPS):
            s = s + p_ref[2 * chip]
        o_ref[...] = s * _dsilu(c_ref[...])

    return pl.pallas_call(
        body, name="cctx_grad", out_shape=jax.ShapeDtypeStruct((1, D), F32),
        in_specs=[pl.BlockSpec(memory_space=pltpu.VMEM)] * 2, out_specs=pl.BlockSpec(memory_space=pltpu.VMEM),
    )(p8, c_ctx)


def _adam_small(w, g, m, v):
    def body(w_ref, g_ref, m_ref, v_ref, dl_ref, m2_ref, v2_ref):
        dl_ref[...], m2_ref[...], v2_ref[...] = _adam(w_ref[...], g_ref[...], m_ref[...], v_ref[...])

    vm = pl.BlockSpec(memory_space=pltpu.VMEM)
    return pl.pallas_call(
        body, name="adam_small", out_shape=[jax.ShapeDtypeStruct(w.shape, F32)] * 3, in_specs=[vm] * 4, out_specs=[vm] * 3,
        compiler_params=pltpu.CompilerParams(vmem_limit_bytes=VMEM_LIMIT),
    )(w, g, m, v)


def _pack(vecs, width=SMALL_W, row_mult=8):
    flat = jnp.concatenate([v.reshape(-1).astype(F32) for v in vecs])
    n = flat.shape[0]
    rows = -(-n // (width * row_mult)) * row_mult
    return jnp.pad(flat, (0, rows * width - n)).reshape(rows, width)


def _unpack(packed, shapes):
    flat = packed.reshape(-1)
    out, off = [], 0
    for s in shapes:
        n = int(np.prod(s))
        out.append(flat[off:off + n].reshape(s))
        off += n
    return out


WEIGHTS = ('c_ctx', 'w_ada', 'b_ada', 'norm1_w', 'w_in', 'gla_up_f', 'gla_bias_f', 'gla_up_b', 'gla_bias_b', 'gla_norm_w',
           'conv_w', 'conv_b', 'dt_bias_f', 'dt_bias_b', 'a_log_f', 'a_log_b', 'd_skip', 'ssm_norm_w', 'w_pa', 'w_pb', 'w_out',
           'norm2_w', 'w_gate', 'w_up', 'w_down', 'final_norm_w')
BIG = ('w_in', 'w_pa', 'w_pb', 'w_out', 'w_gate', 'w_up', 'w_down')
COL_SHARDED = ('w_in', 'w_gate', 'w_up')
SMALL_SHARDED = ('gla_up_f', 'gla_up_b', 'conv_w')
ROW_TILE = 256


def _blocks_to_full(g4, name):
    n, r, c = g4.shape
    return g4.transpose(1, 0, 2).reshape(r, n * c) if name in COL_SHARDED else g4.reshape(n * r, c)


def _full_to_blocks(full, name):
    r, c = full.shape
    if name in COL_SHARDED:
        return full.reshape(r, N_CHIPS, c // N_CHIPS).transpose(1, 0, 2)
    return full.reshape(N_CHIPS, r // N_CHIPS, c)


def _permute_in(w_in_full):
    off = np.concatenate([[0], np.cumsum(IN_WIDTHS)])
    cols = [w_in_full[:, off[p]:off[p + 1]] for p in PERM]
    return jnp.concatenate(cols + [jnp.zeros((w_in_full.shape[0], SMALL_PAD), w_in_full.dtype)], axis=1)


def _unpermute_in(wp):
    off = np.concatenate([[0], np.cumsum([IN_WIDTHS[p] for p in PERM])])
    pieces = {p: wp[:, off[i]:off[i + 1]] for i, p in enumerate(PERM)}
    return jnp.concatenate([pieces[p] for p in range(len(IN_WIDTHS))], axis=1)


def _chip_cols(full, chip, n):
    return lax.dynamic_slice_in_dim(full, chip * n, n, axis=1)


def kernel(x, c, ctx, c_ctx, w_ada, b_ada, norm1_w, w_in, gla_up_f, gla_bias_f, gla_up_b, gla_bias_b, gla_norm_w, conv_w, conv_b, dt_bias_f, dt_bias_b, a_log_f, a_log_b, d_skip, ssm_norm_w, w_pa, w_pb, w_out, norm2_w, w_gate, w_up, w_down, final_norm_w, loss_target, m_c_ctx, m_w_ada, m_b_ada, m_norm1_w, m_w_in, m_gla_up_f, m_gla_bias_f, m_gla_up_b, m_gla_bias_b, m_gla_norm_w, m_conv_w, m_conv_b, m_dt_bias_f, m_dt_bias_b, m_a_log_f, m_a_log_b, m_d_skip, m_ssm_norm_w, m_w_pa, m_w_pb, m_w_out, m_norm2_w, m_w_gate, m_w_up, m_w_down, m_final_norm_w, v_c_ctx, v_w_ada, v_b_ada, v_norm1_w, v_w_in, v_gla_up_f, v_gla_bias_f, v_gla_up_b, v_gla_bias_b, v_gla_norm_w, v_conv_w, v_conv_b, v_dt_bias_f, v_dt_bias_b, v_a_log_f, v_a_log_b, v_d_skip, v_ssm_norm_w, v_w_pa, v_w_pb, v_w_out, v_norm2_w, v_w_gate, v_w_up, v_w_down, v_final_norm_w):
    given = dict(locals())
    W = {n: given[n] for n in WEIGHTS}
    M = {n: given["m_" + n] for n in WEIGHTS}
    V = {n: given["v_" + n] for n in WEIGHTS}
    L, Lc = x.shape[1], ctx.shape[1]
    tr = ROW_TILE
    assert L % tr == 0 and Lc % tr == 0 and L % Lc == 0 and Lc % SSM_C == 0
    n_lat_tiles = L // tr
    xi, yi, ci = _place()
    chip, me = 2 * xi + yi, 4 * xi + 2 * yi + ci
    xall = jnp.concatenate([x[0], ctx[0]], axis=0)

    g0 = _allgather_small(_pack([c[0]] + [W[n][0] for n in SMALL_SHARDED]), "gather_c")
    g0 = g0.reshape(N_DEV, -1)
    c_all = g0[:, :D]
    small_full, off = {}, D
    for n in SMALL_SHARDED:
        r, cols = W[n].shape[1:]
        small_full[n] = jnp.concatenate([g0[2 * k, off:off + r * cols].reshape(r, cols) for k in range(N_CHIPS)], axis=1)
        off += r * cols
    up_f, up_b, conv_w_full = (small_full[n] for n in SMALL_SHARDED)

    cvec = jnp.zeros((ADA_ROWS, D), F32).at[:N_DEV].set(c_all).at[N_DEV].set(c_ctx)
    ada_cols = w_ada.shape[2]
    ada_part = _ada_fwd(cvec, w_ada[0], _chip_cols(b_ada, chip, ada_cols))
    g1_all = _allgather_small(ada_part, "gather_ada")
    ada_full = jnp.concatenate([g1_all[2 * k] for k in range(N_CHIPS)], axis=1)
    mine = lax.dynamic_slice_in_dim(ada_full, me, 1, axis=0)
    sh1, sc1, g1, sh2, sc2, g2 = (mine[:, k * D:(k + 1) * D] for k in range(6))
    csh1, csc1 = ada_full[N_DEV:N_DEV + 1, :D], ada_full[N_DEV:N_DEV + 1, D:2 * D]
    mod = jnp.stack([jnp.stack([sh1, sc1]), jnp.stack([csh1, csc1])])

    gathered = _gather_weights([W[n][0].astype(BF16) for n in BIG], "gather_weights")
    full = {n: _blocks_to_full(g, n) for n, g in zip(BIG, gathered)}
    wp = _permute_in(full['w_in'])

    def lr_rows(up, base):
        return jnp.zeros((128, GLA_H * GLA_DK), F32).at[base:base + GLA_RANK].set(up)
    u2 = jnp.stack([lr_rows(up_f, SM_LRF), lr_rows(up_b, SM_LRB)])
    u4 = u2.reshape(2, 128, GLA_H, GLA_DK).transpose(0, 2, 1, 3)
    u4_hi = u4.astype(BF16)
    u4_lo = (u4 - u4_hi.astype(F32)).astype(BF16)
    ut = u2.transpose(0, 2, 1)
    ut_hi = ut.astype(BF16)
    ut_lo = (ut - ut_hi.astype(F32)).astype(BF16)
    gbias = jnp.stack([gla_bias_f[0], gla_bias_b[0]]).reshape(2, GLA_H, 1, GLA_DK)
    kc = _ssd_consts(jnp.stack([dt_bias_f[0], dt_bias_b[0]]), jnp.stack([a_log_f[0], a_log_b[0]]))
    gw4 = jnp.tile(gla_norm_w, (1, GLA_H))
    dskip_exp = jnp.repeat(d_skip, SSM_P, axis=1)
    n_gla = (L // GLA_C, Lc // GLA_C)
    n_ssd = (L // SSM_C, Lc // SSM_C)

    h1 = _norm_mod(xall, norm1_w, mod, n_lat_tiles, tr)
    parts = _mm(h1, wp, "nn", F32, "mm_in", tm=768, tn=1152)
    xbc = _conv_fwd(parts, conv_w_full, conv_b, L // Lc, Lc)
    o2, gla_hist = _gla_fwd(parts, u4_hi, u4_lo, gbias, *n_gla)
    y2, ssd_hist = _ssd_fwd(xbc, parts, kc, *n_ssd)
    oan = _gla_out(o2, parts, gw4, tr)
    obn = _ssd_out(y2, xbc, parts, dskip_exp, ssm_norm_w, tr)
    ya = _mm(oan, full['w_pa'], "nn", F32, "mm_pa", tm=768)
    yb = _mm(obn, full['w_pb'], "nn", F32, "mm_pb", tm=768)
    merged = _merge(ya, yb, parts, tr)
    mix = _mm(merged, full['w_out'], "nn", F32, "mm_out", tm=768)
    h2, u = _resid_norm_mod(xall, mix, g1, norm2_w, sh2, sc2, tr)
    gp = _mm(u, full['w_gate'], "nn", F32, "mm_gate", tm=768, tn=1408)
    up = _mm(u, full['w_up'], "nn", F32, "mm_up", tm=768, tn=1408)
    act = _swiglu_act(gp, up, tr)
    f = _mm(act, full['w_down'], "nn", F32, "mm_down", tm=768)
    dh3, df, acc_loss = _loss_head(h2, f, loss_target[0], g2, final_norm_w[None], n_lat_tiles, tr)

    dw = {}
    da = _mm(df, full['w_down'], "nt", F32, "mm_down_dx", tm=768, tn=1408)
    dw['w_down'] = _mm(act, df, "tn", BF16, "mm_down_dw", tm=1408, tk=768)
    dgp, dup = _swiglu_act_bwd(da, gp, up, tr)
    du_a = _mm(dgp, full['w_gate'], "nt", F32, "mm_gate_dx", tm=768, tk=1408)
    du_b = _mm(dup, full['w_up'], "nt", F32, "mm_up_dx", tm=768, tk=1408)
    dw['w_gate'] = _mm(u, dgp, "tn", BF16, "mm_gate_dw", tm=1024, tn=1408, tk=768)
    dw['w_up'] = _mm(u, dup, "tn", BF16, "mm_up_dw", tm=1024, tn=1408, tk=768)
    dh2, dmix, acc_ffn = _ffn_in_bwd(du_a, du_b, h2, dh3, mix, sc2, g1, norm2_w, tr)
    dmerged = _mm(dmix, full['w_out'], "nt", F32, "mm_out_dx", tm=768)
    dw['w_out'] = _mm(merged, dmix, "tn", BF16, "mm_out_dw", tm=1024, tk=768)
    dya, dyb, dgates = _merge_bwd(dmerged, ya, yb, parts, tr)
    doan = _mm(dya, full['w_pa'], "nt", F32, "mm_pa_dx", tm=768)
    dw['w_pa'] = _mm(oan, dya, "tn", BF16, "mm_pa_dw", tm=1024, tk=768)
    dobn = _mm(dyb, full['w_pb'], "nt", F32, "mm_pb_dx", tm=768)
    dw['w_pb'] = _mm(obn, dyb, "tn", BF16, "mm_pb_dw", tm=1024, tk=768)
    do, dr, acc_gla = _gla_out_bwd(doan, o2, parts, gw4, tr)
    dq, dk, dv, dpre = _gla_bwd(do, parts, u4_hi, u4_lo, gbias, gla_hist, *n_gla)
    dy, dz, dxs_skip, acc_ssd = _ssd_out_bwd(dobn, y2, xbc, parts, dskip_exp, ssm_norm_w, tr)
    dx_scan, db_scan, dc_scan, dsm, acc_alog = _ssd_bwd(dy, xbc, parts, kc, ssd_hist, *n_ssd)
    dxbc = _xbc_assemble(dx_scan, db_scan, dc_scan, dxs_skip, tr)
    du_conv, acc_conv = _conv_bwd(dxbc, parts, conv_w_full, conv_b, L // Lc, Lc)
    dqkv = _gla_assemble(dq, dk, dv, tr)
    dsmall, dup_gla, acc_gbias, acc_dtb = _small_assemble(dpre, dsm, parts, ut_hi, ut_lo, tr)
    dparts = jnp.concatenate([dz, du_conv, dqkv, dr, dgates, dsmall], axis=1)
    dh1 = _mm(dparts, wp, "nt", F32, "mm_in_dx", tm=768, tk=1152)
    dw['w_in'] = _unpermute_in(_mm(h1, dparts, "tn", BF16, "mm_in_dw", tm=1024, tn=1152, tk=768))
    dxall, acc_n1 = _norm1_bwd(dh1, xall, dh2, norm1_w, mod, n_lat_tiles, tr)

    partial = dict(
        norm1_w=acc_n1[0, 2] + acc_n1[1, 2],
        gla_up_f=dup_gla[0, SM_LRF:SM_LRF + GLA_RANK], gla_bias_f=acc_gbias[0],
        gla_up_b=dup_gla[1, SM_LRB:SM_LRB + GLA_RANK], gla_bias_b=acc_gbias[1],
        gla_norm_w=acc_gla[0].reshape(GLA_H, GLA_DV).sum(0),
        conv_w=acc_conv[:SSM_CONV], conv_b=acc_conv[SSM_CONV],
        dt_bias_f=acc_dtb[0, SM_DTF:SM_DTF + SSM_HEADS], dt_bias_b=acc_dtb[0, SM_DTB:SM_DTB + SSM_HEADS],
        a_log_f=acc_alog[0, :, 0, :SSM_HPG], a_log_b=acc_alog[1, :, 0, :SSM_HPG],
        d_skip=acc_ssd[1].reshape(SSM_HEADS, SSM_P).sum(1), ssm_norm_w=acc_ssd[0],
        norm2_w=acc_ffn[2], final_norm_w=acc_loss[0],
    )
    dada = jnp.concatenate([acc_n1[0, 1], acc_n1[0, 0], acc_ffn[3], acc_ffn[1], acc_ffn[0], acc_loss[1]])
    dada_ctx = jnp.concatenate([acc_n1[1, 1], acc_n1[1, 0], jnp.zeros((4 * D,), F32)])
    names = list(partial)
    payload = [partial[n] for n in names] + [dada + dada_ctx, dada_ctx, acc_loss[2], dada]
    sizes = [int(np.prod(p.shape)) for p in payload]
    g8 = _allgather_small(_pack(payload), "gather_small_grads")
    summed = _unpack(_sum8(g8), [(s,) for s in sizes])
    grads = {n: s.reshape(W[n].shape if n not in SMALL_SHARDED else partial[n].shape) for n, s in zip(names, summed)}
    grads['b_ada'] = summed[len(names)].reshape(b_ada.shape)
    dada_ctx_sum = summed[len(names) + 1]
    loss = jnp.sum(summed[len(names) + 2])
    dada_all = g8.reshape(N_DEV, -1)[:, sum(sizes[:-1]):sum(sizes)]

    dada16 = jnp.zeros((ADA_ROWS, ada_cols), F32)
    dada16 = dada16.at[:N_DEV].set(_chip_cols(dada_all, chip, ada_cols)).at[N_DEV].set(_chip_cols(dada_ctx_sum[None], chip, ada_cols)[0])
    g_wada, dl_wada, m_wada, v_wada, p16 = _wada_bwd_adam(cvec, dada16, w_ada[0], m_w_ada[0], v_w_ada[0])
    p8 = _allgather_small(p16[N_DEV:], "gather_cctx")
    grads['c_ctx'] = _cctx_grad(p8[:, 0:1, :], c_ctx[None])[0]
    for n in SMALL_SHARDED:
        grads[n] = _chip_cols(grads[n], chip, W[n].shape[2])[None]

    small = [n for n in WEIGHTS if n not in BIG and n != 'w_ada']
    shapes = [W[n].shape for n in small]
    dl_s, m_s, v_s = _adam_small(*[_pack([d[n] for n in small]) for d in (W, grads, M, V)])
    delta = dict(zip(small, _unpack(dl_s, shapes)))
    new_m = dict(zip(small, _unpack(m_s, shapes)))
    new_v = dict(zip(small, _unpack(v_s, shapes)))
    grads['w_ada'], delta['w_ada'], new_m['w_ada'], new_v['w_ada'] = g_wada[None], dl_wada[None], m_wada[None], v_wada[None]

    exchanged = _exchange_grads([_full_to_blocks(dw[n], n) for n in BIG], "exchange_grads")
    for n, p8w in zip(BIG, exchanged):
        g, dl, m2, v2 = _reduce_adam(p8w, W[n][0], M[n][0], V[n][0], "adam_" + n)
        grads[n], delta[n], new_m[n], new_v[n] = g[None], dl[None], m2[None], v2[None]

    return (loss, dxall[:L][None], *[grads[n] for n in WEIGHTS], *[delta[n] for n in WEIGHTS],
            *[new_m[n] for n in WEIGHTS], *[new_v[n] for n in WEIGHTS])
```

```python
import functools

import numpy as np
import jax
import jax.numpy as jnp
from jax import lax
from jax.experimental import pallas as pl
from jax.experimental.pallas import tpu as pltpu

F32 = jnp.float32
BF16 = jnp.bfloat16
MESH = pl.DeviceIdType.MESH

D = 1024
EPS = 1e-6
GRID_W = 64
GLA_H, GLA_DK, GLA_DV, GLA_RANK, GLA_TAU = 4, 128, 256, 16, 16.0
GLA_C = 64
SSM_INNER, SSM_P, SSM_HEADS, SSM_G, SSM_HPG, SSM_N = 2048, 64, 32, 4, 8, 128
SSM_C = 128
SSM_CONV, CONV_LEFT = 4, 2
D_FF = 2816
IN_WIDTHS = (512, 512, 1024, 1024, 16, 16, 2048, 2048, 512, 512, 32, 32, 1024, 1024)
D_IN = sum(IN_WIDTHS)
PERM = (6, 7, 8, 9, 0, 1, 2, 3, 12, 13, 4, 5, 10, 11)
PW = 10368
SMALL_PAD = PW - D_IN
COL_Z, COL_XBC, COL_Q, COL_K, COL_V, COL_R, COL_GA, COL_GB, COL_SM = 0, 2048, 5120, 5632, 6144, 7168, 8192, 9216, 10240
SM_LRF, SM_LRB, SM_DTF, SM_DTB = 0, 16, 32, 64
EXP_CLAMP = 80.0
ADAM_LR, ADAM_B1, ADAM_B2, ADAM_EPS, ADAM_WD, ADAM_STEP = 0.001, 0.9, 0.999, 1e-08, 0.01, 10
N_CHIPS, N_DEV = 4, 8
VMEM_LIMIT = 56 * 1024 * 1024


def _cparams(sem=None):
    return pltpu.CompilerParams(dimension_semantics=sem, vmem_limit_bytes=VMEM_LIMIT)


def _dg(a, b, ca, cb):
    return lax.dot_general(a, b, (((ca,), (cb,)), ((), ())), preferred_element_type=F32)


def _nn(a, b):
    return _dg(a, b, 1, 0)


def _nt(a, b):
    return _dg(a, b, 1, 1)


def _tn(a, b):
    return _dg(a, b, 0, 0)


def _bf(x):
    return x.astype(BF16)


def _split(x):
    hi = x.astype(BF16)
    return hi, (x - hi.astype(F32)).astype(BF16)


def _nn_x(a, b_exact):
    hi, lo = _split(a)
    return _nn(hi, b_exact) + _nn(lo, b_exact)


def _x_nn(a_exact, b):
    hi, lo = _split(b)
    return _nn(a_exact, hi) + _nn(a_exact, lo)


def _nn3(a, b_hi, b_lo):
    hi, lo = _split(a)
    return _nn(hi, b_hi) + _nn(lo, b_hi) + _nn(hi, b_lo)


def _sigmoid(x):
    return 1.0 / (1.0 + jnp.exp(-x))


def _silu(x):
    return x * _sigmoid(x)


def _dsilu(x):
    s = _sigmoid(x)
    return s * (1.0 + x * (1.0 - s))


def _softplus(x):
    return jnp.maximum(x, 0.0) + jnp.log(1.0 + jnp.exp(-jnp.abs(x)))


def _log_sigmoid(x):
    return jnp.minimum(x, 0.0) - jnp.log(1.0 + jnp.exp(-jnp.abs(x)))


def _tile(n, target, mult=8):
    best = None
    for t in range(mult, min(n, target) + 1, mult):
        if n % t == 0:
            best = t
    assert best is not None, (n, target, mult)
    return best


def _mm(a, b, mode, out_dtype, name, tm=512, tn=1024, tk=2048):
    if mode == "nn":
        (M, K), N = a.shape, b.shape[1]
    elif mode == "nt":
        (M, K), N = a.shape, b.shape[0]
    else:
        (K, M), N = a.shape, b.shape[1]
    tm, tn, tk = _tile(M, tm, 128), _tile(N, tn, 128), _tile(K, tk, 128)
    nk = K // tk
    ca, cb = {"nn": (1, 0), "nt": (1, 1), "tn": (0, 0)}[mode]

    def body(a_ref, b_ref, o_ref, *acc):
        part = _dg(a_ref[...], b_ref[...], ca, cb)
        if nk == 1:
            o_ref[...] = part.astype(out_dtype)
        else:
            k = pl.program_id(2)

            @pl.when(k == 0)
            def _():
                acc[0][...] = part

            @pl.when(k > 0)
            def _():
                acc[0][...] += part

            @pl.when(k == nk - 1)
            def _():
                o_ref[...] = acc[0][...].astype(out_dtype)

    a_spec = pl.BlockSpec((tk, tm), lambda i, j, k: (k, i)) if mode == "tn" else pl.BlockSpec((tm, tk), lambda i, j, k: (i, k))
    b_spec = pl.BlockSpec((tn, tk), lambda i, j, k: (j, k)) if mode == "nt" else pl.BlockSpec((tk, tn), lambda i, j, k: (k, j))
    return pl.pallas_call(
        body, name=name, grid=(M // tm, N // tn, nk), in_specs=[a_spec, b_spec],
        out_specs=pl.BlockSpec((tm, tn), lambda i, j, k: (i, j)),
        out_shape=jax.ShapeDtypeStruct((M, N), out_dtype),
        scratch_shapes=[pltpu.VMEM((tm, tn), F32)] if nk > 1 else [],
        compiler_params=_cparams(("arbitrary", "arbitrary", "arbitrary")),
    )(a, b)


def _place():
    return lax.axis_index("x"), lax.axis_index("y"), lax.axis_index("c")


def _flip(v, bit):
    return 1 - v if bit else v


def _allgather_small(v, name):
    R, C = v.shape

    def body(v_ref, out_ref, send_sems, recv_sems, local_sem):
        x, y, c = _place()
        me = 4 * x + 2 * y + c
        mine = pltpu.make_async_copy(v_ref, out_ref.at[me], local_sem)
        mine.start()

        def peer(r):
            return _flip(x, (r >> 2) & 1), _flip(y, (r >> 1) & 1), _flip(c, r & 1)

        sends = [pltpu.make_async_remote_copy(
            src_ref=v_ref, dst_ref=out_ref.at[me], send_sem=send_sems.at[r - 1], recv_sem=recv_sems.at[r - 1],
            device_id=peer(r), device_id_type=MESH) for r in range(1, N_DEV)]
        for cp in sends:
            cp.start()
        for r in range(1, N_DEV):
            px, py, pc = peer(r)
            pltpu.make_async_remote_copy(
                src_ref=v_ref, dst_ref=out_ref.at[4 * px + 2 * py + pc], send_sem=send_sems.at[r - 1],
                recv_sem=recv_sems.at[r - 1], device_id=(x, y, c), device_id_type=MESH).wait_recv()
        for cp in sends:
            cp.wait_send()
        mine.wait()

    return pl.pallas_call(
        body, name=name, out_shape=jax.ShapeDtypeStruct((N_DEV, R, C), v.dtype),
        in_specs=[pl.BlockSpec(memory_space=pltpu.VMEM)], out_specs=pl.BlockSpec(memory_space=pltpu.VMEM),
        scratch_shapes=[pltpu.SemaphoreType.DMA((N_DEV - 1,)), pltpu.SemaphoreType.DMA((N_DEV - 1,)), pltpu.SemaphoreType.DMA],
        compiler_params=pltpu.CompilerParams(vmem_limit_bytes=VMEM_LIMIT),
    )(v)


_CHIP_RELATIONS = ((1, 0), (0, 1), (1, 1))


def _gather_weights(shards, name):
    n = len(shards)

    def body(*refs):
        ins, outs = refs[:n], refs[n:2 * n]
        send_sems, recv_sems, local_sems = refs[2 * n:]
        x, y, c = _place()
        chip = 2 * x + y
        local = [pltpu.make_async_copy(ins[i], outs[i].at[chip], local_sems.at[i]) for i in range(n)]
        for cp in local:
            cp.start()
        sends = []
        for i in range(n):
            for j, (fx, fy) in enumerate(_CHIP_RELATIONS):
                sends.append(pltpu.make_async_remote_copy(
                    src_ref=ins[i], dst_ref=outs[i].at[chip], send_sem=send_sems.at[i, j], recv_sem=recv_sems.at[i, j],
                    device_id=(_flip(x, fx), _flip(y, fy), c), device_id_type=MESH))
        for cp in sends:
            cp.start()
        for i in range(n):
            for j, (fx, fy) in enumerate(_CHIP_RELATIONS):
                src = 2 * _flip(x, fx) + _flip(y, fy)
                pltpu.make_async_remote_copy(
                    src_ref=ins[i], dst_ref=outs[i].at[src], send_sem=send_sems.at[i, j], recv_sem=recv_sems.at[i, j],
                    device_id=(x, y, c), device_id_type=MESH).wait_recv()
        for cp in sends:
            cp.wait_send()
        for cp in local:
            cp.wait()

    any_spec = pl.BlockSpec(memory_space=pl.ANY)
    return pl.pallas_call(
        body, name=name, out_shape=[jax.ShapeDtypeStruct((N_CHIPS,) + s.shape, s.dtype) for s in shards],
        in_specs=[any_spec] * n, out_specs=[any_spec] * n,
        scratch_shapes=[pltpu.SemaphoreType.DMA((n, 3)), pltpu.SemaphoreType.DMA((n, 3)), pltpu.SemaphoreType.DMA((n,))],
    )(*shards)


def _exchange_grads(blocks, name):
    n = len(blocks)

    def body(*refs):
        ins, outs = refs[:n], refs[n:2 * n]
        send_sems, recv_sems, local_sems = refs[2 * n:]
        x, y, c = _place()
        chip = 2 * x + y
        sibling = (x, y, 1 - c)
        local = [pltpu.make_async_copy(ins[i].at[chip], outs[i].at[chip], local_sems.at[i]) for i in range(n)]
        for cp in local:
            cp.start()
        first = []
        for i in range(n):
            for j, (fx, fy) in enumerate(_CHIP_RELATIONS):
                px, py = _flip(x, fx), _flip(y, fy)
                first.append(pltpu.make_async_remote_copy(
                    src_ref=ins[i].at[2 * px + py], dst_ref=outs[i].at[chip], send_sem=send_sems.at[i, j],
                    recv_sem=recv_sems.at[i, j], device_id=(px, py, c), device_id_type=MESH))
            first.append(pltpu.make_async_remote_copy(
                src_ref=ins[i].at[chip], dst_ref=outs[i].at[N_CHIPS + chip], send_sem=send_sems.at[i, 3],
                recv_sem=recv_sems.at[i, 3], device_id=sibling, device_id_type=MESH))
        for cp in first:
            cp.start()
        passed = []
        for i in range(n):
            for j, (fx, fy) in enumerate(_CHIP_RELATIONS):
                src = 2 * _flip(x, fx) + _flip(y, fy)
                pltpu.make_async_remote_copy(
                    src_ref=ins[i].at[src], dst_ref=outs[i].at[src], send_sem=send_sems.at[i, j], recv_sem=recv_sems.at[i, j],
                    device_id=(x, y, c), device_id_type=MESH).wait_recv()
                fwd = pltpu.make_async_remote_copy(
                    src_ref=outs[i].at[src], dst_ref=outs[i].at[N_CHIPS + src], send_sem=send_sems.at[i, 4 + j],
                    recv_sem=recv_sems.at[i, 4 + j], device_id=sibling, device_id_type=MESH)
                fwd.start()
                passed.append(fwd)
        for i in range(n):
            for j in (3, 4, 5, 6):
                pltpu.make_async_remote_copy(
                    src_ref=ins[i].at[0], dst_ref=outs[i].at[0], send_sem=send_sems.at[i, j], recv_sem=recv_sems.at[i, j],
                    device_id=(x, y, c), device_id_type=MESH).wait_recv()
        for cp in first + passed:
            cp.wait_send()
        for cp in local:
            cp.wait()

    any_spec = pl.BlockSpec(memory_space=pl.ANY)
    return pl.pallas_call(
        body, name=name, out_shape=[jax.ShapeDtypeStruct((N_DEV,) + b.shape[1:], b.dtype) for b in blocks],
        in_specs=[any_spec] * n, out_specs=[any_spec] * n,
        scratch_shapes=[pltpu.SemaphoreType.DMA((n, 7)), pltpu.SemaphoreType.DMA((n, 7)), pltpu.SemaphoreType.DMA((n,))],
    )(*blocks)


def _row_spec(tr, w, col=0):
    return pl.BlockSpec((tr, w), lambda i: (i, col))


def _vec_spec(w):
    return pl.BlockSpec((1, w), lambda i: (0, 0))


def _acc_spec(w):
    return pl.BlockSpec((8, w), lambda i: (0, 0))


def _rms(x):
    return lax.rsqrt(jnp.mean(x * x, axis=-1, keepdims=True) + EPS)


def _rms_bwd(dn, n, rstd):
    return rstd * (dn - n * jnp.mean(dn * n, axis=-1, keepdims=True))


def _colsum(x):
    return jnp.sum(x, axis=0, keepdims=True)


def _zero_first(ref):
    @pl.when(pl.program_id(0) == 0)
    def _():
        ref[...] = jnp.zeros_like(ref)


def _norm_mod(xall, w, mod, n_lat_tiles, tr):
    R = xall.shape[0]

    def body(x_ref, w_ref, mod_ref, o_ref):
        x = x_ref[...]
        n = x * _rms(x) * w_ref[...]
        o_ref[...] = (n * (1.0 + mod_ref[0, 1]) + mod_ref[0, 0]).astype(BF16)

    return pl.pallas_call(
        body, name="norm1_mod", grid=(R // tr,),
        in_specs=[_row_spec(tr, D), _vec_spec(D),
                  pl.BlockSpec((1, 2, 1, D), lambda i: (jnp.where(i >= n_lat_tiles, 1, 0), 0, 0, 0))],
        out_specs=_row_spec(tr, D), out_shape=jax.ShapeDtypeStruct((R, D), BF16),
        compiler_params=_cparams(("arbitrary",)),
    )(xall, w, mod)


def _resid_norm_mod(xall, mix, g1, w2, sh2, sc2, tr):
    R = xall.shape[0]

    def body(x_ref, mix_ref, g1_ref, w_ref, sh_ref, sc_ref, h2_ref, u_ref):
        h2 = x_ref[...] + g1_ref[...] * mix_ref[...]
        h2_ref[...] = h2
        n = h2 * _rms(h2) * w_ref[...]
        u_ref[...] = (n * (1.0 + sc_ref[...]) + sh_ref[...]).astype(BF16)

    return pl.pallas_call(
        body, name="resid_norm2_mod", grid=(R // tr,),
        in_specs=[_row_spec(tr, D), _row_spec(tr, D)] + [_vec_spec(D)] * 4,
        out_specs=[_row_spec(tr, D), _row_spec(tr, D)],
        out_shape=[jax.ShapeDtypeStruct((R, D), F32), jax.ShapeDtypeStruct((R, D), BF16)],
        compiler_params=_cparams(("arbitrary",)),
    )(xall, mix, g1, w2, sh2, sc2)


def _loss_head(h2, f, target, g2, fw, n_lat_tiles, tr):
    R = h2.shape[0]

    def body(h2_ref, f_ref, t_ref, g2_ref, fw_ref, dh3_ref, df_ref, acc_ref):
        _zero_first(acc_ref)
        lat = pl.program_id(0) < n_lat_tiles
        fv = f_ref[...]
        h3 = h2_ref[...] + g2_ref[...] * fv
        rstd = _rms(h3)
        n = h3 * rstd
        err = n * fw_ref[...] - t_ref[...]
        dy = err * (1.0 / D)
        dh3 = jnp.where(lat, _rms_bwd(dy * fw_ref[...], n, rstd), 0.0)
        dh3_ref[...] = dh3
        df_ref[...] = (g2_ref[...] * dh3).astype(BF16)
        acc_ref[0:1, :] += jnp.where(lat, _colsum(dy * n), 0.0)
        acc_ref[1:2, :] += _colsum(dh3 * fv)
        acc_ref[2:3, :] += jnp.where(lat, _colsum(err * err) * (0.5 / D), 0.0)

    return pl.pallas_call(
        body, name="loss_head", grid=(R // tr,),
        in_specs=[_row_spec(tr, D), _row_spec(tr, D),
                  pl.BlockSpec((tr, D), lambda i: (jnp.minimum(i, n_lat_tiles - 1), 0)), _vec_spec(D), _vec_spec(D)],
        out_specs=[_row_spec(tr, D), _row_spec(tr, D), _acc_spec(D)],
        out_shape=[jax.ShapeDtypeStruct((R, D), F32), jax.ShapeDtypeStruct((R, D), BF16), jax.ShapeDtypeStruct((8, D), F32)],
        compiler_params=_cparams(("arbitrary",)),
    )(h2, f, target, g2, fw)


def _ffn_in_bwd(du_a, du_b, h2, dh3, mix, sc2, g1, w2, tr):
    R = h2.shape[0]

    def body(dua_ref, dub_ref, h2_ref, dh3_ref, mix_ref, sc_ref, g1_ref, w_ref, dh2_ref, dmix_ref, acc_ref):
        _zero_first(acc_ref)
        du = dua_ref[...] + dub_ref[...]
        h2 = h2_ref[...]
        rstd = _rms(h2)
        n = h2 * rstd
        dnw = du * (1.0 + sc_ref[...])
        dh2 = dh3_ref[...] + _rms_bwd(dnw * w_ref[...], n, rstd)
        dh2_ref[...] = dh2
        dmix_ref[...] = (g1_ref[...] * dh2).astype(BF16)
        acc_ref[0:1, :] += _colsum(du * n * w_ref[...])
        acc_ref[1:2, :] += _colsum(du)
        acc_ref[2:3, :] += _colsum(dnw * n)
        acc_ref[3:4, :] += _colsum(dh2 * mix_ref[...])

    return pl.pallas_call(
        body, name="ffn_in_bwd", grid=(R // tr,),
        in_specs=[_row_spec(tr, D)] * 5 + [_vec_spec(D)] * 3,
        out_specs=[_row_spec(tr, D), _row_spec(tr, D), _acc_spec(D)],
        out_shape=[jax.ShapeDtypeStruct((R, D), F32), jax.ShapeDtypeStruct((R, D), BF16), jax.ShapeDtypeStruct((8, D), F32)],
        compiler_params=_cparams(("arbitrary",)),
    )(du_a, du_b, h2, dh3, mix, sc2, g1, w2)


def _norm1_bwd(dh1, xall, dh2, w1, mod, n_lat_tiles, tr):
    R = xall.shape[0]

    def body(dh1_ref, x_ref, dh2_ref, w_ref, mod_ref, dx_ref, acc_ref):
        i = pl.program_id(0)

        @pl.when((i == 0) | (i == n_lat_tiles))
        def _():
            acc_ref[...] = jnp.zeros_like(acc_ref)

        dh1 = dh1_ref[...]
        x = x_ref[...]
        rstd = _rms(x)
        n = x * rstd
        dnw = dh1 * (1.0 + mod_ref[0, 1])
        dx_ref[...] = dh2_ref[...] + _rms_bwd(dnw * w_ref[...], n, rstd)
        acc_ref[0, 0:1, :] += _colsum(dh1 * n * w_ref[...])
        acc_ref[0, 1:2, :] += _colsum(dh1)
        acc_ref[0, 2:3, :] += _colsum(dnw * n)

    sel = lambda i: jnp.where(i >= n_lat_tiles, 1, 0)
    return pl.pallas_call(
        body, name="norm1_bwd", grid=(R // tr,),
        in_specs=[_row_spec(tr, D)] * 3 + [_vec_spec(D), pl.BlockSpec((1, 2, 1, D), lambda i: (sel(i), 0, 0, 0))],
        out_specs=[_row_spec(tr, D), pl.BlockSpec((1, 8, D), lambda i: (sel(i), 0, 0))],
        out_shape=[jax.ShapeDtypeStruct((R, D), F32), jax.ShapeDtypeStruct((2, 8, D), F32)],
        compiler_params=_cparams(("arbitrary",)),
    )(dh1, xall, dh2, w1, mod)


def _swiglu_act(gp, up, tr):
    R = gp.shape[0]

    def body(g_ref, u_ref, o_ref):
        o_ref[...] = (_silu(g_ref[...]) * u_ref[...]).astype(BF16)

    return pl.pallas_call(
        body, name="swiglu_act", grid=(R // tr,), in_specs=[_row_spec(tr, D_FF)] * 2, out_specs=_row_spec(tr, D_FF),
        out_shape=jax.ShapeDtypeStruct((R, D_FF), BF16), compiler_params=_cparams(("arbitrary",)),
    )(gp, up)


def _swiglu_act_bwd(da, gp, up, tr):
    R = gp.shape[0]

    def body(da_ref, g_ref, u_ref, dg_ref, du_ref):
        da, g = da_ref[...], g_ref[...]
        dg_ref[...] = (da * u_ref[...] * _dsilu(g)).astype(BF16)
        du_ref[...] = (da * _silu(g)).astype(BF16)

    return pl.pallas_call(
        body, name="swiglu_act_bwd", grid=(R // tr,), in_specs=[_row_spec(tr, D_FF)] * 3, out_specs=[_row_spec(tr, D_FF)] * 2,
        out_shape=[jax.ShapeDtypeStruct((R, D_FF), BF16)] * 2, compiler_params=_cparams(("arbitrary",)),
    )(da, gp, up)


def _merge(ya, yb, parts, tr):
    R = ya.shape[0]

    def body(ya_ref, yb_ref, ga_ref, gb_ref, o_ref):
        o_ref[...] = (_sigmoid(ga_ref[...]) * ya_ref[...] + _sigmoid(gb_ref[...]) * yb_ref[...]).astype(BF16)

    return pl.pallas_call(
        body, name="merge", grid=(R // tr,),
        in_specs=[_row_spec(tr, D), _row_spec(tr, D), _row_spec(tr, D, COL_GA // D), _row_spec(tr, D, COL_GB // D)],
        out_specs=_row_spec(tr, D), out_shape=jax.ShapeDtypeStruct((R, D), BF16), compiler_params=_cparams(("arbitrary",)),
    )(ya, yb, parts, parts)


def _merge_bwd(dm, ya, yb, parts, tr):
    R = ya.shape[0]

    def body(dm_ref, ya_ref, yb_ref, ga_ref, gb_ref, dya_ref, dyb_ref, dg_ref):
        dm = dm_ref[...]
        sa, sb = _sigmoid(ga_ref[...]), _sigmoid(gb_ref[...])
        dya_ref[...] = (dm * sa).astype(BF16)
        dyb_ref[...] = (dm * sb).astype(BF16)
        dg_ref[:, 0:D] = (dm * ya_ref[...] * sa * (1.0 - sa)).astype(BF16)
        dg_ref[:, D:2 * D] = (dm * yb_ref[...] * sb * (1.0 - sb)).astype(BF16)

    return pl.pallas_call(
        body, name="merge_bwd", grid=(R // tr,),
        in_specs=[_row_spec(tr, D)] * 3 + [_row_spec(tr, D, COL_GA // D), _row_spec(tr, D, COL_GB // D)],
        out_specs=[_row_spec(tr, D), _row_spec(tr, D), _row_spec(tr, 2 * D)],
        out_shape=[jax.ShapeDtypeStruct((R, D), BF16)] * 2 + [jax.ShapeDtypeStruct((R, 2 * D), BF16)],
        compiler_params=_cparams(("arbitrary",)),
    )(dm, ya, yb, parts, parts)


def _gla_out(o2, parts, gw4, tr):
    R = parts.shape[0]

    def body(of_ref, ob_ref, r_ref, w_ref, out_ref):
        oa = of_ref[...] + ob_ref[...]
        sr = _silu(r_ref[...])
        for h in range(GLA_H):
            s = slice(h * GLA_DV, (h + 1) * GLA_DV)
            o = oa[:, s]
            out_ref[:, s] = (o * _rms(o) * w_ref[:, s] * sr[:, s]).astype(BF16)

    return pl.pallas_call(
        body, name="gla_out", grid=(R // tr,),
        in_specs=[_row_spec(tr, D), _row_spec(tr, D), _row_spec(tr, D, COL_R // D), _vec_spec(D)],
        out_specs=_row_spec(tr, D), out_shape=jax.ShapeDtypeStruct((R, D), BF16), compiler_params=_cparams(("arbitrary",)),
    )(o2[0], o2[1], parts, gw4)


def _gla_out_bwd(dout, o2, parts, gw4, tr):
    R = parts.shape[0]

    def body(d_ref, of_ref, ob_ref, r_ref, w_ref, do_ref, dr_ref, acc_ref):
        _zero_first(acc_ref)
        oa = of_ref[...] + ob_ref[...]
        r = r_ref[...]
        sr = _silu(r)
        dout = d_ref[...]
        for h in range(GLA_H):
            s = slice(h * GLA_DV, (h + 1) * GLA_DV)
            o = oa[:, s]
            rstd = _rms(o)
            n = o * rstd
            w = w_ref[:, s]
            dr_ref[:, s] = (dout[:, s] * n * w * _dsilu(r[:, s])).astype(BF16)
            dnw = dout[:, s] * sr[:, s]
            do_ref[:, s] = _rms_bwd(dnw * w, n, rstd)
            acc_ref[0:1, s] += _colsum(dnw * n)

    return pl.pallas_call(
        body, name="gla_out_bwd", grid=(R // tr,),
        in_specs=[_row_spec(tr, D), _row_spec(tr, D), _row_spec(tr, D), _row_spec(tr, D, COL_R // D), _vec_spec(D)],
        out_specs=[_row_spec(tr, D), _row_spec(tr, D), _acc_spec(D)],
        out_shape=[jax.ShapeDtypeStruct((R, D), F32), jax.ShapeDtypeStruct((R, D), BF16), jax.ShapeDtypeStruct((8, D), F32)],
        compiler_params=_cparams(("arbitrary",)),
    )(dout, o2[0], o2[1], parts, gw4)


SSM_GW = SSM_INNER // SSM_G


def _ssd_out(y2, xbc, parts, dskip, nw, tr):
    R = parts.shape[0]

    def body(yf_ref, yb_ref, x_ref, z_ref, ds_ref, w_ref, out_ref):
        ob = (yf_ref[...] + yb_ref[...] + ds_ref[...] * x_ref[...]) * _silu(z_ref[...])
        for g in range(SSM_G):
            s = slice(g * SSM_GW, (g + 1) * SSM_GW)
            o = ob[:, s]
            out_ref[:, s] = (o * _rms(o) * w_ref[:, s]).astype(BF16)

    return pl.pallas_call(
        body, name="ssd_out", grid=(R // tr,),
        in_specs=[_row_spec(tr, SSM_INNER)] * 3 + [_row_spec(tr, SSM_INNER, COL_Z // SSM_INNER),
                                                   _vec_spec(SSM_INNER), _vec_spec(SSM_INNER)],
        out_specs=_row_spec(tr, SSM_INNER), out_shape=jax.ShapeDtypeStruct((R, SSM_INNER), BF16),
        compiler_params=_cparams(("arbitrary",)),
    )(y2[0], y2[1], xbc, parts, dskip, nw)


def _ssd_out_bwd(dout, y2, xbc, parts, dskip, nw, tr):
    R = parts.shape[0]

    def body(d_ref, yf_ref, yb_ref, x_ref, z_ref, ds_ref, w_ref, dy_ref, dz_ref, dxs_ref, acc_ref):
        _zero_first(acc_ref)
        x, z = x_ref[...], z_ref[...]
        pre = yf_ref[...] + yb_ref[...] + ds_ref[...] * x
        sz = _silu(z)
        ob = pre * sz
        dout = d_ref[...]
        for g in range(SSM_G):
            s = slice(g * SSM_GW, (g + 1) * SSM_GW)
            o = ob[:, s]
            rstd = _rms(o)
            n = o * rstd
            dob = _rms_bwd(dout[:, s] * w_ref[:, s], n, rstd)
            dz_ref[:, s] = (dob * pre[:, s] * _dsilu(z[:, s])).astype(BF16)
            dy = dob * sz[:, s]
            dy_ref[:, s] = dy
            dxs_ref[:, s] = dy * ds_ref[:, s]
            acc_ref[0:1, s] += _colsum(dout[:, s] * n)
            acc_ref[1:2, s] += _colsum(dy * x[:, s])

    return pl.pallas_call(
        body, name="ssd_out_bwd", grid=(R // tr,),
        in_specs=[_row_spec(tr, SSM_INNER)] * 4 + [_row_spec(tr, SSM_INNER, COL_Z // SSM_INNER),
                                                   _vec_spec(SSM_INNER), _vec_spec(SSM_INNER)],
        out_specs=[_row_spec(tr, SSM_INNER), _row_spec(tr, SSM_INNER), _row_spec(tr, SSM_INNER), _acc_spec(SSM_INNER)],
        out_shape=[jax.ShapeDtypeStruct((R, SSM_INNER), F32), jax.ShapeDtypeStruct((R, SSM_INNER), BF16),
                   jax.ShapeDtypeStruct((R, SSM_INNER), F32), jax.ShapeDtypeStruct((8, SSM_INNER), F32)],
        compiler_params=_cparams(("arbitrary",)),
    )(dout, y2[0], y2[1], xbc, parts, dskip, nw)


CONV_W = SSM_INNER + 2 * SSM_G * SSM_N
CONV_BLK = 1024


def _conv_masks(tr, is_ctx):
    t = lax.broadcasted_iota(jnp.int32, (tr, 1), 0)
    pos = jnp.where(is_ctx, t, t & (GRID_W - 1))
    seg = jnp.where(is_ctx, tr, GRID_W)
    return pos, seg


def _shift_rows(u, s, tr):
    return u if s == 0 else pltpu.roll(u, (-s) % tr, 0)


def _conv_fwd(parts, cw, cb, n_lat_tiles, tr):
    R = parts.shape[0]

    def body(u_ref, w_ref, b_ref, o_ref):
        pos, seg = _conv_masks(tr, pl.program_id(0) >= n_lat_tiles)
        u = u_ref[...]
        acc = jnp.zeros_like(u) + b_ref[...]
        for j in range(SSM_CONV):
            s = j - CONV_LEFT
            ok = (pos + s >= 0) & (pos + s < seg)
            acc = acc + jnp.where(ok, _shift_rows(u, s, tr), 0.0) * w_ref[j:j + 1, :]
        o_ref[...] = _silu(acc)

    return pl.pallas_call(
        body, name="conv_fwd", grid=(R // tr, CONV_W // CONV_BLK),
        in_specs=[pl.BlockSpec((tr, CONV_BLK), lambda i, j: (i, COL_XBC // CONV_BLK + j)),
                  pl.BlockSpec((SSM_CONV, CONV_BLK), lambda i, j: (0, j)), pl.BlockSpec((1, CONV_BLK), lambda i, j: (0, j))],
        out_specs=pl.BlockSpec((tr, CONV_BLK), lambda i, j: (i, j)), out_shape=jax.ShapeDtypeStruct((R, CONV_W), F32),
        compiler_params=_cparams(("arbitrary", "arbitrary")),
    )(parts, cw, cb)


def _conv_bwd(dxbc, parts, cw, cb, n_lat_tiles, tr):
    R = parts.shape[0]

    def body(d_ref, u_ref, w_ref, b_ref, du_ref, acc_ref):
        @pl.when(pl.program_id(1) == 0)
        def _():
            acc_ref[...] = jnp.zeros_like(acc_ref)

        pos, seg = _conv_masks(tr, pl.program_id(1) >= n_lat_tiles)
        u = u_ref[...]
        pre = jnp.zeros_like(u) + b_ref[...]
        taps = []
        for j in range(SSM_CONV):
            s = j - CONV_LEFT
            ok = (pos + s >= 0) & (pos + s < seg)
            tap = jnp.where(ok, _shift_rows(u, s, tr), 0.0)
            taps.append(tap)
            pre = pre + tap * w_ref[j:j + 1, :]
        dpre = d_ref[...] * _dsilu(pre)
        du = jnp.zeros_like(u)
        for j in range(SSM_CONV):
            s = j - CONV_LEFT
            acc_ref[j:j + 1, :] += _colsum(dpre * taps[j])
            ok = (pos - s >= 0) & (pos - s < seg)
            du = du + jnp.where(ok, _shift_rows(dpre, -s, tr), 0.0) * w_ref[j:j + 1, :]
        acc_ref[SSM_CONV:SSM_CONV + 1, :] += _colsum(dpre)
        du_ref[...] = du.astype(BF16)

    return pl.pallas_call(
        body, name="conv_bwd", grid=(CONV_W // CONV_BLK, R // tr),
        in_specs=[pl.BlockSpec((tr, CONV_BLK), lambda j, i: (i, j)),
                  pl.BlockSpec((tr, CONV_BLK), lambda j, i: (i, COL_XBC // CONV_BLK + j)),
                  pl.BlockSpec((SSM_CONV, CONV_BLK), lambda j, i: (0, j)), pl.BlockSpec((1, CONV_BLK), lambda j, i: (0, j))],
        out_specs=[pl.BlockSpec((tr, CONV_BLK), lambda j, i: (i, j)), pl.BlockSpec((8, CONV_BLK), lambda j, i: (0, j))],
        out_shape=[jax.ShapeDtypeStruct((R, CONV_W), BF16), jax.ShapeDtypeStruct((8, CONV_W), F32)],
        compiler_params=_cparams(("arbitrary", "arbitrary")),
    )(dxbc, parts, cw, cb)


def _chunk_row_block(d, i, n_lat, n_ctx):
    fwd = jnp.where(i < n_ctx, n_lat + i, i - n_ctx)
    rev = n_lat + n_ctx - 1 - i
    if isinstance(d, int):
        return rev if d else fwd
    return jnp.where(d == 0, fwd, rev)


def _tri(n, d, transpose=False):
    row = lax.broadcasted_iota(jnp.int32, (n, n), 0)
    col = lax.broadcasted_iota(jnp.int32, (n, n), 1)
    diff = (col - row) if transpose else (row - col)
    return diff * (1 - 2 * d) >= 0


def _gla_gates(sm, uhi, ulo, bias, d):
    pre = _nn3(sm, uhi, ulo) + bias
    g = _log_sigmoid(pre) * (1.0 / GLA_TAU)
    mask = _tri(GLA_C, d)
    b = _x_nn(mask.astype(BF16), g)
    b_tot = _colsum(g)
    b_ref = b[GLA_C // 2:GLA_C // 2 + 1, :]
    e_q = jnp.exp(jnp.minimum(b - b_ref, EXP_CLAMP))
    e_k = jnp.exp(jnp.minimum(b_ref - b, EXP_CLAMP))
    return pre, mask, b_tot, e_q, e_k, jnp.exp(b), jnp.exp(b_tot - b)


GLA_QK = GLA_H * GLA_DK
GLA_V = GLA_H * GLA_DV


def _gla_specs(n_lat, n_ctx, step_of):
    rbs = [lambda i, d=d: _chunk_row_block(d, step_of(i), n_lat, n_ctx) for d in range(2)]
    specs = []
    for rb in rbs:
        specs += [pl.BlockSpec((GLA_C, GLA_QK), lambda i, rb=rb: (rb(i), COL_Q // GLA_QK)),
                  pl.BlockSpec((GLA_C, GLA_QK), lambda i, rb=rb: (rb(i), COL_K // GLA_QK)),
                  pl.BlockSpec((GLA_C, GLA_V), lambda i, rb=rb: (rb(i), COL_V // GLA_V)),
                  pl.BlockSpec((GLA_C, 128), lambda i, rb=rb: (rb(i), COL_SM // 128))]
    specs += [pl.BlockSpec((2, 128, GLA_QK), lambda i: (0, 0, 0)), pl.BlockSpec((2, 128, GLA_QK), lambda i: (0, 0, 0)),
              pl.BlockSpec((2, 1, GLA_QK), lambda i: (0, 0, 0))]
    return specs, rbs


def _gla_fwd(parts, uhi, ulo, bias, n_lat, n_ctx):
    R = parts.shape[0]
    n_steps = n_lat + n_ctx
    scale = GLA_DK ** -0.5

    def body(*refs):
        ins, (uhi_ref, ulo_ref, bias_ref), o_refs, hist_ref, st = refs[:8], refs[8:11], refs[11:13], refs[13], refs[14]

        @pl.when(pl.program_id(0) == 0)
        def _():
            st[...] = jnp.zeros_like(st)

        for d in range(2):
            q_ref, k_ref, v_ref, sm_ref = ins[4 * d:4 * d + 4]
            _, mask, b_tot, e_q, e_k, e_in, e_out = _gla_gates(sm_ref[...], uhi_ref[d], ulo_ref[d], bias_ref[d], d)
            q, k, v = q_ref[...] * scale, k_ref[...], _bf(v_ref[...])
            qb, kb, q_in, k_out, decay = _bf(q * e_q), _bf(k * e_k), _bf(q * e_in), _bf(k * e_out), jnp.exp(b_tot)
            for h in range(GLA_H):
                sk, sv = slice(h * GLA_DK, (h + 1) * GLA_DK), slice(h * GLA_DV, (h + 1) * GLA_DV)
                att = jnp.where(mask, _nt(qb[:, sk], kb[:, sk]), 0.0)
                s_in = st[d, h]
                hist_ref[d, 0, h] = s_in
                o_refs[d][:, sv] = _nn(_bf(att), v[:, sv]) + _nt(q_in[:, sk], _bf(s_in))
                st[d, h] = decay[:, sk] * s_in + _tn(v[:, sv], k_out[:, sk])

    in_specs, rbs = _gla_specs(n_lat, n_ctx, lambda i: i)
    return pl.pallas_call(
        body, name="gla_fwd", grid=(n_steps,), in_specs=in_specs,
        out_specs=[pl.BlockSpec((GLA_C, GLA_V), lambda i, rb=rb: (rb(i), 0)) for rb in rbs]
        + [pl.BlockSpec((2, 1, GLA_H, GLA_DV, GLA_DK), lambda i: (0, i, 0, 0, 0))],
        out_shape=[jax.ShapeDtypeStruct((R, GLA_V), F32)] * 2 + [jax.ShapeDtypeStruct((2, n_steps, GLA_H, GLA_DV, GLA_DK), F32)],
        scratch_shapes=[pltpu.VMEM((2, GLA_H, GLA_DV, GLA_DK), F32)],
        compiler_params=_cparams(("arbitrary",)),
    )(*([parts] * 8), uhi, ulo, bias)


def _gla_bwd(do, parts, uhi, ulo, bias, hist, n_lat, n_ctx):
    R = parts.shape[0]
    n_steps = n_lat + n_ctx
    scale = GLA_DK ** -0.5
    step_of = lambda j: n_steps - 1 - j

    def body(*refs):
        ins, (uhi_ref, ulo_ref, bias_ref), do_refs, hist_ref = refs[:8], refs[8:11], refs[11:13], refs[13]
        outs, dst = refs[14:22], refs[22]

        @pl.when(pl.program_id(0) == 0)
        def _():
            dst[...] = jnp.zeros_like(dst)

        for d in range(2):
            q_ref, k_ref, v_ref, sm_ref = ins[4 * d:4 * d + 4]
            dq_ref, dk_ref, dv_ref, dp_ref = outs[4 * d:4 * d + 4]
            pre, mask, b_tot, e_q, e_k, e_in, e_out = _gla_gates(sm_ref[...], uhi_ref[d], ulo_ref[d], bias_ref[d], d)
            q, k, v = q_ref[...] * scale, k_ref[...], _bf(v_ref[...])
            dout = _bf(do_refs[d][...])
            k_out_f = k * e_out
            qb, kb, q_in, k_out, decay = _bf(q * e_q), _bf(k * e_k), _bf(q * e_in), _bf(k_out_f), jnp.exp(b_tot)
            dqs, dks, dk_outs, dss = [], [], [], []
            for h in range(GLA_H):
                sk, sv = slice(h * GLA_DK, (h + 1) * GLA_DK), slice(h * GLA_DV, (h + 1) * GLA_DV)
                s_in, ds = hist_ref[d, 0, h], dst[d, h]
                att = jnp.where(mask, _nt(qb[:, sk], kb[:, sk]), 0.0)
                datt = _bf(jnp.where(mask, _nt(dout[:, sv], v[:, sv]), 0.0))
                dv_ref[:, sv] = _tn(_bf(att), dout[:, sv]) + _nt(k_out[:, sk], _bf(ds))
                dqs.append(_nn(datt, kb[:, sk]) * e_q[:, sk] + _nn(dout[:, sv], _bf(s_in)) * e_in[:, sk])
                dk_o = _nn(v[:, sv], _bf(ds))
                dk_outs.append(dk_o)
                dks.append(_tn(datt, qb[:, sk]) * e_k[:, sk])
                dss.append(_colsum(ds * s_in))
                dst[d, h] = decay[:, sk] * ds + _tn(dout[:, sv], q_in[:, sk])
            dq, dk_out = jnp.concatenate(dqs, axis=1), jnp.concatenate(dk_outs, axis=1)
            dk = jnp.concatenate(dks, axis=1) + dk_out * e_out
            dq_ref[...] = dq * scale
            dk_ref[...] = dk
            db_tot = _colsum(dk_out * k_out_f) + decay * jnp.concatenate(dss, axis=1)
            dg = _x_nn(_tri(GLA_C, d, transpose=True).astype(BF16), dq * q - dk * k) + db_tot
            dp_ref[...] = dg * (1.0 / GLA_TAU) * _sigmoid(-pre)

    in_specs, rbs = _gla_specs(n_lat, n_ctx, step_of)
    in_specs += [pl.BlockSpec((GLA_C, GLA_V), lambda j, rb=rb: (rb(j), 0)) for rb in rbs]
    in_specs += [pl.BlockSpec((2, 1, GLA_H, GLA_DV, GLA_DK), lambda j: (0, step_of(j), 0, 0, 0))]
    out_specs, out_shape = [], []
    for rb in rbs:
        for w in (GLA_QK, GLA_QK, GLA_V, GLA_QK):
            out_specs.append(pl.BlockSpec((GLA_C, w), lambda j, rb=rb: (rb(j), 0)))
            out_shape.append(jax.ShapeDtypeStruct((R, w), F32))
    outs = pl.pallas_call(
        body, name="gla_bwd", grid=(n_steps,), in_specs=in_specs, out_specs=out_specs, out_shape=out_shape,
        scratch_shapes=[pltpu.VMEM((2, GLA_H, GLA_DV, GLA_DK), F32)],
        compiler_params=_cparams(("arbitrary",)),
    )(*([parts] * 8), uhi, ulo, bias, do, do, hist)
    return [(outs[k], outs[4 + k]) for k in range(4)]


def _ssd_consts(dt_bias, a_log):
    sel = np.zeros((2, SSM_G, 128, 128), np.float32)
    for d, base in enumerate((SM_DTF, SM_DTB)):
        for g in range(SSM_G):
            for e in range(SSM_HPG):
                sel[d, g, base + SSM_HPG * g + e, e] = 1.0
    e512 = np.zeros((128, SSM_GW), np.float32)
    e1024 = np.zeros((128, SSM_HPG * 128), np.float32)
    for e in range(SSM_HPG):
        e512[e, SSM_P * e:SSM_P * (e + 1)] = 1.0
        e1024[e, 128 * e:128 * (e + 1)] = 1.0
    pad = lambda v: jnp.pad(v.reshape(2, SSM_G, 1, SSM_HPG), ((0, 0), (0, 0), (0, 0), (0, 128 - SSM_HPG)))
    return dict(
        sel=jnp.asarray(sel, BF16), sel_t=jnp.asarray(sel.transpose(0, 1, 3, 2), BF16),
        e512=jnp.asarray(e512, BF16), e512_t=jnp.asarray(e512.T, BF16), e1024=jnp.asarray(e1024, BF16),
        dtb=pad(dt_bias), a=pad(-jnp.exp(a_log)))


def _ssd_common(sm, sel, dtb, a_neg, e512, e1024, d):
    dtr8 = _nn_x(sm, sel) + dtb
    dt8 = _softplus(dtr8)
    a8 = a_neg * dt8
    mask = _tri(SSM_C, d)
    mask_t = _tri(SSM_C, d, transpose=True).astype(BF16)
    cum8 = _x_nn(mask.astype(BF16), a8)
    a_hi, a_lo = _split(a8)
    cum_t = _tn(a_hi, mask_t) + _tn(a_lo, mask_t)
    a_exp = _nn_x(a8, e512)
    return dict(dtr8=dtr8, a8=a8, mask=mask, mask_t=mask_t, cum_t=cum_t, dt_exp=_nn_x(dt8, e512), a_exp=a_exp,
                cum_exp=_nn_x(cum8, e512), cum_rep=_nn_x(cum8, e1024), tot_exp=_colsum(a_exp))


def _ssd_decay(cm, e):
    diff = cm["cum_rep"][:, 128 * e:128 * (e + 1)] - cm["cum_t"][e:e + 1, :]
    return jnp.where(cm["mask"], jnp.exp(jnp.minimum(diff, 0.0)), 0.0)


def _ssd_specs(n_lat, n_ctx, step_of):
    rbs = [lambda i, d=d: _chunk_row_block(d, step_of(i), n_lat, n_ctx) for d in range(2)]
    specs = []
    for rb in rbs:
        specs += [pl.BlockSpec((SSM_C, SSM_GW), lambda g, i, rb=rb: (rb(i), g)),
                  pl.BlockSpec((SSM_C, SSM_N), lambda g, i, rb=rb: (rb(i), SSM_INNER // SSM_N + g)),
                  pl.BlockSpec((SSM_C, SSM_N), lambda g, i, rb=rb: (rb(i), SSM_INNER // SSM_N + SSM_G + g)),
                  pl.BlockSpec((SSM_C, 128), lambda g, i, rb=rb: (rb(i), COL_SM // 128))]
    specs += [pl.BlockSpec((2, 1, 128, 128), lambda g, i: (0, g, 0, 0)),
              pl.BlockSpec((2, 1, 1, 128), lambda g, i: (0, g, 0, 0)),
              pl.BlockSpec((2, 1, 1, 128), lambda g, i: (0, g, 0, 0)),
              pl.BlockSpec((128, SSM_GW), lambda g, i: (0, 0)),
              pl.BlockSpec((128, SSM_HPG * 128), lambda g, i: (0, 0))]
    return specs, rbs


def _ssd_fwd(xbc, parts, k, n_lat, n_ctx):
    R = parts.shape[0]
    n_steps = n_lat + n_ctx

    def body(*refs):
        ins, (sel_ref, dtb_ref, a_ref, e512_ref, e1024_ref), y_refs, hist_ref, st = refs[:8], refs[8:13], refs[13:15], refs[15], refs[16]

        @pl.when(pl.program_id(1) == 0)
        def _():
            st[...] = jnp.zeros_like(st)

        for d in range(2):
            x_ref, b_ref, c_ref, sm_ref = ins[4 * d:4 * d + 4]
            cm = _ssd_common(sm_ref[...], sel_ref[d, 0], dtb_ref[d, 0], a_ref[d, 0], e512_ref[...], e1024_ref[...], d)
            bm, cmat = _bf(b_ref[...]), _bf(c_ref[...])
            xdt = x_ref[...] * cm["dt_exp"]
            cb = _nt(cmat, bm)
            ys = [_nn(_bf(cb * _ssd_decay(cm, e)), _bf(xdt[:, SSM_P * e:SSM_P * (e + 1)])) for e in range(SSM_HPG)]
            s_in = st[d]
            hist_ref[d, 0, 0] = s_in
            y_refs[d][...] = jnp.concatenate(ys, axis=1) + jnp.exp(cm["cum_exp"]) * _nn(cmat, _bf(s_in))
            st[d] = jnp.exp(cm["tot_exp"]) * s_in + _tn(bm, _bf(xdt * jnp.exp(cm["tot_exp"] - cm["cum_exp"])))

    in_specs, rbs = _ssd_specs(n_lat, n_ctx, lambda i: i)
    return pl.pallas_call(
        body, name="ssd_fwd", grid=(SSM_G, n_steps), in_specs=in_specs,
        out_specs=[pl.BlockSpec((SSM_C, SSM_GW), lambda g, i, rb=rb: (rb(i), g)) for rb in rbs]
        + [pl.BlockSpec((2, 1, 1, SSM_N, SSM_GW), lambda g, i: (0, i, g, 0, 0))],
        out_shape=[jax.ShapeDtypeStruct((R, SSM_INNER), F32)] * 2
        + [jax.ShapeDtypeStruct((2, n_steps, SSM_G, SSM_N, SSM_GW), F32)],
        scratch_shapes=[pltpu.VMEM((2, SSM_N, SSM_GW), F32)],
        compiler_params=_cparams(("arbitrary", "arbitrary")),
    )(*([xbc, xbc, xbc, parts] * 2), k["sel"], k["dtb"], k["a"], k["e512"], k["e1024"])


def _ssd_bwd(dy, xbc, parts, k, hist, n_lat, n_ctx):
    R = parts.shape[0]
    n_steps = n_lat + n_ctx
    step_of = lambda j: n_steps - 1 - j

    def one(d, x_ref, b_ref, c_ref, sm_ref, sel_ref, dtb_ref, a_ref, e512_ref, e1024_ref, selt_ref, e512t_ref, dy_ref,
            hist_ref, dx_ref, db_ref, dc_ref, dsm_ref, acc_ref, dst):
        a_neg, e512_t = a_ref[d, 0], e512t_ref[...]
        cm = _ssd_common(sm_ref[...], sel_ref[d, 0], dtb_ref[d, 0], a_neg, e512_ref[...], e1024_ref[...], d)
        x, dyv = x_ref[...], dy_ref[...]
        bm, cmat = _bf(b_ref[...]), _bf(c_ref[...])
        xdt = x * cm["dt_exp"]
        cb = _nt(cmat, bm)
        s_in, ds = hist_ref[d, 0, 0], dst[d]
        w = jnp.exp(cm["tot_exp"] - cm["cum_exp"])
        z = _nn(bm, _bf(ds))
        decay_in = jnp.exp(cm["cum_exp"])
        gy = _bf(dyv * decay_in)
        dcb = jnp.zeros((SSM_C, SSM_C), F32)
        dxs, crossing = [], []
        row = lax.broadcasted_iota(jnp.int32, (SSM_C, SSM_C), 0)
        col = lax.broadcasted_iota(jnp.int32, (SSM_C, SSM_C), 1)
        eye = (row == col).astype(BF16)
        before = (cm["mask_t"] - eye)
        for e in range(SSM_HPG):
            s = slice(SSM_P * e, SSM_P * (e + 1))
            lm = _ssd_decay(cm, e)
            dy_e = _bf(dyv[:, s])
            m_e = cb * lm
            dm_e = _nt(dy_e, _bf(xdt[:, s]))
            dcb = dcb + dm_e * lm
            dxs.append(_tn(_bf(m_e), dy_e))
            through = jnp.where(cm["mask"], _nn_x(dm_e * m_e, before), 0.0)
            crossing.append(_colsum(through))
        da_rows = jnp.concatenate(crossing + [jnp.zeros((128 - SSM_HPG, SSM_C), F32)], axis=0)
        r_hi, r_lo = _split(da_rows)
        da8_intra = _tn(r_hi, eye) + _tn(r_lo, eye)
        dx_state = w * z
        dxdt = jnp.concatenate(dxs, axis=1) + dx_state
        dcb = _bf(dcb)
        c_s = _nn(cmat, _bf(s_in))
        dc_ref[...] = _nn(dcb, bm) + _nt(gy, _bf(s_in))
        db_ref[...] = _tn(dcb, cmat) + _nt(_bf(w * xdt), _bf(ds))
        dst[d] = jnp.exp(cm["tot_exp"]) * ds + _tn(cmat, gy)
        dcum8 = _nn_x(dyv * decay_in * c_s - xdt * dx_state, e512_t)
        tot8 = _colsum(cm["a8"])
        dtot8 = _colsum(_nn_x(xdt * dx_state, e512_t)) + jnp.exp(tot8) * _colsum(_nn_x(ds * s_in, e512_t))
        da8 = da8_intra + _x_nn(cm["mask_t"], dcum8) + dtot8
        ddt8 = da8 * a_neg + _nn_x(dxdt * x, e512_t)
        dsm_ref[0] = _nn_x(ddt8 * _sigmoid(cm["dtr8"]), selt_ref[d, 0])
        dx_ref[...] = dxdt * cm["dt_exp"]
        acc_ref[d, 0, 0:1, :] += _colsum(da8 * cm["a8"])

    def body(*refs):
        ins, consts, (selt_ref, e512t_ref), dy_refs, hist_ref = refs[:8], refs[8:13], refs[13:15], refs[15:17], refs[17]
        outs, acc_ref, dst = refs[18:26], refs[26], refs[27]

        @pl.when(pl.program_id(1) == 0)
        def _():
            dst[...] = jnp.zeros_like(dst)
            acc_ref[...] = jnp.zeros_like(acc_ref)

        for d in range(2):
            one(d, *ins[4 * d:4 * d + 4], *consts, selt_ref, e512t_ref, dy_refs[d], hist_ref, *outs[4 * d:4 * d + 4], acc_ref, dst)

    in_specs, rbs = _ssd_specs(n_lat, n_ctx, step_of)
    in_specs += [pl.BlockSpec((2, 1, 128, 128), lambda g, j: (0, g, 0, 0)), pl.BlockSpec((SSM_GW, 128), lambda g, j: (0, 0))]
    in_specs += [pl.BlockSpec((SSM_C, SSM_GW), lambda g, j, rb=rb: (rb(j), g)) for rb in rbs]
    in_specs += [pl.BlockSpec((2, 1, 1, SSM_N, SSM_GW), lambda g, j: (0, step_of(j), g, 0, 0))]
    out_specs, out_shape = [], []
    for rb in rbs:
        out_specs += [pl.BlockSpec((SSM_C, SSM_GW), lambda g, j, rb=rb: (rb(j), g)),
                      pl.BlockSpec((SSM_C, SSM_N), lambda g, j, rb=rb: (rb(j), g)),
                      pl.BlockSpec((SSM_C, SSM_N), lambda g, j, rb=rb: (rb(j), g)),
                      pl.BlockSpec((1, SSM_C, 128), lambda g, j, rb=rb: (g, rb(j), 0))]
        out_shape += [jax.ShapeDtypeStruct((R, SSM_INNER), F32), jax.ShapeDtypeStruct((R, SSM_G * SSM_N), F32),
                      jax.ShapeDtypeStruct((R, SSM_G * SSM_N), F32), jax.ShapeDtypeStruct((SSM_G, R, 128), F32)]
    out_specs.append(pl.BlockSpec((2, 1, 8, 128), lambda g, j: (0, g, 0, 0)))
    out_shape.append(jax.ShapeDtypeStruct((2, SSM_G, 8, 128), F32))
    outs = pl.pallas_call(
        body, name="ssd_bwd", grid=(SSM_G, n_steps), in_specs=in_specs, out_specs=out_specs, out_shape=out_shape,
        scratch_shapes=[pltpu.VMEM((2, SSM_N, SSM_GW), F32)],
        compiler_params=_cparams(("arbitrary", "arbitrary")),
    )(*([xbc, xbc, xbc, parts] * 2), k["sel"], k["dtb"], k["a"], k["e512"], k["e1024"], k["sel_t"], k["e512_t"], dy, dy, hist)
    return [(outs[n], outs[4 + n]) for n in range(4)] + [outs[8]]


def _gla_assemble(dq, dk, dv, tr):
    R = dq[0].shape[0]
    qk = GLA_H * GLA_DK

    def body(dqf_ref, dqb_ref, dkf_ref, dkb_ref, dvf_ref, dvb_ref, o_ref):
        o_ref[:, 0:qk] = (dqf_ref[...] + dqb_ref[...]).astype(BF16)
        o_ref[:, qk:2 * qk] = (dkf_ref[...] + dkb_ref[...]).astype(BF16)
        o_ref[:, 2 * qk:] = (dvf_ref[...] + dvb_ref[...]).astype(BF16)

    return pl.pallas_call(
        body, name="gla_assemble", grid=(R // tr,), in_specs=[_row_spec(tr, qk)] * 4 + [_row_spec(tr, D)] * 2,
        out_specs=_row_spec(tr, 2 * D),
        out_shape=jax.ShapeDtypeStruct((R, 2 * D), BF16), compiler_params=_cparams(("arbitrary",)),
    )(*dq, *dk, *dv)


def _xbc_assemble(dx, db, dc, dxs_skip, tr):
    R = dx[0].shape[0]
    bc = SSM_G * SSM_N

    def body(dxf_ref, dxb_ref, dbf_ref, dbb_ref, dcf_ref, dcb_ref, sk_ref, o_ref):
        o_ref[:, 0:SSM_INNER] = dxf_ref[...] + dxb_ref[...] + sk_ref[...]
        o_ref[:, SSM_INNER:SSM_INNER + bc] = dbf_ref[...] + dbb_ref[...]
        o_ref[:, SSM_INNER + bc:] = dcf_ref[...] + dcb_ref[...]

    return pl.pallas_call(
        body, name="xbc_assemble", grid=(R // tr,),
        in_specs=[_row_spec(tr, SSM_INNER)] * 2 + [_row_spec(tr, bc)] * 4 + [_row_spec(tr, SSM_INNER)],
        out_specs=_row_spec(tr, CONV_W), out_shape=jax.ShapeDtypeStruct((R, CONV_W), F32),
        compiler_params=_cparams(("arbitrary",)),
    )(*dx, *db, *dc, dxs_skip)


def _small_assemble(dp, dsm, parts, ut_hi, ut_lo, tr):
    R = parts.shape[0]
    qk = GLA_H * GLA_DK

    def body(dpf_ref, dpb_ref, dsmf_ref, dsmb_ref, sm_ref, uth_ref, utl_ref, o_ref, dup_ref, acc_ref, acc2_ref):
        @pl.when(pl.program_id(0) == 0)
        def _():
            dup_ref[...] = jnp.zeros_like(dup_ref)
            acc_ref[...] = jnp.zeros_like(acc_ref)
            acc2_ref[...] = jnp.zeros_like(acc2_ref)

        ssd = dsmf_ref[0] + dsmb_ref[0]
        for g in range(1, SSM_G):
            ssd = ssd + (dsmf_ref[g] + dsmb_ref[g])
        acc2_ref[0:1, :] += _colsum(ssd)
        sm_hi, sm_lo = _split(sm_ref[...])
        out = ssd
        for d, dp_ref in enumerate((dpf_ref, dpb_ref)):
            dpd = dp_ref[...]
            out = out + _nn3(dpd, uth_ref[d], utl_ref[d])
            p_hi, p_lo = _split(dpd)
            dup_ref[d] += _tn(sm_hi, p_hi) + _tn(sm_lo, p_hi) + _tn(sm_hi, p_lo)
            acc_ref[d:d + 1, :] += _colsum(dpd)
        o_ref[...] = out.astype(BF16)

    return pl.pallas_call(
        body, name="small_assemble", grid=(R // tr,),
        in_specs=[_row_spec(tr, qk)] * 2 + [pl.BlockSpec((SSM_G, tr, 128), lambda i: (0, i, 0))] * 2
        + [_row_spec(tr, 128, COL_SM // 128), pl.BlockSpec((2, qk, 128), lambda i: (0, 0, 0)),
           pl.BlockSpec((2, qk, 128), lambda i: (0, 0, 0))],
        out_specs=[_row_spec(tr, 128), pl.BlockSpec((2, 128, qk), lambda i: (0, 0, 0)), _acc_spec(qk), _acc_spec(128)],
        out_shape=[jax.ShapeDtypeStruct((R, 128), BF16), jax.ShapeDtypeStruct((2, 128, qk), F32),
                   jax.ShapeDtypeStruct((8, qk), F32), jax.ShapeDtypeStruct((8, 128), F32)],
        compiler_params=_cparams(("arbitrary",)),
    )(*dp, *dsm, parts, ut_hi, ut_lo)


ADA_ROWS = 16
ADA_TILE = 512


def _dot3_f32(a, b, ca, cb):
    a_hi, a_lo = _split(a)
    b_hi, b_lo = _split(b)
    return _dg(a_hi, b_hi, ca, cb) + _dg(a_lo, b_hi, ca, cb) + _dg(a_hi, b_lo, ca, cb)


def _ada_fwd(cvec, w, b):
    cols = w.shape[1]

    def body(c_ref, w_ref, b_ref, o_ref):
        o_ref[...] = _dot3_f32(_silu(c_ref[...]), w_ref[...], 1, 0) + b_ref[...]

    return pl.pallas_call(
        body, name="ada_fwd", grid=(cols // ADA_TILE,),
        in_specs=[pl.BlockSpec((ADA_ROWS, D), lambda j: (0, 0)), pl.BlockSpec((D, ADA_TILE), lambda j: (0, j)),
                  pl.BlockSpec((1, ADA_TILE), lambda j: (0, j))],
        out_specs=pl.BlockSpec((ADA_ROWS, ADA_TILE), lambda j: (0, j)), out_shape=jax.ShapeDtypeStruct((ADA_ROWS, cols), F32),
        compiler_params=_cparams(("arbitrary",)),
    )(cvec, w, b)


def _adam(w, g, m, v):
    m2 = ADAM_B1 * m + (1.0 - ADAM_B1) * g
    v2 = ADAM_B2 * v + (1.0 - ADAM_B2) * (g * g)
    m_hat = m2 / (1.0 - ADAM_B1 ** ADAM_STEP)
    v_hat = v2 / (1.0 - ADAM_B2 ** ADAM_STEP)
    return -ADAM_LR * (m_hat / (jnp.sqrt(v_hat) + ADAM_EPS) + ADAM_WD * w), m2, v2


def _wada_bwd_adam(cvec, dada, w, m, v):
    rows, cols = w.shape
    tr = _tile(rows, 256, 128)

    def body(c_ref, d_ref, w_ref, m_ref, v_ref, g_ref, dl_ref, m2_ref, v2_ref, p_ref):
        wv = w_ref[...]
        g = _dot3_f32(_silu(c_ref[...]), d_ref[...], 0, 0)
        g_ref[...] = g
        dl_ref[...], m2_ref[...], v2_ref[...] = _adam(wv, g, m_ref[...], v_ref[...])
        p_ref[...] = _dot3_f32(d_ref[...], wv, 1, 1)

    blk = pl.BlockSpec((tr, cols), lambda i: (i, 0))
    return pl.pallas_call(
        body, name="wada_bwd_adam", grid=(rows // tr,),
        in_specs=[pl.BlockSpec((ADA_ROWS, tr), lambda i: (0, i)), pl.BlockSpec((ADA_ROWS, cols), lambda i: (0, 0)), blk, blk, blk],
        out_specs=[blk, blk, blk, blk, pl.BlockSpec((ADA_ROWS, tr), lambda i: (0, i))],
        out_shape=[jax.ShapeDtypeStruct((rows, cols), F32)] * 4 + [jax.ShapeDtypeStruct((ADA_ROWS, rows), F32)],
        compiler_params=_cparams(("arbitrary",)),
    )(cvec, dada, w, m, v)


def _reduce_adam(parts8, w, m, v, name):
    rows, cols = w.shape
    tr = _tile(rows, 64, 16)

    def body(p_ref, w_ref, m_ref, v_ref, g_ref, dl_ref, m2_ref, v2_ref):
        g = p_ref[0].astype(F32) + p_ref[N_CHIPS].astype(F32)
        for j in range(1, N_CHIPS):
            g = g + (p_ref[j].astype(F32) + p_ref[N_CHIPS + j].astype(F32))
        g_ref[...] = g
        dl_ref[...], m2_ref[...], v2_ref[...] = _adam(w_ref[...], g, m_ref[...], v_ref[...])

    blk = pl.BlockSpec((tr, cols), lambda i: (i, 0))
    return pl.pallas_call(
        body, name=name, grid=(rows // tr,), in_specs=[pl.BlockSpec((N_DEV, tr, cols), lambda i: (0, i, 0)), blk, blk, blk],
        out_specs=[blk] * 4, out_shape=[jax.ShapeDtypeStruct((rows, cols), F32)] * 4, compiler_params=_cparams(("arbitrary",)),
    )(parts8, w, m, v)


SMALL_W = 1024


def _sum8(g8):
    rows = g8.shape[1]

    def body(g_ref, o_ref):
        s = g_ref[0]
        for j in range(1, N_DEV):
            s = s + g_ref[j]
        o_ref[...] = s

    return pl.pallas_call(
        body, name="sum8", out_shape=jax.ShapeDtypeStruct((rows, SMALL_W), F32),
        in_specs=[pl.BlockSpec(memory_space=pltpu.VMEM)], out_specs=pl.BlockSpec(memory_space=pltpu.VMEM),
        compiler_params=pltpu.CompilerParams(vmem_limit_bytes=VMEM_LIMIT),
    )(g8)


def _cctx_grad(p8, c_ctx):
    def body(p_ref, c_ref, o_ref):
        s = p_ref[0]
        for chip in range(1, N_CHIPS):
            s = s + p_ref[2 * chip]
        o_ref[...] = s * _dsilu(c_ref[...])

    return pl.pallas_call(
        body, name="cctx_grad", out_shape=jax.ShapeDtypeStruct((1, D), F32),
        in_specs=[pl.BlockSpec(memory_space=pltpu.VMEM)] * 2, out_specs=pl.BlockSpec(memory_space=pltpu.VMEM),
    )(p8, c_ctx)


def _adam_small(w, g, m, v):
    def body(w_ref, g_ref, m_ref, v_ref, dl_ref, m2_ref, v2_ref):
        dl_ref[...], m2_ref[...], v2_ref[...] = _adam(w_ref[...], g_ref[...], m_ref[...], v_ref[...])

    vm = pl.BlockSpec(memory_space=pltpu.VMEM)
    return pl.pallas_call(
        body, name="adam_small", out_shape=[jax.ShapeDtypeStruct(w.shape, F32)] * 3, in_specs=[vm] * 4, out_specs=[vm] * 3,
        compiler_params=pltpu.CompilerParams(vmem_limit_bytes=VMEM_LIMIT),
    )(w, g, m, v)


def _pack(vecs, width=SMALL_W, row_mult=8):
    flat = jnp.concatenate([v.reshape(-1).astype(F32) for v in vecs])
    n = flat.shape[0]
    rows = -(-n // (width * row_mult)) * row_mult
    return jnp.pad(flat, (0, rows * width - n)).reshape(rows, width)


def _unpack(packed, shapes):
    flat = packed.reshape(-1)
    out, off = [], 0
    for s in shapes:
        n = int(np.prod(s))
        out.append(flat[off:off + n].reshape(s))
        off += n
    return out


WEIGHTS = ('c_ctx', 'w_ada', 'b_ada', 'norm1_w', 'w_in', 'gla_up_f', 'gla_bias_f', 'gla_up_b', 'gla_bias_b', 'gla_norm_w',
           'conv_w', 'conv_b', 'dt_bias_f', 'dt_bias_b', 'a_log_f', 'a_log_b', 'd_skip', 'ssm_norm_w', 'w_pa', 'w_pb', 'w_out',
           'norm2_w', 'w_gate', 'w_up', 'w_down', 'final_norm_w')
BIG = ('w_in', 'w_pa', 'w_pb', 'w_out', 'w_gate', 'w_up', 'w_down')
COL_SHARDED = ('w_in', 'w_gate', 'w_up')
SMALL_SHARDED = ('gla_up_f', 'gla_up_b', 'conv_w')
ROW_TILE = 256


def _blocks_to_full(g4, name):
    n, r, c = g4.shape
    return g4.transpose(1, 0, 2).reshape(r, n * c) if name in COL_SHARDED else g4.reshape(n * r, c)


def _full_to_blocks(full, name):
    r, c = full.shape
    if name in COL_SHARDED:
        return full.reshape(r, N_CHIPS, c // N_CHIPS).transpose(1, 0, 2)
    return full.reshape(N_CHIPS, r // N_CHIPS, c)


def _permute_in(w_in_full):
    off = np.concatenate([[0], np.cumsum(IN_WIDTHS)])
    cols = [w_in_full[:, off[p]:off[p + 1]] for p in PERM]
    return jnp.concatenate(cols + [jnp.zeros((w_in_full.shape[0], SMALL_PAD), w_in_full.dtype)], axis=1)


def _unpermute_in(wp):
    off = np.concatenate([[0], np.cumsum([IN_WIDTHS[p] for p in PERM])])
    pieces = {p: wp[:, off[i]:off[i + 1]] for i, p in enumerate(PERM)}
    return jnp.concatenate([pieces[p] for p in range(len(IN_WIDTHS))], axis=1)


def _chip_cols(full, chip, n):
    return lax.dynamic_slice_in_dim(full, chip * n, n, axis=1)


def kernel(x, c, ctx, c_ctx, w_ada, b_ada, norm1_w, w_in, gla_up_f, gla_bias_f, gla_up_b, gla_bias_b, gla_norm_w, conv_w, conv_b, dt_bias_f, dt_bias_b, a_log_f, a_log_b, d_skip, ssm_norm_w, w_pa, w_pb, w_out, norm2_w, w_gate, w_up, w_down, final_norm_w, loss_target, m_c_ctx, m_w_ada, m_b_ada, m_norm1_w, m_w_in, m_gla_up_f, m_gla_bias_f, m_gla_up_b, m_gla_bias_b, m_gla_norm_w, m_conv_w, m_conv_b, m_dt_bias_f, m_dt_bias_b, m_a_log_f, m_a_log_b, m_d_skip, m_ssm_norm_w, m_w_pa, m_w_pb, m_w_out, m_norm2_w, m_w_gate, m_w_up, m_w_down, m_final_norm_w, v_c_ctx, v_w_ada, v_b_ada, v_norm1_w, v_w_in, v_gla_up_f, v_gla_bias_f, v_gla_up_b, v_gla_bias_b, v_gla_norm_w, v_conv_w, v_conv_b, v_dt_bias_f, v_dt_bias_b, v_a_log_f, v_a_log_b, v_d_skip, v_ssm_norm_w, v_w_pa, v_w_pb, v_w_out, v_norm2_w, v_w_gate, v_w_up, v_w_down, v_final_norm_w):
    given = dict(locals())
    W = {n: given[n] for n in WEIGHTS}
    M = {n: given["m_" + n] for n in WEIGHTS}
    V = {n: given["v_" + n] for n in WEIGHTS}
    L, Lc = x.shape[1], ctx.shape[1]
    tr = ROW_TILE
    assert L % tr == 0 and Lc % tr == 0 and L % Lc == 0 and Lc % SSM_C == 0
    n_lat_tiles = L // tr
    xi, yi, ci = _place()
    chip, me = 2 * xi + yi, 4 * xi + 2 * yi + ci
    xall = jnp.concatenate([x[0], ctx[0]], axis=0)

    g0 = _allgather_small(_pack([c[0]] + [W[n][0] for n in SMALL_SHARDED]), "gather_c")
    g0 = g0.reshape(N_DEV, -1)
    c_all = g0[:, :D]
    small_full, off = {}, D
    for n in SMALL_SHARDED:
        r, cols = W[n].shape[1:]
        small_full[n] = jnp.concatenate([g0[2 * k, off:off + r * cols].reshape(r, cols) for k in range(N_CHIPS)], axis=1)
        off += r * cols
    up_f, up_b, conv_w_full = (small_full[n] for n in SMALL_SHARDED)

    cvec = jnp.zeros((ADA_ROWS, D), F32).at[:N_DEV].set(c_all).at[N_DEV].set(c_ctx)
    ada_cols = w_ada.shape[2]
    ada_part = _ada_fwd(cvec, w_ada[0], _chip_cols(b_ada, chip, ada_cols))
    g1_all = _allgather_small(ada_part, "gather_ada")
    ada_full = jnp.concatenate([g1_all[2 * k] for k in range(N_CHIPS)], axis=1)
    mine = lax.dynamic_slice_in_dim(ada_full, me, 1, axis=0)
    sh1, sc1, g1, sh2, sc2, g2 = (mine[:, k * D:(k + 1) * D] for k in range(6))
    csh1, csc1 = ada_full[N_DEV:N_DEV + 1, :D], ada_full[N_DEV:N_DEV + 1, D:2 * D]
    mod = jnp.stack([jnp.stack([sh1, sc1]), jnp.stack([csh1, csc1])])

    gathered = _gather_weights([W[n][0].astype(BF16) for n in BIG], "gather_weights")
    full = {n: _blocks_to_full(g, n) for n, g in zip(BIG, gathered)}
    wp = _permute_in(full['w_in'])

    def lr_rows(up, base):
        return jnp.zeros((128, GLA_H * GLA_DK), F32).at[base:base + GLA_RANK].set(up)
    u2 = jnp.stack([lr_rows(up_f, SM_LRF), lr_rows(up_b, SM_LRB)])
    u2_hi = u2.astype(BF16)
    u2_lo = (u2 - u2_hi.astype(F32)).astype(BF16)
    ut = u2.transpose(0, 2, 1)
    ut_hi = ut.astype(BF16)
    ut_lo = (ut - ut_hi.astype(F32)).astype(BF16)
    gbias = jnp.stack([gla_bias_f, gla_bias_b])
    kc = _ssd_consts(jnp.stack([dt_bias_f[0], dt_bias_b[0]]), jnp.stack([a_log_f[0], a_log_b[0]]))
    gw4 = jnp.tile(gla_norm_w, (1, GLA_H))
    dskip_exp = jnp.repeat(d_skip, SSM_P, axis=1)
    n_gla = (L // GLA_C, Lc // GLA_C)
    n_ssd = (L // SSM_C, Lc // SSM_C)

    h1 = _norm_mod(xall, norm1_w, mod, n_lat_tiles, tr)
    parts = _mm(h1, wp, "nn", F32, "mm_in", tm=768, tn=1152)
    xbc = _conv_fwd(parts, conv_w_full, conv_b, L // Lc, Lc)
    *o2, gla_hist = _gla_fwd(parts, u2_hi, u2_lo, gbias, *n_gla)
    *y2, ssd_hist = _ssd_fwd(xbc, parts, kc, *n_ssd)
    oan = _gla_out(o2, parts, gw4, tr)
    obn = _ssd_out(y2, xbc, parts, dskip_exp, ssm_norm_w, tr)
    ya = _mm(oan, full['w_pa'], "nn", F32, "mm_pa", tm=768)
    yb = _mm(obn, full['w_pb'], "nn", F32, "mm_pb", tm=768)
    merged = _merge(ya, yb, parts, tr)
    mix = _mm(merged, full['w_out'], "nn", F32, "mm_out", tm=768)
    h2, u = _resid_norm_mod(xall, mix, g1, norm2_w, sh2, sc2, tr)
    gp = _mm(u, full['w_gate'], "nn", F32, "mm_gate", tm=768, tn=1408)
    up = _mm(u, full['w_up'], "nn", F32, "mm_up", tm=768, tn=1408)
    act = _swiglu_act(gp, up, tr)
    f = _mm(act, full['w_down'], "nn", F32, "mm_down", tm=768)
    dh3, df, acc_loss = _loss_head(h2, f, loss_target[0], g2, final_norm_w[None], n_lat_tiles, tr)

    dw = {}
    da = _mm(df, full['w_down'], "nt", F32, "mm_down_dx", tm=768, tn=1408)
    dw['w_down'] = _mm(act, df, "tn", BF16, "mm_down_dw", tm=1408, tk=768)
    dgp, dup = _swiglu_act_bwd(da, gp, up, tr)
    du_a = _mm(dgp, full['w_gate'], "nt", F32, "mm_gate_dx", tm=768, tk=1408)
    du_b = _mm(dup, full['w_up'], "nt", F32, "mm_up_dx", tm=768, tk=1408)
    dw['w_gate'] = _mm(u, dgp, "tn", BF16, "mm_gate_dw", tm=1024, tn=1408, tk=768)
    dw['w_up'] = _mm(u, dup, "tn", BF16, "mm_up_dw", tm=1024, tn=1408, tk=768)
    dh2, dmix, acc_ffn = _ffn_in_bwd(du_a, du_b, h2, dh3, mix, sc2, g1, norm2_w, tr)
    dmerged = _mm(dmix, full['w_out'], "nt", F32, "mm_out_dx", tm=768)
    dw['w_out'] = _mm(merged, dmix, "tn", BF16, "mm_out_dw", tm=1024, tk=768)
    dya, dyb, dgates = _merge_bwd(dmerged, ya, yb, parts, tr)
    doan = _mm(dya, full['w_pa'], "nt", F32, "mm_pa_dx", tm=768)
    dw['w_pa'] = _mm(oan, dya, "tn", BF16, "mm_pa_dw", tm=1024, tk=768)
    dobn = _mm(dyb, full['w_pb'], "nt", F32, "mm_pb_dx", tm=768)
    dw['w_pb'] = _mm(obn, dyb, "tn", BF16, "mm_pb_dw", tm=1024, tk=768)
    do, dr, acc_gla = _gla_out_bwd(doan, o2, parts, gw4, tr)
    dq, dk, dv, dpre = _gla_bwd(do, parts, u2_hi, u2_lo, gbias, gla_hist, *n_gla)
    dy, dz, dxs_skip, acc_ssd = _ssd_out_bwd(dobn, y2, xbc, parts, dskip_exp, ssm_norm_w, tr)
    dx_scan, db_scan, dc_scan, dsm, acc_alog = _ssd_bwd(dy, xbc, parts, kc, ssd_hist, *n_ssd)
    dxbc = _xbc_assemble(dx_scan, db_scan, dc_scan, dxs_skip, tr)
    du_conv, acc_conv = _conv_bwd(dxbc, parts, conv_w_full, conv_b, L // Lc, Lc)
    dqkv = _gla_assemble(dq, dk, dv, tr)
    dsmall, dup_gla, acc_gbias, acc_dtb = _small_assemble(dpre, dsm, parts, ut_hi, ut_lo, tr)
    dparts = jnp.concatenate([dz, du_conv, dqkv, dr, dgates, dsmall], axis=1)
    dh1 = _mm(dparts, wp, "nt", F32, "mm_in_dx", tm=768, tk=1152)
    dw['w_in'] = _unpermute_in(_mm(h1, dparts, "tn", BF16, "mm_in_dw", tm=1024, tn=1152, tk=768))
    dxall, acc_n1 = _norm1_bwd(dh1, xall, dh2, norm1_w, mod, n_lat_tiles, tr)

    partial = dict(
        norm1_w=acc_n1[0, 2] + acc_n1[1, 2],
        gla_up_f=dup_gla[0, SM_LRF:SM_LRF + GLA_RANK], gla_bias_f=acc_gbias[0],
        gla_up_b=dup_gla[1, SM_LRB:SM_LRB + GLA_RANK], gla_bias_b=acc_gbias[1],
        gla_norm_w=acc_gla[0].reshape(GLA_H, GLA_DV).sum(0),
        conv_w=acc_conv[:SSM_CONV], conv_b=acc_conv[SSM_CONV],
        dt_bias_f=acc_dtb[0, SM_DTF:SM_DTF + SSM_HEADS], dt_bias_b=acc_dtb[0, SM_DTB:SM_DTB + SSM_HEADS],
        a_log_f=acc_alog[0, :, 0, :SSM_HPG], a_log_b=acc_alog[1, :, 0, :SSM_HPG],
        d_skip=acc_ssd[1].reshape(SSM_HEADS, SSM_P).sum(1), ssm_norm_w=acc_ssd[0],
        norm2_w=acc_ffn[2], final_norm_w=acc_loss[0],
    )
    dada = jnp.concatenate([acc_n1[0, 1], acc_n1[0, 0], acc_ffn[3], acc_ffn[1], acc_ffn[0], acc_loss[1]])
    dada_ctx = jnp.concatenate([acc_n1[1, 1], acc_n1[1, 0], jnp.zeros((4 * D,), F32)])
    names = list(partial)
    payload = [partial[n] for n in names] + [dada + dada_ctx, dada_ctx, acc_loss[2], dada]
    sizes = [int(np.prod(p.shape)) for p in payload]
    g8 = _allgather_small(_pack(payload), "gather_small_grads")
    summed = _unpack(_sum8(g8), [(s,) for s in sizes])
    grads = {n: s.reshape(W[n].shape if n not in SMALL_SHARDED else partial[n].shape) for n, s in zip(names, summed)}
    grads['b_ada'] = summed[len(names)].reshape(b_ada.shape)
    dada_ctx_sum = summed[len(names) + 1]
    loss = jnp.sum(summed[len(names) + 2])
    dada_all = g8.reshape(N_DEV, -1)[:, sum(sizes[:-1]):sum(sizes)]

    dada16 = jnp.zeros((ADA_ROWS, ada_cols), F32)
    dada16 = dada16.at[:N_DEV].set(_chip_cols(dada_all, chip, ada_cols)).at[N_DEV].set(_chip_cols(dada_ctx_sum[None], chip, ada_cols)[0])
    g_wada, dl_wada, m_wada, v_wada, p16 = _wada_bwd_adam(cvec, dada16, w_ada[0], m_w_ada[0], v_w_ada[0])
    p8 = _allgather_small(p16[N_DEV:], "gather_cctx")
    grads['c_ctx'] = _cctx_grad(p8[:, 0:1, :], c_ctx[None])[0]
    for n in SMALL_SHARDED:
        grads[n] = _chip_cols(grads[n], chip, W[n].shape[2])[None]

    small = [n for n in WEIGHTS if n not in BIG and n != 'w_ada']
    shapes = [W[n].shape for n in small]
    dl_s, m_s, v_s = _adam_small(*[_pack([d[n] for n in small]) for d in (W, grads, M, V)])
    delta = dict(zip(small, _unpack(dl_s, shapes)))
    new_m = dict(zip(small, _unpack(m_s, shapes)))
    new_v = dict(zip(small, _unpack(v_s, shapes)))
    grads['w_ada'], delta['w_ada'], new_m['w_ada'], new_v['w_ada'] = g_wada[None], dl_wada[None], m_wada[None], v_wada[None]

    exchanged = _exchange_grads([_full_to_blocks(dw[n], n) for n in BIG], "exchange_grads")
    for n, p8w in zip(BIG, exchanged):
        g, dl, m2, v2 = _reduce_adam(p8w, W[n][0], M[n][0], V[n][0], "adam_" + n)
        grads[n], delta[n], new_m[n], new_v[n] = g[None], dl[None], m2[None], v2[None]

    return (loss, dxall[:L][None], *[grads[n] for n in WEIGHTS], *[delta[n] for n in WEIGHTS],
            *[new_m[n] for n in WEIGHTS], *[new_v[n] for n in WEIGHTS])
```

```python
import functools

import numpy as np
import jax
import jax.numpy as jnp
from jax import lax
from jax.experimental import pallas as pl
from jax.experimental.pallas import tpu as pltpu

F32 = jnp.float32
BF16 = jnp.bfloat16
MESH = pl.DeviceIdType.MESH

D = 1024
EPS = 1e-6
GRID_W = 64
GLA_H, GLA_DK, GLA_DV, GLA_RANK, GLA_TAU = 4, 128, 256, 16, 16.0
GLA_C = 64
SSM_INNER, SSM_P, SSM_HEADS, SSM_G, SSM_HPG, SSM_N = 2048, 64, 32, 4, 8, 128
SSM_C = 128
SSM_CONV, CONV_LEFT = 4, 2
D_FF = 2816
IN_WIDTHS = (512, 512, 1024, 1024, 16, 16, 2048, 2048, 512, 512, 32, 32, 1024, 1024)
D_IN = sum(IN_WIDTHS)
PERM = (6, 7, 8, 9, 0, 1, 2, 3, 12, 13, 4, 5, 10, 11)
PW = 10368
SMALL_PAD = PW - D_IN
COL_Z, COL_XBC, COL_Q, COL_K, COL_V, COL_R, COL_GA, COL_GB, COL_SM = 0, 2048, 5120, 5632, 6144, 7168, 8192, 9216, 10240
SM_LRF, SM_LRB, SM_DTF, SM_DTB = 0, 16, 32, 64
EXP_CLAMP = 80.0
ADAM_LR, ADAM_B1, ADAM_B2, ADAM_EPS, ADAM_WD, ADAM_STEP = 0.001, 0.9, 0.999, 1e-08, 0.01, 10
N_CHIPS, N_DEV = 4, 8
VMEM_LIMIT = 56 * 1024 * 1024


def _cparams(sem=None):
    return pltpu.CompilerParams(dimension_semantics=sem, vmem_limit_bytes=VMEM_LIMIT)


def _dg(a, b, ca, cb):
    return lax.dot_general(a, b, (((ca,), (cb,)), ((), ())), preferred_element_type=F32)


def _nn(a, b):
    return _dg(a, b, 1, 0)


def _nt(a, b):
    return _dg(a, b, 1, 1)


def _tn(a, b):
    return _dg(a, b, 0, 0)


def _bf(x):
    return x.astype(BF16)


def _split(x):
    hi = x.astype(BF16)
    return hi, (x - hi.astype(F32)).astype(BF16)


def _nn_x(a, b_exact):
    hi, lo = _split(a)
    return _nn(hi, b_exact) + _nn(lo, b_exact)


def _x_nn(a_exact, b):
    hi, lo = _split(b)
    return _nn(a_exact, hi) + _nn(a_exact, lo)


def _nn3(a, b_hi, b_lo):
    hi, lo = _split(a)
    return _nn(hi, b_hi) + _nn(lo, b_hi) + _nn(hi, b_lo)


def _sigmoid(x):
    return 1.0 / (1.0 + jnp.exp(-x))


def _silu(x):
    return x * _sigmoid(x)


def _dsilu(x):
    s = _sigmoid(x)
    return s * (1.0 + x * (1.0 - s))


def _softplus(x):
    return jnp.maximum(x, 0.0) + jnp.log(1.0 + jnp.exp(-jnp.abs(x)))


def _log_sigmoid(x):
    return jnp.minimum(x, 0.0) - jnp.log(1.0 + jnp.exp(-jnp.abs(x)))


def _tile(n, target, mult=8):
    best = None
    for t in range(mult, min(n, target) + 1, mult):
        if n % t == 0:
            best = t
    assert best is not None, (n, target, mult)
    return best


def _mm(a, b, mode, out_dtype, name, tm=512, tn=1024, tk=2048):
    if mode == "nn":
        (M, K), N = a.shape, b.shape[1]
    elif mode == "nt":
        (M, K), N = a.shape, b.shape[0]
    else:
        (K, M), N = a.shape, b.shape[1]
    tm, tn, tk = _tile(M, tm, 128), _tile(N, tn, 128), _tile(K, tk, 128)
    nk = K // tk
    ca, cb = {"nn": (1, 0), "nt": (1, 1), "tn": (0, 0)}[mode]

    def body(a_ref, b_ref, o_ref, *acc):
        part = _dg(a_ref[...], b_ref[...], ca, cb)
        if nk == 1:
            o_ref[...] = part.astype(out_dtype)
        else:
            k = pl.program_id(2)

            @pl.when(k == 0)
            def _():
                acc[0][...] = part

            @pl.when(k > 0)
            def _():
                acc[0][...] += part

            @pl.when(k == nk - 1)
            def _():
                o_ref[...] = acc[0][...].astype(out_dtype)

    a_spec = pl.BlockSpec((tk, tm), lambda i, j, k: (k, i)) if mode == "tn" else pl.BlockSpec((tm, tk), lambda i, j, k: (i, k))
    b_spec = pl.BlockSpec((tn, tk), lambda i, j, k: (j, k)) if mode == "nt" else pl.BlockSpec((tk, tn), lambda i, j, k: (k, j))
    return pl.pallas_call(
        body, name=name, grid=(M // tm, N // tn, nk), in_specs=[a_spec, b_spec],
        out_specs=pl.BlockSpec((tm, tn), lambda i, j, k: (i, j)),
        out_shape=jax.ShapeDtypeStruct((M, N), out_dtype),
        scratch_shapes=[pltpu.VMEM((tm, tn), F32)] if nk > 1 else [],
        compiler_params=_cparams(("arbitrary", "arbitrary", "arbitrary")),
    )(a, b)


def _place():
    return lax.axis_index("x"), lax.axis_index("y"), lax.axis_index("c")


def _flip(v, bit):
    return 1 - v if bit else v


def _allgather_small(v, name):
    R, C = v.shape

    def body(v_ref, out_ref, send_sems, recv_sems, local_sem):
        x, y, c = _place()
        me = 4 * x + 2 * y + c
        mine = pltpu.make_async_copy(v_ref, out_ref.at[me], local_sem)
        mine.start()

        def peer(r):
            return _flip(x, (r >> 2) & 1), _flip(y, (r >> 1) & 1), _flip(c, r & 1)

        sends = [pltpu.make_async_remote_copy(
            src_ref=v_ref, dst_ref=out_ref.at[me], send_sem=send_sems.at[r - 1], recv_sem=recv_sems.at[r - 1],
            device_id=peer(r), device_id_type=MESH) for r in range(1, N_DEV)]
        for cp in sends:
            cp.start()
        for r in range(1, N_DEV):
            px, py, pc = peer(r)
            pltpu.make_async_remote_copy(
                src_ref=v_ref, dst_ref=out_ref.at[4 * px + 2 * py + pc], send_sem=send_sems.at[r - 1],
                recv_sem=recv_sems.at[r - 1], device_id=(x, y, c), device_id_type=MESH).wait_recv()
        for cp in sends:
            cp.wait_send()
        mine.wait()

    return pl.pallas_call(
        body, name=name, out_shape=jax.ShapeDtypeStruct((N_DEV, R, C), v.dtype),
        in_specs=[pl.BlockSpec(memory_space=pltpu.VMEM)], out_specs=pl.BlockSpec(memory_space=pltpu.VMEM),
        scratch_shapes=[pltpu.SemaphoreType.DMA((N_DEV - 1,)), pltpu.SemaphoreType.DMA((N_DEV - 1,)), pltpu.SemaphoreType.DMA],
        compiler_params=pltpu.CompilerParams(vmem_limit_bytes=VMEM_LIMIT),
    )(v)


_CHIP_RELATIONS = ((1, 0), (0, 1), (1, 1))


def _gather_weights(shards, name):
    n = len(shards)

    def body(*refs):
        ins, outs = refs[:n], refs[n:2 * n]
        send_sems, recv_sems, local_sems = refs[2 * n:]
        x, y, c = _place()
        chip = 2 * x + y
        local = [pltpu.make_async_copy(ins[i], outs[i].at[chip], local_sems.at[i]) for i in range(n)]
        for cp in local:
            cp.start()
        sends = []
        for i in range(n):
            for j, (fx, fy) in enumerate(_CHIP_RELATIONS):
                sends.append(pltpu.make_async_remote_copy(
                    src_ref=ins[i], dst_ref=outs[i].at[chip], send_sem=send_sems.at[i, j], recv_sem=recv_sems.at[i, j],
                    device_id=(_flip(x, fx), _flip(y, fy), c), device_id_type=MESH))
        for cp in sends:
            cp.start()
        for i in range(n):
            for j, (fx, fy) in enumerate(_CHIP_RELATIONS):
                src = 2 * _flip(x, fx) + _flip(y, fy)
                pltpu.make_async_remote_copy(
                    src_ref=ins[i], dst_ref=outs[i].at[src], send_sem=send_sems.at[i, j], recv_sem=recv_sems.at[i, j],
                    device_id=(x, y, c), device_id_type=MESH).wait_recv()
        for cp in sends:
            cp.wait_send()
        for cp in local:
            cp.wait()

    any_spec = pl.BlockSpec(memory_space=pl.ANY)
    return pl.pallas_call(
        body, name=name, out_shape=[jax.ShapeDtypeStruct((N_CHIPS,) + s.shape, s.dtype) for s in shards],
        in_specs=[any_spec] * n, out_specs=[any_spec] * n,
        scratch_shapes=[pltpu.SemaphoreType.DMA((n, 3)), pltpu.SemaphoreType.DMA((n, 3)), pltpu.SemaphoreType.DMA((n,))],
    )(*shards)


def _exchange_grads(blocks, name):
    n = len(blocks)

    def body(*refs):
        ins, outs = refs[:n], refs[n:2 * n]
        send_sems, recv_sems, local_sems = refs[2 * n:]
        x, y, c = _place()
        chip = 2 * x + y
        sibling = (x, y, 1 - c)
        local = [pltpu.make_async_copy(ins[i].at[chip], outs[i].at[chip], local_sems.at[i]) for i in range(n)]
        for cp in local:
            cp.start()
        first = []
        for i in range(n):
            for j, (fx, fy) in enumerate(_CHIP_RELATIONS):
                px, py = _flip(x, fx), _flip(y, fy)
                first.append(pltpu.make_async_remote_copy(
                    src_ref=ins[i].at[2 * px + py], dst_ref=outs[i].at[chip], send_sem=send_sems.at[i, j],
                    recv_sem=recv_sems.at[i, j], device_id=(px, py, c), device_id_type=MESH))
            first.append(pltpu.make_async_remote_copy(
                src_ref=ins[i].at[chip], dst_ref=outs[i].at[N_CHIPS + chip], send_sem=send_sems.at[i, 3],
                recv_sem=recv_sems.at[i, 3], device_id=sibling, device_id_type=MESH))
        for cp in first:
            cp.start()
        passed = []
        for i in range(n):
            for j, (fx, fy) in enumerate(_CHIP_RELATIONS):
                src = 2 * _flip(x, fx) + _flip(y, fy)
                pltpu.make_async_remote_copy(
                    src_ref=ins[i].at[src], dst_ref=outs[i].at[src], send_sem=send_sems.at[i, j], recv_sem=recv_sems.at[i, j],
                    device_id=(x, y, c), device_id_type=MESH).wait_recv()
                fwd = pltpu.make_async_remote_copy(
                    src_ref=outs[i].at[src], dst_ref=outs[i].at[N_CHIPS + src], send_sem=send_sems.at[i, 4 + j],
                    recv_sem=recv_sems.at[i, 4 + j], device_id=sibling, device_id_type=MESH)
                fwd.start()
                passed.append(fwd)
        for i in range(n):
            for j in (3, 4, 5, 6):
                pltpu.make_async_remote_copy(
                    src_ref=ins[i].at[0], dst_ref=outs[i].at[0], send_sem=send_sems.at[i, j], recv_sem=recv_sems.at[i, j],
                    device_id=(x, y, c), device_id_type=MESH).wait_recv()
        for cp in first + passed:
            cp.wait_send()
        for cp in local:
            cp.wait()

    any_spec = pl.BlockSpec(memory_space=pl.ANY)
    return pl.pallas_call(
        body, name=name, out_shape=[jax.ShapeDtypeStruct((N_DEV,) + b.shape[1:], b.dtype) for b in blocks],
        in_specs=[any_spec] * n, out_specs=[any_spec] * n,
        scratch_shapes=[pltpu.SemaphoreType.DMA((n, 7)), pltpu.SemaphoreType.DMA((n, 7)), pltpu.SemaphoreType.DMA((n,))],
    )(*blocks)


def _row_spec(tr, w, col=0):
    return pl.BlockSpec((tr, w), lambda i: (i, col))


def _vec_spec(w):
    return pl.BlockSpec((1, w), lambda i: (0, 0))


def _acc_spec(w):
    return pl.BlockSpec((8, w), lambda i: (0, 0))


def _rms(x):
    return lax.rsqrt(jnp.mean(x * x, axis=-1, keepdims=True) + EPS)


def _rms_bwd(dn, n, rstd):
    return rstd * (dn - n * jnp.mean(dn * n, axis=-1, keepdims=True))


def _colsum(x):
    return jnp.sum(x, axis=0, keepdims=True)


def _zero_first(ref):
    @pl.when(pl.program_id(0) == 0)
    def _():
        ref[...] = jnp.zeros_like(ref)


def _norm_mod(xall, w, mod, n_lat_tiles, tr):
    R = xall.shape[0]

    def body(x_ref, w_ref, mod_ref, o_ref):
        x = x_ref[...]
        n = x * _rms(x) * w_ref[...]
        o_ref[...] = (n * (1.0 + mod_ref[0, 1]) + mod_ref[0, 0]).astype(BF16)

    return pl.pallas_call(
        body, name="norm1_mod", grid=(R // tr,),
        in_specs=[_row_spec(tr, D), _vec_spec(D),
                  pl.BlockSpec((1, 2, 1, D), lambda i: (jnp.where(i >= n_lat_tiles, 1, 0), 0, 0, 0))],
        out_specs=_row_spec(tr, D), out_shape=jax.ShapeDtypeStruct((R, D), BF16),
        compiler_params=_cparams(("arbitrary",)),
    )(xall, w, mod)


def _resid_norm_mod(xall, mix, g1, w2, sh2, sc2, tr):
    R = xall.shape[0]

    def body(x_ref, mix_ref, g1_ref, w_ref, sh_ref, sc_ref, h2_ref, u_ref):
        h2 = x_ref[...] + g1_ref[...] * mix_ref[...]
        h2_ref[...] = h2
        n = h2 * _rms(h2) * w_ref[...]
        u_ref[...] = (n * (1.0 + sc_ref[...]) + sh_ref[...]).astype(BF16)

    return pl.pallas_call(
        body, name="resid_norm2_mod", grid=(R // tr,),
        in_specs=[_row_spec(tr, D), _row_spec(tr, D)] + [_vec_spec(D)] * 4,
        out_specs=[_row_spec(tr, D), _row_spec(tr, D)],
        out_shape=[jax.ShapeDtypeStruct((R, D), F32), jax.ShapeDtypeStruct((R, D), BF16)],
        compiler_params=_cparams(("arbitrary",)),
    )(xall, mix, g1, w2, sh2, sc2)


def _loss_head(h2, f, target, g2, fw, n_lat_tiles, tr):
    R = h2.shape[0]

    def body(h2_ref, f_ref, t_ref, g2_ref, fw_ref, dh3_ref, df_ref, acc_ref):
        _zero_first(acc_ref)
        lat = pl.program_id(0) < n_lat_tiles
        fv = f_ref[...]
        h3 = h2_ref[...] + g2_ref[...] * fv
        rstd = _rms(h3)
        n = h3 * rstd
        err = n * fw_ref[...] - t_ref[...]
        dy = err * (1.0 / D)
        dh3 = jnp.where(lat, _rms_bwd(dy * fw_ref[...], n, rstd), 0.0)
        dh3_ref[...] = dh3
        df_ref[...] = (g2_ref[...] * dh3).astype(BF16)
        acc_ref[0:1, :] += jnp.where(lat, _colsum(dy * n), 0.0)
        acc_ref[1:2, :] += _colsum(dh3 * fv)
        acc_ref[2:3, :] += jnp.where(lat, _colsum(err * err) * (0.5 / D), 0.0)

    return pl.pallas_call(
        body, name="loss_head", grid=(R // tr,),
        in_specs=[_row_spec(tr, D), _row_spec(tr, D),
                  pl.BlockSpec((tr, D), lambda i: (jnp.minimum(i, n_lat_tiles - 1), 0)), _vec_spec(D), _vec_spec(D)],
        out_specs=[_row_spec(tr, D), _row_spec(tr, D), _acc_spec(D)],
        out_shape=[jax.ShapeDtypeStruct((R, D), F32), jax.ShapeDtypeStruct((R, D), BF16), jax.ShapeDtypeStruct((8, D), F32)],
        compiler_params=_cparams(("arbitrary",)),
    )(h2, f, target, g2, fw)


def _ffn_in_bwd(du_a, du_b, h2, dh3, mix, sc2, g1, w2, tr):
    R = h2.shape[0]

    def body(dua_ref, dub_ref, h2_ref, dh3_ref, mix_ref, sc_ref, g1_ref, w_ref, dh2_ref, dmix_ref, acc_ref):
        _zero_first(acc_ref)
        du = dua_ref[...] + dub_ref[...]
        h2 = h2_ref[...]
        rstd = _rms(h2)
        n = h2 * rstd
        dnw = du * (1.0 + sc_ref[...])
        dh2 = dh3_ref[...] + _rms_bwd(dnw * w_ref[...], n, rstd)
        dh2_ref[...] = dh2
        dmix_ref[...] = (g1_ref[...] * dh2).astype(BF16)
        acc_ref[0:1, :] += _colsum(du * n * w_ref[...])
        acc_ref[1:2, :] += _colsum(du)
        acc_ref[2:3, :] += _colsum(dnw * n)
        acc_ref[3:4, :] += _colsum(dh2 * mix_ref[...])

    return pl.pallas_call(
        body, name="ffn_in_bwd", grid=(R // tr,),
        in_specs=[_row_spec(tr, D)] * 5 + [_vec_spec(D)] * 3,
        out_specs=[_row_spec(tr, D), _row_spec(tr, D), _acc_spec(D)],
        out_shape=[jax.ShapeDtypeStruct((R, D), F32), jax.ShapeDtypeStruct((R, D), BF16), jax.ShapeDtypeStruct((8, D), F32)],
        compiler_params=_cparams(("arbitrary",)),
    )(du_a, du_b, h2, dh3, mix, sc2, g1, w2)


def _norm1_bwd(dh1, xall, dh2, w1, mod, n_lat_tiles, tr):
    R = xall.shape[0]

    def body(dh1_ref, x_ref, dh2_ref, w_ref, mod_ref, dx_ref, acc_ref):
        i = pl.program_id(0)

        @pl.when((i == 0) | (i == n_lat_tiles))
        def _():
            acc_ref[...] = jnp.zeros_like(acc_ref)

        dh1 = dh1_ref[...]
        x = x_ref[...]
        rstd = _rms(x)
        n = x * rstd
        dnw = dh1 * (1.0 + mod_ref[0, 1])
        dx_ref[...] = dh2_ref[...] + _rms_bwd(dnw * w_ref[...], n, rstd)
        acc_ref[0, 0:1, :] += _colsum(dh1 * n * w_ref[...])
        acc_ref[0, 1:2, :] += _colsum(dh1)
        acc_ref[0, 2:3, :] += _colsum(dnw * n)

    sel = lambda i: jnp.where(i >= n_lat_tiles, 1, 0)
    return pl.pallas_call(
        body, name="norm1_bwd", grid=(R // tr,),
        in_specs=[_row_spec(tr, D)] * 3 + [_vec_spec(D), pl.BlockSpec((1, 2, 1, D), lambda i: (sel(i), 0, 0, 0))],
        out_specs=[_row_spec(tr, D), pl.BlockSpec((1, 8, D), lambda i: (sel(i), 0, 0))],
        out_shape=[jax.ShapeDtypeStruct((R, D), F32), jax.ShapeDtypeStruct((2, 8, D), F32)],
        compiler_params=_cparams(("arbitrary",)),
    )(dh1, xall, dh2, w1, mod)


def _swiglu_act(gp, up, tr):
    R = gp.shape[0]

    def body(g_ref, u_ref, o_ref):
        o_ref[...] = (_silu(g_ref[...]) * u_ref[...]).astype(BF16)

    return pl.pallas_call(
        body, name="swiglu_act", grid=(R // tr,), in_specs=[_row_spec(tr, D_FF)] * 2, out_specs=_row_spec(tr, D_FF),
        out_shape=jax.ShapeDtypeStruct((R, D_FF), BF16), compiler_params=_cparams(("arbitrary",)),
    )(gp, up)


def _swiglu_act_bwd(da, gp, up, tr):
    R = gp.shape[0]

    def body(da_ref, g_ref, u_ref, dg_ref, du_ref):
        da, g = da_ref[...], g_ref[...]
        dg_ref[...] = (da * u_ref[...] * _dsilu(g)).astype(BF16)
        du_ref[...] = (da * _silu(g)).astype(BF16)

    return pl.pallas_call(
        body, name="swiglu_act_bwd", grid=(R // tr,), in_specs=[_row_spec(tr, D_FF)] * 3, out_specs=[_row_spec(tr, D_FF)] * 2,
        out_shape=[jax.ShapeDtypeStruct((R, D_FF), BF16)] * 2, compiler_params=_cparams(("arbitrary",)),
    )(da, gp, up)


def _merge(ya, yb, parts, tr):
    R = ya.shape[0]

    def body(ya_ref, yb_ref, ga_ref, gb_ref, o_ref):
        o_ref[...] = (_sigmoid(ga_ref[...]) * ya_ref[...] + _sigmoid(gb_ref[...]) * yb_ref[...]).astype(BF16)

    return pl.pallas_call(
        body, name="merge", grid=(R // tr,),
        in_specs=[_row_spec(tr, D), _row_spec(tr, D), _row_spec(tr, D, COL_GA // D), _row_spec(tr, D, COL_GB // D)],
        out_specs=_row_spec(tr, D), out_shape=jax.ShapeDtypeStruct((R, D), BF16), compiler_params=_cparams(("arbitrary",)),
    )(ya, yb, parts, parts)


def _merge_bwd(dm, ya, yb, parts, tr):
    R = ya.shape[0]

    def body(dm_ref, ya_ref, yb_ref, ga_ref, gb_ref, dya_ref, dyb_ref, dg_ref):
        dm = dm_ref[...]
        sa, sb = _sigmoid(ga_ref[...]), _sigmoid(gb_ref[...])
        dya_ref[...] = (dm * sa).astype(BF16)
        dyb_ref[...] = (dm * sb).astype(BF16)
        dg_ref[:, 0:D] = (dm * ya_ref[...] * sa * (1.0 - sa)).astype(BF16)
        dg_ref[:, D:2 * D] = (dm * yb_ref[...] * sb * (1.0 - sb)).astype(BF16)

    return pl.pallas_call(
        body, name="merge_bwd", grid=(R // tr,),
        in_specs=[_row_spec(tr, D)] * 3 + [_row_spec(tr, D, COL_GA // D), _row_spec(tr, D, COL_GB // D)],
        out_specs=[_row_spec(tr, D), _row_spec(tr, D), _row_spec(tr, 2 * D)],
        out_shape=[jax.ShapeDtypeStruct((R, D), BF16)] * 2 + [jax.ShapeDtypeStruct((R, 2 * D), BF16)],
        compiler_params=_cparams(("arbitrary",)),
    )(dm, ya, yb, parts, parts)


def _gla_out(o2, parts, gw4, tr):
    R = parts.shape[0]

    def body(of_ref, ob_ref, r_ref, w_ref, out_ref):
        oa = of_ref[...] + ob_ref[...]
        sr = _silu(r_ref[...])
        for h in range(GLA_H):
            s = slice(h * GLA_DV, (h + 1) * GLA_DV)
            o = oa[:, s]
            out_ref[:, s] = (o * _rms(o) * w_ref[:, s] * sr[:, s]).astype(BF16)

    return pl.pallas_call(
        body, name="gla_out", grid=(R // tr,),
        in_specs=[_row_spec(tr, D), _row_spec(tr, D), _row_spec(tr, D, COL_R // D), _vec_spec(D)],
        out_specs=_row_spec(tr, D), out_shape=jax.ShapeDtypeStruct((R, D), BF16), compiler_params=_cparams(("arbitrary",)),
    )(o2[0], o2[1], parts, gw4)


def _gla_out_bwd(dout, o2, parts, gw4, tr):
    R = parts.shape[0]

    def body(d_ref, of_ref, ob_ref, r_ref, w_ref, do_ref, dr_ref, acc_ref):
        _zero_first(acc_ref)
        oa = of_ref[...] + ob_ref[...]
        r = r_ref[...]
        sr = _silu(r)
        dout = d_ref[...]
        for h in range(GLA_H):
            s = slice(h * GLA_DV, (h + 1) * GLA_DV)
            o = oa[:, s]
            rstd = _rms(o)
            n = o * rstd
            w = w_ref[:, s]
            dr_ref[:, s] = (dout[:, s] * n * w * _dsilu(r[:, s])).astype(BF16)
            dnw = dout[:, s] * sr[:, s]
            do_ref[:, s] = _rms_bwd(dnw * w, n, rstd)
            acc_ref[0:1, s] += _colsum(dnw * n)

    return pl.pallas_call(
        body, name="gla_out_bwd", grid=(R // tr,),
        in_specs=[_row_spec(tr, D), _row_spec(tr, D), _row_spec(tr, D), _row_spec(tr, D, COL_R // D), _vec_spec(D)],
        out_specs=[_row_spec(tr, D), _row_spec(tr, D), _acc_spec(D)],
        out_shape=[jax.ShapeDtypeStruct((R, D), F32), jax.ShapeDtypeStruct((R, D), BF16), jax.ShapeDtypeStruct((8, D), F32)],
        compiler_params=_cparams(("arbitrary",)),
    )(dout, o2[0], o2[1], parts, gw4)


SSM_GW = SSM_INNER // SSM_G


def _ssd_out(y2, xbc, parts, dskip, nw, tr):
    R = parts.shape[0]

    def body(yf_ref, yb_ref, x_ref, z_ref, ds_ref, w_ref, out_ref):
        ob = (yf_ref[...] + yb_ref[...] + ds_ref[...] * x_ref[...]) * _silu(z_ref[...])
        for g in range(SSM_G):
            s = slice(g * SSM_GW, (g + 1) * SSM_GW)
            o = ob[:, s]
            out_ref[:, s] = (o * _rms(o) * w_ref[:, s]).astype(BF16)

    return pl.pallas_call(
        body, name="ssd_out", grid=(R // tr,),
        in_specs=[_row_spec(tr, SSM_INNER)] * 3 + [_row_spec(tr, SSM_INNER, COL_Z // SSM_INNER),
                                                   _vec_spec(SSM_INNER), _vec_spec(SSM_INNER)],
        out_specs=_row_spec(tr, SSM_INNER), out_shape=jax.ShapeDtypeStruct((R, SSM_INNER), BF16),
        compiler_params=_cparams(("arbitrary",)),
    )(y2[0], y2[1], xbc, parts, dskip, nw)


def _ssd_out_bwd(dout, y2, xbc, parts, dskip, nw, tr):
    R = parts.shape[0]

    def body(d_ref, yf_ref, yb_ref, x_ref, z_ref, ds_ref, w_ref, dy_ref, dz_ref, dxs_ref, acc_ref):
        _zero_first(acc_ref)
        x, z = x_ref[...], z_ref[...]
        pre = yf_ref[...] + yb_ref[...] + ds_ref[...] * x
        sz = _silu(z)
        ob = pre * sz
        dout = d_ref[...]
        for g in range(SSM_G):
            s = slice(g * SSM_GW, (g + 1) * SSM_GW)
            o = ob[:, s]
            rstd = _rms(o)
            n = o * rstd
            dob = _rms_bwd(dout[:, s] * w_ref[:, s], n, rstd)
            dz_ref[:, s] = (dob * pre[:, s] * _dsilu(z[:, s])).astype(BF16)
            dy = dob * sz[:, s]
            dy_ref[:, s] = dy
            dxs_ref[:, s] = dy * ds_ref[:, s]
            acc_ref[0:1, s] += _colsum(dout[:, s] * n)
            acc_ref[1:2, s] += _colsum(dy * x[:, s])

    return pl.pallas_call(
        body, name="ssd_out_bwd", grid=(R // tr,),
        in_specs=[_row_spec(tr, SSM_INNER)] * 4 + [_row_spec(tr, SSM_INNER, COL_Z // SSM_INNER),
                                                   _vec_spec(SSM_INNER), _vec_spec(SSM_INNER)],
        out_specs=[_row_spec(tr, SSM_INNER), _row_spec(tr, SSM_INNER), _row_spec(tr, SSM_INNER), _acc_spec(SSM_INNER)],
        out_shape=[jax.ShapeDtypeStruct((R, SSM_INNER), F32), jax.ShapeDtypeStruct((R, SSM_INNER), BF16),
                   jax.ShapeDtypeStruct((R, SSM_INNER), F32), jax.ShapeDtypeStruct((8, SSM_INNER), F32)],
        compiler_params=_cparams(("arbitrary",)),
    )(dout, y2[0], y2[1], xbc, parts, dskip, nw)


CONV_W = SSM_INNER + 2 * SSM_G * SSM_N
CONV_BLK = 1024


def _conv_masks(tr, is_ctx):
    t = lax.broadcasted_iota(jnp.int32, (tr, 1), 0)
    pos = jnp.where(is_ctx, t, t & (GRID_W - 1))
    seg = jnp.where(is_ctx, tr, GRID_W)
    return pos, seg


def _shift_rows(u, s, tr):
    return u if s == 0 else pltpu.roll(u, (-s) % tr, 0)


def _conv_fwd(parts, cw, cb, n_lat_tiles, tr):
    R = parts.shape[0]

    def body(u_ref, w_ref, b_ref, o_ref):
        pos, seg = _conv_masks(tr, pl.program_id(0) >= n_lat_tiles)
        u = u_ref[...]
        acc = jnp.zeros_like(u) + b_ref[...]
        for j in range(SSM_CONV):
            s = j - CONV_LEFT
            ok = (pos + s >= 0) & (pos + s < seg)
            acc = acc + jnp.where(ok, _shift_rows(u, s, tr), 0.0) * w_ref[j:j + 1, :]
        o_ref[...] = _silu(acc)

    return pl.pallas_call(
        body, name="conv_fwd", grid=(R // tr, CONV_W // CONV_BLK),
        in_specs=[pl.BlockSpec((tr, CONV_BLK), lambda i, j: (i, COL_XBC // CONV_BLK + j)),
                  pl.BlockSpec((SSM_CONV, CONV_BLK), lambda i, j: (0, j)), pl.BlockSpec((1, CONV_BLK), lambda i, j: (0, j))],
        out_specs=pl.BlockSpec((tr, CONV_BLK), lambda i, j: (i, j)), out_shape=jax.ShapeDtypeStruct((R, CONV_W), F32),
        compiler_params=_cparams(("arbitrary", "arbitrary")),
    )(parts, cw, cb)


def _conv_bwd(dxbc, parts, cw, cb, n_lat_tiles, tr):
    R = parts.shape[0]

    def body(d_ref, u_ref, w_ref, b_ref, du_ref, acc_ref):
        @pl.when(pl.program_id(1) == 0)
        def _():
            acc_ref[...] = jnp.zeros_like(acc_ref)

        pos, seg = _conv_masks(tr, pl.program_id(1) >= n_lat_tiles)
        u = u_ref[...]
        pre = jnp.zeros_like(u) + b_ref[...]
        taps = []
        for j in range(SSM_CONV):
            s = j - CONV_LEFT
            ok = (pos + s >= 0) & (pos + s < seg)
            tap = jnp.where(ok, _shift_rows(u, s, tr), 0.0)
            taps.append(tap)
            pre = pre + tap * w_ref[j:j + 1, :]
        dpre = d_ref[...] * _dsilu(pre)
        du = jnp.zeros_like(u)
        for j in range(SSM_CONV):
            s = j - CONV_LEFT
            acc_ref[j:j + 1, :] += _colsum(dpre * taps[j])
            ok = (pos - s >= 0) & (pos - s < seg)
            du = du + jnp.where(ok, _shift_rows(dpre, -s, tr), 0.0) * w_ref[j:j + 1, :]
        acc_ref[SSM_CONV:SSM_CONV + 1, :] += _colsum(dpre)
        du_ref[...] = du.astype(BF16)

    return pl.pallas_call(
        body, name="conv_bwd", grid=(CONV_W // CONV_BLK, R // tr),
        in_specs=[pl.BlockSpec((tr, CONV_BLK), lambda j, i: (i, j)),
                  pl.BlockSpec((tr, CONV_BLK), lambda j, i: (i, COL_XBC // CONV_BLK + j)),
                  pl.BlockSpec((SSM_CONV, CONV_BLK), lambda j, i: (0, j)), pl.BlockSpec((1, CONV_BLK), lambda j, i: (0, j))],
        out_specs=[pl.BlockSpec((tr, CONV_BLK), lambda j, i: (i, j)), pl.BlockSpec((8, CONV_BLK), lambda j, i: (0, j))],
        out_shape=[jax.ShapeDtypeStruct((R, CONV_W), BF16), jax.ShapeDtypeStruct((8, CONV_W), F32)],
        compiler_params=_cparams(("arbitrary", "arbitrary")),
    )(dxbc, parts, cw, cb)


def _chunk_row_block(d, i, n_lat, n_ctx):
    fwd = jnp.where(i < n_ctx, n_lat + i, i - n_ctx)
    rev = n_lat + n_ctx - 1 - i
    if isinstance(d, int):
        return rev if d else fwd
    return jnp.where(d == 0, fwd, rev)


def _tri(n, d, transpose=False):
    row = lax.broadcasted_iota(jnp.int32, (n, n), 0)
    col = lax.broadcasted_iota(jnp.int32, (n, n), 1)
    diff = (col - row) if transpose else (row - col)
    return diff * (1 - 2 * d) >= 0


def _gla_gates(sm, uhi, ulo, bias, d):
    pre = _nn3(sm, uhi, ulo) + bias
    g = _log_sigmoid(pre) * (1.0 / GLA_TAU)
    mask = _tri(GLA_C, d)
    b = _x_nn(mask.astype(BF16), g)
    b_tot = _colsum(g)
    b_ref = b[GLA_C // 2:GLA_C // 2 + 1, :]
    e_q = jnp.exp(jnp.minimum(b - b_ref, EXP_CLAMP))
    e_k = jnp.exp(jnp.minimum(b_ref - b, EXP_CLAMP))
    return pre, mask, b_tot, e_q, e_k, jnp.exp(b), jnp.exp(b_tot - b)


GLA_QK = GLA_H * GLA_DK
GLA_V = GLA_H * GLA_DV


def _gla_specs(n_lat, n_ctx, step_of):
    rbs = [lambda i, d=d: _chunk_row_block(d, step_of(i), n_lat, n_ctx) for d in range(2)]
    specs = []
    for rb in rbs:
        specs += [pl.BlockSpec((GLA_C, GLA_QK), lambda i, rb=rb: (rb(i), COL_Q // GLA_QK)),
                  pl.BlockSpec((GLA_C, GLA_QK), lambda i, rb=rb: (rb(i), COL_K // GLA_QK)),
                  pl.BlockSpec((GLA_C, GLA_V), lambda i, rb=rb: (rb(i), COL_V // GLA_V)),
                  pl.BlockSpec((GLA_C, 128), lambda i, rb=rb: (rb(i), COL_SM // 128))]
    specs += [pl.BlockSpec((2, 128, GLA_QK), lambda i: (0, 0, 0)), pl.BlockSpec((2, 128, GLA_QK), lambda i: (0, 0, 0)),
              pl.BlockSpec((2, 1, GLA_QK), lambda i: (0, 0, 0))]
    return specs, rbs


def _gla_fwd(parts, uhi, ulo, bias, n_lat, n_ctx):
    R = parts.shape[0]
    n_steps = n_lat + n_ctx
    scale = GLA_DK ** -0.5

    def body(*refs):
        ins, (uhi_ref, ulo_ref, bias_ref), o_refs, hist_ref, st = refs[:8], refs[8:11], refs[11:13], refs[13], refs[14]

        @pl.when(pl.program_id(0) == 0)
        def _():
            st[...] = jnp.zeros_like(st)

        for d in range(2):
            q_ref, k_ref, v_ref, sm_ref = ins[4 * d:4 * d + 4]
            _, mask, b_tot, e_q, e_k, e_in, e_out = _gla_gates(sm_ref[...], uhi_ref[d], ulo_ref[d], bias_ref[d], d)
            q, k, v = q_ref[...] * scale, k_ref[...], _bf(v_ref[...])
            qb, kb, q_in, k_out, decay = _bf(q * e_q), _bf(k * e_k), _bf(q * e_in), _bf(k * e_out), jnp.exp(b_tot)
            for h in range(GLA_H):
                sk, sv = slice(h * GLA_DK, (h + 1) * GLA_DK), slice(h * GLA_DV, (h + 1) * GLA_DV)
                att = jnp.where(mask, _nt(qb[:, sk], kb[:, sk]), 0.0)
                s_in = st[d, h]
                hist_ref[d, 0, h] = s_in
                o_refs[d][:, sv] = _nn(_bf(att), v[:, sv]) + _nt(q_in[:, sk], _bf(s_in))
                st[d, h] = decay[:, sk] * s_in + _tn(v[:, sv], k_out[:, sk])

    in_specs, rbs = _gla_specs(n_lat, n_ctx, lambda i: i)
    return pl.pallas_call(
        body, name="gla_fwd", grid=(n_steps,), in_specs=in_specs,
        out_specs=[pl.BlockSpec((GLA_C, GLA_V), lambda i, rb=rb: (rb(i), 0)) for rb in rbs]
        + [pl.BlockSpec((2, 1, GLA_H, GLA_DV, GLA_DK), lambda i: (0, i, 0, 0, 0))],
        out_shape=[jax.ShapeDtypeStruct((R, GLA_V), F32)] * 2 + [jax.ShapeDtypeStruct((2, n_steps, GLA_H, GLA_DV, GLA_DK), F32)],
        scratch_shapes=[pltpu.VMEM((2, GLA_H, GLA_DV, GLA_DK), F32)],
        compiler_params=_cparams(("arbitrary",)),
    )(*([parts] * 8), uhi, ulo, bias)


def _gla_bwd(do, parts, uhi, ulo, bias, hist, n_lat, n_ctx):
    R = parts.shape[0]
    n_steps = n_lat + n_ctx
    scale = GLA_DK ** -0.5
    step_of = lambda j: n_steps - 1 - j

    def body(*refs):
        ins, (uhi_ref, ulo_ref, bias_ref), do_refs, hist_ref = refs[:8], refs[8:11], refs[11:13], refs[13]
        outs, dst = refs[14:22], refs[22]

        @pl.when(pl.program_id(0) == 0)
        def _():
            dst[...] = jnp.zeros_like(dst)

        for d in range(2):
            q_ref, k_ref, v_ref, sm_ref = ins[4 * d:4 * d + 4]
            dq_ref, dk_ref, dv_ref, dp_ref = outs[4 * d:4 * d + 4]
            pre, mask, b_tot, e_q, e_k, e_in, e_out = _gla_gates(sm_ref[...], uhi_ref[d], ulo_ref[d], bias_ref[d], d)
            q, k, v = q_ref[...] * scale, k_ref[...], _bf(v_ref[...])
            dout = _bf(do_refs[d][...])
            k_out_f = k * e_out
            qb, kb, q_in, k_out, decay = _bf(q * e_q), _bf(k * e_k), _bf(q * e_in), _bf(k_out_f), jnp.exp(b_tot)
            dqs, dks, dk_outs, dss = [], [], [], []
            for h in range(GLA_H):
                sk, sv = slice(h * GLA_DK, (h + 1) * GLA_DK), slice(h * GLA_DV, (h + 1) * GLA_DV)
                s_in, ds = hist_ref[d, 0, h], dst[d, h]
                att = jnp.where(mask, _nt(qb[:, sk], kb[:, sk]), 0.0)
                datt = _bf(jnp.where(mask, _nt(dout[:, sv], v[:, sv]), 0.0))
                dv_ref[:, sv] = _tn(_bf(att), dout[:, sv]) + _nt(k_out[:, sk], _bf(ds))
                dqs.append(_nn(datt, kb[:, sk]) * e_q[:, sk] + _nn(dout[:, sv], _bf(s_in)) * e_in[:, sk])
                dk_o = _nn(v[:, sv], _bf(ds))
                dk_outs.append(dk_o)
                dks.append(_tn(datt, qb[:, sk]) * e_k[:, sk])
                dss.append(_colsum(ds * s_in))
                dst[d, h] = decay[:, sk] * ds + _tn(dout[:, sv], q_in[:, sk])
            dq, dk_out = jnp.concatenate(dqs, axis=1), jnp.concatenate(dk_outs, axis=1)
            dk = jnp.concatenate(dks, axis=1) + dk_out * e_out
            dq_ref[...] = dq * scale
            dk_ref[...] = dk
            db_tot = _colsum(dk_out * k_out_f) + decay * jnp.concatenate(dss, axis=1)
            dg = _x_nn(_tri(GLA_C, d, transpose=True).astype(BF16), dq * q - dk * k) + db_tot
            dp_ref[...] = dg * (1.0 / GLA_TAU) * _sigmoid(-pre)

    in_specs, rbs = _gla_specs(n_lat, n_ctx, step_of)
    in_specs += [pl.BlockSpec((GLA_C, GLA_V), lambda j, rb=rb: (rb(j), 0)) for rb in rbs]
    in_specs += [pl.BlockSpec((2, 1, GLA_H, GLA_DV, GLA_DK), lambda j: (0, step_of(j), 0, 0, 0))]
    out_specs, out_shape = [], []
    for rb in rbs:
        for w in (GLA_QK, GLA_QK, GLA_V, GLA_QK):
            out_specs.append(pl.BlockSpec((GLA_C, w), lambda j, rb=rb: (rb(j), 0)))
            out_shape.append(jax.ShapeDtypeStruct((R, w), F32))
    outs = pl.pallas_call(
        body, name="gla_bwd", grid=(n_steps,), in_specs=in_specs, out_specs=out_specs, out_shape=out_shape,
        scratch_shapes=[pltpu.VMEM((2, GLA_H, GLA_DV, GLA_DK), F32)],
        compiler_params=_cparams(("arbitrary",)),
    )(*([parts] * 8), uhi, ulo, bias, do, do, hist)
    return [(outs[k], outs[4 + k]) for k in range(4)]


def _ssd_consts(dt_bias, a_log):
    sel = np.zeros((2, SSM_G, 128, 128), np.float32)
    for d, base in enumerate((SM_DTF, SM_DTB)):
        for g in range(SSM_G):
            for e in range(SSM_HPG):
                sel[d, g, base + SSM_HPG * g + e, e] = 1.0
    e512 = np.zeros((128, SSM_GW), np.float32)
    for e in range(SSM_HPG):
        e512[e, SSM_P * e:SSM_P * (e + 1)] = 1.0
    a_neg = -jnp.exp(a_log)
    pad = lambda v: jnp.pad(v.reshape(2, SSM_G, 1, SSM_HPG), ((0, 0), (0, 0), (0, 0), (0, 128 - SSM_HPG)))
    return dict(
        sel=jnp.asarray(sel, BF16), sel_t=jnp.asarray(sel.transpose(0, 1, 3, 2), BF16), e512_t=jnp.asarray(e512.T, BF16),
        dtb=pad(dt_bias), a=pad(a_neg), a512=jnp.repeat(a_neg, SSM_P, axis=1).reshape(2, SSM_G, 1, SSM_GW))


def _head_columns(x8):
    return [jnp.broadcast_to(x8[:, e:e + 1], (x8.shape[0], 128)) for e in range(SSM_HPG)]


def _head_layout(cols):
    low = lax.broadcasted_iota(jnp.int32, (1, 128), 1) < SSM_P
    return jnp.concatenate([jnp.where(low, cols[2 * j], cols[2 * j + 1]) for j in range(SSM_HPG // 2)], axis=1)


def _ssd_common(sm, sel, dtb, a_neg, a512, d):
    dtr8 = _nn_x(sm, sel) + dtb
    dt8 = _softplus(dtr8)
    a8 = a_neg * dt8
    mask = _tri(SSM_C, d)
    mask_t = _tri(SSM_C, d, transpose=True).astype(BF16)
    cum8 = _x_nn(mask.astype(BF16), a8)
    a_hi, a_lo = _split(a8)
    cum_t = _tn(a_hi, mask_t) + _tn(a_lo, mask_t)
    cum_cols = _head_columns(cum8)
    dt_exp = _head_layout(_head_columns(dt8))
    a_exp = a512 * dt_exp
    return dict(dtr8=dtr8, a8=a8, mask=mask, mask_t=mask_t, cum_t=cum_t, dt_exp=dt_exp, a_exp=a_exp,
                cum_exp=_head_layout(cum_cols), cum_cols=cum_cols, tot_exp=_colsum(a_exp))


def _ssd_decay(cm, e):
    diff = cm["cum_cols"][e] - cm["cum_t"][e:e + 1, :]
    return jnp.where(cm["mask"], jnp.exp(jnp.minimum(diff, 0.0)), 0.0)


SSM_GPS = 2


def _ssd_specs(n_lat, n_ctx, step_of):
    rbs = [lambda i, d=d: _chunk_row_block(d, step_of(i), n_lat, n_ctx) for d in range(2)]
    xw, nw = SSM_GPS * SSM_GW, SSM_GPS * SSM_N
    specs = []
    for rb in rbs:
        specs += [pl.BlockSpec((SSM_C, xw), lambda g, i, rb=rb: (rb(i), g)),
                  pl.BlockSpec((SSM_C, nw), lambda g, i, rb=rb: (rb(i), SSM_INNER // nw + g)),
                  pl.BlockSpec((SSM_C, nw), lambda g, i, rb=rb: (rb(i), (SSM_INNER + SSM_G * SSM_N) // nw + g)),
                  pl.BlockSpec((SSM_C, 128), lambda g, i, rb=rb: (rb(i), COL_SM // 128))]
    specs += [pl.BlockSpec((2, SSM_GPS, 128, 128), lambda g, i: (0, g, 0, 0)),
              pl.BlockSpec((2, SSM_GPS, 1, 128), lambda g, i: (0, g, 0, 0)),
              pl.BlockSpec((2, SSM_GPS, 1, 128), lambda g, i: (0, g, 0, 0)),
              pl.BlockSpec((2, SSM_GPS, 1, SSM_GW), lambda g, i: (0, g, 0, 0))]
    return specs, rbs


def _ssd_fwd(xbc, parts, k, n_lat, n_ctx):
    R = parts.shape[0]
    n_steps = n_lat + n_ctx

    def body(*refs):
        ins, (sel_ref, dtb_ref, a_ref, a512_ref), y_refs, hist_ref, st = refs[:8], refs[8:12], refs[12:14], refs[14], refs[15]

        @pl.when(pl.program_id(1) == 0)
        def _():
            st[...] = jnp.zeros_like(st)

        for d in range(2):
            x_ref, b_ref, c_ref, sm_ref = ins[4 * d:4 * d + 4]
            sm = sm_ref[...]
            for gg in range(SSM_GPS):
                sx, sn = slice(gg * SSM_GW, (gg + 1) * SSM_GW), slice(gg * SSM_N, (gg + 1) * SSM_N)
                cm = _ssd_common(sm, sel_ref[d, gg], dtb_ref[d, gg], a_ref[d, gg], a512_ref[d, gg], d)
                bm, cmat = _bf(b_ref[:, sn]), _bf(c_ref[:, sn])
                xdt = x_ref[:, sx] * cm["dt_exp"]
                cb = _nt(cmat, bm)
                ys = [_nn(_bf(cb * _ssd_decay(cm, e)), _bf(xdt[:, SSM_P * e:SSM_P * (e + 1)])) for e in range(SSM_HPG)]
                s_in = st[d, gg]
                hist_ref[d, 0, gg] = s_in
                y_refs[d][:, sx] = jnp.concatenate(ys, axis=1) + jnp.exp(cm["cum_exp"]) * _nn(cmat, _bf(s_in))
                st[d, gg] = jnp.exp(cm["tot_exp"]) * s_in + _tn(bm, _bf(xdt * jnp.exp(cm["tot_exp"] - cm["cum_exp"])))

    in_specs, rbs = _ssd_specs(n_lat, n_ctx, lambda i: i)
    return pl.pallas_call(
        body, name="ssd_fwd", grid=(SSM_G // SSM_GPS, n_steps), in_specs=in_specs,
        out_specs=[pl.BlockSpec((SSM_C, SSM_GPS * SSM_GW), lambda g, i, rb=rb: (rb(i), g)) for rb in rbs]
        + [pl.BlockSpec((2, 1, SSM_GPS, SSM_N, SSM_GW), lambda g, i: (0, i, g, 0, 0))],
        out_shape=[jax.ShapeDtypeStruct((R, SSM_INNER), F32)] * 2
        + [jax.ShapeDtypeStruct((2, n_steps, SSM_G, SSM_N, SSM_GW), F32)],
        scratch_shapes=[pltpu.VMEM((2, SSM_GPS, SSM_N, SSM_GW), F32)],
        compiler_params=_cparams(("arbitrary", "arbitrary")),
    )(*([xbc, xbc, xbc, parts] * 2), k["sel"], k["dtb"], k["a"], k["a512"])


def _ssd_bwd(dy, xbc, parts, k, hist, n_lat, n_ctx):
    R = parts.shape[0]
    n_steps = n_lat + n_ctx
    step_of = lambda j: n_steps - 1 - j

    def one(d, gg, x_ref, b_ref, c_ref, sm_ref, sel_ref, dtb_ref, a_ref, a512_ref, selt_ref, e512t_ref, dy_ref,
            hist_ref, dx_ref, db_ref, dc_ref, dsm_ref, acc_ref, dst):
        sx, sn = slice(gg * SSM_GW, (gg + 1) * SSM_GW), slice(gg * SSM_N, (gg + 1) * SSM_N)
        a_neg, e512_t = a_ref[d, gg], e512t_ref[...]
        cm = _ssd_common(sm_ref[...], sel_ref[d, gg], dtb_ref[d, gg], a_neg, a512_ref[d, gg], d)
        x, dyv = x_ref[:, sx], dy_ref[:, sx]
        bm, cmat = _bf(b_ref[:, sn]), _bf(c_ref[:, sn])
        xdt = x * cm["dt_exp"]
        cb = _nt(cmat, bm)
        s_in, ds = hist_ref[d, 0, gg], dst[d, gg]
        w = jnp.exp(cm["tot_exp"] - cm["cum_exp"])
        z = _nn(bm, _bf(ds))
        decay_in = jnp.exp(cm["cum_exp"])
        gy = _bf(dyv * decay_in)
        dcb = jnp.zeros((SSM_C, SSM_C), F32)
        dxs, crossing = [], []
        row = lax.broadcasted_iota(jnp.int32, (SSM_C, SSM_C), 0)
        col = lax.broadcasted_iota(jnp.int32, (SSM_C, SSM_C), 1)
        eye = (row == col).astype(BF16)
        before = (cm["mask_t"] - eye)
        for e in range(SSM_HPG):
            s = slice(SSM_P * e, SSM_P * (e + 1))
            lm = _ssd_decay(cm, e)
            dy_e = _bf(dyv[:, s])
            m_e = cb * lm
            dm_e = _nt(dy_e, _bf(xdt[:, s]))
            dcb = dcb + dm_e * lm
            dxs.append(_tn(_bf(m_e), dy_e))
            through = jnp.where(cm["mask"], _nn(_bf(dm_e * m_e), before), 0.0)
            crossing.append(_colsum(through))
        da_rows = jnp.concatenate(crossing + [jnp.zeros((128 - SSM_HPG, SSM_C), F32)], axis=0)
        r_hi, r_lo = _split(da_rows)
        da8_intra = _tn(r_hi, eye) + _tn(r_lo, eye)
        dx_state = w * z
        dxdt = jnp.concatenate(dxs, axis=1) + dx_state
        dcb = _bf(dcb)
        c_s = _nn(cmat, _bf(s_in))
        dc_ref[:, sn] = _nn(dcb, bm) + _nt(gy, _bf(s_in))
        db_ref[:, sn] = _tn(dcb, cmat) + _nt(_bf(w * xdt), _bf(ds))
        dst[d, gg] = jnp.exp(cm["tot_exp"]) * ds + _tn(cmat, gy)
        state_path = xdt * dx_state
        per_token = _nn_x(jnp.concatenate([dyv * decay_in * c_s - state_path, dxdt * x], axis=0), e512_t)
        totals = jnp.concatenate([_colsum(state_path), _colsum(ds * s_in), jnp.zeros((6, SSM_GW), F32)], axis=0)
        totals = _nn_x(totals, e512_t)
        tot8 = _colsum(cm["a8"])
        dtot8 = totals[0:1] + jnp.exp(tot8) * totals[1:2]
        da8 = da8_intra + _x_nn(cm["mask_t"], per_token[:SSM_C]) + dtot8
        ddt8 = da8 * a_neg + per_token[SSM_C:]
        dsm_ref[gg] = _nn_x(ddt8 * _sigmoid(cm["dtr8"]), selt_ref[d, gg])
        dx_ref[:, sx] = dxdt * cm["dt_exp"]
        acc_ref[d, gg, 0:1, :] += _colsum(da8 * cm["a8"])

    def body(*refs):
        ins, consts, (selt_ref, e512t_ref), dy_refs, hist_ref = refs[:8], refs[8:12], refs[12:14], refs[14:16], refs[16]
        outs, acc_ref, dst = refs[17:25], refs[25], refs[26]

        @pl.when(pl.program_id(1) == 0)
        def _():
            dst[...] = jnp.zeros_like(dst)
            acc_ref[...] = jnp.zeros_like(acc_ref)

        for d in range(2):
            for gg in range(SSM_GPS):
                one(d, gg, *ins[4 * d:4 * d + 4], *consts, selt_ref, e512t_ref, dy_refs[d], hist_ref,
                    *outs[4 * d:4 * d + 4], acc_ref, dst)

    xw, nw = SSM_GPS * SSM_GW, SSM_GPS * SSM_N
    in_specs, rbs = _ssd_specs(n_lat, n_ctx, step_of)
    in_specs += [pl.BlockSpec((2, SSM_GPS, 128, 128), lambda g, j: (0, g, 0, 0)), pl.BlockSpec((SSM_GW, 128), lambda g, j: (0, 0))]
    in_specs += [pl.BlockSpec((SSM_C, xw), lambda g, j, rb=rb: (rb(j), g)) for rb in rbs]
    in_specs += [pl.BlockSpec((2, 1, SSM_GPS, SSM_N, SSM_GW), lambda g, j: (0, step_of(j), g, 0, 0))]
    out_specs, out_shape = [], []
    for rb in rbs:
        out_specs += [pl.BlockSpec((SSM_C, xw), lambda g, j, rb=rb: (rb(j), g)),
                      pl.BlockSpec((SSM_C, nw), lambda g, j, rb=rb: (rb(j), g)),
                      pl.BlockSpec((SSM_C, nw), lambda g, j, rb=rb: (rb(j), g)),
                      pl.BlockSpec((SSM_GPS, SSM_C, 128), lambda g, j, rb=rb: (g, rb(j), 0))]
        out_shape += [jax.ShapeDtypeStruct((R, SSM_INNER), F32), jax.ShapeDtypeStruct((R, SSM_G * SSM_N), F32),
                      jax.ShapeDtypeStruct((R, SSM_G * SSM_N), F32), jax.ShapeDtypeStruct((SSM_G, R, 128), F32)]
    out_specs.append(pl.BlockSpec((2, SSM_GPS, 8, 128), lambda g, j: (0, g, 0, 0)))
    out_shape.append(jax.ShapeDtypeStruct((2, SSM_G, 8, 128), F32))
    outs = pl.pallas_call(
        body, name="ssd_bwd", grid=(SSM_G // SSM_GPS, n_steps), in_specs=in_specs, out_specs=out_specs, out_shape=out_shape,
        scratch_shapes=[pltpu.VMEM((2, SSM_GPS, SSM_N, SSM_GW), F32)],
        compiler_params=_cparams(("arbitrary", "arbitrary")),
    )(*([xbc, xbc, xbc, parts] * 2), k["sel"], k["dtb"], k["a"], k["a512"], k["sel_t"], k["e512_t"], dy, dy, hist)
    return [(outs[n], outs[4 + n]) for n in range(4)] + [outs[8]]


def _gla_assemble(dq, dk, dv, tr):
    R = dq[0].shape[0]
    qk = GLA_H * GLA_DK

    def body(dqf_ref, dqb_ref, dkf_ref, dkb_ref, dvf_ref, dvb_ref, o_ref):
        o_ref[:, 0:qk] = (dqf_ref[...] + dqb_ref[...]).astype(BF16)
        o_ref[:, qk:2 * qk] = (dkf_ref[...] + dkb_ref[...]).astype(BF16)
        o_ref[:, 2 * qk:] = (dvf_ref[...] + dvb_ref[...]).astype(BF16)

    return pl.pallas_call(
        body, name="gla_assemble", grid=(R // tr,), in_specs=[_row_spec(tr, qk)] * 4 + [_row_spec(tr, D)] * 2,
        out_specs=_row_spec(tr, 2 * D),
        out_shape=jax.ShapeDtypeStruct((R, 2 * D), BF16), compiler_params=_cparams(("arbitrary",)),
    )(*dq, *dk, *dv)


def _xbc_assemble(dx, db, dc, dxs_skip, tr):
    R = dx[0].shape[0]
    bc = SSM_G * SSM_N

    def body(dxf_ref, dxb_ref, dbf_ref, dbb_ref, dcf_ref, dcb_ref, sk_ref, o_ref):
        o_ref[:, 0:SSM_INNER] = dxf_ref[...] + dxb_ref[...] + sk_ref[...]
        o_ref[:, SSM_INNER:SSM_INNER + bc] = dbf_ref[...] + dbb_ref[...]
        o_ref[:, SSM_INNER + bc:] = dcf_ref[...] + dcb_ref[...]

    return pl.pallas_call(
        body, name="xbc_assemble", grid=(R // tr,),
        in_specs=[_row_spec(tr, SSM_INNER)] * 2 + [_row_spec(tr, bc)] * 4 + [_row_spec(tr, SSM_INNER)],
        out_specs=_row_spec(tr, CONV_W), out_shape=jax.ShapeDtypeStruct((R, CONV_W), F32),
        compiler_params=_cparams(("arbitrary",)),
    )(*dx, *db, *dc, dxs_skip)


def _small_assemble(dp, dsm, parts, ut_hi, ut_lo, tr):
    R = parts.shape[0]
    qk = GLA_H * GLA_DK

    def body(dpf_ref, dpb_ref, dsmf_ref, dsmb_ref, sm_ref, uth_ref, utl_ref, o_ref, dup_ref, acc_ref, acc2_ref):
        @pl.when(pl.program_id(0) == 0)
        def _():
            dup_ref[...] = jnp.zeros_like(dup_ref)
            acc_ref[...] = jnp.zeros_like(acc_ref)
            acc2_ref[...] = jnp.zeros_like(acc2_ref)

        ssd = dsmf_ref[0] + dsmb_ref[0]
        for g in range(1, SSM_G):
            ssd = ssd + (dsmf_ref[g] + dsmb_ref[g])
        acc2_ref[0:1, :] += _colsum(ssd)
        sm_hi, sm_lo = _split(sm_ref[...])
        out = ssd
        for d, dp_ref in enumerate((dpf_ref, dpb_ref)):
            dpd = dp_ref[...]
            out = out + _nn3(dpd, uth_ref[d], utl_ref[d])
            p_hi, p_lo = _split(dpd)
            dup_ref[d] += _tn(sm_hi, p_hi) + _tn(sm_lo, p_hi) + _tn(sm_hi, p_lo)
            acc_ref[d:d + 1, :] += _colsum(dpd)
        o_ref[...] = out.astype(BF16)

    return pl.pallas_call(
        body, name="small_assemble", grid=(R // tr,),
        in_specs=[_row_spec(tr, qk)] * 2 + [pl.BlockSpec((SSM_G, tr, 128), lambda i: (0, i, 0))] * 2
        + [_row_spec(tr, 128, COL_SM // 128), pl.BlockSpec((2, qk, 128), lambda i: (0, 0, 0)),
           pl.BlockSpec((2, qk, 128), lambda i: (0, 0, 0))],
        out_specs=[_row_spec(tr, 128), pl.BlockSpec((2, 128, qk), lambda i: (0, 0, 0)), _acc_spec(qk), _acc_spec(128)],
        out_shape=[jax.ShapeDtypeStruct((R, 128), BF16), jax.ShapeDtypeStruct((2, 128, qk), F32),
                   jax.ShapeDtypeStruct((8, qk), F32), jax.ShapeDtypeStruct((8, 128), F32)],
        compiler_params=_cparams(("arbitrary",)),
    )(*dp, *dsm, parts, ut_hi, ut_lo)


ADA_ROWS = 16
ADA_TILE = 512


def _dot3_f32(a, b, ca, cb):
    a_hi, a_lo = _split(a)
    b_hi, b_lo = _split(b)
    return _dg(a_hi, b_hi, ca, cb) + _dg(a_lo, b_hi, ca, cb) + _dg(a_hi, b_lo, ca, cb)


def _ada_fwd(cvec, w, b):
    cols = w.shape[1]

    def body(c_ref, w_ref, b_ref, o_ref):
        o_ref[...] = _dot3_f32(_silu(c_ref[...]), w_ref[...], 1, 0) + b_ref[...]

    return pl.pallas_call(
        body, name="ada_fwd", grid=(cols // ADA_TILE,),
        in_specs=[pl.BlockSpec((ADA_ROWS, D), lambda j: (0, 0)), pl.BlockSpec((D, ADA_TILE), lambda j: (0, j)),
                  pl.BlockSpec((1, ADA_TILE), lambda j: (0, j))],
        out_specs=pl.BlockSpec((ADA_ROWS, ADA_TILE), lambda j: (0, j)), out_shape=jax.ShapeDtypeStruct((ADA_ROWS, cols), F32),
        compiler_params=_cparams(("arbitrary",)),
    )(cvec, w, b)


def _adam(w, g, m, v):
    m2 = ADAM_B1 * m + (1.0 - ADAM_B1) * g
    v2 = ADAM_B2 * v + (1.0 - ADAM_B2) * (g * g)
    m_hat = m2 / (1.0 - ADAM_B1 ** ADAM_STEP)
    v_hat = v2 / (1.0 - ADAM_B2 ** ADAM_STEP)
    return -ADAM_LR * (m_hat / (jnp.sqrt(v_hat) + ADAM_EPS) + ADAM_WD * w), m2, v2


def _wada_bwd_adam(cvec, dada, w, m, v):
    rows, cols = w.shape
    tr = _tile(rows, 256, 128)

    def body(c_ref, d_ref, w_ref, m_ref, v_ref, g_ref, dl_ref, m2_ref, v2_ref, p_ref):
        wv = w_ref[...]
        g = _dot3_f32(_silu(c_ref[...]), d_ref[...], 0, 0)
        g_ref[...] = g
        dl_ref[...], m2_ref[...], v2_ref[...] = _adam(wv, g, m_ref[...], v_ref[...])
        p_ref[...] = _dot3_f32(d_ref[...], wv, 1, 1)

    blk = pl.BlockSpec((tr, cols), lambda i: (i, 0))
    return pl.pallas_call(
        body, name="wada_bwd_adam", grid=(rows // tr,),
        in_specs=[pl.BlockSpec((ADA_ROWS, tr), lambda i: (0, i)), pl.BlockSpec((ADA_ROWS, cols), lambda i: (0, 0)), blk, blk, blk],
        out_specs=[blk, blk, blk, blk, pl.BlockSpec((ADA_ROWS, tr), lambda i: (0, i))],
        out_shape=[jax.ShapeDtypeStruct((rows, cols), F32)] * 4 + [jax.ShapeDtypeStruct((ADA_ROWS, rows), F32)],
        compiler_params=_cparams(("arbitrary",)),
    )(cvec, dada, w, m, v)


def _reduce_adam(parts8, w, m, v, name):
    rows, cols = w.shape
    tr = _tile(rows, 64, 16)

    def body(p_ref, w_ref, m_ref, v_ref, g_ref, dl_ref, m2_ref, v2_ref):
        g = p_ref[0].astype(F32) + p_ref[N_CHIPS].astype(F32)
        for j in range(1, N_CHIPS):
            g = g + (p_ref[j].astype(F32) + p_ref[N_CHIPS + j].astype(F32))
        g_ref[...] = g
        dl_ref[...], m2_ref[...], v2_ref[...] = _adam(w_ref[...], g, m_ref[...], v_ref[...])

    blk = pl.BlockSpec((tr, cols), lambda i: (i, 0))
    return pl.pallas_call(
        body, name=name, grid=(rows // tr,), in_specs=[pl.BlockSpec((N_DEV, tr, cols), lambda i: (0, i, 0)), blk, blk, blk],
        out_specs=[blk] * 4, out_shape=[jax.ShapeDtypeStruct((rows, cols), F32)] * 4, compiler_params=_cparams(("arbitrary",)),
    )(parts8, w, m, v)


SMALL_W = 1024


def _sum8(g8):
    rows = g8.shape[1]

    def body(g_ref, o_ref):
        s = g_ref[0]
        for j in range(1, N_DEV):
            s = s + g_ref[j]
        o_ref[...] = s

    return pl.pallas_call(
        body, name="sum8", out_shape=jax.ShapeDtypeStruct((rows, SMALL_W), F32),
        in_specs=[pl.BlockSpec(memory_space=pltpu.VMEM)], out_specs=pl.BlockSpec(memory_space=pltpu.VMEM),
        compiler_params=pltpu.CompilerParams(vmem_limit_bytes=VMEM_LIMIT),
    )(g8)


def _cctx_grad(p8, c_ctx):
    def body(p_ref, c_ref, o_ref):
        s = p_ref[0]
        for chip in range(1, N_CHIPS):
            s = s + p_ref[2 * chip]
        o_ref[...] = s * _dsilu(c_ref[...])

    return pl.pallas_call(
        body, name="cctx_grad", out_shape=jax.ShapeDtypeStruct((1, D), F32),
        in_specs=[pl.BlockSpec(memory_space=pltpu.VMEM)] * 2, out_specs=pl.BlockSpec(memory_space=pltpu.VMEM),
    )(p8, c_ctx)


def _adam_small(w, g, m, v):
    def body(w_ref, g_ref, m_ref, v_ref, dl_ref, m2_ref, v2_ref):
        dl_ref[...], m2_ref[...], v2_ref[...] = _adam(w_ref[...], g_ref[...], m_ref[...], v_ref[...])

    vm = pl.BlockSpec(memory_space=pltpu.VMEM)
    return pl.pallas_call(
        body, name="adam_small", out_shape=[jax.ShapeDtypeStruct(w.shape, F32)] * 3, in_specs=[vm] * 4, out_specs=[vm] * 3,
        compiler_params=pltpu.CompilerParams(vmem_limit_bytes=VMEM_LIMIT),
    )(w, g, m, v)


def _pack(vecs, width=SMALL_W, row_mult=8):
    flat = jnp.concatenate([v.reshape(-1).astype(F32) for v in vecs])
    n = flat.shape[0]
    rows = -(-n // (width * row_mult)) * row_mult
    return jnp.pad(flat, (0, rows * width - n)).reshape(rows, width)


def _unpack(packed, shapes):
    flat = packed.reshape(-1)
    out, off = [], 0
    for s in shapes:
        n = int(np.prod(s))
        out.append(flat[off:off + n].reshape(s))
        off += n
    return out


WEIGHTS = ('c_ctx', 'w_ada', 'b_ada', 'norm1_w', 'w_in', 'gla_up_f', 'gla_bias_f', 'gla_up_b', 'gla_bias_b', 'gla_norm_w',
           'conv_w', 'conv_b', 'dt_bias_f', 'dt_bias_b', 'a_log_f', 'a_log_b', 'd_skip', 'ssm_norm_w', 'w_pa', 'w_pb', 'w_out',
           'norm2_w', 'w_gate', 'w_up', 'w_down', 'final_norm_w')
BIG = ('w_in', 'w_pa', 'w_pb', 'w_out', 'w_gate', 'w_up', 'w_down')
COL_SHARDED = ('w_in', 'w_gate', 'w_up')
SMALL_SHARDED = ('gla_up_f', 'gla_up_b', 'conv_w')
ROW_TILE = 256


def _blocks_to_full(g4, name):
    n, r, c = g4.shape
    return g4.transpose(1, 0, 2).reshape(r, n * c) if name in COL_SHARDED else g4.reshape(n * r, c)


def _full_to_blocks(full, name):
    r, c = full.shape
    if name in COL_SHARDED:
        return full.reshape(r, N_CHIPS, c // N_CHIPS).transpose(1, 0, 2)
    return full.reshape(N_CHIPS, r // N_CHIPS, c)


def _permute_in(w_in_full):
    off = np.concatenate([[0], np.cumsum(IN_WIDTHS)])
    cols = [w_in_full[:, off[p]:off[p + 1]] for p in PERM]
    return jnp.concatenate(cols + [jnp.zeros((w_in_full.shape[0], SMALL_PAD), w_in_full.dtype)], axis=1)


def _unpermute_in(wp):
    off = np.concatenate([[0], np.cumsum([IN_WIDTHS[p] for p in PERM])])
    pieces = {p: wp[:, off[i]:off[i + 1]] for i, p in enumerate(PERM)}
    return jnp.concatenate([pieces[p] for p in range(len(IN_WIDTHS))], axis=1)


def _chip_cols(full, chip, n):
    return lax.dynamic_slice_in_dim(full, chip * n, n, axis=1)


def kernel(x, c, ctx, c_ctx, w_ada, b_ada, norm1_w, w_in, gla_up_f, gla_bias_f, gla_up_b, gla_bias_b, gla_norm_w, conv_w, conv_b, dt_bias_f, dt_bias_b, a_log_f, a_log_b, d_skip, ssm_norm_w, w_pa, w_pb, w_out, norm2_w, w_gate, w_up, w_down, final_norm_w, loss_target, m_c_ctx, m_w_ada, m_b_ada, m_norm1_w, m_w_in, m_gla_up_f, m_gla_bias_f, m_gla_up_b, m_gla_bias_b, m_gla_norm_w, m_conv_w, m_conv_b, m_dt_bias_f, m_dt_bias_b, m_a_log_f, m_a_log_b, m_d_skip, m_ssm_norm_w, m_w_pa, m_w_pb, m_w_out, m_norm2_w, m_w_gate, m_w_up, m_w_down, m_final_norm_w, v_c_ctx, v_w_ada, v_b_ada, v_norm1_w, v_w_in, v_gla_up_f, v_gla_bias_f, v_gla_up_b, v_gla_bias_b, v_gla_norm_w, v_conv_w, v_conv_b, v_dt_bias_f, v_dt_bias_b, v_a_log_f, v_a_log_b, v_d_skip, v_ssm_norm_w, v_w_pa, v_w_pb, v_w_out, v_norm2_w, v_w_gate, v_w_up, v_w_down, v_final_norm_w):
    given = dict(locals())
    W = {n: given[n] for n in WEIGHTS}
    M = {n: given["m_" + n] for n in WEIGHTS}
    V = {n: given["v_" + n] for n in WEIGHTS}
    L, Lc = x.shape[1], ctx.shape[1]
    tr = ROW_TILE
    assert L % tr == 0 and Lc % tr == 0 and L % Lc == 0 and Lc % SSM_C == 0
    n_lat_tiles = L // tr
    xi, yi, ci = _place()
    chip, me = 2 * xi + yi, 4 * xi + 2 * yi + ci
    xall = jnp.concatenate([x[0], ctx[0]], axis=0)

    g0 = _allgather_small(_pack([c[0]] + [W[n][0] for n in SMALL_SHARDED]), "gather_c")
    g0 = g0.reshape(N_DEV, -1)
    c_all = g0[:, :D]
    small_full, off = {}, D
    for n in SMALL_SHARDED:
        r, cols = W[n].shape[1:]
        small_full[n] = jnp.concatenate([g0[2 * k, off:off + r * cols].reshape(r, cols) for k in range(N_CHIPS)], axis=1)
        off += r * cols
    up_f, up_b, conv_w_full = (small_full[n] for n in SMALL_SHARDED)

    cvec = jnp.zeros((ADA_ROWS, D), F32).at[:N_DEV].set(c_all).at[N_DEV].set(c_ctx)
    ada_cols = w_ada.shape[2]
    ada_part = _ada_fwd(cvec, w_ada[0], _chip_cols(b_ada, chip, ada_cols))
    g1_all = _allgather_small(ada_part, "gather_ada")
    ada_full = jnp.concatenate([g1_all[2 * k] for k in range(N_CHIPS)], axis=1)
    mine = lax.dynamic_slice_in_dim(ada_full, me, 1, axis=0)
    sh1, sc1, g1, sh2, sc2, g2 = (mine[:, k * D:(k + 1) * D] for k in range(6))
    csh1, csc1 = ada_full[N_DEV:N_DEV + 1, :D], ada_full[N_DEV:N_DEV + 1, D:2 * D]
    mod = jnp.stack([jnp.stack([sh1, sc1]), jnp.stack([csh1, csc1])])

    gathered = _gather_weights([W[n][0].astype(BF16) for n in BIG], "gather_weights")
    full = {n: _blocks_to_full(g, n) for n, g in zip(BIG, gathered)}
    wp = _permute_in(full['w_in'])

    def lr_rows(up, base):
        return jnp.zeros((128, GLA_H * GLA_DK), F32).at[base:base + GLA_RANK].set(up)
    u2 = jnp.stack([lr_rows(up_f, SM_LRF), lr_rows(up_b, SM_LRB)])
    u2_hi = u2.astype(BF16)
    u2_lo = (u2 - u2_hi.astype(F32)).astype(BF16)
    ut = u2.transpose(0, 2, 1)
    ut_hi = ut.astype(BF16)
    ut_lo = (ut - ut_hi.astype(F32)).astype(BF16)
    gbias = jnp.stack([gla_bias_f, gla_bias_b])
    kc = _ssd_consts(jnp.stack([dt_bias_f[0], dt_bias_b[0]]), jnp.stack([a_log_f[0], a_log_b[0]]))
    gw4 = jnp.tile(gla_norm_w, (1, GLA_H))
    dskip_exp = jnp.repeat(d_skip, SSM_P, axis=1)
    n_gla = (L // GLA_C, Lc // GLA_C)
    n_ssd = (L // SSM_C, Lc // SSM_C)

    h1 = _norm_mod(xall, norm1_w, mod, n_lat_tiles, tr)
    parts = _mm(h1, wp, "nn", F32, "mm_in", tm=768, tn=1152)
    xbc = _conv_fwd(parts, conv_w_full, conv_b, L // Lc, Lc)
    *o2, gla_hist = _gla_fwd(parts, u2_hi, u2_lo, gbias, *n_gla)
    *y2, ssd_hist = _ssd_fwd(xbc, parts, kc, *n_ssd)
    oan = _gla_out(o2, parts, gw4, tr)
    obn = _ssd_out(y2, xbc, parts, dskip_exp, ssm_norm_w, tr)
    ya = _mm(oan, full['w_pa'], "nn", F32, "mm_pa", tm=768)
    yb = _mm(obn, full['w_pb'], "nn", F32, "mm_pb", tm=768)
    merged = _merge(ya, yb, parts, tr)
    mix = _mm(merged, full['w_out'], "nn", F32, "mm_out", tm=768)
    h2, u = _resid_norm_mod(xall, mix, g1, norm2_w, sh2, sc2, tr)
    gp = _mm(u, full['w_gate'], "nn", F32, "mm_gate", tm=768, tn=1408)
    up = _mm(u, full['w_up'], "nn", F32, "mm_up", tm=768, tn=1408)
    act = _swiglu_act(gp, up, tr)
    f = _mm(act, full['w_down'], "nn", F32, "mm_down", tm=768)
    dh3, df, acc_loss = _loss_head(h2, f, loss_target[0], g2, final_norm_w[None], n_lat_tiles, tr)

    dw = {}
    da = _mm(df, full['w_down'], "nt", F32, "mm_down_dx", tm=768, tn=1408)
    dw['w_down'] = _mm(act, df, "tn", BF16, "mm_down_dw", tm=1408, tk=768)
    dgp, dup = _swiglu_act_bwd(da, gp, up, tr)
    du_a = _mm(dgp, full['w_gate'], "nt", F32, "mm_gate_dx", tm=768, tk=1408)
    du_b = _mm(dup, full['w_up'], "nt", F32, "mm_up_dx", tm=768, tk=1408)
    dw['w_gate'] = _mm(u, dgp, "tn", BF16, "mm_gate_dw", tm=1024, tn=1408, tk=768)
    dw['w_up'] = _mm(u, dup, "tn", BF16, "mm_up_dw", tm=1024, tn=1408, tk=768)
    dh2, dmix, acc_ffn = _ffn_in_bwd(du_a, du_b, h2, dh3, mix, sc2, g1, norm2_w, tr)
    dmerged = _mm(dmix, full['w_out'], "nt", F32, "mm_out_dx", tm=768)
    dw['w_out'] = _mm(merged, dmix, "tn", BF16, "mm_out_dw", tm=1024, tk=768)
    dya, dyb, dgates = _merge_bwd(dmerged, ya, yb, parts, tr)
    doan = _mm(dya, full['w_pa'], "nt", F32, "mm_pa_dx", tm=768)
    dw['w_pa'] = _mm(oan, dya, "tn", BF16, "mm_pa_dw", tm=1024, tk=768)
    dobn = _mm(dyb, full['w_pb'], "nt", F32, "mm_pb_dx", tm=768)
    dw['w_pb'] = _mm(obn, dyb, "tn", BF16, "mm_pb_dw", tm=1024, tk=768)
    do, dr, acc_gla = _gla_out_bwd(doan, o2, parts, gw4, tr)
    dq, dk, dv, dpre = _gla_bwd(do, parts, u2_hi, u2_lo, gbias, gla_hist, *n_gla)
    dy, dz, dxs_skip, acc_ssd = _ssd_out_bwd(dobn, y2, xbc, parts, dskip_exp, ssm_norm_w, tr)
    dx_scan, db_scan, dc_scan, dsm, acc_alog = _ssd_bwd(dy, xbc, parts, kc, ssd_hist, *n_ssd)
    dxbc = _xbc_assemble(dx_scan, db_scan, dc_scan, dxs_skip, tr)
    du_conv, acc_conv = _conv_bwd(dxbc, parts, conv_w_full, conv_b, L // Lc, Lc)
    dqkv = _gla_assemble(dq, dk, dv, tr)
    dsmall, dup_gla, acc_gbias, acc_dtb = _small_assemble(dpre, dsm, parts, ut_hi, ut_lo, tr)
    dparts = jnp.concatenate([dz, du_conv, dqkv, dr, dgates, dsmall], axis=1)
    dh1 = _mm(dparts, wp, "nt", F32, "mm_in_dx", tm=768, tk=1152)
    dw['w_in'] = _unpermute_in(_mm(h1, dparts, "tn", BF16, "mm_in_dw", tm=1024, tn=1152, tk=768))
    dxall, acc_n1 = _norm1_bwd(dh1, xall, dh2, norm1_w, mod, n_lat_tiles, tr)

    partial = dict(
        norm1_w=acc_n1[0, 2] + acc_n1[1, 2],
        gla_up_f=dup_gla[0, SM_LRF:SM_LRF + GLA_RANK], gla_bias_f=acc_gbias[0],
        gla_up_b=dup_gla[1, SM_LRB:SM_LRB + GLA_RANK], gla_bias_b=acc_gbias[1],
        gla_norm_w=acc_gla[0].reshape(GLA_H, GLA_DV).sum(0),
        conv_w=acc_conv[:SSM_CONV], conv_b=acc_conv[SSM_CONV],
        dt_bias_f=acc_dtb[0, SM_DTF:SM_DTF + SSM_HEADS], dt_bias_b=acc_dtb[0, SM_DTB:SM_DTB + SSM_HEADS],
        a_log_f=acc_alog[0, :, 0, :SSM_HPG], a_log_b=acc_alog[1, :, 0, :SSM_HPG],
        d_skip=acc_ssd[1].reshape(SSM_HEADS, SSM_P).sum(1), ssm_norm_w=acc_ssd[0],
        norm2_w=acc_ffn[2], final_norm_w=acc_loss[0],
    )
    dada = jnp.concatenate([acc_n1[0, 1], acc_n1[0, 0], acc_ffn[3], acc_ffn[1], acc_ffn[0], acc_loss[1]])
    dada_ctx = jnp.concatenate([acc_n1[1, 1], acc_n1[1, 0], jnp.zeros((4 * D,), F32)])
    names = list(partial)
    payload = [partial[n] for n in names] + [dada + dada_ctx, dada_ctx, acc_loss[2], dada]
    sizes = [int(np.prod(p.shape)) for p in payload]
    g8 = _allgather_small(_pack(payload), "gather_small_grads")
    summed = _unpack(_sum8(g8), [(s,) for s in sizes])
    grads = {n: s.reshape(W[n].shape if n not in SMALL_SHARDED else partial[n].shape) for n, s in zip(names, summed)}
    grads['b_ada'] = summed[len(names)].reshape(b_ada.shape)
    dada_ctx_sum = summed[len(names) + 1]
    loss = jnp.sum(summed[len(names) + 2])
    dada_all = g8.reshape(N_DEV, -1)[:, sum(sizes[:-1]):sum(sizes)]

    dada16 = jnp.zeros((ADA_ROWS, ada_cols), F32)
    dada16 = dada16.at[:N_DEV].set(_chip_cols(dada_all, chip, ada_cols)).at[N_DEV].set(_chip_cols(dada_ctx_sum[None], chip, ada_cols)[0])
    g_wada, dl_wada, m_wada, v_wada, p16 = _wada_bwd_adam(cvec, dada16, w_ada[0], m_w_ada[0], v_w_ada[0])
    p8 = _allgather_small(p16[N_DEV:], "gather_cctx")
    grads['c_ctx'] = _cctx_grad(p8[:, 0:1, :], c_ctx[None])[0]
    for n in SMALL_SHARDED:
        grads[n] = _chip_cols(grads[n], chip, W[n].shape[2])[None]

    small = [n for n in WEIGHTS if n not in BIG and n != 'w_ada']
    shapes = [W[n].shape for n in small]
    dl_s, m_s, v_s = _adam_small(*[_pack([d[n] for n in small]) for d in (W, grads, M, V)])
    delta = dict(zip(small, _unpack(dl_s, shapes)))
    new_m = dict(zip(small, _unpack(m_s, shapes)))
    new_v = dict(zip(small, _unpack(v_s, shapes)))
    grads['w_ada'], delta['w_ada'], new_m['w_ada'], new_v['w_ada'] = g_wada[None], dl_wada[None], m_wada[None], v_wada[None]

    exchanged = _exchange_grads([_full_to_blocks(dw[n], n) for n in BIG], "exchange_grads")
    for n, p8w in zip(BIG, exchanged):
        g, dl, m2, v2 = _reduce_adam(p8w, W[n][0], M[n][0], V[n][0], "adam_" + n)
        grads[n], delta[n], new_m[n], new_v[n] = g[None], dl[None], m2[None], v2[None]

    return (loss, dxall[:L][None], *[grads[n] for n in WEIGHTS], *[delta[n] for n in WEIGHTS],
            *[new_m[n] for n in WEIGHTS], *[new_v[n] for n in WEIGHTS])
```

```python
import functools

import numpy as np
import jax
import jax.numpy as jnp
from jax import lax
from jax.experimental import pallas as pl
from jax.experimental.pallas import tpu as pltpu

F32 = jnp.float32
BF16 = jnp.bfloat16
MESH = pl.DeviceIdType.MESH

D = 1024
EPS = 1e-6
GRID_W = 64
GLA_H, GLA_DK, GLA_DV, GLA_RANK, GLA_TAU = 4, 128, 256, 16, 16.0
GLA_C = 64
SSM_INNER, SSM_P, SSM_HEADS, SSM_G, SSM_HPG, SSM_N = 2048, 64, 32, 4, 8, 128
SSM_C = 128
SSM_CONV, CONV_LEFT = 4, 2
D_FF = 2816
IN_WIDTHS = (512, 512, 1024, 1024, 16, 16, 2048, 2048, 512, 512, 32, 32, 1024, 1024)
D_IN = sum(IN_WIDTHS)
PERM = (6, 7, 8, 9, 0, 1, 2, 3, 12, 13, 4, 5, 10, 11)
PW = 10368
SMALL_PAD = PW - D_IN
COL_Z, COL_XBC, COL_Q, COL_K, COL_V, COL_R, COL_GA, COL_GB, COL_SM = 0, 2048, 5120, 5632, 6144, 7168, 8192, 9216, 10240
SM_LRF, SM_LRB, SM_DTF, SM_DTB = 0, 16, 32, 64
EXP_CLAMP = 80.0
ADAM_LR, ADAM_B1, ADAM_B2, ADAM_EPS, ADAM_WD, ADAM_STEP = 0.001, 0.9, 0.999, 1e-08, 0.01, 10
N_CHIPS, N_DEV = 4, 8
VMEM_LIMIT = 56 * 1024 * 1024


def _cparams(sem=None):
    return pltpu.CompilerParams(dimension_semantics=sem, vmem_limit_bytes=VMEM_LIMIT)


def _dg(a, b, ca, cb):
    return lax.dot_general(a, b, (((ca,), (cb,)), ((), ())), preferred_element_type=F32)


def _nn(a, b):
    return _dg(a, b, 1, 0)


def _nt(a, b):
    return _dg(a, b, 1, 1)


def _tn(a, b):
    return _dg(a, b, 0, 0)


def _bf(x):
    return x.astype(BF16)


def _split(x):
    hi = x.astype(BF16)
    return hi, (x - hi.astype(F32)).astype(BF16)


def _nn_x(a, b_exact):
    hi, lo = _split(a)
    return _nn(hi, b_exact) + _nn(lo, b_exact)


def _x_nn(a_exact, b):
    hi, lo = _split(b)
    return _nn(a_exact, hi) + _nn(a_exact, lo)


def _nn3(a, b_hi, b_lo):
    hi, lo = _split(a)
    return _nn(hi, b_hi) + _nn(lo, b_hi) + _nn(hi, b_lo)


def _sigmoid(x):
    return 1.0 / (1.0 + jnp.exp(-x))


def _silu(x):
    return x * _sigmoid(x)


def _dsilu(x):
    s = _sigmoid(x)
    return s * (1.0 + x * (1.0 - s))


def _softplus(x):
    return jnp.maximum(x, 0.0) + jnp.log(1.0 + jnp.exp(-jnp.abs(x)))


def _log_sigmoid(x):
    return jnp.minimum(x, 0.0) - jnp.log(1.0 + jnp.exp(-jnp.abs(x)))


def _tile(n, target, mult=8):
    best = None
    for t in range(mult, min(n, target) + 1, mult):
        if n % t == 0:
            best = t
    assert best is not None, (n, target, mult)
    return best


def _mm(a, b, mode, out_dtype, name, tm=512, tn=1024, tk=2048, comm=None):
    if mode == "nn":
        (M, K), N = a.shape, b.shape[1]
    elif mode == "nt":
        (M, K), N = a.shape, b.shape[0]
    else:
        (K, M), N = a.shape, b.shape[1]
    tm, tn, tk = _tile(M, tm, 128), _tile(N, tn, 128), _tile(K, tk, 128)
    nk = K // tk
    ca, cb = {"nn": (1, 0), "nt": (1, 1), "tn": (0, 0)}[mode]

    def body(a_ref, b_ref, o_ref, *acc):
        part = _dg(a_ref[...], b_ref[...], ca, cb)
        if nk == 1:
            o_ref[...] = part.astype(out_dtype)
        else:
            k = pl.program_id(2)

            @pl.when(k == 0)
            def _():
                acc[0][...] = part

            @pl.when(k > 0)
            def _():
                acc[0][...] += part

            @pl.when(k == nk - 1)
            def _():
                o_ref[...] = acc[0][...].astype(out_dtype)

    a_spec = pl.BlockSpec((tk, tm), lambda i, j, k: (k, i)) if mode == "tn" else pl.BlockSpec((tm, tk), lambda i, j, k: (i, k))
    b_spec = pl.BlockSpec((tn, tk), lambda i, j, k: (j, k)) if mode == "nt" else pl.BlockSpec((tk, tn), lambda i, j, k: (k, j))
    gi, gj = M // tm, N // tn
    scratch = [pltpu.VMEM((tm, tn), F32)] if nk > 1 else []
    out_spec, out_shape = pl.BlockSpec((tm, tn), lambda i, j, k: (i, j)), jax.ShapeDtypeStruct((M, N), out_dtype)
    if comm is None:
        return pl.pallas_call(
            body, name=name, grid=(gi, gj, nk), in_specs=[a_spec, b_spec], out_specs=out_spec, out_shape=out_shape,
            scratch_shapes=scratch, compiler_params=_cparams(("arbitrary", "arbitrary", "arbitrary")),
        )(a, b)
    at = lambda i, j, k: (pl.program_id(0) == i) & (pl.program_id(1) == j) & (pl.program_id(2) == k)
    hosted = _hosted(body, 2, 1, len(scratch), comm, lambda: at(0, 0, 0), lambda: at(gi - 1, 0, 0),
                     lambda: at(gi - 1, gj - 1, nk - 1))
    outs = pl.pallas_call(
        hosted, name=name, grid=(gi, gj, nk), in_specs=[a_spec, b_spec] + [_ANY] * len(comm.arrays),
        out_specs=[out_spec] + [_ANY] * len(comm.out_shape), out_shape=[out_shape] + comm.out_shape,
        scratch_shapes=scratch + comm.scratch, compiler_params=_cparams(("arbitrary", "arbitrary", "arbitrary")),
    )(a, b, *comm.arrays)
    return outs[0], outs[1:]


def _place():
    return lax.axis_index("x"), lax.axis_index("y"), lax.axis_index("c")


def _flip(v, bit):
    return 1 - v if bit else v


def _allgather_small(v, name):
    R, C = v.shape

    def body(v_ref, out_ref, send_sems, recv_sems, local_sem):
        x, y, c = _place()
        me = 4 * x + 2 * y + c
        mine = pltpu.make_async_copy(v_ref, out_ref.at[me], local_sem)
        mine.start()

        def peer(r):
            return _flip(x, (r >> 2) & 1), _flip(y, (r >> 1) & 1), _flip(c, r & 1)

        sends = [pltpu.make_async_remote_copy(
            src_ref=v_ref, dst_ref=out_ref.at[me], send_sem=send_sems.at[r - 1], recv_sem=recv_sems.at[r - 1],
            device_id=peer(r), device_id_type=MESH) for r in range(1, N_DEV)]
        for cp in sends:
            cp.start()
        for r in range(1, N_DEV):
            px, py, pc = peer(r)
            pltpu.make_async_remote_copy(
                src_ref=v_ref, dst_ref=out_ref.at[4 * px + 2 * py + pc], send_sem=send_sems.at[r - 1],
                recv_sem=recv_sems.at[r - 1], device_id=(x, y, c), device_id_type=MESH).wait_recv()
        for cp in sends:
            cp.wait_send()
        mine.wait()

    return pl.pallas_call(
        body, name=name, out_shape=jax.ShapeDtypeStruct((N_DEV, R, C), v.dtype),
        in_specs=[pl.BlockSpec(memory_space=pltpu.VMEM)], out_specs=pl.BlockSpec(memory_space=pltpu.VMEM),
        scratch_shapes=[pltpu.SemaphoreType.DMA((N_DEV - 1,)), pltpu.SemaphoreType.DMA((N_DEV - 1,)), pltpu.SemaphoreType.DMA],
        compiler_params=pltpu.CompilerParams(vmem_limit_bytes=VMEM_LIMIT),
    )(v)


_CHIP_RELATIONS = ((1, 0), (0, 1), (1, 1))


def _gather_split(shard, name):
    rows, cols = shard.shape
    half = rows // 2

    def body(in_ref, out_ref, send_sems, recv_sems, local_sem):
        x, y, c = _place()
        chip = 2 * x + y
        mine = pl.ds(pl.multiple_of(c * half, 16), half)
        local = pltpu.make_async_copy(in_ref, out_ref.at[chip], local_sem)
        local.start()
        peers = [(_flip(x, fx), _flip(y, fy)) for fx, fy in _CHIP_RELATIONS]
        sends = [pltpu.make_async_remote_copy(
            src_ref=in_ref.at[mine], dst_ref=out_ref.at[chip, mine], send_sem=send_sems.at[j], recv_sem=recv_sems.at[j],
            device_id=(px, py, c), device_id_type=MESH) for j, (px, py) in enumerate(peers)]
        for cp in sends:
            cp.start()
        for j, (px, py) in enumerate(peers):
            landed = out_ref.at[2 * px + py, mine]
            pltpu.make_async_remote_copy(
                src_ref=landed, dst_ref=landed, send_sem=send_sems.at[j], recv_sem=recv_sems.at[j],
                device_id=(x, y, c), device_id_type=MESH).wait_recv()
            fwd = pltpu.make_async_remote_copy(
                src_ref=landed, dst_ref=landed, send_sem=send_sems.at[3 + j], recv_sem=recv_sems.at[3 + j],
                device_id=(x, y, 1 - c), device_id_type=MESH)
            fwd.start()
            sends.append(fwd)
        for j in range(3):
            landed = out_ref.at[0, mine]
            pltpu.make_async_remote_copy(
                src_ref=landed, dst_ref=landed, send_sem=send_sems.at[3 + j], recv_sem=recv_sems.at[3 + j],
                device_id=(x, y, c), device_id_type=MESH).wait_recv()
        for cp in sends:
            cp.wait_send()
        local.wait()

    any_spec = pl.BlockSpec(memory_space=pl.ANY)
    return pl.pallas_call(
        body, name=name, out_shape=jax.ShapeDtypeStruct((N_CHIPS, rows, cols), shard.dtype),
        in_specs=[any_spec], out_specs=any_spec,
        scratch_shapes=[pltpu.SemaphoreType.DMA((6,)), pltpu.SemaphoreType.DMA((6,)), pltpu.SemaphoreType.DMA],
    )(shard)


class _Comm:
    def __init__(self, arrays, out_shape, scratch, start, middle, finish):
        self.arrays, self.out_shape, self.scratch = arrays, out_shape, scratch
        self.start, self.middle, self.finish = start, middle, finish


def _hosted(body, n_in, n_out, n_scratch, comm, first, middle, last):
    nc, no = len(comm.arrays), len(comm.out_shape)

    def wrapped(*refs):
        a = n_in + nc
        b = a + n_out + no
        ins, c_ins, outs, c_outs = refs[:n_in], refs[n_in:a], refs[a:a + n_out], refs[a + n_out:b]
        scratch, c_sems = refs[b:b + n_scratch], refs[b + n_scratch:]

        @pl.when(first())
        def _():
            comm.start(c_ins, c_outs, c_sems)

        body(*ins, *outs, *scratch)
        if comm.middle is not None:
            @pl.when(middle())
            def _():
                comm.middle(c_ins, c_outs, c_sems)

        @pl.when(last())
        def _():
            comm.finish(c_ins, c_outs, c_sems)

    return wrapped


_ANY = pl.BlockSpec(memory_space=pl.ANY)


def _gather_comm(shards):
    n = len(shards)

    def copies(kind, ins, outs, sems):
        send_sems, recv_sems, local_sems = sems
        x, y, c = _place()
        chip = 2 * x + y
        if kind == "local":
            return [pltpu.make_async_copy(ins[i], outs[i].at[chip], local_sems.at[i]) for i in range(n)]
        made = []
        for i in range(n):
            for j, (fx, fy) in enumerate(_CHIP_RELATIONS):
                px, py = _flip(x, fx), _flip(y, fy)
                slot, to = (chip, (px, py, c)) if kind == "send" else (2 * px + py, (x, y, c))
                made.append(pltpu.make_async_remote_copy(
                    src_ref=ins[i], dst_ref=outs[i].at[slot], send_sem=send_sems.at[i, j], recv_sem=recv_sems.at[i, j],
                    device_id=to, device_id_type=MESH))
        return made

    def start(ins, outs, sems):
        for cp in copies("local", ins, outs, sems) + copies("send", ins, outs, sems):
            cp.start()

    def finish(ins, outs, sems):
        for cp in copies("recv", ins, outs, sems):
            cp.wait_recv()
        for cp in copies("send", ins, outs, sems):
            cp.wait_send()
        for cp in copies("local", ins, outs, sems):
            cp.wait()

    return _Comm(list(shards), [jax.ShapeDtypeStruct((N_CHIPS,) + s.shape, s.dtype) for s in shards],
                 [pltpu.SemaphoreType.DMA((n, 3)), pltpu.SemaphoreType.DMA((n, 3)), pltpu.SemaphoreType.DMA((n,))],
                 start, None, finish)


def _exchange_comm(blocks):
    n = len(blocks)

    def copies(kind, ins, outs, sems):
        send_sems, recv_sems, local_sems = sems
        x, y, c = _place()
        chip = 2 * x + y
        me, sibling = (x, y, c), (x, y, 1 - c)

        def remote(src, dst, i, j, to):
            return pltpu.make_async_remote_copy(src_ref=src, dst_ref=dst, send_sem=send_sems.at[i, j],
                                                recv_sem=recv_sems.at[i, j], device_id=to, device_id_type=MESH)

        made = []
        for i in range(n):
            if kind == "local":
                made.append(pltpu.make_async_copy(ins[i].at[chip], outs[i].at[chip], local_sems.at[i]))
                continue
            for j, (fx, fy) in enumerate(_CHIP_RELATIONS):
                px, py = _flip(x, fx), _flip(y, fy)
                src = 2 * px + py
                if kind == "first":
                    made.append(remote(ins[i].at[src], outs[i].at[chip], i, j, (px, py, c)))
                elif kind == "landed":
                    made.append(remote(ins[i].at[src], outs[i].at[src], i, j, me))
                elif kind == "passed":
                    made.append(remote(outs[i].at[src], outs[i].at[N_CHIPS + src], i, 4 + j, sibling))
            if kind == "first":
                made.append(remote(ins[i].at[chip], outs[i].at[N_CHIPS + chip], i, 3, sibling))
            if kind == "arrivals":
                made += [remote(ins[i].at[0], outs[i].at[0], i, j, me) for j in (3, 4, 5, 6)]
        return made

    def start(ins, outs, sems):
        for cp in copies("local", ins, outs, sems) + copies("first", ins, outs, sems):
            cp.start()

    def middle(ins, outs, sems):
        for got, fwd in zip(copies("landed", ins, outs, sems), copies("passed", ins, outs, sems)):
            got.wait_recv()
            fwd.start()

    def finish(ins, outs, sems):
        for cp in copies("arrivals", ins, outs, sems):
            cp.wait_recv()
        for cp in copies("first", ins, outs, sems) + copies("passed", ins, outs, sems):
            cp.wait_send()
        for cp in copies("local", ins, outs, sems):
            cp.wait()

    return _Comm(list(blocks), [jax.ShapeDtypeStruct((N_DEV,) + b.shape[1:], b.dtype) for b in blocks],
                 [pltpu.SemaphoreType.DMA((n, 7)), pltpu.SemaphoreType.DMA((n, 7)), pltpu.SemaphoreType.DMA((n,))],
                 start, middle, finish)


def _row_spec(tr, w, col=0):
    return pl.BlockSpec((tr, w), lambda i: (i, col))


def _vec_spec(w):
    return pl.BlockSpec((1, w), lambda i: (0, 0))


def _acc_spec(w):
    return pl.BlockSpec((8, w), lambda i: (0, 0))


def _rms(x):
    return lax.rsqrt(jnp.mean(x * x, axis=-1, keepdims=True) + EPS)


def _rms_bwd(dn, n, rstd):
    return rstd * (dn - n * jnp.mean(dn * n, axis=-1, keepdims=True))


def _colsum(x):
    return jnp.sum(x, axis=0, keepdims=True)


def _zero_first(ref):
    @pl.when(pl.program_id(0) == 0)
    def _():
        ref[...] = jnp.zeros_like(ref)


def _norm_mod(xall, w, mod, n_lat_tiles, tr):
    R = xall.shape[0]

    def body(x_ref, w_ref, mod_ref, o_ref):
        x = x_ref[...]
        n = x * _rms(x) * w_ref[...]
        o_ref[...] = (n * (1.0 + mod_ref[0, 1]) + mod_ref[0, 0]).astype(BF16)

    return pl.pallas_call(
        body, name="norm1_mod", grid=(R // tr,),
        in_specs=[_row_spec(tr, D), _vec_spec(D),
                  pl.BlockSpec((1, 2, 1, D), lambda i: (jnp.where(i >= n_lat_tiles, 1, 0), 0, 0, 0))],
        out_specs=_row_spec(tr, D), out_shape=jax.ShapeDtypeStruct((R, D), BF16),
        compiler_params=_cparams(("arbitrary",)),
    )(xall, w, mod)


def _resid_norm_mod(xall, mix, g1, w2, sh2, sc2, tr):
    R = xall.shape[0]

    def body(x_ref, mix_ref, g1_ref, w_ref, sh_ref, sc_ref, h2_ref, u_ref):
        h2 = x_ref[...] + g1_ref[...] * mix_ref[...]
        h2_ref[...] = h2
        n = h2 * _rms(h2) * w_ref[...]
        u_ref[...] = (n * (1.0 + sc_ref[...]) + sh_ref[...]).astype(BF16)

    return pl.pallas_call(
        body, name="resid_norm2_mod", grid=(R // tr,),
        in_specs=[_row_spec(tr, D), _row_spec(tr, D)] + [_vec_spec(D)] * 4,
        out_specs=[_row_spec(tr, D), _row_spec(tr, D)],
        out_shape=[jax.ShapeDtypeStruct((R, D), F32), jax.ShapeDtypeStruct((R, D), BF16)],
        compiler_params=_cparams(("arbitrary",)),
    )(xall, mix, g1, w2, sh2, sc2)


def _loss_head(h2, f, target, g2, fw, n_lat_tiles, tr):
    R = h2.shape[0]

    def body(h2_ref, f_ref, t_ref, g2_ref, fw_ref, dh3_ref, df_ref, acc_ref):
        _zero_first(acc_ref)
        lat = pl.program_id(0) < n_lat_tiles
        fv = f_ref[...]
        h3 = h2_ref[...] + g2_ref[...] * fv
        rstd = _rms(h3)
        n = h3 * rstd
        err = n * fw_ref[...] - t_ref[...]
        dy = err * (1.0 / D)
        dh3 = jnp.where(lat, _rms_bwd(dy * fw_ref[...], n, rstd), 0.0)
        dh3_ref[...] = dh3
        df_ref[...] = (g2_ref[...] * dh3).astype(BF16)
        acc_ref[0:1, :] += jnp.where(lat, _colsum(dy * n), 0.0)
        acc_ref[1:2, :] += _colsum(dh3 * fv)
        acc_ref[2:3, :] += jnp.where(lat, _colsum(err * err) * (0.5 / D), 0.0)

    return pl.pallas_call(
        body, name="loss_head", grid=(R // tr,),
        in_specs=[_row_spec(tr, D), _row_spec(tr, D),
                  pl.BlockSpec((tr, D), lambda i: (jnp.minimum(i, n_lat_tiles - 1), 0)), _vec_spec(D), _vec_spec(D)],
        out_specs=[_row_spec(tr, D), _row_spec(tr, D), _acc_spec(D)],
        out_shape=[jax.ShapeDtypeStruct((R, D), F32), jax.ShapeDtypeStruct((R, D), BF16), jax.ShapeDtypeStruct((8, D), F32)],
        compiler_params=_cparams(("arbitrary",)),
    )(h2, f, target, g2, fw)


def _ffn_in_bwd(du_a, du_b, h2, dh3, mix, sc2, g1, w2, tr):
    R = h2.shape[0]

    def body(dua_ref, dub_ref, h2_ref, dh3_ref, mix_ref, sc_ref, g1_ref, w_ref, dh2_ref, dmix_ref, acc_ref):
        _zero_first(acc_ref)
        du = dua_ref[...] + dub_ref[...]
        h2 = h2_ref[...]
        rstd = _rms(h2)
        n = h2 * rstd
        dnw = du * (1.0 + sc_ref[...])
        dh2 = dh3_ref[...] + _rms_bwd(dnw * w_ref[...], n, rstd)
        dh2_ref[...] = dh2
        dmix_ref[...] = (g1_ref[...] * dh2).astype(BF16)
        acc_ref[0:1, :] += _colsum(du * n * w_ref[...])
        acc_ref[1:2, :] += _colsum(du)
        acc_ref[2:3, :] += _colsum(dnw * n)
        acc_ref[3:4, :] += _colsum(dh2 * mix_ref[...])

    return pl.pallas_call(
        body, name="ffn_in_bwd", grid=(R // tr,),
        in_specs=[_row_spec(tr, D)] * 5 + [_vec_spec(D)] * 3,
        out_specs=[_row_spec(tr, D), _row_spec(tr, D), _acc_spec(D)],
        out_shape=[jax.ShapeDtypeStruct((R, D), F32), jax.ShapeDtypeStruct((R, D), BF16), jax.ShapeDtypeStruct((8, D), F32)],
        compiler_params=_cparams(("arbitrary",)),
    )(du_a, du_b, h2, dh3, mix, sc2, g1, w2)


def _norm1_bwd(dh1, xall, dh2, w1, mod, n_lat_tiles, tr):
    R = xall.shape[0]

    def body(dh1_ref, x_ref, dh2_ref, w_ref, mod_ref, dx_ref, acc_ref):
        i = pl.program_id(0)

        @pl.when((i == 0) | (i == n_lat_tiles))
        def _():
            acc_ref[...] = jnp.zeros_like(acc_ref)

        dh1 = dh1_ref[...]
        x = x_ref[...]
        rstd = _rms(x)
        n = x * rstd
        dnw = dh1 * (1.0 + mod_ref[0, 1])
        dx_ref[...] = dh2_ref[...] + _rms_bwd(dnw * w_ref[...], n, rstd)
        acc_ref[0, 0:1, :] += _colsum(dh1 * n * w_ref[...])
        acc_ref[0, 1:2, :] += _colsum(dh1)
        acc_ref[0, 2:3, :] += _colsum(dnw * n)

    sel = lambda i: jnp.where(i >= n_lat_tiles, 1, 0)
    return pl.pallas_call(
        body, name="norm1_bwd", grid=(R // tr,),
        in_specs=[_row_spec(tr, D)] * 3 + [_vec_spec(D), pl.BlockSpec((1, 2, 1, D), lambda i: (sel(i), 0, 0, 0))],
        out_specs=[_row_spec(tr, D), pl.BlockSpec((1, 8, D), lambda i: (sel(i), 0, 0))],
        out_shape=[jax.ShapeDtypeStruct((R, D), F32), jax.ShapeDtypeStruct((2, 8, D), F32)],
        compiler_params=_cparams(("arbitrary",)),
    )(dh1, xall, dh2, w1, mod)


def _swiglu_act(gp, up, tr):
    R = gp.shape[0]

    def body(g_ref, u_ref, o_ref):
        o_ref[...] = (_silu(g_ref[...]) * u_ref[...]).astype(BF16)

    return pl.pallas_call(
        body, name="swiglu_act", grid=(R // tr,), in_specs=[_row_spec(tr, D_FF)] * 2, out_specs=_row_spec(tr, D_FF),
        out_shape=jax.ShapeDtypeStruct((R, D_FF), BF16), compiler_params=_cparams(("arbitrary",)),
    )(gp, up)


def _swiglu_act_bwd(da, gp, up, tr):
    R = gp.shape[0]

    def body(da_ref, g_ref, u_ref, dg_ref, du_ref):
        da, g = da_ref[...], g_ref[...]
        dg_ref[...] = (da * u_ref[...] * _dsilu(g)).astype(BF16)
        du_ref[...] = (da * _silu(g)).astype(BF16)

    return pl.pallas_call(
        body, name="swiglu_act_bwd", grid=(R // tr,), in_specs=[_row_spec(tr, D_FF)] * 3, out_specs=[_row_spec(tr, D_FF)] * 2,
        out_shape=[jax.ShapeDtypeStruct((R, D_FF), BF16)] * 2, compiler_params=_cparams(("arbitrary",)),
    )(da, gp, up)


def _merge(ya, yb, parts, tr):
    R = ya.shape[0]

    def body(ya_ref, yb_ref, ga_ref, gb_ref, o_ref):
        o_ref[...] = (_sigmoid(ga_ref[...]) * ya_ref[...] + _sigmoid(gb_ref[...]) * yb_ref[...]).astype(BF16)

    return pl.pallas_call(
        body, name="merge", grid=(R // tr,),
        in_specs=[_row_spec(tr, D), _row_spec(tr, D), _row_spec(tr, D, COL_GA // D), _row_spec(tr, D, COL_GB // D)],
        out_specs=_row_spec(tr, D), out_shape=jax.ShapeDtypeStruct((R, D), BF16), compiler_params=_cparams(("arbitrary",)),
    )(ya, yb, parts, parts)


def _merge_bwd(dm, ya, yb, parts, tr):
    R = ya.shape[0]

    def body(dm_ref, ya_ref, yb_ref, ga_ref, gb_ref, dya_ref, dyb_ref, dg_ref):
        dm = dm_ref[...]
        sa, sb = _sigmoid(ga_ref[...]), _sigmoid(gb_ref[...])
        dya_ref[...] = (dm * sa).astype(BF16)
        dyb_ref[...] = (dm * sb).astype(BF16)
        dg_ref[:, 0:D] = (dm * ya_ref[...] * sa * (1.0 - sa)).astype(BF16)
        dg_ref[:, D:2 * D] = (dm * yb_ref[...] * sb * (1.0 - sb)).astype(BF16)

    return pl.pallas_call(
        body, name="merge_bwd", grid=(R // tr,),
        in_specs=[_row_spec(tr, D)] * 3 + [_row_spec(tr, D, COL_GA // D), _row_spec(tr, D, COL_GB // D)],
        out_specs=[_row_spec(tr, D), _row_spec(tr, D), _row_spec(tr, 2 * D)],
        out_shape=[jax.ShapeDtypeStruct((R, D), BF16)] * 2 + [jax.ShapeDtypeStruct((R, 2 * D), BF16)],
        compiler_params=_cparams(("arbitrary",)),
    )(dm, ya, yb, parts, parts)


def _gla_out(o2, parts, gw4, tr):
    R = parts.shape[0]

    def body(of_ref, ob_ref, r_ref, w_ref, out_ref):
        oa = of_ref[...] + ob_ref[...]
        sr = _silu(r_ref[...])
        for h in range(GLA_H):
            s = slice(h * GLA_DV, (h + 1) * GLA_DV)
            o = oa[:, s]
            out_ref[:, s] = (o * _rms(o) * w_ref[:, s] * sr[:, s]).astype(BF16)

    return pl.pallas_call(
        body, name="gla_out", grid=(R // tr,),
        in_specs=[_row_spec(tr, D), _row_spec(tr, D), _row_spec(tr, D, COL_R // D), _vec_spec(D)],
        out_specs=_row_spec(tr, D), out_shape=jax.ShapeDtypeStruct((R, D), BF16), compiler_params=_cparams(("arbitrary",)),
    )(o2[0], o2[1], parts, gw4)


def _gla_out_bwd(dout, o2, parts, gw4, tr):
    R = parts.shape[0]

    def body(d_ref, of_ref, ob_ref, r_ref, w_ref, do_ref, dr_ref, acc_ref):
        _zero_first(acc_ref)
        oa = of_ref[...] + ob_ref[...]
        r = r_ref[...]
        sr = _silu(r)
        dout = d_ref[...]
        for h in range(GLA_H):
            s = slice(h * GLA_DV, (h + 1) * GLA_DV)
            o = oa[:, s]
            rstd = _rms(o)
            n = o * rstd
            w = w_ref[:, s]
            dr_ref[:, s] = (dout[:, s] * n * w * _dsilu(r[:, s])).astype(BF16)
            dnw = dout[:, s] * sr[:, s]
            do_ref[:, s] = _rms_bwd(dnw * w, n, rstd)
            acc_ref[0:1, s] += _colsum(dnw * n)

    return pl.pallas_call(
        body, name="gla_out_bwd", grid=(R // tr,),
        in_specs=[_row_spec(tr, D), _row_spec(tr, D), _row_spec(tr, D), _row_spec(tr, D, COL_R // D), _vec_spec(D)],
        out_specs=[_row_spec(tr, D), _row_spec(tr, D), _acc_spec(D)],
        out_shape=[jax.ShapeDtypeStruct((R, D), F32), jax.ShapeDtypeStruct((R, D), BF16), jax.ShapeDtypeStruct((8, D), F32)],
        compiler_params=_cparams(("arbitrary",)),
    )(dout, o2[0], o2[1], parts, gw4)


SSM_GW = SSM_INNER // SSM_G


def _ssd_out(y2, xbc, parts, dskip, nw, tr):
    R = parts.shape[0]

    def body(yf_ref, yb_ref, x_ref, z_ref, ds_ref, w_ref, out_ref):
        ob = (yf_ref[...] + yb_ref[...] + ds_ref[...] * x_ref[...]) * _silu(z_ref[...])
        for g in range(SSM_G):
            s = slice(g * SSM_GW, (g + 1) * SSM_GW)
            o = ob[:, s]
            out_ref[:, s] = (o * _rms(o) * w_ref[:, s]).astype(BF16)

    return pl.pallas_call(
        body, name="ssd_out", grid=(R // tr,),
        in_specs=[_row_spec(tr, SSM_INNER)] * 3 + [_row_spec(tr, SSM_INNER, COL_Z // SSM_INNER),
                                                   _vec_spec(SSM_INNER), _vec_spec(SSM_INNER)],
        out_specs=_row_spec(tr, SSM_INNER), out_shape=jax.ShapeDtypeStruct((R, SSM_INNER), BF16),
        compiler_params=_cparams(("arbitrary",)),
    )(y2[0], y2[1], xbc, parts, dskip, nw)


def _ssd_out_bwd(dout, y2, xbc, parts, dskip, nw, tr):
    R = parts.shape[0]

    def body(d_ref, yf_ref, yb_ref, x_ref, z_ref, ds_ref, w_ref, dy_ref, dz_ref, dxs_ref, acc_ref):
        _zero_first(acc_ref)
        x, z = x_ref[...], z_ref[...]
        pre = yf_ref[...] + yb_ref[...] + ds_ref[...] * x
        sz = _silu(z)
        ob = pre * sz
        dout = d_ref[...]
        for g in range(SSM_G):
            s = slice(g * SSM_GW, (g + 1) * SSM_GW)
            o = ob[:, s]
            rstd = _rms(o)
            n = o * rstd
            dob = _rms_bwd(dout[:, s] * w_ref[:, s], n, rstd)
            dz_ref[:, s] = (dob * pre[:, s] * _dsilu(z[:, s])).astype(BF16)
            dy = dob * sz[:, s]
            dy_ref[:, s] = dy
            dxs_ref[:, s] = dy * ds_ref[:, s]
            acc_ref[0:1, s] += _colsum(dout[:, s] * n)
            acc_ref[1:2, s] += _colsum(dy * x[:, s])

    return pl.pallas_call(
        body, name="ssd_out_bwd", grid=(R // tr,),
        in_specs=[_row_spec(tr, SSM_INNER)] * 4 + [_row_spec(tr, SSM_INNER, COL_Z // SSM_INNER),
                                                   _vec_spec(SSM_INNER), _vec_spec(SSM_INNER)],
        out_specs=[_row_spec(tr, SSM_INNER), _row_spec(tr, SSM_INNER), _row_spec(tr, SSM_INNER), _acc_spec(SSM_INNER)],
        out_shape=[jax.ShapeDtypeStruct((R, SSM_INNER), F32), jax.ShapeDtypeStruct((R, SSM_INNER), BF16),
                   jax.ShapeDtypeStruct((R, SSM_INNER), F32), jax.ShapeDtypeStruct((8, SSM_INNER), F32)],
        compiler_params=_cparams(("arbitrary",)),
    )(dout, y2[0], y2[1], xbc, parts, dskip, nw)


CONV_W = SSM_INNER + 2 * SSM_G * SSM_N
CONV_BLK = 1024


def _conv_masks(tr, is_ctx):
    t = lax.broadcasted_iota(jnp.int32, (tr, 1), 0)
    pos = jnp.where(is_ctx, t, t & (GRID_W - 1))
    seg = jnp.where(is_ctx, tr, GRID_W)
    return pos, seg


def _shift_rows(u, s, tr):
    return u if s == 0 else pltpu.roll(u, (-s) % tr, 0)


def _conv_fwd(parts, cw, cb, n_lat_tiles, tr):
    R = parts.shape[0]

    def body(u_ref, w_ref, b_ref, o_ref):
        pos, seg = _conv_masks(tr, pl.program_id(0) >= n_lat_tiles)
        u = u_ref[...]
        acc = jnp.zeros_like(u) + b_ref[...]
        for j in range(SSM_CONV):
            s = j - CONV_LEFT
            ok = (pos + s >= 0) & (pos + s < seg)
            acc = acc + jnp.where(ok, _shift_rows(u, s, tr), 0.0) * w_ref[j:j + 1, :]
        o_ref[...] = _silu(acc)

    return pl.pallas_call(
        body, name="conv_fwd", grid=(R // tr, CONV_W // CONV_BLK),
        in_specs=[pl.BlockSpec((tr, CONV_BLK), lambda i, j: (i, COL_XBC // CONV_BLK + j)),
                  pl.BlockSpec((SSM_CONV, CONV_BLK), lambda i, j: (0, j)), pl.BlockSpec((1, CONV_BLK), lambda i, j: (0, j))],
        out_specs=pl.BlockSpec((tr, CONV_BLK), lambda i, j: (i, j)), out_shape=jax.ShapeDtypeStruct((R, CONV_W), F32),
        compiler_params=_cparams(("arbitrary", "arbitrary")),
    )(parts, cw, cb)


def _conv_bwd(dxbc, parts, cw, cb, n_lat_tiles, tr):
    R = parts.shape[0]

    def body(d_ref, u_ref, w_ref, b_ref, du_ref, acc_ref):
        @pl.when(pl.program_id(1) == 0)
        def _():
            acc_ref[...] = jnp.zeros_like(acc_ref)

        pos, seg = _conv_masks(tr, pl.program_id(1) >= n_lat_tiles)
        u = u_ref[...]
        pre = jnp.zeros_like(u) + b_ref[...]
        taps = []
        for j in range(SSM_CONV):
            s = j - CONV_LEFT
            ok = (pos + s >= 0) & (pos + s < seg)
            tap = jnp.where(ok, _shift_rows(u, s, tr), 0.0)
            taps.append(tap)
            pre = pre + tap * w_ref[j:j + 1, :]
        dpre = d_ref[...] * _dsilu(pre)
        du = jnp.zeros_like(u)
        for j in range(SSM_CONV):
            s = j - CONV_LEFT
            acc_ref[j:j + 1, :] += _colsum(dpre * taps[j])
            ok = (pos - s >= 0) & (pos - s < seg)
            du = du + jnp.where(ok, _shift_rows(dpre, -s, tr), 0.0) * w_ref[j:j + 1, :]
        acc_ref[SSM_CONV:SSM_CONV + 1, :] += _colsum(dpre)
        du_ref[...] = du.astype(BF16)

    return pl.pallas_call(
        body, name="conv_bwd", grid=(CONV_W // CONV_BLK, R // tr),
        in_specs=[pl.BlockSpec((tr, CONV_BLK), lambda j, i: (i, j)),
                  pl.BlockSpec((tr, CONV_BLK), lambda j, i: (i, COL_XBC // CONV_BLK + j)),
                  pl.BlockSpec((SSM_CONV, CONV_BLK), lambda j, i: (0, j)), pl.BlockSpec((1, CONV_BLK), lambda j, i: (0, j))],
        out_specs=[pl.BlockSpec((tr, CONV_BLK), lambda j, i: (i, j)), pl.BlockSpec((8, CONV_BLK), lambda j, i: (0, j))],
        out_shape=[jax.ShapeDtypeStruct((R, CONV_W), BF16), jax.ShapeDtypeStruct((8, CONV_W), F32)],
        compiler_params=_cparams(("arbitrary", "arbitrary")),
    )(dxbc, parts, cw, cb)


def _chunk_row_block(d, i, n_lat, n_ctx):
    fwd = jnp.where(i < n_ctx, n_lat + i, i - n_ctx)
    rev = n_lat + n_ctx - 1 - i
    if isinstance(d, int):
        return rev if d else fwd
    return jnp.where(d == 0, fwd, rev)


def _tri(n, d, transpose=False):
    row = lax.broadcasted_iota(jnp.int32, (n, n), 0)
    col = lax.broadcasted_iota(jnp.int32, (n, n), 1)
    diff = (col - row) if transpose else (row - col)
    return diff * (1 - 2 * d) >= 0


def _gla_gates(sm, uhi, ulo, bias, d):
    pre = _nn3(sm, uhi, ulo) + bias
    g = _log_sigmoid(pre) * (1.0 / GLA_TAU)
    mask = _tri(GLA_C, d)
    b = _x_nn(mask.astype(BF16), g)
    b_tot = _colsum(g)
    b_ref = b[GLA_C // 2:GLA_C // 2 + 1, :]
    e_q = jnp.exp(jnp.minimum(b - b_ref, EXP_CLAMP))
    e_k = jnp.exp(jnp.minimum(b_ref - b, EXP_CLAMP))
    return pre, mask, b_tot, e_q, e_k, jnp.exp(b), jnp.exp(b_tot - b)


GLA_QK = GLA_H * GLA_DK
GLA_V = GLA_H * GLA_DV


def _gla_specs(n_lat, n_ctx, step_of):
    rbs = [lambda i, d=d: _chunk_row_block(d, step_of(i), n_lat, n_ctx) for d in range(2)]
    specs = []
    for rb in rbs:
        specs += [pl.BlockSpec((GLA_C, GLA_QK), lambda i, rb=rb: (rb(i), COL_Q // GLA_QK)),
                  pl.BlockSpec((GLA_C, GLA_QK), lambda i, rb=rb: (rb(i), COL_K // GLA_QK)),
                  pl.BlockSpec((GLA_C, GLA_V), lambda i, rb=rb: (rb(i), COL_V // GLA_V)),
                  pl.BlockSpec((GLA_C, 128), lambda i, rb=rb: (rb(i), COL_SM // 128))]
    specs += [pl.BlockSpec((2, 128, GLA_QK), lambda i: (0, 0, 0)), pl.BlockSpec((2, 128, GLA_QK), lambda i: (0, 0, 0)),
              pl.BlockSpec((2, 1, GLA_QK), lambda i: (0, 0, 0))]
    return specs, rbs


def _gla_fwd(parts, uhi, ulo, bias, n_lat, n_ctx):
    R = parts.shape[0]
    n_steps = n_lat + n_ctx
    scale = GLA_DK ** -0.5

    def body(*refs):
        ins, (uhi_ref, ulo_ref, bias_ref), o_refs, hist_ref, st = refs[:8], refs[8:11], refs[11:13], refs[13], refs[14]

        @pl.when(pl.program_id(0) == 0)
        def _():
            st[...] = jnp.zeros_like(st)

        for d in range(2):
            q_ref, k_ref, v_ref, sm_ref = ins[4 * d:4 * d + 4]
            _, mask, b_tot, e_q, e_k, e_in, e_out = _gla_gates(sm_ref[...], uhi_ref[d], ulo_ref[d], bias_ref[d], d)
            q, k, v = q_ref[...] * scale, k_ref[...], _bf(v_ref[...])
            qb, kb, q_in, k_out, decay = _bf(q * e_q), _bf(k * e_k), _bf(q * e_in), _bf(k * e_out), jnp.exp(b_tot)
            for h in range(GLA_H):
                sk, sv = slice(h * GLA_DK, (h + 1) * GLA_DK), slice(h * GLA_DV, (h + 1) * GLA_DV)
                att = jnp.where(mask, _nt(qb[:, sk], kb[:, sk]), 0.0)
                s_in = st[d, h]
                hist_ref[d, 0, h] = s_in
                o_refs[d][:, sv] = _nn(_bf(att), v[:, sv]) + _nt(q_in[:, sk], _bf(s_in))
                st[d, h] = decay[:, sk] * s_in + _tn(v[:, sv], k_out[:, sk])

    in_specs, rbs = _gla_specs(n_lat, n_ctx, lambda i: i)
    return pl.pallas_call(
        body, name="gla_fwd", grid=(n_steps,), in_specs=in_specs,
        out_specs=[pl.BlockSpec((GLA_C, GLA_V), lambda i, rb=rb: (rb(i), 0)) for rb in rbs]
        + [pl.BlockSpec((2, 1, GLA_H, GLA_DV, GLA_DK), lambda i: (0, i, 0, 0, 0))],
        out_shape=[jax.ShapeDtypeStruct((R, GLA_V), F32)] * 2 + [jax.ShapeDtypeStruct((2, n_steps, GLA_H, GLA_DV, GLA_DK), F32)],
        scratch_shapes=[pltpu.VMEM((2, GLA_H, GLA_DV, GLA_DK), F32)],
        compiler_params=_cparams(("arbitrary",)),
    )(*([parts] * 8), uhi, ulo, bias)


def _gla_bwd(do, parts, uhi, ulo, bias, hist, n_lat, n_ctx):
    R = parts.shape[0]
    n_steps = n_lat + n_ctx
    scale = GLA_DK ** -0.5
    step_of = lambda j: n_steps - 1 - j

    def body(*refs):
        ins, (uhi_ref, ulo_ref, bias_ref), do_refs, hist_ref = refs[:8], refs[8:11], refs[11:13], refs[13]
        outs, dst = refs[14:22], refs[22]

        @pl.when(pl.program_id(0) == 0)
        def _():
            dst[...] = jnp.zeros_like(dst)

        for d in range(2):
            q_ref, k_ref, v_ref, sm_ref = ins[4 * d:4 * d + 4]
            dq_ref, dk_ref, dv_ref, dp_ref = outs[4 * d:4 * d + 4]
            pre, mask, b_tot, e_q, e_k, e_in, e_out = _gla_gates(sm_ref[...], uhi_ref[d], ulo_ref[d], bias_ref[d], d)
            q, k, v = q_ref[...] * scale, k_ref[...], _bf(v_ref[...])
            dout = _bf(do_refs[d][...])
            k_out_f = k * e_out
            qb, kb, q_in, k_out, decay = _bf(q * e_q), _bf(k * e_k), _bf(q * e_in), _bf(k_out_f), jnp.exp(b_tot)
            dqs, dks, dk_outs, dss = [], [], [], []
            for h in range(GLA_H):
                sk, sv = slice(h * GLA_DK, (h + 1) * GLA_DK), slice(h * GLA_DV, (h + 1) * GLA_DV)
                s_in, ds = hist_ref[d, 0, h], dst[d, h]
                att = jnp.where(mask, _nt(qb[:, sk], kb[:, sk]), 0.0)
                datt = _bf(jnp.where(mask, _nt(dout[:, sv], v[:, sv]), 0.0))
                dv_ref[:, sv] = _tn(_bf(att), dout[:, sv]) + _nt(k_out[:, sk], _bf(ds))
                dqs.append(_nn(datt, kb[:, sk]) * e_q[:, sk] + _nn(dout[:, sv], _bf(s_in)) * e_in[:, sk])
                dk_o = _nn(v[:, sv], _bf(ds))
                dk_outs.append(dk_o)
                dks.append(_tn(datt, qb[:, sk]) * e_k[:, sk])
                dss.append(_colsum(ds * s_in))
                dst[d, h] = decay[:, sk] * ds + _tn(dout[:, sv], q_in[:, sk])
            dq, dk_out = jnp.concatenate(dqs, axis=1), jnp.concatenate(dk_outs, axis=1)
            dk = jnp.concatenate(dks, axis=1) + dk_out * e_out
            dq_ref[...] = dq * scale
            dk_ref[...] = dk
            db_tot = _colsum(dk_out * k_out_f) + decay * jnp.concatenate(dss, axis=1)
            dg = _x_nn(_tri(GLA_C, d, transpose=True).astype(BF16), dq * q - dk * k) + db_tot
            dp_ref[...] = dg * (1.0 / GLA_TAU) * _sigmoid(-pre)

    in_specs, rbs = _gla_specs(n_lat, n_ctx, step_of)
    in_specs += [pl.BlockSpec((GLA_C, GLA_V), lambda j, rb=rb: (rb(j), 0)) for rb in rbs]
    in_specs += [pl.BlockSpec((2, 1, GLA_H, GLA_DV, GLA_DK), lambda j: (0, step_of(j), 0, 0, 0))]
    out_specs, out_shape = [], []
    for rb in rbs:
        for w in (GLA_QK, GLA_QK, GLA_V, GLA_QK):
            out_specs.append(pl.BlockSpec((GLA_C, w), lambda j, rb=rb: (rb(j), 0)))
            out_shape.append(jax.ShapeDtypeStruct((R, w), F32))
    outs = pl.pallas_call(
        body, name="gla_bwd", grid=(n_steps,), in_specs=in_specs, out_specs=out_specs, out_shape=out_shape,
        scratch_shapes=[pltpu.VMEM((2, GLA_H, GLA_DV, GLA_DK), F32)],
        compiler_params=_cparams(("arbitrary",)),
    )(*([parts] * 8), uhi, ulo, bias, do, do, hist)
    return [(outs[k], outs[4 + k]) for k in range(4)]


def _ssd_consts(dt_bias, a_log):
    sel = np.zeros((2, SSM_G, 128, 128), np.float32)
    for d, base in enumerate((SM_DTF, SM_DTB)):
        for g in range(SSM_G):
            for e in range(SSM_HPG):
                sel[d, g, base + SSM_HPG * g + e, e] = 1.0
    e512 = np.zeros((128, SSM_GW), np.float32)
    for e in range(SSM_HPG):
        e512[e, SSM_P * e:SSM_P * (e + 1)] = 1.0
    a_neg = -jnp.exp(a_log)
    pad = lambda v: jnp.pad(v.reshape(2, SSM_G, 1, SSM_HPG), ((0, 0), (0, 0), (0, 0), (0, 128 - SSM_HPG)))
    return dict(
        sel=jnp.asarray(sel, BF16), sel_t=jnp.asarray(sel.transpose(0, 1, 3, 2), BF16), e512_t=jnp.asarray(e512.T, BF16),
        dtb=pad(dt_bias), a=pad(a_neg), a512=jnp.repeat(a_neg, SSM_P, axis=1).reshape(2, SSM_G, 1, SSM_GW))


def _head_columns(x8):
    return [jnp.broadcast_to(x8[:, e:e + 1], (x8.shape[0], 128)) for e in range(SSM_HPG)]


def _head_layout(cols):
    low = lax.broadcasted_iota(jnp.int32, (1, 128), 1) < SSM_P
    return jnp.concatenate([jnp.where(low, cols[2 * j], cols[2 * j + 1]) for j in range(SSM_HPG // 2)], axis=1)


def _ssd_common(sm, sel, dtb, a_neg, a512, d):
    dtr8 = _nn_x(sm, sel) + dtb
    dt8 = _softplus(dtr8)
    a8 = a_neg * dt8
    mask = _tri(SSM_C, d)
    mask_t = _tri(SSM_C, d, transpose=True).astype(BF16)
    cum8 = _x_nn(mask.astype(BF16), a8)
    a_hi, a_lo = _split(a8)
    cum_t = _tn(a_hi, mask_t) + _tn(a_lo, mask_t)
    cum_cols = _head_columns(cum8)
    dt_exp = _head_layout(_head_columns(dt8))
    a_exp = a512 * dt_exp
    return dict(dtr8=dtr8, a8=a8, mask=mask, mask_t=mask_t, cum_t=cum_t, dt_exp=dt_exp, a_exp=a_exp,
                cum_exp=_head_layout(cum_cols), cum_cols=cum_cols, tot_exp=_colsum(a_exp))


def _ssd_decay(cm, e):
    diff = cm["cum_cols"][e] - cm["cum_t"][e:e + 1, :]
    return jnp.where(cm["mask"], jnp.exp(jnp.minimum(diff, 0.0)), 0.0)


SSM_GPS = 2


def _ssd_specs(n_lat, n_ctx, step_of):
    rbs = [lambda i, d=d: _chunk_row_block(d, step_of(i), n_lat, n_ctx) for d in range(2)]
    xw, nw = SSM_GPS * SSM_GW, SSM_GPS * SSM_N
    specs = []
    for rb in rbs:
        specs += [pl.BlockSpec((SSM_C, xw), lambda g, i, rb=rb: (rb(i), g)),
                  pl.BlockSpec((SSM_C, nw), lambda g, i, rb=rb: (rb(i), SSM_INNER // nw + g)),
                  pl.BlockSpec((SSM_C, nw), lambda g, i, rb=rb: (rb(i), (SSM_INNER + SSM_G * SSM_N) // nw + g)),
                  pl.BlockSpec((SSM_C, 128), lambda g, i, rb=rb: (rb(i), COL_SM // 128))]
    specs += [pl.BlockSpec((2, SSM_GPS, 128, 128), lambda g, i: (0, g, 0, 0)),
              pl.BlockSpec((2, SSM_GPS, 1, 128), lambda g, i: (0, g, 0, 0)),
              pl.BlockSpec((2, SSM_GPS, 1, 128), lambda g, i: (0, g, 0, 0)),
              pl.BlockSpec((2, SSM_GPS, 1, SSM_GW), lambda g, i: (0, g, 0, 0))]
    return specs, rbs


def _ssd_fwd(xbc, parts, k, n_lat, n_ctx, comm):
    R = parts.shape[0]
    n_steps = n_lat + n_ctx

    def body(*refs):
        ins, (sel_ref, dtb_ref, a_ref, a512_ref), y_refs, hist_ref, st = refs[:8], refs[8:12], refs[12:14], refs[14], refs[15]

        @pl.when(pl.program_id(1) == 0)
        def _():
            st[...] = jnp.zeros_like(st)

        for d in range(2):
            x_ref, b_ref, c_ref, sm_ref = ins[4 * d:4 * d + 4]
            sm = sm_ref[...]
            for gg in range(SSM_GPS):
                sx, sn = slice(gg * SSM_GW, (gg + 1) * SSM_GW), slice(gg * SSM_N, (gg + 1) * SSM_N)
                cm = _ssd_common(sm, sel_ref[d, gg], dtb_ref[d, gg], a_ref[d, gg], a512_ref[d, gg], d)
                bm, cmat = _bf(b_ref[:, sn]), _bf(c_ref[:, sn])
                xdt = x_ref[:, sx] * cm["dt_exp"]
                cb = _nt(cmat, bm)
                ys = [_nn(_bf(cb * _ssd_decay(cm, e)), _bf(xdt[:, SSM_P * e:SSM_P * (e + 1)])) for e in range(SSM_HPG)]
                s_in = st[d, gg]
                hist_ref[d, 0, gg] = s_in
                y_refs[d][:, sx] = jnp.concatenate(ys, axis=1) + jnp.exp(cm["cum_exp"]) * _nn(cmat, _bf(s_in))
                st[d, gg] = jnp.exp(cm["tot_exp"]) * s_in + _tn(bm, _bf(xdt * jnp.exp(cm["tot_exp"] - cm["cum_exp"])))

    in_specs, rbs = _ssd_specs(n_lat, n_ctx, lambda i: i)
    out_specs = [pl.BlockSpec((SSM_C, SSM_GPS * SSM_GW), lambda g, i, rb=rb: (rb(i), g)) for rb in rbs]
    out_specs += [pl.BlockSpec((2, 1, SSM_GPS, SSM_N, SSM_GW), lambda g, i: (0, i, g, 0, 0))]
    out_shape = [jax.ShapeDtypeStruct((R, SSM_INNER), F32)] * 2 + [jax.ShapeDtypeStruct((2, n_steps, SSM_G, SSM_N, SSM_GW), F32)]
    args = [xbc, xbc, xbc, parts] * 2 + [k["sel"], k["dtb"], k["a"], k["a512"]]
    n_host_out = len(out_shape)
    outs = pl.pallas_call(
        _hosted(body, len(args), n_host_out, 1, comm, *_ssd_comm_steps(n_steps)), name="ssd_fwd",
        grid=(SSM_G // SSM_GPS, n_steps), in_specs=in_specs + [_ANY] * len(comm.arrays),
        out_specs=out_specs + [_ANY] * len(comm.out_shape), out_shape=out_shape + comm.out_shape,
        scratch_shapes=[pltpu.VMEM((2, SSM_GPS, SSM_N, SSM_GW), F32)] + comm.scratch,
        compiler_params=_cparams(("arbitrary", "arbitrary")),
    )(*args, *comm.arrays)
    return outs[:n_host_out], outs[n_host_out:]


def _ssd_comm_steps(n_steps):
    n_g = SSM_G // SSM_GPS
    at = lambda g, i: (pl.program_id(0) == g) & (pl.program_id(1) == i)
    return (lambda: at(0, 0)), (lambda: at(n_g // 2, 0)), (lambda: at(n_g - 1, n_steps - 1))


def _ssd_bwd(dy, xbc, parts, k, hist, n_lat, n_ctx, comm):
    R = parts.shape[0]
    n_steps = n_lat + n_ctx
    step_of = lambda j: n_steps - 1 - j

    def one(d, gg, x_ref, b_ref, c_ref, sm_ref, sel_ref, dtb_ref, a_ref, a512_ref, selt_ref, e512t_ref, dy_ref,
            hist_ref, dx_ref, db_ref, dc_ref, dsm_ref, acc_ref, dst):
        sx, sn = slice(gg * SSM_GW, (gg + 1) * SSM_GW), slice(gg * SSM_N, (gg + 1) * SSM_N)
        a_neg, e512_t = a_ref[d, gg], e512t_ref[...]
        cm = _ssd_common(sm_ref[...], sel_ref[d, gg], dtb_ref[d, gg], a_neg, a512_ref[d, gg], d)
        x, dyv = x_ref[:, sx], dy_ref[:, sx]
        bm, cmat = _bf(b_ref[:, sn]), _bf(c_ref[:, sn])
        xdt = x * cm["dt_exp"]
        cb = _nt(cmat, bm)
        s_in, ds = hist_ref[d, 0, gg], dst[d, gg]
        w = jnp.exp(cm["tot_exp"] - cm["cum_exp"])
        z = _nn(bm, _bf(ds))
        decay_in = jnp.exp(cm["cum_exp"])
        gy = _bf(dyv * decay_in)
        dcb = jnp.zeros((SSM_C, SSM_C), F32)
        dxs, crossing = [], []
        row = lax.broadcasted_iota(jnp.int32, (SSM_C, SSM_C), 0)
        col = lax.broadcasted_iota(jnp.int32, (SSM_C, SSM_C), 1)
        eye = (row == col).astype(BF16)
        before = (cm["mask_t"] - eye)
        for e in range(SSM_HPG):
            s = slice(SSM_P * e, SSM_P * (e + 1))
            lm = _ssd_decay(cm, e)
            dy_e = _bf(dyv[:, s])
            m_e = cb * lm
            dm_e = _nt(dy_e, _bf(xdt[:, s]))
            dcb = dcb + dm_e * lm
            dxs.append(_tn(_bf(m_e), dy_e))
            through = jnp.where(cm["mask"], _nn(_bf(dm_e * m_e), before), 0.0)
            crossing.append(_colsum(through))
        da_rows = jnp.concatenate(crossing + [jnp.zeros((128 - SSM_HPG, SSM_C), F32)], axis=0)
        r_hi, r_lo = _split(da_rows)
        da8_intra = _tn(r_hi, eye) + _tn(r_lo, eye)
        dx_state = w * z
        dxdt = jnp.concatenate(dxs, axis=1) + dx_state
        dcb = _bf(dcb)
        c_s = _nn(cmat, _bf(s_in))
        dc_ref[:, sn] = _nn(dcb, bm) + _nt(gy, _bf(s_in))
        db_ref[:, sn] = _tn(dcb, cmat) + _nt(_bf(w * xdt), _bf(ds))
        dst[d, gg] = jnp.exp(cm["tot_exp"]) * ds + _tn(cmat, gy)
        state_path = xdt * dx_state
        per_token = _nn_x(jnp.concatenate([dyv * decay_in * c_s - state_path, dxdt * x], axis=0), e512_t)
        totals = jnp.concatenate([_colsum(state_path), _colsum(ds * s_in), jnp.zeros((6, SSM_GW), F32)], axis=0)
        totals = _nn_x(totals, e512_t)
        tot8 = _colsum(cm["a8"])
        dtot8 = totals[0:1] + jnp.exp(tot8) * totals[1:2]
        da8 = da8_intra + _x_nn(cm["mask_t"], per_token[:SSM_C]) + dtot8
        ddt8 = da8 * a_neg + per_token[SSM_C:]
        dsm_ref[gg] = _nn_x(ddt8 * _sigmoid(cm["dtr8"]), selt_ref[d, gg])
        dx_ref[:, sx] = dxdt * cm["dt_exp"]
        acc_ref[d, gg, 0:1, :] += _colsum(da8 * cm["a8"])

    def body(*refs):
        ins, consts, (selt_ref, e512t_ref), dy_refs, hist_ref = refs[:8], refs[8:12], refs[12:14], refs[14:16], refs[16]
        outs, acc_ref, dst = refs[17:25], refs[25], refs[26]

        @pl.when(pl.program_id(1) == 0)
        def _():
            dst[...] = jnp.zeros_like(dst)
            acc_ref[...] = jnp.zeros_like(acc_ref)

        for d in range(2):
            for gg in range(SSM_GPS):
                one(d, gg, *ins[4 * d:4 * d + 4], *consts, selt_ref, e512t_ref, dy_refs[d], hist_ref,
                    *outs[4 * d:4 * d + 4], acc_ref, dst)

    xw, nw = SSM_GPS * SSM_GW, SSM_GPS * SSM_N
    in_specs, rbs = _ssd_specs(n_lat, n_ctx, step_of)
    in_specs += [pl.BlockSpec((2, SSM_GPS, 128, 128), lambda g, j: (0, g, 0, 0)), pl.BlockSpec((SSM_GW, 128), lambda g, j: (0, 0))]
    in_specs += [pl.BlockSpec((SSM_C, xw), lambda g, j, rb=rb: (rb(j), g)) for rb in rbs]
    in_specs += [pl.BlockSpec((2, 1, SSM_GPS, SSM_N, SSM_GW), lambda g, j: (0, step_of(j), g, 0, 0))]
    out_specs, out_shape = [], []
    for rb in rbs:
        out_specs += [pl.BlockSpec((SSM_C, xw), lambda g, j, rb=rb: (rb(j), g)),
                      pl.BlockSpec((SSM_C, nw), lambda g, j, rb=rb: (rb(j), g)),
                      pl.BlockSpec((SSM_C, nw), lambda g, j, rb=rb: (rb(j), g)),
                      pl.BlockSpec((SSM_GPS, SSM_C, 128), lambda g, j, rb=rb: (g, rb(j), 0))]
        out_shape += [jax.ShapeDtypeStruct((R, SSM_INNER), F32), jax.ShapeDtypeStruct((R, SSM_G * SSM_N), F32),
                      jax.ShapeDtypeStruct((R, SSM_G * SSM_N), F32), jax.ShapeDtypeStruct((SSM_G, R, 128), F32)]
    out_specs.append(pl.BlockSpec((2, SSM_GPS, 8, 128), lambda g, j: (0, g, 0, 0)))
    out_shape.append(jax.ShapeDtypeStruct((2, SSM_G, 8, 128), F32))
    args = [xbc, xbc, xbc, parts] * 2 + [k["sel"], k["dtb"], k["a"], k["a512"], k["sel_t"], k["e512_t"], dy, dy, hist]
    n_host_out = len(out_shape)
    outs = pl.pallas_call(
        _hosted(body, len(args), n_host_out, 1, comm, *_ssd_comm_steps(n_steps)), name="ssd_bwd",
        grid=(SSM_G // SSM_GPS, n_steps), in_specs=in_specs + [_ANY] * len(comm.arrays),
        out_specs=out_specs + [_ANY] * len(comm.out_shape), out_shape=out_shape + comm.out_shape,
        scratch_shapes=[pltpu.VMEM((2, SSM_GPS, SSM_N, SSM_GW), F32)] + comm.scratch,
        compiler_params=_cparams(("arbitrary", "arbitrary")),
    )(*args, *comm.arrays)
    return [(outs[n], outs[4 + n]) for n in range(4)] + [outs[8]], outs[n_host_out:]


def _gla_assemble(dq, dk, dv, tr):
    R = dq[0].shape[0]
    qk = GLA_H * GLA_DK

    def body(dqf_ref, dqb_ref, dkf_ref, dkb_ref, dvf_ref, dvb_ref, o_ref):
        o_ref[:, 0:qk] = (dqf_ref[...] + dqb_ref[...]).astype(BF16)
        o_ref[:, qk:2 * qk] = (dkf_ref[...] + dkb_ref[...]).astype(BF16)
        o_ref[:, 2 * qk:] = (dvf_ref[...] + dvb_ref[...]).astype(BF16)

    return pl.pallas_call(
        body, name="gla_assemble", grid=(R // tr,), in_specs=[_row_spec(tr, qk)] * 4 + [_row_spec(tr, D)] * 2,
        out_specs=_row_spec(tr, 2 * D),
        out_shape=jax.ShapeDtypeStruct((R, 2 * D), BF16), compiler_params=_cparams(("arbitrary",)),
    )(*dq, *dk, *dv)


def _xbc_assemble(dx, db, dc, dxs_skip, tr):
    R = dx[0].shape[0]
    bc = SSM_G * SSM_N

    def body(dxf_ref, dxb_ref, dbf_ref, dbb_ref, dcf_ref, dcb_ref, sk_ref, o_ref):
        o_ref[:, 0:SSM_INNER] = dxf_ref[...] + dxb_ref[...] + sk_ref[...]
        o_ref[:, SSM_INNER:SSM_INNER + bc] = dbf_ref[...] + dbb_ref[...]
        o_ref[:, SSM_INNER + bc:] = dcf_ref[...] + dcb_ref[...]

    return pl.pallas_call(
        body, name="xbc_assemble", grid=(R // tr,),
        in_specs=[_row_spec(tr, SSM_INNER)] * 2 + [_row_spec(tr, bc)] * 4 + [_row_spec(tr, SSM_INNER)],
        out_specs=_row_spec(tr, CONV_W), out_shape=jax.ShapeDtypeStruct((R, CONV_W), F32),
        compiler_params=_cparams(("arbitrary",)),
    )(*dx, *db, *dc, dxs_skip)


def _small_assemble(dp, dsm, parts, ut_hi, ut_lo, tr):
    R = parts.shape[0]
    qk = GLA_H * GLA_DK

    def body(dpf_ref, dpb_ref, dsmf_ref, dsmb_ref, sm_ref, uth_ref, utl_ref, o_ref, dup_ref, acc_ref, acc2_ref):
        @pl.when(pl.program_id(0) == 0)
        def _():
            dup_ref[...] = jnp.zeros_like(dup_ref)
            acc_ref[...] = jnp.zeros_like(acc_ref)
            acc2_ref[...] = jnp.zeros_like(acc2_ref)

        ssd = dsmf_ref[0] + dsmb_ref[0]
        for g in range(1, SSM_G):
            ssd = ssd + (dsmf_ref[g] + dsmb_ref[g])
        acc2_ref[0:1, :] += _colsum(ssd)
        sm_hi, sm_lo = _split(sm_ref[...])
        out = ssd
        for d, dp_ref in enumerate((dpf_ref, dpb_ref)):
            dpd = dp_ref[...]
            out = out + _nn3(dpd, uth_ref[d], utl_ref[d])
            p_hi, p_lo = _split(dpd)
            dup_ref[d] += _tn(sm_hi, p_hi) + _tn(sm_lo, p_hi) + _tn(sm_hi, p_lo)
            acc_ref[d:d + 1, :] += _colsum(dpd)
        o_ref[...] = out.astype(BF16)

    return pl.pallas_call(
        body, name="small_assemble", grid=(R // tr,),
        in_specs=[_row_spec(tr, qk)] * 2 + [pl.BlockSpec((SSM_G, tr, 128), lambda i: (0, i, 0))] * 2
        + [_row_spec(tr, 128, COL_SM // 128), pl.BlockSpec((2, qk, 128), lambda i: (0, 0, 0)),
           pl.BlockSpec((2, qk, 128), lambda i: (0, 0, 0))],
        out_specs=[_row_spec(tr, 128), pl.BlockSpec((2, 128, qk), lambda i: (0, 0, 0)), _acc_spec(qk), _acc_spec(128)],
        out_shape=[jax.ShapeDtypeStruct((R, 128), BF16), jax.ShapeDtypeStruct((2, 128, qk), F32),
                   jax.ShapeDtypeStruct((8, qk), F32), jax.ShapeDtypeStruct((8, 128), F32)],
        compiler_params=_cparams(("arbitrary",)),
    )(*dp, *dsm, parts, ut_hi, ut_lo)


ADA_ROWS = 16
ADA_TILE = 512


def _dot3_f32(a, b, ca, cb):
    a_hi, a_lo = _split(a)
    b_hi, b_lo = _split(b)
    return _dg(a_hi, b_hi, ca, cb) + _dg(a_lo, b_hi, ca, cb) + _dg(a_hi, b_lo, ca, cb)


def _ada_fwd(cvec, w, b):
    cols = w.shape[1]

    def body(c_ref, w_ref, b_ref, o_ref):
        o_ref[...] = _dot3_f32(_silu(c_ref[...]), w_ref[...], 1, 0) + b_ref[...]

    return pl.pallas_call(
        body, name="ada_fwd", grid=(cols // ADA_TILE,),
        in_specs=[pl.BlockSpec((ADA_ROWS, D), lambda j: (0, 0)), pl.BlockSpec((D, ADA_TILE), lambda j: (0, j)),
                  pl.BlockSpec((1, ADA_TILE), lambda j: (0, j))],
        out_specs=pl.BlockSpec((ADA_ROWS, ADA_TILE), lambda j: (0, j)), out_shape=jax.ShapeDtypeStruct((ADA_ROWS, cols), F32),
        compiler_params=_cparams(("arbitrary",)),
    )(cvec, w, b)


def _adam(w, g, m, v):
    m2 = ADAM_B1 * m + (1.0 - ADAM_B1) * g
    v2 = ADAM_B2 * v + (1.0 - ADAM_B2) * (g * g)
    m_hat = m2 / (1.0 - ADAM_B1 ** ADAM_STEP)
    v_hat = v2 / (1.0 - ADAM_B2 ** ADAM_STEP)
    return -ADAM_LR * (m_hat / (jnp.sqrt(v_hat) + ADAM_EPS) + ADAM_WD * w), m2, v2


def _wada_bwd_adam(cvec, dada, w, m, v):
    rows, cols = w.shape
    tr = _tile(rows, 256, 128)

    def body(c_ref, d_ref, w_ref, m_ref, v_ref, g_ref, dl_ref, m2_ref, v2_ref, p_ref):
        wv = w_ref[...]
        g = _dot3_f32(_silu(c_ref[...]), d_ref[...], 0, 0)
        g_ref[...] = g
        dl_ref[...], m2_ref[...], v2_ref[...] = _adam(wv, g, m_ref[...], v_ref[...])
        p_ref[...] = _dot3_f32(d_ref[...], wv, 1, 1)

    blk = pl.BlockSpec((tr, cols), lambda i: (i, 0))
    return pl.pallas_call(
        body, name="wada_bwd_adam", grid=(rows // tr,),
        in_specs=[pl.BlockSpec((ADA_ROWS, tr), lambda i: (0, i)), pl.BlockSpec((ADA_ROWS, cols), lambda i: (0, 0)), blk, blk, blk],
        out_specs=[blk, blk, blk, blk, pl.BlockSpec((ADA_ROWS, tr), lambda i: (0, i))],
        out_shape=[jax.ShapeDtypeStruct((rows, cols), F32)] * 4 + [jax.ShapeDtypeStruct((ADA_ROWS, rows), F32)],
        compiler_params=_cparams(("arbitrary",)),
    )(cvec, dada, w, m, v)


def _reduce_adam(parts8, w, m, v, name):
    rows, cols = w.shape
    tr = _tile(rows, 64, 16)

    def body(p_ref, w_ref, m_ref, v_ref, g_ref, dl_ref, m2_ref, v2_ref):
        g = p_ref[0].astype(F32) + p_ref[N_CHIPS].astype(F32)
        for j in range(1, N_CHIPS):
            g = g + (p_ref[j].astype(F32) + p_ref[N_CHIPS + j].astype(F32))
        g_ref[...] = g
        dl_ref[...], m2_ref[...], v2_ref[...] = _adam(w_ref[...], g, m_ref[...], v_ref[...])

    blk = pl.BlockSpec((tr, cols), lambda i: (i, 0))
    return pl.pallas_call(
        body, name=name, grid=(rows // tr,), in_specs=[pl.BlockSpec((N_DEV, tr, cols), lambda i: (0, i, 0)), blk, blk, blk],
        out_specs=[blk] * 4, out_shape=[jax.ShapeDtypeStruct((rows, cols), F32)] * 4, compiler_params=_cparams(("arbitrary",)),
    )(parts8, w, m, v)


SMALL_W = 1024


def _sum8(g8):
    rows = g8.shape[1]

    def body(g_ref, o_ref):
        s = g_ref[0]
        for j in range(1, N_DEV):
            s = s + g_ref[j]
        o_ref[...] = s

    return pl.pallas_call(
        body, name="sum8", out_shape=jax.ShapeDtypeStruct((rows, SMALL_W), F32),
        in_specs=[pl.BlockSpec(memory_space=pltpu.VMEM)], out_specs=pl.BlockSpec(memory_space=pltpu.VMEM),
        compiler_params=pltpu.CompilerParams(vmem_limit_bytes=VMEM_LIMIT),
    )(g8)


def _cctx_grad(p8, c_ctx):
    def body(p_ref, c_ref, o_ref):
        s = p_ref[0]
        for chip in range(1, N_CHIPS):
            s = s + p_ref[2 * chip]
        o_ref[...] = s * _dsilu(c_ref[...])

    return pl.pallas_call(
        body, name="cctx_grad", out_shape=jax.ShapeDtypeStruct((1, D), F32),
        in_specs=[pl.BlockSpec(memory_space=pltpu.VMEM)] * 2, out_specs=pl.BlockSpec(memory_space=pltpu.VMEM),
    )(p8, c_ctx)


def _adam_small(w, g, m, v):
    def body(w_ref, g_ref, m_ref, v_ref, dl_ref, m2_ref, v2_ref):
        dl_ref[...], m2_ref[...], v2_ref[...] = _adam(w_ref[...], g_ref[...], m_ref[...], v_ref[...])

    vm = pl.BlockSpec(memory_space=pltpu.VMEM)
    return pl.pallas_call(
        body, name="adam_small", out_shape=[jax.ShapeDtypeStruct(w.shape, F32)] * 3, in_specs=[vm] * 4, out_specs=[vm] * 3,
        compiler_params=pltpu.CompilerParams(vmem_limit_bytes=VMEM_LIMIT),
    )(w, g, m, v)


def _pack(vecs, width=SMALL_W, row_mult=8):
    flat = jnp.concatenate([v.reshape(-1).astype(F32) for v in vecs])
    n = flat.shape[0]
    rows = -(-n // (width * row_mult)) * row_mult
    return jnp.pad(flat, (0, rows * width - n)).reshape(rows, width)


def _unpack(packed, shapes):
    flat = packed.reshape(-1)
    out, off = [], 0
    for s in shapes:
        n = int(np.prod(s))
        out.append(flat[off:off + n].reshape(s))
        off += n
    return out


WEIGHTS = ('c_ctx', 'w_ada', 'b_ada', 'norm1_w', 'w_in', 'gla_up_f', 'gla_bias_f', 'gla_up_b', 'gla_bias_b', 'gla_norm_w',
           'conv_w', 'conv_b', 'dt_bias_f', 'dt_bias_b', 'a_log_f', 'a_log_b', 'd_skip', 'ssm_norm_w', 'w_pa', 'w_pb', 'w_out',
           'norm2_w', 'w_gate', 'w_up', 'w_down', 'final_norm_w')
BIG = ('w_in', 'w_pa', 'w_pb', 'w_out', 'w_gate', 'w_up', 'w_down')
COL_SHARDED = ('w_in', 'w_gate', 'w_up')
SMALL_SHARDED = ('gla_up_f', 'gla_up_b', 'conv_w')
ROW_TILE = 256


def _blocks_to_full(g4, name):
    n, r, c = g4.shape
    return g4.transpose(1, 0, 2).reshape(r, n * c) if name in COL_SHARDED else g4.reshape(n * r, c)


def _full_to_blocks(full, name):
    r, c = full.shape
    if name in COL_SHARDED:
        return full.reshape(r, N_CHIPS, c // N_CHIPS).transpose(1, 0, 2)
    return full.reshape(N_CHIPS, r // N_CHIPS, c)


def _permute_in(w_in_full):
    off = np.concatenate([[0], np.cumsum(IN_WIDTHS)])
    cols = [w_in_full[:, off[p]:off[p + 1]] for p in PERM]
    return jnp.concatenate(cols + [jnp.zeros((w_in_full.shape[0], SMALL_PAD), w_in_full.dtype)], axis=1)


def _unpermute_in(wp):
    off = np.concatenate([[0], np.cumsum([IN_WIDTHS[p] for p in PERM])])
    pieces = {p: wp[:, off[i]:off[i + 1]] for i, p in enumerate(PERM)}
    return jnp.concatenate([pieces[p] for p in range(len(IN_WIDTHS))], axis=1)


def _chip_cols(full, chip, n):
    return lax.dynamic_slice_in_dim(full, chip * n, n, axis=1)


def kernel(x, c, ctx, c_ctx, w_ada, b_ada, norm1_w, w_in, gla_up_f, gla_bias_f, gla_up_b, gla_bias_b, gla_norm_w, conv_w, conv_b, dt_bias_f, dt_bias_b, a_log_f, a_log_b, d_skip, ssm_norm_w, w_pa, w_pb, w_out, norm2_w, w_gate, w_up, w_down, final_norm_w, loss_target, m_c_ctx, m_w_ada, m_b_ada, m_norm1_w, m_w_in, m_gla_up_f, m_gla_bias_f, m_gla_up_b, m_gla_bias_b, m_gla_norm_w, m_conv_w, m_conv_b, m_dt_bias_f, m_dt_bias_b, m_a_log_f, m_a_log_b, m_d_skip, m_ssm_norm_w, m_w_pa, m_w_pb, m_w_out, m_norm2_w, m_w_gate, m_w_up, m_w_down, m_final_norm_w, v_c_ctx, v_w_ada, v_b_ada, v_norm1_w, v_w_in, v_gla_up_f, v_gla_bias_f, v_gla_up_b, v_gla_bias_b, v_gla_norm_w, v_conv_w, v_conv_b, v_dt_bias_f, v_dt_bias_b, v_a_log_f, v_a_log_b, v_d_skip, v_ssm_norm_w, v_w_pa, v_w_pb, v_w_out, v_norm2_w, v_w_gate, v_w_up, v_w_down, v_final_norm_w):
    given = dict(locals())
    W = {n: given[n] for n in WEIGHTS}
    M = {n: given["m_" + n] for n in WEIGHTS}
    V = {n: given["v_" + n] for n in WEIGHTS}
    L, Lc = x.shape[1], ctx.shape[1]
    tr = ROW_TILE
    assert L % tr == 0 and Lc % tr == 0 and L % Lc == 0 and Lc % SSM_C == 0
    n_lat_tiles = L // tr
    xi, yi, ci = _place()
    chip, me = 2 * xi + yi, 4 * xi + 2 * yi + ci
    xall = jnp.concatenate([x[0], ctx[0]], axis=0)

    g0 = _allgather_small(_pack([c[0]] + [W[n][0] for n in SMALL_SHARDED]), "gather_c")
    g0 = g0.reshape(N_DEV, -1)
    c_all = g0[:, :D]
    small_full, off = {}, D
    for n in SMALL_SHARDED:
        r, cols = W[n].shape[1:]
        small_full[n] = jnp.concatenate([g0[2 * k, off:off + r * cols].reshape(r, cols) for k in range(N_CHIPS)], axis=1)
        off += r * cols
    up_f, up_b, conv_w_full = (small_full[n] for n in SMALL_SHARDED)

    cvec = jnp.zeros((ADA_ROWS, D), F32).at[:N_DEV].set(c_all).at[N_DEV].set(c_ctx)
    ada_cols = w_ada.shape[2]
    ada_part = _ada_fwd(cvec, w_ada[0], _chip_cols(b_ada, chip, ada_cols))
    g1_all = _allgather_small(ada_part, "gather_ada")
    ada_full = jnp.concatenate([g1_all[2 * k] for k in range(N_CHIPS)], axis=1)
    mine = lax.dynamic_slice_in_dim(ada_full, me, 1, axis=0)
    sh1, sc1, g1, sh2, sc2, g2 = (mine[:, k * D:(k + 1) * D] for k in range(6))
    csh1, csc1 = ada_full[N_DEV:N_DEV + 1, :D], ada_full[N_DEV:N_DEV + 1, D:2 * D]
    mod = jnp.stack([jnp.stack([sh1, sc1]), jnp.stack([csh1, csc1])])

    full = {'w_in': _blocks_to_full(_gather_split(w_in[0].astype(BF16), "gather_w_in"), 'w_in')}
    wp = _permute_in(full['w_in'])
    later = [n for n in BIG if n != 'w_in']

    def lr_rows(up, base):
        return jnp.zeros((128, GLA_H * GLA_DK), F32).at[base:base + GLA_RANK].set(up)
    u2 = jnp.stack([lr_rows(up_f, SM_LRF), lr_rows(up_b, SM_LRB)])
    u2_hi = u2.astype(BF16)
    u2_lo = (u2 - u2_hi.astype(F32)).astype(BF16)
    ut = u2.transpose(0, 2, 1)
    ut_hi = ut.astype(BF16)
    ut_lo = (ut - ut_hi.astype(F32)).astype(BF16)
    gbias = jnp.stack([gla_bias_f, gla_bias_b])
    kc = _ssd_consts(jnp.stack([dt_bias_f[0], dt_bias_b[0]]), jnp.stack([a_log_f[0], a_log_b[0]]))
    gw4 = jnp.tile(gla_norm_w, (1, GLA_H))
    dskip_exp = jnp.repeat(d_skip, SSM_P, axis=1)
    n_gla = (L // GLA_C, Lc // GLA_C)
    n_ssd = (L // SSM_C, Lc // SSM_C)

    h1 = _norm_mod(xall, norm1_w, mod, n_lat_tiles, tr)
    parts = _mm(h1, wp, "nn", F32, "mm_in", tm=768, tn=1152)
    xbc = _conv_fwd(parts, conv_w_full, conv_b, L // Lc, Lc)
    *o2, gla_hist = _gla_fwd(parts, u2_hi, u2_lo, gbias, *n_gla)
    (*y2, ssd_hist), gathered = _ssd_fwd(xbc, parts, kc, *n_ssd, _gather_comm([W[n][0].astype(BF16) for n in later]))
    full.update({n: _blocks_to_full(g, n) for n, g in zip(later, gathered)})
    oan = _gla_out(o2, parts, gw4, tr)
    obn = _ssd_out(y2, xbc, parts, dskip_exp, ssm_norm_w, tr)
    ya = _mm(oan, full['w_pa'], "nn", F32, "mm_pa", tm=768)
    yb = _mm(obn, full['w_pb'], "nn", F32, "mm_pb", tm=768)
    merged = _merge(ya, yb, parts, tr)
    mix = _mm(merged, full['w_out'], "nn", F32, "mm_out", tm=768)
    h2, u = _resid_norm_mod(xall, mix, g1, norm2_w, sh2, sc2, tr)
    gp = _mm(u, full['w_gate'], "nn", F32, "mm_gate", tm=768, tn=1408)
    up = _mm(u, full['w_up'], "nn", F32, "mm_up", tm=768, tn=1408)
    act = _swiglu_act(gp, up, tr)
    f = _mm(act, full['w_down'], "nn", F32, "mm_down", tm=768)
    dh3, df, acc_loss = _loss_head(h2, f, loss_target[0], g2, final_norm_w[None], n_lat_tiles, tr)

    dw = {}
    da = _mm(df, full['w_down'], "nt", F32, "mm_down_dx", tm=768, tn=1408)
    dw['w_down'] = _mm(act, df, "tn", BF16, "mm_down_dw", tm=1408, tk=768)
    dgp, dup = _swiglu_act_bwd(da, gp, up, tr)
    du_a = _mm(dgp, full['w_gate'], "nt", F32, "mm_gate_dx", tm=768, tk=1408)
    du_b = _mm(dup, full['w_up'], "nt", F32, "mm_up_dx", tm=768, tk=1408)
    dw['w_gate'] = _mm(u, dgp, "tn", BF16, "mm_gate_dw", tm=1024, tn=1408, tk=768)
    dw['w_up'] = _mm(u, dup, "tn", BF16, "mm_up_dw", tm=1024, tn=1408, tk=768)
    dh2, dmix, acc_ffn = _ffn_in_bwd(du_a, du_b, h2, dh3, mix, sc2, g1, norm2_w, tr)
    dmerged = _mm(dmix, full['w_out'], "nt", F32, "mm_out_dx", tm=768)
    dw['w_out'] = _mm(merged, dmix, "tn", BF16, "mm_out_dw", tm=1024, tk=768)
    dya, dyb, dgates = _merge_bwd(dmerged, ya, yb, parts, tr)
    doan = _mm(dya, full['w_pa'], "nt", F32, "mm_pa_dx", tm=768)
    dw['w_pa'] = _mm(oan, dya, "tn", BF16, "mm_pa_dw", tm=1024, tk=768)
    dobn = _mm(dyb, full['w_pb'], "nt", F32, "mm_pb_dx", tm=768)
    dw['w_pb'] = _mm(obn, dyb, "tn", BF16, "mm_pb_dw", tm=1024, tk=768)
    do, dr, acc_gla = _gla_out_bwd(doan, o2, parts, gw4, tr)
    dq, dk, dv, dpre = _gla_bwd(do, parts, u2_hi, u2_lo, gbias, gla_hist, *n_gla)
    dy, dz, dxs_skip, acc_ssd = _ssd_out_bwd(dobn, y2, xbc, parts, dskip_exp, ssm_norm_w, tr)
    (dx_scan, db_scan, dc_scan, dsm, acc_alog), exchanged = _ssd_bwd(
        dy, xbc, parts, kc, ssd_hist, *n_ssd, _exchange_comm([_full_to_blocks(dw[n], n) for n in later]))
    exchanged = dict(zip(later, exchanged))
    dxbc = _xbc_assemble(dx_scan, db_scan, dc_scan, dxs_skip, tr)
    du_conv, acc_conv = _conv_bwd(dxbc, parts, conv_w_full, conv_b, L // Lc, Lc)
    dqkv = _gla_assemble(dq, dk, dv, tr)
    dsmall, dup_gla, acc_gbias, acc_dtb = _small_assemble(dpre, dsm, parts, ut_hi, ut_lo, tr)
    dparts = jnp.concatenate([dz, du_conv, dqkv, dr, dgates, dsmall], axis=1)
    dw['w_in'] = _unpermute_in(_mm(h1, dparts, "tn", BF16, "mm_in_dw", tm=1024, tn=1152, tk=768))
    dh1, (exchanged['w_in'],) = _mm(dparts, wp, "nt", F32, "mm_in_dx", tm=768, tk=1152,
                                    comm=_exchange_comm([_full_to_blocks(dw['w_in'], 'w_in')]))
    dxall, acc_n1 = _norm1_bwd(dh1, xall, dh2, norm1_w, mod, n_lat_tiles, tr)

    partial = dict(
        norm1_w=acc_n1[0, 2] + acc_n1[1, 2],
        gla_up_f=dup_gla[0, SM_LRF:SM_LRF + GLA_RANK], gla_bias_f=acc_gbias[0],
        gla_up_b=dup_gla[1, SM_LRB:SM_LRB + GLA_RANK], gla_bias_b=acc_gbias[1],
        gla_norm_w=acc_gla[0].reshape(GLA_H, GLA_DV).sum(0),
        conv_w=acc_conv[:SSM_CONV], conv_b=acc_conv[SSM_CONV],
        dt_bias_f=acc_dtb[0, SM_DTF:SM_DTF + SSM_HEADS], dt_bias_b=acc_dtb[0, SM_DTB:SM_DTB + SSM_HEADS],
        a_log_f=acc_alog[0, :, 0, :SSM_HPG], a_log_b=acc_alog[1, :, 0, :SSM_HPG],
        d_skip=acc_ssd[1].reshape(SSM_HEADS, SSM_P).sum(1), ssm_norm_w=acc_ssd[0],
        norm2_w=acc_ffn[2], final_norm_w=acc_loss[0],
    )
    dada = jnp.concatenate([acc_n1[0, 1], acc_n1[0, 0], acc_ffn[3], acc_ffn[1], acc_ffn[0], acc_loss[1]])
    dada_ctx = jnp.concatenate([acc_n1[1, 1], acc_n1[1, 0], jnp.zeros((4 * D,), F32)])
    names = list(partial)
    payload = [partial[n] for n in names] + [dada + dada_ctx, dada_ctx, acc_loss[2], dada]
    sizes = [int(np.prod(p.shape)) for p in payload]
    g8 = _allgather_small(_pack(payload), "gather_small_grads")
    summed = _unpack(_sum8(g8), [(s,) for s in sizes])
    grads = {n: s.reshape(W[n].shape if n not in SMALL_SHARDED else partial[n].shape) for n, s in zip(names, summed)}
    grads['b_ada'] = summed[len(names)].reshape(b_ada.shape)
    dada_ctx_sum = summed[len(names) + 1]
    loss = jnp.sum(summed[len(names) + 2])
    dada_all = g8.reshape(N_DEV, -1)[:, sum(sizes[:-1]):sum(sizes)]

    dada16 = jnp.zeros((ADA_ROWS, ada_cols), F32)
    dada16 = dada16.at[:N_DEV].set(_chip_cols(dada_all, chip, ada_cols)).at[N_DEV].set(_chip_cols(dada_ctx_sum[None], chip, ada_cols)[0])
    g_wada, dl_wada, m_wada, v_wada, p16 = _wada_bwd_adam(cvec, dada16, w_ada[0], m_w_ada[0], v_w_ada[0])
    p8 = _allgather_small(p16[N_DEV:], "gather_cctx")
    grads['c_ctx'] = _cctx_grad(p8[:, 0:1, :], c_ctx[None])[0]
    for n in SMALL_SHARDED:
        grads[n] = _chip_cols(grads[n], chip, W[n].shape[2])[None]

    small = [n for n in WEIGHTS if n not in BIG and n != 'w_ada']
    shapes = [W[n].shape for n in small]
    dl_s, m_s, v_s = _adam_small(*[_pack([d[n] for n in small]) for d in (W, grads, M, V)])
    delta = dict(zip(small, _unpack(dl_s, shapes)))
    new_m = dict(zip(small, _unpack(m_s, shapes)))
    new_v = dict(zip(small, _unpack(v_s, shapes)))
    grads['w_ada'], delta['w_ada'], new_m['w_ada'], new_v['w_ada'] = g_wada[None], dl_wada[None], m_wada[None], v_wada[None]

    for n in BIG:
        g, dl, m2, v2 = _reduce_adam(exchanged[n], W[n][0], M[n][0], V[n][0], "adam_" + n)
        grads[n], delta[n], new_m[n], new_v[n] = g[None], dl[None], m2[None], v2[None]

    return (loss, dxall[:L][None], *[grads[n] for n in WEIGHTS], *[delta[n] for n in WEIGHTS],
            *[new_m[n] for n in WEIGHTS], *[new_v[n] for n in WEIGHTS])
```

```python
import functools

import numpy as np
import jax
import jax.numpy as jnp
from jax import lax
from jax.experimental import pallas as pl
from jax.experimental.pallas import tpu as pltpu

F32 = jnp.float32
BF16 = jnp.bfloat16
MESH = pl.DeviceIdType.MESH

D = 1024
EPS = 1e-6
GRID_W = 64
GLA_H, GLA_DK, GLA_DV, GLA_RANK, GLA_TAU = 4, 128, 256, 16, 16.0
GLA_C = 64
SSM_INNER, SSM_P, SSM_HEADS, SSM_G, SSM_HPG, SSM_N = 2048, 64, 32, 4, 8, 128
SSM_C = 128
SSM_CONV, CONV_LEFT = 4, 2
D_FF = 2816
IN_WIDTHS = (512, 512, 1024, 1024, 16, 16, 2048, 2048, 512, 512, 32, 32, 1024, 1024)
D_IN = sum(IN_WIDTHS)
PERM = (6, 7, 8, 9, 3, 0, 1, 2, 12, 13, 4, 5, 10, 11)
PW = 10368
SMALL_PAD = PW - D_IN
COL_Z, COL_XBC, COL_R, COL_Q, COL_K, COL_V, COL_GA, COL_GB, COL_SM = 0, 2048, 5120, 6144, 6656, 7168, 8192, 9216, 10240
SM_LRF, SM_LRB, SM_DTF, SM_DTB = 0, 16, 32, 64
EXP_CLAMP = 80.0
ADAM_LR, ADAM_B1, ADAM_B2, ADAM_EPS, ADAM_WD, ADAM_STEP = 0.001, 0.9, 0.999, 1e-08, 0.01, 10
N_CHIPS, N_DEV = 4, 8
VMEM_LIMIT = 56 * 1024 * 1024


def _cparams(sem=None):
    return pltpu.CompilerParams(dimension_semantics=sem, vmem_limit_bytes=VMEM_LIMIT)


def _dg(a, b, ca, cb):
    return lax.dot_general(a, b, (((ca,), (cb,)), ((), ())), preferred_element_type=F32)


def _nn(a, b):
    return _dg(a, b, 1, 0)


def _nt(a, b):
    return _dg(a, b, 1, 1)


def _tn(a, b):
    return _dg(a, b, 0, 0)


def _bf(x):
    return x.astype(BF16)


def _split(x):
    hi = x.astype(BF16)
    return hi, (x - hi.astype(F32)).astype(BF16)


def _nn_x(a, b_exact):
    hi, lo = _split(a)
    return _nn(hi, b_exact) + _nn(lo, b_exact)


def _x_nn(a_exact, b):
    hi, lo = _split(b)
    return _nn(a_exact, hi) + _nn(a_exact, lo)


def _nn3(a, b_hi, b_lo):
    hi, lo = _split(a)
    return _nn(hi, b_hi) + _nn(lo, b_hi) + _nn(hi, b_lo)


def _sigmoid(x):
    return 1.0 / (1.0 + jnp.exp(-x))


def _silu(x):
    return x * _sigmoid(x)


def _dsilu(x):
    s = _sigmoid(x)
    return s * (1.0 + x * (1.0 - s))


def _softplus(x):
    return jnp.maximum(x, 0.0) + jnp.log(1.0 + jnp.exp(-jnp.abs(x)))


def _log_sigmoid(x):
    return jnp.minimum(x, 0.0) - jnp.log(1.0 + jnp.exp(-jnp.abs(x)))


def _tile(n, target, mult=8):
    best = None
    for t in range(mult, min(n, target) + 1, mult):
        if n % t == 0:
            best = t
    assert best is not None, (n, target, mult)
    return best


def _mm(a, b, mode, out_dtype, name, tm=512, tn=1024, tk=2048, comm=None):
    if mode == "nn":
        (M, K), N = a.shape, b.shape[1]
    elif mode == "nt":
        (M, K), N = a.shape, b.shape[0]
    else:
        (K, M), N = a.shape, b.shape[1]
    tm, tn, tk = _tile(M, tm, 128), _tile(N, tn, 128), _tile(K, tk, 128)
    nk = K // tk
    ca, cb = {"nn": (1, 0), "nt": (1, 1), "tn": (0, 0)}[mode]

    def body(a_ref, b_ref, o_ref, *acc):
        part = _dg(a_ref[...], b_ref[...], ca, cb)
        if nk == 1:
            o_ref[...] = part.astype(out_dtype)
        else:
            k = pl.program_id(2)

            @pl.when(k == 0)
            def _():
                acc[0][...] = part

            @pl.when(k > 0)
            def _():
                acc[0][...] += part

            @pl.when(k == nk - 1)
            def _():
                o_ref[...] = acc[0][...].astype(out_dtype)

    a_spec = pl.BlockSpec((tk, tm), lambda i, j, k: (k, i)) if mode == "tn" else pl.BlockSpec((tm, tk), lambda i, j, k: (i, k))
    b_spec = pl.BlockSpec((tn, tk), lambda i, j, k: (j, k)) if mode == "nt" else pl.BlockSpec((tk, tn), lambda i, j, k: (k, j))
    gi, gj = M // tm, N // tn
    scratch = [pltpu.VMEM((tm, tn), F32)] if nk > 1 else []
    out_spec, out_shape = pl.BlockSpec((tm, tn), lambda i, j, k: (i, j)), jax.ShapeDtypeStruct((M, N), out_dtype)
    if comm is None:
        return pl.pallas_call(
            body, name=name, grid=(gi, gj, nk), in_specs=[a_spec, b_spec], out_specs=out_spec, out_shape=out_shape,
            scratch_shapes=scratch, compiler_params=_cparams(("arbitrary", "arbitrary", "arbitrary")),
        )(a, b)
    at = lambda i, j, k: (pl.program_id(0) == i) & (pl.program_id(1) == j) & (pl.program_id(2) == k)
    hosted = _hosted(body, 2, 1, len(scratch), comm, lambda: at(0, 0, 0), lambda: at(gi - 1, 0, 0),
                     lambda: at(gi - 1, gj - 1, nk - 1))
    outs = pl.pallas_call(
        hosted, name=name, grid=(gi, gj, nk), in_specs=[a_spec, b_spec] + [_ANY] * len(comm.arrays),
        out_specs=[out_spec] + [_ANY] * len(comm.out_shape), out_shape=[out_shape] + comm.out_shape,
        scratch_shapes=scratch + comm.scratch, compiler_params=_cparams(("arbitrary", "arbitrary", "arbitrary")),
    )(a, b, *comm.arrays)
    return outs[0], outs[1:]


def _place():
    return lax.axis_index("x"), lax.axis_index("y"), lax.axis_index("c")


def _flip(v, bit):
    return 1 - v if bit else v


def _allgather_small(v, name):
    R, C = v.shape

    def body(v_ref, out_ref, send_sems, recv_sems, local_sem):
        x, y, c = _place()
        me = 4 * x + 2 * y + c
        mine = pltpu.make_async_copy(v_ref, out_ref.at[me], local_sem)
        mine.start()

        def peer(r):
            return _flip(x, (r >> 2) & 1), _flip(y, (r >> 1) & 1), _flip(c, r & 1)

        sends = [pltpu.make_async_remote_copy(
            src_ref=v_ref, dst_ref=out_ref.at[me], send_sem=send_sems.at[r - 1], recv_sem=recv_sems.at[r - 1],
            device_id=peer(r), device_id_type=MESH) for r in range(1, N_DEV)]
        for cp in sends:
            cp.start()
        for r in range(1, N_DEV):
            px, py, pc = peer(r)
            pltpu.make_async_remote_copy(
                src_ref=v_ref, dst_ref=out_ref.at[4 * px + 2 * py + pc], send_sem=send_sems.at[r - 1],
                recv_sem=recv_sems.at[r - 1], device_id=(x, y, c), device_id_type=MESH).wait_recv()
        for cp in sends:
            cp.wait_send()
        mine.wait()

    return pl.pallas_call(
        body, name=name, out_shape=jax.ShapeDtypeStruct((N_DEV, R, C), v.dtype),
        in_specs=[pl.BlockSpec(memory_space=pltpu.VMEM)], out_specs=pl.BlockSpec(memory_space=pltpu.VMEM),
        scratch_shapes=[pltpu.SemaphoreType.DMA((N_DEV - 1,)), pltpu.SemaphoreType.DMA((N_DEV - 1,)), pltpu.SemaphoreType.DMA],
        compiler_params=pltpu.CompilerParams(vmem_limit_bytes=VMEM_LIMIT),
    )(v)


_CHIP_RELATIONS = ((1, 0), (0, 1), (1, 1))


def _gather_split(shard, name):
    rows, cols = shard.shape
    half = rows // 2

    def body(in_ref, out_ref, send_sems, recv_sems, local_sem):
        x, y, c = _place()
        chip = 2 * x + y
        mine = pl.ds(pl.multiple_of(c * half, 16), half)
        local = pltpu.make_async_copy(in_ref, out_ref.at[chip], local_sem)
        local.start()
        peers = [(_flip(x, fx), _flip(y, fy)) for fx, fy in _CHIP_RELATIONS]
        sends = [pltpu.make_async_remote_copy(
            src_ref=in_ref.at[mine], dst_ref=out_ref.at[chip, mine], send_sem=send_sems.at[j], recv_sem=recv_sems.at[j],
            device_id=(px, py, c), device_id_type=MESH) for j, (px, py) in enumerate(peers)]
        for cp in sends:
            cp.start()
        for j, (px, py) in enumerate(peers):
            landed = out_ref.at[2 * px + py, mine]
            pltpu.make_async_remote_copy(
                src_ref=landed, dst_ref=landed, send_sem=send_sems.at[j], recv_sem=recv_sems.at[j],
                device_id=(x, y, c), device_id_type=MESH).wait_recv()
            fwd = pltpu.make_async_remote_copy(
                src_ref=landed, dst_ref=landed, send_sem=send_sems.at[3 + j], recv_sem=recv_sems.at[3 + j],
                device_id=(x, y, 1 - c), device_id_type=MESH)
            fwd.start()
            sends.append(fwd)
        for j in range(3):
            landed = out_ref.at[0, mine]
            pltpu.make_async_remote_copy(
                src_ref=landed, dst_ref=landed, send_sem=send_sems.at[3 + j], recv_sem=recv_sems.at[3 + j],
                device_id=(x, y, c), device_id_type=MESH).wait_recv()
        for cp in sends:
            cp.wait_send()
        local.wait()

    any_spec = pl.BlockSpec(memory_space=pl.ANY)
    return pl.pallas_call(
        body, name=name, out_shape=jax.ShapeDtypeStruct((N_CHIPS, rows, cols), shard.dtype),
        in_specs=[any_spec], out_specs=any_spec,
        scratch_shapes=[pltpu.SemaphoreType.DMA((6,)), pltpu.SemaphoreType.DMA((6,)), pltpu.SemaphoreType.DMA],
    )(shard)


class _Comm:
    def __init__(self, arrays, out_shape, scratch, start, middle, finish):
        self.arrays, self.out_shape, self.scratch = arrays, out_shape, scratch
        self.start, self.middle, self.finish = start, middle, finish


def _hosted(body, n_in, n_out, n_scratch, comm, first, middle, last):
    nc, no = len(comm.arrays), len(comm.out_shape)

    def wrapped(*refs):
        a = n_in + nc
        b = a + n_out + no
        ins, c_ins, outs, c_outs = refs[:n_in], refs[n_in:a], refs[a:a + n_out], refs[a + n_out:b]
        scratch, c_sems = refs[b:b + n_scratch], refs[b + n_scratch:]

        @pl.when(first())
        def _():
            comm.start(c_ins, c_outs, c_sems)

        body(*ins, *outs, *scratch)
        if comm.middle is not None:
            @pl.when(middle())
            def _():
                comm.middle(c_ins, c_outs, c_sems)

        @pl.when(last())
        def _():
            comm.finish(c_ins, c_outs, c_sems)

    return wrapped


_ANY = pl.BlockSpec(memory_space=pl.ANY)


def _gather_comm(shards):
    n = len(shards)

    def copies(kind, ins, outs, sems):
        send_sems, recv_sems, local_sems = sems
        x, y, c = _place()
        chip = 2 * x + y
        if kind == "local":
            return [pltpu.make_async_copy(ins[i], outs[i].at[chip], local_sems.at[i]) for i in range(n)]
        made = []
        for i in range(n):
            for j, (fx, fy) in enumerate(_CHIP_RELATIONS):
                px, py = _flip(x, fx), _flip(y, fy)
                slot, to = (chip, (px, py, c)) if kind == "send" else (2 * px + py, (x, y, c))
                made.append(pltpu.make_async_remote_copy(
                    src_ref=ins[i], dst_ref=outs[i].at[slot], send_sem=send_sems.at[i, j], recv_sem=recv_sems.at[i, j],
                    device_id=to, device_id_type=MESH))
        return made

    def start(ins, outs, sems):
        for cp in copies("local", ins, outs, sems) + copies("send", ins, outs, sems):
            cp.start()

    def finish(ins, outs, sems):
        for cp in copies("recv", ins, outs, sems):
            cp.wait_recv()
        for cp in copies("send", ins, outs, sems):
            cp.wait_send()
        for cp in copies("local", ins, outs, sems):
            cp.wait()

    return _Comm(list(shards), [jax.ShapeDtypeStruct((N_CHIPS,) + s.shape, s.dtype) for s in shards],
                 [pltpu.SemaphoreType.DMA((n, 3)), pltpu.SemaphoreType.DMA((n, 3)), pltpu.SemaphoreType.DMA((n,))],
                 start, None, finish)


def _exchange_comm(blocks):
    n = len(blocks)

    def copies(kind, ins, outs, sems):
        send_sems, recv_sems, local_sems = sems
        x, y, c = _place()
        chip = 2 * x + y
        me, sibling = (x, y, c), (x, y, 1 - c)

        def remote(src, dst, i, j, to):
            return pltpu.make_async_remote_copy(src_ref=src, dst_ref=dst, send_sem=send_sems.at[i, j],
                                                recv_sem=recv_sems.at[i, j], device_id=to, device_id_type=MESH)

        made = []
        for i in range(n):
            if kind == "local":
                made.append(pltpu.make_async_copy(ins[i].at[chip], outs[i].at[chip], local_sems.at[i]))
                continue
            for j, (fx, fy) in enumerate(_CHIP_RELATIONS):
                px, py = _flip(x, fx), _flip(y, fy)
                src = 2 * px + py
                if kind == "first":
                    made.append(remote(ins[i].at[src], outs[i].at[chip], i, j, (px, py, c)))
                elif kind == "landed":
                    made.append(remote(ins[i].at[src], outs[i].at[src], i, j, me))
                elif kind == "passed":
                    made.append(remote(outs[i].at[src], outs[i].at[N_CHIPS + src], i, 4 + j, sibling))
            if kind == "first":
                made.append(remote(ins[i].at[chip], outs[i].at[N_CHIPS + chip], i, 3, sibling))
            if kind == "arrivals":
                made += [remote(ins[i].at[0], outs[i].at[0], i, j, me) for j in (3, 4, 5, 6)]
        return made

    def start(ins, outs, sems):
        for cp in copies("local", ins, outs, sems) + copies("first", ins, outs, sems):
            cp.start()

    def middle(ins, outs, sems):
        for got, fwd in zip(copies("landed", ins, outs, sems), copies("passed", ins, outs, sems)):
            got.wait_recv()
            fwd.start()

    def finish(ins, outs, sems):
        for cp in copies("arrivals", ins, outs, sems):
            cp.wait_recv()
        for cp in copies("first", ins, outs, sems) + copies("passed", ins, outs, sems):
            cp.wait_send()
        for cp in copies("local", ins, outs, sems):
            cp.wait()

    return _Comm(list(blocks), [jax.ShapeDtypeStruct((N_DEV,) + b.shape[1:], b.dtype) for b in blocks],
                 [pltpu.SemaphoreType.DMA((n, 7)), pltpu.SemaphoreType.DMA((n, 7)), pltpu.SemaphoreType.DMA((n,))],
                 start, middle, finish)


def _row_spec(tr, w, col=0):
    return pl.BlockSpec((tr, w), lambda i: (i, col))


def _vec_spec(w):
    return pl.BlockSpec((1, w), lambda i: (0, 0))


def _acc_spec(w):
    return pl.BlockSpec((8, w), lambda i: (0, 0))


def _rms(x):
    return lax.rsqrt(jnp.mean(x * x, axis=-1, keepdims=True) + EPS)


def _rms_bwd(dn, n, rstd):
    return rstd * (dn - n * jnp.mean(dn * n, axis=-1, keepdims=True))


def _colsum(x):
    return jnp.sum(x, axis=0, keepdims=True)


def _zero_first(ref):
    @pl.when(pl.program_id(0) == 0)
    def _():
        ref[...] = jnp.zeros_like(ref)


def _norm_mod(xall, w, mod, n_lat_tiles, tr):
    R = xall.shape[0]

    def body(x_ref, w_ref, mod_ref, o_ref):
        x = x_ref[...]
        n = x * _rms(x) * w_ref[...]
        o_ref[...] = (n * (1.0 + mod_ref[0, 1]) + mod_ref[0, 0]).astype(BF16)

    return pl.pallas_call(
        body, name="norm1_mod", grid=(R // tr,),
        in_specs=[_row_spec(tr, D), _vec_spec(D),
                  pl.BlockSpec((1, 2, 1, D), lambda i: (jnp.where(i >= n_lat_tiles, 1, 0), 0, 0, 0))],
        out_specs=_row_spec(tr, D), out_shape=jax.ShapeDtypeStruct((R, D), BF16),
        compiler_params=_cparams(("arbitrary",)),
    )(xall, w, mod)


def _resid_norm_mod(xall, mix, g1, w2, sh2, sc2, tr):
    R = xall.shape[0]

    def body(x_ref, mix_ref, g1_ref, w_ref, sh_ref, sc_ref, h2_ref, u_ref):
        h2 = x_ref[...] + g1_ref[...] * mix_ref[...]
        h2_ref[...] = h2
        n = h2 * _rms(h2) * w_ref[...]
        u_ref[...] = (n * (1.0 + sc_ref[...]) + sh_ref[...]).astype(BF16)

    return pl.pallas_call(
        body, name="resid_norm2_mod", grid=(R // tr,),
        in_specs=[_row_spec(tr, D), _row_spec(tr, D)] + [_vec_spec(D)] * 4,
        out_specs=[_row_spec(tr, D), _row_spec(tr, D)],
        out_shape=[jax.ShapeDtypeStruct((R, D), F32), jax.ShapeDtypeStruct((R, D), BF16)],
        compiler_params=_cparams(("arbitrary",)),
    )(xall, mix, g1, w2, sh2, sc2)


def _loss_head(h2, f, target, g2, fw, n_lat_tiles, tr):
    R = h2.shape[0]

    def body(h2_ref, f_ref, t_ref, g2_ref, fw_ref, dh3_ref, df_ref, acc_ref):
        _zero_first(acc_ref)
        lat = pl.program_id(0) < n_lat_tiles
        fv = f_ref[...]
        h3 = h2_ref[...] + g2_ref[...] * fv
        rstd = _rms(h3)
        n = h3 * rstd
        err = n * fw_ref[...] - t_ref[...]
        dy = err * (1.0 / D)
        dh3 = jnp.where(lat, _rms_bwd(dy * fw_ref[...], n, rstd), 0.0)
        dh3_ref[...] = dh3
        df_ref[...] = (g2_ref[...] * dh3).astype(BF16)
        acc_ref[0:1, :] += jnp.where(lat, _colsum(dy * n), 0.0)
        acc_ref[1:2, :] += _colsum(dh3 * fv)
        acc_ref[2:3, :] += jnp.where(lat, _colsum(err * err) * (0.5 / D), 0.0)

    return pl.pallas_call(
        body, name="loss_head", grid=(R // tr,),
        in_specs=[_row_spec(tr, D), _row_spec(tr, D),
                  pl.BlockSpec((tr, D), lambda i: (jnp.minimum(i, n_lat_tiles - 1), 0)), _vec_spec(D), _vec_spec(D)],
        out_specs=[_row_spec(tr, D), _row_spec(tr, D), _acc_spec(D)],
        out_shape=[jax.ShapeDtypeStruct((R, D), F32), jax.ShapeDtypeStruct((R, D), BF16), jax.ShapeDtypeStruct((8, D), F32)],
        compiler_params=_cparams(("arbitrary",)),
    )(h2, f, target, g2, fw)


def _ffn_in_bwd(du_a, du_b, h2, dh3, mix, sc2, g1, w2, tr):
    R = h2.shape[0]

    def body(dua_ref, dub_ref, h2_ref, dh3_ref, mix_ref, sc_ref, g1_ref, w_ref, dh2_ref, dmix_ref, acc_ref):
        _zero_first(acc_ref)
        du = dua_ref[...] + dub_ref[...]
        h2 = h2_ref[...]
        rstd = _rms(h2)
        n = h2 * rstd
        dnw = du * (1.0 + sc_ref[...])
        dh2 = dh3_ref[...] + _rms_bwd(dnw * w_ref[...], n, rstd)
        dh2_ref[...] = dh2
        dmix_ref[...] = (g1_ref[...] * dh2).astype(BF16)
        acc_ref[0:1, :] += _colsum(du * n * w_ref[...])
        acc_ref[1:2, :] += _colsum(du)
        acc_ref[2:3, :] += _colsum(dnw * n)
        acc_ref[3:4, :] += _colsum(dh2 * mix_ref[...])

    return pl.pallas_call(
        body, name="ffn_in_bwd", grid=(R // tr,),
        in_specs=[_row_spec(tr, D)] * 5 + [_vec_spec(D)] * 3,
        out_specs=[_row_spec(tr, D), _row_spec(tr, D), _acc_spec(D)],
        out_shape=[jax.ShapeDtypeStruct((R, D), F32), jax.ShapeDtypeStruct((R, D), BF16), jax.ShapeDtypeStruct((8, D), F32)],
        compiler_params=_cparams(("arbitrary",)),
    )(du_a, du_b, h2, dh3, mix, sc2, g1, w2)


def _norm1_bwd(dh1, xall, dh2, w1, mod, n_lat_tiles, tr):
    R = xall.shape[0]

    def body(dh1_ref, x_ref, dh2_ref, w_ref, mod_ref, dx_ref, acc_ref):
        i = pl.program_id(0)

        @pl.when((i == 0) | (i == n_lat_tiles))
        def _():
            acc_ref[...] = jnp.zeros_like(acc_ref)

        dh1 = dh1_ref[...]
        x = x_ref[...]
        rstd = _rms(x)
        n = x * rstd
        dnw = dh1 * (1.0 + mod_ref[0, 1])
        dx_ref[...] = dh2_ref[...] + _rms_bwd(dnw * w_ref[...], n, rstd)
        acc_ref[0, 0:1, :] += _colsum(dh1 * n * w_ref[...])
        acc_ref[0, 1:2, :] += _colsum(dh1)
        acc_ref[0, 2:3, :] += _colsum(dnw * n)

    sel = lambda i: jnp.where(i >= n_lat_tiles, 1, 0)
    return pl.pallas_call(
        body, name="norm1_bwd", grid=(R // tr,),
        in_specs=[_row_spec(tr, D)] * 3 + [_vec_spec(D), pl.BlockSpec((1, 2, 1, D), lambda i: (sel(i), 0, 0, 0))],
        out_specs=[_row_spec(tr, D), pl.BlockSpec((1, 8, D), lambda i: (sel(i), 0, 0))],
        out_shape=[jax.ShapeDtypeStruct((R, D), F32), jax.ShapeDtypeStruct((2, 8, D), F32)],
        compiler_params=_cparams(("arbitrary",)),
    )(dh1, xall, dh2, w1, mod)


def _swiglu_act(gp, up, tr):
    R = gp.shape[0]

    def body(g_ref, u_ref, o_ref):
        o_ref[...] = (_silu(g_ref[...]) * u_ref[...]).astype(BF16)

    return pl.pallas_call(
        body, name="swiglu_act", grid=(R // tr,), in_specs=[_row_spec(tr, D_FF)] * 2, out_specs=_row_spec(tr, D_FF),
        out_shape=jax.ShapeDtypeStruct((R, D_FF), BF16), compiler_params=_cparams(("arbitrary",)),
    )(gp, up)


def _swiglu_act_bwd(da, gp, up, tr):
    R = gp.shape[0]

    def body(da_ref, g_ref, u_ref, dg_ref, du_ref):
        da, g = da_ref[...], g_ref[...]
        dg_ref[...] = (da * u_ref[...] * _dsilu(g)).astype(BF16)
        du_ref[...] = (da * _silu(g)).astype(BF16)

    return pl.pallas_call(
        body, name="swiglu_act_bwd", grid=(R // tr,), in_specs=[_row_spec(tr, D_FF)] * 3, out_specs=[_row_spec(tr, D_FF)] * 2,
        out_shape=[jax.ShapeDtypeStruct((R, D_FF), BF16)] * 2, compiler_params=_cparams(("arbitrary",)),
    )(da, gp, up)


def _merge(ya, yb, parts, tr):
    R = ya.shape[0]

    def body(ya_ref, yb_ref, ga_ref, gb_ref, o_ref):
        o_ref[...] = (_sigmoid(ga_ref[...]) * ya_ref[...] + _sigmoid(gb_ref[...]) * yb_ref[...]).astype(BF16)

    return pl.pallas_call(
        body, name="merge", grid=(R // tr,),
        in_specs=[_row_spec(tr, D), _row_spec(tr, D), _row_spec(tr, D, COL_GA // D), _row_spec(tr, D, COL_GB // D)],
        out_specs=_row_spec(tr, D), out_shape=jax.ShapeDtypeStruct((R, D), BF16), compiler_params=_cparams(("arbitrary",)),
    )(ya, yb, parts, parts)


def _dparts_out(tr, w, col, nd=1):
    blk = col // w
    return pl.BlockSpec((tr, w), (lambda i: (i, blk)) if nd == 1 else (lambda i, j: (i, blk + j)))


def _merge_bwd(dm, ya, yb, parts, dparts, tr):
    R = ya.shape[0]

    def body(dm_ref, ya_ref, yb_ref, ga_ref, gb_ref, _, dya_ref, dyb_ref, dg_ref):
        dm = dm_ref[...]
        sa, sb = _sigmoid(ga_ref[...]), _sigmoid(gb_ref[...])
        dya_ref[...] = (dm * sa).astype(BF16)
        dyb_ref[...] = (dm * sb).astype(BF16)
        dg_ref[:, 0:D] = (dm * ya_ref[...] * sa * (1.0 - sa)).astype(BF16)
        dg_ref[:, D:2 * D] = (dm * yb_ref[...] * sb * (1.0 - sb)).astype(BF16)

    return pl.pallas_call(
        body, name="merge_bwd", grid=(R // tr,),
        in_specs=[_row_spec(tr, D)] * 3 + [_row_spec(tr, D, COL_GA // D), _row_spec(tr, D, COL_GB // D), _ANY],
        out_specs=[_row_spec(tr, D), _row_spec(tr, D), _dparts_out(tr, 2 * D, COL_GA)],
        out_shape=[jax.ShapeDtypeStruct((R, D), BF16)] * 2 + [jax.ShapeDtypeStruct(dparts.shape, BF16)],
        input_output_aliases={5: 2}, compiler_params=_cparams(("arbitrary",)),
    )(dm, ya, yb, parts, parts, dparts)


def _gla_out(o2, parts, gw4, tr):
    R = parts.shape[0]

    def body(of_ref, ob_ref, r_ref, w_ref, out_ref):
        oa = of_ref[...] + ob_ref[...]
        sr = _silu(r_ref[...])
        for h in range(GLA_H):
            s = slice(h * GLA_DV, (h + 1) * GLA_DV)
            o = oa[:, s]
            out_ref[:, s] = (o * _rms(o) * w_ref[:, s] * sr[:, s]).astype(BF16)

    return pl.pallas_call(
        body, name="gla_out", grid=(R // tr,),
        in_specs=[_row_spec(tr, D), _row_spec(tr, D), _row_spec(tr, D, COL_R // D), _vec_spec(D)],
        out_specs=_row_spec(tr, D), out_shape=jax.ShapeDtypeStruct((R, D), BF16), compiler_params=_cparams(("arbitrary",)),
    )(o2[0], o2[1], parts, gw4)


def _gla_out_bwd(dout, o2, parts, gw4, dparts, tr):
    R = parts.shape[0]

    def body(d_ref, of_ref, ob_ref, r_ref, w_ref, _, do_ref, dr_ref, acc_ref):
        _zero_first(acc_ref)
        oa = of_ref[...] + ob_ref[...]
        r = r_ref[...]
        sr = _silu(r)
        dout = d_ref[...]
        for h in range(GLA_H):
            s = slice(h * GLA_DV, (h + 1) * GLA_DV)
            o = oa[:, s]
            rstd = _rms(o)
            n = o * rstd
            w = w_ref[:, s]
            dr_ref[:, s] = (dout[:, s] * n * w * _dsilu(r[:, s])).astype(BF16)
            dnw = dout[:, s] * sr[:, s]
            do_ref[:, s] = _rms_bwd(dnw * w, n, rstd)
            acc_ref[0:1, s] += _colsum(dnw * n)

    return pl.pallas_call(
        body, name="gla_out_bwd", grid=(R // tr,),
        in_specs=[_row_spec(tr, D), _row_spec(tr, D), _row_spec(tr, D), _row_spec(tr, D, COL_R // D), _vec_spec(D), _ANY],
        out_specs=[_row_spec(tr, D), _dparts_out(tr, D, COL_R), _acc_spec(D)],
        out_shape=[jax.ShapeDtypeStruct((R, D), F32), jax.ShapeDtypeStruct(dparts.shape, BF16), jax.ShapeDtypeStruct((8, D), F32)],
        input_output_aliases={5: 1}, compiler_params=_cparams(("arbitrary",)),
    )(dout, o2[0], o2[1], parts, gw4, dparts)


SSM_GW = SSM_INNER // SSM_G


def _ssd_out(y2, xbc, parts, dskip, nw, tr):
    R = parts.shape[0]

    def body(yf_ref, yb_ref, x_ref, z_ref, ds_ref, w_ref, out_ref):
        ob = (yf_ref[...] + yb_ref[...] + ds_ref[...] * x_ref[...]) * _silu(z_ref[...])
        for g in range(SSM_G):
            s = slice(g * SSM_GW, (g + 1) * SSM_GW)
            o = ob[:, s]
            out_ref[:, s] = (o * _rms(o) * w_ref[:, s]).astype(BF16)

    return pl.pallas_call(
        body, name="ssd_out", grid=(R // tr,),
        in_specs=[_row_spec(tr, SSM_INNER)] * 3 + [_row_spec(tr, SSM_INNER, COL_Z // SSM_INNER),
                                                   _vec_spec(SSM_INNER), _vec_spec(SSM_INNER)],
        out_specs=_row_spec(tr, SSM_INNER), out_shape=jax.ShapeDtypeStruct((R, SSM_INNER), BF16),
        compiler_params=_cparams(("arbitrary",)),
    )(y2[0], y2[1], xbc, parts, dskip, nw)


def _ssd_out_bwd(dout, y2, xbc, parts, dskip, nw, dparts, tr):
    R = parts.shape[0]

    def body(d_ref, yf_ref, yb_ref, x_ref, z_ref, ds_ref, w_ref, _, dy_ref, dz_ref, acc_ref):
        _zero_first(acc_ref)
        x, z = x_ref[...], z_ref[...]
        pre = yf_ref[...] + yb_ref[...] + ds_ref[...] * x
        sz = _silu(z)
        ob = pre * sz
        dout = d_ref[...]
        for g in range(SSM_G):
            s = slice(g * SSM_GW, (g + 1) * SSM_GW)
            o = ob[:, s]
            rstd = _rms(o)
            n = o * rstd
            dob = _rms_bwd(dout[:, s] * w_ref[:, s], n, rstd)
            dz_ref[:, s] = (dob * pre[:, s] * _dsilu(z[:, s])).astype(BF16)
            dy = dob * sz[:, s]
            dy_ref[:, s] = dy
            acc_ref[0:1, s] += _colsum(dout[:, s] * n)
            acc_ref[1:2, s] += _colsum(dy * x[:, s])

    return pl.pallas_call(
        body, name="ssd_out_bwd", grid=(R // tr,),
        in_specs=[_row_spec(tr, SSM_INNER)] * 4 + [_row_spec(tr, SSM_INNER, COL_Z // SSM_INNER),
                                                   _vec_spec(SSM_INNER), _vec_spec(SSM_INNER), _ANY],
        out_specs=[_row_spec(tr, SSM_INNER), _dparts_out(tr, SSM_INNER, COL_Z), _acc_spec(SSM_INNER)],
        out_shape=[jax.ShapeDtypeStruct((R, SSM_INNER), F32), jax.ShapeDtypeStruct(dparts.shape, BF16),
                   jax.ShapeDtypeStruct((8, SSM_INNER), F32)],
        input_output_aliases={7: 1}, compiler_params=_cparams(("arbitrary",)),
    )(dout, y2[0], y2[1], xbc, parts, dskip, nw, dparts)


CONV_W = SSM_INNER + 2 * SSM_G * SSM_N
CONV_BLK = 1024


def _conv_masks(tr, is_ctx):
    t = lax.broadcasted_iota(jnp.int32, (tr, 1), 0)
    pos = jnp.where(is_ctx, t, t & (GRID_W - 1))
    seg = jnp.where(is_ctx, tr, GRID_W)
    return pos, seg


def _shift_rows(u, s, tr):
    return u if s == 0 else pltpu.roll(u, (-s) % tr, 0)


def _conv_fwd(parts, cw, cb, n_lat_tiles, tr):
    R = parts.shape[0]

    def body(u_ref, w_ref, b_ref, o_ref):
        pos, seg = _conv_masks(tr, pl.program_id(0) >= n_lat_tiles)
        u = u_ref[...]
        acc = jnp.zeros_like(u) + b_ref[...]
        for j in range(SSM_CONV):
            s = j - CONV_LEFT
            ok = (pos + s >= 0) & (pos + s < seg)
            acc = acc + jnp.where(ok, _shift_rows(u, s, tr), 0.0) * w_ref[j:j + 1, :]
        o_ref[...] = _silu(acc)

    return pl.pallas_call(
        body, name="conv_fwd", grid=(R // tr, CONV_W // CONV_BLK),
        in_specs=[pl.BlockSpec((tr, CONV_BLK), lambda i, j: (i, COL_XBC // CONV_BLK + j)),
                  pl.BlockSpec((SSM_CONV, CONV_BLK), lambda i, j: (0, j)), pl.BlockSpec((1, CONV_BLK), lambda i, j: (0, j))],
        out_specs=pl.BlockSpec((tr, CONV_BLK), lambda i, j: (i, j)), out_shape=jax.ShapeDtypeStruct((R, CONV_W), F32),
        compiler_params=_cparams(("arbitrary", "arbitrary")),
    )(parts, cw, cb)


def _conv_bwd(dx, db, dc, dy, dskip, parts, cw, cb, dparts, n_lat_tiles, tr):
    R = parts.shape[0]
    half = CONV_BLK // 2
    n_x = SSM_INNER // CONV_BLK

    def body(dxf_ref, dxb_ref, dy_ref, ds_ref, dbf_ref, dbb_ref, dcf_ref, dcb_ref, u_ref, w_ref, b_ref, _, du_ref, acc_ref):
        @pl.when(pl.program_id(1) == 0)
        def _():
            acc_ref[...] = jnp.zeros_like(acc_ref)

        d_x = dxf_ref[...] + dxb_ref[...] + dy_ref[...] * ds_ref[...]
        d_bc = jnp.concatenate([dbf_ref[...] + dbb_ref[...], dcf_ref[...] + dcb_ref[...]], axis=1)
        d = jnp.where(pl.program_id(0) < n_x, d_x, d_bc)
        pos, seg = _conv_masks(tr, pl.program_id(1) >= n_lat_tiles)
        u = u_ref[...]
        pre = jnp.zeros_like(u) + b_ref[...]
        taps = []
        for j in range(SSM_CONV):
            s = j - CONV_LEFT
            ok = (pos + s >= 0) & (pos + s < seg)
            tap = jnp.where(ok, _shift_rows(u, s, tr), 0.0)
            taps.append(tap)
            pre = pre + tap * w_ref[j:j + 1, :]
        dpre = d * _dsilu(pre)
        du = jnp.zeros_like(u)
        for j in range(SSM_CONV):
            s = j - CONV_LEFT
            acc_ref[j:j + 1, :] += _colsum(dpre * taps[j])
            ok = (pos - s >= 0) & (pos - s < seg)
            du = du + jnp.where(ok, _shift_rows(dpre, -s, tr), 0.0) * w_ref[j:j + 1, :]
        acc_ref[SSM_CONV:SSM_CONV + 1, :] += _colsum(dpre)
        du_ref[...] = du.astype(BF16)

    return pl.pallas_call(
        body, name="conv_bwd", grid=(CONV_W // CONV_BLK, R // tr),
        in_specs=[pl.BlockSpec((tr, CONV_BLK), lambda j, i: (jnp.where(j < n_x, i, 0), jnp.minimum(j, n_x - 1)))] * 3
        + [pl.BlockSpec((1, CONV_BLK), lambda j, i: (0, jnp.minimum(j, n_x - 1)))]
        + [pl.BlockSpec((tr, half), lambda j, i: (jnp.where(j < n_x, 0, i), 0))] * 4
        + [pl.BlockSpec((tr, CONV_BLK), lambda j, i: (i, COL_XBC // CONV_BLK + j)),
           pl.BlockSpec((SSM_CONV, CONV_BLK), lambda j, i: (0, j)), pl.BlockSpec((1, CONV_BLK), lambda j, i: (0, j)), _ANY],
        out_specs=[pl.BlockSpec((tr, CONV_BLK), lambda j, i: (i, COL_XBC // CONV_BLK + j)),
                   pl.BlockSpec((8, CONV_BLK), lambda j, i: (0, j))],
        out_shape=[jax.ShapeDtypeStruct(dparts.shape, BF16), jax.ShapeDtypeStruct((8, CONV_W), F32)],
        input_output_aliases={11: 0}, compiler_params=_cparams(("arbitrary", "arbitrary")),
    )(*dx, dy, dskip, *db, *dc, parts, cw, cb, dparts)


def _chunk_row_block(d, i, n_lat, n_ctx):
    fwd = jnp.where(i < n_ctx, n_lat + i, i - n_ctx)
    rev = n_lat + n_ctx - 1 - i
    if isinstance(d, int):
        return rev if d else fwd
    return jnp.where(d == 0, fwd, rev)


def _tri(n, d, transpose=False):
    row = lax.broadcasted_iota(jnp.int32, (n, n), 0)
    col = lax.broadcasted_iota(jnp.int32, (n, n), 1)
    diff = (col - row) if transpose else (row - col)
    return diff * (1 - 2 * d) >= 0


def _gla_gates(sm, uhi, ulo, bias, d):
    pre = _nn3(sm, uhi, ulo) + bias
    g = _log_sigmoid(pre) * (1.0 / GLA_TAU)
    mask = _tri(GLA_C, d)
    b = _x_nn(mask.astype(BF16), g)
    b_tot = _colsum(g)
    b_ref = b[GLA_C // 2:GLA_C // 2 + 1, :]
    e_q = jnp.exp(jnp.minimum(b - b_ref, EXP_CLAMP))
    e_k = jnp.exp(jnp.minimum(b_ref - b, EXP_CLAMP))
    return pre, mask, b_tot, e_q, e_k, jnp.exp(b), jnp.exp(b_tot - b)


GLA_QK = GLA_H * GLA_DK
GLA_V = GLA_H * GLA_DV


def _gla_specs(n_lat, n_ctx, step_of):
    rbs = [lambda i, d=d: _chunk_row_block(d, step_of(i), n_lat, n_ctx) for d in range(2)]
    specs = []
    for rb in rbs:
        specs += [pl.BlockSpec((GLA_C, GLA_QK), lambda i, rb=rb: (rb(i), COL_Q // GLA_QK)),
                  pl.BlockSpec((GLA_C, GLA_QK), lambda i, rb=rb: (rb(i), COL_K // GLA_QK)),
                  pl.BlockSpec((GLA_C, GLA_V), lambda i, rb=rb: (rb(i), COL_V // GLA_V)),
                  pl.BlockSpec((GLA_C, 128), lambda i, rb=rb: (rb(i), COL_SM // 128))]
    specs += [pl.BlockSpec((2, 128, GLA_QK), lambda i: (0, 0, 0)), pl.BlockSpec((2, 128, GLA_QK), lambda i: (0, 0, 0)),
              pl.BlockSpec((2, 1, GLA_QK), lambda i: (0, 0, 0))]
    return specs, rbs


def _gla_fwd(parts, uhi, ulo, bias, n_lat, n_ctx):
    R = parts.shape[0]
    n_steps = n_lat + n_ctx
    scale = GLA_DK ** -0.5

    def body(*refs):
        ins, (uhi_ref, ulo_ref, bias_ref), o_refs, hist_ref, st = refs[:8], refs[8:11], refs[11:13], refs[13], refs[14]

        @pl.when(pl.program_id(0) == 0)
        def _():
            st[...] = jnp.zeros_like(st)

        for d in range(2):
            q_ref, k_ref, v_ref, sm_ref = ins[4 * d:4 * d + 4]
            _, mask, b_tot, e_q, e_k, e_in, e_out = _gla_gates(sm_ref[...], uhi_ref[d], ulo_ref[d], bias_ref[d], d)
            q, k, v = q_ref[...] * scale, k_ref[...], _bf(v_ref[...])
            qb, kb, q_in, k_out, decay = _bf(q * e_q), _bf(k * e_k), _bf(q * e_in), _bf(k * e_out), jnp.exp(b_tot)
            for h in range(GLA_H):
                sk, sv = slice(h * GLA_DK, (h + 1) * GLA_DK), slice(h * GLA_DV, (h + 1) * GLA_DV)
                att = jnp.where(mask, _nt(qb[:, sk], kb[:, sk]), 0.0)
                s_in = st[d, h]
                hist_ref[d, 0, h] = s_in
                o_refs[d][:, sv] = _nn(_bf(att), v[:, sv]) + _nt(q_in[:, sk], _bf(s_in))
                st[d, h] = decay[:, sk] * s_in + _tn(v[:, sv], k_out[:, sk])

    in_specs, rbs = _gla_specs(n_lat, n_ctx, lambda i: i)
    return pl.pallas_call(
        body, name="gla_fwd", grid=(n_steps,), in_specs=in_specs,
        out_specs=[pl.BlockSpec((GLA_C, GLA_V), lambda i, rb=rb: (rb(i), 0)) for rb in rbs]
        + [pl.BlockSpec((2, 1, GLA_H, GLA_DV, GLA_DK), lambda i: (0, i, 0, 0, 0))],
        out_shape=[jax.ShapeDtypeStruct((R, GLA_V), F32)] * 2 + [jax.ShapeDtypeStruct((2, n_steps, GLA_H, GLA_DV, GLA_DK), F32)],
        scratch_shapes=[pltpu.VMEM((2, GLA_H, GLA_DV, GLA_DK), F32)],
        compiler_params=_cparams(("arbitrary",)),
    )(*([parts] * 8), uhi, ulo, bias)


def _gla_bwd(do, parts, uhi, ulo, bias, hist, n_lat, n_ctx):
    R = parts.shape[0]
    n_steps = n_lat + n_ctx
    scale = GLA_DK ** -0.5
    step_of = lambda j: n_steps - 1 - j

    def body(*refs):
        ins, (uhi_ref, ulo_ref, bias_ref), do_refs, hist_ref = refs[:8], refs[8:11], refs[11:13], refs[13]
        outs, dst = refs[14:22], refs[22]

        @pl.when(pl.program_id(0) == 0)
        def _():
            dst[...] = jnp.zeros_like(dst)

        for d in range(2):
            q_ref, k_ref, v_ref, sm_ref = ins[4 * d:4 * d + 4]
            dq_ref, dk_ref, dv_ref, dp_ref = outs[4 * d:4 * d + 4]
            pre, mask, b_tot, e_q, e_k, e_in, e_out = _gla_gates(sm_ref[...], uhi_ref[d], ulo_ref[d], bias_ref[d], d)
            q, k, v = q_ref[...] * scale, k_ref[...], _bf(v_ref[...])
            dout = _bf(do_refs[d][...])
            k_out_f = k * e_out
            qb, kb, q_in, k_out, decay = _bf(q * e_q), _bf(k * e_k), _bf(q * e_in), _bf(k_out_f), jnp.exp(b_tot)
            dqs, dks, dk_outs, dss = [], [], [], []
            for h in range(GLA_H):
                sk, sv = slice(h * GLA_DK, (h + 1) * GLA_DK), slice(h * GLA_DV, (h + 1) * GLA_DV)
                s_in, ds = hist_ref[d, 0, h], dst[d, h]
                att = jnp.where(mask, _nt(qb[:, sk], kb[:, sk]), 0.0)
                datt = _bf(jnp.where(mask, _nt(dout[:, sv], v[:, sv]), 0.0))
                dv_ref[:, sv] = _tn(_bf(att), dout[:, sv]) + _nt(k_out[:, sk], _bf(ds))
                dqs.append(_nn(datt, kb[:, sk]) * e_q[:, sk] + _nn(dout[:, sv], _bf(s_in)) * e_in[:, sk])
                dk_o = _nn(v[:, sv], _bf(ds))
                dk_outs.append(dk_o)
                dks.append(_tn(datt, qb[:, sk]) * e_k[:, sk])
                dss.append(_colsum(ds * s_in))
                dst[d, h] = decay[:, sk] * ds + _tn(dout[:, sv], q_in[:, sk])
            dq, dk_out = jnp.concatenate(dqs, axis=1), jnp.concatenate(dk_outs, axis=1)
            dk = jnp.concatenate(dks, axis=1) + dk_out * e_out
            dq_ref[...] = dq * scale
            dk_ref[...] = dk
            db_tot = _colsum(dk_out * k_out_f) + decay * jnp.concatenate(dss, axis=1)
            dg = _x_nn(_tri(GLA_C, d, transpose=True).astype(BF16), dq * q - dk * k) + db_tot
            dp_ref[...] = dg * (1.0 / GLA_TAU) * _sigmoid(-pre)

    in_specs, rbs = _gla_specs(n_lat, n_ctx, step_of)
    in_specs += [pl.BlockSpec((GLA_C, GLA_V), lambda j, rb=rb: (rb(j), 0)) for rb in rbs]
    in_specs += [pl.BlockSpec((2, 1, GLA_H, GLA_DV, GLA_DK), lambda j: (0, step_of(j), 0, 0, 0))]
    out_specs, out_shape = [], []
    for rb in rbs:
        for w in (GLA_QK, GLA_QK, GLA_V, GLA_QK):
            out_specs.append(pl.BlockSpec((GLA_C, w), lambda j, rb=rb: (rb(j), 0)))
            out_shape.append(jax.ShapeDtypeStruct((R, w), F32))
    outs = pl.pallas_call(
        body, name="gla_bwd", grid=(n_steps,), in_specs=in_specs, out_specs=out_specs, out_shape=out_shape,
        scratch_shapes=[pltpu.VMEM((2, GLA_H, GLA_DV, GLA_DK), F32)],
        compiler_params=_cparams(("arbitrary",)),
    )(*([parts] * 8), uhi, ulo, bias, do, do, hist)
    return [(outs[k], outs[4 + k]) for k in range(4)]


def _ssd_consts(dt_bias, a_log):
    sel = np.zeros((2, SSM_G, 128, 128), np.float32)
    for d, base in enumerate((SM_DTF, SM_DTB)):
        for g in range(SSM_G):
            for e in range(SSM_HPG):
                sel[d, g, base + SSM_HPG * g + e, e] = 1.0
    e512 = np.zeros((128, SSM_GW), np.float32)
    for e in range(SSM_HPG):
        e512[e, SSM_P * e:SSM_P * (e + 1)] = 1.0
    a_neg = -jnp.exp(a_log)
    pad = lambda v: jnp.pad(v.reshape(2, SSM_G, 1, SSM_HPG), ((0, 0), (0, 0), (0, 0), (0, 128 - SSM_HPG)))
    return dict(
        sel=jnp.asarray(sel, BF16), sel_t=jnp.asarray(sel.transpose(0, 1, 3, 2), BF16), e512_t=jnp.asarray(e512.T, BF16),
        dtb=pad(dt_bias), a=pad(a_neg), a512=jnp.repeat(a_neg, SSM_P, axis=1).reshape(2, SSM_G, 1, SSM_GW))


def _head_columns(x8):
    return [jnp.broadcast_to(x8[:, e:e + 1], (x8.shape[0], 128)) for e in range(SSM_HPG)]


def _head_layout(cols):
    low = lax.broadcasted_iota(jnp.int32, (1, 128), 1) < SSM_P
    return jnp.concatenate([jnp.where(low, cols[2 * j], cols[2 * j + 1]) for j in range(SSM_HPG // 2)], axis=1)


def _ssd_common(sm, sel, dtb, a_neg, a512, d):
    dtr8 = _nn_x(sm, sel) + dtb
    dt8 = _softplus(dtr8)
    a8 = a_neg * dt8
    mask = _tri(SSM_C, d)
    mask_t = _tri(SSM_C, d, transpose=True).astype(BF16)
    cum8 = _x_nn(mask.astype(BF16), a8)
    a_hi, a_lo = _split(a8)
    cum_t = _tn(a_hi, mask_t) + _tn(a_lo, mask_t)
    cum_cols = _head_columns(cum8)
    dt_exp = _head_layout(_head_columns(dt8))
    a_exp = a512 * dt_exp
    return dict(dtr8=dtr8, a8=a8, mask=mask, mask_t=mask_t, cum_t=cum_t, dt_exp=dt_exp, a_exp=a_exp,
                cum_exp=_head_layout(cum_cols), cum_cols=cum_cols, tot_exp=_colsum(a_exp))


def _ssd_decay(cm, e):
    diff = cm["cum_cols"][e] - cm["cum_t"][e:e + 1, :]
    return jnp.where(cm["mask"], jnp.exp(jnp.minimum(diff, 0.0)), 0.0)


SSM_GPS = 2


def _ssd_specs(n_lat, n_ctx, step_of):
    rbs = [lambda i, d=d: _chunk_row_block(d, step_of(i), n_lat, n_ctx) for d in range(2)]
    xw, nw = SSM_GPS * SSM_GW, SSM_GPS * SSM_N
    specs = []
    for rb in rbs:
        specs += [pl.BlockSpec((SSM_C, xw), lambda g, i, rb=rb: (rb(i), g)),
                  pl.BlockSpec((SSM_C, nw), lambda g, i, rb=rb: (rb(i), SSM_INNER // nw + g)),
                  pl.BlockSpec((SSM_C, nw), lambda g, i, rb=rb: (rb(i), (SSM_INNER + SSM_G * SSM_N) // nw + g)),
                  pl.BlockSpec((SSM_C, 128), lambda g, i, rb=rb: (rb(i), COL_SM // 128))]
    specs += [pl.BlockSpec((2, SSM_GPS, 128, 128), lambda g, i: (0, g, 0, 0)),
              pl.BlockSpec((2, SSM_GPS, 1, 128), lambda g, i: (0, g, 0, 0)),
              pl.BlockSpec((2, SSM_GPS, 1, 128), lambda g, i: (0, g, 0, 0)),
              pl.BlockSpec((2, SSM_GPS, 1, SSM_GW), lambda g, i: (0, g, 0, 0))]
    return specs, rbs


def _ssd_fwd(xbc, parts, k, n_lat, n_ctx, comm):
    R = parts.shape[0]
    n_steps = n_lat + n_ctx

    def body(*refs):
        ins, (sel_ref, dtb_ref, a_ref, a512_ref), y_refs, hist_ref, st = refs[:8], refs[8:12], refs[12:14], refs[14], refs[15]

        @pl.when(pl.program_id(1) == 0)
        def _():
            st[...] = jnp.zeros_like(st)

        for d in range(2):
            x_ref, b_ref, c_ref, sm_ref = ins[4 * d:4 * d + 4]
            sm = sm_ref[...]
            for gg in range(SSM_GPS):
                sx, sn = slice(gg * SSM_GW, (gg + 1) * SSM_GW), slice(gg * SSM_N, (gg + 1) * SSM_N)
                cm = _ssd_common(sm, sel_ref[d, gg], dtb_ref[d, gg], a_ref[d, gg], a512_ref[d, gg], d)
                bm, cmat = _bf(b_ref[:, sn]), _bf(c_ref[:, sn])
                xdt = x_ref[:, sx] * cm["dt_exp"]
                cb = _nt(cmat, bm)
                ys = [_nn(_bf(cb * _ssd_decay(cm, e)), _bf(xdt[:, SSM_P * e:SSM_P * (e + 1)])) for e in range(SSM_HPG)]
                s_in = st[d, gg]
                hist_ref[d, 0, gg] = s_in
                y_refs[d][:, sx] = jnp.concatenate(ys, axis=1) + jnp.exp(cm["cum_exp"]) * _nn(cmat, _bf(s_in))
                st[d, gg] = jnp.exp(cm["tot_exp"]) * s_in + _tn(bm, _bf(xdt * jnp.exp(cm["tot_exp"] - cm["cum_exp"])))

    in_specs, rbs = _ssd_specs(n_lat, n_ctx, lambda i: i)
    out_specs = [pl.BlockSpec((SSM_C, SSM_GPS * SSM_GW), lambda g, i, rb=rb: (rb(i), g)) for rb in rbs]
    out_specs += [pl.BlockSpec((2, 1, SSM_GPS, SSM_N, SSM_GW), lambda g, i: (0, i, g, 0, 0))]
    out_shape = [jax.ShapeDtypeStruct((R, SSM_INNER), F32)] * 2 + [jax.ShapeDtypeStruct((2, n_steps, SSM_G, SSM_N, SSM_GW), F32)]
    args = [xbc, xbc, xbc, parts] * 2 + [k["sel"], k["dtb"], k["a"], k["a512"]]
    n_host_out = len(out_shape)
    outs = pl.pallas_call(
        _hosted(body, len(args), n_host_out, 1, comm, *_ssd_comm_steps(n_steps)), name="ssd_fwd",
        grid=(SSM_G // SSM_GPS, n_steps), in_specs=in_specs + [_ANY] * len(comm.arrays),
        out_specs=out_specs + [_ANY] * len(comm.out_shape), out_shape=out_shape + comm.out_shape,
        scratch_shapes=[pltpu.VMEM((2, SSM_GPS, SSM_N, SSM_GW), F32)] + comm.scratch,
        compiler_params=_cparams(("arbitrary", "arbitrary")),
    )(*args, *comm.arrays)
    return outs[:n_host_out], outs[n_host_out:]


def _ssd_comm_steps(n_steps):
    n_g = SSM_G // SSM_GPS
    at = lambda g, i: (pl.program_id(0) == g) & (pl.program_id(1) == i)
    return (lambda: at(0, 0)), (lambda: at(n_g // 2, 0)), (lambda: at(n_g - 1, n_steps - 1))


def _ssd_bwd(dy, xbc, parts, k, hist, n_lat, n_ctx, comm):
    R = parts.shape[0]
    n_steps = n_lat + n_ctx
    step_of = lambda j: n_steps - 1 - j

    def one(d, gg, x_ref, b_ref, c_ref, sm_ref, sel_ref, dtb_ref, a_ref, a512_ref, selt_ref, e512t_ref, dy_ref,
            hist_ref, dx_ref, db_ref, dc_ref, dsm_ref, acc_ref, dst):
        sx, sn = slice(gg * SSM_GW, (gg + 1) * SSM_GW), slice(gg * SSM_N, (gg + 1) * SSM_N)
        a_neg, e512_t = a_ref[d, gg], e512t_ref[...]
        cm = _ssd_common(sm_ref[...], sel_ref[d, gg], dtb_ref[d, gg], a_neg, a512_ref[d, gg], d)
        x, dyv = x_ref[:, sx], dy_ref[:, sx]
        bm, cmat = _bf(b_ref[:, sn]), _bf(c_ref[:, sn])
        xdt = x * cm["dt_exp"]
        cb = _nt(cmat, bm)
        s_in, ds = hist_ref[d, 0, gg], dst[d, gg]
        w = jnp.exp(cm["tot_exp"] - cm["cum_exp"])
        z = _nn(bm, _bf(ds))
        decay_in = jnp.exp(cm["cum_exp"])
        gy = _bf(dyv * decay_in)
        dcb = jnp.zeros((SSM_C, SSM_C), F32)
        dxs, crossing = [], []
        row = lax.broadcasted_iota(jnp.int32, (SSM_C, SSM_C), 0)
        col = lax.broadcasted_iota(jnp.int32, (SSM_C, SSM_C), 1)
        eye = (row == col).astype(BF16)
        before = (cm["mask_t"] - eye)
        for e in range(SSM_HPG):
            s = slice(SSM_P * e, SSM_P * (e + 1))
            lm = _ssd_decay(cm, e)
            dy_e = _bf(dyv[:, s])
            m_e = cb * lm
            dm_e = _nt(dy_e, _bf(xdt[:, s]))
            dcb = dcb + dm_e * lm
            dxs.append(_tn(_bf(m_e), dy_e))
            through = jnp.where(cm["mask"], _nn(_bf(dm_e * m_e), before), 0.0)
            crossing.append(_colsum(through))
        da_rows = jnp.concatenate(crossing + [jnp.zeros((128 - SSM_HPG, SSM_C), F32)], axis=0)
        r_hi, r_lo = _split(da_rows)
        da8_intra = _tn(r_hi, eye) + _tn(r_lo, eye)
        dx_state = w * z
        dxdt = jnp.concatenate(dxs, axis=1) + dx_state
        dcb = _bf(dcb)
        c_s = _nn(cmat, _bf(s_in))
        dc_ref[:, sn] = _nn(dcb, bm) + _nt(gy, _bf(s_in))
        db_ref[:, sn] = _tn(dcb, cmat) + _nt(_bf(w * xdt), _bf(ds))
        dst[d, gg] = jnp.exp(cm["tot_exp"]) * ds + _tn(cmat, gy)
        state_path = xdt * dx_state
        per_token = _nn_x(jnp.concatenate([dyv * decay_in * c_s - state_path, dxdt * x], axis=0), e512_t)
        totals = jnp.concatenate([_colsum(state_path), _colsum(ds * s_in), jnp.zeros((6, SSM_GW), F32)], axis=0)
        totals = _nn_x(totals, e512_t)
        tot8 = _colsum(cm["a8"])
        dtot8 = totals[0:1] + jnp.exp(tot8) * totals[1:2]
        da8 = da8_intra + _x_nn(cm["mask_t"], per_token[:SSM_C]) + dtot8
        ddt8 = da8 * a_neg + per_token[SSM_C:]
        dsm_ref[gg] = _nn_x(ddt8 * _sigmoid(cm["dtr8"]), selt_ref[d, gg])
        dx_ref[:, sx] = dxdt * cm["dt_exp"]
        acc_ref[d, gg, 0:1, :] += _colsum(da8 * cm["a8"])

    def body(*refs):
        ins, consts, (selt_ref, e512t_ref), dy_refs, hist_ref = refs[:8], refs[8:12], refs[12:14], refs[14:16], refs[16]
        outs, acc_ref, dst = refs[17:25], refs[25], refs[26]

        @pl.when(pl.program_id(1) == 0)
        def _():
            dst[...] = jnp.zeros_like(dst)
            acc_ref[...] = jnp.zeros_like(acc_ref)

        for d in range(2):
            for gg in range(SSM_GPS):
                one(d, gg, *ins[4 * d:4 * d + 4], *consts, selt_ref, e512t_ref, dy_refs[d], hist_ref,
                    *outs[4 * d:4 * d + 4], acc_ref, dst)

    xw, nw = SSM_GPS * SSM_GW, SSM_GPS * SSM_N
    in_specs, rbs = _ssd_specs(n_lat, n_ctx, step_of)
    in_specs += [pl.BlockSpec((2, SSM_GPS, 128, 128), lambda g, j: (0, g, 0, 0)), pl.BlockSpec((SSM_GW, 128), lambda g, j: (0, 0))]
    in_specs += [pl.BlockSpec((SSM_C, xw), lambda g, j, rb=rb: (rb(j), g)) for rb in rbs]
    in_specs += [pl.BlockSpec((2, 1, SSM_GPS, SSM_N, SSM_GW), lambda g, j: (0, step_of(j), g, 0, 0))]
    out_specs, out_shape = [], []
    for rb in rbs:
        out_specs += [pl.BlockSpec((SSM_C, xw), lambda g, j, rb=rb: (rb(j), g)),
                      pl.BlockSpec((SSM_C, nw), lambda g, j, rb=rb: (rb(j), g)),
                      pl.BlockSpec((SSM_C, nw), lambda g, j, rb=rb: (rb(j), g)),
                      pl.BlockSpec((SSM_GPS, SSM_C, 128), lambda g, j, rb=rb: (g, rb(j), 0))]
        out_shape += [jax.ShapeDtypeStruct((R, SSM_INNER), F32), jax.ShapeDtypeStruct((R, SSM_G * SSM_N), F32),
                      jax.ShapeDtypeStruct((R, SSM_G * SSM_N), F32), jax.ShapeDtypeStruct((SSM_G, R, 128), F32)]
    out_specs.append(pl.BlockSpec((2, SSM_GPS, 8, 128), lambda g, j: (0, g, 0, 0)))
    out_shape.append(jax.ShapeDtypeStruct((2, SSM_G, 8, 128), F32))
    args = [xbc, xbc, xbc, parts] * 2 + [k["sel"], k["dtb"], k["a"], k["a512"], k["sel_t"], k["e512_t"], dy, dy, hist]
    n_host_out = len(out_shape)
    outs = pl.pallas_call(
        _hosted(body, len(args), n_host_out, 1, comm, *_ssd_comm_steps(n_steps)), name="ssd_bwd",
        grid=(SSM_G // SSM_GPS, n_steps), in_specs=in_specs + [_ANY] * len(comm.arrays),
        out_specs=out_specs + [_ANY] * len(comm.out_shape), out_shape=out_shape + comm.out_shape,
        scratch_shapes=[pltpu.VMEM((2, SSM_GPS, SSM_N, SSM_GW), F32)] + comm.scratch,
        compiler_params=_cparams(("arbitrary", "arbitrary")),
    )(*args, *comm.arrays)
    return [(outs[n], outs[4 + n]) for n in range(4)] + [outs[8]], outs[n_host_out:]


def _gla_assemble(dq, dk, dv, dparts, tr):
    R = dq[0].shape[0]
    qk = GLA_H * GLA_DK

    def body(dqf_ref, dqb_ref, dkf_ref, dkb_ref, dvf_ref, dvb_ref, _, o_ref):
        o_ref[:, 0:qk] = (dqf_ref[...] + dqb_ref[...]).astype(BF16)
        o_ref[:, qk:2 * qk] = (dkf_ref[...] + dkb_ref[...]).astype(BF16)
        o_ref[:, 2 * qk:] = (dvf_ref[...] + dvb_ref[...]).astype(BF16)

    return pl.pallas_call(
        body, name="gla_assemble", grid=(R // tr,), in_specs=[_row_spec(tr, qk)] * 4 + [_row_spec(tr, D)] * 2 + [_ANY],
        out_specs=_dparts_out(tr, 2 * D, COL_Q), out_shape=jax.ShapeDtypeStruct(dparts.shape, BF16),
        input_output_aliases={6: 0}, compiler_params=_cparams(("arbitrary",)),
    )(*dq, *dk, *dv, dparts)


def _small_assemble(dp, dsm, parts, ut_hi, ut_lo, dparts, tr):
    R = parts.shape[0]
    qk = GLA_H * GLA_DK

    def body(dpf_ref, dpb_ref, dsmf_ref, dsmb_ref, sm_ref, uth_ref, utl_ref, _, o_ref, dup_ref, acc_ref, acc2_ref):
        @pl.when(pl.program_id(0) == 0)
        def _():
            dup_ref[...] = jnp.zeros_like(dup_ref)
            acc_ref[...] = jnp.zeros_like(acc_ref)
            acc2_ref[...] = jnp.zeros_like(acc2_ref)

        ssd = dsmf_ref[0] + dsmb_ref[0]
        for g in range(1, SSM_G):
            ssd = ssd + (dsmf_ref[g] + dsmb_ref[g])
        acc2_ref[0:1, :] += _colsum(ssd)
        sm_hi, sm_lo = _split(sm_ref[...])
        out = ssd
        for d, dp_ref in enumerate((dpf_ref, dpb_ref)):
            dpd = dp_ref[...]
            out = out + _nn3(dpd, uth_ref[d], utl_ref[d])
            p_hi, p_lo = _split(dpd)
            dup_ref[d] += _tn(sm_hi, p_hi) + _tn(sm_lo, p_hi) + _tn(sm_hi, p_lo)
            acc_ref[d:d + 1, :] += _colsum(dpd)
        o_ref[...] = out.astype(BF16)

    return pl.pallas_call(
        body, name="small_assemble", grid=(R // tr,),
        in_specs=[_row_spec(tr, qk)] * 2 + [pl.BlockSpec((SSM_G, tr, 128), lambda i: (0, i, 0))] * 2
        + [_row_spec(tr, 128, COL_SM // 128), pl.BlockSpec((2, qk, 128), lambda i: (0, 0, 0)),
           pl.BlockSpec((2, qk, 128), lambda i: (0, 0, 0)), _ANY],
        out_specs=[_dparts_out(tr, 128, COL_SM), pl.BlockSpec((2, 128, qk), lambda i: (0, 0, 0)), _acc_spec(qk), _acc_spec(128)],
        out_shape=[jax.ShapeDtypeStruct(dparts.shape, BF16), jax.ShapeDtypeStruct((2, 128, qk), F32),
                   jax.ShapeDtypeStruct((8, qk), F32), jax.ShapeDtypeStruct((8, 128), F32)],
        input_output_aliases={7: 0}, compiler_params=_cparams(("arbitrary",)),
    )(*dp, *dsm, parts, ut_hi, ut_lo, dparts)


ADA_ROWS = 16
ADA_TILE = 512


def _dot3_f32(a, b, ca, cb):
    a_hi, a_lo = _split(a)
    b_hi, b_lo = _split(b)
    return _dg(a_hi, b_hi, ca, cb) + _dg(a_lo, b_hi, ca, cb) + _dg(a_hi, b_lo, ca, cb)


def _ada_fwd(cvec, w, b):
    cols = w.shape[1]

    def body(c_ref, w_ref, b_ref, o_ref):
        o_ref[...] = _dot3_f32(_silu(c_ref[...]), w_ref[...], 1, 0) + b_ref[...]

    return pl.pallas_call(
        body, name="ada_fwd", grid=(cols // ADA_TILE,),
        in_specs=[pl.BlockSpec((ADA_ROWS, D), lambda j: (0, 0)), pl.BlockSpec((D, ADA_TILE), lambda j: (0, j)),
                  pl.BlockSpec((1, ADA_TILE), lambda j: (0, j))],
        out_specs=pl.BlockSpec((ADA_ROWS, ADA_TILE), lambda j: (0, j)), out_shape=jax.ShapeDtypeStruct((ADA_ROWS, cols), F32),
        compiler_params=_cparams(("arbitrary",)),
    )(cvec, w, b)


def _adam(w, g, m, v):
    m2 = ADAM_B1 * m + (1.0 - ADAM_B1) * g
    v2 = ADAM_B2 * v + (1.0 - ADAM_B2) * (g * g)
    m_hat = m2 / (1.0 - ADAM_B1 ** ADAM_STEP)
    v_hat = v2 / (1.0 - ADAM_B2 ** ADAM_STEP)
    return -ADAM_LR * (m_hat / (jnp.sqrt(v_hat) + ADAM_EPS) + ADAM_WD * w), m2, v2


def _wada_bwd_adam(cvec, dada, w, m, v):
    rows, cols = w.shape
    tr = _tile(rows, 256, 128)

    def body(c_ref, d_ref, w_ref, m_ref, v_ref, g_ref, dl_ref, m2_ref, v2_ref, p_ref):
        wv = w_ref[...]
        g = _dot3_f32(_silu(c_ref[...]), d_ref[...], 0, 0)
        g_ref[...] = g
        dl_ref[...], m2_ref[...], v2_ref[...] = _adam(wv, g, m_ref[...], v_ref[...])
        p_ref[...] = _dot3_f32(d_ref[...], wv, 1, 1)

    blk = pl.BlockSpec((tr, cols), lambda i: (i, 0))
    return pl.pallas_call(
        body, name="wada_bwd_adam", grid=(rows // tr,),
        in_specs=[pl.BlockSpec((ADA_ROWS, tr), lambda i: (0, i)), pl.BlockSpec((ADA_ROWS, cols), lambda i: (0, 0)), blk, blk, blk],
        out_specs=[blk, blk, blk, blk, pl.BlockSpec((ADA_ROWS, tr), lambda i: (0, i))],
        out_shape=[jax.ShapeDtypeStruct((rows, cols), F32)] * 4 + [jax.ShapeDtypeStruct((ADA_ROWS, rows), F32)],
        compiler_params=_cparams(("arbitrary",)),
    )(cvec, dada, w, m, v)


def _reduce_adam(parts8, w, m, v, name):
    rows, cols = w.shape
    tr = _tile(rows, 64, 16)

    def body(p_ref, w_ref, m_ref, v_ref, g_ref, dl_ref, m2_ref, v2_ref):
        g = p_ref[0].astype(F32) + p_ref[N_CHIPS].astype(F32)
        for j in range(1, N_CHIPS):
            g = g + (p_ref[j].astype(F32) + p_ref[N_CHIPS + j].astype(F32))
        g_ref[...] = g
        dl_ref[...], m2_ref[...], v2_ref[...] = _adam(w_ref[...], g, m_ref[...], v_ref[...])

    blk = pl.BlockSpec((tr, cols), lambda i: (i, 0))
    return pl.pallas_call(
        body, name=name, grid=(rows // tr,), in_specs=[pl.BlockSpec((N_DEV, tr, cols), lambda i: (0, i, 0)), blk, blk, blk],
        out_specs=[blk] * 4, out_shape=[jax.ShapeDtypeStruct((rows, cols), F32)] * 4, compiler_params=_cparams(("arbitrary",)),
    )(parts8, w, m, v)


SMALL_W = 1024


def _sum8(g8):
    rows = g8.shape[1]

    def body(g_ref, o_ref):
        s = g_ref[0]
        for j in range(1, N_DEV):
            s = s + g_ref[j]
        o_ref[...] = s

    return pl.pallas_call(
        body, name="sum8", out_shape=jax.ShapeDtypeStruct((rows, SMALL_W), F32),
        in_specs=[pl.BlockSpec(memory_space=pltpu.VMEM)], out_specs=pl.BlockSpec(memory_space=pltpu.VMEM),
        compiler_params=pltpu.CompilerParams(vmem_limit_bytes=VMEM_LIMIT),
    )(g8)


def _cctx_grad(p8, c_ctx):
    def body(p_ref, c_ref, o_ref):
        s = p_ref[0]
        for chip in range(1, N_CHIPS):
            s = s + p_ref[2 * chip]
        o_ref[...] = s * _dsilu(c_ref[...])

    return pl.pallas_call(
        body, name="cctx_grad", out_shape=jax.ShapeDtypeStruct((1, D), F32),
        in_specs=[pl.BlockSpec(memory_space=pltpu.VMEM)] * 2, out_specs=pl.BlockSpec(memory_space=pltpu.VMEM),
    )(p8, c_ctx)


def _adam_small(w, g, m, v):
    def body(w_ref, g_ref, m_ref, v_ref, dl_ref, m2_ref, v2_ref):
        dl_ref[...], m2_ref[...], v2_ref[...] = _adam(w_ref[...], g_ref[...], m_ref[...], v_ref[...])

    vm = pl.BlockSpec(memory_space=pltpu.VMEM)
    return pl.pallas_call(
        body, name="adam_small", out_shape=[jax.ShapeDtypeStruct(w.shape, F32)] * 3, in_specs=[vm] * 4, out_specs=[vm] * 3,
        compiler_params=pltpu.CompilerParams(vmem_limit_bytes=VMEM_LIMIT),
    )(w, g, m, v)


def _pack(vecs, width=SMALL_W, row_mult=8):
    flat = jnp.concatenate([v.reshape(-1).astype(F32) for v in vecs])
    n = flat.shape[0]
    rows = -(-n // (width * row_mult)) * row_mult
    return jnp.pad(flat, (0, rows * width - n)).reshape(rows, width)


def _unpack(packed, shapes):
    flat = packed.reshape(-1)
    out, off = [], 0
    for s in shapes:
        n = int(np.prod(s))
        out.append(flat[off:off + n].reshape(s))
        off += n
    return out


WEIGHTS = ('c_ctx', 'w_ada', 'b_ada', 'norm1_w', 'w_in', 'gla_up_f', 'gla_bias_f', 'gla_up_b', 'gla_bias_b', 'gla_norm_w',
           'conv_w', 'conv_b', 'dt_bias_f', 'dt_bias_b', 'a_log_f', 'a_log_b', 'd_skip', 'ssm_norm_w', 'w_pa', 'w_pb', 'w_out',
           'norm2_w', 'w_gate', 'w_up', 'w_down', 'final_norm_w')
BIG = ('w_in', 'w_pa', 'w_pb', 'w_out', 'w_gate', 'w_up', 'w_down')
COL_SHARDED = ('w_in', 'w_gate', 'w_up')
SMALL_SHARDED = ('gla_up_f', 'gla_up_b', 'conv_w')
ROW_TILE = 256


def _blocks_to_full(g4, name):
    n, r, c = g4.shape
    return g4.transpose(1, 0, 2).reshape(r, n * c) if name in COL_SHARDED else g4.reshape(n * r, c)


def _full_to_blocks(full, name):
    r, c = full.shape
    if name in COL_SHARDED:
        return full.reshape(r, N_CHIPS, c // N_CHIPS).transpose(1, 0, 2)
    return full.reshape(N_CHIPS, r // N_CHIPS, c)


def _permute_in(w_in_full):
    off = np.concatenate([[0], np.cumsum(IN_WIDTHS)])
    cols = [w_in_full[:, off[p]:off[p + 1]] for p in PERM]
    return jnp.concatenate(cols + [jnp.zeros((w_in_full.shape[0], SMALL_PAD), w_in_full.dtype)], axis=1)


def _unpermute_in(wp):
    off = np.concatenate([[0], np.cumsum([IN_WIDTHS[p] for p in PERM])])
    pieces = {p: wp[:, off[i]:off[i + 1]] for i, p in enumerate(PERM)}
    return jnp.concatenate([pieces[p] for p in range(len(IN_WIDTHS))], axis=1)


def _chip_cols(full, chip, n):
    return lax.dynamic_slice_in_dim(full, chip * n, n, axis=1)


def kernel(x, c, ctx, c_ctx, w_ada, b_ada, norm1_w, w_in, gla_up_f, gla_bias_f, gla_up_b, gla_bias_b, gla_norm_w, conv_w, conv_b, dt_bias_f, dt_bias_b, a_log_f, a_log_b, d_skip, ssm_norm_w, w_pa, w_pb, w_out, norm2_w, w_gate, w_up, w_down, final_norm_w, loss_target, m_c_ctx, m_w_ada, m_b_ada, m_norm1_w, m_w_in, m_gla_up_f, m_gla_bias_f, m_gla_up_b, m_gla_bias_b, m_gla_norm_w, m_conv_w, m_conv_b, m_dt_bias_f, m_dt_bias_b, m_a_log_f, m_a_log_b, m_d_skip, m_ssm_norm_w, m_w_pa, m_w_pb, m_w_out, m_norm2_w, m_w_gate, m_w_up, m_w_down, m_final_norm_w, v_c_ctx, v_w_ada, v_b_ada, v_norm1_w, v_w_in, v_gla_up_f, v_gla_bias_f, v_gla_up_b, v_gla_bias_b, v_gla_norm_w, v_conv_w, v_conv_b, v_dt_bias_f, v_dt_bias_b, v_a_log_f, v_a_log_b, v_d_skip, v_ssm_norm_w, v_w_pa, v_w_pb, v_w_out, v_norm2_w, v_w_gate, v_w_up, v_w_down, v_final_norm_w):
    given = dict(locals())
    W = {n: given[n] for n in WEIGHTS}
    M = {n: given["m_" + n] for n in WEIGHTS}
    V = {n: given["v_" + n] for n in WEIGHTS}
    L, Lc = x.shape[1], ctx.shape[1]
    tr = ROW_TILE
    assert L % tr == 0 and Lc % tr == 0 and L % Lc == 0 and Lc % SSM_C == 0
    n_lat_tiles = L // tr
    xi, yi, ci = _place()
    chip, me = 2 * xi + yi, 4 * xi + 2 * yi + ci
    xall = jnp.concatenate([x[0], ctx[0]], axis=0)

    g0 = _allgather_small(_pack([c[0]] + [W[n][0] for n in SMALL_SHARDED]), "gather_c")
    g0 = g0.reshape(N_DEV, -1)
    c_all = g0[:, :D]
    small_full, off = {}, D
    for n in SMALL_SHARDED:
        r, cols = W[n].shape[1:]
        small_full[n] = jnp.concatenate([g0[2 * k, off:off + r * cols].reshape(r, cols) for k in range(N_CHIPS)], axis=1)
        off += r * cols
    up_f, up_b, conv_w_full = (small_full[n] for n in SMALL_SHARDED)

    cvec = jnp.zeros((ADA_ROWS, D), F32).at[:N_DEV].set(c_all).at[N_DEV].set(c_ctx)
    ada_cols = w_ada.shape[2]
    ada_part = _ada_fwd(cvec, w_ada[0], _chip_cols(b_ada, chip, ada_cols))
    g1_all = _allgather_small(ada_part, "gather_ada")
    ada_full = jnp.concatenate([g1_all[2 * k] for k in range(N_CHIPS)], axis=1)
    mine = lax.dynamic_slice_in_dim(ada_full, me, 1, axis=0)
    sh1, sc1, g1, sh2, sc2, g2 = (mine[:, k * D:(k + 1) * D] for k in range(6))
    csh1, csc1 = ada_full[N_DEV:N_DEV + 1, :D], ada_full[N_DEV:N_DEV + 1, D:2 * D]
    mod = jnp.stack([jnp.stack([sh1, sc1]), jnp.stack([csh1, csc1])])

    full = {'w_in': _blocks_to_full(_gather_split(w_in[0].astype(BF16), "gather_w_in"), 'w_in')}
    wp = _permute_in(full['w_in'])
    later = [n for n in BIG if n != 'w_in']

    def lr_rows(up, base):
        return jnp.zeros((128, GLA_H * GLA_DK), F32).at[base:base + GLA_RANK].set(up)
    u2 = jnp.stack([lr_rows(up_f, SM_LRF), lr_rows(up_b, SM_LRB)])
    u2_hi = u2.astype(BF16)
    u2_lo = (u2 - u2_hi.astype(F32)).astype(BF16)
    ut = u2.transpose(0, 2, 1)
    ut_hi = ut.astype(BF16)
    ut_lo = (ut - ut_hi.astype(F32)).astype(BF16)
    gbias = jnp.stack([gla_bias_f, gla_bias_b])
    kc = _ssd_consts(jnp.stack([dt_bias_f[0], dt_bias_b[0]]), jnp.stack([a_log_f[0], a_log_b[0]]))
    gw4 = jnp.tile(gla_norm_w, (1, GLA_H))
    dskip_exp = jnp.repeat(d_skip, SSM_P, axis=1)
    n_gla = (L // GLA_C, Lc // GLA_C)
    n_ssd = (L // SSM_C, Lc // SSM_C)

    h1 = _norm_mod(xall, norm1_w, mod, n_lat_tiles, tr)
    parts = _mm(h1, wp, "nn", F32, "mm_in", tm=768, tn=1152)
    xbc = _conv_fwd(parts, conv_w_full, conv_b, L // Lc, Lc)
    *o2, gla_hist = _gla_fwd(parts, u2_hi, u2_lo, gbias, *n_gla)
    (*y2, ssd_hist), gathered = _ssd_fwd(xbc, parts, kc, *n_ssd, _gather_comm([W[n][0].astype(BF16) for n in later]))
    full.update({n: _blocks_to_full(g, n) for n, g in zip(later, gathered)})
    oan = _gla_out(o2, parts, gw4, tr)
    obn = _ssd_out(y2, xbc, parts, dskip_exp, ssm_norm_w, tr)
    ya = _mm(oan, full['w_pa'], "nn", F32, "mm_pa", tm=768)
    yb = _mm(obn, full['w_pb'], "nn", F32, "mm_pb", tm=768)
    merged = _merge(ya, yb, parts, tr)
    mix = _mm(merged, full['w_out'], "nn", F32, "mm_out", tm=768)
    h2, u = _resid_norm_mod(xall, mix, g1, norm2_w, sh2, sc2, tr)
    gp = _mm(u, full['w_gate'], "nn", F32, "mm_gate", tm=768, tn=1408)
    up = _mm(u, full['w_up'], "nn", F32, "mm_up", tm=768, tn=1408)
    act = _swiglu_act(gp, up, tr)
    f = _mm(act, full['w_down'], "nn", F32, "mm_down", tm=768)
    dh3, df, acc_loss = _loss_head(h2, f, loss_target[0], g2, final_norm_w[None], n_lat_tiles, tr)

    dw = {}
    da = _mm(df, full['w_down'], "nt", F32, "mm_down_dx", tm=768, tn=1408)
    dw['w_down'] = _mm(act, df, "tn", BF16, "mm_down_dw", tm=1408, tk=768)
    dgp, dup = _swiglu_act_bwd(da, gp, up, tr)
    du_a = _mm(dgp, full['w_gate'], "nt", F32, "mm_gate_dx", tm=768, tk=1408)
    du_b = _mm(dup, full['w_up'], "nt", F32, "mm_up_dx", tm=768, tk=1408)
    dw['w_gate'] = _mm(u, dgp, "tn", BF16, "mm_gate_dw", tm=1024, tn=1408, tk=768)
    dw['w_up'] = _mm(u, dup, "tn", BF16, "mm_up_dw", tm=1024, tn=1408, tk=768)
    dh2, dmix, acc_ffn = _ffn_in_bwd(du_a, du_b, h2, dh3, mix, sc2, g1, norm2_w, tr)
    dmerged = _mm(dmix, full['w_out'], "nt", F32, "mm_out_dx", tm=768)
    dw['w_out'] = _mm(merged, dmix, "tn", BF16, "mm_out_dw", tm=1024, tk=768)
    dya, dyb, dparts = _merge_bwd(dmerged, ya, yb, parts, lax.empty((L + Lc, PW), BF16), tr)
    doan = _mm(dya, full['w_pa'], "nt", F32, "mm_pa_dx", tm=768)
    dw['w_pa'] = _mm(oan, dya, "tn", BF16, "mm_pa_dw", tm=1024, tk=768)
    dobn = _mm(dyb, full['w_pb'], "nt", F32, "mm_pb_dx", tm=768)
    dw['w_pb'] = _mm(obn, dyb, "tn", BF16, "mm_pb_dw", tm=1024, tk=768)
    do, dparts, acc_gla = _gla_out_bwd(doan, o2, parts, gw4, dparts, tr)
    dq, dk, dv, dpre = _gla_bwd(do, parts, u2_hi, u2_lo, gbias, gla_hist, *n_gla)
    dy, dparts, acc_ssd = _ssd_out_bwd(dobn, y2, xbc, parts, dskip_exp, ssm_norm_w, dparts, tr)
    (dx_scan, db_scan, dc_scan, dsm, acc_alog), exchanged = _ssd_bwd(
        dy, xbc, parts, kc, ssd_hist, *n_ssd, _exchange_comm([_full_to_blocks(dw[n], n) for n in later]))
    exchanged = dict(zip(later, exchanged))
    dparts, acc_conv = _conv_bwd(dx_scan, db_scan, dc_scan, dy, dskip_exp, parts, conv_w_full, conv_b, dparts, L // Lc, Lc)
    dparts = _gla_assemble(dq, dk, dv, dparts, tr)
    dparts, dup_gla, acc_gbias, acc_dtb = _small_assemble(dpre, dsm, parts, ut_hi, ut_lo, dparts, tr)
    dw['w_in'] = _unpermute_in(_mm(h1, dparts, "tn", BF16, "mm_in_dw", tm=1024, tn=1152, tk=768))
    dh1, (exchanged['w_in'],) = _mm(dparts, wp, "nt", F32, "mm_in_dx", tm=768, tk=1152,
                                    comm=_exchange_comm([_full_to_blocks(dw['w_in'], 'w_in')]))
    dxall, acc_n1 = _norm1_bwd(dh1, xall, dh2, norm1_w, mod, n_lat_tiles, tr)

    partial = dict(
        norm1_w=acc_n1[0, 2] + acc_n1[1, 2],
        gla_up_f=dup_gla[0, SM_LRF:SM_LRF + GLA_RANK], gla_bias_f=acc_gbias[0],
        gla_up_b=dup_gla[1, SM_LRB:SM_LRB + GLA_RANK], gla_bias_b=acc_gbias[1],
        gla_norm_w=acc_gla[0].reshape(GLA_H, GLA_DV).sum(0),
        conv_w=acc_conv[:SSM_CONV], conv_b=acc_conv[SSM_CONV],
        dt_bias_f=acc_dtb[0, SM_DTF:SM_DTF + SSM_HEADS], dt_bias_b=acc_dtb[0, SM_DTB:SM_DTB + SSM_HEADS],
        a_log_f=acc_alog[0, :, 0, :SSM_HPG], a_log_b=acc_alog[1, :, 0, :SSM_HPG],
        d_skip=acc_ssd[1].reshape(SSM_HEADS, SSM_P).sum(1), ssm_norm_w=acc_ssd[0],
        norm2_w=acc_ffn[2], final_norm_w=acc_loss[0],
    )
    dada = jnp.concatenate([acc_n1[0, 1], acc_n1[0, 0], acc_ffn[3], acc_ffn[1], acc_ffn[0], acc_loss[1]])
    dada_ctx = jnp.concatenate([acc_n1[1, 1], acc_n1[1, 0], jnp.zeros((4 * D,), F32)])
    names = list(partial)
    payload = [partial[n] for n in names] + [dada + dada_ctx, dada_ctx, acc_loss[2], dada]
    sizes = [int(np.prod(p.shape)) for p in payload]
    g8 = _allgather_small(_pack(payload), "gather_small_grads")
    summed = _unpack(_sum8(g8), [(s,) for s in sizes])
    grads = {n: s.reshape(W[n].shape if n not in SMALL_SHARDED else partial[n].shape) for n, s in zip(names, summed)}
    grads['b_ada'] = summed[len(names)].reshape(b_ada.shape)
    dada_ctx_sum = summed[len(names) + 1]
    loss = jnp.sum(summed[len(names) + 2])
    dada_all = g8.reshape(N_DEV, -1)[:, sum(sizes[:-1]):sum(sizes)]

    dada16 = jnp.zeros((ADA_ROWS, ada_cols), F32)
    dada16 = dada16.at[:N_DEV].set(_chip_cols(dada_all, chip, ada_cols)).at[N_DEV].set(_chip_cols(dada_ctx_sum[None], chip, ada_cols)[0])
    g_wada, dl_wada, m_wada, v_wada, p16 = _wada_bwd_adam(cvec, dada16, w_ada[0], m_w_ada[0], v_w_ada[0])
    p8 = _allgather_small(p16[N_DEV:], "gather_cctx")
    grads['c_ctx'] = _cctx_grad(p8[:, 0:1, :], c_ctx[None])[0]
    for n in SMALL_SHARDED:
        grads[n] = _chip_cols(grads[n], chip, W[n].shape[2])[None]

    small = [n for n in WEIGHTS if n not in BIG and n != 'w_ada']
    shapes = [W[n].shape for n in small]
    dl_s, m_s, v_s = _adam_small(*[_pack([d[n] for n in small]) for d in (W, grads, M, V)])
    delta = dict(zip(small, _unpack(dl_s, shapes)))
    new_m = dict(zip(small, _unpack(m_s, shapes)))
    new_v = dict(zip(small, _unpack(v_s, shapes)))
    grads['w_ada'], delta['w_ada'], new_m['w_ada'], new_v['w_ada'] = g_wada[None], dl_wada[None], m_wada[None], v_wada[None]

    for n in BIG:
        g, dl, m2, v2 = _reduce_adam(exchanged[n], W[n][0], M[n][0], V[n][0], "adam_" + n)
        grads[n], delta[n], new_m[n], new_v[n] = g[None], dl[None], m2[None], v2[None]

    return (loss, dxall[:L][None], *[grads[n] for n in WEIGHTS], *[delta[n] for n in WEIGHTS],
            *[new_m[n] for n in WEIGHTS], *[new_v[n] for n in WEIGHTS])
```

```python
import functools

import numpy as np
import jax
import jax.numpy as jnp
from jax import lax
from jax.experimental import pallas as pl
from jax.experimental.pallas import tpu as pltpu

F32 = jnp.float32
BF16 = jnp.bfloat16
MESH = pl.DeviceIdType.MESH

D = 1024
EPS = 1e-6
GRID_W = 64
GLA_H, GLA_DK, GLA_DV, GLA_RANK, GLA_TAU = 4, 128, 256, 16, 16.0
GLA_C = 64
SSM_INNER, SSM_P, SSM_HEADS, SSM_G, SSM_HPG, SSM_N = 2048, 64, 32, 4, 8, 128
SSM_C = 128
SSM_CONV, CONV_LEFT = 4, 2
D_FF = 2816
IN_WIDTHS = (512, 512, 1024, 1024, 16, 16, 2048, 2048, 512, 512, 32, 32, 1024, 1024)
D_IN = sum(IN_WIDTHS)
PERM = (6, 7, 8, 9, 3, 0, 1, 2, 12, 13, 4, 5, 10, 11)
PW = 10368
SMALL_PAD = PW - D_IN
COL_Z, COL_XBC, COL_R, COL_Q, COL_K, COL_V, COL_GA, COL_GB, COL_SM = 0, 2048, 5120, 6144, 6656, 7168, 8192, 9216, 10240
SM_LRF, SM_LRB, SM_DTF, SM_DTB = 0, 16, 32, 64
EXP_CLAMP = 80.0
ADAM_LR, ADAM_B1, ADAM_B2, ADAM_EPS, ADAM_WD, ADAM_STEP = 0.001, 0.9, 0.999, 1e-08, 0.01, 10
N_CHIPS, N_DEV = 4, 8
VMEM_LIMIT = 56 * 1024 * 1024


def _cparams(sem=None):
    return pltpu.CompilerParams(dimension_semantics=sem, vmem_limit_bytes=VMEM_LIMIT)


def _dg(a, b, ca, cb):
    return lax.dot_general(a, b, (((ca,), (cb,)), ((), ())), preferred_element_type=F32)


def _nn(a, b):
    return _dg(a, b, 1, 0)


def _nt(a, b):
    return _dg(a, b, 1, 1)


def _tn(a, b):
    return _dg(a, b, 0, 0)


def _bf(x):
    return x.astype(BF16)


def _split(x):
    hi = x.astype(BF16)
    return hi, (x - hi.astype(F32)).astype(BF16)


def _nn_x(a, b_exact):
    hi, lo = _split(a)
    return _nn(hi, b_exact) + _nn(lo, b_exact)


def _x_nn(a_exact, b):
    hi, lo = _split(b)
    return _nn(a_exact, hi) + _nn(a_exact, lo)


def _nn3(a, b_hi, b_lo):
    hi, lo = _split(a)
    return _nn(hi, b_hi) + _nn(lo, b_hi) + _nn(hi, b_lo)


def _sigmoid(x):
    return 1.0 / (1.0 + jnp.exp(-x))


def _silu(x):
    return x * _sigmoid(x)


def _dsilu(x):
    s = _sigmoid(x)
    return s * (1.0 + x * (1.0 - s))


def _softplus(x):
    return jnp.maximum(x, 0.0) + jnp.log(1.0 + jnp.exp(-jnp.abs(x)))


def _log_sigmoid(x):
    return jnp.minimum(x, 0.0) - jnp.log(1.0 + jnp.exp(-jnp.abs(x)))


def _tile(n, target, mult=8):
    best = None
    for t in range(mult, min(n, target) + 1, mult):
        if n % t == 0:
            best = t
    assert best is not None, (n, target, mult)
    return best


def _mm(a, b, mode, out_dtype, name, tm=512, tn=1024, tk=2048, comm=None):
    if mode == "nn":
        (M, K), N = a.shape, b.shape[1]
    elif mode == "nt":
        (M, K), N = a.shape, b.shape[0]
    else:
        (K, M), N = a.shape, b.shape[1]
    tm, tn, tk = _tile(M, tm, 128), _tile(N, tn, 128), _tile(K, tk, 128)
    nk = K // tk
    ca, cb = {"nn": (1, 0), "nt": (1, 1), "tn": (0, 0)}[mode]

    def body(a_ref, b_ref, o_ref, *acc):
        part = _dg(a_ref[...], b_ref[...], ca, cb)
        if nk == 1:
            o_ref[...] = part.astype(out_dtype)
        else:
            k = pl.program_id(2)

            @pl.when(k == 0)
            def _():
                acc[0][...] = part

            @pl.when(k > 0)
            def _():
                acc[0][...] += part

            @pl.when(k == nk - 1)
            def _():
                o_ref[...] = acc[0][...].astype(out_dtype)

    a_spec = pl.BlockSpec((tk, tm), lambda i, j, k: (k, i)) if mode == "tn" else pl.BlockSpec((tm, tk), lambda i, j, k: (i, k))
    b_spec = pl.BlockSpec((tn, tk), lambda i, j, k: (j, k)) if mode == "nt" else pl.BlockSpec((tk, tn), lambda i, j, k: (k, j))
    gi, gj = M // tm, N // tn
    scratch = [pltpu.VMEM((tm, tn), F32)] if nk > 1 else []
    out_spec, out_shape = pl.BlockSpec((tm, tn), lambda i, j, k: (i, j)), jax.ShapeDtypeStruct((M, N), out_dtype)
    if comm is None:
        return pl.pallas_call(
            body, name=name, grid=(gi, gj, nk), in_specs=[a_spec, b_spec], out_specs=out_spec, out_shape=out_shape,
            scratch_shapes=scratch, compiler_params=_cparams(("arbitrary", "arbitrary", "arbitrary")),
        )(a, b)
    at = lambda i, j, k: (pl.program_id(0) == i) & (pl.program_id(1) == j) & (pl.program_id(2) == k)
    hosted = _hosted(body, 2, 1, len(scratch), comm, lambda: at(0, 0, 0), lambda: at(gi - 1, 0, 0),
                     lambda: at(gi - 1, gj - 1, nk - 1))
    outs = pl.pallas_call(
        hosted, name=name, grid=(gi, gj, nk), in_specs=[a_spec, b_spec] + [_ANY] * len(comm.arrays),
        out_specs=[out_spec] + [_ANY] * len(comm.out_shape), out_shape=[out_shape] + comm.out_shape,
        scratch_shapes=scratch + comm.scratch, compiler_params=_cparams(("arbitrary", "arbitrary", "arbitrary")),
    )(a, b, *comm.arrays)
    return outs[0], outs[1:]


def _place():
    return lax.axis_index("x"), lax.axis_index("y"), lax.axis_index("c")


def _flip(v, bit):
    return 1 - v if bit else v


def _allgather_small(v, name):
    R, C = v.shape

    def body(v_ref, out_ref, send_sems, recv_sems, local_sem):
        x, y, c = _place()
        me = 4 * x + 2 * y + c
        mine = pltpu.make_async_copy(v_ref, out_ref.at[me], local_sem)
        mine.start()

        def peer(r):
            return _flip(x, (r >> 2) & 1), _flip(y, (r >> 1) & 1), _flip(c, r & 1)

        sends = [pltpu.make_async_remote_copy(
            src_ref=v_ref, dst_ref=out_ref.at[me], send_sem=send_sems.at[r - 1], recv_sem=recv_sems.at[r - 1],
            device_id=peer(r), device_id_type=MESH) for r in range(1, N_DEV)]
        for cp in sends:
            cp.start()
        for r in range(1, N_DEV):
            px, py, pc = peer(r)
            pltpu.make_async_remote_copy(
                src_ref=v_ref, dst_ref=out_ref.at[4 * px + 2 * py + pc], send_sem=send_sems.at[r - 1],
                recv_sem=recv_sems.at[r - 1], device_id=(x, y, c), device_id_type=MESH).wait_recv()
        for cp in sends:
            cp.wait_send()
        mine.wait()

    return pl.pallas_call(
        body, name=name, out_shape=jax.ShapeDtypeStruct((N_DEV, R, C), v.dtype),
        in_specs=[pl.BlockSpec(memory_space=pltpu.VMEM)], out_specs=pl.BlockSpec(memory_space=pltpu.VMEM),
        scratch_shapes=[pltpu.SemaphoreType.DMA((N_DEV - 1,)), pltpu.SemaphoreType.DMA((N_DEV - 1,)), pltpu.SemaphoreType.DMA],
        compiler_params=pltpu.CompilerParams(vmem_limit_bytes=VMEM_LIMIT),
    )(v)


_CHIP_RELATIONS = ((1, 0), (0, 1), (1, 1))


def _gather_split(shard, name):
    rows, cols = shard.shape
    half = rows // 2

    def body(in_ref, out_ref, send_sems, recv_sems, local_sem):
        x, y, c = _place()
        chip = 2 * x + y
        mine = pl.ds(pl.multiple_of(c * half, 16), half)
        local = pltpu.make_async_copy(in_ref, out_ref.at[chip], local_sem)
        local.start()
        peers = [(_flip(x, fx), _flip(y, fy)) for fx, fy in _CHIP_RELATIONS]
        sends = [pltpu.make_async_remote_copy(
            src_ref=in_ref.at[mine], dst_ref=out_ref.at[chip, mine], send_sem=send_sems.at[j], recv_sem=recv_sems.at[j],
            device_id=(px, py, c), device_id_type=MESH) for j, (px, py) in enumerate(peers)]
        for cp in sends:
            cp.start()
        for j, (px, py) in enumerate(peers):
            landed = out_ref.at[2 * px + py, mine]
            pltpu.make_async_remote_copy(
                src_ref=landed, dst_ref=landed, send_sem=send_sems.at[j], recv_sem=recv_sems.at[j],
                device_id=(x, y, c), device_id_type=MESH).wait_recv()
            fwd = pltpu.make_async_remote_copy(
                src_ref=landed, dst_ref=landed, send_sem=send_sems.at[3 + j], recv_sem=recv_sems.at[3 + j],
                device_id=(x, y, 1 - c), device_id_type=MESH)
            fwd.start()
            sends.append(fwd)
        for j in range(3):
            landed = out_ref.at[0, mine]
            pltpu.make_async_remote_copy(
                src_ref=landed, dst_ref=landed, send_sem=send_sems.at[3 + j], recv_sem=recv_sems.at[3 + j],
                device_id=(x, y, c), device_id_type=MESH).wait_recv()
        for cp in sends:
            cp.wait_send()
        local.wait()

    any_spec = pl.BlockSpec(memory_space=pl.ANY)
    return pl.pallas_call(
        body, name=name, out_shape=jax.ShapeDtypeStruct((N_CHIPS, rows, cols), shard.dtype),
        in_specs=[any_spec], out_specs=any_spec,
        scratch_shapes=[pltpu.SemaphoreType.DMA((6,)), pltpu.SemaphoreType.DMA((6,)), pltpu.SemaphoreType.DMA],
    )(shard)


class _Comm:
    def __init__(self, arrays, out_shape, scratch, start, middle, finish):
        self.arrays, self.out_shape, self.scratch = arrays, out_shape, scratch
        self.start, self.middle, self.finish = start, middle, finish


def _hosted(body, n_in, n_out, n_scratch, comm, first, middle, last):
    nc, no = len(comm.arrays), len(comm.out_shape)

    def wrapped(*refs):
        a = n_in + nc
        b = a + n_out + no
        ins, c_ins, outs, c_outs = refs[:n_in], refs[n_in:a], refs[a:a + n_out], refs[a + n_out:b]
        scratch, c_sems = refs[b:b + n_scratch], refs[b + n_scratch:]

        @pl.when(first())
        def _():
            comm.start(c_ins, c_outs, c_sems)

        body(*ins, *outs, *scratch)
        if comm.middle is not None:
            @pl.when(middle())
            def _():
                comm.middle(c_ins, c_outs, c_sems)

        @pl.when(last())
        def _():
            comm.finish(c_ins, c_outs, c_sems)

    return wrapped


_ANY = pl.BlockSpec(memory_space=pl.ANY)


def _gather_comm(shards):
    n = len(shards)

    def copies(kind, ins, outs, sems):
        send_sems, recv_sems, local_sems = sems
        x, y, c = _place()
        chip = 2 * x + y
        if kind == "local":
            return [pltpu.make_async_copy(ins[i], outs[i].at[chip], local_sems.at[i]) for i in range(n)]
        made = []
        for i in range(n):
            for j, (fx, fy) in enumerate(_CHIP_RELATIONS):
                px, py = _flip(x, fx), _flip(y, fy)
                slot, to = (chip, (px, py, c)) if kind == "send" else (2 * px + py, (x, y, c))
                made.append(pltpu.make_async_remote_copy(
                    src_ref=ins[i], dst_ref=outs[i].at[slot], send_sem=send_sems.at[i, j], recv_sem=recv_sems.at[i, j],
                    device_id=to, device_id_type=MESH))
        return made

    def start(ins, outs, sems):
        for cp in copies("local", ins, outs, sems) + copies("send", ins, outs, sems):
            cp.start()

    def finish(ins, outs, sems):
        for cp in copies("recv", ins, outs, sems):
            cp.wait_recv()
        for cp in copies("send", ins, outs, sems):
            cp.wait_send()
        for cp in copies("local", ins, outs, sems):
            cp.wait()

    return _Comm(list(shards), [jax.ShapeDtypeStruct((N_CHIPS,) + s.shape, s.dtype) for s in shards],
                 [pltpu.SemaphoreType.DMA((n, 3)), pltpu.SemaphoreType.DMA((n, 3)), pltpu.SemaphoreType.DMA((n,))],
                 start, None, finish)


def _exchange_comm(blocks):
    n = len(blocks)

    def copies(kind, ins, outs, sems):
        send_sems, recv_sems, local_sems = sems
        x, y, c = _place()
        chip = 2 * x + y
        me, sibling = (x, y, c), (x, y, 1 - c)

        def remote(src, dst, i, j, to):
            return pltpu.make_async_remote_copy(src_ref=src, dst_ref=dst, send_sem=send_sems.at[i, j],
                                                recv_sem=recv_sems.at[i, j], device_id=to, device_id_type=MESH)

        made = []
        for i in range(n):
            if kind == "local":
                made.append(pltpu.make_async_copy(ins[i].at[chip], outs[i].at[chip], local_sems.at[i]))
                continue
            for j, (fx, fy) in enumerate(_CHIP_RELATIONS):
                px, py = _flip(x, fx), _flip(y, fy)
                src = 2 * px + py
                if kind == "first":
                    made.append(remote(ins[i].at[src], outs[i].at[chip], i, j, (px, py, c)))
                elif kind == "landed":
                    made.append(remote(ins[i].at[src], outs[i].at[src], i, j, me))
                elif kind == "passed":
                    made.append(remote(outs[i].at[src], outs[i].at[N_CHIPS + src], i, 4 + j, sibling))
            if kind == "first":
                made.append(remote(ins[i].at[chip], outs[i].at[N_CHIPS + chip], i, 3, sibling))
            if kind == "arrivals":
                made += [remote(ins[i].at[0], outs[i].at[0], i, j, me) for j in (3, 4, 5, 6)]
        return made

    def start(ins, outs, sems):
        for cp in copies("local", ins, outs, sems) + copies("first", ins, outs, sems):
            cp.start()

    def middle(ins, outs, sems):
        for got, fwd in zip(copies("landed", ins, outs, sems), copies("passed", ins, outs, sems)):
            got.wait_recv()
            fwd.start()

    def finish(ins, outs, sems):
        for cp in copies("arrivals", ins, outs, sems):
            cp.wait_recv()
        for cp in copies("first", ins, outs, sems) + copies("passed", ins, outs, sems):
            cp.wait_send()
        for cp in copies("local", ins, outs, sems):
            cp.wait()

    return _Comm(list(blocks), [jax.ShapeDtypeStruct((N_DEV,) + b.shape[1:], b.dtype) for b in blocks],
                 [pltpu.SemaphoreType.DMA((n, 7)), pltpu.SemaphoreType.DMA((n, 7)), pltpu.SemaphoreType.DMA((n,))],
                 start, middle, finish)


def _row_spec(tr, w, col=0):
    return pl.BlockSpec((tr, w), lambda i: (i, col))


def _vec_spec(w):
    return pl.BlockSpec((1, w), lambda i: (0, 0))


def _acc_spec(w):
    return pl.BlockSpec((8, w), lambda i: (0, 0))


def _rms(x):
    return lax.rsqrt(jnp.mean(x * x, axis=-1, keepdims=True) + EPS)


def _rms_bwd(dn, n, rstd):
    return rstd * (dn - n * jnp.mean(dn * n, axis=-1, keepdims=True))


def _colsum(x):
    return jnp.sum(x, axis=0, keepdims=True)


def _zero_first(ref):
    @pl.when(pl.program_id(0) == 0)
    def _():
        ref[...] = jnp.zeros_like(ref)


def _norm_mod(xall, w, mod, n_lat_tiles, tr):
    R = xall.shape[0]

    def body(x_ref, w_ref, mod_ref, o_ref):
        x = x_ref[...]
        n = x * _rms(x) * w_ref[...]
        o_ref[...] = (n * (1.0 + mod_ref[0, 1]) + mod_ref[0, 0]).astype(BF16)

    return pl.pallas_call(
        body, name="norm1_mod", grid=(R // tr,),
        in_specs=[_row_spec(tr, D), _vec_spec(D),
                  pl.BlockSpec((1, 2, 1, D), lambda i: (jnp.where(i >= n_lat_tiles, 1, 0), 0, 0, 0))],
        out_specs=_row_spec(tr, D), out_shape=jax.ShapeDtypeStruct((R, D), BF16),
        compiler_params=_cparams(("arbitrary",)),
    )(xall, w, mod)


def _resid_norm_mod(xall, mix, g1, w2, sh2, sc2, tr):
    R = xall.shape[0]

    def body(x_ref, mix_ref, g1_ref, w_ref, sh_ref, sc_ref, h2_ref, u_ref):
        h2 = x_ref[...] + g1_ref[...] * mix_ref[...]
        h2_ref[...] = h2
        n = h2 * _rms(h2) * w_ref[...]
        u_ref[...] = (n * (1.0 + sc_ref[...]) + sh_ref[...]).astype(BF16)

    return pl.pallas_call(
        body, name="resid_norm2_mod", grid=(R // tr,),
        in_specs=[_row_spec(tr, D), _row_spec(tr, D)] + [_vec_spec(D)] * 4,
        out_specs=[_row_spec(tr, D), _row_spec(tr, D)],
        out_shape=[jax.ShapeDtypeStruct((R, D), F32), jax.ShapeDtypeStruct((R, D), BF16)],
        compiler_params=_cparams(("arbitrary",)),
    )(xall, mix, g1, w2, sh2, sc2)


def _loss_head(h2, f, target, g2, fw, n_lat_tiles, tr):
    R = h2.shape[0]

    def body(h2_ref, f_ref, t_ref, g2_ref, fw_ref, dh3_ref, df_ref, acc_ref):
        _zero_first(acc_ref)
        lat = pl.program_id(0) < n_lat_tiles
        fv = f_ref[...]
        h3 = h2_ref[...] + g2_ref[...] * fv
        rstd = _rms(h3)
        n = h3 * rstd
        err = n * fw_ref[...] - t_ref[...]
        dy = err * (1.0 / D)
        dh3 = jnp.where(lat, _rms_bwd(dy * fw_ref[...], n, rstd), 0.0)
        dh3_ref[...] = dh3
        df_ref[...] = (g2_ref[...] * dh3).astype(BF16)
        acc_ref[0:1, :] += jnp.where(lat, _colsum(dy * n), 0.0)
        acc_ref[1:2, :] += _colsum(dh3 * fv)
        acc_ref[2:3, :] += jnp.where(lat, _colsum(err * err) * (0.5 / D), 0.0)

    return pl.pallas_call(
        body, name="loss_head", grid=(R // tr,),
        in_specs=[_row_spec(tr, D), _row_spec(tr, D),
                  pl.BlockSpec((tr, D), lambda i: (jnp.minimum(i, n_lat_tiles - 1), 0)), _vec_spec(D), _vec_spec(D)],
        out_specs=[_row_spec(tr, D), _row_spec(tr, D), _acc_spec(D)],
        out_shape=[jax.ShapeDtypeStruct((R, D), F32), jax.ShapeDtypeStruct((R, D), BF16), jax.ShapeDtypeStruct((8, D), F32)],
        compiler_params=_cparams(("arbitrary",)),
    )(h2, f, target, g2, fw)


def _ffn_in_bwd(du_a, du_b, h2, dh3, mix, sc2, g1, w2, tr):
    R = h2.shape[0]

    def body(dua_ref, dub_ref, h2_ref, dh3_ref, mix_ref, sc_ref, g1_ref, w_ref, dh2_ref, dmix_ref, acc_ref):
        _zero_first(acc_ref)
        du = _f32(dua_ref) + _f32(dub_ref)
        h2 = h2_ref[...]
        rstd = _rms(h2)
        n = h2 * rstd
        dnw = du * (1.0 + sc_ref[...])
        dh2 = dh3_ref[...] + _rms_bwd(dnw * w_ref[...], n, rstd)
        dh2_ref[...] = dh2
        dmix_ref[...] = (g1_ref[...] * dh2).astype(BF16)
        acc_ref[0:1, :] += _colsum(du * n * w_ref[...])
        acc_ref[1:2, :] += _colsum(du)
        acc_ref[2:3, :] += _colsum(dnw * n)
        acc_ref[3:4, :] += _colsum(dh2 * mix_ref[...])

    return pl.pallas_call(
        body, name="ffn_in_bwd", grid=(R // tr,),
        in_specs=[_row_spec(tr, D)] * 5 + [_vec_spec(D)] * 3,
        out_specs=[_row_spec(tr, D), _row_spec(tr, D), _acc_spec(D)],
        out_shape=[jax.ShapeDtypeStruct((R, D), F32), jax.ShapeDtypeStruct((R, D), BF16), jax.ShapeDtypeStruct((8, D), F32)],
        compiler_params=_cparams(("arbitrary",)),
    )(du_a, du_b, h2, dh3, mix, sc2, g1, w2)


def _norm1_bwd(dh1, xall, dh2, w1, mod, n_lat_tiles, tr):
    R = xall.shape[0]

    def body(dh1_ref, x_ref, dh2_ref, w_ref, mod_ref, dx_ref, acc_ref):
        i = pl.program_id(0)

        @pl.when((i == 0) | (i == n_lat_tiles))
        def _():
            acc_ref[...] = jnp.zeros_like(acc_ref)

        dh1 = dh1_ref[...]
        x = x_ref[...]
        rstd = _rms(x)
        n = x * rstd
        dnw = dh1 * (1.0 + mod_ref[0, 1])
        dx_ref[...] = dh2_ref[...] + _rms_bwd(dnw * w_ref[...], n, rstd)
        acc_ref[0, 0:1, :] += _colsum(dh1 * n * w_ref[...])
        acc_ref[0, 1:2, :] += _colsum(dh1)
        acc_ref[0, 2:3, :] += _colsum(dnw * n)

    sel = lambda i: jnp.where(i >= n_lat_tiles, 1, 0)
    return pl.pallas_call(
        body, name="norm1_bwd", grid=(R // tr,),
        in_specs=[_row_spec(tr, D)] * 3 + [_vec_spec(D), pl.BlockSpec((1, 2, 1, D), lambda i: (sel(i), 0, 0, 0))],
        out_specs=[_row_spec(tr, D), pl.BlockSpec((1, 8, D), lambda i: (sel(i), 0, 0))],
        out_shape=[jax.ShapeDtypeStruct((R, D), F32), jax.ShapeDtypeStruct((2, 8, D), F32)],
        compiler_params=_cparams(("arbitrary",)),
    )(dh1, xall, dh2, w1, mod)


def _swiglu_act(gp, up, tr):
    R = gp.shape[0]

    def body(g_ref, u_ref, o_ref):
        o_ref[...] = (_silu(_f32(g_ref)) * _f32(u_ref)).astype(BF16)

    return pl.pallas_call(
        body, name="swiglu_act", grid=(R // tr,), in_specs=[_row_spec(tr, D_FF)] * 2, out_specs=_row_spec(tr, D_FF),
        out_shape=jax.ShapeDtypeStruct((R, D_FF), BF16), compiler_params=_cparams(("arbitrary",)),
    )(gp, up)


def _swiglu_act_bwd(da, gp, up, tr):
    R = gp.shape[0]

    def body(da_ref, g_ref, u_ref, dg_ref, du_ref):
        da, g = _f32(da_ref), _f32(g_ref)
        dg_ref[...] = (da * _f32(u_ref) * _dsilu(g)).astype(BF16)
        du_ref[...] = (da * _silu(g)).astype(BF16)

    return pl.pallas_call(
        body, name="swiglu_act_bwd", grid=(R // tr,), in_specs=[_row_spec(tr, D_FF)] * 3, out_specs=[_row_spec(tr, D_FF)] * 2,
        out_shape=[jax.ShapeDtypeStruct((R, D_FF), BF16)] * 2, compiler_params=_cparams(("arbitrary",)),
    )(da, gp, up)


def _merge(ya, yb, parts, tr):
    R = ya.shape[0]

    def body(ya_ref, yb_ref, ga_ref, gb_ref, o_ref):
        o_ref[...] = (_sigmoid(_f32(ga_ref)) * _f32(ya_ref) + _sigmoid(_f32(gb_ref)) * _f32(yb_ref)).astype(BF16)

    return pl.pallas_call(
        body, name="merge", grid=(R // tr,),
        in_specs=[_row_spec(tr, D), _row_spec(tr, D), _row_spec(tr, D, COL_GA // D), _row_spec(tr, D, COL_GB // D)],
        out_specs=_row_spec(tr, D), out_shape=jax.ShapeDtypeStruct((R, D), BF16), compiler_params=_cparams(("arbitrary",)),
    )(ya, yb, parts, parts)


def _dparts_out(tr, w, col, nd=1):
    blk = col // w
    return pl.BlockSpec((tr, w), (lambda i: (i, blk)) if nd == 1 else (lambda i, j: (i, blk + j)))


def _merge_bwd(dm, ya, yb, parts, dparts, tr):
    R = ya.shape[0]

    def body(dm_ref, ya_ref, yb_ref, ga_ref, gb_ref, _, dya_ref, dyb_ref, dg_ref):
        dm = _f32(dm_ref)
        sa, sb = _sigmoid(_f32(ga_ref)), _sigmoid(_f32(gb_ref))
        dya_ref[...] = (dm * sa).astype(BF16)
        dyb_ref[...] = (dm * sb).astype(BF16)
        dg_ref[:, 0:D] = (dm * _f32(ya_ref) * sa * (1.0 - sa)).astype(BF16)
        dg_ref[:, D:2 * D] = (dm * _f32(yb_ref) * sb * (1.0 - sb)).astype(BF16)

    return pl.pallas_call(
        body, name="merge_bwd", grid=(R // tr,),
        in_specs=[_row_spec(tr, D)] * 3 + [_row_spec(tr, D, COL_GA // D), _row_spec(tr, D, COL_GB // D), _ANY],
        out_specs=[_row_spec(tr, D), _row_spec(tr, D), _dparts_out(tr, 2 * D, COL_GA)],
        out_shape=[jax.ShapeDtypeStruct((R, D), BF16)] * 2 + [jax.ShapeDtypeStruct(dparts.shape, BF16)],
        input_output_aliases={5: 2}, compiler_params=_cparams(("arbitrary",)),
    )(dm, ya, yb, parts, parts, dparts)


def _gla_out(o2, parts, gw4, tr):
    R = parts.shape[0]

    def body(of_ref, ob_ref, r_ref, w_ref, out_ref):
        oa = _f32(of_ref) + _f32(ob_ref)
        sr = _silu(_f32(r_ref))
        for h in range(GLA_H):
            s = slice(h * GLA_DV, (h + 1) * GLA_DV)
            o = oa[:, s]
            out_ref[:, s] = (o * _rms(o) * w_ref[:, s] * sr[:, s]).astype(BF16)

    return pl.pallas_call(
        body, name="gla_out", grid=(R // tr,),
        in_specs=[_row_spec(tr, D), _row_spec(tr, D), _row_spec(tr, D, COL_R // D), _vec_spec(D)],
        out_specs=_row_spec(tr, D), out_shape=jax.ShapeDtypeStruct((R, D), BF16), compiler_params=_cparams(("arbitrary",)),
    )(o2[0], o2[1], parts, gw4)


def _gla_out_bwd(dout, o2, parts, gw4, dparts, tr):
    R = parts.shape[0]

    def body(d_ref, of_ref, ob_ref, r_ref, w_ref, _, do_ref, dr_ref, acc_ref):
        _zero_first(acc_ref)
        oa = _f32(of_ref) + _f32(ob_ref)
        r = _f32(r_ref)
        sr = _silu(r)
        dout = _f32(d_ref)
        for h in range(GLA_H):
            s = slice(h * GLA_DV, (h + 1) * GLA_DV)
            o = oa[:, s]
            rstd = _rms(o)
            n = o * rstd
            w = w_ref[:, s]
            dr_ref[:, s] = (dout[:, s] * n * w * _dsilu(r[:, s])).astype(BF16)
            dnw = dout[:, s] * sr[:, s]
            do_ref[:, s] = _rms_bwd(dnw * w, n, rstd).astype(ACT)
            acc_ref[0:1, s] += _colsum(dnw * n)

    return pl.pallas_call(
        body, name="gla_out_bwd", grid=(R // tr,),
        in_specs=[_row_spec(tr, D), _row_spec(tr, D), _row_spec(tr, D), _row_spec(tr, D, COL_R // D), _vec_spec(D), _ANY],
        out_specs=[_row_spec(tr, D), _dparts_out(tr, D, COL_R), _acc_spec(D)],
        out_shape=[jax.ShapeDtypeStruct((R, D), ACT), jax.ShapeDtypeStruct(dparts.shape, BF16), jax.ShapeDtypeStruct((8, D), F32)],
        input_output_aliases={5: 1}, compiler_params=_cparams(("arbitrary",)),
    )(dout, o2[0], o2[1], parts, gw4, dparts)


SSM_GW = SSM_INNER // SSM_G


def _ssd_out(y2, xbc, parts, dskip, nw, tr):
    R = parts.shape[0]

    def body(yf_ref, yb_ref, x_ref, z_ref, ds_ref, w_ref, out_ref):
        ob = (_f32(yf_ref) + _f32(yb_ref) + ds_ref[...] * _f32(x_ref)) * _silu(_f32(z_ref))
        for g in range(SSM_G):
            s = slice(g * SSM_GW, (g + 1) * SSM_GW)
            o = ob[:, s]
            out_ref[:, s] = (o * _rms(o) * w_ref[:, s]).astype(BF16)

    return pl.pallas_call(
        body, name="ssd_out", grid=(R // tr,),
        in_specs=[_row_spec(tr, SSM_INNER)] * 3 + [_row_spec(tr, SSM_INNER, COL_Z // SSM_INNER),
                                                   _vec_spec(SSM_INNER), _vec_spec(SSM_INNER)],
        out_specs=_row_spec(tr, SSM_INNER), out_shape=jax.ShapeDtypeStruct((R, SSM_INNER), BF16),
        compiler_params=_cparams(("arbitrary",)),
    )(y2[0], y2[1], xbc, parts, dskip, nw)


def _ssd_out_bwd(dout, y2, xbc, parts, dskip, nw, dparts, tr):
    R = parts.shape[0]

    def body(d_ref, yf_ref, yb_ref, x_ref, z_ref, ds_ref, w_ref, _, dy_ref, dz_ref, acc_ref):
        _zero_first(acc_ref)
        x, z = _f32(x_ref), _f32(z_ref)
        pre = _f32(yf_ref) + _f32(yb_ref) + ds_ref[...] * x
        sz = _silu(z)
        ob = pre * sz
        dout = _f32(d_ref)
        for g in range(SSM_G):
            s = slice(g * SSM_GW, (g + 1) * SSM_GW)
            o = ob[:, s]
            rstd = _rms(o)
            n = o * rstd
            dob = _rms_bwd(dout[:, s] * w_ref[:, s], n, rstd)
            dz_ref[:, s] = (dob * pre[:, s] * _dsilu(z[:, s])).astype(BF16)
            dy = dob * sz[:, s]
            dy_ref[:, s] = dy.astype(ACT)
            acc_ref[0:1, s] += _colsum(dout[:, s] * n)
            acc_ref[1:2, s] += _colsum(dy * x[:, s])

    return pl.pallas_call(
        body, name="ssd_out_bwd", grid=(R // tr,),
        in_specs=[_row_spec(tr, SSM_INNER)] * 4 + [_row_spec(tr, SSM_INNER, COL_Z // SSM_INNER),
                                                   _vec_spec(SSM_INNER), _vec_spec(SSM_INNER), _ANY],
        out_specs=[_row_spec(tr, SSM_INNER), _dparts_out(tr, SSM_INNER, COL_Z), _acc_spec(SSM_INNER)],
        out_shape=[jax.ShapeDtypeStruct((R, SSM_INNER), ACT), jax.ShapeDtypeStruct(dparts.shape, BF16),
                   jax.ShapeDtypeStruct((8, SSM_INNER), F32)],
        input_output_aliases={7: 1}, compiler_params=_cparams(("arbitrary",)),
    )(dout, y2[0], y2[1], xbc, parts, dskip, nw, dparts)


CONV_W = SSM_INNER + 2 * SSM_G * SSM_N
CONV_BLK = 1024


def _conv_masks(tr, is_ctx):
    t = lax.broadcasted_iota(jnp.int32, (tr, 1), 0)
    pos = jnp.where(is_ctx, t, t & (GRID_W - 1))
    seg = jnp.where(is_ctx, tr, GRID_W)
    return pos, seg


def _shift_rows(u, s, tr):
    return u if s == 0 else pltpu.roll(u, (-s) % tr, 0)


def _conv_fwd(parts, cw, cb, n_lat_tiles, tr):
    R = parts.shape[0]

    def body(u_ref, w_ref, b_ref, o_ref):
        pos, seg = _conv_masks(tr, pl.program_id(0) >= n_lat_tiles)

        def lanes(l, carry):
            sl = pl.ds(pl.multiple_of(l * 128, 128), 128)
            u, w = u_ref[:, sl].astype(F32), w_ref[:, sl]
            acc = jnp.zeros_like(u) + b_ref[:, sl]
            for j in range(SSM_CONV):
                s = j - CONV_LEFT
                ok = (pos + s >= 0) & (pos + s < seg)
                acc = acc + jnp.where(ok, _shift_rows(u, s, tr), 0.0) * w[j:j + 1, :]
            o_ref[:, sl] = _silu(acc).astype(ACT)
            return carry

        lax.fori_loop(0, CONV_BLK // 128, lanes, 0)

    return pl.pallas_call(
        body, name="conv_fwd", grid=(R // tr, CONV_W // CONV_BLK),
        in_specs=[pl.BlockSpec((tr, CONV_BLK), lambda i, j: (i, COL_XBC // CONV_BLK + j)),
                  pl.BlockSpec((SSM_CONV, CONV_BLK), lambda i, j: (0, j)), pl.BlockSpec((1, CONV_BLK), lambda i, j: (0, j))],
        out_specs=pl.BlockSpec((tr, CONV_BLK), lambda i, j: (i, j)), out_shape=jax.ShapeDtypeStruct((R, CONV_W), ACT),
        compiler_params=_cparams(("arbitrary", "arbitrary")),
    )(parts, cw, cb)


def _conv_bwd(dx, db, dc, dy, dskip, parts, cw, cb, dparts, n_lat_tiles, tr):
    R = parts.shape[0]
    half = CONV_BLK // 2
    n_x = SSM_INNER // CONV_BLK

    def body(dxf_ref, dxb_ref, dy_ref, ds_ref, dbf_ref, dbb_ref, dcf_ref, dcb_ref, u_ref, w_ref, b_ref, _, du_ref, acc_ref, d_scr):
        @pl.when(pl.program_id(1) == 0)
        def _():
            acc_ref[...] = jnp.zeros_like(acc_ref)

        @pl.when(pl.program_id(0) < n_x)
        def _():
            d_scr[...] = dxf_ref[...] + dxb_ref[...] + _f32(dy_ref) * ds_ref[...]

        @pl.when(pl.program_id(0) >= n_x)
        def _():
            d_scr[:, 0:half] = dbf_ref[...] + dbb_ref[...]
            d_scr[:, half:] = dcf_ref[...] + dcb_ref[...]

        pos, seg = _conv_masks(tr, pl.program_id(1) >= n_lat_tiles)

        def lanes(l, carry):
            sl = pl.ds(pl.multiple_of(l * 128, 128), 128)
            u, w = u_ref[:, sl].astype(F32), w_ref[:, sl]
            pre = jnp.zeros_like(u) + b_ref[:, sl]
            taps = []
            for j in range(SSM_CONV):
                s = j - CONV_LEFT
                ok = (pos + s >= 0) & (pos + s < seg)
                tap = jnp.where(ok, _shift_rows(u, s, tr), 0.0)
                taps.append(tap)
                pre = pre + tap * w[j:j + 1, :]
            dpre = d_scr[:, sl] * _dsilu(pre)
            du = jnp.zeros_like(u)
            sums = []
            for j in range(SSM_CONV):
                s = j - CONV_LEFT
                sums.append(_colsum(dpre * taps[j]))
                ok = (pos - s >= 0) & (pos - s < seg)
                du = du + jnp.where(ok, _shift_rows(dpre, -s, tr), 0.0) * w[j:j + 1, :]
            sums += [_colsum(dpre), jnp.zeros((8 - SSM_CONV - 1, 128), F32)]
            acc_ref[:, sl] += jnp.concatenate(sums, axis=0)
            du_ref[:, sl] = du.astype(BF16)
            return carry

        lax.fori_loop(0, CONV_BLK // 128, lanes, 0)

    return pl.pallas_call(
        body, name="conv_bwd", grid=(CONV_W // CONV_BLK, R // tr),
        in_specs=[pl.BlockSpec((tr, CONV_BLK), lambda j, i: (jnp.where(j < n_x, i, 0), jnp.minimum(j, n_x - 1)))] * 3
        + [pl.BlockSpec((1, CONV_BLK), lambda j, i: (0, jnp.minimum(j, n_x - 1)))]
        + [pl.BlockSpec((tr, half), lambda j, i: (jnp.where(j < n_x, 0, i), 0))] * 4
        + [pl.BlockSpec((tr, CONV_BLK), lambda j, i: (i, COL_XBC // CONV_BLK + j)),
           pl.BlockSpec((SSM_CONV, CONV_BLK), lambda j, i: (0, j)), pl.BlockSpec((1, CONV_BLK), lambda j, i: (0, j)), _ANY],
        out_specs=[pl.BlockSpec((tr, CONV_BLK), lambda j, i: (i, COL_XBC // CONV_BLK + j)),
                   pl.BlockSpec((8, CONV_BLK), lambda j, i: (0, j))],
        out_shape=[jax.ShapeDtypeStruct(dparts.shape, BF16), jax.ShapeDtypeStruct((8, CONV_W), F32)],
        scratch_shapes=[pltpu.VMEM((tr, CONV_BLK), F32)],
        input_output_aliases={11: 0}, compiler_params=_cparams(("arbitrary", "arbitrary")),
    )(*dx, dy, dskip, *db, *dc, parts, cw, cb, dparts)


def _chunk_row_block(d, i, n_lat, n_ctx):
    fwd = jnp.where(i < n_ctx, n_lat + i, i - n_ctx)
    rev = n_lat + n_ctx - 1 - i
    if isinstance(d, int):
        return rev if d else fwd
    return jnp.where(d == 0, fwd, rev)


def _tri(n, d, transpose=False):
    row = lax.broadcasted_iota(jnp.int32, (n, n), 0)
    col = lax.broadcasted_iota(jnp.int32, (n, n), 1)
    diff = (col - row) if transpose else (row - col)
    return diff * (1 - 2 * d) >= 0


def _gla_gates(sm, uhi, ulo, bias, d):
    pre = _nn3(sm, uhi, ulo) + bias
    g = _log_sigmoid(pre) * (1.0 / GLA_TAU)
    mask = _tri(GLA_C, d)
    b = _x_nn(mask.astype(BF16), g)
    b_tot = _colsum(g)
    b_ref = b[GLA_C // 2:GLA_C // 2 + 1, :]
    e_q = jnp.exp(jnp.minimum(b - b_ref, EXP_CLAMP))
    e_k = jnp.exp(jnp.minimum(b_ref - b, EXP_CLAMP))
    return pre, mask, b_tot, e_q, e_k, jnp.exp(b), jnp.exp(b_tot - b)


GLA_QK = GLA_H * GLA_DK
GLA_V = GLA_H * GLA_DV


def _gla_specs(n_lat, n_ctx, step_of):
    rbs = [lambda i, d=d: _chunk_row_block(d, step_of(i), n_lat, n_ctx) for d in range(2)]
    specs = []
    for rb in rbs:
        specs += [pl.BlockSpec((GLA_C, GLA_QK), lambda i, rb=rb: (rb(i), COL_Q // GLA_QK)),
                  pl.BlockSpec((GLA_C, GLA_QK), lambda i, rb=rb: (rb(i), COL_K // GLA_QK)),
                  pl.BlockSpec((GLA_C, GLA_V), lambda i, rb=rb: (rb(i), COL_V // GLA_V)),
                  pl.BlockSpec((GLA_C, 128), lambda i, rb=rb: (rb(i), 0))]
    specs += [pl.BlockSpec((2, 128, GLA_QK), lambda i: (0, 0, 0)), pl.BlockSpec((2, 128, GLA_QK), lambda i: (0, 0, 0)),
              pl.BlockSpec((2, 1, GLA_QK), lambda i: (0, 0, 0))]
    return specs, rbs


ACT = BF16


def _f32(ref_or_value):
    return ref_or_value[...].astype(F32)


def _gla_fwd(parts, sm, uhi, ulo, bias, n_lat, n_ctx):
    R = parts.shape[0]
    n_steps = n_lat + n_ctx
    scale = GLA_DK ** -0.5

    def body(*refs):
        ins, (uhi_ref, ulo_ref, bias_ref), o_refs, hist_ref, st = refs[:8], refs[8:11], refs[11:13], refs[13], refs[14]

        @pl.when(pl.program_id(0) == 0)
        def _():
            st[...] = jnp.zeros_like(st)

        for d in range(2):
            q_ref, k_ref, v_ref, sm_ref = ins[4 * d:4 * d + 4]
            _, mask, b_tot, e_q, e_k, e_in, e_out = _gla_gates(sm_ref[...], uhi_ref[d], ulo_ref[d], bias_ref[d], d)
            q, k, v = _f32(q_ref) * scale, _f32(k_ref), _bf(v_ref[...])
            qb, kb, q_in, k_out, decay = _bf(q * e_q), _bf(k * e_k), _bf(q * e_in), _bf(k * e_out), jnp.exp(b_tot)
            for h in range(GLA_H):
                sk, sv = slice(h * GLA_DK, (h + 1) * GLA_DK), slice(h * GLA_DV, (h + 1) * GLA_DV)
                att = jnp.where(mask, _nt(qb[:, sk], kb[:, sk]), 0.0)
                s_in = st[d, h]
                hist_ref[d, 0, h] = s_in
                o_refs[d][:, sv] = (_nn(_bf(att), v[:, sv]) + _nt(q_in[:, sk], _bf(s_in))).astype(ACT)
                st[d, h] = decay[:, sk] * s_in + _tn(v[:, sv], k_out[:, sk])

    in_specs, rbs = _gla_specs(n_lat, n_ctx, lambda i: i)
    return pl.pallas_call(
        body, name="gla_fwd", grid=(n_steps,), in_specs=in_specs,
        out_specs=[pl.BlockSpec((GLA_C, GLA_V), lambda i, rb=rb: (rb(i), 0)) for rb in rbs]
        + [pl.BlockSpec((2, 1, GLA_H, GLA_DV, GLA_DK), lambda i: (0, i, 0, 0, 0))],
        out_shape=[jax.ShapeDtypeStruct((R, GLA_V), ACT)] * 2 + [jax.ShapeDtypeStruct((2, n_steps, GLA_H, GLA_DV, GLA_DK), F32)],
        scratch_shapes=[pltpu.VMEM((2, GLA_H, GLA_DV, GLA_DK), F32)],
        compiler_params=_cparams(("arbitrary",)),
    )(*([parts, parts, parts, sm] * 2), uhi, ulo, bias)


def _gla_bwd(do, parts, sm, uhi, ulo, bias, hist, n_lat, n_ctx):
    R = parts.shape[0]
    n_steps = n_lat + n_ctx
    scale = GLA_DK ** -0.5
    step_of = lambda j: n_steps - 1 - j

    def body(*refs):
        ins, (uhi_ref, ulo_ref, bias_ref), do_refs, hist_ref = refs[:8], refs[8:11], refs[11:13], refs[13]
        outs, dst = refs[14:22], refs[22]

        @pl.when(pl.program_id(0) == 0)
        def _():
            dst[...] = jnp.zeros_like(dst)

        for d in range(2):
            q_ref, k_ref, v_ref, sm_ref = ins[4 * d:4 * d + 4]
            dq_ref, dk_ref, dv_ref, dp_ref = outs[4 * d:4 * d + 4]
            pre, mask, b_tot, e_q, e_k, e_in, e_out = _gla_gates(sm_ref[...], uhi_ref[d], ulo_ref[d], bias_ref[d], d)
            q, k, v = _f32(q_ref) * scale, _f32(k_ref), _bf(v_ref[...])
            dout = _bf(do_refs[d][...])
            k_out_f = k * e_out
            qb, kb, q_in, k_out, decay = _bf(q * e_q), _bf(k * e_k), _bf(q * e_in), _bf(k_out_f), jnp.exp(b_tot)
            dqs, dks, dk_outs, dss = [], [], [], []
            for h in range(GLA_H):
                sk, sv = slice(h * GLA_DK, (h + 1) * GLA_DK), slice(h * GLA_DV, (h + 1) * GLA_DV)
                s_in, ds = hist_ref[d, 0, h], dst[d, h]
                att = jnp.where(mask, _nt(qb[:, sk], kb[:, sk]), 0.0)
                datt = _bf(jnp.where(mask, _nt(dout[:, sv], v[:, sv]), 0.0))
                dv_ref[:, sv] = _tn(_bf(att), dout[:, sv]) + _nt(k_out[:, sk], _bf(ds))
                dqs.append(_nn(datt, kb[:, sk]) * e_q[:, sk] + _nn(dout[:, sv], _bf(s_in)) * e_in[:, sk])
                dk_o = _nn(v[:, sv], _bf(ds))
                dk_outs.append(dk_o)
                dks.append(_tn(datt, qb[:, sk]) * e_k[:, sk])
                dss.append(_colsum(ds * s_in))
                dst[d, h] = decay[:, sk] * ds + _tn(dout[:, sv], q_in[:, sk])
            dq, dk_out = jnp.concatenate(dqs, axis=1), jnp.concatenate(dk_outs, axis=1)
            dk = jnp.concatenate(dks, axis=1) + dk_out * e_out
            dq_ref[...] = dq * scale
            dk_ref[...] = dk
            db_tot = _colsum(dk_out * k_out_f) + decay * jnp.concatenate(dss, axis=1)
            dg = _x_nn(_tri(GLA_C, d, transpose=True).astype(BF16), dq * q - dk * k) + db_tot
            dp_ref[...] = dg * (1.0 / GLA_TAU) * _sigmoid(-pre)

    in_specs, rbs = _gla_specs(n_lat, n_ctx, step_of)
    in_specs += [pl.BlockSpec((GLA_C, GLA_V), lambda j, rb=rb: (rb(j), 0)) for rb in rbs]
    in_specs += [pl.BlockSpec((2, 1, GLA_H, GLA_DV, GLA_DK), lambda j: (0, step_of(j), 0, 0, 0))]
    out_specs, out_shape = [], []
    for rb in rbs:
        for w in (GLA_QK, GLA_QK, GLA_V, GLA_QK):
            out_specs.append(pl.BlockSpec((GLA_C, w), lambda j, rb=rb: (rb(j), 0)))
            out_shape.append(jax.ShapeDtypeStruct((R, w), F32))
    outs = pl.pallas_call(
        body, name="gla_bwd", grid=(n_steps,), in_specs=in_specs, out_specs=out_specs, out_shape=out_shape,
        scratch_shapes=[pltpu.VMEM((2, GLA_H, GLA_DV, GLA_DK), F32)],
        compiler_params=_cparams(("arbitrary",)),
    )(*([parts, parts, parts, sm] * 2), uhi, ulo, bias, do, do, hist)
    return [(outs[k], outs[4 + k]) for k in range(4)]


def _ssd_consts(dt_bias, a_log):
    sel = np.zeros((2, SSM_G, 128, 128), np.float32)
    for d, base in enumerate((SM_DTF, SM_DTB)):
        for g in range(SSM_G):
            for e in range(SSM_HPG):
                sel[d, g, base + SSM_HPG * g + e, e] = 1.0
    e512 = np.zeros((128, SSM_GW), np.float32)
    for e in range(SSM_HPG):
        e512[e, SSM_P * e:SSM_P * (e + 1)] = 1.0
    a_neg = -jnp.exp(a_log)
    pad = lambda v: jnp.pad(v.reshape(2, SSM_G, 1, SSM_HPG), ((0, 0), (0, 0), (0, 0), (0, 128 - SSM_HPG)))
    return dict(
        sel=jnp.asarray(sel, BF16), sel_t=jnp.asarray(sel.transpose(0, 1, 3, 2), BF16), e512_t=jnp.asarray(e512.T, BF16),
        dtb=pad(dt_bias), a=pad(a_neg), a512=jnp.repeat(a_neg, SSM_P, axis=1).reshape(2, SSM_G, 1, SSM_GW))


def _head_columns(x8):
    return [jnp.broadcast_to(x8[:, e:e + 1], (x8.shape[0], 128)) for e in range(SSM_HPG)]


def _head_layout(cols):
    low = lax.broadcasted_iota(jnp.int32, (1, 128), 1) < SSM_P
    return jnp.concatenate([jnp.where(low, cols[2 * j], cols[2 * j + 1]) for j in range(SSM_HPG // 2)], axis=1)


def _ssd_common(sm, sel, dtb, a_neg, a512, d):
    dtr8 = _nn_x(sm, sel) + dtb
    dt8 = _softplus(dtr8)
    a8 = a_neg * dt8
    mask = _tri(SSM_C, d)
    mask_t = _tri(SSM_C, d, transpose=True).astype(BF16)
    cum8 = _x_nn(mask.astype(BF16), a8)
    a_hi, a_lo = _split(a8)
    cum_t = _tn(a_hi, mask_t) + _tn(a_lo, mask_t)
    cum_cols = _head_columns(cum8)
    dt_exp = _head_layout(_head_columns(dt8))
    a_exp = a512 * dt_exp
    return dict(dtr8=dtr8, a8=a8, mask=mask, mask_t=mask_t, cum_t=cum_t, dt_exp=dt_exp, a_exp=a_exp,
                cum_exp=_head_layout(cum_cols), cum_cols=cum_cols, tot_exp=_colsum(a_exp))


def _ssd_decay(cm, e):
    diff = cm["cum_cols"][e] - cm["cum_t"][e:e + 1, :]
    return jnp.where(cm["mask"], jnp.exp(jnp.minimum(diff, 0.0)), 0.0)


SSM_GPS = 2


def _ssd_specs(n_lat, n_ctx, step_of):
    rbs = [lambda i, d=d: _chunk_row_block(d, step_of(i), n_lat, n_ctx) for d in range(2)]
    xw, nw = SSM_GPS * SSM_GW, SSM_GPS * SSM_N
    specs = []
    for rb in rbs:
        specs += [pl.BlockSpec((SSM_C, xw), lambda g, i, rb=rb: (rb(i), g)),
                  pl.BlockSpec((SSM_C, nw), lambda g, i, rb=rb: (rb(i), SSM_INNER // nw + g)),
                  pl.BlockSpec((SSM_C, nw), lambda g, i, rb=rb: (rb(i), (SSM_INNER + SSM_G * SSM_N) // nw + g)),
                  pl.BlockSpec((SSM_C, 128), lambda g, i, rb=rb: (rb(i), 0))]
    specs += [pl.BlockSpec((2, SSM_GPS, 128, 128), lambda g, i: (0, g, 0, 0)),
              pl.BlockSpec((2, SSM_GPS, 1, 128), lambda g, i: (0, g, 0, 0)),
              pl.BlockSpec((2, SSM_GPS, 1, 128), lambda g, i: (0, g, 0, 0)),
              pl.BlockSpec((2, SSM_GPS, 1, SSM_GW), lambda g, i: (0, g, 0, 0))]
    return specs, rbs


def _ssd_fwd(xbc, sm, k, n_lat, n_ctx, comm):
    R = xbc.shape[0]
    n_steps = n_lat + n_ctx

    def body(*refs):
        ins, (sel_ref, dtb_ref, a_ref, a512_ref), y_refs, hist_ref, st = refs[:8], refs[8:12], refs[12:14], refs[14], refs[15]

        @pl.when(pl.program_id(1) == 0)
        def _():
            st[...] = jnp.zeros_like(st)

        for d in range(2):
            x_ref, b_ref, c_ref, sm_ref = ins[4 * d:4 * d + 4]
            sm = sm_ref[...]
            for gg in range(SSM_GPS):
                sx, sn = slice(gg * SSM_GW, (gg + 1) * SSM_GW), slice(gg * SSM_N, (gg + 1) * SSM_N)
                cm = _ssd_common(sm, sel_ref[d, gg], dtb_ref[d, gg], a_ref[d, gg], a512_ref[d, gg], d)
                bm, cmat = _bf(b_ref[:, sn]), _bf(c_ref[:, sn])
                xdt = x_ref[:, sx].astype(F32) * cm["dt_exp"]
                cb = _nt(cmat, bm)
                ys = [_nn(_bf(cb * _ssd_decay(cm, e)), _bf(xdt[:, SSM_P * e:SSM_P * (e + 1)])) for e in range(SSM_HPG)]
                s_in = st[d, gg]
                hist_ref[d, 0, gg] = s_in
                y = jnp.concatenate(ys, axis=1) + jnp.exp(cm["cum_exp"]) * _nn(cmat, _bf(s_in))
                y_refs[d][:, sx] = y.astype(ACT)
                st[d, gg] = jnp.exp(cm["tot_exp"]) * s_in + _tn(bm, _bf(xdt * jnp.exp(cm["tot_exp"] - cm["cum_exp"])))

    in_specs, rbs = _ssd_specs(n_lat, n_ctx, lambda i: i)
    out_specs = [pl.BlockSpec((SSM_C, SSM_GPS * SSM_GW), lambda g, i, rb=rb: (rb(i), g)) for rb in rbs]
    out_specs += [pl.BlockSpec((2, 1, SSM_GPS, SSM_N, SSM_GW), lambda g, i: (0, i, g, 0, 0))]
    out_shape = [jax.ShapeDtypeStruct((R, SSM_INNER), ACT)] * 2 + [jax.ShapeDtypeStruct((2, n_steps, SSM_G, SSM_N, SSM_GW), F32)]
    args = [xbc, xbc, xbc, sm] * 2 + [k["sel"], k["dtb"], k["a"], k["a512"]]
    n_host_out = len(out_shape)
    outs = pl.pallas_call(
        _hosted(body, len(args), n_host_out, 1, comm, *_ssd_comm_steps(n_steps)), name="ssd_fwd",
        grid=(SSM_G // SSM_GPS, n_steps), in_specs=in_specs + [_ANY] * len(comm.arrays),
        out_specs=out_specs + [_ANY] * len(comm.out_shape), out_shape=out_shape + comm.out_shape,
        scratch_shapes=[pltpu.VMEM((2, SSM_GPS, SSM_N, SSM_GW), F32)] + comm.scratch,
        compiler_params=_cparams(("arbitrary", "arbitrary")),
    )(*args, *comm.arrays)
    return outs[:n_host_out], outs[n_host_out:]


def _ssd_comm_steps(n_steps):
    n_g = SSM_G // SSM_GPS
    at = lambda g, i: (pl.program_id(0) == g) & (pl.program_id(1) == i)
    return (lambda: at(0, 0)), (lambda: at(n_g // 2, 0)), (lambda: at(n_g - 1, n_steps - 1))


def _ssd_bwd(dy, xbc, sm, k, hist, n_lat, n_ctx, comm):
    R = xbc.shape[0]
    n_steps = n_lat + n_ctx
    step_of = lambda j: n_steps - 1 - j

    def one(d, gg, x_ref, b_ref, c_ref, sm_ref, sel_ref, dtb_ref, a_ref, a512_ref, selt_ref, e512t_ref, dy_ref,
            hist_ref, dx_ref, db_ref, dc_ref, dsm_ref, acc_ref, dst):
        sx, sn = slice(gg * SSM_GW, (gg + 1) * SSM_GW), slice(gg * SSM_N, (gg + 1) * SSM_N)
        a_neg, e512_t = a_ref[d, gg], e512t_ref[...]
        cm = _ssd_common(sm_ref[...], sel_ref[d, gg], dtb_ref[d, gg], a_neg, a512_ref[d, gg], d)
        x, dyv = x_ref[:, sx].astype(F32), dy_ref[:, sx].astype(F32)
        bm, cmat = _bf(b_ref[:, sn]), _bf(c_ref[:, sn])
        xdt = x * cm["dt_exp"]
        cb = _nt(cmat, bm)
        s_in, ds = hist_ref[d, 0, gg], dst[d, gg]
        w = jnp.exp(cm["tot_exp"] - cm["cum_exp"])
        z = _nn(bm, _bf(ds))
        decay_in = jnp.exp(cm["cum_exp"])
        gy = _bf(dyv * decay_in)
        dcb = jnp.zeros((SSM_C, SSM_C), F32)
        dxs, crossing = [], []
        row = lax.broadcasted_iota(jnp.int32, (SSM_C, SSM_C), 0)
        col = lax.broadcasted_iota(jnp.int32, (SSM_C, SSM_C), 1)
        eye = (row == col).astype(BF16)
        before = (cm["mask_t"] - eye)
        for e in range(SSM_HPG):
            s = slice(SSM_P * e, SSM_P * (e + 1))
            lm = _ssd_decay(cm, e)
            dy_e = _bf(dyv[:, s])
            m_e = cb * lm
            dm_e = _nt(dy_e, _bf(xdt[:, s]))
            dcb = dcb + dm_e * lm
            dxs.append(_tn(_bf(m_e), dy_e))
            through = jnp.where(cm["mask"], _nn(_bf(dm_e * m_e), before), 0.0)
            crossing.append(_colsum(through))
        da_rows = jnp.concatenate(crossing + [jnp.zeros((128 - SSM_HPG, SSM_C), F32)], axis=0)
        r_hi, r_lo = _split(da_rows)
        da8_intra = _tn(r_hi, eye) + _tn(r_lo, eye)
        dx_state = w * z
        dxdt = jnp.concatenate(dxs, axis=1) + dx_state
        dcb = _bf(dcb)
        c_s = _nn(cmat, _bf(s_in))
        dc_ref[:, sn] = _nn(dcb, bm) + _nt(gy, _bf(s_in))
        db_ref[:, sn] = _tn(dcb, cmat) + _nt(_bf(w * xdt), _bf(ds))
        dst[d, gg] = jnp.exp(cm["tot_exp"]) * ds + _tn(cmat, gy)
        state_path = xdt * dx_state
        per_token = _nn_x(jnp.concatenate([dyv * decay_in * c_s - state_path, dxdt * x], axis=0), e512_t)
        totals = jnp.concatenate([_colsum(state_path), _colsum(ds * s_in), jnp.zeros((6, SSM_GW), F32)], axis=0)
        totals = _nn_x(totals, e512_t)
        tot8 = _colsum(cm["a8"])
        dtot8 = totals[0:1] + jnp.exp(tot8) * totals[1:2]
        da8 = da8_intra + _x_nn(cm["mask_t"], per_token[:SSM_C]) + dtot8
        ddt8 = da8 * a_neg + per_token[SSM_C:]
        dsm_ref[gg] = _nn_x(ddt8 * _sigmoid(cm["dtr8"]), selt_ref[d, gg])
        dx_ref[:, sx] = dxdt * cm["dt_exp"]
        acc_ref[d, gg, 0:1, :] += _colsum(da8 * cm["a8"])

    def body(*refs):
        ins, consts, (selt_ref, e512t_ref), dy_refs, hist_ref = refs[:8], refs[8:12], refs[12:14], refs[14:16], refs[16]
        outs, acc_ref, dst = refs[17:25], refs[25], refs[26]

        @pl.when(pl.program_id(1) == 0)
        def _():
            dst[...] = jnp.zeros_like(dst)
            acc_ref[...] = jnp.zeros_like(acc_ref)

        for d in range(2):
            for gg in range(SSM_GPS):
                one(d, gg, *ins[4 * d:4 * d + 4], *consts, selt_ref, e512t_ref, dy_refs[d], hist_ref,
                    *outs[4 * d:4 * d + 4], acc_ref, dst)

    xw, nw = SSM_GPS * SSM_GW, SSM_GPS * SSM_N
    in_specs, rbs = _ssd_specs(n_lat, n_ctx, step_of)
    in_specs += [pl.BlockSpec((2, SSM_GPS, 128, 128), lambda g, j: (0, g, 0, 0)), pl.BlockSpec((SSM_GW, 128), lambda g, j: (0, 0))]
    in_specs += [pl.BlockSpec((SSM_C, xw), lambda g, j, rb=rb: (rb(j), g)) for rb in rbs]
    in_specs += [pl.BlockSpec((2, 1, SSM_GPS, SSM_N, SSM_GW), lambda g, j: (0, step_of(j), g, 0, 0))]
    out_specs, out_shape = [], []
    for rb in rbs:
        out_specs += [pl.BlockSpec((SSM_C, xw), lambda g, j, rb=rb: (rb(j), g)),
                      pl.BlockSpec((SSM_C, nw), lambda g, j, rb=rb: (rb(j), g)),
                      pl.BlockSpec((SSM_C, nw), lambda g, j, rb=rb: (rb(j), g)),
                      pl.BlockSpec((SSM_GPS, SSM_C, 128), lambda g, j, rb=rb: (g, rb(j), 0))]
        out_shape += [jax.ShapeDtypeStruct((R, SSM_INNER), F32), jax.ShapeDtypeStruct((R, SSM_G * SSM_N), F32),
                      jax.ShapeDtypeStruct((R, SSM_G * SSM_N), F32), jax.ShapeDtypeStruct((SSM_G, R, 128), F32)]
    out_specs.append(pl.BlockSpec((2, SSM_GPS, 8, 128), lambda g, j: (0, g, 0, 0)))
    out_shape.append(jax.ShapeDtypeStruct((2, SSM_G, 8, 128), F32))
    args = [xbc, xbc, xbc, sm] * 2 + [k["sel"], k["dtb"], k["a"], k["a512"], k["sel_t"], k["e512_t"], dy, dy, hist]
    n_host_out = len(out_shape)
    outs = pl.pallas_call(
        _hosted(body, len(args), n_host_out, 1, comm, *_ssd_comm_steps(n_steps)), name="ssd_bwd",
        grid=(SSM_G // SSM_GPS, n_steps), in_specs=in_specs + [_ANY] * len(comm.arrays),
        out_specs=out_specs + [_ANY] * len(comm.out_shape), out_shape=out_shape + comm.out_shape,
        scratch_shapes=[pltpu.VMEM((2, SSM_GPS, SSM_N, SSM_GW), F32)] + comm.scratch,
        compiler_params=_cparams(("arbitrary", "arbitrary")),
    )(*args, *comm.arrays)
    return [(outs[n], outs[4 + n]) for n in range(4)] + [outs[8]], outs[n_host_out:]


def _gla_assemble(dq, dk, dv, dparts, tr):
    R = dq[0].shape[0]
    qk = GLA_H * GLA_DK

    def body(dqf_ref, dqb_ref, dkf_ref, dkb_ref, dvf_ref, dvb_ref, _, o_ref):
        o_ref[:, 0:qk] = (dqf_ref[...] + dqb_ref[...]).astype(BF16)
        o_ref[:, qk:2 * qk] = (dkf_ref[...] + dkb_ref[...]).astype(BF16)
        o_ref[:, 2 * qk:] = (dvf_ref[...] + dvb_ref[...]).astype(BF16)

    return pl.pallas_call(
        body, name="gla_assemble", grid=(R // tr,), in_specs=[_row_spec(tr, qk)] * 4 + [_row_spec(tr, D)] * 2 + [_ANY],
        out_specs=_dparts_out(tr, 2 * D, COL_Q), out_shape=jax.ShapeDtypeStruct(dparts.shape, BF16),
        input_output_aliases={6: 0}, compiler_params=_cparams(("arbitrary",)),
    )(*dq, *dk, *dv, dparts)


def _small_assemble(dp, dsm, sm, ut_hi, ut_lo, dparts, tr):
    R = sm.shape[0]
    qk = GLA_H * GLA_DK

    def body(dpf_ref, dpb_ref, dsmf_ref, dsmb_ref, sm_ref, uth_ref, utl_ref, _, o_ref, dup_ref, acc_ref, acc2_ref):
        @pl.when(pl.program_id(0) == 0)
        def _():
            dup_ref[...] = jnp.zeros_like(dup_ref)
            acc_ref[...] = jnp.zeros_like(acc_ref)
            acc2_ref[...] = jnp.zeros_like(acc2_ref)

        ssd = dsmf_ref[0] + dsmb_ref[0]
        for g in range(1, SSM_G):
            ssd = ssd + (dsmf_ref[g] + dsmb_ref[g])
        acc2_ref[0:1, :] += _colsum(ssd)
        sm_hi, sm_lo = _split(sm_ref[...])
        out = ssd
        for d, dp_ref in enumerate((dpf_ref, dpb_ref)):
            dpd = dp_ref[...]
            out = out + _nn3(dpd, uth_ref[d], utl_ref[d])
            p_hi, p_lo = _split(dpd)
            dup_ref[d] += _tn(sm_hi, p_hi) + _tn(sm_lo, p_hi) + _tn(sm_hi, p_lo)
            acc_ref[d:d + 1, :] += _colsum(dpd)
        o_ref[...] = out.astype(BF16)

    return pl.pallas_call(
        body, name="small_assemble", grid=(R // tr,),
        in_specs=[_row_spec(tr, qk)] * 2 + [pl.BlockSpec((SSM_G, tr, 128), lambda i: (0, i, 0))] * 2
        + [_row_spec(tr, 128), pl.BlockSpec((2, qk, 128), lambda i: (0, 0, 0)),
           pl.BlockSpec((2, qk, 128), lambda i: (0, 0, 0)), _ANY],
        out_specs=[_dparts_out(tr, 128, COL_SM), pl.BlockSpec((2, 128, qk), lambda i: (0, 0, 0)), _acc_spec(qk), _acc_spec(128)],
        out_shape=[jax.ShapeDtypeStruct(dparts.shape, BF16), jax.ShapeDtypeStruct((2, 128, qk), F32),
                   jax.ShapeDtypeStruct((8, qk), F32), jax.ShapeDtypeStruct((8, 128), F32)],
        input_output_aliases={7: 0}, compiler_params=_cparams(("arbitrary",)),
    )(*dp, *dsm, sm, ut_hi, ut_lo, dparts)


ADA_ROWS = 16
ADA_TILE = 512


def _dot3_f32(a, b, ca, cb):
    a_hi, a_lo = _split(a)
    b_hi, b_lo = _split(b)
    return _dg(a_hi, b_hi, ca, cb) + _dg(a_lo, b_hi, ca, cb) + _dg(a_hi, b_lo, ca, cb)


def _ada_fwd(cvec, w, b):
    cols = w.shape[1]

    def body(c_ref, w_ref, b_ref, o_ref):
        o_ref[...] = _dot3_f32(_silu(c_ref[...]), w_ref[...], 1, 0) + b_ref[...]

    return pl.pallas_call(
        body, name="ada_fwd", grid=(cols // ADA_TILE,),
        in_specs=[pl.BlockSpec((ADA_ROWS, D), lambda j: (0, 0)), pl.BlockSpec((D, ADA_TILE), lambda j: (0, j)),
                  pl.BlockSpec((1, ADA_TILE), lambda j: (0, j))],
        out_specs=pl.BlockSpec((ADA_ROWS, ADA_TILE), lambda j: (0, j)), out_shape=jax.ShapeDtypeStruct((ADA_ROWS, cols), F32),
        compiler_params=_cparams(("arbitrary",)),
    )(cvec, w, b)


def _adam(w, g, m, v):
    m2 = ADAM_B1 * m + (1.0 - ADAM_B1) * g
    v2 = ADAM_B2 * v + (1.0 - ADAM_B2) * (g * g)
    m_hat = m2 / (1.0 - ADAM_B1 ** ADAM_STEP)
    v_hat = v2 / (1.0 - ADAM_B2 ** ADAM_STEP)
    return -ADAM_LR * (m_hat / (jnp.sqrt(v_hat) + ADAM_EPS) + ADAM_WD * w), m2, v2


def _wada_bwd_adam(cvec, dada, w, m, v):
    rows, cols = w.shape
    tr = _tile(rows, 256, 128)

    def body(c_ref, d_ref, w_ref, m_ref, v_ref, g_ref, dl_ref, m2_ref, v2_ref, p_ref):
        wv = w_ref[...]
        g = _dot3_f32(_silu(c_ref[...]), d_ref[...], 0, 0)
        g_ref[...] = g
        dl_ref[...], m2_ref[...], v2_ref[...] = _adam(wv, g, m_ref[...], v_ref[...])
        p_ref[...] = _dot3_f32(d_ref[...], wv, 1, 1)

    blk = pl.BlockSpec((tr, cols), lambda i: (i, 0))
    return pl.pallas_call(
        body, name="wada_bwd_adam", grid=(rows // tr,),
        in_specs=[pl.BlockSpec((ADA_ROWS, tr), lambda i: (0, i)), pl.BlockSpec((ADA_ROWS, cols), lambda i: (0, 0)), blk, blk, blk],
        out_specs=[blk, blk, blk, blk, pl.BlockSpec((ADA_ROWS, tr), lambda i: (0, i))],
        out_shape=[jax.ShapeDtypeStruct((rows, cols), F32)] * 4 + [jax.ShapeDtypeStruct((ADA_ROWS, rows), F32)],
        compiler_params=_cparams(("arbitrary",)),
    )(cvec, dada, w, m, v)


def _reduce_adam(parts8, w, m, v, name):
    rows, cols = w.shape
    tr = _tile(rows, 64, 16)

    def body(p_ref, w_ref, m_ref, v_ref, g_ref, dl_ref, m2_ref, v2_ref):
        g = p_ref[0].astype(F32) + p_ref[N_CHIPS].astype(F32)
        for j in range(1, N_CHIPS):
            g = g + (p_ref[j].astype(F32) + p_ref[N_CHIPS + j].astype(F32))
        g_ref[...] = g
        dl_ref[...], m2_ref[...], v2_ref[...] = _adam(w_ref[...], g, m_ref[...], v_ref[...])

    blk = pl.BlockSpec((tr, cols), lambda i: (i, 0))
    return pl.pallas_call(
        body, name=name, grid=(rows // tr,), in_specs=[pl.BlockSpec((N_DEV, tr, cols), lambda i: (0, i, 0)), blk, blk, blk],
        out_specs=[blk] * 4, out_shape=[jax.ShapeDtypeStruct((rows, cols), F32)] * 4, compiler_params=_cparams(("arbitrary",)),
    )(parts8, w, m, v)


SMALL_W = 1024


def _sum8(g8):
    rows = g8.shape[1]

    def body(g_ref, o_ref):
        s = g_ref[0]
        for j in range(1, N_DEV):
            s = s + g_ref[j]
        o_ref[...] = s

    return pl.pallas_call(
        body, name="sum8", out_shape=jax.ShapeDtypeStruct((rows, SMALL_W), F32),
        in_specs=[pl.BlockSpec(memory_space=pltpu.VMEM)], out_specs=pl.BlockSpec(memory_space=pltpu.VMEM),
        compiler_params=pltpu.CompilerParams(vmem_limit_bytes=VMEM_LIMIT),
    )(g8)


def _cctx_grad(p8, c_ctx):
    def body(p_ref, c_ref, o_ref):
        s = p_ref[0]
        for chip in range(1, N_CHIPS):
            s = s + p_ref[2 * chip]
        o_ref[...] = s * _dsilu(c_ref[...])

    return pl.pallas_call(
        body, name="cctx_grad", out_shape=jax.ShapeDtypeStruct((1, D), F32),
        in_specs=[pl.BlockSpec(memory_space=pltpu.VMEM)] * 2, out_specs=pl.BlockSpec(memory_space=pltpu.VMEM),
    )(p8, c_ctx)


def _adam_small(w, g, m, v):
    def body(w_ref, g_ref, m_ref, v_ref, dl_ref, m2_ref, v2_ref):
        dl_ref[...], m2_ref[...], v2_ref[...] = _adam(w_ref[...], g_ref[...], m_ref[...], v_ref[...])

    vm = pl.BlockSpec(memory_space=pltpu.VMEM)
    return pl.pallas_call(
        body, name="adam_small", out_shape=[jax.ShapeDtypeStruct(w.shape, F32)] * 3, in_specs=[vm] * 4, out_specs=[vm] * 3,
        compiler_params=pltpu.CompilerParams(vmem_limit_bytes=VMEM_LIMIT),
    )(w, g, m, v)


def _pack(vecs, width=SMALL_W, row_mult=8):
    flat = jnp.concatenate([v.reshape(-1).astype(F32) for v in vecs])
    n = flat.shape[0]
    rows = -(-n // (width * row_mult)) * row_mult
    return jnp.pad(flat, (0, rows * width - n)).reshape(rows, width)


def _unpack(packed, shapes):
    flat = packed.reshape(-1)
    out, off = [], 0
    for s in shapes:
        n = int(np.prod(s))
        out.append(flat[off:off + n].reshape(s))
        off += n
    return out


WEIGHTS = ('c_ctx', 'w_ada', 'b_ada', 'norm1_w', 'w_in', 'gla_up_f', 'gla_bias_f', 'gla_up_b', 'gla_bias_b', 'gla_norm_w',
           'conv_w', 'conv_b', 'dt_bias_f', 'dt_bias_b', 'a_log_f', 'a_log_b', 'd_skip', 'ssm_norm_w', 'w_pa', 'w_pb', 'w_out',
           'norm2_w', 'w_gate', 'w_up', 'w_down', 'final_norm_w')
BIG = ('w_in', 'w_pa', 'w_pb', 'w_out', 'w_gate', 'w_up', 'w_down')
COL_SHARDED = ('w_in', 'w_gate', 'w_up')
SMALL_SHARDED = ('gla_up_f', 'gla_up_b', 'conv_w')
ROW_TILE = 256


def _blocks_to_full(g4, name):
    n, r, c = g4.shape
    return g4.transpose(1, 0, 2).reshape(r, n * c) if name in COL_SHARDED else g4.reshape(n * r, c)


def _full_to_blocks(full, name):
    r, c = full.shape
    if name in COL_SHARDED:
        return full.reshape(r, N_CHIPS, c // N_CHIPS).transpose(1, 0, 2)
    return full.reshape(N_CHIPS, r // N_CHIPS, c)


def _permute_in(w_in_full):
    off = np.concatenate([[0], np.cumsum(IN_WIDTHS)])
    cols = [w_in_full[:, off[p]:off[p + 1]] for p in PERM]
    return jnp.concatenate(cols + [jnp.zeros((w_in_full.shape[0], SMALL_PAD), w_in_full.dtype)], axis=1)


def _unpermute_in(wp):
    off = np.concatenate([[0], np.cumsum([IN_WIDTHS[p] for p in PERM])])
    pieces = {p: wp[:, off[i]:off[i + 1]] for i, p in enumerate(PERM)}
    return jnp.concatenate([pieces[p] for p in range(len(IN_WIDTHS))], axis=1)


def _chip_cols(full, chip, n):
    return lax.dynamic_slice_in_dim(full, chip * n, n, axis=1)


def kernel(x, c, ctx, c_ctx, w_ada, b_ada, norm1_w, w_in, gla_up_f, gla_bias_f, gla_up_b, gla_bias_b, gla_norm_w, conv_w, conv_b, dt_bias_f, dt_bias_b, a_log_f, a_log_b, d_skip, ssm_norm_w, w_pa, w_pb, w_out, norm2_w, w_gate, w_up, w_down, final_norm_w, loss_target, m_c_ctx, m_w_ada, m_b_ada, m_norm1_w, m_w_in, m_gla_up_f, m_gla_bias_f, m_gla_up_b, m_gla_bias_b, m_gla_norm_w, m_conv_w, m_conv_b, m_dt_bias_f, m_dt_bias_b, m_a_log_f, m_a_log_b, m_d_skip, m_ssm_norm_w, m_w_pa, m_w_pb, m_w_out, m_norm2_w, m_w_gate, m_w_up, m_w_down, m_final_norm_w, v_c_ctx, v_w_ada, v_b_ada, v_norm1_w, v_w_in, v_gla_up_f, v_gla_bias_f, v_gla_up_b, v_gla_bias_b, v_gla_norm_w, v_conv_w, v_conv_b, v_dt_bias_f, v_dt_bias_b, v_a_log_f, v_a_log_b, v_d_skip, v_ssm_norm_w, v_w_pa, v_w_pb, v_w_out, v_norm2_w, v_w_gate, v_w_up, v_w_down, v_final_norm_w):
    given = dict(locals())
    W = {n: given[n] for n in WEIGHTS}
    M = {n: given["m_" + n] for n in WEIGHTS}
    V = {n: given["v_" + n] for n in WEIGHTS}
    L, Lc = x.shape[1], ctx.shape[1]
    tr = ROW_TILE
    assert L % tr == 0 and Lc % tr == 0 and L % Lc == 0 and Lc % SSM_C == 0
    n_lat_tiles = L // tr
    xi, yi, ci = _place()
    chip, me = 2 * xi + yi, 4 * xi + 2 * yi + ci
    xall = jnp.concatenate([x[0], ctx[0]], axis=0)

    g0 = _allgather_small(_pack([c[0]] + [W[n][0] for n in SMALL_SHARDED]), "gather_c")
    g0 = g0.reshape(N_DEV, -1)
    c_all = g0[:, :D]
    small_full, off = {}, D
    for n in SMALL_SHARDED:
        r, cols = W[n].shape[1:]
        small_full[n] = jnp.concatenate([g0[2 * k, off:off + r * cols].reshape(r, cols) for k in range(N_CHIPS)], axis=1)
        off += r * cols
    up_f, up_b, conv_w_full = (small_full[n] for n in SMALL_SHARDED)

    cvec = jnp.zeros((ADA_ROWS, D), F32).at[:N_DEV].set(c_all).at[N_DEV].set(c_ctx)
    ada_cols = w_ada.shape[2]
    ada_part = _ada_fwd(cvec, w_ada[0], _chip_cols(b_ada, chip, ada_cols))
    g1_all = _allgather_small(ada_part, "gather_ada")
    ada_full = jnp.concatenate([g1_all[2 * k] for k in range(N_CHIPS)], axis=1)
    mine = lax.dynamic_slice_in_dim(ada_full, me, 1, axis=0)
    sh1, sc1, g1, sh2, sc2, g2 = (mine[:, k * D:(k + 1) * D] for k in range(6))
    csh1, csc1 = ada_full[N_DEV:N_DEV + 1, :D], ada_full[N_DEV:N_DEV + 1, D:2 * D]
    mod = jnp.stack([jnp.stack([sh1, sc1]), jnp.stack([csh1, csc1])])

    full = {'w_in': _blocks_to_full(_gather_split(w_in[0].astype(BF16), "gather_w_in"), 'w_in')}
    wp = _permute_in(full['w_in'])
    later = [n for n in BIG if n != 'w_in']

    def lr_rows(up, base):
        return jnp.zeros((128, GLA_H * GLA_DK), F32).at[base:base + GLA_RANK].set(up)
    u2 = jnp.stack([lr_rows(up_f, SM_LRF), lr_rows(up_b, SM_LRB)])
    u2_hi = u2.astype(BF16)
    u2_lo = (u2 - u2_hi.astype(F32)).astype(BF16)
    ut = u2.transpose(0, 2, 1)
    ut_hi = ut.astype(BF16)
    ut_lo = (ut - ut_hi.astype(F32)).astype(BF16)
    gbias = jnp.stack([gla_bias_f, gla_bias_b])
    kc = _ssd_consts(jnp.stack([dt_bias_f[0], dt_bias_b[0]]), jnp.stack([a_log_f[0], a_log_b[0]]))
    gw4 = jnp.tile(gla_norm_w, (1, GLA_H))
    dskip_exp = jnp.repeat(d_skip, SSM_P, axis=1)
    n_gla = (L // GLA_C, Lc // GLA_C)
    n_ssd = (L // SSM_C, Lc // SSM_C)

    h1 = _norm_mod(xall, norm1_w, mod, n_lat_tiles, tr)
    parts = _mm(h1, wp, "nn", ACT, "mm_in", tm=768, tn=1152)
    sm = _mm(h1, wp[:, COL_SM:], "nn", F32, "mm_in_small", tm=768)
    xbc = _conv_fwd(parts, conv_w_full, conv_b, L // Lc, Lc)
    *o2, gla_hist = _gla_fwd(parts, sm, u2_hi, u2_lo, gbias, *n_gla)
    (*y2, ssd_hist), gathered = _ssd_fwd(xbc, sm, kc, *n_ssd, _gather_comm([W[n][0].astype(BF16) for n in later]))
    full.update({n: _blocks_to_full(g, n) for n, g in zip(later, gathered)})
    oan = _gla_out(o2, parts, gw4, tr)
    obn = _ssd_out(y2, xbc, parts, dskip_exp, ssm_norm_w, tr)
    ya = _mm(oan, full['w_pa'], "nn", ACT, "mm_pa", tm=768)
    yb = _mm(obn, full['w_pb'], "nn", ACT, "mm_pb", tm=768)
    merged = _merge(ya, yb, parts, tr)
    mix = _mm(merged, full['w_out'], "nn", ACT, "mm_out", tm=768)
    h2, u = _resid_norm_mod(xall, mix, g1, norm2_w, sh2, sc2, tr)
    gp = _mm(u, full['w_gate'], "nn", ACT, "mm_gate", tm=768, tn=1408)
    up = _mm(u, full['w_up'], "nn", ACT, "mm_up", tm=768, tn=1408)
    act = _swiglu_act(gp, up, tr)
    f = _mm(act, full['w_down'], "nn", ACT, "mm_down", tm=768)
    dh3, df, acc_loss = _loss_head(h2, f, loss_target[0], g2, final_norm_w[None], n_lat_tiles, tr)

    dw = {}
    da = _mm(df, full['w_down'], "nt", ACT, "mm_down_dx", tm=768, tn=1408)
    dw['w_down'] = _mm(act, df, "tn", BF16, "mm_down_dw", tm=1408, tk=768)
    dgp, dup = _swiglu_act_bwd(da, gp, up, tr)
    du_a = _mm(dgp, full['w_gate'], "nt", ACT, "mm_gate_dx", tm=768, tk=1408)
    du_b = _mm(dup, full['w_up'], "nt", ACT, "mm_up_dx", tm=768, tk=1408)
    dw['w_gate'] = _mm(u, dgp, "tn", BF16, "mm_gate_dw", tm=1024, tn=1408, tk=768)
    dw['w_up'] = _mm(u, dup, "tn", BF16, "mm_up_dw", tm=1024, tn=1408, tk=768)
    dh2, dmix, acc_ffn = _ffn_in_bwd(du_a, du_b, h2, dh3, mix, sc2, g1, norm2_w, tr)
    dmerged = _mm(dmix, full['w_out'], "nt", ACT, "mm_out_dx", tm=768)
    dw['w_out'] = _mm(merged, dmix, "tn", BF16, "mm_out_dw", tm=1024, tk=768)
    dya, dyb, dparts = _merge_bwd(dmerged, ya, yb, parts, lax.empty((L + Lc, PW), BF16), tr)
    doan = _mm(dya, full['w_pa'], "nt", ACT, "mm_pa_dx", tm=768)
    dw['w_pa'] = _mm(oan, dya, "tn", BF16, "mm_pa_dw", tm=1024, tk=768)
    dobn = _mm(dyb, full['w_pb'], "nt", ACT, "mm_pb_dx", tm=768)
    dw['w_pb'] = _mm(obn, dyb, "tn", BF16, "mm_pb_dw", tm=1024, tk=768)
    do, dparts, acc_gla = _gla_out_bwd(doan, o2, parts, gw4, dparts, tr)
    dq, dk, dv, dpre = _gla_bwd(do, parts, sm, u2_hi, u2_lo, gbias, gla_hist, *n_gla)
    dy, dparts, acc_ssd = _ssd_out_bwd(dobn, y2, xbc, parts, dskip_exp, ssm_norm_w, dparts, tr)
    (dx_scan, db_scan, dc_scan, dsm, acc_alog), exchanged = _ssd_bwd(
        dy, xbc, sm, kc, ssd_hist, *n_ssd, _exchange_comm([_full_to_blocks(dw[n], n) for n in later]))
    exchanged = dict(zip(later, exchanged))
    dparts, acc_conv = _conv_bwd(dx_scan, db_scan, dc_scan, dy, dskip_exp, parts, conv_w_full, conv_b, dparts, L // Lc, Lc)
    dparts = _gla_assemble(dq, dk, dv, dparts, tr)
    dparts, dup_gla, acc_gbias, acc_dtb = _small_assemble(dpre, dsm, sm, ut_hi, ut_lo, dparts, tr)
    dw['w_in'] = _unpermute_in(_mm(h1, dparts, "tn", BF16, "mm_in_dw", tm=1024, tn=1152, tk=768))
    dh1, (exchanged['w_in'],) = _mm(dparts, wp, "nt", F32, "mm_in_dx", tm=768, tk=1152,
                                    comm=_exchange_comm([_full_to_blocks(dw['w_in'], 'w_in')]))
    dxall, acc_n1 = _norm1_bwd(dh1, xall, dh2, norm1_w, mod, n_lat_tiles, tr)

    partial = dict(
        norm1_w=acc_n1[0, 2] + acc_n1[1, 2],
        gla_up_f=dup_gla[0, SM_LRF:SM_LRF + GLA_RANK], gla_bias_f=acc_gbias[0],
        gla_up_b=dup_gla[1, SM_LRB:SM_LRB + GLA_RANK], gla_bias_b=acc_gbias[1],
        gla_norm_w=acc_gla[0].reshape(GLA_H, GLA_DV).sum(0),
        conv_w=acc_conv[:SSM_CONV], conv_b=acc_conv[SSM_CONV],
        dt_bias_f=acc_dtb[0, SM_DTF:SM_DTF + SSM_HEADS], dt_bias_b=acc_dtb[0, SM_DTB:SM_DTB + SSM_HEADS],
        a_log_f=acc_alog[0, :, 0, :SSM_HPG], a_log_b=acc_alog[1, :, 0, :SSM_HPG],
        d_skip=acc_ssd[1].reshape(SSM_HEADS, SSM_P).sum(1), ssm_norm_w=acc_ssd[0],
        norm2_w=acc_ffn[2], final_norm_w=acc_loss[0],
    )
    dada = jnp.concatenate([acc_n1[0, 1], acc_n1[0, 0], acc_ffn[3], acc_ffn[1], acc_ffn[0], acc_loss[1]])
    dada_ctx = jnp.concatenate([acc_n1[1, 1], acc_n1[1, 0], jnp.zeros((4 * D,), F32)])
    names = list(partial)
    payload = [partial[n] for n in names] + [dada + dada_ctx, dada_ctx, acc_loss[2], dada]
    sizes = [int(np.prod(p.shape)) for p in payload]
    g8 = _allgather_small(_pack(payload), "gather_small_grads")
    summed = _unpack(_sum8(g8), [(s,) for s in sizes])
    grads = {n: s.reshape(W[n].shape if n not in SMALL_SHARDED else partial[n].shape) for n, s in zip(names, summed)}
    grads['b_ada'] = summed[len(names)].reshape(b_ada.shape)
    dada_ctx_sum = summed[len(names) + 1]
    loss = jnp.sum(summed[len(names) + 2])
    dada_all = g8.reshape(N_DEV, -1)[:, sum(sizes[:-1]):sum(sizes)]

    dada16 = jnp.zeros((ADA_ROWS, ada_cols), F32)
    dada16 = dada16.at[:N_DEV].set(_chip_cols(dada_all, chip, ada_cols)).at[N_DEV].set(_chip_cols(dada_ctx_sum[None], chip, ada_cols)[0])
    g_wada, dl_wada, m_wada, v_wada, p16 = _wada_bwd_adam(cvec, dada16, w_ada[0], m_w_ada[0], v_w_ada[0])
    p8 = _allgather_small(p16[N_DEV:], "gather_cctx")
    grads['c_ctx'] = _cctx_grad(p8[:, 0:1, :], c_ctx[None])[0]
    for n in SMALL_SHARDED:
        grads[n] = _chip_cols(grads[n], chip, W[n].shape[2])[None]

    small = [n for n in WEIGHTS if n not in BIG and n != 'w_ada']
    shapes = [W[n].shape for n in small]
    dl_s, m_s, v_s = _adam_small(*[_pack([d[n] for n in small]) for d in (W, grads, M, V)])
    delta = dict(zip(small, _unpack(dl_s, shapes)))
    new_m = dict(zip(small, _unpack(m_s, shapes)))
    new_v = dict(zip(small, _unpack(v_s, shapes)))
    grads['w_ada'], delta['w_ada'], new_m['w_ada'], new_v['w_ada'] = g_wada[None], dl_wada[None], m_wada[None], v_wada[None]

    for n in BIG:
        g, dl, m2, v2 = _reduce_adam(exchanged[n], W[n][0], M[n][0], V[n][0], "adam_" + n)
        grads[n], delta[n], new_m[n], new_v[n] = g[None], dl[None], m2[None], v2[None]

    return (loss, dxall[:L][None], *[grads[n] for n in WEIGHTS], *[delta[n] for n in WEIGHTS],
            *[new_m[n] for n in WEIGHTS], *[new_v[n] for n in WEIGHTS])
```

```python
import functools

import numpy as np
import jax
import jax.numpy as jnp
from jax import lax
from jax.experimental import pallas as pl
from jax.experimental.pallas import tpu as pltpu

F32 = jnp.float32
BF16 = jnp.bfloat16
MESH = pl.DeviceIdType.MESH

D = 1024
EPS = 1e-6
GRID_W = 64
GLA_H, GLA_DK, GLA_DV, GLA_RANK, GLA_TAU = 4, 128, 256, 16, 16.0
GLA_C = 128
SSM_INNER, SSM_P, SSM_HEADS, SSM_G, SSM_HPG, SSM_N = 2048, 64, 32, 4, 8, 128
SSM_C = 128
SSM_CONV, CONV_LEFT = 4, 2
D_FF = 2816
IN_WIDTHS = (512, 512, 1024, 1024, 16, 16, 2048, 2048, 512, 512, 32, 32, 1024, 1024)
D_IN = sum(IN_WIDTHS)
PERM = (6, 7, 8, 9, 3, 0, 1, 2, 12, 13, 4, 5, 10, 11)
PW = 10368
SMALL_PAD = PW - D_IN
COL_Z, COL_XBC, COL_R, COL_Q, COL_K, COL_V, COL_GA, COL_GB, COL_SM = 0, 2048, 5120, 6144, 6656, 7168, 8192, 9216, 10240
SM_LRF, SM_LRB, SM_DTF, SM_DTB = 0, 16, 32, 64
EXP_CLAMP = 80.0
ADAM_LR, ADAM_B1, ADAM_B2, ADAM_EPS, ADAM_WD, ADAM_STEP = 0.001, 0.9, 0.999, 1e-08, 0.01, 10
N_CHIPS, N_DEV = 4, 8
VMEM_LIMIT = 56 * 1024 * 1024


def _cparams(sem=None):
    return pltpu.CompilerParams(dimension_semantics=sem, vmem_limit_bytes=VMEM_LIMIT)


def _dg(a, b, ca, cb):
    return lax.dot_general(a, b, (((ca,), (cb,)), ((), ())), preferred_element_type=F32)


def _nn(a, b):
    return _dg(a, b, 1, 0)


def _nt(a, b):
    return _dg(a, b, 1, 1)


def _tn(a, b):
    return _dg(a, b, 0, 0)


def _bf(x):
    return x.astype(BF16)


def _split(x):
    hi = x.astype(BF16)
    return hi, (x - hi.astype(F32)).astype(BF16)


def _nn_x(a, b_exact):
    hi, lo = _split(a)
    return _nn(hi, b_exact) + _nn(lo, b_exact)


def _x_nn(a_exact, b):
    hi, lo = _split(b)
    return _nn(a_exact, hi) + _nn(a_exact, lo)


def _nn3(a, b_hi, b_lo):
    hi, lo = _split(a)
    return _nn(hi, b_hi) + _nn(lo, b_hi) + _nn(hi, b_lo)


def _sigmoid(x):
    return 1.0 / (1.0 + jnp.exp(-x))


def _silu(x):
    return x * _sigmoid(x)


def _dsilu(x):
    s = _sigmoid(x)
    return s * (1.0 + x * (1.0 - s))


def _softplus(x):
    return jnp.maximum(x, 0.0) + jnp.log(1.0 + jnp.exp(-jnp.abs(x)))


def _log_sigmoid(x):
    return jnp.minimum(x, 0.0) - jnp.log(1.0 + jnp.exp(-jnp.abs(x)))


def _tile(n, target, mult=8):
    best = None
    for t in range(mult, min(n, target) + 1, mult):
        if n % t == 0:
            best = t
    assert best is not None, (n, target, mult)
    return best


def _mm(a, b, mode, out_dtype, name, tm=512, tn=1024, tk=2048, comm=None):
    if mode == "nn":
        (M, K), N = a.shape, b.shape[1]
    elif mode == "nt":
        (M, K), N = a.shape, b.shape[0]
    else:
        (K, M), N = a.shape, b.shape[1]
    tm, tn, tk = _tile(M, tm, 128), _tile(N, tn, 128), _tile(K, tk, 128)
    nk = K // tk
    ca, cb = {"nn": (1, 0), "nt": (1, 1), "tn": (0, 0)}[mode]

    def body(a_ref, b_ref, o_ref, *acc):
        part = _dg(a_ref[...], b_ref[...], ca, cb)
        if nk == 1:
            o_ref[...] = part.astype(out_dtype)
        else:
            k = pl.program_id(2)

            @pl.when(k == 0)
            def _():
                acc[0][...] = part

            @pl.when(k > 0)
            def _():
                acc[0][...] += part

            @pl.when(k == nk - 1)
            def _():
                o_ref[...] = acc[0][...].astype(out_dtype)

    a_spec = pl.BlockSpec((tk, tm), lambda i, j, k: (k, i)) if mode == "tn" else pl.BlockSpec((tm, tk), lambda i, j, k: (i, k))
    b_spec = pl.BlockSpec((tn, tk), lambda i, j, k: (j, k)) if mode == "nt" else pl.BlockSpec((tk, tn), lambda i, j, k: (k, j))
    gi, gj = M // tm, N // tn
    scratch = [pltpu.VMEM((tm, tn), F32)] if nk > 1 else []
    out_spec, out_shape = pl.BlockSpec((tm, tn), lambda i, j, k: (i, j)), jax.ShapeDtypeStruct((M, N), out_dtype)
    if comm is None:
        return pl.pallas_call(
            body, name=name, grid=(gi, gj, nk), in_specs=[a_spec, b_spec], out_specs=out_spec, out_shape=out_shape,
            scratch_shapes=scratch, compiler_params=_cparams(("arbitrary", "arbitrary", "arbitrary")),
        )(a, b)
    at = lambda i, j, k: (pl.program_id(0) == i) & (pl.program_id(1) == j) & (pl.program_id(2) == k)
    hosted = _hosted(body, 2, 1, len(scratch), comm, lambda: at(0, 0, 0), lambda: at(gi - 1, 0, 0),
                     lambda: at(gi - 1, gj - 1, nk - 1))
    outs = pl.pallas_call(
        hosted, name=name, grid=(gi, gj, nk), in_specs=[a_spec, b_spec] + [_ANY] * len(comm.arrays),
        out_specs=[out_spec] + [_ANY] * len(comm.out_shape), out_shape=[out_shape] + comm.out_shape,
        scratch_shapes=scratch + comm.scratch, compiler_params=_cparams(("arbitrary", "arbitrary", "arbitrary")),
    )(a, b, *comm.arrays)
    return outs[0], outs[1:]


def _place():
    return lax.axis_index("x"), lax.axis_index("y"), lax.axis_index("c")


def _flip(v, bit):
    return 1 - v if bit else v


def _allgather_small(v, name):
    R, C = v.shape

    def body(v_ref, out_ref, send_sems, recv_sems, local_sem):
        x, y, c = _place()
        me = 4 * x + 2 * y + c
        mine = pltpu.make_async_copy(v_ref, out_ref.at[me], local_sem)
        mine.start()

        def peer(r):
            return _flip(x, (r >> 2) & 1), _flip(y, (r >> 1) & 1), _flip(c, r & 1)

        sends = [pltpu.make_async_remote_copy(
            src_ref=v_ref, dst_ref=out_ref.at[me], send_sem=send_sems.at[r - 1], recv_sem=recv_sems.at[r - 1],
            device_id=peer(r), device_id_type=MESH) for r in range(1, N_DEV)]
        for cp in sends:
            cp.start()
        for r in range(1, N_DEV):
            px, py, pc = peer(r)
            pltpu.make_async_remote_copy(
                src_ref=v_ref, dst_ref=out_ref.at[4 * px + 2 * py + pc], send_sem=send_sems.at[r - 1],
                recv_sem=recv_sems.at[r - 1], device_id=(x, y, c), device_id_type=MESH).wait_recv()
        for cp in sends:
            cp.wait_send()
        mine.wait()

    return pl.pallas_call(
        body, name=name, out_shape=jax.ShapeDtypeStruct((N_DEV, R, C), v.dtype),
        in_specs=[pl.BlockSpec(memory_space=pltpu.VMEM)], out_specs=pl.BlockSpec(memory_space=pltpu.VMEM),
        scratch_shapes=[pltpu.SemaphoreType.DMA((N_DEV - 1,)), pltpu.SemaphoreType.DMA((N_DEV - 1,)), pltpu.SemaphoreType.DMA],
        compiler_params=pltpu.CompilerParams(vmem_limit_bytes=VMEM_LIMIT),
    )(v)


_CHIP_RELATIONS = ((1, 0), (0, 1), (1, 1))


def _gather_split(shard, name):
    rows, cols = shard.shape
    half = rows // 2

    def body(in_ref, out_ref, send_sems, recv_sems, local_sem):
        x, y, c = _place()
        chip = 2 * x + y
        mine = pl.ds(pl.multiple_of(c * half, 16), half)
        local = pltpu.make_async_copy(in_ref, out_ref.at[chip], local_sem)
        local.start()
        peers = [(_flip(x, fx), _flip(y, fy)) for fx, fy in _CHIP_RELATIONS]
        sends = [pltpu.make_async_remote_copy(
            src_ref=in_ref.at[mine], dst_ref=out_ref.at[chip, mine], send_sem=send_sems.at[j], recv_sem=recv_sems.at[j],
            device_id=(px, py, c), device_id_type=MESH) for j, (px, py) in enumerate(peers)]
        for cp in sends:
            cp.start()
        for j, (px, py) in enumerate(peers):
            landed = out_ref.at[2 * px + py, mine]
            pltpu.make_async_remote_copy(
                src_ref=landed, dst_ref=landed, send_sem=send_sems.at[j], recv_sem=recv_sems.at[j],
                device_id=(x, y, c), device_id_type=MESH).wait_recv()
            fwd = pltpu.make_async_remote_copy(
                src_ref=landed, dst_ref=landed, send_sem=send_sems.at[3 + j], recv_sem=recv_sems.at[3 + j],
                device_id=(x, y, 1 - c), device_id_type=MESH)
            fwd.start()
            sends.append(fwd)
        for j in range(3):
            landed = out_ref.at[0, mine]
            pltpu.make_async_remote_copy(
                src_ref=landed, dst_ref=landed, send_sem=send_sems.at[3 + j], recv_sem=recv_sems.at[3 + j],
                device_id=(x, y, c), device_id_type=MESH).wait_recv()
        for cp in sends:
            cp.wait_send()
        local.wait()

    any_spec = pl.BlockSpec(memory_space=pl.ANY)
    return pl.pallas_call(
        body, name=name, out_shape=jax.ShapeDtypeStruct((N_CHIPS, rows, cols), shard.dtype),
        in_specs=[any_spec], out_specs=any_spec,
        scratch_shapes=[pltpu.SemaphoreType.DMA((6,)), pltpu.SemaphoreType.DMA((6,)), pltpu.SemaphoreType.DMA],
    )(shard)


class _Comm:
    def __init__(self, arrays, out_shape, scratch, start, middle, finish):
        self.arrays, self.out_shape, self.scratch = arrays, out_shape, scratch
        self.start, self.middle, self.finish = start, middle, finish


def _hosted(body, n_in, n_out, n_scratch, comm, first, middle, last):
    nc, no = len(comm.arrays), len(comm.out_shape)

    def wrapped(*refs):
        a = n_in + nc
        b = a + n_out + no
        ins, c_ins, outs, c_outs = refs[:n_in], refs[n_in:a], refs[a:a + n_out], refs[a + n_out:b]
        scratch, c_sems = refs[b:b + n_scratch], refs[b + n_scratch:]

        @pl.when(first())
        def _():
            comm.start(c_ins, c_outs, c_sems)

        body(*ins, *outs, *scratch)
        if comm.middle is not None:
            @pl.when(middle())
            def _():
                comm.middle(c_ins, c_outs, c_sems)

        @pl.when(last())
        def _():
            comm.finish(c_ins, c_outs, c_sems)

    return wrapped


_ANY = pl.BlockSpec(memory_space=pl.ANY)


def _gather_comm(shards):
    n = len(shards)

    def copies(kind, ins, outs, sems):
        send_sems, recv_sems, local_sems = sems
        x, y, c = _place()
        chip = 2 * x + y
        if kind == "local":
            return [pltpu.make_async_copy(ins[i], outs[i].at[chip], local_sems.at[i]) for i in range(n)]
        made = []
        for i in range(n):
            for j, (fx, fy) in enumerate(_CHIP_RELATIONS):
                px, py = _flip(x, fx), _flip(y, fy)
                slot, to = (chip, (px, py, c)) if kind == "send" else (2 * px + py, (x, y, c))
                made.append(pltpu.make_async_remote_copy(
                    src_ref=ins[i], dst_ref=outs[i].at[slot], send_sem=send_sems.at[i, j], recv_sem=recv_sems.at[i, j],
                    device_id=to, device_id_type=MESH))
        return made

    def start(ins, outs, sems):
        for cp in copies("local", ins, outs, sems) + copies("send", ins, outs, sems):
            cp.start()

    def finish(ins, outs, sems):
        for cp in copies("recv", ins, outs, sems):
            cp.wait_recv()
        for cp in copies("send", ins, outs, sems):
            cp.wait_send()
        for cp in copies("local", ins, outs, sems):
            cp.wait()

    return _Comm(list(shards), [jax.ShapeDtypeStruct((N_CHIPS,) + s.shape, s.dtype) for s in shards],
                 [pltpu.SemaphoreType.DMA((n, 3)), pltpu.SemaphoreType.DMA((n, 3)), pltpu.SemaphoreType.DMA((n,))],
                 start, None, finish)


def _exchange_comm(blocks):
    n = len(blocks)

    def copies(kind, ins, outs, sems):
        send_sems, recv_sems, local_sems = sems
        x, y, c = _place()
        chip = 2 * x + y
        me, sibling = (x, y, c), (x, y, 1 - c)

        def remote(src, dst, i, j, to):
            return pltpu.make_async_remote_copy(src_ref=src, dst_ref=dst, send_sem=send_sems.at[i, j],
                                                recv_sem=recv_sems.at[i, j], device_id=to, device_id_type=MESH)

        made = []
        for i in range(n):
            if kind == "local":
                made.append(pltpu.make_async_copy(ins[i].at[chip], outs[i].at[chip], local_sems.at[i]))
                continue
            for j, (fx, fy) in enumerate(_CHIP_RELATIONS):
                px, py = _flip(x, fx), _flip(y, fy)
                src = 2 * px + py
                if kind == "first":
                    made.append(remote(ins[i].at[src], outs[i].at[chip], i, j, (px, py, c)))
                elif kind == "landed":
                    made.append(remote(ins[i].at[src], outs[i].at[src], i, j, me))
                elif kind == "passed":
                    made.append(remote(outs[i].at[src], outs[i].at[N_CHIPS + src], i, 4 + j, sibling))
            if kind == "first":
                made.append(remote(ins[i].at[chip], outs[i].at[N_CHIPS + chip], i, 3, sibling))
            if kind == "arrivals":
                made += [remote(ins[i].at[0], outs[i].at[0], i, j, me) for j in (3, 4, 5, 6)]
        return made

    def start(ins, outs, sems):
        for cp in copies("local", ins, outs, sems) + copies("first", ins, outs, sems):
            cp.start()

    def middle(ins, outs, sems):
        for got, fwd in zip(copies("landed", ins, outs, sems), copies("passed", ins, outs, sems)):
            got.wait_recv()
            fwd.start()

    def finish(ins, outs, sems):
        for cp in copies("arrivals", ins, outs, sems):
            cp.wait_recv()
        for cp in copies("first", ins, outs, sems) + copies("passed", ins, outs, sems):
            cp.wait_send()
        for cp in copies("local", ins, outs, sems):
            cp.wait()

    return _Comm(list(blocks), [jax.ShapeDtypeStruct((N_DEV,) + b.shape[1:], b.dtype) for b in blocks],
                 [pltpu.SemaphoreType.DMA((n, 7)), pltpu.SemaphoreType.DMA((n, 7)), pltpu.SemaphoreType.DMA((n,))],
                 start, middle, finish)


def _row_spec(tr, w, col=0):
    return pl.BlockSpec((tr, w), lambda i: (i, col))


def _vec_spec(w):
    return pl.BlockSpec((1, w), lambda i: (0, 0))


def _acc_spec(w):
    return pl.BlockSpec((8, w), lambda i: (0, 0))


def _rms(x):
    return lax.rsqrt(jnp.mean(x * x, axis=-1, keepdims=True) + EPS)


def _rms_bwd(dn, n, rstd):
    return rstd * (dn - n * jnp.mean(dn * n, axis=-1, keepdims=True))


def _colsum(x):
    return jnp.sum(x, axis=0, keepdims=True)


def _zero_first(ref):
    @pl.when(pl.program_id(0) == 0)
    def _():
        ref[...] = jnp.zeros_like(ref)


def _norm_mod(xall, w, mod, n_lat_tiles, tr):
    R = xall.shape[0]

    def body(x_ref, w_ref, mod_ref, o_ref):
        x = x_ref[...]
        n = x * _rms(x) * w_ref[...]
        o_ref[...] = (n * (1.0 + mod_ref[0, 1]) + mod_ref[0, 0]).astype(BF16)

    return pl.pallas_call(
        body, name="norm1_mod", grid=(R // tr,),
        in_specs=[_row_spec(tr, D), _vec_spec(D),
                  pl.BlockSpec((1, 2, 1, D), lambda i: (jnp.where(i >= n_lat_tiles, 1, 0), 0, 0, 0))],
        out_specs=_row_spec(tr, D), out_shape=jax.ShapeDtypeStruct((R, D), BF16),
        compiler_params=_cparams(("arbitrary",)),
    )(xall, w, mod)


def _resid_norm_mod(xall, mix, g1, w2, sh2, sc2, tr):
    R = xall.shape[0]

    def body(x_ref, mix_ref, g1_ref, w_ref, sh_ref, sc_ref, h2_ref, u_ref):
        h2 = x_ref[...] + g1_ref[...] * mix_ref[...]
        h2_ref[...] = h2
        n = h2 * _rms(h2) * w_ref[...]
        u_ref[...] = (n * (1.0 + sc_ref[...]) + sh_ref[...]).astype(BF16)

    return pl.pallas_call(
        body, name="resid_norm2_mod", grid=(R // tr,),
        in_specs=[_row_spec(tr, D), _row_spec(tr, D)] + [_vec_spec(D)] * 4,
        out_specs=[_row_spec(tr, D), _row_spec(tr, D)],
        out_shape=[jax.ShapeDtypeStruct((R, D), F32), jax.ShapeDtypeStruct((R, D), BF16)],
        compiler_params=_cparams(("arbitrary",)),
    )(xall, mix, g1, w2, sh2, sc2)


def _loss_head(h2, f, target, g2, fw, n_lat_tiles, tr):
    R = h2.shape[0]

    def body(h2_ref, f_ref, t_ref, g2_ref, fw_ref, dh3_ref, df_ref, acc_ref):
        _zero_first(acc_ref)
        lat = pl.program_id(0) < n_lat_tiles
        fv = f_ref[...]
        h3 = h2_ref[...] + g2_ref[...] * fv
        rstd = _rms(h3)
        n = h3 * rstd
        err = n * fw_ref[...] - t_ref[...]
        dy = err * (1.0 / D)
        dh3 = jnp.where(lat, _rms_bwd(dy * fw_ref[...], n, rstd), 0.0)
        dh3_ref[...] = dh3
        df_ref[...] = (g2_ref[...] * dh3).astype(BF16)
        acc_ref[0:1, :] += jnp.where(lat, _colsum(dy * n), 0.0)
        acc_ref[1:2, :] += _colsum(dh3 * fv)
        acc_ref[2:3, :] += jnp.where(lat, _colsum(err * err) * (0.5 / D), 0.0)

    return pl.pallas_call(
        body, name="loss_head", grid=(R // tr,),
        in_specs=[_row_spec(tr, D), _row_spec(tr, D),
                  pl.BlockSpec((tr, D), lambda i: (jnp.minimum(i, n_lat_tiles - 1), 0)), _vec_spec(D), _vec_spec(D)],
        out_specs=[_row_spec(tr, D), _row_spec(tr, D), _acc_spec(D)],
        out_shape=[jax.ShapeDtypeStruct((R, D), F32), jax.ShapeDtypeStruct((R, D), BF16), jax.ShapeDtypeStruct((8, D), F32)],
        compiler_params=_cparams(("arbitrary",)),
    )(h2, f, target, g2, fw)


def _ffn_in_bwd(du_a, du_b, h2, dh3, mix, sc2, g1, w2, tr):
    R = h2.shape[0]

    def body(dua_ref, dub_ref, h2_ref, dh3_ref, mix_ref, sc_ref, g1_ref, w_ref, dh2_ref, dmix_ref, acc_ref):
        _zero_first(acc_ref)
        du = _f32(dua_ref) + _f32(dub_ref)
        h2 = h2_ref[...]
        rstd = _rms(h2)
        n = h2 * rstd
        dnw = du * (1.0 + sc_ref[...])
        dh2 = dh3_ref[...] + _rms_bwd(dnw * w_ref[...], n, rstd)
        dh2_ref[...] = dh2
        dmix_ref[...] = (g1_ref[...] * dh2).astype(BF16)
        acc_ref[0:1, :] += _colsum(du * n * w_ref[...])
        acc_ref[1:2, :] += _colsum(du)
        acc_ref[2:3, :] += _colsum(dnw * n)
        acc_ref[3:4, :] += _colsum(dh2 * mix_ref[...])

    return pl.pallas_call(
        body, name="ffn_in_bwd", grid=(R // tr,),
        in_specs=[_row_spec(tr, D)] * 5 + [_vec_spec(D)] * 3,
        out_specs=[_row_spec(tr, D), _row_spec(tr, D), _acc_spec(D)],
        out_shape=[jax.ShapeDtypeStruct((R, D), F32), jax.ShapeDtypeStruct((R, D), BF16), jax.ShapeDtypeStruct((8, D), F32)],
        compiler_params=_cparams(("arbitrary",)),
    )(du_a, du_b, h2, dh3, mix, sc2, g1, w2)


def _norm1_bwd(dh1, xall, dh2, w1, mod, n_lat_tiles, tr):
    R = xall.shape[0]

    def body(dh1_ref, x_ref, dh2_ref, w_ref, mod_ref, dx_ref, acc_ref):
        i = pl.program_id(0)

        @pl.when((i == 0) | (i == n_lat_tiles))
        def _():
            acc_ref[...] = jnp.zeros_like(acc_ref)

        dh1 = dh1_ref[...]
        x = x_ref[...]
        rstd = _rms(x)
        n = x * rstd
        dnw = dh1 * (1.0 + mod_ref[0, 1])
        dx_ref[...] = dh2_ref[...] + _rms_bwd(dnw * w_ref[...], n, rstd)
        acc_ref[0, 0:1, :] += _colsum(dh1 * n * w_ref[...])
        acc_ref[0, 1:2, :] += _colsum(dh1)
        acc_ref[0, 2:3, :] += _colsum(dnw * n)

    sel = lambda i: jnp.where(i >= n_lat_tiles, 1, 0)
    return pl.pallas_call(
        body, name="norm1_bwd", grid=(R // tr,),
        in_specs=[_row_spec(tr, D)] * 3 + [_vec_spec(D), pl.BlockSpec((1, 2, 1, D), lambda i: (sel(i), 0, 0, 0))],
        out_specs=[_row_spec(tr, D), pl.BlockSpec((1, 8, D), lambda i: (sel(i), 0, 0))],
        out_shape=[jax.ShapeDtypeStruct((R, D), F32), jax.ShapeDtypeStruct((2, 8, D), F32)],
        compiler_params=_cparams(("arbitrary",)),
    )(dh1, xall, dh2, w1, mod)


def _swiglu_act(gp, up, tr):
    R = gp.shape[0]

    def body(g_ref, u_ref, o_ref):
        o_ref[...] = (_silu(_f32(g_ref)) * _f32(u_ref)).astype(BF16)

    return pl.pallas_call(
        body, name="swiglu_act", grid=(R // tr,), in_specs=[_row_spec(tr, D_FF)] * 2, out_specs=_row_spec(tr, D_FF),
        out_shape=jax.ShapeDtypeStruct((R, D_FF), BF16), compiler_params=_cparams(("arbitrary",)),
    )(gp, up)


def _swiglu_act_bwd(da, gp, up, tr):
    R = gp.shape[0]

    def body(da_ref, g_ref, u_ref, dg_ref, du_ref):
        da, g = _f32(da_ref), _f32(g_ref)
        dg_ref[...] = (da * _f32(u_ref) * _dsilu(g)).astype(BF16)
        du_ref[...] = (da * _silu(g)).astype(BF16)

    return pl.pallas_call(
        body, name="swiglu_act_bwd", grid=(R // tr,), in_specs=[_row_spec(tr, D_FF)] * 3, out_specs=[_row_spec(tr, D_FF)] * 2,
        out_shape=[jax.ShapeDtypeStruct((R, D_FF), BF16)] * 2, compiler_params=_cparams(("arbitrary",)),
    )(da, gp, up)


def _merge(ya, yb, parts, tr):
    R = ya.shape[0]

    def body(ya_ref, yb_ref, ga_ref, gb_ref, o_ref):
        o_ref[...] = (_sigmoid(_f32(ga_ref)) * _f32(ya_ref) + _sigmoid(_f32(gb_ref)) * _f32(yb_ref)).astype(BF16)

    return pl.pallas_call(
        body, name="merge", grid=(R // tr,),
        in_specs=[_row_spec(tr, D), _row_spec(tr, D), _row_spec(tr, D, COL_GA // D), _row_spec(tr, D, COL_GB // D)],
        out_specs=_row_spec(tr, D), out_shape=jax.ShapeDtypeStruct((R, D), BF16), compiler_params=_cparams(("arbitrary",)),
    )(ya, yb, parts, parts)


def _dparts_out(tr, w, col, nd=1):
    blk = col // w
    return pl.BlockSpec((tr, w), (lambda i: (i, blk)) if nd == 1 else (lambda i, j: (i, blk + j)))


def _merge_bwd(dm, ya, yb, parts, dparts, tr):
    R = ya.shape[0]

    def body(dm_ref, ya_ref, yb_ref, ga_ref, gb_ref, _, dya_ref, dyb_ref, dg_ref):
        dm = _f32(dm_ref)
        sa, sb = _sigmoid(_f32(ga_ref)), _sigmoid(_f32(gb_ref))
        dya_ref[...] = (dm * sa).astype(BF16)
        dyb_ref[...] = (dm * sb).astype(BF16)
        dg_ref[:, 0:D] = (dm * _f32(ya_ref) * sa * (1.0 - sa)).astype(BF16)
        dg_ref[:, D:2 * D] = (dm * _f32(yb_ref) * sb * (1.0 - sb)).astype(BF16)

    return pl.pallas_call(
        body, name="merge_bwd", grid=(R // tr,),
        in_specs=[_row_spec(tr, D)] * 3 + [_row_spec(tr, D, COL_GA // D), _row_spec(tr, D, COL_GB // D), _ANY],
        out_specs=[_row_spec(tr, D), _row_spec(tr, D), _dparts_out(tr, 2 * D, COL_GA)],
        out_shape=[jax.ShapeDtypeStruct((R, D), BF16)] * 2 + [jax.ShapeDtypeStruct(dparts.shape, BF16)],
        input_output_aliases={5: 2}, compiler_params=_cparams(("arbitrary",)),
    )(dm, ya, yb, parts, parts, dparts)


def _gla_out(o2, parts, gw4, tr):
    R = parts.shape[0]

    def body(of_ref, ob_ref, r_ref, w_ref, out_ref):
        oa = _f32(of_ref) + _f32(ob_ref)
        sr = _silu(_f32(r_ref))
        for h in range(GLA_H):
            s = slice(h * GLA_DV, (h + 1) * GLA_DV)
            o = oa[:, s]
            out_ref[:, s] = (o * _rms(o) * w_ref[:, s] * sr[:, s]).astype(BF16)

    return pl.pallas_call(
        body, name="gla_out", grid=(R // tr,),
        in_specs=[_row_spec(tr, D), _row_spec(tr, D), _row_spec(tr, D, COL_R // D), _vec_spec(D)],
        out_specs=_row_spec(tr, D), out_shape=jax.ShapeDtypeStruct((R, D), BF16), compiler_params=_cparams(("arbitrary",)),
    )(o2[0], o2[1], parts, gw4)


def _gla_out_bwd(dout, o2, parts, gw4, dparts, tr):
    R = parts.shape[0]

    def body(d_ref, of_ref, ob_ref, r_ref, w_ref, _, do_ref, dr_ref, acc_ref):
        _zero_first(acc_ref)
        oa = _f32(of_ref) + _f32(ob_ref)
        r = _f32(r_ref)
        sr = _silu(r)
        dout = _f32(d_ref)
        for h in range(GLA_H):
            s = slice(h * GLA_DV, (h + 1) * GLA_DV)
            o = oa[:, s]
            rstd = _rms(o)
            n = o * rstd
            w = w_ref[:, s]
            dr_ref[:, s] = (dout[:, s] * n * w * _dsilu(r[:, s])).astype(BF16)
            dnw = dout[:, s] * sr[:, s]
            do_ref[:, s] = _rms_bwd(dnw * w, n, rstd).astype(ACT)
            acc_ref[0:1, s] += _colsum(dnw * n)

    return pl.pallas_call(
        body, name="gla_out_bwd", grid=(R // tr,),
        in_specs=[_row_spec(tr, D), _row_spec(tr, D), _row_spec(tr, D), _row_spec(tr, D, COL_R // D), _vec_spec(D), _ANY],
        out_specs=[_row_spec(tr, D), _dparts_out(tr, D, COL_R), _acc_spec(D)],
        out_shape=[jax.ShapeDtypeStruct((R, D), ACT), jax.ShapeDtypeStruct(dparts.shape, BF16), jax.ShapeDtypeStruct((8, D), F32)],
        input_output_aliases={5: 1}, compiler_params=_cparams(("arbitrary",)),
    )(dout, o2[0], o2[1], parts, gw4, dparts)


SSM_GW = SSM_INNER // SSM_G


def _ssd_out(y2, xbc, parts, dskip, nw, tr):
    R = parts.shape[0]

    def body(yf_ref, yb_ref, x_ref, z_ref, ds_ref, w_ref, out_ref):
        ob = (_f32(yf_ref) + _f32(yb_ref) + ds_ref[...] * _f32(x_ref)) * _silu(_f32(z_ref))
        for g in range(SSM_G):
            s = slice(g * SSM_GW, (g + 1) * SSM_GW)
            o = ob[:, s]
            out_ref[:, s] = (o * _rms(o) * w_ref[:, s]).astype(BF16)

    return pl.pallas_call(
        body, name="ssd_out", grid=(R // tr,),
        in_specs=[_row_spec(tr, SSM_INNER)] * 3 + [_row_spec(tr, SSM_INNER, COL_Z // SSM_INNER),
                                                   _vec_spec(SSM_INNER), _vec_spec(SSM_INNER)],
        out_specs=_row_spec(tr, SSM_INNER), out_shape=jax.ShapeDtypeStruct((R, SSM_INNER), BF16),
        compiler_params=_cparams(("arbitrary",)),
    )(y2[0], y2[1], xbc, parts, dskip, nw)


def _ssd_out_bwd(dout, y2, xbc, parts, dskip, nw, dparts, tr):
    R = parts.shape[0]

    def body(d_ref, yf_ref, yb_ref, x_ref, z_ref, ds_ref, w_ref, _, dy_ref, dz_ref, acc_ref):
        _zero_first(acc_ref)
        x, z = _f32(x_ref), _f32(z_ref)
        pre = _f32(yf_ref) + _f32(yb_ref) + ds_ref[...] * x
        sz = _silu(z)
        ob = pre * sz
        dout = _f32(d_ref)
        for g in range(SSM_G):
            s = slice(g * SSM_GW, (g + 1) * SSM_GW)
            o = ob[:, s]
            rstd = _rms(o)
            n = o * rstd
            dob = _rms_bwd(dout[:, s] * w_ref[:, s], n, rstd)
            dz_ref[:, s] = (dob * pre[:, s] * _dsilu(z[:, s])).astype(BF16)
            dy = dob * sz[:, s]
            dy_ref[:, s] = dy.astype(ACT)
            acc_ref[0:1, s] += _colsum(dout[:, s] * n)
            acc_ref[1:2, s] += _colsum(dy * x[:, s])

    return pl.pallas_call(
        body, name="ssd_out_bwd", grid=(R // tr,),
        in_specs=[_row_spec(tr, SSM_INNER)] * 4 + [_row_spec(tr, SSM_INNER, COL_Z // SSM_INNER),
                                                   _vec_spec(SSM_INNER), _vec_spec(SSM_INNER), _ANY],
        out_specs=[_row_spec(tr, SSM_INNER), _dparts_out(tr, SSM_INNER, COL_Z), _acc_spec(SSM_INNER)],
        out_shape=[jax.ShapeDtypeStruct((R, SSM_INNER), ACT), jax.ShapeDtypeStruct(dparts.shape, BF16),
                   jax.ShapeDtypeStruct((8, SSM_INNER), F32)],
        input_output_aliases={7: 1}, compiler_params=_cparams(("arbitrary",)),
    )(dout, y2[0], y2[1], xbc, parts, dskip, nw, dparts)


CONV_W = SSM_INNER + 2 * SSM_G * SSM_N
CONV_BLK = 1024


CONV_SHIFTS = (-2, -1, 1, 2)


def _conv_mask_table(tr):
    t = np.arange(tr)
    table = np.zeros((2, len(CONV_SHIFTS), tr, 128), np.float32)
    for kind, (pos, seg) in enumerate(((t % GRID_W, GRID_W), (t, tr))):
        for k, s in enumerate(CONV_SHIFTS):
            table[kind, k] = ((pos + s >= 0) & (pos + s < seg)).astype(np.float32)[:, None]
    return jnp.asarray(table)


def _shifted(u, s, mask_ref, tr):
    return u if s == 0 else pltpu.roll(u, (-s) % tr, 0) * mask_ref[0, CONV_SHIFTS.index(s)]


def _conv_mask_spec(tr, n_lat_tiles, row_axis):
    return pl.BlockSpec((1, len(CONV_SHIFTS), tr, 128),
                        lambda *ids: (jnp.where(ids[row_axis] >= n_lat_tiles, 1, 0), 0, 0, 0))


def _conv_fwd(parts, cw, cb, n_lat_tiles, tr):
    R = parts.shape[0]

    def body(u_ref, w_ref, b_ref, mask_ref, o_ref):
        def lanes(l, carry):
            sl = pl.ds(pl.multiple_of(l * 128, 128), 128)
            u, w = u_ref[:, sl].astype(F32), w_ref[:, sl]
            acc = jnp.zeros_like(u) + b_ref[:, sl]
            for j in range(SSM_CONV):
                acc = acc + _shifted(u, j - CONV_LEFT, mask_ref, tr) * w[j:j + 1, :]
            o_ref[:, sl] = _silu(acc).astype(ACT)
            return carry

        lax.fori_loop(0, CONV_BLK // 128, lanes, 0)

    return pl.pallas_call(
        body, name="conv_fwd", grid=(R // tr, CONV_W // CONV_BLK),
        in_specs=[pl.BlockSpec((tr, CONV_BLK), lambda i, j: (i, COL_XBC // CONV_BLK + j)),
                  pl.BlockSpec((SSM_CONV, CONV_BLK), lambda i, j: (0, j)), pl.BlockSpec((1, CONV_BLK), lambda i, j: (0, j)),
                  _conv_mask_spec(tr, n_lat_tiles, 0)],
        out_specs=pl.BlockSpec((tr, CONV_BLK), lambda i, j: (i, j)), out_shape=jax.ShapeDtypeStruct((R, CONV_W), ACT),
        compiler_params=_cparams(("arbitrary", "arbitrary")),
    )(parts, cw, cb, _conv_mask_table(tr))


def _conv_bwd(dx, db, dc, dy, dskip, parts, cw, cb, dparts, n_lat_tiles, tr):
    R = parts.shape[0]
    half = CONV_BLK // 2
    n_x = SSM_INNER // CONV_BLK

    def body(dxf_ref, dxb_ref, dy_ref, ds_ref, dbf_ref, dbb_ref, dcf_ref, dcb_ref, u_ref, w_ref, b_ref, _, mask_ref,
             du_ref, acc_ref, d_scr):
        @pl.when(pl.program_id(1) == 0)
        def _():
            acc_ref[...] = jnp.zeros_like(acc_ref)

        @pl.when(pl.program_id(0) < n_x)
        def _():
            d_scr[...] = dxf_ref[...] + dxb_ref[...] + _f32(dy_ref) * ds_ref[...]

        @pl.when(pl.program_id(0) >= n_x)
        def _():
            d_scr[:, 0:half] = dbf_ref[...] + dbb_ref[...]
            d_scr[:, half:] = dcf_ref[...] + dcb_ref[...]

        def lanes(l, carry):
            sl = pl.ds(pl.multiple_of(l * 128, 128), 128)
            u, w = u_ref[:, sl].astype(F32), w_ref[:, sl]
            pre = jnp.zeros_like(u) + b_ref[:, sl]
            taps = []
            for j in range(SSM_CONV):
                tap = _shifted(u, j - CONV_LEFT, mask_ref, tr)
                taps.append(tap)
                pre = pre + tap * w[j:j + 1, :]
            dpre = d_scr[:, sl] * _dsilu(pre)
            du = jnp.zeros_like(u)
            sums = []
            for j in range(SSM_CONV):
                sums.append(_colsum(dpre * taps[j]))
                du = du + _shifted(dpre, CONV_LEFT - j, mask_ref, tr) * w[j:j + 1, :]
            sums += [_colsum(dpre), jnp.zeros((8 - SSM_CONV - 1, 128), F32)]
            acc_ref[:, sl] += jnp.concatenate(sums, axis=0)
            du_ref[:, sl] = du.astype(BF16)
            return carry

        lax.fori_loop(0, CONV_BLK // 128, lanes, 0)

    return pl.pallas_call(
        body, name="conv_bwd", grid=(CONV_W // CONV_BLK, R // tr),
        in_specs=[pl.BlockSpec((tr, CONV_BLK), lambda j, i: (jnp.where(j < n_x, i, 0), jnp.minimum(j, n_x - 1)))] * 3
        + [pl.BlockSpec((1, CONV_BLK), lambda j, i: (0, jnp.minimum(j, n_x - 1)))]
        + [pl.BlockSpec((tr, half), lambda j, i: (jnp.where(j < n_x, 0, i), 0))] * 4
        + [pl.BlockSpec((tr, CONV_BLK), lambda j, i: (i, COL_XBC // CONV_BLK + j)),
           pl.BlockSpec((SSM_CONV, CONV_BLK), lambda j, i: (0, j)), pl.BlockSpec((1, CONV_BLK), lambda j, i: (0, j)), _ANY,
           _conv_mask_spec(tr, n_lat_tiles, 1)],
        out_specs=[pl.BlockSpec((tr, CONV_BLK), lambda j, i: (i, COL_XBC // CONV_BLK + j)),
                   pl.BlockSpec((8, CONV_BLK), lambda j, i: (0, j))],
        out_shape=[jax.ShapeDtypeStruct(dparts.shape, BF16), jax.ShapeDtypeStruct((8, CONV_W), F32)],
        scratch_shapes=[pltpu.VMEM((tr, CONV_BLK), F32)],
        input_output_aliases={11: 0}, compiler_params=_cparams(("arbitrary", "arbitrary")),
    )(*dx, dy, dskip, *db, *dc, parts, cw, cb, dparts, _conv_mask_table(tr))


def _chunk_row_block(d, i, n_lat, n_ctx):
    fwd = jnp.where(i < n_ctx, n_lat + i, i - n_ctx)
    rev = n_lat + n_ctx - 1 - i
    if isinstance(d, int):
        return rev if d else fwd
    return jnp.where(d == 0, fwd, rev)


def _tri(n, d, transpose=False):
    row = lax.broadcasted_iota(jnp.int32, (n, n), 0)
    col = lax.broadcasted_iota(jnp.int32, (n, n), 1)
    diff = (col - row) if transpose else (row - col)
    return diff * (1 - 2 * d) >= 0


def _gla_gates(sm, uhi, ulo, bias, d):
    pre = _nn3(sm, uhi, ulo) + bias
    g = _log_sigmoid(pre) * (1.0 / GLA_TAU)
    mask = _tri(GLA_C, d)
    b = _x_nn(mask.astype(BF16), g)
    b_tot = _colsum(g)
    b_ref = b[GLA_C // 2:GLA_C // 2 + 1, :]
    e_q = jnp.exp(jnp.minimum(b - b_ref, EXP_CLAMP))
    e_k = jnp.exp(jnp.minimum(b_ref - b, EXP_CLAMP))
    return pre, mask, b_tot, e_q, e_k, jnp.exp(b), jnp.exp(b_tot - b)


GLA_QK = GLA_H * GLA_DK
GLA_V = GLA_H * GLA_DV


def _gla_specs(n_lat, n_ctx, step_of):
    rbs = [lambda i, d=d: _chunk_row_block(d, step_of(i), n_lat, n_ctx) for d in range(2)]
    specs = []
    for rb in rbs:
        specs += [pl.BlockSpec((GLA_C, GLA_QK), lambda i, rb=rb: (rb(i), COL_Q // GLA_QK)),
                  pl.BlockSpec((GLA_C, GLA_QK), lambda i, rb=rb: (rb(i), COL_K // GLA_QK)),
                  pl.BlockSpec((GLA_C, GLA_V), lambda i, rb=rb: (rb(i), COL_V // GLA_V)),
                  pl.BlockSpec((GLA_C, 128), lambda i, rb=rb: (rb(i), 0))]
    specs += [pl.BlockSpec((2, 128, GLA_QK), lambda i: (0, 0, 0)), pl.BlockSpec((2, 128, GLA_QK), lambda i: (0, 0, 0)),
              pl.BlockSpec((2, 1, GLA_QK), lambda i: (0, 0, 0))]
    return specs, rbs


ACT = BF16


def _f32(ref_or_value):
    return ref_or_value[...].astype(F32)


def _gla_fwd(parts, sm, uhi, ulo, bias, n_lat, n_ctx):
    R = parts.shape[0]
    n_steps = n_lat + n_ctx
    scale = GLA_DK ** -0.5

    def body(*refs):
        ins, (uhi_ref, ulo_ref, bias_ref), o_refs, hist_ref, st = refs[:8], refs[8:11], refs[11:13], refs[13], refs[14]

        @pl.when(pl.program_id(0) == 0)
        def _():
            st[...] = jnp.zeros_like(st)

        for d in range(2):
            q_ref, k_ref, v_ref, sm_ref = ins[4 * d:4 * d + 4]
            _, mask, b_tot, e_q, e_k, e_in, e_out = _gla_gates(sm_ref[...], uhi_ref[d], ulo_ref[d], bias_ref[d], d)
            q, k, v = _f32(q_ref) * scale, _f32(k_ref), _bf(v_ref[...])
            qb, kb, q_in, k_out, decay = _bf(q * e_q), _bf(k * e_k), _bf(q * e_in), _bf(k * e_out), jnp.exp(b_tot)
            for h in range(GLA_H):
                sk, sv = slice(h * GLA_DK, (h + 1) * GLA_DK), slice(h * GLA_DV, (h + 1) * GLA_DV)
                att = jnp.where(mask, _nt(qb[:, sk], kb[:, sk]), 0.0)
                s_in = st[d, h]
                hist_ref[d, 0, h] = s_in
                o_refs[d][:, sv] = (_nn(_bf(att), v[:, sv]) + _nt(q_in[:, sk], _bf(s_in))).astype(ACT)
                st[d, h] = decay[:, sk] * s_in + _tn(v[:, sv], k_out[:, sk])

    in_specs, rbs = _gla_specs(n_lat, n_ctx, lambda i: i)
    return pl.pallas_call(
        body, name="gla_fwd", grid=(n_steps,), in_specs=in_specs,
        out_specs=[pl.BlockSpec((GLA_C, GLA_V), lambda i, rb=rb: (rb(i), 0)) for rb in rbs]
        + [pl.BlockSpec((2, 1, GLA_H, GLA_DV, GLA_DK), lambda i: (0, i, 0, 0, 0))],
        out_shape=[jax.ShapeDtypeStruct((R, GLA_V), ACT)] * 2 + [jax.ShapeDtypeStruct((2, n_steps, GLA_H, GLA_DV, GLA_DK), F32)],
        scratch_shapes=[pltpu.VMEM((2, GLA_H, GLA_DV, GLA_DK), F32)],
        compiler_params=_cparams(("arbitrary",)),
    )(*([parts, parts, parts, sm] * 2), uhi, ulo, bias)


def _gla_bwd(do, parts, sm, uhi, ulo, bias, hist, n_lat, n_ctx):
    R = parts.shape[0]
    n_steps = n_lat + n_ctx
    scale = GLA_DK ** -0.5
    step_of = lambda j: n_steps - 1 - j

    def body(*refs):
        ins, (uhi_ref, ulo_ref, bias_ref), do_refs, hist_ref = refs[:8], refs[8:11], refs[11:13], refs[13]
        outs, dst = refs[14:22], refs[22]

        @pl.when(pl.program_id(0) == 0)
        def _():
            dst[...] = jnp.zeros_like(dst)

        for d in range(2):
            q_ref, k_ref, v_ref, sm_ref = ins[4 * d:4 * d + 4]
            dq_ref, dk_ref, dv_ref, dp_ref = outs[4 * d:4 * d + 4]
            pre, mask, b_tot, e_q, e_k, e_in, e_out = _gla_gates(sm_ref[...], uhi_ref[d], ulo_ref[d], bias_ref[d], d)
            q, k, v = _f32(q_ref) * scale, _f32(k_ref), _bf(v_ref[...])
            dout = _bf(do_refs[d][...])
            k_out_f = k * e_out
            qb, kb, q_in, k_out, decay = _bf(q * e_q), _bf(k * e_k), _bf(q * e_in), _bf(k_out_f), jnp.exp(b_tot)
            dqs, dks, dk_outs, dss = [], [], [], []
            for h in range(GLA_H):
                sk, sv = slice(h * GLA_DK, (h + 1) * GLA_DK), slice(h * GLA_DV, (h + 1) * GLA_DV)
                s_in, ds = hist_ref[d, 0, h], dst[d, h]
                att = jnp.where(mask, _nt(qb[:, sk], kb[:, sk]), 0.0)
                datt = _bf(jnp.where(mask, _nt(dout[:, sv], v[:, sv]), 0.0))
                dv_ref[:, sv] = _tn(_bf(att), dout[:, sv]) + _nt(k_out[:, sk], _bf(ds))
                dqs.append(_nn(datt, kb[:, sk]) * e_q[:, sk] + _nn(dout[:, sv], _bf(s_in)) * e_in[:, sk])
                dk_o = _nn(v[:, sv], _bf(ds))
                dk_outs.append(dk_o)
                dks.append(_tn(datt, qb[:, sk]) * e_k[:, sk])
                dss.append(_colsum(ds * s_in))
                dst[d, h] = decay[:, sk] * ds + _tn(dout[:, sv], q_in[:, sk])
            dq, dk_out = jnp.concatenate(dqs, axis=1), jnp.concatenate(dk_outs, axis=1)
            dk = jnp.concatenate(dks, axis=1) + dk_out * e_out
            dq_ref[...] = dq * scale
            dk_ref[...] = dk
            db_tot = _colsum(dk_out * k_out_f) + decay * jnp.concatenate(dss, axis=1)
            dg = _x_nn(_tri(GLA_C, d, transpose=True).astype(BF16), dq * q - dk * k) + db_tot
            dp_ref[...] = dg * (1.0 / GLA_TAU) * _sigmoid(-pre)

    in_specs, rbs = _gla_specs(n_lat, n_ctx, step_of)
    in_specs += [pl.BlockSpec((GLA_C, GLA_V), lambda j, rb=rb: (rb(j), 0)) for rb in rbs]
    in_specs += [pl.BlockSpec((2, 1, GLA_H, GLA_DV, GLA_DK), lambda j: (0, step_of(j), 0, 0, 0))]
    out_specs, out_shape = [], []
    for rb in rbs:
        for w in (GLA_QK, GLA_QK, GLA_V, GLA_QK):
            out_specs.append(pl.BlockSpec((GLA_C, w), lambda j, rb=rb: (rb(j), 0)))
            out_shape.append(jax.ShapeDtypeStruct((R, w), F32))
    outs = pl.pallas_call(
        body, name="gla_bwd", grid=(n_steps,), in_specs=in_specs, out_specs=out_specs, out_shape=out_shape,
        scratch_shapes=[pltpu.VMEM((2, GLA_H, GLA_DV, GLA_DK), F32)],
        compiler_params=_cparams(("arbitrary",)),
    )(*([parts, parts, parts, sm] * 2), uhi, ulo, bias, do, do, hist)
    return [(outs[k], outs[4 + k]) for k in range(4)]


def _ssd_consts(dt_bias, a_log):
    sel = np.zeros((2, SSM_G, 128, 128), np.float32)
    for d, base in enumerate((SM_DTF, SM_DTB)):
        for g in range(SSM_G):
            for e in range(SSM_HPG):
                sel[d, g, base + SSM_HPG * g + e, e] = 1.0
    e512 = np.zeros((128, SSM_GW), np.float32)
    for e in range(SSM_HPG):
        e512[e, SSM_P * e:SSM_P * (e + 1)] = 1.0
    a_neg = -jnp.exp(a_log)
    pad = lambda v: jnp.pad(v.reshape(2, SSM_G, 1, SSM_HPG), ((0, 0), (0, 0), (0, 0), (0, 128 - SSM_HPG)))
    return dict(
        sel=jnp.asarray(sel, BF16), sel_t=jnp.asarray(sel.transpose(0, 1, 3, 2), BF16), e512_t=jnp.asarray(e512.T, BF16),
        dtb=pad(dt_bias), a=pad(a_neg), a512=jnp.repeat(a_neg, SSM_P, axis=1).reshape(2, SSM_G, 1, SSM_GW))


def _head_columns(x8):
    return [jnp.broadcast_to(x8[:, e:e + 1], (x8.shape[0], 128)) for e in range(SSM_HPG)]


def _head_layout(cols):
    low = lax.broadcasted_iota(jnp.int32, (1, 128), 1) < SSM_P
    return jnp.concatenate([jnp.where(low, cols[2 * j], cols[2 * j + 1]) for j in range(SSM_HPG // 2)], axis=1)


def _ssd_common(sm, sel, dtb, a_neg, a512, d):
    dtr8 = _nn_x(sm, sel) + dtb
    dt8 = _softplus(dtr8)
    a8 = a_neg * dt8
    mask = _tri(SSM_C, d)
    mask_t = _tri(SSM_C, d, transpose=True).astype(BF16)
    cum8 = _x_nn(mask.astype(BF16), a8)
    a_hi, a_lo = _split(a8)
    cum_t = _tn(a_hi, mask_t) + _tn(a_lo, mask_t)
    cum_cols = _head_columns(cum8)
    dt_exp = _head_layout(_head_columns(dt8))
    a_exp = a512 * dt_exp
    return dict(dtr8=dtr8, a8=a8, mask=mask, mask_t=mask_t, cum_t=cum_t, dt_exp=dt_exp, a_exp=a_exp,
                cum_exp=_head_layout(cum_cols), cum_cols=cum_cols, tot_exp=_colsum(a_exp))


def _ssd_decay(cm, e):
    diff = cm["cum_cols"][e] - cm["cum_t"][e:e + 1, :]
    return jnp.where(cm["mask"], jnp.exp(jnp.minimum(diff, 0.0)), 0.0)


SSM_GPS = 2


def _ssd_specs(n_lat, n_ctx, step_of):
    rbs = [lambda i, d=d: _chunk_row_block(d, step_of(i), n_lat, n_ctx) for d in range(2)]
    xw, nw = SSM_GPS * SSM_GW, SSM_GPS * SSM_N
    specs = []
    for rb in rbs:
        specs += [pl.BlockSpec((SSM_C, xw), lambda g, i, rb=rb: (rb(i), g)),
                  pl.BlockSpec((SSM_C, nw), lambda g, i, rb=rb: (rb(i), SSM_INNER // nw + g)),
                  pl.BlockSpec((SSM_C, nw), lambda g, i, rb=rb: (rb(i), (SSM_INNER + SSM_G * SSM_N) // nw + g)),
                  pl.BlockSpec((SSM_C, 128), lambda g, i, rb=rb: (rb(i), 0))]
    specs += [pl.BlockSpec((2, SSM_GPS, 128, 128), lambda g, i: (0, g, 0, 0)),
              pl.BlockSpec((2, SSM_GPS, 1, 128), lambda g, i: (0, g, 0, 0)),
              pl.BlockSpec((2, SSM_GPS, 1, 128), lambda g, i: (0, g, 0, 0)),
              pl.BlockSpec((2, SSM_GPS, 1, SSM_GW), lambda g, i: (0, g, 0, 0))]
    return specs, rbs


def _ssd_fwd(xbc, sm, k, n_lat, n_ctx, comm):
    R = xbc.shape[0]
    n_steps = n_lat + n_ctx

    def body(*refs):
        ins, (sel_ref, dtb_ref, a_ref, a512_ref), y_refs, hist_ref, st = refs[:8], refs[8:12], refs[12:14], refs[14], refs[15]

        @pl.when(pl.program_id(1) == 0)
        def _():
            st[...] = jnp.zeros_like(st)

        for d in range(2):
            x_ref, b_ref, c_ref, sm_ref = ins[4 * d:4 * d + 4]
            sm = sm_ref[...]
            for gg in range(SSM_GPS):
                sx, sn = slice(gg * SSM_GW, (gg + 1) * SSM_GW), slice(gg * SSM_N, (gg + 1) * SSM_N)
                cm = _ssd_common(sm, sel_ref[d, gg], dtb_ref[d, gg], a_ref[d, gg], a512_ref[d, gg], d)
                bm, cmat = _bf(b_ref[:, sn]), _bf(c_ref[:, sn])
                xdt = x_ref[:, sx].astype(F32) * cm["dt_exp"]
                cb = _nt(cmat, bm)
                ys = [_nn(_bf(cb * _ssd_decay(cm, e)), _bf(xdt[:, SSM_P * e:SSM_P * (e + 1)])) for e in range(SSM_HPG)]
                s_in = st[d, gg]
                hist_ref[d, 0, gg] = s_in
                y = jnp.concatenate(ys, axis=1) + jnp.exp(cm["cum_exp"]) * _nn(cmat, _bf(s_in))
                y_refs[d][:, sx] = y.astype(ACT)
                st[d, gg] = jnp.exp(cm["tot_exp"]) * s_in + _tn(bm, _bf(xdt * jnp.exp(cm["tot_exp"] - cm["cum_exp"])))

    in_specs, rbs = _ssd_specs(n_lat, n_ctx, lambda i: i)
    out_specs = [pl.BlockSpec((SSM_C, SSM_GPS * SSM_GW), lambda g, i, rb=rb: (rb(i), g)) for rb in rbs]
    out_specs += [pl.BlockSpec((2, 1, SSM_GPS, SSM_N, SSM_GW), lambda g, i: (0, i, g, 0, 0))]
    out_shape = [jax.ShapeDtypeStruct((R, SSM_INNER), ACT)] * 2 + [jax.ShapeDtypeStruct((2, n_steps, SSM_G, SSM_N, SSM_GW), F32)]
    args = [xbc, xbc, xbc, sm] * 2 + [k["sel"], k["dtb"], k["a"], k["a512"]]
    n_host_out = len(out_shape)
    outs = pl.pallas_call(
        _hosted(body, len(args), n_host_out, 1, comm, *_ssd_comm_steps(n_steps)), name="ssd_fwd",
        grid=(SSM_G // SSM_GPS, n_steps), in_specs=in_specs + [_ANY] * len(comm.arrays),
        out_specs=out_specs + [_ANY] * len(comm.out_shape), out_shape=out_shape + comm.out_shape,
        scratch_shapes=[pltpu.VMEM((2, SSM_GPS, SSM_N, SSM_GW), F32)] + comm.scratch,
        compiler_params=_cparams(("arbitrary", "arbitrary")),
    )(*args, *comm.arrays)
    return outs[:n_host_out], outs[n_host_out:]


def _ssd_comm_steps(n_steps):
    n_g = SSM_G // SSM_GPS
    at = lambda g, i: (pl.program_id(0) == g) & (pl.program_id(1) == i)
    return (lambda: at(0, 0)), (lambda: at(n_g // 2, 0)), (lambda: at(n_g - 1, n_steps - 1))


def _ssd_bwd(dy, xbc, sm, k, hist, n_lat, n_ctx, comm):
    R = xbc.shape[0]
    n_steps = n_lat + n_ctx
    step_of = lambda j: n_steps - 1 - j

    def one(d, gg, x_ref, b_ref, c_ref, sm_ref, sel_ref, dtb_ref, a_ref, a512_ref, selt_ref, e512t_ref, dy_ref,
            hist_ref, dx_ref, db_ref, dc_ref, dsm_ref, acc_ref, dst):
        sx, sn = slice(gg * SSM_GW, (gg + 1) * SSM_GW), slice(gg * SSM_N, (gg + 1) * SSM_N)
        a_neg, e512_t = a_ref[d, gg], e512t_ref[...]
        cm = _ssd_common(sm_ref[...], sel_ref[d, gg], dtb_ref[d, gg], a_neg, a512_ref[d, gg], d)
        x, dyv = x_ref[:, sx].astype(F32), dy_ref[:, sx].astype(F32)
        bm, cmat = _bf(b_ref[:, sn]), _bf(c_ref[:, sn])
        xdt = x * cm["dt_exp"]
        cb = _nt(cmat, bm)
        s_in, ds = hist_ref[d, 0, gg], dst[d, gg]
        w = jnp.exp(cm["tot_exp"] - cm["cum_exp"])
        z = _nn(bm, _bf(ds))
        decay_in = jnp.exp(cm["cum_exp"])
        gy = _bf(dyv * decay_in)
        dcb = jnp.zeros((SSM_C, SSM_C), F32)
        dxs, crossing = [], []
        row = lax.broadcasted_iota(jnp.int32, (SSM_C, SSM_C), 0)
        col = lax.broadcasted_iota(jnp.int32, (SSM_C, SSM_C), 1)
        eye = (row == col).astype(BF16)
        before = (cm["mask_t"] - eye)
        for e in range(SSM_HPG):
            s = slice(SSM_P * e, SSM_P * (e + 1))
            lm = _ssd_decay(cm, e)
            dy_e = _bf(dyv[:, s])
            m_e = cb * lm
            dm_e = _nt(dy_e, _bf(xdt[:, s]))
            dcb = dcb + dm_e * lm
            dxs.append(_tn(_bf(m_e), dy_e))
            through = jnp.where(cm["mask"], _nn(_bf(dm_e * m_e), before), 0.0)
            crossing.append(_colsum(through))
        da_rows = jnp.concatenate(crossing + [jnp.zeros((128 - SSM_HPG, SSM_C), F32)], axis=0)
        r_hi, r_lo = _split(da_rows)
        da8_intra = _tn(r_hi, eye) + _tn(r_lo, eye)
        dx_state = w * z
        dxdt = jnp.concatenate(dxs, axis=1) + dx_state
        dcb = _bf(dcb)
        c_s = _nn(cmat, _bf(s_in))
        dc_ref[:, sn] = _nn(dcb, bm) + _nt(gy, _bf(s_in))
        db_ref[:, sn] = _tn(dcb, cmat) + _nt(_bf(w * xdt), _bf(ds))
        dst[d, gg] = jnp.exp(cm["tot_exp"]) * ds + _tn(cmat, gy)
        state_path = xdt * dx_state
        per_token = _nn_x(jnp.concatenate([dyv * decay_in * c_s - state_path, dxdt * x], axis=0), e512_t)
        totals = jnp.concatenate([_colsum(state_path), _colsum(ds * s_in), jnp.zeros((6, SSM_GW), F32)], axis=0)
        totals = _nn_x(totals, e512_t)
        tot8 = _colsum(cm["a8"])
        dtot8 = totals[0:1] + jnp.exp(tot8) * totals[1:2]
        da8 = da8_intra + _x_nn(cm["mask_t"], per_token[:SSM_C]) + dtot8
        ddt8 = da8 * a_neg + per_token[SSM_C:]
        dsm_ref[gg] = _nn_x(ddt8 * _sigmoid(cm["dtr8"]), selt_ref[d, gg])
        dx_ref[:, sx] = dxdt * cm["dt_exp"]
        acc_ref[d, gg, 0:1, :] += _colsum(da8 * cm["a8"])

    def body(*refs):
        ins, consts, (selt_ref, e512t_ref), dy_refs, hist_ref = refs[:8], refs[8:12], refs[12:14], refs[14:16], refs[16]
        outs, acc_ref, dst = refs[17:25], refs[25], refs[26]

        @pl.when(pl.program_id(1) == 0)
        def _():
            dst[...] = jnp.zeros_like(dst)
            acc_ref[...] = jnp.zeros_like(acc_ref)

        for d in range(2):
            for gg in range(SSM_GPS):
                one(d, gg, *ins[4 * d:4 * d + 4], *consts, selt_ref, e512t_ref, dy_refs[d], hist_ref,
                    *outs[4 * d:4 * d + 4], acc_ref, dst)

    xw, nw = SSM_GPS * SSM_GW, SSM_GPS * SSM_N
    in_specs, rbs = _ssd_specs(n_lat, n_ctx, step_of)
    in_specs += [pl.BlockSpec((2, SSM_GPS, 128, 128), lambda g, j: (0, g, 0, 0)), pl.BlockSpec((SSM_GW, 128), lambda g, j: (0, 0))]
    in_specs += [pl.BlockSpec((SSM_C, xw), lambda g, j, rb=rb: (rb(j), g)) for rb in rbs]
    in_specs += [pl.BlockSpec((2, 1, SSM_GPS, SSM_N, SSM_GW), lambda g, j: (0, step_of(j), g, 0, 0))]
    out_specs, out_shape = [], []
    for rb in rbs:
        out_specs += [pl.BlockSpec((SSM_C, xw), lambda g, j, rb=rb: (rb(j), g)),
                      pl.BlockSpec((SSM_C, nw), lambda g, j, rb=rb: (rb(j), g)),
                      pl.BlockSpec((SSM_C, nw), lambda g, j, rb=rb: (rb(j), g)),
                      pl.BlockSpec((SSM_GPS, SSM_C, 128), lambda g, j, rb=rb: (g, rb(j), 0))]
        out_shape += [jax.ShapeDtypeStruct((R, SSM_INNER), F32), jax.ShapeDtypeStruct((R, SSM_G * SSM_N), F32),
                      jax.ShapeDtypeStruct((R, SSM_G * SSM_N), F32), jax.ShapeDtypeStruct((SSM_G, R, 128), F32)]
    out_specs.append(pl.BlockSpec((2, SSM_GPS, 8, 128), lambda g, j: (0, g, 0, 0)))
    out_shape.append(jax.ShapeDtypeStruct((2, SSM_G, 8, 128), F32))
    args = [xbc, xbc, xbc, sm] * 2 + [k["sel"], k["dtb"], k["a"], k["a512"], k["sel_t"], k["e512_t"], dy, dy, hist]
    n_host_out = len(out_shape)
    outs = pl.pallas_call(
        _hosted(body, len(args), n_host_out, 1, comm, *_ssd_comm_steps(n_steps)), name="ssd_bwd",
        grid=(SSM_G // SSM_GPS, n_steps), in_specs=in_specs + [_ANY] * len(comm.arrays),
        out_specs=out_specs + [_ANY] * len(comm.out_shape), out_shape=out_shape + comm.out_shape,
        scratch_shapes=[pltpu.VMEM((2, SSM_GPS, SSM_N, SSM_GW), F32)] + comm.scratch,
        compiler_params=_cparams(("arbitrary", "arbitrary")),
    )(*args, *comm.arrays)
    return [(outs[n], outs[4 + n]) for n in range(4)] + [outs[8]], outs[n_host_out:]


def _gla_assemble(dq, dk, dv, dparts, tr):
    R = dq[0].shape[0]
    qk = GLA_H * GLA_DK

    def body(dqf_ref, dqb_ref, dkf_ref, dkb_ref, dvf_ref, dvb_ref, _, o_ref):
        o_ref[:, 0:qk] = (dqf_ref[...] + dqb_ref[...]).astype(BF16)
        o_ref[:, qk:2 * qk] = (dkf_ref[...] + dkb_ref[...]).astype(BF16)
        o_ref[:, 2 * qk:] = (dvf_ref[...] + dvb_ref[...]).astype(BF16)

    return pl.pallas_call(
        body, name="gla_assemble", grid=(R // tr,), in_specs=[_row_spec(tr, qk)] * 4 + [_row_spec(tr, D)] * 2 + [_ANY],
        out_specs=_dparts_out(tr, 2 * D, COL_Q), out_shape=jax.ShapeDtypeStruct(dparts.shape, BF16),
        input_output_aliases={6: 0}, compiler_params=_cparams(("arbitrary",)),
    )(*dq, *dk, *dv, dparts)


def _small_assemble(dp, dsm, sm, ut_hi, ut_lo, dparts, tr):
    R = sm.shape[0]
    qk = GLA_H * GLA_DK

    def body(dpf_ref, dpb_ref, dsmf_ref, dsmb_ref, sm_ref, uth_ref, utl_ref, _, o_ref, dup_ref, acc_ref, acc2_ref):
        @pl.when(pl.program_id(0) == 0)
        def _():
            dup_ref[...] = jnp.zeros_like(dup_ref)
            acc_ref[...] = jnp.zeros_like(acc_ref)
            acc2_ref[...] = jnp.zeros_like(acc2_ref)

        ssd = dsmf_ref[0] + dsmb_ref[0]
        for g in range(1, SSM_G):
            ssd = ssd + (dsmf_ref[g] + dsmb_ref[g])
        acc2_ref[0:1, :] += _colsum(ssd)
        sm_hi, sm_lo = _split(sm_ref[...])
        out = ssd
        for d, dp_ref in enumerate((dpf_ref, dpb_ref)):
            dpd = dp_ref[...]
            out = out + _nn3(dpd, uth_ref[d], utl_ref[d])
            p_hi, p_lo = _split(dpd)
            dup_ref[d] += _tn(sm_hi, p_hi) + _tn(sm_lo, p_hi) + _tn(sm_hi, p_lo)
            acc_ref[d:d + 1, :] += _colsum(dpd)
        o_ref[...] = out.astype(BF16)

    return pl.pallas_call(
        body, name="small_assemble", grid=(R // tr,),
        in_specs=[_row_spec(tr, qk)] * 2 + [pl.BlockSpec((SSM_G, tr, 128), lambda i: (0, i, 0))] * 2
        + [_row_spec(tr, 128), pl.BlockSpec((2, qk, 128), lambda i: (0, 0, 0)),
           pl.BlockSpec((2, qk, 128), lambda i: (0, 0, 0)), _ANY],
        out_specs=[_dparts_out(tr, 128, COL_SM), pl.BlockSpec((2, 128, qk), lambda i: (0, 0, 0)), _acc_spec(qk), _acc_spec(128)],
        out_shape=[jax.ShapeDtypeStruct(dparts.shape, BF16), jax.ShapeDtypeStruct((2, 128, qk), F32),
                   jax.ShapeDtypeStruct((8, qk), F32), jax.ShapeDtypeStruct((8, 128), F32)],
        input_output_aliases={7: 0}, compiler_params=_cparams(("arbitrary",)),
    )(*dp, *dsm, sm, ut_hi, ut_lo, dparts)


ADA_ROWS = 16
ADA_TILE = 512


def _dot3_f32(a, b, ca, cb):
    a_hi, a_lo = _split(a)
    b_hi, b_lo = _split(b)
    return _dg(a_hi, b_hi, ca, cb) + _dg(a_lo, b_hi, ca, cb) + _dg(a_hi, b_lo, ca, cb)


def _ada_fwd(cvec, w, b):
    cols = w.shape[1]

    def body(c_ref, w_ref, b_ref, o_ref):
        o_ref[...] = _dot3_f32(_silu(c_ref[...]), w_ref[...], 1, 0) + b_ref[...]

    return pl.pallas_call(
        body, name="ada_fwd", grid=(cols // ADA_TILE,),
        in_specs=[pl.BlockSpec((ADA_ROWS, D), lambda j: (0, 0)), pl.BlockSpec((D, ADA_TILE), lambda j: (0, j)),
                  pl.BlockSpec((1, ADA_TILE), lambda j: (0, j))],
        out_specs=pl.BlockSpec((ADA_ROWS, ADA_TILE), lambda j: (0, j)), out_shape=jax.ShapeDtypeStruct((ADA_ROWS, cols), F32),
        compiler_params=_cparams(("arbitrary",)),
    )(cvec, w, b)


def _adam(w, g, m, v):
    m2 = ADAM_B1 * m + (1.0 - ADAM_B1) * g
    v2 = ADAM_B2 * v + (1.0 - ADAM_B2) * (g * g)
    m_hat = m2 / (1.0 - ADAM_B1 ** ADAM_STEP)
    v_hat = v2 / (1.0 - ADAM_B2 ** ADAM_STEP)
    return -ADAM_LR * (m_hat / (jnp.sqrt(v_hat) + ADAM_EPS) + ADAM_WD * w), m2, v2


def _wada_bwd_adam(cvec, dada, w, m, v):
    rows, cols = w.shape
    tr = _tile(rows, 256, 128)

    def body(c_ref, d_ref, w_ref, m_ref, v_ref, g_ref, dl_ref, m2_ref, v2_ref, p_ref):
        wv = w_ref[...]
        g = _dot3_f32(_silu(c_ref[...]), d_ref[...], 0, 0)
        g_ref[...] = g
        dl_ref[...], m2_ref[...], v2_ref[...] = _adam(wv, g, m_ref[...], v_ref[...])
        p_ref[...] = _dot3_f32(d_ref[...], wv, 1, 1)

    blk = pl.BlockSpec((tr, cols), lambda i: (i, 0))
    return pl.pallas_call(
        body, name="wada_bwd_adam", grid=(rows // tr,),
        in_specs=[pl.BlockSpec((ADA_ROWS, tr), lambda i: (0, i)), pl.BlockSpec((ADA_ROWS, cols), lambda i: (0, 0)), blk, blk, blk],
        out_specs=[blk, blk, blk, blk, pl.BlockSpec((ADA_ROWS, tr), lambda i: (0, i))],
        out_shape=[jax.ShapeDtypeStruct((rows, cols), F32)] * 4 + [jax.ShapeDtypeStruct((ADA_ROWS, rows), F32)],
        compiler_params=_cparams(("arbitrary",)),
    )(cvec, dada, w, m, v)


def _reduce_adam(parts8, w, m, v, name):
    rows, cols = w.shape
    tr = _tile(rows, 64, 16)

    def body(p_ref, w_ref, m_ref, v_ref, g_ref, dl_ref, m2_ref, v2_ref):
        g = p_ref[0].astype(F32) + p_ref[N_CHIPS].astype(F32)
        for j in range(1, N_CHIPS):
            g = g + (p_ref[j].astype(F32) + p_ref[N_CHIPS + j].astype(F32))
        g_ref[...] = g
        dl_ref[...], m2_ref[...], v2_ref[...] = _adam(w_ref[...], g, m_ref[...], v_ref[...])

    blk = pl.BlockSpec((tr, cols), lambda i: (i, 0))
    return pl.pallas_call(
        body, name=name, grid=(rows // tr,), in_specs=[pl.BlockSpec((N_DEV, tr, cols), lambda i: (0, i, 0)), blk, blk, blk],
        out_specs=[blk] * 4, out_shape=[jax.ShapeDtypeStruct((rows, cols), F32)] * 4, compiler_params=_cparams(("arbitrary",)),
    )(parts8, w, m, v)


SMALL_W = 1024


def _sum8(g8):
    rows = g8.shape[1]

    def body(g_ref, o_ref):
        s = g_ref[0]
        for j in range(1, N_DEV):
            s = s + g_ref[j]
        o_ref[...] = s

    return pl.pallas_call(
        body, name="sum8", out_shape=jax.ShapeDtypeStruct((rows, SMALL_W), F32),
        in_specs=[pl.BlockSpec(memory_space=pltpu.VMEM)], out_specs=pl.BlockSpec(memory_space=pltpu.VMEM),
        compiler_params=pltpu.CompilerParams(vmem_limit_bytes=VMEM_LIMIT),
    )(g8)


def _cctx_grad(p8, c_ctx):
    def body(p_ref, c_ref, o_ref):
        s = p_ref[0]
        for chip in range(1, N_CHIPS):
            s = s + p_ref[2 * chip]
        o_ref[...] = s * _dsilu(c_ref[...])

    return pl.pallas_call(
        body, name="cctx_grad", out_shape=jax.ShapeDtypeStruct((1, D), F32),
        in_specs=[pl.BlockSpec(memory_space=pltpu.VMEM)] * 2, out_specs=pl.BlockSpec(memory_space=pltpu.VMEM),
    )(p8, c_ctx)


def _adam_small(w, g, m, v):
    def body(w_ref, g_ref, m_ref, v_ref, dl_ref, m2_ref, v2_ref):
        dl_ref[...], m2_ref[...], v2_ref[...] = _adam(w_ref[...], g_ref[...], m_ref[...], v_ref[...])

    vm = pl.BlockSpec(memory_space=pltpu.VMEM)
    return pl.pallas_call(
        body, name="adam_small", out_shape=[jax.ShapeDtypeStruct(w.shape, F32)] * 3, in_specs=[vm] * 4, out_specs=[vm] * 3,
        compiler_params=pltpu.CompilerParams(vmem_limit_bytes=VMEM_LIMIT),
    )(w, g, m, v)


def _pack(vecs, width=SMALL_W, row_mult=8):
    flat = jnp.concatenate([v.reshape(-1).astype(F32) for v in vecs])
    n = flat.shape[0]
    rows = -(-n // (width * row_mult)) * row_mult
    return jnp.pad(flat, (0, rows * width - n)).reshape(rows, width)


def _unpack(packed, shapes):
    flat = packed.reshape(-1)
    out, off = [], 0
    for s in shapes:
        n = int(np.prod(s))
        out.append(flat[off:off + n].reshape(s))
        off += n
    return out


WEIGHTS = ('c_ctx', 'w_ada', 'b_ada', 'norm1_w', 'w_in', 'gla_up_f', 'gla_bias_f', 'gla_up_b', 'gla_bias_b', 'gla_norm_w',
           'conv_w', 'conv_b', 'dt_bias_f', 'dt_bias_b', 'a_log_f', 'a_log_b', 'd_skip', 'ssm_norm_w', 'w_pa', 'w_pb', 'w_out',
           'norm2_w', 'w_gate', 'w_up', 'w_down', 'final_norm_w')
BIG = ('w_in', 'w_pa', 'w_pb', 'w_out', 'w_gate', 'w_up', 'w_down')
COL_SHARDED = ('w_in', 'w_gate', 'w_up')
SMALL_SHARDED = ('gla_up_f', 'gla_up_b', 'conv_w')
ROW_TILE = 256


def _blocks_to_full(g4, name):
    n, r, c = g4.shape
    return g4.transpose(1, 0, 2).reshape(r, n * c) if name in COL_SHARDED else g4.reshape(n * r, c)


def _full_to_blocks(full, name):
    r, c = full.shape
    if name in COL_SHARDED:
        return full.reshape(r, N_CHIPS, c // N_CHIPS).transpose(1, 0, 2)
    return full.reshape(N_CHIPS, r // N_CHIPS, c)


def _permute_in(w_in_full):
    off = np.concatenate([[0], np.cumsum(IN_WIDTHS)])
    cols = [w_in_full[:, off[p]:off[p + 1]] for p in PERM]
    return jnp.concatenate(cols + [jnp.zeros((w_in_full.shape[0], SMALL_PAD), w_in_full.dtype)], axis=1)


def _unpermute_in(wp):
    off = np.concatenate([[0], np.cumsum([IN_WIDTHS[p] for p in PERM])])
    pieces = {p: wp[:, off[i]:off[i + 1]] for i, p in enumerate(PERM)}
    return jnp.concatenate([pieces[p] for p in range(len(IN_WIDTHS))], axis=1)


def _chip_cols(full, chip, n):
    return lax.dynamic_slice_in_dim(full, chip * n, n, axis=1)


def kernel(x, c, ctx, c_ctx, w_ada, b_ada, norm1_w, w_in, gla_up_f, gla_bias_f, gla_up_b, gla_bias_b, gla_norm_w, conv_w, conv_b, dt_bias_f, dt_bias_b, a_log_f, a_log_b, d_skip, ssm_norm_w, w_pa, w_pb, w_out, norm2_w, w_gate, w_up, w_down, final_norm_w, loss_target, m_c_ctx, m_w_ada, m_b_ada, m_norm1_w, m_w_in, m_gla_up_f, m_gla_bias_f, m_gla_up_b, m_gla_bias_b, m_gla_norm_w, m_conv_w, m_conv_b, m_dt_bias_f, m_dt_bias_b, m_a_log_f, m_a_log_b, m_d_skip, m_ssm_norm_w, m_w_pa, m_w_pb, m_w_out, m_norm2_w, m_w_gate, m_w_up, m_w_down, m_final_norm_w, v_c_ctx, v_w_ada, v_b_ada, v_norm1_w, v_w_in, v_gla_up_f, v_gla_bias_f, v_gla_up_b, v_gla_bias_b, v_gla_norm_w, v_conv_w, v_conv_b, v_dt_bias_f, v_dt_bias_b, v_a_log_f, v_a_log_b, v_d_skip, v_ssm_norm_w, v_w_pa, v_w_pb, v_w_out, v_norm2_w, v_w_gate, v_w_up, v_w_down, v_final_norm_w):
    given = dict(locals())
    W = {n: given[n] for n in WEIGHTS}
    M = {n: given["m_" + n] for n in WEIGHTS}
    V = {n: given["v_" + n] for n in WEIGHTS}
    L, Lc = x.shape[1], ctx.shape[1]
    tr = ROW_TILE
    assert L % tr == 0 and Lc % tr == 0 and L % Lc == 0 and Lc % SSM_C == 0
    n_lat_tiles = L // tr
    xi, yi, ci = _place()
    chip, me = 2 * xi + yi, 4 * xi + 2 * yi + ci
    xall = jnp.concatenate([x[0], ctx[0]], axis=0)

    g0 = _allgather_small(_pack([c[0]] + [W[n][0] for n in SMALL_SHARDED]), "gather_c")
    g0 = g0.reshape(N_DEV, -1)
    c_all = g0[:, :D]
    small_full, off = {}, D
    for n in SMALL_SHARDED:
        r, cols = W[n].shape[1:]
        small_full[n] = jnp.concatenate([g0[2 * k, off:off + r * cols].reshape(r, cols) for k in range(N_CHIPS)], axis=1)
        off += r * cols
    up_f, up_b, conv_w_full = (small_full[n] for n in SMALL_SHARDED)

    cvec = jnp.zeros((ADA_ROWS, D), F32).at[:N_DEV].set(c_all).at[N_DEV].set(c_ctx)
    ada_cols = w_ada.shape[2]
    ada_part = _ada_fwd(cvec, w_ada[0], _chip_cols(b_ada, chip, ada_cols))
    g1_all = _allgather_small(ada_part, "gather_ada")
    ada_full = jnp.concatenate([g1_all[2 * k] for k in range(N_CHIPS)], axis=1)
    mine = lax.dynamic_slice_in_dim(ada_full, me, 1, axis=0)
    sh1, sc1, g1, sh2, sc2, g2 = (mine[:, k * D:(k + 1) * D] for k in range(6))
    csh1, csc1 = ada_full[N_DEV:N_DEV + 1, :D], ada_full[N_DEV:N_DEV + 1, D:2 * D]
    mod = jnp.stack([jnp.stack([sh1, sc1]), jnp.stack([csh1, csc1])])

    full = {'w_in': _blocks_to_full(_gather_split(w_in[0].astype(BF16), "gather_w_in"), 'w_in')}
    wp = _permute_in(full['w_in'])
    later = [n for n in BIG if n != 'w_in']

    def lr_rows(up, base):
        return jnp.zeros((128, GLA_H * GLA_DK), F32).at[base:base + GLA_RANK].set(up)
    u2 = jnp.stack([lr_rows(up_f, SM_LRF), lr_rows(up_b, SM_LRB)])
    u2_hi = u2.astype(BF16)
    u2_lo = (u2 - u2_hi.astype(F32)).astype(BF16)
    ut = u2.transpose(0, 2, 1)
    ut_hi = ut.astype(BF16)
    ut_lo = (ut - ut_hi.astype(F32)).astype(BF16)
    gbias = jnp.stack([gla_bias_f, gla_bias_b])
    kc = _ssd_consts(jnp.stack([dt_bias_f[0], dt_bias_b[0]]), jnp.stack([a_log_f[0], a_log_b[0]]))
    gw4 = jnp.tile(gla_norm_w, (1, GLA_H))
    dskip_exp = jnp.repeat(d_skip, SSM_P, axis=1)
    n_gla = (L // GLA_C, Lc // GLA_C)
    n_ssd = (L // SSM_C, Lc // SSM_C)

    h1 = _norm_mod(xall, norm1_w, mod, n_lat_tiles, tr)
    parts = _mm(h1, wp, "nn", ACT, "mm_in", tm=768, tn=1152)
    sm = _mm(h1, wp[:, COL_SM:], "nn", F32, "mm_in_small", tm=768)
    xbc = _conv_fwd(parts, conv_w_full, conv_b, L // Lc, Lc)
    *o2, gla_hist = _gla_fwd(parts, sm, u2_hi, u2_lo, gbias, *n_gla)
    (*y2, ssd_hist), gathered = _ssd_fwd(xbc, sm, kc, *n_ssd, _gather_comm([W[n][0].astype(BF16) for n in later]))
    full.update({n: _blocks_to_full(g, n) for n, g in zip(later, gathered)})
    oan = _gla_out(o2, parts, gw4, tr)
    obn = _ssd_out(y2, xbc, parts, dskip_exp, ssm_norm_w, tr)
    ya = _mm(oan, full['w_pa'], "nn", ACT, "mm_pa", tm=768)
    yb = _mm(obn, full['w_pb'], "nn", ACT, "mm_pb", tm=768)
    merged = _merge(ya, yb, parts, tr)
    mix = _mm(merged, full['w_out'], "nn", ACT, "mm_out", tm=768)
    h2, u = _resid_norm_mod(xall, mix, g1, norm2_w, sh2, sc2, tr)
    gp = _mm(u, full['w_gate'], "nn", ACT, "mm_gate", tm=768, tn=1408)
    up = _mm(u, full['w_up'], "nn", ACT, "mm_up", tm=768, tn=1408)
    act = _swiglu_act(gp, up, tr)
    f = _mm(act, full['w_down'], "nn", ACT, "mm_down", tm=768)
    dh3, df, acc_loss = _loss_head(h2, f, loss_target[0], g2, final_norm_w[None], n_lat_tiles, tr)

    dw = {}
    da = _mm(df, full['w_down'], "nt", ACT, "mm_down_dx", tm=768, tn=1408)
    dw['w_down'] = _mm(act, df, "tn", BF16, "mm_down_dw", tm=1408, tk=768)
    dgp, dup = _swiglu_act_bwd(da, gp, up, tr)
    du_a = _mm(dgp, full['w_gate'], "nt", ACT, "mm_gate_dx", tm=768, tk=1408)
    du_b = _mm(dup, full['w_up'], "nt", ACT, "mm_up_dx", tm=768, tk=1408)
    dw['w_gate'] = _mm(u, dgp, "tn", BF16, "mm_gate_dw", tm=1024, tn=1408, tk=768)
    dw['w_up'] = _mm(u, dup, "tn", BF16, "mm_up_dw", tm=1024, tn=1408, tk=768)
    dh2, dmix, acc_ffn = _ffn_in_bwd(du_a, du_b, h2, dh3, mix, sc2, g1, norm2_w, tr)
    dmerged = _mm(dmix, full['w_out'], "nt", ACT, "mm_out_dx", tm=768)
    dw['w_out'] = _mm(merged, dmix, "tn", BF16, "mm_out_dw", tm=1024, tk=768)
    dya, dyb, dparts = _merge_bwd(dmerged, ya, yb, parts, lax.empty((L + Lc, PW), BF16), tr)
    doan = _mm(dya, full['w_pa'], "nt", ACT, "mm_pa_dx", tm=768)
    dw['w_pa'] = _mm(oan, dya, "tn", BF16, "mm_pa_dw", tm=1024, tk=768)
    dobn = _mm(dyb, full['w_pb'], "nt", ACT, "mm_pb_dx", tm=768)
    dw['w_pb'] = _mm(obn, dyb, "tn", BF16, "mm_pb_dw", tm=1024, tk=768)
    do, dparts, acc_gla = _gla_out_bwd(doan, o2, parts, gw4, dparts, tr)
    dq, dk, dv, dpre = _gla_bwd(do, parts, sm, u2_hi, u2_lo, gbias, gla_hist, *n_gla)
    dy, dparts, acc_ssd = _ssd_out_bwd(dobn, y2, xbc, parts, dskip_exp, ssm_norm_w, dparts, tr)
    (dx_scan, db_scan, dc_scan, dsm, acc_alog), exchanged = _ssd_bwd(
        dy, xbc, sm, kc, ssd_hist, *n_ssd, _exchange_comm([_full_to_blocks(dw[n], n) for n in later]))
    exchanged = dict(zip(later, exchanged))
    dparts, acc_conv = _conv_bwd(dx_scan, db_scan, dc_scan, dy, dskip_exp, parts, conv_w_full, conv_b, dparts, L // Lc, Lc)
    dparts = _gla_assemble(dq, dk, dv, dparts, tr)
    dparts, dup_gla, acc_gbias, acc_dtb = _small_assemble(dpre, dsm, sm, ut_hi, ut_lo, dparts, tr)
    dw['w_in'] = _unpermute_in(_mm(h1, dparts, "tn", BF16, "mm_in_dw", tm=1024, tn=1152, tk=768))
    dh1, (exchanged['w_in'],) = _mm(dparts, wp, "nt", F32, "mm_in_dx", tm=768, tk=1152,
                                    comm=_exchange_comm([_full_to_blocks(dw['w_in'], 'w_in')]))
    dxall, acc_n1 = _norm1_bwd(dh1, xall, dh2, norm1_w, mod, n_lat_tiles, tr)

    partial = dict(
        norm1_w=acc_n1[0, 2] + acc_n1[1, 2],
        gla_up_f=dup_gla[0, SM_LRF:SM_LRF + GLA_RANK], gla_bias_f=acc_gbias[0],
        gla_up_b=dup_gla[1, SM_LRB:SM_LRB + GLA_RANK], gla_bias_b=acc_gbias[1],
        gla_norm_w=acc_gla[0].reshape(GLA_H, GLA_DV).sum(0),
        conv_w=acc_conv[:SSM_CONV], conv_b=acc_conv[SSM_CONV],
        dt_bias_f=acc_dtb[0, SM_DTF:SM_DTF + SSM_HEADS], dt_bias_b=acc_dtb[0, SM_DTB:SM_DTB + SSM_HEADS],
        a_log_f=acc_alog[0, :, 0, :SSM_HPG], a_log_b=acc_alog[1, :, 0, :SSM_HPG],
        d_skip=acc_ssd[1].reshape(SSM_HEADS, SSM_P).sum(1), ssm_norm_w=acc_ssd[0],
        norm2_w=acc_ffn[2], final_norm_w=acc_loss[0],
    )
    dada = jnp.concatenate([acc_n1[0, 1], acc_n1[0, 0], acc_ffn[3], acc_ffn[1], acc_ffn[0], acc_loss[1]])
    dada_ctx = jnp.concatenate([acc_n1[1, 1], acc_n1[1, 0], jnp.zeros((4 * D,), F32)])
    names = list(partial)
    payload = [partial[n] for n in names] + [dada + dada_ctx, dada_ctx, acc_loss[2], dada]
    sizes = [int(np.prod(p.shape)) for p in payload]
    g8 = _allgather_small(_pack(payload), "gather_small_grads")
    summed = _unpack(_sum8(g8), [(s,) for s in sizes])
    grads = {n: s.reshape(W[n].shape if n not in SMALL_SHARDED else partial[n].shape) for n, s in zip(names, summed)}
    grads['b_ada'] = summed[len(names)].reshape(b_ada.shape)
    dada_ctx_sum = summed[len(names) + 1]
    loss = jnp.sum(summed[len(names) + 2])
    dada_all = g8.reshape(N_DEV, -1)[:, sum(sizes[:-1]):sum(sizes)]

    dada16 = jnp.zeros((ADA_ROWS, ada_cols), F32)
    dada16 = dada16.at[:N_DEV].set(_chip_cols(dada_all, chip, ada_cols)).at[N_DEV].set(_chip_cols(dada_ctx_sum[None], chip, ada_cols)[0])
    g_wada, dl_wada, m_wada, v_wada, p16 = _wada_bwd_adam(cvec, dada16, w_ada[0], m_w_ada[0], v_w_ada[0])
    p8 = _allgather_small(p16[N_DEV:], "gather_cctx")
    grads['c_ctx'] = _cctx_grad(p8[:, 0:1, :], c_ctx[None])[0]
    for n in SMALL_SHARDED:
        grads[n] = _chip_cols(grads[n], chip, W[n].shape[2])[None]

    small = [n for n in WEIGHTS if n not in BIG and n != 'w_ada']
    shapes = [W[n].shape for n in small]
    dl_s, m_s, v_s = _adam_small(*[_pack([d[n] for n in small]) for d in (W, grads, M, V)])
    delta = dict(zip(small, _unpack(dl_s, shapes)))
    new_m = dict(zip(small, _unpack(m_s, shapes)))
    new_v = dict(zip(small, _unpack(v_s, shapes)))
    grads['w_ada'], delta['w_ada'], new_m['w_ada'], new_v['w_ada'] = g_wada[None], dl_wada[None], m_wada[None], v_wada[None]

    for n in BIG:
        g, dl, m2, v2 = _reduce_adam(exchanged[n], W[n][0], M[n][0], V[n][0], "adam_" + n)
        grads[n], delta[n], new_m[n], new_v[n] = g[None], dl[None], m2[None], v2[None]

    return (loss, dxall[:L][None], *[grads[n] for n in WEIGHTS], *[delta[n] for n in WEIGHTS],
            *[new_m[n] for n in WEIGHTS], *[new_v[n] for n in WEIGHTS])
```

```python
import functools

import numpy as np
import jax
import jax.numpy as jnp
from jax import lax
from jax.experimental import pallas as pl
from jax.experimental.pallas import tpu as pltpu

F32 = jnp.float32
BF16 = jnp.bfloat16
MESH = pl.DeviceIdType.MESH

D = 1024
EPS = 1e-6
GRID_W = 64
GLA_H, GLA_DK, GLA_DV, GLA_RANK, GLA_TAU = 4, 128, 256, 16, 16.0
GLA_C = 128
SSM_INNER, SSM_P, SSM_HEADS, SSM_G, SSM_HPG, SSM_N = 2048, 64, 32, 4, 8, 128
SSM_C = 128
SSM_CONV, CONV_LEFT = 4, 2
D_FF = 2816
IN_WIDTHS = (512, 512, 1024, 1024, 16, 16, 2048, 2048, 512, 512, 32, 32, 1024, 1024)
D_IN = sum(IN_WIDTHS)
PERM = (6, 7, 8, 9, 3, 0, 1, 2, 12, 13, 4, 5, 10, 11)
PW = 10368
SMALL_PAD = PW - D_IN
COL_Z, COL_XBC, COL_R, COL_Q, COL_K, COL_V, COL_GA, COL_GB, COL_SM = 0, 2048, 5120, 6144, 6656, 7168, 8192, 9216, 10240
SM_LRF, SM_LRB, SM_DTF, SM_DTB = 0, 16, 32, 64
EXP_CLAMP = 80.0
ADAM_LR, ADAM_B1, ADAM_B2, ADAM_EPS, ADAM_WD, ADAM_STEP = 0.001, 0.9, 0.999, 1e-08, 0.01, 10
N_CHIPS, N_DEV = 4, 8
VMEM_LIMIT = 56 * 1024 * 1024


def _cparams(sem=None):
    return pltpu.CompilerParams(dimension_semantics=sem, vmem_limit_bytes=VMEM_LIMIT)


def _dg(a, b, ca, cb):
    return lax.dot_general(a, b, (((ca,), (cb,)), ((), ())), preferred_element_type=F32)


def _nn(a, b):
    return _dg(a, b, 1, 0)


def _nt(a, b):
    return _dg(a, b, 1, 1)


def _tn(a, b):
    return _dg(a, b, 0, 0)


def _bf(x):
    return x.astype(BF16)


def _split(x):
    hi = x.astype(BF16)
    return hi, (x - hi.astype(F32)).astype(BF16)


def _nn_x(a, b_exact):
    hi, lo = _split(a)
    return _nn(hi, b_exact) + _nn(lo, b_exact)


def _x_nn(a_exact, b):
    hi, lo = _split(b)
    return _nn(a_exact, hi) + _nn(a_exact, lo)


def _nn3(a, b_hi, b_lo):
    hi, lo = _split(a)
    return _nn(hi, b_hi) + _nn(lo, b_hi) + _nn(hi, b_lo)


def _sigmoid(x):
    return 1.0 / (1.0 + jnp.exp(-x))


def _silu(x):
    return x * _sigmoid(x)


def _dsilu(x):
    s = _sigmoid(x)
    return s * (1.0 + x * (1.0 - s))


def _softplus(x):
    return jnp.maximum(x, 0.0) + jnp.log(1.0 + jnp.exp(-jnp.abs(x)))


def _log_sigmoid(x):
    return jnp.minimum(x, 0.0) - jnp.log(1.0 + jnp.exp(-jnp.abs(x)))


def _tile(n, target, mult=8):
    best = None
    for t in range(mult, min(n, target) + 1, mult):
        if n % t == 0:
            best = t
    assert best is not None, (n, target, mult)
    return best


def _mm(a, b, mode, out_dtype, name, tm=512, tn=1024, tk=2048, comm=None):
    if mode == "nn":
        (M, K), N = a.shape, b.shape[1]
    elif mode == "nt":
        (M, K), N = a.shape, b.shape[0]
    else:
        (K, M), N = a.shape, b.shape[1]
    tm, tn, tk = _tile(M, tm, 128), _tile(N, tn, 128), _tile(K, tk, 128)
    nk = K // tk
    ca, cb = {"nn": (1, 0), "nt": (1, 1), "tn": (0, 0)}[mode]

    def body(a_ref, b_ref, o_ref, *acc):
        part = _dg(a_ref[...], b_ref[...], ca, cb)
        if nk == 1:
            o_ref[...] = part.astype(out_dtype)
        else:
            k = pl.program_id(2)

            @pl.when(k == 0)
            def _():
                acc[0][...] = part

            @pl.when(k > 0)
            def _():
                acc[0][...] += part

            @pl.when(k == nk - 1)
            def _():
                o_ref[...] = acc[0][...].astype(out_dtype)

    a_spec = pl.BlockSpec((tk, tm), lambda i, j, k: (k, i)) if mode == "tn" else pl.BlockSpec((tm, tk), lambda i, j, k: (i, k))
    b_spec = pl.BlockSpec((tn, tk), lambda i, j, k: (j, k)) if mode == "nt" else pl.BlockSpec((tk, tn), lambda i, j, k: (k, j))
    gi, gj = M // tm, N // tn
    scratch = [pltpu.VMEM((tm, tn), F32)] if nk > 1 else []
    out_spec, out_shape = pl.BlockSpec((tm, tn), lambda i, j, k: (i, j)), jax.ShapeDtypeStruct((M, N), out_dtype)
    if comm is None:
        return pl.pallas_call(
            body, name=name, grid=(gi, gj, nk), in_specs=[a_spec, b_spec], out_specs=out_spec, out_shape=out_shape,
            scratch_shapes=scratch, compiler_params=_cparams(("arbitrary", "arbitrary", "arbitrary")),
        )(a, b)
    at = lambda i, j, k: (pl.program_id(0) == i) & (pl.program_id(1) == j) & (pl.program_id(2) == k)
    hosted = _hosted(body, 2, 1, len(scratch), comm, lambda: at(0, 0, 0), lambda: at(gi - 1, 0, 0),
                     lambda: at(gi - 1, gj - 1, nk - 1))
    outs = pl.pallas_call(
        hosted, name=name, grid=(gi, gj, nk), in_specs=[a_spec, b_spec] + [_ANY] * len(comm.arrays),
        out_specs=[out_spec] + [_ANY] * len(comm.out_shape), out_shape=[out_shape] + comm.out_shape,
        scratch_shapes=scratch + comm.scratch, compiler_params=_cparams(("arbitrary", "arbitrary", "arbitrary")),
    )(a, b, *comm.arrays)
    return outs[0], outs[1:]


def _place():
    return lax.axis_index("x"), lax.axis_index("y"), lax.axis_index("c")


def _flip(v, bit):
    return 1 - v if bit else v


def _allgather_small(v, name):
    R, C = v.shape

    def body(v_ref, out_ref, send_sems, recv_sems, local_sem):
        x, y, c = _place()
        me = 4 * x + 2 * y + c
        mine = pltpu.make_async_copy(v_ref, out_ref.at[me], local_sem)
        mine.start()

        def peer(r):
            return _flip(x, (r >> 2) & 1), _flip(y, (r >> 1) & 1), _flip(c, r & 1)

        sends = [pltpu.make_async_remote_copy(
            src_ref=v_ref, dst_ref=out_ref.at[me], send_sem=send_sems.at[r - 1], recv_sem=recv_sems.at[r - 1],
            device_id=peer(r), device_id_type=MESH) for r in range(1, N_DEV)]
        for cp in sends:
            cp.start()
        for r in range(1, N_DEV):
            px, py, pc = peer(r)
            pltpu.make_async_remote_copy(
                src_ref=v_ref, dst_ref=out_ref.at[4 * px + 2 * py + pc], send_sem=send_sems.at[r - 1],
                recv_sem=recv_sems.at[r - 1], device_id=(x, y, c), device_id_type=MESH).wait_recv()
        for cp in sends:
            cp.wait_send()
        mine.wait()

    return pl.pallas_call(
        body, name=name, out_shape=jax.ShapeDtypeStruct((N_DEV, R, C), v.dtype),
        in_specs=[pl.BlockSpec(memory_space=pltpu.VMEM)], out_specs=pl.BlockSpec(memory_space=pltpu.VMEM),
        scratch_shapes=[pltpu.SemaphoreType.DMA((N_DEV - 1,)), pltpu.SemaphoreType.DMA((N_DEV - 1,)), pltpu.SemaphoreType.DMA],
        compiler_params=pltpu.CompilerParams(vmem_limit_bytes=VMEM_LIMIT),
    )(v)


_CHIP_RELATIONS = ((1, 0), (0, 1), (1, 1))


def _gather_split(shard, name):
    rows, cols = shard.shape
    half = rows // 2

    def body(in_ref, out_ref, send_sems, recv_sems, local_sem):
        x, y, c = _place()
        chip = 2 * x + y
        mine = pl.ds(pl.multiple_of(c * half, 16), half)
        local = pltpu.make_async_copy(in_ref, out_ref.at[chip], local_sem)
        local.start()
        peers = [(_flip(x, fx), _flip(y, fy)) for fx, fy in _CHIP_RELATIONS]
        sends = [pltpu.make_async_remote_copy(
            src_ref=in_ref.at[mine], dst_ref=out_ref.at[chip, mine], send_sem=send_sems.at[j], recv_sem=recv_sems.at[j],
            device_id=(px, py, c), device_id_type=MESH) for j, (px, py) in enumerate(peers)]
        for cp in sends:
            cp.start()
        for j, (px, py) in enumerate(peers):
            landed = out_ref.at[2 * px + py, mine]
            pltpu.make_async_remote_copy(
                src_ref=landed, dst_ref=landed, send_sem=send_sems.at[j], recv_sem=recv_sems.at[j],
                device_id=(x, y, c), device_id_type=MESH).wait_recv()
            fwd = pltpu.make_async_remote_copy(
                src_ref=landed, dst_ref=landed, send_sem=send_sems.at[3 + j], recv_sem=recv_sems.at[3 + j],
                device_id=(x, y, 1 - c), device_id_type=MESH)
            fwd.start()
            sends.append(fwd)
        for j in range(3):
            landed = out_ref.at[0, mine]
            pltpu.make_async_remote_copy(
                src_ref=landed, dst_ref=landed, send_sem=send_sems.at[3 + j], recv_sem=recv_sems.at[3 + j],
                device_id=(x, y, c), device_id_type=MESH).wait_recv()
        for cp in sends:
            cp.wait_send()
        local.wait()

    any_spec = pl.BlockSpec(memory_space=pl.ANY)
    return pl.pallas_call(
        body, name=name, out_shape=jax.ShapeDtypeStruct((N_CHIPS, rows, cols), shard.dtype),
        in_specs=[any_spec], out_specs=any_spec,
        scratch_shapes=[pltpu.SemaphoreType.DMA((6,)), pltpu.SemaphoreType.DMA((6,)), pltpu.SemaphoreType.DMA],
    )(shard)


class _Comm:
    def __init__(self, arrays, out_shape, scratch, start, middle, finish):
        self.arrays, self.out_shape, self.scratch = arrays, out_shape, scratch
        self.start, self.middle, self.finish = start, middle, finish


def _hosted(body, n_in, n_out, n_scratch, comm, first, middle, last):
    nc, no = len(comm.arrays), len(comm.out_shape)

    def wrapped(*refs):
        a = n_in + nc
        b = a + n_out + no
        ins, c_ins, outs, c_outs = refs[:n_in], refs[n_in:a], refs[a:a + n_out], refs[a + n_out:b]
        scratch, c_sems = refs[b:b + n_scratch], refs[b + n_scratch:]

        @pl.when(first())
        def _():
            comm.start(c_ins, c_outs, c_sems)

        body(*ins, *outs, *scratch)
        if comm.middle is not None:
            @pl.when(middle())
            def _():
                comm.middle(c_ins, c_outs, c_sems)

        @pl.when(last())
        def _():
            comm.finish(c_ins, c_outs, c_sems)

    return wrapped


_ANY = pl.BlockSpec(memory_space=pl.ANY)


def _gather_comm(shards):
    n = len(shards)

    def copies(kind, ins, outs, sems):
        send_sems, recv_sems, local_sems = sems
        x, y, c = _place()
        chip = 2 * x + y
        if kind == "local":
            return [pltpu.make_async_copy(ins[i], outs[i].at[chip], local_sems.at[i]) for i in range(n)]
        made = []
        for i in range(n):
            for j, (fx, fy) in enumerate(_CHIP_RELATIONS):
                px, py = _flip(x, fx), _flip(y, fy)
                slot, to = (chip, (px, py, c)) if kind == "send" else (2 * px + py, (x, y, c))
                made.append(pltpu.make_async_remote_copy(
                    src_ref=ins[i], dst_ref=outs[i].at[slot], send_sem=send_sems.at[i, j], recv_sem=recv_sems.at[i, j],
                    device_id=to, device_id_type=MESH))
        return made

    def start(ins, outs, sems):
        for cp in copies("local", ins, outs, sems) + copies("send", ins, outs, sems):
            cp.start()

    def finish(ins, outs, sems):
        for cp in copies("recv", ins, outs, sems):
            cp.wait_recv()
        for cp in copies("send", ins, outs, sems):
            cp.wait_send()
        for cp in copies("local", ins, outs, sems):
            cp.wait()

    return _Comm(list(shards), [jax.ShapeDtypeStruct((N_CHIPS,) + s.shape, s.dtype) for s in shards],
                 [pltpu.SemaphoreType.DMA((n, 3)), pltpu.SemaphoreType.DMA((n, 3)), pltpu.SemaphoreType.DMA((n,))],
                 start, None, finish)


def _exchange_comm(blocks):
    n = len(blocks)

    def copies(kind, ins, outs, sems):
        send_sems, recv_sems, local_sems = sems
        x, y, c = _place()
        chip = 2 * x + y
        me, sibling = (x, y, c), (x, y, 1 - c)

        def remote(src, dst, i, j, to):
            return pltpu.make_async_remote_copy(src_ref=src, dst_ref=dst, send_sem=send_sems.at[i, j],
                                                recv_sem=recv_sems.at[i, j], device_id=to, device_id_type=MESH)

        made = []
        for i in range(n):
            if kind == "local":
                made.append(pltpu.make_async_copy(ins[i].at[chip], outs[i].at[chip], local_sems.at[i]))
                continue
            for j, (fx, fy) in enumerate(_CHIP_RELATIONS):
                px, py = _flip(x, fx), _flip(y, fy)
                src = 2 * px + py
                if kind == "first":
                    made.append(remote(ins[i].at[src], outs[i].at[chip], i, j, (px, py, c)))
                elif kind == "landed":
                    made.append(remote(ins[i].at[src], outs[i].at[src], i, j, me))
                elif kind == "passed":
                    made.append(remote(outs[i].at[src], outs[i].at[N_CHIPS + src], i, 4 + j, sibling))
            if kind == "first":
                made.append(remote(ins[i].at[chip], outs[i].at[N_CHIPS + chip], i, 3, sibling))
            if kind == "arrivals":
                made += [remote(ins[i].at[0], outs[i].at[0], i, j, me) for j in (3, 4, 5, 6)]
        return made

    def start(ins, outs, sems):
        for cp in copies("local", ins, outs, sems) + copies("first", ins, outs, sems):
            cp.start()

    def middle(ins, outs, sems):
        for got, fwd in zip(copies("landed", ins, outs, sems), copies("passed", ins, outs, sems)):
            got.wait_recv()
            fwd.start()

    def finish(ins, outs, sems):
        for cp in copies("arrivals", ins, outs, sems):
            cp.wait_recv()
        for cp in copies("first", ins, outs, sems) + copies("passed", ins, outs, sems):
            cp.wait_send()
        for cp in copies("local", ins, outs, sems):
            cp.wait()

    return _Comm(list(blocks), [jax.ShapeDtypeStruct((N_DEV,) + b.shape[1:], b.dtype) for b in blocks],
                 [pltpu.SemaphoreType.DMA((n, 7)), pltpu.SemaphoreType.DMA((n, 7)), pltpu.SemaphoreType.DMA((n,))],
                 start, middle, finish)


def _row_spec(tr, w, col=0):
    return pl.BlockSpec((tr, w), lambda i: (i, col))


def _vec_spec(w):
    return pl.BlockSpec((1, w), lambda i: (0, 0))


def _acc_spec(w):
    return pl.BlockSpec((8, w), lambda i: (0, 0))


def _rms(x):
    return lax.rsqrt(jnp.mean(x * x, axis=-1, keepdims=True) + EPS)


def _rms_bwd(dn, n, rstd):
    return rstd * (dn - n * jnp.mean(dn * n, axis=-1, keepdims=True))


def _colsum(x):
    return jnp.sum(x, axis=0, keepdims=True)


def _zero_first(ref):
    @pl.when(pl.program_id(0) == 0)
    def _():
        ref[...] = jnp.zeros_like(ref)


def _norm_mod(xall, w, mod, n_lat_tiles, tr):
    R = xall.shape[0]

    def body(x_ref, w_ref, mod_ref, o_ref):
        x = x_ref[...]
        n = x * _rms(x) * w_ref[...]
        o_ref[...] = (n * (1.0 + mod_ref[0, 1]) + mod_ref[0, 0]).astype(BF16)

    return pl.pallas_call(
        body, name="norm1_mod", grid=(R // tr,),
        in_specs=[_row_spec(tr, D), _vec_spec(D),
                  pl.BlockSpec((1, 2, 1, D), lambda i: (jnp.where(i >= n_lat_tiles, 1, 0), 0, 0, 0))],
        out_specs=_row_spec(tr, D), out_shape=jax.ShapeDtypeStruct((R, D), BF16),
        compiler_params=_cparams(("arbitrary",)),
    )(xall, w, mod)


def _resid_norm_mod(xall, mix, g1, w2, sh2, sc2, tr):
    R = xall.shape[0]

    def body(x_ref, mix_ref, g1_ref, w_ref, sh_ref, sc_ref, h2_ref, u_ref):
        h2 = x_ref[...] + g1_ref[...] * mix_ref[...]
        h2_ref[...] = h2
        n = h2 * _rms(h2) * w_ref[...]
        u_ref[...] = (n * (1.0 + sc_ref[...]) + sh_ref[...]).astype(BF16)

    return pl.pallas_call(
        body, name="resid_norm2_mod", grid=(R // tr,),
        in_specs=[_row_spec(tr, D), _row_spec(tr, D)] + [_vec_spec(D)] * 4,
        out_specs=[_row_spec(tr, D), _row_spec(tr, D)],
        out_shape=[jax.ShapeDtypeStruct((R, D), F32), jax.ShapeDtypeStruct((R, D), BF16)],
        compiler_params=_cparams(("arbitrary",)),
    )(xall, mix, g1, w2, sh2, sc2)


def _loss_head(h2, f, target, g2, fw, n_lat_tiles, tr):
    R = h2.shape[0]

    def body(h2_ref, f_ref, t_ref, g2_ref, fw_ref, dh3_ref, df_ref, acc_ref):
        _zero_first(acc_ref)
        lat = pl.program_id(0) < n_lat_tiles
        fv = f_ref[...]
        h3 = h2_ref[...] + g2_ref[...] * fv
        rstd = _rms(h3)
        n = h3 * rstd
        err = n * fw_ref[...] - t_ref[...]
        dy = err * (1.0 / D)
        dh3 = jnp.where(lat, _rms_bwd(dy * fw_ref[...], n, rstd), 0.0)
        dh3_ref[...] = dh3
        df_ref[...] = (g2_ref[...] * dh3).astype(BF16)
        acc_ref[0:1, :] += jnp.where(lat, _colsum(dy * n), 0.0)
        acc_ref[1:2, :] += _colsum(dh3 * fv)
        acc_ref[2:3, :] += jnp.where(lat, _colsum(err * err) * (0.5 / D), 0.0)

    return pl.pallas_call(
        body, name="loss_head", grid=(R // tr,),
        in_specs=[_row_spec(tr, D), _row_spec(tr, D),
                  pl.BlockSpec((tr, D), lambda i: (jnp.minimum(i, n_lat_tiles - 1), 0)), _vec_spec(D), _vec_spec(D)],
        out_specs=[_row_spec(tr, D), _row_spec(tr, D), _acc_spec(D)],
        out_shape=[jax.ShapeDtypeStruct((R, D), F32), jax.ShapeDtypeStruct((R, D), BF16), jax.ShapeDtypeStruct((8, D), F32)],
        compiler_params=_cparams(("arbitrary",)),
    )(h2, f, target, g2, fw)


def _ffn_in_bwd(du_a, du_b, h2, dh3, mix, sc2, g1, w2, tr):
    R = h2.shape[0]

    def body(dua_ref, dub_ref, h2_ref, dh3_ref, mix_ref, sc_ref, g1_ref, w_ref, dh2_ref, dmix_ref, acc_ref):
        _zero_first(acc_ref)
        du = _f32(dua_ref) + _f32(dub_ref)
        h2 = h2_ref[...]
        rstd = _rms(h2)
        n = h2 * rstd
        dnw = du * (1.0 + sc_ref[...])
        dh2 = dh3_ref[...] + _rms_bwd(dnw * w_ref[...], n, rstd)
        dh2_ref[...] = dh2
        dmix_ref[...] = (g1_ref[...] * dh2).astype(BF16)
        acc_ref[0:1, :] += _colsum(du * n * w_ref[...])
        acc_ref[1:2, :] += _colsum(du)
        acc_ref[2:3, :] += _colsum(dnw * n)
        acc_ref[3:4, :] += _colsum(dh2 * mix_ref[...])

    return pl.pallas_call(
        body, name="ffn_in_bwd", grid=(R // tr,),
        in_specs=[_row_spec(tr, D)] * 5 + [_vec_spec(D)] * 3,
        out_specs=[_row_spec(tr, D), _row_spec(tr, D), _acc_spec(D)],
        out_shape=[jax.ShapeDtypeStruct((R, D), F32), jax.ShapeDtypeStruct((R, D), BF16), jax.ShapeDtypeStruct((8, D), F32)],
        compiler_params=_cparams(("arbitrary",)),
    )(du_a, du_b, h2, dh3, mix, sc2, g1, w2)


def _norm1_bwd(dh1, xall, dh2, w1, mod, n_lat_tiles, tr):
    R = xall.shape[0]

    def body(dh1_ref, x_ref, dh2_ref, w_ref, mod_ref, dx_ref, acc_ref):
        i = pl.program_id(0)

        @pl.when((i == 0) | (i == n_lat_tiles))
        def _():
            acc_ref[...] = jnp.zeros_like(acc_ref)

        dh1 = dh1_ref[...]
        x = x_ref[...]
        rstd = _rms(x)
        n = x * rstd
        dnw = dh1 * (1.0 + mod_ref[0, 1])

        @pl.when(i < n_lat_tiles)
        def _():
            dx_ref[...] = dh2_ref[...] + _rms_bwd(dnw * w_ref[...], n, rstd)

        acc_ref[0, 0:1, :] += _colsum(dh1 * n * w_ref[...])
        acc_ref[0, 1:2, :] += _colsum(dh1)
        acc_ref[0, 2:3, :] += _colsum(dnw * n)

    sel = lambda i: jnp.where(i >= n_lat_tiles, 1, 0)
    return pl.pallas_call(
        body, name="norm1_bwd", grid=(R // tr,),
        in_specs=[_row_spec(tr, D)] * 3 + [_vec_spec(D), pl.BlockSpec((1, 2, 1, D), lambda i: (sel(i), 0, 0, 0))],
        out_specs=[pl.BlockSpec((tr, D), lambda i: (jnp.minimum(i, n_lat_tiles - 1), 0)),
                   pl.BlockSpec((1, 8, D), lambda i: (sel(i), 0, 0))],
        out_shape=[jax.ShapeDtypeStruct((n_lat_tiles * tr, D), F32), jax.ShapeDtypeStruct((2, 8, D), F32)],
        compiler_params=_cparams(("arbitrary",)),
    )(dh1, xall, dh2, w1, mod)


def _swiglu_act(gp, up, tr):
    R = gp.shape[0]

    def body(g_ref, u_ref, o_ref):
        o_ref[...] = (_silu(_f32(g_ref)) * _f32(u_ref)).astype(BF16)

    return pl.pallas_call(
        body, name="swiglu_act", grid=(R // tr,), in_specs=[_row_spec(tr, D_FF)] * 2, out_specs=_row_spec(tr, D_FF),
        out_shape=jax.ShapeDtypeStruct((R, D_FF), BF16), compiler_params=_cparams(("arbitrary",)),
    )(gp, up)


def _swiglu_act_bwd(da, gp, up, tr):
    R = gp.shape[0]

    def body(da_ref, g_ref, u_ref, dg_ref, du_ref):
        da, g = _f32(da_ref), _f32(g_ref)
        dg_ref[...] = (da * _f32(u_ref) * _dsilu(g)).astype(BF16)
        du_ref[...] = (da * _silu(g)).astype(BF16)

    return pl.pallas_call(
        body, name="swiglu_act_bwd", grid=(R // tr,), in_specs=[_row_spec(tr, D_FF)] * 3, out_specs=[_row_spec(tr, D_FF)] * 2,
        out_shape=[jax.ShapeDtypeStruct((R, D_FF), BF16)] * 2, compiler_params=_cparams(("arbitrary",)),
    )(da, gp, up)


def _merge(ya, yb, parts, tr):
    R = ya.shape[0]

    def body(ya_ref, yb_ref, ga_ref, gb_ref, o_ref):
        o_ref[...] = (_sigmoid(_f32(ga_ref)) * _f32(ya_ref) + _sigmoid(_f32(gb_ref)) * _f32(yb_ref)).astype(BF16)

    return pl.pallas_call(
        body, name="merge", grid=(R // tr,),
        in_specs=[_row_spec(tr, D), _row_spec(tr, D), _row_spec(tr, D, COL_GA // D), _row_spec(tr, D, COL_GB // D)],
        out_specs=_row_spec(tr, D), out_shape=jax.ShapeDtypeStruct((R, D), BF16), compiler_params=_cparams(("arbitrary",)),
    )(ya, yb, parts, parts)


def _dparts_out(tr, w, col, nd=1):
    blk = col // w
    return pl.BlockSpec((tr, w), (lambda i: (i, blk)) if nd == 1 else (lambda i, j: (i, blk + j)))


def _merge_bwd(dm, ya, yb, parts, dparts, tr):
    R = ya.shape[0]

    def body(dm_ref, ya_ref, yb_ref, ga_ref, gb_ref, _, dya_ref, dyb_ref, dg_ref):
        dm = _f32(dm_ref)
        sa, sb = _sigmoid(_f32(ga_ref)), _sigmoid(_f32(gb_ref))
        dya_ref[...] = (dm * sa).astype(BF16)
        dyb_ref[...] = (dm * sb).astype(BF16)
        dg_ref[:, 0:D] = (dm * _f32(ya_ref) * sa * (1.0 - sa)).astype(BF16)
        dg_ref[:, D:2 * D] = (dm * _f32(yb_ref) * sb * (1.0 - sb)).astype(BF16)

    return pl.pallas_call(
        body, name="merge_bwd", grid=(R // tr,),
        in_specs=[_row_spec(tr, D)] * 3 + [_row_spec(tr, D, COL_GA // D), _row_spec(tr, D, COL_GB // D), _ANY],
        out_specs=[_row_spec(tr, D), _row_spec(tr, D), _dparts_out(tr, 2 * D, COL_GA)],
        out_shape=[jax.ShapeDtypeStruct((R, D), BF16)] * 2 + [jax.ShapeDtypeStruct(dparts.shape, BF16)],
        input_output_aliases={5: 2}, compiler_params=_cparams(("arbitrary",)),
    )(dm, ya, yb, parts, parts, dparts)


def _gla_out(o2, parts, gw4, tr):
    R = parts.shape[0]

    def body(of_ref, ob_ref, r_ref, w_ref, out_ref):
        oa = _f32(of_ref) + _f32(ob_ref)
        sr = _silu(_f32(r_ref))
        for h in range(GLA_H):
            s = slice(h * GLA_DV, (h + 1) * GLA_DV)
            o = oa[:, s]
            out_ref[:, s] = (o * _rms(o) * w_ref[:, s] * sr[:, s]).astype(BF16)

    return pl.pallas_call(
        body, name="gla_out", grid=(R // tr,),
        in_specs=[_row_spec(tr, D), _row_spec(tr, D), _row_spec(tr, D, COL_R // D), _vec_spec(D)],
        out_specs=_row_spec(tr, D), out_shape=jax.ShapeDtypeStruct((R, D), BF16), compiler_params=_cparams(("arbitrary",)),
    )(o2[0], o2[1], parts, gw4)


def _gla_out_bwd(dout, o2, parts, gw4, dparts, tr):
    R = parts.shape[0]

    def body(d_ref, of_ref, ob_ref, r_ref, w_ref, _, do_ref, dr_ref, acc_ref):
        _zero_first(acc_ref)
        oa = _f32(of_ref) + _f32(ob_ref)
        r = _f32(r_ref)
        sr = _silu(r)
        dout = _f32(d_ref)
        for h in range(GLA_H):
            s = slice(h * GLA_DV, (h + 1) * GLA_DV)
            o = oa[:, s]
            rstd = _rms(o)
            n = o * rstd
            w = w_ref[:, s]
            dr_ref[:, s] = (dout[:, s] * n * w * _dsilu(r[:, s])).astype(BF16)
            dnw = dout[:, s] * sr[:, s]
            do_ref[:, s] = _rms_bwd(dnw * w, n, rstd).astype(ACT)
            acc_ref[0:1, s] += _colsum(dnw * n)

    return pl.pallas_call(
        body, name="gla_out_bwd", grid=(R // tr,),
        in_specs=[_row_spec(tr, D), _row_spec(tr, D), _row_spec(tr, D), _row_spec(tr, D, COL_R // D), _vec_spec(D), _ANY],
        out_specs=[_row_spec(tr, D), _dparts_out(tr, D, COL_R), _acc_spec(D)],
        out_shape=[jax.ShapeDtypeStruct((R, D), ACT), jax.ShapeDtypeStruct(dparts.shape, BF16), jax.ShapeDtypeStruct((8, D), F32)],
        input_output_aliases={5: 1}, compiler_params=_cparams(("arbitrary",)),
    )(dout, o2[0], o2[1], parts, gw4, dparts)


SSM_GW = SSM_INNER // SSM_G


def _ssd_out(y2, xbc, parts, dskip, nw, tr):
    R = parts.shape[0]

    def body(yf_ref, yb_ref, x_ref, z_ref, ds_ref, w_ref, out_ref):
        ob = (_f32(yf_ref) + _f32(yb_ref) + ds_ref[...] * _f32(x_ref)) * _silu(_f32(z_ref))
        for g in range(SSM_G):
            s = slice(g * SSM_GW, (g + 1) * SSM_GW)
            o = ob[:, s]
            out_ref[:, s] = (o * _rms(o) * w_ref[:, s]).astype(BF16)

    return pl.pallas_call(
        body, name="ssd_out", grid=(R // tr,),
        in_specs=[_row_spec(tr, SSM_INNER)] * 3 + [_row_spec(tr, SSM_INNER, COL_Z // SSM_INNER),
                                                   _vec_spec(SSM_INNER), _vec_spec(SSM_INNER)],
        out_specs=_row_spec(tr, SSM_INNER), out_shape=jax.ShapeDtypeStruct((R, SSM_INNER), BF16),
        compiler_params=_cparams(("arbitrary",)),
    )(y2[0], y2[1], xbc, parts, dskip, nw)


def _ssd_out_bwd(dout, y2, xbc, parts, dskip, nw, dparts, tr):
    R = parts.shape[0]

    def body(d_ref, yf_ref, yb_ref, x_ref, z_ref, ds_ref, w_ref, _, dy_ref, dz_ref, acc_ref):
        _zero_first(acc_ref)
        x, z = _f32(x_ref), _f32(z_ref)
        pre = _f32(yf_ref) + _f32(yb_ref) + ds_ref[...] * x
        sz = _silu(z)
        ob = pre * sz
        dout = _f32(d_ref)
        for g in range(SSM_G):
            s = slice(g * SSM_GW, (g + 1) * SSM_GW)
            o = ob[:, s]
            rstd = _rms(o)
            n = o * rstd
            dob = _rms_bwd(dout[:, s] * w_ref[:, s], n, rstd)
            dz_ref[:, s] = (dob * pre[:, s] * _dsilu(z[:, s])).astype(BF16)
            dy = dob * sz[:, s]
            dy_ref[:, s] = dy.astype(ACT)
            acc_ref[0:1, s] += _colsum(dout[:, s] * n)
            acc_ref[1:2, s] += _colsum(dy * x[:, s])

    return pl.pallas_call(
        body, name="ssd_out_bwd", grid=(R // tr,),
        in_specs=[_row_spec(tr, SSM_INNER)] * 4 + [_row_spec(tr, SSM_INNER, COL_Z // SSM_INNER),
                                                   _vec_spec(SSM_INNER), _vec_spec(SSM_INNER), _ANY],
        out_specs=[_row_spec(tr, SSM_INNER), _dparts_out(tr, SSM_INNER, COL_Z), _acc_spec(SSM_INNER)],
        out_shape=[jax.ShapeDtypeStruct((R, SSM_INNER), ACT), jax.ShapeDtypeStruct(dparts.shape, BF16),
                   jax.ShapeDtypeStruct((8, SSM_INNER), F32)],
        input_output_aliases={7: 1}, compiler_params=_cparams(("arbitrary",)),
    )(dout, y2[0], y2[1], xbc, parts, dskip, nw, dparts)


CONV_W = SSM_INNER + 2 * SSM_G * SSM_N
CONV_BLK = 1024


CONV_SHIFTS = (-2, -1, 1, 2)


def _conv_mask_table(tr):
    t = np.arange(tr)
    table = np.zeros((2, len(CONV_SHIFTS), tr, 128), np.float32)
    for kind, (pos, seg) in enumerate(((t % GRID_W, GRID_W), (t, tr))):
        for k, s in enumerate(CONV_SHIFTS):
            table[kind, k] = ((pos + s >= 0) & (pos + s < seg)).astype(np.float32)[:, None]
    return jnp.asarray(table)


def _shifted(u, s, mask_ref, tr):
    return u if s == 0 else pltpu.roll(u, (-s) % tr, 0) * mask_ref[0, CONV_SHIFTS.index(s)]


def _conv_mask_spec(tr, n_lat_tiles, row_axis):
    return pl.BlockSpec((1, len(CONV_SHIFTS), tr, 128),
                        lambda *ids: (jnp.where(ids[row_axis] >= n_lat_tiles, 1, 0), 0, 0, 0))


def _conv_fwd(parts, cw, cb, n_lat_tiles, tr):
    R = parts.shape[0]

    def body(u_ref, w_ref, b_ref, mask_ref, o_ref):
        def lanes(l, carry):
            sl = pl.ds(pl.multiple_of(l * 128, 128), 128)
            u, w = u_ref[:, sl].astype(F32), w_ref[:, sl]
            acc = jnp.zeros_like(u) + b_ref[:, sl]
            for j in range(SSM_CONV):
                acc = acc + _shifted(u, j - CONV_LEFT, mask_ref, tr) * w[j:j + 1, :]
            o_ref[:, sl] = _silu(acc).astype(ACT)
            return carry

        lax.fori_loop(0, CONV_BLK // 128, lanes, 0)

    return pl.pallas_call(
        body, name="conv_fwd", grid=(R // tr, CONV_W // CONV_BLK),
        in_specs=[pl.BlockSpec((tr, CONV_BLK), lambda i, j: (i, COL_XBC // CONV_BLK + j)),
                  pl.BlockSpec((SSM_CONV, CONV_BLK), lambda i, j: (0, j)), pl.BlockSpec((1, CONV_BLK), lambda i, j: (0, j)),
                  _conv_mask_spec(tr, n_lat_tiles, 0)],
        out_specs=pl.BlockSpec((tr, CONV_BLK), lambda i, j: (i, j)), out_shape=jax.ShapeDtypeStruct((R, CONV_W), ACT),
        compiler_params=_cparams(("arbitrary", "arbitrary")),
    )(parts, cw, cb, _conv_mask_table(tr))


def _conv_bwd(dx, db, dc, dy, dskip, parts, cw, cb, dparts, n_lat_tiles, tr):
    R = parts.shape[0]
    half = CONV_BLK // 2
    n_x = SSM_INNER // CONV_BLK

    def body(dxf_ref, dxb_ref, dy_ref, ds_ref, dbf_ref, dbb_ref, dcf_ref, dcb_ref, u_ref, w_ref, b_ref, _, mask_ref,
             du_ref, acc_ref, d_scr):
        @pl.when(pl.program_id(1) == 0)
        def _():
            acc_ref[...] = jnp.zeros_like(acc_ref)

        @pl.when(pl.program_id(0) < n_x)
        def _():
            d_scr[...] = dxf_ref[...] + dxb_ref[...] + _f32(dy_ref) * ds_ref[...]

        @pl.when(pl.program_id(0) >= n_x)
        def _():
            d_scr[:, 0:half] = dbf_ref[...] + dbb_ref[...]
            d_scr[:, half:] = dcf_ref[...] + dcb_ref[...]

        def lanes(l, carry):
            sl = pl.ds(pl.multiple_of(l * 128, 128), 128)
            u, w = u_ref[:, sl].astype(F32), w_ref[:, sl]
            pre = jnp.zeros_like(u) + b_ref[:, sl]
            taps = []
            for j in range(SSM_CONV):
                tap = _shifted(u, j - CONV_LEFT, mask_ref, tr)
                taps.append(tap)
                pre = pre + tap * w[j:j + 1, :]
            dpre = d_scr[:, sl] * _dsilu(pre)
            du = jnp.zeros_like(u)
            sums = []
            for j in range(SSM_CONV):
                sums.append(_colsum(dpre * taps[j]))
                du = du + _shifted(dpre, CONV_LEFT - j, mask_ref, tr) * w[j:j + 1, :]
            sums += [_colsum(dpre), jnp.zeros((8 - SSM_CONV - 1, 128), F32)]
            acc_ref[:, sl] += jnp.concatenate(sums, axis=0)
            du_ref[:, sl] = du.astype(BF16)
            return carry

        lax.fori_loop(0, CONV_BLK // 128, lanes, 0)

    return pl.pallas_call(
        body, name="conv_bwd", grid=(CONV_W // CONV_BLK, R // tr),
        in_specs=[pl.BlockSpec((tr, CONV_BLK), lambda j, i: (jnp.where(j < n_x, i, 0), jnp.minimum(j, n_x - 1)))] * 3
        + [pl.BlockSpec((1, CONV_BLK), lambda j, i: (0, jnp.minimum(j, n_x - 1)))]
        + [pl.BlockSpec((tr, half), lambda j, i: (jnp.where(j < n_x, 0, i), 0))] * 4
        + [pl.BlockSpec((tr, CONV_BLK), lambda j, i: (i, COL_XBC // CONV_BLK + j)),
           pl.BlockSpec((SSM_CONV, CONV_BLK), lambda j, i: (0, j)), pl.BlockSpec((1, CONV_BLK), lambda j, i: (0, j)), _ANY,
           _conv_mask_spec(tr, n_lat_tiles, 1)],
        out_specs=[pl.BlockSpec((tr, CONV_BLK), lambda j, i: (i, COL_XBC // CONV_BLK + j)),
                   pl.BlockSpec((8, CONV_BLK), lambda j, i: (0, j))],
        out_shape=[jax.ShapeDtypeStruct(dparts.shape, BF16), jax.ShapeDtypeStruct((8, CONV_W), F32)],
        scratch_shapes=[pltpu.VMEM((tr, CONV_BLK), F32)],
        input_output_aliases={11: 0}, compiler_params=_cparams(("arbitrary", "arbitrary")),
    )(*dx, dy, dskip, *db, *dc, parts, cw, cb, dparts, _conv_mask_table(tr))


def _chunk_row_block(d, i, n_lat, n_ctx):
    fwd = jnp.where(i < n_ctx, n_lat + i, i - n_ctx)
    rev = n_lat + n_ctx - 1 - i
    if isinstance(d, int):
        return rev if d else fwd
    return jnp.where(d == 0, fwd, rev)


def _tri(n, d, transpose=False):
    row = lax.broadcasted_iota(jnp.int32, (n, n), 0)
    col = lax.broadcasted_iota(jnp.int32, (n, n), 1)
    diff = (col - row) if transpose else (row - col)
    return diff * (1 - 2 * d) >= 0


def _gla_gates(sm, uhi, ulo, bias, d):
    pre = _nn3(sm, uhi, ulo) + bias
    g = _log_sigmoid(pre) * (1.0 / GLA_TAU)
    mask = _tri(GLA_C, d)
    b = _x_nn(mask.astype(BF16), g)
    b_tot = _colsum(g)
    b_ref = b[GLA_C // 2:GLA_C // 2 + 1, :]
    e_q = jnp.exp(jnp.minimum(b - b_ref, EXP_CLAMP))
    e_k = jnp.exp(jnp.minimum(b_ref - b, EXP_CLAMP))
    return pre, mask, b_tot, e_q, e_k, jnp.exp(b), jnp.exp(b_tot - b)


GLA_QK = GLA_H * GLA_DK
GLA_V = GLA_H * GLA_DV


def _gla_specs(n_lat, n_ctx, step_of):
    rbs = [lambda i, d=d: _chunk_row_block(d, step_of(i), n_lat, n_ctx) for d in range(2)]
    specs = []
    for rb in rbs:
        specs += [pl.BlockSpec((GLA_C, GLA_QK), lambda i, rb=rb: (rb(i), COL_Q // GLA_QK)),
                  pl.BlockSpec((GLA_C, GLA_QK), lambda i, rb=rb: (rb(i), COL_K // GLA_QK)),
                  pl.BlockSpec((GLA_C, GLA_V), lambda i, rb=rb: (rb(i), COL_V // GLA_V)),
                  pl.BlockSpec((GLA_C, 128), lambda i, rb=rb: (rb(i), 0))]
    specs += [pl.BlockSpec((2, 128, GLA_QK), lambda i: (0, 0, 0)), pl.BlockSpec((2, 128, GLA_QK), lambda i: (0, 0, 0)),
              pl.BlockSpec((2, 1, GLA_QK), lambda i: (0, 0, 0))]
    return specs, rbs


ACT = BF16


def _f32(ref_or_value):
    return ref_or_value[...].astype(F32)


def _gla_fwd(parts, sm, uhi, ulo, bias, n_lat, n_ctx):
    R = parts.shape[0]
    n_steps = n_lat + n_ctx
    scale = GLA_DK ** -0.5

    def body(*refs):
        ins, (uhi_ref, ulo_ref, bias_ref), o_refs, hist_ref, st = refs[:8], refs[8:11], refs[11:13], refs[13], refs[14]

        @pl.when(pl.program_id(0) == 0)
        def _():
            st[...] = jnp.zeros_like(st)

        for d in range(2):
            q_ref, k_ref, v_ref, sm_ref = ins[4 * d:4 * d + 4]
            _, mask, b_tot, e_q, e_k, e_in, e_out = _gla_gates(sm_ref[...], uhi_ref[d], ulo_ref[d], bias_ref[d], d)
            q, k, v = _f32(q_ref) * scale, _f32(k_ref), _bf(v_ref[...])
            qb, kb, q_in, k_out, decay = _bf(q * e_q), _bf(k * e_k), _bf(q * e_in), _bf(k * e_out), jnp.exp(b_tot)
            for h in range(GLA_H):
                sk, sv = slice(h * GLA_DK, (h + 1) * GLA_DK), slice(h * GLA_DV, (h + 1) * GLA_DV)
                att = jnp.where(mask, _nt(qb[:, sk], kb[:, sk]), 0.0)
                s_in = st[d, h]
                hist_ref[d, 0, h] = s_in
                o_refs[d][:, sv] = (_nn(_bf(att), v[:, sv]) + _nt(q_in[:, sk], _bf(s_in))).astype(ACT)
                st[d, h] = decay[:, sk] * s_in + _tn(v[:, sv], k_out[:, sk])

    in_specs, rbs = _gla_specs(n_lat, n_ctx, lambda i: i)
    return pl.pallas_call(
        body, name="gla_fwd", grid=(n_steps,), in_specs=in_specs,
        out_specs=[pl.BlockSpec((GLA_C, GLA_V), lambda i, rb=rb: (rb(i), 0)) for rb in rbs]
        + [pl.BlockSpec((2, 1, GLA_H, GLA_DV, GLA_DK), lambda i: (0, i, 0, 0, 0))],
        out_shape=[jax.ShapeDtypeStruct((R, GLA_V), ACT)] * 2 + [jax.ShapeDtypeStruct((2, n_steps, GLA_H, GLA_DV, GLA_DK), F32)],
        scratch_shapes=[pltpu.VMEM((2, GLA_H, GLA_DV, GLA_DK), F32)],
        compiler_params=_cparams(("arbitrary",)),
    )(*([parts, parts, parts, sm] * 2), uhi, ulo, bias)


def _gla_bwd(do, parts, sm, uhi, ulo, bias, hist, n_lat, n_ctx):
    R = parts.shape[0]
    n_steps = n_lat + n_ctx
    scale = GLA_DK ** -0.5
    step_of = lambda j: n_steps - 1 - j

    def body(*refs):
        ins, (uhi_ref, ulo_ref, bias_ref), do_refs, hist_ref = refs[:8], refs[8:11], refs[11:13], refs[13]
        outs, dst = refs[14:22], refs[22]

        @pl.when(pl.program_id(0) == 0)
        def _():
            dst[...] = jnp.zeros_like(dst)

        for d in range(2):
            q_ref, k_ref, v_ref, sm_ref = ins[4 * d:4 * d + 4]
            dq_ref, dk_ref, dv_ref, dp_ref = outs[4 * d:4 * d + 4]
            pre, mask, b_tot, e_q, e_k, e_in, e_out = _gla_gates(sm_ref[...], uhi_ref[d], ulo_ref[d], bias_ref[d], d)
            q, k, v = _f32(q_ref) * scale, _f32(k_ref), _bf(v_ref[...])
            dout = _bf(do_refs[d][...])
            k_out_f = k * e_out
            qb, kb, q_in, k_out, decay = _bf(q * e_q), _bf(k * e_k), _bf(q * e_in), _bf(k_out_f), jnp.exp(b_tot)
            dqs, dks, dk_outs, dss = [], [], [], []
            for h in range(GLA_H):
                sk, sv = slice(h * GLA_DK, (h + 1) * GLA_DK), slice(h * GLA_DV, (h + 1) * GLA_DV)
                s_in, ds = hist_ref[d, 0, h], dst[d, h]
                att = jnp.where(mask, _nt(qb[:, sk], kb[:, sk]), 0.0)
                datt = _bf(jnp.where(mask, _nt(dout[:, sv], v[:, sv]), 0.0))
                dv_ref[:, sv] = _tn(_bf(att), dout[:, sv]) + _nt(k_out[:, sk], _bf(ds))
                dqs.append(_nn(datt, kb[:, sk]) * e_q[:, sk] + _nn(dout[:, sv], _bf(s_in)) * e_in[:, sk])
                dk_o = _nn(v[:, sv], _bf(ds))
                dk_outs.append(dk_o)
                dks.append(_tn(datt, qb[:, sk]) * e_k[:, sk])
                dss.append(_colsum(ds * s_in))
                dst[d, h] = decay[:, sk] * ds + _tn(dout[:, sv], q_in[:, sk])
            dq, dk_out = jnp.concatenate(dqs, axis=1), jnp.concatenate(dk_outs, axis=1)
            dk = jnp.concatenate(dks, axis=1) + dk_out * e_out
            dq_ref[...] = dq * scale
            dk_ref[...] = dk
            db_tot = _colsum(dk_out * k_out_f) + decay * jnp.concatenate(dss, axis=1)
            dg = _x_nn(_tri(GLA_C, d, transpose=True).astype(BF16), dq * q - dk * k) + db_tot
            dp_ref[...] = dg * (1.0 / GLA_TAU) * _sigmoid(-pre)

    in_specs, rbs = _gla_specs(n_lat, n_ctx, step_of)
    in_specs += [pl.BlockSpec((GLA_C, GLA_V), lambda j, rb=rb: (rb(j), 0)) for rb in rbs]
    in_specs += [pl.BlockSpec((2, 1, GLA_H, GLA_DV, GLA_DK), lambda j: (0, step_of(j), 0, 0, 0))]
    out_specs, out_shape = [], []
    for rb in rbs:
        for w in (GLA_QK, GLA_QK, GLA_V, GLA_QK):
            out_specs.append(pl.BlockSpec((GLA_C, w), lambda j, rb=rb: (rb(j), 0)))
            out_shape.append(jax.ShapeDtypeStruct((R, w), F32))
    outs = pl.pallas_call(
        body, name="gla_bwd", grid=(n_steps,), in_specs=in_specs, out_specs=out_specs, out_shape=out_shape,
        scratch_shapes=[pltpu.VMEM((2, GLA_H, GLA_DV, GLA_DK), F32)],
        compiler_params=_cparams(("arbitrary",)),
    )(*([parts, parts, parts, sm] * 2), uhi, ulo, bias, do, do, hist)
    return [(outs[k], outs[4 + k]) for k in range(4)]


def _ssd_consts(dt_bias, a_log):
    sel = np.zeros((2, SSM_G, 128, 128), np.float32)
    for d, base in enumerate((SM_DTF, SM_DTB)):
        for g in range(SSM_G):
            for e in range(SSM_HPG):
                sel[d, g, base + SSM_HPG * g + e, e] = 1.0
    e512 = np.zeros((128, SSM_GW), np.float32)
    for e in range(SSM_HPG):
        e512[e, SSM_P * e:SSM_P * (e + 1)] = 1.0
    a_neg = -jnp.exp(a_log)
    pad = lambda v: jnp.pad(v.reshape(2, SSM_G, 1, SSM_HPG), ((0, 0), (0, 0), (0, 0), (0, 128 - SSM_HPG)))
    return dict(
        sel=jnp.asarray(sel, BF16), sel_t=jnp.asarray(sel.transpose(0, 1, 3, 2), BF16), e512_t=jnp.asarray(e512.T, BF16),
        dtb=pad(dt_bias), a=pad(a_neg), a512=jnp.repeat(a_neg, SSM_P, axis=1).reshape(2, SSM_G, 1, SSM_GW))


def _head_columns(x8):
    return [jnp.broadcast_to(x8[:, e:e + 1], (x8.shape[0], 128)) for e in range(SSM_HPG)]


def _head_layout(cols):
    low = lax.broadcasted_iota(jnp.int32, (1, 128), 1) < SSM_P
    return jnp.concatenate([jnp.where(low, cols[2 * j], cols[2 * j + 1]) for j in range(SSM_HPG // 2)], axis=1)


def _ssd_common(sm, sel, dtb, a_neg, a512, d):
    dtr8 = _nn_x(sm, sel) + dtb
    dt8 = _softplus(dtr8)
    a8 = a_neg * dt8
    mask = _tri(SSM_C, d)
    mask_t = _tri(SSM_C, d, transpose=True).astype(BF16)
    cum8 = _x_nn(mask.astype(BF16), a8)
    a_hi, a_lo = _split(a8)
    cum_t = _tn(a_hi, mask_t) + _tn(a_lo, mask_t)
    cum_cols = _head_columns(cum8)
    dt_exp = _head_layout(_head_columns(dt8))
    a_exp = a512 * dt_exp
    return dict(dtr8=dtr8, a8=a8, mask=mask, mask_t=mask_t, cum_t=cum_t, dt_exp=dt_exp, a_exp=a_exp,
                cum_exp=_head_layout(cum_cols), cum_cols=cum_cols, tot_exp=_colsum(a_exp))


def _ssd_decay(cm, e):
    diff = cm["cum_cols"][e] - cm["cum_t"][e:e + 1, :]
    return jnp.where(cm["mask"], jnp.exp(jnp.minimum(diff, 0.0)), 0.0)


SSM_GPS = 4


def _ssd_specs(n_lat, n_ctx, step_of):
    rbs = [lambda i, d=d: _chunk_row_block(d, step_of(i), n_lat, n_ctx) for d in range(2)]
    xw, nw = SSM_GPS * SSM_GW, SSM_GPS * SSM_N
    specs = []
    for rb in rbs:
        specs += [pl.BlockSpec((SSM_C, xw), lambda g, i, rb=rb: (rb(i), g)),
                  pl.BlockSpec((SSM_C, nw), lambda g, i, rb=rb: (rb(i), SSM_INNER // nw + g)),
                  pl.BlockSpec((SSM_C, nw), lambda g, i, rb=rb: (rb(i), (SSM_INNER + SSM_G * SSM_N) // nw + g)),
                  pl.BlockSpec((SSM_C, 128), lambda g, i, rb=rb: (rb(i), 0))]
    specs += [pl.BlockSpec((2, SSM_GPS, 128, 128), lambda g, i: (0, g, 0, 0)),
              pl.BlockSpec((2, SSM_GPS, 1, 128), lambda g, i: (0, g, 0, 0)),
              pl.BlockSpec((2, SSM_GPS, 1, 128), lambda g, i: (0, g, 0, 0)),
              pl.BlockSpec((2, SSM_GPS, 1, SSM_GW), lambda g, i: (0, g, 0, 0))]
    return specs, rbs


def _ssd_fwd(xbc, sm, k, n_lat, n_ctx, comm):
    R = xbc.shape[0]
    n_steps = n_lat + n_ctx

    def body(*refs):
        ins, (sel_ref, dtb_ref, a_ref, a512_ref), y_refs, hist_ref, st = refs[:8], refs[8:12], refs[12:14], refs[14], refs[15]

        @pl.when(pl.program_id(1) == 0)
        def _():
            st[...] = jnp.zeros_like(st)

        for d in range(2):
            x_ref, b_ref, c_ref, sm_ref = ins[4 * d:4 * d + 4]
            sm = sm_ref[...]
            for gg in range(SSM_GPS):
                sx, sn = slice(gg * SSM_GW, (gg + 1) * SSM_GW), slice(gg * SSM_N, (gg + 1) * SSM_N)
                cm = _ssd_common(sm, sel_ref[d, gg], dtb_ref[d, gg], a_ref[d, gg], a512_ref[d, gg], d)
                bm, cmat = _bf(b_ref[:, sn]), _bf(c_ref[:, sn])
                xdt = x_ref[:, sx].astype(F32) * cm["dt_exp"]
                cb = _nt(cmat, bm)
                ys = [_nn(_bf(cb * _ssd_decay(cm, e)), _bf(xdt[:, SSM_P * e:SSM_P * (e + 1)])) for e in range(SSM_HPG)]
                s_in = st[d, gg]
                hist_ref[d, 0, gg] = s_in
                y = jnp.concatenate(ys, axis=1) + jnp.exp(cm["cum_exp"]) * _nn(cmat, _bf(s_in))
                y_refs[d][:, sx] = y.astype(ACT)
                st[d, gg] = jnp.exp(cm["tot_exp"]) * s_in + _tn(bm, _bf(xdt * jnp.exp(cm["tot_exp"] - cm["cum_exp"])))

    in_specs, rbs = _ssd_specs(n_lat, n_ctx, lambda i: i)
    out_specs = [pl.BlockSpec((SSM_C, SSM_GPS * SSM_GW), lambda g, i, rb=rb: (rb(i), g)) for rb in rbs]
    out_specs += [pl.BlockSpec((2, 1, SSM_GPS, SSM_N, SSM_GW), lambda g, i: (0, i, g, 0, 0))]
    out_shape = [jax.ShapeDtypeStruct((R, SSM_INNER), ACT)] * 2 + [jax.ShapeDtypeStruct((2, n_steps, SSM_G, SSM_N, SSM_GW), F32)]
    args = [xbc, xbc, xbc, sm] * 2 + [k["sel"], k["dtb"], k["a"], k["a512"]]
    n_host_out = len(out_shape)
    outs = pl.pallas_call(
        _hosted(body, len(args), n_host_out, 1, comm, *_ssd_comm_steps(n_steps)), name="ssd_fwd",
        grid=(SSM_G // SSM_GPS, n_steps), in_specs=in_specs + [_ANY] * len(comm.arrays),
        out_specs=out_specs + [_ANY] * len(comm.out_shape), out_shape=out_shape + comm.out_shape,
        scratch_shapes=[pltpu.VMEM((2, SSM_GPS, SSM_N, SSM_GW), F32)] + comm.scratch,
        compiler_params=_cparams(("arbitrary", "arbitrary")),
    )(*args, *comm.arrays)
    return outs[:n_host_out], outs[n_host_out:]


def _ssd_comm_steps(n_steps):
    n_g = SSM_G // SSM_GPS
    at = lambda g, i: (pl.program_id(0) == g) & (pl.program_id(1) == i)
    half = (n_g // 2, n_steps // 2 if n_g % 2 else 0)
    return (lambda: at(0, 0)), (lambda: at(*half)), (lambda: at(n_g - 1, n_steps - 1))


def _ssd_bwd(dy, xbc, sm, k, hist, n_lat, n_ctx, comm):
    R = xbc.shape[0]
    n_steps = n_lat + n_ctx
    step_of = lambda j: n_steps - 1 - j

    def one(d, gg, x_ref, b_ref, c_ref, sm_ref, sel_ref, dtb_ref, a_ref, a512_ref, selt_ref, e512t_ref, dy_ref,
            hist_ref, dx_ref, db_ref, dc_ref, dsm_ref, acc_ref, dst):
        sx, sn = slice(gg * SSM_GW, (gg + 1) * SSM_GW), slice(gg * SSM_N, (gg + 1) * SSM_N)
        a_neg, e512_t = a_ref[d, gg], e512t_ref[...]
        cm = _ssd_common(sm_ref[...], sel_ref[d, gg], dtb_ref[d, gg], a_neg, a512_ref[d, gg], d)
        x, dyv = x_ref[:, sx].astype(F32), dy_ref[:, sx].astype(F32)
        bm, cmat = _bf(b_ref[:, sn]), _bf(c_ref[:, sn])
        xdt = x * cm["dt_exp"]
        cb = _nt(cmat, bm)
        s_in, ds = hist_ref[d, 0, gg], dst[d, gg]
        w = jnp.exp(cm["tot_exp"] - cm["cum_exp"])
        z = _nn(bm, _bf(ds))
        decay_in = jnp.exp(cm["cum_exp"])
        gy = _bf(dyv * decay_in)
        dcb = jnp.zeros((SSM_C, SSM_C), F32)
        dxs, crossing = [], []
        row = lax.broadcasted_iota(jnp.int32, (SSM_C, SSM_C), 0)
        col = lax.broadcasted_iota(jnp.int32, (SSM_C, SSM_C), 1)
        eye = (row == col).astype(BF16)
        before = (cm["mask_t"] - eye)
        for e in range(SSM_HPG):
            s = slice(SSM_P * e, SSM_P * (e + 1))
            lm = _ssd_decay(cm, e)
            dy_e = _bf(dyv[:, s])
            m_e = cb * lm
            dm_e = _nt(dy_e, _bf(xdt[:, s]))
            dcb = dcb + dm_e * lm
            dxs.append(_tn(_bf(m_e), dy_e))
            through = jnp.where(cm["mask"], _nn(_bf(dm_e * m_e), before), 0.0)
            crossing.append(_colsum(through))
        da_rows = jnp.concatenate(crossing + [jnp.zeros((128 - SSM_HPG, SSM_C), F32)], axis=0)
        r_hi, r_lo = _split(da_rows)
        da8_intra = _tn(r_hi, eye) + _tn(r_lo, eye)
        dx_state = w * z
        dxdt = jnp.concatenate(dxs, axis=1) + dx_state
        dcb = _bf(dcb)
        c_s = _nn(cmat, _bf(s_in))
        dc_ref[:, sn] = _nn(dcb, bm) + _nt(gy, _bf(s_in))
        db_ref[:, sn] = _tn(dcb, cmat) + _nt(_bf(w * xdt), _bf(ds))
        dst[d, gg] = jnp.exp(cm["tot_exp"]) * ds + _tn(cmat, gy)
        state_path = xdt * dx_state
        per_token = _nn_x(jnp.concatenate([dyv * decay_in * c_s - state_path, dxdt * x], axis=0), e512_t)
        totals = jnp.concatenate([_colsum(state_path), _colsum(ds * s_in), jnp.zeros((6, SSM_GW), F32)], axis=0)
        totals = _nn_x(totals, e512_t)
        tot8 = _colsum(cm["a8"])
        dtot8 = totals[0:1] + jnp.exp(tot8) * totals[1:2]
        da8 = da8_intra + _x_nn(cm["mask_t"], per_token[:SSM_C]) + dtot8
        ddt8 = da8 * a_neg + per_token[SSM_C:]
        dsm_ref[gg] = _nn_x(ddt8 * _sigmoid(cm["dtr8"]), selt_ref[d, gg])
        dx_ref[:, sx] = dxdt * cm["dt_exp"]
        acc_ref[d, gg, 0:1, :] += _colsum(da8 * cm["a8"])

    def body(*refs):
        ins, consts, (selt_ref, e512t_ref), dy_refs, hist_ref = refs[:8], refs[8:12], refs[12:14], refs[14:16], refs[16]
        outs, acc_ref, dst = refs[17:25], refs[25], refs[26]

        @pl.when(pl.program_id(1) == 0)
        def _():
            dst[...] = jnp.zeros_like(dst)
            acc_ref[...] = jnp.zeros_like(acc_ref)

        for d in range(2):
            for gg in range(SSM_GPS):
                one(d, gg, *ins[4 * d:4 * d + 4], *consts, selt_ref, e512t_ref, dy_refs[d], hist_ref,
                    *outs[4 * d:4 * d + 4], acc_ref, dst)

    xw, nw = SSM_GPS * SSM_GW, SSM_GPS * SSM_N
    in_specs, rbs = _ssd_specs(n_lat, n_ctx, step_of)
    in_specs += [pl.BlockSpec((2, SSM_GPS, 128, 128), lambda g, j: (0, g, 0, 0)), pl.BlockSpec((SSM_GW, 128), lambda g, j: (0, 0))]
    in_specs += [pl.BlockSpec((SSM_C, xw), lambda g, j, rb=rb: (rb(j), g)) for rb in rbs]
    in_specs += [pl.BlockSpec((2, 1, SSM_GPS, SSM_N, SSM_GW), lambda g, j: (0, step_of(j), g, 0, 0))]
    out_specs, out_shape = [], []
    for rb in rbs:
        out_specs += [pl.BlockSpec((SSM_C, xw), lambda g, j, rb=rb: (rb(j), g)),
                      pl.BlockSpec((SSM_C, nw), lambda g, j, rb=rb: (rb(j), g)),
                      pl.BlockSpec((SSM_C, nw), lambda g, j, rb=rb: (rb(j), g)),
                      pl.BlockSpec((SSM_GPS, SSM_C, 128), lambda g, j, rb=rb: (g, rb(j), 0))]
        out_shape += [jax.ShapeDtypeStruct((R, SSM_INNER), F32), jax.ShapeDtypeStruct((R, SSM_G * SSM_N), F32),
                      jax.ShapeDtypeStruct((R, SSM_G * SSM_N), F32), jax.ShapeDtypeStruct((SSM_G, R, 128), F32)]
    out_specs.append(pl.BlockSpec((2, SSM_GPS, 8, 128), lambda g, j: (0, g, 0, 0)))
    out_shape.append(jax.ShapeDtypeStruct((2, SSM_G, 8, 128), F32))
    args = [xbc, xbc, xbc, sm] * 2 + [k["sel"], k["dtb"], k["a"], k["a512"], k["sel_t"], k["e512_t"], dy, dy, hist]
    n_host_out = len(out_shape)
    outs = pl.pallas_call(
        _hosted(body, len(args), n_host_out, 1, comm, *_ssd_comm_steps(n_steps)), name="ssd_bwd",
        grid=(SSM_G // SSM_GPS, n_steps), in_specs=in_specs + [_ANY] * len(comm.arrays),
        out_specs=out_specs + [_ANY] * len(comm.out_shape), out_shape=out_shape + comm.out_shape,
        scratch_shapes=[pltpu.VMEM((2, SSM_GPS, SSM_N, SSM_GW), F32)] + comm.scratch,
        compiler_params=_cparams(("arbitrary", "arbitrary")),
    )(*args, *comm.arrays)
    return [(outs[n], outs[4 + n]) for n in range(4)] + [outs[8]], outs[n_host_out:]


def _gla_assemble(dq, dk, dv, dparts, tr):
    R = dq[0].shape[0]
    qk = GLA_H * GLA_DK

    def body(dqf_ref, dqb_ref, dkf_ref, dkb_ref, dvf_ref, dvb_ref, _, o_ref):
        o_ref[:, 0:qk] = (dqf_ref[...] + dqb_ref[...]).astype(BF16)
        o_ref[:, qk:2 * qk] = (dkf_ref[...] + dkb_ref[...]).astype(BF16)
        o_ref[:, 2 * qk:] = (dvf_ref[...] + dvb_ref[...]).astype(BF16)

    return pl.pallas_call(
        body, name="gla_assemble", grid=(R // tr,), in_specs=[_row_spec(tr, qk)] * 4 + [_row_spec(tr, D)] * 2 + [_ANY],
        out_specs=_dparts_out(tr, 2 * D, COL_Q), out_shape=jax.ShapeDtypeStruct(dparts.shape, BF16),
        input_output_aliases={6: 0}, compiler_params=_cparams(("arbitrary",)),
    )(*dq, *dk, *dv, dparts)


def _small_assemble(dp, dsm, sm, ut_hi, ut_lo, dparts, tr):
    R = sm.shape[0]
    qk = GLA_H * GLA_DK

    def body(dpf_ref, dpb_ref, dsmf_ref, dsmb_ref, sm_ref, uth_ref, utl_ref, _, o_ref, dup_ref, acc_ref, acc2_ref):
        @pl.when(pl.program_id(0) == 0)
        def _():
            dup_ref[...] = jnp.zeros_like(dup_ref)
            acc_ref[...] = jnp.zeros_like(acc_ref)
            acc2_ref[...] = jnp.zeros_like(acc2_ref)

        ssd = dsmf_ref[0] + dsmb_ref[0]
        for g in range(1, SSM_G):
            ssd = ssd + (dsmf_ref[g] + dsmb_ref[g])
        acc2_ref[0:1, :] += _colsum(ssd)
        sm_hi, sm_lo = _split(sm_ref[...])
        out = ssd
        for d, dp_ref in enumerate((dpf_ref, dpb_ref)):
            dpd = dp_ref[...]
            out = out + _nn3(dpd, uth_ref[d], utl_ref[d])
            p_hi, p_lo = _split(dpd)
            dup_ref[d] += _tn(sm_hi, p_hi) + _tn(sm_lo, p_hi) + _tn(sm_hi, p_lo)
            acc_ref[d:d + 1, :] += _colsum(dpd)
        o_ref[...] = out.astype(BF16)

    return pl.pallas_call(
        body, name="small_assemble", grid=(R // tr,),
        in_specs=[_row_spec(tr, qk)] * 2 + [pl.BlockSpec((SSM_G, tr, 128), lambda i: (0, i, 0))] * 2
        + [_row_spec(tr, 128), pl.BlockSpec((2, qk, 128), lambda i: (0, 0, 0)),
           pl.BlockSpec((2, qk, 128), lambda i: (0, 0, 0)), _ANY],
        out_specs=[_dparts_out(tr, 128, COL_SM), pl.BlockSpec((2, 128, qk), lambda i: (0, 0, 0)), _acc_spec(qk), _acc_spec(128)],
        out_shape=[jax.ShapeDtypeStruct(dparts.shape, BF16), jax.ShapeDtypeStruct((2, 128, qk), F32),
                   jax.ShapeDtypeStruct((8, qk), F32), jax.ShapeDtypeStruct((8, 128), F32)],
        input_output_aliases={7: 0}, compiler_params=_cparams(("arbitrary",)),
    )(*dp, *dsm, sm, ut_hi, ut_lo, dparts)


ADA_ROWS = 16
ADA_TILE = 512


def _dot3_f32(a, b, ca, cb):
    a_hi, a_lo = _split(a)
    b_hi, b_lo = _split(b)
    return _dg(a_hi, b_hi, ca, cb) + _dg(a_lo, b_hi, ca, cb) + _dg(a_hi, b_lo, ca, cb)


def _ada_fwd(cvec, w, b):
    cols = w.shape[1]

    def body(c_ref, w_ref, b_ref, o_ref):
        o_ref[...] = _dot3_f32(_silu(c_ref[...]), w_ref[...], 1, 0) + b_ref[...]

    return pl.pallas_call(
        body, name="ada_fwd", grid=(cols // ADA_TILE,),
        in_specs=[pl.BlockSpec((ADA_ROWS, D), lambda j: (0, 0)), pl.BlockSpec((D, ADA_TILE), lambda j: (0, j)),
                  pl.BlockSpec((1, ADA_TILE), lambda j: (0, j))],
        out_specs=pl.BlockSpec((ADA_ROWS, ADA_TILE), lambda j: (0, j)), out_shape=jax.ShapeDtypeStruct((ADA_ROWS, cols), F32),
        compiler_params=_cparams(("arbitrary",)),
    )(cvec, w, b)


def _adam(w, g, m, v):
    m2 = ADAM_B1 * m + (1.0 - ADAM_B1) * g
    v2 = ADAM_B2 * v + (1.0 - ADAM_B2) * (g * g)
    m_hat = m2 / (1.0 - ADAM_B1 ** ADAM_STEP)
    v_hat = v2 / (1.0 - ADAM_B2 ** ADAM_STEP)
    return -ADAM_LR * (m_hat / (jnp.sqrt(v_hat) + ADAM_EPS) + ADAM_WD * w), m2, v2


def _wada_bwd_adam(cvec, dada, w, m, v):
    rows, cols = w.shape
    tr = _tile(rows, 256, 128)

    def body(c_ref, d_ref, w_ref, m_ref, v_ref, g_ref, dl_ref, m2_ref, v2_ref, p_ref):
        wv = w_ref[...]
        g = _dot3_f32(_silu(c_ref[...]), d_ref[...], 0, 0)
        g_ref[...] = g
        dl_ref[...], m2_ref[...], v2_ref[...] = _adam(wv, g, m_ref[...], v_ref[...])
        p_ref[...] = _dot3_f32(d_ref[...], wv, 1, 1)

    blk = pl.BlockSpec((tr, cols), lambda i: (i, 0))
    return pl.pallas_call(
        body, name="wada_bwd_adam", grid=(rows // tr,),
        in_specs=[pl.BlockSpec((ADA_ROWS, tr), lambda i: (0, i)), pl.BlockSpec((ADA_ROWS, cols), lambda i: (0, 0)), blk, blk, blk],
        out_specs=[blk, blk, blk, blk, pl.BlockSpec((ADA_ROWS, tr), lambda i: (0, i))],
        out_shape=[jax.ShapeDtypeStruct((rows, cols), F32)] * 4 + [jax.ShapeDtypeStruct((ADA_ROWS, rows), F32)],
        compiler_params=_cparams(("arbitrary",)),
    )(cvec, dada, w, m, v)


def _reduce_adam(parts8, w, m, v, name):
    rows, cols = w.shape
    tr = _tile(rows, 64, 16)

    def body(p_ref, w_ref, m_ref, v_ref, g_ref, dl_ref, m2_ref, v2_ref):
        g = p_ref[0].astype(F32) + p_ref[N_CHIPS].astype(F32)
        for j in range(1, N_CHIPS):
            g = g + (p_ref[j].astype(F32) + p_ref[N_CHIPS + j].astype(F32))
        g_ref[...] = g
        dl_ref[...], m2_ref[...], v2_ref[...] = _adam(w_ref[...], g, m_ref[...], v_ref[...])

    blk = pl.BlockSpec((tr, cols), lambda i: (i, 0))
    return pl.pallas_call(
        body, name=name, grid=(rows // tr,), in_specs=[pl.BlockSpec((N_DEV, tr, cols), lambda i: (0, i, 0)), blk, blk, blk],
        out_specs=[blk] * 4, out_shape=[jax.ShapeDtypeStruct((rows, cols), F32)] * 4, compiler_params=_cparams(("arbitrary",)),
    )(parts8, w, m, v)


SMALL_W = 1024


def _sum8(g8):
    rows = g8.shape[1]

    def body(g_ref, o_ref):
        s = g_ref[0]
        for j in range(1, N_DEV):
            s = s + g_ref[j]
        o_ref[...] = s

    return pl.pallas_call(
        body, name="sum8", out_shape=jax.ShapeDtypeStruct((rows, SMALL_W), F32),
        in_specs=[pl.BlockSpec(memory_space=pltpu.VMEM)], out_specs=pl.BlockSpec(memory_space=pltpu.VMEM),
        compiler_params=pltpu.CompilerParams(vmem_limit_bytes=VMEM_LIMIT),
    )(g8)


def _cctx_grad(p8, c_ctx):
    def body(p_ref, c_ref, o_ref):
        s = p_ref[0]
        for chip in range(1, N_CHIPS):
            s = s + p_ref[2 * chip]
        o_ref[...] = s * _dsilu(c_ref[...])

    return pl.pallas_call(
        body, name="cctx_grad", out_shape=jax.ShapeDtypeStruct((1, D), F32),
        in_specs=[pl.BlockSpec(memory_space=pltpu.VMEM)] * 2, out_specs=pl.BlockSpec(memory_space=pltpu.VMEM),
    )(p8, c_ctx)


def _adam_small(w, g, m, v):
    def body(w_ref, g_ref, m_ref, v_ref, dl_ref, m2_ref, v2_ref):
        dl_ref[...], m2_ref[...], v2_ref[...] = _adam(w_ref[...], g_ref[...], m_ref[...], v_ref[...])

    vm = pl.BlockSpec(memory_space=pltpu.VMEM)
    return pl.pallas_call(
        body, name="adam_small", out_shape=[jax.ShapeDtypeStruct(w.shape, F32)] * 3, in_specs=[vm] * 4, out_specs=[vm] * 3,
        compiler_params=pltpu.CompilerParams(vmem_limit_bytes=VMEM_LIMIT),
    )(w, g, m, v)


def _pack(vecs, width=SMALL_W, row_mult=8):
    flat = jnp.concatenate([v.reshape(-1).astype(F32) for v in vecs])
    n = flat.shape[0]
    rows = -(-n // (width * row_mult)) * row_mult
    return jnp.pad(flat, (0, rows * width - n)).reshape(rows, width)


def _unpack(packed, shapes):
    flat = packed.reshape(-1)
    out, off = [], 0
    for s in shapes:
        n = int(np.prod(s))
        out.append(flat[off:off + n].reshape(s))
        off += n
    return out


WEIGHTS = ('c_ctx', 'w_ada', 'b_ada', 'norm1_w', 'w_in', 'gla_up_f', 'gla_bias_f', 'gla_up_b', 'gla_bias_b', 'gla_norm_w',
           'conv_w', 'conv_b', 'dt_bias_f', 'dt_bias_b', 'a_log_f', 'a_log_b', 'd_skip', 'ssm_norm_w', 'w_pa', 'w_pb', 'w_out',
           'norm2_w', 'w_gate', 'w_up', 'w_down', 'final_norm_w')
BIG = ('w_in', 'w_pa', 'w_pb', 'w_out', 'w_gate', 'w_up', 'w_down')
COL_SHARDED = ('w_in', 'w_gate', 'w_up')
SMALL_SHARDED = ('gla_up_f', 'gla_up_b', 'conv_w')
ROW_TILE = 256


def _blocks_to_full(g4, name):
    n, r, c = g4.shape
    return g4.transpose(1, 0, 2).reshape(r, n * c) if name in COL_SHARDED else g4.reshape(n * r, c)


def _full_to_blocks(full, name):
    r, c = full.shape
    if name in COL_SHARDED:
        return full.reshape(r, N_CHIPS, c // N_CHIPS).transpose(1, 0, 2)
    return full.reshape(N_CHIPS, r // N_CHIPS, c)


def _permute_in(w_in_full):
    off = np.concatenate([[0], np.cumsum(IN_WIDTHS)])
    cols = [w_in_full[:, off[p]:off[p + 1]] for p in PERM]
    return jnp.concatenate(cols + [jnp.zeros((w_in_full.shape[0], SMALL_PAD), w_in_full.dtype)], axis=1)


def _unpermute_in(wp):
    off = np.concatenate([[0], np.cumsum([IN_WIDTHS[p] for p in PERM])])
    pieces = {p: wp[:, off[i]:off[i + 1]] for i, p in enumerate(PERM)}
    return jnp.concatenate([pieces[p] for p in range(len(IN_WIDTHS))], axis=1)


def _chip_cols(full, chip, n):
    return lax.dynamic_slice_in_dim(full, chip * n, n, axis=1)


def kernel(x, c, ctx, c_ctx, w_ada, b_ada, norm1_w, w_in, gla_up_f, gla_bias_f, gla_up_b, gla_bias_b, gla_norm_w, conv_w, conv_b, dt_bias_f, dt_bias_b, a_log_f, a_log_b, d_skip, ssm_norm_w, w_pa, w_pb, w_out, norm2_w, w_gate, w_up, w_down, final_norm_w, loss_target, m_c_ctx, m_w_ada, m_b_ada, m_norm1_w, m_w_in, m_gla_up_f, m_gla_bias_f, m_gla_up_b, m_gla_bias_b, m_gla_norm_w, m_conv_w, m_conv_b, m_dt_bias_f, m_dt_bias_b, m_a_log_f, m_a_log_b, m_d_skip, m_ssm_norm_w, m_w_pa, m_w_pb, m_w_out, m_norm2_w, m_w_gate, m_w_up, m_w_down, m_final_norm_w, v_c_ctx, v_w_ada, v_b_ada, v_norm1_w, v_w_in, v_gla_up_f, v_gla_bias_f, v_gla_up_b, v_gla_bias_b, v_gla_norm_w, v_conv_w, v_conv_b, v_dt_bias_f, v_dt_bias_b, v_a_log_f, v_a_log_b, v_d_skip, v_ssm_norm_w, v_w_pa, v_w_pb, v_w_out, v_norm2_w, v_w_gate, v_w_up, v_w_down, v_final_norm_w):
    given = dict(locals())
    W = {n: given[n] for n in WEIGHTS}
    M = {n: given["m_" + n] for n in WEIGHTS}
    V = {n: given["v_" + n] for n in WEIGHTS}
    L, Lc = x.shape[1], ctx.shape[1]
    tr = ROW_TILE
    assert L % tr == 0 and Lc % tr == 0 and L % Lc == 0 and Lc % SSM_C == 0
    n_lat_tiles = L // tr
    xi, yi, ci = _place()
    chip, me = 2 * xi + yi, 4 * xi + 2 * yi + ci
    xall = jnp.concatenate([x[0], ctx[0]], axis=0)

    g0 = _allgather_small(_pack([c[0]] + [W[n][0] for n in SMALL_SHARDED]), "gather_c")
    g0 = g0.reshape(N_DEV, -1)
    c_all = g0[:, :D]
    small_full, off = {}, D
    for n in SMALL_SHARDED:
        r, cols = W[n].shape[1:]
        small_full[n] = jnp.concatenate([g0[2 * k, off:off + r * cols].reshape(r, cols) for k in range(N_CHIPS)], axis=1)
        off += r * cols
    up_f, up_b, conv_w_full = (small_full[n] for n in SMALL_SHARDED)

    cvec = jnp.zeros((ADA_ROWS, D), F32).at[:N_DEV].set(c_all).at[N_DEV].set(c_ctx)
    ada_cols = w_ada.shape[2]
    ada_part = _ada_fwd(cvec, w_ada[0], _chip_cols(b_ada, chip, ada_cols))
    g1_all = _allgather_small(ada_part, "gather_ada")
    ada_full = jnp.concatenate([g1_all[2 * k] for k in range(N_CHIPS)], axis=1)
    mine = lax.dynamic_slice_in_dim(ada_full, me, 1, axis=0)
    sh1, sc1, g1, sh2, sc2, g2 = (mine[:, k * D:(k + 1) * D] for k in range(6))
    csh1, csc1 = ada_full[N_DEV:N_DEV + 1, :D], ada_full[N_DEV:N_DEV + 1, D:2 * D]
    mod = jnp.stack([jnp.stack([sh1, sc1]), jnp.stack([csh1, csc1])])

    full = {'w_in': _blocks_to_full(_gather_split(w_in[0].astype(BF16), "gather_w_in"), 'w_in')}
    wp = _permute_in(full['w_in'])
    later = [n for n in BIG if n != 'w_in']

    def lr_rows(up, base):
        return jnp.zeros((128, GLA_H * GLA_DK), F32).at[base:base + GLA_RANK].set(up)
    u2 = jnp.stack([lr_rows(up_f, SM_LRF), lr_rows(up_b, SM_LRB)])
    u2_hi = u2.astype(BF16)
    u2_lo = (u2 - u2_hi.astype(F32)).astype(BF16)
    ut = u2.transpose(0, 2, 1)
    ut_hi = ut.astype(BF16)
    ut_lo = (ut - ut_hi.astype(F32)).astype(BF16)
    gbias = jnp.stack([gla_bias_f, gla_bias_b])
    kc = _ssd_consts(jnp.stack([dt_bias_f[0], dt_bias_b[0]]), jnp.stack([a_log_f[0], a_log_b[0]]))
    gw4 = jnp.tile(gla_norm_w, (1, GLA_H))
    dskip_exp = jnp.repeat(d_skip, SSM_P, axis=1)
    n_gla = (L // GLA_C, Lc // GLA_C)
    n_ssd = (L // SSM_C, Lc // SSM_C)

    h1 = _norm_mod(xall, norm1_w, mod, n_lat_tiles, tr)
    parts = _mm(h1, wp, "nn", ACT, "mm_in", tm=768, tn=1152)
    sm = _mm(h1, wp[:, COL_SM:], "nn", F32, "mm_in_small", tm=768)
    xbc = _conv_fwd(parts, conv_w_full, conv_b, L // Lc, Lc)
    *o2, gla_hist = _gla_fwd(parts, sm, u2_hi, u2_lo, gbias, *n_gla)
    (*y2, ssd_hist), gathered = _ssd_fwd(xbc, sm, kc, *n_ssd, _gather_comm([W[n][0].astype(BF16) for n in later]))
    full.update({n: _blocks_to_full(g, n) for n, g in zip(later, gathered)})
    oan = _gla_out(o2, parts, gw4, tr)
    obn = _ssd_out(y2, xbc, parts, dskip_exp, ssm_norm_w, tr)
    ya = _mm(oan, full['w_pa'], "nn", ACT, "mm_pa", tm=768)
    yb = _mm(obn, full['w_pb'], "nn", ACT, "mm_pb", tm=768)
    merged = _merge(ya, yb, parts, tr)
    mix = _mm(merged, full['w_out'], "nn", ACT, "mm_out", tm=768)
    h2, u = _resid_norm_mod(xall, mix, g1, norm2_w, sh2, sc2, tr)
    gp = _mm(u, full['w_gate'], "nn", ACT, "mm_gate", tm=768, tn=1408)
    up = _mm(u, full['w_up'], "nn", ACT, "mm_up", tm=768, tn=1408)
    act = _swiglu_act(gp, up, tr)
    f = _mm(act, full['w_down'], "nn", ACT, "mm_down", tm=768, tk=D_FF)
    dh3, df, acc_loss = _loss_head(h2, f, loss_target[0], g2, final_norm_w[None], n_lat_tiles, tr)

    dw = {}
    da = _mm(df, full['w_down'], "nt", ACT, "mm_down_dx", tm=768, tn=1408)
    dw['w_down'] = _mm(act, df, "tn", BF16, "mm_down_dw", tm=1408, tk=1408)
    dgp, dup = _swiglu_act_bwd(da, gp, up, tr)
    du_a = _mm(dgp, full['w_gate'], "nt", ACT, "mm_gate_dx", tm=768, tk=D_FF)
    du_b = _mm(dup, full['w_up'], "nt", ACT, "mm_up_dx", tm=768, tk=D_FF)
    dw['w_gate'] = _mm(u, dgp, "tn", BF16, "mm_gate_dw", tm=1024, tn=1408, tk=1408)
    dw['w_up'] = _mm(u, dup, "tn", BF16, "mm_up_dw", tm=1024, tn=1408, tk=1408)
    dh2, dmix, acc_ffn = _ffn_in_bwd(du_a, du_b, h2, dh3, mix, sc2, g1, norm2_w, tr)
    dmerged = _mm(dmix, full['w_out'], "nt", ACT, "mm_out_dx", tm=768)
    dw['w_out'] = _mm(merged, dmix, "tn", BF16, "mm_out_dw", tm=1024, tk=1408)
    dya, dyb, dparts = _merge_bwd(dmerged, ya, yb, parts, lax.empty((L + Lc, PW), BF16), tr)
    doan = _mm(dya, full['w_pa'], "nt", ACT, "mm_pa_dx", tm=768)
    dw['w_pa'] = _mm(oan, dya, "tn", BF16, "mm_pa_dw", tm=1024, tk=1408)
    dobn = _mm(dyb, full['w_pb'], "nt", ACT, "mm_pb_dx", tm=768)
    dw['w_pb'] = _mm(obn, dyb, "tn", BF16, "mm_pb_dw", tm=1024, tk=1408)
    do, dparts, acc_gla = _gla_out_bwd(doan, o2, parts, gw4, dparts, tr)
    dq, dk, dv, dpre = _gla_bwd(do, parts, sm, u2_hi, u2_lo, gbias, gla_hist, *n_gla)
    dy, dparts, acc_ssd = _ssd_out_bwd(dobn, y2, xbc, parts, dskip_exp, ssm_norm_w, dparts, tr)
    (dx_scan, db_scan, dc_scan, dsm, acc_alog), exchanged = _ssd_bwd(
        dy, xbc, sm, kc, ssd_hist, *n_ssd, _exchange_comm([_full_to_blocks(dw[n], n) for n in later]))
    exchanged = dict(zip(later, exchanged))
    dparts, acc_conv = _conv_bwd(dx_scan, db_scan, dc_scan, dy, dskip_exp, parts, conv_w_full, conv_b, dparts, L // Lc, Lc)
    dparts = _gla_assemble(dq, dk, dv, dparts, tr)
    dparts, dup_gla, acc_gbias, acc_dtb = _small_assemble(dpre, dsm, sm, ut_hi, ut_lo, dparts, tr)
    dw['w_in'] = _unpermute_in(_mm(h1, dparts, "tn", BF16, "mm_in_dw", tm=1024, tn=1152, tk=1408))
    dh1, (exchanged['w_in'],) = _mm(dparts, wp, "nt", F32, "mm_in_dx", tm=768, tk=3456,
                                    comm=_exchange_comm([_full_to_blocks(dw['w_in'], 'w_in')]))
    dx, acc_n1 = _norm1_bwd(dh1, xall, dh2, norm1_w, mod, n_lat_tiles, tr)

    partial = dict(
        norm1_w=acc_n1[0, 2] + acc_n1[1, 2],
        gla_up_f=dup_gla[0, SM_LRF:SM_LRF + GLA_RANK], gla_bias_f=acc_gbias[0],
        gla_up_b=dup_gla[1, SM_LRB:SM_LRB + GLA_RANK], gla_bias_b=acc_gbias[1],
        gla_norm_w=acc_gla[0].reshape(GLA_H, GLA_DV).sum(0),
        conv_w=acc_conv[:SSM_CONV], conv_b=acc_conv[SSM_CONV],
        dt_bias_f=acc_dtb[0, SM_DTF:SM_DTF + SSM_HEADS], dt_bias_b=acc_dtb[0, SM_DTB:SM_DTB + SSM_HEADS],
        a_log_f=acc_alog[0, :, 0, :SSM_HPG], a_log_b=acc_alog[1, :, 0, :SSM_HPG],
        d_skip=acc_ssd[1].reshape(SSM_HEADS, SSM_P).sum(1), ssm_norm_w=acc_ssd[0],
        norm2_w=acc_ffn[2], final_norm_w=acc_loss[0],
    )
    dada = jnp.concatenate([acc_n1[0, 1], acc_n1[0, 0], acc_ffn[3], acc_ffn[1], acc_ffn[0], acc_loss[1]])
    dada_ctx = jnp.concatenate([acc_n1[1, 1], acc_n1[1, 0], jnp.zeros((4 * D,), F32)])
    names = list(partial)
    payload = [partial[n] for n in names] + [dada + dada_ctx, dada_ctx, acc_loss[2], dada]
    sizes = [int(np.prod(p.shape)) for p in payload]
    g8 = _allgather_small(_pack(payload), "gather_small_grads")
    summed = _unpack(_sum8(g8), [(s,) for s in sizes])
    grads = {n: s.reshape(W[n].shape if n not in SMALL_SHARDED else partial[n].shape) for n, s in zip(names, summed)}
    grads['b_ada'] = summed[len(names)].reshape(b_ada.shape)
    dada_ctx_sum = summed[len(names) + 1]
    loss = jnp.sum(summed[len(names) + 2])
    dada_all = g8.reshape(N_DEV, -1)[:, sum(sizes[:-1]):sum(sizes)]

    dada16 = jnp.zeros((ADA_ROWS, ada_cols), F32)
    dada16 = dada16.at[:N_DEV].set(_chip_cols(dada_all, chip, ada_cols)).at[N_DEV].set(_chip_cols(dada_ctx_sum[None], chip, ada_cols)[0])
    g_wada, dl_wada, m_wada, v_wada, p16 = _wada_bwd_adam(cvec, dada16, w_ada[0], m_w_ada[0], v_w_ada[0])
    p8 = _allgather_small(p16[N_DEV:], "gather_cctx")
    grads['c_ctx'] = _cctx_grad(p8[:, 0:1, :], c_ctx[None])[0]
    for n in SMALL_SHARDED:
        grads[n] = _chip_cols(grads[n], chip, W[n].shape[2])[None]

    small = [n for n in WEIGHTS if n not in BIG and n != 'w_ada']
    shapes = [W[n].shape for n in small]
    dl_s, m_s, v_s = _adam_small(*[_pack([d[n] for n in small]) for d in (W, grads, M, V)])
    delta = dict(zip(small, _unpack(dl_s, shapes)))
    new_m = dict(zip(small, _unpack(m_s, shapes)))
    new_v = dict(zip(small, _unpack(v_s, shapes)))
    grads['w_ada'], delta['w_ada'], new_m['w_ada'], new_v['w_ada'] = g_wada[None], dl_wada[None], m_wada[None], v_wada[None]

    for n in BIG:
        g, dl, m2, v2 = _reduce_adam(exchanged[n], W[n][0], M[n][0], V[n][0], "adam_" + n)
        grads[n], delta[n], new_m[n], new_v[n] = g[None], dl[None], m2[None], v2[None]

    return (loss, dx[None], *[grads[n] for n in WEIGHTS], *[delta[n] for n in WEIGHTS],
            *[new_m[n] for n in WEIGHTS], *[new_v[n] for n in WEIGHTS])
```

```python
import functools

import numpy as np
import jax
import jax.numpy as jnp
from jax import lax
from jax.experimental import pallas as pl
from jax.experimental.pallas import tpu as pltpu

F32 = jnp.float32
BF16 = jnp.bfloat16
MESH = pl.DeviceIdType.MESH

D = 1024
EPS = 1e-6
GRID_W = 64
GLA_H, GLA_DK, GLA_DV, GLA_RANK, GLA_TAU = 4, 128, 256, 16, 16.0
GLA_C = 128
SSM_INNER, SSM_P, SSM_HEADS, SSM_G, SSM_HPG, SSM_N = 2048, 64, 32, 4, 8, 128
SSM_C = 128
SSM_CONV, CONV_LEFT = 4, 2
D_FF = 2816
IN_WIDTHS = (512, 512, 1024, 1024, 16, 16, 2048, 2048, 512, 512, 32, 32, 1024, 1024)
D_IN = sum(IN_WIDTHS)
PERM = (6, 7, 8, 9, 3, 0, 1, 2, 12, 13, 4, 5, 10, 11)
PW = 10368
SMALL_PAD = PW - D_IN
COL_Z, COL_XBC, COL_R, COL_Q, COL_K, COL_V, COL_GA, COL_GB, COL_SM = 0, 2048, 5120, 6144, 6656, 7168, 8192, 9216, 10240
SM_LRF, SM_LRB, SM_DTF, SM_DTB = 0, 16, 32, 64
EXP_CLAMP = 80.0
ADAM_LR, ADAM_B1, ADAM_B2, ADAM_EPS, ADAM_WD, ADAM_STEP = 0.001, 0.9, 0.999, 1e-08, 0.01, 10
N_CHIPS, N_DEV = 4, 8
VMEM_LIMIT = 56 * 1024 * 1024


def _cparams(sem=None):
    return pltpu.CompilerParams(dimension_semantics=sem, vmem_limit_bytes=VMEM_LIMIT)


def _dg(a, b, ca, cb):
    return lax.dot_general(a, b, (((ca,), (cb,)), ((), ())), preferred_element_type=F32)


def _nn(a, b):
    return _dg(a, b, 1, 0)


def _nt(a, b):
    return _dg(a, b, 1, 1)


def _tn(a, b):
    return _dg(a, b, 0, 0)


def _bf(x):
    return x.astype(BF16)


def _split(x):
    hi = x.astype(BF16)
    return hi, (x - hi.astype(F32)).astype(BF16)


def _nn_x(a, b_exact):
    hi, lo = _split(a)
    return _nn(hi, b_exact) + _nn(lo, b_exact)


def _x_nn(a_exact, b):
    hi, lo = _split(b)
    return _nn(a_exact, hi) + _nn(a_exact, lo)


def _nn3(a, b_hi, b_lo):
    hi, lo = _split(a)
    return _nn(hi, b_hi) + _nn(lo, b_hi) + _nn(hi, b_lo)


def _sigmoid(x):
    return 1.0 / (1.0 + jnp.exp(-x))


def _silu(x):
    return x * _sigmoid(x)


def _dsilu(x):
    s = _sigmoid(x)
    return s * (1.0 + x * (1.0 - s))


def _softplus(x):
    return jnp.maximum(x, 0.0) + jnp.log(1.0 + jnp.exp(-jnp.abs(x)))


def _log_sigmoid(x):
    return jnp.minimum(x, 0.0) - jnp.log(1.0 + jnp.exp(-jnp.abs(x)))


def _tile(n, target, mult=8):
    best = None
    for t in range(mult, min(n, target) + 1, mult):
        if n % t == 0:
            best = t
    assert best is not None, (n, target, mult)
    return best


def _mm(a, b, mode, out_dtype, name, tm=512, tn=1024, tk=2048, comm=None):
    if mode == "nn":
        (M, K), N = a.shape, b.shape[1]
    elif mode == "nt":
        (M, K), N = a.shape, b.shape[0]
    else:
        (K, M), N = a.shape, b.shape[1]
    tm, tn, tk = _tile(M, tm, 128), _tile(N, tn, 128), _tile(K, tk, 128)
    nk = K // tk
    ca, cb = {"nn": (1, 0), "nt": (1, 1), "tn": (0, 0)}[mode]

    def body(a_ref, b_ref, o_ref, *acc):
        part = _dg(a_ref[...], b_ref[...], ca, cb)
        if nk == 1:
            o_ref[...] = part.astype(out_dtype)
        else:
            k = pl.program_id(2)

            @pl.when(k == 0)
            def _():
                acc[0][...] = part

            @pl.when(k > 0)
            def _():
                acc[0][...] += part

            @pl.when(k == nk - 1)
            def _():
                o_ref[...] = acc[0][...].astype(out_dtype)

    a_spec = pl.BlockSpec((tk, tm), lambda i, j, k: (k, i)) if mode == "tn" else pl.BlockSpec((tm, tk), lambda i, j, k: (i, k))
    b_spec = pl.BlockSpec((tn, tk), lambda i, j, k: (j, k)) if mode == "nt" else pl.BlockSpec((tk, tn), lambda i, j, k: (k, j))
    gi, gj = M // tm, N // tn
    scratch = [pltpu.VMEM((tm, tn), F32)] if nk > 1 else []
    out_spec, out_shape = pl.BlockSpec((tm, tn), lambda i, j, k: (i, j)), jax.ShapeDtypeStruct((M, N), out_dtype)
    if comm is None:
        return pl.pallas_call(
            body, name=name, grid=(gi, gj, nk), in_specs=[a_spec, b_spec], out_specs=out_spec, out_shape=out_shape,
            scratch_shapes=scratch, compiler_params=_cparams(("arbitrary", "arbitrary", "arbitrary")),
        )(a, b)
    at = lambda i, j, k: (pl.program_id(0) == i) & (pl.program_id(1) == j) & (pl.program_id(2) == k)
    hosted = _hosted(body, 2, 1, len(scratch), comm, lambda: at(0, 0, 0), lambda: at(gi - 1, 0, 0),
                     lambda: at(gi - 1, gj - 1, nk - 1))
    outs = pl.pallas_call(
        hosted, name=name, grid=(gi, gj, nk), in_specs=[a_spec, b_spec] + [_ANY] * len(comm.arrays),
        out_specs=[out_spec] + [_ANY] * len(comm.out_shape), out_shape=[out_shape] + comm.out_shape,
        scratch_shapes=scratch + comm.scratch, compiler_params=_cparams(("arbitrary", "arbitrary", "arbitrary")),
    )(a, b, *comm.arrays)
    return outs[0], outs[1:]


def _place():
    return lax.axis_index("x"), lax.axis_index("y"), lax.axis_index("c")


def _flip(v, bit):
    return 1 - v if bit else v


def _allgather_small(v, name):
    R, C = v.shape

    def body(v_ref, out_ref, send_sems, recv_sems, local_sem):
        x, y, c = _place()
        me = 4 * x + 2 * y + c
        mine = pltpu.make_async_copy(v_ref, out_ref.at[me], local_sem)
        mine.start()

        def peer(r):
            return _flip(x, (r >> 2) & 1), _flip(y, (r >> 1) & 1), _flip(c, r & 1)

        sends = [pltpu.make_async_remote_copy(
            src_ref=v_ref, dst_ref=out_ref.at[me], send_sem=send_sems.at[r - 1], recv_sem=recv_sems.at[r - 1],
            device_id=peer(r), device_id_type=MESH) for r in range(1, N_DEV)]
        for cp in sends:
            cp.start()
        for r in range(1, N_DEV):
            px, py, pc = peer(r)
            pltpu.make_async_remote_copy(
                src_ref=v_ref, dst_ref=out_ref.at[4 * px + 2 * py + pc], send_sem=send_sems.at[r - 1],
                recv_sem=recv_sems.at[r - 1], device_id=(x, y, c), device_id_type=MESH).wait_recv()
        for cp in sends:
            cp.wait_send()
        mine.wait()

    return pl.pallas_call(
        body, name=name, out_shape=jax.ShapeDtypeStruct((N_DEV, R, C), v.dtype),
        in_specs=[pl.BlockSpec(memory_space=pltpu.VMEM)], out_specs=pl.BlockSpec(memory_space=pltpu.VMEM),
        scratch_shapes=[pltpu.SemaphoreType.DMA((N_DEV - 1,)), pltpu.SemaphoreType.DMA((N_DEV - 1,)), pltpu.SemaphoreType.DMA],
        compiler_params=pltpu.CompilerParams(vmem_limit_bytes=VMEM_LIMIT),
    )(v)


_CHIP_RELATIONS = ((1, 0), (0, 1), (1, 1))


def _gather_split(shard, name):
    rows, cols = shard.shape
    half = rows // 2

    def body(in_ref, out_ref, send_sems, recv_sems, local_sem):
        x, y, c = _place()
        chip = 2 * x + y
        mine = pl.ds(pl.multiple_of(c * half, 16), half)
        local = pltpu.make_async_copy(in_ref, out_ref.at[chip], local_sem)
        local.start()
        peers = [(_flip(x, fx), _flip(y, fy)) for fx, fy in _CHIP_RELATIONS]
        sends = [pltpu.make_async_remote_copy(
            src_ref=in_ref.at[mine], dst_ref=out_ref.at[chip, mine], send_sem=send_sems.at[j], recv_sem=recv_sems.at[j],
            device_id=(px, py, c), device_id_type=MESH) for j, (px, py) in enumerate(peers)]
        for cp in sends:
            cp.start()
        for j, (px, py) in enumerate(peers):
            landed = out_ref.at[2 * px + py, mine]
            pltpu.make_async_remote_copy(
                src_ref=landed, dst_ref=landed, send_sem=send_sems.at[j], recv_sem=recv_sems.at[j],
                device_id=(x, y, c), device_id_type=MESH).wait_recv()
            fwd = pltpu.make_async_remote_copy(
                src_ref=landed, dst_ref=landed, send_sem=send_sems.at[3 + j], recv_sem=recv_sems.at[3 + j],
                device_id=(x, y, 1 - c), device_id_type=MESH)
            fwd.start()
            sends.append(fwd)
        for j in range(3):
            landed = out_ref.at[0, mine]
            pltpu.make_async_remote_copy(
                src_ref=landed, dst_ref=landed, send_sem=send_sems.at[3 + j], recv_sem=recv_sems.at[3 + j],
                device_id=(x, y, c), device_id_type=MESH).wait_recv()
        for cp in sends:
            cp.wait_send()
        local.wait()

    any_spec = pl.BlockSpec(memory_space=pl.ANY)
    return pl.pallas_call(
        body, name=name, out_shape=jax.ShapeDtypeStruct((N_CHIPS, rows, cols), shard.dtype),
        in_specs=[any_spec], out_specs=any_spec,
        scratch_shapes=[pltpu.SemaphoreType.DMA((6,)), pltpu.SemaphoreType.DMA((6,)), pltpu.SemaphoreType.DMA],
    )(shard)


class _Comm:
    def __init__(self, arrays, out_shape, scratch, start, middle, finish):
        self.arrays, self.out_shape, self.scratch = arrays, out_shape, scratch
        self.start, self.middle, self.finish = start, middle, finish


def _hosted(body, n_in, n_out, n_scratch, comm, first, middle, last):
    nc, no = len(comm.arrays), len(comm.out_shape)

    def wrapped(*refs):
        a = n_in + nc
        b = a + n_out + no
        ins, c_ins, outs, c_outs = refs[:n_in], refs[n_in:a], refs[a:a + n_out], refs[a + n_out:b]
        scratch, c_sems = refs[b:b + n_scratch], refs[b + n_scratch:]

        @pl.when(first())
        def _():
            comm.start(c_ins, c_outs, c_sems)

        body(*ins, *outs, *scratch)
        if comm.middle is not None:
            @pl.when(middle())
            def _():
                comm.middle(c_ins, c_outs, c_sems)

        @pl.when(last())
        def _():
            comm.finish(c_ins, c_outs, c_sems)

    return wrapped


_ANY = pl.BlockSpec(memory_space=pl.ANY)


def _gather_comm(shards):
    n = len(shards)

    def copies(kind, ins, outs, sems):
        send_sems, recv_sems, local_sems = sems
        x, y, c = _place()
        chip = 2 * x + y
        if kind == "local":
            return [pltpu.make_async_copy(ins[i], outs[i].at[chip], local_sems.at[i]) for i in range(n)]
        made = []
        for i in range(n):
            for j, (fx, fy) in enumerate(_CHIP_RELATIONS):
                px, py = _flip(x, fx), _flip(y, fy)
                slot, to = (chip, (px, py, c)) if kind == "send" else (2 * px + py, (x, y, c))
                made.append(pltpu.make_async_remote_copy(
                    src_ref=ins[i], dst_ref=outs[i].at[slot], send_sem=send_sems.at[i, j], recv_sem=recv_sems.at[i, j],
                    device_id=to, device_id_type=MESH))
        return made

    def start(ins, outs, sems):
        for cp in copies("local", ins, outs, sems) + copies("send", ins, outs, sems):
            cp.start()

    def finish(ins, outs, sems):
        for cp in copies("recv", ins, outs, sems):
            cp.wait_recv()
        for cp in copies("send", ins, outs, sems):
            cp.wait_send()
        for cp in copies("local", ins, outs, sems):
            cp.wait()

    return _Comm(list(shards), [jax.ShapeDtypeStruct((N_CHIPS,) + s.shape, s.dtype) for s in shards],
                 [pltpu.SemaphoreType.DMA((n, 3)), pltpu.SemaphoreType.DMA((n, 3)), pltpu.SemaphoreType.DMA((n,))],
                 start, None, finish)


def _exchange_comm(blocks):
    n = len(blocks)

    def copies(kind, ins, outs, sems):
        send_sems, recv_sems, local_sems = sems
        x, y, c = _place()
        chip = 2 * x + y
        me, sibling = (x, y, c), (x, y, 1 - c)

        def remote(src, dst, i, j, to):
            return pltpu.make_async_remote_copy(src_ref=src, dst_ref=dst, send_sem=send_sems.at[i, j],
                                                recv_sem=recv_sems.at[i, j], device_id=to, device_id_type=MESH)

        made = []
        for i in range(n):
            if kind == "local":
                made.append(pltpu.make_async_copy(ins[i].at[chip], outs[i].at[chip], local_sems.at[i]))
                continue
            for j, (fx, fy) in enumerate(_CHIP_RELATIONS):
                px, py = _flip(x, fx), _flip(y, fy)
                src = 2 * px + py
                if kind == "first":
                    made.append(remote(ins[i].at[src], outs[i].at[chip], i, j, (px, py, c)))
                elif kind == "landed":
                    made.append(remote(ins[i].at[src], outs[i].at[src], i, j, me))
                elif kind == "passed":
                    made.append(remote(outs[i].at[src], outs[i].at[N_CHIPS + src], i, 4 + j, sibling))
            if kind == "first":
                made.append(remote(ins[i].at[chip], outs[i].at[N_CHIPS + chip], i, 3, sibling))
            if kind == "arrivals":
                made += [remote(ins[i].at[0], outs[i].at[0], i, j, me) for j in (3, 4, 5, 6)]
        return made

    def start(ins, outs, sems):
        for cp in copies("local", ins, outs, sems) + copies("first", ins, outs, sems):
            cp.start()

    def middle(ins, outs, sems):
        for got, fwd in zip(copies("landed", ins, outs, sems), copies("passed", ins, outs, sems)):
            got.wait_recv()
            fwd.start()

    def finish(ins, outs, sems):
        for cp in copies("arrivals", ins, outs, sems):
            cp.wait_recv()
        for cp in copies("first", ins, outs, sems) + copies("passed", ins, outs, sems):
            cp.wait_send()
        for cp in copies("local", ins, outs, sems):
            cp.wait()

    return _Comm(list(blocks), [jax.ShapeDtypeStruct((N_DEV,) + b.shape[1:], b.dtype) for b in blocks],
                 [pltpu.SemaphoreType.DMA((n, 7)), pltpu.SemaphoreType.DMA((n, 7)), pltpu.SemaphoreType.DMA((n,))],
                 start, middle, finish)


def _row_spec(tr, w, col=0):
    return pl.BlockSpec((tr, w), lambda i: (i, col))


def _vec_spec(w):
    return pl.BlockSpec((1, w), lambda i: (0, 0))


def _acc_spec(w):
    return pl.BlockSpec((8, w), lambda i: (0, 0))


def _rms(x):
    return lax.rsqrt(jnp.mean(x * x, axis=-1, keepdims=True) + EPS)


def _rms_bwd(dn, n, rstd):
    return rstd * (dn - n * jnp.mean(dn * n, axis=-1, keepdims=True))


def _colsum(x):
    return jnp.sum(x, axis=0, keepdims=True)


def _zero_first(ref):
    @pl.when(pl.program_id(0) == 0)
    def _():
        ref[...] = jnp.zeros_like(ref)


def _x_specs(tr, n_lat_tiles):
    return [pl.BlockSpec((tr, D), lambda i: (jnp.minimum(i, n_lat_tiles - 1), 0)),
            pl.BlockSpec((tr, D), lambda i: (jnp.maximum(i - n_lat_tiles, 0), 0))]


def _x_tile(x_ref, c_ref, n_lat_tiles):
    return jnp.where(pl.program_id(0) >= n_lat_tiles, c_ref[...], x_ref[...])


def _norm_mod(x, ctx, w, mod, n_lat_tiles, tr):
    R = x.shape[0] + ctx.shape[0]

    def body(x_ref, c_ref, w_ref, mod_ref, o_ref):
        xv = _x_tile(x_ref, c_ref, n_lat_tiles)
        n = xv * _rms(xv) * w_ref[...]
        o_ref[...] = (n * (1.0 + mod_ref[0, 1]) + mod_ref[0, 0]).astype(BF16)

    return pl.pallas_call(
        body, name="norm1_mod", grid=(R // tr,),
        in_specs=_x_specs(tr, n_lat_tiles) + [_vec_spec(D), pl.BlockSpec(
            (1, 2, 1, D), lambda i: (jnp.where(i >= n_lat_tiles, 1, 0), 0, 0, 0))],
        out_specs=_row_spec(tr, D), out_shape=jax.ShapeDtypeStruct((R, D), BF16),
        compiler_params=_cparams(("arbitrary",)),
    )(x, ctx, w, mod)


def _resid_norm_mod(x, ctx, mix, g1, w2, sh2, sc2, n_lat_tiles, tr):
    R = x.shape[0] + ctx.shape[0]

    def body(x_ref, c_ref, mix_ref, g1_ref, w_ref, sh_ref, sc_ref, h2_ref, u_ref):
        h2 = _x_tile(x_ref, c_ref, n_lat_tiles) + g1_ref[...] * mix_ref[...]
        h2_ref[...] = h2
        n = h2 * _rms(h2) * w_ref[...]
        u_ref[...] = (n * (1.0 + sc_ref[...]) + sh_ref[...]).astype(BF16)

    return pl.pallas_call(
        body, name="resid_norm2_mod", grid=(R // tr,),
        in_specs=_x_specs(tr, n_lat_tiles) + [_row_spec(tr, D)] + [_vec_spec(D)] * 4,
        out_specs=[_row_spec(tr, D), _row_spec(tr, D)],
        out_shape=[jax.ShapeDtypeStruct((R, D), F32), jax.ShapeDtypeStruct((R, D), BF16)],
        compiler_params=_cparams(("arbitrary",)),
    )(x, ctx, mix, g1, w2, sh2, sc2)


def _loss_head(h2, f, target, g2, fw, n_lat_tiles, tr):
    R = h2.shape[0]

    def body(h2_ref, f_ref, t_ref, g2_ref, fw_ref, dh3_ref, df_ref, acc_ref):
        _zero_first(acc_ref)
        lat = pl.program_id(0) < n_lat_tiles
        fv = f_ref[...]
        h3 = h2_ref[...] + g2_ref[...] * fv
        rstd = _rms(h3)
        n = h3 * rstd
        err = n * fw_ref[...] - t_ref[...]
        dy = err * (1.0 / D)
        dh3 = jnp.where(lat, _rms_bwd(dy * fw_ref[...], n, rstd), 0.0)
        dh3_ref[...] = dh3
        df_ref[...] = (g2_ref[...] * dh3).astype(BF16)
        acc_ref[0:1, :] += jnp.where(lat, _colsum(dy * n), 0.0)
        acc_ref[1:2, :] += _colsum(dh3 * fv)
        acc_ref[2:3, :] += jnp.where(lat, _colsum(err * err) * (0.5 / D), 0.0)

    return pl.pallas_call(
        body, name="loss_head", grid=(R // tr,),
        in_specs=[_row_spec(tr, D), _row_spec(tr, D),
                  pl.BlockSpec((tr, D), lambda i: (jnp.minimum(i, n_lat_tiles - 1), 0)), _vec_spec(D), _vec_spec(D)],
        out_specs=[_row_spec(tr, D), _row_spec(tr, D), _acc_spec(D)],
        out_shape=[jax.ShapeDtypeStruct((R, D), F32), jax.ShapeDtypeStruct((R, D), BF16), jax.ShapeDtypeStruct((8, D), F32)],
        compiler_params=_cparams(("arbitrary",)),
    )(h2, f, target, g2, fw)


def _ffn_in_bwd(du_a, du_b, h2, dh3, mix, sc2, g1, w2, tr):
    R = h2.shape[0]

    def body(dua_ref, dub_ref, h2_ref, dh3_ref, mix_ref, sc_ref, g1_ref, w_ref, dh2_ref, dmix_ref, acc_ref):
        _zero_first(acc_ref)
        du = _f32(dua_ref) + _f32(dub_ref)
        h2 = h2_ref[...]
        rstd = _rms(h2)
        n = h2 * rstd
        dnw = du * (1.0 + sc_ref[...])
        dh2 = dh3_ref[...] + _rms_bwd(dnw * w_ref[...], n, rstd)
        dh2_ref[...] = dh2
        dmix_ref[...] = (g1_ref[...] * dh2).astype(BF16)
        acc_ref[0:1, :] += _colsum(du * n * w_ref[...])
        acc_ref[1:2, :] += _colsum(du)
        acc_ref[2:3, :] += _colsum(dnw * n)
        acc_ref[3:4, :] += _colsum(dh2 * mix_ref[...])

    return pl.pallas_call(
        body, name="ffn_in_bwd", grid=(R // tr,),
        in_specs=[_row_spec(tr, D)] * 5 + [_vec_spec(D)] * 3,
        out_specs=[_row_spec(tr, D), _row_spec(tr, D), _acc_spec(D)],
        out_shape=[jax.ShapeDtypeStruct((R, D), F32), jax.ShapeDtypeStruct((R, D), BF16), jax.ShapeDtypeStruct((8, D), F32)],
        compiler_params=_cparams(("arbitrary",)),
    )(du_a, du_b, h2, dh3, mix, sc2, g1, w2)


def _norm1_bwd(dh1, x, ctx, dh2, w1, mod, n_lat_tiles, tr):
    R = x.shape[0] + ctx.shape[0]

    def body(dh1_ref, x_ref, c_ref, dh2_ref, w_ref, mod_ref, dx_ref, acc_ref):
        i = pl.program_id(0)

        @pl.when((i == 0) | (i == n_lat_tiles))
        def _():
            acc_ref[...] = jnp.zeros_like(acc_ref)

        dh1 = dh1_ref[...]
        x = _x_tile(x_ref, c_ref, n_lat_tiles)
        rstd = _rms(x)
        n = x * rstd
        dnw = dh1 * (1.0 + mod_ref[0, 1])

        @pl.when(i < n_lat_tiles)
        def _():
            dx_ref[...] = dh2_ref[...] + _rms_bwd(dnw * w_ref[...], n, rstd)

        acc_ref[0, 0:1, :] += _colsum(dh1 * n * w_ref[...])
        acc_ref[0, 1:2, :] += _colsum(dh1)
        acc_ref[0, 2:3, :] += _colsum(dnw * n)

    sel = lambda i: jnp.where(i >= n_lat_tiles, 1, 0)
    return pl.pallas_call(
        body, name="norm1_bwd", grid=(R // tr,),
        in_specs=[_row_spec(tr, D)] + _x_specs(tr, n_lat_tiles) + [_row_spec(tr, D), _vec_spec(D),
                                                                   pl.BlockSpec((1, 2, 1, D), lambda i: (sel(i), 0, 0, 0))],
        out_specs=[pl.BlockSpec((tr, D), lambda i: (jnp.minimum(i, n_lat_tiles - 1), 0)),
                   pl.BlockSpec((1, 8, D), lambda i: (sel(i), 0, 0))],
        out_shape=[jax.ShapeDtypeStruct((n_lat_tiles * tr, D), F32), jax.ShapeDtypeStruct((2, 8, D), F32)],
        compiler_params=_cparams(("arbitrary",)),
    )(dh1, x, ctx, dh2, w1, mod)


def _swiglu_act(gp, up, tr):
    R = gp.shape[0]

    def body(g_ref, u_ref, o_ref):
        o_ref[...] = (_silu(_f32(g_ref)) * _f32(u_ref)).astype(BF16)

    return pl.pallas_call(
        body, name="swiglu_act", grid=(R // tr,), in_specs=[_row_spec(tr, D_FF)] * 2, out_specs=_row_spec(tr, D_FF),
        out_shape=jax.ShapeDtypeStruct((R, D_FF), BF16), compiler_params=_cparams(("arbitrary",)),
    )(gp, up)


def _swiglu_act_bwd(da, gp, up, tr):
    R = gp.shape[0]

    def body(da_ref, g_ref, u_ref, dg_ref, du_ref):
        da, g = _f32(da_ref), _f32(g_ref)
        dg_ref[...] = (da * _f32(u_ref) * _dsilu(g)).astype(BF16)
        du_ref[...] = (da * _silu(g)).astype(BF16)

    return pl.pallas_call(
        body, name="swiglu_act_bwd", grid=(R // tr,), in_specs=[_row_spec(tr, D_FF)] * 3, out_specs=[_row_spec(tr, D_FF)] * 2,
        out_shape=[jax.ShapeDtypeStruct((R, D_FF), BF16)] * 2, compiler_params=_cparams(("arbitrary",)),
    )(da, gp, up)


def _merge(ya, yb, parts, tr):
    R = ya.shape[0]

    def body(ya_ref, yb_ref, ga_ref, gb_ref, o_ref):
        o_ref[...] = (_sigmoid(_f32(ga_ref)) * _f32(ya_ref) + _sigmoid(_f32(gb_ref)) * _f32(yb_ref)).astype(BF16)

    return pl.pallas_call(
        body, name="merge", grid=(R // tr,),
        in_specs=[_row_spec(tr, D), _row_spec(tr, D), _row_spec(tr, D, COL_GA // D), _row_spec(tr, D, COL_GB // D)],
        out_specs=_row_spec(tr, D), out_shape=jax.ShapeDtypeStruct((R, D), BF16), compiler_params=_cparams(("arbitrary",)),
    )(ya, yb, parts, parts)


def _dparts_out(tr, w, col, nd=1):
    blk = col // w
    return pl.BlockSpec((tr, w), (lambda i: (i, blk)) if nd == 1 else (lambda i, j: (i, blk + j)))


def _merge_bwd(dm, ya, yb, parts, dparts, tr):
    R = ya.shape[0]

    def body(dm_ref, ya_ref, yb_ref, ga_ref, gb_ref, _, dya_ref, dyb_ref, dg_ref):
        dm = _f32(dm_ref)
        sa, sb = _sigmoid(_f32(ga_ref)), _sigmoid(_f32(gb_ref))
        dya_ref[...] = (dm * sa).astype(BF16)
        dyb_ref[...] = (dm * sb).astype(BF16)
        dg_ref[:, 0:D] = (dm * _f32(ya_ref) * sa * (1.0 - sa)).astype(BF16)
        dg_ref[:, D:2 * D] = (dm * _f32(yb_ref) * sb * (1.0 - sb)).astype(BF16)

    return pl.pallas_call(
        body, name="merge_bwd", grid=(R // tr,),
        in_specs=[_row_spec(tr, D)] * 3 + [_row_spec(tr, D, COL_GA // D), _row_spec(tr, D, COL_GB // D), _ANY],
        out_specs=[_row_spec(tr, D), _row_spec(tr, D), _dparts_out(tr, 2 * D, COL_GA)],
        out_shape=[jax.ShapeDtypeStruct((R, D), BF16)] * 2 + [jax.ShapeDtypeStruct(dparts.shape, BF16)],
        input_output_aliases={5: 2}, compiler_params=_cparams(("arbitrary",)),
    )(dm, ya, yb, parts, parts, dparts)


def _gla_out(o2, parts, gw4, tr):
    R = parts.shape[0]

    def body(of_ref, ob_ref, r_ref, w_ref, out_ref):
        oa = _f32(of_ref) + _f32(ob_ref)
        sr = _silu(_f32(r_ref))
        for h in range(GLA_H):
            s = slice(h * GLA_DV, (h + 1) * GLA_DV)
            o = oa[:, s]
            out_ref[:, s] = (o * _rms(o) * w_ref[:, s] * sr[:, s]).astype(BF16)

    return pl.pallas_call(
        body, name="gla_out", grid=(R // tr,),
        in_specs=[_row_spec(tr, D), _row_spec(tr, D), _row_spec(tr, D, COL_R // D), _vec_spec(D)],
        out_specs=_row_spec(tr, D), out_shape=jax.ShapeDtypeStruct((R, D), BF16), compiler_params=_cparams(("arbitrary",)),
    )(o2[0], o2[1], parts, gw4)


def _gla_out_bwd(dout, o2, parts, gw4, dparts, tr):
    R = parts.shape[0]

    def body(d_ref, of_ref, ob_ref, r_ref, w_ref, _, do_ref, dr_ref, acc_ref):
        _zero_first(acc_ref)
        oa = _f32(of_ref) + _f32(ob_ref)
        r = _f32(r_ref)
        sr = _silu(r)
        dout = _f32(d_ref)
        for h in range(GLA_H):
            s = slice(h * GLA_DV, (h + 1) * GLA_DV)
            o = oa[:, s]
            rstd = _rms(o)
            n = o * rstd
            w = w_ref[:, s]
            dr_ref[:, s] = (dout[:, s] * n * w * _dsilu(r[:, s])).astype(BF16)
            dnw = dout[:, s] * sr[:, s]
            do_ref[:, s] = _rms_bwd(dnw * w, n, rstd).astype(ACT)
            acc_ref[0:1, s] += _colsum(dnw * n)

    return pl.pallas_call(
        body, name="gla_out_bwd", grid=(R // tr,),
        in_specs=[_row_spec(tr, D), _row_spec(tr, D), _row_spec(tr, D), _row_spec(tr, D, COL_R // D), _vec_spec(D), _ANY],
        out_specs=[_row_spec(tr, D), _dparts_out(tr, D, COL_R), _acc_spec(D)],
        out_shape=[jax.ShapeDtypeStruct((R, D), ACT), jax.ShapeDtypeStruct(dparts.shape, BF16), jax.ShapeDtypeStruct((8, D), F32)],
        input_output_aliases={5: 1}, compiler_params=_cparams(("arbitrary",)),
    )(dout, o2[0], o2[1], parts, gw4, dparts)


SSM_GW = SSM_INNER // SSM_G


def _ssd_out(y2, xbc, parts, dskip, nw, tr):
    R = parts.shape[0]

    def body(yf_ref, yb_ref, x_ref, z_ref, ds_ref, w_ref, out_ref):
        ob = (_f32(yf_ref) + _f32(yb_ref) + ds_ref[...] * _f32(x_ref)) * _silu(_f32(z_ref))
        for g in range(SSM_G):
            s = slice(g * SSM_GW, (g + 1) * SSM_GW)
            o = ob[:, s]
            out_ref[:, s] = (o * _rms(o) * w_ref[:, s]).astype(BF16)

    return pl.pallas_call(
        body, name="ssd_out", grid=(R // tr,),
        in_specs=[_row_spec(tr, SSM_INNER)] * 3 + [_row_spec(tr, SSM_INNER, COL_Z // SSM_INNER),
                                                   _vec_spec(SSM_INNER), _vec_spec(SSM_INNER)],
        out_specs=_row_spec(tr, SSM_INNER), out_shape=jax.ShapeDtypeStruct((R, SSM_INNER), BF16),
        compiler_params=_cparams(("arbitrary",)),
    )(y2[0], y2[1], xbc, parts, dskip, nw)


def _ssd_out_bwd(dout, y2, xbc, parts, dskip, nw, dparts, tr):
    R = parts.shape[0]

    def body(d_ref, yf_ref, yb_ref, x_ref, z_ref, ds_ref, w_ref, _, dy_ref, dz_ref, acc_ref):
        _zero_first(acc_ref)
        x, z = _f32(x_ref), _f32(z_ref)
        pre = _f32(yf_ref) + _f32(yb_ref) + ds_ref[...] * x
        sz = _silu(z)
        ob = pre * sz
        dout = _f32(d_ref)
        for g in range(SSM_G):
            s = slice(g * SSM_GW, (g + 1) * SSM_GW)
            o = ob[:, s]
            rstd = _rms(o)
            n = o * rstd
            dob = _rms_bwd(dout[:, s] * w_ref[:, s], n, rstd)
            dz_ref[:, s] = (dob * pre[:, s] * _dsilu(z[:, s])).astype(BF16)
            dy = dob * sz[:, s]
            dy_ref[:, s] = dy.astype(ACT)
            acc_ref[0:1, s] += _colsum(dout[:, s] * n)
            acc_ref[1:2, s] += _colsum(dy * x[:, s])

    return pl.pallas_call(
        body, name="ssd_out_bwd", grid=(R // tr,),
        in_specs=[_row_spec(tr, SSM_INNER)] * 4 + [_row_spec(tr, SSM_INNER, COL_Z // SSM_INNER),
                                                   _vec_spec(SSM_INNER), _vec_spec(SSM_INNER), _ANY],
        out_specs=[_row_spec(tr, SSM_INNER), _dparts_out(tr, SSM_INNER, COL_Z), _acc_spec(SSM_INNER)],
        out_shape=[jax.ShapeDtypeStruct((R, SSM_INNER), ACT), jax.ShapeDtypeStruct(dparts.shape, BF16),
                   jax.ShapeDtypeStruct((8, SSM_INNER), F32)],
        input_output_aliases={7: 1}, compiler_params=_cparams(("arbitrary",)),
    )(dout, y2[0], y2[1], xbc, parts, dskip, nw, dparts)


CONV_W = SSM_INNER + 2 * SSM_G * SSM_N
CONV_BLK = 1024


CONV_SHIFTS = (-2, -1, 1, 2)


def _conv_mask_table(tr):
    t = np.arange(tr)
    table = np.zeros((2, len(CONV_SHIFTS), tr, 128), np.float32)
    for kind, (pos, seg) in enumerate(((t % GRID_W, GRID_W), (t, tr))):
        for k, s in enumerate(CONV_SHIFTS):
            table[kind, k] = ((pos + s >= 0) & (pos + s < seg)).astype(np.float32)[:, None]
    return jnp.asarray(table)


def _shifted(u, s, mask_ref, tr):
    return u if s == 0 else pltpu.roll(u, (-s) % tr, 0) * mask_ref[0, CONV_SHIFTS.index(s)]


def _conv_mask_spec(tr, n_lat_tiles, row_axis):
    return pl.BlockSpec((1, len(CONV_SHIFTS), tr, 128),
                        lambda *ids: (jnp.where(ids[row_axis] >= n_lat_tiles, 1, 0), 0, 0, 0))


def _conv_fwd(parts, cw, cb, n_lat_tiles, tr):
    R = parts.shape[0]

    def body(u_ref, w_ref, b_ref, mask_ref, o_ref):
        def lanes(l, carry):
            sl = pl.ds(pl.multiple_of(l * 128, 128), 128)
            u, w = u_ref[:, sl].astype(F32), w_ref[:, sl]
            acc = jnp.zeros_like(u) + b_ref[:, sl]
            for j in range(SSM_CONV):
                acc = acc + _shifted(u, j - CONV_LEFT, mask_ref, tr) * w[j:j + 1, :]
            o_ref[:, sl] = _silu(acc).astype(ACT)
            return carry

        lax.fori_loop(0, CONV_BLK // 128, lanes, 0)

    return pl.pallas_call(
        body, name="conv_fwd", grid=(R // tr, CONV_W // CONV_BLK),
        in_specs=[pl.BlockSpec((tr, CONV_BLK), lambda i, j: (i, COL_XBC // CONV_BLK + j)),
                  pl.BlockSpec((SSM_CONV, CONV_BLK), lambda i, j: (0, j)), pl.BlockSpec((1, CONV_BLK), lambda i, j: (0, j)),
                  _conv_mask_spec(tr, n_lat_tiles, 0)],
        out_specs=pl.BlockSpec((tr, CONV_BLK), lambda i, j: (i, j)), out_shape=jax.ShapeDtypeStruct((R, CONV_W), ACT),
        compiler_params=_cparams(("arbitrary", "arbitrary")),
    )(parts, cw, cb, _conv_mask_table(tr))


def _conv_bwd(dx, db, dc, dy, dskip, parts, cw, cb, dparts, n_lat_tiles, tr):
    R = parts.shape[0]
    half = CONV_BLK // 2
    n_x = SSM_INNER // CONV_BLK

    def body(dxf_ref, dxb_ref, dy_ref, ds_ref, dbf_ref, dbb_ref, dcf_ref, dcb_ref, u_ref, w_ref, b_ref, _, mask_ref,
             du_ref, acc_ref, d_scr):
        @pl.when(pl.program_id(1) == 0)
        def _():
            acc_ref[...] = jnp.zeros_like(acc_ref)

        @pl.when(pl.program_id(0) < n_x)
        def _():
            d_scr[...] = dxf_ref[...] + dxb_ref[...] + _f32(dy_ref) * ds_ref[...]

        @pl.when(pl.program_id(0) >= n_x)
        def _():
            d_scr[:, 0:half] = dbf_ref[...] + dbb_ref[...]
            d_scr[:, half:] = dcf_ref[...] + dcb_ref[...]

        def lanes(l, carry):
            sl = pl.ds(pl.multiple_of(l * 128, 128), 128)
            u, w = u_ref[:, sl].astype(F32), w_ref[:, sl]
            pre = jnp.zeros_like(u) + b_ref[:, sl]
            taps = []
            for j in range(SSM_CONV):
                tap = _shifted(u, j - CONV_LEFT, mask_ref, tr)
                taps.append(tap)
                pre = pre + tap * w[j:j + 1, :]
            dpre = d_scr[:, sl] * _dsilu(pre)
            du = jnp.zeros_like(u)
            sums = []
            for j in range(SSM_CONV):
                sums.append(_colsum(dpre * taps[j]))
                du = du + _shifted(dpre, CONV_LEFT - j, mask_ref, tr) * w[j:j + 1, :]
            sums += [_colsum(dpre), jnp.zeros((8 - SSM_CONV - 1, 128), F32)]
            acc_ref[:, sl] += jnp.concatenate(sums, axis=0)
            du_ref[:, sl] = du.astype(BF16)
            return carry

        lax.fori_loop(0, CONV_BLK // 128, lanes, 0)

    return pl.pallas_call(
        body, name="conv_bwd", grid=(CONV_W // CONV_BLK, R // tr),
        in_specs=[pl.BlockSpec((tr, CONV_BLK), lambda j, i: (jnp.where(j < n_x, i, 0), jnp.minimum(j, n_x - 1)))] * 3
        + [pl.BlockSpec((1, CONV_BLK), lambda j, i: (0, jnp.minimum(j, n_x - 1)))]
        + [pl.BlockSpec((tr, half), lambda j, i: (jnp.where(j < n_x, 0, i), 0))] * 4
        + [pl.BlockSpec((tr, CONV_BLK), lambda j, i: (i, COL_XBC // CONV_BLK + j)),
           pl.BlockSpec((SSM_CONV, CONV_BLK), lambda j, i: (0, j)), pl.BlockSpec((1, CONV_BLK), lambda j, i: (0, j)), _ANY,
           _conv_mask_spec(tr, n_lat_tiles, 1)],
        out_specs=[pl.BlockSpec((tr, CONV_BLK), lambda j, i: (i, COL_XBC // CONV_BLK + j)),
                   pl.BlockSpec((8, CONV_BLK), lambda j, i: (0, j))],
        out_shape=[jax.ShapeDtypeStruct(dparts.shape, BF16), jax.ShapeDtypeStruct((8, CONV_W), F32)],
        scratch_shapes=[pltpu.VMEM((tr, CONV_BLK), F32)],
        input_output_aliases={11: 0}, compiler_params=_cparams(("arbitrary", "arbitrary")),
    )(*dx, dy, dskip, *db, *dc, parts, cw, cb, dparts, _conv_mask_table(tr))


def _chunk_row_block(d, i, n_lat, n_ctx):
    fwd = jnp.where(i < n_ctx, n_lat + i, i - n_ctx)
    rev = n_lat + n_ctx - 1 - i
    if isinstance(d, int):
        return rev if d else fwd
    return jnp.where(d == 0, fwd, rev)


def _tri(n, d, transpose=False):
    row = lax.broadcasted_iota(jnp.int32, (n, n), 0)
    col = lax.broadcasted_iota(jnp.int32, (n, n), 1)
    diff = (col - row) if transpose else (row - col)
    return diff * (1 - 2 * d) >= 0


def _gla_gates(sm, uhi, ulo, bias, d):
    pre = _nn3(sm, uhi, ulo) + bias
    g = _log_sigmoid(pre) * (1.0 / GLA_TAU)
    mask = _tri(GLA_C, d)
    b = _x_nn(mask.astype(BF16), g)
    b_tot = _colsum(g)
    b_ref = b[GLA_C // 2:GLA_C // 2 + 1, :]
    e_q = jnp.exp(jnp.minimum(b - b_ref, EXP_CLAMP))
    e_k = jnp.exp(jnp.minimum(b_ref - b, EXP_CLAMP))
    return pre, mask, b_tot, e_q, e_k, jnp.exp(b), jnp.exp(b_tot - b)


GLA_QK = GLA_H * GLA_DK
GLA_V = GLA_H * GLA_DV


def _gla_specs(n_lat, n_ctx, step_of):
    rbs = [lambda i, d=d: _chunk_row_block(d, step_of(i), n_lat, n_ctx) for d in range(2)]
    specs = []
    for rb in rbs:
        specs += [pl.BlockSpec((GLA_C, GLA_QK), lambda i, rb=rb: (rb(i), COL_Q // GLA_QK)),
                  pl.BlockSpec((GLA_C, GLA_QK), lambda i, rb=rb: (rb(i), COL_K // GLA_QK)),
                  pl.BlockSpec((GLA_C, GLA_V), lambda i, rb=rb: (rb(i), COL_V // GLA_V)),
                  pl.BlockSpec((GLA_C, 128), lambda i, rb=rb: (rb(i), 0))]
    specs += [pl.BlockSpec((2, 128, GLA_QK), lambda i: (0, 0, 0)), pl.BlockSpec((2, 128, GLA_QK), lambda i: (0, 0, 0)),
              pl.BlockSpec((2, 1, GLA_QK), lambda i: (0, 0, 0))]
    return specs, rbs


ACT = BF16


def _f32(ref_or_value):
    return ref_or_value[...].astype(F32)


def _gla_fwd(parts, sm, uhi, ulo, bias, n_lat, n_ctx):
    R = parts.shape[0]
    n_steps = n_lat + n_ctx
    scale = GLA_DK ** -0.5

    def body(*refs):
        ins, (uhi_ref, ulo_ref, bias_ref), o_refs, hist_ref, st = refs[:8], refs[8:11], refs[11:13], refs[13], refs[14]

        @pl.when(pl.program_id(0) == 0)
        def _():
            st[...] = jnp.zeros_like(st)

        for d in range(2):
            q_ref, k_ref, v_ref, sm_ref = ins[4 * d:4 * d + 4]
            _, mask, b_tot, e_q, e_k, e_in, e_out = _gla_gates(sm_ref[...], uhi_ref[d], ulo_ref[d], bias_ref[d], d)
            q, k, v = _f32(q_ref) * scale, _f32(k_ref), _bf(v_ref[...])
            qb, kb, q_in, k_out, decay = _bf(q * e_q), _bf(k * e_k), _bf(q * e_in), _bf(k * e_out), jnp.exp(b_tot)
            for h in range(GLA_H):
                sk, sv = slice(h * GLA_DK, (h + 1) * GLA_DK), slice(h * GLA_DV, (h + 1) * GLA_DV)
                att = jnp.where(mask, _nt(qb[:, sk], kb[:, sk]), 0.0)
                s_in = st[d, h]
                hist_ref[d, 0, h] = s_in
                o_refs[d][:, sv] = (_nn(_bf(att), v[:, sv]) + _nt(q_in[:, sk], _bf(s_in))).astype(ACT)
                st[d, h] = decay[:, sk] * s_in + _tn(v[:, sv], k_out[:, sk])

    in_specs, rbs = _gla_specs(n_lat, n_ctx, lambda i: i)
    return pl.pallas_call(
        body, name="gla_fwd", grid=(n_steps,), in_specs=in_specs,
        out_specs=[pl.BlockSpec((GLA_C, GLA_V), lambda i, rb=rb: (rb(i), 0)) for rb in rbs]
        + [pl.BlockSpec((2, 1, GLA_H, GLA_DV, GLA_DK), lambda i: (0, i, 0, 0, 0))],
        out_shape=[jax.ShapeDtypeStruct((R, GLA_V), ACT)] * 2 + [jax.ShapeDtypeStruct((2, n_steps, GLA_H, GLA_DV, GLA_DK), F32)],
        scratch_shapes=[pltpu.VMEM((2, GLA_H, GLA_DV, GLA_DK), F32)],
        compiler_params=_cparams(("arbitrary",)),
    )(*([parts, parts, parts, sm] * 2), uhi, ulo, bias)


def _gla_bwd(do, parts, sm, uhi, ulo, bias, hist, n_lat, n_ctx):
    R = parts.shape[0]
    n_steps = n_lat + n_ctx
    scale = GLA_DK ** -0.5
    step_of = lambda j: n_steps - 1 - j

    def body(*refs):
        ins, (uhi_ref, ulo_ref, bias_ref), do_refs, hist_ref = refs[:8], refs[8:11], refs[11:13], refs[13]
        outs, dst = refs[14:22], refs[22]

        @pl.when(pl.program_id(0) == 0)
        def _():
            dst[...] = jnp.zeros_like(dst)

        for d in range(2):
            q_ref, k_ref, v_ref, sm_ref = ins[4 * d:4 * d + 4]
            dq_ref, dk_ref, dv_ref, dp_ref = outs[4 * d:4 * d + 4]
            pre, mask, b_tot, e_q, e_k, e_in, e_out = _gla_gates(sm_ref[...], uhi_ref[d], ulo_ref[d], bias_ref[d], d)
            q, k, v = _f32(q_ref) * scale, _f32(k_ref), _bf(v_ref[...])
            dout = _bf(do_refs[d][...])
            k_out_f = k * e_out
            qb, kb, q_in, k_out, decay = _bf(q * e_q), _bf(k * e_k), _bf(q * e_in), _bf(k_out_f), jnp.exp(b_tot)
            dqs, dks, dk_outs, dss = [], [], [], []
            for h in range(GLA_H):
                sk, sv = slice(h * GLA_DK, (h + 1) * GLA_DK), slice(h * GLA_DV, (h + 1) * GLA_DV)
                s_in, ds = hist_ref[d, 0, h], dst[d, h]
                att = jnp.where(mask, _nt(qb[:, sk], kb[:, sk]), 0.0)
                datt = _bf(jnp.where(mask, _nt(dout[:, sv], v[:, sv]), 0.0))
                dv_ref[:, sv] = _tn(_bf(att), dout[:, sv]) + _nt(k_out[:, sk], _bf(ds))
                dqs.append(_nn(datt, kb[:, sk]) * e_q[:, sk] + _nn(dout[:, sv], _bf(s_in)) * e_in[:, sk])
                dk_o = _nn(v[:, sv], _bf(ds))
                dk_outs.append(dk_o)
                dks.append(_tn(datt, qb[:, sk]) * e_k[:, sk])
                dss.append(_colsum(ds * s_in))
                dst[d, h] = decay[:, sk] * ds + _tn(dout[:, sv], q_in[:, sk])
            dq, dk_out = jnp.concatenate(dqs, axis=1), jnp.concatenate(dk_outs, axis=1)
            dk = jnp.concatenate(dks, axis=1) + dk_out * e_out
            dq_ref[...] = dq * scale
            dk_ref[...] = dk
            db_tot = _colsum(dk_out * k_out_f) + decay * jnp.concatenate(dss, axis=1)
            dg = _x_nn(_tri(GLA_C, d, transpose=True).astype(BF16), dq * q - dk * k) + db_tot
            dp_ref[...] = dg * (1.0 / GLA_TAU) * _sigmoid(-pre)

    in_specs, rbs = _gla_specs(n_lat, n_ctx, step_of)
    in_specs += [pl.BlockSpec((GLA_C, GLA_V), lambda j, rb=rb: (rb(j), 0)) for rb in rbs]
    in_specs += [pl.BlockSpec((2, 1, GLA_H, GLA_DV, GLA_DK), lambda j: (0, step_of(j), 0, 0, 0))]
    out_specs, out_shape = [], []
    for rb in rbs:
        for w in (GLA_QK, GLA_QK, GLA_V, GLA_QK):
            out_specs.append(pl.BlockSpec((GLA_C, w), lambda j, rb=rb: (rb(j), 0)))
            out_shape.append(jax.ShapeDtypeStruct((R, w), F32))
    outs = pl.pallas_call(
        body, name="gla_bwd", grid=(n_steps,), in_specs=in_specs, out_specs=out_specs, out_shape=out_shape,
        scratch_shapes=[pltpu.VMEM((2, GLA_H, GLA_DV, GLA_DK), F32)],
        compiler_params=_cparams(("arbitrary",)),
    )(*([parts, parts, parts, sm] * 2), uhi, ulo, bias, do, do, hist)
    return [(outs[k], outs[4 + k]) for k in range(4)]


def _ssd_consts(dt_bias, a_log):
    sel = np.zeros((2, SSM_G, 128, 128), np.float32)
    for d, base in enumerate((SM_DTF, SM_DTB)):
        for g in range(SSM_G):
            for e in range(SSM_HPG):
                sel[d, g, base + SSM_HPG * g + e, e] = 1.0
    e512 = np.zeros((128, SSM_GW), np.float32)
    for e in range(SSM_HPG):
        e512[e, SSM_P * e:SSM_P * (e + 1)] = 1.0
    a_neg = -jnp.exp(a_log)
    pad = lambda v: jnp.pad(v.reshape(2, SSM_G, 1, SSM_HPG), ((0, 0), (0, 0), (0, 0), (0, 128 - SSM_HPG)))
    return dict(
        sel=jnp.asarray(sel, BF16), sel_t=jnp.asarray(sel.transpose(0, 1, 3, 2), BF16), e512_t=jnp.asarray(e512.T, BF16),
        dtb=pad(dt_bias), a=pad(a_neg), a512=jnp.repeat(a_neg, SSM_P, axis=1).reshape(2, SSM_G, 1, SSM_GW))


def _head_columns(x8):
    return [jnp.broadcast_to(x8[:, e:e + 1], (x8.shape[0], 128)) for e in range(SSM_HPG)]


def _head_layout(cols):
    low = lax.broadcasted_iota(jnp.int32, (1, 128), 1) < SSM_P
    return jnp.concatenate([jnp.where(low, cols[2 * j], cols[2 * j + 1]) for j in range(SSM_HPG // 2)], axis=1)


def _ssd_common(sm, sel, dtb, a_neg, a512, d):
    dtr8 = _nn_x(sm, sel) + dtb
    dt8 = _softplus(dtr8)
    a8 = a_neg * dt8
    mask = _tri(SSM_C, d)
    mask_t = _tri(SSM_C, d, transpose=True).astype(BF16)
    cum8 = _x_nn(mask.astype(BF16), a8)
    a_hi, a_lo = _split(a8)
    cum_t = _tn(a_hi, mask_t) + _tn(a_lo, mask_t)
    cum_cols = _head_columns(cum8)
    dt_exp = _head_layout(_head_columns(dt8))
    a_exp = a512 * dt_exp
    return dict(dtr8=dtr8, a8=a8, mask=mask, mask_t=mask_t, cum_t=cum_t, dt_exp=dt_exp, a_exp=a_exp,
                cum_exp=_head_layout(cum_cols), cum_cols=cum_cols, tot_exp=_colsum(a_exp))


def _ssd_decay(cm, e):
    diff = cm["cum_cols"][e] - cm["cum_t"][e:e + 1, :]
    return jnp.where(cm["mask"], jnp.exp(jnp.minimum(diff, 0.0)), 0.0)


SSM_GPS = 4


def _ssd_specs(n_lat, n_ctx, step_of):
    rbs = [lambda i, d=d: _chunk_row_block(d, step_of(i), n_lat, n_ctx) for d in range(2)]
    xw, nw = SSM_GPS * SSM_GW, SSM_GPS * SSM_N
    specs = []
    for rb in rbs:
        specs += [pl.BlockSpec((SSM_C, xw), lambda g, i, rb=rb: (rb(i), g)),
                  pl.BlockSpec((SSM_C, nw), lambda g, i, rb=rb: (rb(i), SSM_INNER // nw + g)),
                  pl.BlockSpec((SSM_C, nw), lambda g, i, rb=rb: (rb(i), (SSM_INNER + SSM_G * SSM_N) // nw + g)),
                  pl.BlockSpec((SSM_C, 128), lambda g, i, rb=rb: (rb(i), 0))]
    specs += [pl.BlockSpec((2, SSM_GPS, 128, 128), lambda g, i: (0, g, 0, 0)),
              pl.BlockSpec((2, SSM_GPS, 1, 128), lambda g, i: (0, g, 0, 0)),
              pl.BlockSpec((2, SSM_GPS, 1, 128), lambda g, i: (0, g, 0, 0)),
              pl.BlockSpec((2, SSM_GPS, 1, SSM_GW), lambda g, i: (0, g, 0, 0))]
    return specs, rbs


def _ssd_fwd(xbc, sm, k, n_lat, n_ctx, comm):
    R = xbc.shape[0]
    n_steps = n_lat + n_ctx

    def body(*refs):
        ins, (sel_ref, dtb_ref, a_ref, a512_ref), y_refs, hist_ref, st = refs[:8], refs[8:12], refs[12:14], refs[14], refs[15]

        @pl.when(pl.program_id(1) == 0)
        def _():
            st[...] = jnp.zeros_like(st)

        for d in range(2):
            x_ref, b_ref, c_ref, sm_ref = ins[4 * d:4 * d + 4]
            sm = sm_ref[...]
            for gg in range(SSM_GPS):
                sx, sn = slice(gg * SSM_GW, (gg + 1) * SSM_GW), slice(gg * SSM_N, (gg + 1) * SSM_N)
                cm = _ssd_common(sm, sel_ref[d, gg], dtb_ref[d, gg], a_ref[d, gg], a512_ref[d, gg], d)
                bm, cmat = _bf(b_ref[:, sn]), _bf(c_ref[:, sn])
                xdt = x_ref[:, sx].astype(F32) * cm["dt_exp"]
                cb = _nt(cmat, bm)
                ys = [_nn(_bf(cb * _ssd_decay(cm, e)), _bf(xdt[:, SSM_P * e:SSM_P * (e + 1)])) for e in range(SSM_HPG)]
                s_in = st[d, gg]
                hist_ref[d, 0, gg] = s_in
                y = jnp.concatenate(ys, axis=1) + jnp.exp(cm["cum_exp"]) * _nn(cmat, _bf(s_in))
                y_refs[d][:, sx] = y.astype(ACT)
                st[d, gg] = jnp.exp(cm["tot_exp"]) * s_in + _tn(bm, _bf(xdt * jnp.exp(cm["tot_exp"] - cm["cum_exp"])))

    in_specs, rbs = _ssd_specs(n_lat, n_ctx, lambda i: i)
    out_specs = [pl.BlockSpec((SSM_C, SSM_GPS * SSM_GW), lambda g, i, rb=rb: (rb(i), g)) for rb in rbs]
    out_specs += [pl.BlockSpec((2, 1, SSM_GPS, SSM_N, SSM_GW), lambda g, i: (0, i, g, 0, 0))]
    out_shape = [jax.ShapeDtypeStruct((R, SSM_INNER), ACT)] * 2 + [jax.ShapeDtypeStruct((2, n_steps, SSM_G, SSM_N, SSM_GW), F32)]
    args = [xbc, xbc, xbc, sm] * 2 + [k["sel"], k["dtb"], k["a"], k["a512"]]
    n_host_out = len(out_shape)
    outs = pl.pallas_call(
        _hosted(body, len(args), n_host_out, 1, comm, *_ssd_comm_steps(n_steps)), name="ssd_fwd",
        grid=(SSM_G // SSM_GPS, n_steps), in_specs=in_specs + [_ANY] * len(comm.arrays),
        out_specs=out_specs + [_ANY] * len(comm.out_shape), out_shape=out_shape + comm.out_shape,
        scratch_shapes=[pltpu.VMEM((2, SSM_GPS, SSM_N, SSM_GW), F32)] + comm.scratch,
        compiler_params=_cparams(("arbitrary", "arbitrary")),
    )(*args, *comm.arrays)
    return outs[:n_host_out], outs[n_host_out:]


def _ssd_comm_steps(n_steps):
    n_g = SSM_G // SSM_GPS
    at = lambda g, i: (pl.program_id(0) == g) & (pl.program_id(1) == i)
    half = (n_g // 2, n_steps // 2 if n_g % 2 else 0)
    return (lambda: at(0, 0)), (lambda: at(*half)), (lambda: at(n_g - 1, n_steps - 1))


def _ssd_bwd(dy, xbc, sm, k, hist, n_lat, n_ctx, comm):
    R = xbc.shape[0]
    n_steps = n_lat + n_ctx
    step_of = lambda j: n_steps - 1 - j

    def one(d, gg, x_ref, b_ref, c_ref, sm_ref, sel_ref, dtb_ref, a_ref, a512_ref, selt_ref, e512t_ref, dy_ref,
            hist_ref, dx_ref, db_ref, dc_ref, dsm_ref, acc_ref, dst):
        sx, sn = slice(gg * SSM_GW, (gg + 1) * SSM_GW), slice(gg * SSM_N, (gg + 1) * SSM_N)
        a_neg, e512_t = a_ref[d, gg], e512t_ref[...]
        cm = _ssd_common(sm_ref[...], sel_ref[d, gg], dtb_ref[d, gg], a_neg, a512_ref[d, gg], d)
        x, dyv = x_ref[:, sx].astype(F32), dy_ref[:, sx].astype(F32)
        bm, cmat = _bf(b_ref[:, sn]), _bf(c_ref[:, sn])
        xdt = x * cm["dt_exp"]
        cb = _nt(cmat, bm)
        s_in, ds = hist_ref[d, 0, gg], dst[d, gg]
        w = jnp.exp(cm["tot_exp"] - cm["cum_exp"])
        z = _nn(bm, _bf(ds))
        decay_in = jnp.exp(cm["cum_exp"])
        gy = _bf(dyv * decay_in)
        dcb = jnp.zeros((SSM_C, SSM_C), F32)
        dxs, crossing = [], []
        row = lax.broadcasted_iota(jnp.int32, (SSM_C, SSM_C), 0)
        col = lax.broadcasted_iota(jnp.int32, (SSM_C, SSM_C), 1)
        eye = (row == col).astype(BF16)
        before = (cm["mask_t"] - eye)
        for e in range(SSM_HPG):
            s = slice(SSM_P * e, SSM_P * (e + 1))
            lm = _ssd_decay(cm, e)
            dy_e = _bf(dyv[:, s])
            m_e = cb * lm
            dm_e = _nt(dy_e, _bf(xdt[:, s]))
            dcb = dcb + dm_e * lm
            dxs.append(_tn(_bf(m_e), dy_e))
            through = jnp.where(cm["mask"], _nn(_bf(dm_e * m_e), before), 0.0)
            crossing.append(_colsum(through))
        da_rows = jnp.concatenate(crossing + [jnp.zeros((128 - SSM_HPG, SSM_C), F32)], axis=0)
        r_hi, r_lo = _split(da_rows)
        da8_intra = _tn(r_hi, eye) + _tn(r_lo, eye)
        dx_state = w * z
        dxdt = jnp.concatenate(dxs, axis=1) + dx_state
        dcb = _bf(dcb)
        c_s = _nn(cmat, _bf(s_in))
        dc_ref[:, sn] = _nn(dcb, bm) + _nt(gy, _bf(s_in))
        db_ref[:, sn] = _tn(dcb, cmat) + _nt(_bf(w * xdt), _bf(ds))
        dst[d, gg] = jnp.exp(cm["tot_exp"]) * ds + _tn(cmat, gy)
        state_path = xdt * dx_state
        per_token = _nn_x(jnp.concatenate([dyv * decay_in * c_s - state_path, dxdt * x], axis=0), e512_t)
        totals = jnp.concatenate([_colsum(state_path), _colsum(ds * s_in), jnp.zeros((6, SSM_GW), F32)], axis=0)
        totals = _nn_x(totals, e512_t)
        tot8 = _colsum(cm["a8"])
        dtot8 = totals[0:1] + jnp.exp(tot8) * totals[1:2]
        da8 = da8_intra + _x_nn(cm["mask_t"], per_token[:SSM_C]) + dtot8
        ddt8 = da8 * a_neg + per_token[SSM_C:]
        dsm_ref[gg] = _nn_x(ddt8 * _sigmoid(cm["dtr8"]), selt_ref[d, gg])
        dx_ref[:, sx] = dxdt * cm["dt_exp"]
        acc_ref[d, gg, 0:1, :] += _colsum(da8 * cm["a8"])

    def body(*refs):
        ins, consts, (selt_ref, e512t_ref), dy_refs, hist_ref = refs[:8], refs[8:12], refs[12:14], refs[14:16], refs[16]
        outs, acc_ref, dst = refs[17:25], refs[25], refs[26]

        @pl.when(pl.program_id(1) == 0)
        def _():
            dst[...] = jnp.zeros_like(dst)
            acc_ref[...] = jnp.zeros_like(acc_ref)

        for d in range(2):
            for gg in range(SSM_GPS):
                one(d, gg, *ins[4 * d:4 * d + 4], *consts, selt_ref, e512t_ref, dy_refs[d], hist_ref,
                    *outs[4 * d:4 * d + 4], acc_ref, dst)

    xw, nw = SSM_GPS * SSM_GW, SSM_GPS * SSM_N
    in_specs, rbs = _ssd_specs(n_lat, n_ctx, step_of)
    in_specs += [pl.BlockSpec((2, SSM_GPS, 128, 128), lambda g, j: (0, g, 0, 0)), pl.BlockSpec((SSM_GW, 128), lambda g, j: (0, 0))]
    in_specs += [pl.BlockSpec((SSM_C, xw), lambda g, j, rb=rb: (rb(j), g)) for rb in rbs]
    in_specs += [pl.BlockSpec((2, 1, SSM_GPS, SSM_N, SSM_GW), lambda g, j: (0, step_of(j), g, 0, 0))]
    out_specs, out_shape = [], []
    for rb in rbs:
        out_specs += [pl.BlockSpec((SSM_C, xw), lambda g, j, rb=rb: (rb(j), g)),
                      pl.BlockSpec((SSM_C, nw), lambda g, j, rb=rb: (rb(j), g)),
                      pl.BlockSpec((SSM_C, nw), lambda g, j, rb=rb: (rb(j), g)),
                      pl.BlockSpec((SSM_GPS, SSM_C, 128), lambda g, j, rb=rb: (g, rb(j), 0))]
        out_shape += [jax.ShapeDtypeStruct((R, SSM_INNER), F32), jax.ShapeDtypeStruct((R, SSM_G * SSM_N), F32),
                      jax.ShapeDtypeStruct((R, SSM_G * SSM_N), F32), jax.ShapeDtypeStruct((SSM_G, R, 128), F32)]
    out_specs.append(pl.BlockSpec((2, SSM_GPS, 8, 128), lambda g, j: (0, g, 0, 0)))
    out_shape.append(jax.ShapeDtypeStruct((2, SSM_G, 8, 128), F32))
    args = [xbc, xbc, xbc, sm] * 2 + [k["sel"], k["dtb"], k["a"], k["a512"], k["sel_t"], k["e512_t"], dy, dy, hist]
    n_host_out = len(out_shape)
    outs = pl.pallas_call(
        _hosted(body, len(args), n_host_out, 1, comm, *_ssd_comm_steps(n_steps)), name="ssd_bwd",
        grid=(SSM_G // SSM_GPS, n_steps), in_specs=in_specs + [_ANY] * len(comm.arrays),
        out_specs=out_specs + [_ANY] * len(comm.out_shape), out_shape=out_shape + comm.out_shape,
        scratch_shapes=[pltpu.VMEM((2, SSM_GPS, SSM_N, SSM_GW), F32)] + comm.scratch,
        compiler_params=_cparams(("arbitrary", "arbitrary")),
    )(*args, *comm.arrays)
    return [(outs[n], outs[4 + n]) for n in range(4)] + [outs[8]], outs[n_host_out:]


def _gla_assemble(dq, dk, dv, dparts, tr):
    R = dq[0].shape[0]
    qk = GLA_H * GLA_DK

    def body(dqf_ref, dqb_ref, dkf_ref, dkb_ref, dvf_ref, dvb_ref, _, o_ref):
        o_ref[:, 0:qk] = (dqf_ref[...] + dqb_ref[...]).astype(BF16)
        o_ref[:, qk:2 * qk] = (dkf_ref[...] + dkb_ref[...]).astype(BF16)
        o_ref[:, 2 * qk:] = (dvf_ref[...] + dvb_ref[...]).astype(BF16)

    return pl.pallas_call(
        body, name="gla_assemble", grid=(R // tr,), in_specs=[_row_spec(tr, qk)] * 4 + [_row_spec(tr, D)] * 2 + [_ANY],
        out_specs=_dparts_out(tr, 2 * D, COL_Q), out_shape=jax.ShapeDtypeStruct(dparts.shape, BF16),
        input_output_aliases={6: 0}, compiler_params=_cparams(("arbitrary",)),
    )(*dq, *dk, *dv, dparts)


def _small_assemble(dp, dsm, sm, ut_hi, ut_lo, dparts, tr):
    R = sm.shape[0]
    qk = GLA_H * GLA_DK

    def body(dpf_ref, dpb_ref, dsmf_ref, dsmb_ref, sm_ref, uth_ref, utl_ref, _, o_ref, dup_ref, acc_ref, acc2_ref):
        @pl.when(pl.program_id(0) == 0)
        def _():
            dup_ref[...] = jnp.zeros_like(dup_ref)
            acc_ref[...] = jnp.zeros_like(acc_ref)
            acc2_ref[...] = jnp.zeros_like(acc2_ref)

        ssd = dsmf_ref[0] + dsmb_ref[0]
        for g in range(1, SSM_G):
            ssd = ssd + (dsmf_ref[g] + dsmb_ref[g])
        acc2_ref[0:1, :] += _colsum(ssd)
        sm_hi, sm_lo = _split(sm_ref[...])
        out = ssd
        for d, dp_ref in enumerate((dpf_ref, dpb_ref)):
            dpd = dp_ref[...]
            out = out + _nn3(dpd, uth_ref[d], utl_ref[d])
            p_hi, p_lo = _split(dpd)
            dup_ref[d] += _tn(sm_hi, p_hi) + _tn(sm_lo, p_hi) + _tn(sm_hi, p_lo)
            acc_ref[d:d + 1, :] += _colsum(dpd)
        o_ref[...] = out.astype(BF16)

    return pl.pallas_call(
        body, name="small_assemble", grid=(R // tr,),
        in_specs=[_row_spec(tr, qk)] * 2 + [pl.BlockSpec((SSM_G, tr, 128), lambda i: (0, i, 0))] * 2
        + [_row_spec(tr, 128), pl.BlockSpec((2, qk, 128), lambda i: (0, 0, 0)),
           pl.BlockSpec((2, qk, 128), lambda i: (0, 0, 0)), _ANY],
        out_specs=[_dparts_out(tr, 128, COL_SM), pl.BlockSpec((2, 128, qk), lambda i: (0, 0, 0)), _acc_spec(qk), _acc_spec(128)],
        out_shape=[jax.ShapeDtypeStruct(dparts.shape, BF16), jax.ShapeDtypeStruct((2, 128, qk), F32),
                   jax.ShapeDtypeStruct((8, qk), F32), jax.ShapeDtypeStruct((8, 128), F32)],
        input_output_aliases={7: 0}, compiler_params=_cparams(("arbitrary",)),
    )(*dp, *dsm, sm, ut_hi, ut_lo, dparts)


ADA_ROWS = 16
ADA_TILE = 512


def _dot3_f32(a, b, ca, cb):
    a_hi, a_lo = _split(a)
    b_hi, b_lo = _split(b)
    return _dg(a_hi, b_hi, ca, cb) + _dg(a_lo, b_hi, ca, cb) + _dg(a_hi, b_lo, ca, cb)


def _ada_fwd(cvec, w, b):
    cols = w.shape[1]

    def body(c_ref, w_ref, b_ref, o_ref):
        o_ref[...] = _dot3_f32(_silu(c_ref[...]), w_ref[...], 1, 0) + b_ref[...]

    return pl.pallas_call(
        body, name="ada_fwd", grid=(cols // ADA_TILE,),
        in_specs=[pl.BlockSpec((ADA_ROWS, D), lambda j: (0, 0)), pl.BlockSpec((D, ADA_TILE), lambda j: (0, j)),
                  pl.BlockSpec((1, ADA_TILE), lambda j: (0, j))],
        out_specs=pl.BlockSpec((ADA_ROWS, ADA_TILE), lambda j: (0, j)), out_shape=jax.ShapeDtypeStruct((ADA_ROWS, cols), F32),
        compiler_params=_cparams(("arbitrary",)),
    )(cvec, w, b)


def _adam(w, g, m, v):
    m2 = ADAM_B1 * m + (1.0 - ADAM_B1) * g
    v2 = ADAM_B2 * v + (1.0 - ADAM_B2) * (g * g)
    m_hat = m2 / (1.0 - ADAM_B1 ** ADAM_STEP)
    v_hat = v2 / (1.0 - ADAM_B2 ** ADAM_STEP)
    return -ADAM_LR * (m_hat / (jnp.sqrt(v_hat) + ADAM_EPS) + ADAM_WD * w), m2, v2


def _wada_bwd_adam(cvec, dada, w, m, v):
    rows, cols = w.shape
    tr = _tile(rows, 256, 128)

    def body(c_ref, d_ref, w_ref, m_ref, v_ref, g_ref, dl_ref, m2_ref, v2_ref, p_ref):
        wv = w_ref[...]
        g = _dot3_f32(_silu(c_ref[...]), d_ref[...], 0, 0)
        g_ref[...] = g
        dl_ref[...], m2_ref[...], v2_ref[...] = _adam(wv, g, m_ref[...], v_ref[...])
        p_ref[...] = _dot3_f32(d_ref[...], wv, 1, 1)

    blk = pl.BlockSpec((tr, cols), lambda i: (i, 0))
    return pl.pallas_call(
        body, name="wada_bwd_adam", grid=(rows // tr,),
        in_specs=[pl.BlockSpec((ADA_ROWS, tr), lambda i: (0, i)), pl.BlockSpec((ADA_ROWS, cols), lambda i: (0, 0)), blk, blk, blk],
        out_specs=[blk, blk, blk, blk, pl.BlockSpec((ADA_ROWS, tr), lambda i: (0, i))],
        out_shape=[jax.ShapeDtypeStruct((rows, cols), F32)] * 4 + [jax.ShapeDtypeStruct((ADA_ROWS, rows), F32)],
        compiler_params=_cparams(("arbitrary",)),
    )(cvec, dada, w, m, v)


def _reduce_adam(parts8, w, m, v, name):
    rows, cols = w.shape
    tr = _tile(rows, 64, 16)

    def body(p_ref, w_ref, m_ref, v_ref, g_ref, dl_ref, m2_ref, v2_ref):
        g = p_ref[0].astype(F32) + p_ref[N_CHIPS].astype(F32)
        for j in range(1, N_CHIPS):
            g = g + (p_ref[j].astype(F32) + p_ref[N_CHIPS + j].astype(F32))
        g_ref[...] = g
        dl_ref[...], m2_ref[...], v2_ref[...] = _adam(w_ref[...], g, m_ref[...], v_ref[...])

    blk = pl.BlockSpec((tr, cols), lambda i: (i, 0))
    return pl.pallas_call(
        body, name=name, grid=(rows // tr,), in_specs=[pl.BlockSpec((N_DEV, tr, cols), lambda i: (0, i, 0)), blk, blk, blk],
        out_specs=[blk] * 4, out_shape=[jax.ShapeDtypeStruct((rows, cols), F32)] * 4, compiler_params=_cparams(("arbitrary",)),
    )(parts8, w, m, v)


SMALL_W = 1024


def _sum8(g8):
    rows = g8.shape[1]

    def body(g_ref, o_ref):
        s = g_ref[0]
        for j in range(1, N_DEV):
            s = s + g_ref[j]
        o_ref[...] = s

    return pl.pallas_call(
        body, name="sum8", out_shape=jax.ShapeDtypeStruct((rows, SMALL_W), F32),
        in_specs=[pl.BlockSpec(memory_space=pltpu.VMEM)], out_specs=pl.BlockSpec(memory_space=pltpu.VMEM),
        compiler_params=pltpu.CompilerParams(vmem_limit_bytes=VMEM_LIMIT),
    )(g8)


def _cctx_grad(p8, c_ctx):
    def body(p_ref, c_ref, o_ref):
        s = p_ref[0]
        for chip in range(1, N_CHIPS):
            s = s + p_ref[2 * chip]
        o_ref[...] = s * _dsilu(c_ref[...])

    return pl.pallas_call(
        body, name="cctx_grad", out_shape=jax.ShapeDtypeStruct((1, D), F32),
        in_specs=[pl.BlockSpec(memory_space=pltpu.VMEM)] * 2, out_specs=pl.BlockSpec(memory_space=pltpu.VMEM),
    )(p8, c_ctx)


def _adam_small(w, g, m, v):
    def body(w_ref, g_ref, m_ref, v_ref, dl_ref, m2_ref, v2_ref):
        dl_ref[...], m2_ref[...], v2_ref[...] = _adam(w_ref[...], g_ref[...], m_ref[...], v_ref[...])

    vm = pl.BlockSpec(memory_space=pltpu.VMEM)
    return pl.pallas_call(
        body, name="adam_small", out_shape=[jax.ShapeDtypeStruct(w.shape, F32)] * 3, in_specs=[vm] * 4, out_specs=[vm] * 3,
        compiler_params=pltpu.CompilerParams(vmem_limit_bytes=VMEM_LIMIT),
    )(w, g, m, v)


def _pack(vecs, width=SMALL_W, row_mult=8):
    flat = jnp.concatenate([v.reshape(-1).astype(F32) for v in vecs])
    n = flat.shape[0]
    rows = -(-n // (width * row_mult)) * row_mult
    return jnp.pad(flat, (0, rows * width - n)).reshape(rows, width)


def _unpack(packed, shapes):
    flat = packed.reshape(-1)
    out, off = [], 0
    for s in shapes:
        n = int(np.prod(s))
        out.append(flat[off:off + n].reshape(s))
        off += n
    return out


WEIGHTS = ('c_ctx', 'w_ada', 'b_ada', 'norm1_w', 'w_in', 'gla_up_f', 'gla_bias_f', 'gla_up_b', 'gla_bias_b', 'gla_norm_w',
           'conv_w', 'conv_b', 'dt_bias_f', 'dt_bias_b', 'a_log_f', 'a_log_b', 'd_skip', 'ssm_norm_w', 'w_pa', 'w_pb', 'w_out',
           'norm2_w', 'w_gate', 'w_up', 'w_down', 'final_norm_w')
BIG = ('w_in', 'w_pa', 'w_pb', 'w_out', 'w_gate', 'w_up', 'w_down')
COL_SHARDED = ('w_in', 'w_gate', 'w_up')
SMALL_SHARDED = ('gla_up_f', 'gla_up_b', 'conv_w')
ROW_TILE = 256


def _blocks_to_full(g4, name):
    n, r, c = g4.shape
    return g4.transpose(1, 0, 2).reshape(r, n * c) if name in COL_SHARDED else g4.reshape(n * r, c)


def _full_to_blocks(full, name):
    r, c = full.shape
    if name in COL_SHARDED:
        return full.reshape(r, N_CHIPS, c // N_CHIPS).transpose(1, 0, 2)
    return full.reshape(N_CHIPS, r // N_CHIPS, c)


def _permute_in(w_in_full):
    off = np.concatenate([[0], np.cumsum(IN_WIDTHS)])
    cols = [w_in_full[:, off[p]:off[p + 1]] for p in PERM]
    return jnp.concatenate(cols + [jnp.zeros((w_in_full.shape[0], SMALL_PAD), w_in_full.dtype)], axis=1)


def _unpermute_in(wp):
    off = np.concatenate([[0], np.cumsum([IN_WIDTHS[p] for p in PERM])])
    pieces = {p: wp[:, off[i]:off[i + 1]] for i, p in enumerate(PERM)}
    return jnp.concatenate([pieces[p] for p in range(len(IN_WIDTHS))], axis=1)


def _chip_cols(full, chip, n):
    return lax.dynamic_slice_in_dim(full, chip * n, n, axis=1)


def kernel(x, c, ctx, c_ctx, w_ada, b_ada, norm1_w, w_in, gla_up_f, gla_bias_f, gla_up_b, gla_bias_b, gla_norm_w, conv_w, conv_b, dt_bias_f, dt_bias_b, a_log_f, a_log_b, d_skip, ssm_norm_w, w_pa, w_pb, w_out, norm2_w, w_gate, w_up, w_down, final_norm_w, loss_target, m_c_ctx, m_w_ada, m_b_ada, m_norm1_w, m_w_in, m_gla_up_f, m_gla_bias_f, m_gla_up_b, m_gla_bias_b, m_gla_norm_w, m_conv_w, m_conv_b, m_dt_bias_f, m_dt_bias_b, m_a_log_f, m_a_log_b, m_d_skip, m_ssm_norm_w, m_w_pa, m_w_pb, m_w_out, m_norm2_w, m_w_gate, m_w_up, m_w_down, m_final_norm_w, v_c_ctx, v_w_ada, v_b_ada, v_norm1_w, v_w_in, v_gla_up_f, v_gla_bias_f, v_gla_up_b, v_gla_bias_b, v_gla_norm_w, v_conv_w, v_conv_b, v_dt_bias_f, v_dt_bias_b, v_a_log_f, v_a_log_b, v_d_skip, v_ssm_norm_w, v_w_pa, v_w_pb, v_w_out, v_norm2_w, v_w_gate, v_w_up, v_w_down, v_final_norm_w):
    given = dict(locals())
    W = {n: given[n] for n in WEIGHTS}
    M = {n: given["m_" + n] for n in WEIGHTS}
    V = {n: given["v_" + n] for n in WEIGHTS}
    L, Lc = x.shape[1], ctx.shape[1]
    tr = ROW_TILE
    assert L % tr == 0 and Lc % tr == 0 and L % Lc == 0 and Lc % SSM_C == 0
    n_lat_tiles = L // tr
    xi, yi, ci = _place()
    chip, me = 2 * xi + yi, 4 * xi + 2 * yi + ci
    x2, ctx2 = x[0], ctx[0]

    g0 = _allgather_small(_pack([c[0]] + [W[n][0] for n in SMALL_SHARDED]), "gather_c")
    g0 = g0.reshape(N_DEV, -1)
    c_all = g0[:, :D]
    small_full, off = {}, D
    for n in SMALL_SHARDED:
        r, cols = W[n].shape[1:]
        small_full[n] = jnp.concatenate([g0[2 * k, off:off + r * cols].reshape(r, cols) for k in range(N_CHIPS)], axis=1)
        off += r * cols
    up_f, up_b, conv_w_full = (small_full[n] for n in SMALL_SHARDED)

    cvec = jnp.zeros((ADA_ROWS, D), F32).at[:N_DEV].set(c_all).at[N_DEV].set(c_ctx)
    ada_cols = w_ada.shape[2]
    ada_part = _ada_fwd(cvec, w_ada[0], _chip_cols(b_ada, chip, ada_cols))
    g1_all = _allgather_small(ada_part, "gather_ada")
    ada_full = jnp.concatenate([g1_all[2 * k] for k in range(N_CHIPS)], axis=1)
    mine = lax.dynamic_slice_in_dim(ada_full, me, 1, axis=0)
    sh1, sc1, g1, sh2, sc2, g2 = (mine[:, k * D:(k + 1) * D] for k in range(6))
    csh1, csc1 = ada_full[N_DEV:N_DEV + 1, :D], ada_full[N_DEV:N_DEV + 1, D:2 * D]
    mod = jnp.stack([jnp.stack([sh1, sc1]), jnp.stack([csh1, csc1])])

    full = {'w_in': _blocks_to_full(_gather_split(w_in[0].astype(BF16), "gather_w_in"), 'w_in')}
    wp = _permute_in(full['w_in'])
    later = [n for n in BIG if n != 'w_in']

    def lr_rows(up, base):
        return jnp.zeros((128, GLA_H * GLA_DK), F32).at[base:base + GLA_RANK].set(up)
    u2 = jnp.stack([lr_rows(up_f, SM_LRF), lr_rows(up_b, SM_LRB)])
    u2_hi = u2.astype(BF16)
    u2_lo = (u2 - u2_hi.astype(F32)).astype(BF16)
    ut = u2.transpose(0, 2, 1)
    ut_hi = ut.astype(BF16)
    ut_lo = (ut - ut_hi.astype(F32)).astype(BF16)
    gbias = jnp.stack([gla_bias_f, gla_bias_b])
    kc = _ssd_consts(jnp.stack([dt_bias_f[0], dt_bias_b[0]]), jnp.stack([a_log_f[0], a_log_b[0]]))
    gw4 = jnp.tile(gla_norm_w, (1, GLA_H))
    dskip_exp = jnp.repeat(d_skip, SSM_P, axis=1)
    n_gla = (L // GLA_C, Lc // GLA_C)
    n_ssd = (L // SSM_C, Lc // SSM_C)

    h1 = _norm_mod(x2, ctx2, norm1_w, mod, n_lat_tiles, tr)
    parts = _mm(h1, wp, "nn", ACT, "mm_in", tm=1408, tn=1152)
    sm = _mm(h1, wp[:, COL_SM:], "nn", F32, "mm_in_small", tm=1408)
    xbc = _conv_fwd(parts, conv_w_full, conv_b, L // Lc, Lc)
    *o2, gla_hist = _gla_fwd(parts, sm, u2_hi, u2_lo, gbias, *n_gla)
    (*y2, ssd_hist), gathered = _ssd_fwd(xbc, sm, kc, *n_ssd, _gather_comm([W[n][0].astype(BF16) for n in later]))
    full.update({n: _blocks_to_full(g, n) for n, g in zip(later, gathered)})
    oan = _gla_out(o2, parts, gw4, tr)
    obn = _ssd_out(y2, xbc, parts, dskip_exp, ssm_norm_w, tr)
    ya = _mm(oan, full['w_pa'], "nn", ACT, "mm_pa", tm=1408)
    yb = _mm(obn, full['w_pb'], "nn", ACT, "mm_pb", tm=1408)
    merged = _merge(ya, yb, parts, tr)
    mix = _mm(merged, full['w_out'], "nn", ACT, "mm_out", tm=1408)
    h2, u = _resid_norm_mod(x2, ctx2, mix, g1, norm2_w, sh2, sc2, n_lat_tiles, tr)
    gp = _mm(u, full['w_gate'], "nn", ACT, "mm_gate", tm=1408, tn=1408)
    up = _mm(u, full['w_up'], "nn", ACT, "mm_up", tm=1408, tn=1408)
    act = _swiglu_act(gp, up, tr)
    f = _mm(act, full['w_down'], "nn", ACT, "mm_down", tm=1408, tk=D_FF)
    dh3, df, acc_loss = _loss_head(h2, f, loss_target[0], g2, final_norm_w[None], n_lat_tiles, tr)

    dw = {}
    da = _mm(df, full['w_down'], "nt", ACT, "mm_down_dx", tm=1408, tn=1408)
    dw['w_down'] = _mm(act, df, "tn", BF16, "mm_down_dw", tm=1408, tk=2816)
    dgp, dup = _swiglu_act_bwd(da, gp, up, tr)
    du_a = _mm(dgp, full['w_gate'], "nt", ACT, "mm_gate_dx", tm=1408, tk=D_FF)
    du_b = _mm(dup, full['w_up'], "nt", ACT, "mm_up_dx", tm=1408, tk=D_FF)
    dw['w_gate'] = _mm(u, dgp, "tn", BF16, "mm_gate_dw", tm=1024, tn=1408, tk=2816)
    dw['w_up'] = _mm(u, dup, "tn", BF16, "mm_up_dw", tm=1024, tn=1408, tk=2816)
    dh2, dmix, acc_ffn = _ffn_in_bwd(du_a, du_b, h2, dh3, mix, sc2, g1, norm2_w, tr)
    dmerged = _mm(dmix, full['w_out'], "nt", ACT, "mm_out_dx", tm=1408)
    dw['w_out'] = _mm(merged, dmix, "tn", BF16, "mm_out_dw", tm=1024, tk=2816)
    dya, dyb, dparts = _merge_bwd(dmerged, ya, yb, parts, lax.empty((L + Lc, PW), BF16), tr)
    doan = _mm(dya, full['w_pa'], "nt", ACT, "mm_pa_dx", tm=1408)
    dw['w_pa'] = _mm(oan, dya, "tn", BF16, "mm_pa_dw", tm=1024, tk=2816)
    dobn = _mm(dyb, full['w_pb'], "nt", ACT, "mm_pb_dx", tm=1408)
    dw['w_pb'] = _mm(obn, dyb, "tn", BF16, "mm_pb_dw", tm=1024, tk=2816)
    do, dparts, acc_gla = _gla_out_bwd(doan, o2, parts, gw4, dparts, tr)
    dq, dk, dv, dpre = _gla_bwd(do, parts, sm, u2_hi, u2_lo, gbias, gla_hist, *n_gla)
    dy, dparts, acc_ssd = _ssd_out_bwd(dobn, y2, xbc, parts, dskip_exp, ssm_norm_w, dparts, tr)
    (dx_scan, db_scan, dc_scan, dsm, acc_alog), exchanged = _ssd_bwd(
        dy, xbc, sm, kc, ssd_hist, *n_ssd, _exchange_comm([_full_to_blocks(dw[n], n) for n in later]))
    exchanged = dict(zip(later, exchanged))
    dparts, acc_conv = _conv_bwd(dx_scan, db_scan, dc_scan, dy, dskip_exp, parts, conv_w_full, conv_b, dparts, L // Lc, Lc)
    dparts = _gla_assemble(dq, dk, dv, dparts, tr)
    dparts, dup_gla, acc_gbias, acc_dtb = _small_assemble(dpre, dsm, sm, ut_hi, ut_lo, dparts, tr)
    dw['w_in'] = _unpermute_in(_mm(h1, dparts, "tn", BF16, "mm_in_dw", tm=1024, tn=1152, tk=2816))
    dh1, (exchanged['w_in'],) = _mm(dparts, wp, "nt", F32, "mm_in_dx", tm=768, tk=3456,
                                    comm=_exchange_comm([_full_to_blocks(dw['w_in'], 'w_in')]))
    dx, acc_n1 = _norm1_bwd(dh1, x2, ctx2, dh2, norm1_w, mod, n_lat_tiles, tr)

    partial = dict(
        norm1_w=acc_n1[0, 2] + acc_n1[1, 2],
        gla_up_f=dup_gla[0, SM_LRF:SM_LRF + GLA_RANK], gla_bias_f=acc_gbias[0],
        gla_up_b=dup_gla[1, SM_LRB:SM_LRB + GLA_RANK], gla_bias_b=acc_gbias[1],
        gla_norm_w=acc_gla[0].reshape(GLA_H, GLA_DV).sum(0),
        conv_w=acc_conv[:SSM_CONV], conv_b=acc_conv[SSM_CONV],
        dt_bias_f=acc_dtb[0, SM_DTF:SM_DTF + SSM_HEADS], dt_bias_b=acc_dtb[0, SM_DTB:SM_DTB + SSM_HEADS],
        a_log_f=acc_alog[0, :, 0, :SSM_HPG], a_log_b=acc_alog[1, :, 0, :SSM_HPG],
        d_skip=acc_ssd[1].reshape(SSM_HEADS, SSM_P).sum(1), ssm_norm_w=acc_ssd[0],
        norm2_w=acc_ffn[2], final_norm_w=acc_loss[0],
    )
    dada = jnp.concatenate([acc_n1[0, 1], acc_n1[0, 0], acc_ffn[3], acc_ffn[1], acc_ffn[0], acc_loss[1]])
    dada_ctx = jnp.concatenate([acc_n1[1, 1], acc_n1[1, 0], jnp.zeros((4 * D,), F32)])
    names = list(partial)
    payload = [partial[n] for n in names] + [dada + dada_ctx, dada_ctx, acc_loss[2], dada]
    sizes = [int(np.prod(p.shape)) for p in payload]
    g8 = _allgather_small(_pack(payload), "gather_small_grads")
    summed = _unpack(_sum8(g8), [(s,) for s in sizes])
    grads = {n: s.reshape(W[n].shape if n not in SMALL_SHARDED else partial[n].shape) for n, s in zip(names, summed)}
    grads['b_ada'] = summed[len(names)].reshape(b_ada.shape)
    dada_ctx_sum = summed[len(names) + 1]
    loss = jnp.sum(summed[len(names) + 2])
    dada_all = g8.reshape(N_DEV, -1)[:, sum(sizes[:-1]):sum(sizes)]

    dada16 = jnp.zeros((ADA_ROWS, ada_cols), F32)
    dada16 = dada16.at[:N_DEV].set(_chip_cols(dada_all, chip, ada_cols)).at[N_DEV].set(_chip_cols(dada_ctx_sum[None], chip, ada_cols)[0])
    g_wada, dl_wada, m_wada, v_wada, p16 = _wada_bwd_adam(cvec, dada16, w_ada[0], m_w_ada[0], v_w_ada[0])
    p8 = _allgather_small(p16[N_DEV:], "gather_cctx")
    grads['c_ctx'] = _cctx_grad(p8[:, 0:1, :], c_ctx[None])[0]
    for n in SMALL_SHARDED:
        grads[n] = _chip_cols(grads[n], chip, W[n].shape[2])[None]

    small = [n for n in WEIGHTS if n not in BIG and n != 'w_ada']
    shapes = [W[n].shape for n in small]
    dl_s, m_s, v_s = _adam_small(*[_pack([d[n] for n in small]) for d in (W, grads, M, V)])
    delta = dict(zip(small, _unpack(dl_s, shapes)))
    new_m = dict(zip(small, _unpack(m_s, shapes)))
    new_v = dict(zip(small, _unpack(v_s, shapes)))
    grads['w_ada'], delta['w_ada'], new_m['w_ada'], new_v['w_ada'] = g_wada[None], dl_wada[None], m_wada[None], v_wada[None]

    for n in BIG:
        g, dl, m2, v2 = _reduce_adam(exchanged[n], W[n][0], M[n][0], V[n][0], "adam_" + n)
        grads[n], delta[n], new_m[n], new_v[n] = g[None], dl[None], m2[None], v2[None]

    return (loss, dx[None], *[grads[n] for n in WEIGHTS], *[delta[n] for n in WEIGHTS],
            *[new_m[n] for n in WEIGHTS], *[new_v[n] for n in WEIGHTS])
```

```python
import functools

import numpy as np
import jax
import jax.numpy as jnp
from jax import lax
from jax.experimental import pallas as pl
from jax.experimental.pallas import tpu as pltpu

F32 = jnp.float32
BF16 = jnp.bfloat16
MESH = pl.DeviceIdType.MESH

D = 1024
EPS = 1e-6
GRID_W = 64
GLA_H, GLA_DK, GLA_DV, GLA_RANK, GLA_TAU = 4, 128, 256, 16, 16.0
GLA_C = 128
SSM_INNER, SSM_P, SSM_HEADS, SSM_G, SSM_HPG, SSM_N = 2048, 64, 32, 4, 8, 128
SSM_C = 128
SSM_CONV, CONV_LEFT = 4, 2
D_FF = 2816
IN_WIDTHS = (512, 512, 1024, 1024, 16, 16, 2048, 2048, 512, 512, 32, 32, 1024, 1024)
D_IN = sum(IN_WIDTHS)
PERM = (6, 7, 8, 9, 3, 0, 1, 2, 12, 13, 4, 5, 10, 11)
PW = 10368
SMALL_PAD = PW - D_IN
COL_Z, COL_XBC, COL_R, COL_Q, COL_K, COL_V, COL_GA, COL_GB, COL_SM = 0, 2048, 5120, 6144, 6656, 7168, 8192, 9216, 10240
SM_LRF, SM_LRB, SM_DTF, SM_DTB = 0, 16, 32, 64
EXP_CLAMP = 80.0
ADAM_LR, ADAM_B1, ADAM_B2, ADAM_EPS, ADAM_WD, ADAM_STEP = 0.001, 0.9, 0.999, 1e-08, 0.01, 10
N_CHIPS, N_DEV = 4, 8
VMEM_LIMIT = 56 * 1024 * 1024


def _cparams(sem=None):
    return pltpu.CompilerParams(dimension_semantics=sem, vmem_limit_bytes=VMEM_LIMIT)


def _dg(a, b, ca, cb):
    return lax.dot_general(a, b, (((ca,), (cb,)), ((), ())), preferred_element_type=F32)


def _nn(a, b):
    return _dg(a, b, 1, 0)


def _nt(a, b):
    return _dg(a, b, 1, 1)


def _tn(a, b):
    return _dg(a, b, 0, 0)


def _bf(x):
    return x.astype(BF16)


def _split(x):
    hi = x.astype(BF16)
    return hi, (x - hi.astype(F32)).astype(BF16)


def _nn_x(a, b_exact):
    hi, lo = _split(a)
    return _nn(hi, b_exact) + _nn(lo, b_exact)


def _x_nn(a_exact, b):
    hi, lo = _split(b)
    return _nn(a_exact, hi) + _nn(a_exact, lo)


def _nn3(a, b_hi, b_lo):
    hi, lo = _split(a)
    return _nn(hi, b_hi) + _nn(lo, b_hi) + _nn(hi, b_lo)


def _sigmoid(x):
    return 1.0 / (1.0 + jnp.exp(-x))


def _silu(x):
    return x * _sigmoid(x)


def _dsilu(x):
    s = _sigmoid(x)
    return s * (1.0 + x * (1.0 - s))


def _softplus(x):
    return jnp.maximum(x, 0.0) + jnp.log(1.0 + jnp.exp(-jnp.abs(x)))


def _log_sigmoid(x):
    return jnp.minimum(x, 0.0) - jnp.log(1.0 + jnp.exp(-jnp.abs(x)))


def _tile(n, target, mult=8):
    best = None
    for t in range(mult, min(n, target) + 1, mult):
        if n % t == 0:
            best = t
    assert best is not None, (n, target, mult)
    return best


def _mm(a, b, mode, out_dtype, name, tm=512, tn=1024, tk=2048, comm=None):
    if mode == "nn":
        (M, K), N = a.shape, b.shape[1]
    elif mode == "nt":
        (M, K), N = a.shape, b.shape[0]
    else:
        (K, M), N = a.shape, b.shape[1]
    tm, tn, tk = _tile(M, tm, 128), _tile(N, tn, 128), _tile(K, tk, 128)
    nk = K // tk
    ca, cb = {"nn": (1, 0), "nt": (1, 1), "tn": (0, 0)}[mode]

    def body(a_ref, b_ref, o_ref, *acc):
        part = _dg(a_ref[...], b_ref[...], ca, cb)
        if nk == 1:
            o_ref[...] = part.astype(out_dtype)
        else:
            k = pl.program_id(2)

            @pl.when(k == 0)
            def _():
                acc[0][...] = part

            @pl.when(k > 0)
            def _():
                acc[0][...] += part

            @pl.when(k == nk - 1)
            def _():
                o_ref[...] = acc[0][...].astype(out_dtype)

    a_spec = pl.BlockSpec((tk, tm), lambda i, j, k: (k, i)) if mode == "tn" else pl.BlockSpec((tm, tk), lambda i, j, k: (i, k))
    b_spec = pl.BlockSpec((tn, tk), lambda i, j, k: (j, k)) if mode == "nt" else pl.BlockSpec((tk, tn), lambda i, j, k: (k, j))
    gi, gj = M // tm, N // tn
    scratch = [pltpu.VMEM((tm, tn), F32)] if nk > 1 else []
    out_spec, out_shape = pl.BlockSpec((tm, tn), lambda i, j, k: (i, j)), jax.ShapeDtypeStruct((M, N), out_dtype)
    if comm is None:
        return pl.pallas_call(
            body, name=name, grid=(gi, gj, nk), in_specs=[a_spec, b_spec], out_specs=out_spec, out_shape=out_shape,
            scratch_shapes=scratch, compiler_params=_cparams(("arbitrary", "arbitrary", "arbitrary")),
        )(a, b)
    at = lambda i, j, k: (pl.program_id(0) == i) & (pl.program_id(1) == j) & (pl.program_id(2) == k)
    hosted = _hosted(body, 2, 1, len(scratch), comm, lambda: at(0, 0, 0), lambda: at(gi - 1, 0, 0),
                     lambda: at(gi - 1, gj - 1, nk - 1))
    outs = pl.pallas_call(
        hosted, name=name, grid=(gi, gj, nk), in_specs=[a_spec, b_spec] + [_ANY] * len(comm.arrays),
        out_specs=[out_spec] + [_ANY] * len(comm.out_shape), out_shape=[out_shape] + comm.out_shape,
        scratch_shapes=scratch + comm.scratch, compiler_params=_cparams(("arbitrary", "arbitrary", "arbitrary")),
    )(a, b, *comm.arrays)
    return outs[0], outs[1:]


def _place():
    return lax.axis_index("x"), lax.axis_index("y"), lax.axis_index("c")


def _flip(v, bit):
    return 1 - v if bit else v


def _allgather_small(v, name):
    R, C = v.shape

    def body(v_ref, out_ref, send_sems, recv_sems, local_sem):
        x, y, c = _place()
        me = 4 * x + 2 * y + c
        mine = pltpu.make_async_copy(v_ref, out_ref.at[me], local_sem)
        mine.start()

        def peer(r):
            return _flip(x, (r >> 2) & 1), _flip(y, (r >> 1) & 1), _flip(c, r & 1)

        sends = [pltpu.make_async_remote_copy(
            src_ref=v_ref, dst_ref=out_ref.at[me], send_sem=send_sems.at[r - 1], recv_sem=recv_sems.at[r - 1],
            device_id=peer(r), device_id_type=MESH) for r in range(1, N_DEV)]
        for cp in sends:
            cp.start()
        for r in range(1, N_DEV):
            px, py, pc = peer(r)
            pltpu.make_async_remote_copy(
                src_ref=v_ref, dst_ref=out_ref.at[4 * px + 2 * py + pc], send_sem=send_sems.at[r - 1],
                recv_sem=recv_sems.at[r - 1], device_id=(x, y, c), device_id_type=MESH).wait_recv()
        for cp in sends:
            cp.wait_send()
        mine.wait()

    return pl.pallas_call(
        body, name=name, out_shape=jax.ShapeDtypeStruct((N_DEV, R, C), v.dtype),
        in_specs=[pl.BlockSpec(memory_space=pltpu.VMEM)], out_specs=pl.BlockSpec(memory_space=pltpu.VMEM),
        scratch_shapes=[pltpu.SemaphoreType.DMA((N_DEV - 1,)), pltpu.SemaphoreType.DMA((N_DEV - 1,)), pltpu.SemaphoreType.DMA],
        compiler_params=pltpu.CompilerParams(vmem_limit_bytes=VMEM_LIMIT),
    )(v)


_CHIP_RELATIONS = ((1, 0), (0, 1), (1, 1))


def _gather_split(shard, name):
    rows, cols = shard.shape
    half = rows // 2

    def body(in_ref, out_ref, send_sems, recv_sems, local_sem):
        x, y, c = _place()
        chip = 2 * x + y
        mine = pl.ds(pl.multiple_of(c * half, 16), half)
        local = pltpu.make_async_copy(in_ref, out_ref.at[chip], local_sem)
        local.start()
        peers = [(_flip(x, fx), _flip(y, fy)) for fx, fy in _CHIP_RELATIONS]
        sends = [pltpu.make_async_remote_copy(
            src_ref=in_ref.at[mine], dst_ref=out_ref.at[chip, mine], send_sem=send_sems.at[j], recv_sem=recv_sems.at[j],
            device_id=(px, py, c), device_id_type=MESH) for j, (px, py) in enumerate(peers)]
        for cp in sends:
            cp.start()
        for j, (px, py) in enumerate(peers):
            landed = out_ref.at[2 * px + py, mine]
            pltpu.make_async_remote_copy(
                src_ref=landed, dst_ref=landed, send_sem=send_sems.at[j], recv_sem=recv_sems.at[j],
                device_id=(x, y, c), device_id_type=MESH).wait_recv()
            fwd = pltpu.make_async_remote_copy(
                src_ref=landed, dst_ref=landed, send_sem=send_sems.at[3 + j], recv_sem=recv_sems.at[3 + j],
                device_id=(x, y, 1 - c), device_id_type=MESH)
            fwd.start()
            sends.append(fwd)
        for j in range(3):
            landed = out_ref.at[0, mine]
            pltpu.make_async_remote_copy(
                src_ref=landed, dst_ref=landed, send_sem=send_sems.at[3 + j], recv_sem=recv_sems.at[3 + j],
                device_id=(x, y, c), device_id_type=MESH).wait_recv()
        for cp in sends:
            cp.wait_send()
        local.wait()

    any_spec = pl.BlockSpec(memory_space=pl.ANY)
    return pl.pallas_call(
        body, name=name, out_shape=jax.ShapeDtypeStruct((N_CHIPS, rows, cols), shard.dtype),
        in_specs=[any_spec], out_specs=any_spec,
        scratch_shapes=[pltpu.SemaphoreType.DMA((6,)), pltpu.SemaphoreType.DMA((6,)), pltpu.SemaphoreType.DMA],
    )(shard)


class _Comm:
    def __init__(self, arrays, out_shape, scratch, start, middle, finish):
        self.arrays, self.out_shape, self.scratch = arrays, out_shape, scratch
        self.start, self.middle, self.finish = start, middle, finish


def _hosted(body, n_in, n_out, n_scratch, comm, first, middle, last):
    nc, no = len(comm.arrays), len(comm.out_shape)

    def wrapped(*refs):
        a = n_in + nc
        b = a + n_out + no
        ins, c_ins, outs, c_outs = refs[:n_in], refs[n_in:a], refs[a:a + n_out], refs[a + n_out:b]
        scratch, c_sems = refs[b:b + n_scratch], refs[b + n_scratch:]

        @pl.when(first())
        def _():
            comm.start(c_ins, c_outs, c_sems)

        body(*ins, *outs, *scratch)
        if comm.middle is not None:
            @pl.when(middle())
            def _():
                comm.middle(c_ins, c_outs, c_sems)

        @pl.when(last())
        def _():
            comm.finish(c_ins, c_outs, c_sems)

    return wrapped


_ANY = pl.BlockSpec(memory_space=pl.ANY)


def _gather_comm(shards):
    n = len(shards)

    def copies(kind, ins, outs, sems):
        send_sems, recv_sems, local_sems = sems
        x, y, c = _place()
        chip = 2 * x + y
        if kind == "local":
            return [pltpu.make_async_copy(ins[i], outs[i].at[chip], local_sems.at[i]) for i in range(n)]
        made = []
        for i in range(n):
            for j, (fx, fy) in enumerate(_CHIP_RELATIONS):
                px, py = _flip(x, fx), _flip(y, fy)
                slot, to = (chip, (px, py, c)) if kind == "send" else (2 * px + py, (x, y, c))
                made.append(pltpu.make_async_remote_copy(
                    src_ref=ins[i], dst_ref=outs[i].at[slot], send_sem=send_sems.at[i, j], recv_sem=recv_sems.at[i, j],
                    device_id=to, device_id_type=MESH))
        return made

    def start(ins, outs, sems):
        for cp in copies("local", ins, outs, sems) + copies("send", ins, outs, sems):
            cp.start()

    def finish(ins, outs, sems):
        for cp in copies("recv", ins, outs, sems):
            cp.wait_recv()
        for cp in copies("send", ins, outs, sems):
            cp.wait_send()
        for cp in copies("local", ins, outs, sems):
            cp.wait()

    return _Comm(list(shards), [jax.ShapeDtypeStruct((N_CHIPS,) + s.shape, s.dtype) for s in shards],
                 [pltpu.SemaphoreType.DMA((n, 3)), pltpu.SemaphoreType.DMA((n, 3)), pltpu.SemaphoreType.DMA((n,))],
                 start, None, finish)


def _exchange_comm(blocks):
    n = len(blocks)

    def copies(kind, ins, outs, sems):
        send_sems, recv_sems, local_sems = sems
        x, y, c = _place()
        chip = 2 * x + y
        me, sibling = (x, y, c), (x, y, 1 - c)

        def remote(src, dst, i, j, to):
            return pltpu.make_async_remote_copy(src_ref=src, dst_ref=dst, send_sem=send_sems.at[i, j],
                                                recv_sem=recv_sems.at[i, j], device_id=to, device_id_type=MESH)

        made = []
        for i in range(n):
            if kind == "local":
                made.append(pltpu.make_async_copy(ins[i].at[chip], outs[i].at[chip], local_sems.at[i]))
                continue
            for j, (fx, fy) in enumerate(_CHIP_RELATIONS):
                px, py = _flip(x, fx), _flip(y, fy)
                src = 2 * px + py
                if kind == "first":
                    made.append(remote(ins[i].at[src], outs[i].at[chip], i, j, (px, py, c)))
                elif kind == "landed":
                    made.append(remote(ins[i].at[src], outs[i].at[src], i, j, me))
                elif kind == "passed":
                    made.append(remote(outs[i].at[src], outs[i].at[N_CHIPS + src], i, 4 + j, sibling))
            if kind == "first":
                made.append(remote(ins[i].at[chip], outs[i].at[N_CHIPS + chip], i, 3, sibling))
            if kind == "arrivals":
                made += [remote(ins[i].at[0], outs[i].at[0], i, j, me) for j in (3, 4, 5, 6)]
        return made

    def start(ins, outs, sems):
        for cp in copies("local", ins, outs, sems) + copies("first", ins, outs, sems):
            cp.start()

    def middle(ins, outs, sems):
        for got, fwd in zip(copies("landed", ins, outs, sems), copies("passed", ins, outs, sems)):
            got.wait_recv()
            fwd.start()

    def finish(ins, outs, sems):
        for cp in copies("arrivals", ins, outs, sems):
            cp.wait_recv()
        for cp in copies("first", ins, outs, sems) + copies("passed", ins, outs, sems):
            cp.wait_send()
        for cp in copies("local", ins, outs, sems):
            cp.wait()

    return _Comm(list(blocks), [jax.ShapeDtypeStruct((N_DEV,) + b.shape[1:], b.dtype) for b in blocks],
                 [pltpu.SemaphoreType.DMA((n, 7)), pltpu.SemaphoreType.DMA((n, 7)), pltpu.SemaphoreType.DMA((n,))],
                 start, middle, finish)


def _row_spec(tr, w, col=0):
    return pl.BlockSpec((tr, w), lambda i: (i, col))


def _vec_spec(w):
    return pl.BlockSpec((1, w), lambda i: (0, 0))


def _acc_spec(w):
    return pl.BlockSpec((8, w), lambda i: (0, 0))


def _rms(x):
    return lax.rsqrt(jnp.mean(x * x, axis=-1, keepdims=True) + EPS)


def _rms_bwd(dn, n, rstd):
    return rstd * (dn - n * jnp.mean(dn * n, axis=-1, keepdims=True))


def _colsum(x):
    return jnp.sum(x, axis=0, keepdims=True)


def _zero_first(ref):
    @pl.when(pl.program_id(0) == 0)
    def _():
        ref[...] = jnp.zeros_like(ref)


def _x_specs(tr, n_lat_tiles):
    return [pl.BlockSpec((tr, D), lambda i: (jnp.minimum(i, n_lat_tiles - 1), 0)),
            pl.BlockSpec((tr, D), lambda i: (jnp.maximum(i - n_lat_tiles, 0), 0))]


def _x_tile(x_ref, c_ref, n_lat_tiles):
    return jnp.where(pl.program_id(0) >= n_lat_tiles, c_ref[...], x_ref[...])


def _norm_mod(x, ctx, w, mod, n_lat_tiles, tr):
    R = x.shape[0] + ctx.shape[0]

    def body(x_ref, c_ref, w_ref, mod_ref, o_ref):
        xv = _x_tile(x_ref, c_ref, n_lat_tiles)
        n = xv * _rms(xv) * w_ref[...]
        o_ref[...] = (n * (1.0 + mod_ref[0, 1]) + mod_ref[0, 0]).astype(BF16)

    return pl.pallas_call(
        body, name="norm1_mod", grid=(R // tr,),
        in_specs=_x_specs(tr, n_lat_tiles) + [_vec_spec(D), pl.BlockSpec(
            (1, 2, 1, D), lambda i: (jnp.where(i >= n_lat_tiles, 1, 0), 0, 0, 0))],
        out_specs=_row_spec(tr, D), out_shape=jax.ShapeDtypeStruct((R, D), BF16),
        compiler_params=_cparams(("arbitrary",)),
    )(x, ctx, w, mod)


def _resid_norm_mod(x, ctx, mix, g1, w2, sh2, sc2, n_lat_tiles, tr):
    R = x.shape[0] + ctx.shape[0]

    def body(x_ref, c_ref, mix_ref, g1_ref, w_ref, sh_ref, sc_ref, h2_ref, u_ref):
        h2 = _x_tile(x_ref, c_ref, n_lat_tiles) + g1_ref[...] * mix_ref[...]
        h2_ref[...] = h2
        n = h2 * _rms(h2) * w_ref[...]
        u_ref[...] = (n * (1.0 + sc_ref[...]) + sh_ref[...]).astype(BF16)

    return pl.pallas_call(
        body, name="resid_norm2_mod", grid=(R // tr,),
        in_specs=_x_specs(tr, n_lat_tiles) + [_row_spec(tr, D)] + [_vec_spec(D)] * 4,
        out_specs=[_row_spec(tr, D), _row_spec(tr, D)],
        out_shape=[jax.ShapeDtypeStruct((R, D), F32), jax.ShapeDtypeStruct((R, D), BF16)],
        compiler_params=_cparams(("arbitrary",)),
    )(x, ctx, mix, g1, w2, sh2, sc2)


def _loss_head(h2, f, target, g2, fw, n_lat_tiles, tr):
    R = h2.shape[0]

    def body(h2_ref, f_ref, t_ref, g2_ref, fw_ref, dh3_ref, df_ref, acc_ref):
        _zero_first(acc_ref)
        lat = pl.program_id(0) < n_lat_tiles
        fv = f_ref[...]
        h3 = h2_ref[...] + g2_ref[...] * fv
        rstd = _rms(h3)
        n = h3 * rstd
        err = n * fw_ref[...] - t_ref[...]
        dy = err * (1.0 / D)
        dh3 = jnp.where(lat, _rms_bwd(dy * fw_ref[...], n, rstd), 0.0)
        dh3_ref[...] = dh3
        df_ref[...] = (g2_ref[...] * dh3).astype(BF16)
        acc_ref[0:1, :] += jnp.where(lat, _colsum(dy * n), 0.0)
        acc_ref[1:2, :] += _colsum(dh3 * fv)
        acc_ref[2:3, :] += jnp.where(lat, _colsum(err * err) * (0.5 / D), 0.0)

    return pl.pallas_call(
        body, name="loss_head", grid=(R // tr,),
        in_specs=[_row_spec(tr, D), _row_spec(tr, D),
                  pl.BlockSpec((tr, D), lambda i: (jnp.minimum(i, n_lat_tiles - 1), 0)), _vec_spec(D), _vec_spec(D)],
        out_specs=[_row_spec(tr, D), _row_spec(tr, D), _acc_spec(D)],
        out_shape=[jax.ShapeDtypeStruct((R, D), F32), jax.ShapeDtypeStruct((R, D), BF16), jax.ShapeDtypeStruct((8, D), F32)],
        compiler_params=_cparams(("arbitrary",)),
    )(h2, f, target, g2, fw)


def _ffn_in_bwd(du_a, du_b, h2, dh3, mix, sc2, g1, w2, tr):
    R = h2.shape[0]

    def body(dua_ref, dub_ref, h2_ref, dh3_ref, mix_ref, sc_ref, g1_ref, w_ref, dh2_ref, dmix_ref, acc_ref):
        _zero_first(acc_ref)
        du = _f32(dua_ref) + _f32(dub_ref)
        h2 = h2_ref[...]
        rstd = _rms(h2)
        n = h2 * rstd
        dnw = du * (1.0 + sc_ref[...])
        dh2 = dh3_ref[...] + _rms_bwd(dnw * w_ref[...], n, rstd)
        dh2_ref[...] = dh2
        dmix_ref[...] = (g1_ref[...] * dh2).astype(BF16)
        acc_ref[0:1, :] += _colsum(du * n * w_ref[...])
        acc_ref[1:2, :] += _colsum(du)
        acc_ref[2:3, :] += _colsum(dnw * n)
        acc_ref[3:4, :] += _colsum(dh2 * mix_ref[...])

    return pl.pallas_call(
        body, name="ffn_in_bwd", grid=(R // tr,),
        in_specs=[_row_spec(tr, D)] * 5 + [_vec_spec(D)] * 3,
        out_specs=[_row_spec(tr, D), _row_spec(tr, D), _acc_spec(D)],
        out_shape=[jax.ShapeDtypeStruct((R, D), F32), jax.ShapeDtypeStruct((R, D), BF16), jax.ShapeDtypeStruct((8, D), F32)],
        compiler_params=_cparams(("arbitrary",)),
    )(du_a, du_b, h2, dh3, mix, sc2, g1, w2)


def _norm1_bwd(dh1, x, ctx, dh2, w1, mod, n_lat_tiles, tr):
    R = x.shape[0] + ctx.shape[0]

    def body(dh1_ref, x_ref, c_ref, dh2_ref, w_ref, mod_ref, dx_ref, acc_ref):
        i = pl.program_id(0)

        @pl.when((i == 0) | (i == n_lat_tiles))
        def _():
            acc_ref[...] = jnp.zeros_like(acc_ref)

        dh1 = dh1_ref[...]
        x = _x_tile(x_ref, c_ref, n_lat_tiles)
        rstd = _rms(x)
        n = x * rstd
        dnw = dh1 * (1.0 + mod_ref[0, 1])

        @pl.when(i < n_lat_tiles)
        def _():
            dx_ref[...] = dh2_ref[...] + _rms_bwd(dnw * w_ref[...], n, rstd)

        acc_ref[0, 0:1, :] += _colsum(dh1 * n * w_ref[...])
        acc_ref[0, 1:2, :] += _colsum(dh1)
        acc_ref[0, 2:3, :] += _colsum(dnw * n)

    sel = lambda i: jnp.where(i >= n_lat_tiles, 1, 0)
    return pl.pallas_call(
        body, name="norm1_bwd", grid=(R // tr,),
        in_specs=[_row_spec(tr, D)] + _x_specs(tr, n_lat_tiles) + [_row_spec(tr, D), _vec_spec(D),
                                                                   pl.BlockSpec((1, 2, 1, D), lambda i: (sel(i), 0, 0, 0))],
        out_specs=[pl.BlockSpec((tr, D), lambda i: (jnp.minimum(i, n_lat_tiles - 1), 0)),
                   pl.BlockSpec((1, 8, D), lambda i: (sel(i), 0, 0))],
        out_shape=[jax.ShapeDtypeStruct((n_lat_tiles * tr, D), F32), jax.ShapeDtypeStruct((2, 8, D), F32)],
        compiler_params=_cparams(("arbitrary",)),
    )(dh1, x, ctx, dh2, w1, mod)


def _swiglu_act(gp, up, tr):
    R = gp.shape[0]

    def body(g_ref, u_ref, o_ref):
        o_ref[...] = (_silu(_f32(g_ref)) * _f32(u_ref)).astype(BF16)

    return pl.pallas_call(
        body, name="swiglu_act", grid=(R // tr,), in_specs=[_row_spec(tr, D_FF)] * 2, out_specs=_row_spec(tr, D_FF),
        out_shape=jax.ShapeDtypeStruct((R, D_FF), BF16), compiler_params=_cparams(("arbitrary",)),
    )(gp, up)


def _swiglu_act_bwd(da, gp, up, tr):
    R = gp.shape[0]

    def body(da_ref, g_ref, u_ref, dg_ref, du_ref):
        da, g = _f32(da_ref), _f32(g_ref)
        dg_ref[...] = (da * _f32(u_ref) * _dsilu(g)).astype(BF16)
        du_ref[...] = (da * _silu(g)).astype(BF16)

    return pl.pallas_call(
        body, name="swiglu_act_bwd", grid=(R // tr,), in_specs=[_row_spec(tr, D_FF)] * 3, out_specs=[_row_spec(tr, D_FF)] * 2,
        out_shape=[jax.ShapeDtypeStruct((R, D_FF), BF16)] * 2, compiler_params=_cparams(("arbitrary",)),
    )(da, gp, up)


def _merge(ya, yb, parts, tr):
    R = ya.shape[0]

    def body(ya_ref, yb_ref, ga_ref, gb_ref, o_ref):
        o_ref[...] = (_sigmoid(_f32(ga_ref)) * _f32(ya_ref) + _sigmoid(_f32(gb_ref)) * _f32(yb_ref)).astype(BF16)

    return pl.pallas_call(
        body, name="merge", grid=(R // tr,),
        in_specs=[_row_spec(tr, D), _row_spec(tr, D), _row_spec(tr, D, COL_GA // D), _row_spec(tr, D, COL_GB // D)],
        out_specs=_row_spec(tr, D), out_shape=jax.ShapeDtypeStruct((R, D), BF16), compiler_params=_cparams(("arbitrary",)),
    )(ya, yb, parts, parts)


def _dparts_out(tr, w, col, nd=1):
    blk = col // w
    return pl.BlockSpec((tr, w), (lambda i: (i, blk)) if nd == 1 else (lambda i, j: (i, blk + j)))


def _merge_bwd(dm, ya, yb, parts, dparts, tr):
    R = ya.shape[0]

    def body(dm_ref, ya_ref, yb_ref, ga_ref, gb_ref, _, dya_ref, dyb_ref, dg_ref):
        dm = _f32(dm_ref)
        sa, sb = _sigmoid(_f32(ga_ref)), _sigmoid(_f32(gb_ref))
        dya_ref[...] = (dm * sa).astype(BF16)
        dyb_ref[...] = (dm * sb).astype(BF16)
        dg_ref[:, 0:D] = (dm * _f32(ya_ref) * sa * (1.0 - sa)).astype(BF16)
        dg_ref[:, D:2 * D] = (dm * _f32(yb_ref) * sb * (1.0 - sb)).astype(BF16)

    return pl.pallas_call(
        body, name="merge_bwd", grid=(R // tr,),
        in_specs=[_row_spec(tr, D)] * 3 + [_row_spec(tr, D, COL_GA // D), _row_spec(tr, D, COL_GB // D), _ANY],
        out_specs=[_row_spec(tr, D), _row_spec(tr, D), _dparts_out(tr, 2 * D, COL_GA)],
        out_shape=[jax.ShapeDtypeStruct((R, D), BF16)] * 2 + [jax.ShapeDtypeStruct(dparts.shape, BF16)],
        input_output_aliases={5: 2}, compiler_params=_cparams(("arbitrary",)),
    )(dm, ya, yb, parts, parts, dparts)


def _gla_out(o2, parts, gw4, tr):
    R = parts.shape[0]

    def body(of_ref, ob_ref, r_ref, w_ref, out_ref):
        oa = _f32(of_ref) + _f32(ob_ref)
        sr = _silu(_f32(r_ref))
        for h in range(GLA_H):
            s = slice(h * GLA_DV, (h + 1) * GLA_DV)
            o = oa[:, s]
            out_ref[:, s] = (o * _rms(o) * w_ref[:, s] * sr[:, s]).astype(BF16)

    return pl.pallas_call(
        body, name="gla_out", grid=(R // tr,),
        in_specs=[_row_spec(tr, D), _row_spec(tr, D), _row_spec(tr, D, COL_R // D), _vec_spec(D)],
        out_specs=_row_spec(tr, D), out_shape=jax.ShapeDtypeStruct((R, D), BF16), compiler_params=_cparams(("arbitrary",)),
    )(o2[0], o2[1], parts, gw4)


def _gla_out_bwd(dout, o2, parts, gw4, dparts, tr):
    R = parts.shape[0]

    def body(d_ref, of_ref, ob_ref, r_ref, w_ref, _, do_ref, dr_ref, acc_ref):
        _zero_first(acc_ref)
        oa = _f32(of_ref) + _f32(ob_ref)
        r = _f32(r_ref)
        sr = _silu(r)
        dout = _f32(d_ref)
        for h in range(GLA_H):
            s = slice(h * GLA_DV, (h + 1) * GLA_DV)
            o = oa[:, s]
            rstd = _rms(o)
            n = o * rstd
            w = w_ref[:, s]
            dr_ref[:, s] = (dout[:, s] * n * w * _dsilu(r[:, s])).astype(BF16)
            dnw = dout[:, s] * sr[:, s]
            do_ref[:, s] = _rms_bwd(dnw * w, n, rstd).astype(ACT)
            acc_ref[0:1, s] += _colsum(dnw * n)

    return pl.pallas_call(
        body, name="gla_out_bwd", grid=(R // tr,),
        in_specs=[_row_spec(tr, D), _row_spec(tr, D), _row_spec(tr, D), _row_spec(tr, D, COL_R // D), _vec_spec(D), _ANY],
        out_specs=[_row_spec(tr, D), _dparts_out(tr, D, COL_R), _acc_spec(D)],
        out_shape=[jax.ShapeDtypeStruct((R, D), ACT), jax.ShapeDtypeStruct(dparts.shape, BF16), jax.ShapeDtypeStruct((8, D), F32)],
        input_output_aliases={5: 1}, compiler_params=_cparams(("arbitrary",)),
    )(dout, o2[0], o2[1], parts, gw4, dparts)


SSM_GW = SSM_INNER // SSM_G


def _ssd_out(y2, xbc, parts, dskip, nw, tr):
    R = parts.shape[0]

    def body(yf_ref, yb_ref, x_ref, z_ref, ds_ref, w_ref, out_ref):
        ob = (_f32(yf_ref) + _f32(yb_ref) + ds_ref[...] * _f32(x_ref)) * _silu(_f32(z_ref))
        for g in range(SSM_G):
            s = slice(g * SSM_GW, (g + 1) * SSM_GW)
            o = ob[:, s]
            out_ref[:, s] = (o * _rms(o) * w_ref[:, s]).astype(BF16)

    return pl.pallas_call(
        body, name="ssd_out", grid=(R // tr,),
        in_specs=[_row_spec(tr, SSM_INNER)] * 3 + [_row_spec(tr, SSM_INNER, COL_Z // SSM_INNER),
                                                   _vec_spec(SSM_INNER), _vec_spec(SSM_INNER)],
        out_specs=_row_spec(tr, SSM_INNER), out_shape=jax.ShapeDtypeStruct((R, SSM_INNER), BF16),
        compiler_params=_cparams(("arbitrary",)),
    )(y2[0], y2[1], xbc, parts, dskip, nw)


def _ssd_out_bwd(dout, y2, xbc, parts, dskip, nw, dparts, tr):
    R = parts.shape[0]

    def body(d_ref, yf_ref, yb_ref, x_ref, z_ref, ds_ref, w_ref, _, dy_ref, dz_ref, acc_ref):
        _zero_first(acc_ref)
        x, z = _f32(x_ref), _f32(z_ref)
        pre = _f32(yf_ref) + _f32(yb_ref) + ds_ref[...] * x
        sz = _silu(z)
        ob = pre * sz
        dout = _f32(d_ref)
        for g in range(SSM_G):
            s = slice(g * SSM_GW, (g + 1) * SSM_GW)
            o = ob[:, s]
            rstd = _rms(o)
            n = o * rstd
            dob = _rms_bwd(dout[:, s] * w_ref[:, s], n, rstd)
            dz_ref[:, s] = (dob * pre[:, s] * _dsilu(z[:, s])).astype(BF16)
            dy = dob * sz[:, s]
            dy_ref[:, s] = dy.astype(ACT)
            acc_ref[0:1, s] += _colsum(dout[:, s] * n)
            acc_ref[1:2, s] += _colsum(dy * x[:, s])

    return pl.pallas_call(
        body, name="ssd_out_bwd", grid=(R // tr,),
        in_specs=[_row_spec(tr, SSM_INNER)] * 4 + [_row_spec(tr, SSM_INNER, COL_Z // SSM_INNER),
                                                   _vec_spec(SSM_INNER), _vec_spec(SSM_INNER), _ANY],
        out_specs=[_row_spec(tr, SSM_INNER), _dparts_out(tr, SSM_INNER, COL_Z), _acc_spec(SSM_INNER)],
        out_shape=[jax.ShapeDtypeStruct((R, SSM_INNER), ACT), jax.ShapeDtypeStruct(dparts.shape, BF16),
                   jax.ShapeDtypeStruct((8, SSM_INNER), F32)],
        input_output_aliases={7: 1}, compiler_params=_cparams(("arbitrary",)),
    )(dout, y2[0], y2[1], xbc, parts, dskip, nw, dparts)


CONV_W = SSM_INNER + 2 * SSM_G * SSM_N
CONV_BLK = 1024


CONV_SHIFTS = (-2, -1, 1, 2)


def _conv_mask_table(tr):
    t = np.arange(tr)
    table = np.zeros((2, len(CONV_SHIFTS), tr, 128), np.float32)
    for kind, (pos, seg) in enumerate(((t % GRID_W, GRID_W), (t, tr))):
        for k, s in enumerate(CONV_SHIFTS):
            table[kind, k] = ((pos + s >= 0) & (pos + s < seg)).astype(np.float32)[:, None]
    return jnp.asarray(table)


def _shifted(u, s, mask_ref, tr):
    return u if s == 0 else pltpu.roll(u, (-s) % tr, 0) * mask_ref[0, CONV_SHIFTS.index(s)]


def _conv_mask_spec(tr, n_lat_tiles, row_axis):
    return pl.BlockSpec((1, len(CONV_SHIFTS), tr, 128),
                        lambda *ids: (jnp.where(ids[row_axis] >= n_lat_tiles, 1, 0), 0, 0, 0))


def _conv_fwd(parts, cw, cb, n_lat_tiles, tr):
    R = parts.shape[0]

    def body(u_ref, w_ref, b_ref, mask_ref, o_ref):
        def lanes(l, carry):
            sl = pl.ds(pl.multiple_of(l * 128, 128), 128)
            u, w = u_ref[:, sl].astype(F32), w_ref[:, sl]
            acc = jnp.zeros_like(u) + b_ref[:, sl]
            for j in range(SSM_CONV):
                acc = acc + _shifted(u, j - CONV_LEFT, mask_ref, tr) * w[j:j + 1, :]
            o_ref[:, sl] = _silu(acc).astype(ACT)
            return carry

        lax.fori_loop(0, CONV_BLK // 128, lanes, 0)

    return pl.pallas_call(
        body, name="conv_fwd", grid=(R // tr, CONV_W // CONV_BLK),
        in_specs=[pl.BlockSpec((tr, CONV_BLK), lambda i, j: (i, COL_XBC // CONV_BLK + j)),
                  pl.BlockSpec((SSM_CONV, CONV_BLK), lambda i, j: (0, j)), pl.BlockSpec((1, CONV_BLK), lambda i, j: (0, j)),
                  _conv_mask_spec(tr, n_lat_tiles, 0)],
        out_specs=pl.BlockSpec((tr, CONV_BLK), lambda i, j: (i, j)), out_shape=jax.ShapeDtypeStruct((R, CONV_W), ACT),
        compiler_params=_cparams(("arbitrary", "arbitrary")),
    )(parts, cw, cb, _conv_mask_table(tr))


def _conv_bwd(dx, db, dc, dy, dskip, parts, cw, cb, dparts, n_lat_tiles, tr):
    R = parts.shape[0]
    half = CONV_BLK // 2
    n_x = SSM_INNER // CONV_BLK

    def body(dxf_ref, dxb_ref, dy_ref, ds_ref, dbf_ref, dbb_ref, dcf_ref, dcb_ref, u_ref, w_ref, b_ref, _, mask_ref,
             du_ref, acc_ref, d_scr):
        @pl.when(pl.program_id(1) == 0)
        def _():
            acc_ref[...] = jnp.zeros_like(acc_ref)

        @pl.when(pl.program_id(0) < n_x)
        def _():
            d_scr[...] = dxf_ref[...] + dxb_ref[...] + _f32(dy_ref) * ds_ref[...]

        @pl.when(pl.program_id(0) >= n_x)
        def _():
            d_scr[:, 0:half] = dbf_ref[...] + dbb_ref[...]
            d_scr[:, half:] = dcf_ref[...] + dcb_ref[...]

        def lanes(l, carry):
            sl = pl.ds(pl.multiple_of(l * 128, 128), 128)
            u, w = u_ref[:, sl].astype(F32), w_ref[:, sl]
            pre = jnp.zeros_like(u) + b_ref[:, sl]
            taps = []
            for j in range(SSM_CONV):
                tap = _shifted(u, j - CONV_LEFT, mask_ref, tr)
                taps.append(tap)
                pre = pre + tap * w[j:j + 1, :]
            dpre = d_scr[:, sl] * _dsilu(pre)
            du = jnp.zeros_like(u)
            sums = []
            for j in range(SSM_CONV):
                sums.append(_colsum(dpre * taps[j]))
                du = du + _shifted(dpre, CONV_LEFT - j, mask_ref, tr) * w[j:j + 1, :]
            sums += [_colsum(dpre), jnp.zeros((8 - SSM_CONV - 1, 128), F32)]
            acc_ref[:, sl] += jnp.concatenate(sums, axis=0)
            du_ref[:, sl] = du.astype(BF16)
            return carry

        lax.fori_loop(0, CONV_BLK // 128, lanes, 0)

    return pl.pallas_call(
        body, name="conv_bwd", grid=(CONV_W // CONV_BLK, R // tr),
        in_specs=[pl.BlockSpec((tr, CONV_BLK), lambda j, i: (jnp.where(j < n_x, i, 0), jnp.minimum(j, n_x - 1)))] * 3
        + [pl.BlockSpec((1, CONV_BLK), lambda j, i: (0, jnp.minimum(j, n_x - 1)))]
        + [pl.BlockSpec((tr, half), lambda j, i: (jnp.where(j < n_x, 0, i), 0))] * 4
        + [pl.BlockSpec((tr, CONV_BLK), lambda j, i: (i, COL_XBC // CONV_BLK + j)),
           pl.BlockSpec((SSM_CONV, CONV_BLK), lambda j, i: (0, j)), pl.BlockSpec((1, CONV_BLK), lambda j, i: (0, j)), _ANY,
           _conv_mask_spec(tr, n_lat_tiles, 1)],
        out_specs=[pl.BlockSpec((tr, CONV_BLK), lambda j, i: (i, COL_XBC // CONV_BLK + j)),
                   pl.BlockSpec((8, CONV_BLK), lambda j, i: (0, j))],
        out_shape=[jax.ShapeDtypeStruct(dparts.shape, BF16), jax.ShapeDtypeStruct((8, CONV_W), F32)],
        scratch_shapes=[pltpu.VMEM((tr, CONV_BLK), F32)],
        input_output_aliases={11: 0}, compiler_params=_cparams(("arbitrary", "arbitrary")),
    )(*dx, dy, dskip, *db, *dc, parts, cw, cb, dparts, _conv_mask_table(tr))


def _chunk_row_block(d, i, n_lat, n_ctx):
    fwd = jnp.where(i < n_ctx, n_lat + i, i - n_ctx)
    rev = n_lat + n_ctx - 1 - i
    if isinstance(d, int):
        return rev if d else fwd
    return jnp.where(d == 0, fwd, rev)


def _tri(n, d, transpose=False):
    row = lax.broadcasted_iota(jnp.int32, (n, n), 0)
    col = lax.broadcasted_iota(jnp.int32, (n, n), 1)
    diff = (col - row) if transpose else (row - col)
    return diff * (1 - 2 * d) >= 0


def _gla_gates(sm, uhi, ulo, bias, d):
    pre = _nn3(sm, uhi, ulo) + bias
    g = _log_sigmoid(pre) * (1.0 / GLA_TAU)
    mask = _tri(GLA_C, d)
    b = _x_nn(mask.astype(BF16), g)
    b_tot = _colsum(g)
    b_ref = b[GLA_C // 2:GLA_C // 2 + 1, :]
    e_q = jnp.exp(jnp.minimum(b - b_ref, EXP_CLAMP))
    e_k = jnp.exp(jnp.minimum(b_ref - b, EXP_CLAMP))
    return pre, mask, b_tot, e_q, e_k, jnp.exp(b), jnp.exp(b_tot - b)


GLA_QK = GLA_H * GLA_DK
GLA_V = GLA_H * GLA_DV


def _gla_specs(n_lat, n_ctx, step_of):
    rbs = [lambda i, d=d: _chunk_row_block(d, step_of(i), n_lat, n_ctx) for d in range(2)]
    specs = []
    for rb in rbs:
        specs += [pl.BlockSpec((GLA_C, GLA_QK), lambda i, rb=rb: (rb(i), COL_Q // GLA_QK)),
                  pl.BlockSpec((GLA_C, GLA_QK), lambda i, rb=rb: (rb(i), COL_K // GLA_QK)),
                  pl.BlockSpec((GLA_C, GLA_V), lambda i, rb=rb: (rb(i), COL_V // GLA_V)),
                  pl.BlockSpec((GLA_C, 128), lambda i, rb=rb: (rb(i), 0))]
    specs += [pl.BlockSpec((2, 128, GLA_QK), lambda i: (0, 0, 0)), pl.BlockSpec((2, 128, GLA_QK), lambda i: (0, 0, 0)),
              pl.BlockSpec((2, 1, GLA_QK), lambda i: (0, 0, 0))]
    return specs, rbs


ACT = BF16


def _f32(ref_or_value):
    return ref_or_value[...].astype(F32)


def _gla_fwd(parts, sm, uhi, ulo, bias, n_lat, n_ctx):
    R = parts.shape[0]
    n_steps = n_lat + n_ctx
    scale = GLA_DK ** -0.5

    def body(*refs):
        ins, (uhi_ref, ulo_ref, bias_ref), o_refs, hist_ref, st = refs[:8], refs[8:11], refs[11:13], refs[13], refs[14]

        @pl.when(pl.program_id(0) == 0)
        def _():
            st[...] = jnp.zeros_like(st)

        for d in range(2):
            q_ref, k_ref, v_ref, sm_ref = ins[4 * d:4 * d + 4]
            _, mask, b_tot, e_q, e_k, e_in, e_out = _gla_gates(sm_ref[...], uhi_ref[d], ulo_ref[d], bias_ref[d], d)
            q, k, v = _f32(q_ref) * scale, _f32(k_ref), _bf(v_ref[...])
            qb, kb, q_in, k_out, decay = _bf(q * e_q), _bf(k * e_k), _bf(q * e_in), _bf(k * e_out), jnp.exp(b_tot)
            for h in range(GLA_H):
                sk, sv = slice(h * GLA_DK, (h + 1) * GLA_DK), slice(h * GLA_DV, (h + 1) * GLA_DV)
                att = jnp.where(mask, _nt(qb[:, sk], kb[:, sk]), 0.0)
                s_in = st[d, h]
                hist_ref[d, 0, h] = s_in
                o_refs[d][:, sv] = (_nn(_bf(att), v[:, sv]) + _nt(q_in[:, sk], _bf(s_in))).astype(ACT)
                st[d, h] = decay[:, sk] * s_in + _tn(v[:, sv], k_out[:, sk])

    in_specs, rbs = _gla_specs(n_lat, n_ctx, lambda i: i)
    return pl.pallas_call(
        body, name="gla_fwd", grid=(n_steps,), in_specs=in_specs,
        out_specs=[pl.BlockSpec((GLA_C, GLA_V), lambda i, rb=rb: (rb(i), 0)) for rb in rbs]
        + [pl.BlockSpec((2, 1, GLA_H, GLA_DV, GLA_DK), lambda i: (0, i, 0, 0, 0))],
        out_shape=[jax.ShapeDtypeStruct((R, GLA_V), ACT)] * 2 + [jax.ShapeDtypeStruct((2, n_steps, GLA_H, GLA_DV, GLA_DK), F32)],
        scratch_shapes=[pltpu.VMEM((2, GLA_H, GLA_DV, GLA_DK), F32)],
        compiler_params=_cparams(("arbitrary",)),
    )(*([parts, parts, parts, sm] * 2), uhi, ulo, bias)


def _gla_bwd(do, parts, sm, uhi, ulo, bias, hist, n_lat, n_ctx):
    R = parts.shape[0]
    n_steps = n_lat + n_ctx
    scale = GLA_DK ** -0.5
    step_of = lambda j: n_steps - 1 - j

    def body(*refs):
        ins, (uhi_ref, ulo_ref, bias_ref), do_refs, hist_ref = refs[:8], refs[8:11], refs[11:13], refs[13]
        outs, dst = refs[14:22], refs[22]

        @pl.when(pl.program_id(0) == 0)
        def _():
            dst[...] = jnp.zeros_like(dst)

        for d in range(2):
            q_ref, k_ref, v_ref, sm_ref = ins[4 * d:4 * d + 4]
            dq_ref, dk_ref, dv_ref, dp_ref = outs[4 * d:4 * d + 4]
            pre, mask, b_tot, e_q, e_k, e_in, e_out = _gla_gates(sm_ref[...], uhi_ref[d], ulo_ref[d], bias_ref[d], d)
            q, k, v = _f32(q_ref) * scale, _f32(k_ref), _bf(v_ref[...])
            dout = _bf(do_refs[d][...])
            k_out_f = k * e_out
            qb, kb, q_in, k_out, decay = _bf(q * e_q), _bf(k * e_k), _bf(q * e_in), _bf(k_out_f), jnp.exp(b_tot)
            dqs, dks, dk_outs, dss = [], [], [], []
            for h in range(GLA_H):
                sk, sv = slice(h * GLA_DK, (h + 1) * GLA_DK), slice(h * GLA_DV, (h + 1) * GLA_DV)
                s_in, ds = hist_ref[d, 0, h], dst[d, h]
                att = jnp.where(mask, _nt(qb[:, sk], kb[:, sk]), 0.0)
                datt = _bf(jnp.where(mask, _nt(dout[:, sv], v[:, sv]), 0.0))
                dv_ref[:, sv] = _tn(_bf(att), dout[:, sv]) + _nt(k_out[:, sk], _bf(ds))
                dqs.append(_nn(datt, kb[:, sk]) * e_q[:, sk] + _nn(dout[:, sv], _bf(s_in)) * e_in[:, sk])
                dk_o = _nn(v[:, sv], _bf(ds))
                dk_outs.append(dk_o)
                dks.append(_tn(datt, qb[:, sk]) * e_k[:, sk])
                dss.append(_colsum(ds * s_in))
                dst[d, h] = decay[:, sk] * ds + _tn(dout[:, sv], q_in[:, sk])
            dq, dk_out = jnp.concatenate(dqs, axis=1), jnp.concatenate(dk_outs, axis=1)
            dk = jnp.concatenate(dks, axis=1) + dk_out * e_out
            dq_ref[...] = dq * scale
            dk_ref[...] = dk
            db_tot = _colsum(dk_out * k_out_f) + decay * jnp.concatenate(dss, axis=1)
            dg = _x_nn(_tri(GLA_C, d, transpose=True).astype(BF16), dq * q - dk * k) + db_tot
            dp_ref[...] = dg * (1.0 / GLA_TAU) * _sigmoid(-pre)

    in_specs, rbs = _gla_specs(n_lat, n_ctx, step_of)
    in_specs += [pl.BlockSpec((GLA_C, GLA_V), lambda j, rb=rb: (rb(j), 0)) for rb in rbs]
    in_specs += [pl.BlockSpec((2, 1, GLA_H, GLA_DV, GLA_DK), lambda j: (0, step_of(j), 0, 0, 0))]
    out_specs, out_shape = [], []
    for rb in rbs:
        for w in (GLA_QK, GLA_QK, GLA_V, GLA_QK):
            out_specs.append(pl.BlockSpec((GLA_C, w), lambda j, rb=rb: (rb(j), 0)))
            out_shape.append(jax.ShapeDtypeStruct((R, w), F32))
    outs = pl.pallas_call(
        body, name="gla_bwd", grid=(n_steps,), in_specs=in_specs, out_specs=out_specs, out_shape=out_shape,
        scratch_shapes=[pltpu.VMEM((2, GLA_H, GLA_DV, GLA_DK), F32)],
        compiler_params=_cparams(("arbitrary",)),
    )(*([parts, parts, parts, sm] * 2), uhi, ulo, bias, do, do, hist)
    return [(outs[k], outs[4 + k]) for k in range(4)]


def _ssd_consts(dt_bias, a_log):
    sel = np.zeros((2, SSM_G, 128, 128), np.float32)
    for d, base in enumerate((SM_DTF, SM_DTB)):
        for g in range(SSM_G):
            for e in range(SSM_HPG):
                sel[d, g, base + SSM_HPG * g + e, e] = 1.0
    e512 = np.zeros((128, SSM_GW), np.float32)
    for e in range(SSM_HPG):
        e512[e, SSM_P * e:SSM_P * (e + 1)] = 1.0
    a_neg = -jnp.exp(a_log)
    pad = lambda v: jnp.pad(v.reshape(2, SSM_G, 1, SSM_HPG), ((0, 0), (0, 0), (0, 0), (0, 128 - SSM_HPG)))
    return dict(
        sel=jnp.asarray(sel, BF16), sel_t=jnp.asarray(sel.transpose(0, 1, 3, 2), BF16), e512_t=jnp.asarray(e512.T, BF16),
        dtb=pad(dt_bias), a=pad(a_neg), a512=jnp.repeat(a_neg, SSM_P, axis=1).reshape(2, SSM_G, 1, SSM_GW))


def _head_columns(x8):
    return [jnp.broadcast_to(x8[:, e:e + 1], (x8.shape[0], 128)) for e in range(SSM_HPG)]


def _head_layout(cols):
    low = lax.broadcasted_iota(jnp.int32, (1, 128), 1) < SSM_P
    return jnp.concatenate([jnp.where(low, cols[2 * j], cols[2 * j + 1]) for j in range(SSM_HPG // 2)], axis=1)


def _ssd_common(sm, sel, dtb, a_neg, a512, d):
    dtr8 = _nn_x(sm, sel) + dtb
    dt8 = _softplus(dtr8)
    a8 = a_neg * dt8
    mask = _tri(SSM_C, d)
    mask_t = _tri(SSM_C, d, transpose=True).astype(BF16)
    cum8 = _x_nn(mask.astype(BF16), a8)
    a_hi, a_lo = _split(a8)
    cum_t = _tn(a_hi, mask_t) + _tn(a_lo, mask_t)
    cum_cols = _head_columns(cum8)
    dt_exp = _head_layout(_head_columns(dt8))
    a_exp = a512 * dt_exp
    return dict(dtr8=dtr8, a8=a8, mask=mask, mask_t=mask_t, cum_t=cum_t, dt_exp=dt_exp, a_exp=a_exp,
                cum_exp=_head_layout(cum_cols), cum_cols=cum_cols, tot_exp=_colsum(a_exp))


def _ssd_decay(cm, e):
    diff = cm["cum_cols"][e] - cm["cum_t"][e:e + 1, :]
    return jnp.where(cm["mask"], jnp.exp(jnp.minimum(diff, 0.0)), 0.0)


SSM_GPS = 4


def _ssd_specs(n_lat, n_ctx, step_of):
    rbs = [lambda i, d=d: _chunk_row_block(d, step_of(i), n_lat, n_ctx) for d in range(2)]
    xw, nw = SSM_GPS * SSM_GW, SSM_GPS * SSM_N
    specs = []
    for rb in rbs:
        specs += [pl.BlockSpec((SSM_C, xw), lambda g, i, rb=rb: (rb(i), g)),
                  pl.BlockSpec((SSM_C, nw), lambda g, i, rb=rb: (rb(i), SSM_INNER // nw + g)),
                  pl.BlockSpec((SSM_C, nw), lambda g, i, rb=rb: (rb(i), (SSM_INNER + SSM_G * SSM_N) // nw + g)),
                  pl.BlockSpec((SSM_C, 128), lambda g, i, rb=rb: (rb(i), 0))]
    specs += [pl.BlockSpec((2, SSM_GPS, 128, 128), lambda g, i: (0, g, 0, 0)),
              pl.BlockSpec((2, SSM_GPS, 1, 128), lambda g, i: (0, g, 0, 0)),
              pl.BlockSpec((2, SSM_GPS, 1, 128), lambda g, i: (0, g, 0, 0)),
              pl.BlockSpec((2, SSM_GPS, 1, SSM_GW), lambda g, i: (0, g, 0, 0))]
    return specs, rbs


def _ssd_fwd(xbc, sm, k, n_lat, n_ctx, comm):
    R = xbc.shape[0]
    n_steps = n_lat + n_ctx

    def body(*refs):
        ins, (sel_ref, dtb_ref, a_ref, a512_ref), y_refs, hist_ref, st = refs[:8], refs[8:12], refs[12:14], refs[14], refs[15]

        @pl.when(pl.program_id(1) == 0)
        def _():
            st[...] = jnp.zeros_like(st)

        for d in range(2):
            x_ref, b_ref, c_ref, sm_ref = ins[4 * d:4 * d + 4]
            sm = sm_ref[...]
            for gg in range(SSM_GPS):
                sx, sn = slice(gg * SSM_GW, (gg + 1) * SSM_GW), slice(gg * SSM_N, (gg + 1) * SSM_N)
                cm = _ssd_common(sm, sel_ref[d, gg], dtb_ref[d, gg], a_ref[d, gg], a512_ref[d, gg], d)
                bm, cmat = _bf(b_ref[:, sn]), _bf(c_ref[:, sn])
                xdt = x_ref[:, sx].astype(F32) * cm["dt_exp"]
                cb = _nt(cmat, bm)
                ys = [_nn(_bf(cb * _ssd_decay(cm, e)), _bf(xdt[:, SSM_P * e:SSM_P * (e + 1)])) for e in range(SSM_HPG)]
                s_in = st[d, gg]
                hist_ref[d, 0, gg] = s_in
                y = jnp.concatenate(ys, axis=1) + jnp.exp(cm["cum_exp"]) * _nn(cmat, _bf(s_in))
                y_refs[d][:, sx] = y.astype(ACT)
                st[d, gg] = jnp.exp(cm["tot_exp"]) * s_in + _tn(bm, _bf(xdt * jnp.exp(cm["tot_exp"] - cm["cum_exp"])))

    in_specs, rbs = _ssd_specs(n_lat, n_ctx, lambda i: i)
    out_specs = [pl.BlockSpec((SSM_C, SSM_GPS * SSM_GW), lambda g, i, rb=rb: (rb(i), g)) for rb in rbs]
    out_specs += [pl.BlockSpec((2, 1, SSM_GPS, SSM_N, SSM_GW), lambda g, i: (0, i, g, 0, 0))]
    out_shape = [jax.ShapeDtypeStruct((R, SSM_INNER), ACT)] * 2 + [jax.ShapeDtypeStruct((2, n_steps, SSM_G, SSM_N, SSM_GW), F32)]
    args = [xbc, xbc, xbc, sm] * 2 + [k["sel"], k["dtb"], k["a"], k["a512"]]
    n_host_out = len(out_shape)
    outs = pl.pallas_call(
        _hosted(body, len(args), n_host_out, 1, comm, *_ssd_comm_steps(n_steps)), name="ssd_fwd",
        grid=(SSM_G // SSM_GPS, n_steps), in_specs=in_specs + [_ANY] * len(comm.arrays),
        out_specs=out_specs + [_ANY] * len(comm.out_shape), out_shape=out_shape + comm.out_shape,
        scratch_shapes=[pltpu.VMEM((2, SSM_GPS, SSM_N, SSM_GW), F32)] + comm.scratch,
        compiler_params=_cparams(("arbitrary", "arbitrary")),
    )(*args, *comm.arrays)
    return outs[:n_host_out], outs[n_host_out:]


def _ssd_comm_steps(n_steps):
    n_g = SSM_G // SSM_GPS
    at = lambda g, i: (pl.program_id(0) == g) & (pl.program_id(1) == i)
    half = (n_g // 2, n_steps // 2 if n_g % 2 else 0)
    return (lambda: at(0, 0)), (lambda: at(*half)), (lambda: at(n_g - 1, n_steps - 1))


def _ssd_bwd(dy, xbc, sm, k, hist, n_lat, n_ctx, comm):
    R = xbc.shape[0]
    n_steps = n_lat + n_ctx
    step_of = lambda j: n_steps - 1 - j

    def one(d, gg, x_ref, b_ref, c_ref, sm_ref, sel_ref, dtb_ref, a_ref, a512_ref, selt_ref, e512t_ref, dy_ref,
            hist_ref, dx_ref, db_ref, dc_ref, dsm_ref, acc_ref, dst):
        sx, sn = slice(gg * SSM_GW, (gg + 1) * SSM_GW), slice(gg * SSM_N, (gg + 1) * SSM_N)
        a_neg, e512_t = a_ref[d, gg], e512t_ref[...]
        cm = _ssd_common(sm_ref[...], sel_ref[d, gg], dtb_ref[d, gg], a_neg, a512_ref[d, gg], d)
        x, dyv = x_ref[:, sx].astype(F32), dy_ref[:, sx].astype(F32)
        bm, cmat = _bf(b_ref[:, sn]), _bf(c_ref[:, sn])
        xdt = x * cm["dt_exp"]
        cb = _nt(cmat, bm)
        s_in, ds = hist_ref[d, 0, gg], dst[d, gg]
        w = jnp.exp(cm["tot_exp"] - cm["cum_exp"])
        z = _nn(bm, _bf(ds))
        decay_in = jnp.exp(cm["cum_exp"])
        gy = _bf(dyv * decay_in)
        dcb = jnp.zeros((SSM_C, SSM_C), F32)
        dxs, crossing = [], []
        row = lax.broadcasted_iota(jnp.int32, (SSM_C, SSM_C), 0)
        col = lax.broadcasted_iota(jnp.int32, (SSM_C, SSM_C), 1)
        eye = (row == col).astype(BF16)
        before = (cm["mask_t"] - eye)
        for e in range(SSM_HPG):
            s = slice(SSM_P * e, SSM_P * (e + 1))
            lm = _ssd_decay(cm, e)
            dy_e = _bf(dyv[:, s])
            m_e = cb * lm
            dm_e = _nt(dy_e, _bf(xdt[:, s]))
            dcb = dcb + dm_e * lm
            dxs.append(_tn(_bf(m_e), dy_e))
            crossing.append(_bf(dm_e * m_e))
        through = _nn(jnp.concatenate(crossing, axis=0), before)
        crossing = [_colsum(jnp.where(cm["mask"], through[e * SSM_C:(e + 1) * SSM_C], 0.0)) for e in range(SSM_HPG)]
        da_rows = jnp.concatenate(crossing + [jnp.zeros((128 - SSM_HPG, SSM_C), F32)], axis=0)
        r_hi, r_lo = _split(da_rows)
        da8_intra = _tn(r_hi, eye) + _tn(r_lo, eye)
        dx_state = w * z
        dxdt = jnp.concatenate(dxs, axis=1) + dx_state
        dcb = _bf(dcb)
        c_s = _nn(cmat, _bf(s_in))
        dc_ref[:, sn] = _nn(dcb, bm) + _nt(gy, _bf(s_in))
        db_ref[:, sn] = _tn(dcb, cmat) + _nt(_bf(w * xdt), _bf(ds))
        dst[d, gg] = jnp.exp(cm["tot_exp"]) * ds + _tn(cmat, gy)
        state_path = xdt * dx_state
        per_token = _nn_x(jnp.concatenate([dyv * decay_in * c_s - state_path, dxdt * x], axis=0), e512_t)
        totals = jnp.concatenate([_colsum(state_path), _colsum(ds * s_in), jnp.zeros((6, SSM_GW), F32)], axis=0)
        totals = _nn_x(totals, e512_t)
        tot8 = _colsum(cm["a8"])
        dtot8 = totals[0:1] + jnp.exp(tot8) * totals[1:2]
        da8 = da8_intra + _x_nn(cm["mask_t"], per_token[:SSM_C]) + dtot8
        ddt8 = da8 * a_neg + per_token[SSM_C:]
        dsm_ref[gg] = _nn_x(ddt8 * _sigmoid(cm["dtr8"]), selt_ref[d, gg])
        dx_ref[:, sx] = dxdt * cm["dt_exp"]
        acc_ref[d, gg, 0:1, :] += _colsum(da8 * cm["a8"])

    def body(*refs):
        ins, consts, (selt_ref, e512t_ref), dy_refs, hist_ref = refs[:8], refs[8:12], refs[12:14], refs[14:16], refs[16]
        outs, acc_ref, dst = refs[17:25], refs[25], refs[26]

        @pl.when(pl.program_id(1) == 0)
        def _():
            dst[...] = jnp.zeros_like(dst)
            acc_ref[...] = jnp.zeros_like(acc_ref)

        for d in range(2):
            for gg in range(SSM_GPS):
                one(d, gg, *ins[4 * d:4 * d + 4], *consts, selt_ref, e512t_ref, dy_refs[d], hist_ref,
                    *outs[4 * d:4 * d + 4], acc_ref, dst)

    xw, nw = SSM_GPS * SSM_GW, SSM_GPS * SSM_N
    in_specs, rbs = _ssd_specs(n_lat, n_ctx, step_of)
    in_specs += [pl.BlockSpec((2, SSM_GPS, 128, 128), lambda g, j: (0, g, 0, 0)), pl.BlockSpec((SSM_GW, 128), lambda g, j: (0, 0))]
    in_specs += [pl.BlockSpec((SSM_C, xw), lambda g, j, rb=rb: (rb(j), g)) for rb in rbs]
    in_specs += [pl.BlockSpec((2, 1, SSM_GPS, SSM_N, SSM_GW), lambda g, j: (0, step_of(j), g, 0, 0))]
    out_specs, out_shape = [], []
    for rb in rbs:
        out_specs += [pl.BlockSpec((SSM_C, xw), lambda g, j, rb=rb: (rb(j), g)),
                      pl.BlockSpec((SSM_C, nw), lambda g, j, rb=rb: (rb(j), g)),
                      pl.BlockSpec((SSM_C, nw), lambda g, j, rb=rb: (rb(j), g)),
                      pl.BlockSpec((SSM_GPS, SSM_C, 128), lambda g, j, rb=rb: (g, rb(j), 0))]
        out_shape += [jax.ShapeDtypeStruct((R, SSM_INNER), F32), jax.ShapeDtypeStruct((R, SSM_G * SSM_N), F32),
                      jax.ShapeDtypeStruct((R, SSM_G * SSM_N), F32), jax.ShapeDtypeStruct((SSM_G, R, 128), F32)]
    out_specs.append(pl.BlockSpec((2, SSM_GPS, 8, 128), lambda g, j: (0, g, 0, 0)))
    out_shape.append(jax.ShapeDtypeStruct((2, SSM_G, 8, 128), F32))
    args = [xbc, xbc, xbc, sm] * 2 + [k["sel"], k["dtb"], k["a"], k["a512"], k["sel_t"], k["e512_t"], dy, dy, hist]
    n_host_out = len(out_shape)
    outs = pl.pallas_call(
        _hosted(body, len(args), n_host_out, 1, comm, *_ssd_comm_steps(n_steps)), name="ssd_bwd",
        grid=(SSM_G // SSM_GPS, n_steps), in_specs=in_specs + [_ANY] * len(comm.arrays),
        out_specs=out_specs + [_ANY] * len(comm.out_shape), out_shape=out_shape + comm.out_shape,
        scratch_shapes=[pltpu.VMEM((2, SSM_GPS, SSM_N, SSM_GW), F32)] + comm.scratch,
        compiler_params=_cparams(("arbitrary", "arbitrary")),
    )(*args, *comm.arrays)
    return [(outs[n], outs[4 + n]) for n in range(4)] + [outs[8]], outs[n_host_out:]


def _gla_assemble(dq, dk, dv, dparts, tr):
    R = dq[0].shape[0]
    qk = GLA_H * GLA_DK

    def body(dqf_ref, dqb_ref, dkf_ref, dkb_ref, dvf_ref, dvb_ref, _, o_ref):
        o_ref[:, 0:qk] = (dqf_ref[...] + dqb_ref[...]).astype(BF16)
        o_ref[:, qk:2 * qk] = (dkf_ref[...] + dkb_ref[...]).astype(BF16)
        o_ref[:, 2 * qk:] = (dvf_ref[...] + dvb_ref[...]).astype(BF16)

    return pl.pallas_call(
        body, name="gla_assemble", grid=(R // tr,), in_specs=[_row_spec(tr, qk)] * 4 + [_row_spec(tr, D)] * 2 + [_ANY],
        out_specs=_dparts_out(tr, 2 * D, COL_Q), out_shape=jax.ShapeDtypeStruct(dparts.shape, BF16),
        input_output_aliases={6: 0}, compiler_params=_cparams(("arbitrary",)),
    )(*dq, *dk, *dv, dparts)


def _small_assemble(dp, dsm, sm, ut_hi, ut_lo, dparts, tr):
    R = sm.shape[0]
    qk = GLA_H * GLA_DK

    def body(dpf_ref, dpb_ref, dsmf_ref, dsmb_ref, sm_ref, uth_ref, utl_ref, _, o_ref, dup_ref, acc_ref, acc2_ref):
        @pl.when(pl.program_id(0) == 0)
        def _():
            dup_ref[...] = jnp.zeros_like(dup_ref)
            acc_ref[...] = jnp.zeros_like(acc_ref)
            acc2_ref[...] = jnp.zeros_like(acc2_ref)

        ssd = dsmf_ref[0] + dsmb_ref[0]
        for g in range(1, SSM_G):
            ssd = ssd + (dsmf_ref[g] + dsmb_ref[g])
        acc2_ref[0:1, :] += _colsum(ssd)
        sm_hi, sm_lo = _split(sm_ref[...])
        out = ssd
        for d, dp_ref in enumerate((dpf_ref, dpb_ref)):
            dpd = dp_ref[...]
            out = out + _nn3(dpd, uth_ref[d], utl_ref[d])
            p_hi, p_lo = _split(dpd)
            dup_ref[d] += _tn(sm_hi, p_hi) + _tn(sm_lo, p_hi) + _tn(sm_hi, p_lo)
            acc_ref[d:d + 1, :] += _colsum(dpd)
        o_ref[...] = out.astype(BF16)

    return pl.pallas_call(
        body, name="small_assemble", grid=(R // tr,),
        in_specs=[_row_spec(tr, qk)] * 2 + [pl.BlockSpec((SSM_G, tr, 128), lambda i: (0, i, 0))] * 2
        + [_row_spec(tr, 128), pl.BlockSpec((2, qk, 128), lambda i: (0, 0, 0)),
           pl.BlockSpec((2, qk, 128), lambda i: (0, 0, 0)), _ANY],
        out_specs=[_dparts_out(tr, 128, COL_SM), pl.BlockSpec((2, 128, qk), lambda i: (0, 0, 0)), _acc_spec(qk), _acc_spec(128)],
        out_shape=[jax.ShapeDtypeStruct(dparts.shape, BF16), jax.ShapeDtypeStruct((2, 128, qk), F32),
                   jax.ShapeDtypeStruct((8, qk), F32), jax.ShapeDtypeStruct((8, 128), F32)],
        input_output_aliases={7: 0}, compiler_params=_cparams(("arbitrary",)),
    )(*dp, *dsm, sm, ut_hi, ut_lo, dparts)


ADA_ROWS = 16
ADA_TILE = 512


def _dot3_f32(a, b, ca, cb):
    a_hi, a_lo = _split(a)
    b_hi, b_lo = _split(b)
    return _dg(a_hi, b_hi, ca, cb) + _dg(a_lo, b_hi, ca, cb) + _dg(a_hi, b_lo, ca, cb)


def _ada_fwd(cvec, w, b):
    cols = w.shape[1]

    def body(c_ref, w_ref, b_ref, o_ref):
        o_ref[...] = _dot3_f32(_silu(c_ref[...]), w_ref[...], 1, 0) + b_ref[...]

    return pl.pallas_call(
        body, name="ada_fwd", grid=(cols // ADA_TILE,),
        in_specs=[pl.BlockSpec((ADA_ROWS, D), lambda j: (0, 0)), pl.BlockSpec((D, ADA_TILE), lambda j: (0, j)),
                  pl.BlockSpec((1, ADA_TILE), lambda j: (0, j))],
        out_specs=pl.BlockSpec((ADA_ROWS, ADA_TILE), lambda j: (0, j)), out_shape=jax.ShapeDtypeStruct((ADA_ROWS, cols), F32),
        compiler_params=_cparams(("arbitrary",)),
    )(cvec, w, b)


def _adam(w, g, m, v):
    m2 = ADAM_B1 * m + (1.0 - ADAM_B1) * g
    v2 = ADAM_B2 * v + (1.0 - ADAM_B2) * (g * g)
    m_hat = m2 / (1.0 - ADAM_B1 ** ADAM_STEP)
    v_hat = v2 / (1.0 - ADAM_B2 ** ADAM_STEP)
    return -ADAM_LR * (m_hat / (jnp.sqrt(v_hat) + ADAM_EPS) + ADAM_WD * w), m2, v2


def _wada_bwd_adam(cvec, dada, w, m, v):
    rows, cols = w.shape
    tr = _tile(rows, 256, 128)

    def body(c_ref, d_ref, w_ref, m_ref, v_ref, g_ref, dl_ref, m2_ref, v2_ref, p_ref):
        wv = w_ref[...]
        g = _dot3_f32(_silu(c_ref[...]), d_ref[...], 0, 0)
        g_ref[...] = g
        dl_ref[...], m2_ref[...], v2_ref[...] = _adam(wv, g, m_ref[...], v_ref[...])
        p_ref[...] = _dot3_f32(d_ref[...], wv, 1, 1)

    blk = pl.BlockSpec((tr, cols), lambda i: (i, 0))
    return pl.pallas_call(
        body, name="wada_bwd_adam", grid=(rows // tr,),
        in_specs=[pl.BlockSpec((ADA_ROWS, tr), lambda i: (0, i)), pl.BlockSpec((ADA_ROWS, cols), lambda i: (0, 0)), blk, blk, blk],
        out_specs=[blk, blk, blk, blk, pl.BlockSpec((ADA_ROWS, tr), lambda i: (0, i))],
        out_shape=[jax.ShapeDtypeStruct((rows, cols), F32)] * 4 + [jax.ShapeDtypeStruct((ADA_ROWS, rows), F32)],
        compiler_params=_cparams(("arbitrary",)),
    )(cvec, dada, w, m, v)


def _reduce_adam(parts8, w, m, v, name):
    rows, cols = w.shape
    tr = _tile(rows, 64, 16)

    def body(p_ref, w_ref, m_ref, v_ref, g_ref, dl_ref, m2_ref, v2_ref):
        g = p_ref[0].astype(F32) + p_ref[N_CHIPS].astype(F32)
        for j in range(1, N_CHIPS):
            g = g + (p_ref[j].astype(F32) + p_ref[N_CHIPS + j].astype(F32))
        g_ref[...] = g
        dl_ref[...], m2_ref[...], v2_ref[...] = _adam(w_ref[...], g, m_ref[...], v_ref[...])

    blk = pl.BlockSpec((tr, cols), lambda i: (i, 0))
    return pl.pallas_call(
        body, name=name, grid=(rows // tr,), in_specs=[pl.BlockSpec((N_DEV, tr, cols), lambda i: (0, i, 0)), blk, blk, blk],
        out_specs=[blk] * 4, out_shape=[jax.ShapeDtypeStruct((rows, cols), F32)] * 4, compiler_params=_cparams(("arbitrary",)),
    )(parts8, w, m, v)


SMALL_W = 1024


def _sum8(g8):
    rows = g8.shape[1]

    def body(g_ref, o_ref):
        s = g_ref[0]
        for j in range(1, N_DEV):
            s = s + g_ref[j]
        o_ref[...] = s

    return pl.pallas_call(
        body, name="sum8", out_shape=jax.ShapeDtypeStruct((rows, SMALL_W), F32),
        in_specs=[pl.BlockSpec(memory_space=pltpu.VMEM)], out_specs=pl.BlockSpec(memory_space=pltpu.VMEM),
        compiler_params=pltpu.CompilerParams(vmem_limit_bytes=VMEM_LIMIT),
    )(g8)


def _cctx_grad(p8, c_ctx):
    def body(p_ref, c_ref, o_ref):
        s = p_ref[0]
        for chip in range(1, N_CHIPS):
            s = s + p_ref[2 * chip]
        o_ref[...] = s * _dsilu(c_ref[...])

    return pl.pallas_call(
        body, name="cctx_grad", out_shape=jax.ShapeDtypeStruct((1, D), F32),
        in_specs=[pl.BlockSpec(memory_space=pltpu.VMEM)] * 2, out_specs=pl.BlockSpec(memory_space=pltpu.VMEM),
    )(p8, c_ctx)


def _adam_small(w, g, m, v):
    def body(w_ref, g_ref, m_ref, v_ref, dl_ref, m2_ref, v2_ref):
        dl_ref[...], m2_ref[...], v2_ref[...] = _adam(w_ref[...], g_ref[...], m_ref[...], v_ref[...])

    vm = pl.BlockSpec(memory_space=pltpu.VMEM)
    return pl.pallas_call(
        body, name="adam_small", out_shape=[jax.ShapeDtypeStruct(w.shape, F32)] * 3, in_specs=[vm] * 4, out_specs=[vm] * 3,
        compiler_params=pltpu.CompilerParams(vmem_limit_bytes=VMEM_LIMIT),
    )(w, g, m, v)


def _pack(vecs, width=SMALL_W, row_mult=8):
    flat = jnp.concatenate([v.reshape(-1).astype(F32) for v in vecs])
    n = flat.shape[0]
    rows = -(-n // (width * row_mult)) * row_mult
    return jnp.pad(flat, (0, rows * width - n)).reshape(rows, width)


def _unpack(packed, shapes):
    flat = packed.reshape(-1)
    out, off = [], 0
    for s in shapes:
        n = int(np.prod(s))
        out.append(flat[off:off + n].reshape(s))
        off += n
    return out


WEIGHTS = ('c_ctx', 'w_ada', 'b_ada', 'norm1_w', 'w_in', 'gla_up_f', 'gla_bias_f', 'gla_up_b', 'gla_bias_b', 'gla_norm_w',
           'conv_w', 'conv_b', 'dt_bias_f', 'dt_bias_b', 'a_log_f', 'a_log_b', 'd_skip', 'ssm_norm_w', 'w_pa', 'w_pb', 'w_out',
           'norm2_w', 'w_gate', 'w_up', 'w_down', 'final_norm_w')
BIG = ('w_in', 'w_pa', 'w_pb', 'w_out', 'w_gate', 'w_up', 'w_down')
COL_SHARDED = ('w_in', 'w_gate', 'w_up')
SMALL_SHARDED = ('gla_up_f', 'gla_up_b', 'conv_w')
ROW_TILE = 256


def _blocks_to_full(g4, name):
    n, r, c = g4.shape
    return g4.transpose(1, 0, 2).reshape(r, n * c) if name in COL_SHARDED else g4.reshape(n * r, c)


def _full_to_blocks(full, name):
    r, c = full.shape
    if name in COL_SHARDED:
        return full.reshape(r, N_CHIPS, c // N_CHIPS).transpose(1, 0, 2)
    return full.reshape(N_CHIPS, r // N_CHIPS, c)


def _permute_in(w_in_full):
    off = np.concatenate([[0], np.cumsum(IN_WIDTHS)])
    cols = [w_in_full[:, off[p]:off[p + 1]] for p in PERM]
    return jnp.concatenate(cols + [jnp.zeros((w_in_full.shape[0], SMALL_PAD), w_in_full.dtype)], axis=1)


def _unpermute_in(wp):
    off = np.concatenate([[0], np.cumsum([IN_WIDTHS[p] for p in PERM])])
    pieces = {p: wp[:, off[i]:off[i + 1]] for i, p in enumerate(PERM)}
    return jnp.concatenate([pieces[p] for p in range(len(IN_WIDTHS))], axis=1)


def _chip_cols(full, chip, n):
    return lax.dynamic_slice_in_dim(full, chip * n, n, axis=1)


def kernel(x, c, ctx, c_ctx, w_ada, b_ada, norm1_w, w_in, gla_up_f, gla_bias_f, gla_up_b, gla_bias_b, gla_norm_w, conv_w, conv_b, dt_bias_f, dt_bias_b, a_log_f, a_log_b, d_skip, ssm_norm_w, w_pa, w_pb, w_out, norm2_w, w_gate, w_up, w_down, final_norm_w, loss_target, m_c_ctx, m_w_ada, m_b_ada, m_norm1_w, m_w_in, m_gla_up_f, m_gla_bias_f, m_gla_up_b, m_gla_bias_b, m_gla_norm_w, m_conv_w, m_conv_b, m_dt_bias_f, m_dt_bias_b, m_a_log_f, m_a_log_b, m_d_skip, m_ssm_norm_w, m_w_pa, m_w_pb, m_w_out, m_norm2_w, m_w_gate, m_w_up, m_w_down, m_final_norm_w, v_c_ctx, v_w_ada, v_b_ada, v_norm1_w, v_w_in, v_gla_up_f, v_gla_bias_f, v_gla_up_b, v_gla_bias_b, v_gla_norm_w, v_conv_w, v_conv_b, v_dt_bias_f, v_dt_bias_b, v_a_log_f, v_a_log_b, v_d_skip, v_ssm_norm_w, v_w_pa, v_w_pb, v_w_out, v_norm2_w, v_w_gate, v_w_up, v_w_down, v_final_norm_w):
    given = dict(locals())
    W = {n: given[n] for n in WEIGHTS}
    M = {n: given["m_" + n] for n in WEIGHTS}
    V = {n: given["v_" + n] for n in WEIGHTS}
    L, Lc = x.shape[1], ctx.shape[1]
    tr = ROW_TILE
    assert L % tr == 0 and Lc % tr == 0 and L % Lc == 0 and Lc % SSM_C == 0
    n_lat_tiles = L // tr
    xi, yi, ci = _place()
    chip, me = 2 * xi + yi, 4 * xi + 2 * yi + ci
    x2, ctx2 = x[0], ctx[0]

    g0 = _allgather_small(_pack([c[0]] + [W[n][0] for n in SMALL_SHARDED]), "gather_c")
    g0 = g0.reshape(N_DEV, -1)
    c_all = g0[:, :D]
    small_full, off = {}, D
    for n in SMALL_SHARDED:
        r, cols = W[n].shape[1:]
        small_full[n] = jnp.concatenate([g0[2 * k, off:off + r * cols].reshape(r, cols) for k in range(N_CHIPS)], axis=1)
        off += r * cols
    up_f, up_b, conv_w_full = (small_full[n] for n in SMALL_SHARDED)

    cvec = jnp.zeros((ADA_ROWS, D), F32).at[:N_DEV].set(c_all).at[N_DEV].set(c_ctx)
    ada_cols = w_ada.shape[2]
    ada_part = _ada_fwd(cvec, w_ada[0], _chip_cols(b_ada, chip, ada_cols))
    g1_all = _allgather_small(ada_part, "gather_ada")
    ada_full = jnp.concatenate([g1_all[2 * k] for k in range(N_CHIPS)], axis=1)
    mine = lax.dynamic_slice_in_dim(ada_full, me, 1, axis=0)
    sh1, sc1, g1, sh2, sc2, g2 = (mine[:, k * D:(k + 1) * D] for k in range(6))
    csh1, csc1 = ada_full[N_DEV:N_DEV + 1, :D], ada_full[N_DEV:N_DEV + 1, D:2 * D]
    mod = jnp.stack([jnp.stack([sh1, sc1]), jnp.stack([csh1, csc1])])

    full = {'w_in': _blocks_to_full(_gather_split(w_in[0].astype(BF16), "gather_w_in"), 'w_in')}
    wp = _permute_in(full['w_in'])
    later = [n for n in BIG if n != 'w_in']

    def lr_rows(up, base):
        return jnp.zeros((128, GLA_H * GLA_DK), F32).at[base:base + GLA_RANK].set(up)
    u2 = jnp.stack([lr_rows(up_f, SM_LRF), lr_rows(up_b, SM_LRB)])
    u2_hi = u2.astype(BF16)
    u2_lo = (u2 - u2_hi.astype(F32)).astype(BF16)
    ut = u2.transpose(0, 2, 1)
    ut_hi = ut.astype(BF16)
    ut_lo = (ut - ut_hi.astype(F32)).astype(BF16)
    gbias = jnp.stack([gla_bias_f, gla_bias_b])
    kc = _ssd_consts(jnp.stack([dt_bias_f[0], dt_bias_b[0]]), jnp.stack([a_log_f[0], a_log_b[0]]))
    gw4 = jnp.tile(gla_norm_w, (1, GLA_H))
    dskip_exp = jnp.repeat(d_skip, SSM_P, axis=1)
    n_gla = (L // GLA_C, Lc // GLA_C)
    n_ssd = (L // SSM_C, Lc // SSM_C)

    h1 = _norm_mod(x2, ctx2, norm1_w, mod, n_lat_tiles, tr)
    parts = _mm(h1, wp, "nn", ACT, "mm_in", tm=1408, tn=3456)
    sm = _mm(h1, wp[:, COL_SM:], "nn", F32, "mm_in_small", tm=1408)
    xbc = _conv_fwd(parts, conv_w_full, conv_b, L // Lc, Lc)
    *o2, gla_hist = _gla_fwd(parts, sm, u2_hi, u2_lo, gbias, *n_gla)
    (*y2, ssd_hist), gathered = _ssd_fwd(xbc, sm, kc, *n_ssd, _gather_comm([W[n][0].astype(BF16) for n in later]))
    full.update({n: _blocks_to_full(g, n) for n, g in zip(later, gathered)})
    oan = _gla_out(o2, parts, gw4, tr)
    obn = _ssd_out(y2, xbc, parts, dskip_exp, ssm_norm_w, tr)
    ya = _mm(oan, full['w_pa'], "nn", ACT, "mm_pa", tm=1408)
    yb = _mm(obn, full['w_pb'], "nn", ACT, "mm_pb", tm=1408)
    merged = _merge(ya, yb, parts, tr)
    mix = _mm(merged, full['w_out'], "nn", ACT, "mm_out", tm=1408)
    h2, u = _resid_norm_mod(x2, ctx2, mix, g1, norm2_w, sh2, sc2, n_lat_tiles, tr)
    gp = _mm(u, full['w_gate'], "nn", ACT, "mm_gate", tm=1408, tn=D_FF)
    up = _mm(u, full['w_up'], "nn", ACT, "mm_up", tm=1408, tn=D_FF)
    act = _swiglu_act(gp, up, tr)
    f = _mm(act, full['w_down'], "nn", ACT, "mm_down", tm=1408, tk=D_FF)
    dh3, df, acc_loss = _loss_head(h2, f, loss_target[0], g2, final_norm_w[None], n_lat_tiles, tr)

    dw = {}
    da = _mm(df, full['w_down'], "nt", ACT, "mm_down_dx", tm=1408, tn=D_FF)
    dw['w_down'] = _mm(act, df, "tn", BF16, "mm_down_dw", tm=1408, tk=2816)
    dgp, dup = _swiglu_act_bwd(da, gp, up, tr)
    du_a = _mm(dgp, full['w_gate'], "nt", ACT, "mm_gate_dx", tm=1408, tk=D_FF)
    du_b = _mm(dup, full['w_up'], "nt", ACT, "mm_up_dx", tm=1408, tk=D_FF)
    dw['w_gate'] = _mm(u, dgp, "tn", BF16, "mm_gate_dw", tm=1024, tn=1408, tk=2816)
    dw['w_up'] = _mm(u, dup, "tn", BF16, "mm_up_dw", tm=1024, tn=1408, tk=2816)
    dh2, dmix, acc_ffn = _ffn_in_bwd(du_a, du_b, h2, dh3, mix, sc2, g1, norm2_w, tr)
    dmerged = _mm(dmix, full['w_out'], "nt", ACT, "mm_out_dx", tm=1408)
    dw['w_out'] = _mm(merged, dmix, "tn", BF16, "mm_out_dw", tm=1024, tk=2816)
    dya, dyb, dparts = _merge_bwd(dmerged, ya, yb, parts, lax.empty((L + Lc, PW), BF16), tr)
    doan = _mm(dya, full['w_pa'], "nt", ACT, "mm_pa_dx", tm=1408)
    dw['w_pa'] = _mm(oan, dya, "tn", BF16, "mm_pa_dw", tm=1024, tk=2816)
    dobn = _mm(dyb, full['w_pb'], "nt", ACT, "mm_pb_dx", tm=1408)
    dw['w_pb'] = _mm(obn, dyb, "tn", BF16, "mm_pb_dw", tm=1024, tk=2816)
    do, dparts, acc_gla = _gla_out_bwd(doan, o2, parts, gw4, dparts, tr)
    dq, dk, dv, dpre = _gla_bwd(do, parts, sm, u2_hi, u2_lo, gbias, gla_hist, *n_gla)
    dy, dparts, acc_ssd = _ssd_out_bwd(dobn, y2, xbc, parts, dskip_exp, ssm_norm_w, dparts, tr)
    (dx_scan, db_scan, dc_scan, dsm, acc_alog), exchanged = _ssd_bwd(
        dy, xbc, sm, kc, ssd_hist, *n_ssd, _exchange_comm([_full_to_blocks(dw[n], n) for n in later]))
    exchanged = dict(zip(later, exchanged))
    dparts, acc_conv = _conv_bwd(dx_scan, db_scan, dc_scan, dy, dskip_exp, parts, conv_w_full, conv_b, dparts, L // Lc, Lc)
    dparts = _gla_assemble(dq, dk, dv, dparts, tr)
    dparts, dup_gla, acc_gbias, acc_dtb = _small_assemble(dpre, dsm, sm, ut_hi, ut_lo, dparts, tr)
    dw['w_in'] = _unpermute_in(_mm(h1, dparts, "tn", BF16, "mm_in_dw", tm=1024, tn=1152, tk=2816))
    dh1, (exchanged['w_in'],) = _mm(dparts, wp, "nt", F32, "mm_in_dx", tm=1408, tk=3456,
                                    comm=_exchange_comm([_full_to_blocks(dw['w_in'], 'w_in')]))
    dx, acc_n1 = _norm1_bwd(dh1, x2, ctx2, dh2, norm1_w, mod, n_lat_tiles, tr)

    partial = dict(
        norm1_w=acc_n1[0, 2] + acc_n1[1, 2],
        gla_up_f=dup_gla[0, SM_LRF:SM_LRF + GLA_RANK], gla_bias_f=acc_gbias[0],
        gla_up_b=dup_gla[1, SM_LRB:SM_LRB + GLA_RANK], gla_bias_b=acc_gbias[1],
        gla_norm_w=acc_gla[0].reshape(GLA_H, GLA_DV).sum(0),
        conv_w=acc_conv[:SSM_CONV], conv_b=acc_conv[SSM_CONV],
        dt_bias_f=acc_dtb[0, SM_DTF:SM_DTF + SSM_HEADS], dt_bias_b=acc_dtb[0, SM_DTB:SM_DTB + SSM_HEADS],
        a_log_f=acc_alog[0, :, 0, :SSM_HPG], a_log_b=acc_alog[1, :, 0, :SSM_HPG],
        d_skip=acc_ssd[1].reshape(SSM_HEADS, SSM_P).sum(1), ssm_norm_w=acc_ssd[0],
        norm2_w=acc_ffn[2], final_norm_w=acc_loss[0],
    )
    dada = jnp.concatenate([acc_n1[0, 1], acc_n1[0, 0], acc_ffn[3], acc_ffn[1], acc_ffn[0], acc_loss[1]])
    dada_ctx = jnp.concatenate([acc_n1[1, 1], acc_n1[1, 0], jnp.zeros((4 * D,), F32)])
    names = list(partial)
    payload = [partial[n] for n in names] + [dada + dada_ctx, dada_ctx, acc_loss[2], dada]
    sizes = [int(np.prod(p.shape)) for p in payload]
    g8 = _allgather_small(_pack(payload), "gather_small_grads")
    summed = _unpack(_sum8(g8), [(s,) for s in sizes])
    grads = {n: s.reshape(W[n].shape if n not in SMALL_SHARDED else partial[n].shape) for n, s in zip(names, summed)}
    grads['b_ada'] = summed[len(names)].reshape(b_ada.shape)
    dada_ctx_sum = summed[len(names) + 1]
    loss = jnp.sum(summed[len(names) + 2])
    dada_all = g8.reshape(N_DEV, -1)[:, sum(sizes[:-1]):sum(sizes)]

    dada16 = jnp.zeros((ADA_ROWS, ada_cols), F32)
    dada16 = dada16.at[:N_DEV].set(_chip_cols(dada_all, chip, ada_cols)).at[N_DEV].set(_chip_cols(dada_ctx_sum[None], chip, ada_cols)[0])
    g_wada, dl_wada, m_wada, v_wada, p16 = _wada_bwd_adam(cvec, dada16, w_ada[0], m_w_ada[0], v_w_ada[0])
    p8 = _allgather_small(p16[N_DEV:], "gather_cctx")
    grads['c_ctx'] = _cctx_grad(p8[:, 0:1, :], c_ctx[None])[0]
    for n in SMALL_SHARDED:
        grads[n] = _chip_cols(grads[n], chip, W[n].shape[2])[None]

    small = [n for n in WEIGHTS if n not in BIG and n != 'w_ada']
    shapes = [W[n].shape for n in small]
    dl_s, m_s, v_s = _adam_small(*[_pack([d[n] for n in small]) for d in (W, grads, M, V)])
    delta = dict(zip(small, _unpack(dl_s, shapes)))
    new_m = dict(zip(small, _unpack(m_s, shapes)))
    new_v = dict(zip(small, _unpack(v_s, shapes)))
    grads['w_ada'], delta['w_ada'], new_m['w_ada'], new_v['w_ada'] = g_wada[None], dl_wada[None], m_wada[None], v_wada[None]

    for n in BIG:
        g, dl, m2, v2 = _reduce_adam(exchanged[n], W[n][0], M[n][0], V[n][0], "adam_" + n)
        grads[n], delta[n], new_m[n], new_v[n] = g[None], dl[None], m2[None], v2[None]

    return (loss, dx[None], *[grads[n] for n in WEIGHTS], *[delta[n] for n in WEIGHTS],
            *[new_m[n] for n in WEIGHTS], *[new_v[n] for n in WEIGHTS])
```

```python
import functools

import numpy as np
import jax
import jax.numpy as jnp
from jax import lax
from jax.experimental import pallas as pl
from jax.experimental.pallas import tpu as pltpu

F32 = jnp.float32
BF16 = jnp.bfloat16
MESH = pl.DeviceIdType.MESH

D = 1024
EPS = 1e-6
GRID_W = 64
GLA_H, GLA_DK, GLA_DV, GLA_RANK, GLA_TAU = 4, 128, 256, 16, 16.0
GLA_C = 128
SSM_INNER, SSM_P, SSM_HEADS, SSM_G, SSM_HPG, SSM_N = 2048, 64, 32, 4, 8, 128
SSM_C = 128
SSM_CONV, CONV_LEFT = 4, 2
D_FF = 2816
IN_WIDTHS = (512, 512, 1024, 1024, 16, 16, 2048, 2048, 512, 512, 32, 32, 1024, 1024)
D_IN = sum(IN_WIDTHS)
PERM = (6, 7, 8, 9, 3, 0, 1, 2, 12, 13, 4, 5, 10, 11)
PW = 10368
SMALL_PAD = PW - D_IN
COL_Z, COL_XBC, COL_R, COL_Q, COL_K, COL_V, COL_GA, COL_GB, COL_SM = 0, 2048, 5120, 6144, 6656, 7168, 8192, 9216, 10240
SM_LRF, SM_LRB, SM_DTF, SM_DTB = 0, 16, 32, 64
EXP_CLAMP = 80.0
ADAM_LR, ADAM_B1, ADAM_B2, ADAM_EPS, ADAM_WD, ADAM_STEP = 0.001, 0.9, 0.999, 1e-08, 0.01, 10
N_CHIPS, N_DEV = 4, 8
VMEM_LIMIT = 56 * 1024 * 1024


def _cparams(sem=None):
    return pltpu.CompilerParams(dimension_semantics=sem, vmem_limit_bytes=VMEM_LIMIT)


def _dg(a, b, ca, cb):
    return lax.dot_general(a, b, (((ca,), (cb,)), ((), ())), preferred_element_type=F32)


def _nn(a, b):
    return _dg(a, b, 1, 0)


def _nt(a, b):
    return _dg(a, b, 1, 1)


def _tn(a, b):
    return _dg(a, b, 0, 0)


def _bf(x):
    return x.astype(BF16)


def _split(x):
    hi = x.astype(BF16)
    return hi, (x - hi.astype(F32)).astype(BF16)


def _nn_x(a, b_exact):
    hi, lo = _split(a)
    return _nn(hi, b_exact) + _nn(lo, b_exact)


def _x_nn(a_exact, b):
    hi, lo = _split(b)
    return _nn(a_exact, hi) + _nn(a_exact, lo)


def _nn3(a, b_hi, b_lo):
    hi, lo = _split(a)
    return _nn(hi, b_hi) + _nn(lo, b_hi) + _nn(hi, b_lo)


def _sigmoid(x):
    return 1.0 / (1.0 + jnp.exp(-x))


def _silu(x):
    return x * _sigmoid(x)


def _dsilu(x):
    s = _sigmoid(x)
    return s * (1.0 + x * (1.0 - s))


def _softplus(x):
    return jnp.maximum(x, 0.0) + jnp.log(1.0 + jnp.exp(-jnp.abs(x)))


def _log_sigmoid(x):
    return jnp.minimum(x, 0.0) - jnp.log(1.0 + jnp.exp(-jnp.abs(x)))


def _tile(n, target, mult=8):
    best = None
    for t in range(mult, min(n, target) + 1, mult):
        if n % t == 0:
            best = t
    assert best is not None, (n, target, mult)
    return best


def _mm(a, b, mode, out_dtype, name, tm=512, tn=1024, tk=2048, comm=None, swiglu_up=None):
    if mode == "nn":
        (M, K), N = a.shape, b.shape[1]
    elif mode == "nt":
        (M, K), N = a.shape, b.shape[0]
    else:
        (K, M), N = a.shape, b.shape[1]
    tm, tn, tk = _tile(M, tm, 128), _tile(N, tn, 128), _tile(K, tk, 128)
    nk = K // tk
    ca, cb = {"nn": (1, 0), "nt": (1, 1), "tn": (0, 0)}[mode]

    def body(a_ref, *rest):
        if swiglu_up is None:
            (b_ref, o_ref, *acc), av = rest, a_ref[...]
        else:
            (up_ref, b_ref, o_ref, *acc) = rest
            av = (_silu(a_ref[...].astype(F32)) * up_ref[...].astype(F32)).astype(BF16)
        part = _dg(av, b_ref[...], ca, cb)
        if nk == 1:
            o_ref[...] = part.astype(out_dtype)
        else:
            k = pl.program_id(2)

            @pl.when(k == 0)
            def _():
                acc[0][...] = part

            @pl.when(k > 0)
            def _():
                acc[0][...] += part

            @pl.when(k == nk - 1)
            def _():
                o_ref[...] = acc[0][...].astype(out_dtype)

    a_spec = pl.BlockSpec((tk, tm), lambda i, j, k: (k, i)) if mode == "tn" else pl.BlockSpec((tm, tk), lambda i, j, k: (i, k))
    b_spec = pl.BlockSpec((tn, tk), lambda i, j, k: (j, k)) if mode == "nt" else pl.BlockSpec((tk, tn), lambda i, j, k: (k, j))
    gi, gj = M // tm, N // tn
    scratch = [pltpu.VMEM((tm, tn), F32)] if nk > 1 else []
    out_spec, out_shape = pl.BlockSpec((tm, tn), lambda i, j, k: (i, j)), jax.ShapeDtypeStruct((M, N), out_dtype)
    if comm is None:
        lhs = [a] if swiglu_up is None else [a, swiglu_up]
        return pl.pallas_call(
            body, name=name, grid=(gi, gj, nk), in_specs=[a_spec] * len(lhs) + [b_spec], out_specs=out_spec,
            out_shape=out_shape, scratch_shapes=scratch, compiler_params=_cparams(("arbitrary", "arbitrary", "arbitrary")),
        )(*lhs, b)
    assert swiglu_up is None
    at = lambda i, j, k: (pl.program_id(0) == i) & (pl.program_id(1) == j) & (pl.program_id(2) == k)
    hosted = _hosted(body, 2, 1, len(scratch), comm, lambda: at(0, 0, 0), lambda: at(gi - 1, 0, 0),
                     lambda: at(gi - 1, gj - 1, nk - 1))
    outs = pl.pallas_call(
        hosted, name=name, grid=(gi, gj, nk), in_specs=[a_spec, b_spec] + [_ANY] * len(comm.arrays),
        out_specs=[out_spec] + [_ANY] * len(comm.out_shape), out_shape=[out_shape] + comm.out_shape,
        scratch_shapes=scratch + comm.scratch, compiler_params=_cparams(("arbitrary", "arbitrary", "arbitrary")),
    )(a, b, *comm.arrays)
    return outs[0], outs[1:]


def _place():
    return lax.axis_index("x"), lax.axis_index("y"), lax.axis_index("c")


def _flip(v, bit):
    return 1 - v if bit else v


def _allgather_small(v, name):
    R, C = v.shape

    def body(v_ref, out_ref, send_sems, recv_sems, local_sem):
        x, y, c = _place()
        me = 4 * x + 2 * y + c
        mine = pltpu.make_async_copy(v_ref, out_ref.at[me], local_sem)
        mine.start()

        def peer(r):
            return _flip(x, (r >> 2) & 1), _flip(y, (r >> 1) & 1), _flip(c, r & 1)

        sends = [pltpu.make_async_remote_copy(
            src_ref=v_ref, dst_ref=out_ref.at[me], send_sem=send_sems.at[r - 1], recv_sem=recv_sems.at[r - 1],
            device_id=peer(r), device_id_type=MESH) for r in range(1, N_DEV)]
        for cp in sends:
            cp.start()
        for r in range(1, N_DEV):
            px, py, pc = peer(r)
            pltpu.make_async_remote_copy(
                src_ref=v_ref, dst_ref=out_ref.at[4 * px + 2 * py + pc], send_sem=send_sems.at[r - 1],
                recv_sem=recv_sems.at[r - 1], device_id=(x, y, c), device_id_type=MESH).wait_recv()
        for cp in sends:
            cp.wait_send()
        mine.wait()

    return pl.pallas_call(
        body, name=name, out_shape=jax.ShapeDtypeStruct((N_DEV, R, C), v.dtype),
        in_specs=[pl.BlockSpec(memory_space=pltpu.VMEM)], out_specs=pl.BlockSpec(memory_space=pltpu.VMEM),
        scratch_shapes=[pltpu.SemaphoreType.DMA((N_DEV - 1,)), pltpu.SemaphoreType.DMA((N_DEV - 1,)), pltpu.SemaphoreType.DMA],
        compiler_params=pltpu.CompilerParams(vmem_limit_bytes=VMEM_LIMIT),
    )(v)


_CHIP_RELATIONS = ((1, 0), (0, 1), (1, 1))


def _gather_split(shard, name):
    rows, cols = shard.shape
    half = rows // 2

    def body(in_ref, out_ref, send_sems, recv_sems, local_sem):
        x, y, c = _place()
        chip = 2 * x + y
        mine = pl.ds(pl.multiple_of(c * half, 16), half)
        local = pltpu.make_async_copy(in_ref, out_ref.at[chip], local_sem)
        local.start()
        peers = [(_flip(x, fx), _flip(y, fy)) for fx, fy in _CHIP_RELATIONS]
        sends = [pltpu.make_async_remote_copy(
            src_ref=in_ref.at[mine], dst_ref=out_ref.at[chip, mine], send_sem=send_sems.at[j], recv_sem=recv_sems.at[j],
            device_id=(px, py, c), device_id_type=MESH) for j, (px, py) in enumerate(peers)]
        for cp in sends:
            cp.start()
        for j, (px, py) in enumerate(peers):
            landed = out_ref.at[2 * px + py, mine]
            pltpu.make_async_remote_copy(
                src_ref=landed, dst_ref=landed, send_sem=send_sems.at[j], recv_sem=recv_sems.at[j],
                device_id=(x, y, c), device_id_type=MESH).wait_recv()
            fwd = pltpu.make_async_remote_copy(
                src_ref=landed, dst_ref=landed, send_sem=send_sems.at[3 + j], recv_sem=recv_sems.at[3 + j],
                device_id=(x, y, 1 - c), device_id_type=MESH)
            fwd.start()
            sends.append(fwd)
        for j in range(3):
            landed = out_ref.at[0, mine]
            pltpu.make_async_remote_copy(
                src_ref=landed, dst_ref=landed, send_sem=send_sems.at[3 + j], recv_sem=recv_sems.at[3 + j],
                device_id=(x, y, c), device_id_type=MESH).wait_recv()
        for cp in sends:
            cp.wait_send()
        local.wait()

    any_spec = pl.BlockSpec(memory_space=pl.ANY)
    return pl.pallas_call(
        body, name=name, out_shape=jax.ShapeDtypeStruct((N_CHIPS, rows, cols), shard.dtype),
        in_specs=[any_spec], out_specs=any_spec,
        scratch_shapes=[pltpu.SemaphoreType.DMA((6,)), pltpu.SemaphoreType.DMA((6,)), pltpu.SemaphoreType.DMA],
    )(shard)


class _Comm:
    def __init__(self, arrays, out_shape, scratch, start, middle, finish):
        self.arrays, self.out_shape, self.scratch = arrays, out_shape, scratch
        self.start, self.middle, self.finish = start, middle, finish


def _hosted(body, n_in, n_out, n_scratch, comm, first, middle, last):
    nc, no = len(comm.arrays), len(comm.out_shape)

    def wrapped(*refs):
        a = n_in + nc
        b = a + n_out + no
        ins, c_ins, outs, c_outs = refs[:n_in], refs[n_in:a], refs[a:a + n_out], refs[a + n_out:b]
        scratch, c_sems = refs[b:b + n_scratch], refs[b + n_scratch:]

        @pl.when(first())
        def _():
            comm.start(c_ins, c_outs, c_sems)

        body(*ins, *outs, *scratch)
        if comm.middle is not None:
            @pl.when(middle())
            def _():
                comm.middle(c_ins, c_outs, c_sems)

        @pl.when(last())
        def _():
            comm.finish(c_ins, c_outs, c_sems)

    return wrapped


_ANY = pl.BlockSpec(memory_space=pl.ANY)


def _gather_comm(shards):
    n = len(shards)

    def copies(kind, ins, outs, sems):
        send_sems, recv_sems, local_sems = sems
        x, y, c = _place()
        chip = 2 * x + y
        if kind == "local":
            return [pltpu.make_async_copy(ins[i], outs[i].at[chip], local_sems.at[i]) for i in range(n)]
        made = []
        for i in range(n):
            for j, (fx, fy) in enumerate(_CHIP_RELATIONS):
                px, py = _flip(x, fx), _flip(y, fy)
                slot, to = (chip, (px, py, c)) if kind == "send" else (2 * px + py, (x, y, c))
                made.append(pltpu.make_async_remote_copy(
                    src_ref=ins[i], dst_ref=outs[i].at[slot], send_sem=send_sems.at[i, j], recv_sem=recv_sems.at[i, j],
                    device_id=to, device_id_type=MESH))
        return made

    def start(ins, outs, sems):
        for cp in copies("local", ins, outs, sems) + copies("send", ins, outs, sems):
            cp.start()

    def finish(ins, outs, sems):
        for cp in copies("recv", ins, outs, sems):
            cp.wait_recv()
        for cp in copies("send", ins, outs, sems):
            cp.wait_send()
        for cp in copies("local", ins, outs, sems):
            cp.wait()

    return _Comm(list(shards), [jax.ShapeDtypeStruct((N_CHIPS,) + s.shape, s.dtype) for s in shards],
                 [pltpu.SemaphoreType.DMA((n, 3)), pltpu.SemaphoreType.DMA((n, 3)), pltpu.SemaphoreType.DMA((n,))],
                 start, None, finish)


def _exchange_comm(blocks):
    n = len(blocks)

    def copies(kind, ins, outs, sems):
        send_sems, recv_sems, local_sems = sems
        x, y, c = _place()
        chip = 2 * x + y
        me, sibling = (x, y, c), (x, y, 1 - c)

        def remote(src, dst, i, j, to):
            return pltpu.make_async_remote_copy(src_ref=src, dst_ref=dst, send_sem=send_sems.at[i, j],
                                                recv_sem=recv_sems.at[i, j], device_id=to, device_id_type=MESH)

        made = []
        for i in range(n):
            if kind == "local":
                made.append(pltpu.make_async_copy(ins[i].at[chip], outs[i].at[chip], local_sems.at[i]))
                continue
            for j, (fx, fy) in enumerate(_CHIP_RELATIONS):
                px, py = _flip(x, fx), _flip(y, fy)
                src = 2 * px + py
                if kind == "first":
                    made.append(remote(ins[i].at[src], outs[i].at[chip], i, j, (px, py, c)))
                elif kind == "landed":
                    made.append(remote(ins[i].at[src], outs[i].at[src], i, j, me))
                elif kind == "passed":
                    made.append(remote(outs[i].at[src], outs[i].at[N_CHIPS + src], i, 4 + j, sibling))
            if kind == "first":
                made.append(remote(ins[i].at[chip], outs[i].at[N_CHIPS + chip], i, 3, sibling))
            if kind == "arrivals":
                made += [remote(ins[i].at[0], outs[i].at[0], i, j, me) for j in (3, 4, 5, 6)]
        return made

    def start(ins, outs, sems):
        for cp in copies("local", ins, outs, sems) + copies("first", ins, outs, sems):
            cp.start()

    def middle(ins, outs, sems):
        for got, fwd in zip(copies("landed", ins, outs, sems), copies("passed", ins, outs, sems)):
            got.wait_recv()
            fwd.start()

    def finish(ins, outs, sems):
        for cp in copies("arrivals", ins, outs, sems):
            cp.wait_recv()
        for cp in copies("first", ins, outs, sems) + copies("passed", ins, outs, sems):
            cp.wait_send()
        for cp in copies("local", ins, outs, sems):
            cp.wait()

    return _Comm(list(blocks), [jax.ShapeDtypeStruct((N_DEV,) + b.shape[1:], b.dtype) for b in blocks],
                 [pltpu.SemaphoreType.DMA((n, 7)), pltpu.SemaphoreType.DMA((n, 7)), pltpu.SemaphoreType.DMA((n,))],
                 start, middle, finish)


def _row_spec(tr, w, col=0):
    return pl.BlockSpec((tr, w), lambda i: (i, col))


def _vec_spec(w):
    return pl.BlockSpec((1, w), lambda i: (0, 0))


def _acc_spec(w):
    return pl.BlockSpec((8, w), lambda i: (0, 0))


def _rms(x):
    return lax.rsqrt(jnp.mean(x * x, axis=-1, keepdims=True) + EPS)


def _rms_bwd(dn, n, rstd):
    return rstd * (dn - n * jnp.mean(dn * n, axis=-1, keepdims=True))


def _colsum(x):
    return jnp.sum(x, axis=0, keepdims=True)


def _zero_first(ref):
    @pl.when(pl.program_id(0) == 0)
    def _():
        ref[...] = jnp.zeros_like(ref)


def _x_specs(tr, n_lat_tiles):
    return [pl.BlockSpec((tr, D), lambda i: (jnp.minimum(i, n_lat_tiles - 1), 0)),
            pl.BlockSpec((tr, D), lambda i: (jnp.maximum(i - n_lat_tiles, 0), 0))]


def _x_tile(x_ref, c_ref, n_lat_tiles):
    return jnp.where(pl.program_id(0) >= n_lat_tiles, c_ref[...], x_ref[...])


def _norm_mod(x, ctx, w, mod, n_lat_tiles, tr):
    R = x.shape[0] + ctx.shape[0]

    def body(x_ref, c_ref, w_ref, mod_ref, o_ref):
        xv = _x_tile(x_ref, c_ref, n_lat_tiles)
        n = xv * _rms(xv) * w_ref[...]
        o_ref[...] = (n * (1.0 + mod_ref[0, 1]) + mod_ref[0, 0]).astype(BF16)

    return pl.pallas_call(
        body, name="norm1_mod", grid=(R // tr,),
        in_specs=_x_specs(tr, n_lat_tiles) + [_vec_spec(D), pl.BlockSpec(
            (1, 2, 1, D), lambda i: (jnp.where(i >= n_lat_tiles, 1, 0), 0, 0, 0))],
        out_specs=_row_spec(tr, D), out_shape=jax.ShapeDtypeStruct((R, D), BF16),
        compiler_params=_cparams(("arbitrary",)),
    )(x, ctx, w, mod)


def _resid_norm_mod(x, ctx, mix, g1, w2, sh2, sc2, n_lat_tiles, tr):
    R = x.shape[0] + ctx.shape[0]

    def body(x_ref, c_ref, mix_ref, g1_ref, w_ref, sh_ref, sc_ref, h2_ref, u_ref):
        h2 = _x_tile(x_ref, c_ref, n_lat_tiles) + g1_ref[...] * mix_ref[...]
        h2_ref[...] = h2
        n = h2 * _rms(h2) * w_ref[...]
        u_ref[...] = (n * (1.0 + sc_ref[...]) + sh_ref[...]).astype(BF16)

    return pl.pallas_call(
        body, name="resid_norm2_mod", grid=(R // tr,),
        in_specs=_x_specs(tr, n_lat_tiles) + [_row_spec(tr, D)] + [_vec_spec(D)] * 4,
        out_specs=[_row_spec(tr, D), _row_spec(tr, D)],
        out_shape=[jax.ShapeDtypeStruct((R, D), F32), jax.ShapeDtypeStruct((R, D), BF16)],
        compiler_params=_cparams(("arbitrary",)),
    )(x, ctx, mix, g1, w2, sh2, sc2)


def _loss_head(h2, f, target, g2, fw, n_lat_tiles, tr):
    R = h2.shape[0]

    def body(h2_ref, f_ref, t_ref, g2_ref, fw_ref, dh3_ref, df_ref, acc_ref):
        _zero_first(acc_ref)
        lat = pl.program_id(0) < n_lat_tiles
        fv = f_ref[...]
        h3 = h2_ref[...] + g2_ref[...] * fv
        rstd = _rms(h3)
        n = h3 * rstd
        err = n * fw_ref[...] - t_ref[...]
        dy = err * (1.0 / D)
        dh3 = jnp.where(lat, _rms_bwd(dy * fw_ref[...], n, rstd), 0.0)
        dh3_ref[...] = dh3
        df_ref[...] = (g2_ref[...] * dh3).astype(BF16)
        acc_ref[0:1, :] += jnp.where(lat, _colsum(dy * n), 0.0)
        acc_ref[1:2, :] += _colsum(dh3 * fv)
        acc_ref[2:3, :] += jnp.where(lat, _colsum(err * err) * (0.5 / D), 0.0)

    return pl.pallas_call(
        body, name="loss_head", grid=(R // tr,),
        in_specs=[_row_spec(tr, D), _row_spec(tr, D),
                  pl.BlockSpec((tr, D), lambda i: (jnp.minimum(i, n_lat_tiles - 1), 0)), _vec_spec(D), _vec_spec(D)],
        out_specs=[_row_spec(tr, D), _row_spec(tr, D), _acc_spec(D)],
        out_shape=[jax.ShapeDtypeStruct((R, D), F32), jax.ShapeDtypeStruct((R, D), BF16), jax.ShapeDtypeStruct((8, D), F32)],
        compiler_params=_cparams(("arbitrary",)),
    )(h2, f, target, g2, fw)


def _ffn_in_bwd(du_a, du_b, h2, dh3, mix, sc2, g1, w2, tr):
    R = h2.shape[0]

    def body(dua_ref, dub_ref, h2_ref, dh3_ref, mix_ref, sc_ref, g1_ref, w_ref, dh2_ref, dmix_ref, acc_ref):
        _zero_first(acc_ref)
        du = _f32(dua_ref) + _f32(dub_ref)
        h2 = h2_ref[...]
        rstd = _rms(h2)
        n = h2 * rstd
        dnw = du * (1.0 + sc_ref[...])
        dh2 = dh3_ref[...] + _rms_bwd(dnw * w_ref[...], n, rstd)
        dh2_ref[...] = dh2
        dmix_ref[...] = (g1_ref[...] * dh2).astype(BF16)
        acc_ref[0:1, :] += _colsum(du * n * w_ref[...])
        acc_ref[1:2, :] += _colsum(du)
        acc_ref[2:3, :] += _colsum(dnw * n)
        acc_ref[3:4, :] += _colsum(dh2 * mix_ref[...])

    return pl.pallas_call(
        body, name="ffn_in_bwd", grid=(R // tr,),
        in_specs=[_row_spec(tr, D)] * 5 + [_vec_spec(D)] * 3,
        out_specs=[_row_spec(tr, D), _row_spec(tr, D), _acc_spec(D)],
        out_shape=[jax.ShapeDtypeStruct((R, D), F32), jax.ShapeDtypeStruct((R, D), BF16), jax.ShapeDtypeStruct((8, D), F32)],
        compiler_params=_cparams(("arbitrary",)),
    )(du_a, du_b, h2, dh3, mix, sc2, g1, w2)


def _norm1_bwd(dh1, x, ctx, dh2, w1, mod, n_lat_tiles, tr):
    R = x.shape[0] + ctx.shape[0]

    def body(dh1_ref, x_ref, c_ref, dh2_ref, w_ref, mod_ref, dx_ref, acc_ref):
        i = pl.program_id(0)

        @pl.when((i == 0) | (i == n_lat_tiles))
        def _():
            acc_ref[...] = jnp.zeros_like(acc_ref)

        dh1 = dh1_ref[...]
        x = _x_tile(x_ref, c_ref, n_lat_tiles)
        rstd = _rms(x)
        n = x * rstd
        dnw = dh1 * (1.0 + mod_ref[0, 1])

        @pl.when(i < n_lat_tiles)
        def _():
            dx_ref[...] = dh2_ref[...] + _rms_bwd(dnw * w_ref[...], n, rstd)

        acc_ref[0, 0:1, :] += _colsum(dh1 * n * w_ref[...])
        acc_ref[0, 1:2, :] += _colsum(dh1)
        acc_ref[0, 2:3, :] += _colsum(dnw * n)

    sel = lambda i: jnp.where(i >= n_lat_tiles, 1, 0)
    return pl.pallas_call(
        body, name="norm1_bwd", grid=(R // tr,),
        in_specs=[_row_spec(tr, D)] + _x_specs(tr, n_lat_tiles) + [_row_spec(tr, D), _vec_spec(D),
                                                                   pl.BlockSpec((1, 2, 1, D), lambda i: (sel(i), 0, 0, 0))],
        out_specs=[pl.BlockSpec((tr, D), lambda i: (jnp.minimum(i, n_lat_tiles - 1), 0)),
                   pl.BlockSpec((1, 8, D), lambda i: (sel(i), 0, 0))],
        out_shape=[jax.ShapeDtypeStruct((n_lat_tiles * tr, D), F32), jax.ShapeDtypeStruct((2, 8, D), F32)],
        compiler_params=_cparams(("arbitrary",)),
    )(dh1, x, ctx, dh2, w1, mod)


def _swiglu_act_bwd(da, gp, up, tr):
    R = gp.shape[0]

    def body(da_ref, g_ref, u_ref, dg_ref, du_ref):
        da, g = _f32(da_ref), _f32(g_ref)
        dg_ref[...] = (da * _f32(u_ref) * _dsilu(g)).astype(BF16)
        du_ref[...] = (da * _silu(g)).astype(BF16)

    return pl.pallas_call(
        body, name="swiglu_act_bwd", grid=(R // tr,), in_specs=[_row_spec(tr, D_FF)] * 3, out_specs=[_row_spec(tr, D_FF)] * 2,
        out_shape=[jax.ShapeDtypeStruct((R, D_FF), BF16)] * 2, compiler_params=_cparams(("arbitrary",)),
    )(da, gp, up)


def _merge(ya, yb, parts, tr):
    R = ya.shape[0]

    def body(ya_ref, yb_ref, ga_ref, gb_ref, o_ref):
        o_ref[...] = (_sigmoid(_f32(ga_ref)) * _f32(ya_ref) + _sigmoid(_f32(gb_ref)) * _f32(yb_ref)).astype(BF16)

    return pl.pallas_call(
        body, name="merge", grid=(R // tr,),
        in_specs=[_row_spec(tr, D), _row_spec(tr, D), _row_spec(tr, D, COL_GA // D), _row_spec(tr, D, COL_GB // D)],
        out_specs=_row_spec(tr, D), out_shape=jax.ShapeDtypeStruct((R, D), BF16), compiler_params=_cparams(("arbitrary",)),
    )(ya, yb, parts, parts)


def _dparts_out(tr, w, col, nd=1):
    blk = col // w
    return pl.BlockSpec((tr, w), (lambda i: (i, blk)) if nd == 1 else (lambda i, j: (i, blk + j)))


def _merge_bwd(dm, ya, yb, parts, dparts, tr):
    R = ya.shape[0]

    def body(dm_ref, ya_ref, yb_ref, ga_ref, gb_ref, _, dya_ref, dyb_ref, dg_ref):
        dm = _f32(dm_ref)
        sa, sb = _sigmoid(_f32(ga_ref)), _sigmoid(_f32(gb_ref))
        dya_ref[...] = (dm * sa).astype(BF16)
        dyb_ref[...] = (dm * sb).astype(BF16)
        dg_ref[:, 0:D] = (dm * _f32(ya_ref) * sa * (1.0 - sa)).astype(BF16)
        dg_ref[:, D:2 * D] = (dm * _f32(yb_ref) * sb * (1.0 - sb)).astype(BF16)

    return pl.pallas_call(
        body, name="merge_bwd", grid=(R // tr,),
        in_specs=[_row_spec(tr, D)] * 3 + [_row_spec(tr, D, COL_GA // D), _row_spec(tr, D, COL_GB // D), _ANY],
        out_specs=[_row_spec(tr, D), _row_spec(tr, D), _dparts_out(tr, 2 * D, COL_GA)],
        out_shape=[jax.ShapeDtypeStruct((R, D), BF16)] * 2 + [jax.ShapeDtypeStruct(dparts.shape, BF16)],
        input_output_aliases={5: 2}, compiler_params=_cparams(("arbitrary",)),
    )(dm, ya, yb, parts, parts, dparts)


def _gla_out(o2, parts, gw4, tr):
    R = parts.shape[0]

    def body(of_ref, ob_ref, r_ref, w_ref, out_ref):
        oa = _f32(of_ref) + _f32(ob_ref)
        sr = _silu(_f32(r_ref))
        for h in range(GLA_H):
            s = slice(h * GLA_DV, (h + 1) * GLA_DV)
            o = oa[:, s]
            out_ref[:, s] = (o * _rms(o) * w_ref[:, s] * sr[:, s]).astype(BF16)

    return pl.pallas_call(
        body, name="gla_out", grid=(R // tr,),
        in_specs=[_row_spec(tr, D), _row_spec(tr, D), _row_spec(tr, D, COL_R // D), _vec_spec(D)],
        out_specs=_row_spec(tr, D), out_shape=jax.ShapeDtypeStruct((R, D), BF16), compiler_params=_cparams(("arbitrary",)),
    )(o2[0], o2[1], parts, gw4)


def _gla_out_bwd(dout, o2, parts, gw4, dparts, tr):
    R = parts.shape[0]

    def body(d_ref, of_ref, ob_ref, r_ref, w_ref, _, do_ref, dr_ref, acc_ref):
        _zero_first(acc_ref)
        oa = _f32(of_ref) + _f32(ob_ref)
        r = _f32(r_ref)
        sr = _silu(r)
        dout = _f32(d_ref)
        for h in range(GLA_H):
            s = slice(h * GLA_DV, (h + 1) * GLA_DV)
            o = oa[:, s]
            rstd = _rms(o)
            n = o * rstd
            w = w_ref[:, s]
            dr_ref[:, s] = (dout[:, s] * n * w * _dsilu(r[:, s])).astype(BF16)
            dnw = dout[:, s] * sr[:, s]
            do_ref[:, s] = _rms_bwd(dnw * w, n, rstd).astype(ACT)
            acc_ref[0:1, s] += _colsum(dnw * n)

    return pl.pallas_call(
        body, name="gla_out_bwd", grid=(R // tr,),
        in_specs=[_row_spec(tr, D), _row_spec(tr, D), _row_spec(tr, D), _row_spec(tr, D, COL_R // D), _vec_spec(D), _ANY],
        out_specs=[_row_spec(tr, D), _dparts_out(tr, D, COL_R), _acc_spec(D)],
        out_shape=[jax.ShapeDtypeStruct((R, D), ACT), jax.ShapeDtypeStruct(dparts.shape, BF16), jax.ShapeDtypeStruct((8, D), F32)],
        input_output_aliases={5: 1}, compiler_params=_cparams(("arbitrary",)),
    )(dout, o2[0], o2[1], parts, gw4, dparts)


SSM_GW = SSM_INNER // SSM_G


def _ssd_out(y2, xbc, parts, dskip, nw, tr):
    R = parts.shape[0]

    def body(yf_ref, yb_ref, x_ref, z_ref, ds_ref, w_ref, out_ref):
        ob = (_f32(yf_ref) + _f32(yb_ref) + ds_ref[...] * _f32(x_ref)) * _silu(_f32(z_ref))
        for g in range(SSM_G):
            s = slice(g * SSM_GW, (g + 1) * SSM_GW)
            o = ob[:, s]
            out_ref[:, s] = (o * _rms(o) * w_ref[:, s]).astype(BF16)

    return pl.pallas_call(
        body, name="ssd_out", grid=(R // tr,),
        in_specs=[_row_spec(tr, SSM_INNER)] * 3 + [_row_spec(tr, SSM_INNER, COL_Z // SSM_INNER),
                                                   _vec_spec(SSM_INNER), _vec_spec(SSM_INNER)],
        out_specs=_row_spec(tr, SSM_INNER), out_shape=jax.ShapeDtypeStruct((R, SSM_INNER), BF16),
        compiler_params=_cparams(("arbitrary",)),
    )(y2[0], y2[1], xbc, parts, dskip, nw)


def _ssd_out_bwd(dout, y2, xbc, parts, dskip, nw, dparts, tr):
    R = parts.shape[0]

    def body(d_ref, yf_ref, yb_ref, x_ref, z_ref, ds_ref, w_ref, _, dy_ref, dz_ref, acc_ref):
        _zero_first(acc_ref)
        x, z = _f32(x_ref), _f32(z_ref)
        pre = _f32(yf_ref) + _f32(yb_ref) + ds_ref[...] * x
        sz = _silu(z)
        ob = pre * sz
        dout = _f32(d_ref)
        for g in range(SSM_G):
            s = slice(g * SSM_GW, (g + 1) * SSM_GW)
            o = ob[:, s]
            rstd = _rms(o)
            n = o * rstd
            dob = _rms_bwd(dout[:, s] * w_ref[:, s], n, rstd)
            dz_ref[:, s] = (dob * pre[:, s] * _dsilu(z[:, s])).astype(BF16)
            dy = dob * sz[:, s]
            dy_ref[:, s] = dy.astype(ACT)
            acc_ref[0:1, s] += _colsum(dout[:, s] * n)
            acc_ref[1:2, s] += _colsum(dy * x[:, s])

    return pl.pallas_call(
        body, name="ssd_out_bwd", grid=(R // tr,),
        in_specs=[_row_spec(tr, SSM_INNER)] * 4 + [_row_spec(tr, SSM_INNER, COL_Z // SSM_INNER),
                                                   _vec_spec(SSM_INNER), _vec_spec(SSM_INNER), _ANY],
        out_specs=[_row_spec(tr, SSM_INNER), _dparts_out(tr, SSM_INNER, COL_Z), _acc_spec(SSM_INNER)],
        out_shape=[jax.ShapeDtypeStruct((R, SSM_INNER), ACT), jax.ShapeDtypeStruct(dparts.shape, BF16),
                   jax.ShapeDtypeStruct((8, SSM_INNER), F32)],
        input_output_aliases={7: 1}, compiler_params=_cparams(("arbitrary",)),
    )(dout, y2[0], y2[1], xbc, parts, dskip, nw, dparts)


CONV_W = SSM_INNER + 2 * SSM_G * SSM_N
CONV_BLK = 1024


CONV_SHIFTS = (-2, -1, 1, 2)


def _conv_mask_table(tr):
    t = np.arange(tr)
    table = np.zeros((2, len(CONV_SHIFTS), tr, 128), np.float32)
    for kind, (pos, seg) in enumerate(((t % GRID_W, GRID_W), (t, tr))):
        for k, s in enumerate(CONV_SHIFTS):
            table[kind, k] = ((pos + s >= 0) & (pos + s < seg)).astype(np.float32)[:, None]
    return jnp.asarray(table)


def _shifted(u, s, mask_ref, tr):
    return u if s == 0 else pltpu.roll(u, (-s) % tr, 0) * mask_ref[0, CONV_SHIFTS.index(s)]


def _conv_mask_spec(tr, n_lat_tiles, row_axis):
    return pl.BlockSpec((1, len(CONV_SHIFTS), tr, 128),
                        lambda *ids: (jnp.where(ids[row_axis] >= n_lat_tiles, 1, 0), 0, 0, 0))


def _conv_fwd(parts, cw, cb, n_lat_tiles, tr):
    R = parts.shape[0]

    def body(u_ref, w_ref, b_ref, mask_ref, o_ref):
        def lanes(l, carry):
            sl = pl.ds(pl.multiple_of(l * 128, 128), 128)
            u, w = u_ref[:, sl].astype(F32), w_ref[:, sl]
            acc = jnp.zeros_like(u) + b_ref[:, sl]
            for j in range(SSM_CONV):
                acc = acc + _shifted(u, j - CONV_LEFT, mask_ref, tr) * w[j:j + 1, :]
            o_ref[:, sl] = _silu(acc).astype(ACT)
            return carry

        lax.fori_loop(0, CONV_BLK // 128, lanes, 0)

    return pl.pallas_call(
        body, name="conv_fwd", grid=(R // tr, CONV_W // CONV_BLK),
        in_specs=[pl.BlockSpec((tr, CONV_BLK), lambda i, j: (i, COL_XBC // CONV_BLK + j)),
                  pl.BlockSpec((SSM_CONV, CONV_BLK), lambda i, j: (0, j)), pl.BlockSpec((1, CONV_BLK), lambda i, j: (0, j)),
                  _conv_mask_spec(tr, n_lat_tiles, 0)],
        out_specs=pl.BlockSpec((tr, CONV_BLK), lambda i, j: (i, j)), out_shape=jax.ShapeDtypeStruct((R, CONV_W), ACT),
        compiler_params=_cparams(("arbitrary", "arbitrary")),
    )(parts, cw, cb, _conv_mask_table(tr))


def _conv_bwd(dx, db, dc, dy, dskip, parts, cw, cb, dparts, n_lat_tiles, tr):
    R = parts.shape[0]
    half = CONV_BLK // 2
    n_x = SSM_INNER // CONV_BLK

    def body(dxf_ref, dxb_ref, dy_ref, ds_ref, dbf_ref, dbb_ref, dcf_ref, dcb_ref, u_ref, w_ref, b_ref, _, mask_ref,
             du_ref, acc_ref, d_scr):
        @pl.when(pl.program_id(1) == 0)
        def _():
            acc_ref[...] = jnp.zeros_like(acc_ref)

        @pl.when(pl.program_id(0) < n_x)
        def _():
            d_scr[...] = dxf_ref[...] + dxb_ref[...] + _f32(dy_ref) * ds_ref[...]

        @pl.when(pl.program_id(0) >= n_x)
        def _():
            d_scr[:, 0:half] = dbf_ref[...] + dbb_ref[...]
            d_scr[:, half:] = dcf_ref[...] + dcb_ref[...]

        def lanes(l, carry):
            sl = pl.ds(pl.multiple_of(l * 128, 128), 128)
            u, w = u_ref[:, sl].astype(F32), w_ref[:, sl]
            pre = jnp.zeros_like(u) + b_ref[:, sl]
            taps = []
            for j in range(SSM_CONV):
                tap = _shifted(u, j - CONV_LEFT, mask_ref, tr)
                taps.append(tap)
                pre = pre + tap * w[j:j + 1, :]
            dpre = d_scr[:, sl] * _dsilu(pre)
            du = jnp.zeros_like(u)
            sums = []
            for j in range(SSM_CONV):
                sums.append(_colsum(dpre * taps[j]))
                du = du + _shifted(dpre, CONV_LEFT - j, mask_ref, tr) * w[j:j + 1, :]
            sums += [_colsum(dpre), jnp.zeros((8 - SSM_CONV - 1, 128), F32)]
            acc_ref[:, sl] += jnp.concatenate(sums, axis=0)
            du_ref[:, sl] = du.astype(BF16)
            return carry

        lax.fori_loop(0, CONV_BLK // 128, lanes, 0)

    return pl.pallas_call(
        body, name="conv_bwd", grid=(CONV_W // CONV_BLK, R // tr),
        in_specs=[pl.BlockSpec((tr, CONV_BLK), lambda j, i: (jnp.where(j < n_x, i, 0), jnp.minimum(j, n_x - 1)))] * 3
        + [pl.BlockSpec((1, CONV_BLK), lambda j, i: (0, jnp.minimum(j, n_x - 1)))]
        + [pl.BlockSpec((tr, half), lambda j, i: (jnp.where(j < n_x, 0, i), 0))] * 4
        + [pl.BlockSpec((tr, CONV_BLK), lambda j, i: (i, COL_XBC // CONV_BLK + j)),
           pl.BlockSpec((SSM_CONV, CONV_BLK), lambda j, i: (0, j)), pl.BlockSpec((1, CONV_BLK), lambda j, i: (0, j)), _ANY,
           _conv_mask_spec(tr, n_lat_tiles, 1)],
        out_specs=[pl.BlockSpec((tr, CONV_BLK), lambda j, i: (i, COL_XBC // CONV_BLK + j)),
                   pl.BlockSpec((8, CONV_BLK), lambda j, i: (0, j))],
        out_shape=[jax.ShapeDtypeStruct(dparts.shape, BF16), jax.ShapeDtypeStruct((8, CONV_W), F32)],
        scratch_shapes=[pltpu.VMEM((tr, CONV_BLK), F32)],
        input_output_aliases={11: 0}, compiler_params=_cparams(("arbitrary", "arbitrary")),
    )(*dx, dy, dskip, *db, *dc, parts, cw, cb, dparts, _conv_mask_table(tr))


def _chunk_row_block(d, i, n_lat, n_ctx):
    fwd = jnp.where(i < n_ctx, n_lat + i, i - n_ctx)
    rev = n_lat + n_ctx - 1 - i
    if isinstance(d, int):
        return rev if d else fwd
    return jnp.where(d == 0, fwd, rev)


def _tri(n, d, transpose=False):
    row = lax.broadcasted_iota(jnp.int32, (n, n), 0)
    col = lax.broadcasted_iota(jnp.int32, (n, n), 1)
    diff = (col - row) if transpose else (row - col)
    return diff * (1 - 2 * d) >= 0


def _gla_gates(sm, uhi, ulo, bias, d):
    pre = _nn3(sm, uhi, ulo) + bias
    g = _log_sigmoid(pre) * (1.0 / GLA_TAU)
    mask = _tri(GLA_C, d)
    b = _x_nn(mask.astype(BF16), g)
    b_tot = _colsum(g)
    b_ref = b[GLA_C // 2:GLA_C // 2 + 1, :]
    e_q = jnp.exp(jnp.minimum(b - b_ref, EXP_CLAMP))
    e_k = jnp.exp(jnp.minimum(b_ref - b, EXP_CLAMP))
    return pre, mask, b_tot, e_q, e_k, jnp.exp(b), jnp.exp(b_tot - b)


GLA_QK = GLA_H * GLA_DK
GLA_V = GLA_H * GLA_DV


def _gla_specs(n_lat, n_ctx, step_of):
    rbs = [lambda i, d=d: _chunk_row_block(d, step_of(i), n_lat, n_ctx) for d in range(2)]
    specs = []
    for rb in rbs:
        specs += [pl.BlockSpec((GLA_C, GLA_QK), lambda i, rb=rb: (rb(i), COL_Q // GLA_QK)),
                  pl.BlockSpec((GLA_C, GLA_QK), lambda i, rb=rb: (rb(i), COL_K // GLA_QK)),
                  pl.BlockSpec((GLA_C, GLA_V), lambda i, rb=rb: (rb(i), COL_V // GLA_V)),
                  pl.BlockSpec((GLA_C, 128), lambda i, rb=rb: (rb(i), 0))]
    specs += [pl.BlockSpec((2, 128, GLA_QK), lambda i: (0, 0, 0)), pl.BlockSpec((2, 128, GLA_QK), lambda i: (0, 0, 0)),
              pl.BlockSpec((2, 1, GLA_QK), lambda i: (0, 0, 0))]
    return specs, rbs


ACT = BF16


def _f32(ref_or_value):
    return ref_or_value[...].astype(F32)


def _gla_fwd(parts, sm, uhi, ulo, bias, n_lat, n_ctx):
    R = parts.shape[0]
    n_steps = n_lat + n_ctx
    scale = GLA_DK ** -0.5

    def body(*refs):
        ins, (uhi_ref, ulo_ref, bias_ref), o_refs, hist_ref, st = refs[:8], refs[8:11], refs[11:13], refs[13], refs[14]

        @pl.when(pl.program_id(0) == 0)
        def _():
            st[...] = jnp.zeros_like(st)

        for d in range(2):
            q_ref, k_ref, v_ref, sm_ref = ins[4 * d:4 * d + 4]
            _, mask, b_tot, e_q, e_k, e_in, e_out = _gla_gates(sm_ref[...], uhi_ref[d], ulo_ref[d], bias_ref[d], d)
            q, k, v = _f32(q_ref) * scale, _f32(k_ref), _bf(v_ref[...])
            qb, kb, q_in, k_out, decay = _bf(q * e_q), _bf(k * e_k), _bf(q * e_in), _bf(k * e_out), jnp.exp(b_tot)
            for h in range(GLA_H):
                sk, sv = slice(h * GLA_DK, (h + 1) * GLA_DK), slice(h * GLA_DV, (h + 1) * GLA_DV)
                att = jnp.where(mask, _nt(qb[:, sk], kb[:, sk]), 0.0)
                s_in = st[d, h]
                hist_ref[d, 0, h] = s_in
                o_refs[d][:, sv] = (_nn(_bf(att), v[:, sv]) + _nt(q_in[:, sk], _bf(s_in))).astype(ACT)
                st[d, h] = decay[:, sk] * s_in + _tn(v[:, sv], k_out[:, sk])

    in_specs, rbs = _gla_specs(n_lat, n_ctx, lambda i: i)
    return pl.pallas_call(
        body, name="gla_fwd", grid=(n_steps,), in_specs=in_specs,
        out_specs=[pl.BlockSpec((GLA_C, GLA_V), lambda i, rb=rb: (rb(i), 0)) for rb in rbs]
        + [pl.BlockSpec((2, 1, GLA_H, GLA_DV, GLA_DK), lambda i: (0, i, 0, 0, 0))],
        out_shape=[jax.ShapeDtypeStruct((R, GLA_V), ACT)] * 2 + [jax.ShapeDtypeStruct((2, n_steps, GLA_H, GLA_DV, GLA_DK), F32)],
        scratch_shapes=[pltpu.VMEM((2, GLA_H, GLA_DV, GLA_DK), F32)],
        compiler_params=_cparams(("arbitrary",)),
    )(*([parts, parts, parts, sm] * 2), uhi, ulo, bias)


def _gla_bwd(do, parts, sm, uhi, ulo, bias, hist, n_lat, n_ctx):
    R = parts.shape[0]
    n_steps = n_lat + n_ctx
    scale = GLA_DK ** -0.5
    step_of = lambda j: n_steps - 1 - j

    def body(*refs):
        ins, (uhi_ref, ulo_ref, bias_ref), do_refs, hist_ref = refs[:8], refs[8:11], refs[11:13], refs[13]
        outs, dst = refs[14:22], refs[22]

        @pl.when(pl.program_id(0) == 0)
        def _():
            dst[...] = jnp.zeros_like(dst)

        for d in range(2):
            q_ref, k_ref, v_ref, sm_ref = ins[4 * d:4 * d + 4]
            dq_ref, dk_ref, dv_ref, dp_ref = outs[4 * d:4 * d + 4]
            pre, mask, b_tot, e_q, e_k, e_in, e_out = _gla_gates(sm_ref[...], uhi_ref[d], ulo_ref[d], bias_ref[d], d)
            q, k, v = _f32(q_ref) * scale, _f32(k_ref), _bf(v_ref[...])
            dout = _bf(do_refs[d][...])
            k_out_f = k * e_out
            qb, kb, q_in, k_out, decay = _bf(q * e_q), _bf(k * e_k), _bf(q * e_in), _bf(k_out_f), jnp.exp(b_tot)
            dqs, dks, dk_outs, dss = [], [], [], []
            for h in range(GLA_H):
                sk, sv = slice(h * GLA_DK, (h + 1) * GLA_DK), slice(h * GLA_DV, (h + 1) * GLA_DV)
                s_in, ds = hist_ref[d, 0, h], dst[d, h]
                att = jnp.where(mask, _nt(qb[:, sk], kb[:, sk]), 0.0)
                datt = _bf(jnp.where(mask, _nt(dout[:, sv], v[:, sv]), 0.0))
                dv_ref[:, sv] = _tn(_bf(att), dout[:, sv]) + _nt(k_out[:, sk], _bf(ds))
                dqs.append(_nn(datt, kb[:, sk]) * e_q[:, sk] + _nn(dout[:, sv], _bf(s_in)) * e_in[:, sk])
                dk_o = _nn(v[:, sv], _bf(ds))
                dk_outs.append(dk_o)
                dks.append(_tn(datt, qb[:, sk]) * e_k[:, sk])
                dss.append(_colsum(ds * s_in))
                dst[d, h] = decay[:, sk] * ds + _tn(dout[:, sv], q_in[:, sk])
            dq, dk_out = jnp.concatenate(dqs, axis=1), jnp.concatenate(dk_outs, axis=1)
            dk = jnp.concatenate(dks, axis=1) + dk_out * e_out
            dq_ref[...] = dq * scale
            dk_ref[...] = dk
            db_tot = _colsum(dk_out * k_out_f) + decay * jnp.concatenate(dss, axis=1)
            dg = _x_nn(_tri(GLA_C, d, transpose=True).astype(BF16), dq * q - dk * k) + db_tot
            dp_ref[...] = dg * (1.0 / GLA_TAU) * _sigmoid(-pre)

    in_specs, rbs = _gla_specs(n_lat, n_ctx, step_of)
    in_specs += [pl.BlockSpec((GLA_C, GLA_V), lambda j, rb=rb: (rb(j), 0)) for rb in rbs]
    in_specs += [pl.BlockSpec((2, 1, GLA_H, GLA_DV, GLA_DK), lambda j: (0, step_of(j), 0, 0, 0))]
    out_specs, out_shape = [], []
    for rb in rbs:
        for w in (GLA_QK, GLA_QK, GLA_V, GLA_QK):
            out_specs.append(pl.BlockSpec((GLA_C, w), lambda j, rb=rb: (rb(j), 0)))
            out_shape.append(jax.ShapeDtypeStruct((R, w), F32))
    outs = pl.pallas_call(
        body, name="gla_bwd", grid=(n_steps,), in_specs=in_specs, out_specs=out_specs, out_shape=out_shape,
        scratch_shapes=[pltpu.VMEM((2, GLA_H, GLA_DV, GLA_DK), F32)],
        compiler_params=_cparams(("arbitrary",)),
    )(*([parts, parts, parts, sm] * 2), uhi, ulo, bias, do, do, hist)
    return [(outs[k], outs[4 + k]) for k in range(4)]


def _ssd_consts(dt_bias, a_log):
    sel = np.zeros((2, SSM_G, 128, 128), np.float32)
    for d, base in enumerate((SM_DTF, SM_DTB)):
        for g in range(SSM_G):
            for e in range(SSM_HPG):
                sel[d, g, base + SSM_HPG * g + e, e] = 1.0
    e512 = np.zeros((128, SSM_GW), np.float32)
    for e in range(SSM_HPG):
        e512[e, SSM_P * e:SSM_P * (e + 1)] = 1.0
    a_neg = -jnp.exp(a_log)
    pad = lambda v: jnp.pad(v.reshape(2, SSM_G, 1, SSM_HPG), ((0, 0), (0, 0), (0, 0), (0, 128 - SSM_HPG)))
    return dict(
        sel=jnp.asarray(sel, BF16), sel_t=jnp.asarray(sel.transpose(0, 1, 3, 2), BF16), e512_t=jnp.asarray(e512.T, BF16),
        dtb=pad(dt_bias), a=pad(a_neg), a512=jnp.repeat(a_neg, SSM_P, axis=1).reshape(2, SSM_G, 1, SSM_GW))


def _head_columns(x8):
    return [jnp.broadcast_to(x8[:, e:e + 1], (x8.shape[0], 128)) for e in range(SSM_HPG)]


def _head_layout(cols):
    low = lax.broadcasted_iota(jnp.int32, (1, 128), 1) < SSM_P
    return jnp.concatenate([jnp.where(low, cols[2 * j], cols[2 * j + 1]) for j in range(SSM_HPG // 2)], axis=1)


def _ssd_common(sm, sel, dtb, a_neg, a512, d):
    dtr8 = _nn_x(sm, sel) + dtb
    dt8 = _softplus(dtr8)
    a8 = a_neg * dt8
    mask = _tri(SSM_C, d)
    mask_t = _tri(SSM_C, d, transpose=True).astype(BF16)
    cum8 = _x_nn(mask.astype(BF16), a8)
    a_hi, a_lo = _split(a8)
    cum_t = _tn(a_hi, mask_t) + _tn(a_lo, mask_t)
    cum_cols = _head_columns(cum8)
    dt_exp = _head_layout(_head_columns(dt8))
    a_exp = a512 * dt_exp
    return dict(dtr8=dtr8, a8=a8, mask=mask, mask_t=mask_t, cum_t=cum_t, dt_exp=dt_exp, a_exp=a_exp,
                cum_exp=_head_layout(cum_cols), cum_cols=cum_cols, tot_exp=_colsum(a_exp))


def _ssd_decay(cm, e):
    diff = cm["cum_cols"][e] - cm["cum_t"][e:e + 1, :]
    return jnp.where(cm["mask"], jnp.exp(jnp.minimum(diff, 0.0)), 0.0)


SSM_GPS = 4


def _ssd_specs(n_lat, n_ctx, step_of):
    rbs = [lambda i, d=d: _chunk_row_block(d, step_of(i), n_lat, n_ctx) for d in range(2)]
    xw, nw = SSM_GPS * SSM_GW, SSM_GPS * SSM_N
    specs = []
    for rb in rbs:
        specs += [pl.BlockSpec((SSM_C, xw), lambda g, i, rb=rb: (rb(i), g)),
                  pl.BlockSpec((SSM_C, nw), lambda g, i, rb=rb: (rb(i), SSM_INNER // nw + g)),
                  pl.BlockSpec((SSM_C, nw), lambda g, i, rb=rb: (rb(i), (SSM_INNER + SSM_G * SSM_N) // nw + g)),
                  pl.BlockSpec((SSM_C, 128), lambda g, i, rb=rb: (rb(i), 0))]
    specs += [pl.BlockSpec((2, SSM_GPS, 128, 128), lambda g, i: (0, g, 0, 0)),
              pl.BlockSpec((2, SSM_GPS, 1, 128), lambda g, i: (0, g, 0, 0)),
              pl.BlockSpec((2, SSM_GPS, 1, 128), lambda g, i: (0, g, 0, 0)),
              pl.BlockSpec((2, SSM_GPS, 1, SSM_GW), lambda g, i: (0, g, 0, 0))]
    return specs, rbs


def _ssd_fwd(xbc, sm, k, n_lat, n_ctx, comm):
    R = xbc.shape[0]
    n_steps = n_lat + n_ctx

    def body(*refs):
        ins, (sel_ref, dtb_ref, a_ref, a512_ref), y_refs, hist_ref, st = refs[:8], refs[8:12], refs[12:14], refs[14], refs[15]

        @pl.when(pl.program_id(1) == 0)
        def _():
            st[...] = jnp.zeros_like(st)

        for d in range(2):
            x_ref, b_ref, c_ref, sm_ref = ins[4 * d:4 * d + 4]
            sm = sm_ref[...]
            for gg in range(SSM_GPS):
                sx, sn = slice(gg * SSM_GW, (gg + 1) * SSM_GW), slice(gg * SSM_N, (gg + 1) * SSM_N)
                cm = _ssd_common(sm, sel_ref[d, gg], dtb_ref[d, gg], a_ref[d, gg], a512_ref[d, gg], d)
                bm, cmat = _bf(b_ref[:, sn]), _bf(c_ref[:, sn])
                xdt = x_ref[:, sx].astype(F32) * cm["dt_exp"]
                cb = _nt(cmat, bm)
                ys = [_nn(_bf(cb * _ssd_decay(cm, e)), _bf(xdt[:, SSM_P * e:SSM_P * (e + 1)])) for e in range(SSM_HPG)]
                s_in = st[d, gg]
                hist_ref[d, 0, gg] = s_in
                y = jnp.concatenate(ys, axis=1) + jnp.exp(cm["cum_exp"]) * _nn(cmat, _bf(s_in))
                y_refs[d][:, sx] = y.astype(ACT)
                st[d, gg] = jnp.exp(cm["tot_exp"]) * s_in + _tn(bm, _bf(xdt * jnp.exp(cm["tot_exp"] - cm["cum_exp"])))

    in_specs, rbs = _ssd_specs(n_lat, n_ctx, lambda i: i)
    out_specs = [pl.BlockSpec((SSM_C, SSM_GPS * SSM_GW), lambda g, i, rb=rb: (rb(i), g)) for rb in rbs]
    out_specs += [pl.BlockSpec((2, 1, SSM_GPS, SSM_N, SSM_GW), lambda g, i: (0, i, g, 0, 0))]
    out_shape = [jax.ShapeDtypeStruct((R, SSM_INNER), ACT)] * 2 + [jax.ShapeDtypeStruct((2, n_steps, SSM_G, SSM_N, SSM_GW), F32)]
    args = [xbc, xbc, xbc, sm] * 2 + [k["sel"], k["dtb"], k["a"], k["a512"]]
    n_host_out = len(out_shape)
    outs = pl.pallas_call(
        _hosted(body, len(args), n_host_out, 1, comm, *_ssd_comm_steps(n_steps)), name="ssd_fwd",
        grid=(SSM_G // SSM_GPS, n_steps), in_specs=in_specs + [_ANY] * len(comm.arrays),
        out_specs=out_specs + [_ANY] * len(comm.out_shape), out_shape=out_shape + comm.out_shape,
        scratch_shapes=[pltpu.VMEM((2, SSM_GPS, SSM_N, SSM_GW), F32)] + comm.scratch,
        compiler_params=_cparams(("arbitrary", "arbitrary")),
    )(*args, *comm.arrays)
    return outs[:n_host_out], outs[n_host_out:]


def _ssd_comm_steps(n_steps):
    n_g = SSM_G // SSM_GPS
    at = lambda g, i: (pl.program_id(0) == g) & (pl.program_id(1) == i)
    half = (n_g // 2, n_steps // 2 if n_g % 2 else 0)
    return (lambda: at(0, 0)), (lambda: at(*half)), (lambda: at(n_g - 1, n_steps - 1))


def _ssd_bwd(dy, xbc, sm, k, hist, n_lat, n_ctx, comm):
    R = xbc.shape[0]
    n_steps = n_lat + n_ctx
    step_of = lambda j: n_steps - 1 - j

    def one(d, gg, x_ref, b_ref, c_ref, sm_ref, sel_ref, dtb_ref, a_ref, a512_ref, selt_ref, e512t_ref, dy_ref,
            hist_ref, dx_ref, db_ref, dc_ref, dsm_ref, acc_ref, dst):
        sx, sn = slice(gg * SSM_GW, (gg + 1) * SSM_GW), slice(gg * SSM_N, (gg + 1) * SSM_N)
        a_neg, e512_t = a_ref[d, gg], e512t_ref[...]
        cm = _ssd_common(sm_ref[...], sel_ref[d, gg], dtb_ref[d, gg], a_neg, a512_ref[d, gg], d)
        x, dyv = x_ref[:, sx].astype(F32), dy_ref[:, sx].astype(F32)
        bm, cmat = _bf(b_ref[:, sn]), _bf(c_ref[:, sn])
        xdt = x * cm["dt_exp"]
        cb = _nt(cmat, bm)
        s_in, ds = hist_ref[d, 0, gg], dst[d, gg]
        w = jnp.exp(cm["tot_exp"] - cm["cum_exp"])
        z = _nn(bm, _bf(ds))
        decay_in = jnp.exp(cm["cum_exp"])
        gy = _bf(dyv * decay_in)
        dcb = jnp.zeros((SSM_C, SSM_C), F32)
        dxs, crossing = [], []
        row = lax.broadcasted_iota(jnp.int32, (SSM_C, SSM_C), 0)
        col = lax.broadcasted_iota(jnp.int32, (SSM_C, SSM_C), 1)
        eye = (row == col).astype(BF16)
        before = (cm["mask_t"] - eye)
        for e in range(SSM_HPG):
            s = slice(SSM_P * e, SSM_P * (e + 1))
            lm = _ssd_decay(cm, e)
            dy_e = _bf(dyv[:, s])
            m_e = cb * lm
            dm_e = _nt(dy_e, _bf(xdt[:, s]))
            dcb = dcb + dm_e * lm
            dxs.append(_tn(_bf(m_e), dy_e))
            crossing.append(_bf(dm_e * m_e))
        through = _nn(jnp.concatenate(crossing, axis=0), before)
        crossing = [_colsum(jnp.where(cm["mask"], through[e * SSM_C:(e + 1) * SSM_C], 0.0)) for e in range(SSM_HPG)]
        da_rows = jnp.concatenate(crossing + [jnp.zeros((128 - SSM_HPG, SSM_C), F32)], axis=0)
        r_hi, r_lo = _split(da_rows)
        da8_intra = _tn(r_hi, eye) + _tn(r_lo, eye)
        dx_state = w * z
        dxdt = jnp.concatenate(dxs, axis=1) + dx_state
        dcb = _bf(dcb)
        c_s = _nn(cmat, _bf(s_in))
        dc_ref[:, sn] = _nn(dcb, bm) + _nt(gy, _bf(s_in))
        db_ref[:, sn] = _tn(dcb, cmat) + _nt(_bf(w * xdt), _bf(ds))
        dst[d, gg] = jnp.exp(cm["tot_exp"]) * ds + _tn(cmat, gy)
        state_path = xdt * dx_state
        per_token = _nn_x(jnp.concatenate([dyv * decay_in * c_s - state_path, dxdt * x], axis=0), e512_t)
        totals = jnp.concatenate([_colsum(state_path), _colsum(ds * s_in), jnp.zeros((6, SSM_GW), F32)], axis=0)
        totals = _nn_x(totals, e512_t)
        tot8 = _colsum(cm["a8"])
        dtot8 = totals[0:1] + jnp.exp(tot8) * totals[1:2]
        da8 = da8_intra + _x_nn(cm["mask_t"], per_token[:SSM_C]) + dtot8
        ddt8 = da8 * a_neg + per_token[SSM_C:]
        dsm_ref[gg] = _nn_x(ddt8 * _sigmoid(cm["dtr8"]), selt_ref[d, gg])
        dx_ref[:, sx] = dxdt * cm["dt_exp"]
        acc_ref[d, gg, 0:1, :] += _colsum(da8 * cm["a8"])

    def body(*refs):
        ins, consts, (selt_ref, e512t_ref), dy_refs, hist_ref = refs[:8], refs[8:12], refs[12:14], refs[14:16], refs[16]
        outs, acc_ref, dst = refs[17:25], refs[25], refs[26]

        @pl.when(pl.program_id(1) == 0)
        def _():
            dst[...] = jnp.zeros_like(dst)
            acc_ref[...] = jnp.zeros_like(acc_ref)

        for d in range(2):
            for gg in range(SSM_GPS):
                one(d, gg, *ins[4 * d:4 * d + 4], *consts, selt_ref, e512t_ref, dy_refs[d], hist_ref,
                    *outs[4 * d:4 * d + 4], acc_ref, dst)

    xw, nw = SSM_GPS * SSM_GW, SSM_GPS * SSM_N
    in_specs, rbs = _ssd_specs(n_lat, n_ctx, step_of)
    in_specs += [pl.BlockSpec((2, SSM_GPS, 128, 128), lambda g, j: (0, g, 0, 0)), pl.BlockSpec((SSM_GW, 128), lambda g, j: (0, 0))]
    in_specs += [pl.BlockSpec((SSM_C, xw), lambda g, j, rb=rb: (rb(j), g)) for rb in rbs]
    in_specs += [pl.BlockSpec((2, 1, SSM_GPS, SSM_N, SSM_GW), lambda g, j: (0, step_of(j), g, 0, 0))]
    out_specs, out_shape = [], []
    for rb in rbs:
        out_specs += [pl.BlockSpec((SSM_C, xw), lambda g, j, rb=rb: (rb(j), g)),
                      pl.BlockSpec((SSM_C, nw), lambda g, j, rb=rb: (rb(j), g)),
                      pl.BlockSpec((SSM_C, nw), lambda g, j, rb=rb: (rb(j), g)),
                      pl.BlockSpec((SSM_GPS, SSM_C, 128), lambda g, j, rb=rb: (g, rb(j), 0))]
        out_shape += [jax.ShapeDtypeStruct((R, SSM_INNER), F32), jax.ShapeDtypeStruct((R, SSM_G * SSM_N), F32),
                      jax.ShapeDtypeStruct((R, SSM_G * SSM_N), F32), jax.ShapeDtypeStruct((SSM_G, R, 128), F32)]
    out_specs.append(pl.BlockSpec((2, SSM_GPS, 8, 128), lambda g, j: (0, g, 0, 0)))
    out_shape.append(jax.ShapeDtypeStruct((2, SSM_G, 8, 128), F32))
    args = [xbc, xbc, xbc, sm] * 2 + [k["sel"], k["dtb"], k["a"], k["a512"], k["sel_t"], k["e512_t"], dy, dy, hist]
    n_host_out = len(out_shape)
    outs = pl.pallas_call(
        _hosted(body, len(args), n_host_out, 1, comm, *_ssd_comm_steps(n_steps)), name="ssd_bwd",
        grid=(SSM_G // SSM_GPS, n_steps), in_specs=in_specs + [_ANY] * len(comm.arrays),
        out_specs=out_specs + [_ANY] * len(comm.out_shape), out_shape=out_shape + comm.out_shape,
        scratch_shapes=[pltpu.VMEM((2, SSM_GPS, SSM_N, SSM_GW), F32)] + comm.scratch,
        compiler_params=_cparams(("arbitrary", "arbitrary")),
    )(*args, *comm.arrays)
    return [(outs[n], outs[4 + n]) for n in range(4)] + [outs[8]], outs[n_host_out:]


def _gla_assemble(dq, dk, dv, dparts, tr):
    R = dq[0].shape[0]
    qk = GLA_H * GLA_DK

    def body(dqf_ref, dqb_ref, dkf_ref, dkb_ref, dvf_ref, dvb_ref, _, o_ref):
        o_ref[:, 0:qk] = (dqf_ref[...] + dqb_ref[...]).astype(BF16)
        o_ref[:, qk:2 * qk] = (dkf_ref[...] + dkb_ref[...]).astype(BF16)
        o_ref[:, 2 * qk:] = (dvf_ref[...] + dvb_ref[...]).astype(BF16)

    return pl.pallas_call(
        body, name="gla_assemble", grid=(R // tr,), in_specs=[_row_spec(tr, qk)] * 4 + [_row_spec(tr, D)] * 2 + [_ANY],
        out_specs=_dparts_out(tr, 2 * D, COL_Q), out_shape=jax.ShapeDtypeStruct(dparts.shape, BF16),
        input_output_aliases={6: 0}, compiler_params=_cparams(("arbitrary",)),
    )(*dq, *dk, *dv, dparts)


def _small_assemble(dp, dsm, sm, ut_hi, ut_lo, dparts, tr):
    R = sm.shape[0]
    qk = GLA_H * GLA_DK

    def body(dpf_ref, dpb_ref, dsmf_ref, dsmb_ref, sm_ref, uth_ref, utl_ref, _, o_ref, dup_ref, acc_ref, acc2_ref):
        @pl.when(pl.program_id(0) == 0)
        def _():
            dup_ref[...] = jnp.zeros_like(dup_ref)
            acc_ref[...] = jnp.zeros_like(acc_ref)
            acc2_ref[...] = jnp.zeros_like(acc2_ref)

        ssd = dsmf_ref[0] + dsmb_ref[0]
        for g in range(1, SSM_G):
            ssd = ssd + (dsmf_ref[g] + dsmb_ref[g])
        acc2_ref[0:1, :] += _colsum(ssd)
        sm_hi, sm_lo = _split(sm_ref[...])
        out = ssd
        for d, dp_ref in enumerate((dpf_ref, dpb_ref)):
            dpd = dp_ref[...]
            out = out + _nn3(dpd, uth_ref[d], utl_ref[d])
            p_hi, p_lo = _split(dpd)
            dup_ref[d] += _tn(sm_hi, p_hi) + _tn(sm_lo, p_hi) + _tn(sm_hi, p_lo)
            acc_ref[d:d + 1, :] += _colsum(dpd)
        o_ref[...] = out.astype(BF16)

    return pl.pallas_call(
        body, name="small_assemble", grid=(R // tr,),
        in_specs=[_row_spec(tr, qk)] * 2 + [pl.BlockSpec((SSM_G, tr, 128), lambda i: (0, i, 0))] * 2
        + [_row_spec(tr, 128), pl.BlockSpec((2, qk, 128), lambda i: (0, 0, 0)),
           pl.BlockSpec((2, qk, 128), lambda i: (0, 0, 0)), _ANY],
        out_specs=[_dparts_out(tr, 128, COL_SM), pl.BlockSpec((2, 128, qk), lambda i: (0, 0, 0)), _acc_spec(qk), _acc_spec(128)],
        out_shape=[jax.ShapeDtypeStruct(dparts.shape, BF16), jax.ShapeDtypeStruct((2, 128, qk), F32),
                   jax.ShapeDtypeStruct((8, qk), F32), jax.ShapeDtypeStruct((8, 128), F32)],
        input_output_aliases={7: 0}, compiler_params=_cparams(("arbitrary",)),
    )(*dp, *dsm, sm, ut_hi, ut_lo, dparts)


ADA_ROWS = 16
ADA_TILE = 512


def _dot3_f32(a, b, ca, cb):
    a_hi, a_lo = _split(a)
    b_hi, b_lo = _split(b)
    return _dg(a_hi, b_hi, ca, cb) + _dg(a_lo, b_hi, ca, cb) + _dg(a_hi, b_lo, ca, cb)


def _ada_fwd(cvec, w, b):
    cols = w.shape[1]

    def body(c_ref, w_ref, b_ref, o_ref):
        o_ref[...] = _dot3_f32(_silu(c_ref[...]), w_ref[...], 1, 0) + b_ref[...]

    return pl.pallas_call(
        body, name="ada_fwd", grid=(cols // ADA_TILE,),
        in_specs=[pl.BlockSpec((ADA_ROWS, D), lambda j: (0, 0)), pl.BlockSpec((D, ADA_TILE), lambda j: (0, j)),
                  pl.BlockSpec((1, ADA_TILE), lambda j: (0, j))],
        out_specs=pl.BlockSpec((ADA_ROWS, ADA_TILE), lambda j: (0, j)), out_shape=jax.ShapeDtypeStruct((ADA_ROWS, cols), F32),
        compiler_params=_cparams(("arbitrary",)),
    )(cvec, w, b)


def _adam(w, g, m, v):
    m2 = ADAM_B1 * m + (1.0 - ADAM_B1) * g
    v2 = ADAM_B2 * v + (1.0 - ADAM_B2) * (g * g)
    m_hat = m2 / (1.0 - ADAM_B1 ** ADAM_STEP)
    v_hat = v2 / (1.0 - ADAM_B2 ** ADAM_STEP)
    return -ADAM_LR * (m_hat / (jnp.sqrt(v_hat) + ADAM_EPS) + ADAM_WD * w), m2, v2


def _wada_bwd_adam(cvec, dada, w, m, v):
    rows, cols = w.shape
    tr = _tile(rows, 256, 128)

    def body(c_ref, d_ref, w_ref, m_ref, v_ref, g_ref, dl_ref, m2_ref, v2_ref, p_ref):
        wv = w_ref[...]
        g = _dot3_f32(_silu(c_ref[...]), d_ref[...], 0, 0)
        g_ref[...] = g
        dl_ref[...], m2_ref[...], v2_ref[...] = _adam(wv, g, m_ref[...], v_ref[...])
        p_ref[...] = _dot3_f32(d_ref[...], wv, 1, 1)

    blk = pl.BlockSpec((tr, cols), lambda i: (i, 0))
    return pl.pallas_call(
        body, name="wada_bwd_adam", grid=(rows // tr,),
        in_specs=[pl.BlockSpec((ADA_ROWS, tr), lambda i: (0, i)), pl.BlockSpec((ADA_ROWS, cols), lambda i: (0, 0)), blk, blk, blk],
        out_specs=[blk, blk, blk, blk, pl.BlockSpec((ADA_ROWS, tr), lambda i: (0, i))],
        out_shape=[jax.ShapeDtypeStruct((rows, cols), F32)] * 4 + [jax.ShapeDtypeStruct((ADA_ROWS, rows), F32)],
        compiler_params=_cparams(("arbitrary",)),
    )(cvec, dada, w, m, v)


def _reduce_adam(parts8, w, m, v, name):
    rows, cols = w.shape
    tr = _tile(rows, 64, 16)

    def body(p_ref, w_ref, m_ref, v_ref, g_ref, dl_ref, m2_ref, v2_ref):
        g = p_ref[0].astype(F32) + p_ref[N_CHIPS].astype(F32)
        for j in range(1, N_CHIPS):
            g = g + (p_ref[j].astype(F32) + p_ref[N_CHIPS + j].astype(F32))
        g_ref[...] = g
        dl_ref[...], m2_ref[...], v2_ref[...] = _adam(w_ref[...], g, m_ref[...], v_ref[...])

    blk = pl.BlockSpec((tr, cols), lambda i: (i, 0))
    return pl.pallas_call(
        body, name=name, grid=(rows // tr,), in_specs=[pl.BlockSpec((N_DEV, tr, cols), lambda i: (0, i, 0)), blk, blk, blk],
        out_specs=[blk] * 4, out_shape=[jax.ShapeDtypeStruct((rows, cols), F32)] * 4, compiler_params=_cparams(("arbitrary",)),
    )(parts8, w, m, v)


SMALL_W = 1024


def _sum8(g8):
    rows = g8.shape[1]

    def body(g_ref, o_ref):
        s = g_ref[0]
        for j in range(1, N_DEV):
            s = s + g_ref[j]
        o_ref[...] = s

    return pl.pallas_call(
        body, name="sum8", out_shape=jax.ShapeDtypeStruct((rows, SMALL_W), F32),
        in_specs=[pl.BlockSpec(memory_space=pltpu.VMEM)], out_specs=pl.BlockSpec(memory_space=pltpu.VMEM),
        compiler_params=pltpu.CompilerParams(vmem_limit_bytes=VMEM_LIMIT),
    )(g8)


def _cctx_grad(p8, c_ctx):
    def body(p_ref, c_ref, o_ref):
        s = p_ref[0]
        for chip in range(1, N_CHIPS):
            s = s + p_ref[2 * chip]
        o_ref[...] = s * _dsilu(c_ref[...])

    return pl.pallas_call(
        body, name="cctx_grad", out_shape=jax.ShapeDtypeStruct((1, D), F32),
        in_specs=[pl.BlockSpec(memory_space=pltpu.VMEM)] * 2, out_specs=pl.BlockSpec(memory_space=pltpu.VMEM),
    )(p8, c_ctx)


def _adam_small(w, g, m, v):
    def body(w_ref, g_ref, m_ref, v_ref, dl_ref, m2_ref, v2_ref):
        dl_ref[...], m2_ref[...], v2_ref[...] = _adam(w_ref[...], g_ref[...], m_ref[...], v_ref[...])

    vm = pl.BlockSpec(memory_space=pltpu.VMEM)
    return pl.pallas_call(
        body, name="adam_small", out_shape=[jax.ShapeDtypeStruct(w.shape, F32)] * 3, in_specs=[vm] * 4, out_specs=[vm] * 3,
        compiler_params=pltpu.CompilerParams(vmem_limit_bytes=VMEM_LIMIT),
    )(w, g, m, v)


def _pack(vecs, width=SMALL_W, row_mult=8):
    flat = jnp.concatenate([v.reshape(-1).astype(F32) for v in vecs])
    n = flat.shape[0]
    rows = -(-n // (width * row_mult)) * row_mult
    return jnp.pad(flat, (0, rows * width - n)).reshape(rows, width)


def _unpack(packed, shapes):
    flat = packed.reshape(-1)
    out, off = [], 0
    for s in shapes:
        n = int(np.prod(s))
        out.append(flat[off:off + n].reshape(s))
        off += n
    return out


WEIGHTS = ('c_ctx', 'w_ada', 'b_ada', 'norm1_w', 'w_in', 'gla_up_f', 'gla_bias_f', 'gla_up_b', 'gla_bias_b', 'gla_norm_w',
           'conv_w', 'conv_b', 'dt_bias_f', 'dt_bias_b', 'a_log_f', 'a_log_b', 'd_skip', 'ssm_norm_w', 'w_pa', 'w_pb', 'w_out',
           'norm2_w', 'w_gate', 'w_up', 'w_down', 'final_norm_w')
BIG = ('w_in', 'w_pa', 'w_pb', 'w_out', 'w_gate', 'w_up', 'w_down')
COL_SHARDED = ('w_in', 'w_gate', 'w_up')
SMALL_SHARDED = ('gla_up_f', 'gla_up_b', 'conv_w')
ROW_TILE = 256


def _blocks_to_full(g4, name):
    n, r, c = g4.shape
    return g4.transpose(1, 0, 2).reshape(r, n * c) if name in COL_SHARDED else g4.reshape(n * r, c)


def _full_to_blocks(full, name):
    r, c = full.shape
    if name in COL_SHARDED:
        return full.reshape(r, N_CHIPS, c // N_CHIPS).transpose(1, 0, 2)
    return full.reshape(N_CHIPS, r // N_CHIPS, c)


def _permute_in(w_in_full):
    off = np.concatenate([[0], np.cumsum(IN_WIDTHS)])
    cols = [w_in_full[:, off[p]:off[p + 1]] for p in PERM]
    return jnp.concatenate(cols + [jnp.zeros((w_in_full.shape[0], SMALL_PAD), w_in_full.dtype)], axis=1)


def _unpermute_in(wp):
    off = np.concatenate([[0], np.cumsum([IN_WIDTHS[p] for p in PERM])])
    pieces = {p: wp[:, off[i]:off[i + 1]] for i, p in enumerate(PERM)}
    return jnp.concatenate([pieces[p] for p in range(len(IN_WIDTHS))], axis=1)


def _chip_cols(full, chip, n):
    return lax.dynamic_slice_in_dim(full, chip * n, n, axis=1)


def kernel(x, c, ctx, c_ctx, w_ada, b_ada, norm1_w, w_in, gla_up_f, gla_bias_f, gla_up_b, gla_bias_b, gla_norm_w, conv_w, conv_b, dt_bias_f, dt_bias_b, a_log_f, a_log_b, d_skip, ssm_norm_w, w_pa, w_pb, w_out, norm2_w, w_gate, w_up, w_down, final_norm_w, loss_target, m_c_ctx, m_w_ada, m_b_ada, m_norm1_w, m_w_in, m_gla_up_f, m_gla_bias_f, m_gla_up_b, m_gla_bias_b, m_gla_norm_w, m_conv_w, m_conv_b, m_dt_bias_f, m_dt_bias_b, m_a_log_f, m_a_log_b, m_d_skip, m_ssm_norm_w, m_w_pa, m_w_pb, m_w_out, m_norm2_w, m_w_gate, m_w_up, m_w_down, m_final_norm_w, v_c_ctx, v_w_ada, v_b_ada, v_norm1_w, v_w_in, v_gla_up_f, v_gla_bias_f, v_gla_up_b, v_gla_bias_b, v_gla_norm_w, v_conv_w, v_conv_b, v_dt_bias_f, v_dt_bias_b, v_a_log_f, v_a_log_b, v_d_skip, v_ssm_norm_w, v_w_pa, v_w_pb, v_w_out, v_norm2_w, v_w_gate, v_w_up, v_w_down, v_final_norm_w):
    given = dict(locals())
    W = {n: given[n] for n in WEIGHTS}
    M = {n: given["m_" + n] for n in WEIGHTS}
    V = {n: given["v_" + n] for n in WEIGHTS}
    L, Lc = x.shape[1], ctx.shape[1]
    tr = ROW_TILE
    assert L % tr == 0 and Lc % tr == 0 and L % Lc == 0 and Lc % SSM_C == 0
    n_lat_tiles = L // tr
    xi, yi, ci = _place()
    chip, me = 2 * xi + yi, 4 * xi + 2 * yi + ci
    x2, ctx2 = x[0], ctx[0]

    g0 = _allgather_small(_pack([c[0]] + [W[n][0] for n in SMALL_SHARDED]), "gather_c")
    g0 = g0.reshape(N_DEV, -1)
    c_all = g0[:, :D]
    small_full, off = {}, D
    for n in SMALL_SHARDED:
        r, cols = W[n].shape[1:]
        small_full[n] = jnp.concatenate([g0[2 * k, off:off + r * cols].reshape(r, cols) for k in range(N_CHIPS)], axis=1)
        off += r * cols
    up_f, up_b, conv_w_full = (small_full[n] for n in SMALL_SHARDED)

    cvec = jnp.zeros((ADA_ROWS, D), F32).at[:N_DEV].set(c_all).at[N_DEV].set(c_ctx)
    ada_cols = w_ada.shape[2]
    ada_part = _ada_fwd(cvec, w_ada[0], _chip_cols(b_ada, chip, ada_cols))
    g1_all = _allgather_small(ada_part, "gather_ada")
    ada_full = jnp.concatenate([g1_all[2 * k] for k in range(N_CHIPS)], axis=1)
    mine = lax.dynamic_slice_in_dim(ada_full, me, 1, axis=0)
    sh1, sc1, g1, sh2, sc2, g2 = (mine[:, k * D:(k + 1) * D] for k in range(6))
    csh1, csc1 = ada_full[N_DEV:N_DEV + 1, :D], ada_full[N_DEV:N_DEV + 1, D:2 * D]
    mod = jnp.stack([jnp.stack([sh1, sc1]), jnp.stack([csh1, csc1])])

    full = {'w_in': _blocks_to_full(_gather_split(w_in[0].astype(BF16), "gather_w_in"), 'w_in')}
    wp = _permute_in(full['w_in'])
    later = [n for n in BIG if n != 'w_in']

    def lr_rows(up, base):
        return jnp.zeros((128, GLA_H * GLA_DK), F32).at[base:base + GLA_RANK].set(up)
    u2 = jnp.stack([lr_rows(up_f, SM_LRF), lr_rows(up_b, SM_LRB)])
    u2_hi = u2.astype(BF16)
    u2_lo = (u2 - u2_hi.astype(F32)).astype(BF16)
    ut = u2.transpose(0, 2, 1)
    ut_hi = ut.astype(BF16)
    ut_lo = (ut - ut_hi.astype(F32)).astype(BF16)
    gbias = jnp.stack([gla_bias_f, gla_bias_b])
    kc = _ssd_consts(jnp.stack([dt_bias_f[0], dt_bias_b[0]]), jnp.stack([a_log_f[0], a_log_b[0]]))
    gw4 = jnp.tile(gla_norm_w, (1, GLA_H))
    dskip_exp = jnp.repeat(d_skip, SSM_P, axis=1)
    n_gla = (L // GLA_C, Lc // GLA_C)
    n_ssd = (L // SSM_C, Lc // SSM_C)

    h1 = _norm_mod(x2, ctx2, norm1_w, mod, n_lat_tiles, tr)
    parts = _mm(h1, wp, "nn", ACT, "mm_in", tm=1408, tn=3456)
    sm = _mm(h1, wp[:, COL_SM:], "nn", F32, "mm_in_small", tm=1408)
    xbc = _conv_fwd(parts, conv_w_full, conv_b, L // Lc, Lc)
    *o2, gla_hist = _gla_fwd(parts, sm, u2_hi, u2_lo, gbias, *n_gla)
    (*y2, ssd_hist), gathered = _ssd_fwd(xbc, sm, kc, *n_ssd, _gather_comm([W[n][0].astype(BF16) for n in later]))
    full.update({n: _blocks_to_full(g, n) for n, g in zip(later, gathered)})
    oan = _gla_out(o2, parts, gw4, tr)
    obn = _ssd_out(y2, xbc, parts, dskip_exp, ssm_norm_w, tr)
    ya = _mm(oan, full['w_pa'], "nn", ACT, "mm_pa", tm=1408)
    yb = _mm(obn, full['w_pb'], "nn", ACT, "mm_pb", tm=1408)
    merged = _merge(ya, yb, parts, tr)
    mix = _mm(merged, full['w_out'], "nn", ACT, "mm_out", tm=1408)
    h2, u = _resid_norm_mod(x2, ctx2, mix, g1, norm2_w, sh2, sc2, n_lat_tiles, tr)
    gp = _mm(u, full['w_gate'], "nn", ACT, "mm_gate", tm=1408, tn=D_FF)
    up = _mm(u, full['w_up'], "nn", ACT, "mm_up", tm=1408, tn=D_FF)
    f = _mm(gp, full['w_down'], "nn", ACT, "mm_down", tm=768, tk=1408, swiglu_up=up)
    dh3, df, acc_loss = _loss_head(h2, f, loss_target[0], g2, final_norm_w[None], n_lat_tiles, tr)

    dw = {}
    da = _mm(df, full['w_down'], "nt", ACT, "mm_down_dx", tm=1408, tn=D_FF)
    dw['w_down'] = _mm(gp, df, "tn", BF16, "mm_down_dw", tm=1408, tk=1408, swiglu_up=up)
    dgp, dup = _swiglu_act_bwd(da, gp, up, tr)
    du_a = _mm(dgp, full['w_gate'], "nt", ACT, "mm_gate_dx", tm=1408, tk=D_FF)
    du_b = _mm(dup, full['w_up'], "nt", ACT, "mm_up_dx", tm=1408, tk=D_FF)
    dw['w_gate'] = _mm(u, dgp, "tn", BF16, "mm_gate_dw", tm=1024, tn=1408, tk=2816)
    dw['w_up'] = _mm(u, dup, "tn", BF16, "mm_up_dw", tm=1024, tn=1408, tk=2816)
    dh2, dmix, acc_ffn = _ffn_in_bwd(du_a, du_b, h2, dh3, mix, sc2, g1, norm2_w, tr)
    dmerged = _mm(dmix, full['w_out'], "nt", ACT, "mm_out_dx", tm=1408)
    dw['w_out'] = _mm(merged, dmix, "tn", BF16, "mm_out_dw", tm=1024, tk=2816)
    dya, dyb, dparts = _merge_bwd(dmerged, ya, yb, parts, lax.empty((L + Lc, PW), BF16), tr)
    doan = _mm(dya, full['w_pa'], "nt", ACT, "mm_pa_dx", tm=1408)
    dw['w_pa'] = _mm(oan, dya, "tn", BF16, "mm_pa_dw", tm=1024, tk=2816)
    dobn = _mm(dyb, full['w_pb'], "nt", ACT, "mm_pb_dx", tm=1408)
    dw['w_pb'] = _mm(obn, dyb, "tn", BF16, "mm_pb_dw", tm=1024, tk=2816)
    do, dparts, acc_gla = _gla_out_bwd(doan, o2, parts, gw4, dparts, tr)
    dq, dk, dv, dpre = _gla_bwd(do, parts, sm, u2_hi, u2_lo, gbias, gla_hist, *n_gla)
    dy, dparts, acc_ssd = _ssd_out_bwd(dobn, y2, xbc, parts, dskip_exp, ssm_norm_w, dparts, tr)
    (dx_scan, db_scan, dc_scan, dsm, acc_alog), exchanged = _ssd_bwd(
        dy, xbc, sm, kc, ssd_hist, *n_ssd, _exchange_comm([_full_to_blocks(dw[n], n) for n in later]))
    exchanged = dict(zip(later, exchanged))
    dparts, acc_conv = _conv_bwd(dx_scan, db_scan, dc_scan, dy, dskip_exp, parts, conv_w_full, conv_b, dparts, L // Lc, Lc)
    dparts = _gla_assemble(dq, dk, dv, dparts, tr)
    dparts, dup_gla, acc_gbias, acc_dtb = _small_assemble(dpre, dsm, sm, ut_hi, ut_lo, dparts, tr)
    dw['w_in'] = _unpermute_in(_mm(h1, dparts, "tn", BF16, "mm_in_dw", tm=1024, tn=1152, tk=2816))
    dh1, (exchanged['w_in'],) = _mm(dparts, wp, "nt", F32, "mm_in_dx", tm=768, tk=3456,
                                    comm=_exchange_comm([_full_to_blocks(dw['w_in'], 'w_in')]))
    dx, acc_n1 = _norm1_bwd(dh1, x2, ctx2, dh2, norm1_w, mod, n_lat_tiles, tr)

    partial = dict(
        norm1_w=acc_n1[0, 2] + acc_n1[1, 2],
        gla_up_f=dup_gla[0, SM_LRF:SM_LRF + GLA_RANK], gla_bias_f=acc_gbias[0],
        gla_up_b=dup_gla[1, SM_LRB:SM_LRB + GLA_RANK], gla_bias_b=acc_gbias[1],
        gla_norm_w=acc_gla[0].reshape(GLA_H, GLA_DV).sum(0),
        conv_w=acc_conv[:SSM_CONV], conv_b=acc_conv[SSM_CONV],
        dt_bias_f=acc_dtb[0, SM_DTF:SM_DTF + SSM_HEADS], dt_bias_b=acc_dtb[0, SM_DTB:SM_DTB + SSM_HEADS],
        a_log_f=acc_alog[0, :, 0, :SSM_HPG], a_log_b=acc_alog[1, :, 0, :SSM_HPG],
        d_skip=acc_ssd[1].reshape(SSM_HEADS, SSM_P).sum(1), ssm_norm_w=acc_ssd[0],
        norm2_w=acc_ffn[2], final_norm_w=acc_loss[0],
    )
    dada = jnp.concatenate([acc_n1[0, 1], acc_n1[0, 0], acc_ffn[3], acc_ffn[1], acc_ffn[0], acc_loss[1]])
    dada_ctx = jnp.concatenate([acc_n1[1, 1], acc_n1[1, 0], jnp.zeros((4 * D,), F32)])
    names = list(partial)
    payload = [partial[n] for n in names] + [dada + dada_ctx, dada_ctx, acc_loss[2], dada]
    sizes = [int(np.prod(p.shape)) for p in payload]
    g8 = _allgather_small(_pack(payload), "gather_small_grads")
    summed = _unpack(_sum8(g8), [(s,) for s in sizes])
    grads = {n: s.reshape(W[n].shape if n not in SMALL_SHARDED else partial[n].shape) for n, s in zip(names, summed)}
    grads['b_ada'] = summed[len(names)].reshape(b_ada.shape)
    dada_ctx_sum = summed[len(names) + 1]
    loss = jnp.sum(summed[len(names) + 2])
    dada_all = g8.reshape(N_DEV, -1)[:, sum(sizes[:-1]):sum(sizes)]

    dada16 = jnp.zeros((ADA_ROWS, ada_cols), F32)
    dada16 = dada16.at[:N_DEV].set(_chip_cols(dada_all, chip, ada_cols)).at[N_DEV].set(_chip_cols(dada_ctx_sum[None], chip, ada_cols)[0])
    g_wada, dl_wada, m_wada, v_wada, p16 = _wada_bwd_adam(cvec, dada16, w_ada[0], m_w_ada[0], v_w_ada[0])
    p8 = _allgather_small(p16[N_DEV:], "gather_cctx")
    grads['c_ctx'] = _cctx_grad(p8[:, 0:1, :], c_ctx[None])[0]
    for n in SMALL_SHARDED:
        grads[n] = _chip_cols(grads[n], chip, W[n].shape[2])[None]

    small = [n for n in WEIGHTS if n not in BIG and n != 'w_ada']
    shapes = [W[n].shape for n in small]
    dl_s, m_s, v_s = _adam_small(*[_pack([d[n] for n in small]) for d in (W, grads, M, V)])
    delta = dict(zip(small, _unpack(dl_s, shapes)))
    new_m = dict(zip(small, _unpack(m_s, shapes)))
    new_v = dict(zip(small, _unpack(v_s, shapes)))
    grads['w_ada'], delta['w_ada'], new_m['w_ada'], new_v['w_ada'] = g_wada[None], dl_wada[None], m_wada[None], v_wada[None]

    for n in BIG:
        g, dl, m2, v2 = _reduce_adam(exchanged[n], W[n][0], M[n][0], V[n][0], "adam_" + n)
        grads[n], delta[n], new_m[n], new_v[n] = g[None], dl[None], m2[None], v2[None]

    return (loss, dx[None], *[grads[n] for n in WEIGHTS], *[delta[n] for n in WEIGHTS],
            *[new_m[n] for n in WEIGHTS], *[new_v[n] for n in WEIGHTS])
```

```python
import numpy as np
import jax
import jax.numpy as jnp
from jax import lax
from jax.experimental import pallas as pl
from jax.experimental.pallas import tpu as pltpu

F32 = jnp.float32
BF16 = jnp.bfloat16
ACT = BF16
MESH = pl.DeviceIdType.MESH

D = 1024
EPS = 1e-6
GRID_W = 64
GLA_H, GLA_DK, GLA_DV, GLA_RANK, GLA_TAU = 4, 128, 256, 16, 16.0
GLA_C = 128
SSM_INNER, SSM_P, SSM_HEADS, SSM_G, SSM_HPG, SSM_N = 2048, 64, 32, 4, 8, 128
SSM_C = 128
SSM_CONV, CONV_LEFT = 4, 2
D_FF = 2816
IN_WIDTHS = (512, 512, 1024, 1024, 16, 16, 2048, 2048, 512, 512, 32, 32, 1024, 1024)
D_IN = sum(IN_WIDTHS)
PERM = (6, 7, 8, 9, 3, 0, 1, 2, 12, 13, 4, 5, 10, 11)
PW = 10368
SMALL_PAD = PW - D_IN
COL_Z, COL_XBC, COL_R, COL_Q, COL_K, COL_V, COL_GA, COL_GB, COL_SM = 0, 2048, 5120, 6144, 6656, 7168, 8192, 9216, 10240
SM_LRF, SM_LRB, SM_DTF, SM_DTB = 0, 16, 32, 64
EXP_CLAMP = 80.0
ADAM_LR, ADAM_B1, ADAM_B2, ADAM_EPS, ADAM_WD, ADAM_STEP = 0.001, 0.9, 0.999, 1e-08, 0.01, 10
N_CHIPS, N_DEV = 4, 8
VMEM_LIMIT = 56 * 1024 * 1024


def _cparams(sem=None):
    return pltpu.CompilerParams(dimension_semantics=sem, vmem_limit_bytes=VMEM_LIMIT)


def _dg(a, b, ca, cb):
    return lax.dot_general(a, b, (((ca,), (cb,)), ((), ())), preferred_element_type=F32)


def _nn(a, b):
    return _dg(a, b, 1, 0)


def _nt(a, b):
    return _dg(a, b, 1, 1)


def _tn(a, b):
    return _dg(a, b, 0, 0)


def _bf(x):
    return x.astype(BF16)


def _f32(ref):
    return ref[...].astype(F32)


def _split(x):
    hi = x.astype(BF16)
    return hi, (x - hi.astype(F32)).astype(BF16)


def _nn_x(a, b_exact):
    hi, lo = _split(a)
    return _nn(hi, b_exact) + _nn(lo, b_exact)


def _x_nn(a_exact, b):
    hi, lo = _split(b)
    return _nn(a_exact, hi) + _nn(a_exact, lo)


def _nn3(a, b_hi, b_lo):
    hi, lo = _split(a)
    return _nn(hi, b_hi) + _nn(lo, b_hi) + _nn(hi, b_lo)


def _sigmoid(x):
    return 1.0 / (1.0 + jnp.exp(-x))


def _silu(x):
    return x * _sigmoid(x)


def _dsilu(x):
    s = _sigmoid(x)
    return s * (1.0 + x * (1.0 - s))


def _softplus(x):
    return jnp.maximum(x, 0.0) + jnp.log(1.0 + jnp.exp(-jnp.abs(x)))


def _log_sigmoid(x):
    return jnp.minimum(x, 0.0) - jnp.log(1.0 + jnp.exp(-jnp.abs(x)))


def _tile(n, target, mult=8):
    best = None
    for t in range(mult, min(n, target) + 1, mult):
        if n % t == 0:
            best = t
    assert best is not None, (n, target, mult)
    return best


def _mm(a, b, mode, out_dtype, name, tm=512, tn=1024, tk=2048, comm=None, swiglu_up=None):
    if mode == "nn":
        (M, K), N = a.shape, b.shape[1]
    elif mode == "nt":
        (M, K), N = a.shape, b.shape[0]
    else:
        (K, M), N = a.shape, b.shape[1]
    tm, tn, tk = _tile(M, tm, 128), _tile(N, tn, 128), _tile(K, tk, 128)
    nk = K // tk
    ca, cb = {"nn": (1, 0), "nt": (1, 1), "tn": (0, 0)}[mode]

    def body(a_ref, *rest):
        if swiglu_up is None:
            (b_ref, o_ref, *acc), av = rest, a_ref[...]
        else:
            (up_ref, b_ref, o_ref, *acc) = rest
            av = (_silu(a_ref[...].astype(F32)) * up_ref[...].astype(F32)).astype(BF16)
        part = _dg(av, b_ref[...], ca, cb)
        if nk == 1:
            o_ref[...] = part.astype(out_dtype)
        else:
            k = pl.program_id(2)

            @pl.when(k == 0)
            def _():
                acc[0][...] = part

            @pl.when(k > 0)
            def _():
                acc[0][...] += part

            @pl.when(k == nk - 1)
            def _():
                o_ref[...] = acc[0][...].astype(out_dtype)

    a_spec = pl.BlockSpec((tk, tm), lambda i, j, k: (k, i)) if mode == "tn" else pl.BlockSpec((tm, tk), lambda i, j, k: (i, k))
    b_spec = pl.BlockSpec((tn, tk), lambda i, j, k: (j, k)) if mode == "nt" else pl.BlockSpec((tk, tn), lambda i, j, k: (k, j))
    gi, gj = M // tm, N // tn
    scratch = [pltpu.VMEM((tm, tn), F32)] if nk > 1 else []
    out_spec, out_shape = pl.BlockSpec((tm, tn), lambda i, j, k: (i, j)), jax.ShapeDtypeStruct((M, N), out_dtype)
    if comm is None:
        lhs = [a] if swiglu_up is None else [a, swiglu_up]
        return pl.pallas_call(
            body, name=name, grid=(gi, gj, nk), in_specs=[a_spec] * len(lhs) + [b_spec], out_specs=out_spec,
            out_shape=out_shape, scratch_shapes=scratch, compiler_params=_cparams(("arbitrary", "arbitrary", "arbitrary")),
        )(*lhs, b)
    assert swiglu_up is None
    at = lambda i, j, k: (pl.program_id(0) == i) & (pl.program_id(1) == j) & (pl.program_id(2) == k)
    hosted = _hosted(body, 2, 1, len(scratch), comm, lambda: at(0, 0, 0), lambda: at(gi - 1, 0, 0),
                     lambda: at(gi - 1, gj - 1, nk - 1))
    outs = pl.pallas_call(
        hosted, name=name, grid=(gi, gj, nk), in_specs=[a_spec, b_spec] + [_ANY] * len(comm.arrays),
        out_specs=[out_spec] + [_ANY] * len(comm.out_shape), out_shape=[out_shape] + comm.out_shape,
        scratch_shapes=scratch + comm.scratch, compiler_params=_cparams(("arbitrary", "arbitrary", "arbitrary")),
    )(a, b, *comm.arrays)
    return outs[0], outs[1:]


def _place():
    return lax.axis_index("x"), lax.axis_index("y"), lax.axis_index("c")


def _flip(v, bit):
    return 1 - v if bit else v


def _allgather_small(v, name):
    R, C = v.shape

    def body(v_ref, out_ref, send_sems, recv_sems, local_sem):
        x, y, c = _place()
        me = 4 * x + 2 * y + c
        mine = pltpu.make_async_copy(v_ref, out_ref.at[me], local_sem)
        mine.start()

        def peer(r):
            return _flip(x, (r >> 2) & 1), _flip(y, (r >> 1) & 1), _flip(c, r & 1)

        sends = [pltpu.make_async_remote_copy(
            src_ref=v_ref, dst_ref=out_ref.at[me], send_sem=send_sems.at[r - 1], recv_sem=recv_sems.at[r - 1],
            device_id=peer(r), device_id_type=MESH) for r in range(1, N_DEV)]
        for cp in sends:
            cp.start()
        for r in range(1, N_DEV):
            px, py, pc = peer(r)
            pltpu.make_async_remote_copy(
                src_ref=v_ref, dst_ref=out_ref.at[4 * px + 2 * py + pc], send_sem=send_sems.at[r - 1],
                recv_sem=recv_sems.at[r - 1], device_id=(x, y, c), device_id_type=MESH).wait_recv()
        for cp in sends:
            cp.wait_send()
        mine.wait()

    return pl.pallas_call(
        body, name=name, out_shape=jax.ShapeDtypeStruct((N_DEV, R, C), v.dtype),
        in_specs=[pl.BlockSpec(memory_space=pltpu.VMEM)], out_specs=pl.BlockSpec(memory_space=pltpu.VMEM),
        scratch_shapes=[pltpu.SemaphoreType.DMA((N_DEV - 1,)), pltpu.SemaphoreType.DMA((N_DEV - 1,)), pltpu.SemaphoreType.DMA],
        compiler_params=pltpu.CompilerParams(vmem_limit_bytes=VMEM_LIMIT),
    )(v)


_CHIP_RELATIONS = ((1, 0), (0, 1), (1, 1))


def _gather_split(shard, name):
    rows, cols = shard.shape
    half = rows // 2

    def body(in_ref, out_ref, send_sems, recv_sems, local_sem):
        x, y, c = _place()
        chip = 2 * x + y
        mine = pl.ds(pl.multiple_of(c * half, 16), half)
        local = pltpu.make_async_copy(in_ref, out_ref.at[chip], local_sem)
        local.start()
        peers = [(_flip(x, fx), _flip(y, fy)) for fx, fy in _CHIP_RELATIONS]
        sends = [pltpu.make_async_remote_copy(
            src_ref=in_ref.at[mine], dst_ref=out_ref.at[chip, mine], send_sem=send_sems.at[j], recv_sem=recv_sems.at[j],
            device_id=(px, py, c), device_id_type=MESH) for j, (px, py) in enumerate(peers)]
        for cp in sends:
            cp.start()
        for j, (px, py) in enumerate(peers):
            landed = out_ref.at[2 * px + py, mine]
            pltpu.make_async_remote_copy(
                src_ref=landed, dst_ref=landed, send_sem=send_sems.at[j], recv_sem=recv_sems.at[j],
                device_id=(x, y, c), device_id_type=MESH).wait_recv()
            fwd = pltpu.make_async_remote_copy(
                src_ref=landed, dst_ref=landed, send_sem=send_sems.at[3 + j], recv_sem=recv_sems.at[3 + j],
                device_id=(x, y, 1 - c), device_id_type=MESH)
            fwd.start()
            sends.append(fwd)
        for j in range(3):
            landed = out_ref.at[0, mine]
            pltpu.make_async_remote_copy(
                src_ref=landed, dst_ref=landed, send_sem=send_sems.at[3 + j], recv_sem=recv_sems.at[3 + j],
                device_id=(x, y, c), device_id_type=MESH).wait_recv()
        for cp in sends:
            cp.wait_send()
        local.wait()

    any_spec = pl.BlockSpec(memory_space=pl.ANY)
    return pl.pallas_call(
        body, name=name, out_shape=jax.ShapeDtypeStruct((N_CHIPS, rows, cols), shard.dtype),
        in_specs=[any_spec], out_specs=any_spec,
        scratch_shapes=[pltpu.SemaphoreType.DMA((6,)), pltpu.SemaphoreType.DMA((6,)), pltpu.SemaphoreType.DMA],
    )(shard)


class _Comm:
    def __init__(self, arrays, out_shape, scratch, start, middle, finish):
        self.arrays, self.out_shape, self.scratch = arrays, out_shape, scratch
        self.start, self.middle, self.finish = start, middle, finish


def _hosted(body, n_in, n_out, n_scratch, comm, first, middle, last):
    nc, no = len(comm.arrays), len(comm.out_shape)

    def wrapped(*refs):
        a = n_in + nc
        b = a + n_out + no
        ins, c_ins, outs, c_outs = refs[:n_in], refs[n_in:a], refs[a:a + n_out], refs[a + n_out:b]
        scratch, c_sems = refs[b:b + n_scratch], refs[b + n_scratch:]

        @pl.when(first())
        def _():
            comm.start(c_ins, c_outs, c_sems)

        body(*ins, *outs, *scratch)
        if comm.middle is not None:
            @pl.when(middle())
            def _():
                comm.middle(c_ins, c_outs, c_sems)

        @pl.when(last())
        def _():
            comm.finish(c_ins, c_outs, c_sems)

    return wrapped


_ANY = pl.BlockSpec(memory_space=pl.ANY)


def _gather_comm(shards):
    n = len(shards)

    def copies(kind, ins, outs, sems):
        send_sems, recv_sems, local_sems = sems
        x, y, c = _place()
        chip = 2 * x + y
        if kind == "local":
            return [pltpu.make_async_copy(ins[i], outs[i].at[chip], local_sems.at[i]) for i in range(n)]
        made = []
        for i in range(n):
            for j, (fx, fy) in enumerate(_CHIP_RELATIONS):
                px, py = _flip(x, fx), _flip(y, fy)
                slot, to = (chip, (px, py, c)) if kind == "send" else (2 * px + py, (x, y, c))
                made.append(pltpu.make_async_remote_copy(
                    src_ref=ins[i], dst_ref=outs[i].at[slot], send_sem=send_sems.at[i, j], recv_sem=recv_sems.at[i, j],
                    device_id=to, device_id_type=MESH))
        return made

    def start(ins, outs, sems):
        for cp in copies("local", ins, outs, sems) + copies("send", ins, outs, sems):
            cp.start()

    def finish(ins, outs, sems):
        for cp in copies("recv", ins, outs, sems):
            cp.wait_recv()
        for cp in copies("send", ins, outs, sems):
            cp.wait_send()
        for cp in copies("local", ins, outs, sems):
            cp.wait()

    return _Comm(list(shards), [jax.ShapeDtypeStruct((N_CHIPS,) + s.shape, s.dtype) for s in shards],
                 [pltpu.SemaphoreType.DMA((n, 3)), pltpu.SemaphoreType.DMA((n, 3)), pltpu.SemaphoreType.DMA((n,))],
                 start, None, finish)


def _exchange_comm(blocks):
    n = len(blocks)

    def copies(kind, ins, outs, sems):
        send_sems, recv_sems, local_sems = sems
        x, y, c = _place()
        chip = 2 * x + y
        me, sibling = (x, y, c), (x, y, 1 - c)

        def remote(src, dst, i, j, to):
            return pltpu.make_async_remote_copy(src_ref=src, dst_ref=dst, send_sem=send_sems.at[i, j],
                                                recv_sem=recv_sems.at[i, j], device_id=to, device_id_type=MESH)

        made = []
        for i in range(n):
            if kind == "local":
                made.append(pltpu.make_async_copy(ins[i].at[chip], outs[i].at[chip], local_sems.at[i]))
                continue
            for j, (fx, fy) in enumerate(_CHIP_RELATIONS):
                px, py = _flip(x, fx), _flip(y, fy)
                src = 2 * px + py
                if kind == "first":
                    made.append(remote(ins[i].at[src], outs[i].at[chip], i, j, (px, py, c)))
                elif kind == "landed":
                    made.append(remote(ins[i].at[src], outs[i].at[src], i, j, me))
                elif kind == "passed":
                    made.append(remote(outs[i].at[src], outs[i].at[N_CHIPS + src], i, 4 + j, sibling))
            if kind == "first":
                made.append(remote(ins[i].at[chip], outs[i].at[N_CHIPS + chip], i, 3, sibling))
            if kind == "arrivals":
                made += [remote(ins[i].at[0], outs[i].at[0], i, j, me) for j in (3, 4, 5, 6)]
        return made

    def start(ins, outs, sems):
        for cp in copies("local", ins, outs, sems) + copies("first", ins, outs, sems):
            cp.start()

    def middle(ins, outs, sems):
        for got, fwd in zip(copies("landed", ins, outs, sems), copies("passed", ins, outs, sems)):
            got.wait_recv()
            fwd.start()

    def finish(ins, outs, sems):
        for cp in copies("arrivals", ins, outs, sems):
            cp.wait_recv()
        for cp in copies("first", ins, outs, sems) + copies("passed", ins, outs, sems):
            cp.wait_send()
        for cp in copies("local", ins, outs, sems):
            cp.wait()

    return _Comm(list(blocks), [jax.ShapeDtypeStruct((N_DEV,) + b.shape[1:], b.dtype) for b in blocks],
                 [pltpu.SemaphoreType.DMA((n, 7)), pltpu.SemaphoreType.DMA((n, 7)), pltpu.SemaphoreType.DMA((n,))],
                 start, middle, finish)


def _row_spec(tr, w, col=0):
    return pl.BlockSpec((tr, w), lambda i: (i, col))


def _vec_spec(w):
    return pl.BlockSpec((1, w), lambda i: (0, 0))


def _acc_spec(w):
    return pl.BlockSpec((8, w), lambda i: (0, 0))


def _rms(x):
    return lax.rsqrt(jnp.mean(x * x, axis=-1, keepdims=True) + EPS)


def _rms_bwd(dn, n, rstd):
    return rstd * (dn - n * jnp.mean(dn * n, axis=-1, keepdims=True))


def _colsum(x):
    return jnp.sum(x, axis=0, keepdims=True)


def _zero_first(ref):
    @pl.when(pl.program_id(0) == 0)
    def _():
        ref[...] = jnp.zeros_like(ref)


def _x_specs(tr, n_lat_tiles):
    return [pl.BlockSpec((tr, D), lambda i: (jnp.minimum(i, n_lat_tiles - 1), 0)),
            pl.BlockSpec((tr, D), lambda i: (jnp.maximum(i - n_lat_tiles, 0), 0))]


def _x_tile(x_ref, c_ref, n_lat_tiles):
    return jnp.where(pl.program_id(0) >= n_lat_tiles, c_ref[...], x_ref[...])


def _norm_mod(x, ctx, w, mod, n_lat_tiles, tr):
    R = x.shape[0] + ctx.shape[0]

    def body(x_ref, c_ref, w_ref, mod_ref, o_ref):
        xv = _x_tile(x_ref, c_ref, n_lat_tiles)
        n = xv * _rms(xv) * w_ref[...]
        o_ref[...] = (n * (1.0 + mod_ref[0, 1]) + mod_ref[0, 0]).astype(BF16)

    return pl.pallas_call(
        body, name="norm1_mod", grid=(R // tr,),
        in_specs=_x_specs(tr, n_lat_tiles) + [_vec_spec(D), pl.BlockSpec(
            (1, 2, 1, D), lambda i: (jnp.where(i >= n_lat_tiles, 1, 0), 0, 0, 0))],
        out_specs=_row_spec(tr, D), out_shape=jax.ShapeDtypeStruct((R, D), BF16),
        compiler_params=_cparams(("arbitrary",)),
    )(x, ctx, w, mod)


def _resid_norm_mod(x, ctx, mix, g1, w2, sh2, sc2, n_lat_tiles, tr):
    R = x.shape[0] + ctx.shape[0]

    def body(x_ref, c_ref, mix_ref, g1_ref, w_ref, sh_ref, sc_ref, h2_ref, u_ref):
        h2 = _x_tile(x_ref, c_ref, n_lat_tiles) + g1_ref[...] * mix_ref[...]
        h2_ref[...] = h2
        n = h2 * _rms(h2) * w_ref[...]
        u_ref[...] = (n * (1.0 + sc_ref[...]) + sh_ref[...]).astype(BF16)

    return pl.pallas_call(
        body, name="resid_norm2_mod", grid=(R // tr,),
        in_specs=_x_specs(tr, n_lat_tiles) + [_row_spec(tr, D)] + [_vec_spec(D)] * 4,
        out_specs=[_row_spec(tr, D), _row_spec(tr, D)],
        out_shape=[jax.ShapeDtypeStruct((R, D), F32), jax.ShapeDtypeStruct((R, D), BF16)],
        compiler_params=_cparams(("arbitrary",)),
    )(x, ctx, mix, g1, w2, sh2, sc2)


def _loss_head(h2, f, target, g2, fw, n_lat_tiles, tr):
    R = h2.shape[0]

    def body(h2_ref, f_ref, t_ref, g2_ref, fw_ref, dh3_ref, df_ref, acc_ref):
        _zero_first(acc_ref)
        lat = pl.program_id(0) < n_lat_tiles
        fv = f_ref[...]
        h3 = h2_ref[...] + g2_ref[...] * fv
        rstd = _rms(h3)
        n = h3 * rstd
        err = n * fw_ref[...] - t_ref[...]
        dy = err * (1.0 / D)
        dh3 = jnp.where(lat, _rms_bwd(dy * fw_ref[...], n, rstd), 0.0)
        dh3_ref[...] = dh3
        df_ref[...] = (g2_ref[...] * dh3).astype(BF16)
        acc_ref[0:1, :] += jnp.where(lat, _colsum(dy * n), 0.0)
        acc_ref[1:2, :] += _colsum(dh3 * fv)
        acc_ref[2:3, :] += jnp.where(lat, _colsum(err * err) * (0.5 / D), 0.0)

    return pl.pallas_call(
        body, name="loss_head", grid=(R // tr,),
        in_specs=[_row_spec(tr, D), _row_spec(tr, D),
                  pl.BlockSpec((tr, D), lambda i: (jnp.minimum(i, n_lat_tiles - 1), 0)), _vec_spec(D), _vec_spec(D)],
        out_specs=[_row_spec(tr, D), _row_spec(tr, D), _acc_spec(D)],
        out_shape=[jax.ShapeDtypeStruct((R, D), F32), jax.ShapeDtypeStruct((R, D), BF16), jax.ShapeDtypeStruct((8, D), F32)],
        compiler_params=_cparams(("arbitrary",)),
    )(h2, f, target, g2, fw)


def _ffn_in_bwd(du_a, du_b, h2, dh3, mix, sc2, g1, w2, tr):
    R = h2.shape[0]

    def body(dua_ref, dub_ref, h2_ref, dh3_ref, mix_ref, sc_ref, g1_ref, w_ref, dh2_ref, dmix_ref, acc_ref):
        _zero_first(acc_ref)
        du = _f32(dua_ref) + _f32(dub_ref)
        h2 = h2_ref[...]
        rstd = _rms(h2)
        n = h2 * rstd
        dnw = du * (1.0 + sc_ref[...])
        dh2 = dh3_ref[...] + _rms_bwd(dnw * w_ref[...], n, rstd)
        dh2_ref[...] = dh2
        dmix_ref[...] = (g1_ref[...] * dh2).astype(BF16)
        acc_ref[0:1, :] += _colsum(du * n * w_ref[...])
        acc_ref[1:2, :] += _colsum(du)
        acc_ref[2:3, :] += _colsum(dnw * n)
        acc_ref[3:4, :] += _colsum(dh2 * mix_ref[...])

    return pl.pallas_call(
        body, name="ffn_in_bwd", grid=(R // tr,),
        in_specs=[_row_spec(tr, D)] * 5 + [_vec_spec(D)] * 3,
        out_specs=[_row_spec(tr, D), _row_spec(tr, D), _acc_spec(D)],
        out_shape=[jax.ShapeDtypeStruct((R, D), F32), jax.ShapeDtypeStruct((R, D), BF16), jax.ShapeDtypeStruct((8, D), F32)],
        compiler_params=_cparams(("arbitrary",)),
    )(du_a, du_b, h2, dh3, mix, sc2, g1, w2)


def _norm1_bwd(dh1, x, ctx, dh2, w1, mod, n_lat_tiles, tr):
    R = x.shape[0] + ctx.shape[0]

    def body(dh1_ref, x_ref, c_ref, dh2_ref, w_ref, mod_ref, dx_ref, acc_ref):
        i = pl.program_id(0)

        @pl.when((i == 0) | (i == n_lat_tiles))
        def _():
            acc_ref[...] = jnp.zeros_like(acc_ref)

        dh1 = dh1_ref[...]
        x = _x_tile(x_ref, c_ref, n_lat_tiles)
        rstd = _rms(x)
        n = x * rstd
        dnw = dh1 * (1.0 + mod_ref[0, 1])

        @pl.when(i < n_lat_tiles)
        def _():
            dx_ref[...] = dh2_ref[...] + _rms_bwd(dnw * w_ref[...], n, rstd)

        acc_ref[0, 0:1, :] += _colsum(dh1 * n * w_ref[...])
        acc_ref[0, 1:2, :] += _colsum(dh1)
        acc_ref[0, 2:3, :] += _colsum(dnw * n)

    sel = lambda i: jnp.where(i >= n_lat_tiles, 1, 0)
    return pl.pallas_call(
        body, name="norm1_bwd", grid=(R // tr,),
        in_specs=[_row_spec(tr, D)] + _x_specs(tr, n_lat_tiles) + [_row_spec(tr, D), _vec_spec(D),
                                                                   pl.BlockSpec((1, 2, 1, D), lambda i: (sel(i), 0, 0, 0))],
        out_specs=[pl.BlockSpec((tr, D), lambda i: (jnp.minimum(i, n_lat_tiles - 1), 0)),
                   pl.BlockSpec((1, 8, D), lambda i: (sel(i), 0, 0))],
        out_shape=[jax.ShapeDtypeStruct((n_lat_tiles * tr, D), F32), jax.ShapeDtypeStruct((2, 8, D), F32)],
        compiler_params=_cparams(("arbitrary",)),
    )(dh1, x, ctx, dh2, w1, mod)


def _swiglu_act_bwd(da, gp, up, tr):
    R = gp.shape[0]

    def body(da_ref, g_ref, u_ref, dg_ref, du_ref):
        da, g = _f32(da_ref), _f32(g_ref)
        dg_ref[...] = (da * _f32(u_ref) * _dsilu(g)).astype(BF16)
        du_ref[...] = (da * _silu(g)).astype(BF16)

    return pl.pallas_call(
        body, name="swiglu_act_bwd", grid=(R // tr,), in_specs=[_row_spec(tr, D_FF)] * 3, out_specs=[_row_spec(tr, D_FF)] * 2,
        out_shape=[jax.ShapeDtypeStruct((R, D_FF), BF16)] * 2, compiler_params=_cparams(("arbitrary",)),
    )(da, gp, up)


def _merge(ya, yb, parts, tr):
    R = ya.shape[0]

    def body(ya_ref, yb_ref, ga_ref, gb_ref, o_ref):
        o_ref[...] = (_sigmoid(_f32(ga_ref)) * _f32(ya_ref) + _sigmoid(_f32(gb_ref)) * _f32(yb_ref)).astype(BF16)

    return pl.pallas_call(
        body, name="merge", grid=(R // tr,),
        in_specs=[_row_spec(tr, D), _row_spec(tr, D), _row_spec(tr, D, COL_GA // D), _row_spec(tr, D, COL_GB // D)],
        out_specs=_row_spec(tr, D), out_shape=jax.ShapeDtypeStruct((R, D), BF16), compiler_params=_cparams(("arbitrary",)),
    )(ya, yb, parts, parts)


def _dparts_out(tr, w, col, nd=1):
    blk = col // w
    return pl.BlockSpec((tr, w), (lambda i: (i, blk)) if nd == 1 else (lambda i, j: (i, blk + j)))


def _merge_bwd(dm, ya, yb, parts, dparts, tr):
    R = ya.shape[0]

    def body(dm_ref, ya_ref, yb_ref, ga_ref, gb_ref, _, dya_ref, dyb_ref, dg_ref):
        dm = _f32(dm_ref)
        sa, sb = _sigmoid(_f32(ga_ref)), _sigmoid(_f32(gb_ref))
        dya_ref[...] = (dm * sa).astype(BF16)
        dyb_ref[...] = (dm * sb).astype(BF16)
        dg_ref[:, 0:D] = (dm * _f32(ya_ref) * sa * (1.0 - sa)).astype(BF16)
        dg_ref[:, D:2 * D] = (dm * _f32(yb_ref) * sb * (1.0 - sb)).astype(BF16)

    return pl.pallas_call(
        body, name="merge_bwd", grid=(R // tr,),
        in_specs=[_row_spec(tr, D)] * 3 + [_row_spec(tr, D, COL_GA // D), _row_spec(tr, D, COL_GB // D), _ANY],
        out_specs=[_row_spec(tr, D), _row_spec(tr, D), _dparts_out(tr, 2 * D, COL_GA)],
        out_shape=[jax.ShapeDtypeStruct((R, D), BF16)] * 2 + [jax.ShapeDtypeStruct(dparts.shape, BF16)],
        input_output_aliases={5: 2}, compiler_params=_cparams(("arbitrary",)),
    )(dm, ya, yb, parts, parts, dparts)


def _gla_out(o2, parts, gw4, tr):
    R = parts.shape[0]

    def body(of_ref, ob_ref, r_ref, w_ref, out_ref):
        oa = _f32(of_ref) + _f32(ob_ref)
        sr = _silu(_f32(r_ref))
        for h in range(GLA_H):
            s = slice(h * GLA_DV, (h + 1) * GLA_DV)
            o = oa[:, s]
            out_ref[:, s] = (o * _rms(o) * w_ref[:, s] * sr[:, s]).astype(BF16)

    return pl.pallas_call(
        body, name="gla_out", grid=(R // tr,),
        in_specs=[_row_spec(tr, D), _row_spec(tr, D), _row_spec(tr, D, COL_R // D), _vec_spec(D)],
        out_specs=_row_spec(tr, D), out_shape=jax.ShapeDtypeStruct((R, D), BF16), compiler_params=_cparams(("arbitrary",)),
    )(o2[0], o2[1], parts, gw4)


def _gla_out_bwd(dout, o2, parts, gw4, dparts, tr):
    R = parts.shape[0]

    def body(d_ref, of_ref, ob_ref, r_ref, w_ref, _, do_ref, dr_ref, acc_ref):
        _zero_first(acc_ref)
        oa = _f32(of_ref) + _f32(ob_ref)
        r = _f32(r_ref)
        sr = _silu(r)
        dout = _f32(d_ref)
        for h in range(GLA_H):
            s = slice(h * GLA_DV, (h + 1) * GLA_DV)
            o = oa[:, s]
            rstd = _rms(o)
            n = o * rstd
            w = w_ref[:, s]
            dr_ref[:, s] = (dout[:, s] * n * w * _dsilu(r[:, s])).astype(BF16)
            dnw = dout[:, s] * sr[:, s]
            do_ref[:, s] = _rms_bwd(dnw * w, n, rstd).astype(ACT)
            acc_ref[0:1, s] += _colsum(dnw * n)

    return pl.pallas_call(
        body, name="gla_out_bwd", grid=(R // tr,),
        in_specs=[_row_spec(tr, D), _row_spec(tr, D), _row_spec(tr, D), _row_spec(tr, D, COL_R // D), _vec_spec(D), _ANY],
        out_specs=[_row_spec(tr, D), _dparts_out(tr, D, COL_R), _acc_spec(D)],
        out_shape=[jax.ShapeDtypeStruct((R, D), ACT), jax.ShapeDtypeStruct(dparts.shape, BF16), jax.ShapeDtypeStruct((8, D), F32)],
        input_output_aliases={5: 1}, compiler_params=_cparams(("arbitrary",)),
    )(dout, o2[0], o2[1], parts, gw4, dparts)


SSM_GW = SSM_INNER // SSM_G


def _ssd_out(y2, xbc, parts, dskip, nw, tr):
    R = parts.shape[0]

    def body(yf_ref, yb_ref, x_ref, z_ref, ds_ref, w_ref, out_ref):
        ob = (_f32(yf_ref) + _f32(yb_ref) + ds_ref[...] * _f32(x_ref)) * _silu(_f32(z_ref))
        for g in range(SSM_G):
            s = slice(g * SSM_GW, (g + 1) * SSM_GW)
            o = ob[:, s]
            out_ref[:, s] = (o * _rms(o) * w_ref[:, s]).astype(BF16)

    return pl.pallas_call(
        body, name="ssd_out", grid=(R // tr,),
        in_specs=[_row_spec(tr, SSM_INNER)] * 3 + [_row_spec(tr, SSM_INNER, COL_Z // SSM_INNER),
                                                   _vec_spec(SSM_INNER), _vec_spec(SSM_INNER)],
        out_specs=_row_spec(tr, SSM_INNER), out_shape=jax.ShapeDtypeStruct((R, SSM_INNER), BF16),
        compiler_params=_cparams(("arbitrary",)),
    )(y2[0], y2[1], xbc, parts, dskip, nw)


def _ssd_out_bwd(dout, y2, xbc, parts, dskip, nw, dparts, tr):
    R = parts.shape[0]

    def body(d_ref, yf_ref, yb_ref, x_ref, z_ref, ds_ref, w_ref, _, dy_ref, dz_ref, acc_ref):
        _zero_first(acc_ref)
        x, z = _f32(x_ref), _f32(z_ref)
        pre = _f32(yf_ref) + _f32(yb_ref) + ds_ref[...] * x
        sz = _silu(z)
        ob = pre * sz
        dout = _f32(d_ref)
        for g in range(SSM_G):
            s = slice(g * SSM_GW, (g + 1) * SSM_GW)
            o = ob[:, s]
            rstd = _rms(o)
            n = o * rstd
            dob = _rms_bwd(dout[:, s] * w_ref[:, s], n, rstd)
            dz_ref[:, s] = (dob * pre[:, s] * _dsilu(z[:, s])).astype(BF16)
            dy = dob * sz[:, s]
            dy_ref[:, s] = dy.astype(ACT)
            acc_ref[0:1, s] += _colsum(dout[:, s] * n)
            acc_ref[1:2, s] += _colsum(dy * x[:, s])

    return pl.pallas_call(
        body, name="ssd_out_bwd", grid=(R // tr,),
        in_specs=[_row_spec(tr, SSM_INNER)] * 4 + [_row_spec(tr, SSM_INNER, COL_Z // SSM_INNER),
                                                   _vec_spec(SSM_INNER), _vec_spec(SSM_INNER), _ANY],
        out_specs=[_row_spec(tr, SSM_INNER), _dparts_out(tr, SSM_INNER, COL_Z), _acc_spec(SSM_INNER)],
        out_shape=[jax.ShapeDtypeStruct((R, SSM_INNER), ACT), jax.ShapeDtypeStruct(dparts.shape, BF16),
                   jax.ShapeDtypeStruct((8, SSM_INNER), F32)],
        input_output_aliases={7: 1}, compiler_params=_cparams(("arbitrary",)),
    )(dout, y2[0], y2[1], xbc, parts, dskip, nw, dparts)


CONV_W = SSM_INNER + 2 * SSM_G * SSM_N
CONV_BLK = 1024


CONV_SHIFTS = (-2, -1, 1, 2)


def _conv_mask_table(tr):
    t = np.arange(tr)
    table = np.zeros((2, len(CONV_SHIFTS), tr, 128), np.float32)
    for kind, (pos, seg) in enumerate(((t % GRID_W, GRID_W), (t, tr))):
        for k, s in enumerate(CONV_SHIFTS):
            table[kind, k] = ((pos + s >= 0) & (pos + s < seg)).astype(np.float32)[:, None]
    return jnp.asarray(table)


def _shifted(u, s, mask_ref, tr):
    return u if s == 0 else pltpu.roll(u, (-s) % tr, 0) * mask_ref[0, CONV_SHIFTS.index(s)]


def _conv_mask_spec(tr, n_lat_tiles, row_axis):
    return pl.BlockSpec((1, len(CONV_SHIFTS), tr, 128),
                        lambda *ids: (jnp.where(ids[row_axis] >= n_lat_tiles, 1, 0), 0, 0, 0))


def _conv_fwd(parts, cw, cb, n_lat_tiles, tr):
    R = parts.shape[0]

    def body(u_ref, w_ref, b_ref, mask_ref, o_ref):
        def lanes(l, carry):
            sl = pl.ds(pl.multiple_of(l * 128, 128), 128)
            u, w = u_ref[:, sl].astype(F32), w_ref[:, sl]
            acc = jnp.zeros_like(u) + b_ref[:, sl]
            for j in range(SSM_CONV):
                acc = acc + _shifted(u, j - CONV_LEFT, mask_ref, tr) * w[j:j + 1, :]
            o_ref[:, sl] = _silu(acc).astype(ACT)
            return carry

        lax.fori_loop(0, CONV_BLK // 128, lanes, 0)

    return pl.pallas_call(
        body, name="conv_fwd", grid=(R // tr, CONV_W // CONV_BLK),
        in_specs=[pl.BlockSpec((tr, CONV_BLK), lambda i, j: (i, COL_XBC // CONV_BLK + j)),
                  pl.BlockSpec((SSM_CONV, CONV_BLK), lambda i, j: (0, j)), pl.BlockSpec((1, CONV_BLK), lambda i, j: (0, j)),
                  _conv_mask_spec(tr, n_lat_tiles, 0)],
        out_specs=pl.BlockSpec((tr, CONV_BLK), lambda i, j: (i, j)), out_shape=jax.ShapeDtypeStruct((R, CONV_W), ACT),
        compiler_params=_cparams(("arbitrary", "arbitrary")),
    )(parts, cw, cb, _conv_mask_table(tr))


def _conv_bwd(dx, db, dc, dy, dskip, parts, cw, cb, dparts, n_lat_tiles, tr):
    R = parts.shape[0]
    half = CONV_BLK // 2
    n_x = SSM_INNER // CONV_BLK

    def body(dxf_ref, dxb_ref, dy_ref, ds_ref, dbf_ref, dbb_ref, dcf_ref, dcb_ref, u_ref, w_ref, b_ref, _, mask_ref,
             du_ref, acc_ref, d_scr):
        @pl.when(pl.program_id(1) == 0)
        def _():
            acc_ref[...] = jnp.zeros_like(acc_ref)

        @pl.when(pl.program_id(0) < n_x)
        def _():
            d_scr[...] = dxf_ref[...] + dxb_ref[...] + _f32(dy_ref) * ds_ref[...]

        @pl.when(pl.program_id(0) >= n_x)
        def _():
            d_scr[:, 0:half] = dbf_ref[...] + dbb_ref[...]
            d_scr[:, half:] = dcf_ref[...] + dcb_ref[...]

        def lanes(l, carry):
            sl = pl.ds(pl.multiple_of(l * 128, 128), 128)
            u, w = u_ref[:, sl].astype(F32), w_ref[:, sl]
            pre = jnp.zeros_like(u) + b_ref[:, sl]
            taps = []
            for j in range(SSM_CONV):
                tap = _shifted(u, j - CONV_LEFT, mask_ref, tr)
                taps.append(tap)
                pre = pre + tap * w[j:j + 1, :]
            dpre = d_scr[:, sl] * _dsilu(pre)
            du = jnp.zeros_like(u)
            sums = []
            for j in range(SSM_CONV):
                sums.append(_colsum(dpre * taps[j]))
                du = du + _shifted(dpre, CONV_LEFT - j, mask_ref, tr) * w[j:j + 1, :]
            sums += [_colsum(dpre), jnp.zeros((8 - SSM_CONV - 1, 128), F32)]
            acc_ref[:, sl] += jnp.concatenate(sums, axis=0)
            du_ref[:, sl] = du.astype(BF16)
            return carry

        lax.fori_loop(0, CONV_BLK // 128, lanes, 0)

    return pl.pallas_call(
        body, name="conv_bwd", grid=(CONV_W // CONV_BLK, R // tr),
        in_specs=[pl.BlockSpec((tr, CONV_BLK), lambda j, i: (jnp.where(j < n_x, i, 0), jnp.minimum(j, n_x - 1)))] * 3
        + [pl.BlockSpec((1, CONV_BLK), lambda j, i: (0, jnp.minimum(j, n_x - 1)))]
        + [pl.BlockSpec((tr, half), lambda j, i: (jnp.where(j < n_x, 0, i), 0))] * 4
        + [pl.BlockSpec((tr, CONV_BLK), lambda j, i: (i, COL_XBC // CONV_BLK + j)),
           pl.BlockSpec((SSM_CONV, CONV_BLK), lambda j, i: (0, j)), pl.BlockSpec((1, CONV_BLK), lambda j, i: (0, j)), _ANY,
           _conv_mask_spec(tr, n_lat_tiles, 1)],
        out_specs=[pl.BlockSpec((tr, CONV_BLK), lambda j, i: (i, COL_XBC // CONV_BLK + j)),
                   pl.BlockSpec((8, CONV_BLK), lambda j, i: (0, j))],
        out_shape=[jax.ShapeDtypeStruct(dparts.shape, BF16), jax.ShapeDtypeStruct((8, CONV_W), F32)],
        scratch_shapes=[pltpu.VMEM((tr, CONV_BLK), F32)],
        input_output_aliases={11: 0}, compiler_params=_cparams(("arbitrary", "arbitrary")),
    )(*dx, dy, dskip, *db, *dc, parts, cw, cb, dparts, _conv_mask_table(tr))


def _chunk_row_block(d, i, n_lat, n_ctx):
    fwd = jnp.where(i < n_ctx, n_lat + i, i - n_ctx)
    rev = n_lat + n_ctx - 1 - i
    if isinstance(d, int):
        return rev if d else fwd
    return jnp.where(d == 0, fwd, rev)


def _tri(n, d, transpose=False):
    row = lax.broadcasted_iota(jnp.int32, (n, n), 0)
    col = lax.broadcasted_iota(jnp.int32, (n, n), 1)
    diff = (col - row) if transpose else (row - col)
    return diff * (1 - 2 * d) >= 0


def _gla_gates(sm, uhi, ulo, bias, d):
    pre = _nn3(sm, uhi, ulo) + bias
    g = _log_sigmoid(pre) * (1.0 / GLA_TAU)
    mask = _tri(GLA_C, d)
    b = _x_nn(mask.astype(BF16), g)
    b_tot = _colsum(g)
    b_ref = b[GLA_C // 2:GLA_C // 2 + 1, :]
    e_q = jnp.exp(jnp.minimum(b - b_ref, EXP_CLAMP))
    e_k = jnp.exp(jnp.minimum(b_ref - b, EXP_CLAMP))
    return pre, mask, b_tot, e_q, e_k, jnp.exp(b), jnp.exp(b_tot - b)


GLA_QK = GLA_H * GLA_DK
GLA_V = GLA_H * GLA_DV


def _gla_specs(n_lat, n_ctx, step_of):
    rbs = [lambda i, d=d: _chunk_row_block(d, step_of(i), n_lat, n_ctx) for d in range(2)]
    specs = []
    for rb in rbs:
        specs += [pl.BlockSpec((GLA_C, GLA_QK), lambda i, rb=rb: (rb(i), COL_Q // GLA_QK)),
                  pl.BlockSpec((GLA_C, GLA_QK), lambda i, rb=rb: (rb(i), COL_K // GLA_QK)),
                  pl.BlockSpec((GLA_C, GLA_V), lambda i, rb=rb: (rb(i), COL_V // GLA_V)),
                  pl.BlockSpec((GLA_C, 128), lambda i, rb=rb: (rb(i), 0))]
    specs += [pl.BlockSpec((2, 128, GLA_QK), lambda i: (0, 0, 0)), pl.BlockSpec((2, 128, GLA_QK), lambda i: (0, 0, 0)),
              pl.BlockSpec((2, 1, GLA_QK), lambda i: (0, 0, 0))]
    return specs, rbs


def _gla_fwd(parts, sm, uhi, ulo, bias, n_lat, n_ctx):
    R = parts.shape[0]
    n_steps = n_lat + n_ctx
    scale = GLA_DK ** -0.5

    def body(*refs):
        ins, (uhi_ref, ulo_ref, bias_ref), o_refs, hist_ref, st = refs[:8], refs[8:11], refs[11:13], refs[13], refs[14]

        @pl.when(pl.program_id(0) == 0)
        def _():
            st[...] = jnp.zeros_like(st)

        for d in range(2):
            q_ref, k_ref, v_ref, sm_ref = ins[4 * d:4 * d + 4]
            _, mask, b_tot, e_q, e_k, e_in, e_out = _gla_gates(sm_ref[...], uhi_ref[d], ulo_ref[d], bias_ref[d], d)
            q, k, v = _f32(q_ref) * scale, _f32(k_ref), _bf(v_ref[...])
            qb, kb, q_in, k_out, decay = _bf(q * e_q), _bf(k * e_k), _bf(q * e_in), _bf(k * e_out), jnp.exp(b_tot)
            for h in range(GLA_H):
                sk, sv = slice(h * GLA_DK, (h + 1) * GLA_DK), slice(h * GLA_DV, (h + 1) * GLA_DV)
                att = jnp.where(mask, _nt(qb[:, sk], kb[:, sk]), 0.0)
                s_in = st[d, h]
                hist_ref[d, 0, h] = s_in
                o_refs[d][:, sv] = (_nn(_bf(att), v[:, sv]) + _nt(q_in[:, sk], _bf(s_in))).astype(ACT)
                st[d, h] = decay[:, sk] * s_in + _tn(v[:, sv], k_out[:, sk])

    in_specs, rbs = _gla_specs(n_lat, n_ctx, lambda i: i)
    return pl.pallas_call(
        body, name="gla_fwd", grid=(n_steps,), in_specs=in_specs,
        out_specs=[pl.BlockSpec((GLA_C, GLA_V), lambda i, rb=rb: (rb(i), 0)) for rb in rbs]
        + [pl.BlockSpec((2, 1, GLA_H, GLA_DV, GLA_DK), lambda i: (0, i, 0, 0, 0))],
        out_shape=[jax.ShapeDtypeStruct((R, GLA_V), ACT)] * 2 + [jax.ShapeDtypeStruct((2, n_steps, GLA_H, GLA_DV, GLA_DK), F32)],
        scratch_shapes=[pltpu.VMEM((2, GLA_H, GLA_DV, GLA_DK), F32)],
        compiler_params=_cparams(("arbitrary",)),
    )(*([parts, parts, parts, sm] * 2), uhi, ulo, bias)


def _gla_bwd(do, parts, sm, uhi, ulo, bias, hist, n_lat, n_ctx):
    R = parts.shape[0]
    n_steps = n_lat + n_ctx
    scale = GLA_DK ** -0.5
    step_of = lambda j: n_steps - 1 - j

    def body(*refs):
        ins, (uhi_ref, ulo_ref, bias_ref), do_refs, hist_ref = refs[:8], refs[8:11], refs[11:13], refs[13]
        outs, dst = refs[14:22], refs[22]

        @pl.when(pl.program_id(0) == 0)
        def _():
            dst[...] = jnp.zeros_like(dst)

        for d in range(2):
            q_ref, k_ref, v_ref, sm_ref = ins[4 * d:4 * d + 4]
            dq_ref, dk_ref, dv_ref, dp_ref = outs[4 * d:4 * d + 4]
            pre, mask, b_tot, e_q, e_k, e_in, e_out = _gla_gates(sm_ref[...], uhi_ref[d], ulo_ref[d], bias_ref[d], d)
            q, k, v = _f32(q_ref) * scale, _f32(k_ref), _bf(v_ref[...])
            dout = _bf(do_refs[d][...])
            k_out_f = k * e_out
            qb, kb, q_in, k_out, decay = _bf(q * e_q), _bf(k * e_k), _bf(q * e_in), _bf(k_out_f), jnp.exp(b_tot)
            dqs, dks, dk_outs, dss = [], [], [], []
            for h in range(GLA_H):
                sk, sv = slice(h * GLA_DK, (h + 1) * GLA_DK), slice(h * GLA_DV, (h + 1) * GLA_DV)
                s_in, ds = hist_ref[d, 0, h], dst[d, h]
                att = jnp.where(mask, _nt(qb[:, sk], kb[:, sk]), 0.0)
                datt = _bf(jnp.where(mask, _nt(dout[:, sv], v[:, sv]), 0.0))
                dv_ref[:, sv] = (_tn(_bf(att), dout[:, sv]) + _nt(k_out[:, sk], _bf(ds))).astype(ACT)
                dqs.append(_nn(datt, kb[:, sk]) * e_q[:, sk] + _nn(dout[:, sv], _bf(s_in)) * e_in[:, sk])
                dk_o = _nn(v[:, sv], _bf(ds))
                dk_outs.append(dk_o)
                dks.append(_tn(datt, qb[:, sk]) * e_k[:, sk])
                dss.append(_colsum(ds * s_in))
                dst[d, h] = decay[:, sk] * ds + _tn(dout[:, sv], q_in[:, sk])
            dq, dk_out = jnp.concatenate(dqs, axis=1), jnp.concatenate(dk_outs, axis=1)
            dk = jnp.concatenate(dks, axis=1) + dk_out * e_out
            dq_ref[...] = (dq * scale).astype(ACT)
            dk_ref[...] = dk.astype(ACT)
            db_tot = _colsum(dk_out * k_out_f) + decay * jnp.concatenate(dss, axis=1)
            dg = _x_nn(_tri(GLA_C, d, transpose=True).astype(BF16), dq * q - dk * k) + db_tot
            dp_ref[...] = dg * (1.0 / GLA_TAU) * _sigmoid(-pre)

    in_specs, rbs = _gla_specs(n_lat, n_ctx, step_of)
    in_specs += [pl.BlockSpec((GLA_C, GLA_V), lambda j, rb=rb: (rb(j), 0)) for rb in rbs]
    in_specs += [pl.BlockSpec((2, 1, GLA_H, GLA_DV, GLA_DK), lambda j: (0, step_of(j), 0, 0, 0))]
    out_specs, out_shape = [], []
    for rb in rbs:
        for w, dt in ((GLA_QK, ACT), (GLA_QK, ACT), (GLA_V, ACT), (GLA_QK, F32)):
            out_specs.append(pl.BlockSpec((GLA_C, w), lambda j, rb=rb: (rb(j), 0)))
            out_shape.append(jax.ShapeDtypeStruct((R, w), dt))
    outs = pl.pallas_call(
        body, name="gla_bwd", grid=(n_steps,), in_specs=in_specs, out_specs=out_specs, out_shape=out_shape,
        scratch_shapes=[pltpu.VMEM((2, GLA_H, GLA_DV, GLA_DK), F32)],
        compiler_params=_cparams(("arbitrary",)),
    )(*([parts, parts, parts, sm] * 2), uhi, ulo, bias, do, do, hist)
    return [(outs[k], outs[4 + k]) for k in range(4)]


def _ssd_consts(dt_bias, a_log):
    sel = np.zeros((2, SSM_G, 128, 128), np.float32)
    for d, base in enumerate((SM_DTF, SM_DTB)):
        for g in range(SSM_G):
            for e in range(SSM_HPG):
                sel[d, g, base + SSM_HPG * g + e, e] = 1.0
    e512 = np.zeros((128, SSM_GW), np.float32)
    for e in range(SSM_HPG):
        e512[e, SSM_P * e:SSM_P * (e + 1)] = 1.0
    a_neg = -jnp.exp(a_log)
    pad = lambda v: jnp.pad(v.reshape(2, SSM_G, 1, SSM_HPG), ((0, 0), (0, 0), (0, 0), (0, 128 - SSM_HPG)))
    return dict(
        sel=jnp.asarray(sel, BF16), sel_t=jnp.asarray(sel.transpose(0, 1, 3, 2), BF16), e512_t=jnp.asarray(e512.T, BF16),
        dtb=pad(dt_bias), a=pad(a_neg), a512=jnp.repeat(a_neg, SSM_P, axis=1).reshape(2, SSM_G, 1, SSM_GW))


def _head_columns(x8):
    return [jnp.broadcast_to(x8[:, e:e + 1], (x8.shape[0], 128)) for e in range(SSM_HPG)]


def _head_layout(cols):
    low = lax.broadcasted_iota(jnp.int32, (1, 128), 1) < SSM_P
    return jnp.concatenate([jnp.where(low, cols[2 * j], cols[2 * j + 1]) for j in range(SSM_HPG // 2)], axis=1)


def _ssd_common(sm, sel, dtb, a_neg, a512, d):
    dtr8 = _nn_x(sm, sel) + dtb
    dt8 = _softplus(dtr8)
    a8 = a_neg * dt8
    mask = _tri(SSM_C, d)
    mask_t = _tri(SSM_C, d, transpose=True).astype(BF16)
    cum8 = _x_nn(mask.astype(BF16), a8)
    a_hi, a_lo = _split(a8)
    cum_t = _tn(a_hi, mask_t) + _tn(a_lo, mask_t)
    cum_cols = _head_columns(cum8)
    dt_exp = _head_layout(_head_columns(dt8))
    a_exp = a512 * dt_exp
    return dict(dtr8=dtr8, a8=a8, mask=mask, mask_t=mask_t, cum_t=cum_t, dt_exp=dt_exp, a_exp=a_exp,
                cum_exp=_head_layout(cum_cols), cum_cols=cum_cols, tot_exp=_colsum(a_exp))


def _ssd_decay(cm, e):
    diff = cm["cum_cols"][e] - cm["cum_t"][e:e + 1, :]
    return jnp.where(cm["mask"], jnp.exp(jnp.minimum(diff, 0.0)), 0.0)


SSM_GPS = 4


def _ssd_specs(n_lat, n_ctx, step_of):
    rbs = [lambda i, d=d: _chunk_row_block(d, step_of(i), n_lat, n_ctx) for d in range(2)]
    xw, nw = SSM_GPS * SSM_GW, SSM_GPS * SSM_N
    specs = []
    for rb in rbs:
        specs += [pl.BlockSpec((SSM_C, xw), lambda g, i, rb=rb: (rb(i), g)),
                  pl.BlockSpec((SSM_C, nw), lambda g, i, rb=rb: (rb(i), SSM_INNER // nw + g)),
                  pl.BlockSpec((SSM_C, nw), lambda g, i, rb=rb: (rb(i), (SSM_INNER + SSM_G * SSM_N) // nw + g)),
                  pl.BlockSpec((SSM_C, 128), lambda g, i, rb=rb: (rb(i), 0))]
    specs += [pl.BlockSpec((2, SSM_GPS, 128, 128), lambda g, i: (0, g, 0, 0)),
              pl.BlockSpec((2, SSM_GPS, 1, 128), lambda g, i: (0, g, 0, 0)),
              pl.BlockSpec((2, SSM_GPS, 1, 128), lambda g, i: (0, g, 0, 0)),
              pl.BlockSpec((2, SSM_GPS, 1, SSM_GW), lambda g, i: (0, g, 0, 0))]
    return specs, rbs


def _ssd_fwd(xbc, sm, k, n_lat, n_ctx, comm):
    R = xbc.shape[0]
    n_steps = n_lat + n_ctx

    def body(*refs):
        ins, (sel_ref, dtb_ref, a_ref, a512_ref), y_refs, hist_ref, st = refs[:8], refs[8:12], refs[12:14], refs[14], refs[15]

        @pl.when(pl.program_id(1) == 0)
        def _():
            st[...] = jnp.zeros_like(st)

        for d in range(2):
            x_ref, b_ref, c_ref, sm_ref = ins[4 * d:4 * d + 4]
            sm = sm_ref[...]
            for gg in range(SSM_GPS):
                sx, sn = slice(gg * SSM_GW, (gg + 1) * SSM_GW), slice(gg * SSM_N, (gg + 1) * SSM_N)
                cm = _ssd_common(sm, sel_ref[d, gg], dtb_ref[d, gg], a_ref[d, gg], a512_ref[d, gg], d)
                bm, cmat = _bf(b_ref[:, sn]), _bf(c_ref[:, sn])
                xdt = x_ref[:, sx].astype(F32) * cm["dt_exp"]
                cb = _nt(cmat, bm)
                ys = [_nn(_bf(cb * _ssd_decay(cm, e)), _bf(xdt[:, SSM_P * e:SSM_P * (e + 1)])) for e in range(SSM_HPG)]
                s_in = st[d, gg]
                hist_ref[d, 0, gg] = s_in
                y = jnp.concatenate(ys, axis=1) + jnp.exp(cm["cum_exp"]) * _nn(cmat, _bf(s_in))
                y_refs[d][:, sx] = y.astype(ACT)
                st[d, gg] = jnp.exp(cm["tot_exp"]) * s_in + _tn(bm, _bf(xdt * jnp.exp(cm["tot_exp"] - cm["cum_exp"])))

    in_specs, rbs = _ssd_specs(n_lat, n_ctx, lambda i: i)
    out_specs = [pl.BlockSpec((SSM_C, SSM_GPS * SSM_GW), lambda g, i, rb=rb: (rb(i), g)) for rb in rbs]
    out_specs += [pl.BlockSpec((2, 1, SSM_GPS, SSM_N, SSM_GW), lambda g, i: (0, i, g, 0, 0))]
    out_shape = [jax.ShapeDtypeStruct((R, SSM_INNER), ACT)] * 2 + [jax.ShapeDtypeStruct((2, n_steps, SSM_G, SSM_N, SSM_GW), F32)]
    args = [xbc, xbc, xbc, sm] * 2 + [k["sel"], k["dtb"], k["a"], k["a512"]]
    n_host_out = len(out_shape)
    outs = pl.pallas_call(
        _hosted(body, len(args), n_host_out, 1, comm, *_ssd_comm_steps(n_steps)), name="ssd_fwd",
        grid=(SSM_G // SSM_GPS, n_steps), in_specs=in_specs + [_ANY] * len(comm.arrays),
        out_specs=out_specs + [_ANY] * len(comm.out_shape), out_shape=out_shape + comm.out_shape,
        scratch_shapes=[pltpu.VMEM((2, SSM_GPS, SSM_N, SSM_GW), F32)] + comm.scratch,
        compiler_params=_cparams(("arbitrary", "arbitrary")),
    )(*args, *comm.arrays)
    return outs[:n_host_out], outs[n_host_out:]


def _ssd_comm_steps(n_steps):
    n_g = SSM_G // SSM_GPS
    at = lambda g, i: (pl.program_id(0) == g) & (pl.program_id(1) == i)
    half = (n_g // 2, n_steps // 2 if n_g % 2 else 0)
    return (lambda: at(0, 0)), (lambda: at(*half)), (lambda: at(n_g - 1, n_steps - 1))


def _ssd_bwd(dy, xbc, sm, k, hist, n_lat, n_ctx, comm):
    R = xbc.shape[0]
    n_steps = n_lat + n_ctx
    step_of = lambda j: n_steps - 1 - j

    def one(d, gg, x_ref, b_ref, c_ref, sm_ref, sel_ref, dtb_ref, a_ref, a512_ref, selt_ref, e512t_ref, dy_ref,
            hist_ref, dx_ref, db_ref, dc_ref, dsm_ref, acc_ref, dst):
        sx, sn = slice(gg * SSM_GW, (gg + 1) * SSM_GW), slice(gg * SSM_N, (gg + 1) * SSM_N)
        a_neg, e512_t = a_ref[d, gg], e512t_ref[...]
        cm = _ssd_common(sm_ref[...], sel_ref[d, gg], dtb_ref[d, gg], a_neg, a512_ref[d, gg], d)
        x, dyv = x_ref[:, sx].astype(F32), dy_ref[:, sx].astype(F32)
        bm, cmat = _bf(b_ref[:, sn]), _bf(c_ref[:, sn])
        xdt = x * cm["dt_exp"]
        cb = _nt(cmat, bm)
        s_in, ds = hist_ref[d, 0, gg], dst[d, gg]
        w = jnp.exp(cm["tot_exp"] - cm["cum_exp"])
        z = _nn(bm, _bf(ds))
        decay_in = jnp.exp(cm["cum_exp"])
        gy = _bf(dyv * decay_in)
        dcb = jnp.zeros((SSM_C, SSM_C), F32)
        dxs, crossing = [], []
        row = lax.broadcasted_iota(jnp.int32, (SSM_C, SSM_C), 0)
        col = lax.broadcasted_iota(jnp.int32, (SSM_C, SSM_C), 1)
        eye = (row == col).astype(BF16)
        before = (cm["mask_t"] - eye)
        for e in range(SSM_HPG):
            s = slice(SSM_P * e, SSM_P * (e + 1))
            lm = _ssd_decay(cm, e)
            dy_e = _bf(dyv[:, s])
            m_e = cb * lm
            dm_e = _nt(dy_e, _bf(xdt[:, s]))
            dcb = dcb + dm_e * lm
            dxs.append(_tn(_bf(m_e), dy_e))
            crossing.append(_bf(dm_e * m_e))
        through = _nn(jnp.concatenate(crossing, axis=0), before)
        crossing = [_colsum(jnp.where(cm["mask"], through[e * SSM_C:(e + 1) * SSM_C], 0.0)) for e in range(SSM_HPG)]
        da_rows = jnp.concatenate(crossing + [jnp.zeros((128 - SSM_HPG, SSM_C), F32)], axis=0)
        r_hi, r_lo = _split(da_rows)
        da8_intra = _tn(r_hi, eye) + _tn(r_lo, eye)
        dx_state = w * z
        dxdt = jnp.concatenate(dxs, axis=1) + dx_state
        dcb = _bf(dcb)
        c_s = _nn(cmat, _bf(s_in))
        dc_ref[:, sn] = _nn(dcb, bm) + _nt(gy, _bf(s_in))
        db_ref[:, sn] = _tn(dcb, cmat) + _nt(_bf(w * xdt), _bf(ds))
        dst[d, gg] = jnp.exp(cm["tot_exp"]) * ds + _tn(cmat, gy)
        state_path = xdt * dx_state
        per_token = _nn_x(jnp.concatenate([dyv * decay_in * c_s - state_path, dxdt * x], axis=0), e512_t)
        totals = jnp.concatenate([_colsum(state_path), _colsum(ds * s_in), jnp.zeros((6, SSM_GW), F32)], axis=0)
        totals = _nn_x(totals, e512_t)
        tot8 = _colsum(cm["a8"])
        dtot8 = totals[0:1] + jnp.exp(tot8) * totals[1:2]
        da8 = da8_intra + _x_nn(cm["mask_t"], per_token[:SSM_C]) + dtot8
        ddt8 = da8 * a_neg + per_token[SSM_C:]
        dsm_ref[gg] = _nn_x(ddt8 * _sigmoid(cm["dtr8"]), selt_ref[d, gg])
        dx_ref[:, sx] = dxdt * cm["dt_exp"]
        acc_ref[d, gg, 0:1, :] += _colsum(da8 * cm["a8"])

    def body(*refs):
        ins, consts, (selt_ref, e512t_ref), dy_refs, hist_ref = refs[:8], refs[8:12], refs[12:14], refs[14:16], refs[16]
        outs, acc_ref, dst = refs[17:25], refs[25], refs[26]

        @pl.when(pl.program_id(1) == 0)
        def _():
            dst[...] = jnp.zeros_like(dst)
            acc_ref[...] = jnp.zeros_like(acc_ref)

        for d in range(2):
            for gg in range(SSM_GPS):
                one(d, gg, *ins[4 * d:4 * d + 4], *consts, selt_ref, e512t_ref, dy_refs[d], hist_ref,
                    *outs[4 * d:4 * d + 4], acc_ref, dst)

    xw, nw = SSM_GPS * SSM_GW, SSM_GPS * SSM_N
    in_specs, rbs = _ssd_specs(n_lat, n_ctx, step_of)
    in_specs += [pl.BlockSpec((2, SSM_GPS, 128, 128), lambda g, j: (0, g, 0, 0)), pl.BlockSpec((SSM_GW, 128), lambda g, j: (0, 0))]
    in_specs += [pl.BlockSpec((SSM_C, xw), lambda g, j, rb=rb: (rb(j), g)) for rb in rbs]
    in_specs += [pl.BlockSpec((2, 1, SSM_GPS, SSM_N, SSM_GW), lambda g, j: (0, step_of(j), g, 0, 0))]
    out_specs, out_shape = [], []
    for rb in rbs:
        out_specs += [pl.BlockSpec((SSM_C, xw), lambda g, j, rb=rb: (rb(j), g)),
                      pl.BlockSpec((SSM_C, nw), lambda g, j, rb=rb: (rb(j), g)),
                      pl.BlockSpec((SSM_C, nw), lambda g, j, rb=rb: (rb(j), g)),
                      pl.BlockSpec((SSM_GPS, SSM_C, 128), lambda g, j, rb=rb: (g, rb(j), 0))]
        out_shape += [jax.ShapeDtypeStruct((R, SSM_INNER), F32), jax.ShapeDtypeStruct((R, SSM_G * SSM_N), F32),
                      jax.ShapeDtypeStruct((R, SSM_G * SSM_N), F32), jax.ShapeDtypeStruct((SSM_G, R, 128), F32)]
    out_specs.append(pl.BlockSpec((2, SSM_GPS, 8, 128), lambda g, j: (0, g, 0, 0)))
    out_shape.append(jax.ShapeDtypeStruct((2, SSM_G, 8, 128), F32))
    args = [xbc, xbc, xbc, sm] * 2 + [k["sel"], k["dtb"], k["a"], k["a512"], k["sel_t"], k["e512_t"], dy, dy, hist]
    n_host_out = len(out_shape)
    outs = pl.pallas_call(
        _hosted(body, len(args), n_host_out, 1, comm, *_ssd_comm_steps(n_steps)), name="ssd_bwd",
        grid=(SSM_G // SSM_GPS, n_steps), in_specs=in_specs + [_ANY] * len(comm.arrays),
        out_specs=out_specs + [_ANY] * len(comm.out_shape), out_shape=out_shape + comm.out_shape,
        scratch_shapes=[pltpu.VMEM((2, SSM_GPS, SSM_N, SSM_GW), F32)] + comm.scratch,
        compiler_params=_cparams(("arbitrary", "arbitrary")),
    )(*args, *comm.arrays)
    return [(outs[n], outs[4 + n]) for n in range(4)] + [outs[8]], outs[n_host_out:]


def _gla_assemble(dq, dk, dv, dparts, tr):
    R = dq[0].shape[0]
    qk = GLA_H * GLA_DK

    def body(dqf_ref, dqb_ref, dkf_ref, dkb_ref, dvf_ref, dvb_ref, _, o_ref):
        o_ref[:, 0:qk] = (_f32(dqf_ref) + _f32(dqb_ref)).astype(BF16)
        o_ref[:, qk:2 * qk] = (_f32(dkf_ref) + _f32(dkb_ref)).astype(BF16)
        o_ref[:, 2 * qk:] = (_f32(dvf_ref) + _f32(dvb_ref)).astype(BF16)

    return pl.pallas_call(
        body, name="gla_assemble", grid=(R // tr,), in_specs=[_row_spec(tr, qk)] * 4 + [_row_spec(tr, D)] * 2 + [_ANY],
        out_specs=_dparts_out(tr, 2 * D, COL_Q), out_shape=jax.ShapeDtypeStruct(dparts.shape, BF16),
        input_output_aliases={6: 0}, compiler_params=_cparams(("arbitrary",)),
    )(*dq, *dk, *dv, dparts)


def _small_assemble(dp, dsm, sm, ut_hi, ut_lo, dparts, tr):
    R = sm.shape[0]
    qk = GLA_H * GLA_DK

    def body(dpf_ref, dpb_ref, dsmf_ref, dsmb_ref, sm_ref, uth_ref, utl_ref, _, o_ref, dup_ref, acc_ref, acc2_ref):
        @pl.when(pl.program_id(0) == 0)
        def _():
            dup_ref[...] = jnp.zeros_like(dup_ref)
            acc_ref[...] = jnp.zeros_like(acc_ref)
            acc2_ref[...] = jnp.zeros_like(acc2_ref)

        ssd = dsmf_ref[0] + dsmb_ref[0]
        for g in range(1, SSM_G):
            ssd = ssd + (dsmf_ref[g] + dsmb_ref[g])
        acc2_ref[0:1, :] += _colsum(ssd)
        sm_hi, sm_lo = _split(sm_ref[...])
        out = ssd
        for d, dp_ref in enumerate((dpf_ref, dpb_ref)):
            dpd = dp_ref[...]
            out = out + _nn3(dpd, uth_ref[d], utl_ref[d])
            p_hi, p_lo = _split(dpd)
            dup_ref[d] += _tn(sm_hi, p_hi) + _tn(sm_lo, p_hi) + _tn(sm_hi, p_lo)
            acc_ref[d:d + 1, :] += _colsum(dpd)
        o_ref[...] = out.astype(BF16)

    return pl.pallas_call(
        body, name="small_assemble", grid=(R // tr,),
        in_specs=[_row_spec(tr, qk)] * 2 + [pl.BlockSpec((SSM_G, tr, 128), lambda i: (0, i, 0))] * 2
        + [_row_spec(tr, 128), pl.BlockSpec((2, qk, 128), lambda i: (0, 0, 0)),
           pl.BlockSpec((2, qk, 128), lambda i: (0, 0, 0)), _ANY],
        out_specs=[_dparts_out(tr, 128, COL_SM), pl.BlockSpec((2, 128, qk), lambda i: (0, 0, 0)), _acc_spec(qk), _acc_spec(128)],
        out_shape=[jax.ShapeDtypeStruct(dparts.shape, BF16), jax.ShapeDtypeStruct((2, 128, qk), F32),
                   jax.ShapeDtypeStruct((8, qk), F32), jax.ShapeDtypeStruct((8, 128), F32)],
        input_output_aliases={7: 0}, compiler_params=_cparams(("arbitrary",)),
    )(*dp, *dsm, sm, ut_hi, ut_lo, dparts)


ADA_ROWS = 16
ADA_TILE = 512


def _dot3_f32(a, b, ca, cb):
    a_hi, a_lo = _split(a)
    b_hi, b_lo = _split(b)
    return _dg(a_hi, b_hi, ca, cb) + _dg(a_lo, b_hi, ca, cb) + _dg(a_hi, b_lo, ca, cb)


def _ada_fwd(cvec, w, b):
    cols = w.shape[1]

    def body(c_ref, w_ref, b_ref, o_ref):
        o_ref[...] = _dot3_f32(_silu(c_ref[...]), w_ref[...], 1, 0) + b_ref[...]

    return pl.pallas_call(
        body, name="ada_fwd", grid=(cols // ADA_TILE,),
        in_specs=[pl.BlockSpec((ADA_ROWS, D), lambda j: (0, 0)), pl.BlockSpec((D, ADA_TILE), lambda j: (0, j)),
                  pl.BlockSpec((1, ADA_TILE), lambda j: (0, j))],
        out_specs=pl.BlockSpec((ADA_ROWS, ADA_TILE), lambda j: (0, j)), out_shape=jax.ShapeDtypeStruct((ADA_ROWS, cols), F32),
        compiler_params=_cparams(("arbitrary",)),
    )(cvec, w, b)


def _adam(w, g, m, v):
    m2 = ADAM_B1 * m + (1.0 - ADAM_B1) * g
    v2 = ADAM_B2 * v + (1.0 - ADAM_B2) * (g * g)
    m_hat = m2 / (1.0 - ADAM_B1 ** ADAM_STEP)
    v_hat = v2 / (1.0 - ADAM_B2 ** ADAM_STEP)
    return -ADAM_LR * (m_hat / (jnp.sqrt(v_hat) + ADAM_EPS) + ADAM_WD * w), m2, v2


def _wada_bwd_adam(cvec, dada, w, m, v):
    rows, cols = w.shape
    tr = _tile(rows, 256, 128)

    def body(c_ref, d_ref, w_ref, m_ref, v_ref, g_ref, dl_ref, m2_ref, v2_ref, p_ref):
        wv = w_ref[...]
        g = _dot3_f32(_silu(c_ref[...]), d_ref[...], 0, 0)
        g_ref[...] = g
        dl_ref[...], m2_ref[...], v2_ref[...] = _adam(wv, g, m_ref[...], v_ref[...])
        p_ref[...] = _dot3_f32(d_ref[...], wv, 1, 1)

    blk = pl.BlockSpec((tr, cols), lambda i: (i, 0))
    return pl.pallas_call(
        body, name="wada_bwd_adam", grid=(rows // tr,),
        in_specs=[pl.BlockSpec((ADA_ROWS, tr), lambda i: (0, i)), pl.BlockSpec((ADA_ROWS, cols), lambda i: (0, 0)), blk, blk, blk],
        out_specs=[blk, blk, blk, blk, pl.BlockSpec((ADA_ROWS, tr), lambda i: (0, i))],
        out_shape=[jax.ShapeDtypeStruct((rows, cols), F32)] * 4 + [jax.ShapeDtypeStruct((ADA_ROWS, rows), F32)],
        compiler_params=_cparams(("arbitrary",)),
    )(cvec, dada, w, m, v)


def _reduce_adam(parts8, w, m, v, name):
    rows, cols = w.shape
    tr = _tile(rows, 64, 16)

    def body(p_ref, w_ref, m_ref, v_ref, g_ref, dl_ref, m2_ref, v2_ref):
        g = p_ref[0].astype(F32) + p_ref[N_CHIPS].astype(F32)
        for j in range(1, N_CHIPS):
            g = g + (p_ref[j].astype(F32) + p_ref[N_CHIPS + j].astype(F32))
        g_ref[...] = g
        dl_ref[...], m2_ref[...], v2_ref[...] = _adam(w_ref[...], g, m_ref[...], v_ref[...])

    blk = pl.BlockSpec((tr, cols), lambda i: (i, 0))
    return pl.pallas_call(
        body, name=name, grid=(rows // tr,), in_specs=[pl.BlockSpec((N_DEV, tr, cols), lambda i: (0, i, 0)), blk, blk, blk],
        out_specs=[blk] * 4, out_shape=[jax.ShapeDtypeStruct((rows, cols), F32)] * 4, compiler_params=_cparams(("arbitrary",)),
    )(parts8, w, m, v)


SMALL_W = 1024


def _sum8(g8):
    rows = g8.shape[1]

    def body(g_ref, o_ref):
        s = g_ref[0]
        for j in range(1, N_DEV):
            s = s + g_ref[j]
        o_ref[...] = s

    return pl.pallas_call(
        body, name="sum8", out_shape=jax.ShapeDtypeStruct((rows, SMALL_W), F32),
        in_specs=[pl.BlockSpec(memory_space=pltpu.VMEM)], out_specs=pl.BlockSpec(memory_space=pltpu.VMEM),
        compiler_params=pltpu.CompilerParams(vmem_limit_bytes=VMEM_LIMIT),
    )(g8)


def _cctx_grad(p8, c_ctx):
    def body(p_ref, c_ref, o_ref):
        s = p_ref[0]
        for chip in range(1, N_CHIPS):
            s = s + p_ref[2 * chip]
        o_ref[...] = s * _dsilu(c_ref[...])

    return pl.pallas_call(
        body, name="cctx_grad", out_shape=jax.ShapeDtypeStruct((1, D), F32),
        in_specs=[pl.BlockSpec(memory_space=pltpu.VMEM)] * 2, out_specs=pl.BlockSpec(memory_space=pltpu.VMEM),
    )(p8, c_ctx)


def _adam_small(w, g, m, v):
    def body(w_ref, g_ref, m_ref, v_ref, dl_ref, m2_ref, v2_ref):
        dl_ref[...], m2_ref[...], v2_ref[...] = _adam(w_ref[...], g_ref[...], m_ref[...], v_ref[...])

    vm = pl.BlockSpec(memory_space=pltpu.VMEM)
    return pl.pallas_call(
        body, name="adam_small", out_shape=[jax.ShapeDtypeStruct(w.shape, F32)] * 3, in_specs=[vm] * 4, out_specs=[vm] * 3,
        compiler_params=pltpu.CompilerParams(vmem_limit_bytes=VMEM_LIMIT),
    )(w, g, m, v)


def _pack(vecs, width=SMALL_W, row_mult=8):
    flat = jnp.concatenate([v.reshape(-1).astype(F32) for v in vecs])
    n = flat.shape[0]
    rows = -(-n // (width * row_mult)) * row_mult
    return jnp.pad(flat, (0, rows * width - n)).reshape(rows, width)


def _unpack(packed, shapes):
    flat = packed.reshape(-1)
    out, off = [], 0
    for s in shapes:
        n = int(np.prod(s))
        out.append(flat[off:off + n].reshape(s))
        off += n
    return out


WEIGHTS = ('c_ctx', 'w_ada', 'b_ada', 'norm1_w', 'w_in', 'gla_up_f', 'gla_bias_f', 'gla_up_b', 'gla_bias_b', 'gla_norm_w',
           'conv_w', 'conv_b', 'dt_bias_f', 'dt_bias_b', 'a_log_f', 'a_log_b', 'd_skip', 'ssm_norm_w', 'w_pa', 'w_pb', 'w_out',
           'norm2_w', 'w_gate', 'w_up', 'w_down', 'final_norm_w')
BIG = ('w_in', 'w_pa', 'w_pb', 'w_out', 'w_gate', 'w_up', 'w_down')
COL_SHARDED = ('w_in', 'w_gate', 'w_up')
SMALL_SHARDED = ('gla_up_f', 'gla_up_b', 'conv_w')
ROW_TILE = 256
MM_ROWS = 1408
MM_ROWS_SMALL = 768
MM_KROWS = 2816


def _blocks_to_full(g4, name):
    n, r, c = g4.shape
    return g4.transpose(1, 0, 2).reshape(r, n * c) if name in COL_SHARDED else g4.reshape(n * r, c)


def _full_to_blocks(full, name):
    r, c = full.shape
    if name in COL_SHARDED:
        return full.reshape(r, N_CHIPS, c // N_CHIPS).transpose(1, 0, 2)
    return full.reshape(N_CHIPS, r // N_CHIPS, c)


def _permute_in(w_in_full):
    off = np.concatenate([[0], np.cumsum(IN_WIDTHS)])
    cols = [w_in_full[:, off[p]:off[p + 1]] for p in PERM]
    return jnp.concatenate(cols + [jnp.zeros((w_in_full.shape[0], SMALL_PAD), w_in_full.dtype)], axis=1)


def _unpermute_in(wp):
    off = np.concatenate([[0], np.cumsum([IN_WIDTHS[p] for p in PERM])])
    pieces = {p: wp[:, off[i]:off[i + 1]] for i, p in enumerate(PERM)}
    return jnp.concatenate([pieces[p] for p in range(len(IN_WIDTHS))], axis=1)


def _chip_cols(full, chip, n):
    return lax.dynamic_slice_in_dim(full, chip * n, n, axis=1)


def kernel(x, c, ctx, c_ctx, w_ada, b_ada, norm1_w, w_in, gla_up_f, gla_bias_f, gla_up_b, gla_bias_b, gla_norm_w, conv_w, conv_b, dt_bias_f, dt_bias_b, a_log_f, a_log_b, d_skip, ssm_norm_w, w_pa, w_pb, w_out, norm2_w, w_gate, w_up, w_down, final_norm_w, loss_target, m_c_ctx, m_w_ada, m_b_ada, m_norm1_w, m_w_in, m_gla_up_f, m_gla_bias_f, m_gla_up_b, m_gla_bias_b, m_gla_norm_w, m_conv_w, m_conv_b, m_dt_bias_f, m_dt_bias_b, m_a_log_f, m_a_log_b, m_d_skip, m_ssm_norm_w, m_w_pa, m_w_pb, m_w_out, m_norm2_w, m_w_gate, m_w_up, m_w_down, m_final_norm_w, v_c_ctx, v_w_ada, v_b_ada, v_norm1_w, v_w_in, v_gla_up_f, v_gla_bias_f, v_gla_up_b, v_gla_bias_b, v_gla_norm_w, v_conv_w, v_conv_b, v_dt_bias_f, v_dt_bias_b, v_a_log_f, v_a_log_b, v_d_skip, v_ssm_norm_w, v_w_pa, v_w_pb, v_w_out, v_norm2_w, v_w_gate, v_w_up, v_w_down, v_final_norm_w):
    given = dict(locals())
    W = {n: given[n] for n in WEIGHTS}
    M = {n: given["m_" + n] for n in WEIGHTS}
    V = {n: given["v_" + n] for n in WEIGHTS}
    L, Lc = x.shape[1], ctx.shape[1]
    tr = ROW_TILE
    assert L % tr == 0 and Lc % tr == 0 and L % Lc == 0 and Lc % SSM_C == 0
    n_lat_tiles = L // tr
    xi, yi, ci = _place()
    chip, me = 2 * xi + yi, 4 * xi + 2 * yi + ci
    x2, ctx2 = x[0], ctx[0]

    g0 = _allgather_small(_pack([c[0]] + [W[n][0] for n in SMALL_SHARDED]), "gather_c")
    g0 = g0.reshape(N_DEV, -1)
    c_all = g0[:, :D]
    small_full, off = {}, D
    for n in SMALL_SHARDED:
        r, cols = W[n].shape[1:]
        small_full[n] = jnp.concatenate([g0[2 * k, off:off + r * cols].reshape(r, cols) for k in range(N_CHIPS)], axis=1)
        off += r * cols
    up_f, up_b, conv_w_full = (small_full[n] for n in SMALL_SHARDED)

    cvec = jnp.zeros((ADA_ROWS, D), F32).at[:N_DEV].set(c_all).at[N_DEV].set(c_ctx)
    ada_cols = w_ada.shape[2]
    ada_part = _ada_fwd(cvec, w_ada[0], _chip_cols(b_ada, chip, ada_cols))
    g1_all = _allgather_small(ada_part, "gather_ada")
    ada_full = jnp.concatenate([g1_all[2 * k] for k in range(N_CHIPS)], axis=1)
    mine = lax.dynamic_slice_in_dim(ada_full, me, 1, axis=0)
    sh1, sc1, g1, sh2, sc2, g2 = (mine[:, k * D:(k + 1) * D] for k in range(6))
    csh1, csc1 = ada_full[N_DEV:N_DEV + 1, :D], ada_full[N_DEV:N_DEV + 1, D:2 * D]
    mod = jnp.stack([jnp.stack([sh1, sc1]), jnp.stack([csh1, csc1])])

    full = {'w_in': _blocks_to_full(_gather_split(w_in[0].astype(BF16), "gather_w_in"), 'w_in')}
    wp = _permute_in(full['w_in'])
    later = [n for n in BIG if n != 'w_in']

    def lr_rows(up, base):
        return jnp.zeros((128, GLA_H * GLA_DK), F32).at[base:base + GLA_RANK].set(up)
    u2 = jnp.stack([lr_rows(up_f, SM_LRF), lr_rows(up_b, SM_LRB)])
    u2_hi = u2.astype(BF16)
    u2_lo = (u2 - u2_hi.astype(F32)).astype(BF16)
    ut = u2.transpose(0, 2, 1)
    ut_hi = ut.astype(BF16)
    ut_lo = (ut - ut_hi.astype(F32)).astype(BF16)
    gbias = jnp.stack([gla_bias_f, gla_bias_b])
    kc = _ssd_consts(jnp.stack([dt_bias_f[0], dt_bias_b[0]]), jnp.stack([a_log_f[0], a_log_b[0]]))
    gw4 = jnp.tile(gla_norm_w, (1, GLA_H))
    dskip_exp = jnp.repeat(d_skip, SSM_P, axis=1)
    n_gla = (L // GLA_C, Lc // GLA_C)
    n_ssd = (L // SSM_C, Lc // SSM_C)

    h1 = _norm_mod(x2, ctx2, norm1_w, mod, n_lat_tiles, tr)
    parts = _mm(h1, wp, "nn", ACT, "mm_in", tm=MM_ROWS, tn=PW // 3)
    sm = _mm(h1, wp[:, COL_SM:], "nn", F32, "mm_in_small", tm=MM_ROWS)
    xbc = _conv_fwd(parts, conv_w_full, conv_b, L // Lc, Lc)
    *o2, gla_hist = _gla_fwd(parts, sm, u2_hi, u2_lo, gbias, *n_gla)
    (*y2, ssd_hist), gathered = _ssd_fwd(xbc, sm, kc, *n_ssd, _gather_comm([W[n][0].astype(BF16) for n in later]))
    full.update({n: _blocks_to_full(g, n) for n, g in zip(later, gathered)})
    oan = _gla_out(o2, parts, gw4, tr)
    obn = _ssd_out(y2, xbc, parts, dskip_exp, ssm_norm_w, tr)
    ya = _mm(oan, full['w_pa'], "nn", ACT, "mm_pa", tm=MM_ROWS)
    yb = _mm(obn, full['w_pb'], "nn", ACT, "mm_pb", tm=MM_ROWS)
    merged = _merge(ya, yb, parts, tr)
    mix = _mm(merged, full['w_out'], "nn", ACT, "mm_out", tm=MM_ROWS)
    h2, u = _resid_norm_mod(x2, ctx2, mix, g1, norm2_w, sh2, sc2, n_lat_tiles, tr)
    gp = _mm(u, full['w_gate'], "nn", ACT, "mm_gate", tm=MM_ROWS, tn=D_FF)
    up = _mm(u, full['w_up'], "nn", ACT, "mm_up", tm=MM_ROWS, tn=D_FF)
    f = _mm(gp, full['w_down'], "nn", ACT, "mm_down", tm=MM_ROWS_SMALL, tk=D_FF // 2, swiglu_up=up)
    dh3, df, acc_loss = _loss_head(h2, f, loss_target[0], g2, final_norm_w[None], n_lat_tiles, tr)

    dw = {}
    da = _mm(df, full['w_down'], "nt", ACT, "mm_down_dx", tm=MM_ROWS, tn=D_FF)
    dw['w_down'] = _mm(gp, df, "tn", BF16, "mm_down_dw", tm=D_FF // 2, tk=MM_ROWS, swiglu_up=up)
    dgp, dup = _swiglu_act_bwd(da, gp, up, tr)
    du_a = _mm(dgp, full['w_gate'], "nt", ACT, "mm_gate_dx", tm=MM_ROWS, tk=D_FF)
    du_b = _mm(dup, full['w_up'], "nt", ACT, "mm_up_dx", tm=MM_ROWS, tk=D_FF)
    dw['w_gate'] = _mm(u, dgp, "tn", BF16, "mm_gate_dw", tm=D, tn=D_FF // 2, tk=MM_KROWS)
    dw['w_up'] = _mm(u, dup, "tn", BF16, "mm_up_dw", tm=D, tn=D_FF // 2, tk=MM_KROWS)
    dh2, dmix, acc_ffn = _ffn_in_bwd(du_a, du_b, h2, dh3, mix, sc2, g1, norm2_w, tr)
    dmerged = _mm(dmix, full['w_out'], "nt", ACT, "mm_out_dx", tm=MM_ROWS)
    dw['w_out'] = _mm(merged, dmix, "tn", BF16, "mm_out_dw", tm=D, tk=MM_KROWS)
    dya, dyb, dparts = _merge_bwd(dmerged, ya, yb, parts, lax.empty((L + Lc, PW), BF16), tr)
    doan = _mm(dya, full['w_pa'], "nt", ACT, "mm_pa_dx", tm=MM_ROWS)
    dw['w_pa'] = _mm(oan, dya, "tn", BF16, "mm_pa_dw", tm=D, tk=MM_KROWS)
    dobn = _mm(dyb, full['w_pb'], "nt", ACT, "mm_pb_dx", tm=MM_ROWS)
    dw['w_pb'] = _mm(obn, dyb, "tn", BF16, "mm_pb_dw", tm=D, tk=MM_KROWS)
    do, dparts, acc_gla = _gla_out_bwd(doan, o2, parts, gw4, dparts, tr)
    dq, dk, dv, dpre = _gla_bwd(do, parts, sm, u2_hi, u2_lo, gbias, gla_hist, *n_gla)
    dy, dparts, acc_ssd = _ssd_out_bwd(dobn, y2, xbc, parts, dskip_exp, ssm_norm_w, dparts, tr)
    (dx_scan, db_scan, dc_scan, dsm, acc_alog), exchanged = _ssd_bwd(
        dy, xbc, sm, kc, ssd_hist, *n_ssd, _exchange_comm([_full_to_blocks(dw[n], n) for n in later]))
    exchanged = dict(zip(later, exchanged))
    dparts, acc_conv = _conv_bwd(dx_scan, db_scan, dc_scan, dy, dskip_exp, parts, conv_w_full, conv_b, dparts, L // Lc, Lc)
    dparts = _gla_assemble(dq, dk, dv, dparts, tr)
    dparts, dup_gla, acc_gbias, acc_dtb = _small_assemble(dpre, dsm, sm, ut_hi, ut_lo, dparts, tr)
    dw['w_in'] = _unpermute_in(_mm(h1, dparts, "tn", BF16, "mm_in_dw", tm=D, tn=PW // 9, tk=MM_KROWS))
    dh1, (exchanged['w_in'],) = _mm(dparts, wp, "nt", F32, "mm_in_dx", tm=MM_ROWS_SMALL, tk=PW // 3,
                                    comm=_exchange_comm([_full_to_blocks(dw['w_in'], 'w_in')]))
    dx, acc_n1 = _norm1_bwd(dh1, x2, ctx2, dh2, norm1_w, mod, n_lat_tiles, tr)

    partial = dict(
        norm1_w=acc_n1[0, 2] + acc_n1[1, 2],
        gla_up_f=dup_gla[0, SM_LRF:SM_LRF + GLA_RANK], gla_bias_f=acc_gbias[0],
        gla_up_b=dup_gla[1, SM_LRB:SM_LRB + GLA_RANK], gla_bias_b=acc_gbias[1],
        gla_norm_w=acc_gla[0].reshape(GLA_H, GLA_DV).sum(0),
        conv_w=acc_conv[:SSM_CONV], conv_b=acc_conv[SSM_CONV],
        dt_bias_f=acc_dtb[0, SM_DTF:SM_DTF + SSM_HEADS], dt_bias_b=acc_dtb[0, SM_DTB:SM_DTB + SSM_HEADS],
        a_log_f=acc_alog[0, :, 0, :SSM_HPG], a_log_b=acc_alog[1, :, 0, :SSM_HPG],
        d_skip=acc_ssd[1].reshape(SSM_HEADS, SSM_P).sum(1), ssm_norm_w=acc_ssd[0],
        norm2_w=acc_ffn[2], final_norm_w=acc_loss[0],
    )
    dada = jnp.concatenate([acc_n1[0, 1], acc_n1[0, 0], acc_ffn[3], acc_ffn[1], acc_ffn[0], acc_loss[1]])
    dada_ctx = jnp.concatenate([acc_n1[1, 1], acc_n1[1, 0], jnp.zeros((4 * D,), F32)])
    names = list(partial)
    payload = [partial[n] for n in names] + [dada + dada_ctx, dada_ctx, acc_loss[2], dada]
    sizes = [int(np.prod(p.shape)) for p in payload]
    g8 = _allgather_small(_pack(payload), "gather_small_grads")
    summed = _unpack(_sum8(g8), [(s,) for s in sizes])
    grads = {n: s.reshape(W[n].shape if n not in SMALL_SHARDED else partial[n].shape) for n, s in zip(names, summed)}
    grads['b_ada'] = summed[len(names)].reshape(b_ada.shape)
    dada_ctx_sum = summed[len(names) + 1]
    loss = jnp.sum(summed[len(names) + 2])
    dada_all = g8.reshape(N_DEV, -1)[:, sum(sizes[:-1]):sum(sizes)]

    dada16 = jnp.zeros((ADA_ROWS, ada_cols), F32)
    dada16 = dada16.at[:N_DEV].set(_chip_cols(dada_all, chip, ada_cols)).at[N_DEV].set(_chip_cols(dada_ctx_sum[None], chip, ada_cols)[0])
    g_wada, dl_wada, m_wada, v_wada, p16 = _wada_bwd_adam(cvec, dada16, w_ada[0], m_w_ada[0], v_w_ada[0])
    p8 = _allgather_small(p16[N_DEV:], "gather_cctx")
    grads['c_ctx'] = _cctx_grad(p8[:, 0:1, :], c_ctx[None])[0]
    for n in SMALL_SHARDED:
        grads[n] = _chip_cols(grads[n], chip, W[n].shape[2])[None]

    small = [n for n in WEIGHTS if n not in BIG and n != 'w_ada']
    shapes = [W[n].shape for n in small]
    dl_s, m_s, v_s = _adam_small(*[_pack([d[n] for n in small]) for d in (W, grads, M, V)])
    delta = dict(zip(small, _unpack(dl_s, shapes)))
    new_m = dict(zip(small, _unpack(m_s, shapes)))
    new_v = dict(zip(small, _unpack(v_s, shapes)))
    grads['w_ada'], delta['w_ada'], new_m['w_ada'], new_v['w_ada'] = g_wada[None], dl_wada[None], m_wada[None], v_wada[None]

    for n in BIG:
        g, dl, m2, v2 = _reduce_adam(exchanged[n], W[n][0], M[n][0], V[n][0], "adam_" + n)
        grads[n], delta[n], new_m[n], new_v[n] = g[None], dl[None], m2[None], v2[None]

    return (loss, dx[None], *[grads[n] for n in WEIGHTS], *[delta[n] for n in WEIGHTS],
            *[new_m[n] for n in WEIGHTS], *[new_v[n] for n in WEIGHTS])
```

```python
import numpy as np
import jax
import jax.numpy as jnp
from jax import lax
from jax.experimental import pallas as pl
from jax.experimental.pallas import tpu as pltpu

F32 = jnp.float32
BF16 = jnp.bfloat16
ACT = BF16
MESH = pl.DeviceIdType.MESH

D = 1024
EPS = 1e-6
GRID_W = 64
GLA_H, GLA_DK, GLA_DV, GLA_RANK, GLA_TAU = 4, 128, 256, 16, 16.0
GLA_C = 128
SSM_INNER, SSM_P, SSM_HEADS, SSM_G, SSM_HPG, SSM_N = 2048, 64, 32, 4, 8, 128
SSM_C = 128
SSM_CONV, CONV_LEFT = 4, 2
D_FF = 2816
IN_WIDTHS = (512, 512, 1024, 1024, 16, 16, 2048, 2048, 512, 512, 32, 32, 1024, 1024)
D_IN = sum(IN_WIDTHS)
PERM = (6, 7, 8, 9, 3, 0, 1, 2, 12, 13, 4, 5, 10, 11)
PW = 10368
SMALL_PAD = PW - D_IN
COL_Z, COL_XBC, COL_R, COL_Q, COL_K, COL_V, COL_GA, COL_GB, COL_SM = 0, 2048, 5120, 6144, 6656, 7168, 8192, 9216, 10240
SM_LRF, SM_LRB, SM_DTF, SM_DTB = 0, 16, 32, 64
EXP_CLAMP = 80.0
ADAM_LR, ADAM_B1, ADAM_B2, ADAM_EPS, ADAM_WD, ADAM_STEP = 0.001, 0.9, 0.999, 1e-08, 0.01, 10
N_CHIPS, N_DEV = 4, 8
VMEM_LIMIT = 56 * 1024 * 1024


def _cparams(sem=None):
    return pltpu.CompilerParams(dimension_semantics=sem, vmem_limit_bytes=VMEM_LIMIT)


def _dg(a, b, ca, cb):
    return lax.dot_general(a, b, (((ca,), (cb,)), ((), ())), preferred_element_type=F32)


def _nn(a, b):
    return _dg(a, b, 1, 0)


def _nt(a, b):
    return _dg(a, b, 1, 1)


def _tn(a, b):
    return _dg(a, b, 0, 0)


def _bf(x):
    return x.astype(BF16)


def _f32(ref):
    return ref[...].astype(F32)


def _split(x):
    hi = x.astype(BF16)
    return hi, (x - hi.astype(F32)).astype(BF16)


def _nn_x(a, b_exact):
    hi, lo = _split(a)
    return _nn(hi, b_exact) + _nn(lo, b_exact)


def _x_nn(a_exact, b):
    hi, lo = _split(b)
    return _nn(a_exact, hi) + _nn(a_exact, lo)


def _nn3(a, b_hi, b_lo):
    hi, lo = _split(a)
    return _nn(hi, b_hi) + _nn(lo, b_hi) + _nn(hi, b_lo)


def _sigmoid(x):
    return 1.0 / (1.0 + jnp.exp(-x))


def _silu(x):
    return x * _sigmoid(x)


def _dsilu(x):
    s = _sigmoid(x)
    return s * (1.0 + x * (1.0 - s))


def _softplus(x):
    return jnp.maximum(x, 0.0) + jnp.log(1.0 + jnp.exp(-jnp.abs(x)))


def _log_sigmoid(x):
    return jnp.minimum(x, 0.0) - jnp.log(1.0 + jnp.exp(-jnp.abs(x)))


def _tile(n, target, mult=8):
    best = None
    for t in range(mult, min(n, target) + 1, mult):
        if n % t == 0:
            best = t
    assert best is not None, (n, target, mult)
    return best


def _mm(a, b, mode, out_dtype, name, tm=512, tn=1024, tk=2048, comm=None, swiglu_up=None):
    if mode == "nn":
        (M, K), N = a.shape, b.shape[1]
    elif mode == "nt":
        (M, K), N = a.shape, b.shape[0]
    else:
        (K, M), N = a.shape, b.shape[1]
    tm, tn, tk = _tile(M, tm, 128), _tile(N, tn, 128), _tile(K, tk, 128)
    nk = K // tk
    ca, cb = {"nn": (1, 0), "nt": (1, 1), "tn": (0, 0)}[mode]

    def body(a_ref, *rest):
        if swiglu_up is None:
            (b_ref, o_ref, *acc), av = rest, a_ref[...]
        else:
            (up_ref, b_ref, o_ref, *acc) = rest
            av = (_silu(a_ref[...].astype(F32)) * up_ref[...].astype(F32)).astype(BF16)
        part = _dg(av, b_ref[...], ca, cb)
        if nk == 1:
            o_ref[...] = part.astype(out_dtype)
        else:
            k = pl.program_id(2)

            @pl.when(k == 0)
            def _():
                acc[0][...] = part

            @pl.when(k > 0)
            def _():
                acc[0][...] += part

            @pl.when(k == nk - 1)
            def _():
                o_ref[...] = acc[0][...].astype(out_dtype)

    a_spec = pl.BlockSpec((tk, tm), lambda i, j, k: (k, i)) if mode == "tn" else pl.BlockSpec((tm, tk), lambda i, j, k: (i, k))
    b_spec = pl.BlockSpec((tn, tk), lambda i, j, k: (j, k)) if mode == "nt" else pl.BlockSpec((tk, tn), lambda i, j, k: (k, j))
    gi, gj = M // tm, N // tn
    scratch = [pltpu.VMEM((tm, tn), F32)] if nk > 1 else []
    out_spec, out_shape = pl.BlockSpec((tm, tn), lambda i, j, k: (i, j)), jax.ShapeDtypeStruct((M, N), out_dtype)
    if comm is None:
        lhs = [a] if swiglu_up is None else [a, swiglu_up]
        return pl.pallas_call(
            body, name=name, grid=(gi, gj, nk), in_specs=[a_spec] * len(lhs) + [b_spec], out_specs=out_spec,
            out_shape=out_shape, scratch_shapes=scratch, compiler_params=_cparams(("arbitrary", "arbitrary", "arbitrary")),
        )(*lhs, b)
    assert swiglu_up is None
    at = lambda i, j, k: (pl.program_id(0) == i) & (pl.program_id(1) == j) & (pl.program_id(2) == k)
    hosted = _hosted(body, 2, 1, len(scratch), comm, lambda: at(0, 0, 0), lambda: at(gi - 1, 0, 0),
                     lambda: at(gi - 1, gj - 1, nk - 1))
    outs = pl.pallas_call(
        hosted, name=name, grid=(gi, gj, nk), in_specs=[a_spec, b_spec] + [_ANY] * len(comm.arrays),
        out_specs=[out_spec] + [_ANY] * len(comm.out_shape), out_shape=[out_shape] + comm.out_shape,
        scratch_shapes=scratch + comm.scratch, compiler_params=_cparams(("arbitrary", "arbitrary", "arbitrary")),
    )(a, b, *comm.arrays)
    return outs[0], outs[1:]


def _place():
    return lax.axis_index("x"), lax.axis_index("y"), lax.axis_index("c")


def _flip(v, bit):
    return 1 - v if bit else v


def _allgather_small(v, name):
    R, C = v.shape

    def body(v_ref, out_ref, send_sems, recv_sems, local_sem):
        x, y, c = _place()
        me = 4 * x + 2 * y + c
        mine = pltpu.make_async_copy(v_ref, out_ref.at[me], local_sem)
        mine.start()

        def peer(r):
            return _flip(x, (r >> 2) & 1), _flip(y, (r >> 1) & 1), _flip(c, r & 1)

        sends = [pltpu.make_async_remote_copy(
            src_ref=v_ref, dst_ref=out_ref.at[me], send_sem=send_sems.at[r - 1], recv_sem=recv_sems.at[r - 1],
            device_id=peer(r), device_id_type=MESH) for r in range(1, N_DEV)]
        for cp in sends:
            cp.start()
        for r in range(1, N_DEV):
            px, py, pc = peer(r)
            pltpu.make_async_remote_copy(
                src_ref=v_ref, dst_ref=out_ref.at[4 * px + 2 * py + pc], send_sem=send_sems.at[r - 1],
                recv_sem=recv_sems.at[r - 1], device_id=(x, y, c), device_id_type=MESH).wait_recv()
        for cp in sends:
            cp.wait_send()
        mine.wait()

    return pl.pallas_call(
        body, name=name, out_shape=jax.ShapeDtypeStruct((N_DEV, R, C), v.dtype),
        in_specs=[pl.BlockSpec(memory_space=pltpu.VMEM)], out_specs=pl.BlockSpec(memory_space=pltpu.VMEM),
        scratch_shapes=[pltpu.SemaphoreType.DMA((N_DEV - 1,)), pltpu.SemaphoreType.DMA((N_DEV - 1,)), pltpu.SemaphoreType.DMA],
        compiler_params=pltpu.CompilerParams(vmem_limit_bytes=VMEM_LIMIT),
    )(v)


_CHIP_RELATIONS = ((1, 0), (0, 1), (1, 1))


def _gather_split(shard, name):
    rows, cols = shard.shape
    half = rows // 2

    def body(in_ref, out_ref, send_sems, recv_sems, local_sem):
        x, y, c = _place()
        chip = 2 * x + y
        mine = pl.ds(pl.multiple_of(c * half, 16), half)
        local = pltpu.make_async_copy(in_ref, out_ref.at[chip], local_sem)
        local.start()
        peers = [(_flip(x, fx), _flip(y, fy)) for fx, fy in _CHIP_RELATIONS]
        sends = [pltpu.make_async_remote_copy(
            src_ref=in_ref.at[mine], dst_ref=out_ref.at[chip, mine], send_sem=send_sems.at[j], recv_sem=recv_sems.at[j],
            device_id=(px, py, c), device_id_type=MESH) for j, (px, py) in enumerate(peers)]
        for cp in sends:
            cp.start()
        for j, (px, py) in enumerate(peers):
            landed = out_ref.at[2 * px + py, mine]
            pltpu.make_async_remote_copy(
                src_ref=landed, dst_ref=landed, send_sem=send_sems.at[j], recv_sem=recv_sems.at[j],
                device_id=(x, y, c), device_id_type=MESH).wait_recv()
            fwd = pltpu.make_async_remote_copy(
                src_ref=landed, dst_ref=landed, send_sem=send_sems.at[3 + j], recv_sem=recv_sems.at[3 + j],
                device_id=(x, y, 1 - c), device_id_type=MESH)
            fwd.start()
            sends.append(fwd)
        for j in range(3):
            landed = out_ref.at[0, mine]
            pltpu.make_async_remote_copy(
                src_ref=landed, dst_ref=landed, send_sem=send_sems.at[3 + j], recv_sem=recv_sems.at[3 + j],
                device_id=(x, y, c), device_id_type=MESH).wait_recv()
        for cp in sends:
            cp.wait_send()
        local.wait()

    any_spec = pl.BlockSpec(memory_space=pl.ANY)
    return pl.pallas_call(
        body, name=name, out_shape=jax.ShapeDtypeStruct((N_CHIPS, rows, cols), shard.dtype),
        in_specs=[any_spec], out_specs=any_spec,
        scratch_shapes=[pltpu.SemaphoreType.DMA((6,)), pltpu.SemaphoreType.DMA((6,)), pltpu.SemaphoreType.DMA],
    )(shard)


class _Comm:
    def __init__(self, arrays, out_shape, scratch, start, middle, finish):
        self.arrays, self.out_shape, self.scratch = arrays, out_shape, scratch
        self.start, self.middle, self.finish = start, middle, finish


def _hosted(body, n_in, n_out, n_scratch, comm, first, middle, last):
    nc, no = len(comm.arrays), len(comm.out_shape)

    def wrapped(*refs):
        a = n_in + nc
        b = a + n_out + no
        ins, c_ins, outs, c_outs = refs[:n_in], refs[n_in:a], refs[a:a + n_out], refs[a + n_out:b]
        scratch, c_sems = refs[b:b + n_scratch], refs[b + n_scratch:]

        @pl.when(first())
        def _():
            comm.start(c_ins, c_outs, c_sems)

        body(*ins, *outs, *scratch)
        if comm.middle is not None:
            @pl.when(middle())
            def _():
                comm.middle(c_ins, c_outs, c_sems)

        @pl.when(last())
        def _():
            comm.finish(c_ins, c_outs, c_sems)

    return wrapped


_ANY = pl.BlockSpec(memory_space=pl.ANY)


def _gather_comm(shards):
    n = len(shards)

    def copies(kind, ins, outs, sems):
        send_sems, recv_sems, local_sems = sems
        x, y, c = _place()
        chip = 2 * x + y
        if kind == "local":
            return [pltpu.make_async_copy(ins[i], outs[i].at[chip], local_sems.at[i]) for i in range(n)]
        made = []
        for i in range(n):
            for j, (fx, fy) in enumerate(_CHIP_RELATIONS):
                px, py = _flip(x, fx), _flip(y, fy)
                slot, to = (chip, (px, py, c)) if kind == "send" else (2 * px + py, (x, y, c))
                made.append(pltpu.make_async_remote_copy(
                    src_ref=ins[i], dst_ref=outs[i].at[slot], send_sem=send_sems.at[i, j], recv_sem=recv_sems.at[i, j],
                    device_id=to, device_id_type=MESH))
        return made

    def start(ins, outs, sems):
        for cp in copies("local", ins, outs, sems) + copies("send", ins, outs, sems):
            cp.start()

    def finish(ins, outs, sems):
        for cp in copies("recv", ins, outs, sems):
            cp.wait_recv()
        for cp in copies("send", ins, outs, sems):
            cp.wait_send()
        for cp in copies("local", ins, outs, sems):
            cp.wait()

    return _Comm(list(shards), [jax.ShapeDtypeStruct((N_CHIPS,) + s.shape, s.dtype) for s in shards],
                 [pltpu.SemaphoreType.DMA((n, 3)), pltpu.SemaphoreType.DMA((n, 3)), pltpu.SemaphoreType.DMA((n,))],
                 start, None, finish)


def _exchange_comm(blocks):
    n = len(blocks)

    def copies(kind, ins, outs, sems):
        send_sems, recv_sems, local_sems = sems
        x, y, c = _place()
        chip = 2 * x + y
        me, sibling = (x, y, c), (x, y, 1 - c)

        def remote(src, dst, i, j, to):
            return pltpu.make_async_remote_copy(src_ref=src, dst_ref=dst, send_sem=send_sems.at[i, j],
                                                recv_sem=recv_sems.at[i, j], device_id=to, device_id_type=MESH)

        made = []
        for i in range(n):
            if kind == "local":
                made.append(pltpu.make_async_copy(ins[i].at[chip], outs[i].at[chip], local_sems.at[i]))
                continue
            for j, (fx, fy) in enumerate(_CHIP_RELATIONS):
                px, py = _flip(x, fx), _flip(y, fy)
                src = 2 * px + py
                if kind == "first":
                    made.append(remote(ins[i].at[src], outs[i].at[chip], i, j, (px, py, c)))
                elif kind == "landed":
                    made.append(remote(ins[i].at[src], outs[i].at[src], i, j, me))
                elif kind == "passed":
                    made.append(remote(outs[i].at[src], outs[i].at[N_CHIPS + src], i, 4 + j, sibling))
            if kind == "first":
                made.append(remote(ins[i].at[chip], outs[i].at[N_CHIPS + chip], i, 3, sibling))
            if kind == "arrivals":
                made += [remote(ins[i].at[0], outs[i].at[0], i, j, me) for j in (3, 4, 5, 6)]
        return made

    def start(ins, outs, sems):
        for cp in copies("local", ins, outs, sems) + copies("first", ins, outs, sems):
            cp.start()

    def middle(ins, outs, sems):
        for got, fwd in zip(copies("landed", ins, outs, sems), copies("passed", ins, outs, sems)):
            got.wait_recv()
            fwd.start()

    def finish(ins, outs, sems):
        for cp in copies("arrivals", ins, outs, sems):
            cp.wait_recv()
        for cp in copies("first", ins, outs, sems) + copies("passed", ins, outs, sems):
            cp.wait_send()
        for cp in copies("local", ins, outs, sems):
            cp.wait()

    return _Comm(list(blocks), [jax.ShapeDtypeStruct((N_DEV,) + b.shape[1:], b.dtype) for b in blocks],
                 [pltpu.SemaphoreType.DMA((n, 7)), pltpu.SemaphoreType.DMA((n, 7)), pltpu.SemaphoreType.DMA((n,))],
                 start, middle, finish)


def _row_spec(tr, w, col=0):
    return pl.BlockSpec((tr, w), lambda i: (i, col))


def _vec_spec(w):
    return pl.BlockSpec((1, w), lambda i: (0, 0))


def _acc_spec(w):
    return pl.BlockSpec((8, w), lambda i: (0, 0))


def _rms(x):
    return lax.rsqrt(jnp.mean(x * x, axis=-1, keepdims=True) + EPS)


def _rms_bwd(dn, n, rstd):
    return rstd * (dn - n * jnp.mean(dn * n, axis=-1, keepdims=True))


def _colsum(x):
    return jnp.sum(x, axis=0, keepdims=True)


def _zero_first(ref):
    @pl.when(pl.program_id(0) == 0)
    def _():
        ref[...] = jnp.zeros_like(ref)


def _x_specs(tr, n_lat_tiles):
    return [pl.BlockSpec((tr, D), lambda i: (jnp.minimum(i, n_lat_tiles - 1), 0)),
            pl.BlockSpec((tr, D), lambda i: (jnp.maximum(i - n_lat_tiles, 0), 0))]


def _x_tile(x_ref, c_ref, n_lat_tiles):
    return jnp.where(pl.program_id(0) >= n_lat_tiles, c_ref[...], x_ref[...])


def _norm_mod(x, ctx, w, mod, n_lat_tiles, tr):
    R = x.shape[0] + ctx.shape[0]

    def body(x_ref, c_ref, w_ref, mod_ref, o_ref):
        xv = _x_tile(x_ref, c_ref, n_lat_tiles)
        n = xv * _rms(xv) * w_ref[...]
        o_ref[...] = (n * (1.0 + mod_ref[0, 1]) + mod_ref[0, 0]).astype(BF16)

    return pl.pallas_call(
        body, name="norm1_mod", grid=(R // tr,),
        in_specs=_x_specs(tr, n_lat_tiles) + [_vec_spec(D), pl.BlockSpec(
            (1, 2, 1, D), lambda i: (jnp.where(i >= n_lat_tiles, 1, 0), 0, 0, 0))],
        out_specs=_row_spec(tr, D), out_shape=jax.ShapeDtypeStruct((R, D), BF16),
        compiler_params=_cparams(("arbitrary",)),
    )(x, ctx, w, mod)


def _resid_norm_mod(x, ctx, mix, g1, w2, sh2, sc2, n_lat_tiles, tr):
    R = x.shape[0] + ctx.shape[0]

    def body(x_ref, c_ref, mix_ref, g1_ref, w_ref, sh_ref, sc_ref, h2_ref, u_ref):
        h2 = _x_tile(x_ref, c_ref, n_lat_tiles) + g1_ref[...] * mix_ref[...]
        h2_ref[...] = h2
        n = h2 * _rms(h2) * w_ref[...]
        u_ref[...] = (n * (1.0 + sc_ref[...]) + sh_ref[...]).astype(BF16)

    return pl.pallas_call(
        body, name="resid_norm2_mod", grid=(R // tr,),
        in_specs=_x_specs(tr, n_lat_tiles) + [_row_spec(tr, D)] + [_vec_spec(D)] * 4,
        out_specs=[_row_spec(tr, D), _row_spec(tr, D)],
        out_shape=[jax.ShapeDtypeStruct((R, D), F32), jax.ShapeDtypeStruct((R, D), BF16)],
        compiler_params=_cparams(("arbitrary",)),
    )(x, ctx, mix, g1, w2, sh2, sc2)


def _loss_head(h2, f, target, g2, fw, n_lat_tiles, tr):
    R = h2.shape[0]

    def body(h2_ref, f_ref, t_ref, g2_ref, fw_ref, dh3_ref, df_ref, acc_ref):
        _zero_first(acc_ref)
        lat = pl.program_id(0) < n_lat_tiles
        fv = f_ref[...]
        h3 = h2_ref[...] + g2_ref[...] * fv
        rstd = _rms(h3)
        n = h3 * rstd
        err = n * fw_ref[...] - t_ref[...]
        dy = err * (1.0 / D)
        dh3 = jnp.where(lat, _rms_bwd(dy * fw_ref[...], n, rstd), 0.0)
        dh3_ref[...] = dh3
        df_ref[...] = (g2_ref[...] * dh3).astype(BF16)
        acc_ref[0:1, :] += jnp.where(lat, _colsum(dy * n), 0.0)
        acc_ref[1:2, :] += _colsum(dh3 * fv)
        acc_ref[2:3, :] += jnp.where(lat, _colsum(err * err) * (0.5 / D), 0.0)

    return pl.pallas_call(
        body, name="loss_head", grid=(R // tr,),
        in_specs=[_row_spec(tr, D), _row_spec(tr, D),
                  pl.BlockSpec((tr, D), lambda i: (jnp.minimum(i, n_lat_tiles - 1), 0)), _vec_spec(D), _vec_spec(D)],
        out_specs=[_row_spec(tr, D), _row_spec(tr, D), _acc_spec(D)],
        out_shape=[jax.ShapeDtypeStruct((R, D), F32), jax.ShapeDtypeStruct((R, D), BF16), jax.ShapeDtypeStruct((8, D), F32)],
        compiler_params=_cparams(("arbitrary",)),
    )(h2, f, target, g2, fw)


def _ffn_in_bwd(du_a, du_b, h2, dh3, mix, sc2, g1, w2, tr):
    R = h2.shape[0]

    def body(dua_ref, dub_ref, h2_ref, dh3_ref, mix_ref, sc_ref, g1_ref, w_ref, dh2_ref, dmix_ref, acc_ref):
        _zero_first(acc_ref)
        du = _f32(dua_ref) + _f32(dub_ref)
        h2 = h2_ref[...]
        rstd = _rms(h2)
        n = h2 * rstd
        dnw = du * (1.0 + sc_ref[...])
        dh2 = dh3_ref[...] + _rms_bwd(dnw * w_ref[...], n, rstd)
        dh2_ref[...] = dh2
        dmix_ref[...] = (g1_ref[...] * dh2).astype(BF16)
        acc_ref[0:1, :] += _colsum(du * n * w_ref[...])
        acc_ref[1:2, :] += _colsum(du)
        acc_ref[2:3, :] += _colsum(dnw * n)
        acc_ref[3:4, :] += _colsum(dh2 * mix_ref[...])

    return pl.pallas_call(
        body, name="ffn_in_bwd", grid=(R // tr,),
        in_specs=[_row_spec(tr, D)] * 5 + [_vec_spec(D)] * 3,
        out_specs=[_row_spec(tr, D), _row_spec(tr, D), _acc_spec(D)],
        out_shape=[jax.ShapeDtypeStruct((R, D), F32), jax.ShapeDtypeStruct((R, D), BF16), jax.ShapeDtypeStruct((8, D), F32)],
        compiler_params=_cparams(("arbitrary",)),
    )(du_a, du_b, h2, dh3, mix, sc2, g1, w2)


def _norm1_bwd(dh1, x, ctx, dh2, w1, mod, n_lat_tiles, tr):
    R = x.shape[0] + ctx.shape[0]

    def body(dh1_ref, x_ref, c_ref, dh2_ref, w_ref, mod_ref, dx_ref, acc_ref):
        i = pl.program_id(0)

        @pl.when((i == 0) | (i == n_lat_tiles))
        def _():
            acc_ref[...] = jnp.zeros_like(acc_ref)

        dh1 = dh1_ref[...]
        x = _x_tile(x_ref, c_ref, n_lat_tiles)
        rstd = _rms(x)
        n = x * rstd
        dnw = dh1 * (1.0 + mod_ref[0, 1])

        @pl.when(i < n_lat_tiles)
        def _():
            dx_ref[...] = dh2_ref[...] + _rms_bwd(dnw * w_ref[...], n, rstd)

        acc_ref[0, 0:1, :] += _colsum(dh1 * n * w_ref[...])
        acc_ref[0, 1:2, :] += _colsum(dh1)
        acc_ref[0, 2:3, :] += _colsum(dnw * n)

    sel = lambda i: jnp.where(i >= n_lat_tiles, 1, 0)
    return pl.pallas_call(
        body, name="norm1_bwd", grid=(R // tr,),
        in_specs=[_row_spec(tr, D)] + _x_specs(tr, n_lat_tiles) + [_row_spec(tr, D), _vec_spec(D),
                                                                   pl.BlockSpec((1, 2, 1, D), lambda i: (sel(i), 0, 0, 0))],
        out_specs=[pl.BlockSpec((tr, D), lambda i: (jnp.minimum(i, n_lat_tiles - 1), 0)),
                   pl.BlockSpec((1, 8, D), lambda i: (sel(i), 0, 0))],
        out_shape=[jax.ShapeDtypeStruct((n_lat_tiles * tr, D), F32), jax.ShapeDtypeStruct((2, 8, D), F32)],
        compiler_params=_cparams(("arbitrary",)),
    )(dh1, x, ctx, dh2, w1, mod)


def _swiglu_act_bwd(da, gp, up, tr):
    R = gp.shape[0]

    def body(da_ref, g_ref, u_ref, dg_ref, du_ref):
        da, g = _f32(da_ref), _f32(g_ref)
        dg_ref[...] = (da * _f32(u_ref) * _dsilu(g)).astype(BF16)
        du_ref[...] = (da * _silu(g)).astype(BF16)

    return pl.pallas_call(
        body, name="swiglu_act_bwd", grid=(R // tr,), in_specs=[_row_spec(tr, D_FF)] * 3, out_specs=[_row_spec(tr, D_FF)] * 2,
        out_shape=[jax.ShapeDtypeStruct((R, D_FF), BF16)] * 2, compiler_params=_cparams(("arbitrary",)),
    )(da, gp, up)


def _merge(ya, yb, parts, tr):
    R = ya.shape[0]

    def body(ya_ref, yb_ref, ga_ref, gb_ref, o_ref):
        o_ref[...] = (_sigmoid(_f32(ga_ref)) * _f32(ya_ref) + _sigmoid(_f32(gb_ref)) * _f32(yb_ref)).astype(BF16)

    return pl.pallas_call(
        body, name="merge", grid=(R // tr,),
        in_specs=[_row_spec(tr, D), _row_spec(tr, D), _row_spec(tr, D, COL_GA // D), _row_spec(tr, D, COL_GB // D)],
        out_specs=_row_spec(tr, D), out_shape=jax.ShapeDtypeStruct((R, D), BF16), compiler_params=_cparams(("arbitrary",)),
    )(ya, yb, parts, parts)


def _dparts_out(tr, w, col, nd=1):
    blk = col // w
    return pl.BlockSpec((tr, w), (lambda i: (i, blk)) if nd == 1 else (lambda i, j: (i, blk + j)))


def _merge_bwd(dm, ya, yb, parts, dparts, tr):
    R = ya.shape[0]

    def body(dm_ref, ya_ref, yb_ref, ga_ref, gb_ref, _, dya_ref, dyb_ref, dg_ref):
        dm = _f32(dm_ref)
        sa, sb = _sigmoid(_f32(ga_ref)), _sigmoid(_f32(gb_ref))
        dya_ref[...] = (dm * sa).astype(BF16)
        dyb_ref[...] = (dm * sb).astype(BF16)
        dg_ref[:, 0:D] = (dm * _f32(ya_ref) * sa * (1.0 - sa)).astype(BF16)
        dg_ref[:, D:2 * D] = (dm * _f32(yb_ref) * sb * (1.0 - sb)).astype(BF16)

    return pl.pallas_call(
        body, name="merge_bwd", grid=(R // tr,),
        in_specs=[_row_spec(tr, D)] * 3 + [_row_spec(tr, D, COL_GA // D), _row_spec(tr, D, COL_GB // D), _ANY],
        out_specs=[_row_spec(tr, D), _row_spec(tr, D), _dparts_out(tr, 2 * D, COL_GA)],
        out_shape=[jax.ShapeDtypeStruct((R, D), BF16)] * 2 + [jax.ShapeDtypeStruct(dparts.shape, BF16)],
        input_output_aliases={5: 2}, compiler_params=_cparams(("arbitrary",)),
    )(dm, ya, yb, parts, parts, dparts)


def _gla_out(o2, parts, gw4, tr):
    R = parts.shape[0]

    def body(of_ref, ob_ref, r_ref, w_ref, out_ref):
        oa = _f32(of_ref) + _f32(ob_ref)
        sr = _silu(_f32(r_ref))
        for h in range(GLA_H):
            s = slice(h * GLA_DV, (h + 1) * GLA_DV)
            o = oa[:, s]
            out_ref[:, s] = (o * _rms(o) * w_ref[:, s] * sr[:, s]).astype(BF16)

    return pl.pallas_call(
        body, name="gla_out", grid=(R // tr,),
        in_specs=[_row_spec(tr, D), _row_spec(tr, D), _row_spec(tr, D, COL_R // D), _vec_spec(D)],
        out_specs=_row_spec(tr, D), out_shape=jax.ShapeDtypeStruct((R, D), BF16), compiler_params=_cparams(("arbitrary",)),
    )(o2[0], o2[1], parts, gw4)


def _gla_out_bwd(dout, o2, parts, gw4, dparts, tr):
    R = parts.shape[0]

    def body(d_ref, of_ref, ob_ref, r_ref, w_ref, _, do_ref, dr_ref, acc_ref):
        _zero_first(acc_ref)
        oa = _f32(of_ref) + _f32(ob_ref)
        r = _f32(r_ref)
        sr = _silu(r)
        dout = _f32(d_ref)
        for h in range(GLA_H):
            s = slice(h * GLA_DV, (h + 1) * GLA_DV)
            o = oa[:, s]
            rstd = _rms(o)
            n = o * rstd
            w = w_ref[:, s]
            dr_ref[:, s] = (dout[:, s] * n * w * _dsilu(r[:, s])).astype(BF16)
            dnw = dout[:, s] * sr[:, s]
            do_ref[:, s] = _rms_bwd(dnw * w, n, rstd).astype(ACT)
            acc_ref[0:1, s] += _colsum(dnw * n)

    return pl.pallas_call(
        body, name="gla_out_bwd", grid=(R // tr,),
        in_specs=[_row_spec(tr, D), _row_spec(tr, D), _row_spec(tr, D), _row_spec(tr, D, COL_R // D), _vec_spec(D), _ANY],
        out_specs=[_row_spec(tr, D), _dparts_out(tr, D, COL_R), _acc_spec(D)],
        out_shape=[jax.ShapeDtypeStruct((R, D), ACT), jax.ShapeDtypeStruct(dparts.shape, BF16), jax.ShapeDtypeStruct((8, D), F32)],
        input_output_aliases={5: 1}, compiler_params=_cparams(("arbitrary",)),
    )(dout, o2[0], o2[1], parts, gw4, dparts)


SSM_GW = SSM_INNER // SSM_G


def _ssd_out(y2, xbc, parts, dskip, nw, tr):
    R = parts.shape[0]

    def body(yf_ref, yb_ref, x_ref, z_ref, ds_ref, w_ref, out_ref):
        ob = (_f32(yf_ref) + _f32(yb_ref) + ds_ref[...] * _f32(x_ref)) * _silu(_f32(z_ref))
        for g in range(SSM_G):
            s = slice(g * SSM_GW, (g + 1) * SSM_GW)
            o = ob[:, s]
            out_ref[:, s] = (o * _rms(o) * w_ref[:, s]).astype(BF16)

    return pl.pallas_call(
        body, name="ssd_out", grid=(R // tr,),
        in_specs=[_row_spec(tr, SSM_INNER)] * 3 + [_row_spec(tr, SSM_INNER, COL_Z // SSM_INNER),
                                                   _vec_spec(SSM_INNER), _vec_spec(SSM_INNER)],
        out_specs=_row_spec(tr, SSM_INNER), out_shape=jax.ShapeDtypeStruct((R, SSM_INNER), BF16),
        compiler_params=_cparams(("arbitrary",)),
    )(y2[0], y2[1], xbc, parts, dskip, nw)


def _ssd_out_bwd(dout, y2, xbc, parts, dskip, nw, dparts, tr):
    R = parts.shape[0]

    def body(d_ref, yf_ref, yb_ref, x_ref, z_ref, ds_ref, w_ref, _, dy_ref, dz_ref, acc_ref):
        _zero_first(acc_ref)
        x, z = _f32(x_ref), _f32(z_ref)
        pre = _f32(yf_ref) + _f32(yb_ref) + ds_ref[...] * x
        sz = _silu(z)
        ob = pre * sz
        dout = _f32(d_ref)
        for g in range(SSM_G):
            s = slice(g * SSM_GW, (g + 1) * SSM_GW)
            o = ob[:, s]
            rstd = _rms(o)
            n = o * rstd
            dob = _rms_bwd(dout[:, s] * w_ref[:, s], n, rstd)
            dz_ref[:, s] = (dob * pre[:, s] * _dsilu(z[:, s])).astype(BF16)
            dy = dob * sz[:, s]
            dy_ref[:, s] = dy.astype(ACT)
            acc_ref[0:1, s] += _colsum(dout[:, s] * n)
            acc_ref[1:2, s] += _colsum(dy * x[:, s])

    return pl.pallas_call(
        body, name="ssd_out_bwd", grid=(R // tr,),
        in_specs=[_row_spec(tr, SSM_INNER)] * 4 + [_row_spec(tr, SSM_INNER, COL_Z // SSM_INNER),
                                                   _vec_spec(SSM_INNER), _vec_spec(SSM_INNER), _ANY],
        out_specs=[_row_spec(tr, SSM_INNER), _dparts_out(tr, SSM_INNER, COL_Z), _acc_spec(SSM_INNER)],
        out_shape=[jax.ShapeDtypeStruct((R, SSM_INNER), ACT), jax.ShapeDtypeStruct(dparts.shape, BF16),
                   jax.ShapeDtypeStruct((8, SSM_INNER), F32)],
        input_output_aliases={7: 1}, compiler_params=_cparams(("arbitrary",)),
    )(dout, y2[0], y2[1], xbc, parts, dskip, nw, dparts)


CONV_W = SSM_INNER + 2 * SSM_G * SSM_N
CONV_BLK = 1024


CONV_SHIFTS = (-2, -1, 1, 2)


def _conv_mask_table(tr):
    t = np.arange(tr)
    table = np.zeros((2, len(CONV_SHIFTS), tr, 128), np.float32)
    for kind, (pos, seg) in enumerate(((t % GRID_W, GRID_W), (t, tr))):
        for k, s in enumerate(CONV_SHIFTS):
            table[kind, k] = ((pos + s >= 0) & (pos + s < seg)).astype(np.float32)[:, None]
    return jnp.asarray(table)


def _shifted(u, s, mask_ref, tr):
    return u if s == 0 else pltpu.roll(u, (-s) % tr, 0) * mask_ref[0, CONV_SHIFTS.index(s)]


def _conv_mask_spec(tr, n_lat_tiles, row_axis):
    return pl.BlockSpec((1, len(CONV_SHIFTS), tr, 128),
                        lambda *ids: (jnp.where(ids[row_axis] >= n_lat_tiles, 1, 0), 0, 0, 0))


def _conv_fwd(parts, cw, cb, n_lat_tiles, tr):
    R = parts.shape[0]

    def body(u_ref, w_ref, b_ref, mask_ref, o_ref):
        def lanes(l, carry):
            sl = pl.ds(pl.multiple_of(l * 128, 128), 128)
            u, w = u_ref[:, sl].astype(F32), w_ref[:, sl]
            acc = jnp.zeros_like(u) + b_ref[:, sl]
            for j in range(SSM_CONV):
                acc = acc + _shifted(u, j - CONV_LEFT, mask_ref, tr) * w[j:j + 1, :]
            o_ref[:, sl] = _silu(acc).astype(ACT)
            return carry

        lax.fori_loop(0, CONV_BLK // 128, lanes, 0)

    return pl.pallas_call(
        body, name="conv_fwd", grid=(R // tr, CONV_W // CONV_BLK),
        in_specs=[pl.BlockSpec((tr, CONV_BLK), lambda i, j: (i, COL_XBC // CONV_BLK + j)),
                  pl.BlockSpec((SSM_CONV, CONV_BLK), lambda i, j: (0, j)), pl.BlockSpec((1, CONV_BLK), lambda i, j: (0, j)),
                  _conv_mask_spec(tr, n_lat_tiles, 0)],
        out_specs=pl.BlockSpec((tr, CONV_BLK), lambda i, j: (i, j)), out_shape=jax.ShapeDtypeStruct((R, CONV_W), ACT),
        compiler_params=_cparams(("arbitrary", "arbitrary")),
    )(parts, cw, cb, _conv_mask_table(tr))


def _conv_bwd(dx, db, dc, dy, dskip, parts, cw, cb, dparts, n_lat_tiles, tr):
    R = parts.shape[0]
    half = CONV_BLK // 2
    n_x = SSM_INNER // CONV_BLK

    def body(dxf_ref, dxb_ref, dy_ref, ds_ref, dbf_ref, dbb_ref, dcf_ref, dcb_ref, u_ref, w_ref, b_ref, _, mask_ref,
             du_ref, acc_ref, d_scr):
        @pl.when(pl.program_id(1) == 0)
        def _():
            acc_ref[...] = jnp.zeros_like(acc_ref)

        @pl.when(pl.program_id(0) < n_x)
        def _():
            d_scr[...] = dxf_ref[...] + dxb_ref[...] + _f32(dy_ref) * ds_ref[...]

        @pl.when(pl.program_id(0) >= n_x)
        def _():
            d_scr[:, 0:half] = dbf_ref[...] + dbb_ref[...]
            d_scr[:, half:] = dcf_ref[...] + dcb_ref[...]

        def lanes(l, carry):
            sl = pl.ds(pl.multiple_of(l * 128, 128), 128)
            u, w = u_ref[:, sl].astype(F32), w_ref[:, sl]
            pre = jnp.zeros_like(u) + b_ref[:, sl]
            taps = []
            for j in range(SSM_CONV):
                tap = _shifted(u, j - CONV_LEFT, mask_ref, tr)
                taps.append(tap)
                pre = pre + tap * w[j:j + 1, :]
            dpre = d_scr[:, sl] * _dsilu(pre)
            du = jnp.zeros_like(u)
            sums = []
            for j in range(SSM_CONV):
                sums.append(_colsum(dpre * taps[j]))
                du = du + _shifted(dpre, CONV_LEFT - j, mask_ref, tr) * w[j:j + 1, :]
            sums += [_colsum(dpre), jnp.zeros((8 - SSM_CONV - 1, 128), F32)]
            acc_ref[:, sl] += jnp.concatenate(sums, axis=0)
            du_ref[:, sl] = du.astype(BF16)
            return carry

        lax.fori_loop(0, CONV_BLK // 128, lanes, 0)

    return pl.pallas_call(
        body, name="conv_bwd", grid=(CONV_W // CONV_BLK, R // tr),
        in_specs=[pl.BlockSpec((tr, CONV_BLK), lambda j, i: (jnp.where(j < n_x, i, 0), jnp.minimum(j, n_x - 1)))] * 3
        + [pl.BlockSpec((1, CONV_BLK), lambda j, i: (0, jnp.minimum(j, n_x - 1)))]
        + [pl.BlockSpec((tr, half), lambda j, i: (jnp.where(j < n_x, 0, i), 0))] * 4
        + [pl.BlockSpec((tr, CONV_BLK), lambda j, i: (i, COL_XBC // CONV_BLK + j)),
           pl.BlockSpec((SSM_CONV, CONV_BLK), lambda j, i: (0, j)), pl.BlockSpec((1, CONV_BLK), lambda j, i: (0, j)), _ANY,
           _conv_mask_spec(tr, n_lat_tiles, 1)],
        out_specs=[pl.BlockSpec((tr, CONV_BLK), lambda j, i: (i, COL_XBC // CONV_BLK + j)),
                   pl.BlockSpec((8, CONV_BLK), lambda j, i: (0, j))],
        out_shape=[jax.ShapeDtypeStruct(dparts.shape, BF16), jax.ShapeDtypeStruct((8, CONV_W), F32)],
        scratch_shapes=[pltpu.VMEM((tr, CONV_BLK), F32)],
        input_output_aliases={11: 0}, compiler_params=_cparams(("arbitrary", "arbitrary")),
    )(*dx, dy, dskip, *db, *dc, parts, cw, cb, dparts, _conv_mask_table(tr))


def _chunk_row_block(d, i, n_lat, n_ctx):
    fwd = jnp.where(i < n_ctx, n_lat + i, i - n_ctx)
    rev = n_lat + n_ctx - 1 - i
    if isinstance(d, int):
        return rev if d else fwd
    return jnp.where(d == 0, fwd, rev)


def _tri(n, d, transpose=False):
    row = lax.broadcasted_iota(jnp.int32, (n, n), 0)
    col = lax.broadcasted_iota(jnp.int32, (n, n), 1)
    diff = (col - row) if transpose else (row - col)
    return diff * (1 - 2 * d) >= 0


def _gla_gates(sm, uhi, ulo, bias, d):
    pre = _nn3(sm, uhi, ulo) + bias
    g = _log_sigmoid(pre) * (1.0 / GLA_TAU)
    mask = _tri(GLA_C, d)
    b = _x_nn(mask.astype(BF16), g)
    b_tot = _colsum(g)
    b_ref = b[GLA_C // 2:GLA_C // 2 + 1, :]
    e_q = jnp.exp(jnp.minimum(b - b_ref, EXP_CLAMP))
    e_k = jnp.exp(jnp.minimum(b_ref - b, EXP_CLAMP))
    return pre, mask, b_tot, e_q, e_k, jnp.exp(b), jnp.exp(b_tot - b)


GLA_QK = GLA_H * GLA_DK
GLA_V = GLA_H * GLA_DV


def _gla_specs(n_lat, n_ctx, step_of):
    rbs = [lambda i, d=d: _chunk_row_block(d, step_of(i), n_lat, n_ctx) for d in range(2)]
    specs = []
    for rb in rbs:
        specs += [pl.BlockSpec((GLA_C, GLA_QK), lambda i, rb=rb: (rb(i), COL_Q // GLA_QK)),
                  pl.BlockSpec((GLA_C, GLA_QK), lambda i, rb=rb: (rb(i), COL_K // GLA_QK)),
                  pl.BlockSpec((GLA_C, GLA_V), lambda i, rb=rb: (rb(i), COL_V // GLA_V)),
                  pl.BlockSpec((GLA_C, 128), lambda i, rb=rb: (rb(i), 0))]
    specs += [pl.BlockSpec((2, 128, GLA_QK), lambda i: (0, 0, 0)), pl.BlockSpec((2, 128, GLA_QK), lambda i: (0, 0, 0)),
              pl.BlockSpec((2, 1, GLA_QK), lambda i: (0, 0, 0))]
    return specs, rbs


def _gla_fwd(parts, sm, uhi, ulo, bias, n_lat, n_ctx):
    R = parts.shape[0]
    n_steps = n_lat + n_ctx
    scale = GLA_DK ** -0.5

    def body(*refs):
        ins, (uhi_ref, ulo_ref, bias_ref), o_refs, hist_ref, st = refs[:8], refs[8:11], refs[11:13], refs[13], refs[14]

        @pl.when(pl.program_id(0) == 0)
        def _():
            st[...] = jnp.zeros_like(st)

        for d in range(2):
            q_ref, k_ref, v_ref, sm_ref = ins[4 * d:4 * d + 4]
            _, mask, b_tot, e_q, e_k, e_in, e_out = _gla_gates(sm_ref[...], uhi_ref[d], ulo_ref[d], bias_ref[d], d)
            q, k, v = _f32(q_ref) * scale, _f32(k_ref), _bf(v_ref[...])
            qb, kb, q_in, k_out, decay = _bf(q * e_q), _bf(k * e_k), _bf(q * e_in), _bf(k * e_out), jnp.exp(b_tot)
            for h in range(GLA_H):
                sk, sv = slice(h * GLA_DK, (h + 1) * GLA_DK), slice(h * GLA_DV, (h + 1) * GLA_DV)
                att = jnp.where(mask, _nt(qb[:, sk], kb[:, sk]), 0.0)
                s_in = st[d, h]
                hist_ref[d, 0, h] = s_in
                o_refs[d][:, sv] = (_nn(_bf(att), v[:, sv]) + _nt(q_in[:, sk], _bf(s_in))).astype(ACT)
                st[d, h] = decay[:, sk] * s_in + _tn(v[:, sv], k_out[:, sk])

    in_specs, rbs = _gla_specs(n_lat, n_ctx, lambda i: i)
    return pl.pallas_call(
        body, name="gla_fwd", grid=(n_steps,), in_specs=in_specs,
        out_specs=[pl.BlockSpec((GLA_C, GLA_V), lambda i, rb=rb: (rb(i), 0)) for rb in rbs]
        + [pl.BlockSpec((2, 1, GLA_H, GLA_DV, GLA_DK), lambda i: (0, i, 0, 0, 0))],
        out_shape=[jax.ShapeDtypeStruct((R, GLA_V), ACT)] * 2 + [jax.ShapeDtypeStruct((2, n_steps, GLA_H, GLA_DV, GLA_DK), F32)],
        scratch_shapes=[pltpu.VMEM((2, GLA_H, GLA_DV, GLA_DK), F32)],
        compiler_params=_cparams(("arbitrary",)),
    )(*([parts, parts, parts, sm] * 2), uhi, ulo, bias)


def _gla_bwd(do, parts, sm, uhi, ulo, bias, hist, n_lat, n_ctx):
    R = parts.shape[0]
    n_steps = n_lat + n_ctx
    scale = GLA_DK ** -0.5
    step_of = lambda j: n_steps - 1 - j

    def body(*refs):
        ins, (uhi_ref, ulo_ref, bias_ref), do_refs, hist_ref = refs[:8], refs[8:11], refs[11:13], refs[13]
        outs, dst = refs[14:22], refs[22]

        @pl.when(pl.program_id(0) == 0)
        def _():
            dst[...] = jnp.zeros_like(dst)

        for d in range(2):
            q_ref, k_ref, v_ref, sm_ref = ins[4 * d:4 * d + 4]
            dq_ref, dk_ref, dv_ref, dp_ref = outs[4 * d:4 * d + 4]
            pre, mask, b_tot, e_q, e_k, e_in, e_out = _gla_gates(sm_ref[...], uhi_ref[d], ulo_ref[d], bias_ref[d], d)
            q, k, v = _f32(q_ref) * scale, _f32(k_ref), _bf(v_ref[...])
            dout = _bf(do_refs[d][...])
            k_out_f = k * e_out
            qb, kb, q_in, k_out, decay = _bf(q * e_q), _bf(k * e_k), _bf(q * e_in), _bf(k_out_f), jnp.exp(b_tot)
            dqs, dks, dk_outs, dss = [], [], [], []
            for h in range(GLA_H):
                sk, sv = slice(h * GLA_DK, (h + 1) * GLA_DK), slice(h * GLA_DV, (h + 1) * GLA_DV)
                s_in, ds = hist_ref[d, 0, h], dst[d, h]
                att = jnp.where(mask, _nt(qb[:, sk], kb[:, sk]), 0.0)
                datt = _bf(jnp.where(mask, _nt(dout[:, sv], v[:, sv]), 0.0))
                dv_ref[:, sv] = (_tn(_bf(att), dout[:, sv]) + _nt(k_out[:, sk], _bf(ds))).astype(ACT)
                dqs.append(_nn(datt, kb[:, sk]) * e_q[:, sk] + _nn(dout[:, sv], _bf(s_in)) * e_in[:, sk])
                dk_o = _nn(v[:, sv], _bf(ds))
                dk_outs.append(dk_o)
                dks.append(_tn(datt, qb[:, sk]) * e_k[:, sk])
                dss.append(_colsum(ds * s_in))
                dst[d, h] = decay[:, sk] * ds + _tn(dout[:, sv], q_in[:, sk])
            dq, dk_out = jnp.concatenate(dqs, axis=1), jnp.concatenate(dk_outs, axis=1)
            dk = jnp.concatenate(dks, axis=1) + dk_out * e_out
            dq_ref[...] = (dq * scale).astype(ACT)
            dk_ref[...] = dk.astype(ACT)
            db_tot = _colsum(dk_out * k_out_f) + decay * jnp.concatenate(dss, axis=1)
            dg = _x_nn(_tri(GLA_C, d, transpose=True).astype(BF16), dq * q - dk * k) + db_tot
            dp_ref[...] = dg * (1.0 / GLA_TAU) * _sigmoid(-pre)

    in_specs, rbs = _gla_specs(n_lat, n_ctx, step_of)
    in_specs += [pl.BlockSpec((GLA_C, GLA_V), lambda j, rb=rb: (rb(j), 0)) for rb in rbs]
    in_specs += [pl.BlockSpec((2, 1, GLA_H, GLA_DV, GLA_DK), lambda j: (0, step_of(j), 0, 0, 0))]
    out_specs, out_shape = [], []
    for rb in rbs:
        for w, dt in ((GLA_QK, ACT), (GLA_QK, ACT), (GLA_V, ACT), (GLA_QK, F32)):
            out_specs.append(pl.BlockSpec((GLA_C, w), lambda j, rb=rb: (rb(j), 0)))
            out_shape.append(jax.ShapeDtypeStruct((R, w), dt))
    outs = pl.pallas_call(
        body, name="gla_bwd", grid=(n_steps,), in_specs=in_specs, out_specs=out_specs, out_shape=out_shape,
        scratch_shapes=[pltpu.VMEM((2, GLA_H, GLA_DV, GLA_DK), F32)],
        compiler_params=_cparams(("arbitrary",)),
    )(*([parts, parts, parts, sm] * 2), uhi, ulo, bias, do, do, hist)
    return [(outs[k], outs[4 + k]) for k in range(4)]


def _ssd_consts(dt_bias, a_log):
    sel = np.zeros((2, SSM_G, 128, 128), np.float32)
    for d, base in enumerate((SM_DTF, SM_DTB)):
        for g in range(SSM_G):
            for e in range(SSM_HPG):
                sel[d, g, base + SSM_HPG * g + e, e] = 1.0
    e512 = np.zeros((128, SSM_GW), np.float32)
    for e in range(SSM_HPG):
        e512[e, SSM_P * e:SSM_P * (e + 1)] = 1.0
    a_neg = -jnp.exp(a_log)
    pad = lambda v: jnp.pad(v.reshape(2, SSM_G, 1, SSM_HPG), ((0, 0), (0, 0), (0, 0), (0, 128 - SSM_HPG)))
    return dict(
        sel=jnp.asarray(sel, BF16), sel_t=jnp.asarray(sel.transpose(0, 1, 3, 2), BF16), e512_t=jnp.asarray(e512.T, BF16),
        dtb=pad(dt_bias), a=pad(a_neg), a512=jnp.repeat(a_neg, SSM_P, axis=1).reshape(2, SSM_G, 1, SSM_GW))


def _head_columns(x8):
    return [jnp.broadcast_to(x8[:, e:e + 1], (x8.shape[0], 128)) for e in range(SSM_HPG)]


def _head_layout(cols):
    low = lax.broadcasted_iota(jnp.int32, (1, 128), 1) < SSM_P
    return jnp.concatenate([jnp.where(low, cols[2 * j], cols[2 * j + 1]) for j in range(SSM_HPG // 2)], axis=1)


def _ssd_common(sm, sel, dtb, a_neg, a512, d):
    dtr8 = _nn_x(sm, sel) + dtb
    dt8 = _softplus(dtr8)
    a8 = a_neg * dt8
    mask = _tri(SSM_C, d)
    mask_t = _tri(SSM_C, d, transpose=True).astype(BF16)
    cum8 = _x_nn(mask.astype(BF16), a8)
    a_hi, a_lo = _split(a8)
    cum_t = _tn(a_hi, mask_t) + _tn(a_lo, mask_t)
    cum_cols = _head_columns(cum8)
    dt_exp = _head_layout(_head_columns(dt8))
    a_exp = a512 * dt_exp
    return dict(dtr8=dtr8, a8=a8, mask=mask, mask_t=mask_t, cum_t=cum_t, dt_exp=dt_exp, a_exp=a_exp,
                cum_exp=_head_layout(cum_cols), cum_cols=cum_cols, tot_exp=_colsum(a_exp))


def _head_lanes(x, e):
    pair = x[:, 128 * (e // 2):128 * (e // 2 + 1)]
    low = lax.broadcasted_iota(jnp.int32, (1, 128), 1) < SSM_P
    return _bf(jnp.where(low if e % 2 == 0 else jnp.logical_not(low), pair, 0.0))


def _per_head_pairs(fn, x):
    return jnp.concatenate([fn(2 * j, _head_lanes(x, 2 * j)) + fn(2 * j + 1, _head_lanes(x, 2 * j + 1))
                            for j in range(SSM_HPG // 2)], axis=1)


def _ssd_decay(cm, e):
    diff = cm["cum_cols"][e] - cm["cum_t"][e:e + 1, :]
    return jnp.where(cm["mask"], jnp.exp(jnp.minimum(diff, 0.0)), 0.0)


SSM_GPS = 4


def _ssd_specs(n_lat, n_ctx, step_of):
    rbs = [lambda i, d=d: _chunk_row_block(d, step_of(i), n_lat, n_ctx) for d in range(2)]
    xw, nw = SSM_GPS * SSM_GW, SSM_GPS * SSM_N
    specs = []
    for rb in rbs:
        specs += [pl.BlockSpec((SSM_C, xw), lambda g, i, rb=rb: (rb(i), g)),
                  pl.BlockSpec((SSM_C, nw), lambda g, i, rb=rb: (rb(i), SSM_INNER // nw + g)),
                  pl.BlockSpec((SSM_C, nw), lambda g, i, rb=rb: (rb(i), (SSM_INNER + SSM_G * SSM_N) // nw + g)),
                  pl.BlockSpec((SSM_C, 128), lambda g, i, rb=rb: (rb(i), 0))]
    specs += [pl.BlockSpec((2, SSM_GPS, 128, 128), lambda g, i: (0, g, 0, 0)),
              pl.BlockSpec((2, SSM_GPS, 1, 128), lambda g, i: (0, g, 0, 0)),
              pl.BlockSpec((2, SSM_GPS, 1, 128), lambda g, i: (0, g, 0, 0)),
              pl.BlockSpec((2, SSM_GPS, 1, SSM_GW), lambda g, i: (0, g, 0, 0))]
    return specs, rbs


def _ssd_fwd(xbc, sm, k, n_lat, n_ctx, comm):
    R = xbc.shape[0]
    n_steps = n_lat + n_ctx

    def body(*refs):
        ins, (sel_ref, dtb_ref, a_ref, a512_ref), y_refs, hist_ref, st = refs[:8], refs[8:12], refs[12:14], refs[14], refs[15]

        @pl.when(pl.program_id(1) == 0)
        def _():
            st[...] = jnp.zeros_like(st)

        for d in range(2):
            x_ref, b_ref, c_ref, sm_ref = ins[4 * d:4 * d + 4]
            sm = sm_ref[...]
            for gg in range(SSM_GPS):
                sx, sn = slice(gg * SSM_GW, (gg + 1) * SSM_GW), slice(gg * SSM_N, (gg + 1) * SSM_N)
                cm = _ssd_common(sm, sel_ref[d, gg], dtb_ref[d, gg], a_ref[d, gg], a512_ref[d, gg], d)
                bm, cmat = _bf(b_ref[:, sn]), _bf(c_ref[:, sn])
                xdt = x_ref[:, sx].astype(F32) * cm["dt_exp"]
                cb = _nt(cmat, bm)
                ys = _per_head_pairs(lambda e, x_e: _nn(_bf(cb * _ssd_decay(cm, e)), x_e), xdt)
                s_in = st[d, gg]
                hist_ref[d, 0, gg] = s_in
                y = ys + jnp.exp(cm["cum_exp"]) * _nn(cmat, _bf(s_in))
                y_refs[d][:, sx] = y.astype(ACT)
                st[d, gg] = jnp.exp(cm["tot_exp"]) * s_in + _tn(bm, _bf(xdt * jnp.exp(cm["tot_exp"] - cm["cum_exp"])))

    in_specs, rbs = _ssd_specs(n_lat, n_ctx, lambda i: i)
    out_specs = [pl.BlockSpec((SSM_C, SSM_GPS * SSM_GW), lambda g, i, rb=rb: (rb(i), g)) for rb in rbs]
    out_specs += [pl.BlockSpec((2, 1, SSM_GPS, SSM_N, SSM_GW), lambda g, i: (0, i, g, 0, 0))]
    out_shape = [jax.ShapeDtypeStruct((R, SSM_INNER), ACT)] * 2 + [jax.ShapeDtypeStruct((2, n_steps, SSM_G, SSM_N, SSM_GW), F32)]
    args = [xbc, xbc, xbc, sm] * 2 + [k["sel"], k["dtb"], k["a"], k["a512"]]
    n_host_out = len(out_shape)
    outs = pl.pallas_call(
        _hosted(body, len(args), n_host_out, 1, comm, *_ssd_comm_steps(n_steps)), name="ssd_fwd",
        grid=(SSM_G // SSM_GPS, n_steps), in_specs=in_specs + [_ANY] * len(comm.arrays),
        out_specs=out_specs + [_ANY] * len(comm.out_shape), out_shape=out_shape + comm.out_shape,
        scratch_shapes=[pltpu.VMEM((2, SSM_GPS, SSM_N, SSM_GW), F32)] + comm.scratch,
        compiler_params=_cparams(("arbitrary", "arbitrary")),
    )(*args, *comm.arrays)
    return outs[:n_host_out], outs[n_host_out:]


def _ssd_comm_steps(n_steps):
    n_g = SSM_G // SSM_GPS
    at = lambda g, i: (pl.program_id(0) == g) & (pl.program_id(1) == i)
    half = (n_g // 2, n_steps // 2 if n_g % 2 else 0)
    return (lambda: at(0, 0)), (lambda: at(*half)), (lambda: at(n_g - 1, n_steps - 1))


def _ssd_bwd(dy, xbc, sm, k, hist, n_lat, n_ctx, comm):
    R = xbc.shape[0]
    n_steps = n_lat + n_ctx
    step_of = lambda j: n_steps - 1 - j

    def one(d, gg, x_ref, b_ref, c_ref, sm_ref, sel_ref, dtb_ref, a_ref, a512_ref, selt_ref, e512t_ref, dy_ref,
            hist_ref, dx_ref, db_ref, dc_ref, dsm_ref, acc_ref, dst):
        sx, sn = slice(gg * SSM_GW, (gg + 1) * SSM_GW), slice(gg * SSM_N, (gg + 1) * SSM_N)
        a_neg, e512_t = a_ref[d, gg], e512t_ref[...]
        cm = _ssd_common(sm_ref[...], sel_ref[d, gg], dtb_ref[d, gg], a_neg, a512_ref[d, gg], d)
        x, dyv = x_ref[:, sx].astype(F32), dy_ref[:, sx].astype(F32)
        bm, cmat = _bf(b_ref[:, sn]), _bf(c_ref[:, sn])
        xdt = x * cm["dt_exp"]
        cb = _nt(cmat, bm)
        s_in, ds = hist_ref[d, 0, gg], dst[d, gg]
        w = jnp.exp(cm["tot_exp"] - cm["cum_exp"])
        z = _nn(bm, _bf(ds))
        decay_in = jnp.exp(cm["cum_exp"])
        gy = _bf(dyv * decay_in)
        dcb = jnp.zeros((SSM_C, SSM_C), F32)
        crossing = []
        row = lax.broadcasted_iota(jnp.int32, (SSM_C, SSM_C), 0)
        col = lax.broadcasted_iota(jnp.int32, (SSM_C, SSM_C), 1)
        eye = (row == col).astype(BF16)
        before = (cm["mask_t"] - eye)
        xdt_bf = _bf(xdt)

        def head(e, dy_e):
            nonlocal dcb
            lm = _ssd_decay(cm, e)
            m_e = cb * lm
            dm_e = _nt(dy_e, xdt_bf[:, 128 * (e // 2):128 * (e // 2 + 1)])
            dcb = dcb + dm_e * lm
            crossing.append(_bf(dm_e * m_e))
            return _tn(_bf(m_e), dy_e)

        dx_heads = _per_head_pairs(head, dyv)
        through = _nn(jnp.concatenate(crossing, axis=0), before)
        crossing = [_colsum(jnp.where(cm["mask"], through[e * SSM_C:(e + 1) * SSM_C], 0.0)) for e in range(SSM_HPG)]
        da_rows = jnp.concatenate(crossing + [jnp.zeros((128 - SSM_HPG, SSM_C), F32)], axis=0)
        r_hi, r_lo = _split(da_rows)
        da8_intra = _tn(r_hi, eye) + _tn(r_lo, eye)
        dx_state = w * z
        dxdt = dx_heads + dx_state
        dcb = _bf(dcb)
        c_s = _nn(cmat, _bf(s_in))
        dc_ref[:, sn] = _nn(dcb, bm) + _nt(gy, _bf(s_in))
        db_ref[:, sn] = _tn(dcb, cmat) + _nt(_bf(w * xdt), _bf(ds))
        dst[d, gg] = jnp.exp(cm["tot_exp"]) * ds + _tn(cmat, gy)
        state_path = xdt * dx_state
        per_token = _nn_x(jnp.concatenate([dyv * decay_in * c_s - state_path, dxdt * x], axis=0), e512_t)
        totals = jnp.concatenate([_colsum(state_path), _colsum(ds * s_in), jnp.zeros((6, SSM_GW), F32)], axis=0)
        totals = _nn_x(totals, e512_t)
        tot8 = _colsum(cm["a8"])
        dtot8 = totals[0:1] + jnp.exp(tot8) * totals[1:2]
        da8 = da8_intra + _x_nn(cm["mask_t"], per_token[:SSM_C]) + dtot8
        ddt8 = da8 * a_neg + per_token[SSM_C:]
        dsm_ref[gg] = _nn_x(ddt8 * _sigmoid(cm["dtr8"]), selt_ref[d, gg])
        dx_ref[:, sx] = dxdt * cm["dt_exp"]
        acc_ref[d, gg, 0:1, :] += _colsum(da8 * cm["a8"])

    def body(*refs):
        ins, consts, (selt_ref, e512t_ref), dy_refs, hist_ref = refs[:8], refs[8:12], refs[12:14], refs[14:16], refs[16]
        outs, acc_ref, dst = refs[17:25], refs[25], refs[26]

        @pl.when(pl.program_id(1) == 0)
        def _():
            dst[...] = jnp.zeros_like(dst)
            acc_ref[...] = jnp.zeros_like(acc_ref)

        for d in range(2):
            for gg in range(SSM_GPS):
                one(d, gg, *ins[4 * d:4 * d + 4], *consts, selt_ref, e512t_ref, dy_refs[d], hist_ref,
                    *outs[4 * d:4 * d + 4], acc_ref, dst)

    xw, nw = SSM_GPS * SSM_GW, SSM_GPS * SSM_N
    in_specs, rbs = _ssd_specs(n_lat, n_ctx, step_of)
    in_specs += [pl.BlockSpec((2, SSM_GPS, 128, 128), lambda g, j: (0, g, 0, 0)), pl.BlockSpec((SSM_GW, 128), lambda g, j: (0, 0))]
    in_specs += [pl.BlockSpec((SSM_C, xw), lambda g, j, rb=rb: (rb(j), g)) for rb in rbs]
    in_specs += [pl.BlockSpec((2, 1, SSM_GPS, SSM_N, SSM_GW), lambda g, j: (0, step_of(j), g, 0, 0))]
    out_specs, out_shape = [], []
    for rb in rbs:
        out_specs += [pl.BlockSpec((SSM_C, xw), lambda g, j, rb=rb: (rb(j), g)),
                      pl.BlockSpec((SSM_C, nw), lambda g, j, rb=rb: (rb(j), g)),
                      pl.BlockSpec((SSM_C, nw), lambda g, j, rb=rb: (rb(j), g)),
                      pl.BlockSpec((SSM_GPS, SSM_C, 128), lambda g, j, rb=rb: (g, rb(j), 0))]
        out_shape += [jax.ShapeDtypeStruct((R, SSM_INNER), F32), jax.ShapeDtypeStruct((R, SSM_G * SSM_N), F32),
                      jax.ShapeDtypeStruct((R, SSM_G * SSM_N), F32), jax.ShapeDtypeStruct((SSM_G, R, 128), F32)]
    out_specs.append(pl.BlockSpec((2, SSM_GPS, 8, 128), lambda g, j: (0, g, 0, 0)))
    out_shape.append(jax.ShapeDtypeStruct((2, SSM_G, 8, 128), F32))
    args = [xbc, xbc, xbc, sm] * 2 + [k["sel"], k["dtb"], k["a"], k["a512"], k["sel_t"], k["e512_t"], dy, dy, hist]
    n_host_out = len(out_shape)
    outs = pl.pallas_call(
        _hosted(body, len(args), n_host_out, 1, comm, *_ssd_comm_steps(n_steps)), name="ssd_bwd",
        grid=(SSM_G // SSM_GPS, n_steps), in_specs=in_specs + [_ANY] * len(comm.arrays),
        out_specs=out_specs + [_ANY] * len(comm.out_shape), out_shape=out_shape + comm.out_shape,
        scratch_shapes=[pltpu.VMEM((2, SSM_GPS, SSM_N, SSM_GW), F32)] + comm.scratch,
        compiler_params=_cparams(("arbitrary", "arbitrary")),
    )(*args, *comm.arrays)
    return [(outs[n], outs[4 + n]) for n in range(4)] + [outs[8]], outs[n_host_out:]


def _gla_assemble(dq, dk, dv, dparts, tr):
    R = dq[0].shape[0]
    qk = GLA_H * GLA_DK

    def body(dqf_ref, dqb_ref, dkf_ref, dkb_ref, dvf_ref, dvb_ref, _, o_ref):
        o_ref[:, 0:qk] = (_f32(dqf_ref) + _f32(dqb_ref)).astype(BF16)
        o_ref[:, qk:2 * qk] = (_f32(dkf_ref) + _f32(dkb_ref)).astype(BF16)
        o_ref[:, 2 * qk:] = (_f32(dvf_ref) + _f32(dvb_ref)).astype(BF16)

    return pl.pallas_call(
        body, name="gla_assemble", grid=(R // tr,), in_specs=[_row_spec(tr, qk)] * 4 + [_row_spec(tr, D)] * 2 + [_ANY],
        out_specs=_dparts_out(tr, 2 * D, COL_Q), out_shape=jax.ShapeDtypeStruct(dparts.shape, BF16),
        input_output_aliases={6: 0}, compiler_params=_cparams(("arbitrary",)),
    )(*dq, *dk, *dv, dparts)


def _small_assemble(dp, dsm, sm, ut_hi, ut_lo, dparts, tr):
    R = sm.shape[0]
    qk = GLA_H * GLA_DK

    def body(dpf_ref, dpb_ref, dsmf_ref, dsmb_ref, sm_ref, uth_ref, utl_ref, _, o_ref, dup_ref, acc_ref, acc2_ref):
        @pl.when(pl.program_id(0) == 0)
        def _():
            dup_ref[...] = jnp.zeros_like(dup_ref)
            acc_ref[...] = jnp.zeros_like(acc_ref)
            acc2_ref[...] = jnp.zeros_like(acc2_ref)

        ssd = dsmf_ref[0] + dsmb_ref[0]
        for g in range(1, SSM_G):
            ssd = ssd + (dsmf_ref[g] + dsmb_ref[g])
        acc2_ref[0:1, :] += _colsum(ssd)
        sm_hi, sm_lo = _split(sm_ref[...])
        out = ssd
        for d, dp_ref in enumerate((dpf_ref, dpb_ref)):
            dpd = dp_ref[...]
            out = out + _nn3(dpd, uth_ref[d], utl_ref[d])
            p_hi, p_lo = _split(dpd)
            dup_ref[d] += _tn(sm_hi, p_hi) + _tn(sm_lo, p_hi) + _tn(sm_hi, p_lo)
            acc_ref[d:d + 1, :] += _colsum(dpd)
        o_ref[...] = out.astype(BF16)

    return pl.pallas_call(
        body, name="small_assemble", grid=(R // tr,),
        in_specs=[_row_spec(tr, qk)] * 2 + [pl.BlockSpec((SSM_G, tr, 128), lambda i: (0, i, 0))] * 2
        + [_row_spec(tr, 128), pl.BlockSpec((2, qk, 128), lambda i: (0, 0, 0)),
           pl.BlockSpec((2, qk, 128), lambda i: (0, 0, 0)), _ANY],
        out_specs=[_dparts_out(tr, 128, COL_SM), pl.BlockSpec((2, 128, qk), lambda i: (0, 0, 0)), _acc_spec(qk), _acc_spec(128)],
        out_shape=[jax.ShapeDtypeStruct(dparts.shape, BF16), jax.ShapeDtypeStruct((2, 128, qk), F32),
                   jax.ShapeDtypeStruct((8, qk), F32), jax.ShapeDtypeStruct((8, 128), F32)],
        input_output_aliases={7: 0}, compiler_params=_cparams(("arbitrary",)),
    )(*dp, *dsm, sm, ut_hi, ut_lo, dparts)


ADA_ROWS = 16
ADA_TILE = 512


def _dot3_f32(a, b, ca, cb):
    a_hi, a_lo = _split(a)
    b_hi, b_lo = _split(b)
    return _dg(a_hi, b_hi, ca, cb) + _dg(a_lo, b_hi, ca, cb) + _dg(a_hi, b_lo, ca, cb)


def _ada_fwd(cvec, w, b):
    cols = w.shape[1]

    def body(c_ref, w_ref, b_ref, o_ref):
        o_ref[...] = _dot3_f32(_silu(c_ref[...]), w_ref[...], 1, 0) + b_ref[...]

    return pl.pallas_call(
        body, name="ada_fwd", grid=(cols // ADA_TILE,),
        in_specs=[pl.BlockSpec((ADA_ROWS, D), lambda j: (0, 0)), pl.BlockSpec((D, ADA_TILE), lambda j: (0, j)),
                  pl.BlockSpec((1, ADA_TILE), lambda j: (0, j))],
        out_specs=pl.BlockSpec((ADA_ROWS, ADA_TILE), lambda j: (0, j)), out_shape=jax.ShapeDtypeStruct((ADA_ROWS, cols), F32),
        compiler_params=_cparams(("arbitrary",)),
    )(cvec, w, b)


def _adam(w, g, m, v):
    m2 = ADAM_B1 * m + (1.0 - ADAM_B1) * g
    v2 = ADAM_B2 * v + (1.0 - ADAM_B2) * (g * g)
    m_hat = m2 / (1.0 - ADAM_B1 ** ADAM_STEP)
    v_hat = v2 / (1.0 - ADAM_B2 ** ADAM_STEP)
    return -ADAM_LR * (m_hat / (jnp.sqrt(v_hat) + ADAM_EPS) + ADAM_WD * w), m2, v2


def _wada_bwd_adam(cvec, dada, w, m, v):
    rows, cols = w.shape
    tr = _tile(rows, 256, 128)

    def body(c_ref, d_ref, w_ref, m_ref, v_ref, g_ref, dl_ref, m2_ref, v2_ref, p_ref):
        wv = w_ref[...]
        g = _dot3_f32(_silu(c_ref[...]), d_ref[...], 0, 0)
        g_ref[...] = g
        dl_ref[...], m2_ref[...], v2_ref[...] = _adam(wv, g, m_ref[...], v_ref[...])
        p_ref[...] = _dot3_f32(d_ref[...], wv, 1, 1)

    blk = pl.BlockSpec((tr, cols), lambda i: (i, 0))
    return pl.pallas_call(
        body, name="wada_bwd_adam", grid=(rows // tr,),
        in_specs=[pl.BlockSpec((ADA_ROWS, tr), lambda i: (0, i)), pl.BlockSpec((ADA_ROWS, cols), lambda i: (0, 0)), blk, blk, blk],
        out_specs=[blk, blk, blk, blk, pl.BlockSpec((ADA_ROWS, tr), lambda i: (0, i))],
        out_shape=[jax.ShapeDtypeStruct((rows, cols), F32)] * 4 + [jax.ShapeDtypeStruct((ADA_ROWS, rows), F32)],
        compiler_params=_cparams(("arbitrary",)),
    )(cvec, dada, w, m, v)


def _reduce_adam(parts8, w, m, v, name):
    rows, cols = w.shape
    tr = _tile(rows, 64, 16)

    def body(p_ref, w_ref, m_ref, v_ref, g_ref, dl_ref, m2_ref, v2_ref):
        g = p_ref[0].astype(F32) + p_ref[N_CHIPS].astype(F32)
        for j in range(1, N_CHIPS):
            g = g + (p_ref[j].astype(F32) + p_ref[N_CHIPS + j].astype(F32))
        g_ref[...] = g
        dl_ref[...], m2_ref[...], v2_ref[...] = _adam(w_ref[...], g, m_ref[...], v_ref[...])

    blk = pl.BlockSpec((tr, cols), lambda i: (i, 0))
    return pl.pallas_call(
        body, name=name, grid=(rows // tr,), in_specs=[pl.BlockSpec((N_DEV, tr, cols), lambda i: (0, i, 0)), blk, blk, blk],
        out_specs=[blk] * 4, out_shape=[jax.ShapeDtypeStruct((rows, cols), F32)] * 4, compiler_params=_cparams(("arbitrary",)),
    )(parts8, w, m, v)


SMALL_W = 1024


def _sum8(g8):
    rows = g8.shape[1]

    def body(g_ref, o_ref):
        s = g_ref[0]
        for j in range(1, N_DEV):
            s = s + g_ref[j]
        o_ref[...] = s

    return pl.pallas_call(
        body, name="sum8", out_shape=jax.ShapeDtypeStruct((rows, SMALL_W), F32),
        in_specs=[pl.BlockSpec(memory_space=pltpu.VMEM)], out_specs=pl.BlockSpec(memory_space=pltpu.VMEM),
        compiler_params=pltpu.CompilerParams(vmem_limit_bytes=VMEM_LIMIT),
    )(g8)


def _cctx_grad(p8, c_ctx):
    def body(p_ref, c_ref, o_ref):
        s = p_ref[0]
        for chip in range(1, N_CHIPS):
            s = s + p_ref[2 * chip]
        o_ref[...] = s * _dsilu(c_ref[...])

    return pl.pallas_call(
        body, name="cctx_grad", out_shape=jax.ShapeDtypeStruct((1, D), F32),
        in_specs=[pl.BlockSpec(memory_space=pltpu.VMEM)] * 2, out_specs=pl.BlockSpec(memory_space=pltpu.VMEM),
    )(p8, c_ctx)


def _adam_small(w, g, m, v):
    def body(w_ref, g_ref, m_ref, v_ref, dl_ref, m2_ref, v2_ref):
        dl_ref[...], m2_ref[...], v2_ref[...] = _adam(w_ref[...], g_ref[...], m_ref[...], v_ref[...])

    vm = pl.BlockSpec(memory_space=pltpu.VMEM)
    return pl.pallas_call(
        body, name="adam_small", out_shape=[jax.ShapeDtypeStruct(w.shape, F32)] * 3, in_specs=[vm] * 4, out_specs=[vm] * 3,
        compiler_params=pltpu.CompilerParams(vmem_limit_bytes=VMEM_LIMIT),
    )(w, g, m, v)


def _pack(vecs, width=SMALL_W, row_mult=8):
    flat = jnp.concatenate([v.reshape(-1).astype(F32) for v in vecs])
    n = flat.shape[0]
    rows = -(-n // (width * row_mult)) * row_mult
    return jnp.pad(flat, (0, rows * width - n)).reshape(rows, width)


def _unpack(packed, shapes):
    flat = packed.reshape(-1)
    out, off = [], 0
    for s in shapes:
        n = int(np.prod(s))
        out.append(flat[off:off + n].reshape(s))
        off += n
    return out


WEIGHTS = ('c_ctx', 'w_ada', 'b_ada', 'norm1_w', 'w_in', 'gla_up_f', 'gla_bias_f', 'gla_up_b', 'gla_bias_b', 'gla_norm_w',
           'conv_w', 'conv_b', 'dt_bias_f', 'dt_bias_b', 'a_log_f', 'a_log_b', 'd_skip', 'ssm_norm_w', 'w_pa', 'w_pb', 'w_out',
           'norm2_w', 'w_gate', 'w_up', 'w_down', 'final_norm_w')
BIG = ('w_in', 'w_pa', 'w_pb', 'w_out', 'w_gate', 'w_up', 'w_down')
COL_SHARDED = ('w_in', 'w_gate', 'w_up')
SMALL_SHARDED = ('gla_up_f', 'gla_up_b', 'conv_w')
ROW_TILE = 256
MM_ROWS = 1408
MM_ROWS_SMALL = 768
MM_KROWS = 2816


def _blocks_to_full(g4, name):
    n, r, c = g4.shape
    return g4.transpose(1, 0, 2).reshape(r, n * c) if name in COL_SHARDED else g4.reshape(n * r, c)


def _full_to_blocks(full, name):
    r, c = full.shape
    if name in COL_SHARDED:
        return full.reshape(r, N_CHIPS, c // N_CHIPS).transpose(1, 0, 2)
    return full.reshape(N_CHIPS, r // N_CHIPS, c)


def _permute_in(w_in_full):
    off = np.concatenate([[0], np.cumsum(IN_WIDTHS)])
    cols = [w_in_full[:, off[p]:off[p + 1]] for p in PERM]
    return jnp.concatenate(cols + [jnp.zeros((w_in_full.shape[0], SMALL_PAD), w_in_full.dtype)], axis=1)


def _unpermute_in(wp):
    off = np.concatenate([[0], np.cumsum([IN_WIDTHS[p] for p in PERM])])
    pieces = {p: wp[:, off[i]:off[i + 1]] for i, p in enumerate(PERM)}
    return jnp.concatenate([pieces[p] for p in range(len(IN_WIDTHS))], axis=1)


def _chip_cols(full, chip, n):
    return lax.dynamic_slice_in_dim(full, chip * n, n, axis=1)


def kernel(x, c, ctx, c_ctx, w_ada, b_ada, norm1_w, w_in, gla_up_f, gla_bias_f, gla_up_b, gla_bias_b, gla_norm_w, conv_w, conv_b, dt_bias_f, dt_bias_b, a_log_f, a_log_b, d_skip, ssm_norm_w, w_pa, w_pb, w_out, norm2_w, w_gate, w_up, w_down, final_norm_w, loss_target, m_c_ctx, m_w_ada, m_b_ada, m_norm1_w, m_w_in, m_gla_up_f, m_gla_bias_f, m_gla_up_b, m_gla_bias_b, m_gla_norm_w, m_conv_w, m_conv_b, m_dt_bias_f, m_dt_bias_b, m_a_log_f, m_a_log_b, m_d_skip, m_ssm_norm_w, m_w_pa, m_w_pb, m_w_out, m_norm2_w, m_w_gate, m_w_up, m_w_down, m_final_norm_w, v_c_ctx, v_w_ada, v_b_ada, v_norm1_w, v_w_in, v_gla_up_f, v_gla_bias_f, v_gla_up_b, v_gla_bias_b, v_gla_norm_w, v_conv_w, v_conv_b, v_dt_bias_f, v_dt_bias_b, v_a_log_f, v_a_log_b, v_d_skip, v_ssm_norm_w, v_w_pa, v_w_pb, v_w_out, v_norm2_w, v_w_gate, v_w_up, v_w_down, v_final_norm_w):
    given = dict(locals())
    W = {n: given[n] for n in WEIGHTS}
    M = {n: given["m_" + n] for n in WEIGHTS}
    V = {n: given["v_" + n] for n in WEIGHTS}
    L, Lc = x.shape[1], ctx.shape[1]
    tr = ROW_TILE
    assert L % tr == 0 and Lc % tr == 0 and L % Lc == 0 and Lc % SSM_C == 0
    n_lat_tiles = L // tr
    xi, yi, ci = _place()
    chip, me = 2 * xi + yi, 4 * xi + 2 * yi + ci
    x2, ctx2 = x[0], ctx[0]

    g0 = _allgather_small(_pack([c[0]] + [W[n][0] for n in SMALL_SHARDED]), "gather_c")
    g0 = g0.reshape(N_DEV, -1)
    c_all = g0[:, :D]
    small_full, off = {}, D
    for n in SMALL_SHARDED:
        r, cols = W[n].shape[1:]
        small_full[n] = jnp.concatenate([g0[2 * k, off:off + r * cols].reshape(r, cols) for k in range(N_CHIPS)], axis=1)
        off += r * cols
    up_f, up_b, conv_w_full = (small_full[n] for n in SMALL_SHARDED)

    cvec = jnp.zeros((ADA_ROWS, D), F32).at[:N_DEV].set(c_all).at[N_DEV].set(c_ctx)
    ada_cols = w_ada.shape[2]
    ada_part = _ada_fwd(cvec, w_ada[0], _chip_cols(b_ada, chip, ada_cols))
    g1_all = _allgather_small(ada_part, "gather_ada")
    ada_full = jnp.concatenate([g1_all[2 * k] for k in range(N_CHIPS)], axis=1)
    mine = lax.dynamic_slice_in_dim(ada_full, me, 1, axis=0)
    sh1, sc1, g1, sh2, sc2, g2 = (mine[:, k * D:(k + 1) * D] for k in range(6))
    csh1, csc1 = ada_full[N_DEV:N_DEV + 1, :D], ada_full[N_DEV:N_DEV + 1, D:2 * D]
    mod = jnp.stack([jnp.stack([sh1, sc1]), jnp.stack([csh1, csc1])])

    full = {'w_in': _blocks_to_full(_gather_split(w_in[0].astype(BF16), "gather_w_in"), 'w_in')}
    wp = _permute_in(full['w_in'])
    later = [n for n in BIG if n != 'w_in']

    def lr_rows(up, base):
        return jnp.zeros((128, GLA_H * GLA_DK), F32).at[base:base + GLA_RANK].set(up)
    u2 = jnp.stack([lr_rows(up_f, SM_LRF), lr_rows(up_b, SM_LRB)])
    u2_hi = u2.astype(BF16)
    u2_lo = (u2 - u2_hi.astype(F32)).astype(BF16)
    ut = u2.transpose(0, 2, 1)
    ut_hi = ut.astype(BF16)
    ut_lo = (ut - ut_hi.astype(F32)).astype(BF16)
    gbias = jnp.stack([gla_bias_f, gla_bias_b])
    kc = _ssd_consts(jnp.stack([dt_bias_f[0], dt_bias_b[0]]), jnp.stack([a_log_f[0], a_log_b[0]]))
    gw4 = jnp.tile(gla_norm_w, (1, GLA_H))
    dskip_exp = jnp.repeat(d_skip, SSM_P, axis=1)
    n_gla = (L // GLA_C, Lc // GLA_C)
    n_ssd = (L // SSM_C, Lc // SSM_C)

    h1 = _norm_mod(x2, ctx2, norm1_w, mod, n_lat_tiles, tr)
    parts = _mm(h1, wp, "nn", ACT, "mm_in", tm=MM_ROWS, tn=PW // 3)
    sm = _mm(h1, wp[:, COL_SM:], "nn", F32, "mm_in_small", tm=MM_ROWS)
    xbc = _conv_fwd(parts, conv_w_full, conv_b, L // Lc, Lc)
    *o2, gla_hist = _gla_fwd(parts, sm, u2_hi, u2_lo, gbias, *n_gla)
    (*y2, ssd_hist), gathered = _ssd_fwd(xbc, sm, kc, *n_ssd, _gather_comm([W[n][0].astype(BF16) for n in later]))
    full.update({n: _blocks_to_full(g, n) for n, g in zip(later, gathered)})
    oan = _gla_out(o2, parts, gw4, tr)
    obn = _ssd_out(y2, xbc, parts, dskip_exp, ssm_norm_w, tr)
    ya = _mm(oan, full['w_pa'], "nn", ACT, "mm_pa", tm=MM_ROWS)
    yb = _mm(obn, full['w_pb'], "nn", ACT, "mm_pb", tm=MM_ROWS)
    merged = _merge(ya, yb, parts, tr)
    mix = _mm(merged, full['w_out'], "nn", ACT, "mm_out", tm=MM_ROWS)
    h2, u = _resid_norm_mod(x2, ctx2, mix, g1, norm2_w, sh2, sc2, n_lat_tiles, tr)
    gp = _mm(u, full['w_gate'], "nn", ACT, "mm_gate", tm=MM_ROWS, tn=D_FF)
    up = _mm(u, full['w_up'], "nn", ACT, "mm_up", tm=MM_ROWS, tn=D_FF)
    f = _mm(gp, full['w_down'], "nn", ACT, "mm_down", tm=MM_ROWS_SMALL, tk=D_FF // 2, swiglu_up=up)
    dh3, df, acc_loss = _loss_head(h2, f, loss_target[0], g2, final_norm_w[None], n_lat_tiles, tr)

    dw = {}
    da = _mm(df, full['w_down'], "nt", ACT, "mm_down_dx", tm=MM_ROWS, tn=D_FF)
    dw['w_down'] = _mm(gp, df, "tn", BF16, "mm_down_dw", tm=D_FF // 2, tk=MM_ROWS, swiglu_up=up)
    dgp, dup = _swiglu_act_bwd(da, gp, up, tr)
    du_a = _mm(dgp, full['w_gate'], "nt", ACT, "mm_gate_dx", tm=MM_ROWS, tk=D_FF)
    du_b = _mm(dup, full['w_up'], "nt", ACT, "mm_up_dx", tm=MM_ROWS, tk=D_FF)
    dw['w_gate'] = _mm(u, dgp, "tn", BF16, "mm_gate_dw", tm=D, tn=D_FF // 2, tk=MM_KROWS)
    dw['w_up'] = _mm(u, dup, "tn", BF16, "mm_up_dw", tm=D, tn=D_FF // 2, tk=MM_KROWS)
    dh2, dmix, acc_ffn = _ffn_in_bwd(du_a, du_b, h2, dh3, mix, sc2, g1, norm2_w, tr)
    dmerged = _mm(dmix, full['w_out'], "nt", ACT, "mm_out_dx", tm=MM_ROWS)
    dw['w_out'] = _mm(merged, dmix, "tn", BF16, "mm_out_dw", tm=D, tk=MM_KROWS)
    dya, dyb, dparts = _merge_bwd(dmerged, ya, yb, parts, lax.empty((L + Lc, PW), BF16), tr)
    doan = _mm(dya, full['w_pa'], "nt", ACT, "mm_pa_dx", tm=MM_ROWS)
    dw['w_pa'] = _mm(oan, dya, "tn", BF16, "mm_pa_dw", tm=D, tk=MM_KROWS)
    dobn = _mm(dyb, full['w_pb'], "nt", ACT, "mm_pb_dx", tm=MM_ROWS)
    dw['w_pb'] = _mm(obn, dyb, "tn", BF16, "mm_pb_dw", tm=D, tk=MM_KROWS)
    do, dparts, acc_gla = _gla_out_bwd(doan, o2, parts, gw4, dparts, tr)
    dq, dk, dv, dpre = _gla_bwd(do, parts, sm, u2_hi, u2_lo, gbias, gla_hist, *n_gla)
    dy, dparts, acc_ssd = _ssd_out_bwd(dobn, y2, xbc, parts, dskip_exp, ssm_norm_w, dparts, tr)
    (dx_scan, db_scan, dc_scan, dsm, acc_alog), exchanged = _ssd_bwd(
        dy, xbc, sm, kc, ssd_hist, *n_ssd, _exchange_comm([_full_to_blocks(dw[n], n) for n in later]))
    exchanged = dict(zip(later, exchanged))
    dparts, acc_conv = _conv_bwd(dx_scan, db_scan, dc_scan, dy, dskip_exp, parts, conv_w_full, conv_b, dparts, L // Lc, Lc)
    dparts = _gla_assemble(dq, dk, dv, dparts, tr)
    dparts, dup_gla, acc_gbias, acc_dtb = _small_assemble(dpre, dsm, sm, ut_hi, ut_lo, dparts, tr)
    dw['w_in'] = _unpermute_in(_mm(h1, dparts, "tn", BF16, "mm_in_dw", tm=D, tn=PW // 9, tk=MM_KROWS))
    dh1, (exchanged['w_in'],) = _mm(dparts, wp, "nt", F32, "mm_in_dx", tm=MM_ROWS_SMALL, tk=PW // 3,
                                    comm=_exchange_comm([_full_to_blocks(dw['w_in'], 'w_in')]))
    dx, acc_n1 = _norm1_bwd(dh1, x2, ctx2, dh2, norm1_w, mod, n_lat_tiles, tr)

    partial = dict(
        norm1_w=acc_n1[0, 2] + acc_n1[1, 2],
        gla_up_f=dup_gla[0, SM_LRF:SM_LRF + GLA_RANK], gla_bias_f=acc_gbias[0],
        gla_up_b=dup_gla[1, SM_LRB:SM_LRB + GLA_RANK], gla_bias_b=acc_gbias[1],
        gla_norm_w=acc_gla[0].reshape(GLA_H, GLA_DV).sum(0),
        conv_w=acc_conv[:SSM_CONV], conv_b=acc_conv[SSM_CONV],
        dt_bias_f=acc_dtb[0, SM_DTF:SM_DTF + SSM_HEADS], dt_bias_b=acc_dtb[0, SM_DTB:SM_DTB + SSM_HEADS],
        a_log_f=acc_alog[0, :, 0, :SSM_HPG], a_log_b=acc_alog[1, :, 0, :SSM_HPG],
        d_skip=acc_ssd[1].reshape(SSM_HEADS, SSM_P).sum(1), ssm_norm_w=acc_ssd[0],
        norm2_w=acc_ffn[2], final_norm_w=acc_loss[0],
    )
    dada = jnp.concatenate([acc_n1[0, 1], acc_n1[0, 0], acc_ffn[3], acc_ffn[1], acc_ffn[0], acc_loss[1]])
    dada_ctx = jnp.concatenate([acc_n1[1, 1], acc_n1[1, 0], jnp.zeros((4 * D,), F32)])
    names = list(partial)
    payload = [partial[n] for n in names] + [dada + dada_ctx, dada_ctx, acc_loss[2], dada]
    sizes = [int(np.prod(p.shape)) for p in payload]
    g8 = _allgather_small(_pack(payload), "gather_small_grads")
    summed = _unpack(_sum8(g8), [(s,) for s in sizes])
    grads = {n: s.reshape(W[n].shape if n not in SMALL_SHARDED else partial[n].shape) for n, s in zip(names, summed)}
    grads['b_ada'] = summed[len(names)].reshape(b_ada.shape)
    dada_ctx_sum = summed[len(names) + 1]
    loss = jnp.sum(summed[len(names) + 2])
    dada_all = g8.reshape(N_DEV, -1)[:, sum(sizes[:-1]):sum(sizes)]

    dada16 = jnp.zeros((ADA_ROWS, ada_cols), F32)
    dada16 = dada16.at[:N_DEV].set(_chip_cols(dada_all, chip, ada_cols)).at[N_DEV].set(_chip_cols(dada_ctx_sum[None], chip, ada_cols)[0])
    g_wada, dl_wada, m_wada, v_wada, p16 = _wada_bwd_adam(cvec, dada16, w_ada[0], m_w_ada[0], v_w_ada[0])
    p8 = _allgather_small(p16[N_DEV:], "gather_cctx")
    grads['c_ctx'] = _cctx_grad(p8[:, 0:1, :], c_ctx[None])[0]
    for n in SMALL_SHARDED:
        grads[n] = _chip_cols(grads[n], chip, W[n].shape[2])[None]

    small = [n for n in WEIGHTS if n not in BIG and n != 'w_ada']
    shapes = [W[n].shape for n in small]
    dl_s, m_s, v_s = _adam_small(*[_pack([d[n] for n in small]) for d in (W, grads, M, V)])
    delta = dict(zip(small, _unpack(dl_s, shapes)))
    new_m = dict(zip(small, _unpack(m_s, shapes)))
    new_v = dict(zip(small, _unpack(v_s, shapes)))
    grads['w_ada'], delta['w_ada'], new_m['w_ada'], new_v['w_ada'] = g_wada[None], dl_wada[None], m_wada[None], v_wada[None]

    for n in BIG:
        g, dl, m2, v2 = _reduce_adam(exchanged[n], W[n][0], M[n][0], V[n][0], "adam_" + n)
        grads[n], delta[n], new_m[n], new_v[n] = g[None], dl[None], m2[None], v2[None]

    return (loss, dx[None], *[grads[n] for n in WEIGHTS], *[delta[n] for n in WEIGHTS],
            *[new_m[n] for n in WEIGHTS], *[new_v[n] for n in WEIGHTS])
```

```python
import numpy as np
import jax
import jax.numpy as jnp
from jax import lax
from jax.experimental import pallas as pl
from jax.experimental.pallas import tpu as pltpu

F32 = jnp.float32
BF16 = jnp.bfloat16
ACT = BF16
MESH = pl.DeviceIdType.MESH

D = 1024
EPS = 1e-6
GRID_W = 64
GLA_H, GLA_DK, GLA_DV, GLA_RANK, GLA_TAU = 4, 128, 256, 16, 16.0
GLA_C = 128
SSM_INNER, SSM_P, SSM_HEADS, SSM_G, SSM_HPG, SSM_N = 2048, 64, 32, 4, 8, 128
SSM_C = 128
SSM_CONV, CONV_LEFT = 4, 2
D_FF = 2816
IN_WIDTHS = (512, 512, 1024, 1024, 16, 16, 2048, 2048, 512, 512, 32, 32, 1024, 1024)
D_IN = sum(IN_WIDTHS)
PERM = (6, 7, 8, 9, 3, 0, 1, 2, 12, 13, 4, 5, 10, 11)
PW = 10368
SMALL_PAD = PW - D_IN
COL_Z, COL_XBC, COL_R, COL_Q, COL_K, COL_V, COL_GA, COL_GB, COL_SM = 0, 2048, 5120, 6144, 6656, 7168, 8192, 9216, 10240
SM_LRF, SM_LRB, SM_DTF, SM_DTB = 0, 16, 32, 64
EXP_CLAMP = 80.0
ADAM_LR, ADAM_B1, ADAM_B2, ADAM_EPS, ADAM_WD, ADAM_STEP = 0.001, 0.9, 0.999, 1e-08, 0.01, 10
N_CHIPS, N_DEV = 4, 8
VMEM_LIMIT = 56 * 1024 * 1024


def _cparams(sem=None):
    return pltpu.CompilerParams(dimension_semantics=sem, vmem_limit_bytes=VMEM_LIMIT)


def _dg(a, b, ca, cb):
    return lax.dot_general(a, b, (((ca,), (cb,)), ((), ())), preferred_element_type=F32)


def _nn(a, b):
    return _dg(a, b, 1, 0)


def _nt(a, b):
    return _dg(a, b, 1, 1)


def _tn(a, b):
    return _dg(a, b, 0, 0)


def _bf(x):
    return x.astype(BF16)


def _f32(ref):
    return ref[...].astype(F32)


def _split(x):
    hi = x.astype(BF16)
    return hi, (x - hi.astype(F32)).astype(BF16)


def _nn_x(a, b_exact):
    hi, lo = _split(a)
    return _nn(hi, b_exact) + _nn(lo, b_exact)


def _x_nn(a_exact, b):
    hi, lo = _split(b)
    return _nn(a_exact, hi) + _nn(a_exact, lo)


def _nn3(a, b_hi, b_lo):
    hi, lo = _split(a)
    return _nn(hi, b_hi) + _nn(lo, b_hi) + _nn(hi, b_lo)


def _sigmoid(x):
    return 1.0 / (1.0 + jnp.exp(-x))


def _silu(x):
    return x * _sigmoid(x)


def _dsilu(x):
    s = _sigmoid(x)
    return s * (1.0 + x * (1.0 - s))


def _softplus(x):
    return jnp.maximum(x, 0.0) + jnp.log(1.0 + jnp.exp(-jnp.abs(x)))


def _log_sigmoid(x):
    return jnp.minimum(x, 0.0) - jnp.log(1.0 + jnp.exp(-jnp.abs(x)))


def _tile(n, target, mult=8):
    best = None
    for t in range(mult, min(n, target) + 1, mult):
        if n % t == 0:
            best = t
    assert best is not None, (n, target, mult)
    return best


def _mm(a, b, mode, out_dtype, name, tm=512, tn=1024, tk=2048, comm=None, swiglu_up=None):
    if mode == "nn":
        (M, K), N = a.shape, b.shape[1]
    elif mode == "nt":
        (M, K), N = a.shape, b.shape[0]
    else:
        (K, M), N = a.shape, b.shape[1]
    tm, tn, tk = _tile(M, tm, 128), _tile(N, tn, 128), _tile(K, tk, 128)
    nk = K // tk
    ca, cb = {"nn": (1, 0), "nt": (1, 1), "tn": (0, 0)}[mode]

    def body(a_ref, *rest):
        if swiglu_up is None:
            (b_ref, o_ref, *acc), av = rest, a_ref[...]
        else:
            (up_ref, b_ref, o_ref, *acc) = rest
            av = (_silu(a_ref[...].astype(F32)) * up_ref[...].astype(F32)).astype(BF16)
        part = _dg(av, b_ref[...], ca, cb)
        if nk == 1:
            o_ref[...] = part.astype(out_dtype)
        else:
            k = pl.program_id(2)

            @pl.when(k == 0)
            def _():
                acc[0][...] = part

            @pl.when(k > 0)
            def _():
                acc[0][...] += part

            @pl.when(k == nk - 1)
            def _():
                o_ref[...] = acc[0][...].astype(out_dtype)

    a_spec = pl.BlockSpec((tk, tm), lambda i, j, k: (k, i)) if mode == "tn" else pl.BlockSpec((tm, tk), lambda i, j, k: (i, k))
    b_spec = pl.BlockSpec((tn, tk), lambda i, j, k: (j, k)) if mode == "nt" else pl.BlockSpec((tk, tn), lambda i, j, k: (k, j))
    gi, gj = M // tm, N // tn
    scratch = [pltpu.VMEM((tm, tn), F32)] if nk > 1 else []
    out_spec, out_shape = pl.BlockSpec((tm, tn), lambda i, j, k: (i, j)), jax.ShapeDtypeStruct((M, N), out_dtype)
    if comm is None:
        lhs = [a] if swiglu_up is None else [a, swiglu_up]
        return pl.pallas_call(
            body, name=name, grid=(gi, gj, nk), in_specs=[a_spec] * len(lhs) + [b_spec], out_specs=out_spec,
            out_shape=out_shape, scratch_shapes=scratch, compiler_params=_cparams(("arbitrary", "arbitrary", "arbitrary")),
        )(*lhs, b)
    assert swiglu_up is None
    at = lambda i, j, k: (pl.program_id(0) == i) & (pl.program_id(1) == j) & (pl.program_id(2) == k)
    hosted = _hosted(body, 2, 1, len(scratch), comm, lambda: at(0, 0, 0), lambda: at(gi - 1, 0, 0),
                     lambda: at(gi - 1, gj - 1, nk - 1))
    outs = pl.pallas_call(
        hosted, name=name, grid=(gi, gj, nk), in_specs=[a_spec, b_spec] + [_ANY] * len(comm.arrays),
        out_specs=[out_spec] + [_ANY] * len(comm.out_shape), out_shape=[out_shape] + comm.out_shape,
        scratch_shapes=scratch + comm.scratch, compiler_params=_cparams(("arbitrary", "arbitrary", "arbitrary")),
    )(a, b, *comm.arrays)
    return outs[0], outs[1:]


def _place():
    return lax.axis_index("x"), lax.axis_index("y"), lax.axis_index("c")


def _flip(v, bit):
    return 1 - v if bit else v


def _allgather_small(v, name):
    R, C = v.shape

    def body(v_ref, out_ref, send_sems, recv_sems, local_sem):
        x, y, c = _place()
        me = 4 * x + 2 * y + c
        mine = pltpu.make_async_copy(v_ref, out_ref.at[me], local_sem)
        mine.start()

        def peer(r):
            return _flip(x, (r >> 2) & 1), _flip(y, (r >> 1) & 1), _flip(c, r & 1)

        sends = [pltpu.make_async_remote_copy(
            src_ref=v_ref, dst_ref=out_ref.at[me], send_sem=send_sems.at[r - 1], recv_sem=recv_sems.at[r - 1],
            device_id=peer(r), device_id_type=MESH) for r in range(1, N_DEV)]
        for cp in sends:
            cp.start()
        for r in range(1, N_DEV):
            px, py, pc = peer(r)
            pltpu.make_async_remote_copy(
                src_ref=v_ref, dst_ref=out_ref.at[4 * px + 2 * py + pc], send_sem=send_sems.at[r - 1],
                recv_sem=recv_sems.at[r - 1], device_id=(x, y, c), device_id_type=MESH).wait_recv()
        for cp in sends:
            cp.wait_send()
        mine.wait()

    return pl.pallas_call(
        body, name=name, out_shape=jax.ShapeDtypeStruct((N_DEV, R, C), v.dtype),
        in_specs=[pl.BlockSpec(memory_space=pltpu.VMEM)], out_specs=pl.BlockSpec(memory_space=pltpu.VMEM),
        scratch_shapes=[pltpu.SemaphoreType.DMA((N_DEV - 1,)), pltpu.SemaphoreType.DMA((N_DEV - 1,)), pltpu.SemaphoreType.DMA],
        compiler_params=pltpu.CompilerParams(vmem_limit_bytes=VMEM_LIMIT),
    )(v)


_CHIP_RELATIONS = ((1, 0), (0, 1), (1, 1))


class _Comm:
    def __init__(self, arrays, out_shape, scratch, start, middle, finish):
        self.arrays, self.out_shape, self.scratch = arrays, out_shape, scratch
        self.start, self.middle, self.finish = start, middle, finish


def _hosted(body, n_in, n_out, n_scratch, comm, first, middle, last):
    nc, no = len(comm.arrays), len(comm.out_shape)

    def wrapped(*refs):
        a = n_in + nc
        b = a + n_out + no
        ins, c_ins, outs, c_outs = refs[:n_in], refs[n_in:a], refs[a:a + n_out], refs[a + n_out:b]
        scratch, c_sems = refs[b:b + n_scratch], refs[b + n_scratch:]

        @pl.when(first())
        def _():
            comm.start(c_ins, c_outs, c_sems)

        body(*ins, *outs, *scratch)
        if comm.middle is not None:
            @pl.when(middle())
            def _():
                comm.middle(c_ins, c_outs, c_sems)

        @pl.when(last())
        def _():
            comm.finish(c_ins, c_outs, c_sems)

    return wrapped


_ANY = pl.BlockSpec(memory_space=pl.ANY)


def _gather_split_comm(shard):
    rows, cols = shard.shape
    half = rows // 2

    def copies(kind, ins, outs, sems):
        (in_ref,), (out_ref,), (send_sems, recv_sems, local_sem) = ins, outs, sems
        x, y, c = _place()
        chip = 2 * x + y
        mine = pl.ds(pl.multiple_of(c * half, 16), half)
        if kind == "local":
            return [pltpu.make_async_copy(in_ref, out_ref.at[chip], local_sem)]
        made = []
        for j, (fx, fy) in enumerate(_CHIP_RELATIONS):
            px, py = _flip(x, fx), _flip(y, fy)
            landed = out_ref.at[2 * px + py, mine]
            src, dst, k, to = {
                "send": (in_ref.at[mine], out_ref.at[chip, mine], j, (px, py, c)),
                "landed": (landed, landed, j, (x, y, c)),
                "passed": (landed, landed, 3 + j, (x, y, 1 - c)),
                "arrivals": (landed, landed, 3 + j, (x, y, c)),
            }[kind]
            made.append(pltpu.make_async_remote_copy(src_ref=src, dst_ref=dst, send_sem=send_sems.at[k],
                                                     recv_sem=recv_sems.at[k], device_id=to, device_id_type=MESH))
        return made

    def start(ins, outs, sems):
        for cp in copies("local", ins, outs, sems) + copies("send", ins, outs, sems):
            cp.start()

    def middle(ins, outs, sems):
        for got, fwd in zip(copies("landed", ins, outs, sems), copies("passed", ins, outs, sems)):
            got.wait_recv()
            fwd.start()

    def finish(ins, outs, sems):
        for cp in copies("arrivals", ins, outs, sems):
            cp.wait_recv()
        for cp in copies("send", ins, outs, sems) + copies("passed", ins, outs, sems):
            cp.wait_send()
        for cp in copies("local", ins, outs, sems):
            cp.wait()

    return _Comm([shard], [jax.ShapeDtypeStruct((N_CHIPS, rows, cols), shard.dtype)],
                 [pltpu.SemaphoreType.DMA((6,)), pltpu.SemaphoreType.DMA((6,)), pltpu.SemaphoreType.DMA],
                 start, middle, finish)


def _gather_comm(shards):
    n = len(shards)

    def copies(kind, ins, outs, sems):
        send_sems, recv_sems, local_sems = sems
        x, y, c = _place()
        chip = 2 * x + y
        if kind == "local":
            return [pltpu.make_async_copy(ins[i], outs[i].at[chip], local_sems.at[i]) for i in range(n)]
        made = []
        for i in range(n):
            for j, (fx, fy) in enumerate(_CHIP_RELATIONS):
                px, py = _flip(x, fx), _flip(y, fy)
                slot, to = (chip, (px, py, c)) if kind == "send" else (2 * px + py, (x, y, c))
                made.append(pltpu.make_async_remote_copy(
                    src_ref=ins[i], dst_ref=outs[i].at[slot], send_sem=send_sems.at[i, j], recv_sem=recv_sems.at[i, j],
                    device_id=to, device_id_type=MESH))
        return made

    def start(ins, outs, sems):
        for cp in copies("local", ins, outs, sems) + copies("send", ins, outs, sems):
            cp.start()

    def finish(ins, outs, sems):
        for cp in copies("recv", ins, outs, sems):
            cp.wait_recv()
        for cp in copies("send", ins, outs, sems):
            cp.wait_send()
        for cp in copies("local", ins, outs, sems):
            cp.wait()

    return _Comm(list(shards), [jax.ShapeDtypeStruct((N_CHIPS,) + s.shape, s.dtype) for s in shards],
                 [pltpu.SemaphoreType.DMA((n, 3)), pltpu.SemaphoreType.DMA((n, 3)), pltpu.SemaphoreType.DMA((n,))],
                 start, None, finish)


def _exchange_comm(blocks):
    n = len(blocks)

    def copies(kind, ins, outs, sems):
        send_sems, recv_sems, local_sems = sems
        x, y, c = _place()
        chip = 2 * x + y
        me, sibling = (x, y, c), (x, y, 1 - c)

        def remote(src, dst, i, j, to):
            return pltpu.make_async_remote_copy(src_ref=src, dst_ref=dst, send_sem=send_sems.at[i, j],
                                                recv_sem=recv_sems.at[i, j], device_id=to, device_id_type=MESH)

        made = []
        for i in range(n):
            if kind == "local":
                made.append(pltpu.make_async_copy(ins[i].at[chip], outs[i].at[chip], local_sems.at[i]))
                continue
            for j, (fx, fy) in enumerate(_CHIP_RELATIONS):
                px, py = _flip(x, fx), _flip(y, fy)
                src = 2 * px + py
                if kind == "first":
                    made.append(remote(ins[i].at[src], outs[i].at[chip], i, j, (px, py, c)))
                elif kind == "landed":
                    made.append(remote(ins[i].at[src], outs[i].at[src], i, j, me))
                elif kind == "passed":
                    made.append(remote(outs[i].at[src], outs[i].at[N_CHIPS + src], i, 4 + j, sibling))
            if kind == "first":
                made.append(remote(ins[i].at[chip], outs[i].at[N_CHIPS + chip], i, 3, sibling))
            if kind == "arrivals":
                made += [remote(ins[i].at[0], outs[i].at[0], i, j, me) for j in (3, 4, 5, 6)]
        return made

    def start(ins, outs, sems):
        for cp in copies("local", ins, outs, sems) + copies("first", ins, outs, sems):
            cp.start()

    def middle(ins, outs, sems):
        for got, fwd in zip(copies("landed", ins, outs, sems), copies("passed", ins, outs, sems)):
            got.wait_recv()
            fwd.start()

    def finish(ins, outs, sems):
        for cp in copies("arrivals", ins, outs, sems):
            cp.wait_recv()
        for cp in copies("first", ins, outs, sems) + copies("passed", ins, outs, sems):
            cp.wait_send()
        for cp in copies("local", ins, outs, sems):
            cp.wait()

    return _Comm(list(blocks), [jax.ShapeDtypeStruct((N_DEV,) + b.shape[1:], b.dtype) for b in blocks],
                 [pltpu.SemaphoreType.DMA((n, 7)), pltpu.SemaphoreType.DMA((n, 7)), pltpu.SemaphoreType.DMA((n,))],
                 start, middle, finish)


def _row_spec(tr, w, col=0):
    return pl.BlockSpec((tr, w), lambda i: (i, col))


def _vec_spec(w):
    return pl.BlockSpec((1, w), lambda i: (0, 0))


def _acc_spec(w):
    return pl.BlockSpec((8, w), lambda i: (0, 0))


def _rms(x):
    return lax.rsqrt(jnp.mean(x * x, axis=-1, keepdims=True) + EPS)


def _rms_bwd(dn, n, rstd):
    return rstd * (dn - n * jnp.mean(dn * n, axis=-1, keepdims=True))


def _colsum(x):
    return jnp.sum(x, axis=0, keepdims=True)


def _zero_first(ref):
    @pl.when(pl.program_id(0) == 0)
    def _():
        ref[...] = jnp.zeros_like(ref)


def _x_specs(tr, n_lat_tiles):
    return [pl.BlockSpec((tr, D), lambda i: (jnp.minimum(i, n_lat_tiles - 1), 0)),
            pl.BlockSpec((tr, D), lambda i: (jnp.maximum(i - n_lat_tiles, 0), 0))]


def _x_tile(x_ref, c_ref, n_lat_tiles):
    return jnp.where(pl.program_id(0) >= n_lat_tiles, c_ref[...], x_ref[...])


def _norm_mod(x, ctx, w, mod, n_lat_tiles, tr, comm):
    R = x.shape[0] + ctx.shape[0]
    n = R // tr

    def body(x_ref, c_ref, w_ref, mod_ref, o_ref):
        xv = _x_tile(x_ref, c_ref, n_lat_tiles)
        nw = xv * _rms(xv) * w_ref[...]
        o_ref[...] = (nw * (1.0 + mod_ref[0, 1]) + mod_ref[0, 0]).astype(BF16)

    at = lambda i: pl.program_id(0) == i
    outs = pl.pallas_call(
        _hosted(body, 4, 1, 0, comm, lambda: at(0), lambda: at(n // 2), lambda: at(n - 1)), name="norm1_mod", grid=(n,),
        in_specs=_x_specs(tr, n_lat_tiles) + [_vec_spec(D), pl.BlockSpec(
            (1, 2, 1, D), lambda i: (jnp.where(i >= n_lat_tiles, 1, 0), 0, 0, 0))] + [_ANY] * len(comm.arrays),
        out_specs=[_row_spec(tr, D)] + [_ANY] * len(comm.out_shape),
        out_shape=[jax.ShapeDtypeStruct((R, D), BF16)] + comm.out_shape, scratch_shapes=comm.scratch,
        compiler_params=_cparams(("arbitrary",)),
    )(x, ctx, w, mod, *comm.arrays)
    return outs[0], outs[1:]


def _resid_norm_mod(x, ctx, mix, g1, w2, sh2, sc2, n_lat_tiles, tr):
    R = x.shape[0] + ctx.shape[0]

    def body(x_ref, c_ref, mix_ref, g1_ref, w_ref, sh_ref, sc_ref, h2_ref, u_ref):
        h2 = _x_tile(x_ref, c_ref, n_lat_tiles) + g1_ref[...] * mix_ref[...]
        h2_ref[...] = h2
        n = h2 * _rms(h2) * w_ref[...]
        u_ref[...] = (n * (1.0 + sc_ref[...]) + sh_ref[...]).astype(BF16)

    return pl.pallas_call(
        body, name="resid_norm2_mod", grid=(R // tr,),
        in_specs=_x_specs(tr, n_lat_tiles) + [_row_spec(tr, D)] + [_vec_spec(D)] * 4,
        out_specs=[_row_spec(tr, D), _row_spec(tr, D)],
        out_shape=[jax.ShapeDtypeStruct((R, D), F32), jax.ShapeDtypeStruct((R, D), BF16)],
        compiler_params=_cparams(("arbitrary",)),
    )(x, ctx, mix, g1, w2, sh2, sc2)


def _loss_head(h2, f, target, g2, fw, n_lat_tiles, tr):
    R = h2.shape[0]

    def body(h2_ref, f_ref, t_ref, g2_ref, fw_ref, dh3_ref, df_ref, acc_ref):
        _zero_first(acc_ref)
        lat = pl.program_id(0) < n_lat_tiles
        fv = f_ref[...]
        h3 = h2_ref[...] + g2_ref[...] * fv
        rstd = _rms(h3)
        n = h3 * rstd
        err = n * fw_ref[...] - t_ref[...]
        dy = err * (1.0 / D)
        dh3 = jnp.where(lat, _rms_bwd(dy * fw_ref[...], n, rstd), 0.0)
        dh3_ref[...] = dh3
        df_ref[...] = (g2_ref[...] * dh3).astype(BF16)
        acc_ref[0:1, :] += jnp.where(lat, _colsum(dy * n), 0.0)
        acc_ref[1:2, :] += _colsum(dh3 * fv)
        acc_ref[2:3, :] += jnp.where(lat, _colsum(err * err) * (0.5 / D), 0.0)

    return pl.pallas_call(
        body, name="loss_head", grid=(R // tr,),
        in_specs=[_row_spec(tr, D), _row_spec(tr, D),
                  pl.BlockSpec((tr, D), lambda i: (jnp.minimum(i, n_lat_tiles - 1), 0)), _vec_spec(D), _vec_spec(D)],
        out_specs=[_row_spec(tr, D), _row_spec(tr, D), _acc_spec(D)],
        out_shape=[jax.ShapeDtypeStruct((R, D), F32), jax.ShapeDtypeStruct((R, D), BF16), jax.ShapeDtypeStruct((8, D), F32)],
        compiler_params=_cparams(("arbitrary",)),
    )(h2, f, target, g2, fw)


def _ffn_in_bwd(du_a, du_b, h2, dh3, mix, sc2, g1, w2, tr):
    R = h2.shape[0]

    def body(dua_ref, dub_ref, h2_ref, dh3_ref, mix_ref, sc_ref, g1_ref, w_ref, dh2_ref, dmix_ref, acc_ref):
        _zero_first(acc_ref)
        du = _f32(dua_ref) + _f32(dub_ref)
        h2 = h2_ref[...]
        rstd = _rms(h2)
        n = h2 * rstd
        dnw = du * (1.0 + sc_ref[...])
        dh2 = dh3_ref[...] + _rms_bwd(dnw * w_ref[...], n, rstd)
        dh2_ref[...] = dh2
        dmix_ref[...] = (g1_ref[...] * dh2).astype(BF16)
        acc_ref[0:1, :] += _colsum(du * n * w_ref[...])
        acc_ref[1:2, :] += _colsum(du)
        acc_ref[2:3, :] += _colsum(dnw * n)
        acc_ref[3:4, :] += _colsum(dh2 * mix_ref[...])

    return pl.pallas_call(
        body, name="ffn_in_bwd", grid=(R // tr,),
        in_specs=[_row_spec(tr, D)] * 5 + [_vec_spec(D)] * 3,
        out_specs=[_row_spec(tr, D), _row_spec(tr, D), _acc_spec(D)],
        out_shape=[jax.ShapeDtypeStruct((R, D), F32), jax.ShapeDtypeStruct((R, D), BF16), jax.ShapeDtypeStruct((8, D), F32)],
        compiler_params=_cparams(("arbitrary",)),
    )(du_a, du_b, h2, dh3, mix, sc2, g1, w2)


def _norm1_bwd(dh1, x, ctx, dh2, w1, mod, n_lat_tiles, tr):
    R = x.shape[0] + ctx.shape[0]

    def body(dh1_ref, x_ref, c_ref, dh2_ref, w_ref, mod_ref, dx_ref, acc_ref):
        i = pl.program_id(0)

        @pl.when((i == 0) | (i == n_lat_tiles))
        def _():
            acc_ref[...] = jnp.zeros_like(acc_ref)

        dh1 = dh1_ref[...]
        x = _x_tile(x_ref, c_ref, n_lat_tiles)
        rstd = _rms(x)
        n = x * rstd
        dnw = dh1 * (1.0 + mod_ref[0, 1])

        @pl.when(i < n_lat_tiles)
        def _():
            dx_ref[...] = dh2_ref[...] + _rms_bwd(dnw * w_ref[...], n, rstd)

        acc_ref[0, 0:1, :] += _colsum(dh1 * n * w_ref[...])
        acc_ref[0, 1:2, :] += _colsum(dh1)
        acc_ref[0, 2:3, :] += _colsum(dnw * n)

    sel = lambda i: jnp.where(i >= n_lat_tiles, 1, 0)
    return pl.pallas_call(
        body, name="norm1_bwd", grid=(R // tr,),
        in_specs=[_row_spec(tr, D)] + _x_specs(tr, n_lat_tiles) + [_row_spec(tr, D), _vec_spec(D),
                                                                   pl.BlockSpec((1, 2, 1, D), lambda i: (sel(i), 0, 0, 0))],
        out_specs=[pl.BlockSpec((tr, D), lambda i: (jnp.minimum(i, n_lat_tiles - 1), 0)),
                   pl.BlockSpec((1, 8, D), lambda i: (sel(i), 0, 0))],
        out_shape=[jax.ShapeDtypeStruct((n_lat_tiles * tr, D), F32), jax.ShapeDtypeStruct((2, 8, D), F32)],
        compiler_params=_cparams(("arbitrary",)),
    )(dh1, x, ctx, dh2, w1, mod)


def _swiglu_act_bwd(da, gp, up, tr):
    R = gp.shape[0]

    def body(da_ref, g_ref, u_ref, dg_ref, du_ref):
        da, g = _f32(da_ref), _f32(g_ref)
        dg_ref[...] = (da * _f32(u_ref) * _dsilu(g)).astype(BF16)
        du_ref[...] = (da * _silu(g)).astype(BF16)

    return pl.pallas_call(
        body, name="swiglu_act_bwd", grid=(R // tr,), in_specs=[_row_spec(tr, D_FF)] * 3, out_specs=[_row_spec(tr, D_FF)] * 2,
        out_shape=[jax.ShapeDtypeStruct((R, D_FF), BF16)] * 2, compiler_params=_cparams(("arbitrary",)),
    )(da, gp, up)


def _merge(ya, yb, parts, tr):
    R = ya.shape[0]

    def body(ya_ref, yb_ref, ga_ref, gb_ref, o_ref):
        o_ref[...] = (_sigmoid(_f32(ga_ref)) * _f32(ya_ref) + _sigmoid(_f32(gb_ref)) * _f32(yb_ref)).astype(BF16)

    return pl.pallas_call(
        body, name="merge", grid=(R // tr,),
        in_specs=[_row_spec(tr, D), _row_spec(tr, D), _row_spec(tr, D, COL_GA // D), _row_spec(tr, D, COL_GB // D)],
        out_specs=_row_spec(tr, D), out_shape=jax.ShapeDtypeStruct((R, D), BF16), compiler_params=_cparams(("arbitrary",)),
    )(ya, yb, parts, parts)


def _dparts_out(tr, w, col, nd=1):
    blk = col // w
    return pl.BlockSpec((tr, w), (lambda i: (i, blk)) if nd == 1 else (lambda i, j: (i, blk + j)))


def _merge_bwd(dm, ya, yb, parts, dparts, tr):
    R = ya.shape[0]

    def body(dm_ref, ya_ref, yb_ref, ga_ref, gb_ref, _, dya_ref, dyb_ref, dg_ref):
        dm = _f32(dm_ref)
        sa, sb = _sigmoid(_f32(ga_ref)), _sigmoid(_f32(gb_ref))
        dya_ref[...] = (dm * sa).astype(BF16)
        dyb_ref[...] = (dm * sb).astype(BF16)
        dg_ref[:, 0:D] = (dm * _f32(ya_ref) * sa * (1.0 - sa)).astype(BF16)
        dg_ref[:, D:2 * D] = (dm * _f32(yb_ref) * sb * (1.0 - sb)).astype(BF16)

    return pl.pallas_call(
        body, name="merge_bwd", grid=(R // tr,),
        in_specs=[_row_spec(tr, D)] * 3 + [_row_spec(tr, D, COL_GA // D), _row_spec(tr, D, COL_GB // D), _ANY],
        out_specs=[_row_spec(tr, D), _row_spec(tr, D), _dparts_out(tr, 2 * D, COL_GA)],
        out_shape=[jax.ShapeDtypeStruct((R, D), BF16)] * 2 + [jax.ShapeDtypeStruct(dparts.shape, BF16)],
        input_output_aliases={5: 2}, compiler_params=_cparams(("arbitrary",)),
    )(dm, ya, yb, parts, parts, dparts)


def _gla_out(o2, parts, gw4, tr):
    R = parts.shape[0]

    def body(of_ref, ob_ref, r_ref, w_ref, out_ref):
        oa = _f32(of_ref) + _f32(ob_ref)
        sr = _silu(_f32(r_ref))
        for h in range(GLA_H):
            s = slice(h * GLA_DV, (h + 1) * GLA_DV)
            o = oa[:, s]
            out_ref[:, s] = (o * _rms(o) * w_ref[:, s] * sr[:, s]).astype(BF16)

    return pl.pallas_call(
        body, name="gla_out", grid=(R // tr,),
        in_specs=[_row_spec(tr, D), _row_spec(tr, D), _row_spec(tr, D, COL_R // D), _vec_spec(D)],
        out_specs=_row_spec(tr, D), out_shape=jax.ShapeDtypeStruct((R, D), BF16), compiler_params=_cparams(("arbitrary",)),
    )(o2[0], o2[1], parts, gw4)


def _gla_out_bwd(dout, o2, parts, gw4, dparts, tr):
    R = parts.shape[0]

    def body(d_ref, of_ref, ob_ref, r_ref, w_ref, _, do_ref, dr_ref, acc_ref):
        _zero_first(acc_ref)
        oa = _f32(of_ref) + _f32(ob_ref)
        r = _f32(r_ref)
        sr = _silu(r)
        dout = _f32(d_ref)
        for h in range(GLA_H):
            s = slice(h * GLA_DV, (h + 1) * GLA_DV)
            o = oa[:, s]
            rstd = _rms(o)
            n = o * rstd
            w = w_ref[:, s]
            dr_ref[:, s] = (dout[:, s] * n * w * _dsilu(r[:, s])).astype(BF16)
            dnw = dout[:, s] * sr[:, s]
            do_ref[:, s] = _rms_bwd(dnw * w, n, rstd).astype(ACT)
            acc_ref[0:1, s] += _colsum(dnw * n)

    return pl.pallas_call(
        body, name="gla_out_bwd", grid=(R // tr,),
        in_specs=[_row_spec(tr, D), _row_spec(tr, D), _row_spec(tr, D), _row_spec(tr, D, COL_R // D), _vec_spec(D), _ANY],
        out_specs=[_row_spec(tr, D), _dparts_out(tr, D, COL_R), _acc_spec(D)],
        out_shape=[jax.ShapeDtypeStruct((R, D), ACT), jax.ShapeDtypeStruct(dparts.shape, BF16), jax.ShapeDtypeStruct((8, D), F32)],
        input_output_aliases={5: 1}, compiler_params=_cparams(("arbitrary",)),
    )(dout, o2[0], o2[1], parts, gw4, dparts)


SSM_GW = SSM_INNER // SSM_G


def _ssd_out(y2, xbc, parts, dskip, nw, tr):
    R = parts.shape[0]

    def body(yf_ref, yb_ref, x_ref, z_ref, ds_ref, w_ref, out_ref):
        ob = (_f32(yf_ref) + _f32(yb_ref) + ds_ref[...] * _f32(x_ref)) * _silu(_f32(z_ref))
        for g in range(SSM_G):
            s = slice(g * SSM_GW, (g + 1) * SSM_GW)
            o = ob[:, s]
            out_ref[:, s] = (o * _rms(o) * w_ref[:, s]).astype(BF16)

    return pl.pallas_call(
        body, name="ssd_out", grid=(R // tr,),
        in_specs=[_row_spec(tr, SSM_INNER)] * 3 + [_row_spec(tr, SSM_INNER, COL_Z // SSM_INNER),
                                                   _vec_spec(SSM_INNER), _vec_spec(SSM_INNER)],
        out_specs=_row_spec(tr, SSM_INNER), out_shape=jax.ShapeDtypeStruct((R, SSM_INNER), BF16),
        compiler_params=_cparams(("arbitrary",)),
    )(y2[0], y2[1], xbc, parts, dskip, nw)


def _ssd_out_bwd(dout, y2, xbc, parts, dskip, nw, dparts, tr):
    R = parts.shape[0]

    def body(d_ref, yf_ref, yb_ref, x_ref, z_ref, ds_ref, w_ref, _, dy_ref, dz_ref, acc_ref):
        _zero_first(acc_ref)
        x, z = _f32(x_ref), _f32(z_ref)
        pre = _f32(yf_ref) + _f32(yb_ref) + ds_ref[...] * x
        sz = _silu(z)
        ob = pre * sz
        dout = _f32(d_ref)
        for g in range(SSM_G):
            s = slice(g * SSM_GW, (g + 1) * SSM_GW)
            o = ob[:, s]
            rstd = _rms(o)
            n = o * rstd
            dob = _rms_bwd(dout[:, s] * w_ref[:, s], n, rstd)
            dz_ref[:, s] = (dob * pre[:, s] * _dsilu(z[:, s])).astype(BF16)
            dy = dob * sz[:, s]
            dy_ref[:, s] = dy.astype(ACT)
            acc_ref[0:1, s] += _colsum(dout[:, s] * n)
            acc_ref[1:2, s] += _colsum(dy * x[:, s])

    return pl.pallas_call(
        body, name="ssd_out_bwd", grid=(R // tr,),
        in_specs=[_row_spec(tr, SSM_INNER)] * 4 + [_row_spec(tr, SSM_INNER, COL_Z // SSM_INNER),
                                                   _vec_spec(SSM_INNER), _vec_spec(SSM_INNER), _ANY],
        out_specs=[_row_spec(tr, SSM_INNER), _dparts_out(tr, SSM_INNER, COL_Z), _acc_spec(SSM_INNER)],
        out_shape=[jax.ShapeDtypeStruct((R, SSM_INNER), ACT), jax.ShapeDtypeStruct(dparts.shape, BF16),
                   jax.ShapeDtypeStruct((8, SSM_INNER), F32)],
        input_output_aliases={7: 1}, compiler_params=_cparams(("arbitrary",)),
    )(dout, y2[0], y2[1], xbc, parts, dskip, nw, dparts)


CONV_W = SSM_INNER + 2 * SSM_G * SSM_N
CONV_BLK = 1024


CONV_SHIFTS = (-2, -1, 1, 2)


def _conv_mask_table(tr):
    t = np.arange(tr)
    table = np.zeros((2, len(CONV_SHIFTS), tr, 128), np.float32)
    for kind, (pos, seg) in enumerate(((t % GRID_W, GRID_W), (t, tr))):
        for k, s in enumerate(CONV_SHIFTS):
            table[kind, k] = ((pos + s >= 0) & (pos + s < seg)).astype(np.float32)[:, None]
    return jnp.asarray(table)


def _shifted(u, s, mask_ref, tr):
    return u if s == 0 else pltpu.roll(u, (-s) % tr, 0) * mask_ref[0, CONV_SHIFTS.index(s)]


def _conv_mask_spec(tr, n_lat_tiles, row_axis):
    return pl.BlockSpec((1, len(CONV_SHIFTS), tr, 128),
                        lambda *ids: (jnp.where(ids[row_axis] >= n_lat_tiles, 1, 0), 0, 0, 0))


def _conv_fwd(parts, cw, cb, n_lat_tiles, tr):
    R = parts.shape[0]

    def body(u_ref, w_ref, b_ref, mask_ref, o_ref):
        def lanes(l, carry):
            sl = pl.ds(pl.multiple_of(l * 128, 128), 128)
            u, w = u_ref[:, sl].astype(F32), w_ref[:, sl]
            acc = jnp.zeros_like(u) + b_ref[:, sl]
            for j in range(SSM_CONV):
                acc = acc + _shifted(u, j - CONV_LEFT, mask_ref, tr) * w[j:j + 1, :]
            o_ref[:, sl] = _silu(acc).astype(ACT)
            return carry

        lax.fori_loop(0, CONV_BLK // 128, lanes, 0)

    return pl.pallas_call(
        body, name="conv_fwd", grid=(R // tr, CONV_W // CONV_BLK),
        in_specs=[pl.BlockSpec((tr, CONV_BLK), lambda i, j: (i, COL_XBC // CONV_BLK + j)),
                  pl.BlockSpec((SSM_CONV, CONV_BLK), lambda i, j: (0, j)), pl.BlockSpec((1, CONV_BLK), lambda i, j: (0, j)),
                  _conv_mask_spec(tr, n_lat_tiles, 0)],
        out_specs=pl.BlockSpec((tr, CONV_BLK), lambda i, j: (i, j)), out_shape=jax.ShapeDtypeStruct((R, CONV_W), ACT),
        compiler_params=_cparams(("arbitrary", "arbitrary")),
    )(parts, cw, cb, _conv_mask_table(tr))


def _conv_bwd(dx, db, dc, dy, dskip, parts, cw, cb, dparts, n_lat_tiles, tr):
    R = parts.shape[0]
    half = CONV_BLK // 2
    n_x = SSM_INNER // CONV_BLK

    def body(dxf_ref, dxb_ref, dy_ref, ds_ref, dbf_ref, dbb_ref, dcf_ref, dcb_ref, u_ref, w_ref, b_ref, _, mask_ref,
             du_ref, acc_ref, d_scr):
        @pl.when(pl.program_id(1) == 0)
        def _():
            acc_ref[...] = jnp.zeros_like(acc_ref)

        @pl.when(pl.program_id(0) < n_x)
        def _():
            d_scr[...] = dxf_ref[...] + dxb_ref[...] + _f32(dy_ref) * ds_ref[...]

        @pl.when(pl.program_id(0) >= n_x)
        def _():
            d_scr[:, 0:half] = dbf_ref[...] + dbb_ref[...]
            d_scr[:, half:] = dcf_ref[...] + dcb_ref[...]

        def lanes(l, carry):
            sl = pl.ds(pl.multiple_of(l * 128, 128), 128)
            u, w = u_ref[:, sl].astype(F32), w_ref[:, sl]
            pre = jnp.zeros_like(u) + b_ref[:, sl]
            taps = []
            for j in range(SSM_CONV):
                tap = _shifted(u, j - CONV_LEFT, mask_ref, tr)
                taps.append(tap)
                pre = pre + tap * w[j:j + 1, :]
            dpre = d_scr[:, sl] * _dsilu(pre)
            du = jnp.zeros_like(u)
            sums = []
            for j in range(SSM_CONV):
                sums.append(_colsum(dpre * taps[j]))
                du = du + _shifted(dpre, CONV_LEFT - j, mask_ref, tr) * w[j:j + 1, :]
            sums += [_colsum(dpre), jnp.zeros((8 - SSM_CONV - 1, 128), F32)]
            acc_ref[:, sl] += jnp.concatenate(sums, axis=0)
            du_ref[:, sl] = du.astype(BF16)
            return carry

        lax.fori_loop(0, CONV_BLK // 128, lanes, 0)

    return pl.pallas_call(
        body, name="conv_bwd", grid=(CONV_W // CONV_BLK, R // tr),
        in_specs=[pl.BlockSpec((tr, CONV_BLK), lambda j, i: (jnp.where(j < n_x, i, 0), jnp.minimum(j, n_x - 1)))] * 3
        + [pl.BlockSpec((1, CONV_BLK), lambda j, i: (0, jnp.minimum(j, n_x - 1)))]
        + [pl.BlockSpec((tr, half), lambda j, i: (jnp.where(j < n_x, 0, i), 0))] * 4
        + [pl.BlockSpec((tr, CONV_BLK), lambda j, i: (i, COL_XBC // CONV_BLK + j)),
           pl.BlockSpec((SSM_CONV, CONV_BLK), lambda j, i: (0, j)), pl.BlockSpec((1, CONV_BLK), lambda j, i: (0, j)), _ANY,
           _conv_mask_spec(tr, n_lat_tiles, 1)],
        out_specs=[pl.BlockSpec((tr, CONV_BLK), lambda j, i: (i, COL_XBC // CONV_BLK + j)),
                   pl.BlockSpec((8, CONV_BLK), lambda j, i: (0, j))],
        out_shape=[jax.ShapeDtypeStruct(dparts.shape, BF16), jax.ShapeDtypeStruct((8, CONV_W), F32)],
        scratch_shapes=[pltpu.VMEM((tr, CONV_BLK), F32)],
        input_output_aliases={11: 0}, compiler_params=_cparams(("arbitrary", "arbitrary")),
    )(*dx, dy, dskip, *db, *dc, parts, cw, cb, dparts, _conv_mask_table(tr))


def _chunk_row_block(d, i, n_lat, n_ctx):
    fwd = jnp.where(i < n_ctx, n_lat + i, i - n_ctx)
    rev = n_lat + n_ctx - 1 - i
    if isinstance(d, int):
        return rev if d else fwd
    return jnp.where(d == 0, fwd, rev)


def _tri(n, d, transpose=False):
    row = lax.broadcasted_iota(jnp.int32, (n, n), 0)
    col = lax.broadcasted_iota(jnp.int32, (n, n), 1)
    diff = (col - row) if transpose else (row - col)
    return diff * (1 - 2 * d) >= 0


def _gla_gates(sm, uhi, ulo, bias, d):
    pre = _nn3(sm, uhi, ulo) + bias
    g = _log_sigmoid(pre) * (1.0 / GLA_TAU)
    mask = _tri(GLA_C, d)
    b = _x_nn(mask.astype(BF16), g)
    b_tot = _colsum(g)
    b_ref = b[GLA_C // 2:GLA_C // 2 + 1, :]
    e_q = jnp.exp(jnp.minimum(b - b_ref, EXP_CLAMP))
    e_k = jnp.exp(jnp.minimum(b_ref - b, EXP_CLAMP))
    return pre, mask, b_tot, e_q, e_k, jnp.exp(b), jnp.exp(b_tot - b)


GLA_QK = GLA_H * GLA_DK
GLA_V = GLA_H * GLA_DV


def _gla_specs(n_lat, n_ctx, step_of):
    rbs = [lambda i, d=d: _chunk_row_block(d, step_of(i), n_lat, n_ctx) for d in range(2)]
    specs = []
    for rb in rbs:
        specs += [pl.BlockSpec((GLA_C, GLA_QK), lambda i, rb=rb: (rb(i), COL_Q // GLA_QK)),
                  pl.BlockSpec((GLA_C, GLA_QK), lambda i, rb=rb: (rb(i), COL_K // GLA_QK)),
                  pl.BlockSpec((GLA_C, GLA_V), lambda i, rb=rb: (rb(i), COL_V // GLA_V)),
                  pl.BlockSpec((GLA_C, 128), lambda i, rb=rb: (rb(i), 0))]
    specs += [pl.BlockSpec((2, 128, GLA_QK), lambda i: (0, 0, 0)), pl.BlockSpec((2, 128, GLA_QK), lambda i: (0, 0, 0)),
              pl.BlockSpec((2, 1, GLA_QK), lambda i: (0, 0, 0))]
    return specs, rbs


def _gla_fwd(parts, sm, uhi, ulo, bias, n_lat, n_ctx):
    R = parts.shape[0]
    n_steps = n_lat + n_ctx
    scale = GLA_DK ** -0.5

    def body(*refs):
        ins, (uhi_ref, ulo_ref, bias_ref), o_refs, hist_ref, st = refs[:8], refs[8:11], refs[11:13], refs[13], refs[14]

        @pl.when(pl.program_id(0) == 0)
        def _():
            st[...] = jnp.zeros_like(st)

        for d in range(2):
            q_ref, k_ref, v_ref, sm_ref = ins[4 * d:4 * d + 4]
            _, mask, b_tot, e_q, e_k, e_in, e_out = _gla_gates(sm_ref[...], uhi_ref[d], ulo_ref[d], bias_ref[d], d)
            q, k, v = _f32(q_ref) * scale, _f32(k_ref), _bf(v_ref[...])
            qb, kb, q_in, k_out, decay = _bf(q * e_q), _bf(k * e_k), _bf(q * e_in), _bf(k * e_out), jnp.exp(b_tot)
            for h in range(GLA_H):
                sk, sv = slice(h * GLA_DK, (h + 1) * GLA_DK), slice(h * GLA_DV, (h + 1) * GLA_DV)
                att = jnp.where(mask, _nt(qb[:, sk], kb[:, sk]), 0.0)
                s_in = st[d, h]
                hist_ref[d, 0, h] = s_in
                o_refs[d][:, sv] = (_nn(_bf(att), v[:, sv]) + _nt(q_in[:, sk], _bf(s_in))).astype(ACT)
                st[d, h] = decay[:, sk] * s_in + _tn(v[:, sv], k_out[:, sk])

    in_specs, rbs = _gla_specs(n_lat, n_ctx, lambda i: i)
    return pl.pallas_call(
        body, name="gla_fwd", grid=(n_steps,), in_specs=in_specs,
        out_specs=[pl.BlockSpec((GLA_C, GLA_V), lambda i, rb=rb: (rb(i), 0)) for rb in rbs]
        + [pl.BlockSpec((2, 1, GLA_H, GLA_DV, GLA_DK), lambda i: (0, i, 0, 0, 0))],
        out_shape=[jax.ShapeDtypeStruct((R, GLA_V), ACT)] * 2 + [jax.ShapeDtypeStruct((2, n_steps, GLA_H, GLA_DV, GLA_DK), F32)],
        scratch_shapes=[pltpu.VMEM((2, GLA_H, GLA_DV, GLA_DK), F32)],
        compiler_params=_cparams(("arbitrary",)),
    )(*([parts, parts, parts, sm] * 2), uhi, ulo, bias)


def _gla_bwd(do, parts, sm, uhi, ulo, bias, hist, n_lat, n_ctx):
    R = parts.shape[0]
    n_steps = n_lat + n_ctx
    scale = GLA_DK ** -0.5
    step_of = lambda j: n_steps - 1 - j

    def body(*refs):
        ins, (uhi_ref, ulo_ref, bias_ref), do_refs, hist_ref = refs[:8], refs[8:11], refs[11:13], refs[13]
        outs, dst = refs[14:22], refs[22]

        @pl.when(pl.program_id(0) == 0)
        def _():
            dst[...] = jnp.zeros_like(dst)

        for d in range(2):
            q_ref, k_ref, v_ref, sm_ref = ins[4 * d:4 * d + 4]
            dq_ref, dk_ref, dv_ref, dp_ref = outs[4 * d:4 * d + 4]
            pre, mask, b_tot, e_q, e_k, e_in, e_out = _gla_gates(sm_ref[...], uhi_ref[d], ulo_ref[d], bias_ref[d], d)
            q, k, v = _f32(q_ref) * scale, _f32(k_ref), _bf(v_ref[...])
            dout = _bf(do_refs[d][...])
            k_out_f = k * e_out
            qb, kb, q_in, k_out, decay = _bf(q * e_q), _bf(k * e_k), _bf(q * e_in), _bf(k_out_f), jnp.exp(b_tot)
            dqs, dks, dk_outs, dss = [], [], [], []
            for h in range(GLA_H):
                sk, sv = slice(h * GLA_DK, (h + 1) * GLA_DK), slice(h * GLA_DV, (h + 1) * GLA_DV)
                s_in, ds = hist_ref[d, 0, h], dst[d, h]
                att = jnp.where(mask, _nt(qb[:, sk], kb[:, sk]), 0.0)
                datt = _bf(jnp.where(mask, _nt(dout[:, sv], v[:, sv]), 0.0))
                dv_ref[:, sv] = (_tn(_bf(att), dout[:, sv]) + _nt(k_out[:, sk], _bf(ds))).astype(ACT)
                dqs.append(_nn(datt, kb[:, sk]) * e_q[:, sk] + _nn(dout[:, sv], _bf(s_in)) * e_in[:, sk])
                dk_o = _nn(v[:, sv], _bf(ds))
                dk_outs.append(dk_o)
                dks.append(_tn(datt, qb[:, sk]) * e_k[:, sk])
                dss.append(_colsum(ds * s_in))
                dst[d, h] = decay[:, sk] * ds + _tn(dout[:, sv], q_in[:, sk])
            dq, dk_out = jnp.concatenate(dqs, axis=1), jnp.concatenate(dk_outs, axis=1)
            dk = jnp.concatenate(dks, axis=1) + dk_out * e_out
            dq_ref[...] = (dq * scale).astype(ACT)
            dk_ref[...] = dk.astype(ACT)
            db_tot = _colsum(dk_out * k_out_f) + decay * jnp.concatenate(dss, axis=1)
            dg = _x_nn(_tri(GLA_C, d, transpose=True).astype(BF16), dq * q - dk * k) + db_tot
            dp_ref[...] = dg * (1.0 / GLA_TAU) * _sigmoid(-pre)

    in_specs, rbs = _gla_specs(n_lat, n_ctx, step_of)
    in_specs += [pl.BlockSpec((GLA_C, GLA_V), lambda j, rb=rb: (rb(j), 0)) for rb in rbs]
    in_specs += [pl.BlockSpec((2, 1, GLA_H, GLA_DV, GLA_DK), lambda j: (0, step_of(j), 0, 0, 0))]
    out_specs, out_shape = [], []
    for rb in rbs:
        for w, dt in ((GLA_QK, ACT), (GLA_QK, ACT), (GLA_V, ACT), (GLA_QK, F32)):
            out_specs.append(pl.BlockSpec((GLA_C, w), lambda j, rb=rb: (rb(j), 0)))
            out_shape.append(jax.ShapeDtypeStruct((R, w), dt))
    outs = pl.pallas_call(
        body, name="gla_bwd", grid=(n_steps,), in_specs=in_specs, out_specs=out_specs, out_shape=out_shape,
        scratch_shapes=[pltpu.VMEM((2, GLA_H, GLA_DV, GLA_DK), F32)],
        compiler_params=_cparams(("arbitrary",)),
    )(*([parts, parts, parts, sm] * 2), uhi, ulo, bias, do, do, hist)
    return [(outs[k], outs[4 + k]) for k in range(4)]


def _ssd_consts(dt_bias, a_log):
    sel = np.zeros((2, SSM_G, 128, 128), np.float32)
    for d, base in enumerate((SM_DTF, SM_DTB)):
        for g in range(SSM_G):
            for e in range(SSM_HPG):
                sel[d, g, base + SSM_HPG * g + e, e] = 1.0
    e512 = np.zeros((128, SSM_GW), np.float32)
    for e in range(SSM_HPG):
        e512[e, SSM_P * e:SSM_P * (e + 1)] = 1.0
    a_neg = -jnp.exp(a_log)
    pad = lambda v: jnp.pad(v.reshape(2, SSM_G, 1, SSM_HPG), ((0, 0), (0, 0), (0, 0), (0, 128 - SSM_HPG)))
    return dict(
        sel=jnp.asarray(sel, BF16), sel_t=jnp.asarray(sel.transpose(0, 1, 3, 2), BF16), e512_t=jnp.asarray(e512.T, BF16),
        dtb=pad(dt_bias), a=pad(a_neg), a512=jnp.repeat(a_neg, SSM_P, axis=1).reshape(2, SSM_G, 1, SSM_GW))


def _head_columns(x8):
    return [jnp.broadcast_to(x8[:, e:e + 1], (x8.shape[0], 128)) for e in range(SSM_HPG)]


def _head_layout(cols):
    low = lax.broadcasted_iota(jnp.int32, (1, 128), 1) < SSM_P
    return jnp.concatenate([jnp.where(low, cols[2 * j], cols[2 * j + 1]) for j in range(SSM_HPG // 2)], axis=1)


def _ssd_common(sm, sel, dtb, a_neg, a512, d):
    dtr8 = _nn_x(sm, sel) + dtb
    dt8 = _softplus(dtr8)
    a8 = a_neg * dt8
    mask = _tri(SSM_C, d)
    mask_t = _tri(SSM_C, d, transpose=True).astype(BF16)
    cum8 = _x_nn(mask.astype(BF16), a8)
    a_hi, a_lo = _split(a8)
    cum_t = _tn(a_hi, mask_t) + _tn(a_lo, mask_t)
    cum_cols = _head_columns(cum8)
    dt_exp = _head_layout(_head_columns(dt8))
    a_exp = a512 * dt_exp
    return dict(dtr8=dtr8, a8=a8, mask=mask, mask_t=mask_t, cum_t=cum_t, dt_exp=dt_exp, a_exp=a_exp,
                cum_exp=_head_layout(cum_cols), cum_cols=cum_cols, tot_exp=_colsum(a_exp))


def _head_lanes(x, e):
    pair = x[:, 128 * (e // 2):128 * (e // 2 + 1)]
    low = lax.broadcasted_iota(jnp.int32, (1, 128), 1) < SSM_P
    return _bf(jnp.where(low if e % 2 == 0 else jnp.logical_not(low), pair, 0.0))


def _per_head_pairs(fn, x):
    return jnp.concatenate([fn(2 * j, _head_lanes(x, 2 * j)) + fn(2 * j + 1, _head_lanes(x, 2 * j + 1))
                            for j in range(SSM_HPG // 2)], axis=1)


def _ssd_decay(cm, e):
    diff = cm["cum_cols"][e] - cm["cum_t"][e:e + 1, :]
    return jnp.where(cm["mask"], jnp.exp(jnp.minimum(diff, 0.0)), 0.0)


SSM_GPS = 4


def _ssd_specs(n_lat, n_ctx, step_of):
    rbs = [lambda i, d=d: _chunk_row_block(d, step_of(i), n_lat, n_ctx) for d in range(2)]
    xw, nw = SSM_GPS * SSM_GW, SSM_GPS * SSM_N
    specs = []
    for rb in rbs:
        specs += [pl.BlockSpec((SSM_C, xw), lambda g, i, rb=rb: (rb(i), g)),
                  pl.BlockSpec((SSM_C, nw), lambda g, i, rb=rb: (rb(i), SSM_INNER // nw + g)),
                  pl.BlockSpec((SSM_C, nw), lambda g, i, rb=rb: (rb(i), (SSM_INNER + SSM_G * SSM_N) // nw + g)),
                  pl.BlockSpec((SSM_C, 128), lambda g, i, rb=rb: (rb(i), 0))]
    specs += [pl.BlockSpec((2, SSM_GPS, 128, 128), lambda g, i: (0, g, 0, 0)),
              pl.BlockSpec((2, SSM_GPS, 1, 128), lambda g, i: (0, g, 0, 0)),
              pl.BlockSpec((2, SSM_GPS, 1, 128), lambda g, i: (0, g, 0, 0)),
              pl.BlockSpec((2, SSM_GPS, 1, SSM_GW), lambda g, i: (0, g, 0, 0))]
    return specs, rbs


def _ssd_fwd(xbc, sm, k, n_lat, n_ctx, comm):
    R = xbc.shape[0]
    n_steps = n_lat + n_ctx

    def body(*refs):
        ins, (sel_ref, dtb_ref, a_ref, a512_ref), y_refs, hist_ref, st = refs[:8], refs[8:12], refs[12:14], refs[14], refs[15]

        @pl.when(pl.program_id(1) == 0)
        def _():
            st[...] = jnp.zeros_like(st)

        for d in range(2):
            x_ref, b_ref, c_ref, sm_ref = ins[4 * d:4 * d + 4]
            sm = sm_ref[...]
            for gg in range(SSM_GPS):
                sx, sn = slice(gg * SSM_GW, (gg + 1) * SSM_GW), slice(gg * SSM_N, (gg + 1) * SSM_N)
                cm = _ssd_common(sm, sel_ref[d, gg], dtb_ref[d, gg], a_ref[d, gg], a512_ref[d, gg], d)
                bm, cmat = _bf(b_ref[:, sn]), _bf(c_ref[:, sn])
                xdt = x_ref[:, sx].astype(F32) * cm["dt_exp"]
                cb = _nt(cmat, bm)
                ys = _per_head_pairs(lambda e, x_e: _nn(_bf(cb * _ssd_decay(cm, e)), x_e), xdt)
                s_in = st[d, gg]
                hist_ref[d, 0, gg] = s_in
                y = ys + jnp.exp(cm["cum_exp"]) * _nn(cmat, _bf(s_in))
                y_refs[d][:, sx] = y.astype(ACT)
                st[d, gg] = jnp.exp(cm["tot_exp"]) * s_in + _tn(bm, _bf(xdt * jnp.exp(cm["tot_exp"] - cm["cum_exp"])))

    in_specs, rbs = _ssd_specs(n_lat, n_ctx, lambda i: i)
    out_specs = [pl.BlockSpec((SSM_C, SSM_GPS * SSM_GW), lambda g, i, rb=rb: (rb(i), g)) for rb in rbs]
    out_specs += [pl.BlockSpec((2, 1, SSM_GPS, SSM_N, SSM_GW), lambda g, i: (0, i, g, 0, 0))]
    out_shape = [jax.ShapeDtypeStruct((R, SSM_INNER), ACT)] * 2 + [jax.ShapeDtypeStruct((2, n_steps, SSM_G, SSM_N, SSM_GW), F32)]
    args = [xbc, xbc, xbc, sm] * 2 + [k["sel"], k["dtb"], k["a"], k["a512"]]
    n_host_out = len(out_shape)
    outs = pl.pallas_call(
        _hosted(body, len(args), n_host_out, 1, comm, *_ssd_comm_steps(n_steps)), name="ssd_fwd",
        grid=(SSM_G // SSM_GPS, n_steps), in_specs=in_specs + [_ANY] * len(comm.arrays),
        out_specs=out_specs + [_ANY] * len(comm.out_shape), out_shape=out_shape + comm.out_shape,
        scratch_shapes=[pltpu.VMEM((2, SSM_GPS, SSM_N, SSM_GW), F32)] + comm.scratch,
        compiler_params=_cparams(("arbitrary", "arbitrary")),
    )(*args, *comm.arrays)
    return outs[:n_host_out], outs[n_host_out:]


def _ssd_comm_steps(n_steps):
    n_g = SSM_G // SSM_GPS
    at = lambda g, i: (pl.program_id(0) == g) & (pl.program_id(1) == i)
    half = (n_g // 2, n_steps // 2 if n_g % 2 else 0)
    return (lambda: at(0, 0)), (lambda: at(*half)), (lambda: at(n_g - 1, n_steps - 1))


def _ssd_bwd(dy, xbc, sm, k, hist, n_lat, n_ctx, comm):
    R = xbc.shape[0]
    n_steps = n_lat + n_ctx
    step_of = lambda j: n_steps - 1 - j

    def one(d, gg, x_ref, b_ref, c_ref, sm_ref, sel_ref, dtb_ref, a_ref, a512_ref, selt_ref, e512t_ref, dy_ref,
            hist_ref, dx_ref, db_ref, dc_ref, dsm_ref, acc_ref, dst):
        sx, sn = slice(gg * SSM_GW, (gg + 1) * SSM_GW), slice(gg * SSM_N, (gg + 1) * SSM_N)
        a_neg, e512_t = a_ref[d, gg], e512t_ref[...]
        cm = _ssd_common(sm_ref[...], sel_ref[d, gg], dtb_ref[d, gg], a_neg, a512_ref[d, gg], d)
        x, dyv = x_ref[:, sx].astype(F32), dy_ref[:, sx].astype(F32)
        bm, cmat = _bf(b_ref[:, sn]), _bf(c_ref[:, sn])
        xdt = x * cm["dt_exp"]
        cb = _nt(cmat, bm)
        s_in, ds = hist_ref[d, 0, gg], dst[d, gg]
        w = jnp.exp(cm["tot_exp"] - cm["cum_exp"])
        z = _nn(bm, _bf(ds))
        decay_in = jnp.exp(cm["cum_exp"])
        gy = _bf(dyv * decay_in)
        dcb = jnp.zeros((SSM_C, SSM_C), F32)
        crossing = []
        row = lax.broadcasted_iota(jnp.int32, (SSM_C, SSM_C), 0)
        col = lax.broadcasted_iota(jnp.int32, (SSM_C, SSM_C), 1)
        eye = (row == col).astype(BF16)
        before = (cm["mask_t"] - eye)
        xdt_bf = _bf(xdt)

        def head(e, dy_e):
            nonlocal dcb
            lm = _ssd_decay(cm, e)
            m_e = cb * lm
            dm_e = _nt(dy_e, xdt_bf[:, 128 * (e // 2):128 * (e // 2 + 1)])
            dcb = dcb + dm_e * lm
            crossing.append(_bf(dm_e * m_e))
            return _tn(_bf(m_e), dy_e)

        dx_heads = _per_head_pairs(head, dyv)
        through = _nn(jnp.concatenate(crossing, axis=0), before)
        crossing = [_colsum(jnp.where(cm["mask"], through[e * SSM_C:(e + 1) * SSM_C], 0.0)) for e in range(SSM_HPG)]
        da_rows = jnp.concatenate(crossing + [jnp.zeros((128 - SSM_HPG, SSM_C), F32)], axis=0)
        r_hi, r_lo = _split(da_rows)
        da8_intra = _tn(r_hi, eye) + _tn(r_lo, eye)
        dx_state = w * z
        dxdt = dx_heads + dx_state
        dcb = _bf(dcb)
        c_s = _nn(cmat, _bf(s_in))
        dc_ref[:, sn] = _nn(dcb, bm) + _nt(gy, _bf(s_in))
        db_ref[:, sn] = _tn(dcb, cmat) + _nt(_bf(w * xdt), _bf(ds))
        dst[d, gg] = jnp.exp(cm["tot_exp"]) * ds + _tn(cmat, gy)
        state_path = xdt * dx_state
        per_token = _nn_x(jnp.concatenate([dyv * decay_in * c_s - state_path, dxdt * x], axis=0), e512_t)
        totals = jnp.concatenate([_colsum(state_path), _colsum(ds * s_in), jnp.zeros((6, SSM_GW), F32)], axis=0)
        totals = _nn_x(totals, e512_t)
        tot8 = _colsum(cm["a8"])
        dtot8 = totals[0:1] + jnp.exp(tot8) * totals[1:2]
        da8 = da8_intra + _x_nn(cm["mask_t"], per_token[:SSM_C]) + dtot8
        ddt8 = da8 * a_neg + per_token[SSM_C:]
        dsm_ref[gg] = _nn_x(ddt8 * _sigmoid(cm["dtr8"]), selt_ref[d, gg])
        dx_ref[:, sx] = dxdt * cm["dt_exp"]
        acc_ref[d, gg, 0:1, :] += _colsum(da8 * cm["a8"])

    def body(*refs):
        ins, consts, (selt_ref, e512t_ref), dy_refs, hist_ref = refs[:8], refs[8:12], refs[12:14], refs[14:16], refs[16]
        outs, acc_ref, dst = refs[17:25], refs[25], refs[26]

        @pl.when(pl.program_id(1) == 0)
        def _():
            dst[...] = jnp.zeros_like(dst)
            acc_ref[...] = jnp.zeros_like(acc_ref)

        for d in range(2):
            for gg in range(SSM_GPS):
                one(d, gg, *ins[4 * d:4 * d + 4], *consts, selt_ref, e512t_ref, dy_refs[d], hist_ref,
                    *outs[4 * d:4 * d + 4], acc_ref, dst)

    xw, nw = SSM_GPS * SSM_GW, SSM_GPS * SSM_N
    in_specs, rbs = _ssd_specs(n_lat, n_ctx, step_of)
    in_specs += [pl.BlockSpec((2, SSM_GPS, 128, 128), lambda g, j: (0, g, 0, 0)), pl.BlockSpec((SSM_GW, 128), lambda g, j: (0, 0))]
    in_specs += [pl.BlockSpec((SSM_C, xw), lambda g, j, rb=rb: (rb(j), g)) for rb in rbs]
    in_specs += [pl.BlockSpec((2, 1, SSM_GPS, SSM_N, SSM_GW), lambda g, j: (0, step_of(j), g, 0, 0))]
    out_specs, out_shape = [], []
    for rb in rbs:
        out_specs += [pl.BlockSpec((SSM_C, xw), lambda g, j, rb=rb: (rb(j), g)),
                      pl.BlockSpec((SSM_C, nw), lambda g, j, rb=rb: (rb(j), g)),
                      pl.BlockSpec((SSM_C, nw), lambda g, j, rb=rb: (rb(j), g)),
                      pl.BlockSpec((SSM_GPS, SSM_C, 128), lambda g, j, rb=rb: (g, rb(j), 0))]
        out_shape += [jax.ShapeDtypeStruct((R, SSM_INNER), F32), jax.ShapeDtypeStruct((R, SSM_G * SSM_N), F32),
                      jax.ShapeDtypeStruct((R, SSM_G * SSM_N), F32), jax.ShapeDtypeStruct((SSM_G, R, 128), F32)]
    out_specs.append(pl.BlockSpec((2, SSM_GPS, 8, 128), lambda g, j: (0, g, 0, 0)))
    out_shape.append(jax.ShapeDtypeStruct((2, SSM_G, 8, 128), F32))
    args = [xbc, xbc, xbc, sm] * 2 + [k["sel"], k["dtb"], k["a"], k["a512"], k["sel_t"], k["e512_t"], dy, dy, hist]
    n_host_out = len(out_shape)
    outs = pl.pallas_call(
        _hosted(body, len(args), n_host_out, 1, comm, *_ssd_comm_steps(n_steps)), name="ssd_bwd",
        grid=(SSM_G // SSM_GPS, n_steps), in_specs=in_specs + [_ANY] * len(comm.arrays),
        out_specs=out_specs + [_ANY] * len(comm.out_shape), out_shape=out_shape + comm.out_shape,
        scratch_shapes=[pltpu.VMEM((2, SSM_GPS, SSM_N, SSM_GW), F32)] + comm.scratch,
        compiler_params=_cparams(("arbitrary", "arbitrary")),
    )(*args, *comm.arrays)
    return [(outs[n], outs[4 + n]) for n in range(4)] + [outs[8]], outs[n_host_out:]


def _gla_assemble(dq, dk, dv, dparts, tr):
    R = dq[0].shape[0]
    qk = GLA_H * GLA_DK

    def body(dqf_ref, dqb_ref, dkf_ref, dkb_ref, dvf_ref, dvb_ref, _, o_ref):
        o_ref[:, 0:qk] = (_f32(dqf_ref) + _f32(dqb_ref)).astype(BF16)
        o_ref[:, qk:2 * qk] = (_f32(dkf_ref) + _f32(dkb_ref)).astype(BF16)
        o_ref[:, 2 * qk:] = (_f32(dvf_ref) + _f32(dvb_ref)).astype(BF16)

    return pl.pallas_call(
        body, name="gla_assemble", grid=(R // tr,), in_specs=[_row_spec(tr, qk)] * 4 + [_row_spec(tr, D)] * 2 + [_ANY],
        out_specs=_dparts_out(tr, 2 * D, COL_Q), out_shape=jax.ShapeDtypeStruct(dparts.shape, BF16),
        input_output_aliases={6: 0}, compiler_params=_cparams(("arbitrary",)),
    )(*dq, *dk, *dv, dparts)


def _small_assemble(dp, dsm, sm, ut_hi, ut_lo, dparts, tr):
    R = sm.shape[0]
    qk = GLA_H * GLA_DK

    def body(dpf_ref, dpb_ref, dsmf_ref, dsmb_ref, sm_ref, uth_ref, utl_ref, _, o_ref, dup_ref, acc_ref, acc2_ref):
        @pl.when(pl.program_id(0) == 0)
        def _():
            dup_ref[...] = jnp.zeros_like(dup_ref)
            acc_ref[...] = jnp.zeros_like(acc_ref)
            acc2_ref[...] = jnp.zeros_like(acc2_ref)

        ssd = dsmf_ref[0] + dsmb_ref[0]
        for g in range(1, SSM_G):
            ssd = ssd + (dsmf_ref[g] + dsmb_ref[g])
        acc2_ref[0:1, :] += _colsum(ssd)
        sm_hi, sm_lo = _split(sm_ref[...])
        out = ssd
        for d, dp_ref in enumerate((dpf_ref, dpb_ref)):
            dpd = dp_ref[...]
            out = out + _nn3(dpd, uth_ref[d], utl_ref[d])
            p_hi, p_lo = _split(dpd)
            dup_ref[d] += _tn(sm_hi, p_hi) + _tn(sm_lo, p_hi) + _tn(sm_hi, p_lo)
            acc_ref[d:d + 1, :] += _colsum(dpd)
        o_ref[...] = out.astype(BF16)

    return pl.pallas_call(
        body, name="small_assemble", grid=(R // tr,),
        in_specs=[_row_spec(tr, qk)] * 2 + [pl.BlockSpec((SSM_G, tr, 128), lambda i: (0, i, 0))] * 2
        + [_row_spec(tr, 128), pl.BlockSpec((2, qk, 128), lambda i: (0, 0, 0)),
           pl.BlockSpec((2, qk, 128), lambda i: (0, 0, 0)), _ANY],
        out_specs=[_dparts_out(tr, 128, COL_SM), pl.BlockSpec((2, 128, qk), lambda i: (0, 0, 0)), _acc_spec(qk), _acc_spec(128)],
        out_shape=[jax.ShapeDtypeStruct(dparts.shape, BF16), jax.ShapeDtypeStruct((2, 128, qk), F32),
                   jax.ShapeDtypeStruct((8, qk), F32), jax.ShapeDtypeStruct((8, 128), F32)],
        input_output_aliases={7: 0}, compiler_params=_cparams(("arbitrary",)),
    )(*dp, *dsm, sm, ut_hi, ut_lo, dparts)


ADA_ROWS = 16
ADA_TILE = 512


def _dot3_f32(a, b, ca, cb):
    a_hi, a_lo = _split(a)
    b_hi, b_lo = _split(b)
    return _dg(a_hi, b_hi, ca, cb) + _dg(a_lo, b_hi, ca, cb) + _dg(a_hi, b_lo, ca, cb)


def _ada_fwd(cvec, w, b):
    cols = w.shape[1]

    def body(c_ref, w_ref, b_ref, o_ref):
        o_ref[...] = _dot3_f32(_silu(c_ref[...]), w_ref[...], 1, 0) + b_ref[...]

    return pl.pallas_call(
        body, name="ada_fwd", grid=(cols // ADA_TILE,),
        in_specs=[pl.BlockSpec((ADA_ROWS, D), lambda j: (0, 0)), pl.BlockSpec((D, ADA_TILE), lambda j: (0, j)),
                  pl.BlockSpec((1, ADA_TILE), lambda j: (0, j))],
        out_specs=pl.BlockSpec((ADA_ROWS, ADA_TILE), lambda j: (0, j)), out_shape=jax.ShapeDtypeStruct((ADA_ROWS, cols), F32),
        compiler_params=_cparams(("arbitrary",)),
    )(cvec, w, b)


def _adam(w, g, m, v):
    m2 = ADAM_B1 * m + (1.0 - ADAM_B1) * g
    v2 = ADAM_B2 * v + (1.0 - ADAM_B2) * (g * g)
    m_hat = m2 / (1.0 - ADAM_B1 ** ADAM_STEP)
    v_hat = v2 / (1.0 - ADAM_B2 ** ADAM_STEP)
    return -ADAM_LR * (m_hat / (jnp.sqrt(v_hat) + ADAM_EPS) + ADAM_WD * w), m2, v2


def _wada_bwd_adam(cvec, dada, w, m, v):
    rows, cols = w.shape
    tr = _tile(rows, 256, 128)

    def body(c_ref, d_ref, w_ref, m_ref, v_ref, g_ref, dl_ref, m2_ref, v2_ref, p_ref):
        wv = w_ref[...]
        g = _dot3_f32(_silu(c_ref[...]), d_ref[...], 0, 0)
        g_ref[...] = g
        dl_ref[...], m2_ref[...], v2_ref[...] = _adam(wv, g, m_ref[...], v_ref[...])
        p_ref[...] = _dot3_f32(d_ref[...], wv, 1, 1)

    blk = pl.BlockSpec((tr, cols), lambda i: (i, 0))
    return pl.pallas_call(
        body, name="wada_bwd_adam", grid=(rows // tr,),
        in_specs=[pl.BlockSpec((ADA_ROWS, tr), lambda i: (0, i)), pl.BlockSpec((ADA_ROWS, cols), lambda i: (0, 0)), blk, blk, blk],
        out_specs=[blk, blk, blk, blk, pl.BlockSpec((ADA_ROWS, tr), lambda i: (0, i))],
        out_shape=[jax.ShapeDtypeStruct((rows, cols), F32)] * 4 + [jax.ShapeDtypeStruct((ADA_ROWS, rows), F32)],
        compiler_params=_cparams(("arbitrary",)),
    )(cvec, dada, w, m, v)


def _reduce_adam(parts8, w, m, v, name):
    rows, cols = w.shape
    tr = _tile(rows, 64, 16)

    def body(p_ref, w_ref, m_ref, v_ref, g_ref, dl_ref, m2_ref, v2_ref):
        g = p_ref[0].astype(F32) + p_ref[N_CHIPS].astype(F32)
        for j in range(1, N_CHIPS):
            g = g + (p_ref[j].astype(F32) + p_ref[N_CHIPS + j].astype(F32))
        g_ref[...] = g
        dl_ref[...], m2_ref[...], v2_ref[...] = _adam(w_ref[...], g, m_ref[...], v_ref[...])

    blk = pl.BlockSpec((tr, cols), lambda i: (i, 0))
    return pl.pallas_call(
        body, name=name, grid=(rows // tr,), in_specs=[pl.BlockSpec((N_DEV, tr, cols), lambda i: (0, i, 0)), blk, blk, blk],
        out_specs=[blk] * 4, out_shape=[jax.ShapeDtypeStruct((rows, cols), F32)] * 4, compiler_params=_cparams(("arbitrary",)),
    )(parts8, w, m, v)


SMALL_W = 1024


def _sum8(g8):
    rows = g8.shape[1]

    def body(g_ref, o_ref):
        s = g_ref[0]
        for j in range(1, N_DEV):
            s = s + g_ref[j]
        o_ref[...] = s

    return pl.pallas_call(
        body, name="sum8", out_shape=jax.ShapeDtypeStruct((rows, SMALL_W), F32),
        in_specs=[pl.BlockSpec(memory_space=pltpu.VMEM)], out_specs=pl.BlockSpec(memory_space=pltpu.VMEM),
        compiler_params=pltpu.CompilerParams(vmem_limit_bytes=VMEM_LIMIT),
    )(g8)


def _cctx_grad(p8, c_ctx):
    def body(p_ref, c_ref, o_ref):
        s = p_ref[0]
        for chip in range(1, N_CHIPS):
            s = s + p_ref[2 * chip]
        o_ref[...] = s * _dsilu(c_ref[...])

    return pl.pallas_call(
        body, name="cctx_grad", out_shape=jax.ShapeDtypeStruct((1, D), F32),
        in_specs=[pl.BlockSpec(memory_space=pltpu.VMEM)] * 2, out_specs=pl.BlockSpec(memory_space=pltpu.VMEM),
    )(p8, c_ctx)


def _adam_small(w, g, m, v):
    def body(w_ref, g_ref, m_ref, v_ref, dl_ref, m2_ref, v2_ref):
        dl_ref[...], m2_ref[...], v2_ref[...] = _adam(w_ref[...], g_ref[...], m_ref[...], v_ref[...])

    vm = pl.BlockSpec(memory_space=pltpu.VMEM)
    return pl.pallas_call(
        body, name="adam_small", out_shape=[jax.ShapeDtypeStruct(w.shape, F32)] * 3, in_specs=[vm] * 4, out_specs=[vm] * 3,
        compiler_params=pltpu.CompilerParams(vmem_limit_bytes=VMEM_LIMIT),
    )(w, g, m, v)


def _pack(vecs, width=SMALL_W, row_mult=8):
    flat = jnp.concatenate([v.reshape(-1).astype(F32) for v in vecs])
    n = flat.shape[0]
    rows = -(-n // (width * row_mult)) * row_mult
    return jnp.pad(flat, (0, rows * width - n)).reshape(rows, width)


def _unpack(packed, shapes):
    flat = packed.reshape(-1)
    out, off = [], 0
    for s in shapes:
        n = int(np.prod(s))
        out.append(flat[off:off + n].reshape(s))
        off += n
    return out


WEIGHTS = ('c_ctx', 'w_ada', 'b_ada', 'norm1_w', 'w_in', 'gla_up_f', 'gla_bias_f', 'gla_up_b', 'gla_bias_b', 'gla_norm_w',
           'conv_w', 'conv_b', 'dt_bias_f', 'dt_bias_b', 'a_log_f', 'a_log_b', 'd_skip', 'ssm_norm_w', 'w_pa', 'w_pb', 'w_out',
           'norm2_w', 'w_gate', 'w_up', 'w_down', 'final_norm_w')
BIG = ('w_in', 'w_pa', 'w_pb', 'w_out', 'w_gate', 'w_up', 'w_down')
COL_SHARDED = ('w_in', 'w_gate', 'w_up')
SMALL_SHARDED = ('gla_up_f', 'gla_up_b', 'conv_w')
ROW_TILE = 256
MM_ROWS = 1408
MM_ROWS_SMALL = 768
MM_KROWS = 2816


def _blocks_to_full(g4, name):
    n, r, c = g4.shape
    return g4.transpose(1, 0, 2).reshape(r, n * c) if name in COL_SHARDED else g4.reshape(n * r, c)


def _full_to_blocks(full, name):
    r, c = full.shape
    if name in COL_SHARDED:
        return full.reshape(r, N_CHIPS, c // N_CHIPS).transpose(1, 0, 2)
    return full.reshape(N_CHIPS, r // N_CHIPS, c)


def _permute_in(w_in_full):
    off = np.concatenate([[0], np.cumsum(IN_WIDTHS)])
    cols = [w_in_full[:, off[p]:off[p + 1]] for p in PERM]
    return jnp.concatenate(cols + [jnp.zeros((w_in_full.shape[0], SMALL_PAD), w_in_full.dtype)], axis=1)


def _unpermute_in(wp):
    off = np.concatenate([[0], np.cumsum([IN_WIDTHS[p] for p in PERM])])
    pieces = {p: wp[:, off[i]:off[i + 1]] for i, p in enumerate(PERM)}
    return jnp.concatenate([pieces[p] for p in range(len(IN_WIDTHS))], axis=1)


def _chip_cols(full, chip, n):
    return lax.dynamic_slice_in_dim(full, chip * n, n, axis=1)


def kernel(x, c, ctx, c_ctx, w_ada, b_ada, norm1_w, w_in, gla_up_f, gla_bias_f, gla_up_b, gla_bias_b, gla_norm_w, conv_w, conv_b, dt_bias_f, dt_bias_b, a_log_f, a_log_b, d_skip, ssm_norm_w, w_pa, w_pb, w_out, norm2_w, w_gate, w_up, w_down, final_norm_w, loss_target, m_c_ctx, m_w_ada, m_b_ada, m_norm1_w, m_w_in, m_gla_up_f, m_gla_bias_f, m_gla_up_b, m_gla_bias_b, m_gla_norm_w, m_conv_w, m_conv_b, m_dt_bias_f, m_dt_bias_b, m_a_log_f, m_a_log_b, m_d_skip, m_ssm_norm_w, m_w_pa, m_w_pb, m_w_out, m_norm2_w, m_w_gate, m_w_up, m_w_down, m_final_norm_w, v_c_ctx, v_w_ada, v_b_ada, v_norm1_w, v_w_in, v_gla_up_f, v_gla_bias_f, v_gla_up_b, v_gla_bias_b, v_gla_norm_w, v_conv_w, v_conv_b, v_dt_bias_f, v_dt_bias_b, v_a_log_f, v_a_log_b, v_d_skip, v_ssm_norm_w, v_w_pa, v_w_pb, v_w_out, v_norm2_w, v_w_gate, v_w_up, v_w_down, v_final_norm_w):
    given = dict(locals())
    W = {n: given[n] for n in WEIGHTS}
    M = {n: given["m_" + n] for n in WEIGHTS}
    V = {n: given["v_" + n] for n in WEIGHTS}
    L, Lc = x.shape[1], ctx.shape[1]
    tr = ROW_TILE
    assert L % tr == 0 and Lc % tr == 0 and L % Lc == 0 and Lc % SSM_C == 0
    n_lat_tiles = L // tr
    xi, yi, ci = _place()
    chip, me = 2 * xi + yi, 4 * xi + 2 * yi + ci
    x2, ctx2 = x[0], ctx[0]

    g0 = _allgather_small(_pack([c[0]] + [W[n][0] for n in SMALL_SHARDED]), "gather_c")
    g0 = g0.reshape(N_DEV, -1)
    c_all = g0[:, :D]
    small_full, off = {}, D
    for n in SMALL_SHARDED:
        r, cols = W[n].shape[1:]
        small_full[n] = jnp.concatenate([g0[2 * k, off:off + r * cols].reshape(r, cols) for k in range(N_CHIPS)], axis=1)
        off += r * cols
    up_f, up_b, conv_w_full = (small_full[n] for n in SMALL_SHARDED)

    cvec = jnp.zeros((ADA_ROWS, D), F32).at[:N_DEV].set(c_all).at[N_DEV].set(c_ctx)
    ada_cols = w_ada.shape[2]
    ada_part = _ada_fwd(cvec, w_ada[0], _chip_cols(b_ada, chip, ada_cols))
    g1_all = _allgather_small(ada_part, "gather_ada")
    ada_full = jnp.concatenate([g1_all[2 * k] for k in range(N_CHIPS)], axis=1)
    mine = lax.dynamic_slice_in_dim(ada_full, me, 1, axis=0)
    sh1, sc1, g1, sh2, sc2, g2 = (mine[:, k * D:(k + 1) * D] for k in range(6))
    csh1, csc1 = ada_full[N_DEV:N_DEV + 1, :D], ada_full[N_DEV:N_DEV + 1, D:2 * D]
    mod = jnp.stack([jnp.stack([sh1, sc1]), jnp.stack([csh1, csc1])])

    later = [n for n in BIG if n != 'w_in']

    def lr_rows(up, base):
        return jnp.zeros((128, GLA_H * GLA_DK), F32).at[base:base + GLA_RANK].set(up)
    u2 = jnp.stack([lr_rows(up_f, SM_LRF), lr_rows(up_b, SM_LRB)])
    u2_hi = u2.astype(BF16)
    u2_lo = (u2 - u2_hi.astype(F32)).astype(BF16)
    ut = u2.transpose(0, 2, 1)
    ut_hi = ut.astype(BF16)
    ut_lo = (ut - ut_hi.astype(F32)).astype(BF16)
    gbias = jnp.stack([gla_bias_f, gla_bias_b])
    kc = _ssd_consts(jnp.stack([dt_bias_f[0], dt_bias_b[0]]), jnp.stack([a_log_f[0], a_log_b[0]]))
    gw4 = jnp.tile(gla_norm_w, (1, GLA_H))
    dskip_exp = jnp.repeat(d_skip, SSM_P, axis=1)
    n_gla = (L // GLA_C, Lc // GLA_C)
    n_ssd = (L // SSM_C, Lc // SSM_C)

    h1, (w_in_blocks,) = _norm_mod(x2, ctx2, norm1_w, mod, n_lat_tiles, tr, _gather_split_comm(w_in[0].astype(BF16)))
    full = {'w_in': _blocks_to_full(w_in_blocks, 'w_in')}
    wp = _permute_in(full['w_in'])
    parts = _mm(h1, wp, "nn", ACT, "mm_in", tm=MM_ROWS, tn=PW // 3)
    sm = _mm(h1, wp[:, COL_SM:], "nn", F32, "mm_in_small", tm=MM_ROWS)
    xbc = _conv_fwd(parts, conv_w_full, conv_b, L // Lc, Lc)
    *o2, gla_hist = _gla_fwd(parts, sm, u2_hi, u2_lo, gbias, *n_gla)
    (*y2, ssd_hist), gathered = _ssd_fwd(xbc, sm, kc, *n_ssd, _gather_comm([W[n][0].astype(BF16) for n in later]))
    full.update({n: _blocks_to_full(g, n) for n, g in zip(later, gathered)})
    oan = _gla_out(o2, parts, gw4, tr)
    obn = _ssd_out(y2, xbc, parts, dskip_exp, ssm_norm_w, tr)
    ya = _mm(oan, full['w_pa'], "nn", ACT, "mm_pa", tm=MM_ROWS)
    yb = _mm(obn, full['w_pb'], "nn", ACT, "mm_pb", tm=MM_ROWS)
    merged = _merge(ya, yb, parts, tr)
    mix = _mm(merged, full['w_out'], "nn", ACT, "mm_out", tm=MM_ROWS)
    h2, u = _resid_norm_mod(x2, ctx2, mix, g1, norm2_w, sh2, sc2, n_lat_tiles, tr)
    gp = _mm(u, full['w_gate'], "nn", ACT, "mm_gate", tm=MM_ROWS, tn=D_FF)
    up = _mm(u, full['w_up'], "nn", ACT, "mm_up", tm=MM_ROWS, tn=D_FF)
    f = _mm(gp, full['w_down'], "nn", ACT, "mm_down", tm=MM_ROWS_SMALL, tk=D_FF // 2, swiglu_up=up)
    dh3, df, acc_loss = _loss_head(h2, f, loss_target[0], g2, final_norm_w[None], n_lat_tiles, tr)

    dw = {}
    da = _mm(df, full['w_down'], "nt", ACT, "mm_down_dx", tm=MM_ROWS, tn=D_FF)
    dw['w_down'] = _mm(gp, df, "tn", BF16, "mm_down_dw", tm=D_FF // 2, tk=MM_ROWS, swiglu_up=up)
    dgp, dup = _swiglu_act_bwd(da, gp, up, tr)
    du_a = _mm(dgp, full['w_gate'], "nt", ACT, "mm_gate_dx", tm=MM_ROWS, tk=D_FF)
    du_b = _mm(dup, full['w_up'], "nt", ACT, "mm_up_dx", tm=MM_ROWS, tk=D_FF)
    dw['w_gate'] = _mm(u, dgp, "tn", BF16, "mm_gate_dw", tm=D, tn=D_FF // 2, tk=MM_KROWS)
    dw['w_up'] = _mm(u, dup, "tn", BF16, "mm_up_dw", tm=D, tn=D_FF // 2, tk=MM_KROWS)
    dh2, dmix, acc_ffn = _ffn_in_bwd(du_a, du_b, h2, dh3, mix, sc2, g1, norm2_w, tr)
    dmerged = _mm(dmix, full['w_out'], "nt", ACT, "mm_out_dx", tm=MM_ROWS)
    dw['w_out'] = _mm(merged, dmix, "tn", BF16, "mm_out_dw", tm=D, tk=MM_KROWS)
    dya, dyb, dparts = _merge_bwd(dmerged, ya, yb, parts, lax.empty((L + Lc, PW), BF16), tr)
    doan = _mm(dya, full['w_pa'], "nt", ACT, "mm_pa_dx", tm=MM_ROWS)
    dw['w_pa'] = _mm(oan, dya, "tn", BF16, "mm_pa_dw", tm=D, tk=MM_KROWS)
    dobn = _mm(dyb, full['w_pb'], "nt", ACT, "mm_pb_dx", tm=MM_ROWS)
    dw['w_pb'] = _mm(obn, dyb, "tn", BF16, "mm_pb_dw", tm=D, tk=MM_KROWS)
    do, dparts, acc_gla = _gla_out_bwd(doan, o2, parts, gw4, dparts, tr)
    dq, dk, dv, dpre = _gla_bwd(do, parts, sm, u2_hi, u2_lo, gbias, gla_hist, *n_gla)
    dy, dparts, acc_ssd = _ssd_out_bwd(dobn, y2, xbc, parts, dskip_exp, ssm_norm_w, dparts, tr)
    (dx_scan, db_scan, dc_scan, dsm, acc_alog), exchanged = _ssd_bwd(
        dy, xbc, sm, kc, ssd_hist, *n_ssd, _exchange_comm([_full_to_blocks(dw[n], n) for n in later]))
    exchanged = dict(zip(later, exchanged))
    dparts, acc_conv = _conv_bwd(dx_scan, db_scan, dc_scan, dy, dskip_exp, parts, conv_w_full, conv_b, dparts, L // Lc, Lc)
    dparts = _gla_assemble(dq, dk, dv, dparts, tr)
    dparts, dup_gla, acc_gbias, acc_dtb = _small_assemble(dpre, dsm, sm, ut_hi, ut_lo, dparts, tr)
    dw['w_in'] = _unpermute_in(_mm(h1, dparts, "tn", BF16, "mm_in_dw", tm=D, tn=PW // 9, tk=MM_KROWS))
    dh1, (exchanged['w_in'],) = _mm(dparts, wp, "nt", F32, "mm_in_dx", tm=MM_ROWS_SMALL, tk=PW // 3,
                                    comm=_exchange_comm([_full_to_blocks(dw['w_in'], 'w_in')]))
    dx, acc_n1 = _norm1_bwd(dh1, x2, ctx2, dh2, norm1_w, mod, n_lat_tiles, tr)

    partial = dict(
        norm1_w=acc_n1[0, 2] + acc_n1[1, 2],
        gla_up_f=dup_gla[0, SM_LRF:SM_LRF + GLA_RANK], gla_bias_f=acc_gbias[0],
        gla_up_b=dup_gla[1, SM_LRB:SM_LRB + GLA_RANK], gla_bias_b=acc_gbias[1],
        gla_norm_w=acc_gla[0].reshape(GLA_H, GLA_DV).sum(0),
        conv_w=acc_conv[:SSM_CONV], conv_b=acc_conv[SSM_CONV],
        dt_bias_f=acc_dtb[0, SM_DTF:SM_DTF + SSM_HEADS], dt_bias_b=acc_dtb[0, SM_DTB:SM_DTB + SSM_HEADS],
        a_log_f=acc_alog[0, :, 0, :SSM_HPG], a_log_b=acc_alog[1, :, 0, :SSM_HPG],
        d_skip=acc_ssd[1].reshape(SSM_HEADS, SSM_P).sum(1), ssm_norm_w=acc_ssd[0],
        norm2_w=acc_ffn[2], final_norm_w=acc_loss[0],
    )
    dada = jnp.concatenate([acc_n1[0, 1], acc_n1[0, 0], acc_ffn[3], acc_ffn[1], acc_ffn[0], acc_loss[1]])
    dada_ctx = jnp.concatenate([acc_n1[1, 1], acc_n1[1, 0], jnp.zeros((4 * D,), F32)])
    names = list(partial)
    payload = [partial[n] for n in names] + [dada + dada_ctx, dada_ctx, acc_loss[2], dada]
    sizes = [int(np.prod(p.shape)) for p in payload]
    g8 = _allgather_small(_pack(payload), "gather_small_grads")
    summed = _unpack(_sum8(g8), [(s,) for s in sizes])
    grads = {n: s.reshape(W[n].shape if n not in SMALL_SHARDED else partial[n].shape) for n, s in zip(names, summed)}
    grads['b_ada'] = summed[len(names)].reshape(b_ada.shape)
    dada_ctx_sum = summed[len(names) + 1]
    loss = jnp.sum(summed[len(names) + 2])
    dada_all = g8.reshape(N_DEV, -1)[:, sum(sizes[:-1]):sum(sizes)]

    dada16 = jnp.zeros((ADA_ROWS, ada_cols), F32)
    dada16 = dada16.at[:N_DEV].set(_chip_cols(dada_all, chip, ada_cols)).at[N_DEV].set(_chip_cols(dada_ctx_sum[None], chip, ada_cols)[0])
    g_wada, dl_wada, m_wada, v_wada, p16 = _wada_bwd_adam(cvec, dada16, w_ada[0], m_w_ada[0], v_w_ada[0])
    p8 = _allgather_small(p16[N_DEV:], "gather_cctx")
    grads['c_ctx'] = _cctx_grad(p8[:, 0:1, :], c_ctx[None])[0]
    for n in SMALL_SHARDED:
        grads[n] = _chip_cols(grads[n], chip, W[n].shape[2])[None]

    small = [n for n in WEIGHTS if n not in BIG and n != 'w_ada']
    shapes = [W[n].shape for n in small]
    dl_s, m_s, v_s = _adam_small(*[_pack([d[n] for n in small]) for d in (W, grads, M, V)])
    delta = dict(zip(small, _unpack(dl_s, shapes)))
    new_m = dict(zip(small, _unpack(m_s, shapes)))
    new_v = dict(zip(small, _unpack(v_s, shapes)))
    grads['w_ada'], delta['w_ada'], new_m['w_ada'], new_v['w_ada'] = g_wada[None], dl_wada[None], m_wada[None], v_wada[None]

    for n in BIG:
        g, dl, m2, v2 = _reduce_adam(exchanged[n], W[n][0], M[n][0], V[n][0], "adam_" + n)
        grads[n], delta[n], new_m[n], new_v[n] = g[None], dl[None], m2[None], v2[None]

    return (loss, dx[None], *[grads[n] for n in WEIGHTS], *[delta[n] for n in WEIGHTS],
            *[new_m[n] for n in WEIGHTS], *[new_v[n] for n in WEIGHTS])
```

```python
import numpy as np
import jax
import jax.numpy as jnp
from jax import lax
from jax.experimental import pallas as pl
from jax.experimental.pallas import tpu as pltpu

F32 = jnp.float32
BF16 = jnp.bfloat16
ACT = BF16
MESH = pl.DeviceIdType.MESH

D = 1024
EPS = 1e-6
GRID_W = 64
GLA_H, GLA_DK, GLA_DV, GLA_RANK, GLA_TAU = 4, 128, 256, 16, 16.0
GLA_C = 128
SSM_INNER, SSM_P, SSM_HEADS, SSM_G, SSM_HPG, SSM_N = 2048, 64, 32, 4, 8, 128
SSM_C = 128
SSM_CONV, CONV_LEFT = 4, 2
D_FF = 2816
IN_WIDTHS = (512, 512, 1024, 1024, 16, 16, 2048, 2048, 512, 512, 32, 32, 1024, 1024)
D_IN = sum(IN_WIDTHS)
PERM = (6, 7, 8, 9, 3, 0, 1, 2, 12, 13, 4, 5, 10, 11)
PW = 10368
SMALL_PAD = PW - D_IN
COL_Z, COL_XBC, COL_R, COL_Q, COL_K, COL_V, COL_GA, COL_GB, COL_SM = 0, 2048, 5120, 6144, 6656, 7168, 8192, 9216, 10240
SM_LRF, SM_LRB, SM_DTF, SM_DTB = 0, 16, 32, 64
EXP_CLAMP = 80.0
ADAM_LR, ADAM_B1, ADAM_B2, ADAM_EPS, ADAM_WD, ADAM_STEP = 0.001, 0.9, 0.999, 1e-08, 0.01, 10
N_CHIPS, N_DEV = 4, 8
VMEM_LIMIT = 56 * 1024 * 1024


def _cparams(sem=None):
    return pltpu.CompilerParams(dimension_semantics=sem, vmem_limit_bytes=VMEM_LIMIT)


def _dg(a, b, ca, cb):
    return lax.dot_general(a, b, (((ca,), (cb,)), ((), ())), preferred_element_type=F32)


def _nn(a, b):
    return _dg(a, b, 1, 0)


def _nt(a, b):
    return _dg(a, b, 1, 1)


def _tn(a, b):
    return _dg(a, b, 0, 0)


def _bf(x):
    return x.astype(BF16)


def _f32(ref):
    return ref[...].astype(F32)


def _split(x):
    hi = x.astype(BF16)
    return hi, (x - hi.astype(F32)).astype(BF16)


def _nn_x(a, b_exact):
    hi, lo = _split(a)
    return _nn(hi, b_exact) + _nn(lo, b_exact)


def _x_nn(a_exact, b):
    hi, lo = _split(b)
    return _nn(a_exact, hi) + _nn(a_exact, lo)


def _nn3(a, b_hi, b_lo):
    hi, lo = _split(a)
    return _nn(hi, b_hi) + _nn(lo, b_hi) + _nn(hi, b_lo)


def _sigmoid(x):
    return 1.0 / (1.0 + jnp.exp(-x))


def _silu(x):
    return x * _sigmoid(x)


def _dsilu(x):
    s = _sigmoid(x)
    return s * (1.0 + x * (1.0 - s))


def _softplus(x):
    return jnp.maximum(x, 0.0) + jnp.log(1.0 + jnp.exp(-jnp.abs(x)))


def _log_sigmoid(x):
    return jnp.minimum(x, 0.0) - jnp.log(1.0 + jnp.exp(-jnp.abs(x)))


def _tile(n, target, mult=8):
    best = None
    for t in range(mult, min(n, target) + 1, mult):
        if n % t == 0:
            best = t
    assert best is not None, (n, target, mult)
    return best


def _mm(a, b, mode, out_dtype, name, tm=512, tn=1024, tk=2048, comm=None, swiglu_up=None):
    if mode == "nn":
        (M, K), N = a.shape, b.shape[1]
    elif mode == "nt":
        (M, K), N = a.shape, b.shape[0]
    else:
        (K, M), N = a.shape, b.shape[1]
    tm, tn, tk = _tile(M, tm, 128), _tile(N, tn, 128), _tile(K, tk, 128)
    nk = K // tk
    ca, cb = {"nn": (1, 0), "nt": (1, 1), "tn": (0, 0)}[mode]

    def body(a_ref, *rest):
        if swiglu_up is None:
            (b_ref, o_ref, *acc), av = rest, a_ref[...]
        else:
            (up_ref, b_ref, o_ref, *acc) = rest
            av = (_silu(a_ref[...].astype(F32)) * up_ref[...].astype(F32)).astype(BF16)
        part = _dg(av, b_ref[...], ca, cb)
        if nk == 1:
            o_ref[...] = part.astype(out_dtype)
        else:
            k = pl.program_id(2)

            @pl.when(k == 0)
            def _():
                acc[0][...] = part

            @pl.when(k > 0)
            def _():
                acc[0][...] += part

            @pl.when(k == nk - 1)
            def _():
                o_ref[...] = acc[0][...].astype(out_dtype)

    a_spec = pl.BlockSpec((tk, tm), lambda i, j, k: (k, i)) if mode == "tn" else pl.BlockSpec((tm, tk), lambda i, j, k: (i, k))
    b_spec = pl.BlockSpec((tn, tk), lambda i, j, k: (j, k)) if mode == "nt" else pl.BlockSpec((tk, tn), lambda i, j, k: (k, j))
    gi, gj = M // tm, N // tn
    scratch = [pltpu.VMEM((tm, tn), F32)] if nk > 1 else []
    out_spec, out_shape = pl.BlockSpec((tm, tn), lambda i, j, k: (i, j)), jax.ShapeDtypeStruct((M, N), out_dtype)
    if comm is None:
        lhs = [a] if swiglu_up is None else [a, swiglu_up]
        return pl.pallas_call(
            body, name=name, grid=(gi, gj, nk), in_specs=[a_spec] * len(lhs) + [b_spec], out_specs=out_spec,
            out_shape=out_shape, scratch_shapes=scratch, compiler_params=_cparams(("arbitrary", "arbitrary", "arbitrary")),
        )(*lhs, b)
    assert swiglu_up is None
    at = lambda i, j, k: (pl.program_id(0) == i) & (pl.program_id(1) == j) & (pl.program_id(2) == k)
    hosted = _hosted(body, 2, 1, len(scratch), comm, lambda: at(0, 0, 0), lambda: at(gi - 1, 0, 0),
                     lambda: at(gi - 1, gj - 1, nk - 1))
    outs = pl.pallas_call(
        hosted, name=name, grid=(gi, gj, nk), in_specs=[a_spec, b_spec] + [_ANY] * len(comm.arrays),
        out_specs=[out_spec] + [_ANY] * len(comm.out_shape), out_shape=[out_shape] + comm.out_shape,
        scratch_shapes=scratch + comm.scratch, compiler_params=_cparams(("arbitrary", "arbitrary", "arbitrary")),
    )(a, b, *comm.arrays)
    return outs[0], outs[1:]


def _place():
    return lax.axis_index("x"), lax.axis_index("y"), lax.axis_index("c")


def _flip(v, bit):
    return 1 - v if bit else v


def _allgather_small(v, name):
    R, C = v.shape

    def body(v_ref, out_ref, send_sems, recv_sems, local_sem):
        x, y, c = _place()
        me = 4 * x + 2 * y + c
        mine = pltpu.make_async_copy(v_ref, out_ref.at[me], local_sem)
        mine.start()

        def peer(r):
            return _flip(x, (r >> 2) & 1), _flip(y, (r >> 1) & 1), _flip(c, r & 1)

        sends = [pltpu.make_async_remote_copy(
            src_ref=v_ref, dst_ref=out_ref.at[me], send_sem=send_sems.at[r - 1], recv_sem=recv_sems.at[r - 1],
            device_id=peer(r), device_id_type=MESH) for r in range(1, N_DEV)]
        for cp in sends:
            cp.start()
        for r in range(1, N_DEV):
            px, py, pc = peer(r)
            pltpu.make_async_remote_copy(
                src_ref=v_ref, dst_ref=out_ref.at[4 * px + 2 * py + pc], send_sem=send_sems.at[r - 1],
                recv_sem=recv_sems.at[r - 1], device_id=(x, y, c), device_id_type=MESH).wait_recv()
        for cp in sends:
            cp.wait_send()
        mine.wait()

    return pl.pallas_call(
        body, name=name, out_shape=jax.ShapeDtypeStruct((N_DEV, R, C), v.dtype),
        in_specs=[pl.BlockSpec(memory_space=pltpu.VMEM)], out_specs=pl.BlockSpec(memory_space=pltpu.VMEM),
        scratch_shapes=[pltpu.SemaphoreType.DMA((N_DEV - 1,)), pltpu.SemaphoreType.DMA((N_DEV - 1,)), pltpu.SemaphoreType.DMA],
        compiler_params=pltpu.CompilerParams(vmem_limit_bytes=VMEM_LIMIT),
    )(v)


_CHIP_RELATIONS = ((1, 0), (0, 1), (1, 1))


class _Comm:
    def __init__(self, arrays, out_shape, scratch, start, middle, finish):
        self.arrays, self.out_shape, self.scratch = arrays, out_shape, scratch
        self.start, self.middle, self.finish = start, middle, finish


def _hosted(body, n_in, n_out, n_scratch, comm, first, middle, last):
    nc, no = len(comm.arrays), len(comm.out_shape)

    def wrapped(*refs):
        a = n_in + nc
        b = a + n_out + no
        ins, c_ins, outs, c_outs = refs[:n_in], refs[n_in:a], refs[a:a + n_out], refs[a + n_out:b]
        scratch, c_sems = refs[b:b + n_scratch], refs[b + n_scratch:]

        @pl.when(first())
        def _():
            comm.start(c_ins, c_outs, c_sems)

        body(*ins, *outs, *scratch)
        if comm.middle is not None:
            @pl.when(middle())
            def _():
                comm.middle(c_ins, c_outs, c_sems)

        @pl.when(last())
        def _():
            comm.finish(c_ins, c_outs, c_sems)

    return wrapped


_ANY = pl.BlockSpec(memory_space=pl.ANY)


def _gather_split_comm(shard):
    rows, cols = shard.shape
    half = rows // 2

    def copies(kind, ins, outs, sems):
        (in_ref,), (out_ref,), (send_sems, recv_sems, local_sem) = ins, outs, sems
        x, y, c = _place()
        chip = 2 * x + y
        mine = pl.ds(pl.multiple_of(c * half, 16), half)
        if kind == "local":
            return [pltpu.make_async_copy(in_ref, out_ref.at[chip], local_sem)]
        made = []
        for j, (fx, fy) in enumerate(_CHIP_RELATIONS):
            px, py = _flip(x, fx), _flip(y, fy)
            landed = out_ref.at[2 * px + py, mine]
            src, dst, k, to = {
                "send": (in_ref.at[mine], out_ref.at[chip, mine], j, (px, py, c)),
                "landed": (landed, landed, j, (x, y, c)),
                "passed": (landed, landed, 3 + j, (x, y, 1 - c)),
                "arrivals": (landed, landed, 3 + j, (x, y, c)),
            }[kind]
            made.append(pltpu.make_async_remote_copy(src_ref=src, dst_ref=dst, send_sem=send_sems.at[k],
                                                     recv_sem=recv_sems.at[k], device_id=to, device_id_type=MESH))
        return made

    def start(ins, outs, sems):
        for cp in copies("local", ins, outs, sems) + copies("send", ins, outs, sems):
            cp.start()

    def middle(ins, outs, sems):
        for got, fwd in zip(copies("landed", ins, outs, sems), copies("passed", ins, outs, sems)):
            got.wait_recv()
            fwd.start()

    def finish(ins, outs, sems):
        for cp in copies("arrivals", ins, outs, sems):
            cp.wait_recv()
        for cp in copies("send", ins, outs, sems) + copies("passed", ins, outs, sems):
            cp.wait_send()
        for cp in copies("local", ins, outs, sems):
            cp.wait()

    return _Comm([shard], [jax.ShapeDtypeStruct((N_CHIPS, rows, cols), shard.dtype)],
                 [pltpu.SemaphoreType.DMA((6,)), pltpu.SemaphoreType.DMA((6,)), pltpu.SemaphoreType.DMA],
                 start, middle, finish)


def _gather_comm(shards):
    n = len(shards)

    def copies(kind, ins, outs, sems):
        send_sems, recv_sems, local_sems = sems
        x, y, c = _place()
        chip = 2 * x + y
        if kind == "local":
            return [pltpu.make_async_copy(ins[i], outs[i].at[chip], local_sems.at[i]) for i in range(n)]
        made = []
        for i in range(n):
            for j, (fx, fy) in enumerate(_CHIP_RELATIONS):
                px, py = _flip(x, fx), _flip(y, fy)
                slot, to = (chip, (px, py, c)) if kind == "send" else (2 * px + py, (x, y, c))
                made.append(pltpu.make_async_remote_copy(
                    src_ref=ins[i], dst_ref=outs[i].at[slot], send_sem=send_sems.at[i, j], recv_sem=recv_sems.at[i, j],
                    device_id=to, device_id_type=MESH))
        return made

    def start(ins, outs, sems):
        for cp in copies("local", ins, outs, sems) + copies("send", ins, outs, sems):
            cp.start()

    def finish(ins, outs, sems):
        for cp in copies("recv", ins, outs, sems):
            cp.wait_recv()
        for cp in copies("send", ins, outs, sems):
            cp.wait_send()
        for cp in copies("local", ins, outs, sems):
            cp.wait()

    return _Comm(list(shards), [jax.ShapeDtypeStruct((N_CHIPS,) + s.shape, s.dtype) for s in shards],
                 [pltpu.SemaphoreType.DMA((n, 3)), pltpu.SemaphoreType.DMA((n, 3)), pltpu.SemaphoreType.DMA((n,))],
                 start, None, finish)


def _exchange_comm(blocks):
    n = len(blocks)

    def copies(kind, ins, outs, sems):
        send_sems, recv_sems, local_sems = sems
        x, y, c = _place()
        chip = 2 * x + y
        me, sibling = (x, y, c), (x, y, 1 - c)

        def remote(src, dst, i, j, to):
            return pltpu.make_async_remote_copy(src_ref=src, dst_ref=dst, send_sem=send_sems.at[i, j],
                                                recv_sem=recv_sems.at[i, j], device_id=to, device_id_type=MESH)

        made = []
        for i in range(n):
            if kind == "local":
                made.append(pltpu.make_async_copy(ins[i].at[chip], outs[i].at[chip], local_sems.at[i]))
                continue
            for j, (fx, fy) in enumerate(_CHIP_RELATIONS):
                px, py = _flip(x, fx), _flip(y, fy)
                src = 2 * px + py
                if kind == "first":
                    made.append(remote(ins[i].at[src], outs[i].at[chip], i, j, (px, py, c)))
                elif kind == "landed":
                    made.append(remote(ins[i].at[src], outs[i].at[src], i, j, me))
                elif kind == "passed":
                    made.append(remote(outs[i].at[src], outs[i].at[N_CHIPS + src], i, 4 + j, sibling))
            if kind == "first":
                made.append(remote(ins[i].at[chip], outs[i].at[N_CHIPS + chip], i, 3, sibling))
            if kind == "arrivals":
                made += [remote(ins[i].at[0], outs[i].at[0], i, j, me) for j in (3, 4, 5, 6)]
        return made

    def start(ins, outs, sems):
        for cp in copies("local", ins, outs, sems) + copies("first", ins, outs, sems):
            cp.start()

    def middle(ins, outs, sems):
        for got, fwd in zip(copies("landed", ins, outs, sems), copies("passed", ins, outs, sems)):
            got.wait_recv()
            fwd.start()

    def finish(ins, outs, sems):
        for cp in copies("arrivals", ins, outs, sems):
            cp.wait_recv()
        for cp in copies("first", ins, outs, sems) + copies("passed", ins, outs, sems):
            cp.wait_send()
        for cp in copies("local", ins, outs, sems):
            cp.wait()

    return _Comm(list(blocks), [jax.ShapeDtypeStruct((N_DEV,) + b.shape[1:], b.dtype) for b in blocks],
                 [pltpu.SemaphoreType.DMA((n, 7)), pltpu.SemaphoreType.DMA((n, 7)), pltpu.SemaphoreType.DMA((n,))],
                 start, middle, finish)


def _row_spec(tr, w, col=0):
    return pl.BlockSpec((tr, w), lambda i: (i, col))


def _vec_spec(w):
    return pl.BlockSpec((1, w), lambda i: (0, 0))


def _acc_spec(w):
    return pl.BlockSpec((8, w), lambda i: (0, 0))


def _rms(x):
    return lax.rsqrt(jnp.mean(x * x, axis=-1, keepdims=True) + EPS)


def _rms_bwd(dn, n, rstd):
    return rstd * (dn - n * jnp.mean(dn * n, axis=-1, keepdims=True))


def _colsum(x):
    return jnp.sum(x, axis=0, keepdims=True)


def _zero_first(ref):
    @pl.when(pl.program_id(0) == 0)
    def _():
        ref[...] = jnp.zeros_like(ref)


def _x_specs(tr, n_lat_tiles):
    return [pl.BlockSpec((tr, D), lambda i: (jnp.minimum(i, n_lat_tiles - 1), 0)),
            pl.BlockSpec((tr, D), lambda i: (jnp.maximum(i - n_lat_tiles, 0), 0))]


def _x_tile(x_ref, c_ref, n_lat_tiles):
    return jnp.where(pl.program_id(0) >= n_lat_tiles, c_ref[...], x_ref[...])


def _norm_mod(x, ctx, w, mod, n_lat_tiles, tr, comm):
    R = x.shape[0] + ctx.shape[0]
    n = R // tr

    def body(x_ref, c_ref, w_ref, mod_ref, o_ref):
        xv = _x_tile(x_ref, c_ref, n_lat_tiles)
        nw = xv * _rms(xv) * w_ref[...]
        o_ref[...] = (nw * (1.0 + mod_ref[0, 1]) + mod_ref[0, 0]).astype(BF16)

    at = lambda i: pl.program_id(0) == i
    outs = pl.pallas_call(
        _hosted(body, 4, 1, 0, comm, lambda: at(0), lambda: at(n // 2), lambda: at(n - 1)), name="norm1_mod", grid=(n,),
        in_specs=_x_specs(tr, n_lat_tiles) + [_vec_spec(D), pl.BlockSpec(
            (1, 2, 1, D), lambda i: (jnp.where(i >= n_lat_tiles, 1, 0), 0, 0, 0))] + [_ANY] * len(comm.arrays),
        out_specs=[_row_spec(tr, D)] + [_ANY] * len(comm.out_shape),
        out_shape=[jax.ShapeDtypeStruct((R, D), BF16)] + comm.out_shape, scratch_shapes=comm.scratch,
        compiler_params=_cparams(("arbitrary",)),
    )(x, ctx, w, mod, *comm.arrays)
    return outs[0], outs[1:]


def _resid_norm_mod(x, ctx, mix, g1, w2, sh2, sc2, n_lat_tiles, tr):
    R = x.shape[0] + ctx.shape[0]

    def body(x_ref, c_ref, mix_ref, g1_ref, w_ref, sh_ref, sc_ref, h2_ref, u_ref):
        h2 = _x_tile(x_ref, c_ref, n_lat_tiles) + g1_ref[...] * mix_ref[...]
        h2_ref[...] = h2
        n = h2 * _rms(h2) * w_ref[...]
        u_ref[...] = (n * (1.0 + sc_ref[...]) + sh_ref[...]).astype(BF16)

    return pl.pallas_call(
        body, name="resid_norm2_mod", grid=(R // tr,),
        in_specs=_x_specs(tr, n_lat_tiles) + [_row_spec(tr, D)] + [_vec_spec(D)] * 4,
        out_specs=[_row_spec(tr, D), _row_spec(tr, D)],
        out_shape=[jax.ShapeDtypeStruct((R, D), F32), jax.ShapeDtypeStruct((R, D), BF16)],
        compiler_params=_cparams(("arbitrary",)),
    )(x, ctx, mix, g1, w2, sh2, sc2)


def _loss_head(h2, f, target, g2, fw, n_lat_tiles, tr):
    R = h2.shape[0]

    def body(h2_ref, f_ref, t_ref, g2_ref, fw_ref, dh3_ref, df_ref, acc_ref):
        _zero_first(acc_ref)
        lat = pl.program_id(0) < n_lat_tiles
        fv = f_ref[...]
        h3 = h2_ref[...] + g2_ref[...] * fv
        rstd = _rms(h3)
        n = h3 * rstd
        err = n * fw_ref[...] - t_ref[...]
        dy = err * (1.0 / D)
        dh3 = jnp.where(lat, _rms_bwd(dy * fw_ref[...], n, rstd), 0.0)
        dh3_ref[...] = dh3
        df_ref[...] = (g2_ref[...] * dh3).astype(BF16)
        acc_ref[0:1, :] += jnp.where(lat, _colsum(dy * n), 0.0)
        acc_ref[1:2, :] += _colsum(dh3 * fv)
        acc_ref[2:3, :] += jnp.where(lat, _colsum(err * err) * (0.5 / D), 0.0)

    return pl.pallas_call(
        body, name="loss_head", grid=(R // tr,),
        in_specs=[_row_spec(tr, D), _row_spec(tr, D),
                  pl.BlockSpec((tr, D), lambda i: (jnp.minimum(i, n_lat_tiles - 1), 0)), _vec_spec(D), _vec_spec(D)],
        out_specs=[_row_spec(tr, D), _row_spec(tr, D), _acc_spec(D)],
        out_shape=[jax.ShapeDtypeStruct((R, D), F32), jax.ShapeDtypeStruct((R, D), BF16), jax.ShapeDtypeStruct((8, D), F32)],
        compiler_params=_cparams(("arbitrary",)),
    )(h2, f, target, g2, fw)


def _ffn_in_bwd(du_a, du_b, h2, dh3, mix, sc2, g1, w2, tr):
    R = h2.shape[0]

    def body(dua_ref, dub_ref, h2_ref, dh3_ref, mix_ref, sc_ref, g1_ref, w_ref, dh2_ref, dmix_ref, acc_ref):
        _zero_first(acc_ref)
        du = _f32(dua_ref) + _f32(dub_ref)
        h2 = h2_ref[...]
        rstd = _rms(h2)
        n = h2 * rstd
        dnw = du * (1.0 + sc_ref[...])
        dh2 = dh3_ref[...] + _rms_bwd(dnw * w_ref[...], n, rstd)
        dh2_ref[...] = dh2
        dmix_ref[...] = (g1_ref[...] * dh2).astype(BF16)
        acc_ref[0:1, :] += _colsum(du * n * w_ref[...])
        acc_ref[1:2, :] += _colsum(du)
        acc_ref[2:3, :] += _colsum(dnw * n)
        acc_ref[3:4, :] += _colsum(dh2 * mix_ref[...])

    return pl.pallas_call(
        body, name="ffn_in_bwd", grid=(R // tr,),
        in_specs=[_row_spec(tr, D)] * 5 + [_vec_spec(D)] * 3,
        out_specs=[_row_spec(tr, D), _row_spec(tr, D), _acc_spec(D)],
        out_shape=[jax.ShapeDtypeStruct((R, D), F32), jax.ShapeDtypeStruct((R, D), BF16), jax.ShapeDtypeStruct((8, D), F32)],
        compiler_params=_cparams(("arbitrary",)),
    )(du_a, du_b, h2, dh3, mix, sc2, g1, w2)


def _norm1_bwd(dh1, x, ctx, dh2, w1, mod, n_lat_tiles, tr):
    R = x.shape[0] + ctx.shape[0]

    def body(dh1_ref, x_ref, c_ref, dh2_ref, w_ref, mod_ref, dx_ref, acc_ref):
        i = pl.program_id(0)

        @pl.when((i == 0) | (i == n_lat_tiles))
        def _():
            acc_ref[...] = jnp.zeros_like(acc_ref)

        dh1 = dh1_ref[...]
        x = _x_tile(x_ref, c_ref, n_lat_tiles)
        rstd = _rms(x)
        n = x * rstd
        dnw = dh1 * (1.0 + mod_ref[0, 1])

        @pl.when(i < n_lat_tiles)
        def _():
            dx_ref[...] = dh2_ref[...] + _rms_bwd(dnw * w_ref[...], n, rstd)

        acc_ref[0, 0:1, :] += _colsum(dh1 * n * w_ref[...])
        acc_ref[0, 1:2, :] += _colsum(dh1)
        acc_ref[0, 2:3, :] += _colsum(dnw * n)

    sel = lambda i: jnp.where(i >= n_lat_tiles, 1, 0)
    return pl.pallas_call(
        body, name="norm1_bwd", grid=(R // tr,),
        in_specs=[_row_spec(tr, D)] + _x_specs(tr, n_lat_tiles) + [_row_spec(tr, D), _vec_spec(D),
                                                                   pl.BlockSpec((1, 2, 1, D), lambda i: (sel(i), 0, 0, 0))],
        out_specs=[pl.BlockSpec((tr, D), lambda i: (jnp.minimum(i, n_lat_tiles - 1), 0)),
                   pl.BlockSpec((1, 8, D), lambda i: (sel(i), 0, 0))],
        out_shape=[jax.ShapeDtypeStruct((n_lat_tiles * tr, D), F32), jax.ShapeDtypeStruct((2, 8, D), F32)],
        compiler_params=_cparams(("arbitrary",)),
    )(dh1, x, ctx, dh2, w1, mod)


def _swiglu_act_bwd(da, gp, up, tr):
    R = gp.shape[0]

    def body(da_ref, g_ref, u_ref, dg_ref, du_ref):
        da, g = _f32(da_ref), _f32(g_ref)
        dg_ref[...] = (da * _f32(u_ref) * _dsilu(g)).astype(BF16)
        du_ref[...] = (da * _silu(g)).astype(BF16)

    return pl.pallas_call(
        body, name="swiglu_act_bwd", grid=(R // tr,), in_specs=[_row_spec(tr, D_FF)] * 3, out_specs=[_row_spec(tr, D_FF)] * 2,
        out_shape=[jax.ShapeDtypeStruct((R, D_FF), BF16)] * 2, compiler_params=_cparams(("arbitrary",)),
    )(da, gp, up)


def _merge(ya, yb, parts, tr):
    R = ya.shape[0]

    def body(ya_ref, yb_ref, ga_ref, gb_ref, o_ref):
        o_ref[...] = (_sigmoid(_f32(ga_ref)) * _f32(ya_ref) + _sigmoid(_f32(gb_ref)) * _f32(yb_ref)).astype(BF16)

    return pl.pallas_call(
        body, name="merge", grid=(R // tr,),
        in_specs=[_row_spec(tr, D), _row_spec(tr, D), _row_spec(tr, D, COL_GA // D), _row_spec(tr, D, COL_GB // D)],
        out_specs=_row_spec(tr, D), out_shape=jax.ShapeDtypeStruct((R, D), BF16), compiler_params=_cparams(("arbitrary",)),
    )(ya, yb, parts, parts)


def _dparts_out(tr, w, col, nd=1):
    blk = col // w
    return pl.BlockSpec((tr, w), (lambda i: (i, blk)) if nd == 1 else (lambda i, j: (i, blk + j)))


def _merge_bwd(dm, ya, yb, parts, dparts, tr):
    R = ya.shape[0]

    def body(dm_ref, ya_ref, yb_ref, ga_ref, gb_ref, _, dya_ref, dyb_ref, dg_ref):
        dm = _f32(dm_ref)
        sa, sb = _sigmoid(_f32(ga_ref)), _sigmoid(_f32(gb_ref))
        dya_ref[...] = (dm * sa).astype(BF16)
        dyb_ref[...] = (dm * sb).astype(BF16)
        dg_ref[:, 0:D] = (dm * _f32(ya_ref) * sa * (1.0 - sa)).astype(BF16)
        dg_ref[:, D:2 * D] = (dm * _f32(yb_ref) * sb * (1.0 - sb)).astype(BF16)

    return pl.pallas_call(
        body, name="merge_bwd", grid=(R // tr,),
        in_specs=[_row_spec(tr, D)] * 3 + [_row_spec(tr, D, COL_GA // D), _row_spec(tr, D, COL_GB // D), _ANY],
        out_specs=[_row_spec(tr, D), _row_spec(tr, D), _dparts_out(tr, 2 * D, COL_GA)],
        out_shape=[jax.ShapeDtypeStruct((R, D), BF16)] * 2 + [jax.ShapeDtypeStruct(dparts.shape, BF16)],
        input_output_aliases={5: 2}, compiler_params=_cparams(("arbitrary",)),
    )(dm, ya, yb, parts, parts, dparts)


def _gla_out(o2, parts, gw4, tr):
    R = parts.shape[0]

    def body(of_ref, ob_ref, r_ref, w_ref, out_ref):
        oa = _f32(of_ref) + _f32(ob_ref)
        sr = _silu(_f32(r_ref))
        for h in range(GLA_H):
            s = slice(h * GLA_DV, (h + 1) * GLA_DV)
            o = oa[:, s]
            out_ref[:, s] = (o * _rms(o) * w_ref[:, s] * sr[:, s]).astype(BF16)

    return pl.pallas_call(
        body, name="gla_out", grid=(R // tr,),
        in_specs=[_row_spec(tr, D), _row_spec(tr, D), _row_spec(tr, D, COL_R // D), _vec_spec(D)],
        out_specs=_row_spec(tr, D), out_shape=jax.ShapeDtypeStruct((R, D), BF16), compiler_params=_cparams(("arbitrary",)),
    )(o2[0], o2[1], parts, gw4)


def _gla_out_bwd(dout, o2, parts, gw4, dparts, tr):
    R = parts.shape[0]

    def body(d_ref, of_ref, ob_ref, r_ref, w_ref, _, do_ref, dr_ref, acc_ref):
        _zero_first(acc_ref)
        oa = _f32(of_ref) + _f32(ob_ref)
        r = _f32(r_ref)
        sr = _silu(r)
        dout = _f32(d_ref)
        for h in range(GLA_H):
            s = slice(h * GLA_DV, (h + 1) * GLA_DV)
            o = oa[:, s]
            rstd = _rms(o)
            n = o * rstd
            w = w_ref[:, s]
            dr_ref[:, s] = (dout[:, s] * n * w * _dsilu(r[:, s])).astype(BF16)
            dnw = dout[:, s] * sr[:, s]
            do_ref[:, s] = _rms_bwd(dnw * w, n, rstd).astype(ACT)
            acc_ref[0:1, s] += _colsum(dnw * n)

    return pl.pallas_call(
        body, name="gla_out_bwd", grid=(R // tr,),
        in_specs=[_row_spec(tr, D), _row_spec(tr, D), _row_spec(tr, D), _row_spec(tr, D, COL_R // D), _vec_spec(D), _ANY],
        out_specs=[_row_spec(tr, D), _dparts_out(tr, D, COL_R), _acc_spec(D)],
        out_shape=[jax.ShapeDtypeStruct((R, D), ACT), jax.ShapeDtypeStruct(dparts.shape, BF16), jax.ShapeDtypeStruct((8, D), F32)],
        input_output_aliases={5: 1}, compiler_params=_cparams(("arbitrary",)),
    )(dout, o2[0], o2[1], parts, gw4, dparts)


SSM_GW = SSM_INNER // SSM_G


def _ssd_out(y2, xbc, parts, dskip, nw, tr):
    R = parts.shape[0]

    def body(yf_ref, yb_ref, x_ref, z_ref, ds_ref, w_ref, out_ref):
        ob = (_f32(yf_ref) + _f32(yb_ref) + ds_ref[...] * _f32(x_ref)) * _silu(_f32(z_ref))
        for g in range(SSM_G):
            s = slice(g * SSM_GW, (g + 1) * SSM_GW)
            o = ob[:, s]
            out_ref[:, s] = (o * _rms(o) * w_ref[:, s]).astype(BF16)

    return pl.pallas_call(
        body, name="ssd_out", grid=(R // tr,),
        in_specs=[_row_spec(tr, SSM_INNER)] * 3 + [_row_spec(tr, SSM_INNER, COL_Z // SSM_INNER),
                                                   _vec_spec(SSM_INNER), _vec_spec(SSM_INNER)],
        out_specs=_row_spec(tr, SSM_INNER), out_shape=jax.ShapeDtypeStruct((R, SSM_INNER), BF16),
        compiler_params=_cparams(("arbitrary",)),
    )(y2[0], y2[1], xbc, parts, dskip, nw)


def _ssd_out_bwd(dout, y2, xbc, parts, dskip, nw, dparts, tr):
    R = parts.shape[0]

    def body(d_ref, yf_ref, yb_ref, x_ref, z_ref, ds_ref, w_ref, _, dy_ref, dz_ref, acc_ref):
        _zero_first(acc_ref)
        x, z = _f32(x_ref), _f32(z_ref)
        pre = _f32(yf_ref) + _f32(yb_ref) + ds_ref[...] * x
        sz = _silu(z)
        ob = pre * sz
        dout = _f32(d_ref)
        for g in range(SSM_G):
            s = slice(g * SSM_GW, (g + 1) * SSM_GW)
            o = ob[:, s]
            rstd = _rms(o)
            n = o * rstd
            dob = _rms_bwd(dout[:, s] * w_ref[:, s], n, rstd)
            dz_ref[:, s] = (dob * pre[:, s] * _dsilu(z[:, s])).astype(BF16)
            dy = dob * sz[:, s]
            dy_ref[:, s] = dy.astype(ACT)
            acc_ref[0:1, s] += _colsum(dout[:, s] * n)
            acc_ref[1:2, s] += _colsum(dy * x[:, s])

    return pl.pallas_call(
        body, name="ssd_out_bwd", grid=(R // tr,),
        in_specs=[_row_spec(tr, SSM_INNER)] * 4 + [_row_spec(tr, SSM_INNER, COL_Z // SSM_INNER),
                                                   _vec_spec(SSM_INNER), _vec_spec(SSM_INNER), _ANY],
        out_specs=[_row_spec(tr, SSM_INNER), _dparts_out(tr, SSM_INNER, COL_Z), _acc_spec(SSM_INNER)],
        out_shape=[jax.ShapeDtypeStruct((R, SSM_INNER), ACT), jax.ShapeDtypeStruct(dparts.shape, BF16),
                   jax.ShapeDtypeStruct((8, SSM_INNER), F32)],
        input_output_aliases={7: 1}, compiler_params=_cparams(("arbitrary",)),
    )(dout, y2[0], y2[1], xbc, parts, dskip, nw, dparts)


CONV_W = SSM_INNER + 2 * SSM_G * SSM_N
CONV_BLK = 1024


CONV_SHIFTS = (-2, -1, 1, 2)


def _conv_mask_table(tr):
    t = np.arange(tr)
    table = np.zeros((2, len(CONV_SHIFTS), tr, 128), np.float32)
    for kind, (pos, seg) in enumerate(((t % GRID_W, GRID_W), (t, tr))):
        for k, s in enumerate(CONV_SHIFTS):
            table[kind, k] = ((pos + s >= 0) & (pos + s < seg)).astype(np.float32)[:, None]
    return jnp.asarray(table)


def _shifted(u, s, mask_ref, tr):
    return u if s == 0 else pltpu.roll(u, (-s) % tr, 0) * mask_ref[0, CONV_SHIFTS.index(s)]


def _conv_mask_spec(tr, n_lat_tiles, row_axis):
    return pl.BlockSpec((1, len(CONV_SHIFTS), tr, 128),
                        lambda *ids: (jnp.where(ids[row_axis] >= n_lat_tiles, 1, 0), 0, 0, 0))


def _conv_fwd(parts, cw, cb, n_lat_tiles, tr):
    R = parts.shape[0]

    n_blk = CONV_W // CONV_BLK

    def body(*refs):
        u_refs, (w_ref, b_ref, mask_ref, o_ref) = refs[:n_blk], refs[n_blk:]
        for k, u_ref in enumerate(u_refs):
            def lanes(l, carry, u_ref=u_ref, k=k):
                sl = pl.ds(pl.multiple_of(l * 128, 128), 128)
                so = pl.ds(pl.multiple_of(k * CONV_BLK + l * 128, 128), 128)
                u, w = u_ref[:, sl].astype(F32), w_ref[:, so]
                acc = jnp.zeros_like(u) + b_ref[:, so]
                for j in range(SSM_CONV):
                    acc = acc + _shifted(u, j - CONV_LEFT, mask_ref, tr) * w[j:j + 1, :]
                o_ref[:, so] = _silu(acc).astype(ACT)
                return carry

            lax.fori_loop(0, CONV_BLK // 128, lanes, 0)

    return pl.pallas_call(
        body, name="conv_fwd", grid=(R // tr,),
        in_specs=[pl.BlockSpec((tr, CONV_BLK), lambda i, k=k: (i, COL_XBC // CONV_BLK + k)) for k in range(n_blk)]
        + [pl.BlockSpec((SSM_CONV, CONV_W), lambda i: (0, 0)), pl.BlockSpec((1, CONV_W), lambda i: (0, 0)),
           _conv_mask_spec(tr, n_lat_tiles, 0)],
        out_specs=pl.BlockSpec((tr, CONV_W), lambda i: (i, 0)), out_shape=jax.ShapeDtypeStruct((R, CONV_W), ACT),
        compiler_params=_cparams(("arbitrary",)),
    )(*([parts] * n_blk), cw, cb, _conv_mask_table(tr))


def _conv_bwd(dx, db, dc, dy, dskip, parts, cw, cb, dparts, n_lat_tiles, tr):
    R = parts.shape[0]
    half = CONV_BLK // 2
    n_x = SSM_INNER // CONV_BLK

    def body(dxf_ref, dxb_ref, dy_ref, ds_ref, dbf_ref, dbb_ref, dcf_ref, dcb_ref, u_ref, w_ref, b_ref, _, mask_ref,
             du_ref, acc_ref, d_scr):
        @pl.when(pl.program_id(1) == 0)
        def _():
            acc_ref[...] = jnp.zeros_like(acc_ref)

        @pl.when(pl.program_id(0) < n_x)
        def _():
            d_scr[...] = dxf_ref[...] + dxb_ref[...] + _f32(dy_ref) * ds_ref[...]

        @pl.when(pl.program_id(0) >= n_x)
        def _():
            d_scr[:, 0:half] = dbf_ref[...] + dbb_ref[...]
            d_scr[:, half:] = dcf_ref[...] + dcb_ref[...]

        def lanes(l, carry):
            sl = pl.ds(pl.multiple_of(l * 128, 128), 128)
            u, w = u_ref[:, sl].astype(F32), w_ref[:, sl]
            pre = jnp.zeros_like(u) + b_ref[:, sl]
            taps = []
            for j in range(SSM_CONV):
                tap = _shifted(u, j - CONV_LEFT, mask_ref, tr)
                taps.append(tap)
                pre = pre + tap * w[j:j + 1, :]
            dpre = d_scr[:, sl] * _dsilu(pre)
            du = jnp.zeros_like(u)
            sums = []
            for j in range(SSM_CONV):
                sums.append(_colsum(dpre * taps[j]))
                du = du + _shifted(dpre, CONV_LEFT - j, mask_ref, tr) * w[j:j + 1, :]
            sums += [_colsum(dpre), jnp.zeros((8 - SSM_CONV - 1, 128), F32)]
            acc_ref[:, sl] += jnp.concatenate(sums, axis=0)
            du_ref[:, sl] = du.astype(BF16)
            return carry

        lax.fori_loop(0, CONV_BLK // 128, lanes, 0)

    return pl.pallas_call(
        body, name="conv_bwd", grid=(CONV_W // CONV_BLK, R // tr),
        in_specs=[pl.BlockSpec((tr, CONV_BLK), lambda j, i: (jnp.where(j < n_x, i, 0), jnp.minimum(j, n_x - 1)))] * 3
        + [pl.BlockSpec((1, CONV_BLK), lambda j, i: (0, jnp.minimum(j, n_x - 1)))]
        + [pl.BlockSpec((tr, half), lambda j, i: (jnp.where(j < n_x, 0, i), 0))] * 4
        + [pl.BlockSpec((tr, CONV_BLK), lambda j, i: (i, COL_XBC // CONV_BLK + j)),
           pl.BlockSpec((SSM_CONV, CONV_BLK), lambda j, i: (0, j)), pl.BlockSpec((1, CONV_BLK), lambda j, i: (0, j)), _ANY,
           _conv_mask_spec(tr, n_lat_tiles, 1)],
        out_specs=[pl.BlockSpec((tr, CONV_BLK), lambda j, i: (i, COL_XBC // CONV_BLK + j)),
                   pl.BlockSpec((8, CONV_BLK), lambda j, i: (0, j))],
        out_shape=[jax.ShapeDtypeStruct(dparts.shape, BF16), jax.ShapeDtypeStruct((8, CONV_W), F32)],
        scratch_shapes=[pltpu.VMEM((tr, CONV_BLK), F32)],
        input_output_aliases={11: 0}, compiler_params=_cparams(("arbitrary", "arbitrary")),
    )(*dx, dy, dskip, *db, *dc, parts, cw, cb, dparts, _conv_mask_table(tr))


def _chunk_row_block(d, i, n_lat, n_ctx):
    fwd = jnp.where(i < n_ctx, n_lat + i, i - n_ctx)
    rev = n_lat + n_ctx - 1 - i
    if isinstance(d, int):
        return rev if d else fwd
    return jnp.where(d == 0, fwd, rev)


def _tri(n, d, transpose=False):
    row = lax.broadcasted_iota(jnp.int32, (n, n), 0)
    col = lax.broadcasted_iota(jnp.int32, (n, n), 1)
    diff = (col - row) if transpose else (row - col)
    return diff * (1 - 2 * d) >= 0


def _gla_gates(sm, uhi, ulo, bias, d):
    pre = _nn3(sm, uhi, ulo) + bias
    g = _log_sigmoid(pre) * (1.0 / GLA_TAU)
    mask = _tri(GLA_C, d)
    b = _x_nn(mask.astype(BF16), g)
    b_tot = _colsum(g)
    b_ref = b[GLA_C // 2:GLA_C // 2 + 1, :]
    e_q = jnp.exp(jnp.minimum(b - b_ref, EXP_CLAMP))
    e_k = jnp.exp(jnp.minimum(b_ref - b, EXP_CLAMP))
    return pre, mask, b_tot, e_q, e_k, jnp.exp(b), jnp.exp(b_tot - b)


GLA_QK = GLA_H * GLA_DK
GLA_V = GLA_H * GLA_DV


def _gla_specs(n_lat, n_ctx, step_of):
    rbs = [lambda i, d=d: _chunk_row_block(d, step_of(i), n_lat, n_ctx) for d in range(2)]
    specs = []
    for rb in rbs:
        specs += [pl.BlockSpec((GLA_C, GLA_QK), lambda i, rb=rb: (rb(i), COL_Q // GLA_QK)),
                  pl.BlockSpec((GLA_C, GLA_QK), lambda i, rb=rb: (rb(i), COL_K // GLA_QK)),
                  pl.BlockSpec((GLA_C, GLA_V), lambda i, rb=rb: (rb(i), COL_V // GLA_V)),
                  pl.BlockSpec((GLA_C, 128), lambda i, rb=rb: (rb(i), 0))]
    specs += [pl.BlockSpec((2, 128, GLA_QK), lambda i: (0, 0, 0)), pl.BlockSpec((2, 128, GLA_QK), lambda i: (0, 0, 0)),
              pl.BlockSpec((2, 1, GLA_QK), lambda i: (0, 0, 0))]
    return specs, rbs


def _gla_fwd(parts, sm, uhi, ulo, bias, n_lat, n_ctx):
    R = parts.shape[0]
    n_steps = n_lat + n_ctx
    scale = GLA_DK ** -0.5

    def body(*refs):
        ins, (uhi_ref, ulo_ref, bias_ref), o_refs, hist_ref, st = refs[:8], refs[8:11], refs[11:13], refs[13], refs[14]

        @pl.when(pl.program_id(0) == 0)
        def _():
            st[...] = jnp.zeros_like(st)

        for d in range(2):
            q_ref, k_ref, v_ref, sm_ref = ins[4 * d:4 * d + 4]
            _, mask, b_tot, e_q, e_k, e_in, e_out = _gla_gates(sm_ref[...], uhi_ref[d], ulo_ref[d], bias_ref[d], d)
            q, k, v = _f32(q_ref) * scale, _f32(k_ref), _bf(v_ref[...])
            qb, kb, q_in, k_out, decay = _bf(q * e_q), _bf(k * e_k), _bf(q * e_in), _bf(k * e_out), jnp.exp(b_tot)
            for h in range(GLA_H):
                sk, sv = slice(h * GLA_DK, (h + 1) * GLA_DK), slice(h * GLA_DV, (h + 1) * GLA_DV)
                att = jnp.where(mask, _nt(qb[:, sk], kb[:, sk]), 0.0)
                s_in = st[d, h]
                hist_ref[d, 0, h] = s_in
                o_refs[d][:, sv] = (_nn(_bf(att), v[:, sv]) + _nt(q_in[:, sk], _bf(s_in))).astype(ACT)
                st[d, h] = decay[:, sk] * s_in + _tn(v[:, sv], k_out[:, sk])

    in_specs, rbs = _gla_specs(n_lat, n_ctx, lambda i: i)
    return pl.pallas_call(
        body, name="gla_fwd", grid=(n_steps,), in_specs=in_specs,
        out_specs=[pl.BlockSpec((GLA_C, GLA_V), lambda i, rb=rb: (rb(i), 0)) for rb in rbs]
        + [pl.BlockSpec((2, 1, GLA_H, GLA_DV, GLA_DK), lambda i: (0, i, 0, 0, 0))],
        out_shape=[jax.ShapeDtypeStruct((R, GLA_V), ACT)] * 2 + [jax.ShapeDtypeStruct((2, n_steps, GLA_H, GLA_DV, GLA_DK), F32)],
        scratch_shapes=[pltpu.VMEM((2, GLA_H, GLA_DV, GLA_DK), F32)],
        compiler_params=_cparams(("arbitrary",)),
    )(*([parts, parts, parts, sm] * 2), uhi, ulo, bias)


def _gla_bwd(do, parts, sm, uhi, ulo, bias, hist, n_lat, n_ctx):
    R = parts.shape[0]
    n_steps = n_lat + n_ctx
    scale = GLA_DK ** -0.5
    step_of = lambda j: n_steps - 1 - j

    def body(*refs):
        ins, (uhi_ref, ulo_ref, bias_ref), do_refs, hist_ref = refs[:8], refs[8:11], refs[11:13], refs[13]
        outs, dst = refs[14:22], refs[22]

        @pl.when(pl.program_id(0) == 0)
        def _():
            dst[...] = jnp.zeros_like(dst)

        for d in range(2):
            q_ref, k_ref, v_ref, sm_ref = ins[4 * d:4 * d + 4]
            dq_ref, dk_ref, dv_ref, dp_ref = outs[4 * d:4 * d + 4]
            pre, mask, b_tot, e_q, e_k, e_in, e_out = _gla_gates(sm_ref[...], uhi_ref[d], ulo_ref[d], bias_ref[d], d)
            q, k, v = _f32(q_ref) * scale, _f32(k_ref), _bf(v_ref[...])
            dout = _bf(do_refs[d][...])
            k_out_f = k * e_out
            qb, kb, q_in, k_out, decay = _bf(q * e_q), _bf(k * e_k), _bf(q * e_in), _bf(k_out_f), jnp.exp(b_tot)
            dqs, dks, dk_outs, dss = [], [], [], []
            for h in range(GLA_H):
                sk, sv = slice(h * GLA_DK, (h + 1) * GLA_DK), slice(h * GLA_DV, (h + 1) * GLA_DV)
                s_in, ds = hist_ref[d, 0, h], dst[d, h]
                att = jnp.where(mask, _nt(qb[:, sk], kb[:, sk]), 0.0)
                datt = _bf(jnp.where(mask, _nt(dout[:, sv], v[:, sv]), 0.0))
                dv_ref[:, sv] = (_tn(_bf(att), dout[:, sv]) + _nt(k_out[:, sk], _bf(ds))).astype(ACT)
                dqs.append(_nn(datt, kb[:, sk]) * e_q[:, sk] + _nn(dout[:, sv], _bf(s_in)) * e_in[:, sk])
                dk_o = _nn(v[:, sv], _bf(ds))
                dk_outs.append(dk_o)
                dks.append(_tn(datt, qb[:, sk]) * e_k[:, sk])
                dss.append(_colsum(ds * s_in))
                dst[d, h] = decay[:, sk] * ds + _tn(dout[:, sv], q_in[:, sk])
            dq, dk_out = jnp.concatenate(dqs, axis=1), jnp.concatenate(dk_outs, axis=1)
            dk = jnp.concatenate(dks, axis=1) + dk_out * e_out
            dq_ref[...] = (dq * scale).astype(ACT)
            dk_ref[...] = dk.astype(ACT)
            db_tot = _colsum(dk_out * k_out_f) + decay * jnp.concatenate(dss, axis=1)
            dg = _x_nn(_tri(GLA_C, d, transpose=True).astype(BF16), dq * q - dk * k) + db_tot
            dp_ref[...] = dg * (1.0 / GLA_TAU) * _sigmoid(-pre)

    in_specs, rbs = _gla_specs(n_lat, n_ctx, step_of)
    in_specs += [pl.BlockSpec((GLA_C, GLA_V), lambda j, rb=rb: (rb(j), 0)) for rb in rbs]
    in_specs += [pl.BlockSpec((2, 1, GLA_H, GLA_DV, GLA_DK), lambda j: (0, step_of(j), 0, 0, 0))]
    out_specs, out_shape = [], []
    for rb in rbs:
        for w, dt in ((GLA_QK, ACT), (GLA_QK, ACT), (GLA_V, ACT), (GLA_QK, F32)):
            out_specs.append(pl.BlockSpec((GLA_C, w), lambda j, rb=rb: (rb(j), 0)))
            out_shape.append(jax.ShapeDtypeStruct((R, w), dt))
    outs = pl.pallas_call(
        body, name="gla_bwd", grid=(n_steps,), in_specs=in_specs, out_specs=out_specs, out_shape=out_shape,
        scratch_shapes=[pltpu.VMEM((2, GLA_H, GLA_DV, GLA_DK), F32)],
        compiler_params=_cparams(("arbitrary",)),
    )(*([parts, parts, parts, sm] * 2), uhi, ulo, bias, do, do, hist)
    return [(outs[k], outs[4 + k]) for k in range(4)]


def _ssd_consts(dt_bias, a_log):
    sel = np.zeros((2, SSM_G, 128, 128), np.float32)
    for d, base in enumerate((SM_DTF, SM_DTB)):
        for g in range(SSM_G):
            for e in range(SSM_HPG):
                sel[d, g, base + SSM_HPG * g + e, e] = 1.0
    e512 = np.zeros((128, SSM_GW), np.float32)
    for e in range(SSM_HPG):
        e512[e, SSM_P * e:SSM_P * (e + 1)] = 1.0
    a_neg = -jnp.exp(a_log)
    pad = lambda v: jnp.pad(v.reshape(2, SSM_G, 1, SSM_HPG), ((0, 0), (0, 0), (0, 0), (0, 128 - SSM_HPG)))
    return dict(
        sel=jnp.asarray(sel, BF16), sel_t=jnp.asarray(sel.transpose(0, 1, 3, 2), BF16), e512_t=jnp.asarray(e512.T, BF16),
        dtb=pad(dt_bias), a=pad(a_neg), a512=jnp.repeat(a_neg, SSM_P, axis=1).reshape(2, SSM_G, 1, SSM_GW))


def _head_columns(x8):
    return [jnp.broadcast_to(x8[:, e:e + 1], (x8.shape[0], 128)) for e in range(SSM_HPG)]


def _head_layout(cols):
    low = lax.broadcasted_iota(jnp.int32, (1, 128), 1) < SSM_P
    return jnp.concatenate([jnp.where(low, cols[2 * j], cols[2 * j + 1]) for j in range(SSM_HPG // 2)], axis=1)


def _ssd_common(sm, sel, dtb, a_neg, a512, d):
    dtr8 = _nn_x(sm, sel) + dtb
    dt8 = _softplus(dtr8)
    a8 = a_neg * dt8
    mask = _tri(SSM_C, d)
    mask_t = _tri(SSM_C, d, transpose=True).astype(BF16)
    cum8 = _x_nn(mask.astype(BF16), a8)
    a_hi, a_lo = _split(a8)
    cum_t = _tn(a_hi, mask_t) + _tn(a_lo, mask_t)
    cum_cols = _head_columns(cum8)
    dt_exp = _head_layout(_head_columns(dt8))
    a_exp = a512 * dt_exp
    return dict(dtr8=dtr8, a8=a8, mask=mask, mask_t=mask_t, cum_t=cum_t, dt_exp=dt_exp, a_exp=a_exp,
                cum_exp=_head_layout(cum_cols), cum_cols=cum_cols, tot_exp=_colsum(a_exp))


def _head_lanes(x, e):
    pair = x[:, 128 * (e // 2):128 * (e // 2 + 1)]
    low = lax.broadcasted_iota(jnp.int32, (1, 128), 1) < SSM_P
    return _bf(jnp.where(low if e % 2 == 0 else jnp.logical_not(low), pair, 0.0))


def _per_head_pairs(fn, x):
    return jnp.concatenate([fn(2 * j, _head_lanes(x, 2 * j)) + fn(2 * j + 1, _head_lanes(x, 2 * j + 1))
                            for j in range(SSM_HPG // 2)], axis=1)


def _ssd_decay(cm, e):
    diff = cm["cum_cols"][e] - cm["cum_t"][e:e + 1, :]
    return jnp.where(cm["mask"], jnp.exp(jnp.minimum(diff, 0.0)), 0.0)


SSM_GPS = 4


def _ssd_specs(n_lat, n_ctx, step_of):
    rbs = [lambda i, d=d: _chunk_row_block(d, step_of(i), n_lat, n_ctx) for d in range(2)]
    xw, nw = SSM_GPS * SSM_GW, SSM_GPS * SSM_N
    specs = []
    for rb in rbs:
        specs += [pl.BlockSpec((SSM_C, xw), lambda g, i, rb=rb: (rb(i), g)),
                  pl.BlockSpec((SSM_C, nw), lambda g, i, rb=rb: (rb(i), SSM_INNER // nw + g)),
                  pl.BlockSpec((SSM_C, nw), lambda g, i, rb=rb: (rb(i), (SSM_INNER + SSM_G * SSM_N) // nw + g)),
                  pl.BlockSpec((SSM_C, 128), lambda g, i, rb=rb: (rb(i), 0))]
    specs += [pl.BlockSpec((2, SSM_GPS, 128, 128), lambda g, i: (0, g, 0, 0)),
              pl.BlockSpec((2, SSM_GPS, 1, 128), lambda g, i: (0, g, 0, 0)),
              pl.BlockSpec((2, SSM_GPS, 1, 128), lambda g, i: (0, g, 0, 0)),
              pl.BlockSpec((2, SSM_GPS, 1, SSM_GW), lambda g, i: (0, g, 0, 0))]
    return specs, rbs


def _ssd_fwd(xbc, sm, k, n_lat, n_ctx, comm):
    R = xbc.shape[0]
    n_steps = n_lat + n_ctx

    def body(*refs):
        ins, (sel_ref, dtb_ref, a_ref, a512_ref), y_refs, hist_ref, st = refs[:8], refs[8:12], refs[12:14], refs[14], refs[15]

        @pl.when(pl.program_id(1) == 0)
        def _():
            st[...] = jnp.zeros_like(st)

        for d in range(2):
            x_ref, b_ref, c_ref, sm_ref = ins[4 * d:4 * d + 4]
            sm = sm_ref[...]
            for gg in range(SSM_GPS):
                sx, sn = slice(gg * SSM_GW, (gg + 1) * SSM_GW), slice(gg * SSM_N, (gg + 1) * SSM_N)
                cm = _ssd_common(sm, sel_ref[d, gg], dtb_ref[d, gg], a_ref[d, gg], a512_ref[d, gg], d)
                bm, cmat = _bf(b_ref[:, sn]), _bf(c_ref[:, sn])
                xdt = x_ref[:, sx].astype(F32) * cm["dt_exp"]
                cb = _nt(cmat, bm)
                ys = _per_head_pairs(lambda e, x_e: _nn(_bf(cb * _ssd_decay(cm, e)), x_e), xdt)
                s_in = st[d, gg]
                hist_ref[d, 0, gg] = s_in
                y = ys + jnp.exp(cm["cum_exp"]) * _nn(cmat, _bf(s_in))
                y_refs[d][:, sx] = y.astype(ACT)
                st[d, gg] = jnp.exp(cm["tot_exp"]) * s_in + _tn(bm, _bf(xdt * jnp.exp(cm["tot_exp"] - cm["cum_exp"])))

    in_specs, rbs = _ssd_specs(n_lat, n_ctx, lambda i: i)
    out_specs = [pl.BlockSpec((SSM_C, SSM_GPS * SSM_GW), lambda g, i, rb=rb: (rb(i), g)) for rb in rbs]
    out_specs += [pl.BlockSpec((2, 1, SSM_GPS, SSM_N, SSM_GW), lambda g, i: (0, i, g, 0, 0))]
    out_shape = [jax.ShapeDtypeStruct((R, SSM_INNER), ACT)] * 2 + [jax.ShapeDtypeStruct((2, n_steps, SSM_G, SSM_N, SSM_GW), F32)]
    args = [xbc, xbc, xbc, sm] * 2 + [k["sel"], k["dtb"], k["a"], k["a512"]]
    n_host_out = len(out_shape)
    outs = pl.pallas_call(
        _hosted(body, len(args), n_host_out, 1, comm, *_ssd_comm_steps(n_steps)), name="ssd_fwd",
        grid=(SSM_G // SSM_GPS, n_steps), in_specs=in_specs + [_ANY] * len(comm.arrays),
        out_specs=out_specs + [_ANY] * len(comm.out_shape), out_shape=out_shape + comm.out_shape,
        scratch_shapes=[pltpu.VMEM((2, SSM_GPS, SSM_N, SSM_GW), F32)] + comm.scratch,
        compiler_params=_cparams(("arbitrary", "arbitrary")),
    )(*args, *comm.arrays)
    return outs[:n_host_out], outs[n_host_out:]


def _ssd_comm_steps(n_steps):
    n_g = SSM_G // SSM_GPS
    at = lambda g, i: (pl.program_id(0) == g) & (pl.program_id(1) == i)
    half = (n_g // 2, n_steps // 2 if n_g % 2 else 0)
    return (lambda: at(0, 0)), (lambda: at(*half)), (lambda: at(n_g - 1, n_steps - 1))


def _ssd_bwd(dy, xbc, sm, k, hist, n_lat, n_ctx, comm):
    R = xbc.shape[0]
    n_steps = n_lat + n_ctx
    step_of = lambda j: n_steps - 1 - j

    def one(d, gg, x_ref, b_ref, c_ref, sm_ref, sel_ref, dtb_ref, a_ref, a512_ref, selt_ref, e512t_ref, dy_ref,
            hist_ref, dx_ref, db_ref, dc_ref, dsm_ref, acc_ref, dst):
        sx, sn = slice(gg * SSM_GW, (gg + 1) * SSM_GW), slice(gg * SSM_N, (gg + 1) * SSM_N)
        a_neg, e512_t = a_ref[d, gg], e512t_ref[...]
        cm = _ssd_common(sm_ref[...], sel_ref[d, gg], dtb_ref[d, gg], a_neg, a512_ref[d, gg], d)
        x, dyv = x_ref[:, sx].astype(F32), dy_ref[:, sx].astype(F32)
        bm, cmat = _bf(b_ref[:, sn]), _bf(c_ref[:, sn])
        xdt = x * cm["dt_exp"]
        cb = _nt(cmat, bm)
        s_in, ds = hist_ref[d, 0, gg], dst[d, gg]
        w = jnp.exp(cm["tot_exp"] - cm["cum_exp"])
        z = _nn(bm, _bf(ds))
        decay_in = jnp.exp(cm["cum_exp"])
        gy = _bf(dyv * decay_in)
        dcb = jnp.zeros((SSM_C, SSM_C), F32)
        crossing = []
        row = lax.broadcasted_iota(jnp.int32, (SSM_C, SSM_C), 0)
        col = lax.broadcasted_iota(jnp.int32, (SSM_C, SSM_C), 1)
        eye = (row == col).astype(BF16)
        before = (cm["mask_t"] - eye)
        xdt_bf = _bf(xdt)

        def head(e, dy_e):
            nonlocal dcb
            lm = _ssd_decay(cm, e)
            m_e = cb * lm
            dm_e = _nt(dy_e, xdt_bf[:, 128 * (e // 2):128 * (e // 2 + 1)])
            dcb = dcb + dm_e * lm
            crossing.append(_bf(dm_e * m_e))
            return _tn(_bf(m_e), dy_e)

        dx_heads = _per_head_pairs(head, dyv)
        through = _nn(jnp.concatenate(crossing, axis=0), before)
        crossing = [_colsum(jnp.where(cm["mask"], through[e * SSM_C:(e + 1) * SSM_C], 0.0)) for e in range(SSM_HPG)]
        da_rows = jnp.concatenate(crossing + [jnp.zeros((128 - SSM_HPG, SSM_C), F32)], axis=0)
        r_hi, r_lo = _split(da_rows)
        da8_intra = _tn(r_hi, eye) + _tn(r_lo, eye)
        dx_state = w * z
        dxdt = dx_heads + dx_state
        dcb = _bf(dcb)
        c_s = _nn(cmat, _bf(s_in))
        dc_ref[:, sn] = _nn(dcb, bm) + _nt(gy, _bf(s_in))
        db_ref[:, sn] = _tn(dcb, cmat) + _nt(_bf(w * xdt), _bf(ds))
        dst[d, gg] = jnp.exp(cm["tot_exp"]) * ds + _tn(cmat, gy)
        state_path = xdt * dx_state
        per_token = _nn_x(jnp.concatenate([dyv * decay_in * c_s - state_path, dxdt * x], axis=0), e512_t)
        totals = jnp.concatenate([_colsum(state_path), _colsum(ds * s_in), jnp.zeros((6, SSM_GW), F32)], axis=0)
        totals = _nn_x(totals, e512_t)
        tot8 = _colsum(cm["a8"])
        dtot8 = totals[0:1] + jnp.exp(tot8) * totals[1:2]
        da8 = da8_intra + _x_nn(cm["mask_t"], per_token[:SSM_C]) + dtot8
        ddt8 = da8 * a_neg + per_token[SSM_C:]
        dsm_ref[gg] = _nn_x(ddt8 * _sigmoid(cm["dtr8"]), selt_ref[d, gg])
        dx_ref[:, sx] = dxdt * cm["dt_exp"]
        acc_ref[d, gg, 0:1, :] += _colsum(da8 * cm["a8"])

    def body(*refs):
        ins, consts, (selt_ref, e512t_ref), dy_refs, hist_ref = refs[:8], refs[8:12], refs[12:14], refs[14:16], refs[16]
        outs, acc_ref, dst = refs[17:25], refs[25], refs[26]

        @pl.when(pl.program_id(1) == 0)
        def _():
            dst[...] = jnp.zeros_like(dst)
            acc_ref[...] = jnp.zeros_like(acc_ref)

        for d in range(2):
            for gg in range(SSM_GPS):
                one(d, gg, *ins[4 * d:4 * d + 4], *consts, selt_ref, e512t_ref, dy_refs[d], hist_ref,
                    *outs[4 * d:4 * d + 4], acc_ref, dst)

    xw, nw = SSM_GPS * SSM_GW, SSM_GPS * SSM_N
    in_specs, rbs = _ssd_specs(n_lat, n_ctx, step_of)
    in_specs += [pl.BlockSpec((2, SSM_GPS, 128, 128), lambda g, j: (0, g, 0, 0)), pl.BlockSpec((SSM_GW, 128), lambda g, j: (0, 0))]
    in_specs += [pl.BlockSpec((SSM_C, xw), lambda g, j, rb=rb: (rb(j), g)) for rb in rbs]
    in_specs += [pl.BlockSpec((2, 1, SSM_GPS, SSM_N, SSM_GW), lambda g, j: (0, step_of(j), g, 0, 0))]
    out_specs, out_shape = [], []
    for rb in rbs:
        out_specs += [pl.BlockSpec((SSM_C, xw), lambda g, j, rb=rb: (rb(j), g)),
                      pl.BlockSpec((SSM_C, nw), lambda g, j, rb=rb: (rb(j), g)),
                      pl.BlockSpec((SSM_C, nw), lambda g, j, rb=rb: (rb(j), g)),
                      pl.BlockSpec((SSM_GPS, SSM_C, 128), lambda g, j, rb=rb: (g, rb(j), 0))]
        out_shape += [jax.ShapeDtypeStruct((R, SSM_INNER), F32), jax.ShapeDtypeStruct((R, SSM_G * SSM_N), F32),
                      jax.ShapeDtypeStruct((R, SSM_G * SSM_N), F32), jax.ShapeDtypeStruct((SSM_G, R, 128), F32)]
    out_specs.append(pl.BlockSpec((2, SSM_GPS, 8, 128), lambda g, j: (0, g, 0, 0)))
    out_shape.append(jax.ShapeDtypeStruct((2, SSM_G, 8, 128), F32))
    args = [xbc, xbc, xbc, sm] * 2 + [k["sel"], k["dtb"], k["a"], k["a512"], k["sel_t"], k["e512_t"], dy, dy, hist]
    n_host_out = len(out_shape)
    outs = pl.pallas_call(
        _hosted(body, len(args), n_host_out, 1, comm, *_ssd_comm_steps(n_steps)), name="ssd_bwd",
        grid=(SSM_G // SSM_GPS, n_steps), in_specs=in_specs + [_ANY] * len(comm.arrays),
        out_specs=out_specs + [_ANY] * len(comm.out_shape), out_shape=out_shape + comm.out_shape,
        scratch_shapes=[pltpu.VMEM((2, SSM_GPS, SSM_N, SSM_GW), F32)] + comm.scratch,
        compiler_params=_cparams(("arbitrary", "arbitrary")),
    )(*args, *comm.arrays)
    return [(outs[n], outs[4 + n]) for n in range(4)] + [outs[8]], outs[n_host_out:]


def _gla_assemble(dq, dk, dv, dparts, tr):
    R = dq[0].shape[0]
    qk = GLA_H * GLA_DK

    def body(dqf_ref, dqb_ref, dkf_ref, dkb_ref, dvf_ref, dvb_ref, _, o_ref):
        o_ref[:, 0:qk] = (_f32(dqf_ref) + _f32(dqb_ref)).astype(BF16)
        o_ref[:, qk:2 * qk] = (_f32(dkf_ref) + _f32(dkb_ref)).astype(BF16)
        o_ref[:, 2 * qk:] = (_f32(dvf_ref) + _f32(dvb_ref)).astype(BF16)

    return pl.pallas_call(
        body, name="gla_assemble", grid=(R // tr,), in_specs=[_row_spec(tr, qk)] * 4 + [_row_spec(tr, D)] * 2 + [_ANY],
        out_specs=_dparts_out(tr, 2 * D, COL_Q), out_shape=jax.ShapeDtypeStruct(dparts.shape, BF16),
        input_output_aliases={6: 0}, compiler_params=_cparams(("arbitrary",)),
    )(*dq, *dk, *dv, dparts)


def _small_assemble(dp, dsm, sm, ut_hi, ut_lo, dparts, tr):
    R = sm.shape[0]
    qk = GLA_H * GLA_DK

    def body(dpf_ref, dpb_ref, dsmf_ref, dsmb_ref, sm_ref, uth_ref, utl_ref, _, o_ref, dup_ref, acc_ref, acc2_ref):
        @pl.when(pl.program_id(0) == 0)
        def _():
            dup_ref[...] = jnp.zeros_like(dup_ref)
            acc_ref[...] = jnp.zeros_like(acc_ref)
            acc2_ref[...] = jnp.zeros_like(acc2_ref)

        ssd = dsmf_ref[0] + dsmb_ref[0]
        for g in range(1, SSM_G):
            ssd = ssd + (dsmf_ref[g] + dsmb_ref[g])
        acc2_ref[0:1, :] += _colsum(ssd)
        sm_hi, sm_lo = _split(sm_ref[...])
        out = ssd
        for d, dp_ref in enumerate((dpf_ref, dpb_ref)):
            dpd = dp_ref[...]
            out = out + _nn3(dpd, uth_ref[d], utl_ref[d])
            p_hi, p_lo = _split(dpd)
            dup_ref[d] += _tn(sm_hi, p_hi) + _tn(sm_lo, p_hi) + _tn(sm_hi, p_lo)
            acc_ref[d:d + 1, :] += _colsum(dpd)
        o_ref[...] = out.astype(BF16)

    return pl.pallas_call(
        body, name="small_assemble", grid=(R // tr,),
        in_specs=[_row_spec(tr, qk)] * 2 + [pl.BlockSpec((SSM_G, tr, 128), lambda i: (0, i, 0))] * 2
        + [_row_spec(tr, 128), pl.BlockSpec((2, qk, 128), lambda i: (0, 0, 0)),
           pl.BlockSpec((2, qk, 128), lambda i: (0, 0, 0)), _ANY],
        out_specs=[_dparts_out(tr, 128, COL_SM), pl.BlockSpec((2, 128, qk), lambda i: (0, 0, 0)), _acc_spec(qk), _acc_spec(128)],
        out_shape=[jax.ShapeDtypeStruct(dparts.shape, BF16), jax.ShapeDtypeStruct((2, 128, qk), F32),
                   jax.ShapeDtypeStruct((8, qk), F32), jax.ShapeDtypeStruct((8, 128), F32)],
        input_output_aliases={7: 0}, compiler_params=_cparams(("arbitrary",)),
    )(*dp, *dsm, sm, ut_hi, ut_lo, dparts)


ADA_ROWS = 16
ADA_TILE = 512


def _dot3_f32(a, b, ca, cb):
    a_hi, a_lo = _split(a)
    b_hi, b_lo = _split(b)
    return _dg(a_hi, b_hi, ca, cb) + _dg(a_lo, b_hi, ca, cb) + _dg(a_hi, b_lo, ca, cb)


def _ada_fwd(cvec, w, b):
    cols = w.shape[1]

    def body(c_ref, w_ref, b_ref, o_ref):
        o_ref[...] = _dot3_f32(_silu(c_ref[...]), w_ref[...], 1, 0) + b_ref[...]

    return pl.pallas_call(
        body, name="ada_fwd", grid=(cols // ADA_TILE,),
        in_specs=[pl.BlockSpec((ADA_ROWS, D), lambda j: (0, 0)), pl.BlockSpec((D, ADA_TILE), lambda j: (0, j)),
                  pl.BlockSpec((1, ADA_TILE), lambda j: (0, j))],
        out_specs=pl.BlockSpec((ADA_ROWS, ADA_TILE), lambda j: (0, j)), out_shape=jax.ShapeDtypeStruct((ADA_ROWS, cols), F32),
        compiler_params=_cparams(("arbitrary",)),
    )(cvec, w, b)


def _adam(w, g, m, v):
    m2 = ADAM_B1 * m + (1.0 - ADAM_B1) * g
    v2 = ADAM_B2 * v + (1.0 - ADAM_B2) * (g * g)
    m_hat = m2 / (1.0 - ADAM_B1 ** ADAM_STEP)
    v_hat = v2 / (1.0 - ADAM_B2 ** ADAM_STEP)
    return -ADAM_LR * (m_hat / (jnp.sqrt(v_hat) + ADAM_EPS) + ADAM_WD * w), m2, v2


def _wada_bwd_adam(cvec, dada, w, m, v):
    rows, cols = w.shape
    tr = _tile(rows, 256, 128)

    def body(c_ref, d_ref, w_ref, m_ref, v_ref, g_ref, dl_ref, m2_ref, v2_ref, p_ref):
        wv = w_ref[...]
        g = _dot3_f32(_silu(c_ref[...]), d_ref[...], 0, 0)
        g_ref[...] = g
        dl_ref[...], m2_ref[...], v2_ref[...] = _adam(wv, g, m_ref[...], v_ref[...])
        p_ref[...] = _dot3_f32(d_ref[...], wv, 1, 1)

    blk = pl.BlockSpec((tr, cols), lambda i: (i, 0))
    return pl.pallas_call(
        body, name="wada_bwd_adam", grid=(rows // tr,),
        in_specs=[pl.BlockSpec((ADA_ROWS, tr), lambda i: (0, i)), pl.BlockSpec((ADA_ROWS, cols), lambda i: (0, 0)), blk, blk, blk],
        out_specs=[blk, blk, blk, blk, pl.BlockSpec((ADA_ROWS, tr), lambda i: (0, i))],
        out_shape=[jax.ShapeDtypeStruct((rows, cols), F32)] * 4 + [jax.ShapeDtypeStruct((ADA_ROWS, rows), F32)],
        compiler_params=_cparams(("arbitrary",)),
    )(cvec, dada, w, m, v)


def _reduce_adam(parts8, w, m, v, name):
    rows, cols = w.shape
    tr = _tile(rows, 64, 16)

    def body(p_ref, w_ref, m_ref, v_ref, g_ref, dl_ref, m2_ref, v2_ref):
        g = p_ref[0].astype(F32) + p_ref[N_CHIPS].astype(F32)
        for j in range(1, N_CHIPS):
            g = g + (p_ref[j].astype(F32) + p_ref[N_CHIPS + j].astype(F32))
        g_ref[...] = g
        dl_ref[...], m2_ref[...], v2_ref[...] = _adam(w_ref[...], g, m_ref[...], v_ref[...])

    blk = pl.BlockSpec((tr, cols), lambda i: (i, 0))
    return pl.pallas_call(
        body, name=name, grid=(rows // tr,), in_specs=[pl.BlockSpec((N_DEV, tr, cols), lambda i: (0, i, 0)), blk, blk, blk],
        out_specs=[blk] * 4, out_shape=[jax.ShapeDtypeStruct((rows, cols), F32)] * 4, compiler_params=_cparams(("arbitrary",)),
    )(parts8, w, m, v)


SMALL_W = 1024


def _sum8(g8):
    rows = g8.shape[1]

    def body(g_ref, o_ref):
        s = g_ref[0]
        for j in range(1, N_DEV):
            s = s + g_ref[j]
        o_ref[...] = s

    return pl.pallas_call(
        body, name="sum8", out_shape=jax.ShapeDtypeStruct((rows, SMALL_W), F32),
        in_specs=[pl.BlockSpec(memory_space=pltpu.VMEM)], out_specs=pl.BlockSpec(memory_space=pltpu.VMEM),
        compiler_params=pltpu.CompilerParams(vmem_limit_bytes=VMEM_LIMIT),
    )(g8)


def _cctx_grad(p8, c_ctx):
    def body(p_ref, c_ref, o_ref):
        s = p_ref[0]
        for chip in range(1, N_CHIPS):
            s = s + p_ref[2 * chip]
        o_ref[...] = s * _dsilu(c_ref[...])

    return pl.pallas_call(
        body, name="cctx_grad", out_shape=jax.ShapeDtypeStruct((1, D), F32),
        in_specs=[pl.BlockSpec(memory_space=pltpu.VMEM)] * 2, out_specs=pl.BlockSpec(memory_space=pltpu.VMEM),
    )(p8, c_ctx)


def _adam_small(w, g, m, v):
    def body(w_ref, g_ref, m_ref, v_ref, dl_ref, m2_ref, v2_ref):
        dl_ref[...], m2_ref[...], v2_ref[...] = _adam(w_ref[...], g_ref[...], m_ref[...], v_ref[...])

    vm = pl.BlockSpec(memory_space=pltpu.VMEM)
    return pl.pallas_call(
        body, name="adam_small", out_shape=[jax.ShapeDtypeStruct(w.shape, F32)] * 3, in_specs=[vm] * 4, out_specs=[vm] * 3,
        compiler_params=pltpu.CompilerParams(vmem_limit_bytes=VMEM_LIMIT),
    )(w, g, m, v)


def _pack(vecs, width=SMALL_W, row_mult=8):
    flat = jnp.concatenate([v.reshape(-1).astype(F32) for v in vecs])
    n = flat.shape[0]
    rows = -(-n // (width * row_mult)) * row_mult
    return jnp.pad(flat, (0, rows * width - n)).reshape(rows, width)


def _unpack(packed, shapes):
    flat = packed.reshape(-1)
    out, off = [], 0
    for s in shapes:
        n = int(np.prod(s))
        out.append(flat[off:off + n].reshape(s))
        off += n
    return out


WEIGHTS = ('c_ctx', 'w_ada', 'b_ada', 'norm1_w', 'w_in', 'gla_up_f', 'gla_bias_f', 'gla_up_b', 'gla_bias_b', 'gla_norm_w',
           'conv_w', 'conv_b', 'dt_bias_f', 'dt_bias_b', 'a_log_f', 'a_log_b', 'd_skip', 'ssm_norm_w', 'w_pa', 'w_pb', 'w_out',
           'norm2_w', 'w_gate', 'w_up', 'w_down', 'final_norm_w')
BIG = ('w_in', 'w_pa', 'w_pb', 'w_out', 'w_gate', 'w_up', 'w_down')
COL_SHARDED = ('w_in', 'w_gate', 'w_up')
SMALL_SHARDED = ('gla_up_f', 'gla_up_b', 'conv_w')
ROW_TILE = 256
MM_ROWS = 1408
MM_ROWS_SMALL = 768
MM_KROWS = 2816


def _blocks_to_full(g4, name):
    n, r, c = g4.shape
    return g4.transpose(1, 0, 2).reshape(r, n * c) if name in COL_SHARDED else g4.reshape(n * r, c)


def _full_to_blocks(full, name):
    r, c = full.shape
    if name in COL_SHARDED:
        return full.reshape(r, N_CHIPS, c // N_CHIPS).transpose(1, 0, 2)
    return full.reshape(N_CHIPS, r // N_CHIPS, c)


def _permute_in(w_in_full):
    off = np.concatenate([[0], np.cumsum(IN_WIDTHS)])
    cols = [w_in_full[:, off[p]:off[p + 1]] for p in PERM]
    return jnp.concatenate(cols + [jnp.zeros((w_in_full.shape[0], SMALL_PAD), w_in_full.dtype)], axis=1)


def _unpermute_in(wp):
    off = np.concatenate([[0], np.cumsum([IN_WIDTHS[p] for p in PERM])])
    pieces = {p: wp[:, off[i]:off[i + 1]] for i, p in enumerate(PERM)}
    return jnp.concatenate([pieces[p] for p in range(len(IN_WIDTHS))], axis=1)


def _chip_cols(full, chip, n):
    return lax.dynamic_slice_in_dim(full, chip * n, n, axis=1)


def kernel(x, c, ctx, c_ctx, w_ada, b_ada, norm1_w, w_in, gla_up_f, gla_bias_f, gla_up_b, gla_bias_b, gla_norm_w, conv_w, conv_b, dt_bias_f, dt_bias_b, a_log_f, a_log_b, d_skip, ssm_norm_w, w_pa, w_pb, w_out, norm2_w, w_gate, w_up, w_down, final_norm_w, loss_target, m_c_ctx, m_w_ada, m_b_ada, m_norm1_w, m_w_in, m_gla_up_f, m_gla_bias_f, m_gla_up_b, m_gla_bias_b, m_gla_norm_w, m_conv_w, m_conv_b, m_dt_bias_f, m_dt_bias_b, m_a_log_f, m_a_log_b, m_d_skip, m_ssm_norm_w, m_w_pa, m_w_pb, m_w_out, m_norm2_w, m_w_gate, m_w_up, m_w_down, m_final_norm_w, v_c_ctx, v_w_ada, v_b_ada, v_norm1_w, v_w_in, v_gla_up_f, v_gla_bias_f, v_gla_up_b, v_gla_bias_b, v_gla_norm_w, v_conv_w, v_conv_b, v_dt_bias_f, v_dt_bias_b, v_a_log_f, v_a_log_b, v_d_skip, v_ssm_norm_w, v_w_pa, v_w_pb, v_w_out, v_norm2_w, v_w_gate, v_w_up, v_w_down, v_final_norm_w):
    given = dict(locals())
    W = {n: given[n] for n in WEIGHTS}
    M = {n: given["m_" + n] for n in WEIGHTS}
    V = {n: given["v_" + n] for n in WEIGHTS}
    L, Lc = x.shape[1], ctx.shape[1]
    tr = ROW_TILE
    assert L % tr == 0 and Lc % tr == 0 and L % Lc == 0 and Lc % SSM_C == 0
    n_lat_tiles = L // tr
    xi, yi, ci = _place()
    chip, me = 2 * xi + yi, 4 * xi + 2 * yi + ci
    x2, ctx2 = x[0], ctx[0]

    g0 = _allgather_small(_pack([c[0]] + [W[n][0] for n in SMALL_SHARDED]), "gather_c")
    g0 = g0.reshape(N_DEV, -1)
    c_all = g0[:, :D]
    small_full, off = {}, D
    for n in SMALL_SHARDED:
        r, cols = W[n].shape[1:]
        small_full[n] = jnp.concatenate([g0[2 * k, off:off + r * cols].reshape(r, cols) for k in range(N_CHIPS)], axis=1)
        off += r * cols
    up_f, up_b, conv_w_full = (small_full[n] for n in SMALL_SHARDED)

    cvec = jnp.zeros((ADA_ROWS, D), F32).at[:N_DEV].set(c_all).at[N_DEV].set(c_ctx)
    ada_cols = w_ada.shape[2]
    ada_part = _ada_fwd(cvec, w_ada[0], _chip_cols(b_ada, chip, ada_cols))
    g1_all = _allgather_small(ada_part, "gather_ada")
    ada_full = jnp.concatenate([g1_all[2 * k] for k in range(N_CHIPS)], axis=1)
    mine = lax.dynamic_slice_in_dim(ada_full, me, 1, axis=0)
    sh1, sc1, g1, sh2, sc2, g2 = (mine[:, k * D:(k + 1) * D] for k in range(6))
    csh1, csc1 = ada_full[N_DEV:N_DEV + 1, :D], ada_full[N_DEV:N_DEV + 1, D:2 * D]
    mod = jnp.stack([jnp.stack([sh1, sc1]), jnp.stack([csh1, csc1])])

    later = [n for n in BIG if n != 'w_in']

    def lr_rows(up, base):
        return jnp.zeros((128, GLA_H * GLA_DK), F32).at[base:base + GLA_RANK].set(up)
    u2 = jnp.stack([lr_rows(up_f, SM_LRF), lr_rows(up_b, SM_LRB)])
    u2_hi = u2.astype(BF16)
    u2_lo = (u2 - u2_hi.astype(F32)).astype(BF16)
    ut = u2.transpose(0, 2, 1)
    ut_hi = ut.astype(BF16)
    ut_lo = (ut - ut_hi.astype(F32)).astype(BF16)
    gbias = jnp.stack([gla_bias_f, gla_bias_b])
    kc = _ssd_consts(jnp.stack([dt_bias_f[0], dt_bias_b[0]]), jnp.stack([a_log_f[0], a_log_b[0]]))
    gw4 = jnp.tile(gla_norm_w, (1, GLA_H))
    dskip_exp = jnp.repeat(d_skip, SSM_P, axis=1)
    n_gla = (L // GLA_C, Lc // GLA_C)
    n_ssd = (L // SSM_C, Lc // SSM_C)

    h1, (w_in_blocks,) = _norm_mod(x2, ctx2, norm1_w, mod, n_lat_tiles, tr, _gather_split_comm(w_in[0].astype(BF16)))
    full = {'w_in': _blocks_to_full(w_in_blocks, 'w_in')}
    wp = _permute_in(full['w_in'])
    parts = _mm(h1, wp, "nn", ACT, "mm_in", tm=MM_ROWS, tn=PW // 3)
    sm = _mm(h1, wp[:, COL_SM:], "nn", F32, "mm_in_small", tm=MM_ROWS)
    xbc = _conv_fwd(parts, conv_w_full, conv_b, L // Lc, Lc)
    *o2, gla_hist = _gla_fwd(parts, sm, u2_hi, u2_lo, gbias, *n_gla)
    (*y2, ssd_hist), gathered = _ssd_fwd(xbc, sm, kc, *n_ssd, _gather_comm([W[n][0].astype(BF16) for n in later]))
    full.update({n: _blocks_to_full(g, n) for n, g in zip(later, gathered)})
    oan = _gla_out(o2, parts, gw4, tr)
    obn = _ssd_out(y2, xbc, parts, dskip_exp, ssm_norm_w, tr)
    ya = _mm(oan, full['w_pa'], "nn", ACT, "mm_pa", tm=MM_ROWS)
    yb = _mm(obn, full['w_pb'], "nn", ACT, "mm_pb", tm=MM_ROWS)
    merged = _merge(ya, yb, parts, tr)
    mix = _mm(merged, full['w_out'], "nn", ACT, "mm_out", tm=MM_ROWS)
    h2, u = _resid_norm_mod(x2, ctx2, mix, g1, norm2_w, sh2, sc2, n_lat_tiles, tr)
    gp = _mm(u, full['w_gate'], "nn", ACT, "mm_gate", tm=MM_ROWS, tn=D_FF)
    up = _mm(u, full['w_up'], "nn", ACT, "mm_up", tm=MM_ROWS, tn=D_FF)
    f = _mm(gp, full['w_down'], "nn", ACT, "mm_down", tm=MM_ROWS_SMALL, tk=D_FF // 2, swiglu_up=up)
    dh3, df, acc_loss = _loss_head(h2, f, loss_target[0], g2, final_norm_w[None], n_lat_tiles, tr)

    dw = {}
    da = _mm(df, full['w_down'], "nt", ACT, "mm_down_dx", tm=MM_ROWS, tn=D_FF)
    dw['w_down'] = _mm(gp, df, "tn", BF16, "mm_down_dw", tm=D_FF // 2, tk=MM_ROWS, swiglu_up=up)
    dgp, dup = _swiglu_act_bwd(da, gp, up, tr)
    du_a = _mm(dgp, full['w_gate'], "nt", ACT, "mm_gate_dx", tm=MM_ROWS, tk=D_FF)
    du_b = _mm(dup, full['w_up'], "nt", ACT, "mm_up_dx", tm=MM_ROWS, tk=D_FF)
    dw['w_gate'] = _mm(u, dgp, "tn", BF16, "mm_gate_dw", tm=D, tn=D_FF // 2, tk=MM_KROWS)
    dw['w_up'] = _mm(u, dup, "tn", BF16, "mm_up_dw", tm=D, tn=D_FF // 2, tk=MM_KROWS)
    dh2, dmix, acc_ffn = _ffn_in_bwd(du_a, du_b, h2, dh3, mix, sc2, g1, norm2_w, tr)
    dmerged = _mm(dmix, full['w_out'], "nt", ACT, "mm_out_dx", tm=MM_ROWS)
    dw['w_out'] = _mm(merged, dmix, "tn", BF16, "mm_out_dw", tm=D, tk=MM_KROWS)
    dya, dyb, dparts = _merge_bwd(dmerged, ya, yb, parts, lax.empty((L + Lc, PW), BF16), tr)
    doan = _mm(dya, full['w_pa'], "nt", ACT, "mm_pa_dx", tm=MM_ROWS)
    dw['w_pa'] = _mm(oan, dya, "tn", BF16, "mm_pa_dw", tm=D, tk=MM_KROWS)
    dobn = _mm(dyb, full['w_pb'], "nt", ACT, "mm_pb_dx", tm=MM_ROWS)
    dw['w_pb'] = _mm(obn, dyb, "tn", BF16, "mm_pb_dw", tm=D, tk=MM_KROWS)
    do, dparts, acc_gla = _gla_out_bwd(doan, o2, parts, gw4, dparts, tr)
    dq, dk, dv, dpre = _gla_bwd(do, parts, sm, u2_hi, u2_lo, gbias, gla_hist, *n_gla)
    dy, dparts, acc_ssd = _ssd_out_bwd(dobn, y2, xbc, parts, dskip_exp, ssm_norm_w, dparts, tr)
    (dx_scan, db_scan, dc_scan, dsm, acc_alog), exchanged = _ssd_bwd(
        dy, xbc, sm, kc, ssd_hist, *n_ssd, _exchange_comm([_full_to_blocks(dw[n], n) for n in later]))
    exchanged = dict(zip(later, exchanged))
    dparts, acc_conv = _conv_bwd(dx_scan, db_scan, dc_scan, dy, dskip_exp, parts, conv_w_full, conv_b, dparts, L // Lc, Lc)
    dparts = _gla_assemble(dq, dk, dv, dparts, tr)
    dparts, dup_gla, acc_gbias, acc_dtb = _small_assemble(dpre, dsm, sm, ut_hi, ut_lo, dparts, tr)
    dw['w_in'] = _unpermute_in(_mm(h1, dparts, "tn", BF16, "mm_in_dw", tm=D, tn=PW // 9, tk=MM_KROWS))
    dh1, (exchanged['w_in'],) = _mm(dparts, wp, "nt", F32, "mm_in_dx", tm=MM_ROWS_SMALL, tk=PW // 3,
                                    comm=_exchange_comm([_full_to_blocks(dw['w_in'], 'w_in')]))
    dx, acc_n1 = _norm1_bwd(dh1, x2, ctx2, dh2, norm1_w, mod, n_lat_tiles, tr)

    partial = dict(
        norm1_w=acc_n1[0, 2] + acc_n1[1, 2],
        gla_up_f=dup_gla[0, SM_LRF:SM_LRF + GLA_RANK], gla_bias_f=acc_gbias[0],
        gla_up_b=dup_gla[1, SM_LRB:SM_LRB + GLA_RANK], gla_bias_b=acc_gbias[1],
        gla_norm_w=acc_gla[0].reshape(GLA_H, GLA_DV).sum(0),
        conv_w=acc_conv[:SSM_CONV], conv_b=acc_conv[SSM_CONV],
        dt_bias_f=acc_dtb[0, SM_DTF:SM_DTF + SSM_HEADS], dt_bias_b=acc_dtb[0, SM_DTB:SM_DTB + SSM_HEADS],
        a_log_f=acc_alog[0, :, 0, :SSM_HPG], a_log_b=acc_alog[1, :, 0, :SSM_HPG],
        d_skip=acc_ssd[1].reshape(SSM_HEADS, SSM_P).sum(1), ssm_norm_w=acc_ssd[0],
        norm2_w=acc_ffn[2], final_norm_w=acc_loss[0],
    )
    dada = jnp.concatenate([acc_n1[0, 1], acc_n1[0, 0], acc_ffn[3], acc_ffn[1], acc_ffn[0], acc_loss[1]])
    dada_ctx = jnp.concatenate([acc_n1[1, 1], acc_n1[1, 0], jnp.zeros((4 * D,), F32)])
    names = list(partial)
    payload = [partial[n] for n in names] + [dada + dada_ctx, dada_ctx, acc_loss[2], dada]
    sizes = [int(np.prod(p.shape)) for p in payload]
    g8 = _allgather_small(_pack(payload), "gather_small_grads")
    summed = _unpack(_sum8(g8), [(s,) for s in sizes])
    grads = {n: s.reshape(W[n].shape if n not in SMALL_SHARDED else partial[n].shape) for n, s in zip(names, summed)}
    grads['b_ada'] = summed[len(names)].reshape(b_ada.shape)
    dada_ctx_sum = summed[len(names) + 1]
    loss = jnp.sum(summed[len(names) + 2])
    dada_all = g8.reshape(N_DEV, -1)[:, sum(sizes[:-1]):sum(sizes)]

    dada16 = jnp.zeros((ADA_ROWS, ada_cols), F32)
    dada16 = dada16.at[:N_DEV].set(_chip_cols(dada_all, chip, ada_cols)).at[N_DEV].set(_chip_cols(dada_ctx_sum[None], chip, ada_cols)[0])
    g_wada, dl_wada, m_wada, v_wada, p16 = _wada_bwd_adam(cvec, dada16, w_ada[0], m_w_ada[0], v_w_ada[0])
    p8 = _allgather_small(p16[N_DEV:], "gather_cctx")
    grads['c_ctx'] = _cctx_grad(p8[:, 0:1, :], c_ctx[None])[0]
    for n in SMALL_SHARDED:
        grads[n] = _chip_cols(grads[n], chip, W[n].shape[2])[None]

    small = [n for n in WEIGHTS if n not in BIG and n != 'w_ada']
    shapes = [W[n].shape for n in small]
    dl_s, m_s, v_s = _adam_small(*[_pack([d[n] for n in small]) for d in (W, grads, M, V)])
    delta = dict(zip(small, _unpack(dl_s, shapes)))
    new_m = dict(zip(small, _unpack(m_s, shapes)))
    new_v = dict(zip(small, _unpack(v_s, shapes)))
    grads['w_ada'], delta['w_ada'], new_m['w_ada'], new_v['w_ada'] = g_wada[None], dl_wada[None], m_wada[None], v_wada[None]

    for n in BIG:
        g, dl, m2, v2 = _reduce_adam(exchanged[n], W[n][0], M[n][0], V[n][0], "adam_" + n)
        grads[n], delta[n], new_m[n], new_v[n] = g[None], dl[None], m2[None], v2[None]

    return (loss, dx[None], *[grads[n] for n in WEIGHTS], *[delta[n] for n in WEIGHTS],
            *[new_m[n] for n in WEIGHTS], *[new_v[n] for n in WEIGHTS])
```

```python
import numpy as np
import jax
import jax.numpy as jnp
from jax import lax
from jax.experimental import pallas as pl
from jax.experimental.pallas import tpu as pltpu

F32 = jnp.float32
BF16 = jnp.bfloat16
ACT = BF16
MESH = pl.DeviceIdType.MESH

D = 1024
EPS = 1e-6
GRID_W = 64
GLA_H, GLA_DK, GLA_DV, GLA_RANK, GLA_TAU = 4, 128, 256, 16, 16.0
GLA_C = 128
SSM_INNER, SSM_P, SSM_HEADS, SSM_G, SSM_HPG, SSM_N = 2048, 64, 32, 4, 8, 128
SSM_C = 128
SSM_CONV, CONV_LEFT = 4, 2
D_FF = 2816
IN_WIDTHS = (512, 512, 1024, 1024, 16, 16, 2048, 2048, 512, 512, 32, 32, 1024, 1024)
D_IN = sum(IN_WIDTHS)
PERM = (7, 8, 9, 3, 6, 0, 1, 2, 12, 13, 4, 5, 10, 11)
PW = 10368
SMALL_PAD = PW - D_IN
COL_XBC, COL_R, COL_Z, COL_Q, COL_K, COL_V, COL_GA, COL_GB, COL_SM = 0, 3072, 4096, 6144, 6656, 7168, 8192, 9216, 10240
SM_LRF, SM_LRB, SM_DTF, SM_DTB = 0, 16, 32, 64
EXP_CLAMP = 80.0
ADAM_LR, ADAM_B1, ADAM_B2, ADAM_EPS, ADAM_WD, ADAM_STEP = 0.001, 0.9, 0.999, 1e-08, 0.01, 10
N_CHIPS, N_DEV = 4, 8
VMEM_LIMIT = 56 * 1024 * 1024


def _cparams(sem=None):
    return pltpu.CompilerParams(dimension_semantics=sem, vmem_limit_bytes=VMEM_LIMIT)


def _dg(a, b, ca, cb):
    return lax.dot_general(a, b, (((ca,), (cb,)), ((), ())), preferred_element_type=F32)


def _nn(a, b):
    return _dg(a, b, 1, 0)


def _nt(a, b):
    return _dg(a, b, 1, 1)


def _tn(a, b):
    return _dg(a, b, 0, 0)


def _bf(x):
    return x.astype(BF16)


def _f32(ref):
    return ref[...].astype(F32)


def _split(x):
    hi = x.astype(BF16)
    return hi, (x - hi.astype(F32)).astype(BF16)


def _nn_x(a, b_exact):
    hi, lo = _split(a)
    return _nn(hi, b_exact) + _nn(lo, b_exact)


def _x_nn(a_exact, b):
    hi, lo = _split(b)
    return _nn(a_exact, hi) + _nn(a_exact, lo)


def _nn3(a, b_hi, b_lo):
    hi, lo = _split(a)
    return _nn(hi, b_hi) + _nn(lo, b_hi) + _nn(hi, b_lo)


def _sigmoid(x):
    return 1.0 / (1.0 + jnp.exp(-x))


def _silu(x):
    return x * _sigmoid(x)


def _dsilu(x):
    s = _sigmoid(x)
    return s * (1.0 + x * (1.0 - s))


def _softplus(x):
    return jnp.maximum(x, 0.0) + jnp.log(1.0 + jnp.exp(-jnp.abs(x)))


def _log_sigmoid(x):
    return jnp.minimum(x, 0.0) - jnp.log(1.0 + jnp.exp(-jnp.abs(x)))


def _tile(n, target, mult=8):
    best = None
    for t in range(mult, min(n, target) + 1, mult):
        if n % t == 0:
            best = t
    assert best is not None, (n, target, mult)
    return best


def _mm(a, b, mode, out_dtype, name, tm=512, tn=1024, tk=2048, comm=None, swiglu_up=None):
    if mode == "nn":
        (M, K), N = a.shape, b.shape[1]
    elif mode == "nt":
        (M, K), N = a.shape, b.shape[0]
    else:
        (K, M), N = a.shape, b.shape[1]
    tm, tn, tk = _tile(M, tm, 128), _tile(N, tn, 128), _tile(K, tk, 128)
    nk = K // tk
    ca, cb = {"nn": (1, 0), "nt": (1, 1), "tn": (0, 0)}[mode]

    def body(a_ref, *rest):
        if swiglu_up is None:
            (b_ref, o_ref, *acc), av = rest, a_ref[...]
        else:
            (up_ref, b_ref, o_ref, *acc) = rest
            av = (_silu(a_ref[...].astype(F32)) * up_ref[...].astype(F32)).astype(BF16)
        part = _dg(av, b_ref[...], ca, cb)
        if nk == 1:
            o_ref[...] = part.astype(out_dtype)
        else:
            k = pl.program_id(2)

            @pl.when(k == 0)
            def _():
                acc[0][...] = part

            @pl.when(k > 0)
            def _():
                acc[0][...] += part

            @pl.when(k == nk - 1)
            def _():
                o_ref[...] = acc[0][...].astype(out_dtype)

    a_spec = pl.BlockSpec((tk, tm), lambda i, j, k: (k, i)) if mode == "tn" else pl.BlockSpec((tm, tk), lambda i, j, k: (i, k))
    b_spec = pl.BlockSpec((tn, tk), lambda i, j, k: (j, k)) if mode == "nt" else pl.BlockSpec((tk, tn), lambda i, j, k: (k, j))
    gi, gj = M // tm, N // tn
    scratch = [pltpu.VMEM((tm, tn), F32)] if nk > 1 else []
    out_spec, out_shape = pl.BlockSpec((tm, tn), lambda i, j, k: (i, j)), jax.ShapeDtypeStruct((M, N), out_dtype)
    if comm is None:
        lhs = [a] if swiglu_up is None else [a, swiglu_up]
        return pl.pallas_call(
            body, name=name, grid=(gi, gj, nk), in_specs=[a_spec] * len(lhs) + [b_spec], out_specs=out_spec,
            out_shape=out_shape, scratch_shapes=scratch, compiler_params=_cparams(("arbitrary", "arbitrary", "arbitrary")),
        )(*lhs, b)
    assert swiglu_up is None
    at = lambda i, j, k: (pl.program_id(0) == i) & (pl.program_id(1) == j) & (pl.program_id(2) == k)
    hosted = _hosted(body, 2, 1, len(scratch), comm, lambda: at(0, 0, 0), lambda: at(gi - 1, 0, 0),
                     lambda: at(gi - 1, gj - 1, nk - 1))
    outs = pl.pallas_call(
        hosted, name=name, grid=(gi, gj, nk), in_specs=[a_spec, b_spec] + [_ANY] * len(comm.arrays),
        out_specs=[out_spec] + [_ANY] * len(comm.out_shape), out_shape=[out_shape] + comm.out_shape,
        scratch_shapes=scratch + comm.scratch, compiler_params=_cparams(("arbitrary", "arbitrary", "arbitrary")),
    )(a, b, *comm.arrays)
    return outs[0], outs[1:]


def _place():
    return lax.axis_index("x"), lax.axis_index("y"), lax.axis_index("c")


def _flip(v, bit):
    return 1 - v if bit else v


def _allgather_small(v, name):
    R, C = v.shape

    def body(v_ref, out_ref, send_sems, recv_sems, local_sem):
        x, y, c = _place()
        me = 4 * x + 2 * y + c
        mine = pltpu.make_async_copy(v_ref, out_ref.at[me], local_sem)
        mine.start()

        def peer(r):
            return _flip(x, (r >> 2) & 1), _flip(y, (r >> 1) & 1), _flip(c, r & 1)

        sends = [pltpu.make_async_remote_copy(
            src_ref=v_ref, dst_ref=out_ref.at[me], send_sem=send_sems.at[r - 1], recv_sem=recv_sems.at[r - 1],
            device_id=peer(r), device_id_type=MESH) for r in range(1, N_DEV)]
        for cp in sends:
            cp.start()
        for r in range(1, N_DEV):
            px, py, pc = peer(r)
            pltpu.make_async_remote_copy(
                src_ref=v_ref, dst_ref=out_ref.at[4 * px + 2 * py + pc], send_sem=send_sems.at[r - 1],
                recv_sem=recv_sems.at[r - 1], device_id=(x, y, c), device_id_type=MESH).wait_recv()
        for cp in sends:
            cp.wait_send()
        mine.wait()

    return pl.pallas_call(
        body, name=name, out_shape=jax.ShapeDtypeStruct((N_DEV, R, C), v.dtype),
        in_specs=[pl.BlockSpec(memory_space=pltpu.VMEM)], out_specs=pl.BlockSpec(memory_space=pltpu.VMEM),
        scratch_shapes=[pltpu.SemaphoreType.DMA((N_DEV - 1,)), pltpu.SemaphoreType.DMA((N_DEV - 1,)), pltpu.SemaphoreType.DMA],
        compiler_params=pltpu.CompilerParams(vmem_limit_bytes=VMEM_LIMIT),
    )(v)


_CHIP_RELATIONS = ((1, 0), (0, 1), (1, 1))


class _Comm:
    def __init__(self, arrays, out_shape, scratch, start, middle, finish):
        self.arrays, self.out_shape, self.scratch = arrays, out_shape, scratch
        self.start, self.middle, self.finish = start, middle, finish


def _hosted(body, n_in, n_out, n_scratch, comm, first, middle, last):
    nc, no = len(comm.arrays), len(comm.out_shape)

    def wrapped(*refs):
        a = n_in + nc
        b = a + n_out + no
        ins, c_ins, outs, c_outs = refs[:n_in], refs[n_in:a], refs[a:a + n_out], refs[a + n_out:b]
        scratch, c_sems = refs[b:b + n_scratch], refs[b + n_scratch:]

        @pl.when(first())
        def _():
            comm.start(c_ins, c_outs, c_sems)

        body(*ins, *outs, *scratch)
        if comm.middle is not None:
            @pl.when(middle())
            def _():
                comm.middle(c_ins, c_outs, c_sems)

        @pl.when(last())
        def _():
            comm.finish(c_ins, c_outs, c_sems)

    return wrapped


_ANY = pl.BlockSpec(memory_space=pl.ANY)


def _gather_split_comm(shard):
    rows, cols = shard.shape
    half = rows // 2

    def copies(kind, ins, outs, sems):
        (in_ref,), (out_ref,), (send_sems, recv_sems, local_sem) = ins, outs, sems
        x, y, c = _place()
        chip = 2 * x + y
        mine = pl.ds(pl.multiple_of(c * half, 16), half)
        if kind == "local":
            return [pltpu.make_async_copy(in_ref, out_ref.at[chip], local_sem)]
        made = []
        for j, (fx, fy) in enumerate(_CHIP_RELATIONS):
            px, py = _flip(x, fx), _flip(y, fy)
            landed = out_ref.at[2 * px + py, mine]
            src, dst, k, to = {
                "send": (in_ref.at[mine], out_ref.at[chip, mine], j, (px, py, c)),
                "landed": (landed, landed, j, (x, y, c)),
                "passed": (landed, landed, 3 + j, (x, y, 1 - c)),
                "arrivals": (landed, landed, 3 + j, (x, y, c)),
            }[kind]
            made.append(pltpu.make_async_remote_copy(src_ref=src, dst_ref=dst, send_sem=send_sems.at[k],
                                                     recv_sem=recv_sems.at[k], device_id=to, device_id_type=MESH))
        return made

    def start(ins, outs, sems):
        for cp in copies("local", ins, outs, sems) + copies("send", ins, outs, sems):
            cp.start()

    def middle(ins, outs, sems):
        for got, fwd in zip(copies("landed", ins, outs, sems), copies("passed", ins, outs, sems)):
            got.wait_recv()
            fwd.start()

    def finish(ins, outs, sems):
        for cp in copies("arrivals", ins, outs, sems):
            cp.wait_recv()
        for cp in copies("send", ins, outs, sems) + copies("passed", ins, outs, sems):
            cp.wait_send()
        for cp in copies("local", ins, outs, sems):
            cp.wait()

    return _Comm([shard], [jax.ShapeDtypeStruct((N_CHIPS, rows, cols), shard.dtype)],
                 [pltpu.SemaphoreType.DMA((6,)), pltpu.SemaphoreType.DMA((6,)), pltpu.SemaphoreType.DMA],
                 start, middle, finish)


def _gather_comm(shards):
    n = len(shards)

    def copies(kind, ins, outs, sems):
        send_sems, recv_sems, local_sems = sems
        x, y, c = _place()
        chip = 2 * x + y
        if kind == "local":
            return [pltpu.make_async_copy(ins[i], outs[i].at[chip], local_sems.at[i]) for i in range(n)]
        made = []
        for i in range(n):
            for j, (fx, fy) in enumerate(_CHIP_RELATIONS):
                px, py = _flip(x, fx), _flip(y, fy)
                slot, to = (chip, (px, py, c)) if kind == "send" else (2 * px + py, (x, y, c))
                made.append(pltpu.make_async_remote_copy(
                    src_ref=ins[i], dst_ref=outs[i].at[slot], send_sem=send_sems.at[i, j], recv_sem=recv_sems.at[i, j],
                    device_id=to, device_id_type=MESH))
        return made

    def start(ins, outs, sems):
        for cp in copies("local", ins, outs, sems) + copies("send", ins, outs, sems):
            cp.start()

    def finish(ins, outs, sems):
        for cp in copies("recv", ins, outs, sems):
            cp.wait_recv()
        for cp in copies("send", ins, outs, sems):
            cp.wait_send()
        for cp in copies("local", ins, outs, sems):
            cp.wait()

    return _Comm(list(shards), [jax.ShapeDtypeStruct((N_CHIPS,) + s.shape, s.dtype) for s in shards],
                 [pltpu.SemaphoreType.DMA((n, 3)), pltpu.SemaphoreType.DMA((n, 3)), pltpu.SemaphoreType.DMA((n,))],
                 start, None, finish)


def _exchange_comm(blocks):
    n = len(blocks)

    def copies(kind, ins, outs, sems):
        send_sems, recv_sems, local_sems = sems
        x, y, c = _place()
        chip = 2 * x + y
        me, sibling = (x, y, c), (x, y, 1 - c)

        def remote(src, dst, i, j, to):
            return pltpu.make_async_remote_copy(src_ref=src, dst_ref=dst, send_sem=send_sems.at[i, j],
                                                recv_sem=recv_sems.at[i, j], device_id=to, device_id_type=MESH)

        made = []
        for i in range(n):
            if kind == "local":
                made.append(pltpu.make_async_copy(ins[i].at[chip], outs[i].at[chip], local_sems.at[i]))
                continue
            for j, (fx, fy) in enumerate(_CHIP_RELATIONS):
                px, py = _flip(x, fx), _flip(y, fy)
                src = 2 * px + py
                if kind == "first":
                    made.append(remote(ins[i].at[src], outs[i].at[chip], i, j, (px, py, c)))
                elif kind == "landed":
                    made.append(remote(ins[i].at[src], outs[i].at[src], i, j, me))
                elif kind == "passed":
                    made.append(remote(outs[i].at[src], outs[i].at[N_CHIPS + src], i, 4 + j, sibling))
            if kind == "first":
                made.append(remote(ins[i].at[chip], outs[i].at[N_CHIPS + chip], i, 3, sibling))
            if kind == "arrivals":
                made += [remote(ins[i].at[0], outs[i].at[0], i, j, me) for j in (3, 4, 5, 6)]
        return made

    def start(ins, outs, sems):
        for cp in copies("local", ins, outs, sems) + copies("first", ins, outs, sems):
            cp.start()

    def middle(ins, outs, sems):
        for got, fwd in zip(copies("landed", ins, outs, sems), copies("passed", ins, outs, sems)):
            got.wait_recv()
            fwd.start()

    def finish(ins, outs, sems):
        for cp in copies("arrivals", ins, outs, sems):
            cp.wait_recv()
        for cp in copies("first", ins, outs, sems) + copies("passed", ins, outs, sems):
            cp.wait_send()
        for cp in copies("local", ins, outs, sems):
            cp.wait()

    return _Comm(list(blocks), [jax.ShapeDtypeStruct((N_DEV,) + b.shape[1:], b.dtype) for b in blocks],
                 [pltpu.SemaphoreType.DMA((n, 7)), pltpu.SemaphoreType.DMA((n, 7)), pltpu.SemaphoreType.DMA((n,))],
                 start, middle, finish)


def _row_spec(tr, w, col=0):
    return pl.BlockSpec((tr, w), lambda i: (i, col))


def _vec_spec(w):
    return pl.BlockSpec((1, w), lambda i: (0, 0))


def _acc_spec(w):
    return pl.BlockSpec((8, w), lambda i: (0, 0))


def _rms(x):
    return lax.rsqrt(jnp.mean(x * x, axis=-1, keepdims=True) + EPS)


def _rms_bwd(dn, n, rstd):
    return rstd * (dn - n * jnp.mean(dn * n, axis=-1, keepdims=True))


def _colsum(x):
    return jnp.sum(x, axis=0, keepdims=True)


def _zero_first(ref):
    @pl.when(pl.program_id(0) == 0)
    def _():
        ref[...] = jnp.zeros_like(ref)


def _x_specs(tr, n_lat_tiles):
    return [pl.BlockSpec((tr, D), lambda i: (jnp.minimum(i, n_lat_tiles - 1), 0)),
            pl.BlockSpec((tr, D), lambda i: (jnp.maximum(i - n_lat_tiles, 0), 0))]


def _x_tile(x_ref, c_ref, n_lat_tiles):
    return jnp.where(pl.program_id(0) >= n_lat_tiles, c_ref[...], x_ref[...])


def _norm_mod(x, ctx, w, mod, n_lat_tiles, tr, comm):
    R = x.shape[0] + ctx.shape[0]
    n = R // tr

    def body(x_ref, c_ref, w_ref, mod_ref, o_ref):
        xv = _x_tile(x_ref, c_ref, n_lat_tiles)
        nw = xv * _rms(xv) * w_ref[...]
        o_ref[...] = (nw * (1.0 + mod_ref[0, 1]) + mod_ref[0, 0]).astype(BF16)

    at = lambda i: pl.program_id(0) == i
    outs = pl.pallas_call(
        _hosted(body, 4, 1, 0, comm, lambda: at(0), lambda: at(n // 2), lambda: at(n - 1)), name="norm1_mod", grid=(n,),
        in_specs=_x_specs(tr, n_lat_tiles) + [_vec_spec(D), pl.BlockSpec(
            (1, 2, 1, D), lambda i: (jnp.where(i >= n_lat_tiles, 1, 0), 0, 0, 0))] + [_ANY] * len(comm.arrays),
        out_specs=[_row_spec(tr, D)] + [_ANY] * len(comm.out_shape),
        out_shape=[jax.ShapeDtypeStruct((R, D), BF16)] + comm.out_shape, scratch_shapes=comm.scratch,
        compiler_params=_cparams(("arbitrary",)),
    )(x, ctx, w, mod, *comm.arrays)
    return outs[0], outs[1:]


def _resid_norm_mod(x, ctx, mix, g1, w2, sh2, sc2, n_lat_tiles, tr):
    R = x.shape[0] + ctx.shape[0]

    def body(x_ref, c_ref, mix_ref, g1_ref, w_ref, sh_ref, sc_ref, h2_ref, u_ref):
        h2 = _x_tile(x_ref, c_ref, n_lat_tiles) + g1_ref[...] * mix_ref[...]
        h2_ref[...] = h2
        n = h2 * _rms(h2) * w_ref[...]
        u_ref[...] = (n * (1.0 + sc_ref[...]) + sh_ref[...]).astype(BF16)

    return pl.pallas_call(
        body, name="resid_norm2_mod", grid=(R // tr,),
        in_specs=_x_specs(tr, n_lat_tiles) + [_row_spec(tr, D)] + [_vec_spec(D)] * 4,
        out_specs=[_row_spec(tr, D), _row_spec(tr, D)],
        out_shape=[jax.ShapeDtypeStruct((R, D), F32), jax.ShapeDtypeStruct((R, D), BF16)],
        compiler_params=_cparams(("arbitrary",)),
    )(x, ctx, mix, g1, w2, sh2, sc2)


def _loss_head(h2, f, target, g2, fw, n_lat_tiles, tr):
    R = h2.shape[0]

    def body(h2_ref, f_ref, t_ref, g2_ref, fw_ref, dh3_ref, df_ref, acc_ref):
        _zero_first(acc_ref)
        lat = pl.program_id(0) < n_lat_tiles
        fv = f_ref[...]
        h3 = h2_ref[...] + g2_ref[...] * fv
        rstd = _rms(h3)
        n = h3 * rstd
        err = n * fw_ref[...] - t_ref[...]
        dy = err * (1.0 / D)
        dh3 = jnp.where(lat, _rms_bwd(dy * fw_ref[...], n, rstd), 0.0)
        dh3_ref[...] = dh3
        df_ref[...] = (g2_ref[...] * dh3).astype(BF16)
        acc_ref[0:1, :] += jnp.where(lat, _colsum(dy * n), 0.0)
        acc_ref[1:2, :] += _colsum(dh3 * fv)
        acc_ref[2:3, :] += jnp.where(lat, _colsum(err * err) * (0.5 / D), 0.0)

    return pl.pallas_call(
        body, name="loss_head", grid=(R // tr,),
        in_specs=[_row_spec(tr, D), _row_spec(tr, D),
                  pl.BlockSpec((tr, D), lambda i: (jnp.minimum(i, n_lat_tiles - 1), 0)), _vec_spec(D), _vec_spec(D)],
        out_specs=[_row_spec(tr, D), _row_spec(tr, D), _acc_spec(D)],
        out_shape=[jax.ShapeDtypeStruct((R, D), F32), jax.ShapeDtypeStruct((R, D), BF16), jax.ShapeDtypeStruct((8, D), F32)],
        compiler_params=_cparams(("arbitrary",)),
    )(h2, f, target, g2, fw)


def _ffn_in_bwd(du_a, du_b, h2, dh3, mix, sc2, g1, w2, tr):
    R = h2.shape[0]

    def body(dua_ref, dub_ref, h2_ref, dh3_ref, mix_ref, sc_ref, g1_ref, w_ref, dh2_ref, dmix_ref, acc_ref):
        _zero_first(acc_ref)
        du = _f32(dua_ref) + _f32(dub_ref)
        h2 = h2_ref[...]
        rstd = _rms(h2)
        n = h2 * rstd
        dnw = du * (1.0 + sc_ref[...])
        dh2 = dh3_ref[...] + _rms_bwd(dnw * w_ref[...], n, rstd)
        dh2_ref[...] = dh2
        dmix_ref[...] = (g1_ref[...] * dh2).astype(BF16)
        acc_ref[0:1, :] += _colsum(du * n * w_ref[...])
        acc_ref[1:2, :] += _colsum(du)
        acc_ref[2:3, :] += _colsum(dnw * n)
        acc_ref[3:4, :] += _colsum(dh2 * mix_ref[...])

    return pl.pallas_call(
        body, name="ffn_in_bwd", grid=(R // tr,),
        in_specs=[_row_spec(tr, D)] * 5 + [_vec_spec(D)] * 3,
        out_specs=[_row_spec(tr, D), _row_spec(tr, D), _acc_spec(D)],
        out_shape=[jax.ShapeDtypeStruct((R, D), F32), jax.ShapeDtypeStruct((R, D), BF16), jax.ShapeDtypeStruct((8, D), F32)],
        compiler_params=_cparams(("arbitrary",)),
    )(du_a, du_b, h2, dh3, mix, sc2, g1, w2)


def _norm1_bwd(dh1, x, ctx, dh2, w1, mod, n_lat_tiles, tr):
    R = x.shape[0] + ctx.shape[0]

    def body(dh1_ref, x_ref, c_ref, dh2_ref, w_ref, mod_ref, dx_ref, acc_ref):
        i = pl.program_id(0)

        @pl.when((i == 0) | (i == n_lat_tiles))
        def _():
            acc_ref[...] = jnp.zeros_like(acc_ref)

        dh1 = dh1_ref[...]
        x = _x_tile(x_ref, c_ref, n_lat_tiles)
        rstd = _rms(x)
        n = x * rstd
        dnw = dh1 * (1.0 + mod_ref[0, 1])

        @pl.when(i < n_lat_tiles)
        def _():
            dx_ref[...] = dh2_ref[...] + _rms_bwd(dnw * w_ref[...], n, rstd)

        acc_ref[0, 0:1, :] += _colsum(dh1 * n * w_ref[...])
        acc_ref[0, 1:2, :] += _colsum(dh1)
        acc_ref[0, 2:3, :] += _colsum(dnw * n)

    sel = lambda i: jnp.where(i >= n_lat_tiles, 1, 0)
    return pl.pallas_call(
        body, name="norm1_bwd", grid=(R // tr,),
        in_specs=[_row_spec(tr, D)] + _x_specs(tr, n_lat_tiles) + [_row_spec(tr, D), _vec_spec(D),
                                                                   pl.BlockSpec((1, 2, 1, D), lambda i: (sel(i), 0, 0, 0))],
        out_specs=[pl.BlockSpec((tr, D), lambda i: (jnp.minimum(i, n_lat_tiles - 1), 0)),
                   pl.BlockSpec((1, 8, D), lambda i: (sel(i), 0, 0))],
        out_shape=[jax.ShapeDtypeStruct((n_lat_tiles * tr, D), F32), jax.ShapeDtypeStruct((2, 8, D), F32)],
        compiler_params=_cparams(("arbitrary",)),
    )(dh1, x, ctx, dh2, w1, mod)


def _swiglu_act_bwd(da, gp, up, tr):
    R = gp.shape[0]

    def body(da_ref, g_ref, u_ref, dg_ref, du_ref):
        da, g = _f32(da_ref), _f32(g_ref)
        dg_ref[...] = (da * _f32(u_ref) * _dsilu(g)).astype(BF16)
        du_ref[...] = (da * _silu(g)).astype(BF16)

    return pl.pallas_call(
        body, name="swiglu_act_bwd", grid=(R // tr,), in_specs=[_row_spec(tr, D_FF)] * 3, out_specs=[_row_spec(tr, D_FF)] * 2,
        out_shape=[jax.ShapeDtypeStruct((R, D_FF), BF16)] * 2, compiler_params=_cparams(("arbitrary",)),
    )(da, gp, up)


def _merge(ya, yb, parts, tr):
    R = ya.shape[0]

    def body(ya_ref, yb_ref, ga_ref, gb_ref, o_ref):
        o_ref[...] = (_sigmoid(_f32(ga_ref)) * _f32(ya_ref) + _sigmoid(_f32(gb_ref)) * _f32(yb_ref)).astype(BF16)

    return pl.pallas_call(
        body, name="merge", grid=(R // tr,),
        in_specs=[_row_spec(tr, D), _row_spec(tr, D), _row_spec(tr, D, COL_GA // D), _row_spec(tr, D, COL_GB // D)],
        out_specs=_row_spec(tr, D), out_shape=jax.ShapeDtypeStruct((R, D), BF16), compiler_params=_cparams(("arbitrary",)),
    )(ya, yb, parts, parts)


def _dparts_out(tr, w, col, nd=1):
    blk = col // w
    return pl.BlockSpec((tr, w), (lambda i: (i, blk)) if nd == 1 else (lambda i, j: (i, blk + j)))


def _merge_bwd(dm, ya, yb, parts, dparts, tr):
    R = ya.shape[0]

    def body(dm_ref, ya_ref, yb_ref, ga_ref, gb_ref, _, dya_ref, dyb_ref, dg_ref):
        dm = _f32(dm_ref)
        sa, sb = _sigmoid(_f32(ga_ref)), _sigmoid(_f32(gb_ref))
        dya_ref[...] = (dm * sa).astype(BF16)
        dyb_ref[...] = (dm * sb).astype(BF16)
        dg_ref[:, 0:D] = (dm * _f32(ya_ref) * sa * (1.0 - sa)).astype(BF16)
        dg_ref[:, D:2 * D] = (dm * _f32(yb_ref) * sb * (1.0 - sb)).astype(BF16)

    return pl.pallas_call(
        body, name="merge_bwd", grid=(R // tr,),
        in_specs=[_row_spec(tr, D)] * 3 + [_row_spec(tr, D, COL_GA // D), _row_spec(tr, D, COL_GB // D), _ANY],
        out_specs=[_row_spec(tr, D), _row_spec(tr, D), _dparts_out(tr, 2 * D, COL_GA)],
        out_shape=[jax.ShapeDtypeStruct((R, D), BF16)] * 2 + [jax.ShapeDtypeStruct(dparts.shape, BF16)],
        input_output_aliases={5: 2}, compiler_params=_cparams(("arbitrary",)),
    )(dm, ya, yb, parts, parts, dparts)


def _gla_out(o2, parts, gw4, tr):
    R = parts.shape[0]

    def body(of_ref, ob_ref, r_ref, w_ref, out_ref):
        oa = _f32(of_ref) + _f32(ob_ref)
        sr = _silu(_f32(r_ref))
        for h in range(GLA_H):
            s = slice(h * GLA_DV, (h + 1) * GLA_DV)
            o = oa[:, s]
            out_ref[:, s] = (o * _rms(o) * w_ref[:, s] * sr[:, s]).astype(BF16)

    return pl.pallas_call(
        body, name="gla_out", grid=(R // tr,),
        in_specs=[_row_spec(tr, D), _row_spec(tr, D), _row_spec(tr, D, COL_R // D), _vec_spec(D)],
        out_specs=_row_spec(tr, D), out_shape=jax.ShapeDtypeStruct((R, D), BF16), compiler_params=_cparams(("arbitrary",)),
    )(o2[0], o2[1], parts, gw4)


def _gla_out_bwd(dout, o2, parts, gw4, dparts, tr):
    R = parts.shape[0]

    def body(d_ref, of_ref, ob_ref, r_ref, w_ref, _, do_ref, dr_ref, acc_ref):
        _zero_first(acc_ref)
        oa = _f32(of_ref) + _f32(ob_ref)
        r = _f32(r_ref)
        sr = _silu(r)
        dout = _f32(d_ref)
        for h in range(GLA_H):
            s = slice(h * GLA_DV, (h + 1) * GLA_DV)
            o = oa[:, s]
            rstd = _rms(o)
            n = o * rstd
            w = w_ref[:, s]
            dr_ref[:, s] = (dout[:, s] * n * w * _dsilu(r[:, s])).astype(BF16)
            dnw = dout[:, s] * sr[:, s]
            do_ref[:, s] = _rms_bwd(dnw * w, n, rstd).astype(ACT)
            acc_ref[0:1, s] += _colsum(dnw * n)

    return pl.pallas_call(
        body, name="gla_out_bwd", grid=(R // tr,),
        in_specs=[_row_spec(tr, D), _row_spec(tr, D), _row_spec(tr, D), _row_spec(tr, D, COL_R // D), _vec_spec(D), _ANY],
        out_specs=[_row_spec(tr, D), _dparts_out(tr, D, COL_R), _acc_spec(D)],
        out_shape=[jax.ShapeDtypeStruct((R, D), ACT), jax.ShapeDtypeStruct(dparts.shape, BF16), jax.ShapeDtypeStruct((8, D), F32)],
        input_output_aliases={5: 1}, compiler_params=_cparams(("arbitrary",)),
    )(dout, o2[0], o2[1], parts, gw4, dparts)


SSM_GW = SSM_INNER // SSM_G


def _ssd_out(y2, xbc, parts, dskip, nw, tr):
    R = parts.shape[0]

    def body(yf_ref, yb_ref, x_ref, z_ref, ds_ref, w_ref, out_ref):
        ob = (_f32(yf_ref) + _f32(yb_ref) + ds_ref[...] * _f32(x_ref)) * _silu(_f32(z_ref))
        for g in range(SSM_G):
            s = slice(g * SSM_GW, (g + 1) * SSM_GW)
            o = ob[:, s]
            out_ref[:, s] = (o * _rms(o) * w_ref[:, s]).astype(BF16)

    return pl.pallas_call(
        body, name="ssd_out", grid=(R // tr,),
        in_specs=[_row_spec(tr, SSM_INNER)] * 3 + [_row_spec(tr, SSM_INNER, COL_Z // SSM_INNER),
                                                   _vec_spec(SSM_INNER), _vec_spec(SSM_INNER)],
        out_specs=_row_spec(tr, SSM_INNER), out_shape=jax.ShapeDtypeStruct((R, SSM_INNER), BF16),
        compiler_params=_cparams(("arbitrary",)),
    )(y2[0], y2[1], xbc, parts, dskip, nw)


def _ssd_out_bwd(dout, y2, xbc, parts, dskip, nw, dparts, tr):
    R = parts.shape[0]

    def body(d_ref, yf_ref, yb_ref, x_ref, z_ref, ds_ref, w_ref, _, dy_ref, dz_ref, acc_ref):
        _zero_first(acc_ref)
        x, z = _f32(x_ref), _f32(z_ref)
        pre = _f32(yf_ref) + _f32(yb_ref) + ds_ref[...] * x
        sz = _silu(z)
        ob = pre * sz
        dout = _f32(d_ref)
        for g in range(SSM_G):
            s = slice(g * SSM_GW, (g + 1) * SSM_GW)
            o = ob[:, s]
            rstd = _rms(o)
            n = o * rstd
            dob = _rms_bwd(dout[:, s] * w_ref[:, s], n, rstd)
            dz_ref[:, s] = (dob * pre[:, s] * _dsilu(z[:, s])).astype(BF16)
            dy = dob * sz[:, s]
            dy_ref[:, s] = dy.astype(ACT)
            acc_ref[0:1, s] += _colsum(dout[:, s] * n)
            acc_ref[1:2, s] += _colsum(dy * x[:, s])

    return pl.pallas_call(
        body, name="ssd_out_bwd", grid=(R // tr,),
        in_specs=[_row_spec(tr, SSM_INNER)] * 4 + [_row_spec(tr, SSM_INNER, COL_Z // SSM_INNER),
                                                   _vec_spec(SSM_INNER), _vec_spec(SSM_INNER), _ANY],
        out_specs=[_row_spec(tr, SSM_INNER), _dparts_out(tr, SSM_INNER, COL_Z), _acc_spec(SSM_INNER)],
        out_shape=[jax.ShapeDtypeStruct((R, SSM_INNER), ACT), jax.ShapeDtypeStruct(dparts.shape, BF16),
                   jax.ShapeDtypeStruct((8, SSM_INNER), F32)],
        input_output_aliases={7: 1}, compiler_params=_cparams(("arbitrary",)),
    )(dout, y2[0], y2[1], xbc, parts, dskip, nw, dparts)


CONV_W = SSM_INNER + 2 * SSM_G * SSM_N
CONV_BLK = 1024


CONV_SHIFTS = (-2, -1, 1, 2)


def _conv_mask_table(tr):
    t = np.arange(tr)
    table = np.zeros((2, len(CONV_SHIFTS), tr, 128), np.float32)
    for kind, (pos, seg) in enumerate(((t % GRID_W, GRID_W), (t, tr))):
        for k, s in enumerate(CONV_SHIFTS):
            table[kind, k] = ((pos + s >= 0) & (pos + s < seg)).astype(np.float32)[:, None]
    return jnp.asarray(table)


def _shifted(u, s, mask_ref, tr):
    return u if s == 0 else pltpu.roll(u, (-s) % tr, 0) * mask_ref[0, CONV_SHIFTS.index(s)]


def _conv_mask_spec(tr, n_lat_tiles, row_axis):
    return pl.BlockSpec((1, len(CONV_SHIFTS), tr, 128),
                        lambda *ids: (jnp.where(ids[row_axis] >= n_lat_tiles, 1, 0), 0, 0, 0))


def _conv_fwd(parts, cw, cb, n_lat_tiles, tr):
    R = parts.shape[0]

    n_blk = CONV_W // CONV_BLK

    def body(*refs):
        u_refs, (w_ref, b_ref, mask_ref, o_ref) = refs[:n_blk], refs[n_blk:]
        for k, u_ref in enumerate(u_refs):
            def lanes(l, carry, u_ref=u_ref, k=k):
                sl = pl.ds(pl.multiple_of(l * 128, 128), 128)
                so = pl.ds(pl.multiple_of(k * CONV_BLK + l * 128, 128), 128)
                u, w = u_ref[:, sl].astype(F32), w_ref[:, so]
                acc = jnp.zeros_like(u) + b_ref[:, so]
                for j in range(SSM_CONV):
                    acc = acc + _shifted(u, j - CONV_LEFT, mask_ref, tr) * w[j:j + 1, :]
                o_ref[:, so] = _silu(acc).astype(ACT)
                return carry

            lax.fori_loop(0, CONV_BLK // 128, lanes, 0)

    return pl.pallas_call(
        body, name="conv_fwd", grid=(R // tr,),
        in_specs=[pl.BlockSpec((tr, CONV_BLK), lambda i, k=k: (i, COL_XBC // CONV_BLK + k)) for k in range(n_blk)]
        + [pl.BlockSpec((SSM_CONV, CONV_W), lambda i: (0, 0)), pl.BlockSpec((1, CONV_W), lambda i: (0, 0)),
           _conv_mask_spec(tr, n_lat_tiles, 0)],
        out_specs=pl.BlockSpec((tr, CONV_W), lambda i: (i, 0)), out_shape=jax.ShapeDtypeStruct((R, CONV_W), ACT),
        compiler_params=_cparams(("arbitrary",)),
    )(*([parts] * n_blk), cw, cb, _conv_mask_table(tr))


def _conv_bwd(dx, db, dc, dy, dskip, parts, cw, cb, dparts, n_lat_tiles, tr):
    R = parts.shape[0]
    bc = SSM_G * SSM_N

    def body(dxf_ref, dxb_ref, dy_ref, ds_ref, dbf_ref, dbb_ref, dcf_ref, dcb_ref, u_ref, w_ref, b_ref, _, mask_ref,
             du_ref, acc_ref, d_scr):
        _zero_first(acc_ref)
        d_scr[:, 0:SSM_INNER] = dxf_ref[...] + dxb_ref[...] + _f32(dy_ref) * ds_ref[...]
        d_scr[:, SSM_INNER:SSM_INNER + bc] = dbf_ref[...] + dbb_ref[...]
        d_scr[:, SSM_INNER + bc:] = dcf_ref[...] + dcb_ref[...]

        def lanes(l, carry):
            sl = pl.ds(pl.multiple_of(l * 128, 128), 128)
            u, w = u_ref[:, sl].astype(F32), w_ref[:, sl]
            pre = jnp.zeros_like(u) + b_ref[:, sl]
            taps = []
            for j in range(SSM_CONV):
                tap = _shifted(u, j - CONV_LEFT, mask_ref, tr)
                taps.append(tap)
                pre = pre + tap * w[j:j + 1, :]
            dpre = d_scr[:, sl] * _dsilu(pre)
            du = jnp.zeros_like(u)
            sums = []
            for j in range(SSM_CONV):
                sums.append(_colsum(dpre * taps[j]))
                du = du + _shifted(dpre, CONV_LEFT - j, mask_ref, tr) * w[j:j + 1, :]
            sums += [_colsum(dpre), jnp.zeros((8 - SSM_CONV - 1, 128), F32)]
            acc_ref[:, sl] += jnp.concatenate(sums, axis=0)
            du_ref[:, sl] = du.astype(BF16)
            return carry

        lax.fori_loop(0, CONV_W // 128, lanes, 0)

    return pl.pallas_call(
        body, name="conv_bwd", grid=(R // tr,),
        in_specs=[_row_spec(tr, SSM_INNER)] * 3 + [_vec_spec(SSM_INNER)] + [_row_spec(tr, bc)] * 4
        + [_row_spec(tr, CONV_W, COL_XBC // CONV_W), pl.BlockSpec((SSM_CONV, CONV_W), lambda i: (0, 0)), _vec_spec(CONV_W),
           _ANY, _conv_mask_spec(tr, n_lat_tiles, 0)],
        out_specs=[_dparts_out(tr, CONV_W, COL_XBC), _acc_spec(CONV_W)],
        out_shape=[jax.ShapeDtypeStruct(dparts.shape, BF16), jax.ShapeDtypeStruct((8, CONV_W), F32)],
        scratch_shapes=[pltpu.VMEM((tr, CONV_W), F32)],
        input_output_aliases={11: 0}, compiler_params=_cparams(("arbitrary",)),
    )(*dx, dy, dskip, *db, *dc, parts, cw, cb, dparts, _conv_mask_table(tr))


def _chunk_row_block(d, i, n_lat, n_ctx):
    fwd = jnp.where(i < n_ctx, n_lat + i, i - n_ctx)
    rev = n_lat + n_ctx - 1 - i
    if isinstance(d, int):
        return rev if d else fwd
    return jnp.where(d == 0, fwd, rev)


def _tri(n, d, transpose=False):
    row = lax.broadcasted_iota(jnp.int32, (n, n), 0)
    col = lax.broadcasted_iota(jnp.int32, (n, n), 1)
    diff = (col - row) if transpose else (row - col)
    return diff * (1 - 2 * d) >= 0


def _gla_gates(sm, uhi, ulo, bias, d):
    pre = _nn3(sm, uhi, ulo) + bias
    g = _log_sigmoid(pre) * (1.0 / GLA_TAU)
    mask = _tri(GLA_C, d)
    b = _x_nn(mask.astype(BF16), g)
    b_tot = _colsum(g)
    b_ref = b[GLA_C // 2:GLA_C // 2 + 1, :]
    e_q = jnp.exp(jnp.minimum(b - b_ref, EXP_CLAMP))
    e_k = jnp.exp(jnp.minimum(b_ref - b, EXP_CLAMP))
    return pre, mask, b_tot, e_q, e_k, jnp.exp(b), jnp.exp(b_tot - b)


GLA_QK = GLA_H * GLA_DK
GLA_V = GLA_H * GLA_DV


def _gla_specs(n_lat, n_ctx, step_of):
    rbs = [lambda i, d=d: _chunk_row_block(d, step_of(i), n_lat, n_ctx) for d in range(2)]
    specs = []
    for rb in rbs:
        specs += [pl.BlockSpec((GLA_C, GLA_QK), lambda i, rb=rb: (rb(i), COL_Q // GLA_QK)),
                  pl.BlockSpec((GLA_C, GLA_QK), lambda i, rb=rb: (rb(i), COL_K // GLA_QK)),
                  pl.BlockSpec((GLA_C, GLA_V), lambda i, rb=rb: (rb(i), COL_V // GLA_V)),
                  pl.BlockSpec((GLA_C, 128), lambda i, rb=rb: (rb(i), 0))]
    specs += [pl.BlockSpec((2, 128, GLA_QK), lambda i: (0, 0, 0)), pl.BlockSpec((2, 128, GLA_QK), lambda i: (0, 0, 0)),
              pl.BlockSpec((2, 1, GLA_QK), lambda i: (0, 0, 0))]
    return specs, rbs


def _gla_fwd(parts, sm, uhi, ulo, bias, n_lat, n_ctx):
    R = parts.shape[0]
    n_steps = n_lat + n_ctx
    scale = GLA_DK ** -0.5

    def body(*refs):
        ins, (uhi_ref, ulo_ref, bias_ref), o_refs, hist_ref, st = refs[:8], refs[8:11], refs[11:13], refs[13], refs[14]

        @pl.when(pl.program_id(0) == 0)
        def _():
            st[...] = jnp.zeros_like(st)

        for d in range(2):
            q_ref, k_ref, v_ref, sm_ref = ins[4 * d:4 * d + 4]
            _, mask, b_tot, e_q, e_k, e_in, e_out = _gla_gates(sm_ref[...], uhi_ref[d], ulo_ref[d], bias_ref[d], d)
            q, k, v = _f32(q_ref) * scale, _f32(k_ref), _bf(v_ref[...])
            qb, kb, q_in, k_out, decay = _bf(q * e_q), _bf(k * e_k), _bf(q * e_in), _bf(k * e_out), jnp.exp(b_tot)
            for h in range(GLA_H):
                sk, sv = slice(h * GLA_DK, (h + 1) * GLA_DK), slice(h * GLA_DV, (h + 1) * GLA_DV)
                att = jnp.where(mask, _nt(qb[:, sk], kb[:, sk]), 0.0)
                s_in = st[d, h]
                hist_ref[d, 0, h] = s_in
                o_refs[d][:, sv] = (_nn(_bf(att), v[:, sv]) + _nt(q_in[:, sk], _bf(s_in))).astype(ACT)
                st[d, h] = decay[:, sk] * s_in + _tn(v[:, sv], k_out[:, sk])

    in_specs, rbs = _gla_specs(n_lat, n_ctx, lambda i: i)
    return pl.pallas_call(
        body, name="gla_fwd", grid=(n_steps,), in_specs=in_specs,
        out_specs=[pl.BlockSpec((GLA_C, GLA_V), lambda i, rb=rb: (rb(i), 0)) for rb in rbs]
        + [pl.BlockSpec((2, 1, GLA_H, GLA_DV, GLA_DK), lambda i: (0, i, 0, 0, 0))],
        out_shape=[jax.ShapeDtypeStruct((R, GLA_V), ACT)] * 2 + [jax.ShapeDtypeStruct((2, n_steps, GLA_H, GLA_DV, GLA_DK), F32)],
        scratch_shapes=[pltpu.VMEM((2, GLA_H, GLA_DV, GLA_DK), F32)],
        compiler_params=_cparams(("arbitrary",)),
    )(*([parts, parts, parts, sm] * 2), uhi, ulo, bias)


def _gla_bwd(do, parts, sm, uhi, ulo, bias, hist, n_lat, n_ctx):
    R = parts.shape[0]
    n_steps = n_lat + n_ctx
    scale = GLA_DK ** -0.5
    step_of = lambda j: n_steps - 1 - j

    def body(*refs):
        ins, (uhi_ref, ulo_ref, bias_ref), do_refs, hist_ref = refs[:8], refs[8:11], refs[11:13], refs[13]
        outs, dst = refs[14:22], refs[22]

        @pl.when(pl.program_id(0) == 0)
        def _():
            dst[...] = jnp.zeros_like(dst)

        for d in range(2):
            q_ref, k_ref, v_ref, sm_ref = ins[4 * d:4 * d + 4]
            dq_ref, dk_ref, dv_ref, dp_ref = outs[4 * d:4 * d + 4]
            pre, mask, b_tot, e_q, e_k, e_in, e_out = _gla_gates(sm_ref[...], uhi_ref[d], ulo_ref[d], bias_ref[d], d)
            q, k, v = _f32(q_ref) * scale, _f32(k_ref), _bf(v_ref[...])
            dout = _bf(do_refs[d][...])
            k_out_f = k * e_out
            qb, kb, q_in, k_out, decay = _bf(q * e_q), _bf(k * e_k), _bf(q * e_in), _bf(k_out_f), jnp.exp(b_tot)
            dqs, dks, dk_outs, dss = [], [], [], []
            for h in range(GLA_H):
                sk, sv = slice(h * GLA_DK, (h + 1) * GLA_DK), slice(h * GLA_DV, (h + 1) * GLA_DV)
                s_in, ds = hist_ref[d, 0, h], dst[d, h]
                att = jnp.where(mask, _nt(qb[:, sk], kb[:, sk]), 0.0)
                datt = _bf(jnp.where(mask, _nt(dout[:, sv], v[:, sv]), 0.0))
                dv_ref[:, sv] = (_tn(_bf(att), dout[:, sv]) + _nt(k_out[:, sk], _bf(ds))).astype(ACT)
                dqs.append(_nn(datt, kb[:, sk]) * e_q[:, sk] + _nn(dout[:, sv], _bf(s_in)) * e_in[:, sk])
                dk_o = _nn(v[:, sv], _bf(ds))
                dk_outs.append(dk_o)
                dks.append(_tn(datt, qb[:, sk]) * e_k[:, sk])
                dss.append(_colsum(ds * s_in))
                dst[d, h] = decay[:, sk] * ds + _tn(dout[:, sv], q_in[:, sk])
            dq, dk_out = jnp.concatenate(dqs, axis=1), jnp.concatenate(dk_outs, axis=1)
            dk = jnp.concatenate(dks, axis=1) + dk_out * e_out
            dq_ref[...] = (dq * scale).astype(ACT)
            dk_ref[...] = dk.astype(ACT)
            db_tot = _colsum(dk_out * k_out_f) + decay * jnp.concatenate(dss, axis=1)
            dg = _x_nn(_tri(GLA_C, d, transpose=True).astype(BF16), dq * q - dk * k) + db_tot
            dp_ref[...] = dg * (1.0 / GLA_TAU) * _sigmoid(-pre)

    in_specs, rbs = _gla_specs(n_lat, n_ctx, step_of)
    in_specs += [pl.BlockSpec((GLA_C, GLA_V), lambda j, rb=rb: (rb(j), 0)) for rb in rbs]
    in_specs += [pl.BlockSpec((2, 1, GLA_H, GLA_DV, GLA_DK), lambda j: (0, step_of(j), 0, 0, 0))]
    out_specs, out_shape = [], []
    for rb in rbs:
        for w, dt in ((GLA_QK, ACT), (GLA_QK, ACT), (GLA_V, ACT), (GLA_QK, F32)):
            out_specs.append(pl.BlockSpec((GLA_C, w), lambda j, rb=rb: (rb(j), 0)))
            out_shape.append(jax.ShapeDtypeStruct((R, w), dt))
    outs = pl.pallas_call(
        body, name="gla_bwd", grid=(n_steps,), in_specs=in_specs, out_specs=out_specs, out_shape=out_shape,
        scratch_shapes=[pltpu.VMEM((2, GLA_H, GLA_DV, GLA_DK), F32)],
        compiler_params=_cparams(("arbitrary",)),
    )(*([parts, parts, parts, sm] * 2), uhi, ulo, bias, do, do, hist)
    return [(outs[k], outs[4 + k]) for k in range(4)]


def _ssd_consts(dt_bias, a_log):
    sel = np.zeros((2, SSM_G, 128, 128), np.float32)
    for d, base in enumerate((SM_DTF, SM_DTB)):
        for g in range(SSM_G):
            for e in range(SSM_HPG):
                sel[d, g, base + SSM_HPG * g + e, e] = 1.0
    e512 = np.zeros((128, SSM_GW), np.float32)
    for e in range(SSM_HPG):
        e512[e, SSM_P * e:SSM_P * (e + 1)] = 1.0
    a_neg = -jnp.exp(a_log)
    pad = lambda v: jnp.pad(v.reshape(2, SSM_G, 1, SSM_HPG), ((0, 0), (0, 0), (0, 0), (0, 128 - SSM_HPG)))
    return dict(
        sel=jnp.asarray(sel, BF16), sel_t=jnp.asarray(sel.transpose(0, 1, 3, 2), BF16), e512_t=jnp.asarray(e512.T, BF16),
        dtb=pad(dt_bias), a=pad(a_neg), a512=jnp.repeat(a_neg, SSM_P, axis=1).reshape(2, SSM_G, 1, SSM_GW))


def _head_columns(x8):
    return [jnp.broadcast_to(x8[:, e:e + 1], (x8.shape[0], 128)) for e in range(SSM_HPG)]


def _head_layout(cols):
    low = lax.broadcasted_iota(jnp.int32, (1, 128), 1) < SSM_P
    return jnp.concatenate([jnp.where(low, cols[2 * j], cols[2 * j + 1]) for j in range(SSM_HPG // 2)], axis=1)


def _ssd_common(sm, sel, dtb, a_neg, a512, d):
    dtr8 = _nn_x(sm, sel) + dtb
    dt8 = _softplus(dtr8)
    a8 = a_neg * dt8
    mask = _tri(SSM_C, d)
    mask_t = _tri(SSM_C, d, transpose=True).astype(BF16)
    cum8 = _x_nn(mask.astype(BF16), a8)
    a_hi, a_lo = _split(a8)
    cum_t = _tn(a_hi, mask_t) + _tn(a_lo, mask_t)
    cum_cols = _head_columns(cum8)
    dt_exp = _head_layout(_head_columns(dt8))
    a_exp = a512 * dt_exp
    return dict(dtr8=dtr8, a8=a8, mask=mask, mask_t=mask_t, cum_t=cum_t, dt_exp=dt_exp, a_exp=a_exp,
                cum_exp=_head_layout(cum_cols), cum_cols=cum_cols, tot_exp=_colsum(a_exp))


def _head_lanes(x, e):
    pair = x[:, 128 * (e // 2):128 * (e // 2 + 1)]
    low = lax.broadcasted_iota(jnp.int32, (1, 128), 1) < SSM_P
    return _bf(jnp.where(low if e % 2 == 0 else jnp.logical_not(low), pair, 0.0))


def _per_head_pairs(fn, x):
    return jnp.concatenate([fn(2 * j, _head_lanes(x, 2 * j)) + fn(2 * j + 1, _head_lanes(x, 2 * j + 1))
                            for j in range(SSM_HPG // 2)], axis=1)


def _ssd_decay(cm, e):
    diff = cm["cum_cols"][e] - cm["cum_t"][e:e + 1, :]
    return jnp.where(cm["mask"], jnp.exp(jnp.minimum(diff, 0.0)), 0.0)


SSM_GPS = 4


def _ssd_specs(n_lat, n_ctx, step_of):
    rbs = [lambda i, d=d: _chunk_row_block(d, step_of(i), n_lat, n_ctx) for d in range(2)]
    xw, nw = SSM_GPS * SSM_GW, SSM_GPS * SSM_N
    specs = []
    for rb in rbs:
        specs += [pl.BlockSpec((SSM_C, xw), lambda g, i, rb=rb: (rb(i), g)),
                  pl.BlockSpec((SSM_C, nw), lambda g, i, rb=rb: (rb(i), SSM_INNER // nw + g)),
                  pl.BlockSpec((SSM_C, nw), lambda g, i, rb=rb: (rb(i), (SSM_INNER + SSM_G * SSM_N) // nw + g)),
                  pl.BlockSpec((SSM_C, 128), lambda g, i, rb=rb: (rb(i), 0))]
    specs += [pl.BlockSpec((2, SSM_GPS, 128, 128), lambda g, i: (0, g, 0, 0)),
              pl.BlockSpec((2, SSM_GPS, 1, 128), lambda g, i: (0, g, 0, 0)),
              pl.BlockSpec((2, SSM_GPS, 1, 128), lambda g, i: (0, g, 0, 0)),
              pl.BlockSpec((2, SSM_GPS, 1, SSM_GW), lambda g, i: (0, g, 0, 0))]
    return specs, rbs


def _ssd_fwd(xbc, sm, k, n_lat, n_ctx, comm):
    R = xbc.shape[0]
    n_steps = n_lat + n_ctx

    def body(*refs):
        ins, (sel_ref, dtb_ref, a_ref, a512_ref), y_refs, hist_ref, st = refs[:8], refs[8:12], refs[12:14], refs[14], refs[15]

        @pl.when(pl.program_id(1) == 0)
        def _():
            st[...] = jnp.zeros_like(st)

        for d in range(2):
            x_ref, b_ref, c_ref, sm_ref = ins[4 * d:4 * d + 4]
            sm = sm_ref[...]
            for gg in range(SSM_GPS):
                sx, sn = slice(gg * SSM_GW, (gg + 1) * SSM_GW), slice(gg * SSM_N, (gg + 1) * SSM_N)
                cm = _ssd_common(sm, sel_ref[d, gg], dtb_ref[d, gg], a_ref[d, gg], a512_ref[d, gg], d)
                bm, cmat = _bf(b_ref[:, sn]), _bf(c_ref[:, sn])
                xdt = x_ref[:, sx].astype(F32) * cm["dt_exp"]
                cb = _nt(cmat, bm)
                ys = _per_head_pairs(lambda e, x_e: _nn(_bf(cb * _ssd_decay(cm, e)), x_e), xdt)
                s_in = st[d, gg]
                hist_ref[d, 0, gg] = s_in
                y = ys + jnp.exp(cm["cum_exp"]) * _nn(cmat, _bf(s_in))
                y_refs[d][:, sx] = y.astype(ACT)
                st[d, gg] = jnp.exp(cm["tot_exp"]) * s_in + _tn(bm, _bf(xdt * jnp.exp(cm["tot_exp"] - cm["cum_exp"])))

    in_specs, rbs = _ssd_specs(n_lat, n_ctx, lambda i: i)
    out_specs = [pl.BlockSpec((SSM_C, SSM_GPS * SSM_GW), lambda g, i, rb=rb: (rb(i), g)) for rb in rbs]
    out_specs += [pl.BlockSpec((2, 1, SSM_GPS, SSM_N, SSM_GW), lambda g, i: (0, i, g, 0, 0))]
    out_shape = [jax.ShapeDtypeStruct((R, SSM_INNER), ACT)] * 2 + [jax.ShapeDtypeStruct((2, n_steps, SSM_G, SSM_N, SSM_GW), F32)]
    args = [xbc, xbc, xbc, sm] * 2 + [k["sel"], k["dtb"], k["a"], k["a512"]]
    n_host_out = len(out_shape)
    outs = pl.pallas_call(
        _hosted(body, len(args), n_host_out, 1, comm, *_ssd_comm_steps(n_steps)), name="ssd_fwd",
        grid=(SSM_G // SSM_GPS, n_steps), in_specs=in_specs + [_ANY] * len(comm.arrays),
        out_specs=out_specs + [_ANY] * len(comm.out_shape), out_shape=out_shape + comm.out_shape,
        scratch_shapes=[pltpu.VMEM((2, SSM_GPS, SSM_N, SSM_GW), F32)] + comm.scratch,
        compiler_params=_cparams(("arbitrary", "arbitrary")),
    )(*args, *comm.arrays)
    return outs[:n_host_out], outs[n_host_out:]


def _ssd_comm_steps(n_steps):
    n_g = SSM_G // SSM_GPS
    at = lambda g, i: (pl.program_id(0) == g) & (pl.program_id(1) == i)
    half = (n_g // 2, n_steps // 2 if n_g % 2 else 0)
    return (lambda: at(0, 0)), (lambda: at(*half)), (lambda: at(n_g - 1, n_steps - 1))


def _ssd_bwd(dy, xbc, sm, k, hist, n_lat, n_ctx, comm):
    R = xbc.shape[0]
    n_steps = n_lat + n_ctx
    step_of = lambda j: n_steps - 1 - j

    def one(d, gg, x_ref, b_ref, c_ref, sm_ref, sel_ref, dtb_ref, a_ref, a512_ref, selt_ref, e512t_ref, dy_ref,
            hist_ref, dx_ref, db_ref, dc_ref, dsm_ref, acc_ref, dst):
        sx, sn = slice(gg * SSM_GW, (gg + 1) * SSM_GW), slice(gg * SSM_N, (gg + 1) * SSM_N)
        a_neg, e512_t = a_ref[d, gg], e512t_ref[...]
        cm = _ssd_common(sm_ref[...], sel_ref[d, gg], dtb_ref[d, gg], a_neg, a512_ref[d, gg], d)
        x, dyv = x_ref[:, sx].astype(F32), dy_ref[:, sx].astype(F32)
        bm, cmat = _bf(b_ref[:, sn]), _bf(c_ref[:, sn])
        xdt = x * cm["dt_exp"]
        cb = _nt(cmat, bm)
        s_in, ds = hist_ref[d, 0, gg], dst[d, gg]
        w = jnp.exp(cm["tot_exp"] - cm["cum_exp"])
        z = _nn(bm, _bf(ds))
        decay_in = jnp.exp(cm["cum_exp"])
        gy = _bf(dyv * decay_in)
        dcb = jnp.zeros((SSM_C, SSM_C), F32)
        crossing = []
        row = lax.broadcasted_iota(jnp.int32, (SSM_C, SSM_C), 0)
        col = lax.broadcasted_iota(jnp.int32, (SSM_C, SSM_C), 1)
        eye = (row == col).astype(BF16)
        before = (cm["mask_t"] - eye)
        xdt_bf = _bf(xdt)

        def head(e, dy_e):
            nonlocal dcb
            lm = _ssd_decay(cm, e)
            m_e = cb * lm
            dm_e = _nt(dy_e, xdt_bf[:, 128 * (e // 2):128 * (e // 2 + 1)])
            dcb = dcb + dm_e * lm
            crossing.append(_bf(dm_e * m_e))
            return _tn(_bf(m_e), dy_e)

        dx_heads = _per_head_pairs(head, dyv)
        through = _nn(jnp.concatenate(crossing, axis=0), before)
        crossing = [_colsum(jnp.where(cm["mask"], through[e * SSM_C:(e + 1) * SSM_C], 0.0)) for e in range(SSM_HPG)]
        da_rows = jnp.concatenate(crossing + [jnp.zeros((128 - SSM_HPG, SSM_C), F32)], axis=0)
        r_hi, r_lo = _split(da_rows)
        da8_intra = _tn(r_hi, eye) + _tn(r_lo, eye)
        dx_state = w * z
        dxdt = dx_heads + dx_state
        dcb = _bf(dcb)
        c_s = _nn(cmat, _bf(s_in))
        dc_ref[:, sn] = _nn(dcb, bm) + _nt(gy, _bf(s_in))
        db_ref[:, sn] = _tn(dcb, cmat) + _nt(_bf(w * xdt), _bf(ds))
        dst[d, gg] = jnp.exp(cm["tot_exp"]) * ds + _tn(cmat, gy)
        state_path = xdt * dx_state
        per_token = _nn_x(jnp.concatenate([dyv * decay_in * c_s - state_path, dxdt * x], axis=0), e512_t)
        totals = jnp.concatenate([_colsum(state_path), _colsum(ds * s_in), jnp.zeros((6, SSM_GW), F32)], axis=0)
        totals = _nn_x(totals, e512_t)
        tot8 = _colsum(cm["a8"])
        dtot8 = totals[0:1] + jnp.exp(tot8) * totals[1:2]
        da8 = da8_intra + _x_nn(cm["mask_t"], per_token[:SSM_C]) + dtot8
        ddt8 = da8 * a_neg + per_token[SSM_C:]
        dsm_ref[gg] = _nn_x(ddt8 * _sigmoid(cm["dtr8"]), selt_ref[d, gg])
        dx_ref[:, sx] = dxdt * cm["dt_exp"]
        acc_ref[d, gg, 0:1, :] += _colsum(da8 * cm["a8"])

    def body(*refs):
        ins, consts, (selt_ref, e512t_ref), dy_refs, hist_ref = refs[:8], refs[8:12], refs[12:14], refs[14:16], refs[16]
        outs, acc_ref, dst = refs[17:25], refs[25], refs[26]

        @pl.when(pl.program_id(1) == 0)
        def _():
            dst[...] = jnp.zeros_like(dst)
            acc_ref[...] = jnp.zeros_like(acc_ref)

        for d in range(2):
            for gg in range(SSM_GPS):
                one(d, gg, *ins[4 * d:4 * d + 4], *consts, selt_ref, e512t_ref, dy_refs[d], hist_ref,
                    *outs[4 * d:4 * d + 4], acc_ref, dst)

    xw, nw = SSM_GPS * SSM_GW, SSM_GPS * SSM_N
    in_specs, rbs = _ssd_specs(n_lat, n_ctx, step_of)
    in_specs += [pl.BlockSpec((2, SSM_GPS, 128, 128), lambda g, j: (0, g, 0, 0)), pl.BlockSpec((SSM_GW, 128), lambda g, j: (0, 0))]
    in_specs += [pl.BlockSpec((SSM_C, xw), lambda g, j, rb=rb: (rb(j), g)) for rb in rbs]
    in_specs += [pl.BlockSpec((2, 1, SSM_GPS, SSM_N, SSM_GW), lambda g, j: (0, step_of(j), g, 0, 0))]
    out_specs, out_shape = [], []
    for rb in rbs:
        out_specs += [pl.BlockSpec((SSM_C, xw), lambda g, j, rb=rb: (rb(j), g)),
                      pl.BlockSpec((SSM_C, nw), lambda g, j, rb=rb: (rb(j), g)),
                      pl.BlockSpec((SSM_C, nw), lambda g, j, rb=rb: (rb(j), g)),
                      pl.BlockSpec((SSM_GPS, SSM_C, 128), lambda g, j, rb=rb: (g, rb(j), 0))]
        out_shape += [jax.ShapeDtypeStruct((R, SSM_INNER), F32), jax.ShapeDtypeStruct((R, SSM_G * SSM_N), F32),
                      jax.ShapeDtypeStruct((R, SSM_G * SSM_N), F32), jax.ShapeDtypeStruct((SSM_G, R, 128), F32)]
    out_specs.append(pl.BlockSpec((2, SSM_GPS, 8, 128), lambda g, j: (0, g, 0, 0)))
    out_shape.append(jax.ShapeDtypeStruct((2, SSM_G, 8, 128), F32))
    args = [xbc, xbc, xbc, sm] * 2 + [k["sel"], k["dtb"], k["a"], k["a512"], k["sel_t"], k["e512_t"], dy, dy, hist]
    n_host_out = len(out_shape)
    outs = pl.pallas_call(
        _hosted(body, len(args), n_host_out, 1, comm, *_ssd_comm_steps(n_steps)), name="ssd_bwd",
        grid=(SSM_G // SSM_GPS, n_steps), in_specs=in_specs + [_ANY] * len(comm.arrays),
        out_specs=out_specs + [_ANY] * len(comm.out_shape), out_shape=out_shape + comm.out_shape,
        scratch_shapes=[pltpu.VMEM((2, SSM_GPS, SSM_N, SSM_GW), F32)] + comm.scratch,
        compiler_params=_cparams(("arbitrary", "arbitrary")),
    )(*args, *comm.arrays)
    return [(outs[n], outs[4 + n]) for n in range(4)] + [outs[8]], outs[n_host_out:]


def _gla_assemble(dq, dk, dv, dparts, tr):
    R = dq[0].shape[0]
    qk = GLA_H * GLA_DK

    def body(dqf_ref, dqb_ref, dkf_ref, dkb_ref, dvf_ref, dvb_ref, _, o_ref):
        o_ref[:, 0:qk] = (_f32(dqf_ref) + _f32(dqb_ref)).astype(BF16)
        o_ref[:, qk:2 * qk] = (_f32(dkf_ref) + _f32(dkb_ref)).astype(BF16)
        o_ref[:, 2 * qk:] = (_f32(dvf_ref) + _f32(dvb_ref)).astype(BF16)

    return pl.pallas_call(
        body, name="gla_assemble", grid=(R // tr,), in_specs=[_row_spec(tr, qk)] * 4 + [_row_spec(tr, D)] * 2 + [_ANY],
        out_specs=_dparts_out(tr, 2 * D, COL_Q), out_shape=jax.ShapeDtypeStruct(dparts.shape, BF16),
        input_output_aliases={6: 0}, compiler_params=_cparams(("arbitrary",)),
    )(*dq, *dk, *dv, dparts)


def _small_assemble(dp, dsm, sm, ut_hi, ut_lo, dparts, tr):
    R = sm.shape[0]
    qk = GLA_H * GLA_DK

    def body(dpf_ref, dpb_ref, dsmf_ref, dsmb_ref, sm_ref, uth_ref, utl_ref, _, o_ref, dup_ref, acc_ref, acc2_ref):
        @pl.when(pl.program_id(0) == 0)
        def _():
            dup_ref[...] = jnp.zeros_like(dup_ref)
            acc_ref[...] = jnp.zeros_like(acc_ref)
            acc2_ref[...] = jnp.zeros_like(acc2_ref)

        ssd = dsmf_ref[0] + dsmb_ref[0]
        for g in range(1, SSM_G):
            ssd = ssd + (dsmf_ref[g] + dsmb_ref[g])
        acc2_ref[0:1, :] += _colsum(ssd)
        sm_hi, sm_lo = _split(sm_ref[...])
        out = ssd
        for d, dp_ref in enumerate((dpf_ref, dpb_ref)):
            dpd = dp_ref[...]
            out = out + _nn3(dpd, uth_ref[d], utl_ref[d])
            p_hi, p_lo = _split(dpd)
            dup_ref[d] += _tn(sm_hi, p_hi) + _tn(sm_lo, p_hi) + _tn(sm_hi, p_lo)
            acc_ref[d:d + 1, :] += _colsum(dpd)
        o_ref[...] = out.astype(BF16)

    return pl.pallas_call(
        body, name="small_assemble", grid=(R // tr,),
        in_specs=[_row_spec(tr, qk)] * 2 + [pl.BlockSpec((SSM_G, tr, 128), lambda i: (0, i, 0))] * 2
        + [_row_spec(tr, 128), pl.BlockSpec((2, qk, 128), lambda i: (0, 0, 0)),
           pl.BlockSpec((2, qk, 128), lambda i: (0, 0, 0)), _ANY],
        out_specs=[_dparts_out(tr, 128, COL_SM), pl.BlockSpec((2, 128, qk), lambda i: (0, 0, 0)), _acc_spec(qk), _acc_spec(128)],
        out_shape=[jax.ShapeDtypeStruct(dparts.shape, BF16), jax.ShapeDtypeStruct((2, 128, qk), F32),
                   jax.ShapeDtypeStruct((8, qk), F32), jax.ShapeDtypeStruct((8, 128), F32)],
        input_output_aliases={7: 0}, compiler_params=_cparams(("arbitrary",)),
    )(*dp, *dsm, sm, ut_hi, ut_lo, dparts)


ADA_ROWS = 16
ADA_TILE = 512


def _dot3_f32(a, b, ca, cb):
    a_hi, a_lo = _split(a)
    b_hi, b_lo = _split(b)
    return _dg(a_hi, b_hi, ca, cb) + _dg(a_lo, b_hi, ca, cb) + _dg(a_hi, b_lo, ca, cb)


def _ada_fwd(cvec, w, b):
    cols = w.shape[1]

    def body(c_ref, w_ref, b_ref, o_ref):
        o_ref[...] = _dot3_f32(_silu(c_ref[...]), w_ref[...], 1, 0) + b_ref[...]

    return pl.pallas_call(
        body, name="ada_fwd", grid=(cols // ADA_TILE,),
        in_specs=[pl.BlockSpec((ADA_ROWS, D), lambda j: (0, 0)), pl.BlockSpec((D, ADA_TILE), lambda j: (0, j)),
                  pl.BlockSpec((1, ADA_TILE), lambda j: (0, j))],
        out_specs=pl.BlockSpec((ADA_ROWS, ADA_TILE), lambda j: (0, j)), out_shape=jax.ShapeDtypeStruct((ADA_ROWS, cols), F32),
        compiler_params=_cparams(("arbitrary",)),
    )(cvec, w, b)


def _adam(w, g, m, v):
    m2 = ADAM_B1 * m + (1.0 - ADAM_B1) * g
    v2 = ADAM_B2 * v + (1.0 - ADAM_B2) * (g * g)
    m_hat = m2 / (1.0 - ADAM_B1 ** ADAM_STEP)
    v_hat = v2 / (1.0 - ADAM_B2 ** ADAM_STEP)
    return -ADAM_LR * (m_hat / (jnp.sqrt(v_hat) + ADAM_EPS) + ADAM_WD * w), m2, v2


def _wada_bwd_adam(cvec, dada, w, m, v):
    rows, cols = w.shape
    tr = _tile(rows, 256, 128)

    def body(c_ref, d_ref, w_ref, m_ref, v_ref, g_ref, dl_ref, m2_ref, v2_ref, p_ref):
        wv = w_ref[...]
        g = _dot3_f32(_silu(c_ref[...]), d_ref[...], 0, 0)
        g_ref[...] = g
        dl_ref[...], m2_ref[...], v2_ref[...] = _adam(wv, g, m_ref[...], v_ref[...])
        p_ref[...] = _dot3_f32(d_ref[...], wv, 1, 1)

    blk = pl.BlockSpec((tr, cols), lambda i: (i, 0))
    return pl.pallas_call(
        body, name="wada_bwd_adam", grid=(rows // tr,),
        in_specs=[pl.BlockSpec((ADA_ROWS, tr), lambda i: (0, i)), pl.BlockSpec((ADA_ROWS, cols), lambda i: (0, 0)), blk, blk, blk],
        out_specs=[blk, blk, blk, blk, pl.BlockSpec((ADA_ROWS, tr), lambda i: (0, i))],
        out_shape=[jax.ShapeDtypeStruct((rows, cols), F32)] * 4 + [jax.ShapeDtypeStruct((ADA_ROWS, rows), F32)],
        compiler_params=_cparams(("arbitrary",)),
    )(cvec, dada, w, m, v)


def _reduce_adam(parts8, w, m, v, name):
    rows, cols = w.shape
    tr = _tile(rows, 64, 16)

    def body(p_ref, w_ref, m_ref, v_ref, g_ref, dl_ref, m2_ref, v2_ref):
        g = p_ref[0].astype(F32) + p_ref[N_CHIPS].astype(F32)
        for j in range(1, N_CHIPS):
            g = g + (p_ref[j].astype(F32) + p_ref[N_CHIPS + j].astype(F32))
        g_ref[...] = g
        dl_ref[...], m2_ref[...], v2_ref[...] = _adam(w_ref[...], g, m_ref[...], v_ref[...])

    blk = pl.BlockSpec((tr, cols), lambda i: (i, 0))
    return pl.pallas_call(
        body, name=name, grid=(rows // tr,), in_specs=[pl.BlockSpec((N_DEV, tr, cols), lambda i: (0, i, 0)), blk, blk, blk],
        out_specs=[blk] * 4, out_shape=[jax.ShapeDtypeStruct((rows, cols), F32)] * 4, compiler_params=_cparams(("arbitrary",)),
    )(parts8, w, m, v)


SMALL_W = 1024


def _sum8(g8):
    rows = g8.shape[1]

    def body(g_ref, o_ref):
        s = g_ref[0]
        for j in range(1, N_DEV):
            s = s + g_ref[j]
        o_ref[...] = s

    return pl.pallas_call(
        body, name="sum8", out_shape=jax.ShapeDtypeStruct((rows, SMALL_W), F32),
        in_specs=[pl.BlockSpec(memory_space=pltpu.VMEM)], out_specs=pl.BlockSpec(memory_space=pltpu.VMEM),
        compiler_params=pltpu.CompilerParams(vmem_limit_bytes=VMEM_LIMIT),
    )(g8)


def _cctx_grad(p8, c_ctx):
    def body(p_ref, c_ref, o_ref):
        s = p_ref[0]
        for chip in range(1, N_CHIPS):
            s = s + p_ref[2 * chip]
        o_ref[...] = s * _dsilu(c_ref[...])

    return pl.pallas_call(
        body, name="cctx_grad", out_shape=jax.ShapeDtypeStruct((1, D), F32),
        in_specs=[pl.BlockSpec(memory_space=pltpu.VMEM)] * 2, out_specs=pl.BlockSpec(memory_space=pltpu.VMEM),
    )(p8, c_ctx)


def _adam_small(w, g, m, v):
    def body(w_ref, g_ref, m_ref, v_ref, dl_ref, m2_ref, v2_ref):
        dl_ref[...], m2_ref[...], v2_ref[...] = _adam(w_ref[...], g_ref[...], m_ref[...], v_ref[...])

    vm = pl.BlockSpec(memory_space=pltpu.VMEM)
    return pl.pallas_call(
        body, name="adam_small", out_shape=[jax.ShapeDtypeStruct(w.shape, F32)] * 3, in_specs=[vm] * 4, out_specs=[vm] * 3,
        compiler_params=pltpu.CompilerParams(vmem_limit_bytes=VMEM_LIMIT),
    )(w, g, m, v)


def _pack(vecs, width=SMALL_W, row_mult=8):
    flat = jnp.concatenate([v.reshape(-1).astype(F32) for v in vecs])
    n = flat.shape[0]
    rows = -(-n // (width * row_mult)) * row_mult
    return jnp.pad(flat, (0, rows * width - n)).reshape(rows, width)


def _unpack(packed, shapes):
    flat = packed.reshape(-1)
    out, off = [], 0
    for s in shapes:
        n = int(np.prod(s))
        out.append(flat[off:off + n].reshape(s))
        off += n
    return out


WEIGHTS = ('c_ctx', 'w_ada', 'b_ada', 'norm1_w', 'w_in', 'gla_up_f', 'gla_bias_f', 'gla_up_b', 'gla_bias_b', 'gla_norm_w',
           'conv_w', 'conv_b', 'dt_bias_f', 'dt_bias_b', 'a_log_f', 'a_log_b', 'd_skip', 'ssm_norm_w', 'w_pa', 'w_pb', 'w_out',
           'norm2_w', 'w_gate', 'w_up', 'w_down', 'final_norm_w')
BIG = ('w_in', 'w_pa', 'w_pb', 'w_out', 'w_gate', 'w_up', 'w_down')
COL_SHARDED = ('w_in', 'w_gate', 'w_up')
SMALL_SHARDED = ('gla_up_f', 'gla_up_b', 'conv_w')
ROW_TILE = 256
MM_ROWS = 1408
MM_ROWS_SMALL = 768
MM_KROWS = 2816


def _blocks_to_full(g4, name):
    n, r, c = g4.shape
    return g4.transpose(1, 0, 2).reshape(r, n * c) if name in COL_SHARDED else g4.reshape(n * r, c)


def _full_to_blocks(full, name):
    r, c = full.shape
    if name in COL_SHARDED:
        return full.reshape(r, N_CHIPS, c // N_CHIPS).transpose(1, 0, 2)
    return full.reshape(N_CHIPS, r // N_CHIPS, c)


def _permute_in(w_in_full):
    off = np.concatenate([[0], np.cumsum(IN_WIDTHS)])
    cols = [w_in_full[:, off[p]:off[p + 1]] for p in PERM]
    return jnp.concatenate(cols + [jnp.zeros((w_in_full.shape[0], SMALL_PAD), w_in_full.dtype)], axis=1)


def _unpermute_in(wp):
    off = np.concatenate([[0], np.cumsum([IN_WIDTHS[p] for p in PERM])])
    pieces = {p: wp[:, off[i]:off[i + 1]] for i, p in enumerate(PERM)}
    return jnp.concatenate([pieces[p] for p in range(len(IN_WIDTHS))], axis=1)


def _chip_cols(full, chip, n):
    return lax.dynamic_slice_in_dim(full, chip * n, n, axis=1)


def kernel(x, c, ctx, c_ctx, w_ada, b_ada, norm1_w, w_in, gla_up_f, gla_bias_f, gla_up_b, gla_bias_b, gla_norm_w, conv_w, conv_b, dt_bias_f, dt_bias_b, a_log_f, a_log_b, d_skip, ssm_norm_w, w_pa, w_pb, w_out, norm2_w, w_gate, w_up, w_down, final_norm_w, loss_target, m_c_ctx, m_w_ada, m_b_ada, m_norm1_w, m_w_in, m_gla_up_f, m_gla_bias_f, m_gla_up_b, m_gla_bias_b, m_gla_norm_w, m_conv_w, m_conv_b, m_dt_bias_f, m_dt_bias_b, m_a_log_f, m_a_log_b, m_d_skip, m_ssm_norm_w, m_w_pa, m_w_pb, m_w_out, m_norm2_w, m_w_gate, m_w_up, m_w_down, m_final_norm_w, v_c_ctx, v_w_ada, v_b_ada, v_norm1_w, v_w_in, v_gla_up_f, v_gla_bias_f, v_gla_up_b, v_gla_bias_b, v_gla_norm_w, v_conv_w, v_conv_b, v_dt_bias_f, v_dt_bias_b, v_a_log_f, v_a_log_b, v_d_skip, v_ssm_norm_w, v_w_pa, v_w_pb, v_w_out, v_norm2_w, v_w_gate, v_w_up, v_w_down, v_final_norm_w):
    given = dict(locals())
    W = {n: given[n] for n in WEIGHTS}
    M = {n: given["m_" + n] for n in WEIGHTS}
    V = {n: given["v_" + n] for n in WEIGHTS}
    L, Lc = x.shape[1], ctx.shape[1]
    tr = ROW_TILE
    assert L % tr == 0 and Lc % tr == 0 and L % Lc == 0 and Lc % SSM_C == 0
    n_lat_tiles = L // tr
    xi, yi, ci = _place()
    chip, me = 2 * xi + yi, 4 * xi + 2 * yi + ci
    x2, ctx2 = x[0], ctx[0]

    g0 = _allgather_small(_pack([c[0]] + [W[n][0] for n in SMALL_SHARDED]), "gather_c")
    g0 = g0.reshape(N_DEV, -1)
    c_all = g0[:, :D]
    small_full, off = {}, D
    for n in SMALL_SHARDED:
        r, cols = W[n].shape[1:]
        small_full[n] = jnp.concatenate([g0[2 * k, off:off + r * cols].reshape(r, cols) for k in range(N_CHIPS)], axis=1)
        off += r * cols
    up_f, up_b, conv_w_full = (small_full[n] for n in SMALL_SHARDED)

    cvec = jnp.zeros((ADA_ROWS, D), F32).at[:N_DEV].set(c_all).at[N_DEV].set(c_ctx)
    ada_cols = w_ada.shape[2]
    ada_part = _ada_fwd(cvec, w_ada[0], _chip_cols(b_ada, chip, ada_cols))
    g1_all = _allgather_small(ada_part, "gather_ada")
    ada_full = jnp.concatenate([g1_all[2 * k] for k in range(N_CHIPS)], axis=1)
    mine = lax.dynamic_slice_in_dim(ada_full, me, 1, axis=0)
    sh1, sc1, g1, sh2, sc2, g2 = (mine[:, k * D:(k + 1) * D] for k in range(6))
    csh1, csc1 = ada_full[N_DEV:N_DEV + 1, :D], ada_full[N_DEV:N_DEV + 1, D:2 * D]
    mod = jnp.stack([jnp.stack([sh1, sc1]), jnp.stack([csh1, csc1])])

    later = [n for n in BIG if n != 'w_in']

    def lr_rows(up, base):
        return jnp.zeros((128, GLA_H * GLA_DK), F32).at[base:base + GLA_RANK].set(up)
    u2 = jnp.stack([lr_rows(up_f, SM_LRF), lr_rows(up_b, SM_LRB)])
    u2_hi = u2.astype(BF16)
    u2_lo = (u2 - u2_hi.astype(F32)).astype(BF16)
    ut = u2.transpose(0, 2, 1)
    ut_hi = ut.astype(BF16)
    ut_lo = (ut - ut_hi.astype(F32)).astype(BF16)
    gbias = jnp.stack([gla_bias_f, gla_bias_b])
    kc = _ssd_consts(jnp.stack([dt_bias_f[0], dt_bias_b[0]]), jnp.stack([a_log_f[0], a_log_b[0]]))
    gw4 = jnp.tile(gla_norm_w, (1, GLA_H))
    dskip_exp = jnp.repeat(d_skip, SSM_P, axis=1)
    n_gla = (L // GLA_C, Lc // GLA_C)
    n_ssd = (L // SSM_C, Lc // SSM_C)

    h1, (w_in_blocks,) = _norm_mod(x2, ctx2, norm1_w, mod, n_lat_tiles, tr, _gather_split_comm(w_in[0].astype(BF16)))
    full = {'w_in': _blocks_to_full(w_in_blocks, 'w_in')}
    wp = _permute_in(full['w_in'])
    parts = _mm(h1, wp, "nn", ACT, "mm_in", tm=MM_ROWS, tn=PW // 3)
    sm = _mm(h1, wp[:, COL_SM:], "nn", F32, "mm_in_small", tm=MM_ROWS)
    xbc = _conv_fwd(parts, conv_w_full, conv_b, L // Lc, Lc)
    *o2, gla_hist = _gla_fwd(parts, sm, u2_hi, u2_lo, gbias, *n_gla)
    (*y2, ssd_hist), gathered = _ssd_fwd(xbc, sm, kc, *n_ssd, _gather_comm([W[n][0].astype(BF16) for n in later]))
    full.update({n: _blocks_to_full(g, n) for n, g in zip(later, gathered)})
    oan = _gla_out(o2, parts, gw4, tr)
    obn = _ssd_out(y2, xbc, parts, dskip_exp, ssm_norm_w, tr)
    ya = _mm(oan, full['w_pa'], "nn", ACT, "mm_pa", tm=MM_ROWS)
    yb = _mm(obn, full['w_pb'], "nn", ACT, "mm_pb", tm=MM_ROWS)
    merged = _merge(ya, yb, parts, tr)
    mix = _mm(merged, full['w_out'], "nn", ACT, "mm_out", tm=MM_ROWS)
    h2, u = _resid_norm_mod(x2, ctx2, mix, g1, norm2_w, sh2, sc2, n_lat_tiles, tr)
    gp = _mm(u, full['w_gate'], "nn", ACT, "mm_gate", tm=MM_ROWS, tn=D_FF)
    up = _mm(u, full['w_up'], "nn", ACT, "mm_up", tm=MM_ROWS, tn=D_FF)
    f = _mm(gp, full['w_down'], "nn", ACT, "mm_down", tm=MM_ROWS_SMALL, tk=D_FF // 2, swiglu_up=up)
    dh3, df, acc_loss = _loss_head(h2, f, loss_target[0], g2, final_norm_w[None], n_lat_tiles, tr)

    dw = {}
    da = _mm(df, full['w_down'], "nt", ACT, "mm_down_dx", tm=MM_ROWS, tn=D_FF)
    dw['w_down'] = _mm(gp, df, "tn", BF16, "mm_down_dw", tm=D_FF // 2, tk=MM_ROWS, swiglu_up=up)
    dgp, dup = _swiglu_act_bwd(da, gp, up, tr)
    du_a = _mm(dgp, full['w_gate'], "nt", ACT, "mm_gate_dx", tm=MM_ROWS, tk=D_FF)
    du_b = _mm(dup, full['w_up'], "nt", ACT, "mm_up_dx", tm=MM_ROWS, tk=D_FF)
    dw['w_gate'] = _mm(u, dgp, "tn", BF16, "mm_gate_dw", tm=D, tn=D_FF // 2, tk=MM_KROWS)
    dw['w_up'] = _mm(u, dup, "tn", BF16, "mm_up_dw", tm=D, tn=D_FF // 2, tk=MM_KROWS)
    dh2, dmix, acc_ffn = _ffn_in_bwd(du_a, du_b, h2, dh3, mix, sc2, g1, norm2_w, tr)
    dmerged = _mm(dmix, full['w_out'], "nt", ACT, "mm_out_dx", tm=MM_ROWS)
    dw['w_out'] = _mm(merged, dmix, "tn", BF16, "mm_out_dw", tm=D, tk=MM_KROWS)
    dya, dyb, dparts = _merge_bwd(dmerged, ya, yb, parts, lax.empty((L + Lc, PW), BF16), tr)
    doan = _mm(dya, full['w_pa'], "nt", ACT, "mm_pa_dx", tm=MM_ROWS)
    dw['w_pa'] = _mm(oan, dya, "tn", BF16, "mm_pa_dw", tm=D, tk=MM_KROWS)
    dobn = _mm(dyb, full['w_pb'], "nt", ACT, "mm_pb_dx", tm=MM_ROWS)
    dw['w_pb'] = _mm(obn, dyb, "tn", BF16, "mm_pb_dw", tm=D, tk=MM_KROWS)
    do, dparts, acc_gla = _gla_out_bwd(doan, o2, parts, gw4, dparts, tr)
    dq, dk, dv, dpre = _gla_bwd(do, parts, sm, u2_hi, u2_lo, gbias, gla_hist, *n_gla)
    dy, dparts, acc_ssd = _ssd_out_bwd(dobn, y2, xbc, parts, dskip_exp, ssm_norm_w, dparts, tr)
    (dx_scan, db_scan, dc_scan, dsm, acc_alog), exchanged = _ssd_bwd(
        dy, xbc, sm, kc, ssd_hist, *n_ssd, _exchange_comm([_full_to_blocks(dw[n], n) for n in later]))
    exchanged = dict(zip(later, exchanged))
    dparts, acc_conv = _conv_bwd(dx_scan, db_scan, dc_scan, dy, dskip_exp, parts, conv_w_full, conv_b, dparts, L // Lc, Lc)
    dparts = _gla_assemble(dq, dk, dv, dparts, tr)
    dparts, dup_gla, acc_gbias, acc_dtb = _small_assemble(dpre, dsm, sm, ut_hi, ut_lo, dparts, tr)
    dw['w_in'] = _unpermute_in(_mm(h1, dparts, "tn", BF16, "mm_in_dw", tm=D, tn=PW // 9, tk=MM_KROWS))
    dh1, (exchanged['w_in'],) = _mm(dparts, wp, "nt", F32, "mm_in_dx", tm=MM_ROWS_SMALL, tk=PW // 3,
                                    comm=_exchange_comm([_full_to_blocks(dw['w_in'], 'w_in')]))
    dx, acc_n1 = _norm1_bwd(dh1, x2, ctx2, dh2, norm1_w, mod, n_lat_tiles, tr)

    partial = dict(
        norm1_w=acc_n1[0, 2] + acc_n1[1, 2],
        gla_up_f=dup_gla[0, SM_LRF:SM_LRF + GLA_RANK], gla_bias_f=acc_gbias[0],
        gla_up_b=dup_gla[1, SM_LRB:SM_LRB + GLA_RANK], gla_bias_b=acc_gbias[1],
        gla_norm_w=acc_gla[0].reshape(GLA_H, GLA_DV).sum(0),
        conv_w=acc_conv[:SSM_CONV], conv_b=acc_conv[SSM_CONV],
        dt_bias_f=acc_dtb[0, SM_DTF:SM_DTF + SSM_HEADS], dt_bias_b=acc_dtb[0, SM_DTB:SM_DTB + SSM_HEADS],
        a_log_f=acc_alog[0, :, 0, :SSM_HPG], a_log_b=acc_alog[1, :, 0, :SSM_HPG],
        d_skip=acc_ssd[1].reshape(SSM_HEADS, SSM_P).sum(1), ssm_norm_w=acc_ssd[0],
        norm2_w=acc_ffn[2], final_norm_w=acc_loss[0],
    )
    dada = jnp.concatenate([acc_n1[0, 1], acc_n1[0, 0], acc_ffn[3], acc_ffn[1], acc_ffn[0], acc_loss[1]])
    dada_ctx = jnp.concatenate([acc_n1[1, 1], acc_n1[1, 0], jnp.zeros((4 * D,), F32)])
    names = list(partial)
    payload = [partial[n] for n in names] + [dada + dada_ctx, dada_ctx, acc_loss[2], dada]
    sizes = [int(np.prod(p.shape)) for p in payload]
    g8 = _allgather_small(_pack(payload), "gather_small_grads")
    summed = _unpack(_sum8(g8), [(s,) for s in sizes])
    grads = {n: s.reshape(W[n].shape if n not in SMALL_SHARDED else partial[n].shape) for n, s in zip(names, summed)}
    grads['b_ada'] = summed[len(names)].reshape(b_ada.shape)
    dada_ctx_sum = summed[len(names) + 1]
    loss = jnp.sum(summed[len(names) + 2])
    dada_all = g8.reshape(N_DEV, -1)[:, sum(sizes[:-1]):sum(sizes)]

    dada16 = jnp.zeros((ADA_ROWS, ada_cols), F32)
    dada16 = dada16.at[:N_DEV].set(_chip_cols(dada_all, chip, ada_cols)).at[N_DEV].set(_chip_cols(dada_ctx_sum[None], chip, ada_cols)[0])
    g_wada, dl_wada, m_wada, v_wada, p16 = _wada_bwd_adam(cvec, dada16, w_ada[0], m_w_ada[0], v_w_ada[0])
    p8 = _allgather_small(p16[N_DEV:], "gather_cctx")
    grads['c_ctx'] = _cctx_grad(p8[:, 0:1, :], c_ctx[None])[0]
    for n in SMALL_SHARDED:
        grads[n] = _chip_cols(grads[n], chip, W[n].shape[2])[None]

    small = [n for n in WEIGHTS if n not in BIG and n != 'w_ada']
    shapes = [W[n].shape for n in small]
    dl_s, m_s, v_s = _adam_small(*[_pack([d[n] for n in small]) for d in (W, grads, M, V)])
    delta = dict(zip(small, _unpack(dl_s, shapes)))
    new_m = dict(zip(small, _unpack(m_s, shapes)))
    new_v = dict(zip(small, _unpack(v_s, shapes)))
    grads['w_ada'], delta['w_ada'], new_m['w_ada'], new_v['w_ada'] = g_wada[None], dl_wada[None], m_wada[None], v_wada[None]

    for n in BIG:
        g, dl, m2, v2 = _reduce_adam(exchanged[n], W[n][0], M[n][0], V[n][0], "adam_" + n)
        grads[n], delta[n], new_m[n], new_v[n] = g[None], dl[None], m2[None], v2[None]

    return (loss, dx[None], *[grads[n] for n in WEIGHTS], *[delta[n] for n in WEIGHTS],
            *[new_m[n] for n in WEIGHTS], *[new_v[n] for n in WEIGHTS])
```

```python
import numpy as np
import jax
import jax.numpy as jnp
from jax import lax
from jax.experimental import pallas as pl
from jax.experimental.pallas import tpu as pltpu

F32 = jnp.float32
BF16 = jnp.bfloat16
ACT = BF16
MESH = pl.DeviceIdType.MESH

D = 1024
EPS = 1e-6
GRID_W = 64
GLA_H, GLA_DK, GLA_DV, GLA_RANK, GLA_TAU = 4, 128, 256, 16, 16.0
GLA_C = 128
SSM_INNER, SSM_P, SSM_HEADS, SSM_G, SSM_HPG, SSM_N = 2048, 64, 32, 4, 8, 128
SSM_C = 128
SSM_CONV, CONV_LEFT = 4, 2
D_FF = 2816
IN_WIDTHS = (512, 512, 1024, 1024, 16, 16, 2048, 2048, 512, 512, 32, 32, 1024, 1024)
D_IN = sum(IN_WIDTHS)
PERM = (7, 8, 9, 3, 6, 0, 1, 2, 12, 13, 4, 5, 10, 11)
PW = 10368
SMALL_PAD = PW - D_IN
COL_XBC, COL_R, COL_Z, COL_Q, COL_K, COL_V, COL_GA, COL_GB, COL_SM = 0, 3072, 4096, 6144, 6656, 7168, 8192, 9216, 10240
SM_LRF, SM_LRB, SM_DTF, SM_DTB = 0, 16, 32, 64
EXP_CLAMP = 80.0
ADAM_LR, ADAM_B1, ADAM_B2, ADAM_EPS, ADAM_WD, ADAM_STEP = 0.001, 0.9, 0.999, 1e-08, 0.01, 10
N_CHIPS, N_DEV = 4, 8
VMEM_LIMIT = 56 * 1024 * 1024


def _cparams(sem=None):
    return pltpu.CompilerParams(dimension_semantics=sem, vmem_limit_bytes=VMEM_LIMIT)


def _dg(a, b, ca, cb):
    return lax.dot_general(a, b, (((ca,), (cb,)), ((), ())), preferred_element_type=F32)


def _nn(a, b):
    return _dg(a, b, 1, 0)


def _nt(a, b):
    return _dg(a, b, 1, 1)


def _tn(a, b):
    return _dg(a, b, 0, 0)


def _bf(x):
    return x.astype(BF16)


def _f32(ref):
    return ref[...].astype(F32)


def _split(x):
    hi = x.astype(BF16)
    return hi, (x - hi.astype(F32)).astype(BF16)


def _nn_x(a, b_exact):
    hi, lo = _split(a)
    return _nn(hi, b_exact) + _nn(lo, b_exact)


def _x_nn(a_exact, b):
    hi, lo = _split(b)
    return _nn(a_exact, hi) + _nn(a_exact, lo)


def _nn3(a, b_hi, b_lo):
    hi, lo = _split(a)
    return _nn(hi, b_hi) + _nn(lo, b_hi) + _nn(hi, b_lo)


def _sigmoid(x):
    return 1.0 / (1.0 + jnp.exp(-x))


def _silu(x):
    return x * _sigmoid(x)


def _dsilu(x):
    s = _sigmoid(x)
    return s * (1.0 + x * (1.0 - s))


def _softplus(x):
    return jnp.maximum(x, 0.0) + jnp.log(1.0 + jnp.exp(-jnp.abs(x)))


def _log_sigmoid(x):
    return jnp.minimum(x, 0.0) - jnp.log(1.0 + jnp.exp(-jnp.abs(x)))


def _tile(n, target, mult=8):
    best = None
    for t in range(mult, min(n, target) + 1, mult):
        if n % t == 0:
            best = t
    assert best is not None, (n, target, mult)
    return best


def _mm(a, b, mode, out_dtype, name, tm=512, tn=1024, tk=2048, comm=None, swiglu_up=None):
    if mode == "nn":
        (M, K), N = a.shape, b.shape[1]
    elif mode == "nt":
        (M, K), N = a.shape, b.shape[0]
    else:
        (K, M), N = a.shape, b.shape[1]
    tm, tn, tk = _tile(M, tm, 128), _tile(N, tn, 128), _tile(K, tk, 128)
    nk = K // tk
    ca, cb = {"nn": (1, 0), "nt": (1, 1), "tn": (0, 0)}[mode]

    def body(a_ref, *rest):
        if swiglu_up is None:
            (b_ref, o_ref, *acc), av = rest, a_ref[...]
        else:
            (up_ref, b_ref, o_ref, *acc) = rest
            av = (_silu(a_ref[...].astype(F32)) * up_ref[...].astype(F32)).astype(BF16)
        part = _dg(av, b_ref[...], ca, cb)
        if nk == 1:
            o_ref[...] = part.astype(out_dtype)
        else:
            k = pl.program_id(2)

            @pl.when(k == 0)
            def _():
                acc[0][...] = part

            @pl.when(k > 0)
            def _():
                acc[0][...] += part

            @pl.when(k == nk - 1)
            def _():
                o_ref[...] = acc[0][...].astype(out_dtype)

    a_spec = pl.BlockSpec((tk, tm), lambda i, j, k: (k, i)) if mode == "tn" else pl.BlockSpec((tm, tk), lambda i, j, k: (i, k))
    b_spec = pl.BlockSpec((tn, tk), lambda i, j, k: (j, k)) if mode == "nt" else pl.BlockSpec((tk, tn), lambda i, j, k: (k, j))
    gi, gj = M // tm, N // tn
    scratch = [pltpu.VMEM((tm, tn), F32)] if nk > 1 else []
    out_spec, out_shape = pl.BlockSpec((tm, tn), lambda i, j, k: (i, j)), jax.ShapeDtypeStruct((M, N), out_dtype)
    if comm is None:
        lhs = [a] if swiglu_up is None else [a, swiglu_up]
        return pl.pallas_call(
            body, name=name, grid=(gi, gj, nk), in_specs=[a_spec] * len(lhs) + [b_spec], out_specs=out_spec,
            out_shape=out_shape, scratch_shapes=scratch, compiler_params=_cparams(("arbitrary", "arbitrary", "arbitrary")),
        )(*lhs, b)
    assert swiglu_up is None
    at = lambda i, j, k: (pl.program_id(0) == i) & (pl.program_id(1) == j) & (pl.program_id(2) == k)
    hosted = _hosted(body, 2, 1, len(scratch), comm, lambda: at(0, 0, 0), lambda: at(gi - 1, 0, 0),
                     lambda: at(gi - 1, gj - 1, nk - 1))
    outs = pl.pallas_call(
        hosted, name=name, grid=(gi, gj, nk), in_specs=[a_spec, b_spec] + [_ANY] * len(comm.arrays),
        out_specs=[out_spec] + [_ANY] * len(comm.out_shape), out_shape=[out_shape] + comm.out_shape,
        scratch_shapes=scratch + comm.scratch, compiler_params=_cparams(("arbitrary", "arbitrary", "arbitrary")),
    )(a, b, *comm.arrays)
    return outs[0], outs[1:]


def _place():
    return lax.axis_index("x"), lax.axis_index("y"), lax.axis_index("c")


def _flip(v, bit):
    return 1 - v if bit else v


def _allgather_small(v, name):
    R, C = v.shape

    def body(v_ref, out_ref, send_sems, recv_sems, local_sem):
        x, y, c = _place()
        me = 4 * x + 2 * y + c
        mine = pltpu.make_async_copy(v_ref, out_ref.at[me], local_sem)
        mine.start()

        def peer(r):
            return _flip(x, (r >> 2) & 1), _flip(y, (r >> 1) & 1), _flip(c, r & 1)

        sends = [pltpu.make_async_remote_copy(
            src_ref=v_ref, dst_ref=out_ref.at[me], send_sem=send_sems.at[r - 1], recv_sem=recv_sems.at[r - 1],
            device_id=peer(r), device_id_type=MESH) for r in range(1, N_DEV)]
        for cp in sends:
            cp.start()
        for r in range(1, N_DEV):
            px, py, pc = peer(r)
            pltpu.make_async_remote_copy(
                src_ref=v_ref, dst_ref=out_ref.at[4 * px + 2 * py + pc], send_sem=send_sems.at[r - 1],
                recv_sem=recv_sems.at[r - 1], device_id=(x, y, c), device_id_type=MESH).wait_recv()
        for cp in sends:
            cp.wait_send()
        mine.wait()

    return pl.pallas_call(
        body, name=name, out_shape=jax.ShapeDtypeStruct((N_DEV, R, C), v.dtype),
        in_specs=[pl.BlockSpec(memory_space=pltpu.VMEM)], out_specs=pl.BlockSpec(memory_space=pltpu.VMEM),
        scratch_shapes=[pltpu.SemaphoreType.DMA((N_DEV - 1,)), pltpu.SemaphoreType.DMA((N_DEV - 1,)), pltpu.SemaphoreType.DMA],
        compiler_params=pltpu.CompilerParams(vmem_limit_bytes=VMEM_LIMIT),
    )(v)


_CHIP_RELATIONS = ((1, 0), (0, 1), (1, 1))


class _Comm:
    def __init__(self, arrays, out_shape, scratch, start, middle, finish):
        self.arrays, self.out_shape, self.scratch = arrays, out_shape, scratch
        self.start, self.middle, self.finish = start, middle, finish


def _hosted(body, n_in, n_out, n_scratch, comm, first, middle, last):
    nc, no = len(comm.arrays), len(comm.out_shape)

    def wrapped(*refs):
        a = n_in + nc
        b = a + n_out + no
        ins, c_ins, outs, c_outs = refs[:n_in], refs[n_in:a], refs[a:a + n_out], refs[a + n_out:b]
        scratch, c_sems = refs[b:b + n_scratch], refs[b + n_scratch:]

        @pl.when(first())
        def _():
            comm.start(c_ins, c_outs, c_sems)

        body(*ins, *outs, *scratch)
        if comm.middle is not None:
            @pl.when(middle())
            def _():
                comm.middle(c_ins, c_outs, c_sems)

        @pl.when(last())
        def _():
            comm.finish(c_ins, c_outs, c_sems)

    return wrapped


_ANY = pl.BlockSpec(memory_space=pl.ANY)


def _gather_split_comm(shard):
    rows, cols = shard.shape
    half = rows // 2

    def copies(kind, ins, outs, sems):
        (in_ref,), (out_ref,), (send_sems, recv_sems, local_sem) = ins, outs, sems
        x, y, c = _place()
        chip = 2 * x + y
        mine = pl.ds(pl.multiple_of(c * half, 16), half)
        if kind == "local":
            return [pltpu.make_async_copy(in_ref, out_ref.at[chip], local_sem)]
        made = []
        for j, (fx, fy) in enumerate(_CHIP_RELATIONS):
            px, py = _flip(x, fx), _flip(y, fy)
            landed = out_ref.at[2 * px + py, mine]
            src, dst, k, to = {
                "send": (in_ref.at[mine], out_ref.at[chip, mine], j, (px, py, c)),
                "landed": (landed, landed, j, (x, y, c)),
                "passed": (landed, landed, 3 + j, (x, y, 1 - c)),
                "arrivals": (landed, landed, 3 + j, (x, y, c)),
            }[kind]
            made.append(pltpu.make_async_remote_copy(src_ref=src, dst_ref=dst, send_sem=send_sems.at[k],
                                                     recv_sem=recv_sems.at[k], device_id=to, device_id_type=MESH))
        return made

    def start(ins, outs, sems):
        for cp in copies("local", ins, outs, sems) + copies("send", ins, outs, sems):
            cp.start()

    def middle(ins, outs, sems):
        for got, fwd in zip(copies("landed", ins, outs, sems), copies("passed", ins, outs, sems)):
            got.wait_recv()
            fwd.start()

    def finish(ins, outs, sems):
        for cp in copies("arrivals", ins, outs, sems):
            cp.wait_recv()
        for cp in copies("send", ins, outs, sems) + copies("passed", ins, outs, sems):
            cp.wait_send()
        for cp in copies("local", ins, outs, sems):
            cp.wait()

    return _Comm([shard], [jax.ShapeDtypeStruct((N_CHIPS, rows, cols), shard.dtype)],
                 [pltpu.SemaphoreType.DMA((6,)), pltpu.SemaphoreType.DMA((6,)), pltpu.SemaphoreType.DMA],
                 start, middle, finish)


def _gather_comm(shards):
    n = len(shards)

    def copies(kind, ins, outs, sems):
        send_sems, recv_sems, local_sems = sems
        x, y, c = _place()
        chip = 2 * x + y
        if kind == "local":
            return [pltpu.make_async_copy(ins[i], outs[i].at[chip], local_sems.at[i]) for i in range(n)]
        made = []
        for i in range(n):
            for j, (fx, fy) in enumerate(_CHIP_RELATIONS):
                px, py = _flip(x, fx), _flip(y, fy)
                slot, to = (chip, (px, py, c)) if kind == "send" else (2 * px + py, (x, y, c))
                made.append(pltpu.make_async_remote_copy(
                    src_ref=ins[i], dst_ref=outs[i].at[slot], send_sem=send_sems.at[i, j], recv_sem=recv_sems.at[i, j],
                    device_id=to, device_id_type=MESH))
        return made

    def start(ins, outs, sems):
        for cp in copies("local", ins, outs, sems) + copies("send", ins, outs, sems):
            cp.start()

    def finish(ins, outs, sems):
        for cp in copies("recv", ins, outs, sems):
            cp.wait_recv()
        for cp in copies("send", ins, outs, sems):
            cp.wait_send()
        for cp in copies("local", ins, outs, sems):
            cp.wait()

    return _Comm(list(shards), [jax.ShapeDtypeStruct((N_CHIPS,) + s.shape, s.dtype) for s in shards],
                 [pltpu.SemaphoreType.DMA((n, 3)), pltpu.SemaphoreType.DMA((n, 3)), pltpu.SemaphoreType.DMA((n,))],
                 start, None, finish)


def _exchange_comm(blocks):
    n = len(blocks)

    def copies(kind, ins, outs, sems):
        send_sems, recv_sems, local_sems = sems
        x, y, c = _place()
        chip = 2 * x + y
        me, sibling = (x, y, c), (x, y, 1 - c)

        def remote(src, dst, i, j, to):
            return pltpu.make_async_remote_copy(src_ref=src, dst_ref=dst, send_sem=send_sems.at[i, j],
                                                recv_sem=recv_sems.at[i, j], device_id=to, device_id_type=MESH)

        made = []
        for i in range(n):
            if kind == "local":
                made.append(pltpu.make_async_copy(ins[i].at[chip], outs[i].at[chip], local_sems.at[i]))
                continue
            for j, (fx, fy) in enumerate(_CHIP_RELATIONS):
                px, py = _flip(x, fx), _flip(y, fy)
                src = 2 * px + py
                if kind == "first":
                    made.append(remote(ins[i].at[src], outs[i].at[chip], i, j, (px, py, c)))
                elif kind == "landed":
                    made.append(remote(ins[i].at[src], outs[i].at[src], i, j, me))
                elif kind == "passed":
                    made.append(remote(outs[i].at[src], outs[i].at[N_CHIPS + src], i, 4 + j, sibling))
            if kind == "first":
                made.append(remote(ins[i].at[chip], outs[i].at[N_CHIPS + chip], i, 3, sibling))
            if kind == "arrivals":
                made += [remote(ins[i].at[0], outs[i].at[0], i, j, me) for j in (3, 4, 5, 6)]
        return made

    def start(ins, outs, sems):
        for cp in copies("local", ins, outs, sems) + copies("first", ins, outs, sems):
            cp.start()

    def middle(ins, outs, sems):
        for got, fwd in zip(copies("landed", ins, outs, sems), copies("passed", ins, outs, sems)):
            got.wait_recv()
            fwd.start()

    def finish(ins, outs, sems):
        for cp in copies("arrivals", ins, outs, sems):
            cp.wait_recv()
        for cp in copies("first", ins, outs, sems) + copies("passed", ins, outs, sems):
            cp.wait_send()
        for cp in copies("local", ins, outs, sems):
            cp.wait()

    return _Comm(list(blocks), [jax.ShapeDtypeStruct((N_DEV,) + b.shape[1:], b.dtype) for b in blocks],
                 [pltpu.SemaphoreType.DMA((n, 7)), pltpu.SemaphoreType.DMA((n, 7)), pltpu.SemaphoreType.DMA((n,))],
                 start, middle, finish)


def _row_spec(tr, w, col=0):
    return pl.BlockSpec((tr, w), lambda i: (i, col))


def _vec_spec(w):
    return pl.BlockSpec((1, w), lambda i: (0, 0))


def _acc_spec(w):
    return pl.BlockSpec((8, w), lambda i: (0, 0))


def _rms(x):
    return lax.rsqrt(jnp.mean(x * x, axis=-1, keepdims=True) + EPS)


def _rms_bwd(dn, n, rstd):
    return rstd * (dn - n * jnp.mean(dn * n, axis=-1, keepdims=True))


def _colsum(x):
    return jnp.sum(x, axis=0, keepdims=True)


def _zero_first(ref):
    @pl.when(pl.program_id(0) == 0)
    def _():
        ref[...] = jnp.zeros_like(ref)


def _x_specs(tr, n_lat_tiles):
    return [pl.BlockSpec((tr, D), lambda i: (jnp.minimum(i, n_lat_tiles - 1), 0)),
            pl.BlockSpec((tr, D), lambda i: (jnp.maximum(i - n_lat_tiles, 0), 0))]


def _x_tile(x_ref, c_ref, n_lat_tiles):
    return jnp.where(pl.program_id(0) >= n_lat_tiles, c_ref[...], x_ref[...])


def _norm_mod(x, ctx, w, mod, n_lat_tiles, tr, comm):
    R = x.shape[0] + ctx.shape[0]
    n = R // tr

    def body(x_ref, c_ref, w_ref, mod_ref, o_ref):
        xv = _x_tile(x_ref, c_ref, n_lat_tiles)
        nw = xv * _rms(xv) * w_ref[...]
        o_ref[...] = (nw * (1.0 + mod_ref[0, 1]) + mod_ref[0, 0]).astype(BF16)

    at = lambda i: pl.program_id(0) == i
    outs = pl.pallas_call(
        _hosted(body, 4, 1, 0, comm, lambda: at(0), lambda: at(n // 2), lambda: at(n - 1)), name="norm1_mod", grid=(n,),
        in_specs=_x_specs(tr, n_lat_tiles) + [_vec_spec(D), pl.BlockSpec(
            (1, 2, 1, D), lambda i: (jnp.where(i >= n_lat_tiles, 1, 0), 0, 0, 0))] + [_ANY] * len(comm.arrays),
        out_specs=[_row_spec(tr, D)] + [_ANY] * len(comm.out_shape),
        out_shape=[jax.ShapeDtypeStruct((R, D), BF16)] + comm.out_shape, scratch_shapes=comm.scratch,
        compiler_params=_cparams(("arbitrary",)),
    )(x, ctx, w, mod, *comm.arrays)
    return outs[0], outs[1:]


def _resid_norm_mod(x, ctx, mix, g1, w2, sh2, sc2, n_lat_tiles, tr):
    R = x.shape[0] + ctx.shape[0]

    def body(x_ref, c_ref, mix_ref, g1_ref, w_ref, sh_ref, sc_ref, h2_ref, u_ref):
        h2 = _x_tile(x_ref, c_ref, n_lat_tiles) + g1_ref[...] * mix_ref[...]
        h2_ref[...] = h2
        n = h2 * _rms(h2) * w_ref[...]
        u_ref[...] = (n * (1.0 + sc_ref[...]) + sh_ref[...]).astype(BF16)

    return pl.pallas_call(
        body, name="resid_norm2_mod", grid=(R // tr,),
        in_specs=_x_specs(tr, n_lat_tiles) + [_row_spec(tr, D)] + [_vec_spec(D)] * 4,
        out_specs=[_row_spec(tr, D), _row_spec(tr, D)],
        out_shape=[jax.ShapeDtypeStruct((R, D), F32), jax.ShapeDtypeStruct((R, D), BF16)],
        compiler_params=_cparams(("arbitrary",)),
    )(x, ctx, mix, g1, w2, sh2, sc2)


def _loss_head(h2, f, target, g2, fw, n_lat_tiles, tr):
    R = h2.shape[0]

    def body(h2_ref, f_ref, t_ref, g2_ref, fw_ref, dh3_ref, df_ref, acc_ref):
        _zero_first(acc_ref)
        lat = pl.program_id(0) < n_lat_tiles
        fv = f_ref[...]
        h3 = h2_ref[...] + g2_ref[...] * fv
        rstd = _rms(h3)
        n = h3 * rstd
        err = n * fw_ref[...] - t_ref[...]
        dy = err * (1.0 / D)
        dh3 = jnp.where(lat, _rms_bwd(dy * fw_ref[...], n, rstd), 0.0)
        dh3_ref[...] = dh3
        df_ref[...] = (g2_ref[...] * dh3).astype(BF16)
        acc_ref[0:1, :] += jnp.where(lat, _colsum(dy * n), 0.0)
        acc_ref[1:2, :] += _colsum(dh3 * fv)
        acc_ref[2:3, :] += jnp.where(lat, _colsum(err * err) * (0.5 / D), 0.0)

    return pl.pallas_call(
        body, name="loss_head", grid=(R // tr,),
        in_specs=[_row_spec(tr, D), _row_spec(tr, D),
                  pl.BlockSpec((tr, D), lambda i: (jnp.minimum(i, n_lat_tiles - 1), 0)), _vec_spec(D), _vec_spec(D)],
        out_specs=[_row_spec(tr, D), _row_spec(tr, D), _acc_spec(D)],
        out_shape=[jax.ShapeDtypeStruct((R, D), F32), jax.ShapeDtypeStruct((R, D), BF16), jax.ShapeDtypeStruct((8, D), F32)],
        compiler_params=_cparams(("arbitrary",)),
    )(h2, f, target, g2, fw)


def _ffn_in_bwd(du_a, du_b, h2, dh3, mix, sc2, g1, w2, tr):
    R = h2.shape[0]

    def body(dua_ref, dub_ref, h2_ref, dh3_ref, mix_ref, sc_ref, g1_ref, w_ref, dh2_ref, dmix_ref, acc_ref):
        _zero_first(acc_ref)
        du = _f32(dua_ref) + _f32(dub_ref)
        h2 = h2_ref[...]
        rstd = _rms(h2)
        n = h2 * rstd
        dnw = du * (1.0 + sc_ref[...])
        dh2 = dh3_ref[...] + _rms_bwd(dnw * w_ref[...], n, rstd)
        dh2_ref[...] = dh2
        dmix_ref[...] = (g1_ref[...] * dh2).astype(BF16)
        acc_ref[0:1, :] += _colsum(du * n * w_ref[...])
        acc_ref[1:2, :] += _colsum(du)
        acc_ref[2:3, :] += _colsum(dnw * n)
        acc_ref[3:4, :] += _colsum(dh2 * mix_ref[...])

    return pl.pallas_call(
        body, name="ffn_in_bwd", grid=(R // tr,),
        in_specs=[_row_spec(tr, D)] * 5 + [_vec_spec(D)] * 3,
        out_specs=[_row_spec(tr, D), _row_spec(tr, D), _acc_spec(D)],
        out_shape=[jax.ShapeDtypeStruct((R, D), F32), jax.ShapeDtypeStruct((R, D), BF16), jax.ShapeDtypeStruct((8, D), F32)],
        compiler_params=_cparams(("arbitrary",)),
    )(du_a, du_b, h2, dh3, mix, sc2, g1, w2)


def _norm1_bwd(dh1, x, ctx, dh2, w1, mod, n_lat_tiles, tr):
    R = x.shape[0] + ctx.shape[0]

    def body(dh1_ref, x_ref, c_ref, dh2_ref, w_ref, mod_ref, dx_ref, acc_ref):
        i = pl.program_id(0)

        @pl.when((i == 0) | (i == n_lat_tiles))
        def _():
            acc_ref[...] = jnp.zeros_like(acc_ref)

        dh1 = dh1_ref[...]
        x = _x_tile(x_ref, c_ref, n_lat_tiles)
        rstd = _rms(x)
        n = x * rstd
        dnw = dh1 * (1.0 + mod_ref[0, 1])

        @pl.when(i < n_lat_tiles)
        def _():
            dx_ref[...] = dh2_ref[...] + _rms_bwd(dnw * w_ref[...], n, rstd)

        acc_ref[0, 0:1, :] += _colsum(dh1 * n * w_ref[...])
        acc_ref[0, 1:2, :] += _colsum(dh1)
        acc_ref[0, 2:3, :] += _colsum(dnw * n)

    sel = lambda i: jnp.where(i >= n_lat_tiles, 1, 0)
    return pl.pallas_call(
        body, name="norm1_bwd", grid=(R // tr,),
        in_specs=[_row_spec(tr, D)] + _x_specs(tr, n_lat_tiles) + [_row_spec(tr, D), _vec_spec(D),
                                                                   pl.BlockSpec((1, 2, 1, D), lambda i: (sel(i), 0, 0, 0))],
        out_specs=[pl.BlockSpec((tr, D), lambda i: (jnp.minimum(i, n_lat_tiles - 1), 0)),
                   pl.BlockSpec((1, 8, D), lambda i: (sel(i), 0, 0))],
        out_shape=[jax.ShapeDtypeStruct((n_lat_tiles * tr, D), F32), jax.ShapeDtypeStruct((2, 8, D), F32)],
        compiler_params=_cparams(("arbitrary",)),
    )(dh1, x, ctx, dh2, w1, mod)


def _swiglu_act_bwd(da, gp, up, tr):
    R = gp.shape[0]

    def body(da_ref, g_ref, u_ref, dg_ref, du_ref):
        da, g = _f32(da_ref), _f32(g_ref)
        dg_ref[...] = (da * _f32(u_ref) * _dsilu(g)).astype(BF16)
        du_ref[...] = (da * _silu(g)).astype(BF16)

    return pl.pallas_call(
        body, name="swiglu_act_bwd", grid=(R // tr,), in_specs=[_row_spec(tr, D_FF)] * 3, out_specs=[_row_spec(tr, D_FF)] * 2,
        out_shape=[jax.ShapeDtypeStruct((R, D_FF), BF16)] * 2, compiler_params=_cparams(("arbitrary",)),
    )(da, gp, up)


def _merge(ya, yb, parts, tr):
    R = ya.shape[0]

    def body(ya_ref, yb_ref, ga_ref, gb_ref, o_ref):
        o_ref[...] = (_sigmoid(_f32(ga_ref)) * _f32(ya_ref) + _sigmoid(_f32(gb_ref)) * _f32(yb_ref)).astype(BF16)

    return pl.pallas_call(
        body, name="merge", grid=(R // tr,),
        in_specs=[_row_spec(tr, D), _row_spec(tr, D), _row_spec(tr, D, COL_GA // D), _row_spec(tr, D, COL_GB // D)],
        out_specs=_row_spec(tr, D), out_shape=jax.ShapeDtypeStruct((R, D), BF16), compiler_params=_cparams(("arbitrary",)),
    )(ya, yb, parts, parts)


def _dparts_out(tr, w, col, nd=1):
    blk = col // w
    return pl.BlockSpec((tr, w), (lambda i: (i, blk)) if nd == 1 else (lambda i, j: (i, blk + j)))


def _merge_bwd(dm, ya, yb, parts, dparts, tr):
    R = ya.shape[0]

    def body(dm_ref, ya_ref, yb_ref, ga_ref, gb_ref, _, dya_ref, dyb_ref, dg_ref):
        dm = _f32(dm_ref)
        sa, sb = _sigmoid(_f32(ga_ref)), _sigmoid(_f32(gb_ref))
        dya_ref[...] = (dm * sa).astype(BF16)
        dyb_ref[...] = (dm * sb).astype(BF16)
        dg_ref[:, 0:D] = (dm * _f32(ya_ref) * sa * (1.0 - sa)).astype(BF16)
        dg_ref[:, D:2 * D] = (dm * _f32(yb_ref) * sb * (1.0 - sb)).astype(BF16)

    return pl.pallas_call(
        body, name="merge_bwd", grid=(R // tr,),
        in_specs=[_row_spec(tr, D)] * 3 + [_row_spec(tr, D, COL_GA // D), _row_spec(tr, D, COL_GB // D), _ANY],
        out_specs=[_row_spec(tr, D), _row_spec(tr, D), _dparts_out(tr, 2 * D, COL_GA)],
        out_shape=[jax.ShapeDtypeStruct((R, D), BF16)] * 2 + [jax.ShapeDtypeStruct(dparts.shape, BF16)],
        input_output_aliases={5: 2}, compiler_params=_cparams(("arbitrary",)),
    )(dm, ya, yb, parts, parts, dparts)


def _gla_out(o2, parts, gw4, tr):
    R = parts.shape[0]

    def body(of_ref, ob_ref, r_ref, w_ref, out_ref):
        oa = _f32(of_ref) + _f32(ob_ref)
        sr = _silu(_f32(r_ref))
        for h in range(GLA_H):
            s = slice(h * GLA_DV, (h + 1) * GLA_DV)
            o = oa[:, s]
            out_ref[:, s] = (o * _rms(o) * w_ref[:, s] * sr[:, s]).astype(BF16)

    return pl.pallas_call(
        body, name="gla_out", grid=(R // tr,),
        in_specs=[_row_spec(tr, D), _row_spec(tr, D), _row_spec(tr, D, COL_R // D), _vec_spec(D)],
        out_specs=_row_spec(tr, D), out_shape=jax.ShapeDtypeStruct((R, D), BF16), compiler_params=_cparams(("arbitrary",)),
    )(o2[0], o2[1], parts, gw4)


def _gla_out_bwd(dout, o2, parts, gw4, dparts, tr):
    R = parts.shape[0]

    def body(d_ref, of_ref, ob_ref, r_ref, w_ref, _, do_ref, dr_ref, acc_ref):
        _zero_first(acc_ref)
        oa = _f32(of_ref) + _f32(ob_ref)
        r = _f32(r_ref)
        sr = _silu(r)
        dout = _f32(d_ref)
        for h in range(GLA_H):
            s = slice(h * GLA_DV, (h + 1) * GLA_DV)
            o = oa[:, s]
            rstd = _rms(o)
            n = o * rstd
            w = w_ref[:, s]
            dr_ref[:, s] = (dout[:, s] * n * w * _dsilu(r[:, s])).astype(BF16)
            dnw = dout[:, s] * sr[:, s]
            do_ref[:, s] = _rms_bwd(dnw * w, n, rstd).astype(ACT)
            acc_ref[0:1, s] += _colsum(dnw * n)

    return pl.pallas_call(
        body, name="gla_out_bwd", grid=(R // tr,),
        in_specs=[_row_spec(tr, D), _row_spec(tr, D), _row_spec(tr, D), _row_spec(tr, D, COL_R // D), _vec_spec(D), _ANY],
        out_specs=[_row_spec(tr, D), _dparts_out(tr, D, COL_R), _acc_spec(D)],
        out_shape=[jax.ShapeDtypeStruct((R, D), ACT), jax.ShapeDtypeStruct(dparts.shape, BF16), jax.ShapeDtypeStruct((8, D), F32)],
        input_output_aliases={5: 1}, compiler_params=_cparams(("arbitrary",)),
    )(dout, o2[0], o2[1], parts, gw4, dparts)


SSM_GW = SSM_INNER // SSM_G


def _ssd_out(y2, xbc, parts, dskip, nw, tr):
    R = parts.shape[0]

    def body(yf_ref, yb_ref, x_ref, z_ref, ds_ref, w_ref, out_ref):
        ob = (_f32(yf_ref) + _f32(yb_ref) + ds_ref[...] * _f32(x_ref)) * _silu(_f32(z_ref))
        for g in range(SSM_G):
            s = slice(g * SSM_GW, (g + 1) * SSM_GW)
            o = ob[:, s]
            out_ref[:, s] = (o * _rms(o) * w_ref[:, s]).astype(BF16)

    return pl.pallas_call(
        body, name="ssd_out", grid=(R // tr,),
        in_specs=[_row_spec(tr, SSM_INNER)] * 3 + [_row_spec(tr, SSM_INNER, COL_Z // SSM_INNER),
                                                   _vec_spec(SSM_INNER), _vec_spec(SSM_INNER)],
        out_specs=_row_spec(tr, SSM_INNER), out_shape=jax.ShapeDtypeStruct((R, SSM_INNER), BF16),
        compiler_params=_cparams(("arbitrary",)),
    )(y2[0], y2[1], xbc, parts, dskip, nw)


def _ssd_out_bwd(dout, y2, xbc, parts, dskip, nw, dparts, tr):
    R = parts.shape[0]

    def body(d_ref, yf_ref, yb_ref, x_ref, z_ref, ds_ref, w_ref, _, dy_ref, dz_ref, acc_ref):
        _zero_first(acc_ref)
        x, z = _f32(x_ref), _f32(z_ref)
        pre = _f32(yf_ref) + _f32(yb_ref) + ds_ref[...] * x
        sz = _silu(z)
        ob = pre * sz
        dout = _f32(d_ref)
        for g in range(SSM_G):
            s = slice(g * SSM_GW, (g + 1) * SSM_GW)
            o = ob[:, s]
            rstd = _rms(o)
            n = o * rstd
            dob = _rms_bwd(dout[:, s] * w_ref[:, s], n, rstd)
            dz_ref[:, s] = (dob * pre[:, s] * _dsilu(z[:, s])).astype(BF16)
            dy = dob * sz[:, s]
            dy_ref[:, s] = dy.astype(ACT)
            acc_ref[0:1, s] += _colsum(dout[:, s] * n)
            acc_ref[1:2, s] += _colsum(dy * x[:, s])

    return pl.pallas_call(
        body, name="ssd_out_bwd", grid=(R // tr,),
        in_specs=[_row_spec(tr, SSM_INNER)] * 4 + [_row_spec(tr, SSM_INNER, COL_Z // SSM_INNER),
                                                   _vec_spec(SSM_INNER), _vec_spec(SSM_INNER), _ANY],
        out_specs=[_row_spec(tr, SSM_INNER), _dparts_out(tr, SSM_INNER, COL_Z), _acc_spec(SSM_INNER)],
        out_shape=[jax.ShapeDtypeStruct((R, SSM_INNER), ACT), jax.ShapeDtypeStruct(dparts.shape, BF16),
                   jax.ShapeDtypeStruct((8, SSM_INNER), F32)],
        input_output_aliases={7: 1}, compiler_params=_cparams(("arbitrary",)),
    )(dout, y2[0], y2[1], xbc, parts, dskip, nw, dparts)


CONV_W = SSM_INNER + 2 * SSM_G * SSM_N
CONV_BLK = 1024


CONV_SHIFTS = (-2, -1, 1, 2)


def _conv_mask_table(tr):
    t = np.arange(tr)
    table = np.zeros((2, len(CONV_SHIFTS), tr, 128), np.float32)
    for kind, (pos, seg) in enumerate(((t % GRID_W, GRID_W), (t, tr))):
        for k, s in enumerate(CONV_SHIFTS):
            table[kind, k] = ((pos + s >= 0) & (pos + s < seg)).astype(np.float32)[:, None]
    return jnp.asarray(table)


def _shifted(u, s, mask_ref, tr):
    return u if s == 0 else pltpu.roll(u, (-s) % tr, 0) * mask_ref[0, CONV_SHIFTS.index(s)]


def _conv_mask_spec(tr, n_lat_tiles, row_axis):
    return pl.BlockSpec((1, len(CONV_SHIFTS), tr, 128),
                        lambda *ids: (jnp.where(ids[row_axis] >= n_lat_tiles, 1, 0), 0, 0, 0))


def _conv_fwd(parts, cw, cb, n_lat_tiles, tr):
    R = parts.shape[0]

    n_blk = CONV_W // CONV_BLK

    def body(*refs):
        u_refs, (w_ref, b_ref, mask_ref, o_ref) = refs[:n_blk], refs[n_blk:]
        for k, u_ref in enumerate(u_refs):
            def lanes(l, carry, u_ref=u_ref, k=k):
                sl = pl.ds(pl.multiple_of(l * 128, 128), 128)
                so = pl.ds(pl.multiple_of(k * CONV_BLK + l * 128, 128), 128)
                u, w = u_ref[:, sl].astype(F32), w_ref[:, so]
                acc = jnp.zeros_like(u) + b_ref[:, so]
                for j in range(SSM_CONV):
                    acc = acc + _shifted(u, j - CONV_LEFT, mask_ref, tr) * w[j:j + 1, :]
                o_ref[:, so] = _silu(acc).astype(ACT)
                return carry

            lax.fori_loop(0, CONV_BLK // 128, lanes, 0)

    return pl.pallas_call(
        body, name="conv_fwd", grid=(R // tr,),
        in_specs=[pl.BlockSpec((tr, CONV_BLK), lambda i, k=k: (i, COL_XBC // CONV_BLK + k)) for k in range(n_blk)]
        + [pl.BlockSpec((SSM_CONV, CONV_W), lambda i: (0, 0)), pl.BlockSpec((1, CONV_W), lambda i: (0, 0)),
           _conv_mask_spec(tr, n_lat_tiles, 0)],
        out_specs=pl.BlockSpec((tr, CONV_W), lambda i: (i, 0)), out_shape=jax.ShapeDtypeStruct((R, CONV_W), ACT),
        compiler_params=_cparams(("arbitrary",)),
    )(*([parts] * n_blk), cw, cb, _conv_mask_table(tr))


def _conv_bwd(dx, db, dc, dy, dskip, parts, cw, cb, dparts, n_lat_tiles, tr):
    R = parts.shape[0]
    bc = SSM_G * SSM_N

    def body(dxf_ref, dxb_ref, dy_ref, ds_ref, dbf_ref, dbb_ref, dcf_ref, dcb_ref, u_ref, w_ref, b_ref, _, mask_ref,
             du_ref, acc_ref, d_scr):
        _zero_first(acc_ref)
        d_scr[:, 0:SSM_INNER] = dxf_ref[...] + dxb_ref[...] + _f32(dy_ref) * ds_ref[...]
        d_scr[:, SSM_INNER:SSM_INNER + bc] = dbf_ref[...] + dbb_ref[...]
        d_scr[:, SSM_INNER + bc:] = dcf_ref[...] + dcb_ref[...]

        def lanes(l, carry):
            sl = pl.ds(pl.multiple_of(l * 128, 128), 128)
            u, w = u_ref[:, sl].astype(F32), w_ref[:, sl]
            pre = jnp.zeros_like(u) + b_ref[:, sl]
            taps = []
            for j in range(SSM_CONV):
                tap = _shifted(u, j - CONV_LEFT, mask_ref, tr)
                taps.append(tap)
                pre = pre + tap * w[j:j + 1, :]
            dpre = d_scr[:, sl] * _dsilu(pre)
            du = jnp.zeros_like(u)
            sums = []
            for j in range(SSM_CONV):
                sums.append(_colsum(dpre * taps[j]))
                du = du + _shifted(dpre, CONV_LEFT - j, mask_ref, tr) * w[j:j + 1, :]
            sums += [_colsum(dpre), jnp.zeros((8 - SSM_CONV - 1, 128), F32)]
            acc_ref[:, sl] += jnp.concatenate(sums, axis=0)
            du_ref[:, sl] = du.astype(BF16)
            return carry

        lax.fori_loop(0, CONV_W // 128, lanes, 0)

    return pl.pallas_call(
        body, name="conv_bwd", grid=(R // tr,),
        in_specs=[_row_spec(tr, SSM_INNER)] * 3 + [_vec_spec(SSM_INNER)] + [_row_spec(tr, bc)] * 4
        + [_row_spec(tr, CONV_W, COL_XBC // CONV_W), pl.BlockSpec((SSM_CONV, CONV_W), lambda i: (0, 0)), _vec_spec(CONV_W),
           _ANY, _conv_mask_spec(tr, n_lat_tiles, 0)],
        out_specs=[_dparts_out(tr, CONV_W, COL_XBC), _acc_spec(CONV_W)],
        out_shape=[jax.ShapeDtypeStruct(dparts.shape, BF16), jax.ShapeDtypeStruct((8, CONV_W), F32)],
        scratch_shapes=[pltpu.VMEM((tr, CONV_W), F32)],
        input_output_aliases={11: 0}, compiler_params=_cparams(("arbitrary",)),
    )(*dx, dy, dskip, *db, *dc, parts, cw, cb, dparts, _conv_mask_table(tr))


def _chunk_row_block(d, i, n_lat, n_ctx):
    fwd = jnp.where(i < n_ctx, n_lat + i, i - n_ctx)
    rev = n_lat + n_ctx - 1 - i
    if isinstance(d, int):
        return rev if d else fwd
    return jnp.where(d == 0, fwd, rev)


def _tri(n, d, transpose=False):
    row = lax.broadcasted_iota(jnp.int32, (n, n), 0)
    col = lax.broadcasted_iota(jnp.int32, (n, n), 1)
    diff = (col - row) if transpose else (row - col)
    return diff * (1 - 2 * d) >= 0


def _gla_gates(sm, uhi, ulo, bias, d):
    pre = _nn3(sm, uhi, ulo) + bias
    g = _log_sigmoid(pre) * (1.0 / GLA_TAU)
    mask = _tri(GLA_C, d)
    b = _x_nn(mask.astype(BF16), g)
    b_tot = _colsum(g)
    b_ref = b[GLA_C // 2:GLA_C // 2 + 1, :]
    e_q = jnp.exp(jnp.minimum(b - b_ref, EXP_CLAMP))
    e_k = jnp.exp(jnp.minimum(b_ref - b, EXP_CLAMP))
    return pre, mask, b_tot, e_q, e_k, jnp.exp(b), jnp.exp(b_tot - b)


GLA_QK = GLA_H * GLA_DK
GLA_V = GLA_H * GLA_DV


def _gla_specs(n_lat, n_ctx, step_of):
    rbs = [lambda i, d=d: _chunk_row_block(d, step_of(i), n_lat, n_ctx) for d in range(2)]
    specs = []
    for rb in rbs:
        specs += [pl.BlockSpec((GLA_C, GLA_QK), lambda i, rb=rb: (rb(i), COL_Q // GLA_QK)),
                  pl.BlockSpec((GLA_C, GLA_QK), lambda i, rb=rb: (rb(i), COL_K // GLA_QK)),
                  pl.BlockSpec((GLA_C, GLA_V), lambda i, rb=rb: (rb(i), COL_V // GLA_V)),
                  pl.BlockSpec((GLA_C, 128), lambda i, rb=rb: (rb(i), 0))]
    specs += [pl.BlockSpec((2, 128, GLA_QK), lambda i: (0, 0, 0)), pl.BlockSpec((2, 128, GLA_QK), lambda i: (0, 0, 0)),
              pl.BlockSpec((2, 1, GLA_QK), lambda i: (0, 0, 0))]
    return specs, rbs


def _gla_fwd(parts, sm, uhi, ulo, bias, n_lat, n_ctx):
    R = parts.shape[0]
    n_steps = n_lat + n_ctx
    scale = GLA_DK ** -0.5

    def body(*refs):
        ins, (uhi_ref, ulo_ref, bias_ref), o_refs, hist_ref, st = refs[:8], refs[8:11], refs[11:13], refs[13], refs[14]

        @pl.when(pl.program_id(0) == 0)
        def _():
            st[...] = jnp.zeros_like(st)

        for d in range(2):
            q_ref, k_ref, v_ref, sm_ref = ins[4 * d:4 * d + 4]
            _, mask, b_tot, e_q, e_k, e_in, e_out = _gla_gates(sm_ref[...], uhi_ref[d], ulo_ref[d], bias_ref[d], d)
            q, k, v = _f32(q_ref) * scale, _f32(k_ref), _bf(v_ref[...])
            qb, kb, q_in, k_out, decay = _bf(q * e_q), _bf(k * e_k), _bf(q * e_in), _bf(k * e_out), jnp.exp(b_tot)
            for h in range(GLA_H):
                sk, sv = slice(h * GLA_DK, (h + 1) * GLA_DK), slice(h * GLA_DV, (h + 1) * GLA_DV)
                att = jnp.where(mask, _nt(qb[:, sk], kb[:, sk]), 0.0)
                s_in = st[d, h]
                hist_ref[d, 0, h] = s_in
                o_refs[d][:, sv] = (_nn(_bf(att), v[:, sv]) + _nt(q_in[:, sk], _bf(s_in))).astype(ACT)
                st[d, h] = decay[:, sk] * s_in + _tn(v[:, sv], k_out[:, sk])

    in_specs, rbs = _gla_specs(n_lat, n_ctx, lambda i: i)
    return pl.pallas_call(
        body, name="gla_fwd", grid=(n_steps,), in_specs=in_specs,
        out_specs=[pl.BlockSpec((GLA_C, GLA_V), lambda i, rb=rb: (rb(i), 0)) for rb in rbs]
        + [pl.BlockSpec((2, 1, GLA_H, GLA_DV, GLA_DK), lambda i: (0, i, 0, 0, 0))],
        out_shape=[jax.ShapeDtypeStruct((R, GLA_V), ACT)] * 2 + [jax.ShapeDtypeStruct((2, n_steps, GLA_H, GLA_DV, GLA_DK), F32)],
        scratch_shapes=[pltpu.VMEM((2, GLA_H, GLA_DV, GLA_DK), F32)],
        compiler_params=_cparams(("arbitrary",)),
    )(*([parts, parts, parts, sm] * 2), uhi, ulo, bias)


def _gla_bwd(do, parts, sm, uhi, ulo, bias, hist, n_lat, n_ctx):
    R = parts.shape[0]
    n_steps = n_lat + n_ctx
    scale = GLA_DK ** -0.5
    step_of = lambda j: n_steps - 1 - j

    def body(*refs):
        ins, (uhi_ref, ulo_ref, bias_ref), do_refs, hist_ref = refs[:8], refs[8:11], refs[11:13], refs[13]
        outs, dst = refs[14:22], refs[22]

        @pl.when(pl.program_id(0) == 0)
        def _():
            dst[...] = jnp.zeros_like(dst)

        for d in range(2):
            q_ref, k_ref, v_ref, sm_ref = ins[4 * d:4 * d + 4]
            dq_ref, dk_ref, dv_ref, dp_ref = outs[4 * d:4 * d + 4]
            pre, mask, b_tot, e_q, e_k, e_in, e_out = _gla_gates(sm_ref[...], uhi_ref[d], ulo_ref[d], bias_ref[d], d)
            q, k, v = _f32(q_ref) * scale, _f32(k_ref), _bf(v_ref[...])
            dout = _bf(do_refs[d][...])
            k_out_f = k * e_out
            qb, kb, q_in, k_out, decay = _bf(q * e_q), _bf(k * e_k), _bf(q * e_in), _bf(k_out_f), jnp.exp(b_tot)
            dqs, dks, dk_outs, dss = [], [], [], []
            for h in range(GLA_H):
                sk, sv = slice(h * GLA_DK, (h + 1) * GLA_DK), slice(h * GLA_DV, (h + 1) * GLA_DV)
                s_in, ds = hist_ref[d, 0, h], dst[d, h]
                att = jnp.where(mask, _nt(qb[:, sk], kb[:, sk]), 0.0)
                datt = _bf(jnp.where(mask, _nt(dout[:, sv], v[:, sv]), 0.0))
                dv_ref[:, sv] = (_tn(_bf(att), dout[:, sv]) + _nt(k_out[:, sk], _bf(ds))).astype(ACT)
                dqs.append(_nn(datt, kb[:, sk]) * e_q[:, sk] + _nn(dout[:, sv], _bf(s_in)) * e_in[:, sk])
                dk_o = _nn(v[:, sv], _bf(ds))
                dk_outs.append(dk_o)
                dks.append(_tn(datt, qb[:, sk]) * e_k[:, sk])
                dss.append(_colsum(ds * s_in))
                dst[d, h] = decay[:, sk] * ds + _tn(dout[:, sv], q_in[:, sk])
            dq, dk_out = jnp.concatenate(dqs, axis=1), jnp.concatenate(dk_outs, axis=1)
            dk = jnp.concatenate(dks, axis=1) + dk_out * e_out
            dq_ref[...] = (dq * scale).astype(ACT)
            dk_ref[...] = dk.astype(ACT)
            db_tot = _colsum(dk_out * k_out_f) + decay * jnp.concatenate(dss, axis=1)
            dg = _x_nn(_tri(GLA_C, d, transpose=True).astype(BF16), dq * q - dk * k) + db_tot
            dp_ref[...] = dg * (1.0 / GLA_TAU) * _sigmoid(-pre)

    in_specs, rbs = _gla_specs(n_lat, n_ctx, step_of)
    in_specs += [pl.BlockSpec((GLA_C, GLA_V), lambda j, rb=rb: (rb(j), 0)) for rb in rbs]
    in_specs += [pl.BlockSpec((2, 1, GLA_H, GLA_DV, GLA_DK), lambda j: (0, step_of(j), 0, 0, 0))]
    out_specs, out_shape = [], []
    for rb in rbs:
        for w, dt in ((GLA_QK, ACT), (GLA_QK, ACT), (GLA_V, ACT), (GLA_QK, F32)):
            out_specs.append(pl.BlockSpec((GLA_C, w), lambda j, rb=rb: (rb(j), 0)))
            out_shape.append(jax.ShapeDtypeStruct((R, w), dt))
    outs = pl.pallas_call(
        body, name="gla_bwd", grid=(n_steps,), in_specs=in_specs, out_specs=out_specs, out_shape=out_shape,
        scratch_shapes=[pltpu.VMEM((2, GLA_H, GLA_DV, GLA_DK), F32)],
        compiler_params=_cparams(("arbitrary",)),
    )(*([parts, parts, parts, sm] * 2), uhi, ulo, bias, do, do, hist)
    return [(outs[k], outs[4 + k]) for k in range(4)]


def _ssd_consts(dt_bias, a_log):
    sel = np.zeros((2, SSM_G, 128, 128), np.float32)
    for d, base in enumerate((SM_DTF, SM_DTB)):
        for g in range(SSM_G):
            for e in range(SSM_HPG):
                sel[d, g, base + SSM_HPG * g + e, e] = 1.0
    e512 = np.zeros((128, SSM_GW), np.float32)
    for e in range(SSM_HPG):
        e512[e, SSM_P * e:SSM_P * (e + 1)] = 1.0
    a_neg = -jnp.exp(a_log)
    pad = lambda v: jnp.pad(v.reshape(2, SSM_G, 1, SSM_HPG), ((0, 0), (0, 0), (0, 0), (0, 128 - SSM_HPG)))
    return dict(
        sel=jnp.asarray(sel, BF16), sel_t=jnp.asarray(sel.transpose(0, 1, 3, 2), BF16), e512_t=jnp.asarray(e512.T, BF16),
        dtb=pad(dt_bias), a=pad(a_neg), a512=jnp.repeat(a_neg, SSM_P, axis=1).reshape(2, SSM_G, 1, SSM_GW))


def _head_columns(x8):
    return [jnp.broadcast_to(x8[:, e:e + 1], (x8.shape[0], 128)) for e in range(SSM_HPG)]


def _head_layout(cols):
    low = lax.broadcasted_iota(jnp.int32, (1, 128), 1) < SSM_P
    return jnp.concatenate([jnp.where(low, cols[2 * j], cols[2 * j + 1]) for j in range(SSM_HPG // 2)], axis=1)


def _ssd_common(sm, sel, dtb, a_neg, a512, d):
    dtr8 = _nn_x(sm, sel) + dtb
    dt8 = _softplus(dtr8)
    a8 = a_neg * dt8
    mask = _tri(SSM_C, d)
    mask_t = _tri(SSM_C, d, transpose=True).astype(BF16)
    cum8 = _x_nn(mask.astype(BF16), a8)
    a_hi, a_lo = _split(a8)
    cum_t = _tn(a_hi, mask_t) + _tn(a_lo, mask_t)
    cum_cols = _head_columns(cum8)
    dt_exp = _head_layout(_head_columns(dt8))
    a_exp = a512 * dt_exp
    return dict(dtr8=dtr8, a8=a8, mask=mask, mask_t=mask_t, cum_t=cum_t, dt_exp=dt_exp, a_exp=a_exp,
                cum_exp=_head_layout(cum_cols), cum_cols=cum_cols, tot_exp=_colsum(a_exp))


def _head_lanes(x, e):
    pair = x[:, 128 * (e // 2):128 * (e // 2 + 1)]
    low = lax.broadcasted_iota(jnp.int32, (1, 128), 1) < SSM_P
    return _bf(jnp.where(low if e % 2 == 0 else jnp.logical_not(low), pair, 0.0))


def _per_head_pairs(fn, x):
    return jnp.concatenate([fn(2 * j, _head_lanes(x, 2 * j)) + fn(2 * j + 1, _head_lanes(x, 2 * j + 1))
                            for j in range(SSM_HPG // 2)], axis=1)


def _ssd_decay(cm, e):
    diff = cm["cum_cols"][e] - cm["cum_t"][e:e + 1, :]
    return jnp.where(cm["mask"], jnp.exp(jnp.minimum(diff, 0.0)), 0.0)


SSM_GPS = 4


def _ssd_specs(n_lat, n_ctx, step_of):
    rbs = [lambda i, d=d: _chunk_row_block(d, step_of(i), n_lat, n_ctx) for d in range(2)]
    xw, nw = SSM_GPS * SSM_GW, SSM_GPS * SSM_N
    specs = []
    for rb in rbs:
        specs += [pl.BlockSpec((SSM_C, xw), lambda g, i, rb=rb: (rb(i), g)),
                  pl.BlockSpec((SSM_C, nw), lambda g, i, rb=rb: (rb(i), SSM_INNER // nw + g)),
                  pl.BlockSpec((SSM_C, nw), lambda g, i, rb=rb: (rb(i), (SSM_INNER + SSM_G * SSM_N) // nw + g)),
                  pl.BlockSpec((SSM_C, 128), lambda g, i, rb=rb: (rb(i), 0))]
    specs += [pl.BlockSpec((2, SSM_GPS, 128, 128), lambda g, i: (0, g, 0, 0)),
              pl.BlockSpec((2, SSM_GPS, 1, 128), lambda g, i: (0, g, 0, 0)),
              pl.BlockSpec((2, SSM_GPS, 1, 128), lambda g, i: (0, g, 0, 0)),
              pl.BlockSpec((2, SSM_GPS, 1, SSM_GW), lambda g, i: (0, g, 0, 0))]
    return specs, rbs


def _ssd_fwd(xbc, sm, k, n_lat, n_ctx, comm):
    R = xbc.shape[0]
    n_steps = n_lat + n_ctx

    def body(*refs):
        ins, (sel_ref, dtb_ref, a_ref, a512_ref), y_refs, hist_ref, st = refs[:8], refs[8:12], refs[12:14], refs[14], refs[15]

        @pl.when(pl.program_id(1) == 0)
        def _():
            st[...] = jnp.zeros_like(st)

        for d in range(2):
            x_ref, b_ref, c_ref, sm_ref = ins[4 * d:4 * d + 4]
            sm = sm_ref[...]
            for gg in range(SSM_GPS):
                sx, sn = slice(gg * SSM_GW, (gg + 1) * SSM_GW), slice(gg * SSM_N, (gg + 1) * SSM_N)
                cm = _ssd_common(sm, sel_ref[d, gg], dtb_ref[d, gg], a_ref[d, gg], a512_ref[d, gg], d)
                bm, cmat = _bf(b_ref[:, sn]), _bf(c_ref[:, sn])
                xdt = x_ref[:, sx].astype(F32) * cm["dt_exp"]
                cb = _nt(cmat, bm)
                ys = _per_head_pairs(lambda e, x_e: _nn(_bf(cb * _ssd_decay(cm, e)), x_e), xdt)
                s_in = st[d, gg]
                hist_ref[d, 0, gg] = s_in
                y = ys + jnp.exp(cm["cum_exp"]) * _nn(cmat, _bf(s_in))
                y_refs[d][:, sx] = y.astype(ACT)
                st[d, gg] = jnp.exp(cm["tot_exp"]) * s_in + _tn(bm, _bf(xdt * jnp.exp(cm["tot_exp"] - cm["cum_exp"])))

    in_specs, rbs = _ssd_specs(n_lat, n_ctx, lambda i: i)
    out_specs = [pl.BlockSpec((SSM_C, SSM_GPS * SSM_GW), lambda g, i, rb=rb: (rb(i), g)) for rb in rbs]
    out_specs += [pl.BlockSpec((2, 1, SSM_GPS, SSM_N, SSM_GW), lambda g, i: (0, i, g, 0, 0))]
    out_shape = [jax.ShapeDtypeStruct((R, SSM_INNER), ACT)] * 2 + [jax.ShapeDtypeStruct((2, n_steps, SSM_G, SSM_N, SSM_GW), F32)]
    args = [xbc, xbc, xbc, sm] * 2 + [k["sel"], k["dtb"], k["a"], k["a512"]]
    n_host_out = len(out_shape)
    outs = pl.pallas_call(
        _hosted(body, len(args), n_host_out, 1, comm, *_ssd_comm_steps(n_steps)), name="ssd_fwd",
        grid=(SSM_G // SSM_GPS, n_steps), in_specs=in_specs + [_ANY] * len(comm.arrays),
        out_specs=out_specs + [_ANY] * len(comm.out_shape), out_shape=out_shape + comm.out_shape,
        scratch_shapes=[pltpu.VMEM((2, SSM_GPS, SSM_N, SSM_GW), F32)] + comm.scratch,
        compiler_params=_cparams(("arbitrary", "arbitrary")),
    )(*args, *comm.arrays)
    return outs[:n_host_out], outs[n_host_out:]


def _ssd_comm_steps(n_steps):
    n_g = SSM_G // SSM_GPS
    at = lambda g, i: (pl.program_id(0) == g) & (pl.program_id(1) == i)
    half = (n_g // 2, n_steps // 2 if n_g % 2 else 0)
    return (lambda: at(0, 0)), (lambda: at(*half)), (lambda: at(n_g - 1, n_steps - 1))


def _ssd_bwd(dy, xbc, sm, k, hist, n_lat, n_ctx, comm):
    R = xbc.shape[0]
    n_steps = n_lat + n_ctx
    step_of = lambda j: n_steps - 1 - j

    def one(d, gg, x_ref, b_ref, c_ref, sm_ref, sel_ref, dtb_ref, a_ref, a512_ref, selt_ref, e512t_ref, dy_ref,
            hist_ref, dx_ref, db_ref, dc_ref, dsm_ref, acc_ref, dst):
        sx, sn = slice(gg * SSM_GW, (gg + 1) * SSM_GW), slice(gg * SSM_N, (gg + 1) * SSM_N)
        a_neg, e512_t = a_ref[d, gg], e512t_ref[...]
        cm = _ssd_common(sm_ref[...], sel_ref[d, gg], dtb_ref[d, gg], a_neg, a512_ref[d, gg], d)
        x, dyv = x_ref[:, sx].astype(F32), dy_ref[:, sx].astype(F32)
        bm, cmat = _bf(b_ref[:, sn]), _bf(c_ref[:, sn])
        xdt = x * cm["dt_exp"]
        cb = _nt(cmat, bm)
        s_in, ds = hist_ref[d, 0, gg], dst[d, gg]
        w = jnp.exp(cm["tot_exp"] - cm["cum_exp"])
        z = _nn(bm, _bf(ds))
        decay_in = jnp.exp(cm["cum_exp"])
        gy = _bf(dyv * decay_in)
        dcb = jnp.zeros((SSM_C, SSM_C), F32)
        crossing = []
        row = lax.broadcasted_iota(jnp.int32, (SSM_C, SSM_C), 0)
        col = lax.broadcasted_iota(jnp.int32, (SSM_C, SSM_C), 1)
        eye = (row == col).astype(BF16)
        before = (cm["mask_t"] - eye)
        xdt_bf = _bf(xdt)

        def head(e, dy_e):
            nonlocal dcb
            lm = _ssd_decay(cm, e)
            m_e = cb * lm
            dm_e = _nt(dy_e, xdt_bf[:, 128 * (e // 2):128 * (e // 2 + 1)])
            dcb = dcb + dm_e * lm
            crossing.append(_bf(dm_e * m_e))
            return _tn(_bf(m_e), dy_e)

        dx_heads = _per_head_pairs(head, dyv)
        through = _nn(jnp.concatenate(crossing, axis=0), before)
        crossing = [_colsum(jnp.where(cm["mask"], through[e * SSM_C:(e + 1) * SSM_C], 0.0)) for e in range(SSM_HPG)]
        da_rows = jnp.concatenate(crossing + [jnp.zeros((128 - SSM_HPG, SSM_C), F32)], axis=0)
        r_hi, r_lo = _split(da_rows)
        da8_intra = _tn(r_hi, eye) + _tn(r_lo, eye)
        dx_state = w * z
        dxdt = dx_heads + dx_state
        dcb = _bf(dcb)
        c_s = _nn(cmat, _bf(s_in))
        dc_ref[:, sn] = _nn(dcb, bm) + _nt(gy, _bf(s_in))
        db_ref[:, sn] = _tn(dcb, cmat) + _nt(_bf(w * xdt), _bf(ds))
        dst[d, gg] = jnp.exp(cm["tot_exp"]) * ds + _tn(cmat, gy)
        state_path = xdt * dx_state
        per_token = _nn_x(jnp.concatenate([dyv * decay_in * c_s - state_path, dxdt * x], axis=0), e512_t)
        totals = jnp.concatenate([_colsum(state_path), _colsum(ds * s_in), jnp.zeros((6, SSM_GW), F32)], axis=0)
        totals = _nn_x(totals, e512_t)
        tot8 = _colsum(cm["a8"])
        dtot8 = totals[0:1] + jnp.exp(tot8) * totals[1:2]
        da8 = da8_intra + _x_nn(cm["mask_t"], per_token[:SSM_C]) + dtot8
        ddt8 = da8 * a_neg + per_token[SSM_C:]
        dsm_ref[gg] = _nn_x(ddt8 * _sigmoid(cm["dtr8"]), selt_ref[d, gg])
        dx_ref[:, sx] = dxdt * cm["dt_exp"]
        acc_ref[d, gg, 0:1, :] += _colsum(da8 * cm["a8"])

    def body(*refs):
        ins, consts, (selt_ref, e512t_ref), dy_refs, hist_ref = refs[:8], refs[8:12], refs[12:14], refs[14:16], refs[16]
        outs, acc_ref, dst = refs[17:25], refs[25], refs[26]

        @pl.when(pl.program_id(1) == 0)
        def _():
            dst[...] = jnp.zeros_like(dst)
            acc_ref[...] = jnp.zeros_like(acc_ref)

        for d in range(2):
            for gg in range(SSM_GPS):
                one(d, gg, *ins[4 * d:4 * d + 4], *consts, selt_ref, e512t_ref, dy_refs[d], hist_ref,
                    *outs[4 * d:4 * d + 4], acc_ref, dst)

    xw, nw = SSM_GPS * SSM_GW, SSM_GPS * SSM_N
    in_specs, rbs = _ssd_specs(n_lat, n_ctx, step_of)
    in_specs += [pl.BlockSpec((2, SSM_GPS, 128, 128), lambda g, j: (0, g, 0, 0)), pl.BlockSpec((SSM_GW, 128), lambda g, j: (0, 0))]
    in_specs += [pl.BlockSpec((SSM_C, xw), lambda g, j, rb=rb: (rb(j), g)) for rb in rbs]
    in_specs += [pl.BlockSpec((2, 1, SSM_GPS, SSM_N, SSM_GW), lambda g, j: (0, step_of(j), g, 0, 0))]
    out_specs, out_shape = [], []
    for rb in rbs:
        out_specs += [pl.BlockSpec((SSM_C, xw), lambda g, j, rb=rb: (rb(j), g)),
                      pl.BlockSpec((SSM_C, nw), lambda g, j, rb=rb: (rb(j), g)),
                      pl.BlockSpec((SSM_C, nw), lambda g, j, rb=rb: (rb(j), g)),
                      pl.BlockSpec((SSM_GPS, SSM_C, 128), lambda g, j, rb=rb: (g, rb(j), 0))]
        out_shape += [jax.ShapeDtypeStruct((R, SSM_INNER), F32), jax.ShapeDtypeStruct((R, SSM_G * SSM_N), F32),
                      jax.ShapeDtypeStruct((R, SSM_G * SSM_N), F32), jax.ShapeDtypeStruct((SSM_G, R, 128), F32)]
    out_specs.append(pl.BlockSpec((2, SSM_GPS, 8, 128), lambda g, j: (0, g, 0, 0)))
    out_shape.append(jax.ShapeDtypeStruct((2, SSM_G, 8, 128), F32))
    args = [xbc, xbc, xbc, sm] * 2 + [k["sel"], k["dtb"], k["a"], k["a512"], k["sel_t"], k["e512_t"], dy, dy, hist]
    n_host_out = len(out_shape)
    outs = pl.pallas_call(
        _hosted(body, len(args), n_host_out, 1, comm, *_ssd_comm_steps(n_steps)), name="ssd_bwd",
        grid=(SSM_G // SSM_GPS, n_steps), in_specs=in_specs + [_ANY] * len(comm.arrays),
        out_specs=out_specs + [_ANY] * len(comm.out_shape), out_shape=out_shape + comm.out_shape,
        scratch_shapes=[pltpu.VMEM((2, SSM_GPS, SSM_N, SSM_GW), F32)] + comm.scratch,
        compiler_params=_cparams(("arbitrary", "arbitrary")),
    )(*args, *comm.arrays)
    return [(outs[n], outs[4 + n]) for n in range(4)] + [outs[8]], outs[n_host_out:]


def _gla_assemble(dq, dk, dv, dparts, tr):
    R = dq[0].shape[0]
    qk = GLA_H * GLA_DK

    def body(dqf_ref, dqb_ref, dkf_ref, dkb_ref, dvf_ref, dvb_ref, _, o_ref):
        o_ref[:, 0:qk] = (_f32(dqf_ref) + _f32(dqb_ref)).astype(BF16)
        o_ref[:, qk:2 * qk] = (_f32(dkf_ref) + _f32(dkb_ref)).astype(BF16)
        o_ref[:, 2 * qk:] = (_f32(dvf_ref) + _f32(dvb_ref)).astype(BF16)

    return pl.pallas_call(
        body, name="gla_assemble", grid=(R // tr,), in_specs=[_row_spec(tr, qk)] * 4 + [_row_spec(tr, D)] * 2 + [_ANY],
        out_specs=_dparts_out(tr, 2 * D, COL_Q), out_shape=jax.ShapeDtypeStruct(dparts.shape, BF16),
        input_output_aliases={6: 0}, compiler_params=_cparams(("arbitrary",)),
    )(*dq, *dk, *dv, dparts)


def _small_assemble(dp, dsm, sm, ut_hi, ut_lo, dparts, tr):
    R = sm.shape[0]
    qk = GLA_H * GLA_DK

    def body(dpf_ref, dpb_ref, dsmf_ref, dsmb_ref, sm_ref, uth_ref, utl_ref, _, o_ref, dup_ref, acc_ref, acc2_ref):
        @pl.when(pl.program_id(0) == 0)
        def _():
            dup_ref[...] = jnp.zeros_like(dup_ref)
            acc_ref[...] = jnp.zeros_like(acc_ref)
            acc2_ref[...] = jnp.zeros_like(acc2_ref)

        ssd = dsmf_ref[0] + dsmb_ref[0]
        for g in range(1, SSM_G):
            ssd = ssd + (dsmf_ref[g] + dsmb_ref[g])
        acc2_ref[0:1, :] += _colsum(ssd)
        sm_hi, sm_lo = _split(sm_ref[...])
        out = ssd
        for d, dp_ref in enumerate((dpf_ref, dpb_ref)):
            dpd = dp_ref[...]
            out = out + _nn3(dpd, uth_ref[d], utl_ref[d])
            p_hi, p_lo = _split(dpd)
            dup_ref[d] += _tn(sm_hi, p_hi) + _tn(sm_lo, p_hi) + _tn(sm_hi, p_lo)
            acc_ref[d:d + 1, :] += _colsum(dpd)
        o_ref[...] = out.astype(BF16)

    return pl.pallas_call(
        body, name="small_assemble", grid=(R // tr,),
        in_specs=[_row_spec(tr, qk)] * 2 + [pl.BlockSpec((SSM_G, tr, 128), lambda i: (0, i, 0))] * 2
        + [_row_spec(tr, 128), pl.BlockSpec((2, qk, 128), lambda i: (0, 0, 0)),
           pl.BlockSpec((2, qk, 128), lambda i: (0, 0, 0)), _ANY],
        out_specs=[_dparts_out(tr, 128, COL_SM), pl.BlockSpec((2, 128, qk), lambda i: (0, 0, 0)), _acc_spec(qk), _acc_spec(128)],
        out_shape=[jax.ShapeDtypeStruct(dparts.shape, BF16), jax.ShapeDtypeStruct((2, 128, qk), F32),
                   jax.ShapeDtypeStruct((8, qk), F32), jax.ShapeDtypeStruct((8, 128), F32)],
        input_output_aliases={7: 0}, compiler_params=_cparams(("arbitrary",)),
    )(*dp, *dsm, sm, ut_hi, ut_lo, dparts)


ADA_ROWS = 16
ADA_TILE = 512


def _dot3_f32(a, b, ca, cb):
    a_hi, a_lo = _split(a)
    b_hi, b_lo = _split(b)
    return _dg(a_hi, b_hi, ca, cb) + _dg(a_lo, b_hi, ca, cb) + _dg(a_hi, b_lo, ca, cb)


def _ada_fwd(cvec, w, b):
    cols = w.shape[1]

    def body(c_ref, w_ref, b_ref, o_ref):
        o_ref[...] = _dot3_f32(_silu(c_ref[...]), w_ref[...], 1, 0) + b_ref[...]

    return pl.pallas_call(
        body, name="ada_fwd", grid=(cols // ADA_TILE,),
        in_specs=[pl.BlockSpec((ADA_ROWS, D), lambda j: (0, 0)), pl.BlockSpec((D, ADA_TILE), lambda j: (0, j)),
                  pl.BlockSpec((1, ADA_TILE), lambda j: (0, j))],
        out_specs=pl.BlockSpec((ADA_ROWS, ADA_TILE), lambda j: (0, j)), out_shape=jax.ShapeDtypeStruct((ADA_ROWS, cols), F32),
        compiler_params=_cparams(("arbitrary",)),
    )(cvec, w, b)


def _adam(w, g, m, v):
    m2 = ADAM_B1 * m + (1.0 - ADAM_B1) * g
    v2 = ADAM_B2 * v + (1.0 - ADAM_B2) * (g * g)
    m_hat = m2 / (1.0 - ADAM_B1 ** ADAM_STEP)
    v_hat = v2 / (1.0 - ADAM_B2 ** ADAM_STEP)
    return -ADAM_LR * (m_hat / (jnp.sqrt(v_hat) + ADAM_EPS) + ADAM_WD * w), m2, v2


def _wada_bwd_adam(cvec, dada, w, m, v):
    rows, cols = w.shape
    tr = _tile(rows, 256, 128)

    def body(c_ref, d_ref, w_ref, m_ref, v_ref, g_ref, dl_ref, m2_ref, v2_ref, p_ref):
        wv = w_ref[...]
        g = _dot3_f32(_silu(c_ref[...]), d_ref[...], 0, 0)
        g_ref[...] = g
        dl_ref[...], m2_ref[...], v2_ref[...] = _adam(wv, g, m_ref[...], v_ref[...])
        p_ref[...] = _dot3_f32(d_ref[...], wv, 1, 1)

    blk = pl.BlockSpec((tr, cols), lambda i: (i, 0))
    return pl.pallas_call(
        body, name="wada_bwd_adam", grid=(rows // tr,),
        in_specs=[pl.BlockSpec((ADA_ROWS, tr), lambda i: (0, i)), pl.BlockSpec((ADA_ROWS, cols), lambda i: (0, 0)), blk, blk, blk],
        out_specs=[blk, blk, blk, blk, pl.BlockSpec((ADA_ROWS, tr), lambda i: (0, i))],
        out_shape=[jax.ShapeDtypeStruct((rows, cols), F32)] * 4 + [jax.ShapeDtypeStruct((ADA_ROWS, rows), F32)],
        compiler_params=_cparams(("arbitrary",)),
    )(cvec, dada, w, m, v)


def _reduce_adam(parts8, w, m, v, name):
    rows, cols = w.shape
    tr = _tile(rows, 64, 16)

    def body(p_ref, w_ref, m_ref, v_ref, g_ref, dl_ref, m2_ref, v2_ref):
        g = p_ref[0].astype(F32) + p_ref[N_CHIPS].astype(F32)
        for j in range(1, N_CHIPS):
            g = g + (p_ref[j].astype(F32) + p_ref[N_CHIPS + j].astype(F32))
        g_ref[...] = g
        dl_ref[...], m2_ref[...], v2_ref[...] = _adam(w_ref[...], g, m_ref[...], v_ref[...])

    blk = pl.BlockSpec((tr, cols), lambda i: (i, 0))
    return pl.pallas_call(
        body, name=name, grid=(rows // tr,), in_specs=[pl.BlockSpec((N_DEV, tr, cols), lambda i: (0, i, 0)), blk, blk, blk],
        out_specs=[blk] * 4, out_shape=[jax.ShapeDtypeStruct((rows, cols), F32)] * 4, compiler_params=_cparams(("arbitrary",)),
    )(parts8, w, m, v)


SMALL_W = 1024


def _sum8(g8):
    rows = g8.shape[1]

    def body(g_ref, o_ref):
        s = g_ref[0]
        for j in range(1, N_DEV):
            s = s + g_ref[j]
        o_ref[...] = s

    return pl.pallas_call(
        body, name="sum8", out_shape=jax.ShapeDtypeStruct((rows, SMALL_W), F32),
        in_specs=[pl.BlockSpec(memory_space=pltpu.VMEM)], out_specs=pl.BlockSpec(memory_space=pltpu.VMEM),
        compiler_params=pltpu.CompilerParams(vmem_limit_bytes=VMEM_LIMIT),
    )(g8)


def _cctx_grad(p8, c_ctx):
    def body(p_ref, c_ref, o_ref):
        s = p_ref[0]
        for chip in range(1, N_CHIPS):
            s = s + p_ref[2 * chip]
        o_ref[...] = s * _dsilu(c_ref[...])

    return pl.pallas_call(
        body, name="cctx_grad", out_shape=jax.ShapeDtypeStruct((1, D), F32),
        in_specs=[pl.BlockSpec(memory_space=pltpu.VMEM)] * 2, out_specs=pl.BlockSpec(memory_space=pltpu.VMEM),
    )(p8, c_ctx)


def _adam_small(w, g, m, v):
    def body(w_ref, g_ref, m_ref, v_ref, dl_ref, m2_ref, v2_ref):
        dl_ref[...], m2_ref[...], v2_ref[...] = _adam(w_ref[...], g_ref[...], m_ref[...], v_ref[...])

    vm = pl.BlockSpec(memory_space=pltpu.VMEM)
    return pl.pallas_call(
        body, name="adam_small", out_shape=[jax.ShapeDtypeStruct(w.shape, F32)] * 3, in_specs=[vm] * 4, out_specs=[vm] * 3,
        compiler_params=pltpu.CompilerParams(vmem_limit_bytes=VMEM_LIMIT),
    )(w, g, m, v)


def _pack(vecs, width=SMALL_W, row_mult=8):
    flat = jnp.concatenate([v.reshape(-1).astype(F32) for v in vecs])
    n = flat.shape[0]
    rows = -(-n // (width * row_mult)) * row_mult
    return jnp.pad(flat, (0, rows * width - n)).reshape(rows, width)


def _unpack(packed, shapes):
    flat = packed.reshape(-1)
    out, off = [], 0
    for s in shapes:
        n = int(np.prod(s))
        out.append(flat[off:off + n].reshape(s))
        off += n
    return out


WEIGHTS = ('c_ctx', 'w_ada', 'b_ada', 'norm1_w', 'w_in', 'gla_up_f', 'gla_bias_f', 'gla_up_b', 'gla_bias_b', 'gla_norm_w',
           'conv_w', 'conv_b', 'dt_bias_f', 'dt_bias_b', 'a_log_f', 'a_log_b', 'd_skip', 'ssm_norm_w', 'w_pa', 'w_pb', 'w_out',
           'norm2_w', 'w_gate', 'w_up', 'w_down', 'final_norm_w')
BIG = ('w_in', 'w_pa', 'w_pb', 'w_out', 'w_gate', 'w_up', 'w_down')
COL_SHARDED = ('w_in', 'w_gate', 'w_up')
SMALL_SHARDED = ('gla_up_f', 'gla_up_b', 'conv_w')
ROW_TILE = 256
ROW_TILE_WIDE = 768
MM_ROWS = 1408
MM_ROWS_SMALL = 768
MM_KROWS = 2816


def _blocks_to_full(g4, name):
    n, r, c = g4.shape
    return g4.transpose(1, 0, 2).reshape(r, n * c) if name in COL_SHARDED else g4.reshape(n * r, c)


def _full_to_blocks(full, name):
    r, c = full.shape
    if name in COL_SHARDED:
        return full.reshape(r, N_CHIPS, c // N_CHIPS).transpose(1, 0, 2)
    return full.reshape(N_CHIPS, r // N_CHIPS, c)


def _permute_in(w_in_full):
    off = np.concatenate([[0], np.cumsum(IN_WIDTHS)])
    cols = [w_in_full[:, off[p]:off[p + 1]] for p in PERM]
    return jnp.concatenate(cols + [jnp.zeros((w_in_full.shape[0], SMALL_PAD), w_in_full.dtype)], axis=1)


def _unpermute_in(wp):
    off = np.concatenate([[0], np.cumsum([IN_WIDTHS[p] for p in PERM])])
    pieces = {p: wp[:, off[i]:off[i + 1]] for i, p in enumerate(PERM)}
    return jnp.concatenate([pieces[p] for p in range(len(IN_WIDTHS))], axis=1)


def _chip_cols(full, chip, n):
    return lax.dynamic_slice_in_dim(full, chip * n, n, axis=1)


def kernel(x, c, ctx, c_ctx, w_ada, b_ada, norm1_w, w_in, gla_up_f, gla_bias_f, gla_up_b, gla_bias_b, gla_norm_w, conv_w, conv_b, dt_bias_f, dt_bias_b, a_log_f, a_log_b, d_skip, ssm_norm_w, w_pa, w_pb, w_out, norm2_w, w_gate, w_up, w_down, final_norm_w, loss_target, m_c_ctx, m_w_ada, m_b_ada, m_norm1_w, m_w_in, m_gla_up_f, m_gla_bias_f, m_gla_up_b, m_gla_bias_b, m_gla_norm_w, m_conv_w, m_conv_b, m_dt_bias_f, m_dt_bias_b, m_a_log_f, m_a_log_b, m_d_skip, m_ssm_norm_w, m_w_pa, m_w_pb, m_w_out, m_norm2_w, m_w_gate, m_w_up, m_w_down, m_final_norm_w, v_c_ctx, v_w_ada, v_b_ada, v_norm1_w, v_w_in, v_gla_up_f, v_gla_bias_f, v_gla_up_b, v_gla_bias_b, v_gla_norm_w, v_conv_w, v_conv_b, v_dt_bias_f, v_dt_bias_b, v_a_log_f, v_a_log_b, v_d_skip, v_ssm_norm_w, v_w_pa, v_w_pb, v_w_out, v_norm2_w, v_w_gate, v_w_up, v_w_down, v_final_norm_w):
    given = dict(locals())
    W = {n: given[n] for n in WEIGHTS}
    M = {n: given["m_" + n] for n in WEIGHTS}
    V = {n: given["v_" + n] for n in WEIGHTS}
    L, Lc = x.shape[1], ctx.shape[1]
    tr = ROW_TILE
    assert L % tr == 0 and Lc % tr == 0 and L % Lc == 0 and Lc % SSM_C == 0
    n_lat_tiles = L // tr
    trw = _tile(L + Lc, ROW_TILE_WIDE, tr)
    xi, yi, ci = _place()
    chip, me = 2 * xi + yi, 4 * xi + 2 * yi + ci
    x2, ctx2 = x[0], ctx[0]

    g0 = _allgather_small(_pack([c[0]] + [W[n][0] for n in SMALL_SHARDED]), "gather_c")
    g0 = g0.reshape(N_DEV, -1)
    c_all = g0[:, :D]
    small_full, off = {}, D
    for n in SMALL_SHARDED:
        r, cols = W[n].shape[1:]
        small_full[n] = jnp.concatenate([g0[2 * k, off:off + r * cols].reshape(r, cols) for k in range(N_CHIPS)], axis=1)
        off += r * cols
    up_f, up_b, conv_w_full = (small_full[n] for n in SMALL_SHARDED)

    cvec = jnp.zeros((ADA_ROWS, D), F32).at[:N_DEV].set(c_all).at[N_DEV].set(c_ctx)
    ada_cols = w_ada.shape[2]
    ada_part = _ada_fwd(cvec, w_ada[0], _chip_cols(b_ada, chip, ada_cols))
    g1_all = _allgather_small(ada_part, "gather_ada")
    ada_full = jnp.concatenate([g1_all[2 * k] for k in range(N_CHIPS)], axis=1)
    mine = lax.dynamic_slice_in_dim(ada_full, me, 1, axis=0)
    sh1, sc1, g1, sh2, sc2, g2 = (mine[:, k * D:(k + 1) * D] for k in range(6))
    csh1, csc1 = ada_full[N_DEV:N_DEV + 1, :D], ada_full[N_DEV:N_DEV + 1, D:2 * D]
    mod = jnp.stack([jnp.stack([sh1, sc1]), jnp.stack([csh1, csc1])])

    later = [n for n in BIG if n != 'w_in']

    def lr_rows(up, base):
        return jnp.zeros((128, GLA_H * GLA_DK), F32).at[base:base + GLA_RANK].set(up)
    u2 = jnp.stack([lr_rows(up_f, SM_LRF), lr_rows(up_b, SM_LRB)])
    u2_hi = u2.astype(BF16)
    u2_lo = (u2 - u2_hi.astype(F32)).astype(BF16)
    ut = u2.transpose(0, 2, 1)
    ut_hi = ut.astype(BF16)
    ut_lo = (ut - ut_hi.astype(F32)).astype(BF16)
    gbias = jnp.stack([gla_bias_f, gla_bias_b])
    kc = _ssd_consts(jnp.stack([dt_bias_f[0], dt_bias_b[0]]), jnp.stack([a_log_f[0], a_log_b[0]]))
    gw4 = jnp.tile(gla_norm_w, (1, GLA_H))
    dskip_exp = jnp.repeat(d_skip, SSM_P, axis=1)
    n_gla = (L // GLA_C, Lc // GLA_C)
    n_ssd = (L // SSM_C, Lc // SSM_C)

    h1, (w_in_blocks,) = _norm_mod(x2, ctx2, norm1_w, mod, n_lat_tiles, tr, _gather_split_comm(w_in[0].astype(BF16)))
    full = {'w_in': _blocks_to_full(w_in_blocks, 'w_in')}
    wp = _permute_in(full['w_in'])
    parts = _mm(h1, wp, "nn", ACT, "mm_in", tm=MM_ROWS, tn=PW // 3)
    sm = _mm(h1, wp[:, COL_SM:], "nn", F32, "mm_in_small", tm=MM_ROWS)
    xbc = _conv_fwd(parts, conv_w_full, conv_b, L // Lc, Lc)
    *o2, gla_hist = _gla_fwd(parts, sm, u2_hi, u2_lo, gbias, *n_gla)
    (*y2, ssd_hist), gathered = _ssd_fwd(xbc, sm, kc, *n_ssd, _gather_comm([W[n][0].astype(BF16) for n in later]))
    full.update({n: _blocks_to_full(g, n) for n, g in zip(later, gathered)})
    oan = _gla_out(o2, parts, gw4, trw)
    obn = _ssd_out(y2, xbc, parts, dskip_exp, ssm_norm_w, tr)
    ya = _mm(oan, full['w_pa'], "nn", ACT, "mm_pa", tm=MM_ROWS)
    yb = _mm(obn, full['w_pb'], "nn", ACT, "mm_pb", tm=MM_ROWS)
    merged = _merge(ya, yb, parts, trw)
    mix = _mm(merged, full['w_out'], "nn", ACT, "mm_out", tm=MM_ROWS)
    h2, u = _resid_norm_mod(x2, ctx2, mix, g1, norm2_w, sh2, sc2, n_lat_tiles, tr)
    gp = _mm(u, full['w_gate'], "nn", ACT, "mm_gate", tm=MM_ROWS, tn=D_FF)
    up = _mm(u, full['w_up'], "nn", ACT, "mm_up", tm=MM_ROWS, tn=D_FF)
    f = _mm(gp, full['w_down'], "nn", ACT, "mm_down", tm=MM_ROWS_SMALL, tk=D_FF // 2, swiglu_up=up)
    dh3, df, acc_loss = _loss_head(h2, f, loss_target[0], g2, final_norm_w[None], n_lat_tiles, tr)

    dw = {}
    da = _mm(df, full['w_down'], "nt", ACT, "mm_down_dx", tm=MM_ROWS, tn=D_FF)
    dw['w_down'] = _mm(gp, df, "tn", BF16, "mm_down_dw", tm=D_FF // 2, tk=MM_ROWS, swiglu_up=up)
    dgp, dup = _swiglu_act_bwd(da, gp, up, tr)
    du_a = _mm(dgp, full['w_gate'], "nt", ACT, "mm_gate_dx", tm=MM_ROWS, tk=D_FF)
    du_b = _mm(dup, full['w_up'], "nt", ACT, "mm_up_dx", tm=MM_ROWS, tk=D_FF)
    dw['w_gate'] = _mm(u, dgp, "tn", BF16, "mm_gate_dw", tm=D, tn=D_FF // 2, tk=MM_KROWS)
    dw['w_up'] = _mm(u, dup, "tn", BF16, "mm_up_dw", tm=D, tn=D_FF // 2, tk=MM_KROWS)
    dh2, dmix, acc_ffn = _ffn_in_bwd(du_a, du_b, h2, dh3, mix, sc2, g1, norm2_w, trw)
    dmerged = _mm(dmix, full['w_out'], "nt", ACT, "mm_out_dx", tm=MM_ROWS)
    dw['w_out'] = _mm(merged, dmix, "tn", BF16, "mm_out_dw", tm=D, tk=MM_KROWS)
    dya, dyb, dparts = _merge_bwd(dmerged, ya, yb, parts, lax.empty((L + Lc, PW), BF16), trw)
    doan = _mm(dya, full['w_pa'], "nt", ACT, "mm_pa_dx", tm=MM_ROWS)
    dw['w_pa'] = _mm(oan, dya, "tn", BF16, "mm_pa_dw", tm=D, tk=MM_KROWS)
    dobn = _mm(dyb, full['w_pb'], "nt", ACT, "mm_pb_dx", tm=MM_ROWS)
    dw['w_pb'] = _mm(obn, dyb, "tn", BF16, "mm_pb_dw", tm=D, tk=MM_KROWS)
    do, dparts, acc_gla = _gla_out_bwd(doan, o2, parts, gw4, dparts, trw)
    dq, dk, dv, dpre = _gla_bwd(do, parts, sm, u2_hi, u2_lo, gbias, gla_hist, *n_gla)
    dy, dparts, acc_ssd = _ssd_out_bwd(dobn, y2, xbc, parts, dskip_exp, ssm_norm_w, dparts, tr)
    (dx_scan, db_scan, dc_scan, dsm, acc_alog), exchanged = _ssd_bwd(
        dy, xbc, sm, kc, ssd_hist, *n_ssd, _exchange_comm([_full_to_blocks(dw[n], n) for n in later]))
    exchanged = dict(zip(later, exchanged))
    dparts, acc_conv = _conv_bwd(dx_scan, db_scan, dc_scan, dy, dskip_exp, parts, conv_w_full, conv_b, dparts, L // Lc, Lc)
    dparts = _gla_assemble(dq, dk, dv, dparts, trw)
    dparts, dup_gla, acc_gbias, acc_dtb = _small_assemble(dpre, dsm, sm, ut_hi, ut_lo, dparts, trw)
    dw['w_in'] = _unpermute_in(_mm(h1, dparts, "tn", BF16, "mm_in_dw", tm=D, tn=PW // 9, tk=MM_KROWS))
    dh1, (exchanged['w_in'],) = _mm(dparts, wp, "nt", F32, "mm_in_dx", tm=MM_ROWS_SMALL, tk=PW // 3,
                                    comm=_exchange_comm([_full_to_blocks(dw['w_in'], 'w_in')]))
    dx, acc_n1 = _norm1_bwd(dh1, x2, ctx2, dh2, norm1_w, mod, n_lat_tiles, tr)

    partial = dict(
        norm1_w=acc_n1[0, 2] + acc_n1[1, 2],
        gla_up_f=dup_gla[0, SM_LRF:SM_LRF + GLA_RANK], gla_bias_f=acc_gbias[0],
        gla_up_b=dup_gla[1, SM_LRB:SM_LRB + GLA_RANK], gla_bias_b=acc_gbias[1],
        gla_norm_w=acc_gla[0].reshape(GLA_H, GLA_DV).sum(0),
        conv_w=acc_conv[:SSM_CONV], conv_b=acc_conv[SSM_CONV],
        dt_bias_f=acc_dtb[0, SM_DTF:SM_DTF + SSM_HEADS], dt_bias_b=acc_dtb[0, SM_DTB:SM_DTB + SSM_HEADS],
        a_log_f=acc_alog[0, :, 0, :SSM_HPG], a_log_b=acc_alog[1, :, 0, :SSM_HPG],
        d_skip=acc_ssd[1].reshape(SSM_HEADS, SSM_P).sum(1), ssm_norm_w=acc_ssd[0],
        norm2_w=acc_ffn[2], final_norm_w=acc_loss[0],
    )
    dada = jnp.concatenate([acc_n1[0, 1], acc_n1[0, 0], acc_ffn[3], acc_ffn[1], acc_ffn[0], acc_loss[1]])
    dada_ctx = jnp.concatenate([acc_n1[1, 1], acc_n1[1, 0], jnp.zeros((4 * D,), F32)])
    names = list(partial)
    payload = [partial[n] for n in names] + [dada + dada_ctx, dada_ctx, acc_loss[2], dada]
    sizes = [int(np.prod(p.shape)) for p in payload]
    g8 = _allgather_small(_pack(payload), "gather_small_grads")
    summed = _unpack(_sum8(g8), [(s,) for s in sizes])
    grads = {n: s.reshape(W[n].shape if n not in SMALL_SHARDED else partial[n].shape) for n, s in zip(names, summed)}
    grads['b_ada'] = summed[len(names)].reshape(b_ada.shape)
    dada_ctx_sum = summed[len(names) + 1]
    loss = jnp.sum(summed[len(names) + 2])
    dada_all = g8.reshape(N_DEV, -1)[:, sum(sizes[:-1]):sum(sizes)]

    dada16 = jnp.zeros((ADA_ROWS, ada_cols), F32)
    dada16 = dada16.at[:N_DEV].set(_chip_cols(dada_all, chip, ada_cols)).at[N_DEV].set(_chip_cols(dada_ctx_sum[None], chip, ada_cols)[0])
    g_wada, dl_wada, m_wada, v_wada, p16 = _wada_bwd_adam(cvec, dada16, w_ada[0], m_w_ada[0], v_w_ada[0])
    p8 = _allgather_small(p16[N_DEV:], "gather_cctx")
    grads['c_ctx'] = _cctx_grad(p8[:, 0:1, :], c_ctx[None])[0]
    for n in SMALL_SHARDED:
        grads[n] = _chip_cols(grads[n], chip, W[n].shape[2])[None]

    small = [n for n in WEIGHTS if n not in BIG and n != 'w_ada']
    shapes = [W[n].shape for n in small]
    dl_s, m_s, v_s = _adam_small(*[_pack([d[n] for n in small]) for d in (W, grads, M, V)])
    delta = dict(zip(small, _unpack(dl_s, shapes)))
    new_m = dict(zip(small, _unpack(m_s, shapes)))
    new_v = dict(zip(small, _unpack(v_s, shapes)))
    grads['w_ada'], delta['w_ada'], new_m['w_ada'], new_v['w_ada'] = g_wada[None], dl_wada[None], m_wada[None], v_wada[None]

    for n in BIG:
        g, dl, m2, v2 = _reduce_adam(exchanged[n], W[n][0], M[n][0], V[n][0], "adam_" + n)
        grads[n], delta[n], new_m[n], new_v[n] = g[None], dl[None], m2[None], v2[None]

    return (loss, dx[None], *[grads[n] for n in WEIGHTS], *[delta[n] for n in WEIGHTS],
            *[new_m[n] for n in WEIGHTS], *[new_v[n] for n in WEIGHTS])
```

```python
import numpy as np
import jax
import jax.numpy as jnp
from jax import lax
from jax.experimental import pallas as pl
from jax.experimental.pallas import tpu as pltpu

F32 = jnp.float32
BF16 = jnp.bfloat16
ACT = BF16
MESH = pl.DeviceIdType.MESH

D = 1024
EPS = 1e-6
GRID_W = 64
GLA_H, GLA_DK, GLA_DV, GLA_RANK, GLA_TAU = 4, 128, 256, 16, 16.0
GLA_C = 128
SSM_INNER, SSM_P, SSM_HEADS, SSM_G, SSM_HPG, SSM_N = 2048, 64, 32, 4, 8, 128
SSM_C = 128
SSM_CONV, CONV_LEFT = 4, 2
D_FF = 2816
IN_WIDTHS = (512, 512, 1024, 1024, 16, 16, 2048, 2048, 512, 512, 32, 32, 1024, 1024)
D_IN = sum(IN_WIDTHS)
PERM = (7, 8, 9, 3, 6, 0, 1, 2, 12, 13, 4, 5, 10, 11)
PW = 10368
SMALL_PAD = PW - D_IN
COL_XBC, COL_R, COL_Z, COL_Q, COL_K, COL_V, COL_GA, COL_GB, COL_SM = 0, 3072, 4096, 6144, 6656, 7168, 8192, 9216, 10240
SM_LRF, SM_LRB, SM_DTF, SM_DTB = 0, 16, 32, 64
EXP_CLAMP = 80.0
ADAM_LR, ADAM_B1, ADAM_B2, ADAM_EPS, ADAM_WD, ADAM_STEP = 0.001, 0.9, 0.999, 1e-08, 0.01, 10
N_CHIPS, N_DEV = 4, 8
VMEM_LIMIT = 56 * 1024 * 1024


def _cparams(sem=None):
    return pltpu.CompilerParams(dimension_semantics=sem, vmem_limit_bytes=VMEM_LIMIT)


def _dg(a, b, ca, cb):
    return lax.dot_general(a, b, (((ca,), (cb,)), ((), ())), preferred_element_type=F32)


def _nn(a, b):
    return _dg(a, b, 1, 0)


def _nt(a, b):
    return _dg(a, b, 1, 1)


def _tn(a, b):
    return _dg(a, b, 0, 0)


def _bf(x):
    return x.astype(BF16)


def _f32(ref):
    return ref[...].astype(F32)


def _split(x):
    hi = x.astype(BF16)
    return hi, (x - hi.astype(F32)).astype(BF16)


def _nn_x(a, b_exact):
    hi, lo = _split(a)
    return _nn(hi, b_exact) + _nn(lo, b_exact)


def _x_nn(a_exact, b):
    hi, lo = _split(b)
    return _nn(a_exact, hi) + _nn(a_exact, lo)


def _nn3(a, b_hi, b_lo):
    hi, lo = _split(a)
    return _nn(hi, b_hi) + _nn(lo, b_hi) + _nn(hi, b_lo)


def _sigmoid(x):
    return 1.0 / (1.0 + jnp.exp(-x))


def _silu(x):
    return x * _sigmoid(x)


def _dsilu(x):
    s = _sigmoid(x)
    return s * (1.0 + x * (1.0 - s))


def _softplus(x):
    return jnp.maximum(x, 0.0) + jnp.log(1.0 + jnp.exp(-jnp.abs(x)))


def _log_sigmoid(x):
    return jnp.minimum(x, 0.0) - jnp.log(1.0 + jnp.exp(-jnp.abs(x)))


def _tile(n, target, mult=8):
    best = None
    for t in range(mult, min(n, target) + 1, mult):
        if n % t == 0:
            best = t
    assert best is not None, (n, target, mult)
    return best


def _mm(a, b, mode, out_dtype, name, tm=512, tn=1024, tk=2048, comm=None, swiglu_up=None):
    if mode == "nn":
        (M, K), N = a.shape, b.shape[1]
    elif mode == "nt":
        (M, K), N = a.shape, b.shape[0]
    else:
        (K, M), N = a.shape, b.shape[1]
    tm, tn, tk = _tile(M, tm, 128), _tile(N, tn, 128), _tile(K, tk, 128)
    nk = K // tk
    ca, cb = {"nn": (1, 0), "nt": (1, 1), "tn": (0, 0)}[mode]

    def body(a_ref, *rest):
        if swiglu_up is None:
            (b_ref, o_ref, *acc), av = rest, a_ref[...]
        else:
            (up_ref, b_ref, o_ref, *acc) = rest
            av = (_silu(a_ref[...].astype(F32)) * up_ref[...].astype(F32)).astype(BF16)
        part = _dg(av, b_ref[...], ca, cb)
        if nk == 1:
            o_ref[...] = part.astype(out_dtype)
        else:
            k = pl.program_id(2)

            @pl.when(k == 0)
            def _():
                acc[0][...] = part

            @pl.when(k > 0)
            def _():
                acc[0][...] += part

            @pl.when(k == nk - 1)
            def _():
                o_ref[...] = acc[0][...].astype(out_dtype)

    a_spec = pl.BlockSpec((tk, tm), lambda i, j, k: (k, i)) if mode == "tn" else pl.BlockSpec((tm, tk), lambda i, j, k: (i, k))
    b_spec = pl.BlockSpec((tn, tk), lambda i, j, k: (j, k)) if mode == "nt" else pl.BlockSpec((tk, tn), lambda i, j, k: (k, j))
    gi, gj = M // tm, N // tn
    scratch = [pltpu.VMEM((tm, tn), F32)] if nk > 1 else []
    out_spec, out_shape = pl.BlockSpec((tm, tn), lambda i, j, k: (i, j)), jax.ShapeDtypeStruct((M, N), out_dtype)
    if comm is None:
        lhs = [a] if swiglu_up is None else [a, swiglu_up]
        return pl.pallas_call(
            body, name=name, grid=(gi, gj, nk), in_specs=[a_spec] * len(lhs) + [b_spec], out_specs=out_spec,
            out_shape=out_shape, scratch_shapes=scratch, compiler_params=_cparams(("arbitrary", "arbitrary", "arbitrary")),
        )(*lhs, b)
    assert swiglu_up is None
    at = lambda i, j, k: (pl.program_id(0) == i) & (pl.program_id(1) == j) & (pl.program_id(2) == k)
    hosted = _hosted(body, 2, 1, len(scratch), comm, lambda: at(0, 0, 0), lambda: at(gi - 1, 0, 0),
                     lambda: at(gi - 1, gj - 1, nk - 1))
    outs = pl.pallas_call(
        hosted, name=name, grid=(gi, gj, nk), in_specs=[a_spec, b_spec] + [_ANY] * len(comm.arrays),
        out_specs=[out_spec] + [_ANY] * len(comm.out_shape), out_shape=[out_shape] + comm.out_shape,
        scratch_shapes=scratch + comm.scratch, compiler_params=_cparams(("arbitrary", "arbitrary", "arbitrary")),
    )(a, b, *comm.arrays)
    return outs[0], outs[1:]


def _place():
    return lax.axis_index("x"), lax.axis_index("y"), lax.axis_index("c")


def _flip(v, bit):
    return 1 - v if bit else v


def _allgather_small(v, name):
    R, C = v.shape

    def body(v_ref, out_ref, send_sems, recv_sems, local_sem):
        x, y, c = _place()
        me = 4 * x + 2 * y + c
        mine = pltpu.make_async_copy(v_ref, out_ref.at[me], local_sem)
        mine.start()

        def peer(r):
            return _flip(x, (r >> 2) & 1), _flip(y, (r >> 1) & 1), _flip(c, r & 1)

        sends = [pltpu.make_async_remote_copy(
            src_ref=v_ref, dst_ref=out_ref.at[me], send_sem=send_sems.at[r - 1], recv_sem=recv_sems.at[r - 1],
            device_id=peer(r), device_id_type=MESH) for r in range(1, N_DEV)]
        for cp in sends:
            cp.start()
        for r in range(1, N_DEV):
            px, py, pc = peer(r)
            pltpu.make_async_remote_copy(
                src_ref=v_ref, dst_ref=out_ref.at[4 * px + 2 * py + pc], send_sem=send_sems.at[r - 1],
                recv_sem=recv_sems.at[r - 1], device_id=(x, y, c), device_id_type=MESH).wait_recv()
        for cp in sends:
            cp.wait_send()
        mine.wait()

    return pl.pallas_call(
        body, name=name, out_shape=jax.ShapeDtypeStruct((N_DEV, R, C), v.dtype),
        in_specs=[pl.BlockSpec(memory_space=pltpu.VMEM)], out_specs=pl.BlockSpec(memory_space=pltpu.VMEM),
        scratch_shapes=[pltpu.SemaphoreType.DMA((N_DEV - 1,)), pltpu.SemaphoreType.DMA((N_DEV - 1,)), pltpu.SemaphoreType.DMA],
        compiler_params=pltpu.CompilerParams(vmem_limit_bytes=VMEM_LIMIT),
    )(v)


_CHIP_RELATIONS = ((1, 0), (0, 1), (1, 1))


class _Comm:
    def __init__(self, arrays, out_shape, scratch, start, middle, finish):
        self.arrays, self.out_shape, self.scratch = arrays, out_shape, scratch
        self.start, self.middle, self.finish = start, middle, finish


def _hosted(body, n_in, n_out, n_scratch, comm, first, middle, last):
    nc, no = len(comm.arrays), len(comm.out_shape)

    def wrapped(*refs):
        a = n_in + nc
        b = a + n_out + no
        ins, c_ins, outs, c_outs = refs[:n_in], refs[n_in:a], refs[a:a + n_out], refs[a + n_out:b]
        scratch, c_sems = refs[b:b + n_scratch], refs[b + n_scratch:]

        @pl.when(first())
        def _():
            comm.start(c_ins, c_outs, c_sems)

        body(*ins, *outs, *scratch)
        if comm.middle is not None:
            @pl.when(middle())
            def _():
                comm.middle(c_ins, c_outs, c_sems)

        @pl.when(last())
        def _():
            comm.finish(c_ins, c_outs, c_sems)

    return wrapped


_ANY = pl.BlockSpec(memory_space=pl.ANY)


def _gather_split_comm(shard):
    rows, cols = shard.shape
    half = rows // 2

    def copies(kind, ins, outs, sems):
        (in_ref,), (out_ref,), (send_sems, recv_sems, local_sem) = ins, outs, sems
        x, y, c = _place()
        chip = 2 * x + y
        mine = pl.ds(pl.multiple_of(c * half, 16), half)
        if kind == "local":
            return [pltpu.make_async_copy(in_ref, out_ref.at[chip], local_sem)]
        made = []
        for j, (fx, fy) in enumerate(_CHIP_RELATIONS):
            px, py = _flip(x, fx), _flip(y, fy)
            landed = out_ref.at[2 * px + py, mine]
            src, dst, k, to = {
                "send": (in_ref.at[mine], out_ref.at[chip, mine], j, (px, py, c)),
                "landed": (landed, landed, j, (x, y, c)),
                "passed": (landed, landed, 3 + j, (x, y, 1 - c)),
                "arrivals": (landed, landed, 3 + j, (x, y, c)),
            }[kind]
            made.append(pltpu.make_async_remote_copy(src_ref=src, dst_ref=dst, send_sem=send_sems.at[k],
                                                     recv_sem=recv_sems.at[k], device_id=to, device_id_type=MESH))
        return made

    def start(ins, outs, sems):
        for cp in copies("local", ins, outs, sems) + copies("send", ins, outs, sems):
            cp.start()

    def middle(ins, outs, sems):
        for got, fwd in zip(copies("landed", ins, outs, sems), copies("passed", ins, outs, sems)):
            got.wait_recv()
            fwd.start()

    def finish(ins, outs, sems):
        for cp in copies("arrivals", ins, outs, sems):
            cp.wait_recv()
        for cp in copies("send", ins, outs, sems) + copies("passed", ins, outs, sems):
            cp.wait_send()
        for cp in copies("local", ins, outs, sems):
            cp.wait()

    return _Comm([shard], [jax.ShapeDtypeStruct((N_CHIPS, rows, cols), shard.dtype)],
                 [pltpu.SemaphoreType.DMA((6,)), pltpu.SemaphoreType.DMA((6,)), pltpu.SemaphoreType.DMA],
                 start, middle, finish)


def _gather_comm(shards):
    n = len(shards)

    def copies(kind, ins, outs, sems):
        send_sems, recv_sems, local_sems = sems
        x, y, c = _place()
        chip = 2 * x + y
        if kind == "local":
            return [pltpu.make_async_copy(ins[i], outs[i].at[chip], local_sems.at[i]) for i in range(n)]
        made = []
        for i in range(n):
            for j, (fx, fy) in enumerate(_CHIP_RELATIONS):
                px, py = _flip(x, fx), _flip(y, fy)
                slot, to = (chip, (px, py, c)) if kind == "send" else (2 * px + py, (x, y, c))
                made.append(pltpu.make_async_remote_copy(
                    src_ref=ins[i], dst_ref=outs[i].at[slot], send_sem=send_sems.at[i, j], recv_sem=recv_sems.at[i, j],
                    device_id=to, device_id_type=MESH))
        return made

    def start(ins, outs, sems):
        for cp in copies("local", ins, outs, sems) + copies("send", ins, outs, sems):
            cp.start()

    def finish(ins, outs, sems):
        for cp in copies("recv", ins, outs, sems):
            cp.wait_recv()
        for cp in copies("send", ins, outs, sems):
            cp.wait_send()
        for cp in copies("local", ins, outs, sems):
            cp.wait()

    return _Comm(list(shards), [jax.ShapeDtypeStruct((N_CHIPS,) + s.shape, s.dtype) for s in shards],
                 [pltpu.SemaphoreType.DMA((n, 3)), pltpu.SemaphoreType.DMA((n, 3)), pltpu.SemaphoreType.DMA((n,))],
                 start, None, finish)


def _exchange_comm(blocks):
    n = len(blocks)

    def copies(kind, ins, outs, sems):
        send_sems, recv_sems, local_sems = sems
        x, y, c = _place()
        chip = 2 * x + y
        me, sibling = (x, y, c), (x, y, 1 - c)

        def remote(src, dst, i, j, to):
            return pltpu.make_async_remote_copy(src_ref=src, dst_ref=dst, send_sem=send_sems.at[i, j],
                                                recv_sem=recv_sems.at[i, j], device_id=to, device_id_type=MESH)

        made = []
        for i in range(n):
            if kind == "local":
                made.append(pltpu.make_async_copy(ins[i].at[chip], outs[i].at[chip], local_sems.at[i]))
                continue
            for j, (fx, fy) in enumerate(_CHIP_RELATIONS):
                px, py = _flip(x, fx), _flip(y, fy)
                src = 2 * px + py
                if kind == "first":
                    made.append(remote(ins[i].at[src], outs[i].at[chip], i, j, (px, py, c)))
                elif kind == "landed":
                    made.append(remote(ins[i].at[src], outs[i].at[src], i, j, me))
                elif kind == "passed":
                    made.append(remote(outs[i].at[src], outs[i].at[N_CHIPS + src], i, 4 + j, sibling))
            if kind == "first":
                made.append(remote(ins[i].at[chip], outs[i].at[N_CHIPS + chip], i, 3, sibling))
            if kind == "arrivals":
                made += [remote(ins[i].at[0], outs[i].at[0], i, j, me) for j in (3, 4, 5, 6)]
        return made

    def start(ins, outs, sems):
        for cp in copies("local", ins, outs, sems) + copies("first", ins, outs, sems):
            cp.start()

    def middle(ins, outs, sems):
        for got, fwd in zip(copies("landed", ins, outs, sems), copies("passed", ins, outs, sems)):
            got.wait_recv()
            fwd.start()

    def finish(ins, outs, sems):
        for cp in copies("arrivals", ins, outs, sems):
            cp.wait_recv()
        for cp in copies("first", ins, outs, sems) + copies("passed", ins, outs, sems):
            cp.wait_send()
        for cp in copies("local", ins, outs, sems):
            cp.wait()

    return _Comm(list(blocks), [jax.ShapeDtypeStruct((N_DEV,) + b.shape[1:], b.dtype) for b in blocks],
                 [pltpu.SemaphoreType.DMA((n, 7)), pltpu.SemaphoreType.DMA((n, 7)), pltpu.SemaphoreType.DMA((n,))],
                 start, middle, finish)


def _row_spec(tr, w, col=0):
    return pl.BlockSpec((tr, w), lambda i: (i, col))


def _vec_spec(w):
    return pl.BlockSpec((1, w), lambda i: (0, 0))


def _acc_spec(w):
    return pl.BlockSpec((8, w), lambda i: (0, 0))


def _rms(x):
    return lax.rsqrt(jnp.mean(x * x, axis=-1, keepdims=True) + EPS)


def _rms_bwd(dn, n, rstd):
    return rstd * (dn - n * jnp.mean(dn * n, axis=-1, keepdims=True))


def _colsum(x):
    return jnp.sum(x, axis=0, keepdims=True)


def _zero_first(ref):
    @pl.when(pl.program_id(0) == 0)
    def _():
        ref[...] = jnp.zeros_like(ref)


def _x_specs(tr, n_lat_tiles):
    return [pl.BlockSpec((tr, D), lambda i: (jnp.minimum(i, n_lat_tiles - 1), 0)),
            pl.BlockSpec((tr, D), lambda i: (jnp.maximum(i - n_lat_tiles, 0), 0))]


def _x_tile(x_ref, c_ref, n_lat_tiles):
    return jnp.where(pl.program_id(0) >= n_lat_tiles, c_ref[...], x_ref[...])


def _norm_mod(x, ctx, w, mod, n_lat_tiles, tr, comm):
    R = x.shape[0] + ctx.shape[0]
    n = R // tr

    def body(x_ref, c_ref, w_ref, mod_ref, o_ref):
        xv = _x_tile(x_ref, c_ref, n_lat_tiles)
        nw = xv * _rms(xv) * w_ref[...]
        o_ref[...] = (nw * (1.0 + mod_ref[0, 1]) + mod_ref[0, 0]).astype(BF16)

    at = lambda i: pl.program_id(0) == i
    outs = pl.pallas_call(
        _hosted(body, 4, 1, 0, comm, lambda: at(0), lambda: at(n // 2), lambda: at(n - 1)), name="norm1_mod", grid=(n,),
        in_specs=_x_specs(tr, n_lat_tiles) + [_vec_spec(D), pl.BlockSpec(
            (1, 2, 1, D), lambda i: (jnp.where(i >= n_lat_tiles, 1, 0), 0, 0, 0))] + [_ANY] * len(comm.arrays),
        out_specs=[_row_spec(tr, D)] + [_ANY] * len(comm.out_shape),
        out_shape=[jax.ShapeDtypeStruct((R, D), BF16)] + comm.out_shape, scratch_shapes=comm.scratch,
        compiler_params=_cparams(("arbitrary",)),
    )(x, ctx, w, mod, *comm.arrays)
    return outs[0], outs[1:]


def _resid_norm_mod(x, ctx, mix, g1, w2, sh2, sc2, n_lat_tiles, tr):
    R = x.shape[0] + ctx.shape[0]

    def body(x_ref, c_ref, mix_ref, g1_ref, w_ref, sh_ref, sc_ref, h2_ref, u_ref):
        h2 = _x_tile(x_ref, c_ref, n_lat_tiles) + g1_ref[...] * mix_ref[...]
        h2_ref[...] = h2
        n = h2 * _rms(h2) * w_ref[...]
        u_ref[...] = (n * (1.0 + sc_ref[...]) + sh_ref[...]).astype(BF16)

    return pl.pallas_call(
        body, name="resid_norm2_mod", grid=(R // tr,),
        in_specs=_x_specs(tr, n_lat_tiles) + [_row_spec(tr, D)] + [_vec_spec(D)] * 4,
        out_specs=[_row_spec(tr, D), _row_spec(tr, D)],
        out_shape=[jax.ShapeDtypeStruct((R, D), F32), jax.ShapeDtypeStruct((R, D), BF16)],
        compiler_params=_cparams(("arbitrary",)),
    )(x, ctx, mix, g1, w2, sh2, sc2)


def _loss_head(h2, f, target, g2, fw, n_lat_tiles, tr):
    R = h2.shape[0]

    def body(h2_ref, f_ref, t_ref, g2_ref, fw_ref, dh3_ref, df_ref, acc_ref):
        _zero_first(acc_ref)
        lat = pl.program_id(0) < n_lat_tiles
        fv = f_ref[...]
        h3 = h2_ref[...] + g2_ref[...] * fv
        rstd = _rms(h3)
        n = h3 * rstd
        err = n * fw_ref[...] - t_ref[...]
        dy = err * (1.0 / D)
        dh3 = jnp.where(lat, _rms_bwd(dy * fw_ref[...], n, rstd), 0.0)
        dh3_ref[...] = dh3
        df_ref[...] = (g2_ref[...] * dh3).astype(BF16)
        acc_ref[0:1, :] += jnp.where(lat, _colsum(dy * n), 0.0)
        acc_ref[1:2, :] += _colsum(dh3 * fv)
        acc_ref[2:3, :] += jnp.where(lat, _colsum(err * err) * (0.5 / D), 0.0)

    return pl.pallas_call(
        body, name="loss_head", grid=(R // tr,),
        in_specs=[_row_spec(tr, D), _row_spec(tr, D),
                  pl.BlockSpec((tr, D), lambda i: (jnp.minimum(i, n_lat_tiles - 1), 0)), _vec_spec(D), _vec_spec(D)],
        out_specs=[_row_spec(tr, D), _row_spec(tr, D), _acc_spec(D)],
        out_shape=[jax.ShapeDtypeStruct((R, D), F32), jax.ShapeDtypeStruct((R, D), BF16), jax.ShapeDtypeStruct((8, D), F32)],
        compiler_params=_cparams(("arbitrary",)),
    )(h2, f, target, g2, fw)


def _ffn_in_bwd(du_a, du_b, h2, dh3, mix, sc2, g1, w2, tr):
    R = h2.shape[0]

    def body(dua_ref, dub_ref, h2_ref, dh3_ref, mix_ref, sc_ref, g1_ref, w_ref, dh2_ref, dmix_ref, acc_ref):
        _zero_first(acc_ref)
        du = _f32(dua_ref) + _f32(dub_ref)
        h2 = h2_ref[...]
        rstd = _rms(h2)
        n = h2 * rstd
        dnw = du * (1.0 + sc_ref[...])
        dh2 = dh3_ref[...] + _rms_bwd(dnw * w_ref[...], n, rstd)
        dh2_ref[...] = dh2
        dmix_ref[...] = (g1_ref[...] * dh2).astype(BF16)
        acc_ref[0:1, :] += _colsum(du * n * w_ref[...])
        acc_ref[1:2, :] += _colsum(du)
        acc_ref[2:3, :] += _colsum(dnw * n)
        acc_ref[3:4, :] += _colsum(dh2 * mix_ref[...])

    return pl.pallas_call(
        body, name="ffn_in_bwd", grid=(R // tr,),
        in_specs=[_row_spec(tr, D)] * 5 + [_vec_spec(D)] * 3,
        out_specs=[_row_spec(tr, D), _row_spec(tr, D), _acc_spec(D)],
        out_shape=[jax.ShapeDtypeStruct((R, D), F32), jax.ShapeDtypeStruct((R, D), BF16), jax.ShapeDtypeStruct((8, D), F32)],
        compiler_params=_cparams(("arbitrary",)),
    )(du_a, du_b, h2, dh3, mix, sc2, g1, w2)


def _norm1_bwd(dh1, x, ctx, dh2, w1, mod, n_lat_tiles, tr):
    R = x.shape[0] + ctx.shape[0]

    def body(dh1_ref, x_ref, c_ref, dh2_ref, w_ref, mod_ref, dx_ref, acc_ref):
        i = pl.program_id(0)

        @pl.when((i == 0) | (i == n_lat_tiles))
        def _():
            acc_ref[...] = jnp.zeros_like(acc_ref)

        dh1 = dh1_ref[...]
        x = _x_tile(x_ref, c_ref, n_lat_tiles)
        rstd = _rms(x)
        n = x * rstd
        dnw = dh1 * (1.0 + mod_ref[0, 1])

        @pl.when(i < n_lat_tiles)
        def _():
            dx_ref[...] = dh2_ref[...] + _rms_bwd(dnw * w_ref[...], n, rstd)

        acc_ref[0, 0:1, :] += _colsum(dh1 * n * w_ref[...])
        acc_ref[0, 1:2, :] += _colsum(dh1)
        acc_ref[0, 2:3, :] += _colsum(dnw * n)

    sel = lambda i: jnp.where(i >= n_lat_tiles, 1, 0)
    return pl.pallas_call(
        body, name="norm1_bwd", grid=(R // tr,),
        in_specs=[_row_spec(tr, D)] + _x_specs(tr, n_lat_tiles) + [_row_spec(tr, D), _vec_spec(D),
                                                                   pl.BlockSpec((1, 2, 1, D), lambda i: (sel(i), 0, 0, 0))],
        out_specs=[pl.BlockSpec((tr, D), lambda i: (jnp.minimum(i, n_lat_tiles - 1), 0)),
                   pl.BlockSpec((1, 8, D), lambda i: (sel(i), 0, 0))],
        out_shape=[jax.ShapeDtypeStruct((n_lat_tiles * tr, D), F32), jax.ShapeDtypeStruct((2, 8, D), F32)],
        compiler_params=_cparams(("arbitrary",)),
    )(dh1, x, ctx, dh2, w1, mod)


def _swiglu_act_bwd(da, gp, up, tr):
    R = gp.shape[0]

    def body(da_ref, g_ref, u_ref, dg_ref, du_ref):
        def lanes(l, carry):
            sl = pl.ds(pl.multiple_of(l * 128, 128), 128)
            da, g = da_ref[:, sl].astype(F32), g_ref[:, sl].astype(F32)
            dg_ref[:, sl] = (da * u_ref[:, sl].astype(F32) * _dsilu(g)).astype(BF16)
            du_ref[:, sl] = (da * _silu(g)).astype(BF16)
            return carry

        lax.fori_loop(0, D_FF // 128, lanes, 0)

    return pl.pallas_call(
        body, name="swiglu_act_bwd", grid=(R // tr,), in_specs=[_row_spec(tr, D_FF)] * 3, out_specs=[_row_spec(tr, D_FF)] * 2,
        out_shape=[jax.ShapeDtypeStruct((R, D_FF), BF16)] * 2, compiler_params=_cparams(("arbitrary",)),
    )(da, gp, up)


def _merge(ya, yb, parts, tr):
    R = ya.shape[0]

    def body(ya_ref, yb_ref, ga_ref, gb_ref, o_ref):
        o_ref[...] = (_sigmoid(_f32(ga_ref)) * _f32(ya_ref) + _sigmoid(_f32(gb_ref)) * _f32(yb_ref)).astype(BF16)

    return pl.pallas_call(
        body, name="merge", grid=(R // tr,),
        in_specs=[_row_spec(tr, D), _row_spec(tr, D), _row_spec(tr, D, COL_GA // D), _row_spec(tr, D, COL_GB // D)],
        out_specs=_row_spec(tr, D), out_shape=jax.ShapeDtypeStruct((R, D), BF16), compiler_params=_cparams(("arbitrary",)),
    )(ya, yb, parts, parts)


def _dparts_out(tr, w, col, nd=1):
    blk = col // w
    return pl.BlockSpec((tr, w), (lambda i: (i, blk)) if nd == 1 else (lambda i, j: (i, blk + j)))


def _merge_bwd(dm, ya, yb, parts, dparts, tr):
    R = ya.shape[0]

    def body(dm_ref, ya_ref, yb_ref, ga_ref, gb_ref, _, dya_ref, dyb_ref, dg_ref):
        dm = _f32(dm_ref)
        sa, sb = _sigmoid(_f32(ga_ref)), _sigmoid(_f32(gb_ref))
        dya_ref[...] = (dm * sa).astype(BF16)
        dyb_ref[...] = (dm * sb).astype(BF16)
        dg_ref[:, 0:D] = (dm * _f32(ya_ref) * sa * (1.0 - sa)).astype(BF16)
        dg_ref[:, D:2 * D] = (dm * _f32(yb_ref) * sb * (1.0 - sb)).astype(BF16)

    return pl.pallas_call(
        body, name="merge_bwd", grid=(R // tr,),
        in_specs=[_row_spec(tr, D)] * 3 + [_row_spec(tr, D, COL_GA // D), _row_spec(tr, D, COL_GB // D), _ANY],
        out_specs=[_row_spec(tr, D), _row_spec(tr, D), _dparts_out(tr, 2 * D, COL_GA)],
        out_shape=[jax.ShapeDtypeStruct((R, D), BF16)] * 2 + [jax.ShapeDtypeStruct(dparts.shape, BF16)],
        input_output_aliases={5: 2}, compiler_params=_cparams(("arbitrary",)),
    )(dm, ya, yb, parts, parts, dparts)


def _gla_out(o2, parts, gw4, tr):
    R = parts.shape[0]

    def body(of_ref, ob_ref, r_ref, w_ref, out_ref):
        oa = _f32(of_ref) + _f32(ob_ref)
        sr = _silu(_f32(r_ref))
        for h in range(GLA_H):
            s = slice(h * GLA_DV, (h + 1) * GLA_DV)
            o = oa[:, s]
            out_ref[:, s] = (o * _rms(o) * w_ref[:, s] * sr[:, s]).astype(BF16)

    return pl.pallas_call(
        body, name="gla_out", grid=(R // tr,),
        in_specs=[_row_spec(tr, D), _row_spec(tr, D), _row_spec(tr, D, COL_R // D), _vec_spec(D)],
        out_specs=_row_spec(tr, D), out_shape=jax.ShapeDtypeStruct((R, D), BF16), compiler_params=_cparams(("arbitrary",)),
    )(o2[0], o2[1], parts, gw4)


def _gla_out_bwd(dout, o2, parts, gw4, dparts, tr):
    R = parts.shape[0]

    def body(d_ref, of_ref, ob_ref, r_ref, w_ref, _, do_ref, dr_ref, acc_ref):
        _zero_first(acc_ref)
        oa = _f32(of_ref) + _f32(ob_ref)
        r = _f32(r_ref)
        sr = _silu(r)
        dout = _f32(d_ref)
        for h in range(GLA_H):
            s = slice(h * GLA_DV, (h + 1) * GLA_DV)
            o = oa[:, s]
            rstd = _rms(o)
            n = o * rstd
            w = w_ref[:, s]
            dr_ref[:, s] = (dout[:, s] * n * w * _dsilu(r[:, s])).astype(BF16)
            dnw = dout[:, s] * sr[:, s]
            do_ref[:, s] = _rms_bwd(dnw * w, n, rstd).astype(ACT)
            acc_ref[0:1, s] += _colsum(dnw * n)

    return pl.pallas_call(
        body, name="gla_out_bwd", grid=(R // tr,),
        in_specs=[_row_spec(tr, D), _row_spec(tr, D), _row_spec(tr, D), _row_spec(tr, D, COL_R // D), _vec_spec(D), _ANY],
        out_specs=[_row_spec(tr, D), _dparts_out(tr, D, COL_R), _acc_spec(D)],
        out_shape=[jax.ShapeDtypeStruct((R, D), ACT), jax.ShapeDtypeStruct(dparts.shape, BF16), jax.ShapeDtypeStruct((8, D), F32)],
        input_output_aliases={5: 1}, compiler_params=_cparams(("arbitrary",)),
    )(dout, o2[0], o2[1], parts, gw4, dparts)


SSM_GW = SSM_INNER // SSM_G


def _ssd_out(y2, xbc, parts, dskip, nw, tr):
    R = parts.shape[0]

    def body(yf_ref, yb_ref, x_ref, z_ref, ds_ref, w_ref, out_ref):
        def group(g, carry):
            s = pl.ds(pl.multiple_of(g * SSM_GW, SSM_GW), SSM_GW)
            o = (yf_ref[:, s].astype(F32) + yb_ref[:, s].astype(F32) + ds_ref[:, s] * x_ref[:, s].astype(F32)) \
                * _silu(z_ref[:, s].astype(F32))
            out_ref[:, s] = (o * _rms(o) * w_ref[:, s]).astype(BF16)
            return carry

        lax.fori_loop(0, SSM_G, group, 0)

    return pl.pallas_call(
        body, name="ssd_out", grid=(R // tr,),
        in_specs=[_row_spec(tr, SSM_INNER)] * 3 + [_row_spec(tr, SSM_INNER, COL_Z // SSM_INNER),
                                                   _vec_spec(SSM_INNER), _vec_spec(SSM_INNER)],
        out_specs=_row_spec(tr, SSM_INNER), out_shape=jax.ShapeDtypeStruct((R, SSM_INNER), BF16),
        compiler_params=_cparams(("arbitrary",)),
    )(y2[0], y2[1], xbc, parts, dskip, nw)


def _ssd_out_bwd(dout, y2, xbc, parts, dskip, nw, dparts, tr):
    R = parts.shape[0]

    def body(d_ref, yf_ref, yb_ref, x_ref, z_ref, ds_ref, w_ref, _, dy_ref, dz_ref, acc_ref):
        _zero_first(acc_ref)
        x, z = _f32(x_ref), _f32(z_ref)
        pre = _f32(yf_ref) + _f32(yb_ref) + ds_ref[...] * x
        sz = _silu(z)
        ob = pre * sz
        dout = _f32(d_ref)
        for g in range(SSM_G):
            s = slice(g * SSM_GW, (g + 1) * SSM_GW)
            o = ob[:, s]
            rstd = _rms(o)
            n = o * rstd
            dob = _rms_bwd(dout[:, s] * w_ref[:, s], n, rstd)
            dz_ref[:, s] = (dob * pre[:, s] * _dsilu(z[:, s])).astype(BF16)
            dy = dob * sz[:, s]
            dy_ref[:, s] = dy.astype(ACT)
            acc_ref[0:1, s] += _colsum(dout[:, s] * n)
            acc_ref[1:2, s] += _colsum(dy * x[:, s])

    return pl.pallas_call(
        body, name="ssd_out_bwd", grid=(R // tr,),
        in_specs=[_row_spec(tr, SSM_INNER)] * 4 + [_row_spec(tr, SSM_INNER, COL_Z // SSM_INNER),
                                                   _vec_spec(SSM_INNER), _vec_spec(SSM_INNER), _ANY],
        out_specs=[_row_spec(tr, SSM_INNER), _dparts_out(tr, SSM_INNER, COL_Z), _acc_spec(SSM_INNER)],
        out_shape=[jax.ShapeDtypeStruct((R, SSM_INNER), ACT), jax.ShapeDtypeStruct(dparts.shape, BF16),
                   jax.ShapeDtypeStruct((8, SSM_INNER), F32)],
        input_output_aliases={7: 1}, compiler_params=_cparams(("arbitrary",)),
    )(dout, y2[0], y2[1], xbc, parts, dskip, nw, dparts)


CONV_W = SSM_INNER + 2 * SSM_G * SSM_N
CONV_BLK = 1024


CONV_SHIFTS = (-2, -1, 1, 2)


def _conv_mask_table(tr):
    t = np.arange(tr)
    table = np.zeros((2, len(CONV_SHIFTS), tr, 128), np.float32)
    for kind, (pos, seg) in enumerate(((t % GRID_W, GRID_W), (t, tr))):
        for k, s in enumerate(CONV_SHIFTS):
            table[kind, k] = ((pos + s >= 0) & (pos + s < seg)).astype(np.float32)[:, None]
    return jnp.asarray(table)


def _shifted(u, s, mask_ref, tr):
    return u if s == 0 else pltpu.roll(u, (-s) % tr, 0) * mask_ref[0, CONV_SHIFTS.index(s)]


def _conv_mask_spec(tr, n_lat_tiles, row_axis):
    return pl.BlockSpec((1, len(CONV_SHIFTS), tr, 128),
                        lambda *ids: (jnp.where(ids[row_axis] >= n_lat_tiles, 1, 0), 0, 0, 0))


def _conv_fwd(parts, cw, cb, n_lat_tiles, tr):
    R = parts.shape[0]

    n_blk = CONV_W // CONV_BLK

    def body(*refs):
        u_refs, (w_ref, b_ref, mask_ref, o_ref) = refs[:n_blk], refs[n_blk:]
        for k, u_ref in enumerate(u_refs):
            def lanes(l, carry, u_ref=u_ref, k=k):
                sl = pl.ds(pl.multiple_of(l * 128, 128), 128)
                so = pl.ds(pl.multiple_of(k * CONV_BLK + l * 128, 128), 128)
                u, w = u_ref[:, sl].astype(F32), w_ref[:, so]
                acc = jnp.zeros_like(u) + b_ref[:, so]
                for j in range(SSM_CONV):
                    acc = acc + _shifted(u, j - CONV_LEFT, mask_ref, tr) * w[j:j + 1, :]
                o_ref[:, so] = _silu(acc).astype(ACT)
                return carry

            lax.fori_loop(0, CONV_BLK // 128, lanes, 0)

    return pl.pallas_call(
        body, name="conv_fwd", grid=(R // tr,),
        in_specs=[pl.BlockSpec((tr, CONV_BLK), lambda i, k=k: (i, COL_XBC // CONV_BLK + k)) for k in range(n_blk)]
        + [pl.BlockSpec((SSM_CONV, CONV_W), lambda i: (0, 0)), pl.BlockSpec((1, CONV_W), lambda i: (0, 0)),
           _conv_mask_spec(tr, n_lat_tiles, 0)],
        out_specs=pl.BlockSpec((tr, CONV_W), lambda i: (i, 0)), out_shape=jax.ShapeDtypeStruct((R, CONV_W), ACT),
        compiler_params=_cparams(("arbitrary",)),
    )(*([parts] * n_blk), cw, cb, _conv_mask_table(tr))


def _conv_bwd(dx, db, dc, dy, dskip, parts, cw, cb, dparts, n_lat_tiles, tr):
    R = parts.shape[0]
    bc = SSM_G * SSM_N

    def body(dxf_ref, dxb_ref, dy_ref, ds_ref, dbf_ref, dbb_ref, dcf_ref, dcb_ref, u_ref, w_ref, b_ref, _, mask_ref,
             du_ref, acc_ref, d_scr):
        _zero_first(acc_ref)
        d_scr[:, 0:SSM_INNER] = dxf_ref[...] + dxb_ref[...] + _f32(dy_ref) * ds_ref[...]
        d_scr[:, SSM_INNER:SSM_INNER + bc] = dbf_ref[...] + dbb_ref[...]
        d_scr[:, SSM_INNER + bc:] = dcf_ref[...] + dcb_ref[...]

        def lanes(l, carry):
            sl = pl.ds(pl.multiple_of(l * 128, 128), 128)
            u, w = u_ref[:, sl].astype(F32), w_ref[:, sl]
            pre = jnp.zeros_like(u) + b_ref[:, sl]
            taps = []
            for j in range(SSM_CONV):
                tap = _shifted(u, j - CONV_LEFT, mask_ref, tr)
                taps.append(tap)
                pre = pre + tap * w[j:j + 1, :]
            dpre = d_scr[:, sl] * _dsilu(pre)
            du = jnp.zeros_like(u)
            sums = []
            for j in range(SSM_CONV):
                sums.append(_colsum(dpre * taps[j]))
                du = du + _shifted(dpre, CONV_LEFT - j, mask_ref, tr) * w[j:j + 1, :]
            sums += [_colsum(dpre), jnp.zeros((8 - SSM_CONV - 1, 128), F32)]
            acc_ref[:, sl] += jnp.concatenate(sums, axis=0)
            du_ref[:, sl] = du.astype(BF16)
            return carry

        lax.fori_loop(0, CONV_W // 128, lanes, 0)

    return pl.pallas_call(
        body, name="conv_bwd", grid=(R // tr,),
        in_specs=[_row_spec(tr, SSM_INNER)] * 3 + [_vec_spec(SSM_INNER)] + [_row_spec(tr, bc)] * 4
        + [_row_spec(tr, CONV_W, COL_XBC // CONV_W), pl.BlockSpec((SSM_CONV, CONV_W), lambda i: (0, 0)), _vec_spec(CONV_W),
           _ANY, _conv_mask_spec(tr, n_lat_tiles, 0)],
        out_specs=[_dparts_out(tr, CONV_W, COL_XBC), _acc_spec(CONV_W)],
        out_shape=[jax.ShapeDtypeStruct(dparts.shape, BF16), jax.ShapeDtypeStruct((8, CONV_W), F32)],
        scratch_shapes=[pltpu.VMEM((tr, CONV_W), F32)],
        input_output_aliases={11: 0}, compiler_params=_cparams(("arbitrary",)),
    )(*dx, dy, dskip, *db, *dc, parts, cw, cb, dparts, _conv_mask_table(tr))


def _chunk_row_block(d, i, n_lat, n_ctx):
    fwd = jnp.where(i < n_ctx, n_lat + i, i - n_ctx)
    rev = n_lat + n_ctx - 1 - i
    if isinstance(d, int):
        return rev if d else fwd
    return jnp.where(d == 0, fwd, rev)


def _tri(n, d, transpose=False):
    row = lax.broadcasted_iota(jnp.int32, (n, n), 0)
    col = lax.broadcasted_iota(jnp.int32, (n, n), 1)
    diff = (col - row) if transpose else (row - col)
    return diff * (1 - 2 * d) >= 0


def _gla_gates(sm, uhi, ulo, bias, d):
    pre = _nn3(sm, uhi, ulo) + bias
    g = _log_sigmoid(pre) * (1.0 / GLA_TAU)
    mask = _tri(GLA_C, d)
    b = _x_nn(mask.astype(BF16), g)
    b_tot = _colsum(g)
    b_ref = b[GLA_C // 2:GLA_C // 2 + 1, :]
    e_q = jnp.exp(jnp.minimum(b - b_ref, EXP_CLAMP))
    e_k = jnp.exp(jnp.minimum(b_ref - b, EXP_CLAMP))
    return pre, mask, b_tot, e_q, e_k, jnp.exp(b), jnp.exp(b_tot - b)


GLA_QK = GLA_H * GLA_DK
GLA_V = GLA_H * GLA_DV


def _gla_specs(n_lat, n_ctx, step_of):
    rbs = [lambda i, d=d: _chunk_row_block(d, step_of(i), n_lat, n_ctx) for d in range(2)]
    specs = []
    for rb in rbs:
        specs += [pl.BlockSpec((GLA_C, GLA_QK), lambda i, rb=rb: (rb(i), COL_Q // GLA_QK)),
                  pl.BlockSpec((GLA_C, GLA_QK), lambda i, rb=rb: (rb(i), COL_K // GLA_QK)),
                  pl.BlockSpec((GLA_C, GLA_V), lambda i, rb=rb: (rb(i), COL_V // GLA_V)),
                  pl.BlockSpec((GLA_C, 128), lambda i, rb=rb: (rb(i), 0))]
    specs += [pl.BlockSpec((2, 128, GLA_QK), lambda i: (0, 0, 0)), pl.BlockSpec((2, 128, GLA_QK), lambda i: (0, 0, 0)),
              pl.BlockSpec((2, 1, GLA_QK), lambda i: (0, 0, 0))]
    return specs, rbs


def _gla_fwd(parts, sm, uhi, ulo, bias, n_lat, n_ctx):
    R = parts.shape[0]
    n_steps = n_lat + n_ctx
    scale = GLA_DK ** -0.5

    def body(*refs):
        ins, (uhi_ref, ulo_ref, bias_ref), o_refs, hist_ref, st = refs[:8], refs[8:11], refs[11:13], refs[13], refs[14]

        @pl.when(pl.program_id(0) == 0)
        def _():
            st[...] = jnp.zeros_like(st)

        for d in range(2):
            q_ref, k_ref, v_ref, sm_ref = ins[4 * d:4 * d + 4]
            _, mask, b_tot, e_q, e_k, e_in, e_out = _gla_gates(sm_ref[...], uhi_ref[d], ulo_ref[d], bias_ref[d], d)
            q, k, v = _f32(q_ref) * scale, _f32(k_ref), _bf(v_ref[...])
            qb, kb, q_in, k_out, decay = _bf(q * e_q), _bf(k * e_k), _bf(q * e_in), _bf(k * e_out), jnp.exp(b_tot)
            for h in range(GLA_H):
                sk, sv = slice(h * GLA_DK, (h + 1) * GLA_DK), slice(h * GLA_DV, (h + 1) * GLA_DV)
                att = jnp.where(mask, _nt(qb[:, sk], kb[:, sk]), 0.0)
                s_in = st[d, h]
                hist_ref[d, 0, h] = s_in
                o_refs[d][:, sv] = (_nn(_bf(att), v[:, sv]) + _nt(q_in[:, sk], _bf(s_in))).astype(ACT)
                st[d, h] = decay[:, sk] * s_in + _tn(v[:, sv], k_out[:, sk])

    in_specs, rbs = _gla_specs(n_lat, n_ctx, lambda i: i)
    return pl.pallas_call(
        body, name="gla_fwd", grid=(n_steps,), in_specs=in_specs,
        out_specs=[pl.BlockSpec((GLA_C, GLA_V), lambda i, rb=rb: (rb(i), 0)) for rb in rbs]
        + [pl.BlockSpec((2, 1, GLA_H, GLA_DV, GLA_DK), lambda i: (0, i, 0, 0, 0))],
        out_shape=[jax.ShapeDtypeStruct((R, GLA_V), ACT)] * 2 + [jax.ShapeDtypeStruct((2, n_steps, GLA_H, GLA_DV, GLA_DK), F32)],
        scratch_shapes=[pltpu.VMEM((2, GLA_H, GLA_DV, GLA_DK), F32)],
        compiler_params=_cparams(("arbitrary",)),
    )(*([parts, parts, parts, sm] * 2), uhi, ulo, bias)


def _gla_bwd(do, parts, sm, uhi, ulo, bias, hist, n_lat, n_ctx):
    R = parts.shape[0]
    n_steps = n_lat + n_ctx
    scale = GLA_DK ** -0.5
    step_of = lambda j: n_steps - 1 - j

    def body(*refs):
        ins, (uhi_ref, ulo_ref, bias_ref), do_refs, hist_ref = refs[:8], refs[8:11], refs[11:13], refs[13]
        outs, dst = refs[14:22], refs[22]

        @pl.when(pl.program_id(0) == 0)
        def _():
            dst[...] = jnp.zeros_like(dst)

        for d in range(2):
            q_ref, k_ref, v_ref, sm_ref = ins[4 * d:4 * d + 4]
            dq_ref, dk_ref, dv_ref, dp_ref = outs[4 * d:4 * d + 4]
            pre, mask, b_tot, e_q, e_k, e_in, e_out = _gla_gates(sm_ref[...], uhi_ref[d], ulo_ref[d], bias_ref[d], d)
            q, k, v = _f32(q_ref) * scale, _f32(k_ref), _bf(v_ref[...])
            dout = _bf(do_refs[d][...])
            k_out_f = k * e_out
            qb, kb, q_in, k_out, decay = _bf(q * e_q), _bf(k * e_k), _bf(q * e_in), _bf(k_out_f), jnp.exp(b_tot)
            dqs, dks, dk_outs, dss = [], [], [], []
            for h in range(GLA_H):
                sk, sv = slice(h * GLA_DK, (h + 1) * GLA_DK), slice(h * GLA_DV, (h + 1) * GLA_DV)
                s_in, ds = hist_ref[d, 0, h], dst[d, h]
                att = jnp.where(mask, _nt(qb[:, sk], kb[:, sk]), 0.0)
                datt = _bf(jnp.where(mask, _nt(dout[:, sv], v[:, sv]), 0.0))
                dv_ref[:, sv] = (_tn(_bf(att), dout[:, sv]) + _nt(k_out[:, sk], _bf(ds))).astype(ACT)
                dqs.append(_nn(datt, kb[:, sk]) * e_q[:, sk] + _nn(dout[:, sv], _bf(s_in)) * e_in[:, sk])
                dk_o = _nn(v[:, sv], _bf(ds))
                dk_outs.append(dk_o)
                dks.append(_tn(datt, qb[:, sk]) * e_k[:, sk])
                dss.append(_colsum(ds * s_in))
                dst[d, h] = decay[:, sk] * ds + _tn(dout[:, sv], q_in[:, sk])
            dq, dk_out = jnp.concatenate(dqs, axis=1), jnp.concatenate(dk_outs, axis=1)
            dk = jnp.concatenate(dks, axis=1) + dk_out * e_out
            dq_ref[...] = (dq * scale).astype(ACT)
            dk_ref[...] = dk.astype(ACT)
            db_tot = _colsum(dk_out * k_out_f) + decay * jnp.concatenate(dss, axis=1)
            dg = _x_nn(_tri(GLA_C, d, transpose=True).astype(BF16), dq * q - dk * k) + db_tot
            dp_ref[...] = dg * (1.0 / GLA_TAU) * _sigmoid(-pre)

    in_specs, rbs = _gla_specs(n_lat, n_ctx, step_of)
    in_specs += [pl.BlockSpec((GLA_C, GLA_V), lambda j, rb=rb: (rb(j), 0)) for rb in rbs]
    in_specs += [pl.BlockSpec((2, 1, GLA_H, GLA_DV, GLA_DK), lambda j: (0, step_of(j), 0, 0, 0))]
    out_specs, out_shape = [], []
    for rb in rbs:
        for w, dt in ((GLA_QK, ACT), (GLA_QK, ACT), (GLA_V, ACT), (GLA_QK, F32)):
            out_specs.append(pl.BlockSpec((GLA_C, w), lambda j, rb=rb: (rb(j), 0)))
            out_shape.append(jax.ShapeDtypeStruct((R, w), dt))
    outs = pl.pallas_call(
        body, name="gla_bwd", grid=(n_steps,), in_specs=in_specs, out_specs=out_specs, out_shape=out_shape,
        scratch_shapes=[pltpu.VMEM((2, GLA_H, GLA_DV, GLA_DK), F32)],
        compiler_params=_cparams(("arbitrary",)),
    )(*([parts, parts, parts, sm] * 2), uhi, ulo, bias, do, do, hist)
    return [(outs[k], outs[4 + k]) for k in range(4)]


def _ssd_consts(dt_bias, a_log):
    sel = np.zeros((2, SSM_G, 128, 128), np.float32)
    for d, base in enumerate((SM_DTF, SM_DTB)):
        for g in range(SSM_G):
            for e in range(SSM_HPG):
                sel[d, g, base + SSM_HPG * g + e, e] = 1.0
    e512 = np.zeros((128, SSM_GW), np.float32)
    for e in range(SSM_HPG):
        e512[e, SSM_P * e:SSM_P * (e + 1)] = 1.0
    a_neg = -jnp.exp(a_log)
    pad = lambda v: jnp.pad(v.reshape(2, SSM_G, 1, SSM_HPG), ((0, 0), (0, 0), (0, 0), (0, 128 - SSM_HPG)))
    return dict(
        sel=jnp.asarray(sel, BF16), sel_t=jnp.asarray(sel.transpose(0, 1, 3, 2), BF16), e512_t=jnp.asarray(e512.T, BF16),
        dtb=pad(dt_bias), a=pad(a_neg), a512=jnp.repeat(a_neg, SSM_P, axis=1).reshape(2, SSM_G, 1, SSM_GW))


def _head_columns(x8):
    return [jnp.broadcast_to(x8[:, e:e + 1], (x8.shape[0], 128)) for e in range(SSM_HPG)]


def _head_layout(cols):
    low = lax.broadcasted_iota(jnp.int32, (1, 128), 1) < SSM_P
    return jnp.concatenate([jnp.where(low, cols[2 * j], cols[2 * j + 1]) for j in range(SSM_HPG // 2)], axis=1)


def _ssd_common(sm, sel, dtb, a_neg, a512, d):
    dtr8 = _nn_x(sm, sel) + dtb
    dt8 = _softplus(dtr8)
    a8 = a_neg * dt8
    mask = _tri(SSM_C, d)
    mask_t = _tri(SSM_C, d, transpose=True).astype(BF16)
    cum8 = _x_nn(mask.astype(BF16), a8)
    a_hi, a_lo = _split(a8)
    cum_t = _tn(a_hi, mask_t) + _tn(a_lo, mask_t)
    cum_cols = _head_columns(cum8)
    dt_exp = _head_layout(_head_columns(dt8))
    a_exp = a512 * dt_exp
    return dict(dtr8=dtr8, a8=a8, mask=mask, mask_t=mask_t, cum_t=cum_t, dt_exp=dt_exp, a_exp=a_exp,
                cum_exp=_head_layout(cum_cols), cum_cols=cum_cols, tot_exp=_colsum(a_exp))


def _head_lanes(x, e):
    pair = x[:, 128 * (e // 2):128 * (e // 2 + 1)]
    low = lax.broadcasted_iota(jnp.int32, (1, 128), 1) < SSM_P
    return _bf(jnp.where(low if e % 2 == 0 else jnp.logical_not(low), pair, 0.0))


def _per_head_pairs(fn, x):
    return jnp.concatenate([fn(2 * j, _head_lanes(x, 2 * j)) + fn(2 * j + 1, _head_lanes(x, 2 * j + 1))
                            for j in range(SSM_HPG // 2)], axis=1)


def _ssd_decay(cm, e):
    diff = cm["cum_cols"][e] - cm["cum_t"][e:e + 1, :]
    return jnp.where(cm["mask"], jnp.exp(jnp.minimum(diff, 0.0)), 0.0)


SSM_GPS = 4


def _ssd_specs(n_lat, n_ctx, step_of):
    rbs = [lambda i, d=d: _chunk_row_block(d, step_of(i), n_lat, n_ctx) for d in range(2)]
    xw, nw = SSM_GPS * SSM_GW, SSM_GPS * SSM_N
    specs = []
    for rb in rbs:
        specs += [pl.BlockSpec((SSM_C, xw), lambda g, i, rb=rb: (rb(i), g)),
                  pl.BlockSpec((SSM_C, nw), lambda g, i, rb=rb: (rb(i), SSM_INNER // nw + g)),
                  pl.BlockSpec((SSM_C, nw), lambda g, i, rb=rb: (rb(i), (SSM_INNER + SSM_G * SSM_N) // nw + g)),
                  pl.BlockSpec((SSM_C, 128), lambda g, i, rb=rb: (rb(i), 0))]
    specs += [pl.BlockSpec((2, SSM_GPS, 128, 128), lambda g, i: (0, g, 0, 0)),
              pl.BlockSpec((2, SSM_GPS, 1, 128), lambda g, i: (0, g, 0, 0)),
              pl.BlockSpec((2, SSM_GPS, 1, 128), lambda g, i: (0, g, 0, 0)),
              pl.BlockSpec((2, SSM_GPS, 1, SSM_GW), lambda g, i: (0, g, 0, 0))]
    return specs, rbs


def _ssd_fwd(xbc, sm, k, n_lat, n_ctx, comm):
    R = xbc.shape[0]
    n_steps = n_lat + n_ctx

    def body(*refs):
        ins, (sel_ref, dtb_ref, a_ref, a512_ref), y_refs, hist_ref, st = refs[:8], refs[8:12], refs[12:14], refs[14], refs[15]

        @pl.when(pl.program_id(1) == 0)
        def _():
            st[...] = jnp.zeros_like(st)

        for d in range(2):
            x_ref, b_ref, c_ref, sm_ref = ins[4 * d:4 * d + 4]
            sm = sm_ref[...]
            for gg in range(SSM_GPS):
                sx, sn = slice(gg * SSM_GW, (gg + 1) * SSM_GW), slice(gg * SSM_N, (gg + 1) * SSM_N)
                cm = _ssd_common(sm, sel_ref[d, gg], dtb_ref[d, gg], a_ref[d, gg], a512_ref[d, gg], d)
                bm, cmat = _bf(b_ref[:, sn]), _bf(c_ref[:, sn])
                xdt = x_ref[:, sx].astype(F32) * cm["dt_exp"]
                cb = _nt(cmat, bm)
                ys = _per_head_pairs(lambda e, x_e: _nn(_bf(cb * _ssd_decay(cm, e)), x_e), xdt)
                s_in = st[d, gg]
                hist_ref[d, 0, gg] = s_in
                y = ys + jnp.exp(cm["cum_exp"]) * _nn(cmat, _bf(s_in))
                y_refs[d][:, sx] = y.astype(ACT)
                st[d, gg] = jnp.exp(cm["tot_exp"]) * s_in + _tn(bm, _bf(xdt * jnp.exp(cm["tot_exp"] - cm["cum_exp"])))

    in_specs, rbs = _ssd_specs(n_lat, n_ctx, lambda i: i)
    out_specs = [pl.BlockSpec((SSM_C, SSM_GPS * SSM_GW), lambda g, i, rb=rb: (rb(i), g)) for rb in rbs]
    out_specs += [pl.BlockSpec((2, 1, SSM_GPS, SSM_N, SSM_GW), lambda g, i: (0, i, g, 0, 0))]
    out_shape = [jax.ShapeDtypeStruct((R, SSM_INNER), ACT)] * 2 + [jax.ShapeDtypeStruct((2, n_steps, SSM_G, SSM_N, SSM_GW), F32)]
    args = [xbc, xbc, xbc, sm] * 2 + [k["sel"], k["dtb"], k["a"], k["a512"]]
    n_host_out = len(out_shape)
    outs = pl.pallas_call(
        _hosted(body, len(args), n_host_out, 1, comm, *_ssd_comm_steps(n_steps)), name="ssd_fwd",
        grid=(SSM_G // SSM_GPS, n_steps), in_specs=in_specs + [_ANY] * len(comm.arrays),
        out_specs=out_specs + [_ANY] * len(comm.out_shape), out_shape=out_shape + comm.out_shape,
        scratch_shapes=[pltpu.VMEM((2, SSM_GPS, SSM_N, SSM_GW), F32)] + comm.scratch,
        compiler_params=_cparams(("arbitrary", "arbitrary")),
    )(*args, *comm.arrays)
    return outs[:n_host_out], outs[n_host_out:]


def _ssd_comm_steps(n_steps):
    n_g = SSM_G // SSM_GPS
    at = lambda g, i: (pl.program_id(0) == g) & (pl.program_id(1) == i)
    half = (n_g // 2, n_steps // 2 if n_g % 2 else 0)
    return (lambda: at(0, 0)), (lambda: at(*half)), (lambda: at(n_g - 1, n_steps - 1))


def _ssd_bwd(dy, xbc, sm, k, hist, n_lat, n_ctx, comm):
    R = xbc.shape[0]
    n_steps = n_lat + n_ctx
    step_of = lambda j: n_steps - 1 - j

    def one(d, gg, x_ref, b_ref, c_ref, sm_ref, sel_ref, dtb_ref, a_ref, a512_ref, selt_ref, e512t_ref, dy_ref,
            hist_ref, dx_ref, db_ref, dc_ref, dsm_ref, acc_ref, dst):
        sx, sn = slice(gg * SSM_GW, (gg + 1) * SSM_GW), slice(gg * SSM_N, (gg + 1) * SSM_N)
        a_neg, e512_t = a_ref[d, gg], e512t_ref[...]
        cm = _ssd_common(sm_ref[...], sel_ref[d, gg], dtb_ref[d, gg], a_neg, a512_ref[d, gg], d)
        x, dyv = x_ref[:, sx].astype(F32), dy_ref[:, sx].astype(F32)
        bm, cmat = _bf(b_ref[:, sn]), _bf(c_ref[:, sn])
        xdt = x * cm["dt_exp"]
        cb = _nt(cmat, bm)
        s_in, ds = hist_ref[d, 0, gg], dst[d, gg]
        w = jnp.exp(cm["tot_exp"] - cm["cum_exp"])
        z = _nn(bm, _bf(ds))
        decay_in = jnp.exp(cm["cum_exp"])
        gy = _bf(dyv * decay_in)
        dcb = jnp.zeros((SSM_C, SSM_C), F32)
        crossing = []
        row = lax.broadcasted_iota(jnp.int32, (SSM_C, SSM_C), 0)
        col = lax.broadcasted_iota(jnp.int32, (SSM_C, SSM_C), 1)
        eye = (row == col).astype(BF16)
        before = (cm["mask_t"] - eye)
        xdt_bf = _bf(xdt)

        def head(e, dy_e):
            nonlocal dcb
            lm = _ssd_decay(cm, e)
            m_e = cb * lm
            dm_e = _nt(dy_e, xdt_bf[:, 128 * (e // 2):128 * (e // 2 + 1)])
            dcb = dcb + dm_e * lm
            crossing.append(_bf(dm_e * m_e))
            return _tn(_bf(m_e), dy_e)

        dx_heads = _per_head_pairs(head, dyv)
        through = _nn(jnp.concatenate(crossing, axis=0), before)
        crossing = [_colsum(jnp.where(cm["mask"], through[e * SSM_C:(e + 1) * SSM_C], 0.0)) for e in range(SSM_HPG)]
        da_rows = jnp.concatenate(crossing + [jnp.zeros((128 - SSM_HPG, SSM_C), F32)], axis=0)
        r_hi, r_lo = _split(da_rows)
        da8_intra = _tn(r_hi, eye) + _tn(r_lo, eye)
        dx_state = w * z
        dxdt = dx_heads + dx_state
        dcb = _bf(dcb)
        c_s = _nn(cmat, _bf(s_in))
        dc_ref[:, sn] = _nn(dcb, bm) + _nt(gy, _bf(s_in))
        db_ref[:, sn] = _tn(dcb, cmat) + _nt(_bf(w * xdt), _bf(ds))
        dst[d, gg] = jnp.exp(cm["tot_exp"]) * ds + _tn(cmat, gy)
        state_path = xdt * dx_state
        per_token = _nn_x(jnp.concatenate([dyv * decay_in * c_s - state_path, dxdt * x], axis=0), e512_t)
        totals = jnp.concatenate([_colsum(state_path), _colsum(ds * s_in), jnp.zeros((6, SSM_GW), F32)], axis=0)
        totals = _nn_x(totals, e512_t)
        tot8 = _colsum(cm["a8"])
        dtot8 = totals[0:1] + jnp.exp(tot8) * totals[1:2]
        da8 = da8_intra + _x_nn(cm["mask_t"], per_token[:SSM_C]) + dtot8
        ddt8 = da8 * a_neg + per_token[SSM_C:]
        dsm_ref[gg] = _nn_x(ddt8 * _sigmoid(cm["dtr8"]), selt_ref[d, gg])
        dx_ref[:, sx] = dxdt * cm["dt_exp"]
        acc_ref[d, gg, 0:1, :] += _colsum(da8 * cm["a8"])

    def body(*refs):
        ins, consts, (selt_ref, e512t_ref), dy_refs, hist_ref = refs[:8], refs[8:12], refs[12:14], refs[14:16], refs[16]
        outs, acc_ref, dst = refs[17:25], refs[25], refs[26]

        @pl.when(pl.program_id(1) == 0)
        def _():
            dst[...] = jnp.zeros_like(dst)
            acc_ref[...] = jnp.zeros_like(acc_ref)

        for d in range(2):
            for gg in range(SSM_GPS):
                one(d, gg, *ins[4 * d:4 * d + 4], *consts, selt_ref, e512t_ref, dy_refs[d], hist_ref,
                    *outs[4 * d:4 * d + 4], acc_ref, dst)

    xw, nw = SSM_GPS * SSM_GW, SSM_GPS * SSM_N
    in_specs, rbs = _ssd_specs(n_lat, n_ctx, step_of)
    in_specs += [pl.BlockSpec((2, SSM_GPS, 128, 128), lambda g, j: (0, g, 0, 0)), pl.BlockSpec((SSM_GW, 128), lambda g, j: (0, 0))]
    in_specs += [pl.BlockSpec((SSM_C, xw), lambda g, j, rb=rb: (rb(j), g)) for rb in rbs]
    in_specs += [pl.BlockSpec((2, 1, SSM_GPS, SSM_N, SSM_GW), lambda g, j: (0, step_of(j), g, 0, 0))]
    out_specs, out_shape = [], []
    for rb in rbs:
        out_specs += [pl.BlockSpec((SSM_C, xw), lambda g, j, rb=rb: (rb(j), g)),
                      pl.BlockSpec((SSM_C, nw), lambda g, j, rb=rb: (rb(j), g)),
                      pl.BlockSpec((SSM_C, nw), lambda g, j, rb=rb: (rb(j), g)),
                      pl.BlockSpec((SSM_GPS, SSM_C, 128), lambda g, j, rb=rb: (g, rb(j), 0))]
        out_shape += [jax.ShapeDtypeStruct((R, SSM_INNER), F32), jax.ShapeDtypeStruct((R, SSM_G * SSM_N), F32),
                      jax.ShapeDtypeStruct((R, SSM_G * SSM_N), F32), jax.ShapeDtypeStruct((SSM_G, R, 128), F32)]
    out_specs.append(pl.BlockSpec((2, SSM_GPS, 8, 128), lambda g, j: (0, g, 0, 0)))
    out_shape.append(jax.ShapeDtypeStruct((2, SSM_G, 8, 128), F32))
    args = [xbc, xbc, xbc, sm] * 2 + [k["sel"], k["dtb"], k["a"], k["a512"], k["sel_t"], k["e512_t"], dy, dy, hist]
    n_host_out = len(out_shape)
    outs = pl.pallas_call(
        _hosted(body, len(args), n_host_out, 1, comm, *_ssd_comm_steps(n_steps)), name="ssd_bwd",
        grid=(SSM_G // SSM_GPS, n_steps), in_specs=in_specs + [_ANY] * len(comm.arrays),
        out_specs=out_specs + [_ANY] * len(comm.out_shape), out_shape=out_shape + comm.out_shape,
        scratch_shapes=[pltpu.VMEM((2, SSM_GPS, SSM_N, SSM_GW), F32)] + comm.scratch,
        compiler_params=_cparams(("arbitrary", "arbitrary")),
    )(*args, *comm.arrays)
    return [(outs[n], outs[4 + n]) for n in range(4)] + [outs[8]], outs[n_host_out:]


def _gla_assemble(dq, dk, dv, dparts, tr):
    R = dq[0].shape[0]
    qk = GLA_H * GLA_DK

    def body(dqf_ref, dqb_ref, dkf_ref, dkb_ref, dvf_ref, dvb_ref, _, o_ref):
        o_ref[:, 0:qk] = (_f32(dqf_ref) + _f32(dqb_ref)).astype(BF16)
        o_ref[:, qk:2 * qk] = (_f32(dkf_ref) + _f32(dkb_ref)).astype(BF16)
        o_ref[:, 2 * qk:] = (_f32(dvf_ref) + _f32(dvb_ref)).astype(BF16)

    return pl.pallas_call(
        body, name="gla_assemble", grid=(R // tr,), in_specs=[_row_spec(tr, qk)] * 4 + [_row_spec(tr, D)] * 2 + [_ANY],
        out_specs=_dparts_out(tr, 2 * D, COL_Q), out_shape=jax.ShapeDtypeStruct(dparts.shape, BF16),
        input_output_aliases={6: 0}, compiler_params=_cparams(("arbitrary",)),
    )(*dq, *dk, *dv, dparts)


def _small_assemble(dp, dsm, sm, ut_hi, ut_lo, dparts, tr):
    R = sm.shape[0]
    qk = GLA_H * GLA_DK

    def body(dpf_ref, dpb_ref, dsmf_ref, dsmb_ref, sm_ref, uth_ref, utl_ref, _, o_ref, dup_ref, acc_ref, acc2_ref):
        @pl.when(pl.program_id(0) == 0)
        def _():
            dup_ref[...] = jnp.zeros_like(dup_ref)
            acc_ref[...] = jnp.zeros_like(acc_ref)
            acc2_ref[...] = jnp.zeros_like(acc2_ref)

        ssd = dsmf_ref[0] + dsmb_ref[0]
        for g in range(1, SSM_G):
            ssd = ssd + (dsmf_ref[g] + dsmb_ref[g])
        acc2_ref[0:1, :] += _colsum(ssd)
        sm_hi, sm_lo = _split(sm_ref[...])
        out = ssd
        for d, dp_ref in enumerate((dpf_ref, dpb_ref)):
            dpd = dp_ref[...]
            out = out + _nn3(dpd, uth_ref[d], utl_ref[d])
            p_hi, p_lo = _split(dpd)
            dup_ref[d] += _tn(sm_hi, p_hi) + _tn(sm_lo, p_hi) + _tn(sm_hi, p_lo)
            acc_ref[d:d + 1, :] += _colsum(dpd)
        o_ref[...] = out.astype(BF16)

    return pl.pallas_call(
        body, name="small_assemble", grid=(R // tr,),
        in_specs=[_row_spec(tr, qk)] * 2 + [pl.BlockSpec((SSM_G, tr, 128), lambda i: (0, i, 0))] * 2
        + [_row_spec(tr, 128), pl.BlockSpec((2, qk, 128), lambda i: (0, 0, 0)),
           pl.BlockSpec((2, qk, 128), lambda i: (0, 0, 0)), _ANY],
        out_specs=[_dparts_out(tr, 128, COL_SM), pl.BlockSpec((2, 128, qk), lambda i: (0, 0, 0)), _acc_spec(qk), _acc_spec(128)],
        out_shape=[jax.ShapeDtypeStruct(dparts.shape, BF16), jax.ShapeDtypeStruct((2, 128, qk), F32),
                   jax.ShapeDtypeStruct((8, qk), F32), jax.ShapeDtypeStruct((8, 128), F32)],
        input_output_aliases={7: 0}, compiler_params=_cparams(("arbitrary",)),
    )(*dp, *dsm, sm, ut_hi, ut_lo, dparts)


ADA_ROWS = 16
ADA_TILE = 512


def _dot3_f32(a, b, ca, cb):
    a_hi, a_lo = _split(a)
    b_hi, b_lo = _split(b)
    return _dg(a_hi, b_hi, ca, cb) + _dg(a_lo, b_hi, ca, cb) + _dg(a_hi, b_lo, ca, cb)


def _ada_fwd(cvec, w, b):
    cols = w.shape[1]

    def body(c_ref, w_ref, b_ref, o_ref):
        o_ref[...] = _dot3_f32(_silu(c_ref[...]), w_ref[...], 1, 0) + b_ref[...]

    return pl.pallas_call(
        body, name="ada_fwd", grid=(cols // ADA_TILE,),
        in_specs=[pl.BlockSpec((ADA_ROWS, D), lambda j: (0, 0)), pl.BlockSpec((D, ADA_TILE), lambda j: (0, j)),
                  pl.BlockSpec((1, ADA_TILE), lambda j: (0, j))],
        out_specs=pl.BlockSpec((ADA_ROWS, ADA_TILE), lambda j: (0, j)), out_shape=jax.ShapeDtypeStruct((ADA_ROWS, cols), F32),
        compiler_params=_cparams(("arbitrary",)),
    )(cvec, w, b)


def _adam(w, g, m, v):
    m2 = ADAM_B1 * m + (1.0 - ADAM_B1) * g
    v2 = ADAM_B2 * v + (1.0 - ADAM_B2) * (g * g)
    m_hat = m2 / (1.0 - ADAM_B1 ** ADAM_STEP)
    v_hat = v2 / (1.0 - ADAM_B2 ** ADAM_STEP)
    return -ADAM_LR * (m_hat / (jnp.sqrt(v_hat) + ADAM_EPS) + ADAM_WD * w), m2, v2


def _wada_bwd_adam(cvec, dada, w, m, v):
    rows, cols = w.shape
    tr = _tile(rows, 256, 128)

    def body(c_ref, d_ref, w_ref, m_ref, v_ref, g_ref, dl_ref, m2_ref, v2_ref, p_ref):
        wv = w_ref[...]
        g = _dot3_f32(_silu(c_ref[...]), d_ref[...], 0, 0)
        g_ref[...] = g
        dl_ref[...], m2_ref[...], v2_ref[...] = _adam(wv, g, m_ref[...], v_ref[...])
        p_ref[...] = _dot3_f32(d_ref[...], wv, 1, 1)

    blk = pl.BlockSpec((tr, cols), lambda i: (i, 0))
    return pl.pallas_call(
        body, name="wada_bwd_adam", grid=(rows // tr,),
        in_specs=[pl.BlockSpec((ADA_ROWS, tr), lambda i: (0, i)), pl.BlockSpec((ADA_ROWS, cols), lambda i: (0, 0)), blk, blk, blk],
        out_specs=[blk, blk, blk, blk, pl.BlockSpec((ADA_ROWS, tr), lambda i: (0, i))],
        out_shape=[jax.ShapeDtypeStruct((rows, cols), F32)] * 4 + [jax.ShapeDtypeStruct((ADA_ROWS, rows), F32)],
        compiler_params=_cparams(("arbitrary",)),
    )(cvec, dada, w, m, v)


def _reduce_adam(parts8, w, m, v, name):
    rows, cols = w.shape
    tr = _tile(rows, 64, 16)

    def body(p_ref, w_ref, m_ref, v_ref, g_ref, dl_ref, m2_ref, v2_ref):
        g = p_ref[0].astype(F32) + p_ref[N_CHIPS].astype(F32)
        for j in range(1, N_CHIPS):
            g = g + (p_ref[j].astype(F32) + p_ref[N_CHIPS + j].astype(F32))
        g_ref[...] = g
        dl_ref[...], m2_ref[...], v2_ref[...] = _adam(w_ref[...], g, m_ref[...], v_ref[...])

    blk = pl.BlockSpec((tr, cols), lambda i: (i, 0))
    return pl.pallas_call(
        body, name=name, grid=(rows // tr,), in_specs=[pl.BlockSpec((N_DEV, tr, cols), lambda i: (0, i, 0)), blk, blk, blk],
        out_specs=[blk] * 4, out_shape=[jax.ShapeDtypeStruct((rows, cols), F32)] * 4, compiler_params=_cparams(("arbitrary",)),
    )(parts8, w, m, v)


SMALL_W = 1024


def _sum8(g8):
    rows = g8.shape[1]

    def body(g_ref, o_ref):
        s = g_ref[0]
        for j in range(1, N_DEV):
            s = s + g_ref[j]
        o_ref[...] = s

    return pl.pallas_call(
        body, name="sum8", out_shape=jax.ShapeDtypeStruct((rows, SMALL_W), F32),
        in_specs=[pl.BlockSpec(memory_space=pltpu.VMEM)], out_specs=pl.BlockSpec(memory_space=pltpu.VMEM),
        compiler_params=pltpu.CompilerParams(vmem_limit_bytes=VMEM_LIMIT),
    )(g8)


def _cctx_grad(p8, c_ctx):
    def body(p_ref, c_ref, o_ref):
        s = p_ref[0]
        for chip in range(1, N_CHIPS):
            s = s + p_ref[2 * chip]
        o_ref[...] = s * _dsilu(c_ref[...])

    return pl.pallas_call(
        body, name="cctx_grad", out_shape=jax.ShapeDtypeStruct((1, D), F32),
        in_specs=[pl.BlockSpec(memory_space=pltpu.VMEM)] * 2, out_specs=pl.BlockSpec(memory_space=pltpu.VMEM),
    )(p8, c_ctx)


def _adam_small(w, g, m, v):
    def body(w_ref, g_ref, m_ref, v_ref, dl_ref, m2_ref, v2_ref):
        dl_ref[...], m2_ref[...], v2_ref[...] = _adam(w_ref[...], g_ref[...], m_ref[...], v_ref[...])

    vm = pl.BlockSpec(memory_space=pltpu.VMEM)
    return pl.pallas_call(
        body, name="adam_small", out_shape=[jax.ShapeDtypeStruct(w.shape, F32)] * 3, in_specs=[vm] * 4, out_specs=[vm] * 3,
        compiler_params=pltpu.CompilerParams(vmem_limit_bytes=VMEM_LIMIT),
    )(w, g, m, v)


def _pack(vecs, width=SMALL_W, row_mult=8):
    flat = jnp.concatenate([v.reshape(-1).astype(F32) for v in vecs])
    n = flat.shape[0]
    rows = -(-n // (width * row_mult)) * row_mult
    return jnp.pad(flat, (0, rows * width - n)).reshape(rows, width)


def _unpack(packed, shapes):
    flat = packed.reshape(-1)
    out, off = [], 0
    for s in shapes:
        n = int(np.prod(s))
        out.append(flat[off:off + n].reshape(s))
        off += n
    return out


WEIGHTS = ('c_ctx', 'w_ada', 'b_ada', 'norm1_w', 'w_in', 'gla_up_f', 'gla_bias_f', 'gla_up_b', 'gla_bias_b', 'gla_norm_w',
           'conv_w', 'conv_b', 'dt_bias_f', 'dt_bias_b', 'a_log_f', 'a_log_b', 'd_skip', 'ssm_norm_w', 'w_pa', 'w_pb', 'w_out',
           'norm2_w', 'w_gate', 'w_up', 'w_down', 'final_norm_w')
BIG = ('w_in', 'w_pa', 'w_pb', 'w_out', 'w_gate', 'w_up', 'w_down')
COL_SHARDED = ('w_in', 'w_gate', 'w_up')
SMALL_SHARDED = ('gla_up_f', 'gla_up_b', 'conv_w')
ROW_TILE = 256
ROW_TILE_WIDE = 768
MM_ROWS = 1408
MM_ROWS_SMALL = 768
MM_KROWS = 2816


def _blocks_to_full(g4, name):
    n, r, c = g4.shape
    return g4.transpose(1, 0, 2).reshape(r, n * c) if name in COL_SHARDED else g4.reshape(n * r, c)


def _full_to_blocks(full, name):
    r, c = full.shape
    if name in COL_SHARDED:
        return full.reshape(r, N_CHIPS, c // N_CHIPS).transpose(1, 0, 2)
    return full.reshape(N_CHIPS, r // N_CHIPS, c)


def _permute_in(w_in_full):
    off = np.concatenate([[0], np.cumsum(IN_WIDTHS)])
    cols = [w_in_full[:, off[p]:off[p + 1]] for p in PERM]
    return jnp.concatenate(cols + [jnp.zeros((w_in_full.shape[0], SMALL_PAD), w_in_full.dtype)], axis=1)


def _unpermute_in(wp):
    off = np.concatenate([[0], np.cumsum([IN_WIDTHS[p] for p in PERM])])
    pieces = {p: wp[:, off[i]:off[i + 1]] for i, p in enumerate(PERM)}
    return jnp.concatenate([pieces[p] for p in range(len(IN_WIDTHS))], axis=1)


def _chip_cols(full, chip, n):
    return lax.dynamic_slice_in_dim(full, chip * n, n, axis=1)


def kernel(x, c, ctx, c_ctx, w_ada, b_ada, norm1_w, w_in, gla_up_f, gla_bias_f, gla_up_b, gla_bias_b, gla_norm_w, conv_w, conv_b, dt_bias_f, dt_bias_b, a_log_f, a_log_b, d_skip, ssm_norm_w, w_pa, w_pb, w_out, norm2_w, w_gate, w_up, w_down, final_norm_w, loss_target, m_c_ctx, m_w_ada, m_b_ada, m_norm1_w, m_w_in, m_gla_up_f, m_gla_bias_f, m_gla_up_b, m_gla_bias_b, m_gla_norm_w, m_conv_w, m_conv_b, m_dt_bias_f, m_dt_bias_b, m_a_log_f, m_a_log_b, m_d_skip, m_ssm_norm_w, m_w_pa, m_w_pb, m_w_out, m_norm2_w, m_w_gate, m_w_up, m_w_down, m_final_norm_w, v_c_ctx, v_w_ada, v_b_ada, v_norm1_w, v_w_in, v_gla_up_f, v_gla_bias_f, v_gla_up_b, v_gla_bias_b, v_gla_norm_w, v_conv_w, v_conv_b, v_dt_bias_f, v_dt_bias_b, v_a_log_f, v_a_log_b, v_d_skip, v_ssm_norm_w, v_w_pa, v_w_pb, v_w_out, v_norm2_w, v_w_gate, v_w_up, v_w_down, v_final_norm_w):
    given = dict(locals())
    W = {n: given[n] for n in WEIGHTS}
    M = {n: given["m_" + n] for n in WEIGHTS}
    V = {n: given["v_" + n] for n in WEIGHTS}
    L, Lc = x.shape[1], ctx.shape[1]
    tr = ROW_TILE
    assert L % tr == 0 and Lc % tr == 0 and L % Lc == 0 and Lc % SSM_C == 0
    n_lat_tiles = L // tr
    trw = _tile(L + Lc, ROW_TILE_WIDE, tr)
    xi, yi, ci = _place()
    chip, me = 2 * xi + yi, 4 * xi + 2 * yi + ci
    x2, ctx2 = x[0], ctx[0]

    g0 = _allgather_small(_pack([c[0]] + [W[n][0] for n in SMALL_SHARDED]), "gather_c")
    g0 = g0.reshape(N_DEV, -1)
    c_all = g0[:, :D]
    small_full, off = {}, D
    for n in SMALL_SHARDED:
        r, cols = W[n].shape[1:]
        small_full[n] = jnp.concatenate([g0[2 * k, off:off + r * cols].reshape(r, cols) for k in range(N_CHIPS)], axis=1)
        off += r * cols
    up_f, up_b, conv_w_full = (small_full[n] for n in SMALL_SHARDED)

    cvec = jnp.zeros((ADA_ROWS, D), F32).at[:N_DEV].set(c_all).at[N_DEV].set(c_ctx)
    ada_cols = w_ada.shape[2]
    ada_part = _ada_fwd(cvec, w_ada[0], _chip_cols(b_ada, chip, ada_cols))
    g1_all = _allgather_small(ada_part, "gather_ada")
    ada_full = jnp.concatenate([g1_all[2 * k] for k in range(N_CHIPS)], axis=1)
    mine = lax.dynamic_slice_in_dim(ada_full, me, 1, axis=0)
    sh1, sc1, g1, sh2, sc2, g2 = (mine[:, k * D:(k + 1) * D] for k in range(6))
    csh1, csc1 = ada_full[N_DEV:N_DEV + 1, :D], ada_full[N_DEV:N_DEV + 1, D:2 * D]
    mod = jnp.stack([jnp.stack([sh1, sc1]), jnp.stack([csh1, csc1])])

    later = [n for n in BIG if n != 'w_in']

    def lr_rows(up, base):
        return jnp.zeros((128, GLA_H * GLA_DK), F32).at[base:base + GLA_RANK].set(up)
    u2 = jnp.stack([lr_rows(up_f, SM_LRF), lr_rows(up_b, SM_LRB)])
    u2_hi = u2.astype(BF16)
    u2_lo = (u2 - u2_hi.astype(F32)).astype(BF16)
    ut = u2.transpose(0, 2, 1)
    ut_hi = ut.astype(BF16)
    ut_lo = (ut - ut_hi.astype(F32)).astype(BF16)
    gbias = jnp.stack([gla_bias_f, gla_bias_b])
    kc = _ssd_consts(jnp.stack([dt_bias_f[0], dt_bias_b[0]]), jnp.stack([a_log_f[0], a_log_b[0]]))
    gw4 = jnp.tile(gla_norm_w, (1, GLA_H))
    dskip_exp = jnp.repeat(d_skip, SSM_P, axis=1)
    n_gla = (L // GLA_C, Lc // GLA_C)
    n_ssd = (L // SSM_C, Lc // SSM_C)

    h1, (w_in_blocks,) = _norm_mod(x2, ctx2, norm1_w, mod, n_lat_tiles, tr, _gather_split_comm(w_in[0].astype(BF16)))
    full = {'w_in': _blocks_to_full(w_in_blocks, 'w_in')}
    wp = _permute_in(full['w_in'])
    parts = _mm(h1, wp, "nn", ACT, "mm_in", tm=MM_ROWS, tn=PW // 3)
    sm = _mm(h1, wp[:, COL_SM:], "nn", F32, "mm_in_small", tm=MM_ROWS)
    xbc = _conv_fwd(parts, conv_w_full, conv_b, L // Lc, Lc)
    *o2, gla_hist = _gla_fwd(parts, sm, u2_hi, u2_lo, gbias, *n_gla)
    (*y2, ssd_hist), gathered = _ssd_fwd(xbc, sm, kc, *n_ssd, _gather_comm([W[n][0].astype(BF16) for n in later]))
    full.update({n: _blocks_to_full(g, n) for n, g in zip(later, gathered)})
    oan = _gla_out(o2, parts, gw4, trw)
    obn = _ssd_out(y2, xbc, parts, dskip_exp, ssm_norm_w, trw)
    ya = _mm(oan, full['w_pa'], "nn", ACT, "mm_pa", tm=MM_ROWS)
    yb = _mm(obn, full['w_pb'], "nn", ACT, "mm_pb", tm=MM_ROWS)
    merged = _merge(ya, yb, parts, trw)
    mix = _mm(merged, full['w_out'], "nn", ACT, "mm_out", tm=MM_ROWS)
    h2, u = _resid_norm_mod(x2, ctx2, mix, g1, norm2_w, sh2, sc2, n_lat_tiles, tr)
    gp = _mm(u, full['w_gate'], "nn", ACT, "mm_gate", tm=MM_ROWS, tn=D_FF)
    up = _mm(u, full['w_up'], "nn", ACT, "mm_up", tm=MM_ROWS, tn=D_FF)
    f = _mm(gp, full['w_down'], "nn", ACT, "mm_down", tm=MM_ROWS_SMALL, tk=D_FF // 2, swiglu_up=up)
    dh3, df, acc_loss = _loss_head(h2, f, loss_target[0], g2, final_norm_w[None], n_lat_tiles, tr)

    dw = {}
    da = _mm(df, full['w_down'], "nt", ACT, "mm_down_dx", tm=MM_ROWS, tn=D_FF)
    dw['w_down'] = _mm(gp, df, "tn", BF16, "mm_down_dw", tm=D_FF // 2, tk=MM_ROWS, swiglu_up=up)
    dgp, dup = _swiglu_act_bwd(da, gp, up, trw)
    du_a = _mm(dgp, full['w_gate'], "nt", ACT, "mm_gate_dx", tm=MM_ROWS, tk=D_FF)
    du_b = _mm(dup, full['w_up'], "nt", ACT, "mm_up_dx", tm=MM_ROWS, tk=D_FF)
    dw['w_gate'] = _mm(u, dgp, "tn", BF16, "mm_gate_dw", tm=D, tn=D_FF // 2, tk=MM_KROWS)
    dw['w_up'] = _mm(u, dup, "tn", BF16, "mm_up_dw", tm=D, tn=D_FF // 2, tk=MM_KROWS)
    dh2, dmix, acc_ffn = _ffn_in_bwd(du_a, du_b, h2, dh3, mix, sc2, g1, norm2_w, trw)
    dmerged = _mm(dmix, full['w_out'], "nt", ACT, "mm_out_dx", tm=MM_ROWS)
    dw['w_out'] = _mm(merged, dmix, "tn", BF16, "mm_out_dw", tm=D, tk=MM_KROWS)
    dya, dyb, dparts = _merge_bwd(dmerged, ya, yb, parts, lax.empty((L + Lc, PW), BF16), trw)
    doan = _mm(dya, full['w_pa'], "nt", ACT, "mm_pa_dx", tm=MM_ROWS)
    dw['w_pa'] = _mm(oan, dya, "tn", BF16, "mm_pa_dw", tm=D, tk=MM_KROWS)
    dobn = _mm(dyb, full['w_pb'], "nt", ACT, "mm_pb_dx", tm=MM_ROWS)
    dw['w_pb'] = _mm(obn, dyb, "tn", BF16, "mm_pb_dw", tm=D, tk=MM_KROWS)
    do, dparts, acc_gla = _gla_out_bwd(doan, o2, parts, gw4, dparts, trw)
    dq, dk, dv, dpre = _gla_bwd(do, parts, sm, u2_hi, u2_lo, gbias, gla_hist, *n_gla)
    dy, dparts, acc_ssd = _ssd_out_bwd(dobn, y2, xbc, parts, dskip_exp, ssm_norm_w, dparts, tr)
    (dx_scan, db_scan, dc_scan, dsm, acc_alog), exchanged = _ssd_bwd(
        dy, xbc, sm, kc, ssd_hist, *n_ssd, _exchange_comm([_full_to_blocks(dw[n], n) for n in later]))
    exchanged = dict(zip(later, exchanged))
    dparts, acc_conv = _conv_bwd(dx_scan, db_scan, dc_scan, dy, dskip_exp, parts, conv_w_full, conv_b, dparts, L // Lc, Lc)
    dparts = _gla_assemble(dq, dk, dv, dparts, trw)
    dparts, dup_gla, acc_gbias, acc_dtb = _small_assemble(dpre, dsm, sm, ut_hi, ut_lo, dparts, trw)
    dw['w_in'] = _unpermute_in(_mm(h1, dparts, "tn", BF16, "mm_in_dw", tm=D, tn=PW // 9, tk=MM_KROWS))
    dh1, (exchanged['w_in'],) = _mm(dparts, wp, "nt", F32, "mm_in_dx", tm=MM_ROWS_SMALL, tk=PW // 3,
                                    comm=_exchange_comm([_full_to_blocks(dw['w_in'], 'w_in')]))
    dx, acc_n1 = _norm1_bwd(dh1, x2, ctx2, dh2, norm1_w, mod, n_lat_tiles, tr)

    partial = dict(
        norm1_w=acc_n1[0, 2] + acc_n1[1, 2],
        gla_up_f=dup_gla[0, SM_LRF:SM_LRF + GLA_RANK], gla_bias_f=acc_gbias[0],
        gla_up_b=dup_gla[1, SM_LRB:SM_LRB + GLA_RANK], gla_bias_b=acc_gbias[1],
        gla_norm_w=acc_gla[0].reshape(GLA_H, GLA_DV).sum(0),
        conv_w=acc_conv[:SSM_CONV], conv_b=acc_conv[SSM_CONV],
        dt_bias_f=acc_dtb[0, SM_DTF:SM_DTF + SSM_HEADS], dt_bias_b=acc_dtb[0, SM_DTB:SM_DTB + SSM_HEADS],
        a_log_f=acc_alog[0, :, 0, :SSM_HPG], a_log_b=acc_alog[1, :, 0, :SSM_HPG],
        d_skip=acc_ssd[1].reshape(SSM_HEADS, SSM_P).sum(1), ssm_norm_w=acc_ssd[0],
        norm2_w=acc_ffn[2], final_norm_w=acc_loss[0],
    )
    dada = jnp.concatenate([acc_n1[0, 1], acc_n1[0, 0], acc_ffn[3], acc_ffn[1], acc_ffn[0], acc_loss[1]])
    dada_ctx = jnp.concatenate([acc_n1[1, 1], acc_n1[1, 0], jnp.zeros((4 * D,), F32)])
    names = list(partial)
    payload = [partial[n] for n in names] + [dada + dada_ctx, dada_ctx, acc_loss[2], dada]
    sizes = [int(np.prod(p.shape)) for p in payload]
    g8 = _allgather_small(_pack(payload), "gather_small_grads")
    summed = _unpack(_sum8(g8), [(s,) for s in sizes])
    grads = {n: s.reshape(W[n].shape if n not in SMALL_SHARDED else partial[n].shape) for n, s in zip(names, summed)}
    grads['b_ada'] = summed[len(names)].reshape(b_ada.shape)
    dada_ctx_sum = summed[len(names) + 1]
    loss = jnp.sum(summed[len(names) + 2])
    dada_all = g8.reshape(N_DEV, -1)[:, sum(sizes[:-1]):sum(sizes)]

    dada16 = jnp.zeros((ADA_ROWS, ada_cols), F32)
    dada16 = dada16.at[:N_DEV].set(_chip_cols(dada_all, chip, ada_cols)).at[N_DEV].set(_chip_cols(dada_ctx_sum[None], chip, ada_cols)[0])
    g_wada, dl_wada, m_wada, v_wada, p16 = _wada_bwd_adam(cvec, dada16, w_ada[0], m_w_ada[0], v_w_ada[0])
    p8 = _allgather_small(p16[N_DEV:], "gather_cctx")
    grads['c_ctx'] = _cctx_grad(p8[:, 0:1, :], c_ctx[None])[0]
    for n in SMALL_SHARDED:
        grads[n] = _chip_cols(grads[n], chip, W[n].shape[2])[None]

    small = [n for n in WEIGHTS if n not in BIG and n != 'w_ada']
    shapes = [W[n].shape for n in small]
    dl_s, m_s, v_s = _adam_small(*[_pack([d[n] for n in small]) for d in (W, grads, M, V)])
    delta = dict(zip(small, _unpack(dl_s, shapes)))
    new_m = dict(zip(small, _unpack(m_s, shapes)))
    new_v = dict(zip(small, _unpack(v_s, shapes)))
    grads['w_ada'], delta['w_ada'], new_m['w_ada'], new_v['w_ada'] = g_wada[None], dl_wada[None], m_wada[None], v_wada[None]

    for n in BIG:
        g, dl, m2, v2 = _reduce_adam(exchanged[n], W[n][0], M[n][0], V[n][0], "adam_" + n)
        grads[n], delta[n], new_m[n], new_v[n] = g[None], dl[None], m2[None], v2[None]

    return (loss, dx[None], *[grads[n] for n in WEIGHTS], *[delta[n] for n in WEIGHTS],
            *[new_m[n] for n in WEIGHTS], *[new_v[n] for n in WEIGHTS])
```
